```python
import math
import jax
import jax.numpy as jnp
from jax import lax
import numpy as np

D_MODEL = 1024
BATCH = 32
SEQ = 2048
DEPTH = 1

CTX_LEN = 256
GRID_W = 64
SSD_D_INNER = 2 * D_MODEL
SSD_HEADDIM = 64
SSD_HEADS = SSD_D_INNER // SSD_HEADDIM
SSD_GROUPS = 8
SSD_HPG = SSD_HEADS // SSD_GROUPS
SSD_STATE = 128
SSD_CHUNK = 128
CONV_K = 4
CONV_LEFT = 2
LRU_WIDTH = D_MODEL
LRU_BLOCKS = 8
LRU_BLOCK_W = LRU_WIDTH // LRU_BLOCKS
LRU_C = 8.0
MLP_HIDDEN = 4 * D_MODEL
N_BRANCH = 2
N_MOD = 6
DEEPNORM_ALPHA = (2 * DEPTH) ** 0.25
DEEPNORM_BETA = (8 * DEPTH) ** -0.25
LN_EPS = 1e-6
RMS_EPS = 1e-5

SSD_BC_W = SSD_GROUPS * SSD_STATE
SSD_XB = SSD_D_INNER + SSD_BC_W
SSD_XBC = SSD_D_INNER + 2 * SSD_BC_W
SSD_DT = 2 * SSD_HEADS
O_DT = SSD_XB
O_LRU = O_DT + SSD_DT
STATE_COLS = O_LRU + LRU_WIDTH
O_C = STATE_COLS
O_Z = O_C + SSD_BC_W
O_LRU_GATE = O_Z + SSD_D_INNER
O_MERGE = O_LRU_GATE + LRU_WIDTH
IN_COLS = O_MERGE + N_BRANCH * D_MODEL

kernel_name = 'hybrid_ssd_rglru_dit_block'


def layer_norm(x, g=None, b=None):
    xf = x.astype(jnp.float32)
    mu = jnp.mean(xf, axis=-1, keepdims=True)
    var = jnp.mean(jnp.square(xf - mu), axis=-1, keepdims=True)
    y = (xf - mu) * lax.rsqrt(var + LN_EPS)
    if g is not None:
        y = y * g.astype(jnp.float32) + b.astype(jnp.float32)
    return y.astype(x.dtype)


def modulation(cvec, w_mod, b_mod, n_chunks):
    m = jax.nn.silu(cvec) @ w_mod[:, :n_chunks * D_MODEL] + b_mod[:n_chunks * D_MODEL]
    return jnp.split(m, n_chunks, axis=-1)


def modulate(x, shift, scale):
    return layer_norm(x) * (1.0 + scale) + shift


def short_conv(u, w, b, rows):
    bsz, t, ch = u.shape
    v = u if rows is None else u.reshape(bsz, rows, GRID_W, ch)
    n = v.shape[-2]
    pad = [(0, 0)] * (v.ndim - 2) + [(CONV_LEFT, CONV_K - 1 - CONV_LEFT), (0, 0)]
    vp = jnp.pad(v, pad)
    out = b
    for k in range(CONV_K):
        out = out + vp[..., k:k + n, :] * w[k]
    return out.reshape(bsz, t, ch)


def ssd_chunked(xh, dt, a_neg, bm, cm, h0):
    bsz, t = xh.shape[:2]
    nc = t // SSD_CHUNK
    xc = (xh.astype(jnp.float32) * dt[..., None]).reshape(bsz, nc, SSD_CHUNK, SSD_GROUPS, SSD_HPG, SSD_HEADDIM)
    cum = jnp.cumsum((dt * a_neg).reshape(bsz, nc, SSD_CHUNK, SSD_GROUPS, SSD_HPG), axis=2)
    bc = bm.astype(jnp.float32).reshape(bsz, nc, SSD_CHUNK, SSD_GROUPS, SSD_STATE)
    to_end = jnp.exp(cum[:, :, -1:] - cum)
    states = jnp.einsum('bcjgn,bcjgh,bcjghp->bcghpn', bc, to_end, xc)
    chunk_decay = jnp.exp(cum[:, :, -1])

    def step(h, inp):
        dec, st = inp
        return dec[..., None, None] * h + st, h

    h_fin, h_start = lax.scan(step, h0, (jnp.moveaxis(chunk_decay, 1, 0), jnp.moveaxis(states, 1, 0)))
    if cm is None:
        return None, h_fin
    h_start = jnp.moveaxis(h_start, 0, 1)
    cc = cm.astype(jnp.float32).reshape(bsz, nc, SSD_CHUNK, SSD_GROUPS, SSD_STATE)
    seg = cum[:, :, :, None] - cum[:, :, None, :]
    lower = jnp.tril(jnp.ones((SSD_CHUNK, SSD_CHUNK), dtype=bool))[:, :, None, None]
    decay = jnp.exp(jnp.where(lower, seg, -jnp.inf))
    cb = jnp.einsum('bcign,bcjgn->bcijg', cc, bc)
    y = (jnp.einsum('bcijg,bcijgh,bcjghp->bcighp', cb, decay, xc)
         + jnp.einsum('bcign,bcigh,bcghpn->bcighp', cc, jnp.exp(cum), h_start))
    return y.reshape(bsz, t, SSD_HEADS, SSD_HEADDIM), h_fin


def gated_rmsnorm(y, z, w):
    u = (y * jax.nn.silu(z)).astype(jnp.float32)
    ug = u.reshape(*u.shape[:-1], SSD_GROUPS, -1)
    ug = ug * lax.rsqrt(jnp.mean(jnp.square(ug), axis=-1, keepdims=True) + RMS_EPS)
    return (ug.reshape(u.shape) * w.astype(jnp.float32)).astype(y.dtype)


def ssd_branch(xb_raw, c_raw, dt_raw, p, h0_f, h0_b, rows):
    bsz, t, _ = xb_raw.shape
    xb = jax.nn.silu(short_conv(xb_raw, p['ssd_conv_w'][:, :SSD_XB], p['ssd_conv_b'][:SSD_XB], rows))
    xh = xb[..., :SSD_D_INNER].reshape(bsz, t, SSD_HEADS, SSD_HEADDIM)
    bm = xb[..., SSD_D_INNER:].reshape(bsz, t, SSD_GROUPS, SSD_STATE)
    cm = None
    if c_raw is not None:
        cm = jax.nn.silu(short_conv(c_raw, p['ssd_conv_w'][:, SSD_XB:], p['ssd_conv_b'][SSD_XB:], rows))
        cm = cm.reshape(bsz, t, SSD_GROUPS, SSD_STATE)
    dt = jax.nn.softplus(dt_raw.astype(jnp.float32).reshape(bsz, t, 2, SSD_HEADS) + p['ssd_dt_bias'])
    a_neg = -jnp.exp(p['ssd_a_log'].astype(jnp.float32))
    flip = lambda u: None if u is None else jnp.flip(u, axis=1)
    y_f, s_f = ssd_chunked(xh, dt[:, :, 0], a_neg[0], bm, cm, h0_f)
    y_b, s_b = ssd_chunked(flip(xh), flip(dt[:, :, 1]), a_neg[1], flip(bm), flip(cm), h0_b)
    if c_raw is None:
        return None, s_f, s_b
    y = y_f + flip(y_b) + p['ssd_d'][:, None] * xh
    return y.reshape(bsz, t, SSD_D_INNER).astype(xb_raw.dtype), s_f, s_b


def lru_combine(e1, e2):
    a1, b1 = e1
    a2, b2 = e2
    return a1 * a2, a2 * b1 + b2


def rglru(u, wa, ba, wi, bi, lam, h0, reverse):
    bsz, t, w = u.shape
    uf = u.astype(jnp.float32)
    ub = uf.reshape(bsz, t, LRU_BLOCKS, LRU_BLOCK_W)
    r = jax.nn.sigmoid(jnp.einsum('btkc,kcd->btkd', ub, wa).reshape(bsz, t, w) + ba)
    i = jax.nn.sigmoid(jnp.einsum('btkc,kcd->btkd', ub, wi).reshape(bsz, t, w) + bi)
    log_a = -LRU_C * r * jax.nn.softplus(-lam)
    a = jnp.exp(log_a)
    b_in = jnp.sqrt(-jnp.expm1(2.0 * log_a)) * (i * uf)
    edge = t - 1 if reverse else 0
    b_in = b_in.at[:, edge].add(a[:, edge] * h0)
    _, h = lax.associative_scan(lru_combine, (a, b_in), reverse=reverse, axis=1)
    return h, h[:, 0 if reverse else t - 1]


def lru_branch(u_raw, p, h0_f, h0_b, rows, need_y):
    u = short_conv(u_raw, p['lru_conv_w'], p['lru_conv_b'], rows)
    h_f, s_f = rglru(u, p['lru_wa'][0], p['lru_ba'][0], p['lru_wi'][0], p['lru_bi'][0], p['lru_lambda'][0], h0_f, False)
    h_b, s_b = rglru(u, p['lru_wa'][1], p['lru_ba'][1], p['lru_wi'][1], p['lru_bi'][1], p['lru_lambda'][1], h0_b, True)
    if not need_y:
        return None, s_f, s_b
    return (h_f + h_b).astype(u_raw.dtype), s_f, s_b


def token_mixer(h, p, init, rows, need_out):
    cols = IN_COLS if need_out else STATE_COLS
    proj = h @ p['w_in'][:, :cols]
    xb_raw = proj[..., :O_DT]
    dt_raw = proj[..., O_DT:O_LRU]
    lru_raw = proj[..., O_LRU:STATE_COLS]
    c_raw = proj[..., O_C:O_Z] if need_out else None
    y_ssd, s_f, s_b = ssd_branch(xb_raw, c_raw, dt_raw, p, init[0], init[1], rows)
    y_lru, l_f, l_b = lru_branch(lru_raw, p, init[2], init[3], rows, need_out)
    states = (s_f, s_b, l_f, l_b)
    if not need_out:
        return None, states
    z = proj[..., O_Z:O_LRU_GATE]
    lru_gate = proj[..., O_LRU_GATE:O_MERGE]
    gates = jax.nn.sigmoid(proj[..., O_MERGE:] + p['b_gate'])
    g_ssd, g_lru = jnp.split(gates, N_BRANCH, axis=-1)
    br_ssd = gated_rmsnorm(y_ssd, z, p['ssd_norm_w']) @ p['w_br_ssd']
    br_lru = (y_lru * jax.nn.gelu(lru_gate)) @ p['w_br_lru']
    return (g_ssd * br_ssd + g_lru * br_lru) @ p['w_out'], states


def sq_relu_mlp(h, p):
    return jnp.square(jax.nn.relu(h @ p['w_mlp1'] + p['b_mlp1'])) @ p['w_mlp2'] + p['b_mlp2']


def _fwd_setup_inputs(seed: int = 0) -> dict:
    key = jax.random.key(seed)
    ks = jax.random.split(key, 40)
    f32 = jnp.float32

    def nrm(k, shape, fan_in, gain=1.0):
        return jax.random.normal(k, shape, f32) * (gain * fan_in ** -0.5)

    def small(k, shape):
        return 0.01 * jax.random.normal(k, shape, f32)

    dt0 = jnp.exp(jax.random.uniform(ks[8], (DEPTH, 2, SSD_HEADS), f32, minval=math.log(1e-3), maxval=math.log(1e-1)))
    a_pow = jax.random.uniform(ks[17], (DEPTH, 2, LRU_WIDTH), f32, minval=0.9, maxval=0.999)
    a_base = a_pow ** (1.0 / LRU_C)
    return {
        'x': jax.random.normal(ks[0], (BATCH, SEQ, D_MODEL), f32),
        'c': jax.random.normal(ks[1], (BATCH, D_MODEL), f32),
        'ctx': jax.random.normal(ks[2], (BATCH, CTX_LEN, D_MODEL), f32),
        'c_ctx': jax.random.normal(ks[3], (D_MODEL,), f32),
        'w_mod': nrm(ks[4], (DEPTH, D_MODEL, N_MOD * D_MODEL), D_MODEL),
        'b_mod': small(ks[5], (DEPTH, N_MOD * D_MODEL)),
        'w_in': nrm(ks[6], (DEPTH, D_MODEL, IN_COLS), D_MODEL),
        'b_gate': small(ks[7], (DEPTH, N_BRANCH * D_MODEL)),
        'ssd_conv_w': nrm(ks[9], (DEPTH, CONV_K, SSD_XBC), CONV_K),
        'ssd_conv_b': small(ks[10], (DEPTH, SSD_XBC)),
        'ssd_dt_bias': dt0 + jnp.log(-jnp.expm1(-dt0)),
        'ssd_a_log': jnp.log(jax.random.uniform(ks[11], (DEPTH, 2, SSD_HEADS), f32, minval=1.0, maxval=16.0)),
        'ssd_d': 1.0 + small(ks[12], (DEPTH, SSD_HEADS)),
        'ssd_norm_w': 1.0 + small(ks[13], (DEPTH, SSD_D_INNER)),
        'lru_conv_w': nrm(ks[14], (DEPTH, CONV_K, LRU_WIDTH), CONV_K),
        'lru_conv_b': small(ks[15], (DEPTH, LRU_WIDTH)),
        'lru_wa': nrm(ks[16], (DEPTH, 2, LRU_BLOCKS, LRU_BLOCK_W, LRU_BLOCK_W), LRU_BLOCK_W),
        'lru_ba': small(ks[18], (DEPTH, 2, LRU_WIDTH)),
        'lru_wi': nrm(ks[19], (DEPTH, 2, LRU_BLOCKS, LRU_BLOCK_W, LRU_BLOCK_W), LRU_BLOCK_W),
        'lru_bi': small(ks[20], (DEPTH, 2, LRU_WIDTH)),
        'lru_lambda': jnp.log(a_base) - jnp.log1p(-a_base),
        'w_br_ssd': nrm(ks[21], (DEPTH, SSD_D_INNER, D_MODEL), SSD_D_INNER, DEEPNORM_BETA),
        'w_br_lru': nrm(ks[22], (DEPTH, LRU_WIDTH, D_MODEL), LRU_WIDTH, DEEPNORM_BETA),
        'w_out': nrm(ks[23], (DEPTH, D_MODEL, D_MODEL), D_MODEL, DEEPNORM_BETA),
        'ln1_g': 1.0 + small(ks[24], (DEPTH, D_MODEL)),
        'ln1_b': small(ks[25], (DEPTH, D_MODEL)),
        'w_mlp1': nrm(ks[26], (DEPTH, D_MODEL, MLP_HIDDEN), D_MODEL),
        'b_mlp1': small(ks[27], (DEPTH, MLP_HIDDEN)),
        'w_mlp2': nrm(ks[28], (DEPTH, MLP_HIDDEN, D_MODEL), MLP_HIDDEN, DEEPNORM_BETA),
        'b_mlp2': small(ks[29], (DEPTH, D_MODEL)),
        'ln2_g': 1.0 + small(ks[30], (DEPTH, D_MODEL)),
        'ln2_b': small(ks[31], (DEPTH, D_MODEL)),
    }


def _fwd_reference(x, c, ctx, c_ctx, w_mod, b_mod, w_in, b_gate, ssd_conv_w, ssd_conv_b, ssd_dt_bias,
              ssd_a_log, ssd_d, ssd_norm_w, lru_conv_w, lru_conv_b, lru_wa, lru_ba, lru_wi, lru_bi,
              lru_lambda, w_br_ssd, w_br_lru, w_out, ln1_g, ln1_b, w_mlp1, b_mlp1, w_mlp2, b_mlp2,
              ln2_g, ln2_b):
    bsz = x.shape[0]
    rows = x.shape[1] // GRID_W
    for l in range(DEPTH):
        p = dict(w_in=w_in[l], b_gate=b_gate[l], ssd_conv_w=ssd_conv_w[l], ssd_conv_b=ssd_conv_b[l],
                 ssd_dt_bias=ssd_dt_bias[l], ssd_a_log=ssd_a_log[l], ssd_d=ssd_d[l], ssd_norm_w=ssd_norm_w[l],
                 lru_conv_w=lru_conv_w[l], lru_conv_b=lru_conv_b[l], lru_wa=lru_wa[l], lru_ba=lru_ba[l],
                 lru_wi=lru_wi[l], lru_bi=lru_bi[l], lru_lambda=lru_lambda[l], w_br_ssd=w_br_ssd[l],
                 w_br_lru=w_br_lru[l], w_out=w_out[l], w_mlp1=w_mlp1[l], b_mlp1=b_mlp1[l],
                 w_mlp2=w_mlp2[l], b_mlp2=b_mlp2[l])
        last = l == DEPTH - 1
        zero_ssd = jnp.zeros((bsz, SSD_GROUPS, SSD_HPG, SSD_HEADDIM, SSD_STATE), jnp.float32)
        zero_lru = jnp.zeros((bsz, LRU_WIDTH), jnp.float32)
        mc = modulation(c_ctx, w_mod[l], b_mod[l], 2 if last else N_MOD)
        ctx_mix, ctx_states = token_mixer(modulate(ctx, mc[0], mc[1]), p,
                                          (zero_ssd, zero_ssd, zero_lru, zero_lru), None, not last)
        mx = [m[:, None, :] for m in modulation(c, w_mod[l], b_mod[l], N_MOD)]
        x_mix, _ = token_mixer(modulate(x, mx[0], mx[1]), p, ctx_states, rows, True)
        x = layer_norm(DEEPNORM_ALPHA * x + mx[2] * x_mix, ln1_g[l], ln1_b[l])
        x = layer_norm(DEEPNORM_ALPHA * x + mx[5] * sq_relu_mlp(modulate(x, mx[3], mx[4]), p), ln2_g[l], ln2_b[l])
        if not last:
            ctx = layer_norm(DEEPNORM_ALPHA * ctx + mc[2] * ctx_mix, ln1_g[l], ln1_b[l])
            ctx = layer_norm(DEEPNORM_ALPHA * ctx + mc[5] * sq_relu_mlp(modulate(ctx, mc[3], mc[4]), p),
                             ln2_g[l], ln2_b[l])
    return x


import jax as _jax
import jax.numpy as _jnp

TWIN_FORMAT = 'train_step'
FWD_PARAMS = ['x', 'c', 'ctx', 'c_ctx', 'w_mod', 'b_mod', 'w_in', 'b_gate', 'ssd_conv_w', 'ssd_conv_b', 'ssd_dt_bias', 'ssd_a_log', 'ssd_d', 'ssd_norm_w', 'lru_conv_w', 'lru_conv_b', 'lru_wa', 'lru_ba', 'lru_wi', 'lru_bi', 'lru_lambda', 'w_br_ssd', 'w_br_lru', 'w_out', 'ln1_g', 'ln1_b', 'w_mlp1', 'b_mlp1', 'w_mlp2', 'b_mlp2', 'ln2_g', 'ln2_b']
TWIN_WEIGHTS = ['c_ctx', 'w_mod', 'b_mod', 'w_in', 'b_gate', 'ssd_conv_w', 'ssd_conv_b', 'ssd_dt_bias', 'ssd_a_log', 'ssd_d', 'ssd_norm_w', 'lru_conv_w', 'lru_conv_b', 'lru_wa', 'lru_ba', 'lru_wi', 'lru_bi', 'lru_lambda', 'w_br_ssd', 'w_br_lru', 'w_out', 'ln1_g', 'ln1_b', 'w_mlp1', 'b_mlp1', 'w_mlp2', 'b_mlp2', 'ln2_g', 'ln2_b']
TWIN_DIFF_INPUT = 'x'
TWIN_INPUTS = ['x', 'c', 'ctx', 'c_ctx', 'w_mod', 'b_mod', 'w_in', 'b_gate', 'ssd_conv_w', 'ssd_conv_b', 'ssd_dt_bias', 'ssd_a_log', 'ssd_d', 'ssd_norm_w', 'lru_conv_w', 'lru_conv_b', 'lru_wa', 'lru_ba', 'lru_wi', 'lru_bi', 'lru_lambda', 'w_br_ssd', 'w_br_lru', 'w_out', 'ln1_g', 'ln1_b', 'w_mlp1', 'b_mlp1', 'w_mlp2', 'b_mlp2', 'ln2_g', 'ln2_b', 'loss_target', 'm_c_ctx', 'm_w_mod', 'm_b_mod', 'm_w_in', 'm_b_gate', 'm_ssd_conv_w', 'm_ssd_conv_b', 'm_ssd_dt_bias', 'm_ssd_a_log', 'm_ssd_d', 'm_ssd_norm_w', 'm_lru_conv_w', 'm_lru_conv_b', 'm_lru_wa', 'm_lru_ba', 'm_lru_wi', 'm_lru_bi', 'm_lru_lambda', 'm_w_br_ssd', 'm_w_br_lru', 'm_w_out', 'm_ln1_g', 'm_ln1_b', 'm_w_mlp1', 'm_b_mlp1', 'm_w_mlp2', 'm_b_mlp2', 'm_ln2_g', 'm_ln2_b', 'v_c_ctx', 'v_w_mod', 'v_b_mod', 'v_w_in', 'v_b_gate', 'v_ssd_conv_w', 'v_ssd_conv_b', 'v_ssd_dt_bias', 'v_ssd_a_log', 'v_ssd_d', 'v_ssd_norm_w', 'v_lru_conv_w', 'v_lru_conv_b', 'v_lru_wa', 'v_lru_ba', 'v_lru_wi', 'v_lru_bi', 'v_lru_lambda', 'v_w_br_ssd', 'v_w_br_lru', 'v_w_out', 'v_ln1_g', 'v_ln1_b', 'v_w_mlp1', 'v_b_mlp1', 'v_w_mlp2', 'v_b_mlp2', 'v_ln2_g', 'v_ln2_b']
TWIN_OUTPUTS = ['loss', 'grad_x', 'grad_c_ctx', 'grad_w_mod', 'grad_b_mod', 'grad_w_in', 'grad_b_gate', 'grad_ssd_conv_w', 'grad_ssd_conv_b', 'grad_ssd_dt_bias', 'grad_ssd_a_log', 'grad_ssd_d', 'grad_ssd_norm_w', 'grad_lru_conv_w', 'grad_lru_conv_b', 'grad_lru_wa', 'grad_lru_ba', 'grad_lru_wi', 'grad_lru_bi', 'grad_lru_lambda', 'grad_w_br_ssd', 'grad_w_br_lru', 'grad_w_out', 'grad_ln1_g', 'grad_ln1_b', 'grad_w_mlp1', 'grad_b_mlp1', 'grad_w_mlp2', 'grad_b_mlp2', 'grad_ln2_g', 'grad_ln2_b', 'delta_c_ctx', 'delta_w_mod', 'delta_b_mod', 'delta_w_in', 'delta_b_gate', 'delta_ssd_conv_w', 'delta_ssd_conv_b', 'delta_ssd_dt_bias', 'delta_ssd_a_log', 'delta_ssd_d', 'delta_ssd_norm_w', 'delta_lru_conv_w', 'delta_lru_conv_b', 'delta_lru_wa', 'delta_lru_ba', 'delta_lru_wi', 'delta_lru_bi', 'delta_lru_lambda', 'delta_w_br_ssd', 'delta_w_br_lru', 'delta_w_out', 'delta_ln1_g', 'delta_ln1_b', 'delta_w_mlp1', 'delta_b_mlp1', 'delta_w_mlp2', 'delta_b_mlp2', 'delta_ln2_g', 'delta_ln2_b', 'new_m_c_ctx', 'new_m_w_mod', 'new_m_b_mod', 'new_m_w_in', 'new_m_b_gate', 'new_m_ssd_conv_w', 'new_m_ssd_conv_b', 'new_m_ssd_dt_bias', 'new_m_ssd_a_log', 'new_m_ssd_d', 'new_m_ssd_norm_w', 'new_m_lru_conv_w', 'new_m_lru_conv_b', 'new_m_lru_wa', 'new_m_lru_ba', 'new_m_lru_wi', 'new_m_lru_bi', 'new_m_lru_lambda', 'new_m_w_br_ssd', 'new_m_w_br_lru', 'new_m_w_out', 'new_m_ln1_g', 'new_m_ln1_b', 'new_m_w_mlp1', 'new_m_b_mlp1', 'new_m_w_mlp2', 'new_m_b_mlp2', 'new_m_ln2_g', 'new_m_ln2_b', 'new_v_c_ctx', 'new_v_w_mod', 'new_v_b_mod', 'new_v_w_in', 'new_v_b_gate', 'new_v_ssd_conv_w', 'new_v_ssd_conv_b', 'new_v_ssd_dt_bias', 'new_v_ssd_a_log', 'new_v_ssd_d', 'new_v_ssd_norm_w', 'new_v_lru_conv_w', 'new_v_lru_conv_b', 'new_v_lru_wa', 'new_v_lru_ba', 'new_v_lru_wi', 'new_v_lru_bi', 'new_v_lru_lambda', 'new_v_w_br_ssd', 'new_v_w_br_lru', 'new_v_w_out', 'new_v_ln1_g', 'new_v_ln1_b', 'new_v_w_mlp1', 'new_v_b_mlp1', 'new_v_w_mlp2', 'new_v_b_mlp2', 'new_v_ln2_g', 'new_v_ln2_b']
TWIN_LEAF_KINDS = {'loss': 'loss', 'grad_x': 'grad_x', 'grad_c_ctx': 'grad_w', 'grad_w_mod': 'grad_w', 'grad_b_mod': 'grad_w', 'grad_w_in': 'grad_w', 'grad_b_gate': 'grad_w', 'grad_ssd_conv_w': 'grad_w', 'grad_ssd_conv_b': 'grad_w', 'grad_ssd_dt_bias': 'grad_w', 'grad_ssd_a_log': 'grad_w', 'grad_ssd_d': 'grad_w', 'grad_ssd_norm_w': 'grad_w', 'grad_lru_conv_w': 'grad_w', 'grad_lru_conv_b': 'grad_w', 'grad_lru_wa': 'grad_w', 'grad_lru_ba': 'grad_w', 'grad_lru_wi': 'grad_w', 'grad_lru_bi': 'grad_w', 'grad_lru_lambda': 'grad_w', 'grad_w_br_ssd': 'grad_w', 'grad_w_br_lru': 'grad_w', 'grad_w_out': 'grad_w', 'grad_ln1_g': 'grad_w', 'grad_ln1_b': 'grad_w', 'grad_w_mlp1': 'grad_w', 'grad_b_mlp1': 'grad_w', 'grad_w_mlp2': 'grad_w', 'grad_b_mlp2': 'grad_w', 'grad_ln2_g': 'grad_w', 'grad_ln2_b': 'grad_w', 'delta_c_ctx': 'delta_w', 'delta_w_mod': 'delta_w', 'delta_b_mod': 'delta_w', 'delta_w_in': 'delta_w', 'delta_b_gate': 'delta_w', 'delta_ssd_conv_w': 'delta_w', 'delta_ssd_conv_b': 'delta_w', 'delta_ssd_dt_bias': 'delta_w', 'delta_ssd_a_log': 'delta_w', 'delta_ssd_d': 'delta_w', 'delta_ssd_norm_w': 'delta_w', 'delta_lru_conv_w': 'delta_w', 'delta_lru_conv_b': 'delta_w', 'delta_lru_wa': 'delta_w', 'delta_lru_ba': 'delta_w', 'delta_lru_wi': 'delta_w', 'delta_lru_bi': 'delta_w', 'delta_lru_lambda': 'delta_w', 'delta_w_br_ssd': 'delta_w', 'delta_w_br_lru': 'delta_w', 'delta_w_out': 'delta_w', 'delta_ln1_g': 'delta_w', 'delta_ln1_b': 'delta_w', 'delta_w_mlp1': 'delta_w', 'delta_b_mlp1': 'delta_w', 'delta_w_mlp2': 'delta_w', 'delta_b_mlp2': 'delta_w', 'delta_ln2_g': 'delta_w', 'delta_ln2_b': 'delta_w', 'new_m_c_ctx': 'new_m', 'new_m_w_mod': 'new_m', 'new_m_b_mod': 'new_m', 'new_m_w_in': 'new_m', 'new_m_b_gate': 'new_m', 'new_m_ssd_conv_w': 'new_m', 'new_m_ssd_conv_b': 'new_m', 'new_m_ssd_dt_bias': 'new_m', 'new_m_ssd_a_log': 'new_m', 'new_m_ssd_d': 'new_m', 'new_m_ssd_norm_w': 'new_m', 'new_m_lru_conv_w': 'new_m', 'new_m_lru_conv_b': 'new_m', 'new_m_lru_wa': 'new_m', 'new_m_lru_ba': 'new_m', 'new_m_lru_wi': 'new_m', 'new_m_lru_bi': 'new_m', 'new_m_lru_lambda': 'new_m', 'new_m_w_br_ssd': 'new_m', 'new_m_w_br_lru': 'new_m', 'new_m_w_out': 'new_m', 'new_m_ln1_g': 'new_m', 'new_m_ln1_b': 'new_m', 'new_m_w_mlp1': 'new_m', 'new_m_b_mlp1': 'new_m', 'new_m_w_mlp2': 'new_m', 'new_m_b_mlp2': 'new_m', 'new_m_ln2_g': 'new_m', 'new_m_ln2_b': 'new_m', 'new_v_c_ctx': 'new_v', 'new_v_w_mod': 'new_v', 'new_v_b_mod': 'new_v', 'new_v_w_in': 'new_v', 'new_v_b_gate': 'new_v', 'new_v_ssd_conv_w': 'new_v', 'new_v_ssd_conv_b': 'new_v', 'new_v_ssd_dt_bias': 'new_v', 'new_v_ssd_a_log': 'new_v', 'new_v_ssd_d': 'new_v', 'new_v_ssd_norm_w': 'new_v', 'new_v_lru_conv_w': 'new_v', 'new_v_lru_conv_b': 'new_v', 'new_v_lru_wa': 'new_v', 'new_v_lru_ba': 'new_v', 'new_v_lru_wi': 'new_v', 'new_v_lru_bi': 'new_v', 'new_v_lru_lambda': 'new_v', 'new_v_w_br_ssd': 'new_v', 'new_v_w_br_lru': 'new_v', 'new_v_w_out': 'new_v', 'new_v_ln1_g': 'new_v', 'new_v_ln1_b': 'new_v', 'new_v_w_mlp1': 'new_v', 'new_v_b_mlp1': 'new_v', 'new_v_w_mlp2': 'new_v', 'new_v_b_mlp2': 'new_v', 'new_v_ln2_g': 'new_v', 'new_v_ln2_b': 'new_v'}


def _forward(args):
    return _fwd_reference(*[args[k] for k in FWD_PARAMS])


def _output_shape():
    out = _jax.eval_shape(lambda: _forward(_fwd_setup_inputs(0)))
    return out.shape, out.dtype

N_MICROBATCH = 1
ADAM_LR = 0.001
ADAM_B1 = 0.9
ADAM_B2 = 0.999
ADAM_EPS = 1e-08
ADAM_WD = 0.01
ADAM_STEP = 10
PER_EXAMPLE_BATCH_AXIS = {'x': 0, 'c': 0, 'ctx': 0, 'loss_target': 0}
SHARED_INPUTS = []
_WEIGHT_DTYPES = {'c_ctx': _jnp.float32, 'w_mod': _jnp.float32, 'b_mod': _jnp.float32, 'w_in': _jnp.float32, 'b_gate': _jnp.float32, 'ssd_conv_w': _jnp.float32, 'ssd_conv_b': _jnp.float32, 'ssd_dt_bias': _jnp.float32, 'ssd_a_log': _jnp.float32, 'ssd_d': _jnp.float32, 'ssd_norm_w': _jnp.float32, 'lru_conv_w': _jnp.float32, 'lru_conv_b': _jnp.float32, 'lru_wa': _jnp.float32, 'lru_ba': _jnp.float32, 'lru_wi': _jnp.float32, 'lru_bi': _jnp.float32, 'lru_lambda': _jnp.float32, 'w_br_ssd': _jnp.float32, 'w_br_lru': _jnp.float32, 'w_out': _jnp.float32, 'ln1_g': _jnp.float32, 'ln1_b': _jnp.float32, 'w_mlp1': _jnp.float32, 'b_mlp1': _jnp.float32, 'w_mlp2': _jnp.float32, 'b_mlp2': _jnp.float32, 'ln2_g': _jnp.float32, 'ln2_b': _jnp.float32}
MOMENT_SCALE = {'c_ctx': 4.969673e-02, 'w_mod': 1.469848e-01, 'b_mod': 2.402858e-01, 'w_in': 8.419595e-02, 'b_gate': 4.833740e-02, 'ssd_conv_w': 1.107104e-02, 'ssd_conv_b': 1.225428e-02, 'ssd_dt_bias': 2.629829e-02, 'ssd_a_log': 4.782520e-02, 'ssd_d': 3.277801e-02, 'ssd_norm_w': 1.574147e-02, 'lru_conv_w': 1.461362e-01, 'lru_conv_b': 2.573589e-01, 'lru_wa': 1.137303e-02, 'lru_ba': 1.371352e-02, 'lru_wi': 2.261069e-02, 'lru_bi': 2.333275e-02, 'lru_lambda': 3.443755e-02, 'w_br_ssd': 3.813216e-02, 'w_br_lru': 3.119809e-01, 'w_out': 3.115406e-01, 'ln1_g': 5.363798e-01, 'ln1_b': 3.222863e-01, 'w_mlp1': 8.390178e-02, 'b_mlp1': 6.795687e-02, 'w_mlp2': 2.517935e-01, 'b_mlp2': 1.314195e-01, 'ln2_g': 6.693872e+01, 'ln2_b': 1.378170e+01}


def _to_microbatches(a, axis):
    t = _jnp.moveaxis(a, axis, 0)
    t = t.reshape((N_MICROBATCH, t.shape[0] // N_MICROBATCH) + t.shape[1:])
    return _jnp.moveaxis(t, 1, axis + 1)


def setup_inputs(seed: int = 0) -> dict:
    inp = _fwd_setup_inputs(seed)
    key = _jax.random.fold_in(_jax.random.key(seed), 7919)
    shape, _ = _output_shape()
    out = dict(inp)
    out["loss_target"] = _jax.random.normal(_jax.random.fold_in(key, 0), shape, _jnp.float32)
    for i, name in enumerate(TWIN_WEIGHTS):
        w = inp[name].astype(_jnp.float32)
        if MOMENT_SCALE is None:
            s = _jnp.sqrt(_jnp.mean(_jnp.square(w)) + 1e-30)
        else:
            s = MOMENT_SCALE[name]
        km, kv = _jax.random.split(_jax.random.fold_in(key, i + 1))
        out[name] = w
        out["m_" + name] = s * _jax.random.normal(km, w.shape, _jnp.float32)
        out["v_" + name] = (s * s) * _jax.random.uniform(kv, w.shape, _jnp.float32, 0.5, 1.5)
    if N_MICROBATCH > 1:
        for name, axis in PER_EXAMPLE_BATCH_AXIS.items():
            out[name] = _to_microbatches(out[name], axis)
    return {'x': out['x'], 'c': out['c'], 'ctx': out['ctx'], 'c_ctx': out['c_ctx'], 'w_mod': out['w_mod'], 'b_mod': out['b_mod'], 'w_in': out['w_in'], 'b_gate': out['b_gate'], 'ssd_conv_w': out['ssd_conv_w'], 'ssd_conv_b': out['ssd_conv_b'], 'ssd_dt_bias': out['ssd_dt_bias'], 'ssd_a_log': out['ssd_a_log'], 'ssd_d': out['ssd_d'], 'ssd_norm_w': out['ssd_norm_w'], 'lru_conv_w': out['lru_conv_w'], 'lru_conv_b': out['lru_conv_b'], 'lru_wa': out['lru_wa'], 'lru_ba': out['lru_ba'], 'lru_wi': out['lru_wi'], 'lru_bi': out['lru_bi'], 'lru_lambda': out['lru_lambda'], 'w_br_ssd': out['w_br_ssd'], 'w_br_lru': out['w_br_lru'], 'w_out': out['w_out'], 'ln1_g': out['ln1_g'], 'ln1_b': out['ln1_b'], 'w_mlp1': out['w_mlp1'], 'b_mlp1': out['b_mlp1'], 'w_mlp2': out['w_mlp2'], 'b_mlp2': out['b_mlp2'], 'ln2_g': out['ln2_g'], 'ln2_b': out['ln2_b'], 'loss_target': out['loss_target'], 'm_c_ctx': out['m_c_ctx'], 'm_w_mod': out['m_w_mod'], 'm_b_mod': out['m_b_mod'], 'm_w_in': out['m_w_in'], 'm_b_gate': out['m_b_gate'], 'm_ssd_conv_w': out['m_ssd_conv_w'], 'm_ssd_conv_b': out['m_ssd_conv_b'], 'm_ssd_dt_bias': out['m_ssd_dt_bias'], 'm_ssd_a_log': out['m_ssd_a_log'], 'm_ssd_d': out['m_ssd_d'], 'm_ssd_norm_w': out['m_ssd_norm_w'], 'm_lru_conv_w': out['m_lru_conv_w'], 'm_lru_conv_b': out['m_lru_conv_b'], 'm_lru_wa': out['m_lru_wa'], 'm_lru_ba': out['m_lru_ba'], 'm_lru_wi': out['m_lru_wi'], 'm_lru_bi': out['m_lru_bi'], 'm_lru_lambda': out['m_lru_lambda'], 'm_w_br_ssd': out['m_w_br_ssd'], 'm_w_br_lru': out['m_w_br_lru'], 'm_w_out': out['m_w_out'], 'm_ln1_g': out['m_ln1_g'], 'm_ln1_b': out['m_ln1_b'], 'm_w_mlp1': out['m_w_mlp1'], 'm_b_mlp1': out['m_b_mlp1'], 'm_w_mlp2': out['m_w_mlp2'], 'm_b_mlp2': out['m_b_mlp2'], 'm_ln2_g': out['m_ln2_g'], 'm_ln2_b': out['m_ln2_b'], 'v_c_ctx': out['v_c_ctx'], 'v_w_mod': out['v_w_mod'], 'v_b_mod': out['v_b_mod'], 'v_w_in': out['v_w_in'], 'v_b_gate': out['v_b_gate'], 'v_ssd_conv_w': out['v_ssd_conv_w'], 'v_ssd_conv_b': out['v_ssd_conv_b'], 'v_ssd_dt_bias': out['v_ssd_dt_bias'], 'v_ssd_a_log': out['v_ssd_a_log'], 'v_ssd_d': out['v_ssd_d'], 'v_ssd_norm_w': out['v_ssd_norm_w'], 'v_lru_conv_w': out['v_lru_conv_w'], 'v_lru_conv_b': out['v_lru_conv_b'], 'v_lru_wa': out['v_lru_wa'], 'v_lru_ba': out['v_lru_ba'], 'v_lru_wi': out['v_lru_wi'], 'v_lru_bi': out['v_lru_bi'], 'v_lru_lambda': out['v_lru_lambda'], 'v_w_br_ssd': out['v_w_br_ssd'], 'v_w_br_lru': out['v_w_br_lru'], 'v_w_out': out['v_w_out'], 'v_ln1_g': out['v_ln1_g'], 'v_ln1_b': out['v_ln1_b'], 'v_w_mlp1': out['v_w_mlp1'], 'v_b_mlp1': out['v_b_mlp1'], 'v_w_mlp2': out['v_w_mlp2'], 'v_b_mlp2': out['v_b_mlp2'], 'v_ln2_g': out['v_ln2_g'], 'v_ln2_b': out['v_ln2_b']}


def _loss(weights, diff, rest, loss_target):
    with _jax.named_scope("forward"):
        args = {**rest, TWIN_DIFF_INPUT: diff, **{k: w.astype(_WEIGHT_DTYPES[k]) for k, w in weights.items()}}
        y = _forward(args)
    with _jax.named_scope("loss_head"):
        err = _jnp.square(y.astype(_jnp.float32) - loss_target)
        return 0.5 * _jnp.sum(_jnp.mean(err, axis=-1)) if err.ndim else 0.5 * err


def _adamw(w, g, m, v):
    m = ADAM_B1 * m + (1.0 - ADAM_B1) * g
    v = ADAM_B2 * v + (1.0 - ADAM_B2) * _jnp.square(g)
    m_hat = m / (1.0 - ADAM_B1 ** ADAM_STEP)
    v_hat = v / (1.0 - ADAM_B2 ** ADAM_STEP)
    delta = -ADAM_LR * (m_hat / (_jnp.sqrt(v_hat) + ADAM_EPS) + ADAM_WD * w)
    return delta, m, v


def reference(x, c, ctx, c_ctx, w_mod, b_mod, w_in, b_gate, ssd_conv_w, ssd_conv_b, ssd_dt_bias, ssd_a_log, ssd_d, ssd_norm_w, lru_conv_w, lru_conv_b, lru_wa, lru_ba, lru_wi, lru_bi, lru_lambda, w_br_ssd, w_br_lru, w_out, ln1_g, ln1_b, w_mlp1, b_mlp1, w_mlp2, b_mlp2, ln2_g, ln2_b, loss_target, m_c_ctx, m_w_mod, m_b_mod, m_w_in, m_b_gate, m_ssd_conv_w, m_ssd_conv_b, m_ssd_dt_bias, m_ssd_a_log, m_ssd_d, m_ssd_norm_w, m_lru_conv_w, m_lru_conv_b, m_lru_wa, m_lru_ba, m_lru_wi, m_lru_bi, m_lru_lambda, m_w_br_ssd, m_w_br_lru, m_w_out, m_ln1_g, m_ln1_b, m_w_mlp1, m_b_mlp1, m_w_mlp2, m_b_mlp2, m_ln2_g, m_ln2_b, v_c_ctx, v_w_mod, v_b_mod, v_w_in, v_b_gate, v_ssd_conv_w, v_ssd_conv_b, v_ssd_dt_bias, v_ssd_a_log, v_ssd_d, v_ssd_norm_w, v_lru_conv_w, v_lru_conv_b, v_lru_wa, v_lru_ba, v_lru_wi, v_lru_bi, v_lru_lambda, v_w_br_ssd, v_w_br_lru, v_w_out, v_ln1_g, v_ln1_b, v_w_mlp1, v_b_mlp1, v_w_mlp2, v_b_mlp2, v_ln2_g, v_ln2_b):
    given = dict(x=x, c=c, ctx=ctx, c_ctx=c_ctx, w_mod=w_mod, b_mod=b_mod, w_in=w_in, b_gate=b_gate, ssd_conv_w=ssd_conv_w, ssd_conv_b=ssd_conv_b, ssd_dt_bias=ssd_dt_bias, ssd_a_log=ssd_a_log, ssd_d=ssd_d, ssd_norm_w=ssd_norm_w, lru_conv_w=lru_conv_w, lru_conv_b=lru_conv_b, lru_wa=lru_wa, lru_ba=lru_ba, lru_wi=lru_wi, lru_bi=lru_bi, lru_lambda=lru_lambda, w_br_ssd=w_br_ssd, w_br_lru=w_br_lru, w_out=w_out, ln1_g=ln1_g, ln1_b=ln1_b, w_mlp1=w_mlp1, b_mlp1=b_mlp1, w_mlp2=w_mlp2, b_mlp2=b_mlp2, ln2_g=ln2_g, ln2_b=ln2_b, loss_target=loss_target, m_c_ctx=m_c_ctx, m_w_mod=m_w_mod, m_b_mod=m_b_mod, m_w_in=m_w_in, m_b_gate=m_b_gate, m_ssd_conv_w=m_ssd_conv_w, m_ssd_conv_b=m_ssd_conv_b, m_ssd_dt_bias=m_ssd_dt_bias, m_ssd_a_log=m_ssd_a_log, m_ssd_d=m_ssd_d, m_ssd_norm_w=m_ssd_norm_w, m_lru_conv_w=m_lru_conv_w, m_lru_conv_b=m_lru_conv_b, m_lru_wa=m_lru_wa, m_lru_ba=m_lru_ba, m_lru_wi=m_lru_wi, m_lru_bi=m_lru_bi, m_lru_lambda=m_lru_lambda, m_w_br_ssd=m_w_br_ssd, m_w_br_lru=m_w_br_lru, m_w_out=m_w_out, m_ln1_g=m_ln1_g, m_ln1_b=m_ln1_b, m_w_mlp1=m_w_mlp1, m_b_mlp1=m_b_mlp1, m_w_mlp2=m_w_mlp2, m_b_mlp2=m_b_mlp2, m_ln2_g=m_ln2_g, m_ln2_b=m_ln2_b, v_c_ctx=v_c_ctx, v_w_mod=v_w_mod, v_b_mod=v_b_mod, v_w_in=v_w_in, v_b_gate=v_b_gate, v_ssd_conv_w=v_ssd_conv_w, v_ssd_conv_b=v_ssd_conv_b, v_ssd_dt_bias=v_ssd_dt_bias, v_ssd_a_log=v_ssd_a_log, v_ssd_d=v_ssd_d, v_ssd_norm_w=v_ssd_norm_w, v_lru_conv_w=v_lru_conv_w, v_lru_conv_b=v_lru_conv_b, v_lru_wa=v_lru_wa, v_lru_ba=v_lru_ba, v_lru_wi=v_lru_wi, v_lru_bi=v_lru_bi, v_lru_lambda=v_lru_lambda, v_w_br_ssd=v_w_br_ssd, v_w_br_lru=v_w_br_lru, v_w_out=v_w_out, v_ln1_g=v_ln1_g, v_ln1_b=v_ln1_b, v_w_mlp1=v_w_mlp1, v_b_mlp1=v_b_mlp1, v_w_mlp2=v_w_mlp2, v_b_mlp2=v_b_mlp2, v_ln2_g=v_ln2_g, v_ln2_b=v_ln2_b)
    weights = {n: given[n] for n in TWIN_WEIGHTS}
    shared = {n: given[n] for n in SHARED_INPUTS}
    per_example = {n: given[n] for n in ['x', 'c', 'ctx']}
    grad_fn = _jax.value_and_grad(_loss, argnums=(0, 1))

    def one_microbatch(ex, loss_target):
        ex = dict(ex)
        diff = ex.pop(TWIN_DIFF_INPUT)
        return grad_fn(weights, diff, {**shared, **ex}, loss_target)

    if N_MICROBATCH == 1:
        loss, (grad_w, grad_x) = one_microbatch(per_example, given["loss_target"])
    else:
        def body(carry, xs):
            loss_sum, grad_sum = carry
            l_k, (gw_k, gx_k) = one_microbatch(xs[0], xs[1])
            with _jax.named_scope("update"):
                return (loss_sum + l_k, _jax.tree.map(_jnp.add, grad_sum, gw_k)), gx_k

        init = (_jnp.zeros((), _jnp.float32), _jax.tree.map(_jnp.zeros_like, weights))
        (loss, grad_w), grad_x = _jax.lax.scan(body, init, (per_example, given["loss_target"]))
    with _jax.named_scope("update"):
        delta_w, new_m, new_v = {}, {}, {}
        for n in TWIN_WEIGHTS:
            delta_w[n], new_m[n], new_v[n] = _adamw(weights[n], grad_w[n], given["m_" + n], given["v_" + n])
    return (loss, grad_x, *[grad_w[n] for n in TWIN_WEIGHTS], *[delta_w[n] for n in TWIN_WEIGHTS],
            *[new_m[n] for n in TWIN_WEIGHTS], *[new_v[n] for n in TWIN_WEIGHTS])
```

```python
import functools
import math

import jax
import jax.numpy as jnp
from jax import lax
from jax.experimental import pallas as pl
from jax.experimental.pallas import tpu as pltpu

F32 = jnp.float32
BF16 = jnp.bfloat16

D = 1024
GRID_W = 64
DI = 2048
NH = 32
HD = 64
NG = 8
HPG = 4
NS = 128
CH = 128
LW = 1024
LB = 8
LBW = 128
LRU_C = 8.0
MLP = 4096
NMOD = 6
ALPHA = 2.0 ** 0.25
LN_EPS = 1e-6
RMS_EPS = 1e-5
PM = 10240
DTW = 128
CONVW = 5120
NDEV = 8

ADAM_LR = 0.001
ADAM_B1 = 0.9
ADAM_B2 = 0.999
ADAM_EPS = 1e-08
ADAM_WD = 0.01
ADAM_STEP = 10

VMEM_LIMIT_BYTES = 56 * 1024 * 1024


def _cp(n_axes):
    return pltpu.CompilerParams(dimension_semantics=("arbitrary",) * n_axes,
                                vmem_limit_bytes=VMEM_LIMIT_BYTES)


def _sigmoid(x):
    return 1.0 / (1.0 + jnp.exp(-x))


def _silu(x):
    return x * _sigmoid(x)


def _silu_grad(x):
    s = _sigmoid(x)
    return s * (1.0 + x * (1.0 - s))


def _log1p_pos(e):
    return jnp.where(e < 1e-2, e * (1.0 - e * (0.5 - e * (1.0 / 3.0))), jnp.log(1.0 + e))


def _softplus(x):
    return jnp.maximum(x, 0.0) + _log1p_pos(jnp.exp(-jnp.abs(x)))


def _neg_expm1(x):
    series = -x * (1.0 + x * (0.5 + x * (1.0 / 6.0 + x * (1.0 / 24.0))))
    return jnp.where(x > -1e-2, series, 1.0 - jnp.exp(x))


_GELU_K = math.sqrt(2.0 / math.pi)


def _gelu(x):
    t = jnp.tanh(_GELU_K * (x + 0.044715 * x * x * x))
    return 0.5 * x * (1.0 + t)


def _gelu_grad(x):
    t = jnp.tanh(_GELU_K * (x + 0.044715 * x * x * x))
    dt = (1.0 - t * t) * _GELU_K * (1.0 + 3.0 * 0.044715 * x * x)
    return 0.5 * (1.0 + t) + 0.5 * x * dt


def _ln(x):
    mu = jnp.mean(x, axis=-1, keepdims=True)
    xc = x - mu
    var = jnp.mean(xc * xc, axis=-1, keepdims=True)
    rs = lax.rsqrt(var + LN_EPS)
    return xc * rs, rs


def _ln_bwd(dy, xhat, rs):
    m1 = jnp.mean(dy, axis=-1, keepdims=True)
    m2 = jnp.mean(dy * xhat, axis=-1, keepdims=True)
    return rs * (dy - m1 - xhat * m2)


def _dot(a, b):
    return lax.dot_general(a, b, (((1,), (0,)), ((), ())), preferred_element_type=F32)


def _dot_nt(a, b):
    return lax.dot_general(a, b, (((1,), (1,)), ((), ())), preferred_element_type=F32)


def _dot_tn(a, b):
    return lax.dot_general(a, b, (((0,), (0,)), ((), ())), preferred_element_type=F32)


def _split3(a):
    a0 = a.astype(BF16)
    r = a - a0.astype(F32)
    a1 = r.astype(BF16)
    a2 = (r - a1.astype(F32)).astype(BF16)
    return a0, a1, a2


def _dot_exact_l(m_bf, a):
    a0, a1, a2 = _split3(a)
    return _dot(m_bf, a0) + _dot(m_bf, a1) + _dot(m_bf, a2)


def _dot_exact_r(a, m_bf):
    a0, a1, a2 = _split3(a)
    return _dot(a0, m_bf) + _dot(a1, m_bf) + _dot(a2, m_bf)


def _tri(n, upper):
    ii = lax.broadcasted_iota(jnp.int32, (n, n), 0)
    kk = lax.broadcasted_iota(jnp.int32, (n, n), 1)
    m = (kk >= ii) if upper else (kk <= ii)
    return jnp.where(m, 1.0, 0.0).astype(BF16)


def _fit(n, t):
    t = min(t, n)
    while n % t:
        t //= 2
    return t


def _mm(a, b, mode, name, out_dtype=F32, tm=512, tn=512, tk=512):
    if mode == "nn":
        M, K = a.shape
        N = b.shape[1]
    elif mode == "nt":
        M, K = a.shape
        N = b.shape[0]
    else:
        K, M = a.shape
        N = b.shape[1]
    tm, tn, tk = _fit(M, tm), _fit(N, tn), _fit(K, tk)
    assert M % tm == 0 and N % tn == 0 and K % tk == 0, (name, M, N, K, tm, tn, tk)
    nk = K // tk
    if mode == "tn":
        a_spec = pl.BlockSpec((tk, tm), lambda i, j, k: (k, i))
    else:
        a_spec = pl.BlockSpec((tm, tk), lambda i, j, k: (i, k))
    if mode == "nt":
        b_spec = pl.BlockSpec((tn, tk), lambda i, j, k: (j, k))
    else:
        b_spec = pl.BlockSpec((tk, tn), lambda i, j, k: (k, j))
    dn = {"nn": (((1,), (0,)), ((), ())), "nt": (((1,), (1,)), ((), ())), "tn": (((0,), (0,)), ((), ()))}[mode]

    def body(a_ref, b_ref, o_ref, acc_ref):
        k = pl.program_id(2)

        @pl.when(k == 0)
        def _():
            acc_ref[...] = jnp.zeros_like(acc_ref)

        acc_ref[...] += lax.dot_general(a_ref[...].astype(BF16), b_ref[...].astype(BF16), dn,
                                        preferred_element_type=F32)

        @pl.when(k == nk - 1)
        def _():
            o_ref[...] = acc_ref[...].astype(o_ref.dtype)

    return pl.pallas_call(
        body, name=name, grid=(M // tm, N // tn, nk),
        in_specs=[a_spec, b_spec],
        out_specs=pl.BlockSpec((tm, tn), lambda i, j, k: (i, j)),
        out_shape=jax.ShapeDtypeStruct((M, N), out_dtype),
        scratch_shapes=[pltpu.VMEM((tm, tn), F32)],
        compiler_params=_cp(3),
    )(a, b)


class _Cfg:
    def __init__(self, Bn, T, Tc):
        assert T % Tc == 0 and Tc % CH == 0 and Tc % GRID_W == 0
        self.Bn, self.T, self.Tc = Bn, T, Tc
        self.TT = T + Tc
        self.TB = Tc
        self.nbt = self.TT // self.TB
        self.nbl = T // self.TB
        self.NT = Bn * self.TT
        self.N = Bn * T
        self.nct = Tc // CH
        self.nlt = T // CH
        self.nch = self.nct + self.nlt


def _ln_mod_fwd(cfg, xs, shift_tab, scale_tab):
    TB, nbt = cfg.TB, cfg.nbt

    def body(x_ref, sh_ref, sc_ref, o_ref):
        xhat, _ = _ln(x_ref[...])
        o_ref[...] = (xhat * (1.0 + sc_ref[...]) + sh_ref[...]).astype(BF16)

    tab = pl.BlockSpec((None, 1, D), lambda b, j: (2 * b + jnp.minimum(j, 1), 0, 0))
    return pl.pallas_call(
        body, name="ln_mod_fwd", grid=(cfg.Bn, nbt),
        in_specs=[pl.BlockSpec((TB, D), lambda b, j: (b * nbt + j, 0)), tab, tab],
        out_specs=pl.BlockSpec((TB, D), lambda b, j: (b * nbt + j, 0)),
        out_shape=jax.ShapeDtypeStruct((cfg.NT, D), BF16),
        compiler_params=_cp(2),
    )(xs, shift_tab, scale_tab)


def _dt_fwd(cfg, dt_raw, dt_bias, a_log):
    def body(raw_ref, bias_ref, alog_ref, dt_ref, cum_ref):
        dt = _softplus(raw_ref[...] + bias_ref[...])
        a = -jnp.exp(alog_ref[...])
        dta = dt * a
        col = lax.broadcasted_iota(jnp.int32, (CH, DTW), 1)
        cf = _dot_exact_l(_tri(CH, False), dta)
        cr = _dot_exact_l(_tri(CH, True), dta)
        dt_ref[...] = dt
        cum_ref[...] = jnp.where(col < NH, cf, cr)

    blk = pl.BlockSpec((CH, DTW), lambda i: (i, 0))
    row = pl.BlockSpec((1, DTW), lambda i: (0, 0))
    return pl.pallas_call(
        body, name="dt_fwd", grid=(cfg.NT // CH,),
        in_specs=[blk, row, row], out_specs=[blk, blk],
        out_shape=[jax.ShapeDtypeStruct((cfg.NT, DTW), F32)] * 2,
        compiler_params=_cp(1),
    )(dt_raw, dt_bias, a_log)


def _dt_bwd(cfg, dA, dxx, dt_raw, dt, dt_bias, a_log):
    def body(dA_ref, dxx_ref, raw_ref, dt_ref, bias_ref, alog_ref, o_ref, acc_ref):
        i = pl.program_id(0)

        @pl.when(i == 0)
        def _():
            acc_ref[...] = jnp.zeros_like(acc_ref)

        a = -jnp.exp(alog_ref[...])
        col = lax.broadcasted_iota(jnp.int32, (CH, DTW), 1)
        dA_v = dA_ref[...]
        ddta = jnp.where(col < NH, _dot_exact_l(_tri(CH, True), dA_v), _dot_exact_l(_tri(CH, False), dA_v))
        dtv = dt_ref[...]
        ddt = ddta * a + dxx_ref[...]
        draw = ddt * _sigmoid(raw_ref[...] + bias_ref[...])
        draw = jnp.where(col < 2 * NH, draw, 0.0)
        o_ref[...] = draw
        da = jnp.sum(ddta * dtv, axis=0, keepdims=True) * a
        da = jnp.where(col[:1] < 2 * NH, da, 0.0)
        acc_ref[0:1, :] += da
        acc_ref[1:2, :] += jnp.sum(draw, axis=0, keepdims=True)

    blk = pl.BlockSpec((CH, DTW), lambda i: (i, 0))
    row = pl.BlockSpec((1, DTW), lambda i: (0, 0))
    return pl.pallas_call(
        body, name="dt_bwd", grid=(cfg.NT // CH,),
        in_specs=[blk, blk, blk, blk, row, row],
        out_specs=[blk, pl.BlockSpec((8, DTW), lambda i: (0, 0))],
        out_shape=[jax.ShapeDtypeStruct((cfg.NT, DTW), F32), jax.ShapeDtypeStruct((8, DTW), F32)],
        compiler_params=_cp(1),
    )(dA, dxx, dt_raw, dt, dt_bias, a_log)


def _conv_shift(u, s, pos, R):
    n = u.shape[0]
    rolled = pltpu.roll(u, s % n, 0)
    ok = jnp.logical_and(pos - s >= 0, pos - s < R)
    return jnp.where(ok, rolled, 0.0)


_TAPS = (2, 1, 0, -1)


def _conv_fwd(cfg, proj, conv_w, conv_b):
    TB, nbt = cfg.TB, cfg.nbt
    CB = 512
    n_act = (DI + 2 * NG * NS) // CB

    def body(u_ref, w_ref, b_ref, o_ref):
        i = pl.program_id(0)
        j = pl.program_id(1)
        R = jnp.where(i % nbt == 0, cfg.Tc, GRID_W)
        t = lax.broadcasted_iota(jnp.int32, (TB, CB), 0)
        pos = jnp.bitwise_and(t, R - 1)
        u = u_ref[...]
        pre = b_ref[...] + w_ref[2:3, :] * u
        for k in (0, 1, 3):
            pre = pre + w_ref[k:k + 1, :] * _conv_shift(u, _TAPS[k], pos, R)
        o_ref[...] = jnp.where(j < n_act, _silu(pre), pre)

    return pl.pallas_call(
        body, name="conv_fwd", grid=(cfg.NT // TB, CONVW // CB),
        in_specs=[pl.BlockSpec((TB, CB), lambda i, j: (i, j)),
                  pl.BlockSpec((4, CB), lambda i, j: (0, j)),
                  pl.BlockSpec((1, CB), lambda i, j: (0, j))],
        out_specs=pl.BlockSpec((TB, CB), lambda i, j: (i, j)),
        out_shape=jax.ShapeDtypeStruct((cfg.NT, CONVW), F32),
        compiler_params=_cp(2),
    )(proj, conv_w, conv_b)


def _conv_bwd(cfg, name, proj, conv_w, conv_b, addends, col0, width, act, skip=None):
    TB, nbt, nbl = cfg.TB, cfg.nbt, cfg.nbl
    CB = 512
    c0 = col0 // CB
    n_add = len(addends)

    def body(*refs):
        u_ref, w_ref, b_ref = refs[:3]
        add_refs = refs[3:3 + n_add]
        rest = refs[3 + n_add:]
        if skip is not None:
            dy_ref, dv_ref = rest[:2]
            rest = rest[2:]
        o_ref, acc_ref = rest
        i = pl.program_id(1)

        @pl.when(i == 0)
        def _():
            acc_ref[...] = jnp.zeros_like(acc_ref)

        isctx = (i % nbt) == 0
        R = jnp.where(isctx, cfg.Tc, GRID_W)
        t = lax.broadcasted_iota(jnp.int32, (TB, CB), 0)
        pos = jnp.bitwise_and(t, R - 1)
        u = u_ref[...]
        us = [_conv_shift(u, _TAPS[k], pos, R) for k in range(4)]
        g = add_refs[0][...]
        for r in add_refs[1:]:
            g = g + r[...]
        if skip is not None:
            g = g + jnp.where(isctx, 0.0, dv_ref[...] * dy_ref[...])
        if act:
            pre = b_ref[...]
            for k in range(4):
                pre = pre + w_ref[k:k + 1, :] * us[k]
            g = g * _silu_grad(pre)
        dp = jnp.zeros_like(g)
        for k in range(4):
            acc_ref[k:k + 1, :] += jnp.sum(g * us[k], axis=0, keepdims=True)
            dp = dp + w_ref[k:k + 1, :] * _conv_shift(g, -_TAPS[k], pos, R)
        acc_ref[4:5, :] += jnp.sum(g, axis=0, keepdims=True)
        o_ref[...] = dp.astype(BF16)

    blk = pl.BlockSpec((TB, CB), lambda j, i: (i, j))
    in_specs = [pl.BlockSpec((TB, CB), lambda j, i: (i, c0 + j)),
                pl.BlockSpec((4, CB), lambda j, i: (0, c0 + j)),
                pl.BlockSpec((1, CB), lambda j, i: (0, c0 + j))] + [blk] * n_add
    args = [proj, conv_w, conv_b] + list(addends)
    if skip is not None:
        def lat(j, i):
            b = i // nbt
            return (b * nbl + jnp.maximum(i % nbt - 1, 0), j)
        in_specs += [pl.BlockSpec((TB, CB), lat), pl.BlockSpec((1, CB), lambda j, i: (0, j))]
        args += list(skip)
    return pl.pallas_call(
        body, name=name, grid=(width // CB, cfg.NT // TB),
        in_specs=in_specs,
        out_specs=[blk, pl.BlockSpec((8, CB), lambda j, i: (0, j))],
        out_shape=[jax.ShapeDtypeStruct((cfg.NT, width), BF16), jax.ShapeDtypeStruct((8, width), F32)],
        compiler_params=_cp(2),
    )(*args)


def _chunk_of_step(cfg, rev):
    nct, nlt = cfg.nct, cfg.nlt
    if not rev:
        return lambda s: s
    return lambda s: jnp.where(s < nct, nct - 1 - s, 2 * nct + nlt - 1 - s)


def _expand4(v4, band):
    out = v4[:, 3:4]
    for h in (2, 1, 0):
        out = jnp.where(band == h, v4[:, h:h + 1], out)
    return out


def _ssd_common(x_ref, b_ref, c_ref, dt_ref, cum_ref, cumT_ref, rev):
    xh = x_ref[...]
    Bm = b_ref[...].astype(BF16)
    Cm = c_ref[...].astype(BF16)
    dt4 = dt_ref[...]
    cum4 = cum_ref[...]
    cumT4 = cumT_ref[...]
    band = lax.broadcasted_iota(jnp.int32, (CH, HPG * HD), 1) // HD
    last = 0 if rev else CH - 1
    llast = cum4[last:last + 1, :]
    e_exp = _expand4(jnp.exp(llast - cum4), band)
    ecum = _expand4(jnp.exp(cum4), band)
    dt_exp = _expand4(dt4, band)
    X = xh * dt_exp
    G = _dot_nt(Cm, Bm)
    ii = lax.broadcasted_iota(jnp.int32, (CH, CH), 0)
    jj = lax.broadcasted_iota(jnp.int32, (CH, CH), 1)
    mask = (jj >= ii) if rev else (jj <= ii)
    decs = []
    for h in range(HPG):
        seg = cum4[:, h:h + 1] - cumT4[h:h + 1, :]
        decs.append(jnp.exp(jnp.where(mask, seg, -1e30)))
    elast = jnp.exp(llast)
    rb = lax.broadcasted_iota(jnp.int32, (HPG * HD, NS), 0) // HD
    dec_rows = elast[:, 3:4]
    for h in (2, 1, 0):
        dec_rows = jnp.where(rb == h, elast[:, h:h + 1], dec_rows)
    return xh, Bm, Cm, band, e_exp, ecum, dt_exp, X, G, decs, elast, dec_rows, last


def _ssd_specs(cfg, rev):
    nch = cfg.nch
    cmap = _chunk_of_step(cfg, rev)
    d = 1 if rev else 0

    def make(stepmap):
        def row(b, g, sp):
            return b * nch + cmap(stepmap(sp))
        return [
            pl.BlockSpec((CH, HPG * HD), lambda b, g, sp: (row(b, g, sp), g)),
            pl.BlockSpec((CH, NS), lambda b, g, sp: (row(b, g, sp), DI // NS + g)),
            pl.BlockSpec((CH, NS), lambda b, g, sp: (row(b, g, sp), (DI + NG * NS) // NS + g)),
            pl.BlockSpec((None, None, CH, HPG), lambda b, g, sp: (d, g, row(b, g, sp), 0)),
            pl.BlockSpec((None, None, CH, HPG), lambda b, g, sp: (d, g, row(b, g, sp), 0)),
            pl.BlockSpec((None, None, None, HPG, CH), lambda b, g, sp: (d, g, row(b, g, sp), 0, 0)),
        ], row
    return make


def _ssd_fwd(cfg, act, dtg, cumg, cumTg, rev):
    nch = cfg.nch
    in_specs, row = _ssd_specs(cfg, rev)(lambda sp: sp)

    def body(x_ref, b_ref, c_ref, dt_ref, cum_ref, cumT_ref, y_ref, hs_ref, h_scr):
        s = pl.program_id(2)

        @pl.when(s == 0)
        def _():
            h_scr[...] = jnp.zeros_like(h_scr)

        xh, Bm, Cm, band, e_exp, ecum, dt_exp, X, G, decs, elast, dec_rows, last = _ssd_common(
            x_ref, b_ref, c_ref, dt_ref, cum_ref, cumT_ref, rev)
        H = h_scr[...]
        Y = ecum * _dot_nt(Cm, H.astype(BF16))
        for h in range(HPG):
            Mh = (G * decs[h]).astype(BF16)
            Y = Y + _dot(Mh, jnp.where(band == h, X, 0.0).astype(BF16))
        y_ref[...] = Y
        hs_ref[...] = H
        S = _dot_tn((e_exp * X).astype(BF16), Bm)
        h_scr[...] = dec_rows * H + S

    return pl.pallas_call(
        body, name="ssd_fwd_rev" if rev else "ssd_fwd", grid=(cfg.Bn, NG, nch),
        in_specs=in_specs,
        out_specs=[pl.BlockSpec((CH, HPG * HD), lambda b, g, s: (row(b, g, s), g)),
                   pl.BlockSpec((None, None, None, HPG * HD, NS), lambda b, g, s: (b, g, s, 0, 0))],
        out_shape=[jax.ShapeDtypeStruct((cfg.NT, DI), F32),
                   jax.ShapeDtypeStruct((cfg.Bn, NG, nch, HPG * HD, NS), F32)],
        scratch_shapes=[pltpu.VMEM((HPG * HD, NS), F32)],
        compiler_params=_cp(3),
    )(act, act, act, dtg, cumg, cumTg)


def _ssd_bwd(cfg, act, dtg, cumg, cumTg, hs, dy, rev):
    nch, nct, nlt = cfg.nch, cfg.nct, cfg.nlt
    cmap = _chunk_of_step(cfg, rev)
    in_specs, row = _ssd_specs(cfg, rev)(lambda sp: nch - 1 - sp)

    def lat_row(b, g, sp):
        c = cmap(nch - 1 - sp)
        return b * nlt + jnp.maximum(c - nct, 0)

    def body(x_ref, b_ref, c_ref, dt_ref, cum_ref, cumT_ref, dy_ref, hs_ref,
             dxh_ref, dB_ref, dC_ref, dA_ref, dxx_ref, dh_scr):
        sp = pl.program_id(2)

        @pl.when(sp == 0)
        def _():
            dh_scr[...] = jnp.zeros_like(dh_scr)

        c = cmap(nch - 1 - sp)
        xh, Bm, Cm, band, e_exp, ecum, dt_exp, X, G, decs, elast, dec_rows, last = _ssd_common(
            x_ref, b_ref, c_ref, dt_ref, cum_ref, cumT_ref, rev)
        dY = jnp.where(c < nct, 0.0, dy_ref[...])
        Xb = X.astype(BF16)
        H = hs_ref[...]
        Hb = H.astype(BF16)
        dHn = dh_scr[...]
        dHnb = dHn.astype(BF16)
        dYs = ecum * dY
        dYsb = dYs.astype(BF16)
        BdH = _dot_nt(Bm, dHnb)
        Ys = ecum * _dot_nt(Cm, Hb)
        dX = e_exp * BdH
        dG = jnp.zeros((CH, CH), F32)
        lane4 = lax.broadcasted_iota(jnp.int32, (CH, HPG), 1)
        dA = jnp.zeros((CH, HPG), F32)
        for h in range(HPG):
            M = G * decs[h]
            dYh = jnp.where(band == h, dY, 0.0).astype(BF16)
            dM = _dot_nt(dYh, Xb)
            W = dM * M
            dG = dG + dM * decs[h]
            dX = dX + _dot_tn(M.astype(BF16), dYh)
            colv = jnp.sum(W - W.T, axis=1, keepdims=True)
            dA = dA + jnp.where(lane4 == h, colv, 0.0)
        dGb = dG.astype(BF16)
        eX = e_exp * X
        dC_ref[...] = _dot(dGb, Bm) + _dot(dYsb, Hb)
        dB_ref[...] = _dot_tn(dGb, Cm) + _dot(eX.astype(BF16), dHnb)
        dh_scr[...] = dec_rows * dHn + _dot_tn(dYsb, Cm)
        pb = lax.broadcasted_iota(jnp.int32, (HPG * HD, NS), 0) // HD
        pl_ = lax.broadcasted_iota(jnp.int32, (HPG * HD, NS), 1)
        E = jnp.where(pb == pl_, 1.0, 0.0).astype(BF16)
        t2 = _dot_exact_r(dY * Ys, E)[:, :HPG]
        q = _dot_exact_r(eX * BdH, E)[:, :HPG]
        r = jnp.sum(dHn * H, axis=1, keepdims=True)
        lane1 = lax.broadcasted_iota(jnp.int32, (1, HPG), 1)
        hdot = jnp.zeros((1, HPG), F32)
        for h in range(HPG):
            hv = jnp.sum(r[h * HD:(h + 1) * HD, :], axis=0, keepdims=True)
            hdot = hdot + jnp.where(lane1 == h, hv, 0.0)
        dllast = jnp.sum(q, axis=0, keepdims=True) + elast * hdot
        rowi = lax.broadcasted_iota(jnp.int32, (CH, HPG), 0)
        dA_ref[...] = dA + t2 - q + jnp.where(rowi == last, dllast, 0.0)
        dxh_ref[...] = dX * dt_exp
        dxx_ref[...] = _dot_exact_r(dX * xh, E)[:, :HPG]

    d = 1 if rev else 0
    small = pl.BlockSpec((None, CH, HPG), lambda b, g, sp: (g, row(b, g, sp), 0))
    return pl.pallas_call(
        body, name="ssd_bwd_rev" if rev else "ssd_bwd", grid=(cfg.Bn, NG, nch),
        in_specs=in_specs + [
            pl.BlockSpec((CH, HPG * HD), lambda b, g, sp: (lat_row(b, g, sp), g)),
            pl.BlockSpec((None, None, None, HPG * HD, NS), lambda b, g, sp: (b, g, nch - 1 - sp, 0, 0))],
        out_specs=[pl.BlockSpec((CH, HPG * HD), lambda b, g, sp: (row(b, g, sp), g)),
                   pl.BlockSpec((CH, NS), lambda b, g, sp: (row(b, g, sp), g)),
                   pl.BlockSpec((CH, NS), lambda b, g, sp: (row(b, g, sp), g)),
                   small, small],
        out_shape=[jax.ShapeDtypeStruct((cfg.NT, DI), F32),
                   jax.ShapeDtypeStruct((cfg.NT, NG * NS), F32),
                   jax.ShapeDtypeStruct((cfg.NT, NG * NS), F32),
                   jax.ShapeDtypeStruct((NG, cfg.NT, HPG), F32),
                   jax.ShapeDtypeStruct((NG, cfg.NT, HPG), F32)],
        scratch_shapes=[pltpu.VMEM((HPG * HD, NS), F32)],
        compiler_params=_cp(3),
    )(act, act, act, dtg, cumg, cumTg, dy, hs)


def _shift_rows(v, s, fill, toward_later, rowi):
    n = v.shape[0]
    if toward_later:
        return jnp.where(rowi >= s, pltpu.roll(v, s, 0), fill)
    return jnp.where(rowi < n - s, pltpu.roll(v, n - s, 0), fill)


def _lru_gates(u, wa_ref, wi_ref, ba_ref, bi_ref, lam_ref):
    rs, is_ = [], []
    for k in range(LB):
        uk = u[:, k * LBW:(k + 1) * LBW].astype(BF16)
        rs.append(_dot(uk, wa_ref[k].astype(BF16)))
        is_.append(_dot(uk, wi_ref[k].astype(BF16)))
    r = _sigmoid(jnp.concatenate(rs, axis=1) + ba_ref[...])
    ig = _sigmoid(jnp.concatenate(is_, axis=1) + bi_ref[...])
    sp = _softplus(-lam_ref[...])
    la = -LRU_C * r * sp
    a = jnp.exp(la)
    g = jnp.sqrt(_neg_expm1(2.0 * la))
    return r, ig, sp, la, a, g


def _lru_w_specs(d):
    return [pl.BlockSpec((None, LB, LBW, LBW), lambda b, s: (d, 0, 0, 0)),
            pl.BlockSpec((None, LB, LBW, LBW), lambda b, s: (d, 0, 0, 0)),
            pl.BlockSpec((None, 1, LW), lambda b, s: (d, 0, 0)),
            pl.BlockSpec((None, 1, LW), lambda b, s: (d, 0, 0)),
            pl.BlockSpec((None, 1, LW), lambda b, s: (d, 0, 0))]


def _lru_fwd(cfg, act, wa, wi, ba, bi, lam, rev):
    nch = cfg.nch
    cmap = _chunk_of_step(cfg, rev)
    d = 1 if rev else 0
    ucol = (DI + 2 * NG * NS) // LW

    def body(u_ref, wa_ref, wi_ref, ba_ref, bi_ref, lam_ref, h_ref, c_scr):
        s = pl.program_id(1)

        @pl.when(s == 0)
        def _():
            c_scr[...] = jnp.zeros_like(c_scr)

        u = u_ref[...]
        r, ig, sp, la, a, g = _lru_gates(u, wa_ref, wi_ref, ba_ref, bi_ref, lam_ref)
        bv = g * ig * u
        rowi = lax.broadcasted_iota(jnp.int32, (CH, LW), 0)
        av = a
        sh = 1
        while sh < CH:
            a_p = _shift_rows(av, sh, 1.0, not rev, rowi)
            b_p = _shift_rows(bv, sh, 0.0, not rev, rowi)
            bv = av * b_p + bv
            av = av * a_p
            sh *= 2
        h = bv + av * c_scr[0:1, :]
        h_ref[...] = h
        lastr = 0 if rev else CH - 1
        c_scr[0:1, :] = h[lastr:lastr + 1, :]

    return pl.pallas_call(
        body, name="lru_fwd_rev" if rev else "lru_fwd", grid=(cfg.Bn, nch),
        in_specs=[pl.BlockSpec((CH, LW), lambda b, s: (b * nch + cmap(s), ucol))] + _lru_w_specs(d),
        out_specs=pl.BlockSpec((CH, LW), lambda b, s: (b * nch + cmap(s), 0)),
        out_shape=jax.ShapeDtypeStruct((cfg.NT, LW), F32),
        scratch_shapes=[pltpu.VMEM((8, LW), F32)],
        compiler_params=_cp(2),
    )(act, wa, wi, ba, bi, lam)


def _lru_bwd(cfg, act, wa, wi, ba, bi, lam, hd, dyl, rev):
    nch, nct, nlt = cfg.nch, cfg.nct, cfg.nlt
    cmap = _chunk_of_step(cfg, rev)
    d = 1 if rev else 0
    ucol = (DI + 2 * NG * NS) // LW

    def srow(b, sp):
        return b * nch + cmap(nch - 1 - sp)

    def prev_rows(b, sp):
        s = nch - 1 - sp
        cp = cmap(jnp.maximum(s - 1, 0))
        base = (b * nch + cp) * (CH // 8)
        return base + (0 if rev else CH // 8 - 1)

    def lat_row(b, sp):
        c = cmap(nch - 1 - sp)
        return b * nlt + jnp.maximum(c - nct, 0)

    def body(u_ref, wa_ref, wi_ref, ba_ref, bi_ref, lam_ref, h_ref, hp_ref, dy_ref,
             du_ref, dwa_ref, dwi_ref, vec_ref, c_scr):
        b = pl.program_id(0)
        sp_id = pl.program_id(1)
        s = nch - 1 - sp_id

        @pl.when(sp_id == 0)
        def _():
            c_scr[...] = jnp.zeros_like(c_scr)

        @pl.when(jnp.logical_and(b == 0, sp_id == 0))
        def _():
            dwa_ref[...] = jnp.zeros_like(dwa_ref)
            dwi_ref[...] = jnp.zeros_like(dwi_ref)
            vec_ref[...] = jnp.zeros_like(vec_ref)

        c = cmap(s)
        u = u_ref[...]
        r, ig, spl, la, a, g = _lru_gates(u, wa_ref, wi_ref, ba_ref, bi_ref, lam_ref)
        dh = jnp.where(c < nct, 0.0, dy_ref[...])
        rowi = lax.broadcasted_iota(jnp.int32, (CH, LW), 0)
        av = _shift_rows(a, 1, 1.0, rev, rowi)
        bv = dh
        sh = 1
        while sh < CH:
            a_n = _shift_rows(av, sh, 1.0, rev, rowi)
            b_n = _shift_rows(bv, sh, 0.0, rev, rowi)
            bv = av * b_n + bv
            av = av * a_n
            sh *= 2
        lamv = bv + av * c_scr[0:1, :]
        first = CH - 1 if rev else 0
        c_scr[0:1, :] = (a * lamv)[first:first + 1, :]
        hprow = hp_ref[...][(0 if rev else 7):(1 if rev else 8), :]
        hprow = jnp.where(s > 0, hprow, 0.0)
        h_prev = _shift_rows(h_ref[...], 1, hprow, not rev, rowi)
        da = lamv * h_prev
        db = lamv
        iu = ig * u
        dla = da * a - db * iu * (a * a) / g
        dr = dla * (-LRU_C * spl)
        di = db * g * u
        du = db * g * ig
        drp = dr * r * (1.0 - r)
        dip = di * ig * (1.0 - ig)
        dus = []
        for k in range(LB):
            sl = slice(k * LBW, (k + 1) * LBW)
            drk = drp[:, sl].astype(BF16)
            dik = dip[:, sl].astype(BF16)
            uk = u[:, sl].astype(BF16)
            dus.append(_dot_nt(drk, wa_ref[k].astype(BF16)) + _dot_nt(dik, wi_ref[k].astype(BF16)))
            dwa_ref[k] += _dot_tn(uk, drk)
            dwi_ref[k] += _dot_tn(uk, dik)
        du_ref[...] = du + jnp.concatenate(dus, axis=1)
        vec_ref[0:1, :] += jnp.sum(drp, axis=0, keepdims=True)
        vec_ref[1:2, :] += jnp.sum(dip, axis=0, keepdims=True)
        dsp = jnp.sum(dla * (-LRU_C * r), axis=0, keepdims=True)
        vec_ref[2:3, :] += dsp * (-_sigmoid(-lam_ref[...]))

    return pl.pallas_call(
        body, name="lru_bwd_rev" if rev else "lru_bwd", grid=(cfg.Bn, nch),
        in_specs=[pl.BlockSpec((CH, LW), lambda b, sp: (srow(b, sp), ucol))] + _lru_w_specs(d) + [
            pl.BlockSpec((CH, LW), lambda b, sp: (srow(b, sp), 0)),
            pl.BlockSpec((8, LW), lambda b, sp: (prev_rows(b, sp), 0)),
            pl.BlockSpec((CH, LW), lambda b, sp: (lat_row(b, sp), 0))],
        out_specs=[pl.BlockSpec((CH, LW), lambda b, sp: (srow(b, sp), 0)),
                   pl.BlockSpec((LB, LBW, LBW), lambda b, sp: (0, 0, 0)),
                   pl.BlockSpec((LB, LBW, LBW), lambda b, sp: (0, 0, 0)),
                   pl.BlockSpec((8, LW), lambda b, sp: (0, 0))],
        out_shape=[jax.ShapeDtypeStruct((cfg.NT, LW), F32),
                   jax.ShapeDtypeStruct((LB, LBW, LBW), F32),
                   jax.ShapeDtypeStruct((LB, LBW, LBW), F32),
                   jax.ShapeDtypeStruct((8, LW), F32)],
        scratch_shapes=[pltpu.VMEM((8, LW), F32)],
        compiler_params=_cp(2),
    )(act, wa, wi, ba, bi, lam, hd, hd, dyl)


HB = 1024


def _post_ssd_fwd(cfg, yf, yb, act, proj, dvec, norm_w):
    TB, nbt, nbl = cfg.TB, cfg.nbt, cfg.nbl
    zc = CONVW // HB

    def body(yf_ref, yb_ref, xh_ref, z_ref, dv_ref, w_ref, o_ref):
        y = yf_ref[...] + yb_ref[...] + dv_ref[...] * xh_ref[...]
        u = y * _silu(z_ref[...])
        for gi in range(HB // (DI // NG)):
            sl = slice(gi * 256, (gi + 1) * 256)
            ug = u[:, sl]
            rs = lax.rsqrt(jnp.mean(ug * ug, axis=1, keepdims=True) + RMS_EPS)
            o_ref[:, sl] = (ug * rs * w_ref[:, sl]).astype(BF16)

    def st(b, j, cb):
        return (b * nbt + 1 + j, cb)
    return pl.pallas_call(
        body, name="post_ssd_fwd", grid=(cfg.Bn, nbl, DI // HB),
        in_specs=[pl.BlockSpec((TB, HB), st), pl.BlockSpec((TB, HB), st), pl.BlockSpec((TB, HB), st),
                  pl.BlockSpec((TB, HB), lambda b, j, cb: (b * nbt + 1 + j, zc + cb)),
                  pl.BlockSpec((1, HB), lambda b, j, cb: (0, cb)),
                  pl.BlockSpec((1, HB), lambda b, j, cb: (0, cb))],
        out_specs=pl.BlockSpec((TB, HB), lambda b, j, cb: (b * nbl + j, cb)),
        out_shape=jax.ShapeDtypeStruct((cfg.N, DI), BF16),
        compiler_params=_cp(3),
    )(yf, yb, act, proj, dvec, norm_w)


def _post_ssd_bwd(cfg, dn, yf, yb, act, proj, dvec, norm_w):
    TB, nbt, nbl = cfg.TB, cfg.nbt, cfg.nbl
    zc = CONVW // HB

    def body(dn_ref, yf_ref, yb_ref, xh_ref, z_ref, dv_ref, w_ref, dy_ref, dz_ref, acc_ref):
        b = pl.program_id(1)
        j = pl.program_id(2)

        @pl.when(jnp.logical_and(b == 0, j == 0))
        def _():
            acc_ref[...] = jnp.zeros_like(acc_ref)

        xh = xh_ref[...]
        z = z_ref[...]
        y = yf_ref[...] + yb_ref[...] + dv_ref[...] * xh
        sz = _silu(z)
        u = y * sz
        dout = dn_ref[...]
        for gi in range(HB // (DI // NG)):
            sl = slice(gi * 256, (gi + 1) * 256)
            ug0 = u[:, sl]
            rs = lax.rsqrt(jnp.mean(ug0 * ug0, axis=1, keepdims=True) + RMS_EPS)
            ug = ug0 * rs
            do = dout[:, sl]
            acc_ref[0:1, sl] += jnp.sum(do * ug, axis=0, keepdims=True)
            dug = do * w_ref[:, sl]
            du = rs * (dug - ug * jnp.mean(dug * ug, axis=1, keepdims=True))
            dy = du * sz[:, sl]
            dy_ref[:, sl] = dy
            dz_ref[:, sl] = (du * y[:, sl] * _silu_grad(z[:, sl])).astype(BF16)
            acc_ref[1:2, sl] += jnp.sum(dy * xh[:, sl], axis=0, keepdims=True)

    def st(cb, b, j):
        return (b * nbt + 1 + j, cb)

    def la(cb, b, j):
        return (b * nbl + j, cb)
    return pl.pallas_call(
        body, name="post_ssd_bwd", grid=(DI // HB, cfg.Bn, nbl),
        in_specs=[pl.BlockSpec((TB, HB), la), pl.BlockSpec((TB, HB), st), pl.BlockSpec((TB, HB), st),
                  pl.BlockSpec((TB, HB), st),
                  pl.BlockSpec((TB, HB), lambda cb, b, j: (b * nbt + 1 + j, zc + cb)),
                  pl.BlockSpec((1, HB), lambda cb, b, j: (0, cb)),
                  pl.BlockSpec((1, HB), lambda cb, b, j: (0, cb))],
        out_specs=[pl.BlockSpec((TB, HB), la), pl.BlockSpec((TB, HB), la),
                   pl.BlockSpec((8, HB), lambda cb, b, j: (0, cb))],
        out_shape=[jax.ShapeDtypeStruct((cfg.N, DI), F32), jax.ShapeDtypeStruct((cfg.N, DI), BF16),
                   jax.ShapeDtypeStruct((8, DI), F32)],
        compiler_params=_cp(3),
    )(dn, yf, yb, act, proj, dvec, norm_w)


def _post_lru_fwd(cfg, hf, hb, proj):
    TB, nbt, nbl = cfg.TB, cfg.nbt, cfg.nbl
    gc = (CONVW + DI) // HB

    def body(hf_ref, hb_ref, g_ref, o_ref):
        o_ref[...] = ((hf_ref[...] + hb_ref[...]) * _gelu(g_ref[...])).astype(BF16)

    st = pl.BlockSpec((TB, HB), lambda b, j: (b * nbt + 1 + j, 0))
    return pl.pallas_call(
        body, name="post_lru_fwd", grid=(cfg.Bn, nbl),
        in_specs=[st, st, pl.BlockSpec((TB, HB), lambda b, j: (b * nbt + 1 + j, gc))],
        out_specs=pl.BlockSpec((TB, HB), lambda b, j: (b * nbl + j, 0)),
        out_shape=jax.ShapeDtypeStruct((cfg.N, LW), BF16),
        compiler_params=_cp(2),
    )(hf, hb, proj)


def _post_lru_bwd(cfg, dv, hf, hb, proj):
    TB, nbt, nbl = cfg.TB, cfg.nbt, cfg.nbl
    gc = (CONVW + DI) // HB

    def body(dv_ref, hf_ref, hb_ref, g_ref, dy_ref, dg_ref):
        gt = g_ref[...]
        dvv = dv_ref[...]
        dy_ref[...] = dvv * _gelu(gt)
        dg_ref[...] = (dvv * (hf_ref[...] + hb_ref[...]) * _gelu_grad(gt)).astype(BF16)

    st = pl.BlockSpec((TB, HB), lambda b, j: (b * nbt + 1 + j, 0))
    la = pl.BlockSpec((TB, HB), lambda b, j: (b * nbl + j, 0))
    return pl.pallas_call(
        body, name="post_lru_bwd", grid=(cfg.Bn, nbl),
        in_specs=[la, st, st, pl.BlockSpec((TB, HB), lambda b, j: (b * nbt + 1 + j, gc))],
        out_specs=[la, la],
        out_shape=[jax.ShapeDtypeStruct((cfg.N, LW), F32), jax.ShapeDtypeStruct((cfg.N, LW), BF16)],
        compiler_params=_cp(2),
    )(dv, hf, hb, proj)


def _merge_fwd(cfg, proj, b_gate, br_ssd, br_lru):
    TB, nbt, nbl = cfg.TB, cfg.nbt, cfg.nbl
    mc = (CONVW + DI + LW) // HB

    def body(ms_ref, ml_ref, bg_ref, bs_ref, bl_ref, o_ref):
        gs = _sigmoid(ms_ref[...] + bg_ref[:, :D])
        gl = _sigmoid(ml_ref[...] + bg_ref[:, D:])
        o_ref[...] = (gs * bs_ref[...] + gl * bl_ref[...]).astype(BF16)

    la = pl.BlockSpec((TB, D), lambda b, j: (b * nbl + j, 0))
    return pl.pallas_call(
        body, name="merge_fwd", grid=(cfg.Bn, nbl),
        in_specs=[pl.BlockSpec((TB, HB), lambda b, j: (b * nbt + 1 + j, mc)),
                  pl.BlockSpec((TB, HB), lambda b, j: (b * nbt + 1 + j, mc + 1)),
                  pl.BlockSpec((1, 2 * D), lambda b, j: (0, 0)), la, la],
        out_specs=la,
        out_shape=jax.ShapeDtypeStruct((cfg.N, D), BF16),
        compiler_params=_cp(2),
    )(proj, proj, b_gate, br_ssd, br_lru)


def _merge_bwd(cfg, dmix, proj, b_gate, br_ssd, br_lru):
    TB, nbt, nbl = cfg.TB, cfg.nbt, cfg.nbl
    mc = (CONVW + DI + LW) // HB

    def body(dm_ref, ms_ref, ml_ref, bg_ref, bs_ref, bl_ref, ds_ref, dl_ref, dmg_ref, acc_ref):
        b = pl.program_id(0)
        j = pl.program_id(1)

        @pl.when(jnp.logical_and(b == 0, j == 0))
        def _():
            acc_ref[...] = jnp.zeros_like(acc_ref)

        dm = dm_ref[...]
        gs = _sigmoid(ms_ref[...] + bg_ref[:, :D])
        gl = _sigmoid(ml_ref[...] + bg_ref[:, D:])
        ds_ref[...] = (dm * gs).astype(BF16)
        dl_ref[...] = (dm * gl).astype(BF16)
        dps = dm * bs_ref[...] * gs * (1.0 - gs)
        dpl = dm * bl_ref[...] * gl * (1.0 - gl)
        dmg_ref[:, :D] = dps.astype(BF16)
        dmg_ref[:, D:] = dpl.astype(BF16)
        acc_ref[0:1, :D] += jnp.sum(dps, axis=0, keepdims=True)
        acc_ref[0:1, D:] += jnp.sum(dpl, axis=0, keepdims=True)

    la = pl.BlockSpec((TB, D), lambda b, j: (b * nbl + j, 0))
    return pl.pallas_call(
        body, name="merge_bwd", grid=(cfg.Bn, nbl),
        in_specs=[la, pl.BlockSpec((TB, HB), lambda b, j: (b * nbt + 1 + j, mc)),
                  pl.BlockSpec((TB, HB), lambda b, j: (b * nbt + 1 + j, mc + 1)),
                  pl.BlockSpec((1, 2 * D), lambda b, j: (0, 0)), la, la],
        out_specs=[la, la, pl.BlockSpec((TB, 2 * D), lambda b, j: (b * nbl + j, 0)),
                   pl.BlockSpec((8, 2 * D), lambda b, j: (0, 0))],
        out_shape=[jax.ShapeDtypeStruct((cfg.N, D), BF16), jax.ShapeDtypeStruct((cfg.N, D), BF16),
                   jax.ShapeDtypeStruct((cfg.N, 2 * D), BF16), jax.ShapeDtypeStruct((8, 2 * D), F32)],
        compiler_params=_cp(2),
    )(dmix, proj, proj, b_gate, br_ssd, br_lru)


def _resid1_fwd(cfg, xs, x_mix, gate1, shift2, scale2, ln1_g, ln1_b):
    TB, nbt, nbl = cfg.TB, cfg.nbt, cfg.nbl

    def body(x_ref, xm_ref, g1_ref, sh_ref, sc_ref, lg_ref, lb_ref, x1_ref, h2_ref):
        r1 = ALPHA * x_ref[...] + g1_ref[...] * xm_ref[...]
        xh, _ = _ln(r1)
        x1 = xh * lg_ref[...] + lb_ref[...]
        x1_ref[...] = x1
        xh2, _ = _ln(x1)
        h2_ref[...] = (xh2 * (1.0 + sc_ref[...]) + sh_ref[...]).astype(BF16)

    la = pl.BlockSpec((TB, D), lambda b, j: (b * nbl + j, 0))
    ex = pl.BlockSpec((None, 1, D), lambda b, j: (b, 0, 0))
    vec = pl.BlockSpec((1, D), lambda b, j: (0, 0))
    return pl.pallas_call(
        body, name="resid1_fwd", grid=(cfg.Bn, nbl),
        in_specs=[pl.BlockSpec((TB, D), lambda b, j: (b * nbt + 1 + j, 0)), la, ex, ex, ex, vec, vec],
        out_specs=[la, la],
        out_shape=[jax.ShapeDtypeStruct((cfg.N, D), F32), jax.ShapeDtypeStruct((cfg.N, D), BF16)],
        compiler_params=_cp(2),
    )(xs, x_mix, gate1, shift2, scale2, ln1_g, ln1_b)


def _resid1_bwd(cfg, dh2, dx1p, x1, xs, x_mix, gate1, scale2, ln1_g):
    TB, nbt, nbl = cfg.TB, cfg.nbt, cfg.nbl

    def body(dh2_ref, dx1p_ref, x1_ref, x_ref, xm_ref, g1_ref, sc_ref, lg_ref,
             dxm_ref, dxp_ref, ex_ref, gl_ref):
        b = pl.program_id(0)
        j = pl.program_id(1)

        @pl.when(j == 0)
        def _():
            ex_ref[...] = jnp.zeros_like(ex_ref)

        @pl.when(jnp.logical_and(b == 0, j == 0))
        def _():
            gl_ref[...] = jnp.zeros_like(gl_ref)

        dh2 = dh2_ref[...]
        xh2, rs2 = _ln(x1_ref[...])
        ex_ref[0:1, :] += jnp.sum(dh2, axis=0, keepdims=True)
        ex_ref[1:2, :] += jnp.sum(dh2 * xh2, axis=0, keepdims=True)
        dx1 = dx1p_ref[...] + _ln_bwd(dh2 * (1.0 + sc_ref[...]), xh2, rs2)
        xm = xm_ref[...]
        g1 = g1_ref[...]
        r1 = ALPHA * x_ref[...] + g1 * xm
        xh1, rs1 = _ln(r1)
        gl_ref[0:1, :] += jnp.sum(dx1 * xh1, axis=0, keepdims=True)
        gl_ref[1:2, :] += jnp.sum(dx1, axis=0, keepdims=True)
        dr1 = _ln_bwd(dx1 * lg_ref[...], xh1, rs1)
        ex_ref[2:3, :] += jnp.sum(dr1 * xm, axis=0, keepdims=True)
        dxm_ref[...] = (dr1 * g1).astype(BF16)
        dxp_ref[...] = ALPHA * dr1

    la = pl.BlockSpec((TB, D), lambda b, j: (b * nbl + j, 0))
    ex = pl.BlockSpec((None, 1, D), lambda b, j: (b, 0, 0))
    vec = pl.BlockSpec((1, D), lambda b, j: (0, 0))
    return pl.pallas_call(
        body, name="resid1_bwd", grid=(cfg.Bn, nbl),
        in_specs=[la, la, la, pl.BlockSpec((TB, D), lambda b, j: (b * nbt + 1 + j, 0)), la, ex, ex, vec],
        out_specs=[la, la, pl.BlockSpec((None, 8, D), lambda b, j: (b, 0, 0)),
                   pl.BlockSpec((8, D), lambda b, j: (0, 0))],
        out_shape=[jax.ShapeDtypeStruct((cfg.N, D), BF16), jax.ShapeDtypeStruct((cfg.N, D), F32),
                   jax.ShapeDtypeStruct((cfg.Bn, 8, D), F32), jax.ShapeDtypeStruct((8, D), F32)],
        compiler_params=_cp(2),
    )(dh2, dx1p, x1, xs, x_mix, gate1, scale2, ln1_g)


def _mlp_act_fwd(cfg, a1, b1):
    TB = cfg.TB

    def body(a_ref, b_ref, o_ref):
        v = jnp.maximum(a_ref[...] + b_ref[...], 0.0)
        o_ref[...] = (v * v).astype(BF16)

    blk = pl.BlockSpec((TB, HB), lambda i, j: (i, j))
    return pl.pallas_call(
        body, name="mlp_act_fwd", grid=(cfg.N // TB, MLP // HB),
        in_specs=[blk, pl.BlockSpec((1, HB), lambda i, j: (0, j))],
        out_specs=blk, out_shape=jax.ShapeDtypeStruct((cfg.N, MLP), BF16),
        compiler_params=_cp(2),
    )(a1, b1)


def _mlp_act_bwd(cfg, dact, a1, b1):
    TB = cfg.TB

    def body(d_ref, a_ref, b_ref, o_ref, acc_ref):
        i = pl.program_id(1)

        @pl.when(i == 0)
        def _():
            acc_ref[...] = jnp.zeros_like(acc_ref)

        da = d_ref[...] * 2.0 * jnp.maximum(a_ref[...] + b_ref[...], 0.0)
        o_ref[...] = da.astype(BF16)
        acc_ref[0:1, :] += jnp.sum(da, axis=0, keepdims=True)

    blk = pl.BlockSpec((TB, HB), lambda j, i: (i, j))
    return pl.pallas_call(
        body, name="mlp_act_bwd", grid=(MLP // HB, cfg.N // TB),
        in_specs=[blk, blk, pl.BlockSpec((1, HB), lambda j, i: (0, j))],
        out_specs=[blk, pl.BlockSpec((8, HB), lambda j, i: (0, j))],
        out_shape=[jax.ShapeDtypeStruct((cfg.N, MLP), BF16), jax.ShapeDtypeStruct((8, MLP), F32)],
        compiler_params=_cp(2),
    )(dact, a1, b1)


def _final_fwd_bwd(cfg, x1, mlp, b2, gate2, ln2_g, ln2_b, target):
    TB, nbl = cfg.TB, cfg.nbl

    def body(x1_ref, m_ref, b2_ref, g2_ref, lg_ref, lb_ref, t_ref, dm_ref, dx_ref, ex_ref, gl_ref):
        b = pl.program_id(0)
        j = pl.program_id(1)

        @pl.when(j == 0)
        def _():
            ex_ref[...] = jnp.zeros_like(ex_ref)

        @pl.when(jnp.logical_and(b == 0, j == 0))
        def _():
            gl_ref[...] = jnp.zeros_like(gl_ref)

        mv = m_ref[...] + b2_ref[...]
        g2 = g2_ref[...]
        r2 = ALPHA * x1_ref[...] + g2 * mv
        xh, rs = _ln(r2)
        lg = lg_ref[...]
        x2 = xh * lg + lb_ref[...]
        err = x2 - t_ref[...]
        ls = jnp.sum(jnp.sum(err * err, axis=1, keepdims=True), axis=0, keepdims=True) * (0.5 / D)
        gl_ref[3:4, :] += ls
        dx2 = err * (1.0 / D)
        gl_ref[0:1, :] += jnp.sum(dx2 * xh, axis=0, keepdims=True)
        gl_ref[1:2, :] += jnp.sum(dx2, axis=0, keepdims=True)
        dr2 = _ln_bwd(dx2 * lg, xh, rs)
        ex_ref[0:1, :] += jnp.sum(dr2 * mv, axis=0, keepdims=True)
        dmv = dr2 * g2
        gl_ref[2:3, :] += jnp.sum(dmv, axis=0, keepdims=True)
        dm_ref[...] = dmv.astype(BF16)
        dx_ref[...] = ALPHA * dr2

    la = pl.BlockSpec((TB, D), lambda b, j: (b * nbl + j, 0))
    ex = pl.BlockSpec((None, 1, D), lambda b, j: (b, 0, 0))
    vec = pl.BlockSpec((1, D), lambda b, j: (0, 0))
    return pl.pallas_call(
        body, name="final_fwd_bwd", grid=(cfg.Bn, nbl),
        in_specs=[la, la, vec, ex, vec, vec, la],
        out_specs=[la, la, pl.BlockSpec((None, 8, D), lambda b, j: (b, 0, 0)),
                   pl.BlockSpec((8, D), lambda b, j: (0, 0))],
        out_shape=[jax.ShapeDtypeStruct((cfg.N, D), BF16), jax.ShapeDtypeStruct((cfg.N, D), F32),
                   jax.ShapeDtypeStruct((cfg.Bn, 8, D), F32), jax.ShapeDtypeStruct((8, D), F32)],
        compiler_params=_cp(2),
    )(x1, mlp, b2, gate2, ln2_g, ln2_b, target)


def _ln_mod_bwd(cfg, dh_a, dh_b, xs, scale_tab, dxp):
    TB, nbt, nbl = cfg.TB, cfg.nbt, cfg.nbl

    def body(da_ref, db_ref, x_ref, sc_ref, dxp_ref, gx_ref, acc_ref):
        j = pl.program_id(1)

        @pl.when(j <= 1)
        def _():
            acc_ref[...] = jnp.zeros_like(acc_ref)

        dh = da_ref[...] + db_ref[...]
        xhat, rs = _ln(x_ref[...])
        acc_ref[0:1, :] += jnp.sum(dh, axis=0, keepdims=True)
        acc_ref[1:2, :] += jnp.sum(dh * xhat, axis=0, keepdims=True)
        gx_ref[...] = dxp_ref[...] + _ln_bwd(dh * (1.0 + sc_ref[...]), xhat, rs)

    st = pl.BlockSpec((TB, D), lambda b, j: (b * nbt + j, 0))
    la = pl.BlockSpec((TB, D), lambda b, j: (b * nbl + jnp.maximum(j - 1, 0), 0))
    return pl.pallas_call(
        body, name="ln_mod_bwd", grid=(cfg.Bn, nbt),
        in_specs=[st, st, st,
                  pl.BlockSpec((None, 1, D), lambda b, j: (2 * b + jnp.minimum(j, 1), 0, 0)), la],
        out_specs=[la, pl.BlockSpec((None, 8, D), lambda b, j: (2 * b + jnp.minimum(j, 1), 0, 0))],
        out_shape=[jax.ShapeDtypeStruct((cfg.N, D), F32), jax.ShapeDtypeStruct((2 * cfg.Bn, 8, D), F32)],
        compiler_params=_cp(2),
    )(dh_a, dh_b, xs, scale_tab, dxp)


def _perm_w_in(w_in):
    w_main = jnp.concatenate([w_in[:, 0:3072], w_in[:, 4160:5184], w_in[:, 3136:4160], w_in[:, 5184:10304]], axis=1)
    w_dt = jnp.pad(w_in[:, 3072:3136], ((0, 0), (0, DTW - 2 * NH)))
    return w_main, w_dt


def _unperm_w_in(dw_main, dw_dt):
    return jnp.concatenate([dw_main[:, 0:3072], dw_dt[:, :2 * NH], dw_main[:, 4096:5120],
                            dw_main[:, 3072:4096], dw_main[:, 5120:]], axis=1)


def _to_groups(v, cfg):
    return v[:, :2 * NH].reshape(cfg.NT, 2, NG, HPG).transpose(1, 2, 0, 3)


def _local_step(cfg, x, ctx, target, m, mc, W):
    Bn, T, Tc = cfg.Bn, cfg.T, cfg.Tc
    NT, N = cfg.NT, cfg.N
    xs = jnp.concatenate([ctx, x], axis=1).reshape(NT, D)
    mch = [m[:, i * D:(i + 1) * D] for i in range(NMOD)]
    ctx_sh = jnp.broadcast_to(mc[None, :D], (Bn, D))
    ctx_sc = jnp.broadcast_to(mc[None, D:], (Bn, D))
    shift_tab = jnp.stack([ctx_sh, mch[0]], axis=1).reshape(2 * Bn, 1, D)
    scale_tab = jnp.stack([ctx_sc, mch[1]], axis=1).reshape(2 * Bn, 1, D)
    gate1 = mch[2].reshape(Bn, 1, D)
    shift2 = mch[3].reshape(Bn, 1, D)
    scale2 = mch[4].reshape(Bn, 1, D)
    gate2 = mch[5].reshape(Bn, 1, D)

    conv_w = jnp.concatenate([W["ssd_conv_w"], W["lru_conv_w"]], axis=1)
    conv_b = jnp.concatenate([W["ssd_conv_b"], W["lru_conv_b"]], axis=1)
    dt_bias = jnp.pad(W["ssd_dt_bias"].reshape(1, 2 * NH), ((0, 0), (0, DTW - 2 * NH)))
    a_log = jnp.pad(W["ssd_a_log"].reshape(1, 2 * NH), ((0, 0), (0, DTW - 2 * NH)))
    dvec = jnp.repeat(W["ssd_d"].reshape(NH), HD).reshape(1, DI)
    lba = W["lru_ba"].reshape(2, 1, LW)
    lbi = W["lru_bi"].reshape(2, 1, LW)
    llam = W["lru_lambda"].reshape(2, 1, LW)

    h = _ln_mod_fwd(cfg, xs, shift_tab, scale_tab)
    proj = _mm(h, W["w_main"], "nn", "mm_proj", tm=512, tn=1024, tk=1024)
    dt_raw = _mm(h, W["w_dt"], "nn", "mm_dt", tm=512, tn=DTW, tk=1024)
    dt, cum = _dt_fwd(cfg, dt_raw, dt_bias, a_log)
    dtg = _to_groups(dt, cfg)
    cumg = _to_groups(cum, cfg)
    cumTg = cum[:, :2 * NH].reshape(NT // CH, CH, 2, NG, HPG).transpose(2, 3, 0, 4, 1)
    act = _conv_fwd(cfg, proj, conv_w, conv_b)
    ys, hss, hls = [], [], []
    for rev in (False, True):
        y_d, hs_d = _ssd_fwd(cfg, act, dtg, cumg, cumTg, rev)
        ys.append(y_d)
        hss.append(hs_d)
        hls.append(_lru_fwd(cfg, act, W["lru_wa"], W["lru_wi"], lba, lbi, llam, rev))
    nssd = _post_ssd_fwd(cfg, ys[0], ys[1], act, proj, dvec, W["ssd_norm_w"])
    vlru = _post_lru_fwd(cfg, hls[0], hls[1], proj)
    br_ssd = _mm(nssd, W["w_br_ssd"], "nn", "mm_br_ssd", tm=512, tn=1024, tk=1024)
    br_lru = _mm(vlru, W["w_br_lru"], "nn", "mm_br_lru", tm=512, tn=1024, tk=1024)
    mix = _merge_fwd(cfg, proj, W["b_gate"], br_ssd, br_lru)
    x_mix = _mm(mix, W["w_out"], "nn", "mm_out", tm=512, tn=1024, tk=1024)
    x1, h2 = _resid1_fwd(cfg, xs, x_mix, gate1, shift2, scale2, W["ln1_g"], W["ln1_b"])
    a1 = _mm(h2, W["w_mlp1"], "nn", "mm_mlp1", tm=512, tn=1024, tk=1024)
    actm = _mlp_act_fwd(cfg, a1, W["b_mlp1"])
    mlp = _mm(actm, W["w_mlp2"], "nn", "mm_mlp2", tm=512, tn=1024, tk=1024)
    dmlp, dx1p, ex2, gl2 = _final_fwd_bwd(cfg, x1, mlp, W["b_mlp2"], gate2, W["ln2_g"], W["ln2_b"],
                                          target.reshape(N, D))

    g = {}
    g["ln2_g"], g["ln2_b"], g["b_mlp2"] = gl2[0:1], gl2[1:2], gl2[2:3]
    loss_partial = gl2[3, 0]
    dact = _mm(dmlp, W["w_mlp2"], "nt", "mm_dact", tm=512, tn=1024, tk=1024)
    g["w_mlp2"] = _mm(actm, dmlp, "tn", "mm_dw_mlp2", tm=1024, tn=1024, tk=512)
    da1, accb1 = _mlp_act_bwd(cfg, dact, a1, W["b_mlp1"])
    g["b_mlp1"] = accb1[0:1]
    dh2 = _mm(da1, W["w_mlp1"], "nt", "mm_dh2", tm=512, tn=1024, tk=1024)
    g["w_mlp1"] = _mm(h2, da1, "tn", "mm_dw_mlp1", tm=1024, tn=1024, tk=512)
    dx_mix, dxp, ex1, gl1 = _resid1_bwd(cfg, dh2, dx1p, x1, xs, x_mix, gate1, scale2, W["ln1_g"])
    g["ln1_g"], g["ln1_b"] = gl1[0:1], gl1[1:2]
    dmix = _mm(dx_mix, W["w_out"], "nt", "mm_dmix", tm=512, tn=1024, tk=1024)
    g["w_out"] = _mm(mix, dx_mix, "tn", "mm_dw_out", tm=1024, tn=1024, tk=512)
    dbs, dbl, dmerge, accg = _merge_bwd(cfg, dmix, proj, W["b_gate"], br_ssd, br_lru)
    g["b_gate"] = accg[0:1]
    dnssd = _mm(dbs, W["w_br_ssd"], "nt", "mm_dnssd", tm=512, tn=1024, tk=1024)
    g["w_br_ssd"] = _mm(nssd, dbs, "tn", "mm_dw_br_ssd", tm=1024, tn=1024, tk=512)
    dvlru = _mm(dbl, W["w_br_lru"], "nt", "mm_dvlru", tm=512, tn=1024, tk=1024)
    g["w_br_lru"] = _mm(vlru, dbl, "tn", "mm_dw_br_lru", tm=1024, tn=1024, tk=512)
    dy, dz, accs = _post_ssd_bwd(cfg, dnssd, ys[0], ys[1], act, proj, dvec, W["ssd_norm_w"])
    g["ssd_norm_w"] = accs[0:1]
    dD_cols = accs[1:2]
    dyl, dlg = _post_lru_bwd(cfg, dvlru, hls[0], hls[1], proj)

    dxh, dBs, dCs, dAs, dxxs, dus = [], [], [], [], [], []
    g["lru_wa"], g["lru_wi"] = [], []
    lvecs = []
    for i, rev in enumerate((False, True)):
        o = _ssd_bwd(cfg, act, dtg, cumg, cumTg, hss[i], dy, rev)
        dxh.append(o[0]); dBs.append(o[1]); dCs.append(o[2]); dAs.append(o[3]); dxxs.append(o[4])
        du, dwa, dwi, lv = _lru_bwd(cfg, act, W["lru_wa"], W["lru_wi"], lba, lbi, llam, hls[i], dyl, rev)
        dus.append(du); g["lru_wa"].append(dwa); g["lru_wi"].append(dwi); lvecs.append(lv)
    g["lru_wa"] = jnp.stack(g["lru_wa"])
    g["lru_wi"] = jnp.stack(g["lru_wi"])
    g["lru_ba"] = jnp.stack([lvecs[0][0], lvecs[1][0]])
    g["lru_bi"] = jnp.stack([lvecs[0][1], lvecs[1][1]])
    g["lru_lambda"] = jnp.stack([lvecs[0][2], lvecs[1][2]])

    def from_groups(a0, a1):
        v = jnp.stack([a0, a1]).transpose(2, 0, 1, 3).reshape(NT, 2 * NH)
        return jnp.pad(v, ((0, 0), (0, DTW - 2 * NH)))
    ddt_raw, accdt = _dt_bwd(cfg, from_groups(dAs[0], dAs[1]), from_groups(dxxs[0], dxxs[1]),
                             dt_raw, dt, dt_bias, a_log)
    g["ssd_a_log"] = accdt[0, :2 * NH].reshape(2, NH)
    g["ssd_dt_bias"] = accdt[1, :2 * NH].reshape(2, NH)

    dpx, accx = _conv_bwd(cfg, "conv_bwd_x", proj, conv_w, conv_b, [dxh[0], dxh[1]], 0, DI, True, skip=(dy, dvec))
    dpB, accB = _conv_bwd(cfg, "conv_bwd_b", proj, conv_w, conv_b, [dBs[0], dBs[1]], DI, NG * NS, True)
    dpC, accC = _conv_bwd(cfg, "conv_bwd_c", proj, conv_w, conv_b, [dCs[0], dCs[1]], DI + NG * NS, NG * NS, True)
    dpl, accl = _conv_bwd(cfg, "conv_bwd_lru", proj, conv_w, conv_b, [dus[0], dus[1]], DI + 2 * NG * NS, LW, False)
    accssd = jnp.concatenate([accx, accB, accC], axis=1)
    g["ssd_conv_w"], g["ssd_conv_b"] = accssd[0:4], accssd[4:5]
    g["lru_conv_w"], g["lru_conv_b"] = accl[0:4], accl[4:5]

    def to_stream(v):
        w = v.shape[1]
        return jnp.pad(v.reshape(Bn, T, w), ((0, 0), (Tc, 0), (0, 0))).reshape(NT, w)
    dproj = jnp.concatenate([dpx, dpB, dpC, dpl, to_stream(dz), to_stream(dlg), to_stream(dmerge)], axis=1)
    dh_a = _mm(dproj, W["w_main"], "nt", "mm_dh_main", tm=512, tn=1024, tk=1024)
    dh_b = _mm(ddt_raw, W["w_dt"], "nt", "mm_dh_dt", tm=512, tn=1024, tk=DTW)
    dw_main = _mm(h, dproj, "tn", "mm_dw_main", tm=1024, tn=1024, tk=512)
    dw_dt = _mm(h, ddt_raw, "tn", "mm_dw_dt", tm=1024, tn=DTW, tk=512)
    g["w_in"] = _unperm_w_in(dw_main, dw_dt)
    grad_x, acct = _ln_mod_bwd(cfg, dh_a, dh_b, xs, scale_tab, dxp)
    acct = acct.reshape(Bn, 2, 8, D)
    dm = jnp.concatenate([acct[:, 1, 0], acct[:, 1, 1], ex1[:, 2], ex1[:, 0], ex1[:, 1], ex2[:, 0]], axis=1)
    dmc = jnp.concatenate([acct[:, 0, 0], acct[:, 0, 1]], axis=1)
    g["ssd_d_cols"] = dD_cols
    return loss_partial, grad_x.reshape(Bn, T, D), g, dm, dmc


MESH = pl.DeviceIdType.MESH
_HBM = pl.BlockSpec(memory_space=pltpu.HBM)


def _me():
    return 4 * lax.axis_index("x") + 2 * lax.axis_index("y") + lax.axis_index("c")


def _peer(k):
    px = (lax.axis_index("x") + ((k >> 2) & 1)) % 2
    py = (lax.axis_index("y") + ((k >> 1) & 1)) % 2
    pc = (lax.axis_index("c") + (k & 1)) % 2
    return (px, py, pc), 4 * px + 2 * py + pc


def _exchange(x, name, gather):
    blk_shape = x.shape if gather else x.shape[1:]

    def body(x_ref, o_ref, send_sems, recv_sems, loc_sem):
        me = _me()
        src_me = x_ref if gather else x_ref.at[me]
        loc = pltpu.make_async_copy(src_me, o_ref.at[me], loc_sem)
        loc.start()
        sends = []
        for k in range(1, NDEV):
            peer, pid = _peer(k)
            cp = pltpu.make_async_remote_copy(
                src_ref=x_ref if gather else x_ref.at[pid], dst_ref=o_ref.at[me],
                send_sem=send_sems.at[k - 1], recv_sem=recv_sems.at[k - 1],
                device_id=peer, device_id_type=MESH)
            cp.start()
            sends.append(cp)
        for k in range(1, NDEV):
            peer, pid = _peer(k)
            pltpu.make_async_remote_copy(
                src_ref=src_me, dst_ref=o_ref.at[pid],
                send_sem=send_sems.at[k - 1], recv_sem=recv_sems.at[k - 1],
                device_id=peer, device_id_type=MESH).wait_recv()
        for cp in sends:
            cp.wait_send()
        loc.wait()

    return pl.pallas_call(
        body, name=name,
        out_shape=jax.ShapeDtypeStruct((NDEV,) + tuple(blk_shape), x.dtype),
        in_specs=[_HBM], out_specs=_HBM,
        scratch_shapes=[pltpu.SemaphoreType.DMA((NDEV - 1,)), pltpu.SemaphoreType.DMA((NDEV - 1,)),
                        pltpu.SemaphoreType.DMA],
    )(x)


def _row_tile(R, cap):
    best = 8
    t = 8
    while t <= min(R, cap):
        if R % t == 0:
            best = t
        t += 8
    assert R % best == 0, R
    return best


def _sum_slots(x, name):
    _, R, C = x.shape
    tr = _row_tile(R, 256)

    def body(x_ref, o_ref):
        acc = x_ref[0]
        for i in range(1, NDEV):
            acc = acc + x_ref[i]
        o_ref[...] = acc

    return pl.pallas_call(
        body, name=name, grid=(R // tr,),
        in_specs=[pl.BlockSpec((NDEV, tr, C), lambda i: (0, i, 0))],
        out_specs=pl.BlockSpec((tr, C), lambda i: (i, 0)),
        out_shape=jax.ShapeDtypeStruct((R, C), x.dtype),
        compiler_params=_cp(1),
    )(x)


def _adamw(w, g, m, v, name):
    R, C = w.shape
    tr = _row_tile(R, 256)
    c1 = 1.0 / (1.0 - ADAM_B1 ** ADAM_STEP)
    c2 = 1.0 / (1.0 - ADAM_B2 ** ADAM_STEP)

    def body(w_ref, g_ref, m_ref, v_ref, d_ref, nm_ref, nv_ref):
        gv = g_ref[...]
        nm = ADAM_B1 * m_ref[...] + (1.0 - ADAM_B1) * gv
        nv = ADAM_B2 * v_ref[...] + (1.0 - ADAM_B2) * (gv * gv)
        d_ref[...] = -ADAM_LR * ((nm * c1) / (jnp.sqrt(nv * c2) + ADAM_EPS) + ADAM_WD * w_ref[...])
        nm_ref[...] = nm
        nv_ref[...] = nv

    blk = pl.BlockSpec((tr, C), lambda i: (i, 0))
    return pl.pallas_call(
        body, name=name, grid=(R // tr,),
        in_specs=[blk] * 4, out_specs=[blk] * 3,
        out_shape=[jax.ShapeDtypeStruct((R, C), F32)] * 3,
        compiler_params=_cp(1),
    )(w, g, m, v)


def _mod_fwd(c_rows, w_shard, b_shard):
    def body(c_ref, w_ref, b_ref, o_ref):
        s = _silu(c_ref[...]).astype(BF16)
        o_ref[...] = _dot(s, w_ref[...].astype(BF16)) + b_ref[...]

    return pl.pallas_call(
        body, name="mod_fwd",
        out_shape=jax.ShapeDtypeStruct((c_rows.shape[0], w_shard.shape[1]), F32),
        compiler_params=pltpu.CompilerParams(vmem_limit_bytes=VMEM_LIMIT_BYTES),
    )(c_rows, w_shard, b_shard)


def _mod_bwd(c_rows, dm_all, dm_shard, w_shard):
    nrow = c_rows.shape[0]

    def body(c_ref, da_ref, ds_ref, w_ref, gw_ref, gb_ref, cc_ref):
        s = _silu(c_ref[...]).astype(BF16)
        ds = ds_ref[...]
        gw_ref[...] = _dot_tn(s, ds.astype(BF16))
        gb_ref[...] = jnp.sum(da_ref[...], axis=0, keepdims=True)
        rowi = lax.broadcasted_iota(jnp.int32, ds.shape, 0)
        dmc = jnp.sum(jnp.where(rowi % 8 >= 4, ds, 0.0), axis=0, keepdims=True)
        dmc8 = jnp.broadcast_to(dmc, (8, ds.shape[1])).astype(BF16)
        cc_ref[...] = _dot_nt(dmc8, w_ref[...].astype(BF16))

    return pl.pallas_call(
        body, name="mod_bwd",
        out_shape=[jax.ShapeDtypeStruct(w_shard.shape, F32),
                   jax.ShapeDtypeStruct((1, dm_all.shape[1]), F32),
                   jax.ShapeDtypeStruct((8, D), F32)],
        compiler_params=pltpu.CompilerParams(vmem_limit_bytes=VMEM_LIMIT_BYTES),
    )(c_rows, dm_all, dm_shard, w_shard)


def _small_finish(cc_pre, c_ctx, dd_cols):
    def body(cc_ref, c_ref, dd_ref, gc_ref, gd_ref):
        gc_ref[...] = cc_ref[...] * _silu_grad(c_ref[...])
        gd_ref[...] = jnp.sum(dd_ref[...], axis=1, keepdims=True)

    return pl.pallas_call(
        body, name="small_finish",
        out_shape=[jax.ShapeDtypeStruct((1, D), F32), jax.ShapeDtypeStruct((NH, 1), F32)],
    )(cc_pre, c_ctx, dd_cols)


_BIG = (("w_in", (D, 10304), 1), ("w_br_ssd", (DI, D), 0), ("w_br_lru", (LW, D), 0), ("w_out", (D, D), 0),
        ("w_mlp1", (D, MLP), 1), ("w_mlp2", (MLP, D), 0))
_SMALL_SH = (("ssd_conv_w", (4, 4096)), ("lru_conv_w", (4, LW)), ("lru_ba", (2, LW)), ("lru_bi", (2, LW)),
             ("lru_lambda", (2, LW)))
_REPL = (("c_ctx", (D,)), ("b_gate", (2 * D,)), ("ssd_conv_b", (4096,)), ("ssd_dt_bias", (2, NH)),
         ("ssd_a_log", (2, NH)), ("ssd_d", (DI,)), ("ssd_norm_w", (DI,)), ("lru_conv_b", (LW,)),
         ("lru_wa", (2, LB, LBW, LBW)), ("lru_wi", (2, LB, LBW, LBW)), ("ln1_g", (D,)), ("ln1_b", (D,)),
         ("b_mlp1", (MLP,)), ("b_mlp2", (D,)), ("ln2_g", (D,)), ("ln2_b", (D,)))

_WEIGHT_NAMES = ('c_ctx', 'w_mod', 'b_mod', 'w_in', 'b_gate', 'ssd_conv_w', 'ssd_conv_b', 'ssd_dt_bias', 'ssd_a_log',
                 'ssd_d', 'ssd_norm_w', 'lru_conv_w', 'lru_conv_b', 'lru_wa', 'lru_ba', 'lru_wi', 'lru_bi',
                 'lru_lambda', 'w_br_ssd', 'w_br_lru', 'w_out', 'ln1_g', 'ln1_b', 'w_mlp1', 'b_mlp1', 'w_mlp2',
                 'b_mlp2', 'ln2_g', 'ln2_b')
_ARG_NAMES = ('x', 'c', 'ctx') + _WEIGHT_NAMES + ('loss_target',) + tuple('m_' + n for n in _WEIGHT_NAMES) + tuple(
    'v_' + n for n in _WEIGHT_NAMES)


def _to_slots(full, axis):
    n = full.shape[axis] // NDEV
    if axis == 0:
        return full.reshape(NDEV, n, full.shape[1])
    return full.reshape(full.shape[0], NDEV, n).transpose(1, 0, 2)


def _from_slots(slots, axis):
    if axis == 0:
        return slots.reshape(NDEV * slots.shape[1], slots.shape[2])
    return slots.transpose(1, 0, 2).reshape(slots.shape[1], NDEV * slots.shape[2])


def _pack_rows(arrs, width=1024, mult=8):
    flat = jnp.concatenate([a.reshape(-1) for a in arrs])
    n = flat.shape[0]
    per = width * mult
    tot = -(-n // per) * per
    return jnp.pad(flat, (0, tot - n)).reshape(tot // width, width)


def _unpack_rows(packed, shapes, lead=()):
    nl = len(lead)
    flat = packed.reshape(tuple(lead) + (-1,))
    out, off = [], 0
    for s in shapes:
        n = math.prod(s)
        out.append(flat[..., off:off + n].reshape(tuple(lead) + tuple(s)))
        off += n
    return out


def kernel(x, c, ctx, c_ctx, w_mod, b_mod, w_in, b_gate, ssd_conv_w, ssd_conv_b, ssd_dt_bias, ssd_a_log, ssd_d, ssd_norm_w, lru_conv_w, lru_conv_b, lru_wa, lru_ba, lru_wi, lru_bi, lru_lambda, w_br_ssd, w_br_lru, w_out, ln1_g, ln1_b, w_mlp1, b_mlp1, w_mlp2, b_mlp2, ln2_g, ln2_b, loss_target, m_c_ctx, m_w_mod, m_b_mod, m_w_in, m_b_gate, m_ssd_conv_w, m_ssd_conv_b, m_ssd_dt_bias, m_ssd_a_log, m_ssd_d, m_ssd_norm_w, m_lru_conv_w, m_lru_conv_b, m_lru_wa, m_lru_ba, m_lru_wi, m_lru_bi, m_lru_lambda, m_w_br_ssd, m_w_br_lru, m_w_out, m_ln1_g, m_ln1_b, m_w_mlp1, m_b_mlp1, m_w_mlp2, m_b_mlp2, m_ln2_g, m_ln2_b, v_c_ctx, v_w_mod, v_b_mod, v_w_in, v_b_gate, v_ssd_conv_w, v_ssd_conv_b, v_ssd_dt_bias, v_ssd_a_log, v_ssd_d, v_ssd_norm_w, v_lru_conv_w, v_lru_conv_b, v_lru_wa, v_lru_ba, v_lru_wi, v_lru_bi, v_lru_lambda, v_w_br_ssd, v_w_br_lru, v_w_out, v_ln1_g, v_ln1_b, v_w_mlp1, v_b_mlp1, v_w_mlp2, v_b_mlp2, v_ln2_g, v_ln2_b):
    A = dict(zip(_ARG_NAMES, (x, c, ctx, c_ctx, w_mod, b_mod, w_in, b_gate, ssd_conv_w, ssd_conv_b, ssd_dt_bias, ssd_a_log, ssd_d, ssd_norm_w, lru_conv_w, lru_conv_b, lru_wa, lru_ba, lru_wi, lru_bi, lru_lambda, w_br_ssd, w_br_lru, w_out, ln1_g, ln1_b, w_mlp1, b_mlp1, w_mlp2, b_mlp2, ln2_g, ln2_b, loss_target, m_c_ctx, m_w_mod, m_b_mod, m_w_in, m_b_gate, m_ssd_conv_w, m_ssd_conv_b, m_ssd_dt_bias, m_ssd_a_log, m_ssd_d, m_ssd_norm_w, m_lru_conv_w, m_lru_conv_b, m_lru_wa, m_lru_ba, m_lru_wi, m_lru_bi, m_lru_lambda, m_w_br_ssd, m_w_br_lru, m_w_out, m_ln1_g, m_ln1_b, m_w_mlp1, m_b_mlp1, m_w_mlp2, m_b_mlp2, m_ln2_g, m_ln2_b, v_c_ctx, v_w_mod, v_b_mod, v_w_in, v_b_gate, v_ssd_conv_w, v_ssd_conv_b, v_ssd_dt_bias, v_ssd_a_log, v_ssd_d, v_ssd_norm_w, v_lru_conv_w, v_lru_conv_b, v_lru_wa, v_lru_ba, v_lru_wi, v_lru_bi, v_lru_lambda, v_w_br_ssd, v_w_br_lru, v_w_out, v_ln1_g, v_ln1_b, v_w_mlp1, v_b_mlp1, v_w_mlp2, v_b_mlp2, v_ln2_g, v_ln2_b)))
    Bn, T, _ = x.shape
    Tc = ctx.shape[1]
    cfg = _Cfg(Bn, T, Tc)
    me = _me()
    L = {n: (A[n] if n == "c_ctx" else A[n][0]) for n in _WEIGHT_NAMES}
    nmod = L["w_mod"].shape[1]

    c_all = _exchange(c, "ag_c", True)
    c_rows = jnp.concatenate([c_all.reshape(NDEV * Bn, D), jnp.broadcast_to(c_ctx[None, :], (8, D))], axis=0)
    b_shard = lax.dynamic_slice(L["b_mod"], (me * nmod,), (nmod,)).reshape(1, nmod)
    m_part = _mod_fwd(c_rows, L["w_mod"], b_shard)
    m_all = _exchange(m_part, "ag_mod", True)
    m_full = m_all.transpose(1, 0, 2).reshape(NDEV * Bn + 8, NMOD * D)
    m_mine = lax.dynamic_slice(m_full, (me * Bn, 0), (Bn, NMOD * D))
    mc = m_full[NDEV * Bn, :2 * D]

    big_shards = [L[n].astype(BF16).reshape(-1, 1024) for n, _, _ in _BIG]
    big_rows = [b.shape[0] for b in big_shards]
    big_all = _exchange(jnp.concatenate(big_shards, axis=0), "ag_w_big", True)
    small_shapes = [(s[0], s[1] // NDEV) for _, s in _SMALL_SH]
    small_all = _exchange(_pack_rows([L[n] for n, _ in _SMALL_SH], width=512), "ag_w_small", True)
    W = {}
    off = 0
    for (n, shp, axis), r in zip(_BIG, big_rows):
        piece = big_all[:, off:off + r, :]
        off += r
        shard_shape = (shp[0] // NDEV, shp[1]) if axis == 0 else (shp[0], shp[1] // NDEV)
        W[n] = _from_slots(piece.reshape((NDEV,) + shard_shape), axis)
    for (n, shp), piece in zip(_SMALL_SH, _unpack_rows(small_all, small_shapes, lead=(NDEV,))):
        W[n] = piece.transpose(1, 0, 2).reshape(shp)
    W["w_main"], W["w_dt"] = _perm_w_in(W.pop("w_in"))
    for n in ("ssd_conv_b", "lru_conv_b", "ssd_norm_w", "b_gate", "ln1_g", "ln1_b", "b_mlp1", "b_mlp2", "ln2_g", "ln2_b"):
        W[n] = L[n].reshape(1, -1)
    for n in ("ssd_dt_bias", "ssd_a_log", "ssd_d", "lru_wa", "lru_wi"):
        W[n] = L[n]

    loss_part, grad_x, g, dm, dmc = _local_step(cfg, x, ctx, loss_target, m_mine, mc, W)
    loss = lax.psum(loss_part, ("x", "y", "c"))

    dmc_pad = jnp.pad(dmc, ((0, 4 - Bn), (0, (NMOD - 2) * D)))
    dm_all = _exchange(jnp.concatenate([jnp.pad(dm, ((0, 4 - Bn), (0, 0))), dmc_pad], axis=0), "ag_dm", True)
    dm_all = dm_all.reshape(NDEV * 8, NMOD * D)
    c_rows_b = jnp.concatenate([jnp.pad(c_all, ((0, 0), (0, 4 - Bn), (0, 0))),
                                jnp.broadcast_to(c_ctx[None, None, :], (NDEV, 4, D))], axis=1).reshape(NDEV * 8, D)
    dm_shard = lax.dynamic_slice(dm_all, (0, me * nmod), (NDEV * 8, nmod))
    g_w_mod, g_b_mod, cc_part = _mod_bwd(c_rows_b, dm_all, dm_shard, L["w_mod"])
    g["c_ctx"] = cc_part[0]

    g["ssd_d"] = g.pop("ssd_d_cols")
    small_names = [n for n, _ in _REPL] + [n for n, _ in _SMALL_SH]
    small_full_shapes = [s for _, s in _REPL] + [s for _, s in _SMALL_SH]
    sm_all = _exchange(_pack_rows([g[n] for n in small_names]), "ag_g_small", True)
    sm_sum = _sum_slots(sm_all, "sum_g_small")
    gs = dict(zip(small_names, _unpack_rows(sm_sum, small_full_shapes)))
    gcc, gdd = _small_finish(gs["c_ctx"].reshape(1, D), c_ctx.reshape(1, D), gs["ssd_d"].reshape(NH, HD))
    gs["c_ctx"] = gcc.reshape(D)
    gs["ssd_d"] = gdd.reshape(NH)
    for n, shp in _SMALL_SH:
        ns = shp[1] // NDEV
        gs[n] = lax.dynamic_slice(gs[n], (0, me * ns), (shp[0], ns))
    gs["b_mod"] = g_b_mod.reshape(NMOD * D)

    gb = {}
    grp_a = _to_slots(g["w_in"], 1).reshape(NDEV, -1, 1024)
    red_a = _sum_slots(_exchange(grp_a, "rs_w_in", False), "sum_w_in")
    gb["w_in"] = red_a.reshape(D, 10304 // NDEV)
    rest = [(n, shp, axis) for n, shp, axis in _BIG if n != "w_in"]
    grp_b = jnp.concatenate([_to_slots(g[n], axis).reshape(NDEV, -1, 1024) for n, _, axis in rest], axis=1)
    red_b = _sum_slots(_exchange(grp_b, "rs_w_rest", False), "sum_w_rest")
    off = 0
    for n, shp, axis in rest:
        shard_shape = (shp[0] // NDEV, shp[1]) if axis == 0 else (shp[0], shp[1] // NDEV)
        r = math.prod(shard_shape) // 1024
        gb[n] = red_b[off:off + r].reshape(shard_shape)
        off += r
    gb["w_mod"] = g_w_mod

    grads, deltas, new_m, new_v = {}, {}, {}, {}
    big_names = ["w_mod"] + [n for n, _, _ in _BIG]
    for n in big_names:
        shp = L[n].shape
        pk = lambda a: a.reshape(-1, 1024)
        d_, nm_, nv_ = _adamw(pk(L[n]), pk(gb[n]), pk(A["m_" + n][0]), pk(A["v_" + n][0]), "adamw_" + n)
        grads[n], deltas[n], new_m[n], new_v[n] = gb[n], d_.reshape(shp), nm_.reshape(shp), nv_.reshape(shp)
    sm_names = [n for n in _WEIGHT_NAMES if n not in big_names]
    sm_shapes = [L[n].shape for n in sm_names]
    loc = lambda pre: _pack_rows([(A[pre + n] if n == "c_ctx" else A[pre + n][0]) for n in sm_names])
    d_, nm_, nv_ = _adamw(loc(""), _pack_rows([gs[n].reshape(L[n].shape) for n in sm_names]), loc("m_"), loc("v_"),
                          "adamw_small")
    for n, dv, mv, vv in zip(sm_names, _unpack_rows(d_, sm_shapes), _unpack_rows(nm_, sm_shapes),
                             _unpack_rows(nv_, sm_shapes)):
        grads[n], deltas[n], new_m[n], new_v[n] = gs[n].reshape(L[n].shape), dv, mv, vv

    def out(dct):
        return [dct[n] if n == "c_ctx" else dct[n][None] for n in _WEIGHT_NAMES]
    return (loss, grad_x, *out(grads), *out(deltas), *out(new_m), *out(new_v))
```

```python
import functools
import math

import jax
import jax.numpy as jnp
from jax import lax
from jax.experimental import pallas as pl
from jax.experimental.pallas import tpu as pltpu

F32 = jnp.float32
BF16 = jnp.bfloat16

D = 1024
GRID_W = 64
DI = 2048
NH = 32
HD = 64
NG = 8
HPG = 4
NS = 128
CH = 128
LW = 1024
LB = 8
LBW = 128
LRU_C = 8.0
MLP = 4096
NMOD = 6
ALPHA = 2.0 ** 0.25
LN_EPS = 1e-6
RMS_EPS = 1e-5
PM = 10240
DTW = 128
CONVW = 5120
NDEV = 8
W_IN_SLOT_ROWS = 1296

ADAM_LR = 0.001
ADAM_B1 = 0.9
ADAM_B2 = 0.999
ADAM_EPS = 1e-08
ADAM_WD = 0.01
ADAM_STEP = 10

VMEM_LIMIT_BYTES = 56 * 1024 * 1024


def _cp(n_axes):
    return pltpu.CompilerParams(dimension_semantics=("arbitrary",) * n_axes,
                                vmem_limit_bytes=VMEM_LIMIT_BYTES)


def _sigmoid(x):
    return 1.0 / (1.0 + jnp.exp(-x))


def _silu(x):
    return x * _sigmoid(x)


def _silu_grad(x):
    s = _sigmoid(x)
    return s * (1.0 + x * (1.0 - s))


def _log1p_pos(e):
    return jnp.where(e < 1e-2, e * (1.0 - e * (0.5 - e * (1.0 / 3.0))), jnp.log(1.0 + e))


def _softplus(x):
    return jnp.maximum(x, 0.0) + _log1p_pos(jnp.exp(-jnp.abs(x)))


def _neg_expm1(x):
    series = -x * (1.0 + x * (0.5 + x * (1.0 / 6.0 + x * (1.0 / 24.0))))
    return jnp.where(x > -1e-2, series, 1.0 - jnp.exp(x))


_GELU_K = math.sqrt(2.0 / math.pi)


def _gelu(x):
    t = jnp.tanh(_GELU_K * (x + 0.044715 * x * x * x))
    return 0.5 * x * (1.0 + t)


def _gelu_grad(x):
    t = jnp.tanh(_GELU_K * (x + 0.044715 * x * x * x))
    dt = (1.0 - t * t) * _GELU_K * (1.0 + 3.0 * 0.044715 * x * x)
    return 0.5 * (1.0 + t) + 0.5 * x * dt


def _ln(x):
    mu = jnp.mean(x, axis=-1, keepdims=True)
    xc = x - mu
    var = jnp.mean(xc * xc, axis=-1, keepdims=True)
    rs = lax.rsqrt(var + LN_EPS)
    return xc * rs, rs


def _ln_bwd(dy, xhat, rs):
    m1 = jnp.mean(dy, axis=-1, keepdims=True)
    m2 = jnp.mean(dy * xhat, axis=-1, keepdims=True)
    return rs * (dy - m1 - xhat * m2)


def _dot(a, b):
    return lax.dot_general(a, b, (((1,), (0,)), ((), ())), preferred_element_type=F32)


def _dot_nt(a, b):
    return lax.dot_general(a, b, (((1,), (1,)), ((), ())), preferred_element_type=F32)


def _dot_tn(a, b):
    return lax.dot_general(a, b, (((0,), (0,)), ((), ())), preferred_element_type=F32)


def _split3(a):
    a0 = a.astype(BF16)
    r = a - a0.astype(F32)
    a1 = r.astype(BF16)
    a2 = (r - a1.astype(F32)).astype(BF16)
    return a0, a1, a2


def _dot_exact_l(m_bf, a):
    a0, a1, a2 = _split3(a)
    return _dot(m_bf, a0) + _dot(m_bf, a1) + _dot(m_bf, a2)


def _dot_exact_r(a, m_bf):
    a0, a1, a2 = _split3(a)
    return _dot(a0, m_bf) + _dot(a1, m_bf) + _dot(a2, m_bf)


def _tri(n, upper):
    ii = lax.broadcasted_iota(jnp.int32, (n, n), 0)
    kk = lax.broadcasted_iota(jnp.int32, (n, n), 1)
    m = (kk >= ii) if upper else (kk <= ii)
    return jnp.where(m, 1.0, 0.0).astype(BF16)


def _fit(n, t):
    t = min(t, n)
    while n % t:
        t //= 2
    return t


def _mm(a, b, mode, name, out_dtype=F32, tm=512, tn=512, tk=512, xch=None):
    if mode == "nn":
        M, K = a.shape
        N = b.shape[1]
    elif mode == "nt":
        M, K = a.shape
        N = b.shape[0]
    else:
        K, M = a.shape
        N = b.shape[1]
    tm, tn, tk = _fit(M, tm), _fit(N, tn), _fit(K, tk)
    assert M % tm == 0 and N % tn == 0 and K % tk == 0, (name, M, N, K, tm, tn, tk)
    nk = K // tk
    if mode == "tn":
        a_spec = pl.BlockSpec((tk, tm), lambda i, j, k: (k, i))
    else:
        a_spec = pl.BlockSpec((tm, tk), lambda i, j, k: (i, k))
    if mode == "nt":
        b_spec = pl.BlockSpec((tn, tk), lambda i, j, k: (j, k))
    else:
        b_spec = pl.BlockSpec((tk, tn), lambda i, j, k: (k, j))
    dn = {"nn": (((1,), (0,)), ((), ())), "nt": (((1,), (1,)), ((), ())), "tn": (((0,), (0,)), ((), ()))}[mode]

    def body(a_ref, b_ref, o_ref, acc_ref):
        k = pl.program_id(2)

        @pl.when(k == 0)
        def _():
            acc_ref[...] = jnp.zeros_like(acc_ref)

        acc_ref[...] += lax.dot_general(a_ref[...].astype(BF16), b_ref[...].astype(BF16), dn,
                                        preferred_element_type=F32)

        @pl.when(k == nk - 1)
        def _():
            o_ref[...] = acc_ref[...].astype(o_ref.dtype)

    res = _hosted_call(
        body, xch, name=name, grid=(M // tm, N // tn, nk),
        in_specs=[a_spec, b_spec],
        out_specs=[pl.BlockSpec((tm, tn), lambda i, j, k: (i, j))],
        out_shape=[jax.ShapeDtypeStruct((M, N), out_dtype)],
        scratch_shapes=[pltpu.VMEM((tm, tn), F32)],
        compiler_params=_cp(3), args=(a, b))
    if xch is None:
        return res[0]
    return res[0][0], res[1]


class _Cfg:
    def __init__(self, Bn, T, Tc):
        assert T % Tc == 0 and Tc % CH == 0 and Tc % GRID_W == 0
        self.Bn, self.T, self.Tc = Bn, T, Tc
        self.TT = T + Tc
        self.TB = Tc
        self.nbt = self.TT // self.TB
        self.nbl = T // self.TB
        self.NT = Bn * self.TT
        self.N = Bn * T
        self.nct = Tc // CH
        self.nlt = T // CH
        self.nch = self.nct + self.nlt


def _ln_mod_fwd(cfg, xs, shift_tab, scale_tab):
    TB, nbt = cfg.TB, cfg.nbt

    def body(x_ref, sh_ref, sc_ref, o_ref):
        xhat, _ = _ln(x_ref[...])
        o_ref[...] = (xhat * (1.0 + sc_ref[...]) + sh_ref[...]).astype(BF16)

    tab = pl.BlockSpec((None, 1, D), lambda b, j: (2 * b + jnp.minimum(j, 1), 0, 0))
    return pl.pallas_call(
        body, name="ln_mod_fwd", grid=(cfg.Bn, nbt),
        in_specs=[pl.BlockSpec((TB, D), lambda b, j: (b * nbt + j, 0)), tab, tab],
        out_specs=pl.BlockSpec((TB, D), lambda b, j: (b * nbt + j, 0)),
        out_shape=jax.ShapeDtypeStruct((cfg.NT, D), BF16),
        compiler_params=_cp(2),
    )(xs, shift_tab, scale_tab)


def _dt_fwd(cfg, dt_raw, dt_bias, a_log):
    def body(raw_ref, bias_ref, alog_ref, dt_ref, cum_ref):
        dt = _softplus(raw_ref[...] + bias_ref[...])
        a = -jnp.exp(alog_ref[...])
        dta = dt * a
        col = lax.broadcasted_iota(jnp.int32, (CH, DTW), 1)
        cf = _dot_exact_l(_tri(CH, False), dta)
        cr = _dot_exact_l(_tri(CH, True), dta)
        dt_ref[...] = dt
        cum_ref[...] = jnp.where(col < NH, cf, cr)

    blk = pl.BlockSpec((CH, DTW), lambda i: (i, 0))
    row = pl.BlockSpec((1, DTW), lambda i: (0, 0))
    return pl.pallas_call(
        body, name="dt_fwd", grid=(cfg.NT // CH,),
        in_specs=[blk, row, row], out_specs=[blk, blk],
        out_shape=[jax.ShapeDtypeStruct((cfg.NT, DTW), F32)] * 2,
        compiler_params=_cp(1),
    )(dt_raw, dt_bias, a_log)


def _dt_bwd(cfg, dA, dxx, dt_raw, dt, dt_bias, a_log):
    def body(dA_ref, dxx_ref, raw_ref, dt_ref, bias_ref, alog_ref, o_ref, acc_ref):
        i = pl.program_id(0)

        @pl.when(i == 0)
        def _():
            acc_ref[...] = jnp.zeros_like(acc_ref)

        a = -jnp.exp(alog_ref[...])
        col = lax.broadcasted_iota(jnp.int32, (CH, DTW), 1)
        dA_v = dA_ref[...]
        ddta = jnp.where(col < NH, _dot_exact_l(_tri(CH, True), dA_v), _dot_exact_l(_tri(CH, False), dA_v))
        dtv = dt_ref[...]
        ddt = ddta * a + dxx_ref[...]
        draw = ddt * _sigmoid(raw_ref[...] + bias_ref[...])
        draw = jnp.where(col < 2 * NH, draw, 0.0)
        o_ref[...] = draw
        da = jnp.sum(ddta * dtv, axis=0, keepdims=True) * a
        da = jnp.where(col[:1] < 2 * NH, da, 0.0)
        acc_ref[0:1, :] += da
        acc_ref[1:2, :] += jnp.sum(draw, axis=0, keepdims=True)

    blk = pl.BlockSpec((CH, DTW), lambda i: (i, 0))
    row = pl.BlockSpec((1, DTW), lambda i: (0, 0))
    return pl.pallas_call(
        body, name="dt_bwd", grid=(cfg.NT // CH,),
        in_specs=[blk, blk, blk, blk, row, row],
        out_specs=[blk, pl.BlockSpec((8, DTW), lambda i: (0, 0))],
        out_shape=[jax.ShapeDtypeStruct((cfg.NT, DTW), F32), jax.ShapeDtypeStruct((8, DTW), F32)],
        compiler_params=_cp(1),
    )(dA, dxx, dt_raw, dt, dt_bias, a_log)


def _conv_shift(u, s, pos, R):
    n = u.shape[0]
    rolled = pltpu.roll(u, s % n, 0)
    ok = jnp.logical_and(pos - s >= 0, pos - s < R)
    return jnp.where(ok, rolled, 0.0)


_TAPS = (2, 1, 0, -1)


def _conv_fwd(cfg, proj, conv_w, conv_b):
    TB, nbt = cfg.TB, cfg.nbt
    CB = 512
    n_act = (DI + 2 * NG * NS) // CB

    def body(u_ref, w_ref, b_ref, o_ref):
        i = pl.program_id(0)
        j = pl.program_id(1)
        R = jnp.where(i % nbt == 0, cfg.Tc, GRID_W)
        t = lax.broadcasted_iota(jnp.int32, (TB, CB), 0)
        pos = jnp.bitwise_and(t, R - 1)
        u = u_ref[...]
        pre = b_ref[...] + w_ref[2:3, :] * u
        for k in (0, 1, 3):
            pre = pre + w_ref[k:k + 1, :] * _conv_shift(u, _TAPS[k], pos, R)
        o_ref[...] = jnp.where(j < n_act, _silu(pre), pre)

    return pl.pallas_call(
        body, name="conv_fwd", grid=(cfg.NT // TB, CONVW // CB),
        in_specs=[pl.BlockSpec((TB, CB), lambda i, j: (i, j)),
                  pl.BlockSpec((4, CB), lambda i, j: (0, j)),
                  pl.BlockSpec((1, CB), lambda i, j: (0, j))],
        out_specs=pl.BlockSpec((TB, CB), lambda i, j: (i, j)),
        out_shape=jax.ShapeDtypeStruct((cfg.NT, CONVW), F32),
        compiler_params=_cp(2),
    )(proj, conv_w, conv_b)


def _conv_bwd(cfg, name, proj, conv_w, conv_b, addends, col0, width, act, skip=None, xch=None):
    TB, nbt, nbl = cfg.TB, cfg.nbt, cfg.nbl
    CB = 512
    c0 = col0 // CB
    n_add = len(addends)

    def body(*refs):
        u_ref, w_ref, b_ref = refs[:3]
        add_refs = refs[3:3 + n_add]
        rest = refs[3 + n_add:]
        if skip is not None:
            dy_ref, dv_ref = rest[:2]
            rest = rest[2:]
        o_ref, acc_ref = rest
        i = pl.program_id(1)

        @pl.when(i == 0)
        def _():
            acc_ref[...] = jnp.zeros_like(acc_ref)

        isctx = (i % nbt) == 0
        R = jnp.where(isctx, cfg.Tc, GRID_W)
        t = lax.broadcasted_iota(jnp.int32, (TB, CB), 0)
        pos = jnp.bitwise_and(t, R - 1)
        u = u_ref[...]
        us = [_conv_shift(u, _TAPS[k], pos, R) for k in range(4)]
        g = add_refs[0][...]
        for r in add_refs[1:]:
            g = g + r[...]
        if skip is not None:
            g = g + jnp.where(isctx, 0.0, dv_ref[...] * dy_ref[...])
        if act:
            pre = b_ref[...]
            for k in range(4):
                pre = pre + w_ref[k:k + 1, :] * us[k]
            g = g * _silu_grad(pre)
        dp = jnp.zeros_like(g)
        for k in range(4):
            acc_ref[k:k + 1, :] += jnp.sum(g * us[k], axis=0, keepdims=True)
            dp = dp + w_ref[k:k + 1, :] * _conv_shift(g, -_TAPS[k], pos, R)
        acc_ref[4:5, :] += jnp.sum(g, axis=0, keepdims=True)
        o_ref[...] = dp.astype(BF16)

    blk = pl.BlockSpec((TB, CB), lambda j, i: (i, j))
    in_specs = [pl.BlockSpec((TB, CB), lambda j, i: (i, c0 + j)),
                pl.BlockSpec((4, CB), lambda j, i: (0, c0 + j)),
                pl.BlockSpec((1, CB), lambda j, i: (0, c0 + j))] + [blk] * n_add
    args = [proj, conv_w, conv_b] + list(addends)
    if skip is not None:
        def lat(j, i):
            b = i // nbt
            return (b * nbl + jnp.maximum(i % nbt - 1, 0), j)
        in_specs += [pl.BlockSpec((TB, CB), lat), pl.BlockSpec((1, CB), lambda j, i: (0, j))]
        args += list(skip)
    return _hosted_call(
        body, xch, name=name, grid=(width // CB, cfg.NT // TB),
        in_specs=in_specs,
        out_specs=[blk, pl.BlockSpec((8, CB), lambda j, i: (0, j))],
        out_shape=[jax.ShapeDtypeStruct((cfg.NT, width), BF16), jax.ShapeDtypeStruct((8, width), F32)],
        scratch_shapes=[], compiler_params=_cp(2), args=args)


def _chunk_of_step(cfg, rev):
    nct, nlt = cfg.nct, cfg.nlt
    if not rev:
        return lambda s: s
    return lambda s: jnp.where(s < nct, nct - 1 - s, 2 * nct + nlt - 1 - s)


def _expand4(v4, band):
    out = v4[:, 3:4]
    for h in (2, 1, 0):
        out = jnp.where(band == h, v4[:, h:h + 1], out)
    return out


def _ssd_common(x_ref, b_ref, c_ref, dt_ref, cum_ref, cumT_ref, rev):
    xh = x_ref[...]
    Bm = b_ref[...].astype(BF16)
    Cm = c_ref[...].astype(BF16)
    dt4 = dt_ref[...]
    cum4 = cum_ref[...]
    cumT4 = cumT_ref[...]
    band = lax.broadcasted_iota(jnp.int32, (CH, HPG * HD), 1) // HD
    last = 0 if rev else CH - 1
    llast = cum4[last:last + 1, :]
    e_exp = _expand4(jnp.exp(llast - cum4), band)
    ecum = _expand4(jnp.exp(cum4), band)
    dt_exp = _expand4(dt4, band)
    X = xh * dt_exp
    G = _dot_nt(Cm, Bm)
    ii = lax.broadcasted_iota(jnp.int32, (CH, CH), 0)
    jj = lax.broadcasted_iota(jnp.int32, (CH, CH), 1)
    mask = (jj >= ii) if rev else (jj <= ii)
    decs = []
    for h in range(HPG):
        seg = cum4[:, h:h + 1] - cumT4[h:h + 1, :]
        decs.append(jnp.exp(jnp.where(mask, seg, -1e30)))
    elast = jnp.exp(llast)
    rb = lax.broadcasted_iota(jnp.int32, (HPG * HD, NS), 0) // HD
    dec_rows = elast[:, 3:4]
    for h in (2, 1, 0):
        dec_rows = jnp.where(rb == h, elast[:, h:h + 1], dec_rows)
    return xh, Bm, Cm, band, e_exp, ecum, dt_exp, X, G, decs, elast, dec_rows, last


def _ssd_specs(cfg, rev):
    nch = cfg.nch
    cmap = _chunk_of_step(cfg, rev)
    d = 1 if rev else 0

    def make(stepmap):
        def row(b, g, sp):
            return b * nch + cmap(stepmap(sp))
        return [
            pl.BlockSpec((CH, HPG * HD), lambda b, g, sp: (row(b, g, sp), g)),
            pl.BlockSpec((CH, NS), lambda b, g, sp: (row(b, g, sp), DI // NS + g)),
            pl.BlockSpec((CH, NS), lambda b, g, sp: (row(b, g, sp), (DI + NG * NS) // NS + g)),
            pl.BlockSpec((None, None, CH, HPG), lambda b, g, sp: (d, g, row(b, g, sp), 0)),
            pl.BlockSpec((None, None, CH, HPG), lambda b, g, sp: (d, g, row(b, g, sp), 0)),
            pl.BlockSpec((None, None, None, HPG, CH), lambda b, g, sp: (d, g, row(b, g, sp), 0, 0)),
        ], row
    return make


def _ssd_fwd(cfg, act, dtg, cumg, cumTg, rev):
    nch = cfg.nch
    in_specs, row = _ssd_specs(cfg, rev)(lambda sp: sp)

    def body(x_ref, b_ref, c_ref, dt_ref, cum_ref, cumT_ref, y_ref, hs_ref, h_scr):
        s = pl.program_id(2)

        @pl.when(s == 0)
        def _():
            h_scr[...] = jnp.zeros_like(h_scr)

        xh, Bm, Cm, band, e_exp, ecum, dt_exp, X, G, decs, elast, dec_rows, last = _ssd_common(
            x_ref, b_ref, c_ref, dt_ref, cum_ref, cumT_ref, rev)
        H = h_scr[...]
        Y = ecum * _dot_nt(Cm, H.astype(BF16))
        for h in range(HPG):
            Mh = (G * decs[h]).astype(BF16)
            Y = Y + _dot(Mh, jnp.where(band == h, X, 0.0).astype(BF16))
        y_ref[...] = Y
        hs_ref[...] = H
        S = _dot_tn((e_exp * X).astype(BF16), Bm)
        h_scr[...] = dec_rows * H + S

    return pl.pallas_call(
        body, name="ssd_fwd_rev" if rev else "ssd_fwd", grid=(cfg.Bn, NG, nch),
        in_specs=in_specs,
        out_specs=[pl.BlockSpec((CH, HPG * HD), lambda b, g, s: (row(b, g, s), g)),
                   pl.BlockSpec((None, None, None, HPG * HD, NS), lambda b, g, s: (b, g, s, 0, 0))],
        out_shape=[jax.ShapeDtypeStruct((cfg.NT, DI), F32),
                   jax.ShapeDtypeStruct((cfg.Bn, NG, nch, HPG * HD, NS), F32)],
        scratch_shapes=[pltpu.VMEM((HPG * HD, NS), F32)],
        compiler_params=_cp(3),
    )(act, act, act, dtg, cumg, cumTg)


def _ssd_bwd(cfg, act, dtg, cumg, cumTg, hs, dy, rev, xch=None):
    nch, nct, nlt = cfg.nch, cfg.nct, cfg.nlt
    cmap = _chunk_of_step(cfg, rev)
    in_specs, row = _ssd_specs(cfg, rev)(lambda sp: nch - 1 - sp)

    def lat_row(b, g, sp):
        c = cmap(nch - 1 - sp)
        return b * nlt + jnp.maximum(c - nct, 0)

    def body(x_ref, b_ref, c_ref, dt_ref, cum_ref, cumT_ref, dy_ref, hs_ref,
             dxh_ref, dB_ref, dC_ref, dA_ref, dxx_ref, dh_scr):
        sp = pl.program_id(2)

        @pl.when(sp == 0)
        def _():
            dh_scr[...] = jnp.zeros_like(dh_scr)

        c = cmap(nch - 1 - sp)
        xh, Bm, Cm, band, e_exp, ecum, dt_exp, X, G, decs, elast, dec_rows, last = _ssd_common(
            x_ref, b_ref, c_ref, dt_ref, cum_ref, cumT_ref, rev)
        dY = jnp.where(c < nct, 0.0, dy_ref[...])
        Xb = X.astype(BF16)
        H = hs_ref[...]
        Hb = H.astype(BF16)
        dHn = dh_scr[...]
        dHnb = dHn.astype(BF16)
        dYs = ecum * dY
        dYsb = dYs.astype(BF16)
        BdH = _dot_nt(Bm, dHnb)
        Ys = ecum * _dot_nt(Cm, Hb)
        dX = e_exp * BdH
        dG = jnp.zeros((CH, CH), F32)
        lane4 = lax.broadcasted_iota(jnp.int32, (CH, HPG), 1)
        dA = jnp.zeros((CH, HPG), F32)
        for h in range(HPG):
            M = G * decs[h]
            dYh = jnp.where(band == h, dY, 0.0).astype(BF16)
            dM = _dot_nt(dYh, Xb)
            W = dM * M
            dG = dG + dM * decs[h]
            dX = dX + _dot_tn(M.astype(BF16), dYh)
            colv = jnp.sum(W - W.T, axis=1, keepdims=True)
            dA = dA + jnp.where(lane4 == h, colv, 0.0)
        dGb = dG.astype(BF16)
        eX = e_exp * X
        dC_ref[...] = _dot(dGb, Bm) + _dot(dYsb, Hb)
        dB_ref[...] = _dot_tn(dGb, Cm) + _dot(eX.astype(BF16), dHnb)
        dh_scr[...] = dec_rows * dHn + _dot_tn(dYsb, Cm)
        pb = lax.broadcasted_iota(jnp.int32, (HPG * HD, NS), 0) // HD
        pl_ = lax.broadcasted_iota(jnp.int32, (HPG * HD, NS), 1)
        E = jnp.where(pb == pl_, 1.0, 0.0).astype(BF16)
        t2 = _dot_exact_r(dY * Ys, E)[:, :HPG]
        q = _dot_exact_r(eX * BdH, E)[:, :HPG]
        r = jnp.sum(dHn * H, axis=1, keepdims=True)
        lane1 = lax.broadcasted_iota(jnp.int32, (1, HPG), 1)
        hdot = jnp.zeros((1, HPG), F32)
        for h in range(HPG):
            hv = jnp.sum(r[h * HD:(h + 1) * HD, :], axis=0, keepdims=True)
            hdot = hdot + jnp.where(lane1 == h, hv, 0.0)
        dllast = jnp.sum(q, axis=0, keepdims=True) + elast * hdot
        rowi = lax.broadcasted_iota(jnp.int32, (CH, HPG), 0)
        dA_ref[...] = dA + t2 - q + jnp.where(rowi == last, dllast, 0.0)
        dxh_ref[...] = dX * dt_exp
        dxx_ref[...] = _dot_exact_r(dX * xh, E)[:, :HPG]

    small = pl.BlockSpec((None, CH, HPG), lambda b, g, sp: (g, row(b, g, sp), 0))
    return _hosted_call(
        body, xch, name="ssd_bwd_rev" if rev else "ssd_bwd", grid=(cfg.Bn, NG, nch),
        in_specs=in_specs + [
            pl.BlockSpec((CH, HPG * HD), lambda b, g, sp: (lat_row(b, g, sp), g)),
            pl.BlockSpec((None, None, None, HPG * HD, NS), lambda b, g, sp: (b, g, nch - 1 - sp, 0, 0))],
        out_specs=[pl.BlockSpec((CH, HPG * HD), lambda b, g, sp: (row(b, g, sp), g)),
                   pl.BlockSpec((CH, NS), lambda b, g, sp: (row(b, g, sp), g)),
                   pl.BlockSpec((CH, NS), lambda b, g, sp: (row(b, g, sp), g)),
                   small, small],
        out_shape=[jax.ShapeDtypeStruct((cfg.NT, DI), F32),
                   jax.ShapeDtypeStruct((cfg.NT, NG * NS), F32),
                   jax.ShapeDtypeStruct((cfg.NT, NG * NS), F32),
                   jax.ShapeDtypeStruct((NG, cfg.NT, HPG), F32),
                   jax.ShapeDtypeStruct((NG, cfg.NT, HPG), F32)],
        scratch_shapes=[pltpu.VMEM((HPG * HD, NS), F32)],
        compiler_params=_cp(3), args=(act, act, act, dtg, cumg, cumTg, dy, hs))


def _shift_rows(v, s, fill, toward_later, rowi):
    n = v.shape[0]
    if toward_later:
        return jnp.where(rowi >= s, pltpu.roll(v, s, 0), fill)
    return jnp.where(rowi < n - s, pltpu.roll(v, n - s, 0), fill)


def _lru_gates(u, wa_ref, wi_ref, ba_ref, bi_ref, lam_ref):
    rs, is_ = [], []
    for k in range(LB):
        uk = u[:, k * LBW:(k + 1) * LBW].astype(BF16)
        rs.append(_dot(uk, wa_ref[k].astype(BF16)))
        is_.append(_dot(uk, wi_ref[k].astype(BF16)))
    r = _sigmoid(jnp.concatenate(rs, axis=1) + ba_ref[...])
    ig = _sigmoid(jnp.concatenate(is_, axis=1) + bi_ref[...])
    sp = _softplus(-lam_ref[...])
    la = -LRU_C * r * sp
    a = jnp.exp(la)
    g = jnp.sqrt(_neg_expm1(2.0 * la))
    return r, ig, sp, la, a, g


def _lru_w_specs(d):
    return [pl.BlockSpec((None, LB, LBW, LBW), lambda b, s: (d, 0, 0, 0)),
            pl.BlockSpec((None, LB, LBW, LBW), lambda b, s: (d, 0, 0, 0)),
            pl.BlockSpec((None, 1, LW), lambda b, s: (d, 0, 0)),
            pl.BlockSpec((None, 1, LW), lambda b, s: (d, 0, 0)),
            pl.BlockSpec((None, 1, LW), lambda b, s: (d, 0, 0))]


def _lru_fwd(cfg, act, wa, wi, ba, bi, lam, rev):
    nch = cfg.nch
    cmap = _chunk_of_step(cfg, rev)
    d = 1 if rev else 0
    ucol = (DI + 2 * NG * NS) // LW

    def body(u_ref, wa_ref, wi_ref, ba_ref, bi_ref, lam_ref, h_ref, c_scr):
        s = pl.program_id(1)

        @pl.when(s == 0)
        def _():
            c_scr[...] = jnp.zeros_like(c_scr)

        u = u_ref[...]
        r, ig, sp, la, a, g = _lru_gates(u, wa_ref, wi_ref, ba_ref, bi_ref, lam_ref)
        bv = g * ig * u
        rowi = lax.broadcasted_iota(jnp.int32, (CH, LW), 0)
        av = a
        sh = 1
        while sh < CH:
            a_p = _shift_rows(av, sh, 1.0, not rev, rowi)
            b_p = _shift_rows(bv, sh, 0.0, not rev, rowi)
            bv = av * b_p + bv
            av = av * a_p
            sh *= 2
        h = bv + av * c_scr[0:1, :]
        h_ref[...] = h
        lastr = 0 if rev else CH - 1
        c_scr[0:1, :] = h[lastr:lastr + 1, :]

    return pl.pallas_call(
        body, name="lru_fwd_rev" if rev else "lru_fwd", grid=(cfg.Bn, nch),
        in_specs=[pl.BlockSpec((CH, LW), lambda b, s: (b * nch + cmap(s), ucol))] + _lru_w_specs(d),
        out_specs=pl.BlockSpec((CH, LW), lambda b, s: (b * nch + cmap(s), 0)),
        out_shape=jax.ShapeDtypeStruct((cfg.NT, LW), F32),
        scratch_shapes=[pltpu.VMEM((8, LW), F32)],
        compiler_params=_cp(2),
    )(act, wa, wi, ba, bi, lam)


def _lru_bwd(cfg, act, wa, wi, ba, bi, lam, hd, dyl, rev):
    nch, nct, nlt = cfg.nch, cfg.nct, cfg.nlt
    cmap = _chunk_of_step(cfg, rev)
    d = 1 if rev else 0
    ucol = (DI + 2 * NG * NS) // LW

    def srow(b, sp):
        return b * nch + cmap(nch - 1 - sp)

    def prev_rows(b, sp):
        s = nch - 1 - sp
        cp = cmap(jnp.maximum(s - 1, 0))
        base = (b * nch + cp) * (CH // 8)
        return base + (0 if rev else CH // 8 - 1)

    def lat_row(b, sp):
        c = cmap(nch - 1 - sp)
        return b * nlt + jnp.maximum(c - nct, 0)

    def body(u_ref, wa_ref, wi_ref, ba_ref, bi_ref, lam_ref, h_ref, hp_ref, dy_ref,
             du_ref, dwa_ref, dwi_ref, vec_ref, c_scr):
        b = pl.program_id(0)
        sp_id = pl.program_id(1)
        s = nch - 1 - sp_id

        @pl.when(sp_id == 0)
        def _():
            c_scr[...] = jnp.zeros_like(c_scr)

        @pl.when(jnp.logical_and(b == 0, sp_id == 0))
        def _():
            dwa_ref[...] = jnp.zeros_like(dwa_ref)
            dwi_ref[...] = jnp.zeros_like(dwi_ref)
            vec_ref[...] = jnp.zeros_like(vec_ref)

        c = cmap(s)
        u = u_ref[...]
        r, ig, spl, la, a, g = _lru_gates(u, wa_ref, wi_ref, ba_ref, bi_ref, lam_ref)
        dh = jnp.where(c < nct, 0.0, dy_ref[...])
        rowi = lax.broadcasted_iota(jnp.int32, (CH, LW), 0)
        av = _shift_rows(a, 1, 1.0, rev, rowi)
        bv = dh
        sh = 1
        while sh < CH:
            a_n = _shift_rows(av, sh, 1.0, rev, rowi)
            b_n = _shift_rows(bv, sh, 0.0, rev, rowi)
            bv = av * b_n + bv
            av = av * a_n
            sh *= 2
        lamv = bv + av * c_scr[0:1, :]
        first = CH - 1 if rev else 0
        c_scr[0:1, :] = (a * lamv)[first:first + 1, :]
        hprow = hp_ref[...][(0 if rev else 7):(1 if rev else 8), :]
        hprow = jnp.where(s > 0, hprow, 0.0)
        h_prev = _shift_rows(h_ref[...], 1, hprow, not rev, rowi)
        da = lamv * h_prev
        db = lamv
        iu = ig * u
        dla = da * a - db * iu * (a * a) / g
        dr = dla * (-LRU_C * spl)
        di = db * g * u
        du = db * g * ig
        drp = dr * r * (1.0 - r)
        dip = di * ig * (1.0 - ig)
        dus = []
        for k in range(LB):
            sl = slice(k * LBW, (k + 1) * LBW)
            drk = drp[:, sl].astype(BF16)
            dik = dip[:, sl].astype(BF16)
            uk = u[:, sl].astype(BF16)
            dus.append(_dot_nt(drk, wa_ref[k].astype(BF16)) + _dot_nt(dik, wi_ref[k].astype(BF16)))
            dwa_ref[k] += _dot_tn(uk, drk)
            dwi_ref[k] += _dot_tn(uk, dik)
        du_ref[...] = du + jnp.concatenate(dus, axis=1)
        vec_ref[0:1, :] += jnp.sum(drp, axis=0, keepdims=True)
        vec_ref[1:2, :] += jnp.sum(dip, axis=0, keepdims=True)
        dsp = jnp.sum(dla * (-LRU_C * r), axis=0, keepdims=True)
        vec_ref[2:3, :] += dsp * (-_sigmoid(-lam_ref[...]))

    return pl.pallas_call(
        body, name="lru_bwd_rev" if rev else "lru_bwd", grid=(cfg.Bn, nch),
        in_specs=[pl.BlockSpec((CH, LW), lambda b, sp: (srow(b, sp), ucol))] + _lru_w_specs(d) + [
            pl.BlockSpec((CH, LW), lambda b, sp: (srow(b, sp), 0)),
            pl.BlockSpec((8, LW), lambda b, sp: (prev_rows(b, sp), 0)),
            pl.BlockSpec((CH, LW), lambda b, sp: (lat_row(b, sp), 0))],
        out_specs=[pl.BlockSpec((CH, LW), lambda b, sp: (srow(b, sp), 0)),
                   pl.BlockSpec((LB, LBW, LBW), lambda b, sp: (0, 0, 0)),
                   pl.BlockSpec((LB, LBW, LBW), lambda b, sp: (0, 0, 0)),
                   pl.BlockSpec((8, LW), lambda b, sp: (0, 0))],
        out_shape=[jax.ShapeDtypeStruct((cfg.NT, LW), F32),
                   jax.ShapeDtypeStruct((LB, LBW, LBW), F32),
                   jax.ShapeDtypeStruct((LB, LBW, LBW), F32),
                   jax.ShapeDtypeStruct((8, LW), F32)],
        scratch_shapes=[pltpu.VMEM((8, LW), F32)],
        compiler_params=_cp(2),
    )(act, wa, wi, ba, bi, lam, hd, hd, dyl)


HB = 1024


def _post_ssd_fwd(cfg, yf, yb, act, proj, dvec, norm_w):
    TB, nbt, nbl = cfg.TB, cfg.nbt, cfg.nbl
    zc = CONVW // HB

    def body(yf_ref, yb_ref, xh_ref, z_ref, dv_ref, w_ref, o_ref):
        y = yf_ref[...] + yb_ref[...] + dv_ref[...] * xh_ref[...]
        u = y * _silu(z_ref[...])
        for gi in range(HB // (DI // NG)):
            sl = slice(gi * 256, (gi + 1) * 256)
            ug = u[:, sl]
            rs = lax.rsqrt(jnp.mean(ug * ug, axis=1, keepdims=True) + RMS_EPS)
            o_ref[:, sl] = (ug * rs * w_ref[:, sl]).astype(BF16)

    def st(b, j, cb):
        return (b * nbt + 1 + j, cb)
    return pl.pallas_call(
        body, name="post_ssd_fwd", grid=(cfg.Bn, nbl, DI // HB),
        in_specs=[pl.BlockSpec((TB, HB), st), pl.BlockSpec((TB, HB), st), pl.BlockSpec((TB, HB), st),
                  pl.BlockSpec((TB, HB), lambda b, j, cb: (b * nbt + 1 + j, zc + cb)),
                  pl.BlockSpec((1, HB), lambda b, j, cb: (0, cb)),
                  pl.BlockSpec((1, HB), lambda b, j, cb: (0, cb))],
        out_specs=pl.BlockSpec((TB, HB), lambda b, j, cb: (b * nbl + j, cb)),
        out_shape=jax.ShapeDtypeStruct((cfg.N, DI), BF16),
        compiler_params=_cp(3),
    )(yf, yb, act, proj, dvec, norm_w)


def _post_ssd_bwd(cfg, dn, yf, yb, act, proj, dvec, norm_w):
    TB, nbt, nbl = cfg.TB, cfg.nbt, cfg.nbl
    zc = CONVW // HB

    def body(dn_ref, yf_ref, yb_ref, xh_ref, z_ref, dv_ref, w_ref, dy_ref, dz_ref, acc_ref):
        b = pl.program_id(1)
        j = pl.program_id(2)

        @pl.when(jnp.logical_and(b == 0, j == 0))
        def _():
            acc_ref[...] = jnp.zeros_like(acc_ref)

        xh = xh_ref[...]
        z = z_ref[...]
        y = yf_ref[...] + yb_ref[...] + dv_ref[...] * xh
        sz = _silu(z)
        u = y * sz
        dout = dn_ref[...]
        for gi in range(HB // (DI // NG)):
            sl = slice(gi * 256, (gi + 1) * 256)
            ug0 = u[:, sl]
            rs = lax.rsqrt(jnp.mean(ug0 * ug0, axis=1, keepdims=True) + RMS_EPS)
            ug = ug0 * rs
            do = dout[:, sl]
            acc_ref[0:1, sl] += jnp.sum(do * ug, axis=0, keepdims=True)
            dug = do * w_ref[:, sl]
            du = rs * (dug - ug * jnp.mean(dug * ug, axis=1, keepdims=True))
            dy = du * sz[:, sl]
            dy_ref[:, sl] = dy
            dz_ref[:, sl] = (du * y[:, sl] * _silu_grad(z[:, sl])).astype(BF16)
            acc_ref[1:2, sl] += jnp.sum(dy * xh[:, sl], axis=0, keepdims=True)

    def st(cb, b, j):
        return (b * nbt + 1 + j, cb)

    def la(cb, b, j):
        return (b * nbl + j, cb)
    return pl.pallas_call(
        body, name="post_ssd_bwd", grid=(DI // HB, cfg.Bn, nbl),
        in_specs=[pl.BlockSpec((TB, HB), la), pl.BlockSpec((TB, HB), st), pl.BlockSpec((TB, HB), st),
                  pl.BlockSpec((TB, HB), st),
                  pl.BlockSpec((TB, HB), lambda cb, b, j: (b * nbt + 1 + j, zc + cb)),
                  pl.BlockSpec((1, HB), lambda cb, b, j: (0, cb)),
                  pl.BlockSpec((1, HB), lambda cb, b, j: (0, cb))],
        out_specs=[pl.BlockSpec((TB, HB), la), pl.BlockSpec((TB, HB), la),
                   pl.BlockSpec((8, HB), lambda cb, b, j: (0, cb))],
        out_shape=[jax.ShapeDtypeStruct((cfg.N, DI), F32), jax.ShapeDtypeStruct((cfg.N, DI), BF16),
                   jax.ShapeDtypeStruct((8, DI), F32)],
        compiler_params=_cp(3),
    )(dn, yf, yb, act, proj, dvec, norm_w)


def _post_lru_fwd(cfg, hf, hb, proj):
    TB, nbt, nbl = cfg.TB, cfg.nbt, cfg.nbl
    gc = (CONVW + DI) // HB

    def body(hf_ref, hb_ref, g_ref, o_ref):
        o_ref[...] = ((hf_ref[...] + hb_ref[...]) * _gelu(g_ref[...])).astype(BF16)

    st = pl.BlockSpec((TB, HB), lambda b, j: (b * nbt + 1 + j, 0))
    return pl.pallas_call(
        body, name="post_lru_fwd", grid=(cfg.Bn, nbl),
        in_specs=[st, st, pl.BlockSpec((TB, HB), lambda b, j: (b * nbt + 1 + j, gc))],
        out_specs=pl.BlockSpec((TB, HB), lambda b, j: (b * nbl + j, 0)),
        out_shape=jax.ShapeDtypeStruct((cfg.N, LW), BF16),
        compiler_params=_cp(2),
    )(hf, hb, proj)


def _post_lru_bwd(cfg, dv, hf, hb, proj):
    TB, nbt, nbl = cfg.TB, cfg.nbt, cfg.nbl
    gc = (CONVW + DI) // HB

    def body(dv_ref, hf_ref, hb_ref, g_ref, dy_ref, dg_ref):
        gt = g_ref[...]
        dvv = dv_ref[...]
        dy_ref[...] = dvv * _gelu(gt)
        dg_ref[...] = (dvv * (hf_ref[...] + hb_ref[...]) * _gelu_grad(gt)).astype(BF16)

    st = pl.BlockSpec((TB, HB), lambda b, j: (b * nbt + 1 + j, 0))
    la = pl.BlockSpec((TB, HB), lambda b, j: (b * nbl + j, 0))
    return pl.pallas_call(
        body, name="post_lru_bwd", grid=(cfg.Bn, nbl),
        in_specs=[la, st, st, pl.BlockSpec((TB, HB), lambda b, j: (b * nbt + 1 + j, gc))],
        out_specs=[la, la],
        out_shape=[jax.ShapeDtypeStruct((cfg.N, LW), F32), jax.ShapeDtypeStruct((cfg.N, LW), BF16)],
        compiler_params=_cp(2),
    )(dv, hf, hb, proj)


def _merge_fwd(cfg, proj, b_gate, br_ssd, br_lru):
    TB, nbt, nbl = cfg.TB, cfg.nbt, cfg.nbl
    mc = (CONVW + DI + LW) // HB

    def body(ms_ref, ml_ref, bg_ref, bs_ref, bl_ref, o_ref):
        gs = _sigmoid(ms_ref[...] + bg_ref[:, :D])
        gl = _sigmoid(ml_ref[...] + bg_ref[:, D:])
        o_ref[...] = (gs * bs_ref[...] + gl * bl_ref[...]).astype(BF16)

    la = pl.BlockSpec((TB, D), lambda b, j: (b * nbl + j, 0))
    return pl.pallas_call(
        body, name="merge_fwd", grid=(cfg.Bn, nbl),
        in_specs=[pl.BlockSpec((TB, HB), lambda b, j: (b * nbt + 1 + j, mc)),
                  pl.BlockSpec((TB, HB), lambda b, j: (b * nbt + 1 + j, mc + 1)),
                  pl.BlockSpec((1, 2 * D), lambda b, j: (0, 0)), la, la],
        out_specs=la,
        out_shape=jax.ShapeDtypeStruct((cfg.N, D), BF16),
        compiler_params=_cp(2),
    )(proj, proj, b_gate, br_ssd, br_lru)


def _merge_bwd(cfg, dmix, proj, b_gate, br_ssd, br_lru):
    TB, nbt, nbl = cfg.TB, cfg.nbt, cfg.nbl
    mc = (CONVW + DI + LW) // HB

    def body(dm_ref, ms_ref, ml_ref, bg_ref, bs_ref, bl_ref, ds_ref, dl_ref, dmg_ref, acc_ref):
        b = pl.program_id(0)
        j = pl.program_id(1)

        @pl.when(jnp.logical_and(b == 0, j == 0))
        def _():
            acc_ref[...] = jnp.zeros_like(acc_ref)

        dm = dm_ref[...]
        gs = _sigmoid(ms_ref[...] + bg_ref[:, :D])
        gl = _sigmoid(ml_ref[...] + bg_ref[:, D:])
        ds_ref[...] = (dm * gs).astype(BF16)
        dl_ref[...] = (dm * gl).astype(BF16)
        dps = dm * bs_ref[...] * gs * (1.0 - gs)
        dpl = dm * bl_ref[...] * gl * (1.0 - gl)
        dmg_ref[:, :D] = dps.astype(BF16)
        dmg_ref[:, D:] = dpl.astype(BF16)
        acc_ref[0:1, :D] += jnp.sum(dps, axis=0, keepdims=True)
        acc_ref[0:1, D:] += jnp.sum(dpl, axis=0, keepdims=True)

    la = pl.BlockSpec((TB, D), lambda b, j: (b * nbl + j, 0))
    return pl.pallas_call(
        body, name="merge_bwd", grid=(cfg.Bn, nbl),
        in_specs=[la, pl.BlockSpec((TB, HB), lambda b, j: (b * nbt + 1 + j, mc)),
                  pl.BlockSpec((TB, HB), lambda b, j: (b * nbt + 1 + j, mc + 1)),
                  pl.BlockSpec((1, 2 * D), lambda b, j: (0, 0)), la, la],
        out_specs=[la, la, pl.BlockSpec((TB, 2 * D), lambda b, j: (b * nbl + j, 0)),
                   pl.BlockSpec((8, 2 * D), lambda b, j: (0, 0))],
        out_shape=[jax.ShapeDtypeStruct((cfg.N, D), BF16), jax.ShapeDtypeStruct((cfg.N, D), BF16),
                   jax.ShapeDtypeStruct((cfg.N, 2 * D), BF16), jax.ShapeDtypeStruct((8, 2 * D), F32)],
        compiler_params=_cp(2),
    )(dmix, proj, proj, b_gate, br_ssd, br_lru)


def _resid1_fwd(cfg, xs, x_mix, gate1, shift2, scale2, ln1_g, ln1_b):
    TB, nbt, nbl = cfg.TB, cfg.nbt, cfg.nbl

    def body(x_ref, xm_ref, g1_ref, sh_ref, sc_ref, lg_ref, lb_ref, x1_ref, h2_ref):
        r1 = ALPHA * x_ref[...] + g1_ref[...] * xm_ref[...]
        xh, _ = _ln(r1)
        x1 = xh * lg_ref[...] + lb_ref[...]
        x1_ref[...] = x1
        xh2, _ = _ln(x1)
        h2_ref[...] = (xh2 * (1.0 + sc_ref[...]) + sh_ref[...]).astype(BF16)

    la = pl.BlockSpec((TB, D), lambda b, j: (b * nbl + j, 0))
    ex = pl.BlockSpec((None, 1, D), lambda b, j: (b, 0, 0))
    vec = pl.BlockSpec((1, D), lambda b, j: (0, 0))
    return pl.pallas_call(
        body, name="resid1_fwd", grid=(cfg.Bn, nbl),
        in_specs=[pl.BlockSpec((TB, D), lambda b, j: (b * nbt + 1 + j, 0)), la, ex, ex, ex, vec, vec],
        out_specs=[la, la],
        out_shape=[jax.ShapeDtypeStruct((cfg.N, D), F32), jax.ShapeDtypeStruct((cfg.N, D), BF16)],
        compiler_params=_cp(2),
    )(xs, x_mix, gate1, shift2, scale2, ln1_g, ln1_b)


def _resid1_bwd(cfg, dh2, dx1p, x1, xs, x_mix, gate1, scale2, ln1_g):
    TB, nbt, nbl = cfg.TB, cfg.nbt, cfg.nbl

    def body(dh2_ref, dx1p_ref, x1_ref, x_ref, xm_ref, g1_ref, sc_ref, lg_ref,
             dxm_ref, dxp_ref, ex_ref, gl_ref):
        b = pl.program_id(0)
        j = pl.program_id(1)

        @pl.when(j == 0)
        def _():
            ex_ref[...] = jnp.zeros_like(ex_ref)

        @pl.when(jnp.logical_and(b == 0, j == 0))
        def _():
            gl_ref[...] = jnp.zeros_like(gl_ref)

        dh2 = dh2_ref[...]
        xh2, rs2 = _ln(x1_ref[...])
        ex_ref[0:1, :] += jnp.sum(dh2, axis=0, keepdims=True)
        ex_ref[1:2, :] += jnp.sum(dh2 * xh2, axis=0, keepdims=True)
        dx1 = dx1p_ref[...] + _ln_bwd(dh2 * (1.0 + sc_ref[...]), xh2, rs2)
        xm = xm_ref[...]
        g1 = g1_ref[...]
        r1 = ALPHA * x_ref[...] + g1 * xm
        xh1, rs1 = _ln(r1)
        gl_ref[0:1, :] += jnp.sum(dx1 * xh1, axis=0, keepdims=True)
        gl_ref[1:2, :] += jnp.sum(dx1, axis=0, keepdims=True)
        dr1 = _ln_bwd(dx1 * lg_ref[...], xh1, rs1)
        ex_ref[2:3, :] += jnp.sum(dr1 * xm, axis=0, keepdims=True)
        dxm_ref[...] = (dr1 * g1).astype(BF16)
        dxp_ref[...] = ALPHA * dr1

    la = pl.BlockSpec((TB, D), lambda b, j: (b * nbl + j, 0))
    ex = pl.BlockSpec((None, 1, D), lambda b, j: (b, 0, 0))
    vec = pl.BlockSpec((1, D), lambda b, j: (0, 0))
    return pl.pallas_call(
        body, name="resid1_bwd", grid=(cfg.Bn, nbl),
        in_specs=[la, la, la, pl.BlockSpec((TB, D), lambda b, j: (b * nbt + 1 + j, 0)), la, ex, ex, vec],
        out_specs=[la, la, pl.BlockSpec((None, 8, D), lambda b, j: (b, 0, 0)),
                   pl.BlockSpec((8, D), lambda b, j: (0, 0))],
        out_shape=[jax.ShapeDtypeStruct((cfg.N, D), BF16), jax.ShapeDtypeStruct((cfg.N, D), F32),
                   jax.ShapeDtypeStruct((cfg.Bn, 8, D), F32), jax.ShapeDtypeStruct((8, D), F32)],
        compiler_params=_cp(2),
    )(dh2, dx1p, x1, xs, x_mix, gate1, scale2, ln1_g)


def _mlp_act_fwd(cfg, a1, b1):
    TB = cfg.TB

    def body(a_ref, b_ref, o_ref):
        v = jnp.maximum(a_ref[...] + b_ref[...], 0.0)
        o_ref[...] = (v * v).astype(BF16)

    blk = pl.BlockSpec((TB, HB), lambda i, j: (i, j))
    return pl.pallas_call(
        body, name="mlp_act_fwd", grid=(cfg.N // TB, MLP // HB),
        in_specs=[blk, pl.BlockSpec((1, HB), lambda i, j: (0, j))],
        out_specs=blk, out_shape=jax.ShapeDtypeStruct((cfg.N, MLP), BF16),
        compiler_params=_cp(2),
    )(a1, b1)


def _mlp_act_bwd(cfg, dact, a1, b1):
    TB = cfg.TB

    def body(d_ref, a_ref, b_ref, o_ref, acc_ref):
        i = pl.program_id(1)

        @pl.when(i == 0)
        def _():
            acc_ref[...] = jnp.zeros_like(acc_ref)

        da = d_ref[...] * 2.0 * jnp.maximum(a_ref[...] + b_ref[...], 0.0)
        o_ref[...] = da.astype(BF16)
        acc_ref[0:1, :] += jnp.sum(da, axis=0, keepdims=True)

    blk = pl.BlockSpec((TB, HB), lambda j, i: (i, j))
    return pl.pallas_call(
        body, name="mlp_act_bwd", grid=(MLP // HB, cfg.N // TB),
        in_specs=[blk, blk, pl.BlockSpec((1, HB), lambda j, i: (0, j))],
        out_specs=[blk, pl.BlockSpec((8, HB), lambda j, i: (0, j))],
        out_shape=[jax.ShapeDtypeStruct((cfg.N, MLP), BF16), jax.ShapeDtypeStruct((8, MLP), F32)],
        compiler_params=_cp(2),
    )(dact, a1, b1)


def _final_fwd_bwd(cfg, x1, mlp, b2, gate2, ln2_g, ln2_b, target):
    TB, nbl = cfg.TB, cfg.nbl

    def body(x1_ref, m_ref, b2_ref, g2_ref, lg_ref, lb_ref, t_ref, dm_ref, dx_ref, ex_ref, gl_ref):
        b = pl.program_id(0)
        j = pl.program_id(1)

        @pl.when(j == 0)
        def _():
            ex_ref[...] = jnp.zeros_like(ex_ref)

        @pl.when(jnp.logical_and(b == 0, j == 0))
        def _():
            gl_ref[...] = jnp.zeros_like(gl_ref)

        mv = m_ref[...] + b2_ref[...]
        g2 = g2_ref[...]
        r2 = ALPHA * x1_ref[...] + g2 * mv
        xh, rs = _ln(r2)
        lg = lg_ref[...]
        x2 = xh * lg + lb_ref[...]
        err = x2 - t_ref[...]
        ls = jnp.sum(jnp.sum(err * err, axis=1, keepdims=True), axis=0, keepdims=True) * (0.5 / D)
        gl_ref[3:4, :] += ls
        dx2 = err * (1.0 / D)
        gl_ref[0:1, :] += jnp.sum(dx2 * xh, axis=0, keepdims=True)
        gl_ref[1:2, :] += jnp.sum(dx2, axis=0, keepdims=True)
        dr2 = _ln_bwd(dx2 * lg, xh, rs)
        ex_ref[0:1, :] += jnp.sum(dr2 * mv, axis=0, keepdims=True)
        dmv = dr2 * g2
        gl_ref[2:3, :] += jnp.sum(dmv, axis=0, keepdims=True)
        dm_ref[...] = dmv.astype(BF16)
        dx_ref[...] = ALPHA * dr2

    la = pl.BlockSpec((TB, D), lambda b, j: (b * nbl + j, 0))
    ex = pl.BlockSpec((None, 1, D), lambda b, j: (b, 0, 0))
    vec = pl.BlockSpec((1, D), lambda b, j: (0, 0))
    return pl.pallas_call(
        body, name="final_fwd_bwd", grid=(cfg.Bn, nbl),
        in_specs=[la, la, vec, ex, vec, vec, la],
        out_specs=[la, la, pl.BlockSpec((None, 8, D), lambda b, j: (b, 0, 0)),
                   pl.BlockSpec((8, D), lambda b, j: (0, 0))],
        out_shape=[jax.ShapeDtypeStruct((cfg.N, D), BF16), jax.ShapeDtypeStruct((cfg.N, D), F32),
                   jax.ShapeDtypeStruct((cfg.Bn, 8, D), F32), jax.ShapeDtypeStruct((8, D), F32)],
        compiler_params=_cp(2),
    )(x1, mlp, b2, gate2, ln2_g, ln2_b, target)


def _ln_mod_bwd(cfg, dh_a, dh_b, xs, scale_tab, dxp):
    TB, nbt, nbl = cfg.TB, cfg.nbt, cfg.nbl

    def body(da_ref, db_ref, x_ref, sc_ref, dxp_ref, gx_ref, acc_ref):
        j = pl.program_id(1)

        @pl.when(j <= 1)
        def _():
            acc_ref[...] = jnp.zeros_like(acc_ref)

        dh = da_ref[...] + db_ref[...]
        xhat, rs = _ln(x_ref[...])
        acc_ref[0:1, :] += jnp.sum(dh, axis=0, keepdims=True)
        acc_ref[1:2, :] += jnp.sum(dh * xhat, axis=0, keepdims=True)
        gx_ref[...] = dxp_ref[...] + _ln_bwd(dh * (1.0 + sc_ref[...]), xhat, rs)

    st = pl.BlockSpec((TB, D), lambda b, j: (b * nbt + j, 0))
    la = pl.BlockSpec((TB, D), lambda b, j: (b * nbl + jnp.maximum(j - 1, 0), 0))
    return pl.pallas_call(
        body, name="ln_mod_bwd", grid=(cfg.Bn, nbt),
        in_specs=[st, st, st,
                  pl.BlockSpec((None, 1, D), lambda b, j: (2 * b + jnp.minimum(j, 1), 0, 0)), la],
        out_specs=[la, pl.BlockSpec((None, 8, D), lambda b, j: (2 * b + jnp.minimum(j, 1), 0, 0))],
        out_shape=[jax.ShapeDtypeStruct((cfg.N, D), F32), jax.ShapeDtypeStruct((2 * cfg.Bn, 8, D), F32)],
        compiler_params=_cp(2),
    )(dh_a, dh_b, xs, scale_tab, dxp)


def _perm_w_in(w_in):
    w_main = jnp.concatenate([w_in[:, 0:3072], w_in[:, 4160:5184], w_in[:, 3136:4160], w_in[:, 5184:10304]], axis=1)
    w_dt = jnp.pad(w_in[:, 3072:3136], ((0, 0), (0, DTW - 2 * NH)))
    return w_main, w_dt


def _unperm_w_in(dw_main, dw_dt):
    return jnp.concatenate([dw_main[:, 0:3072], dw_dt[:, :2 * NH], dw_main[:, 4096:5120],
                            dw_main[:, 3072:4096], dw_main[:, 5120:]], axis=1)


def _to_groups(v, cfg):
    return v[:, :2 * NH].reshape(cfg.NT, 2, NG, HPG).transpose(1, 2, 0, 3)


def _unpack_rest(rest_all):
    out, off = {}, 0
    for n, shp, axis in _BIG[1:]:
        shard_shape = (shp[0] // NDEV, shp[1]) if axis == 0 else (shp[0], shp[1] // NDEV)
        r = math.prod(shard_shape) // 1024
        out[n] = _from_slots(rest_all[:, off:off + r, :].reshape((NDEV,) + shard_shape), axis)
        off += r
    return out


def _local_step(cfg, x, ctx, target, m, mc, W, rest_payload):
    Bn, T, Tc = cfg.Bn, cfg.T, cfg.Tc
    NT, N = cfg.NT, cfg.N
    xs = jnp.concatenate([ctx, x], axis=1).reshape(NT, D)
    mch = [m[:, i * D:(i + 1) * D] for i in range(NMOD)]
    ctx_sh = jnp.broadcast_to(mc[None, :D], (Bn, D))
    ctx_sc = jnp.broadcast_to(mc[None, D:], (Bn, D))
    shift_tab = jnp.stack([ctx_sh, mch[0]], axis=1).reshape(2 * Bn, 1, D)
    scale_tab = jnp.stack([ctx_sc, mch[1]], axis=1).reshape(2 * Bn, 1, D)
    gate1 = mch[2].reshape(Bn, 1, D)
    shift2 = mch[3].reshape(Bn, 1, D)
    scale2 = mch[4].reshape(Bn, 1, D)
    gate2 = mch[5].reshape(Bn, 1, D)

    conv_w = jnp.concatenate([W["ssd_conv_w"], W["lru_conv_w"]], axis=1)
    conv_b = jnp.concatenate([W["ssd_conv_b"], W["lru_conv_b"]], axis=1)
    dt_bias = jnp.pad(W["ssd_dt_bias"].reshape(1, 2 * NH), ((0, 0), (0, DTW - 2 * NH)))
    a_log = jnp.pad(W["ssd_a_log"].reshape(1, 2 * NH), ((0, 0), (0, DTW - 2 * NH)))
    dvec = jnp.repeat(W["ssd_d"].reshape(NH), HD).reshape(1, DI)
    lba = W["lru_ba"].reshape(2, 1, LW)
    lbi = W["lru_bi"].reshape(2, 1, LW)
    llam = W["lru_lambda"].reshape(2, 1, LW)

    h = _ln_mod_fwd(cfg, xs, shift_tab, scale_tab)
    proj, rest_all = _mm(h, W["w_main"], "nn", "mm_proj", tm=512, tn=1024, tk=1024, xch=(rest_payload, True))
    W = dict(W, **_unpack_rest(rest_all))
    dt_raw = _mm(h, W["w_dt"], "nn", "mm_dt", tm=512, tn=DTW, tk=1024)
    dt, cum = _dt_fwd(cfg, dt_raw, dt_bias, a_log)
    dtg = _to_groups(dt, cfg)
    cumg = _to_groups(cum, cfg)
    cumTg = cum[:, :2 * NH].reshape(NT // CH, CH, 2, NG, HPG).transpose(2, 3, 0, 4, 1)
    act = _conv_fwd(cfg, proj, conv_w, conv_b)
    ys, hss, hls = [], [], []
    for rev in (False, True):
        y_d, hs_d = _ssd_fwd(cfg, act, dtg, cumg, cumTg, rev)
        ys.append(y_d)
        hss.append(hs_d)
        hls.append(_lru_fwd(cfg, act, W["lru_wa"], W["lru_wi"], lba, lbi, llam, rev))
    nssd = _post_ssd_fwd(cfg, ys[0], ys[1], act, proj, dvec, W["ssd_norm_w"])
    vlru = _post_lru_fwd(cfg, hls[0], hls[1], proj)
    br_ssd = _mm(nssd, W["w_br_ssd"], "nn", "mm_br_ssd", tm=512, tn=1024, tk=1024)
    br_lru = _mm(vlru, W["w_br_lru"], "nn", "mm_br_lru", tm=512, tn=1024, tk=1024)
    mix = _merge_fwd(cfg, proj, W["b_gate"], br_ssd, br_lru)
    x_mix = _mm(mix, W["w_out"], "nn", "mm_out", tm=512, tn=1024, tk=1024)
    x1, h2 = _resid1_fwd(cfg, xs, x_mix, gate1, shift2, scale2, W["ln1_g"], W["ln1_b"])
    a1 = _mm(h2, W["w_mlp1"], "nn", "mm_mlp1", tm=512, tn=1024, tk=1024)
    actm = _mlp_act_fwd(cfg, a1, W["b_mlp1"])
    mlp = _mm(actm, W["w_mlp2"], "nn", "mm_mlp2", tm=512, tn=1024, tk=1024)
    dmlp, dx1p, ex2, gl2 = _final_fwd_bwd(cfg, x1, mlp, W["b_mlp2"], gate2, W["ln2_g"], W["ln2_b"],
                                          target.reshape(N, D))

    g = {}
    g["ln2_g"], g["ln2_b"], g["b_mlp2"] = gl2[0:1], gl2[1:2], gl2[2:3]
    loss_partial = gl2[3, 0]
    gw = {}
    dact = _mm(dmlp, W["w_mlp2"], "nt", "mm_dact", tm=512, tn=1024, tk=1024)
    gw["w_mlp2"] = _mm(actm, dmlp, "tn", "mm_dw_mlp2", BF16, tm=1024, tn=1024, tk=512)
    da1, accb1 = _mlp_act_bwd(cfg, dact, a1, W["b_mlp1"])
    g["b_mlp1"] = accb1[0:1]
    dh2 = _mm(da1, W["w_mlp1"], "nt", "mm_dh2", tm=512, tn=1024, tk=1024)
    gw["w_mlp1"] = _mm(h2, da1, "tn", "mm_dw_mlp1", BF16, tm=1024, tn=1024, tk=512)
    dx_mix, dxp, ex1, gl1 = _resid1_bwd(cfg, dh2, dx1p, x1, xs, x_mix, gate1, scale2, W["ln1_g"])
    g["ln1_g"], g["ln1_b"] = gl1[0:1], gl1[1:2]
    dmix = _mm(dx_mix, W["w_out"], "nt", "mm_dmix", tm=512, tn=1024, tk=1024)
    gw["w_out"] = _mm(mix, dx_mix, "tn", "mm_dw_out", BF16, tm=1024, tn=1024, tk=512)
    dbs, dbl, dmerge, accg = _merge_bwd(cfg, dmix, proj, W["b_gate"], br_ssd, br_lru)
    g["b_gate"] = accg[0:1]
    dnssd = _mm(dbs, W["w_br_ssd"], "nt", "mm_dnssd", tm=512, tn=1024, tk=1024)
    gw["w_br_ssd"] = _mm(nssd, dbs, "tn", "mm_dw_br_ssd", BF16, tm=1024, tn=1024, tk=512)
    dvlru = _mm(dbl, W["w_br_lru"], "nt", "mm_dvlru", tm=512, tn=1024, tk=1024)
    gw["w_br_lru"] = _mm(vlru, dbl, "tn", "mm_dw_br_lru", BF16, tm=1024, tn=1024, tk=512)
    dy, dz, accs = _post_ssd_bwd(cfg, dnssd, ys[0], ys[1], act, proj, dvec, W["ssd_norm_w"])
    g["ssd_norm_w"] = accs[0:1]
    dD_cols = accs[1:2]
    dyl, dlg = _post_lru_bwd(cfg, dvlru, hls[0], hls[1], proj)

    rest_slots = jnp.concatenate([_to_slots(gw[n], axis).reshape(NDEV, -1, 1024) for n, _, axis in _BIG[1:]], axis=1)
    xres = {}
    dxh, dBs, dCs, dAs, dxxs, dus = [], [], [], [], [], []
    dwas, dwis, lvecs = [], [], []
    for i, rev in enumerate((False, True)):
        if i == 0:
            o, xres["rs_rest"] = _ssd_bwd(cfg, act, dtg, cumg, cumTg, hss[i], dy, rev, xch=(rest_slots, False))
        else:
            o = _ssd_bwd(cfg, act, dtg, cumg, cumTg, hss[i], dy, rev)
        dxh.append(o[0]); dBs.append(o[1]); dCs.append(o[2]); dAs.append(o[3]); dxxs.append(o[4])
        du, dwa, dwi, lv = _lru_bwd(cfg, act, W["lru_wa"], W["lru_wi"], lba, lbi, llam, hls[i], dyl, rev)
        dus.append(du); dwas.append(dwa); dwis.append(dwi); lvecs.append(lv)
    lru_payload = jnp.stack([jnp.stack(dwas), jnp.stack(dwis)]).reshape(-1, 1024)
    g["lru_ba"] = jnp.stack([lvecs[0][0], lvecs[1][0]])
    g["lru_bi"] = jnp.stack([lvecs[0][1], lvecs[1][1]])
    g["lru_lambda"] = jnp.stack([lvecs[0][2], lvecs[1][2]])

    def from_groups(a0, a1):
        v = jnp.stack([a0, a1]).transpose(2, 0, 1, 3).reshape(NT, 2 * NH)
        return jnp.pad(v, ((0, 0), (0, DTW - 2 * NH)))
    ddt_raw, accdt = _dt_bwd(cfg, from_groups(dAs[0], dAs[1]), from_groups(dxxs[0], dxxs[1]),
                             dt_raw, dt, dt_bias, a_log)
    g["ssd_a_log"] = accdt[0, :2 * NH].reshape(2, NH)
    g["ssd_dt_bias"] = accdt[1, :2 * NH].reshape(2, NH)

    (dpx, accx), xres["ag_lru"] = _conv_bwd(cfg, "conv_bwd_x", proj, conv_w, conv_b, [dxh[0], dxh[1]], 0, DI, True,
                                            skip=(dy, dvec), xch=(lru_payload, True))
    dpB, accB = _conv_bwd(cfg, "conv_bwd_b", proj, conv_w, conv_b, [dBs[0], dBs[1]], DI, NG * NS, True)
    dpC, accC = _conv_bwd(cfg, "conv_bwd_c", proj, conv_w, conv_b, [dCs[0], dCs[1]], DI + NG * NS, NG * NS, True)
    dpl, accl = _conv_bwd(cfg, "conv_bwd_lru", proj, conv_w, conv_b, [dus[0], dus[1]], DI + 2 * NG * NS, LW, False)
    accssd = jnp.concatenate([accx, accB, accC], axis=1)
    g["ssd_conv_w"], g["ssd_conv_b"] = accssd[0:4], accssd[4:5]
    g["lru_conv_w"], g["lru_conv_b"] = accl[0:4], accl[4:5]

    def to_stream(v):
        w = v.shape[1]
        return jnp.pad(v.reshape(Bn, T, w), ((0, 0), (Tc, 0), (0, 0))).reshape(NT, w)
    dproj = jnp.concatenate([dpx, dpB, dpC, dpl, to_stream(dz), to_stream(dlg), to_stream(dmerge)], axis=1)
    dw_main = _mm(h, dproj, "tn", "mm_dw_main", BF16, tm=1024, tn=1024, tk=512)
    dw_dt = _mm(h, ddt_raw, "tn", "mm_dw_dt", BF16, tm=1024, tn=DTW, tk=512)
    w_in_slots = _to_slots(_unperm_w_in(dw_main, dw_dt), 1).reshape(NDEV, -1, 1024)
    w_in_slots = jnp.pad(w_in_slots, ((0, 0), (0, W_IN_SLOT_ROWS - w_in_slots.shape[1]), (0, 0)))
    dh_a, xres["rs_w_in"] = _mm(dproj, W["w_main"], "nt", "mm_dh_main", tm=512, tn=1024, tk=1024,
                                xch=(w_in_slots, False))
    dh_b = _mm(ddt_raw, W["w_dt"], "nt", "mm_dh_dt", tm=512, tn=1024, tk=DTW)
    grad_x, acct = _ln_mod_bwd(cfg, dh_a, dh_b, xs, scale_tab, dxp)
    acct = acct.reshape(Bn, 2, 8, D)
    dm = jnp.concatenate([acct[:, 1, 0], acct[:, 1, 1], ex1[:, 2], ex1[:, 0], ex1[:, 1], ex2[:, 0]], axis=1)
    dmc = jnp.concatenate([acct[:, 0, 0], acct[:, 0, 1]], axis=1)
    g["ssd_d_cols"] = dD_cols
    return loss_partial, grad_x.reshape(Bn, T, D), g, dm, dmc, xres


MESH = pl.DeviceIdType.MESH
_HBM = pl.BlockSpec(memory_space=pltpu.HBM)


def _me():
    return 4 * lax.axis_index("x") + 2 * lax.axis_index("y") + lax.axis_index("c")


def _peer(k):
    px = (lax.axis_index("x") + ((k >> 2) & 1)) % 2
    py = (lax.axis_index("y") + ((k >> 1) & 1)) % 2
    pc = (lax.axis_index("c") + (k & 1)) % 2
    return (px, py, pc), 4 * px + 2 * py + pc


def _xchg_copies(x_ref, o_ref, send_sems, recv_sems, loc_sem, gather):
    me = _me()
    src_me = x_ref if gather else x_ref.at[me]
    loc = pltpu.make_async_copy(src_me, o_ref.at[me], loc_sem)
    sends, recvs = [], []
    for k in range(1, NDEV):
        peer, pid = _peer(k)
        sends.append(pltpu.make_async_remote_copy(
            src_ref=x_ref if gather else x_ref.at[pid], dst_ref=o_ref.at[me],
            send_sem=send_sems.at[k - 1], recv_sem=recv_sems.at[k - 1],
            device_id=peer, device_id_type=MESH))
        recvs.append(pltpu.make_async_remote_copy(
            src_ref=src_me, dst_ref=o_ref.at[pid],
            send_sem=send_sems.at[k - 1], recv_sem=recv_sems.at[k - 1],
            device_id=peer, device_id_type=MESH))
    return loc, sends, recvs


def _xchg_start(*refs, gather):
    loc, sends, _ = _xchg_copies(*refs, gather)
    loc.start()
    for cp in sends:
        cp.start()


def _xchg_wait(*refs, gather):
    loc, sends, recvs = _xchg_copies(*refs, gather)
    for cp in recvs:
        cp.wait_recv()
    for cp in sends:
        cp.wait_send()
    loc.wait()


_XCHG_SCRATCH = [pltpu.SemaphoreType.DMA((NDEV - 1,)), pltpu.SemaphoreType.DMA((NDEV - 1,)), pltpu.SemaphoreType.DMA]


def _xchg_out_shape(x, gather):
    return jax.ShapeDtypeStruct((NDEV,) + tuple(x.shape if gather else x.shape[1:]), x.dtype)


def _exchange(x, name, gather):
    def body(x_ref, o_ref, send_sems, recv_sems, loc_sem):
        _xchg_start(x_ref, o_ref, send_sems, recv_sems, loc_sem, gather=gather)
        _xchg_wait(x_ref, o_ref, send_sems, recv_sems, loc_sem, gather=gather)

    return pl.pallas_call(
        body, name=name, out_shape=_xchg_out_shape(x, gather),
        in_specs=[_HBM], out_specs=_HBM, scratch_shapes=_XCHG_SCRATCH,
    )(x)


def _hosted_call(body, xch, *, name, grid, in_specs, out_specs, out_shape, scratch_shapes, compiler_params, args):
    if xch is None:
        return pl.pallas_call(body, name=name, grid=grid, in_specs=in_specs, out_specs=out_specs,
                              out_shape=out_shape, scratch_shapes=scratch_shapes,
                              compiler_params=compiler_params)(*args)
    xv, gather = xch
    n_in, n_out, n_scr = len(in_specs), len(out_specs), len(scratch_shapes)

    def wrapped(*refs):
        ins = refs[:n_in]
        x_ref = refs[n_in]
        outs = refs[n_in + 1:n_in + 1 + n_out]
        o_ref = refs[n_in + 1 + n_out]
        scr = refs[n_in + 2 + n_out:]
        own, sems = scr[:n_scr], scr[n_scr:]
        first = functools.reduce(jnp.logical_and, [pl.program_id(a) == 0 for a in range(len(grid))])
        last = functools.reduce(jnp.logical_and, [pl.program_id(a) == grid[a] - 1 for a in range(len(grid))])

        @pl.when(first)
        def _():
            _xchg_start(x_ref, o_ref, *sems, gather=gather)

        body(*ins, *outs, *own)

        @pl.when(last)
        def _():
            _xchg_wait(x_ref, o_ref, *sems, gather=gather)

    res = pl.pallas_call(
        wrapped, name=name, grid=grid, in_specs=list(in_specs) + [_HBM], out_specs=list(out_specs) + [_HBM],
        out_shape=list(out_shape) + [_xchg_out_shape(xv, gather)],
        scratch_shapes=list(scratch_shapes) + _XCHG_SCRATCH, compiler_params=compiler_params,
    )(*args, xv)
    return list(res[:n_out]), res[n_out]


def _row_tile(R, cap, mult=8):
    best = mult
    t = mult
    while t <= min(R, cap):
        if R % t == 0:
            best = t
        t += mult
    assert R % best == 0, R
    return best


def _sum_slots(x, name):
    _, R, C = x.shape
    tr = _row_tile(R, 512, 16 if x.dtype == BF16 else 8)

    def body(x_ref, o_ref):
        acc = x_ref[0].astype(F32)
        for i in range(1, NDEV):
            acc = acc + x_ref[i].astype(F32)
        o_ref[...] = acc

    return pl.pallas_call(
        body, name=name, grid=(R // tr,),
        in_specs=[pl.BlockSpec((NDEV, tr, C), lambda i: (0, i, 0))],
        out_specs=pl.BlockSpec((tr, C), lambda i: (i, 0)),
        out_shape=jax.ShapeDtypeStruct((R, C), F32),
        compiler_params=_cp(1),
    )(x)


def _adamw(w, g, m, v, name):
    R, C = w.shape
    tr = _row_tile(R, 256)
    c1 = 1.0 / (1.0 - ADAM_B1 ** ADAM_STEP)
    c2 = 1.0 / (1.0 - ADAM_B2 ** ADAM_STEP)

    def body(w_ref, g_ref, m_ref, v_ref, d_ref, nm_ref, nv_ref):
        gv = g_ref[...]
        nm = ADAM_B1 * m_ref[...] + (1.0 - ADAM_B1) * gv
        nv = ADAM_B2 * v_ref[...] + (1.0 - ADAM_B2) * (gv * gv)
        d_ref[...] = -ADAM_LR * ((nm * c1) / (jnp.sqrt(nv * c2) + ADAM_EPS) + ADAM_WD * w_ref[...])
        nm_ref[...] = nm
        nv_ref[...] = nv

    blk = pl.BlockSpec((tr, C), lambda i: (i, 0))
    return pl.pallas_call(
        body, name=name, grid=(R // tr,),
        in_specs=[blk] * 4, out_specs=[blk] * 3,
        out_shape=[jax.ShapeDtypeStruct((R, C), F32)] * 3,
        compiler_params=_cp(1),
    )(w, g, m, v)


def _mod_fwd(c_rows, w_shard, b_shard):
    def body(c_ref, w_ref, b_ref, o_ref):
        s = _silu(c_ref[...]).astype(BF16)
        o_ref[...] = _dot(s, w_ref[...].astype(BF16)) + b_ref[...]

    return pl.pallas_call(
        body, name="mod_fwd",
        out_shape=jax.ShapeDtypeStruct((c_rows.shape[0], w_shard.shape[1]), F32),
        compiler_params=pltpu.CompilerParams(vmem_limit_bytes=VMEM_LIMIT_BYTES),
    )(c_rows, w_shard, b_shard)


def _mod_bwd(c_rows, dm_all, dm_shard, w_shard):
    nrow = c_rows.shape[0]

    def body(c_ref, da_ref, ds_ref, w_ref, gw_ref, gb_ref, cc_ref):
        s = _silu(c_ref[...]).astype(BF16)
        ds = ds_ref[...]
        gw_ref[...] = _dot_tn(s, ds.astype(BF16))
        gb_ref[...] = jnp.sum(da_ref[...], axis=0, keepdims=True)
        rowi = lax.broadcasted_iota(jnp.int32, ds.shape, 0)
        dmc = jnp.sum(jnp.where(rowi % 8 >= 4, ds, 0.0), axis=0, keepdims=True)
        dmc8 = jnp.broadcast_to(dmc, (8, ds.shape[1])).astype(BF16)
        cc_ref[...] = _dot_nt(dmc8, w_ref[...].astype(BF16))

    return pl.pallas_call(
        body, name="mod_bwd",
        out_shape=[jax.ShapeDtypeStruct(w_shard.shape, F32),
                   jax.ShapeDtypeStruct((1, dm_all.shape[1]), F32),
                   jax.ShapeDtypeStruct((8, D), F32)],
        compiler_params=pltpu.CompilerParams(vmem_limit_bytes=VMEM_LIMIT_BYTES),
    )(c_rows, dm_all, dm_shard, w_shard)


def _small_finish(cc_pre, c_ctx, dd_cols):
    def body(cc_ref, c_ref, dd_ref, gc_ref, gd_ref):
        gc_ref[...] = cc_ref[...] * _silu_grad(c_ref[...])
        gd_ref[...] = jnp.sum(dd_ref[...], axis=1, keepdims=True)

    return pl.pallas_call(
        body, name="small_finish",
        out_shape=[jax.ShapeDtypeStruct((1, D), F32), jax.ShapeDtypeStruct((NH, 1), F32)],
    )(cc_pre, c_ctx, dd_cols)


_BIG = (("w_in", (D, 10304), 1), ("w_br_ssd", (DI, D), 0), ("w_br_lru", (LW, D), 0), ("w_out", (D, D), 0),
        ("w_mlp1", (D, MLP), 1), ("w_mlp2", (MLP, D), 0))
_SMALL_SH = (("ssd_conv_w", (4, 4096)), ("lru_conv_w", (4, LW)), ("lru_ba", (2, LW)), ("lru_bi", (2, LW)),
             ("lru_lambda", (2, LW)))
_REPL = (("c_ctx", (D,)), ("b_gate", (2 * D,)), ("ssd_conv_b", (4096,)), ("ssd_dt_bias", (2, NH)),
         ("ssd_a_log", (2, NH)), ("ssd_d", (DI,)), ("ssd_norm_w", (DI,)), ("lru_conv_b", (LW,)),
         ("ln1_g", (D,)), ("ln1_b", (D,)),
         ("b_mlp1", (MLP,)), ("b_mlp2", (D,)), ("ln2_g", (D,)), ("ln2_b", (D,)))

_WEIGHT_NAMES = ('c_ctx', 'w_mod', 'b_mod', 'w_in', 'b_gate', 'ssd_conv_w', 'ssd_conv_b', 'ssd_dt_bias', 'ssd_a_log',
                 'ssd_d', 'ssd_norm_w', 'lru_conv_w', 'lru_conv_b', 'lru_wa', 'lru_ba', 'lru_wi', 'lru_bi',
                 'lru_lambda', 'w_br_ssd', 'w_br_lru', 'w_out', 'ln1_g', 'ln1_b', 'w_mlp1', 'b_mlp1', 'w_mlp2',
                 'b_mlp2', 'ln2_g', 'ln2_b')
_ARG_NAMES = ('x', 'c', 'ctx') + _WEIGHT_NAMES + ('loss_target',) + tuple('m_' + n for n in _WEIGHT_NAMES) + tuple(
    'v_' + n for n in _WEIGHT_NAMES)


def _to_slots(full, axis):
    n = full.shape[axis] // NDEV
    if axis == 0:
        return full.reshape(NDEV, n, full.shape[1])
    return full.reshape(full.shape[0], NDEV, n).transpose(1, 0, 2)


def _from_slots(slots, axis):
    if axis == 0:
        return slots.reshape(NDEV * slots.shape[1], slots.shape[2])
    return slots.transpose(1, 0, 2).reshape(slots.shape[1], NDEV * slots.shape[2])


def _pack_rows(arrs, width=1024, mult=8):
    flat = jnp.concatenate([a.reshape(-1) for a in arrs])
    n = flat.shape[0]
    per = width * mult
    tot = -(-n // per) * per
    return jnp.pad(flat, (0, tot - n)).reshape(tot // width, width)


def _unpack_rows(packed, shapes, lead=()):
    nl = len(lead)
    flat = packed.reshape(tuple(lead) + (-1,))
    out, off = [], 0
    for s in shapes:
        n = math.prod(s)
        out.append(flat[..., off:off + n].reshape(tuple(lead) + tuple(s)))
        off += n
    return out


def kernel(x, c, ctx, c_ctx, w_mod, b_mod, w_in, b_gate, ssd_conv_w, ssd_conv_b, ssd_dt_bias, ssd_a_log, ssd_d, ssd_norm_w, lru_conv_w, lru_conv_b, lru_wa, lru_ba, lru_wi, lru_bi, lru_lambda, w_br_ssd, w_br_lru, w_out, ln1_g, ln1_b, w_mlp1, b_mlp1, w_mlp2, b_mlp2, ln2_g, ln2_b, loss_target, m_c_ctx, m_w_mod, m_b_mod, m_w_in, m_b_gate, m_ssd_conv_w, m_ssd_conv_b, m_ssd_dt_bias, m_ssd_a_log, m_ssd_d, m_ssd_norm_w, m_lru_conv_w, m_lru_conv_b, m_lru_wa, m_lru_ba, m_lru_wi, m_lru_bi, m_lru_lambda, m_w_br_ssd, m_w_br_lru, m_w_out, m_ln1_g, m_ln1_b, m_w_mlp1, m_b_mlp1, m_w_mlp2, m_b_mlp2, m_ln2_g, m_ln2_b, v_c_ctx, v_w_mod, v_b_mod, v_w_in, v_b_gate, v_ssd_conv_w, v_ssd_conv_b, v_ssd_dt_bias, v_ssd_a_log, v_ssd_d, v_ssd_norm_w, v_lru_conv_w, v_lru_conv_b, v_lru_wa, v_lru_ba, v_lru_wi, v_lru_bi, v_lru_lambda, v_w_br_ssd, v_w_br_lru, v_w_out, v_ln1_g, v_ln1_b, v_w_mlp1, v_b_mlp1, v_w_mlp2, v_b_mlp2, v_ln2_g, v_ln2_b):
    A = dict(zip(_ARG_NAMES, (x, c, ctx, c_ctx, w_mod, b_mod, w_in, b_gate, ssd_conv_w, ssd_conv_b, ssd_dt_bias, ssd_a_log, ssd_d, ssd_norm_w, lru_conv_w, lru_conv_b, lru_wa, lru_ba, lru_wi, lru_bi, lru_lambda, w_br_ssd, w_br_lru, w_out, ln1_g, ln1_b, w_mlp1, b_mlp1, w_mlp2, b_mlp2, ln2_g, ln2_b, loss_target, m_c_ctx, m_w_mod, m_b_mod, m_w_in, m_b_gate, m_ssd_conv_w, m_ssd_conv_b, m_ssd_dt_bias, m_ssd_a_log, m_ssd_d, m_ssd_norm_w, m_lru_conv_w, m_lru_conv_b, m_lru_wa, m_lru_ba, m_lru_wi, m_lru_bi, m_lru_lambda, m_w_br_ssd, m_w_br_lru, m_w_out, m_ln1_g, m_ln1_b, m_w_mlp1, m_b_mlp1, m_w_mlp2, m_b_mlp2, m_ln2_g, m_ln2_b, v_c_ctx, v_w_mod, v_b_mod, v_w_in, v_b_gate, v_ssd_conv_w, v_ssd_conv_b, v_ssd_dt_bias, v_ssd_a_log, v_ssd_d, v_ssd_norm_w, v_lru_conv_w, v_lru_conv_b, v_lru_wa, v_lru_ba, v_lru_wi, v_lru_bi, v_lru_lambda, v_w_br_ssd, v_w_br_lru, v_w_out, v_ln1_g, v_ln1_b, v_w_mlp1, v_b_mlp1, v_w_mlp2, v_b_mlp2, v_ln2_g, v_ln2_b)))
    Bn, T, _ = x.shape
    Tc = ctx.shape[1]
    cfg = _Cfg(Bn, T, Tc)
    me = _me()
    L = {n: (A[n] if n == "c_ctx" else A[n][0]) for n in _WEIGHT_NAMES}
    nmod = L["w_mod"].shape[1]

    c_all = _exchange(c, "ag_c", True)
    c_rows = jnp.concatenate([c_all.reshape(NDEV * Bn, D), jnp.broadcast_to(c_ctx[None, :], (8, D))], axis=0)
    b_shard = lax.dynamic_slice(L["b_mod"], (me * nmod,), (nmod,)).reshape(1, nmod)
    m_part = _mod_fwd(c_rows, L["w_mod"], b_shard)
    m_all = _exchange(m_part, "ag_mod", True)
    m_full = m_all.transpose(1, 0, 2).reshape(NDEV * Bn + 8, NMOD * D)
    m_mine = lax.dynamic_slice(m_full, (me * Bn, 0), (Bn, NMOD * D))
    mc = m_full[NDEV * Bn, :2 * D]

    w_in_all = _exchange(L["w_in"].astype(BF16), "ag_w_in", True)
    rest_payload = jnp.concatenate([L[n].astype(BF16).reshape(-1, 1024) for n, _, _ in _BIG[1:]], axis=0)
    small_shapes = [(s[0], s[1] // NDEV) for _, s in _SMALL_SH]
    small_all = _exchange(_pack_rows([L[n] for n, _ in _SMALL_SH], width=512), "ag_w_small", True)
    W = {}
    for (n, shp), piece in zip(_SMALL_SH, _unpack_rows(small_all, small_shapes, lead=(NDEV,))):
        W[n] = piece.transpose(1, 0, 2).reshape(shp)
    W["w_main"], W["w_dt"] = _perm_w_in(_from_slots(w_in_all, 1))
    for n in ("ssd_conv_b", "lru_conv_b", "ssd_norm_w", "b_gate", "ln1_g", "ln1_b", "b_mlp1", "b_mlp2", "ln2_g", "ln2_b"):
        W[n] = L[n].reshape(1, -1)
    for n in ("ssd_dt_bias", "ssd_a_log", "ssd_d", "lru_wa", "lru_wi"):
        W[n] = L[n]

    loss_part, grad_x, g, dm, dmc, xres = _local_step(cfg, x, ctx, loss_target, m_mine, mc, W, rest_payload)
    loss = lax.psum(loss_part, ("x", "y", "c"))

    dmc_pad = jnp.pad(dmc, ((0, 4 - Bn), (0, (NMOD - 2) * D)))
    dm_all = _exchange(jnp.concatenate([jnp.pad(dm, ((0, 4 - Bn), (0, 0))), dmc_pad], axis=0), "ag_dm", True)
    dm_all = dm_all.reshape(NDEV * 8, NMOD * D)
    c_rows_b = jnp.concatenate([jnp.pad(c_all, ((0, 0), (0, 4 - Bn), (0, 0))),
                                jnp.broadcast_to(c_ctx[None, None, :], (NDEV, 4, D))], axis=1).reshape(NDEV * 8, D)
    dm_shard = lax.dynamic_slice(dm_all, (0, me * nmod), (NDEV * 8, nmod))
    g_w_mod, g_b_mod, cc_part = _mod_bwd(c_rows_b, dm_all, dm_shard, L["w_mod"])
    g["c_ctx"] = cc_part[0]

    g["ssd_d"] = g.pop("ssd_d_cols")
    small_names = [n for n, _ in _REPL] + [n for n, _ in _SMALL_SH]
    small_full_shapes = [s for _, s in _REPL] + [s for _, s in _SMALL_SH]
    sm_all = _exchange(_pack_rows([g[n] for n in small_names]), "ag_g_small", True)
    sm_sum = _sum_slots(sm_all, "sum_g_small")
    gs = dict(zip(small_names, _unpack_rows(sm_sum, small_full_shapes)))
    gcc, gdd = _small_finish(gs["c_ctx"].reshape(1, D), c_ctx.reshape(1, D), gs["ssd_d"].reshape(NH, HD))
    gs["c_ctx"] = gcc.reshape(D)
    gs["ssd_d"] = gdd.reshape(NH)
    for n, shp in _SMALL_SH:
        ns = shp[1] // NDEV
        gs[n] = lax.dynamic_slice(gs[n], (0, me * ns), (shp[0], ns))
    gs["b_mod"] = g_b_mod.reshape(NMOD * D)
    lru_sum = _sum_slots(xres["ag_lru"], "sum_g_lru").reshape(2, 2, LB, LBW, LBW)
    gs["lru_wa"], gs["lru_wi"] = lru_sum[0], lru_sum[1]

    gb = {}
    red_a = _sum_slots(xres["rs_w_in"], "sum_w_in")
    gb["w_in"] = red_a[:D * (10304 // NDEV) // 1024].reshape(D, 10304 // NDEV)
    red_b = _sum_slots(xres["rs_rest"], "sum_w_rest")
    off = 0
    for n, shp, axis in _BIG[1:]:
        shard_shape = (shp[0] // NDEV, shp[1]) if axis == 0 else (shp[0], shp[1] // NDEV)
        r = math.prod(shard_shape) // 1024
        gb[n] = red_b[off:off + r].reshape(shard_shape)
        off += r
    gb["w_mod"] = g_w_mod

    grads, deltas, new_m, new_v = {}, {}, {}, {}
    big_names = ["w_mod"] + [n for n, _, _ in _BIG]
    for n in big_names:
        shp = L[n].shape
        pk = lambda a: a.reshape(-1, 1024)
        d_, nm_, nv_ = _adamw(pk(L[n]), pk(gb[n]), pk(A["m_" + n][0]), pk(A["v_" + n][0]), "adamw_" + n)
        grads[n], deltas[n], new_m[n], new_v[n] = gb[n], d_.reshape(shp), nm_.reshape(shp), nv_.reshape(shp)
    sm_names = [n for n in _WEIGHT_NAMES if n not in big_names]
    sm_shapes = [L[n].shape for n in sm_names]
    loc = lambda pre: _pack_rows([(A[pre + n] if n == "c_ctx" else A[pre + n][0]) for n in sm_names])
    d_, nm_, nv_ = _adamw(loc(""), _pack_rows([gs[n].reshape(L[n].shape) for n in sm_names]), loc("m_"), loc("v_"),
                          "adamw_small")
    for n, dv, mv, vv in zip(sm_names, _unpack_rows(d_, sm_shapes), _unpack_rows(nm_, sm_shapes),
                             _unpack_rows(nv_, sm_shapes)):
        grads[n], deltas[n], new_m[n], new_v[n] = gs[n].reshape(L[n].shape), dv, mv, vv

    def out(dct):
        return [dct[n] if n == "c_ctx" else dct[n][None] for n in _WEIGHT_NAMES]
    return (loss, grad_x, *out(grads), *out(deltas), *out(new_m), *out(new_v))
```

```python
import functools
import math

import jax
import jax.numpy as jnp
from jax import lax
from jax.experimental import pallas as pl
from jax.experimental.pallas import tpu as pltpu

F32 = jnp.float32
BF16 = jnp.bfloat16

D = 1024
GRID_W = 64
DI = 2048
NH = 32
HD = 64
NG = 8
HPG = 4
NS = 128
CH = 128
LW = 1024
LB = 8
LBW = 128
LRU_C = 8.0
MLP = 4096
NMOD = 6
ALPHA = 2.0 ** 0.25
LN_EPS = 1e-6
RMS_EPS = 1e-5
PM = 10240
DTW = 128
CONVW = 5120
NDEV = 8
W_IN_SLOT_ROWS = 1296

ADAM_LR = 0.001
ADAM_B1 = 0.9
ADAM_B2 = 0.999
ADAM_EPS = 1e-08
ADAM_WD = 0.01
ADAM_STEP = 10

VMEM_LIMIT_BYTES = 56 * 1024 * 1024


def _cp(n_axes):
    return pltpu.CompilerParams(dimension_semantics=("arbitrary",) * n_axes,
                                vmem_limit_bytes=VMEM_LIMIT_BYTES)


def _sigmoid(x):
    return 1.0 / (1.0 + jnp.exp(-x))


def _silu(x):
    return x * _sigmoid(x)


def _silu_grad(x):
    s = _sigmoid(x)
    return s * (1.0 + x * (1.0 - s))


def _log1p_pos(e):
    return jnp.where(e < 1e-2, e * (1.0 - e * (0.5 - e * (1.0 / 3.0))), jnp.log(1.0 + e))


def _softplus(x):
    return jnp.maximum(x, 0.0) + _log1p_pos(jnp.exp(-jnp.abs(x)))


def _neg_expm1(x):
    series = -x * (1.0 + x * (0.5 + x * (1.0 / 6.0 + x * (1.0 / 24.0))))
    return jnp.where(x > -1e-2, series, 1.0 - jnp.exp(x))


_GELU_K = math.sqrt(2.0 / math.pi)


def _gelu(x):
    t = jnp.tanh(_GELU_K * (x + 0.044715 * x * x * x))
    return 0.5 * x * (1.0 + t)


def _gelu_grad(x):
    t = jnp.tanh(_GELU_K * (x + 0.044715 * x * x * x))
    dt = (1.0 - t * t) * _GELU_K * (1.0 + 3.0 * 0.044715 * x * x)
    return 0.5 * (1.0 + t) + 0.5 * x * dt


def _ln(x):
    mu = jnp.mean(x, axis=-1, keepdims=True)
    xc = x - mu
    var = jnp.mean(xc * xc, axis=-1, keepdims=True)
    rs = lax.rsqrt(var + LN_EPS)
    return xc * rs, rs


def _ln_bwd(dy, xhat, rs):
    m1 = jnp.mean(dy, axis=-1, keepdims=True)
    m2 = jnp.mean(dy * xhat, axis=-1, keepdims=True)
    return rs * (dy - m1 - xhat * m2)


def _dot(a, b):
    return lax.dot_general(a, b, (((1,), (0,)), ((), ())), preferred_element_type=F32)


def _dot_nt(a, b):
    return lax.dot_general(a, b, (((1,), (1,)), ((), ())), preferred_element_type=F32)


def _dot_tn(a, b):
    return lax.dot_general(a, b, (((0,), (0,)), ((), ())), preferred_element_type=F32)


def _split3(a):
    a0 = a.astype(BF16)
    r = a - a0.astype(F32)
    a1 = r.astype(BF16)
    a2 = (r - a1.astype(F32)).astype(BF16)
    return a0, a1, a2


def _dot_exact_l(m_bf, a):
    a0, a1, a2 = _split3(a)
    return _dot(m_bf, a0) + _dot(m_bf, a1) + _dot(m_bf, a2)


def _dot_exact_r(a, m_bf):
    a0, a1, a2 = _split3(a)
    return _dot(a0, m_bf) + _dot(a1, m_bf) + _dot(a2, m_bf)


def _tri(n, upper):
    ii = lax.broadcasted_iota(jnp.int32, (n, n), 0)
    kk = lax.broadcasted_iota(jnp.int32, (n, n), 1)
    m = (kk >= ii) if upper else (kk <= ii)
    return jnp.where(m, 1.0, 0.0).astype(BF16)


def _fit(n, t):
    t = min(t, n)
    while n % t:
        t //= 2
    return t


def _mm(a, b, mode, name, out_dtype=F32, tm=512, tn=512, tk=512, xch=None):
    if mode == "nn":
        M, K = a.shape
        N = b.shape[1]
    elif mode == "nt":
        M, K = a.shape
        N = b.shape[0]
    else:
        K, M = a.shape
        N = b.shape[1]
    tm, tn, tk = _fit(M, tm), _fit(N, tn), _fit(K, tk)
    assert M % tm == 0 and N % tn == 0 and K % tk == 0, (name, M, N, K, tm, tn, tk)
    nk = K // tk
    if mode == "tn":
        a_spec = pl.BlockSpec((tk, tm), lambda i, j, k: (k, i))
    else:
        a_spec = pl.BlockSpec((tm, tk), lambda i, j, k: (i, k))
    if mode == "nt":
        b_spec = pl.BlockSpec((tn, tk), lambda i, j, k: (j, k))
    else:
        b_spec = pl.BlockSpec((tk, tn), lambda i, j, k: (k, j))
    dn = {"nn": (((1,), (0,)), ((), ())), "nt": (((1,), (1,)), ((), ())), "tn": (((0,), (0,)), ((), ()))}[mode]

    def body(a_ref, b_ref, o_ref, acc_ref):
        k = pl.program_id(2)

        @pl.when(k == 0)
        def _():
            acc_ref[...] = jnp.zeros_like(acc_ref)

        acc_ref[...] += lax.dot_general(a_ref[...].astype(BF16), b_ref[...].astype(BF16), dn,
                                        preferred_element_type=F32)

        @pl.when(k == nk - 1)
        def _():
            o_ref[...] = acc_ref[...].astype(o_ref.dtype)

    res = _hosted_call(
        body, xch, name=name, grid=(M // tm, N // tn, nk),
        in_specs=[a_spec, b_spec],
        out_specs=[pl.BlockSpec((tm, tn), lambda i, j, k: (i, j))],
        out_shape=[jax.ShapeDtypeStruct((M, N), out_dtype)],
        scratch_shapes=[pltpu.VMEM((tm, tn), F32)],
        compiler_params=_cp(3), args=(a, b))
    if xch is None:
        return res[0]
    return res[0][0], res[1]


class _Cfg:
    def __init__(self, Bn, T, Tc):
        assert T % Tc == 0 and Tc % CH == 0 and Tc % GRID_W == 0
        self.Bn, self.T, self.Tc = Bn, T, Tc
        self.TT = T + Tc
        self.TB = Tc
        self.nbt = self.TT // self.TB
        self.nbl = T // self.TB
        self.NT = Bn * self.TT
        self.N = Bn * T
        self.nct = Tc // CH
        self.nlt = T // CH
        self.nch = self.nct + self.nlt


def _ln_mod_fwd(cfg, xs, shift_tab, scale_tab):
    TB, nbt = cfg.TB, cfg.nbt

    def body(x_ref, sh_ref, sc_ref, o_ref):
        xhat, _ = _ln(x_ref[...])
        o_ref[...] = (xhat * (1.0 + sc_ref[...]) + sh_ref[...]).astype(BF16)

    tab = pl.BlockSpec((None, 1, D), lambda b, j: (2 * b + jnp.minimum(j, 1), 0, 0))
    return pl.pallas_call(
        body, name="ln_mod_fwd", grid=(cfg.Bn, nbt),
        in_specs=[pl.BlockSpec((TB, D), lambda b, j: (b * nbt + j, 0)), tab, tab],
        out_specs=pl.BlockSpec((TB, D), lambda b, j: (b * nbt + j, 0)),
        out_shape=jax.ShapeDtypeStruct((cfg.NT, D), BF16),
        compiler_params=_cp(2),
    )(xs, shift_tab, scale_tab)


def _dt_fwd(cfg, dt_raw, dt_bias, a_log):
    def body(raw_ref, bias_ref, alog_ref, dt_ref, cum_ref):
        dt = _softplus(raw_ref[...] + bias_ref[...])
        a = -jnp.exp(alog_ref[...])
        dta = dt * a
        col = lax.broadcasted_iota(jnp.int32, (CH, DTW), 1)
        cf = _dot_exact_l(_tri(CH, False), dta)
        cr = _dot_exact_l(_tri(CH, True), dta)
        dt_ref[...] = dt
        cum_ref[...] = jnp.where(col < NH, cf, cr)

    blk = pl.BlockSpec((CH, DTW), lambda i: (i, 0))
    row = pl.BlockSpec((1, DTW), lambda i: (0, 0))
    return pl.pallas_call(
        body, name="dt_fwd", grid=(cfg.NT // CH,),
        in_specs=[blk, row, row], out_specs=[blk, blk],
        out_shape=[jax.ShapeDtypeStruct((cfg.NT, DTW), F32)] * 2,
        compiler_params=_cp(1),
    )(dt_raw, dt_bias, a_log)


def _dt_bwd(cfg, dA, dxx, dt_raw, dt, dt_bias, a_log):
    def body(dA_ref, dxx_ref, raw_ref, dt_ref, bias_ref, alog_ref, o_ref, acc_ref):
        i = pl.program_id(0)

        @pl.when(i == 0)
        def _():
            acc_ref[...] = jnp.zeros_like(acc_ref)

        a = -jnp.exp(alog_ref[...])
        col = lax.broadcasted_iota(jnp.int32, (CH, DTW), 1)
        dA_v = dA_ref[...]
        ddta = jnp.where(col < NH, _dot_exact_l(_tri(CH, True), dA_v), _dot_exact_l(_tri(CH, False), dA_v))
        dtv = dt_ref[...]
        ddt = ddta * a + dxx_ref[...]
        draw = ddt * _sigmoid(raw_ref[...] + bias_ref[...])
        draw = jnp.where(col < 2 * NH, draw, 0.0)
        o_ref[...] = draw
        da = jnp.sum(ddta * dtv, axis=0, keepdims=True) * a
        da = jnp.where(col[:1] < 2 * NH, da, 0.0)
        acc_ref[0:1, :] += da
        acc_ref[1:2, :] += jnp.sum(draw, axis=0, keepdims=True)

    blk = pl.BlockSpec((CH, DTW), lambda i: (i, 0))
    row = pl.BlockSpec((1, DTW), lambda i: (0, 0))
    return pl.pallas_call(
        body, name="dt_bwd", grid=(cfg.NT // CH,),
        in_specs=[blk, blk, blk, blk, row, row],
        out_specs=[blk, pl.BlockSpec((8, DTW), lambda i: (0, 0))],
        out_shape=[jax.ShapeDtypeStruct((cfg.NT, DTW), F32), jax.ShapeDtypeStruct((8, DTW), F32)],
        compiler_params=_cp(1),
    )(dA, dxx, dt_raw, dt, dt_bias, a_log)


def _conv_shift(u, s, pos, R):
    n = u.shape[0]
    rolled = pltpu.roll(u, s % n, 0)
    ok = jnp.logical_and(pos - s >= 0, pos - s < R)
    return jnp.where(ok, rolled, 0.0)


_TAPS = (2, 1, 0, -1)


def _conv_fwd(cfg, proj, conv_w, conv_b):
    TB, nbt = cfg.TB, cfg.nbt
    CB = CONVW // 2
    n_act = DI + 2 * NG * NS

    def body(u_ref, w_ref, b_ref, o_ref):
        i = pl.program_id(0)
        j = pl.program_id(1)
        R = jnp.where(i % nbt == 0, cfg.Tc, GRID_W)
        t = lax.broadcasted_iota(jnp.int32, (TB, CB), 0)
        pos = jnp.bitwise_and(t, R - 1)
        u = u_ref[...]
        pre = b_ref[...] + w_ref[2:3, :] * u
        for k in (0, 1, 3):
            pre = pre + w_ref[k:k + 1, :] * _conv_shift(u, _TAPS[k], pos, R)
        col = j * CB + lax.broadcasted_iota(jnp.int32, (1, CB), 1)
        o_ref[...] = jnp.where(col < n_act, _silu(pre), pre)

    return pl.pallas_call(
        body, name="conv_fwd", grid=(cfg.NT // TB, CONVW // CB),
        in_specs=[pl.BlockSpec((TB, CB), lambda i, j: (i, j)),
                  pl.BlockSpec((4, CB), lambda i, j: (0, j)),
                  pl.BlockSpec((1, CB), lambda i, j: (0, j))],
        out_specs=pl.BlockSpec((TB, CB), lambda i, j: (i, j)),
        out_shape=jax.ShapeDtypeStruct((cfg.NT, CONVW), F32),
        compiler_params=_cp(2),
    )(proj, conv_w, conv_b)


def _conv_bwd(cfg, name, proj, conv_w, conv_b, addends, col0, width, act, skip=None, xch=None):
    TB, nbt, nbl = cfg.TB, cfg.nbt, cfg.nbl
    CB = 1024
    c0 = col0 // CB
    n_add = len(addends)

    def body(*refs):
        u_ref, w_ref, b_ref = refs[:3]
        add_refs = refs[3:3 + n_add]
        rest = refs[3 + n_add:]
        if skip is not None:
            dy_ref, dv_ref = rest[:2]
            rest = rest[2:]
        o_ref, acc_ref = rest
        i = pl.program_id(1)

        @pl.when(i == 0)
        def _():
            acc_ref[...] = jnp.zeros_like(acc_ref)

        isctx = (i % nbt) == 0
        R = jnp.where(isctx, cfg.Tc, GRID_W)
        t = lax.broadcasted_iota(jnp.int32, (TB, CB), 0)
        pos = jnp.bitwise_and(t, R - 1)
        u = u_ref[...]
        us = [_conv_shift(u, _TAPS[k], pos, R) for k in range(4)]
        g = add_refs[0][...]
        for r in add_refs[1:]:
            g = g + r[...]
        if skip is not None:
            g = g + jnp.where(isctx, 0.0, dv_ref[...] * dy_ref[...])
        if act:
            pre = b_ref[...]
            for k in range(4):
                pre = pre + w_ref[k:k + 1, :] * us[k]
            g = g * _silu_grad(pre)
        dp = jnp.zeros_like(g)
        for k in range(4):
            acc_ref[k:k + 1, :] += jnp.sum(g * us[k], axis=0, keepdims=True)
            dp = dp + w_ref[k:k + 1, :] * _conv_shift(g, -_TAPS[k], pos, R)
        acc_ref[4:5, :] += jnp.sum(g, axis=0, keepdims=True)
        o_ref[...] = dp.astype(BF16)

    blk = pl.BlockSpec((TB, CB), lambda j, i: (i, j))
    in_specs = [pl.BlockSpec((TB, CB), lambda j, i: (i, c0 + j)),
                pl.BlockSpec((4, CB), lambda j, i: (0, c0 + j)),
                pl.BlockSpec((1, CB), lambda j, i: (0, c0 + j))] + [blk] * n_add
    args = [proj, conv_w, conv_b] + list(addends)
    if skip is not None:
        def lat(j, i):
            b = i // nbt
            return (b * nbl + jnp.maximum(i % nbt - 1, 0), j)
        in_specs += [pl.BlockSpec((TB, CB), lat), pl.BlockSpec((1, CB), lambda j, i: (0, j))]
        args += list(skip)
    return _hosted_call(
        body, xch, name=name, grid=(width // CB, cfg.NT // TB),
        in_specs=in_specs,
        out_specs=[blk, pl.BlockSpec((8, CB), lambda j, i: (0, j))],
        out_shape=[jax.ShapeDtypeStruct((cfg.NT, width), BF16), jax.ShapeDtypeStruct((8, width), F32)],
        scratch_shapes=[], compiler_params=_cp(2), args=args)


def _chunk_of_step(cfg, rev):
    nct, nlt = cfg.nct, cfg.nlt
    if not rev:
        return lambda s: s
    return lambda s: jnp.where(s < nct, nct - 1 - s, 2 * nct + nlt - 1 - s)


def _expand4(v4, band):
    out = v4[:, 3:4]
    for h in (2, 1, 0):
        out = jnp.where(band == h, v4[:, h:h + 1], out)
    return out


GP = 4


def _ssd_group_views(gi, x_ref, b_ref, c_ref, dt_ref, cum_ref, cumT_ref):
    return (x_ref[:, gi * HPG * HD:(gi + 1) * HPG * HD], b_ref[:, gi * NS:(gi + 1) * NS],
            c_ref[:, gi * NS:(gi + 1) * NS], dt_ref[gi], cum_ref[gi], cumT_ref[gi])


def _ssd_common(xh, Bf, Cf, dt4, cum4, cumT4, rev):
    Bm = Bf.astype(BF16)
    Cm = Cf.astype(BF16)
    band = lax.broadcasted_iota(jnp.int32, (CH, HPG * HD), 1) // HD
    last = 0 if rev else CH - 1
    llast = cum4[last:last + 1, :]
    e_exp = _expand4(jnp.exp(llast - cum4), band)
    ecum = _expand4(jnp.exp(cum4), band)
    dt_exp = _expand4(dt4, band)
    X = xh * dt_exp
    G = _dot_nt(Cm, Bm)
    ii = lax.broadcasted_iota(jnp.int32, (CH, CH), 0)
    jj = lax.broadcasted_iota(jnp.int32, (CH, CH), 1)
    mask = (jj >= ii) if rev else (jj <= ii)
    decs = []
    for h in range(HPG):
        seg = cum4[:, h:h + 1] - cumT4[h:h + 1, :]
        decs.append(jnp.exp(jnp.where(mask, seg, -1e30)))
    elast = jnp.exp(llast)
    rb = lax.broadcasted_iota(jnp.int32, (HPG * HD, NS), 0) // HD
    dec_rows = elast[:, 3:4]
    for h in (2, 1, 0):
        dec_rows = jnp.where(rb == h, elast[:, h:h + 1], dec_rows)
    return xh, Bm, Cm, band, e_exp, ecum, dt_exp, X, G, decs, elast, dec_rows, last


def _ssd_specs(cfg, rev):
    nch = cfg.nch
    cmap = _chunk_of_step(cfg, rev)
    d = 1 if rev else 0

    def make(stepmap):
        def row(b, g, sp):
            return b * nch + cmap(stepmap(sp))
        bo, co = DI // (GP * NS), (DI + NG * NS) // (GP * NS)
        return [
            pl.BlockSpec((CH, GP * HPG * HD), lambda b, g, sp: (row(b, g, sp), g)),
            pl.BlockSpec((CH, GP * NS), lambda b, g, sp: (row(b, g, sp), bo + g)),
            pl.BlockSpec((CH, GP * NS), lambda b, g, sp: (row(b, g, sp), co + g)),
            pl.BlockSpec((None, GP, CH, HPG), lambda b, g, sp: (d, g, row(b, g, sp), 0)),
            pl.BlockSpec((None, GP, CH, HPG), lambda b, g, sp: (d, g, row(b, g, sp), 0)),
            pl.BlockSpec((None, GP, None, HPG, CH), lambda b, g, sp: (d, g, row(b, g, sp), 0, 0)),
        ], row
    return make


def _ssd_fwd(cfg, act, dtg, cumg, cumTg, rev):
    nch = cfg.nch
    in_specs, row = _ssd_specs(cfg, rev)(lambda sp: sp)

    def body(x_ref, b_ref, c_ref, dt_ref, cum_ref, cumT_ref, y_ref, hs_ref, h_scr):
        s = pl.program_id(2)

        @pl.when(s == 0)
        def _():
            h_scr[...] = jnp.zeros_like(h_scr)

        for gi in range(GP):
            xh, Bm, Cm, band, e_exp, ecum, dt_exp, X, G, decs, elast, dec_rows, last = _ssd_common(
                *_ssd_group_views(gi, x_ref, b_ref, c_ref, dt_ref, cum_ref, cumT_ref), rev)
            H = h_scr[gi]
            Y = ecum * _dot_nt(Cm, H.astype(BF16))
            for h in range(HPG):
                Mh = (G * decs[h]).astype(BF16)
                Y = Y + _dot(Mh, jnp.where(band == h, X, 0.0).astype(BF16))
            y_ref[:, gi * HPG * HD:(gi + 1) * HPG * HD] = Y
            hs_ref[gi] = H
            S = _dot_tn((e_exp * X).astype(BF16), Bm)
            h_scr[gi] = dec_rows * H + S

    return pl.pallas_call(
        body, name="ssd_fwd_rev" if rev else "ssd_fwd", grid=(cfg.Bn, NG // GP, nch),
        in_specs=in_specs,
        out_specs=[pl.BlockSpec((CH, GP * HPG * HD), lambda b, g, s: (row(b, g, s), g)),
                   pl.BlockSpec((None, GP, None, HPG * HD, NS), lambda b, g, s: (b, g, s, 0, 0))],
        out_shape=[jax.ShapeDtypeStruct((cfg.NT, DI), F32),
                   jax.ShapeDtypeStruct((cfg.Bn, NG, nch, HPG * HD, NS), F32)],
        scratch_shapes=[pltpu.VMEM((GP, HPG * HD, NS), F32)],
        compiler_params=_cp(3),
    )(act, act, act, dtg, cumg, cumTg)


def _ssd_bwd(cfg, act, dtg, cumg, cumTg, hs, dy, rev, xch=None):
    nch, nct, nlt = cfg.nch, cfg.nct, cfg.nlt
    cmap = _chunk_of_step(cfg, rev)
    in_specs, row = _ssd_specs(cfg, rev)(lambda sp: nch - 1 - sp)

    def lat_row(b, g, sp):
        c = cmap(nch - 1 - sp)
        return b * nlt + jnp.maximum(c - nct, 0)

    def body(x_ref, b_ref, c_ref, dt_ref, cum_ref, cumT_ref, dy_ref, hs_ref,
             dxh_ref, dB_ref, dC_ref, dA_ref, dxx_ref, dh_scr):
        sp = pl.program_id(2)

        @pl.when(sp == 0)
        def _():
            dh_scr[...] = jnp.zeros_like(dh_scr)

        c = cmap(nch - 1 - sp)
        for gi in range(GP):
            group_bwd(gi, c < nct, x_ref, b_ref, c_ref, dt_ref, cum_ref, cumT_ref, dy_ref, hs_ref,
                      dxh_ref, dB_ref, dC_ref, dA_ref, dxx_ref, dh_scr)

    def group_bwd(gi, isctx, x_ref, b_ref, c_ref, dt_ref, cum_ref, cumT_ref, dy_ref, hs_ref,
                  dxh_ref, dB_ref, dC_ref, dA_ref, dxx_ref, dh_scr):
        xsl = slice(gi * HPG * HD, (gi + 1) * HPG * HD)
        nsl = slice(gi * NS, (gi + 1) * NS)
        xh, Bm, Cm, band, e_exp, ecum, dt_exp, X, G, decs, elast, dec_rows, last = _ssd_common(
            *_ssd_group_views(gi, x_ref, b_ref, c_ref, dt_ref, cum_ref, cumT_ref), rev)
        dY = jnp.where(isctx, 0.0, dy_ref[:, xsl])
        Xb = X.astype(BF16)
        H = hs_ref[gi]
        Hb = H.astype(BF16)
        dHn = dh_scr[gi]
        dHnb = dHn.astype(BF16)
        dYs = ecum * dY
        dYsb = dYs.astype(BF16)
        BdH = _dot_nt(Bm, dHnb)
        Ys = ecum * _dot_nt(Cm, Hb)
        dX = e_exp * BdH
        dG = jnp.zeros((CH, CH), F32)
        lane4 = lax.broadcasted_iota(jnp.int32, (CH, HPG), 1)
        dA = jnp.zeros((CH, HPG), F32)
        for h in range(HPG):
            M = G * decs[h]
            dYh = jnp.where(band == h, dY, 0.0).astype(BF16)
            dM = _dot_nt(dYh, Xb)
            W = dM * M
            dG = dG + dM * decs[h]
            dX = dX + _dot_tn(M.astype(BF16), dYh)
            colv = jnp.sum(W - W.T, axis=1, keepdims=True)
            dA = dA + jnp.where(lane4 == h, colv, 0.0)
        dGb = dG.astype(BF16)
        eX = e_exp * X
        dC_ref[:, nsl] = _dot(dGb, Bm) + _dot(dYsb, Hb)
        dB_ref[:, nsl] = _dot_tn(dGb, Cm) + _dot(eX.astype(BF16), dHnb)
        dh_scr[gi] = dec_rows * dHn + _dot_tn(dYsb, Cm)
        pb = lax.broadcasted_iota(jnp.int32, (HPG * HD, NS), 0) // HD
        pl_ = lax.broadcasted_iota(jnp.int32, (HPG * HD, NS), 1)
        E = jnp.where(pb == pl_, 1.0, 0.0).astype(BF16)
        t2 = _dot_exact_r(dY * Ys, E)[:, :HPG]
        q = _dot_exact_r(eX * BdH, E)[:, :HPG]
        r = jnp.sum(dHn * H, axis=1, keepdims=True)
        lane1 = lax.broadcasted_iota(jnp.int32, (1, HPG), 1)
        hdot = jnp.zeros((1, HPG), F32)
        for h in range(HPG):
            hv = jnp.sum(r[h * HD:(h + 1) * HD, :], axis=0, keepdims=True)
            hdot = hdot + jnp.where(lane1 == h, hv, 0.0)
        dllast = jnp.sum(q, axis=0, keepdims=True) + elast * hdot
        rowi = lax.broadcasted_iota(jnp.int32, (CH, HPG), 0)
        dA_ref[gi] = dA + t2 - q + jnp.where(rowi == last, dllast, 0.0)
        dxh_ref[:, xsl] = dX * dt_exp
        dxx_ref[gi] = _dot_exact_r(dX * xh, E)[:, :HPG]

    small = pl.BlockSpec((GP, CH, HPG), lambda b, g, sp: (g, row(b, g, sp), 0))
    return _hosted_call(
        body, xch, name="ssd_bwd_rev" if rev else "ssd_bwd", grid=(cfg.Bn, NG // GP, nch),
        in_specs=in_specs + [
            pl.BlockSpec((CH, GP * HPG * HD), lambda b, g, sp: (lat_row(b, g, sp), g)),
            pl.BlockSpec((None, GP, None, HPG * HD, NS), lambda b, g, sp: (b, g, nch - 1 - sp, 0, 0))],
        out_specs=[pl.BlockSpec((CH, GP * HPG * HD), lambda b, g, sp: (row(b, g, sp), g)),
                   pl.BlockSpec((CH, GP * NS), lambda b, g, sp: (row(b, g, sp), g)),
                   pl.BlockSpec((CH, GP * NS), lambda b, g, sp: (row(b, g, sp), g)),
                   small, small],
        out_shape=[jax.ShapeDtypeStruct((cfg.NT, DI), F32),
                   jax.ShapeDtypeStruct((cfg.NT, NG * NS), F32),
                   jax.ShapeDtypeStruct((cfg.NT, NG * NS), F32),
                   jax.ShapeDtypeStruct((NG, cfg.NT, HPG), F32),
                   jax.ShapeDtypeStruct((NG, cfg.NT, HPG), F32)],
        scratch_shapes=[pltpu.VMEM((GP, HPG * HD, NS), F32)],
        compiler_params=_cp(3), args=(act, act, act, dtg, cumg, cumTg, dy, hs))


def _shift_rows(v, s, fill, toward_later, rowi):
    n = v.shape[0]
    if toward_later:
        return jnp.where(rowi >= s, pltpu.roll(v, s, 0), fill)
    return jnp.where(rowi < n - s, pltpu.roll(v, n - s, 0), fill)


def _lru_gates(u, wa_ref, wi_ref, ba_ref, bi_ref, lam_ref):
    rs, is_ = [], []
    for k in range(LB):
        uk = u[:, k * LBW:(k + 1) * LBW].astype(BF16)
        rs.append(_dot(uk, wa_ref[k].astype(BF16)))
        is_.append(_dot(uk, wi_ref[k].astype(BF16)))
    r = _sigmoid(jnp.concatenate(rs, axis=1) + ba_ref[...])
    ig = _sigmoid(jnp.concatenate(is_, axis=1) + bi_ref[...])
    sp = _softplus(-lam_ref[...])
    la = -LRU_C * r * sp
    a = jnp.exp(la)
    g = jnp.sqrt(_neg_expm1(2.0 * la))
    return r, ig, sp, la, a, g


def _lru_w_specs(d):
    return [pl.BlockSpec((None, LB, LBW, LBW), lambda b, s: (d, 0, 0, 0)),
            pl.BlockSpec((None, LB, LBW, LBW), lambda b, s: (d, 0, 0, 0)),
            pl.BlockSpec((None, 1, LW), lambda b, s: (d, 0, 0)),
            pl.BlockSpec((None, 1, LW), lambda b, s: (d, 0, 0)),
            pl.BlockSpec((None, 1, LW), lambda b, s: (d, 0, 0))]


def _lru_fwd(cfg, act, wa, wi, ba, bi, lam, rev):
    nch = cfg.nch
    cmap = _chunk_of_step(cfg, rev)
    d = 1 if rev else 0
    ucol = (DI + 2 * NG * NS) // LW

    def body(u_ref, wa_ref, wi_ref, ba_ref, bi_ref, lam_ref, h_ref, c_scr):
        s = pl.program_id(1)

        @pl.when(s == 0)
        def _():
            c_scr[...] = jnp.zeros_like(c_scr)

        u = u_ref[...]
        r, ig, sp, la, a, g = _lru_gates(u, wa_ref, wi_ref, ba_ref, bi_ref, lam_ref)
        bv = g * ig * u
        rowi = lax.broadcasted_iota(jnp.int32, (CH, LW), 0)
        av = a
        sh = 1
        while sh < CH:
            a_p = _shift_rows(av, sh, 1.0, not rev, rowi)
            b_p = _shift_rows(bv, sh, 0.0, not rev, rowi)
            bv = av * b_p + bv
            av = av * a_p
            sh *= 2
        h = bv + av * c_scr[0:1, :]
        h_ref[...] = h
        lastr = 0 if rev else CH - 1
        c_scr[0:1, :] = h[lastr:lastr + 1, :]

    return pl.pallas_call(
        body, name="lru_fwd_rev" if rev else "lru_fwd", grid=(cfg.Bn, nch),
        in_specs=[pl.BlockSpec((CH, LW), lambda b, s: (b * nch + cmap(s), ucol))] + _lru_w_specs(d),
        out_specs=pl.BlockSpec((CH, LW), lambda b, s: (b * nch + cmap(s), 0)),
        out_shape=jax.ShapeDtypeStruct((cfg.NT, LW), F32),
        scratch_shapes=[pltpu.VMEM((8, LW), F32)],
        compiler_params=_cp(2),
    )(act, wa, wi, ba, bi, lam)


def _lru_bwd(cfg, act, wa, wi, ba, bi, lam, hd, dyl, rev):
    nch, nct, nlt = cfg.nch, cfg.nct, cfg.nlt
    cmap = _chunk_of_step(cfg, rev)
    d = 1 if rev else 0
    ucol = (DI + 2 * NG * NS) // LW

    def srow(b, sp):
        return b * nch + cmap(nch - 1 - sp)

    def prev_rows(b, sp):
        s = nch - 1 - sp
        cp = cmap(jnp.maximum(s - 1, 0))
        base = (b * nch + cp) * (CH // 8)
        return base + (0 if rev else CH // 8 - 1)

    def lat_row(b, sp):
        c = cmap(nch - 1 - sp)
        return b * nlt + jnp.maximum(c - nct, 0)

    def body(u_ref, wa_ref, wi_ref, ba_ref, bi_ref, lam_ref, h_ref, hp_ref, dy_ref,
             du_ref, dwa_ref, dwi_ref, vec_ref, c_scr):
        b = pl.program_id(0)
        sp_id = pl.program_id(1)
        s = nch - 1 - sp_id

        @pl.when(sp_id == 0)
        def _():
            c_scr[...] = jnp.zeros_like(c_scr)

        @pl.when(jnp.logical_and(b == 0, sp_id == 0))
        def _():
            dwa_ref[...] = jnp.zeros_like(dwa_ref)
            dwi_ref[...] = jnp.zeros_like(dwi_ref)
            vec_ref[...] = jnp.zeros_like(vec_ref)

        c = cmap(s)
        u = u_ref[...]
        r, ig, spl, la, a, g = _lru_gates(u, wa_ref, wi_ref, ba_ref, bi_ref, lam_ref)
        dh = jnp.where(c < nct, 0.0, dy_ref[...])
        rowi = lax.broadcasted_iota(jnp.int32, (CH, LW), 0)
        av = _shift_rows(a, 1, 1.0, rev, rowi)
        bv = dh
        sh = 1
        while sh < CH:
            a_n = _shift_rows(av, sh, 1.0, rev, rowi)
            b_n = _shift_rows(bv, sh, 0.0, rev, rowi)
            bv = av * b_n + bv
            av = av * a_n
            sh *= 2
        lamv = bv + av * c_scr[0:1, :]
        first = CH - 1 if rev else 0
        c_scr[0:1, :] = (a * lamv)[first:first + 1, :]
        hprow = hp_ref[...][(0 if rev else 7):(1 if rev else 8), :]
        hprow = jnp.where(s > 0, hprow, 0.0)
        h_prev = _shift_rows(h_ref[...], 1, hprow, not rev, rowi)
        da = lamv * h_prev
        db = lamv
        iu = ig * u
        dla = da * a - db * iu * (a * a) / g
        dr = dla * (-LRU_C * spl)
        di = db * g * u
        du = db * g * ig
        drp = dr * r * (1.0 - r)
        dip = di * ig * (1.0 - ig)
        dus = []
        for k in range(LB):
            sl = slice(k * LBW, (k + 1) * LBW)
            drk = drp[:, sl].astype(BF16)
            dik = dip[:, sl].astype(BF16)
            uk = u[:, sl].astype(BF16)
            dus.append(_dot_nt(drk, wa_ref[k].astype(BF16)) + _dot_nt(dik, wi_ref[k].astype(BF16)))
            dwa_ref[k] += _dot_tn(uk, drk)
            dwi_ref[k] += _dot_tn(uk, dik)
        du_ref[...] = du + jnp.concatenate(dus, axis=1)
        vec_ref[0:1, :] += jnp.sum(drp, axis=0, keepdims=True)
        vec_ref[1:2, :] += jnp.sum(dip, axis=0, keepdims=True)
        dsp = jnp.sum(dla * (-LRU_C * r), axis=0, keepdims=True)
        vec_ref[2:3, :] += dsp * (-_sigmoid(-lam_ref[...]))

    return pl.pallas_call(
        body, name="lru_bwd_rev" if rev else "lru_bwd", grid=(cfg.Bn, nch),
        in_specs=[pl.BlockSpec((CH, LW), lambda b, sp: (srow(b, sp), ucol))] + _lru_w_specs(d) + [
            pl.BlockSpec((CH, LW), lambda b, sp: (srow(b, sp), 0)),
            pl.BlockSpec((8, LW), lambda b, sp: (prev_rows(b, sp), 0)),
            pl.BlockSpec((CH, LW), lambda b, sp: (lat_row(b, sp), 0))],
        out_specs=[pl.BlockSpec((CH, LW), lambda b, sp: (srow(b, sp), 0)),
                   pl.BlockSpec((LB, LBW, LBW), lambda b, sp: (0, 0, 0)),
                   pl.BlockSpec((LB, LBW, LBW), lambda b, sp: (0, 0, 0)),
                   pl.BlockSpec((8, LW), lambda b, sp: (0, 0))],
        out_shape=[jax.ShapeDtypeStruct((cfg.NT, LW), F32),
                   jax.ShapeDtypeStruct((LB, LBW, LBW), F32),
                   jax.ShapeDtypeStruct((LB, LBW, LBW), F32),
                   jax.ShapeDtypeStruct((8, LW), F32)],
        scratch_shapes=[pltpu.VMEM((8, LW), F32)],
        compiler_params=_cp(2),
    )(act, wa, wi, ba, bi, lam, hd, hd, dyl)


HB = 1024


def _post_ssd_fwd(cfg, yf, yb, act, proj, dvec, norm_w):
    TB, nbt, nbl = cfg.TB, cfg.nbt, cfg.nbl
    zc = CONVW // HB

    def body(yf_ref, yb_ref, xh_ref, z_ref, dv_ref, w_ref, o_ref):
        y = yf_ref[...] + yb_ref[...] + dv_ref[...] * xh_ref[...]
        u = y * _silu(z_ref[...])
        for gi in range(HB // (DI // NG)):
            sl = slice(gi * 256, (gi + 1) * 256)
            ug = u[:, sl]
            rs = lax.rsqrt(jnp.mean(ug * ug, axis=1, keepdims=True) + RMS_EPS)
            o_ref[:, sl] = (ug * rs * w_ref[:, sl]).astype(BF16)

    def st(b, j, cb):
        return (b * nbt + 1 + j, cb)
    return pl.pallas_call(
        body, name="post_ssd_fwd", grid=(cfg.Bn, nbl, DI // HB),
        in_specs=[pl.BlockSpec((TB, HB), st), pl.BlockSpec((TB, HB), st), pl.BlockSpec((TB, HB), st),
                  pl.BlockSpec((TB, HB), lambda b, j, cb: (b * nbt + 1 + j, zc + cb)),
                  pl.BlockSpec((1, HB), lambda b, j, cb: (0, cb)),
                  pl.BlockSpec((1, HB), lambda b, j, cb: (0, cb))],
        out_specs=pl.BlockSpec((TB, HB), lambda b, j, cb: (b * nbl + j, cb)),
        out_shape=jax.ShapeDtypeStruct((cfg.N, DI), BF16),
        compiler_params=_cp(3),
    )(yf, yb, act, proj, dvec, norm_w)


def _post_ssd_bwd(cfg, dn, yf, yb, act, proj, dvec, norm_w):
    TB, nbt, nbl = cfg.TB, cfg.nbt, cfg.nbl
    zc = CONVW // HB

    def body(dn_ref, yf_ref, yb_ref, xh_ref, z_ref, dv_ref, w_ref, dy_ref, dz_ref, acc_ref):
        b = pl.program_id(1)
        j = pl.program_id(2)

        @pl.when(jnp.logical_and(b == 0, j == 0))
        def _():
            acc_ref[...] = jnp.zeros_like(acc_ref)

        xh = xh_ref[...]
        z = z_ref[...]
        y = yf_ref[...] + yb_ref[...] + dv_ref[...] * xh
        sz = _silu(z)
        u = y * sz
        dout = dn_ref[...]
        for gi in range(HB // (DI // NG)):
            sl = slice(gi * 256, (gi + 1) * 256)
            ug0 = u[:, sl]
            rs = lax.rsqrt(jnp.mean(ug0 * ug0, axis=1, keepdims=True) + RMS_EPS)
            ug = ug0 * rs
            do = dout[:, sl]
            acc_ref[0:1, sl] += jnp.sum(do * ug, axis=0, keepdims=True)
            dug = do * w_ref[:, sl]
            du = rs * (dug - ug * jnp.mean(dug * ug, axis=1, keepdims=True))
            dy = du * sz[:, sl]
            dy_ref[:, sl] = dy
            dz_ref[:, sl] = (du * y[:, sl] * _silu_grad(z[:, sl])).astype(BF16)
            acc_ref[1:2, sl] += jnp.sum(dy * xh[:, sl], axis=0, keepdims=True)

    def st(cb, b, j):
        return (b * nbt + 1 + j, cb)

    def la(cb, b, j):
        return (b * nbl + j, cb)
    return pl.pallas_call(
        body, name="post_ssd_bwd", grid=(DI // HB, cfg.Bn, nbl),
        in_specs=[pl.BlockSpec((TB, HB), la), pl.BlockSpec((TB, HB), st), pl.BlockSpec((TB, HB), st),
                  pl.BlockSpec((TB, HB), st),
                  pl.BlockSpec((TB, HB), lambda cb, b, j: (b * nbt + 1 + j, zc + cb)),
                  pl.BlockSpec((1, HB), lambda cb, b, j: (0, cb)),
                  pl.BlockSpec((1, HB), lambda cb, b, j: (0, cb))],
        out_specs=[pl.BlockSpec((TB, HB), la), pl.BlockSpec((TB, HB), la),
                   pl.BlockSpec((8, HB), lambda cb, b, j: (0, cb))],
        out_shape=[jax.ShapeDtypeStruct((cfg.N, DI), F32), jax.ShapeDtypeStruct((cfg.N, DI), BF16),
                   jax.ShapeDtypeStruct((8, DI), F32)],
        compiler_params=_cp(3),
    )(dn, yf, yb, act, proj, dvec, norm_w)


def _post_lru_fwd(cfg, hf, hb, proj):
    TB, nbt, nbl = cfg.TB, cfg.nbt, cfg.nbl
    gc = (CONVW + DI) // HB

    def body(hf_ref, hb_ref, g_ref, o_ref):
        o_ref[...] = ((hf_ref[...] + hb_ref[...]) * _gelu(g_ref[...])).astype(BF16)

    st = pl.BlockSpec((TB, HB), lambda b, j: (b * nbt + 1 + j, 0))
    return pl.pallas_call(
        body, name="post_lru_fwd", grid=(cfg.Bn, nbl),
        in_specs=[st, st, pl.BlockSpec((TB, HB), lambda b, j: (b * nbt + 1 + j, gc))],
        out_specs=pl.BlockSpec((TB, HB), lambda b, j: (b * nbl + j, 0)),
        out_shape=jax.ShapeDtypeStruct((cfg.N, LW), BF16),
        compiler_params=_cp(2),
    )(hf, hb, proj)


def _post_lru_bwd(cfg, dv, hf, hb, proj):
    TB, nbt, nbl = cfg.TB, cfg.nbt, cfg.nbl
    gc = (CONVW + DI) // HB

    def body(dv_ref, hf_ref, hb_ref, g_ref, dy_ref, dg_ref):
        gt = g_ref[...]
        dvv = dv_ref[...]
        dy_ref[...] = dvv * _gelu(gt)
        dg_ref[...] = (dvv * (hf_ref[...] + hb_ref[...]) * _gelu_grad(gt)).astype(BF16)

    st = pl.BlockSpec((TB, HB), lambda b, j: (b * nbt + 1 + j, 0))
    la = pl.BlockSpec((TB, HB), lambda b, j: (b * nbl + j, 0))
    return pl.pallas_call(
        body, name="post_lru_bwd", grid=(cfg.Bn, nbl),
        in_specs=[la, st, st, pl.BlockSpec((TB, HB), lambda b, j: (b * nbt + 1 + j, gc))],
        out_specs=[la, la],
        out_shape=[jax.ShapeDtypeStruct((cfg.N, LW), F32), jax.ShapeDtypeStruct((cfg.N, LW), BF16)],
        compiler_params=_cp(2),
    )(dv, hf, hb, proj)


def _merge_fwd(cfg, proj, b_gate, br_ssd, br_lru):
    TB, nbt, nbl = cfg.TB, cfg.nbt, cfg.nbl
    mc = (CONVW + DI + LW) // HB

    def body(ms_ref, ml_ref, bg_ref, bs_ref, bl_ref, o_ref):
        gs = _sigmoid(ms_ref[...] + bg_ref[:, :D])
        gl = _sigmoid(ml_ref[...] + bg_ref[:, D:])
        o_ref[...] = (gs * bs_ref[...] + gl * bl_ref[...]).astype(BF16)

    la = pl.BlockSpec((TB, D), lambda b, j: (b * nbl + j, 0))
    return pl.pallas_call(
        body, name="merge_fwd", grid=(cfg.Bn, nbl),
        in_specs=[pl.BlockSpec((TB, HB), lambda b, j: (b * nbt + 1 + j, mc)),
                  pl.BlockSpec((TB, HB), lambda b, j: (b * nbt + 1 + j, mc + 1)),
                  pl.BlockSpec((1, 2 * D), lambda b, j: (0, 0)), la, la],
        out_specs=la,
        out_shape=jax.ShapeDtypeStruct((cfg.N, D), BF16),
        compiler_params=_cp(2),
    )(proj, proj, b_gate, br_ssd, br_lru)


def _merge_bwd(cfg, dmix, proj, b_gate, br_ssd, br_lru):
    TB, nbt, nbl = cfg.TB, cfg.nbt, cfg.nbl
    mc = (CONVW + DI + LW) // HB

    def body(dm_ref, ms_ref, ml_ref, bg_ref, bs_ref, bl_ref, ds_ref, dl_ref, dmg_ref, acc_ref):
        b = pl.program_id(0)
        j = pl.program_id(1)

        @pl.when(jnp.logical_and(b == 0, j == 0))
        def _():
            acc_ref[...] = jnp.zeros_like(acc_ref)

        dm = dm_ref[...]
        gs = _sigmoid(ms_ref[...] + bg_ref[:, :D])
        gl = _sigmoid(ml_ref[...] + bg_ref[:, D:])
        ds_ref[...] = (dm * gs).astype(BF16)
        dl_ref[...] = (dm * gl).astype(BF16)
        dps = dm * bs_ref[...] * gs * (1.0 - gs)
        dpl = dm * bl_ref[...] * gl * (1.0 - gl)
        dmg_ref[:, :D] = dps.astype(BF16)
        dmg_ref[:, D:] = dpl.astype(BF16)
        acc_ref[0:1, :D] += jnp.sum(dps, axis=0, keepdims=True)
        acc_ref[0:1, D:] += jnp.sum(dpl, axis=0, keepdims=True)

    la = pl.BlockSpec((TB, D), lambda b, j: (b * nbl + j, 0))
    return pl.pallas_call(
        body, name="merge_bwd", grid=(cfg.Bn, nbl),
        in_specs=[la, pl.BlockSpec((TB, HB), lambda b, j: (b * nbt + 1 + j, mc)),
                  pl.BlockSpec((TB, HB), lambda b, j: (b * nbt + 1 + j, mc + 1)),
                  pl.BlockSpec((1, 2 * D), lambda b, j: (0, 0)), la, la],
        out_specs=[la, la, pl.BlockSpec((TB, 2 * D), lambda b, j: (b * nbl + j, 0)),
                   pl.BlockSpec((8, 2 * D), lambda b, j: (0, 0))],
        out_shape=[jax.ShapeDtypeStruct((cfg.N, D), BF16), jax.ShapeDtypeStruct((cfg.N, D), BF16),
                   jax.ShapeDtypeStruct((cfg.N, 2 * D), BF16), jax.ShapeDtypeStruct((8, 2 * D), F32)],
        compiler_params=_cp(2),
    )(dmix, proj, proj, b_gate, br_ssd, br_lru)


def _resid1_fwd(cfg, xs, x_mix, gate1, shift2, scale2, ln1_g, ln1_b):
    TB, nbt, nbl = cfg.TB, cfg.nbt, cfg.nbl

    def body(x_ref, xm_ref, g1_ref, sh_ref, sc_ref, lg_ref, lb_ref, x1_ref, h2_ref):
        r1 = ALPHA * x_ref[...] + g1_ref[...] * xm_ref[...]
        xh, _ = _ln(r1)
        x1 = xh * lg_ref[...] + lb_ref[...]
        x1_ref[...] = x1
        xh2, _ = _ln(x1)
        h2_ref[...] = (xh2 * (1.0 + sc_ref[...]) + sh_ref[...]).astype(BF16)

    la = pl.BlockSpec((TB, D), lambda b, j: (b * nbl + j, 0))
    ex = pl.BlockSpec((None, 1, D), lambda b, j: (b, 0, 0))
    vec = pl.BlockSpec((1, D), lambda b, j: (0, 0))
    return pl.pallas_call(
        body, name="resid1_fwd", grid=(cfg.Bn, nbl),
        in_specs=[pl.BlockSpec((TB, D), lambda b, j: (b * nbt + 1 + j, 0)), la, ex, ex, ex, vec, vec],
        out_specs=[la, la],
        out_shape=[jax.ShapeDtypeStruct((cfg.N, D), F32), jax.ShapeDtypeStruct((cfg.N, D), BF16)],
        compiler_params=_cp(2),
    )(xs, x_mix, gate1, shift2, scale2, ln1_g, ln1_b)


def _resid1_bwd(cfg, dh2, dx1p, x1, xs, x_mix, gate1, scale2, ln1_g):
    TB, nbt, nbl = cfg.TB, cfg.nbt, cfg.nbl

    def body(dh2_ref, dx1p_ref, x1_ref, x_ref, xm_ref, g1_ref, sc_ref, lg_ref,
             dxm_ref, dxp_ref, ex_ref, gl_ref):
        b = pl.program_id(0)
        j = pl.program_id(1)

        @pl.when(j == 0)
        def _():
            ex_ref[...] = jnp.zeros_like(ex_ref)

        @pl.when(jnp.logical_and(b == 0, j == 0))
        def _():
            gl_ref[...] = jnp.zeros_like(gl_ref)

        dh2 = dh2_ref[...]
        xh2, rs2 = _ln(x1_ref[...])
        ex_ref[0:1, :] += jnp.sum(dh2, axis=0, keepdims=True)
        ex_ref[1:2, :] += jnp.sum(dh2 * xh2, axis=0, keepdims=True)
        dx1 = dx1p_ref[...] + _ln_bwd(dh2 * (1.0 + sc_ref[...]), xh2, rs2)
        xm = xm_ref[...]
        g1 = g1_ref[...]
        r1 = ALPHA * x_ref[...] + g1 * xm
        xh1, rs1 = _ln(r1)
        gl_ref[0:1, :] += jnp.sum(dx1 * xh1, axis=0, keepdims=True)
        gl_ref[1:2, :] += jnp.sum(dx1, axis=0, keepdims=True)
        dr1 = _ln_bwd(dx1 * lg_ref[...], xh1, rs1)
        ex_ref[2:3, :] += jnp.sum(dr1 * xm, axis=0, keepdims=True)
        dxm_ref[...] = (dr1 * g1).astype(BF16)
        dxp_ref[...] = ALPHA * dr1

    la = pl.BlockSpec((TB, D), lambda b, j: (b * nbl + j, 0))
    ex = pl.BlockSpec((None, 1, D), lambda b, j: (b, 0, 0))
    vec = pl.BlockSpec((1, D), lambda b, j: (0, 0))
    return pl.pallas_call(
        body, name="resid1_bwd", grid=(cfg.Bn, nbl),
        in_specs=[la, la, la, pl.BlockSpec((TB, D), lambda b, j: (b * nbt + 1 + j, 0)), la, ex, ex, vec],
        out_specs=[la, la, pl.BlockSpec((None, 8, D), lambda b, j: (b, 0, 0)),
                   pl.BlockSpec((8, D), lambda b, j: (0, 0))],
        out_shape=[jax.ShapeDtypeStruct((cfg.N, D), BF16), jax.ShapeDtypeStruct((cfg.N, D), F32),
                   jax.ShapeDtypeStruct((cfg.Bn, 8, D), F32), jax.ShapeDtypeStruct((8, D), F32)],
        compiler_params=_cp(2),
    )(dh2, dx1p, x1, xs, x_mix, gate1, scale2, ln1_g)


def _mlp_act_fwd(cfg, a1, b1):
    TB = cfg.TB

    def body(a_ref, b_ref, o_ref):
        v = jnp.maximum(a_ref[...] + b_ref[...], 0.0)
        o_ref[...] = (v * v).astype(BF16)

    blk = pl.BlockSpec((TB, MLP), lambda i: (i, 0))
    return pl.pallas_call(
        body, name="mlp_act_fwd", grid=(cfg.N // TB,),
        in_specs=[blk, pl.BlockSpec((1, MLP), lambda i: (0, 0))],
        out_specs=blk, out_shape=jax.ShapeDtypeStruct((cfg.N, MLP), BF16),
        compiler_params=_cp(1),
    )(a1, b1)


def _mlp_act_bwd(cfg, dact, a1, b1):
    TB = cfg.TB

    def body(d_ref, a_ref, b_ref, o_ref, acc_ref):
        i = pl.program_id(0)

        @pl.when(i == 0)
        def _():
            acc_ref[...] = jnp.zeros_like(acc_ref)

        da = d_ref[...] * 2.0 * jnp.maximum(a_ref[...] + b_ref[...], 0.0)
        o_ref[...] = da.astype(BF16)
        acc_ref[0:1, :] += jnp.sum(da, axis=0, keepdims=True)

    blk = pl.BlockSpec((TB, MLP), lambda i: (i, 0))
    return pl.pallas_call(
        body, name="mlp_act_bwd", grid=(cfg.N // TB,),
        in_specs=[blk, blk, pl.BlockSpec((1, MLP), lambda i: (0, 0))],
        out_specs=[blk, pl.BlockSpec((8, MLP), lambda i: (0, 0))],
        out_shape=[jax.ShapeDtypeStruct((cfg.N, MLP), BF16), jax.ShapeDtypeStruct((8, MLP), F32)],
        compiler_params=_cp(1),
    )(dact, a1, b1)


def _final_fwd_bwd(cfg, x1, mlp, b2, gate2, ln2_g, ln2_b, target):
    TB, nbl = cfg.TB, cfg.nbl

    def body(x1_ref, m_ref, b2_ref, g2_ref, lg_ref, lb_ref, t_ref, dm_ref, dx_ref, ex_ref, gl_ref):
        b = pl.program_id(0)
        j = pl.program_id(1)

        @pl.when(j == 0)
        def _():
            ex_ref[...] = jnp.zeros_like(ex_ref)

        @pl.when(jnp.logical_and(b == 0, j == 0))
        def _():
            gl_ref[...] = jnp.zeros_like(gl_ref)

        mv = m_ref[...] + b2_ref[...]
        g2 = g2_ref[...]
        r2 = ALPHA * x1_ref[...] + g2 * mv
        xh, rs = _ln(r2)
        lg = lg_ref[...]
        x2 = xh * lg + lb_ref[...]
        err = x2 - t_ref[...]
        ls = jnp.sum(jnp.sum(err * err, axis=1, keepdims=True), axis=0, keepdims=True) * (0.5 / D)
        gl_ref[3:4, :] += ls
        dx2 = err * (1.0 / D)
        gl_ref[0:1, :] += jnp.sum(dx2 * xh, axis=0, keepdims=True)
        gl_ref[1:2, :] += jnp.sum(dx2, axis=0, keepdims=True)
        dr2 = _ln_bwd(dx2 * lg, xh, rs)
        ex_ref[0:1, :] += jnp.sum(dr2 * mv, axis=0, keepdims=True)
        dmv = dr2 * g2
        gl_ref[2:3, :] += jnp.sum(dmv, axis=0, keepdims=True)
        dm_ref[...] = dmv.astype(BF16)
        dx_ref[...] = ALPHA * dr2

    la = pl.BlockSpec((TB, D), lambda b, j: (b * nbl + j, 0))
    ex = pl.BlockSpec((None, 1, D), lambda b, j: (b, 0, 0))
    vec = pl.BlockSpec((1, D), lambda b, j: (0, 0))
    return pl.pallas_call(
        body, name="final_fwd_bwd", grid=(cfg.Bn, nbl),
        in_specs=[la, la, vec, ex, vec, vec, la],
        out_specs=[la, la, pl.BlockSpec((None, 8, D), lambda b, j: (b, 0, 0)),
                   pl.BlockSpec((8, D), lambda b, j: (0, 0))],
        out_shape=[jax.ShapeDtypeStruct((cfg.N, D), BF16), jax.ShapeDtypeStruct((cfg.N, D), F32),
                   jax.ShapeDtypeStruct((cfg.Bn, 8, D), F32), jax.ShapeDtypeStruct((8, D), F32)],
        compiler_params=_cp(2),
    )(x1, mlp, b2, gate2, ln2_g, ln2_b, target)


def _ln_mod_bwd(cfg, dh_a, dh_b, xs, scale_tab, dxp):
    TB, nbt, nbl = cfg.TB, cfg.nbt, cfg.nbl

    def body(da_ref, db_ref, x_ref, sc_ref, dxp_ref, gx_ref, acc_ref):
        j = pl.program_id(1)

        @pl.when(j <= 1)
        def _():
            acc_ref[...] = jnp.zeros_like(acc_ref)

        dh = da_ref[...] + db_ref[...]
        xhat, rs = _ln(x_ref[...])
        acc_ref[0:1, :] += jnp.sum(dh, axis=0, keepdims=True)
        acc_ref[1:2, :] += jnp.sum(dh * xhat, axis=0, keepdims=True)
        gx_ref[...] = dxp_ref[...] + _ln_bwd(dh * (1.0 + sc_ref[...]), xhat, rs)

    st = pl.BlockSpec((TB, D), lambda b, j: (b * nbt + j, 0))
    la = pl.BlockSpec((TB, D), lambda b, j: (b * nbl + jnp.maximum(j - 1, 0), 0))
    return pl.pallas_call(
        body, name="ln_mod_bwd", grid=(cfg.Bn, nbt),
        in_specs=[st, st, st,
                  pl.BlockSpec((None, 1, D), lambda b, j: (2 * b + jnp.minimum(j, 1), 0, 0)), la],
        out_specs=[la, pl.BlockSpec((None, 8, D), lambda b, j: (2 * b + jnp.minimum(j, 1), 0, 0))],
        out_shape=[jax.ShapeDtypeStruct((cfg.N, D), F32), jax.ShapeDtypeStruct((2 * cfg.Bn, 8, D), F32)],
        compiler_params=_cp(2),
    )(dh_a, dh_b, xs, scale_tab, dxp)


def _perm_w_in(w_in):
    w_main = jnp.concatenate([w_in[:, 0:3072], w_in[:, 4160:5184], w_in[:, 3136:4160], w_in[:, 5184:10304]], axis=1)
    w_dt = jnp.pad(w_in[:, 3072:3136], ((0, 0), (0, DTW - 2 * NH)))
    return w_main, w_dt


def _unperm_w_in(dw_main, dw_dt):
    return jnp.concatenate([dw_main[:, 0:3072], dw_dt[:, :2 * NH], dw_main[:, 4096:5120],
                            dw_main[:, 3072:4096], dw_main[:, 5120:]], axis=1)


def _to_groups(v, cfg):
    return v[:, :2 * NH].reshape(cfg.NT, 2, NG, HPG).transpose(1, 2, 0, 3)


def _unpack_rest(rest_all):
    out, off = {}, 0
    for n, shp, axis in _BIG[1:]:
        shard_shape = (shp[0] // NDEV, shp[1]) if axis == 0 else (shp[0], shp[1] // NDEV)
        r = math.prod(shard_shape) // 1024
        out[n] = _from_slots(rest_all[:, off:off + r, :].reshape((NDEV,) + shard_shape), axis)
        off += r
    return out


def _local_step(cfg, x, ctx, target, m, mc, W, rest_payload):
    Bn, T, Tc = cfg.Bn, cfg.T, cfg.Tc
    NT, N = cfg.NT, cfg.N
    xs = jnp.concatenate([ctx, x], axis=1).reshape(NT, D)
    mch = [m[:, i * D:(i + 1) * D] for i in range(NMOD)]
    ctx_sh = jnp.broadcast_to(mc[None, :D], (Bn, D))
    ctx_sc = jnp.broadcast_to(mc[None, D:], (Bn, D))
    shift_tab = jnp.stack([ctx_sh, mch[0]], axis=1).reshape(2 * Bn, 1, D)
    scale_tab = jnp.stack([ctx_sc, mch[1]], axis=1).reshape(2 * Bn, 1, D)
    gate1 = mch[2].reshape(Bn, 1, D)
    shift2 = mch[3].reshape(Bn, 1, D)
    scale2 = mch[4].reshape(Bn, 1, D)
    gate2 = mch[5].reshape(Bn, 1, D)

    conv_w = jnp.concatenate([W["ssd_conv_w"], W["lru_conv_w"]], axis=1)
    conv_b = jnp.concatenate([W["ssd_conv_b"], W["lru_conv_b"]], axis=1)
    dt_bias = jnp.pad(W["ssd_dt_bias"].reshape(1, 2 * NH), ((0, 0), (0, DTW - 2 * NH)))
    a_log = jnp.pad(W["ssd_a_log"].reshape(1, 2 * NH), ((0, 0), (0, DTW - 2 * NH)))
    dvec = jnp.repeat(W["ssd_d"].reshape(NH), HD).reshape(1, DI)
    lba = W["lru_ba"].reshape(2, 1, LW)
    lbi = W["lru_bi"].reshape(2, 1, LW)
    llam = W["lru_lambda"].reshape(2, 1, LW)

    h = _ln_mod_fwd(cfg, xs, shift_tab, scale_tab)
    proj, rest_all = _mm(h, W["w_main"], "nn", "mm_proj", tm=512, tn=1024, tk=1024, xch=(rest_payload, True))
    W = dict(W, **_unpack_rest(rest_all))
    dt_raw = _mm(h, W["w_dt"], "nn", "mm_dt", tm=512, tn=DTW, tk=1024)
    dt, cum = _dt_fwd(cfg, dt_raw, dt_bias, a_log)
    dtg = _to_groups(dt, cfg)
    cumg = _to_groups(cum, cfg)
    cumTg = cum[:, :2 * NH].reshape(NT // CH, CH, 2, NG, HPG).transpose(2, 3, 0, 4, 1)
    act = _conv_fwd(cfg, proj, conv_w, conv_b)
    ys, hss, hls = [], [], []
    for rev in (False, True):
        y_d, hs_d = _ssd_fwd(cfg, act, dtg, cumg, cumTg, rev)
        ys.append(y_d)
        hss.append(hs_d)
        hls.append(_lru_fwd(cfg, act, W["lru_wa"], W["lru_wi"], lba, lbi, llam, rev))
    nssd = _post_ssd_fwd(cfg, ys[0], ys[1], act, proj, dvec, W["ssd_norm_w"])
    vlru = _post_lru_fwd(cfg, hls[0], hls[1], proj)
    br_ssd = _mm(nssd, W["w_br_ssd"], "nn", "mm_br_ssd", tm=512, tn=1024, tk=1024)
    br_lru = _mm(vlru, W["w_br_lru"], "nn", "mm_br_lru", tm=512, tn=1024, tk=1024)
    mix = _merge_fwd(cfg, proj, W["b_gate"], br_ssd, br_lru)
    x_mix = _mm(mix, W["w_out"], "nn", "mm_out", tm=512, tn=1024, tk=1024)
    x1, h2 = _resid1_fwd(cfg, xs, x_mix, gate1, shift2, scale2, W["ln1_g"], W["ln1_b"])
    a1 = _mm(h2, W["w_mlp1"], "nn", "mm_mlp1", tm=512, tn=1024, tk=1024)
    actm = _mlp_act_fwd(cfg, a1, W["b_mlp1"])
    mlp = _mm(actm, W["w_mlp2"], "nn", "mm_mlp2", tm=512, tn=1024, tk=1024)
    dmlp, dx1p, ex2, gl2 = _final_fwd_bwd(cfg, x1, mlp, W["b_mlp2"], gate2, W["ln2_g"], W["ln2_b"],
                                          target.reshape(N, D))

    g = {}
    g["ln2_g"], g["ln2_b"], g["b_mlp2"] = gl2[0:1], gl2[1:2], gl2[2:3]
    loss_partial = gl2[3, 0]
    gw = {}
    dact = _mm(dmlp, W["w_mlp2"], "nt", "mm_dact", tm=512, tn=1024, tk=1024)
    gw["w_mlp2"] = _mm(actm, dmlp, "tn", "mm_dw_mlp2", BF16, tm=1024, tn=1024, tk=512)
    da1, accb1 = _mlp_act_bwd(cfg, dact, a1, W["b_mlp1"])
    g["b_mlp1"] = accb1[0:1]
    dh2 = _mm(da1, W["w_mlp1"], "nt", "mm_dh2", tm=512, tn=1024, tk=1024)
    gw["w_mlp1"] = _mm(h2, da1, "tn", "mm_dw_mlp1", BF16, tm=1024, tn=1024, tk=512)
    dx_mix, dxp, ex1, gl1 = _resid1_bwd(cfg, dh2, dx1p, x1, xs, x_mix, gate1, scale2, W["ln1_g"])
    g["ln1_g"], g["ln1_b"] = gl1[0:1], gl1[1:2]
    dmix = _mm(dx_mix, W["w_out"], "nt", "mm_dmix", tm=512, tn=1024, tk=1024)
    gw["w_out"] = _mm(mix, dx_mix, "tn", "mm_dw_out", BF16, tm=1024, tn=1024, tk=512)
    dbs, dbl, dmerge, accg = _merge_bwd(cfg, dmix, proj, W["b_gate"], br_ssd, br_lru)
    g["b_gate"] = accg[0:1]
    dnssd = _mm(dbs, W["w_br_ssd"], "nt", "mm_dnssd", tm=512, tn=1024, tk=1024)
    gw["w_br_ssd"] = _mm(nssd, dbs, "tn", "mm_dw_br_ssd", BF16, tm=1024, tn=1024, tk=512)
    dvlru = _mm(dbl, W["w_br_lru"], "nt", "mm_dvlru", tm=512, tn=1024, tk=1024)
    gw["w_br_lru"] = _mm(vlru, dbl, "tn", "mm_dw_br_lru", BF16, tm=1024, tn=1024, tk=512)
    dy, dz, accs = _post_ssd_bwd(cfg, dnssd, ys[0], ys[1], act, proj, dvec, W["ssd_norm_w"])
    g["ssd_norm_w"] = accs[0:1]
    dD_cols = accs[1:2]
    dyl, dlg = _post_lru_bwd(cfg, dvlru, hls[0], hls[1], proj)

    rest_slots = jnp.concatenate([_to_slots(gw[n], axis).reshape(NDEV, -1, 1024) for n, _, axis in _BIG[1:]], axis=1)
    xres = {}
    dxh, dBs, dCs, dAs, dxxs, dus = [], [], [], [], [], []
    dwas, dwis, lvecs = [], [], []
    for i, rev in enumerate((False, True)):
        if i == 0:
            o, xres["rs_rest"] = _ssd_bwd(cfg, act, dtg, cumg, cumTg, hss[i], dy, rev, xch=(rest_slots, False))
        else:
            o = _ssd_bwd(cfg, act, dtg, cumg, cumTg, hss[i], dy, rev)
        dxh.append(o[0]); dBs.append(o[1]); dCs.append(o[2]); dAs.append(o[3]); dxxs.append(o[4])
        du, dwa, dwi, lv = _lru_bwd(cfg, act, W["lru_wa"], W["lru_wi"], lba, lbi, llam, hls[i], dyl, rev)
        dus.append(du); dwas.append(dwa); dwis.append(dwi); lvecs.append(lv)
    lru_payload = jnp.stack([jnp.stack(dwas), jnp.stack(dwis)]).reshape(-1, 1024)
    g["lru_ba"] = jnp.stack([lvecs[0][0], lvecs[1][0]])
    g["lru_bi"] = jnp.stack([lvecs[0][1], lvecs[1][1]])
    g["lru_lambda"] = jnp.stack([lvecs[0][2], lvecs[1][2]])

    def from_groups(a0, a1):
        v = jnp.stack([a0, a1]).transpose(2, 0, 1, 3).reshape(NT, 2 * NH)
        return jnp.pad(v, ((0, 0), (0, DTW - 2 * NH)))
    ddt_raw, accdt = _dt_bwd(cfg, from_groups(dAs[0], dAs[1]), from_groups(dxxs[0], dxxs[1]),
                             dt_raw, dt, dt_bias, a_log)
    g["ssd_a_log"] = accdt[0, :2 * NH].reshape(2, NH)
    g["ssd_dt_bias"] = accdt[1, :2 * NH].reshape(2, NH)

    (dpx, accx), xres["ag_lru"] = _conv_bwd(cfg, "conv_bwd_x", proj, conv_w, conv_b, [dxh[0], dxh[1]], 0, DI, True,
                                            skip=(dy, dvec), xch=(lru_payload, True))
    dpB, accB = _conv_bwd(cfg, "conv_bwd_b", proj, conv_w, conv_b, [dBs[0], dBs[1]], DI, NG * NS, True)
    dpC, accC = _conv_bwd(cfg, "conv_bwd_c", proj, conv_w, conv_b, [dCs[0], dCs[1]], DI + NG * NS, NG * NS, True)
    dpl, accl = _conv_bwd(cfg, "conv_bwd_lru", proj, conv_w, conv_b, [dus[0], dus[1]], DI + 2 * NG * NS, LW, False)
    accssd = jnp.concatenate([accx, accB, accC], axis=1)
    g["ssd_conv_w"], g["ssd_conv_b"] = accssd[0:4], accssd[4:5]
    g["lru_conv_w"], g["lru_conv_b"] = accl[0:4], accl[4:5]

    def to_stream(v):
        w = v.shape[1]
        return jnp.pad(v.reshape(Bn, T, w), ((0, 0), (Tc, 0), (0, 0))).reshape(NT, w)
    dproj = jnp.concatenate([dpx, dpB, dpC, dpl, to_stream(dz), to_stream(dlg), to_stream(dmerge)], axis=1)
    dw_main = _mm(h, dproj, "tn", "mm_dw_main", BF16, tm=1024, tn=1024, tk=512)
    dw_dt = _mm(h, ddt_raw, "tn", "mm_dw_dt", BF16, tm=1024, tn=DTW, tk=512)
    w_in_slots = _to_slots(_unperm_w_in(dw_main, dw_dt), 1).reshape(NDEV, -1, 1024)
    w_in_slots = jnp.pad(w_in_slots, ((0, 0), (0, W_IN_SLOT_ROWS - w_in_slots.shape[1]), (0, 0)))
    dh_a, xres["rs_w_in"] = _mm(dproj, W["w_main"], "nt", "mm_dh_main", tm=512, tn=1024, tk=1024,
                                xch=(w_in_slots, False))
    dh_b = _mm(ddt_raw, W["w_dt"], "nt", "mm_dh_dt", tm=512, tn=1024, tk=DTW)
    grad_x, acct = _ln_mod_bwd(cfg, dh_a, dh_b, xs, scale_tab, dxp)
    acct = acct.reshape(Bn, 2, 8, D)
    dm = jnp.concatenate([acct[:, 1, 0], acct[:, 1, 1], ex1[:, 2], ex1[:, 0], ex1[:, 1], ex2[:, 0]], axis=1)
    dmc = jnp.concatenate([acct[:, 0, 0], acct[:, 0, 1]], axis=1)
    g["ssd_d_cols"] = dD_cols
    return loss_partial, grad_x.reshape(Bn, T, D), g, dm, dmc, xres


MESH = pl.DeviceIdType.MESH
_HBM = pl.BlockSpec(memory_space=pltpu.HBM)


def _me():
    return 4 * lax.axis_index("x") + 2 * lax.axis_index("y") + lax.axis_index("c")


def _peer(k):
    px = (lax.axis_index("x") + ((k >> 2) & 1)) % 2
    py = (lax.axis_index("y") + ((k >> 1) & 1)) % 2
    pc = (lax.axis_index("c") + (k & 1)) % 2
    return (px, py, pc), 4 * px + 2 * py + pc


def _xchg_copies(x_ref, o_ref, send_sems, recv_sems, loc_sem, gather):
    me = _me()
    src_me = x_ref if gather else x_ref.at[me]
    loc = pltpu.make_async_copy(src_me, o_ref.at[me], loc_sem)
    sends, recvs = [], []
    for k in range(1, NDEV):
        peer, pid = _peer(k)
        sends.append(pltpu.make_async_remote_copy(
            src_ref=x_ref if gather else x_ref.at[pid], dst_ref=o_ref.at[me],
            send_sem=send_sems.at[k - 1], recv_sem=recv_sems.at[k - 1],
            device_id=peer, device_id_type=MESH))
        recvs.append(pltpu.make_async_remote_copy(
            src_ref=src_me, dst_ref=o_ref.at[pid],
            send_sem=send_sems.at[k - 1], recv_sem=recv_sems.at[k - 1],
            device_id=peer, device_id_type=MESH))
    return loc, sends, recvs


def _xchg_start(*refs, gather):
    loc, sends, _ = _xchg_copies(*refs, gather)
    loc.start()
    for cp in sends:
        cp.start()


def _xchg_wait(*refs, gather):
    loc, sends, recvs = _xchg_copies(*refs, gather)
    for cp in recvs:
        cp.wait_recv()
    for cp in sends:
        cp.wait_send()
    loc.wait()


_XCHG_SCRATCH = [pltpu.SemaphoreType.DMA((NDEV - 1,)), pltpu.SemaphoreType.DMA((NDEV - 1,)), pltpu.SemaphoreType.DMA]


def _xchg_out_shape(x, gather):
    return jax.ShapeDtypeStruct((NDEV,) + tuple(x.shape if gather else x.shape[1:]), x.dtype)


def _exchange(x, name, gather):
    def body(x_ref, o_ref, send_sems, recv_sems, loc_sem):
        _xchg_start(x_ref, o_ref, send_sems, recv_sems, loc_sem, gather=gather)
        _xchg_wait(x_ref, o_ref, send_sems, recv_sems, loc_sem, gather=gather)

    return pl.pallas_call(
        body, name=name, out_shape=_xchg_out_shape(x, gather),
        in_specs=[_HBM], out_specs=_HBM, scratch_shapes=_XCHG_SCRATCH,
    )(x)


def _hosted_call(body, xch, *, name, grid, in_specs, out_specs, out_shape, scratch_shapes, compiler_params, args):
    if xch is None:
        return pl.pallas_call(body, name=name, grid=grid, in_specs=in_specs, out_specs=out_specs,
                              out_shape=out_shape, scratch_shapes=scratch_shapes,
                              compiler_params=compiler_params)(*args)
    xv, gather = xch
    n_in, n_out, n_scr = len(in_specs), len(out_specs), len(scratch_shapes)

    def wrapped(*refs):
        ins = refs[:n_in]
        x_ref = refs[n_in]
        outs = refs[n_in + 1:n_in + 1 + n_out]
        o_ref = refs[n_in + 1 + n_out]
        scr = refs[n_in + 2 + n_out:]
        own, sems = scr[:n_scr], scr[n_scr:]
        first = functools.reduce(jnp.logical_and, [pl.program_id(a) == 0 for a in range(len(grid))])
        last = functools.reduce(jnp.logical_and, [pl.program_id(a) == grid[a] - 1 for a in range(len(grid))])

        @pl.when(first)
        def _():
            _xchg_start(x_ref, o_ref, *sems, gather=gather)

        body(*ins, *outs, *own)

        @pl.when(last)
        def _():
            _xchg_wait(x_ref, o_ref, *sems, gather=gather)

    res = pl.pallas_call(
        wrapped, name=name, grid=grid, in_specs=list(in_specs) + [_HBM], out_specs=list(out_specs) + [_HBM],
        out_shape=list(out_shape) + [_xchg_out_shape(xv, gather)],
        scratch_shapes=list(scratch_shapes) + _XCHG_SCRATCH, compiler_params=compiler_params,
    )(*args, xv)
    return list(res[:n_out]), res[n_out]


def _row_tile(R, cap, mult=8):
    best = mult
    t = mult
    while t <= min(R, cap):
        if R % t == 0:
            best = t
        t += mult
    assert R % best == 0, R
    return best


def _sum_slots(x, name):
    _, R, C = x.shape
    tr = _row_tile(R, 512, 16 if x.dtype == BF16 else 8)

    def body(x_ref, o_ref):
        acc = x_ref[0].astype(F32)
        for i in range(1, NDEV):
            acc = acc + x_ref[i].astype(F32)
        o_ref[...] = acc

    return pl.pallas_call(
        body, name=name, grid=(R // tr,),
        in_specs=[pl.BlockSpec((NDEV, tr, C), lambda i: (0, i, 0))],
        out_specs=pl.BlockSpec((tr, C), lambda i: (i, 0)),
        out_shape=jax.ShapeDtypeStruct((R, C), F32),
        compiler_params=_cp(1),
    )(x)


def _adamw(w, g, m, v, name):
    R, C = w.shape
    tr = _row_tile(R, 256)
    c1 = 1.0 / (1.0 - ADAM_B1 ** ADAM_STEP)
    c2 = 1.0 / (1.0 - ADAM_B2 ** ADAM_STEP)

    def body(w_ref, g_ref, m_ref, v_ref, d_ref, nm_ref, nv_ref):
        gv = g_ref[...]
        nm = ADAM_B1 * m_ref[...] + (1.0 - ADAM_B1) * gv
        nv = ADAM_B2 * v_ref[...] + (1.0 - ADAM_B2) * (gv * gv)
        d_ref[...] = -ADAM_LR * ((nm * c1) / (jnp.sqrt(nv * c2) + ADAM_EPS) + ADAM_WD * w_ref[...])
        nm_ref[...] = nm
        nv_ref[...] = nv

    blk = pl.BlockSpec((tr, C), lambda i: (i, 0))
    return pl.pallas_call(
        body, name=name, grid=(R // tr,),
        in_specs=[blk] * 4, out_specs=[blk] * 3,
        out_shape=[jax.ShapeDtypeStruct((R, C), F32)] * 3,
        compiler_params=_cp(1),
    )(w, g, m, v)


def _mod_fwd(c_rows, w_shard, b_shard):
    def body(c_ref, w_ref, b_ref, o_ref):
        s = _silu(c_ref[...]).astype(BF16)
        o_ref[...] = _dot(s, w_ref[...].astype(BF16)) + b_ref[...]

    return pl.pallas_call(
        body, name="mod_fwd",
        out_shape=jax.ShapeDtypeStruct((c_rows.shape[0], w_shard.shape[1]), F32),
        compiler_params=pltpu.CompilerParams(vmem_limit_bytes=VMEM_LIMIT_BYTES),
    )(c_rows, w_shard, b_shard)


def _mod_bwd(c_rows, dm_all, dm_shard, w_shard):
    nrow = c_rows.shape[0]

    def body(c_ref, da_ref, ds_ref, w_ref, gw_ref, gb_ref, cc_ref):
        s = _silu(c_ref[...]).astype(BF16)
        ds = ds_ref[...]
        gw_ref[...] = _dot_tn(s, ds.astype(BF16))
        gb_ref[...] = jnp.sum(da_ref[...], axis=0, keepdims=True)
        rowi = lax.broadcasted_iota(jnp.int32, ds.shape, 0)
        dmc = jnp.sum(jnp.where(rowi % 8 >= 4, ds, 0.0), axis=0, keepdims=True)
        dmc8 = jnp.broadcast_to(dmc, (8, ds.shape[1])).astype(BF16)
        cc_ref[...] = _dot_nt(dmc8, w_ref[...].astype(BF16))

    return pl.pallas_call(
        body, name="mod_bwd",
        out_shape=[jax.ShapeDtypeStruct(w_shard.shape, F32),
                   jax.ShapeDtypeStruct((1, dm_all.shape[1]), F32),
                   jax.ShapeDtypeStruct((8, D), F32)],
        compiler_params=pltpu.CompilerParams(vmem_limit_bytes=VMEM_LIMIT_BYTES),
    )(c_rows, dm_all, dm_shard, w_shard)


def _small_finish(cc_pre, c_ctx, dd_cols):
    def body(cc_ref, c_ref, dd_ref, gc_ref, gd_ref):
        gc_ref[...] = cc_ref[...] * _silu_grad(c_ref[...])
        gd_ref[...] = jnp.sum(dd_ref[...], axis=1, keepdims=True)

    return pl.pallas_call(
        body, name="small_finish",
        out_shape=[jax.ShapeDtypeStruct((1, D), F32), jax.ShapeDtypeStruct((NH, 1), F32)],
    )(cc_pre, c_ctx, dd_cols)


_BIG = (("w_in", (D, 10304), 1), ("w_br_ssd", (DI, D), 0), ("w_br_lru", (LW, D), 0), ("w_out", (D, D), 0),
        ("w_mlp1", (D, MLP), 1), ("w_mlp2", (MLP, D), 0))
_SMALL_SH = (("ssd_conv_w", (4, 4096)), ("lru_conv_w", (4, LW)), ("lru_ba", (2, LW)), ("lru_bi", (2, LW)),
             ("lru_lambda", (2, LW)))
_REPL = (("c_ctx", (D,)), ("b_gate", (2 * D,)), ("ssd_conv_b", (4096,)), ("ssd_dt_bias", (2, NH)),
         ("ssd_a_log", (2, NH)), ("ssd_d", (DI,)), ("ssd_norm_w", (DI,)), ("lru_conv_b", (LW,)),
         ("ln1_g", (D,)), ("ln1_b", (D,)),
         ("b_mlp1", (MLP,)), ("b_mlp2", (D,)), ("ln2_g", (D,)), ("ln2_b", (D,)))

_WEIGHT_NAMES = ('c_ctx', 'w_mod', 'b_mod', 'w_in', 'b_gate', 'ssd_conv_w', 'ssd_conv_b', 'ssd_dt_bias', 'ssd_a_log',
                 'ssd_d', 'ssd_norm_w', 'lru_conv_w', 'lru_conv_b', 'lru_wa', 'lru_ba', 'lru_wi', 'lru_bi',
                 'lru_lambda', 'w_br_ssd', 'w_br_lru', 'w_out', 'ln1_g', 'ln1_b', 'w_mlp1', 'b_mlp1', 'w_mlp2',
                 'b_mlp2', 'ln2_g', 'ln2_b')
_ARG_NAMES = ('x', 'c', 'ctx') + _WEIGHT_NAMES + ('loss_target',) + tuple('m_' + n for n in _WEIGHT_NAMES) + tuple(
    'v_' + n for n in _WEIGHT_NAMES)


def _to_slots(full, axis):
    n = full.shape[axis] // NDEV
    if axis == 0:
        return full.reshape(NDEV, n, full.shape[1])
    return full.reshape(full.shape[0], NDEV, n).transpose(1, 0, 2)


def _from_slots(slots, axis):
    if axis == 0:
        return slots.reshape(NDEV * slots.shape[1], slots.shape[2])
    return slots.transpose(1, 0, 2).reshape(slots.shape[1], NDEV * slots.shape[2])


def _pack_rows(arrs, width=1024, mult=8):
    flat = jnp.concatenate([a.reshape(-1) for a in arrs])
    n = flat.shape[0]
    per = width * mult
    tot = -(-n // per) * per
    return jnp.pad(flat, (0, tot - n)).reshape(tot // width, width)


def _unpack_rows(packed, shapes, lead=()):
    nl = len(lead)
    flat = packed.reshape(tuple(lead) + (-1,))
    out, off = [], 0
    for s in shapes:
        n = math.prod(s)
        out.append(flat[..., off:off + n].reshape(tuple(lead) + tuple(s)))
        off += n
    return out


def kernel(x, c, ctx, c_ctx, w_mod, b_mod, w_in, b_gate, ssd_conv_w, ssd_conv_b, ssd_dt_bias, ssd_a_log, ssd_d, ssd_norm_w, lru_conv_w, lru_conv_b, lru_wa, lru_ba, lru_wi, lru_bi, lru_lambda, w_br_ssd, w_br_lru, w_out, ln1_g, ln1_b, w_mlp1, b_mlp1, w_mlp2, b_mlp2, ln2_g, ln2_b, loss_target, m_c_ctx, m_w_mod, m_b_mod, m_w_in, m_b_gate, m_ssd_conv_w, m_ssd_conv_b, m_ssd_dt_bias, m_ssd_a_log, m_ssd_d, m_ssd_norm_w, m_lru_conv_w, m_lru_conv_b, m_lru_wa, m_lru_ba, m_lru_wi, m_lru_bi, m_lru_lambda, m_w_br_ssd, m_w_br_lru, m_w_out, m_ln1_g, m_ln1_b, m_w_mlp1, m_b_mlp1, m_w_mlp2, m_b_mlp2, m_ln2_g, m_ln2_b, v_c_ctx, v_w_mod, v_b_mod, v_w_in, v_b_gate, v_ssd_conv_w, v_ssd_conv_b, v_ssd_dt_bias, v_ssd_a_log, v_ssd_d, v_ssd_norm_w, v_lru_conv_w, v_lru_conv_b, v_lru_wa, v_lru_ba, v_lru_wi, v_lru_bi, v_lru_lambda, v_w_br_ssd, v_w_br_lru, v_w_out, v_ln1_g, v_ln1_b, v_w_mlp1, v_b_mlp1, v_w_mlp2, v_b_mlp2, v_ln2_g, v_ln2_b):
    A = dict(zip(_ARG_NAMES, (x, c, ctx, c_ctx, w_mod, b_mod, w_in, b_gate, ssd_conv_w, ssd_conv_b, ssd_dt_bias, ssd_a_log, ssd_d, ssd_norm_w, lru_conv_w, lru_conv_b, lru_wa, lru_ba, lru_wi, lru_bi, lru_lambda, w_br_ssd, w_br_lru, w_out, ln1_g, ln1_b, w_mlp1, b_mlp1, w_mlp2, b_mlp2, ln2_g, ln2_b, loss_target, m_c_ctx, m_w_mod, m_b_mod, m_w_in, m_b_gate, m_ssd_conv_w, m_ssd_conv_b, m_ssd_dt_bias, m_ssd_a_log, m_ssd_d, m_ssd_norm_w, m_lru_conv_w, m_lru_conv_b, m_lru_wa, m_lru_ba, m_lru_wi, m_lru_bi, m_lru_lambda, m_w_br_ssd, m_w_br_lru, m_w_out, m_ln1_g, m_ln1_b, m_w_mlp1, m_b_mlp1, m_w_mlp2, m_b_mlp2, m_ln2_g, m_ln2_b, v_c_ctx, v_w_mod, v_b_mod, v_w_in, v_b_gate, v_ssd_conv_w, v_ssd_conv_b, v_ssd_dt_bias, v_ssd_a_log, v_ssd_d, v_ssd_norm_w, v_lru_conv_w, v_lru_conv_b, v_lru_wa, v_lru_ba, v_lru_wi, v_lru_bi, v_lru_lambda, v_w_br_ssd, v_w_br_lru, v_w_out, v_ln1_g, v_ln1_b, v_w_mlp1, v_b_mlp1, v_w_mlp2, v_b_mlp2, v_ln2_g, v_ln2_b)))
    Bn, T, _ = x.shape
    Tc = ctx.shape[1]
    cfg = _Cfg(Bn, T, Tc)
    me = _me()
    L = {n: (A[n] if n == "c_ctx" else A[n][0]) for n in _WEIGHT_NAMES}
    nmod = L["w_mod"].shape[1]

    c_all = _exchange(c, "ag_c", True)
    c_rows = jnp.concatenate([c_all.reshape(NDEV * Bn, D), jnp.broadcast_to(c_ctx[None, :], (8, D))], axis=0)
    b_shard = lax.dynamic_slice(L["b_mod"], (me * nmod,), (nmod,)).reshape(1, nmod)
    m_part = _mod_fwd(c_rows, L["w_mod"], b_shard)
    m_all = _exchange(m_part, "ag_mod", True)
    m_full = m_all.transpose(1, 0, 2).reshape(NDEV * Bn + 8, NMOD * D)
    m_mine = lax.dynamic_slice(m_full, (me * Bn, 0), (Bn, NMOD * D))
    mc = m_full[NDEV * Bn, :2 * D]

    w_in_all = _exchange(L["w_in"].astype(BF16), "ag_w_in", True)
    rest_payload = jnp.concatenate([L[n].astype(BF16).reshape(-1, 1024) for n, _, _ in _BIG[1:]], axis=0)
    small_shapes = [(s[0], s[1] // NDEV) for _, s in _SMALL_SH]
    small_all = _exchange(_pack_rows([L[n] for n, _ in _SMALL_SH], width=512), "ag_w_small", True)
    W = {}
    for (n, shp), piece in zip(_SMALL_SH, _unpack_rows(small_all, small_shapes, lead=(NDEV,))):
        W[n] = piece.transpose(1, 0, 2).reshape(shp)
    W["w_main"], W["w_dt"] = _perm_w_in(_from_slots(w_in_all, 1))
    for n in ("ssd_conv_b", "lru_conv_b", "ssd_norm_w", "b_gate", "ln1_g", "ln1_b", "b_mlp1", "b_mlp2", "ln2_g", "ln2_b"):
        W[n] = L[n].reshape(1, -1)
    for n in ("ssd_dt_bias", "ssd_a_log", "ssd_d", "lru_wa", "lru_wi"):
        W[n] = L[n]

    loss_part, grad_x, g, dm, dmc, xres = _local_step(cfg, x, ctx, loss_target, m_mine, mc, W, rest_payload)
    loss = lax.psum(loss_part, ("x", "y", "c"))

    dmc_pad = jnp.pad(dmc, ((0, 4 - Bn), (0, (NMOD - 2) * D)))
    dm_all = _exchange(jnp.concatenate([jnp.pad(dm, ((0, 4 - Bn), (0, 0))), dmc_pad], axis=0), "ag_dm", True)
    dm_all = dm_all.reshape(NDEV * 8, NMOD * D)
    c_rows_b = jnp.concatenate([jnp.pad(c_all, ((0, 0), (0, 4 - Bn), (0, 0))),
                                jnp.broadcast_to(c_ctx[None, None, :], (NDEV, 4, D))], axis=1).reshape(NDEV * 8, D)
    dm_shard = lax.dynamic_slice(dm_all, (0, me * nmod), (NDEV * 8, nmod))
    g_w_mod, g_b_mod, cc_part = _mod_bwd(c_rows_b, dm_all, dm_shard, L["w_mod"])
    g["c_ctx"] = cc_part[0]

    g["ssd_d"] = g.pop("ssd_d_cols")
    small_names = [n for n, _ in _REPL] + [n for n, _ in _SMALL_SH]
    small_full_shapes = [s for _, s in _REPL] + [s for _, s in _SMALL_SH]
    sm_all = _exchange(_pack_rows([g[n] for n in small_names]), "ag_g_small", True)
    sm_sum = _sum_slots(sm_all, "sum_g_small")
    gs = dict(zip(small_names, _unpack_rows(sm_sum, small_full_shapes)))
    gcc, gdd = _small_finish(gs["c_ctx"].reshape(1, D), c_ctx.reshape(1, D), gs["ssd_d"].reshape(NH, HD))
    gs["c_ctx"] = gcc.reshape(D)
    gs["ssd_d"] = gdd.reshape(NH)
    for n, shp in _SMALL_SH:
        ns = shp[1] // NDEV
        gs[n] = lax.dynamic_slice(gs[n], (0, me * ns), (shp[0], ns))
    gs["b_mod"] = g_b_mod.reshape(NMOD * D)
    lru_sum = _sum_slots(xres["ag_lru"], "sum_g_lru").reshape(2, 2, LB, LBW, LBW)
    gs["lru_wa"], gs["lru_wi"] = lru_sum[0], lru_sum[1]

    gb = {}
    red_a = _sum_slots(xres["rs_w_in"], "sum_w_in")
    gb["w_in"] = red_a[:D * (10304 // NDEV) // 1024].reshape(D, 10304 // NDEV)
    red_b = _sum_slots(xres["rs_rest"], "sum_w_rest")
    off = 0
    for n, shp, axis in _BIG[1:]:
        shard_shape = (shp[0] // NDEV, shp[1]) if axis == 0 else (shp[0], shp[1] // NDEV)
        r = math.prod(shard_shape) // 1024
        gb[n] = red_b[off:off + r].reshape(shard_shape)
        off += r
    gb["w_mod"] = g_w_mod

    grads, deltas, new_m, new_v = {}, {}, {}, {}
    big_names = ["w_mod"] + [n for n, _, _ in _BIG]
    for n in big_names:
        shp = L[n].shape
        pk = lambda a: a.reshape(-1, 1024)
        d_, nm_, nv_ = _adamw(pk(L[n]), pk(gb[n]), pk(A["m_" + n][0]), pk(A["v_" + n][0]), "adamw_" + n)
        grads[n], deltas[n], new_m[n], new_v[n] = gb[n], d_.reshape(shp), nm_.reshape(shp), nv_.reshape(shp)
    sm_names = [n for n in _WEIGHT_NAMES if n not in big_names]
    sm_shapes = [L[n].shape for n in sm_names]
    loc = lambda pre: _pack_rows([(A[pre + n] if n == "c_ctx" else A[pre + n][0]) for n in sm_names])
    d_, nm_, nv_ = _adamw(loc(""), _pack_rows([gs[n].reshape(L[n].shape) for n in sm_names]), loc("m_"), loc("v_"),
                          "adamw_small")
    for n, dv, mv, vv in zip(sm_names, _unpack_rows(d_, sm_shapes), _unpack_rows(nm_, sm_shapes),
                             _unpack_rows(nv_, sm_shapes)):
        grads[n], deltas[n], new_m[n], new_v[n] = gs[n].reshape(L[n].shape), dv, mv, vv

    def out(dct):
        return [dct[n] if n == "c_ctx" else dct[n][None] for n in _WEIGHT_NAMES]
    return (loss, grad_x, *out(grads), *out(deltas), *out(new_m), *out(new_v))
```

```python
import functools
import math

import jax
import jax.numpy as jnp
from jax import lax
from jax.experimental import pallas as pl
from jax.experimental.pallas import tpu as pltpu

F32 = jnp.float32
BF16 = jnp.bfloat16

D = 1024
GRID_W = 64
DI = 2048
NH = 32
HD = 64
NG = 8
HPG = 4
NS = 128
CH = 128
LW = 1024
LB = 8
LBW = 128
LRU_C = 8.0
MLP = 4096
NMOD = 6
ALPHA = 2.0 ** 0.25
LN_EPS = 1e-6
RMS_EPS = 1e-5
PM = 10240
DTW = 128
CONVW = 5120
NDEV = 8

ADAM_LR = 0.001
ADAM_B1 = 0.9
ADAM_B2 = 0.999
ADAM_EPS = 1e-08
ADAM_WD = 0.01
ADAM_STEP = 10

VMEM_LIMIT_BYTES = 56 * 1024 * 1024


def _cp(n_axes):
    return pltpu.CompilerParams(dimension_semantics=("arbitrary",) * n_axes,
                                vmem_limit_bytes=VMEM_LIMIT_BYTES)


def _sigmoid(x):
    return 1.0 / (1.0 + jnp.exp(-x))


def _silu(x):
    return x * _sigmoid(x)


def _silu_grad(x):
    s = _sigmoid(x)
    return s * (1.0 + x * (1.0 - s))


def _log1p_pos(e):
    return jnp.where(e < 1e-2, e * (1.0 - e * (0.5 - e * (1.0 / 3.0))), jnp.log(1.0 + e))


def _softplus(x):
    return jnp.maximum(x, 0.0) + _log1p_pos(jnp.exp(-jnp.abs(x)))


def _neg_expm1(x):
    series = -x * (1.0 + x * (0.5 + x * (1.0 / 6.0 + x * (1.0 / 24.0))))
    return jnp.where(x > -1e-2, series, 1.0 - jnp.exp(x))


_GELU_K = math.sqrt(2.0 / math.pi)


def _gelu(x):
    t = jnp.tanh(_GELU_K * (x + 0.044715 * x * x * x))
    return 0.5 * x * (1.0 + t)


def _gelu_grad(x):
    t = jnp.tanh(_GELU_K * (x + 0.044715 * x * x * x))
    dt = (1.0 - t * t) * _GELU_K * (1.0 + 3.0 * 0.044715 * x * x)
    return 0.5 * (1.0 + t) + 0.5 * x * dt


def _ln(x):
    mu = jnp.mean(x, axis=-1, keepdims=True)
    xc = x - mu
    var = jnp.mean(xc * xc, axis=-1, keepdims=True)
    rs = lax.rsqrt(var + LN_EPS)
    return xc * rs, rs


def _ln_bwd(dy, xhat, rs):
    m1 = jnp.mean(dy, axis=-1, keepdims=True)
    m2 = jnp.mean(dy * xhat, axis=-1, keepdims=True)
    return rs * (dy - m1 - xhat * m2)


def _dot(a, b):
    return lax.dot_general(a, b, (((1,), (0,)), ((), ())), preferred_element_type=F32)


def _dot_nt(a, b):
    return lax.dot_general(a, b, (((1,), (1,)), ((), ())), preferred_element_type=F32)


def _dot_tn(a, b):
    return lax.dot_general(a, b, (((0,), (0,)), ((), ())), preferred_element_type=F32)


def _split3(a):
    a0 = a.astype(BF16)
    r = a - a0.astype(F32)
    a1 = r.astype(BF16)
    a2 = (r - a1.astype(F32)).astype(BF16)
    return a0, a1, a2


def _dot_exact_l(m_bf, a):
    a0, a1, a2 = _split3(a)
    return _dot(m_bf, a0) + _dot(m_bf, a1) + _dot(m_bf, a2)


def _dot_exact_r(a, m_bf):
    a0, a1, a2 = _split3(a)
    return _dot(a0, m_bf) + _dot(a1, m_bf) + _dot(a2, m_bf)


def _tri(n, upper):
    ii = lax.broadcasted_iota(jnp.int32, (n, n), 0)
    kk = lax.broadcasted_iota(jnp.int32, (n, n), 1)
    m = (kk >= ii) if upper else (kk <= ii)
    return jnp.where(m, 1.0, 0.0).astype(BF16)


def _fit(n, t):
    t = min(t, n)
    while n % t:
        t //= 2
    return t


def _mm(a, b, mode, name, out_dtype=F32, tm=512, tn=512, tk=512, xch=None):
    if mode == "nn":
        M, K = a.shape
        N = b.shape[1]
    elif mode == "nt":
        M, K = a.shape
        N = b.shape[0]
    else:
        K, M = a.shape
        N = b.shape[1]
    tm, tn, tk = _fit(M, tm), _fit(N, tn), _fit(K, tk)
    assert M % tm == 0 and N % tn == 0 and K % tk == 0, (name, M, N, K, tm, tn, tk)
    nk = K // tk
    if mode == "tn":
        a_spec = pl.BlockSpec((tk, tm), lambda i, j, k: (k, i))
    else:
        a_spec = pl.BlockSpec((tm, tk), lambda i, j, k: (i, k))
    if mode == "nt":
        b_spec = pl.BlockSpec((tn, tk), lambda i, j, k: (j, k))
    else:
        b_spec = pl.BlockSpec((tk, tn), lambda i, j, k: (k, j))
    dn = {"nn": (((1,), (0,)), ((), ())), "nt": (((1,), (1,)), ((), ())), "tn": (((0,), (0,)), ((), ()))}[mode]

    def body(a_ref, b_ref, o_ref, acc_ref):
        k = pl.program_id(2)

        @pl.when(k == 0)
        def _():
            acc_ref[...] = jnp.zeros_like(acc_ref)

        acc_ref[...] += lax.dot_general(a_ref[...].astype(BF16), b_ref[...].astype(BF16), dn,
                                        preferred_element_type=F32)

        @pl.when(k == nk - 1)
        def _():
            o_ref[...] = acc_ref[...].astype(o_ref.dtype)

    res = _hosted_call(
        body, xch, name=name, grid=(M // tm, N // tn, nk),
        in_specs=[a_spec, b_spec],
        out_specs=[pl.BlockSpec((tm, tn), lambda i, j, k: (i, j))],
        out_shape=[jax.ShapeDtypeStruct((M, N), out_dtype)],
        scratch_shapes=[pltpu.VMEM((tm, tn), F32)],
        compiler_params=_cp(3), args=(a, b))
    if xch is None:
        return res[0]
    return res[0][0], res[1]


class _Cfg:
    def __init__(self, Bn, T, Tc):
        assert T % Tc == 0 and Tc % CH == 0 and Tc % GRID_W == 0
        self.Bn, self.T, self.Tc = Bn, T, Tc
        self.TT = T + Tc
        self.TB = Tc
        self.nbt = self.TT // self.TB
        self.nbl = T // self.TB
        self.NT = Bn * self.TT
        self.N = Bn * T
        self.nct = Tc // CH
        self.nlt = T // CH
        self.nch = self.nct + self.nlt


def _ln_mod_fwd(cfg, xs, shift_tab, scale_tab):
    TB, nbt = cfg.TB, cfg.nbt

    def body(x_ref, sh_ref, sc_ref, o_ref):
        xhat, _ = _ln(x_ref[...])
        o_ref[...] = (xhat * (1.0 + sc_ref[...]) + sh_ref[...]).astype(BF16)

    tab = pl.BlockSpec((None, 1, D), lambda b, j: (2 * b + jnp.minimum(j, 1), 0, 0))
    return pl.pallas_call(
        body, name="ln_mod_fwd", grid=(cfg.Bn, nbt),
        in_specs=[pl.BlockSpec((TB, D), lambda b, j: (b * nbt + j, 0)), tab, tab],
        out_specs=pl.BlockSpec((TB, D), lambda b, j: (b * nbt + j, 0)),
        out_shape=jax.ShapeDtypeStruct((cfg.NT, D), BF16),
        compiler_params=_cp(2),
    )(xs, shift_tab, scale_tab)


GP = 4
NGB = NG // GP
HPB = GP * HPG


def _head_select(d, gb, to_front):
    r = lax.broadcasted_iota(jnp.int32, (DTW, DTW), 0)
    c = lax.broadcasted_iota(jnp.int32, (DTW, DTW), 1)
    src, dst = (r, c) if to_front else (c, r)
    hit = jnp.logical_and(src == d * NH + gb * HPB + dst, dst < HPB)
    return jnp.where(hit, 1.0, 0.0).astype(BF16)


def _dt_fwd(cfg, dt_raw, dt_bias, a_log):
    def body(raw_ref, bias_ref, alog_ref, dt_ref, dtg_ref, cumg_ref, cumT_ref):
        dt = _softplus(raw_ref[...] + bias_ref[...])
        a = -jnp.exp(alog_ref[...])
        dta = dt * a
        col = lax.broadcasted_iota(jnp.int32, (CH, DTW), 1)
        cf = _dot_exact_l(_tri(CH, False), dta)
        cr = _dot_exact_l(_tri(CH, True), dta)
        cum = jnp.where(col < NH, cf, cr)
        dt_ref[...] = dt
        for d in range(2):
            for gb in range(NGB):
                sel = _head_select(d, gb, True)
                dtg_ref[d, gb] = _dot_exact_r(dt, sel)
                cg = _dot_exact_r(cum, sel)
                cumg_ref[d, gb] = cg
                cumT_ref[d, gb] = cg.T

    blk = pl.BlockSpec((CH, DTW), lambda i: (i, 0))
    row = pl.BlockSpec((1, DTW), lambda i: (0, 0))
    gblk = pl.BlockSpec((2, NGB, CH, DTW), lambda i: (0, 0, i, 0))
    return pl.pallas_call(
        body, name="dt_fwd", grid=(cfg.NT // CH,),
        in_specs=[blk, row, row],
        out_specs=[blk, gblk, gblk, pl.BlockSpec((2, NGB, None, DTW, CH), lambda i: (0, 0, i, 0, 0))],
        out_shape=[jax.ShapeDtypeStruct((cfg.NT, DTW), F32),
                   jax.ShapeDtypeStruct((2, NGB, cfg.NT, DTW), F32),
                   jax.ShapeDtypeStruct((2, NGB, cfg.NT, DTW), F32),
                   jax.ShapeDtypeStruct((2, NGB, cfg.NT // CH, DTW, CH), F32)],
        compiler_params=_cp(1),
    )(dt_raw, dt_bias, a_log)


def _dt_bwd(cfg, dAs, dxxs, dt_raw, dt, dt_bias, a_log):
    def body(dAf_ref, dAr_ref, dxf_ref, dxr_ref, raw_ref, dt_ref, bias_ref, alog_ref, o_ref, acc_ref):
        i = pl.program_id(0)

        @pl.when(i == 0)
        def _():
            acc_ref[...] = jnp.zeros_like(acc_ref)

        a = -jnp.exp(alog_ref[...])
        col = lax.broadcasted_iota(jnp.int32, (CH, DTW), 1)
        dA_v = jnp.zeros((CH, DTW), F32)
        dxx_v = jnp.zeros((CH, DTW), F32)
        for d, (ra, rx) in enumerate(((dAf_ref, dxf_ref), (dAr_ref, dxr_ref))):
            for gb in range(NGB):
                sel = _head_select(d, gb, False)
                dA_v = dA_v + _dot_exact_r(ra[gb], sel)
                dxx_v = dxx_v + _dot_exact_r(rx[gb], sel)
        ddta = jnp.where(col < NH, _dot_exact_l(_tri(CH, True), dA_v), _dot_exact_l(_tri(CH, False), dA_v))
        dtv = dt_ref[...]
        ddt = ddta * a + dxx_v
        draw = ddt * _sigmoid(raw_ref[...] + bias_ref[...])
        draw = jnp.where(col < 2 * NH, draw, 0.0)
        o_ref[...] = draw
        da = jnp.sum(ddta * dtv, axis=0, keepdims=True) * a
        da = jnp.where(col[:1] < 2 * NH, da, 0.0)
        acc_ref[0:1, :] += da
        acc_ref[1:2, :] += jnp.sum(draw, axis=0, keepdims=True)

    blk = pl.BlockSpec((CH, DTW), lambda i: (i, 0))
    row = pl.BlockSpec((1, DTW), lambda i: (0, 0))
    gblk = pl.BlockSpec((NGB, CH, DTW), lambda i: (0, i, 0))
    return pl.pallas_call(
        body, name="dt_bwd", grid=(cfg.NT // CH,),
        in_specs=[gblk, gblk, gblk, gblk, blk, blk, row, row],
        out_specs=[blk, pl.BlockSpec((8, DTW), lambda i: (0, 0))],
        out_shape=[jax.ShapeDtypeStruct((cfg.NT, DTW), F32), jax.ShapeDtypeStruct((8, DTW), F32)],
        compiler_params=_cp(1),
    )(dAs[0], dAs[1], dxxs[0], dxxs[1], dt_raw, dt, dt_bias, a_log)


def _conv_shift(u, s, pos, R):
    n = u.shape[0]
    rolled = pltpu.roll(u, s % n, 0)
    ok = jnp.logical_and(pos - s >= 0, pos - s < R)
    return jnp.where(ok, rolled, 0.0)


_TAPS = (2, 1, 0, -1)


def _conv_fwd(cfg, proj, conv_w, conv_b):
    TB, nbt = cfg.TB, cfg.nbt
    CB = CONVW // 2
    n_act = DI + 2 * NG * NS

    def body(u_ref, w_ref, b_ref, o_ref):
        i = pl.program_id(0)
        j = pl.program_id(1)
        R = jnp.where(i % nbt == 0, cfg.Tc, GRID_W)
        t = lax.broadcasted_iota(jnp.int32, (TB, CB), 0)
        pos = jnp.bitwise_and(t, R - 1)
        u = u_ref[...]
        pre = b_ref[...] + w_ref[2:3, :] * u
        for k in (0, 1, 3):
            pre = pre + w_ref[k:k + 1, :] * _conv_shift(u, _TAPS[k], pos, R)
        col = j * CB + lax.broadcasted_iota(jnp.int32, (1, CB), 1)
        o_ref[...] = jnp.where(col < n_act, _silu(pre), pre)

    return pl.pallas_call(
        body, name="conv_fwd", grid=(cfg.NT // TB, CONVW // CB),
        in_specs=[pl.BlockSpec((TB, CB), lambda i, j: (i, j)),
                  pl.BlockSpec((4, CB), lambda i, j: (0, j)),
                  pl.BlockSpec((1, CB), lambda i, j: (0, j))],
        out_specs=pl.BlockSpec((TB, CB), lambda i, j: (i, j)),
        out_shape=jax.ShapeDtypeStruct((cfg.NT, CONVW), F32),
        compiler_params=_cp(2),
    )(proj, conv_w, conv_b)


_ANY = pl.BlockSpec(memory_space=pl.ANY)


def _conv_bwd(cfg, name, dproj, proj, conv_w, conv_b, addends, col0, width, act, skip=None, xch=None):
    TB, nbt, nbl = cfg.TB, cfg.nbt, cfg.nbl
    CB = 1024
    SUB = 256
    c0 = col0 // CB
    n_add = len(addends)

    def body(*refs):
        u_ref, w_ref, b_ref = refs[1:4]
        add_refs = refs[4:4 + n_add]
        rest = refs[4 + n_add:]
        if skip is not None:
            dy_ref, dv_ref = rest[:2]
            rest = rest[2:]
        o_ref, acc_ref = rest
        i = pl.program_id(1)

        @pl.when(i == 0)
        def _():
            acc_ref[...] = jnp.zeros_like(acc_ref)

        isctx = (i % nbt) == 0
        R = jnp.where(isctx, cfg.Tc, GRID_W)
        t = lax.broadcasted_iota(jnp.int32, (TB, SUB), 0)
        pos = jnp.bitwise_and(t, R - 1)
        for q in range(CB // SUB):
            sl = slice(q * SUB, (q + 1) * SUB)
            u = u_ref[:, sl]
            us = [_conv_shift(u, _TAPS[k], pos, R) for k in range(4)]
            g = add_refs[0][:, sl]
            for r in add_refs[1:]:
                g = g + r[:, sl]
            if skip is not None:
                g = g + jnp.where(isctx, 0.0, dv_ref[:, sl] * dy_ref[:, sl])
            if act:
                pre = b_ref[:, sl]
                for k in range(4):
                    pre = pre + w_ref[k:k + 1, sl] * us[k]
                g = g * _silu_grad(pre)
            dp = jnp.zeros_like(g)
            for k in range(4):
                acc_ref[k:k + 1, sl] += jnp.sum(g * us[k], axis=0, keepdims=True)
                dp = dp + w_ref[k:k + 1, sl] * _conv_shift(g, -_TAPS[k], pos, R)
            acc_ref[4:5, sl] += jnp.sum(g, axis=0, keepdims=True)
            o_ref[:, sl] = dp.astype(BF16)

    blk = pl.BlockSpec((TB, CB), lambda j, i: (i, j))
    in_specs = [_ANY,
                pl.BlockSpec((TB, CB), lambda j, i: (i, c0 + j)),
                pl.BlockSpec((4, CB), lambda j, i: (0, c0 + j)),
                pl.BlockSpec((1, CB), lambda j, i: (0, c0 + j))] + [blk] * n_add
    args = [dproj, proj, conv_w, conv_b] + list(addends)
    if skip is not None:
        def lat(j, i):
            b = i // nbt
            return (b * nbl + jnp.maximum(i % nbt - 1, 0), j)
        in_specs += [pl.BlockSpec((TB, CB), lat), pl.BlockSpec((1, CB), lambda j, i: (0, j))]
        args += list(skip)
    return _hosted_call(
        body, xch, name=name, grid=(width // CB, cfg.NT // TB),
        in_specs=in_specs,
        out_specs=[pl.BlockSpec((TB, CB), lambda j, i: (i, c0 + j)), pl.BlockSpec((8, CB), lambda j, i: (0, j))],
        out_shape=[jax.ShapeDtypeStruct((cfg.NT, PM), BF16), jax.ShapeDtypeStruct((8, width), F32)],
        scratch_shapes=[], compiler_params=_cp(2), args=args, aliases={0: 0})


def _chunk_of_step(cfg, rev):
    nct, nlt = cfg.nct, cfg.nlt
    if not rev:
        return lambda s: s
    return lambda s: jnp.where(s < nct, nct - 1 - s, 2 * nct + nlt - 1 - s)


def _expand4(v, band, base):
    out = v[:, base + 3:base + 4]
    for h in (2, 1, 0):
        out = jnp.where(band == h, v[:, base + h:base + h + 1], out)
    return out


def _ssd_step_tiles(dt_ref, cum_ref, cumT_ref, rev):
    cum_t = cum_ref[...]
    last = 0 if rev else CH - 1
    llast = cum_t[last:last + 1, :]
    return (dt_ref[...], cum_t, cumT_ref[...], jnp.exp(llast - cum_t), jnp.exp(cum_t), jnp.exp(llast), last)


def _ssd_common(gi, x_ref, b_ref, c_ref, tiles, rev):
    dt_t, cum_t, cumT_t, ediff_t, ecum_t, elast, last = tiles
    base = gi * HPG
    xh = x_ref[:, gi * HPG * HD:(gi + 1) * HPG * HD]
    Bm = b_ref[:, gi * NS:(gi + 1) * NS].astype(BF16)
    Cm = c_ref[:, gi * NS:(gi + 1) * NS].astype(BF16)
    band = lax.broadcasted_iota(jnp.int32, (CH, HPG * HD), 1) // HD
    e_exp = _expand4(ediff_t, band, base)
    ecum = _expand4(ecum_t, band, base)
    dt_exp = _expand4(dt_t, band, base)
    X = xh * dt_exp
    G = _dot_nt(Cm, Bm)
    ii = lax.broadcasted_iota(jnp.int32, (CH, CH), 0)
    jj = lax.broadcasted_iota(jnp.int32, (CH, CH), 1)
    mask = (jj >= ii) if rev else (jj <= ii)
    decs = []
    for h in range(HPG):
        seg = cum_t[:, base + h:base + h + 1] - cumT_t[base + h:base + h + 1, :]
        decs.append(jnp.exp(jnp.where(mask, seg, -1e30)))
    rb = lax.broadcasted_iota(jnp.int32, (HPG * HD, NS), 0) // HD
    dec_rows = elast[:, base + 3:base + 4]
    for h in (2, 1, 0):
        dec_rows = jnp.where(rb == h, elast[:, base + h:base + h + 1], dec_rows)
    return xh, Bm, Cm, band, e_exp, ecum, dt_exp, X, G, decs, elast, dec_rows, last


def _ssd_specs(cfg, rev):
    nch = cfg.nch
    cmap = _chunk_of_step(cfg, rev)
    d = 1 if rev else 0

    def make(stepmap):
        def row(b, g, sp):
            return b * nch + cmap(stepmap(sp))
        bo, co = DI // (GP * NS), (DI + NG * NS) // (GP * NS)
        return [
            pl.BlockSpec((CH, GP * HPG * HD), lambda b, g, sp: (row(b, g, sp), g)),
            pl.BlockSpec((CH, GP * NS), lambda b, g, sp: (row(b, g, sp), bo + g)),
            pl.BlockSpec((CH, GP * NS), lambda b, g, sp: (row(b, g, sp), co + g)),
            pl.BlockSpec((None, None, CH, DTW), lambda b, g, sp: (d, g, row(b, g, sp), 0)),
            pl.BlockSpec((None, None, CH, DTW), lambda b, g, sp: (d, g, row(b, g, sp), 0)),
            pl.BlockSpec((None, None, None, DTW, CH), lambda b, g, sp: (d, g, row(b, g, sp), 0, 0)),
        ], row
    return make


def _ssd_fwd(cfg, act, dtg, cumg, cumTg, rev):
    nch = cfg.nch
    in_specs, row = _ssd_specs(cfg, rev)(lambda sp: sp)

    def body(x_ref, b_ref, c_ref, dt_ref, cum_ref, cumT_ref, y_ref, hs_ref, h_scr):
        s = pl.program_id(2)

        @pl.when(s == 0)
        def _():
            h_scr[...] = jnp.zeros_like(h_scr)

        tiles = _ssd_step_tiles(dt_ref, cum_ref, cumT_ref, rev)
        for gi in range(GP):
            xh, Bm, Cm, band, e_exp, ecum, dt_exp, X, G, decs, elast, dec_rows, last = _ssd_common(
                gi, x_ref, b_ref, c_ref, tiles, rev)
            H = h_scr[gi]
            Y = ecum * _dot_nt(Cm, H.astype(BF16))
            for h in range(HPG):
                Mh = (G * decs[h]).astype(BF16)
                Y = Y + _dot(Mh, jnp.where(band == h, X, 0.0).astype(BF16))
            y_ref[:, gi * HPG * HD:(gi + 1) * HPG * HD] = Y
            hs_ref[gi] = H
            S = _dot_tn((e_exp * X).astype(BF16), Bm)
            h_scr[gi] = dec_rows * H + S

    return pl.pallas_call(
        body, name="ssd_fwd_rev" if rev else "ssd_fwd", grid=(cfg.Bn, NG // GP, nch),
        in_specs=in_specs,
        out_specs=[pl.BlockSpec((CH, GP * HPG * HD), lambda b, g, s: (row(b, g, s), g)),
                   pl.BlockSpec((None, GP, None, HPG * HD, NS), lambda b, g, s: (b, g, s, 0, 0))],
        out_shape=[jax.ShapeDtypeStruct((cfg.NT, DI), F32),
                   jax.ShapeDtypeStruct((cfg.Bn, NG, nch, HPG * HD, NS), F32)],
        scratch_shapes=[pltpu.VMEM((GP, HPG * HD, NS), F32)],
        compiler_params=_cp(3),
    )(act, act, act, dtg, cumg, cumTg)


def _ssd_bwd(cfg, act, dtg, cumg, cumTg, hs, dy, rev, xch=None):
    nch, nct, nlt = cfg.nch, cfg.nct, cfg.nlt
    cmap = _chunk_of_step(cfg, rev)
    in_specs, row = _ssd_specs(cfg, rev)(lambda sp: nch - 1 - sp)

    def lat_row(b, g, sp):
        c = cmap(nch - 1 - sp)
        return b * nlt + jnp.maximum(c - nct, 0)

    def body(x_ref, b_ref, c_ref, dt_ref, cum_ref, cumT_ref, dy_ref, hs_ref,
             dxh_ref, dB_ref, dC_ref, dA_ref, dxx_ref, dh_scr):
        sp = pl.program_id(2)

        @pl.when(sp == 0)
        def _():
            dh_scr[...] = jnp.zeros_like(dh_scr)

        c = cmap(nch - 1 - sp)
        tiles = _ssd_step_tiles(dt_ref, cum_ref, cumT_ref, rev)
        dA_t = jnp.zeros((CH, DTW), F32)
        dxx_t = jnp.zeros((CH, DTW), F32)
        for gi in range(GP):
            dA_g, dxx_g = group_bwd(gi, c < nct, tiles, x_ref, b_ref, c_ref, dy_ref, hs_ref,
                                    dxh_ref, dB_ref, dC_ref, dh_scr)
            dA_t = dA_t + dA_g
            dxx_t = dxx_t + dxx_g
        dA_ref[...] = dA_t
        dxx_ref[...] = dxx_t

    def group_bwd(gi, isctx, tiles, x_ref, b_ref, c_ref, dy_ref, hs_ref, dxh_ref, dB_ref, dC_ref, dh_scr):
        xsl = slice(gi * HPG * HD, (gi + 1) * HPG * HD)
        nsl = slice(gi * NS, (gi + 1) * NS)
        base = gi * HPG
        xh, Bm, Cm, band, e_exp, ecum, dt_exp, X, G, decs, elast, dec_rows, last = _ssd_common(
            gi, x_ref, b_ref, c_ref, tiles, rev)
        dY = jnp.where(isctx, 0.0, dy_ref[:, xsl])
        Xb = X.astype(BF16)
        H = hs_ref[gi]
        Hb = H.astype(BF16)
        dHn = dh_scr[gi]
        dHnb = dHn.astype(BF16)
        dYs = ecum * dY
        dYsb = dYs.astype(BF16)
        BdH = _dot_nt(Bm, dHnb)
        Ys = ecum * _dot_nt(Cm, Hb)
        dX = e_exp * BdH
        dG = jnp.zeros((CH, CH), F32)
        lanei = lax.broadcasted_iota(jnp.int32, (CH, DTW), 1)
        dA = jnp.zeros((CH, DTW), F32)
        for h in range(HPG):
            M = G * decs[h]
            dYh = jnp.where(band == h, dY, 0.0).astype(BF16)
            dM = _dot_nt(dYh, Xb)
            W = dM * M
            dG = dG + dM * decs[h]
            dX = dX + _dot_tn(M.astype(BF16), dYh)
            colv = jnp.sum(W - W.T, axis=1, keepdims=True)
            dA = dA + jnp.where(lanei == base + h, colv, 0.0)
        dGb = dG.astype(BF16)
        eX = e_exp * X
        dC_ref[:, nsl] = _dot(dGb, Bm) + _dot(dYsb, Hb)
        dB_ref[:, nsl] = _dot_tn(dGb, Cm) + _dot(eX.astype(BF16), dHnb)
        dh_scr[gi] = dec_rows * dHn + _dot_tn(dYsb, Cm)
        pb = lax.broadcasted_iota(jnp.int32, (HPG * HD, NS), 0) // HD
        pl_ = lax.broadcasted_iota(jnp.int32, (HPG * HD, NS), 1)
        E = jnp.where(pb + base == pl_, 1.0, 0.0).astype(BF16)
        t2 = _dot_exact_r(dY * Ys, E)
        q = _dot_exact_r(eX * BdH, E)
        r = jnp.sum(dHn * H, axis=1, keepdims=True)
        lane1 = lax.broadcasted_iota(jnp.int32, (1, DTW), 1)
        hdot = jnp.zeros((1, DTW), F32)
        for h in range(HPG):
            hv = jnp.sum(r[h * HD:(h + 1) * HD, :], axis=0, keepdims=True)
            hdot = hdot + jnp.where(lane1 == base + h, hv, 0.0)
        dllast = jnp.sum(q, axis=0, keepdims=True) + elast * hdot
        rowi = lax.broadcasted_iota(jnp.int32, (CH, DTW), 0)
        dxh_ref[:, xsl] = dX * dt_exp
        return dA + t2 - q + jnp.where(rowi == last, dllast, 0.0), _dot_exact_r(dX * xh, E)

    small = pl.BlockSpec((None, CH, DTW), lambda b, g, sp: (g, row(b, g, sp), 0))
    return _hosted_call(
        body, xch, name="ssd_bwd_rev" if rev else "ssd_bwd", grid=(cfg.Bn, NG // GP, nch),
        in_specs=in_specs + [
            pl.BlockSpec((CH, GP * HPG * HD), lambda b, g, sp: (lat_row(b, g, sp), g)),
            pl.BlockSpec((None, GP, None, HPG * HD, NS), lambda b, g, sp: (b, g, nch - 1 - sp, 0, 0))],
        out_specs=[pl.BlockSpec((CH, GP * HPG * HD), lambda b, g, sp: (row(b, g, sp), g)),
                   pl.BlockSpec((CH, GP * NS), lambda b, g, sp: (row(b, g, sp), g)),
                   pl.BlockSpec((CH, GP * NS), lambda b, g, sp: (row(b, g, sp), g)),
                   small, small],
        out_shape=[jax.ShapeDtypeStruct((cfg.NT, DI), F32),
                   jax.ShapeDtypeStruct((cfg.NT, NG * NS), F32),
                   jax.ShapeDtypeStruct((cfg.NT, NG * NS), F32),
                   jax.ShapeDtypeStruct((NGB, cfg.NT, DTW), F32),
                   jax.ShapeDtypeStruct((NGB, cfg.NT, DTW), F32)],
        scratch_shapes=[pltpu.VMEM((GP, HPG * HD, NS), F32)],
        compiler_params=_cp(3), args=(act, act, act, dtg, cumg, cumTg, dy, hs))


def _shift_rows(v, s, fill, toward_later, rowi):
    n = v.shape[0]
    if toward_later:
        return jnp.where(rowi >= s, pltpu.roll(v, s, 0), fill)
    return jnp.where(rowi < n - s, pltpu.roll(v, n - s, 0), fill)


def _lru_gates(u, wa_ref, wi_ref, ba_ref, bi_ref, lam_ref):
    rs, is_ = [], []
    for k in range(LB):
        uk = u[:, k * LBW:(k + 1) * LBW].astype(BF16)
        rs.append(_dot(uk, wa_ref[k].astype(BF16)))
        is_.append(_dot(uk, wi_ref[k].astype(BF16)))
    r = _sigmoid(jnp.concatenate(rs, axis=1) + ba_ref[...])
    ig = _sigmoid(jnp.concatenate(is_, axis=1) + bi_ref[...])
    sp = _softplus(-lam_ref[...])
    la = -LRU_C * r * sp
    a = jnp.exp(la)
    g = jnp.sqrt(_neg_expm1(2.0 * la))
    return r, ig, sp, la, a, g


def _lru_w_specs(d):
    return [pl.BlockSpec((None, LB, LBW, LBW), lambda b, s: (d, 0, 0, 0)),
            pl.BlockSpec((None, LB, LBW, LBW), lambda b, s: (d, 0, 0, 0)),
            pl.BlockSpec((None, 1, LW), lambda b, s: (d, 0, 0)),
            pl.BlockSpec((None, 1, LW), lambda b, s: (d, 0, 0)),
            pl.BlockSpec((None, 1, LW), lambda b, s: (d, 0, 0))]


def _lru_fwd(cfg, act, wa, wi, ba, bi, lam, rev):
    nch = cfg.nch
    cmap = _chunk_of_step(cfg, rev)
    d = 1 if rev else 0
    ucol = (DI + 2 * NG * NS) // LW

    def body(u_ref, wa_ref, wi_ref, ba_ref, bi_ref, lam_ref, h_ref, c_scr):
        s = pl.program_id(1)

        @pl.when(s == 0)
        def _():
            c_scr[...] = jnp.zeros_like(c_scr)

        u = u_ref[...]
        r, ig, sp, la, a, g = _lru_gates(u, wa_ref, wi_ref, ba_ref, bi_ref, lam_ref)
        bv = g * ig * u
        rowi = lax.broadcasted_iota(jnp.int32, (CH, LW), 0)
        av = a
        sh = 1
        while sh < CH:
            a_p = _shift_rows(av, sh, 1.0, not rev, rowi)
            b_p = _shift_rows(bv, sh, 0.0, not rev, rowi)
            bv = av * b_p + bv
            av = av * a_p
            sh *= 2
        h = bv + av * c_scr[0:1, :]
        h_ref[...] = h
        lastr = 0 if rev else CH - 1
        c_scr[0:1, :] = h[lastr:lastr + 1, :]

    return pl.pallas_call(
        body, name="lru_fwd_rev" if rev else "lru_fwd", grid=(cfg.Bn, nch),
        in_specs=[pl.BlockSpec((CH, LW), lambda b, s: (b * nch + cmap(s), ucol))] + _lru_w_specs(d),
        out_specs=pl.BlockSpec((CH, LW), lambda b, s: (b * nch + cmap(s), 0)),
        out_shape=jax.ShapeDtypeStruct((cfg.NT, LW), F32),
        scratch_shapes=[pltpu.VMEM((8, LW), F32)],
        compiler_params=_cp(2),
    )(act, wa, wi, ba, bi, lam)


def _lru_bwd(cfg, act, wa, wi, ba, bi, lam, hd, dyl, rev):
    nch, nct, nlt = cfg.nch, cfg.nct, cfg.nlt
    cmap = _chunk_of_step(cfg, rev)
    d = 1 if rev else 0
    ucol = (DI + 2 * NG * NS) // LW

    def srow(b, sp):
        return b * nch + cmap(nch - 1 - sp)

    def prev_rows(b, sp):
        s = nch - 1 - sp
        cp = cmap(jnp.maximum(s - 1, 0))
        base = (b * nch + cp) * (CH // 8)
        return base + (0 if rev else CH // 8 - 1)

    def lat_row(b, sp):
        c = cmap(nch - 1 - sp)
        return b * nlt + jnp.maximum(c - nct, 0)

    def body(u_ref, wa_ref, wi_ref, ba_ref, bi_ref, lam_ref, h_ref, hp_ref, dy_ref,
             du_ref, dwa_ref, dwi_ref, vec_ref, c_scr):
        b = pl.program_id(0)
        sp_id = pl.program_id(1)
        s = nch - 1 - sp_id

        @pl.when(sp_id == 0)
        def _():
            c_scr[...] = jnp.zeros_like(c_scr)

        @pl.when(jnp.logical_and(b == 0, sp_id == 0))
        def _():
            dwa_ref[...] = jnp.zeros_like(dwa_ref)
            dwi_ref[...] = jnp.zeros_like(dwi_ref)
            vec_ref[...] = jnp.zeros_like(vec_ref)

        c = cmap(s)
        u = u_ref[...]
        r, ig, spl, la, a, g = _lru_gates(u, wa_ref, wi_ref, ba_ref, bi_ref, lam_ref)
        dh = jnp.where(c < nct, 0.0, dy_ref[...])
        rowi = lax.broadcasted_iota(jnp.int32, (CH, LW), 0)
        av = _shift_rows(a, 1, 1.0, rev, rowi)
        bv = dh
        sh = 1
        while sh < CH:
            a_n = _shift_rows(av, sh, 1.0, rev, rowi)
            b_n = _shift_rows(bv, sh, 0.0, rev, rowi)
            bv = av * b_n + bv
            av = av * a_n
            sh *= 2
        lamv = bv + av * c_scr[0:1, :]
        first = CH - 1 if rev else 0
        c_scr[0:1, :] = (a * lamv)[first:first + 1, :]
        hprow = hp_ref[...][(0 if rev else 7):(1 if rev else 8), :]
        hprow = jnp.where(s > 0, hprow, 0.0)
        h_prev = _shift_rows(h_ref[...], 1, hprow, not rev, rowi)
        da = lamv * h_prev
        db = lamv
        iu = ig * u
        dla = da * a - db * iu * (a * a) / g
        dr = dla * (-LRU_C * spl)
        di = db * g * u
        du = db * g * ig
        drp = dr * r * (1.0 - r)
        dip = di * ig * (1.0 - ig)
        dus = []
        for k in range(LB):
            sl = slice(k * LBW, (k + 1) * LBW)
            drk = drp[:, sl].astype(BF16)
            dik = dip[:, sl].astype(BF16)
            uk = u[:, sl].astype(BF16)
            dus.append(_dot_nt(drk, wa_ref[k].astype(BF16)) + _dot_nt(dik, wi_ref[k].astype(BF16)))
            dwa_ref[k] += _dot_tn(uk, drk)
            dwi_ref[k] += _dot_tn(uk, dik)
        du_ref[...] = du + jnp.concatenate(dus, axis=1)
        vec_ref[0:1, :] += jnp.sum(drp, axis=0, keepdims=True)
        vec_ref[1:2, :] += jnp.sum(dip, axis=0, keepdims=True)
        dsp = jnp.sum(dla * (-LRU_C * r), axis=0, keepdims=True)
        vec_ref[2:3, :] += dsp * (-_sigmoid(-lam_ref[...]))

    return pl.pallas_call(
        body, name="lru_bwd_rev" if rev else "lru_bwd", grid=(cfg.Bn, nch),
        in_specs=[pl.BlockSpec((CH, LW), lambda b, sp: (srow(b, sp), ucol))] + _lru_w_specs(d) + [
            pl.BlockSpec((CH, LW), lambda b, sp: (srow(b, sp), 0)),
            pl.BlockSpec((8, LW), lambda b, sp: (prev_rows(b, sp), 0)),
            pl.BlockSpec((CH, LW), lambda b, sp: (lat_row(b, sp), 0))],
        out_specs=[pl.BlockSpec((CH, LW), lambda b, sp: (srow(b, sp), 0)),
                   pl.BlockSpec((LB, LBW, LBW), lambda b, sp: (0, 0, 0)),
                   pl.BlockSpec((LB, LBW, LBW), lambda b, sp: (0, 0, 0)),
                   pl.BlockSpec((8, LW), lambda b, sp: (0, 0))],
        out_shape=[jax.ShapeDtypeStruct((cfg.NT, LW), F32),
                   jax.ShapeDtypeStruct((LB, LBW, LBW), F32),
                   jax.ShapeDtypeStruct((LB, LBW, LBW), F32),
                   jax.ShapeDtypeStruct((8, LW), F32)],
        scratch_shapes=[pltpu.VMEM((8, LW), F32)],
        compiler_params=_cp(2),
    )(act, wa, wi, ba, bi, lam, hd, hd, dyl)


HB = 1024


def _post_ssd_fwd(cfg, yf, yb, act, proj, dvec, norm_w):
    TB, nbt, nbl = cfg.TB, cfg.nbt, cfg.nbl
    zc = CONVW // HB

    def body(yf_ref, yb_ref, xh_ref, z_ref, dv_ref, w_ref, o_ref):
        y = yf_ref[...] + yb_ref[...] + dv_ref[...] * xh_ref[...]
        u = y * _silu(z_ref[...])
        for gi in range(HB // (DI // NG)):
            sl = slice(gi * 256, (gi + 1) * 256)
            ug = u[:, sl]
            rs = lax.rsqrt(jnp.mean(ug * ug, axis=1, keepdims=True) + RMS_EPS)
            o_ref[:, sl] = (ug * rs * w_ref[:, sl]).astype(BF16)

    def st(b, j, cb):
        return (b * nbt + 1 + j, cb)
    return pl.pallas_call(
        body, name="post_ssd_fwd", grid=(cfg.Bn, nbl, DI // HB),
        in_specs=[pl.BlockSpec((TB, HB), st), pl.BlockSpec((TB, HB), st), pl.BlockSpec((TB, HB), st),
                  pl.BlockSpec((TB, HB), lambda b, j, cb: (b * nbt + 1 + j, zc + cb)),
                  pl.BlockSpec((1, HB), lambda b, j, cb: (0, cb)),
                  pl.BlockSpec((1, HB), lambda b, j, cb: (0, cb))],
        out_specs=pl.BlockSpec((TB, HB), lambda b, j, cb: (b * nbl + j, cb)),
        out_shape=jax.ShapeDtypeStruct((cfg.N, DI), BF16),
        compiler_params=_cp(3),
    )(yf, yb, act, proj, dvec, norm_w)


def _post_ssd_bwd(cfg, dproj, dn, yf, yb, act, proj, dvec, norm_w):
    TB, nbt, nbl = cfg.TB, cfg.nbt, cfg.nbl
    zc = CONVW // HB

    def body(_, dn_ref, yf_ref, yb_ref, xh_ref, z_ref, dv_ref, w_ref, dy_ref, dz_ref, acc_ref):
        b = pl.program_id(1)
        j = pl.program_id(2)

        @pl.when(jnp.logical_and(b == 0, j == 0))
        def _():
            acc_ref[...] = jnp.zeros_like(acc_ref)

        @pl.when(j == 0)
        def _():
            dz_ref[...] = jnp.zeros_like(dz_ref)

        @pl.when(j > 0)
        def _():
            latent(dn_ref, yf_ref, yb_ref, xh_ref, z_ref, dv_ref, w_ref, dy_ref, dz_ref, acc_ref)

    def latent(dn_ref, yf_ref, yb_ref, xh_ref, z_ref, dv_ref, w_ref, dy_ref, dz_ref, acc_ref):
        xh = xh_ref[...]
        z = z_ref[...]
        y = yf_ref[...] + yb_ref[...] + dv_ref[...] * xh
        sz = _silu(z)
        u = y * sz
        dout = dn_ref[...]
        for gi in range(HB // (DI // NG)):
            sl = slice(gi * 256, (gi + 1) * 256)
            ug0 = u[:, sl]
            rs = lax.rsqrt(jnp.mean(ug0 * ug0, axis=1, keepdims=True) + RMS_EPS)
            ug = ug0 * rs
            do = dout[:, sl]
            acc_ref[0:1, sl] += jnp.sum(do * ug, axis=0, keepdims=True)
            dug = do * w_ref[:, sl]
            du = rs * (dug - ug * jnp.mean(dug * ug, axis=1, keepdims=True))
            dy = du * sz[:, sl]
            dy_ref[:, sl] = dy
            dz_ref[:, sl] = (du * y[:, sl] * _silu_grad(z[:, sl])).astype(BF16)
            acc_ref[1:2, sl] += jnp.sum(dy * xh[:, sl], axis=0, keepdims=True)

    def st(cb, b, j):
        return (b * nbt + j, cb)

    def la(cb, b, j):
        return (b * nbl + jnp.maximum(j - 1, 0), cb)
    return pl.pallas_call(
        body, name="post_ssd_bwd", grid=(DI // HB, cfg.Bn, nbt),
        in_specs=[_ANY, pl.BlockSpec((TB, HB), la), pl.BlockSpec((TB, HB), st), pl.BlockSpec((TB, HB), st),
                  pl.BlockSpec((TB, HB), st),
                  pl.BlockSpec((TB, HB), lambda cb, b, j: (b * nbt + j, zc + cb)),
                  pl.BlockSpec((1, HB), lambda cb, b, j: (0, cb)),
                  pl.BlockSpec((1, HB), lambda cb, b, j: (0, cb))],
        out_specs=[pl.BlockSpec((TB, HB), la),
                   pl.BlockSpec((TB, HB), lambda cb, b, j: (b * nbt + j, zc + cb)),
                   pl.BlockSpec((8, HB), lambda cb, b, j: (0, cb))],
        out_shape=[jax.ShapeDtypeStruct((cfg.N, DI), F32), jax.ShapeDtypeStruct((cfg.NT, PM), BF16),
                   jax.ShapeDtypeStruct((8, DI), F32)],
        input_output_aliases={0: 1},
        compiler_params=_cp(3),
    )(dproj, dn, yf, yb, act, proj, dvec, norm_w)


def _post_lru_fwd(cfg, hf, hb, proj):
    TB, nbt, nbl = cfg.TB, cfg.nbt, cfg.nbl
    gc = (CONVW + DI) // HB

    def body(hf_ref, hb_ref, g_ref, o_ref):
        o_ref[...] = ((hf_ref[...] + hb_ref[...]) * _gelu(g_ref[...])).astype(BF16)

    st = pl.BlockSpec((TB, HB), lambda b, j: (b * nbt + 1 + j, 0))
    return pl.pallas_call(
        body, name="post_lru_fwd", grid=(cfg.Bn, nbl),
        in_specs=[st, st, pl.BlockSpec((TB, HB), lambda b, j: (b * nbt + 1 + j, gc))],
        out_specs=pl.BlockSpec((TB, HB), lambda b, j: (b * nbl + j, 0)),
        out_shape=jax.ShapeDtypeStruct((cfg.N, LW), BF16),
        compiler_params=_cp(2),
    )(hf, hb, proj)


def _post_lru_bwd(cfg, dproj, dv, hf, hb, proj):
    TB, nbt, nbl = cfg.TB, cfg.nbt, cfg.nbl
    gc = (CONVW + DI) // HB

    def body(_, dv_ref, hf_ref, hb_ref, g_ref, dy_ref, dg_ref):
        j = pl.program_id(1)

        @pl.when(j == 0)
        def _():
            dg_ref[...] = jnp.zeros_like(dg_ref)

        @pl.when(j > 0)
        def _():
            gt = g_ref[...]
            dvv = dv_ref[...]
            dy_ref[...] = dvv * _gelu(gt)
            dg_ref[...] = (dvv * (hf_ref[...] + hb_ref[...]) * _gelu_grad(gt)).astype(BF16)

    st = pl.BlockSpec((TB, HB), lambda b, j: (b * nbt + j, 0))
    la = pl.BlockSpec((TB, HB), lambda b, j: (b * nbl + jnp.maximum(j - 1, 0), 0))
    gcol = pl.BlockSpec((TB, HB), lambda b, j: (b * nbt + j, gc))
    return pl.pallas_call(
        body, name="post_lru_bwd", grid=(cfg.Bn, nbt),
        in_specs=[_ANY, la, st, st, gcol],
        out_specs=[la, gcol],
        out_shape=[jax.ShapeDtypeStruct((cfg.N, LW), F32), jax.ShapeDtypeStruct((cfg.NT, PM), BF16)],
        input_output_aliases={0: 1},
        compiler_params=_cp(2),
    )(dproj, dv, hf, hb, proj)


def _merge_fwd(cfg, proj, b_gate, br_ssd, br_lru):
    TB, nbt, nbl = cfg.TB, cfg.nbt, cfg.nbl
    mc = (CONVW + DI + LW) // HB

    def body(ms_ref, ml_ref, bg_ref, bs_ref, bl_ref, o_ref):
        gs = _sigmoid(ms_ref[...] + bg_ref[:, :D])
        gl = _sigmoid(ml_ref[...] + bg_ref[:, D:])
        o_ref[...] = (gs * bs_ref[...] + gl * bl_ref[...]).astype(BF16)

    la = pl.BlockSpec((TB, D), lambda b, j: (b * nbl + j, 0))
    return pl.pallas_call(
        body, name="merge_fwd", grid=(cfg.Bn, nbl),
        in_specs=[pl.BlockSpec((TB, HB), lambda b, j: (b * nbt + 1 + j, mc)),
                  pl.BlockSpec((TB, HB), lambda b, j: (b * nbt + 1 + j, mc + 1)),
                  pl.BlockSpec((1, 2 * D), lambda b, j: (0, 0)), la, la],
        out_specs=la,
        out_shape=jax.ShapeDtypeStruct((cfg.N, D), BF16),
        compiler_params=_cp(2),
    )(proj, proj, b_gate, br_ssd, br_lru)


def _merge_bwd(cfg, dmix, proj, b_gate, br_ssd, br_lru):
    TB, nbt, nbl = cfg.TB, cfg.nbt, cfg.nbl
    mc = (CONVW + DI + LW) // HB

    def body(dm_ref, ms_ref, ml_ref, bg_ref, bs_ref, bl_ref, ds_ref, dl_ref, dmg_ref, acc_ref):
        b = pl.program_id(0)
        j = pl.program_id(1)

        @pl.when(jnp.logical_and(b == 0, j == 0))
        def _():
            acc_ref[...] = jnp.zeros_like(acc_ref)

        @pl.when(j == 0)
        def _():
            dmg_ref[...] = jnp.zeros_like(dmg_ref)

        @pl.when(j > 0)
        def _():
            latent(dm_ref, ms_ref, ml_ref, bg_ref, bs_ref, bl_ref, ds_ref, dl_ref, dmg_ref, acc_ref)

    def latent(dm_ref, ms_ref, ml_ref, bg_ref, bs_ref, bl_ref, ds_ref, dl_ref, dmg_ref, acc_ref):
        dm = dm_ref[...]
        gs = _sigmoid(ms_ref[...] + bg_ref[:, :D])
        gl = _sigmoid(ml_ref[...] + bg_ref[:, D:])
        ds_ref[...] = (dm * gs).astype(BF16)
        dl_ref[...] = (dm * gl).astype(BF16)
        dps = dm * bs_ref[...] * gs * (1.0 - gs)
        dpl = dm * bl_ref[...] * gl * (1.0 - gl)
        dmg_ref[:, :D] = dps.astype(BF16)
        dmg_ref[:, D:] = dpl.astype(BF16)
        acc_ref[0:1, :D] += jnp.sum(dps, axis=0, keepdims=True)
        acc_ref[0:1, D:] += jnp.sum(dpl, axis=0, keepdims=True)

    la = pl.BlockSpec((TB, D), lambda b, j: (b * nbl + jnp.maximum(j - 1, 0), 0))
    return pl.pallas_call(
        body, name="merge_bwd", grid=(cfg.Bn, nbt),
        in_specs=[la, pl.BlockSpec((TB, HB), lambda b, j: (b * nbt + j, mc)),
                  pl.BlockSpec((TB, HB), lambda b, j: (b * nbt + j, mc + 1)),
                  pl.BlockSpec((1, 2 * D), lambda b, j: (0, 0)), la, la],
        out_specs=[la, la, pl.BlockSpec((TB, 2 * D), lambda b, j: (b * nbt + j, mc // 2)),
                   pl.BlockSpec((8, 2 * D), lambda b, j: (0, 0))],
        out_shape=[jax.ShapeDtypeStruct((cfg.N, D), BF16), jax.ShapeDtypeStruct((cfg.N, D), BF16),
                   jax.ShapeDtypeStruct((cfg.NT, PM), BF16), jax.ShapeDtypeStruct((8, 2 * D), F32)],
        compiler_params=_cp(2),
    )(dmix, proj, proj, b_gate, br_ssd, br_lru)


def _resid1_fwd(cfg, xs, x_mix, gate1, shift2, scale2, ln1_g, ln1_b):
    TB, nbt, nbl = cfg.TB, cfg.nbt, cfg.nbl

    def body(x_ref, xm_ref, g1_ref, sh_ref, sc_ref, lg_ref, lb_ref, x1_ref, h2_ref):
        r1 = ALPHA * x_ref[...] + g1_ref[...] * xm_ref[...]
        xh, _ = _ln(r1)
        x1 = xh * lg_ref[...] + lb_ref[...]
        x1_ref[...] = x1
        xh2, _ = _ln(x1)
        h2_ref[...] = (xh2 * (1.0 + sc_ref[...]) + sh_ref[...]).astype(BF16)

    la = pl.BlockSpec((TB, D), lambda b, j: (b * nbl + j, 0))
    ex = pl.BlockSpec((None, 1, D), lambda b, j: (b, 0, 0))
    vec = pl.BlockSpec((1, D), lambda b, j: (0, 0))
    return pl.pallas_call(
        body, name="resid1_fwd", grid=(cfg.Bn, nbl),
        in_specs=[pl.BlockSpec((TB, D), lambda b, j: (b * nbt + 1 + j, 0)), la, ex, ex, ex, vec, vec],
        out_specs=[la, la],
        out_shape=[jax.ShapeDtypeStruct((cfg.N, D), F32), jax.ShapeDtypeStruct((cfg.N, D), BF16)],
        compiler_params=_cp(2),
    )(xs, x_mix, gate1, shift2, scale2, ln1_g, ln1_b)


def _resid1_bwd(cfg, dh2, dx1p, x1, xs, x_mix, gate1, scale2, ln1_g):
    TB, nbt, nbl = cfg.TB, cfg.nbt, cfg.nbl

    def body(dh2_ref, dx1p_ref, x1_ref, x_ref, xm_ref, g1_ref, sc_ref, lg_ref,
             dxm_ref, dxp_ref, ex_ref, gl_ref):
        b = pl.program_id(0)
        j = pl.program_id(1)

        @pl.when(j == 0)
        def _():
            ex_ref[...] = jnp.zeros_like(ex_ref)

        @pl.when(jnp.logical_and(b == 0, j == 0))
        def _():
            gl_ref[...] = jnp.zeros_like(gl_ref)

        dh2 = dh2_ref[...]
        xh2, rs2 = _ln(x1_ref[...])
        ex_ref[0:1, :] += jnp.sum(dh2, axis=0, keepdims=True)
        ex_ref[1:2, :] += jnp.sum(dh2 * xh2, axis=0, keepdims=True)
        dx1 = dx1p_ref[...] + _ln_bwd(dh2 * (1.0 + sc_ref[...]), xh2, rs2)
        xm = xm_ref[...]
        g1 = g1_ref[...]
        r1 = ALPHA * x_ref[...] + g1 * xm
        xh1, rs1 = _ln(r1)
        gl_ref[0:1, :] += jnp.sum(dx1 * xh1, axis=0, keepdims=True)
        gl_ref[1:2, :] += jnp.sum(dx1, axis=0, keepdims=True)
        dr1 = _ln_bwd(dx1 * lg_ref[...], xh1, rs1)
        ex_ref[2:3, :] += jnp.sum(dr1 * xm, axis=0, keepdims=True)
        dxm_ref[...] = (dr1 * g1).astype(BF16)
        dxp_ref[...] = ALPHA * dr1

    la = pl.BlockSpec((TB, D), lambda b, j: (b * nbl + j, 0))
    ex = pl.BlockSpec((None, 1, D), lambda b, j: (b, 0, 0))
    vec = pl.BlockSpec((1, D), lambda b, j: (0, 0))
    return pl.pallas_call(
        body, name="resid1_bwd", grid=(cfg.Bn, nbl),
        in_specs=[la, la, la, pl.BlockSpec((TB, D), lambda b, j: (b * nbt + 1 + j, 0)), la, ex, ex, vec],
        out_specs=[la, la, pl.BlockSpec((None, 8, D), lambda b, j: (b, 0, 0)),
                   pl.BlockSpec((8, D), lambda b, j: (0, 0))],
        out_shape=[jax.ShapeDtypeStruct((cfg.N, D), BF16), jax.ShapeDtypeStruct((cfg.N, D), F32),
                   jax.ShapeDtypeStruct((cfg.Bn, 8, D), F32), jax.ShapeDtypeStruct((8, D), F32)],
        compiler_params=_cp(2),
    )(dh2, dx1p, x1, xs, x_mix, gate1, scale2, ln1_g)


def _mlp_act_fwd(cfg, a1, b1):
    TB = cfg.TB

    def body(a_ref, b_ref, o_ref):
        v = jnp.maximum(a_ref[...] + b_ref[...], 0.0)
        o_ref[...] = (v * v).astype(BF16)

    blk = pl.BlockSpec((TB, MLP), lambda i: (i, 0))
    return pl.pallas_call(
        body, name="mlp_act_fwd", grid=(cfg.N // TB,),
        in_specs=[blk, pl.BlockSpec((1, MLP), lambda i: (0, 0))],
        out_specs=blk, out_shape=jax.ShapeDtypeStruct((cfg.N, MLP), BF16),
        compiler_params=_cp(1),
    )(a1, b1)


def _mlp_act_bwd(cfg, dact, a1, b1):
    TB = cfg.TB

    def body(d_ref, a_ref, b_ref, o_ref, acc_ref):
        i = pl.program_id(0)

        @pl.when(i == 0)
        def _():
            acc_ref[...] = jnp.zeros_like(acc_ref)

        da = d_ref[...] * 2.0 * jnp.maximum(a_ref[...] + b_ref[...], 0.0)
        o_ref[...] = da.astype(BF16)
        acc_ref[0:1, :] += jnp.sum(da, axis=0, keepdims=True)

    blk = pl.BlockSpec((TB, MLP), lambda i: (i, 0))
    return pl.pallas_call(
        body, name="mlp_act_bwd", grid=(cfg.N // TB,),
        in_specs=[blk, blk, pl.BlockSpec((1, MLP), lambda i: (0, 0))],
        out_specs=[blk, pl.BlockSpec((8, MLP), lambda i: (0, 0))],
        out_shape=[jax.ShapeDtypeStruct((cfg.N, MLP), BF16), jax.ShapeDtypeStruct((8, MLP), F32)],
        compiler_params=_cp(1),
    )(dact, a1, b1)


def _final_fwd_bwd(cfg, x1, mlp, b2, gate2, ln2_g, ln2_b, target):
    TB, nbl = cfg.TB, cfg.nbl

    def body(x1_ref, m_ref, b2_ref, g2_ref, lg_ref, lb_ref, t_ref, dm_ref, dx_ref, ex_ref, gl_ref):
        b = pl.program_id(0)
        j = pl.program_id(1)

        @pl.when(j == 0)
        def _():
            ex_ref[...] = jnp.zeros_like(ex_ref)

        @pl.when(jnp.logical_and(b == 0, j == 0))
        def _():
            gl_ref[...] = jnp.zeros_like(gl_ref)

        mv = m_ref[...] + b2_ref[...]
        g2 = g2_ref[...]
        r2 = ALPHA * x1_ref[...] + g2 * mv
        xh, rs = _ln(r2)
        lg = lg_ref[...]
        x2 = xh * lg + lb_ref[...]
        err = x2 - t_ref[...]
        ls = jnp.sum(jnp.sum(err * err, axis=1, keepdims=True), axis=0, keepdims=True) * (0.5 / D)
        gl_ref[3:4, :] += ls
        dx2 = err * (1.0 / D)
        gl_ref[0:1, :] += jnp.sum(dx2 * xh, axis=0, keepdims=True)
        gl_ref[1:2, :] += jnp.sum(dx2, axis=0, keepdims=True)
        dr2 = _ln_bwd(dx2 * lg, xh, rs)
        ex_ref[0:1, :] += jnp.sum(dr2 * mv, axis=0, keepdims=True)
        dmv = dr2 * g2
        gl_ref[2:3, :] += jnp.sum(dmv, axis=0, keepdims=True)
        dm_ref[...] = dmv.astype(BF16)
        dx_ref[...] = ALPHA * dr2

    la = pl.BlockSpec((TB, D), lambda b, j: (b * nbl + j, 0))
    ex = pl.BlockSpec((None, 1, D), lambda b, j: (b, 0, 0))
    vec = pl.BlockSpec((1, D), lambda b, j: (0, 0))
    return pl.pallas_call(
        body, name="final_fwd_bwd", grid=(cfg.Bn, nbl),
        in_specs=[la, la, vec, ex, vec, vec, la],
        out_specs=[la, la, pl.BlockSpec((None, 8, D), lambda b, j: (b, 0, 0)),
                   pl.BlockSpec((8, D), lambda b, j: (0, 0))],
        out_shape=[jax.ShapeDtypeStruct((cfg.N, D), BF16), jax.ShapeDtypeStruct((cfg.N, D), F32),
                   jax.ShapeDtypeStruct((cfg.Bn, 8, D), F32), jax.ShapeDtypeStruct((8, D), F32)],
        compiler_params=_cp(2),
    )(x1, mlp, b2, gate2, ln2_g, ln2_b, target)


def _ln_mod_bwd(cfg, dh_a, dh_b, xs, scale_tab, dxp):
    TB, nbt, nbl = cfg.TB, cfg.nbt, cfg.nbl

    def body(da_ref, db_ref, x_ref, sc_ref, dxp_ref, gx_ref, acc_ref):
        j = pl.program_id(1)

        @pl.when(j <= 1)
        def _():
            acc_ref[...] = jnp.zeros_like(acc_ref)

        dh = da_ref[...] + db_ref[...]
        xhat, rs = _ln(x_ref[...])
        acc_ref[0:1, :] += jnp.sum(dh, axis=0, keepdims=True)
        acc_ref[1:2, :] += jnp.sum(dh * xhat, axis=0, keepdims=True)
        gx_ref[...] = dxp_ref[...] + _ln_bwd(dh * (1.0 + sc_ref[...]), xhat, rs)

    st = pl.BlockSpec((TB, D), lambda b, j: (b * nbt + j, 0))
    la = pl.BlockSpec((TB, D), lambda b, j: (b * nbl + jnp.maximum(j - 1, 0), 0))
    return pl.pallas_call(
        body, name="ln_mod_bwd", grid=(cfg.Bn, nbt),
        in_specs=[st, st, st,
                  pl.BlockSpec((None, 1, D), lambda b, j: (2 * b + jnp.minimum(j, 1), 0, 0)), la],
        out_specs=[la, pl.BlockSpec((None, 8, D), lambda b, j: (2 * b + jnp.minimum(j, 1), 0, 0))],
        out_shape=[jax.ShapeDtypeStruct((cfg.N, D), F32), jax.ShapeDtypeStruct((2 * cfg.Bn, 8, D), F32)],
        compiler_params=_cp(2),
    )(dh_a, dh_b, xs, scale_tab, dxp)


def _perm_w_in(w_in):
    w_main = jnp.concatenate([w_in[:, 0:3072], w_in[:, 4160:5184], w_in[:, 3136:4160], w_in[:, 5184:10304]], axis=1)
    w_dt = jnp.pad(w_in[:, 3072:3136], ((0, 0), (0, DTW - 2 * NH)))
    return w_main, w_dt


def _unperm_w_in(dw_main, dw_dt):
    return jnp.concatenate([dw_main[:, 0:3072], dw_dt[:, :2 * NH], dw_main[:, 4096:5120],
                            dw_main[:, 3072:4096], dw_main[:, 5120:]], axis=1)


def _unpack_rest(rest_all):
    out, off = {}, 0
    for n, shp, axis in _BIG[1:]:
        shard_shape = (shp[0] // NDEV, shp[1]) if axis == 0 else (shp[0], shp[1] // NDEV)
        r = math.prod(shard_shape) // 1024
        out[n] = _from_slots(rest_all[:, off:off + r, :].reshape((NDEV,) + shard_shape), axis)
        off += r
    return out


def _local_step(cfg, x, ctx, target, m, mc, W, rest_payload):
    Bn, T, Tc = cfg.Bn, cfg.T, cfg.Tc
    NT, N = cfg.NT, cfg.N
    xs = jnp.concatenate([ctx, x], axis=1).reshape(NT, D)
    mch = [m[:, i * D:(i + 1) * D] for i in range(NMOD)]
    ctx_sh = jnp.broadcast_to(mc[None, :D], (Bn, D))
    ctx_sc = jnp.broadcast_to(mc[None, D:], (Bn, D))
    shift_tab = jnp.stack([ctx_sh, mch[0]], axis=1).reshape(2 * Bn, 1, D)
    scale_tab = jnp.stack([ctx_sc, mch[1]], axis=1).reshape(2 * Bn, 1, D)
    gate1 = mch[2].reshape(Bn, 1, D)
    shift2 = mch[3].reshape(Bn, 1, D)
    scale2 = mch[4].reshape(Bn, 1, D)
    gate2 = mch[5].reshape(Bn, 1, D)

    conv_w = jnp.concatenate([W["ssd_conv_w"], W["lru_conv_w"]], axis=1)
    conv_b = jnp.concatenate([W["ssd_conv_b"], W["lru_conv_b"]], axis=1)
    dt_bias = jnp.pad(W["ssd_dt_bias"].reshape(1, 2 * NH), ((0, 0), (0, DTW - 2 * NH)))
    a_log = jnp.pad(W["ssd_a_log"].reshape(1, 2 * NH), ((0, 0), (0, DTW - 2 * NH)))
    dvec = jnp.repeat(W["ssd_d"].reshape(NH), HD).reshape(1, DI)
    lba = W["lru_ba"].reshape(2, 1, LW)
    lbi = W["lru_bi"].reshape(2, 1, LW)
    llam = W["lru_lambda"].reshape(2, 1, LW)

    h = _ln_mod_fwd(cfg, xs, shift_tab, scale_tab)
    proj, rest_all = _mm(h, W["w_main"], "nn", "mm_proj", tm=1024, tn=1024, tk=1024, xch=(rest_payload, True))
    W = dict(W, **_unpack_rest(rest_all))
    dt_raw = _mm(h, W["w_dt"], "nn", "mm_dt", tm=512, tn=DTW, tk=1024)
    dt, dtg, cumg, cumTg = _dt_fwd(cfg, dt_raw, dt_bias, a_log)
    act = _conv_fwd(cfg, proj, conv_w, conv_b)
    ys, hss, hls = [], [], []
    for rev in (False, True):
        y_d, hs_d = _ssd_fwd(cfg, act, dtg, cumg, cumTg, rev)
        ys.append(y_d)
        hss.append(hs_d)
        hls.append(_lru_fwd(cfg, act, W["lru_wa"], W["lru_wi"], lba, lbi, llam, rev))
    nssd = _post_ssd_fwd(cfg, ys[0], ys[1], act, proj, dvec, W["ssd_norm_w"])
    vlru = _post_lru_fwd(cfg, hls[0], hls[1], proj)
    br_ssd = _mm(nssd, W["w_br_ssd"], "nn", "mm_br_ssd", tm=1024, tn=1024, tk=1024)
    br_lru = _mm(vlru, W["w_br_lru"], "nn", "mm_br_lru", tm=1024, tn=1024, tk=1024)
    mix = _merge_fwd(cfg, proj, W["b_gate"], br_ssd, br_lru)
    x_mix = _mm(mix, W["w_out"], "nn", "mm_out", tm=1024, tn=1024, tk=1024)
    x1, h2 = _resid1_fwd(cfg, xs, x_mix, gate1, shift2, scale2, W["ln1_g"], W["ln1_b"])
    a1 = _mm(h2, W["w_mlp1"], "nn", "mm_mlp1", tm=1024, tn=1024, tk=1024)
    actm = _mlp_act_fwd(cfg, a1, W["b_mlp1"])
    mlp = _mm(actm, W["w_mlp2"], "nn", "mm_mlp2", tm=1024, tn=1024, tk=1024)
    dmlp, dx1p, ex2, gl2 = _final_fwd_bwd(cfg, x1, mlp, W["b_mlp2"], gate2, W["ln2_g"], W["ln2_b"],
                                          target.reshape(N, D))

    g = {}
    g["ln2_g"], g["ln2_b"], g["b_mlp2"] = gl2[0:1], gl2[1:2], gl2[2:3]
    loss_partial = gl2[3, 0]
    gw = {}
    dact = _mm(dmlp, W["w_mlp2"], "nt", "mm_dact", tm=1024, tn=1024, tk=1024)
    gw["w_mlp2"] = _mm(actm, dmlp, "tn", "mm_dw_mlp2", BF16, tm=1024, tn=1024, tk=512)
    da1, accb1 = _mlp_act_bwd(cfg, dact, a1, W["b_mlp1"])
    g["b_mlp1"] = accb1[0:1]
    dh2 = _mm(da1, W["w_mlp1"], "nt", "mm_dh2", tm=1024, tn=1024, tk=1024)
    gw["w_mlp1"] = _mm(h2, da1, "tn", "mm_dw_mlp1", BF16, tm=1024, tn=1024, tk=512)
    dx_mix, dxp, ex1, gl1 = _resid1_bwd(cfg, dh2, dx1p, x1, xs, x_mix, gate1, scale2, W["ln1_g"])
    g["ln1_g"], g["ln1_b"] = gl1[0:1], gl1[1:2]
    dmix = _mm(dx_mix, W["w_out"], "nt", "mm_dmix", tm=1024, tn=1024, tk=1024)
    gw["w_out"] = _mm(mix, dx_mix, "tn", "mm_dw_out", BF16, tm=1024, tn=1024, tk=512)
    dbs, dbl, dproj, accg = _merge_bwd(cfg, dmix, proj, W["b_gate"], br_ssd, br_lru)
    g["b_gate"] = accg[0:1]
    dnssd = _mm(dbs, W["w_br_ssd"], "nt", "mm_dnssd", tm=1024, tn=1024, tk=1024)
    gw["w_br_ssd"] = _mm(nssd, dbs, "tn", "mm_dw_br_ssd", BF16, tm=1024, tn=1024, tk=512)
    dvlru = _mm(dbl, W["w_br_lru"], "nt", "mm_dvlru", tm=1024, tn=1024, tk=1024)
    gw["w_br_lru"] = _mm(vlru, dbl, "tn", "mm_dw_br_lru", BF16, tm=1024, tn=1024, tk=512)
    dy, dproj, accs = _post_ssd_bwd(cfg, dproj, dnssd, ys[0], ys[1], act, proj, dvec, W["ssd_norm_w"])
    g["ssd_norm_w"] = accs[0:1]
    dD_cols = accs[1:2]
    dyl, dproj = _post_lru_bwd(cfg, dproj, dvlru, hls[0], hls[1], proj)

    rest_slots = jnp.concatenate([_to_slots(gw[n], axis).reshape(NDEV, -1, 1024) for n, _, axis in _BIG[1:]], axis=1)
    xres = {}
    dxh, dBs, dCs, dAs, dxxs, dus = [], [], [], [], [], []
    dwas, dwis, lvecs = [], [], []
    for i, rev in enumerate((False, True)):
        if i == 0:
            o, xres["rs_rest"] = _ssd_bwd(cfg, act, dtg, cumg, cumTg, hss[i], dy, rev, xch=(rest_slots, False))
        else:
            o = _ssd_bwd(cfg, act, dtg, cumg, cumTg, hss[i], dy, rev)
        dxh.append(o[0]); dBs.append(o[1]); dCs.append(o[2]); dAs.append(o[3]); dxxs.append(o[4])
        du, dwa, dwi, lv = _lru_bwd(cfg, act, W["lru_wa"], W["lru_wi"], lba, lbi, llam, hls[i], dyl, rev)
        dus.append(du); dwas.append(dwa); dwis.append(dwi); lvecs.append(lv)
    lru_payload = jnp.stack([jnp.stack(dwas), jnp.stack(dwis)]).reshape(-1, 1024)
    g["lru_ba"] = jnp.stack([lvecs[0][0], lvecs[1][0]])
    g["lru_bi"] = jnp.stack([lvecs[0][1], lvecs[1][1]])
    g["lru_lambda"] = jnp.stack([lvecs[0][2], lvecs[1][2]])

    ddt_raw, accdt = _dt_bwd(cfg, dAs, dxxs, dt_raw, dt, dt_bias, a_log)
    g["ssd_a_log"] = accdt[0, :2 * NH].reshape(2, NH)
    g["ssd_dt_bias"] = accdt[1, :2 * NH].reshape(2, NH)

    (dproj, accx), xres["ag_lru"] = _conv_bwd(cfg, "conv_bwd_x", dproj, proj, conv_w, conv_b, [dxh[0], dxh[1]], 0, DI,
                                              True, skip=(dy, dvec), xch=(lru_payload, True))
    dproj, accB = _conv_bwd(cfg, "conv_bwd_b", dproj, proj, conv_w, conv_b, [dBs[0], dBs[1]], DI, NG * NS, True)
    dproj, accC = _conv_bwd(cfg, "conv_bwd_c", dproj, proj, conv_w, conv_b, [dCs[0], dCs[1]], DI + NG * NS, NG * NS,
                            True)
    dproj, accl = _conv_bwd(cfg, "conv_bwd_lru", dproj, proj, conv_w, conv_b, [dus[0], dus[1]], DI + 2 * NG * NS, LW,
                            False)
    accssd = jnp.concatenate([accx, accB, accC], axis=1)
    g["ssd_conv_w"], g["ssd_conv_b"] = accssd[0:4], accssd[4:5]
    g["lru_conv_w"], g["lru_conv_b"] = accl[0:4], accl[4:5]
    dw_main = _mm(h, dproj, "tn", "mm_dw_main", BF16, tm=1024, tn=2048, tk=512)
    dw_dt = _mm(h, ddt_raw, "tn", "mm_dw_dt", BF16, tm=1024, tn=DTW, tk=512)
    w_in_slots = _to_slots(_unperm_w_in(dw_main, dw_dt), 1)
    dh_a, xres["rs_w_in"] = _mm(dproj, W["w_main"], "nt", "mm_dh_main", tm=1024, tn=1024, tk=1024,
                                xch=(w_in_slots, False))
    dh_b = _mm(ddt_raw, W["w_dt"], "nt", "mm_dh_dt", tm=512, tn=1024, tk=DTW)
    grad_x, acct = _ln_mod_bwd(cfg, dh_a, dh_b, xs, scale_tab, dxp)
    acct = acct.reshape(Bn, 2, 8, D)
    dm = jnp.concatenate([acct[:, 1, 0], acct[:, 1, 1], ex1[:, 2], ex1[:, 0], ex1[:, 1], ex2[:, 0]], axis=1)
    dmc = jnp.concatenate([acct[:, 0, 0], acct[:, 0, 1]], axis=1)
    g["ssd_d_cols"] = dD_cols
    return loss_partial, grad_x.reshape(Bn, T, D), g, dm, dmc, xres


MESH = pl.DeviceIdType.MESH
_HBM = pl.BlockSpec(memory_space=pltpu.HBM)


def _me():
    return 4 * lax.axis_index("x") + 2 * lax.axis_index("y") + lax.axis_index("c")


def _peer(k):
    px = (lax.axis_index("x") + ((k >> 2) & 1)) % 2
    py = (lax.axis_index("y") + ((k >> 1) & 1)) % 2
    pc = (lax.axis_index("c") + (k & 1)) % 2
    return (px, py, pc), 4 * px + 2 * py + pc


def _xchg_copies(x_ref, o_ref, send_sems, recv_sems, loc_sem, gather):
    me = _me()
    src_me = x_ref if gather else x_ref.at[me]
    loc = pltpu.make_async_copy(src_me, o_ref.at[me], loc_sem)
    sends, recvs = [], []
    for k in range(1, NDEV):
        peer, pid = _peer(k)
        sends.append(pltpu.make_async_remote_copy(
            src_ref=x_ref if gather else x_ref.at[pid], dst_ref=o_ref.at[me],
            send_sem=send_sems.at[k - 1], recv_sem=recv_sems.at[k - 1],
            device_id=peer, device_id_type=MESH))
        recvs.append(pltpu.make_async_remote_copy(
            src_ref=src_me, dst_ref=o_ref.at[pid],
            send_sem=send_sems.at[k - 1], recv_sem=recv_sems.at[k - 1],
            device_id=peer, device_id_type=MESH))
    return loc, sends, recvs


def _xchg_start(*refs, gather):
    loc, sends, _ = _xchg_copies(*refs, gather)
    loc.start()
    for cp in sends:
        cp.start()


def _xchg_wait(*refs, gather):
    loc, sends, recvs = _xchg_copies(*refs, gather)
    for cp in recvs:
        cp.wait_recv()
    for cp in sends:
        cp.wait_send()
    loc.wait()


_XCHG_SCRATCH = [pltpu.SemaphoreType.DMA((NDEV - 1,)), pltpu.SemaphoreType.DMA((NDEV - 1,)), pltpu.SemaphoreType.DMA]


def _xchg_out_shape(x, gather):
    return jax.ShapeDtypeStruct((NDEV,) + tuple(x.shape if gather else x.shape[1:]), x.dtype)


def _exchange(x, name, gather):
    def body(x_ref, o_ref, send_sems, recv_sems, loc_sem):
        _xchg_start(x_ref, o_ref, send_sems, recv_sems, loc_sem, gather=gather)
        _xchg_wait(x_ref, o_ref, send_sems, recv_sems, loc_sem, gather=gather)

    return pl.pallas_call(
        body, name=name, out_shape=_xchg_out_shape(x, gather),
        in_specs=[_HBM], out_specs=_HBM, scratch_shapes=_XCHG_SCRATCH,
    )(x)


def _hosted_call(body, xch, *, name, grid, in_specs, out_specs, out_shape, scratch_shapes, compiler_params, args,
                 aliases=None):
    aliases = aliases or {}
    if xch is None:
        return pl.pallas_call(body, name=name, grid=grid, in_specs=in_specs, out_specs=out_specs,
                              out_shape=out_shape, scratch_shapes=scratch_shapes, input_output_aliases=aliases,
                              compiler_params=compiler_params)(*args)
    xv, gather = xch
    n_in, n_out, n_scr = len(in_specs), len(out_specs), len(scratch_shapes)

    def wrapped(*refs):
        ins = refs[:n_in]
        x_ref = refs[n_in]
        outs = refs[n_in + 1:n_in + 1 + n_out]
        o_ref = refs[n_in + 1 + n_out]
        scr = refs[n_in + 2 + n_out:]
        own, sems = scr[:n_scr], scr[n_scr:]
        first = functools.reduce(jnp.logical_and, [pl.program_id(a) == 0 for a in range(len(grid))])
        last = functools.reduce(jnp.logical_and, [pl.program_id(a) == grid[a] - 1 for a in range(len(grid))])

        @pl.when(first)
        def _():
            _xchg_start(x_ref, o_ref, *sems, gather=gather)

        body(*ins, *outs, *own)

        @pl.when(last)
        def _():
            _xchg_wait(x_ref, o_ref, *sems, gather=gather)

    res = pl.pallas_call(
        wrapped, name=name, grid=grid, in_specs=list(in_specs) + [_HBM], out_specs=list(out_specs) + [_HBM],
        out_shape=list(out_shape) + [_xchg_out_shape(xv, gather)],
        scratch_shapes=list(scratch_shapes) + _XCHG_SCRATCH, input_output_aliases=aliases,
        compiler_params=compiler_params,
    )(*args, xv)
    return list(res[:n_out]), res[n_out]


def _row_tile(R, cap, mult=8):
    best = mult
    t = mult
    while t <= min(R, cap):
        if R % t == 0:
            best = t
        t += mult
    assert R % best == 0, R
    return best


def _sum_slots(x, name):
    _, R, C = x.shape
    tr = _row_tile(R, 256, 16 if x.dtype == BF16 else 8)

    def body(x_ref, o_ref):
        acc = x_ref[0].astype(F32)
        for i in range(1, NDEV):
            acc = acc + x_ref[i].astype(F32)
        o_ref[...] = acc

    return pl.pallas_call(
        body, name=name, grid=(R // tr,),
        in_specs=[pl.BlockSpec((NDEV, tr, C), lambda i: (0, i, 0))],
        out_specs=pl.BlockSpec((tr, C), lambda i: (i, 0)),
        out_shape=jax.ShapeDtypeStruct((R, C), F32),
        compiler_params=_cp(1),
    )(x)


def _adamw_update(w_ref, g_ref, m_ref, v_ref, d_ref, nm_ref, nv_ref):
    c1 = 1.0 / (1.0 - ADAM_B1 ** ADAM_STEP)
    c2 = 1.0 / (1.0 - ADAM_B2 ** ADAM_STEP)
    gv = g_ref[...]
    nm = ADAM_B1 * m_ref[...] + (1.0 - ADAM_B1) * gv
    nv = ADAM_B2 * v_ref[...] + (1.0 - ADAM_B2) * (gv * gv)
    d_ref[...] = -ADAM_LR * ((nm * c1) / (jnp.sqrt(nv * c2) + ADAM_EPS) + ADAM_WD * w_ref[...])
    nm_ref[...] = nm
    nv_ref[...] = nv


def _adamw_many(ws, gs, ms, vs):
    n = len(ws)

    def body(*refs):
        for i in range(n):
            _adamw_update(refs[i], refs[n + i], refs[2 * n + i], refs[3 * n + i],
                          refs[4 * n + i], refs[5 * n + i], refs[6 * n + i])

    shapes = [jax.ShapeDtypeStruct(w.shape, F32) for w in ws]
    res = pl.pallas_call(
        body, name="adamw_small", out_shape=shapes * 3,
        compiler_params=pltpu.CompilerParams(vmem_limit_bytes=VMEM_LIMIT_BYTES),
    )(*ws, *gs, *ms, *vs)
    return res[:n], res[n:2 * n], res[2 * n:]


def _adamw(w, g, m, v, name):
    R, C = w.shape
    tr = _row_tile(R, 256)

    def body(w_ref, g_ref, m_ref, v_ref, d_ref, nm_ref, nv_ref):
        _adamw_update(w_ref, g_ref, m_ref, v_ref, d_ref, nm_ref, nv_ref)

    blk = pl.BlockSpec((tr, C), lambda i: (i, 0))
    return pl.pallas_call(
        body, name=name, grid=(R // tr,),
        in_specs=[blk] * 4, out_specs=[blk] * 3,
        out_shape=[jax.ShapeDtypeStruct((R, C), F32)] * 3,
        compiler_params=_cp(1),
    )(w, g, m, v)


def _mod_fwd(c_rows, w_shard, b_shard):
    def body(c_ref, w_ref, b_ref, o_ref):
        s = _silu(c_ref[...]).astype(BF16)
        o_ref[...] = _dot(s, w_ref[...].astype(BF16)) + b_ref[...]

    return pl.pallas_call(
        body, name="mod_fwd",
        out_shape=jax.ShapeDtypeStruct((c_rows.shape[0], w_shard.shape[1]), F32),
        compiler_params=pltpu.CompilerParams(vmem_limit_bytes=VMEM_LIMIT_BYTES),
    )(c_rows, w_shard, b_shard)


def _mod_bwd(c_rows, dm_all, dm_shard, w_shard):
    nrow = c_rows.shape[0]

    def body(c_ref, da_ref, ds_ref, w_ref, gw_ref, gb_ref, cc_ref):
        s = _silu(c_ref[...]).astype(BF16)
        ds = ds_ref[...]
        gw_ref[...] = _dot_tn(s, ds.astype(BF16))
        gb_ref[...] = jnp.sum(da_ref[...], axis=0, keepdims=True)
        rowi = lax.broadcasted_iota(jnp.int32, ds.shape, 0)
        dmc = jnp.sum(jnp.where(rowi % 8 >= 4, ds, 0.0), axis=0, keepdims=True)
        dmc8 = jnp.broadcast_to(dmc, (8, ds.shape[1])).astype(BF16)
        cc_ref[...] = _dot_nt(dmc8, w_ref[...].astype(BF16))

    return pl.pallas_call(
        body, name="mod_bwd",
        out_shape=[jax.ShapeDtypeStruct(w_shard.shape, F32),
                   jax.ShapeDtypeStruct((1, dm_all.shape[1]), F32),
                   jax.ShapeDtypeStruct((8, D), F32)],
        compiler_params=pltpu.CompilerParams(vmem_limit_bytes=VMEM_LIMIT_BYTES),
    )(c_rows, dm_all, dm_shard, w_shard)


def _small_finish(cc_pre, c_ctx, dd_cols):
    def body(cc_ref, c_ref, dd_ref, gc_ref, gd_ref):
        gc_ref[...] = cc_ref[...] * _silu_grad(c_ref[...])
        gd_ref[...] = jnp.sum(dd_ref[...], axis=1, keepdims=True)

    return pl.pallas_call(
        body, name="small_finish",
        out_shape=[jax.ShapeDtypeStruct((1, D), F32), jax.ShapeDtypeStruct((NH, 1), F32)],
    )(cc_pre, c_ctx, dd_cols)


_BIG = (("w_in", (D, 10304), 1), ("w_br_ssd", (DI, D), 0), ("w_br_lru", (LW, D), 0), ("w_out", (D, D), 0),
        ("w_mlp1", (D, MLP), 1), ("w_mlp2", (MLP, D), 0))
_SMALL_SH = (("ssd_conv_w", (4, 4096)), ("lru_conv_w", (4, LW)), ("lru_ba", (2, LW)), ("lru_bi", (2, LW)),
             ("lru_lambda", (2, LW)))
_REPL = (("c_ctx", (D,)), ("b_gate", (2 * D,)), ("ssd_conv_b", (4096,)), ("ssd_dt_bias", (2, NH)),
         ("ssd_a_log", (2, NH)), ("ssd_d", (DI,)), ("ssd_norm_w", (DI,)), ("lru_conv_b", (LW,)),
         ("ln1_g", (D,)), ("ln1_b", (D,)),
         ("b_mlp1", (MLP,)), ("b_mlp2", (D,)), ("ln2_g", (D,)), ("ln2_b", (D,)))

_WEIGHT_NAMES = ('c_ctx', 'w_mod', 'b_mod', 'w_in', 'b_gate', 'ssd_conv_w', 'ssd_conv_b', 'ssd_dt_bias', 'ssd_a_log',
                 'ssd_d', 'ssd_norm_w', 'lru_conv_w', 'lru_conv_b', 'lru_wa', 'lru_ba', 'lru_wi', 'lru_bi',
                 'lru_lambda', 'w_br_ssd', 'w_br_lru', 'w_out', 'ln1_g', 'ln1_b', 'w_mlp1', 'b_mlp1', 'w_mlp2',
                 'b_mlp2', 'ln2_g', 'ln2_b')
_ARG_NAMES = ('x', 'c', 'ctx') + _WEIGHT_NAMES + ('loss_target',) + tuple('m_' + n for n in _WEIGHT_NAMES) + tuple(
    'v_' + n for n in _WEIGHT_NAMES)


def _to_slots(full, axis):
    n = full.shape[axis] // NDEV
    if axis == 0:
        return full.reshape(NDEV, n, full.shape[1])
    return full.reshape(full.shape[0], NDEV, n).transpose(1, 0, 2)


def _from_slots(slots, axis):
    if axis == 0:
        return slots.reshape(NDEV * slots.shape[1], slots.shape[2])
    return slots.transpose(1, 0, 2).reshape(slots.shape[1], NDEV * slots.shape[2])


def _pack_rows(arrs, width=1024, mult=8):
    flat = jnp.concatenate([a.reshape(-1) for a in arrs])
    n = flat.shape[0]
    per = width * mult
    tot = -(-n // per) * per
    return jnp.pad(flat, (0, tot - n)).reshape(tot // width, width)


def _unpack_rows(packed, shapes, lead=()):
    nl = len(lead)
    flat = packed.reshape(tuple(lead) + (-1,))
    out, off = [], 0
    for s in shapes:
        n = math.prod(s)
        out.append(flat[..., off:off + n].reshape(tuple(lead) + tuple(s)))
        off += n
    return out


def kernel(x, c, ctx, c_ctx, w_mod, b_mod, w_in, b_gate, ssd_conv_w, ssd_conv_b, ssd_dt_bias, ssd_a_log, ssd_d, ssd_norm_w, lru_conv_w, lru_conv_b, lru_wa, lru_ba, lru_wi, lru_bi, lru_lambda, w_br_ssd, w_br_lru, w_out, ln1_g, ln1_b, w_mlp1, b_mlp1, w_mlp2, b_mlp2, ln2_g, ln2_b, loss_target, m_c_ctx, m_w_mod, m_b_mod, m_w_in, m_b_gate, m_ssd_conv_w, m_ssd_conv_b, m_ssd_dt_bias, m_ssd_a_log, m_ssd_d, m_ssd_norm_w, m_lru_conv_w, m_lru_conv_b, m_lru_wa, m_lru_ba, m_lru_wi, m_lru_bi, m_lru_lambda, m_w_br_ssd, m_w_br_lru, m_w_out, m_ln1_g, m_ln1_b, m_w_mlp1, m_b_mlp1, m_w_mlp2, m_b_mlp2, m_ln2_g, m_ln2_b, v_c_ctx, v_w_mod, v_b_mod, v_w_in, v_b_gate, v_ssd_conv_w, v_ssd_conv_b, v_ssd_dt_bias, v_ssd_a_log, v_ssd_d, v_ssd_norm_w, v_lru_conv_w, v_lru_conv_b, v_lru_wa, v_lru_ba, v_lru_wi, v_lru_bi, v_lru_lambda, v_w_br_ssd, v_w_br_lru, v_w_out, v_ln1_g, v_ln1_b, v_w_mlp1, v_b_mlp1, v_w_mlp2, v_b_mlp2, v_ln2_g, v_ln2_b):
    A = dict(zip(_ARG_NAMES, (x, c, ctx, c_ctx, w_mod, b_mod, w_in, b_gate, ssd_conv_w, ssd_conv_b, ssd_dt_bias, ssd_a_log, ssd_d, ssd_norm_w, lru_conv_w, lru_conv_b, lru_wa, lru_ba, lru_wi, lru_bi, lru_lambda, w_br_ssd, w_br_lru, w_out, ln1_g, ln1_b, w_mlp1, b_mlp1, w_mlp2, b_mlp2, ln2_g, ln2_b, loss_target, m_c_ctx, m_w_mod, m_b_mod, m_w_in, m_b_gate, m_ssd_conv_w, m_ssd_conv_b, m_ssd_dt_bias, m_ssd_a_log, m_ssd_d, m_ssd_norm_w, m_lru_conv_w, m_lru_conv_b, m_lru_wa, m_lru_ba, m_lru_wi, m_lru_bi, m_lru_lambda, m_w_br_ssd, m_w_br_lru, m_w_out, m_ln1_g, m_ln1_b, m_w_mlp1, m_b_mlp1, m_w_mlp2, m_b_mlp2, m_ln2_g, m_ln2_b, v_c_ctx, v_w_mod, v_b_mod, v_w_in, v_b_gate, v_ssd_conv_w, v_ssd_conv_b, v_ssd_dt_bias, v_ssd_a_log, v_ssd_d, v_ssd_norm_w, v_lru_conv_w, v_lru_conv_b, v_lru_wa, v_lru_ba, v_lru_wi, v_lru_bi, v_lru_lambda, v_w_br_ssd, v_w_br_lru, v_w_out, v_ln1_g, v_ln1_b, v_w_mlp1, v_b_mlp1, v_w_mlp2, v_b_mlp2, v_ln2_g, v_ln2_b)))
    Bn, T, _ = x.shape
    Tc = ctx.shape[1]
    cfg = _Cfg(Bn, T, Tc)
    me = _me()
    L = {n: (A[n] if n == "c_ctx" else A[n][0]) for n in _WEIGHT_NAMES}
    nmod = L["w_mod"].shape[1]

    c_all = _exchange(c, "ag_c", True)
    c_rows = jnp.concatenate([c_all.reshape(NDEV * Bn, D), jnp.broadcast_to(c_ctx[None, :], (8, D))], axis=0)
    b_shard = lax.dynamic_slice(L["b_mod"], (me * nmod,), (nmod,)).reshape(1, nmod)
    m_part = _mod_fwd(c_rows, L["w_mod"], b_shard)
    m_all = _exchange(m_part, "ag_mod", True)
    m_full = m_all.transpose(1, 0, 2).reshape(NDEV * Bn + 8, NMOD * D)
    m_mine = lax.dynamic_slice(m_full, (me * Bn, 0), (Bn, NMOD * D))
    mc = m_full[NDEV * Bn, :2 * D]

    w_in_all = _exchange(L["w_in"].astype(BF16), "ag_w_in", True)
    rest_payload = jnp.concatenate([L[n].astype(BF16).reshape(-1, 1024) for n, _, _ in _BIG[1:]], axis=0)
    small_shapes = [(s[0], s[1] // NDEV) for _, s in _SMALL_SH]
    small_all = _exchange(_pack_rows([L[n] for n, _ in _SMALL_SH], width=512), "ag_w_small", True)
    W = {}
    for (n, shp), piece in zip(_SMALL_SH, _unpack_rows(small_all, small_shapes, lead=(NDEV,))):
        W[n] = piece.transpose(1, 0, 2).reshape(shp)
    W["w_main"], W["w_dt"] = _perm_w_in(_from_slots(w_in_all, 1))
    for n in ("ssd_conv_b", "lru_conv_b", "ssd_norm_w", "b_gate", "ln1_g", "ln1_b", "b_mlp1", "b_mlp2", "ln2_g", "ln2_b"):
        W[n] = L[n].reshape(1, -1)
    for n in ("ssd_dt_bias", "ssd_a_log", "ssd_d", "lru_wa", "lru_wi"):
        W[n] = L[n]

    loss_part, grad_x, g, dm, dmc, xres = _local_step(cfg, x, ctx, loss_target, m_mine, mc, W, rest_payload)
    loss = lax.psum(loss_part, ("x", "y", "c"))

    dmc_pad = jnp.pad(dmc, ((0, 4 - Bn), (0, (NMOD - 2) * D)))
    dm_all = _exchange(jnp.concatenate([jnp.pad(dm, ((0, 4 - Bn), (0, 0))), dmc_pad], axis=0), "ag_dm", True)
    dm_all = dm_all.reshape(NDEV * 8, NMOD * D)
    c_rows_b = jnp.concatenate([jnp.pad(c_all, ((0, 0), (0, 4 - Bn), (0, 0))),
                                jnp.broadcast_to(c_ctx[None, None, :], (NDEV, 4, D))], axis=1).reshape(NDEV * 8, D)
    dm_shard = lax.dynamic_slice(dm_all, (0, me * nmod), (NDEV * 8, nmod))
    g_w_mod, g_b_mod, cc_part = _mod_bwd(c_rows_b, dm_all, dm_shard, L["w_mod"])
    g["c_ctx"] = cc_part[0]

    g["ssd_d"] = g.pop("ssd_d_cols")
    small_names = [n for n, _ in _REPL] + [n for n, _ in _SMALL_SH]
    small_full_shapes = [s for _, s in _REPL] + [s for _, s in _SMALL_SH]
    sm_all = _exchange(_pack_rows([g[n] for n in small_names]), "ag_g_small", True)
    sm_sum = _sum_slots(sm_all, "sum_g_small")
    gs = dict(zip(small_names, _unpack_rows(sm_sum, small_full_shapes)))
    gcc, gdd = _small_finish(gs["c_ctx"].reshape(1, D), c_ctx.reshape(1, D), gs["ssd_d"].reshape(NH, HD))
    gs["c_ctx"] = gcc.reshape(D)
    gs["ssd_d"] = gdd.reshape(NH)
    for n, shp in _SMALL_SH:
        ns = shp[1] // NDEV
        gs[n] = lax.dynamic_slice(gs[n], (0, me * ns), (shp[0], ns))
    gs["b_mod"] = g_b_mod.reshape(NMOD * D)
    lru_sum = _sum_slots(xres["ag_lru"], "sum_g_lru").reshape(2, 2, LB, LBW, LBW)
    gs["lru_wa"], gs["lru_wi"] = lru_sum[0], lru_sum[1]

    gb = {}
    gb["w_in"] = _sum_slots(xres["rs_w_in"], "sum_w_in")
    red_b = _sum_slots(xres["rs_rest"], "sum_w_rest")
    off = 0
    for n, shp, axis in _BIG[1:]:
        shard_shape = (shp[0] // NDEV, shp[1]) if axis == 0 else (shp[0], shp[1] // NDEV)
        r = math.prod(shard_shape) // 1024
        gb[n] = red_b[off:off + r].reshape(shard_shape)
        off += r
    gb["w_mod"] = g_w_mod

    grads, deltas, new_m, new_v = {}, {}, {}, {}
    big_names = ["w_mod"] + [n for n, _, _ in _BIG]
    for n in big_names:
        d_, nm_, nv_ = _adamw(L[n], gb[n], A["m_" + n][0], A["v_" + n][0], "adamw_" + n)
        grads[n], deltas[n], new_m[n], new_v[n] = gb[n], d_, nm_, nv_
    sm_names = [n for n in _WEIGHT_NAMES if n not in big_names]

    def two_d(a):
        return a.reshape(1, -1) if a.ndim == 1 else a
    loc = lambda pre: [two_d(A[pre + n] if n == "c_ctx" else A[pre + n][0]) for n in sm_names]
    gsm = [two_d(gs[n].reshape(L[n].shape)) for n in sm_names]
    ds_, nms_, nvs_ = _adamw_many(loc(""), gsm, loc("m_"), loc("v_"))
    for n, gv, dv, mv, vv in zip(sm_names, gsm, ds_, nms_, nvs_):
        shp = L[n].shape
        grads[n], deltas[n], new_m[n], new_v[n] = gv.reshape(shp), dv.reshape(shp), mv.reshape(shp), vv.reshape(shp)

    def out(dct):
        return [dct[n] if n == "c_ctx" else dct[n][None] for n in _WEIGHT_NAMES]
    return (loss, grad_x, *out(grads), *out(deltas), *out(new_m), *out(new_v))
```

```python
import functools
import math

import jax
import jax.numpy as jnp
from jax import lax
from jax.experimental import pallas as pl
from jax.experimental.pallas import tpu as pltpu

F32 = jnp.float32
BF16 = jnp.bfloat16

D = 1024
GRID_W = 64
DI = 2048
NH = 32
HD = 64
NG = 8
HPG = 4
NS = 128
CH = 128
LW = 1024
LB = 8
LBW = 128
LRU_C = 8.0
MLP = 4096
NMOD = 6
ALPHA = 2.0 ** 0.25
LN_EPS = 1e-6
RMS_EPS = 1e-5
PM = 10240
DTW = 128
CONVW = 5120
NDEV = 8

ADAM_LR = 0.001
ADAM_B1 = 0.9
ADAM_B2 = 0.999
ADAM_EPS = 1e-08
ADAM_WD = 0.01
ADAM_STEP = 10

VMEM_LIMIT_BYTES = 56 * 1024 * 1024


def _cp(n_axes):
    return pltpu.CompilerParams(dimension_semantics=("arbitrary",) * n_axes,
                                vmem_limit_bytes=VMEM_LIMIT_BYTES)


def _sigmoid(x):
    return 1.0 / (1.0 + jnp.exp(-x))


def _silu(x):
    return x * _sigmoid(x)


def _silu_grad(x):
    s = _sigmoid(x)
    return s * (1.0 + x * (1.0 - s))


def _log1p_pos(e):
    return jnp.where(e < 1e-2, e * (1.0 - e * (0.5 - e * (1.0 / 3.0))), jnp.log(1.0 + e))


def _softplus(x):
    return jnp.maximum(x, 0.0) + _log1p_pos(jnp.exp(-jnp.abs(x)))


def _neg_expm1(x):
    series = -x * (1.0 + x * (0.5 + x * (1.0 / 6.0 + x * (1.0 / 24.0))))
    return jnp.where(x > -1e-2, series, 1.0 - jnp.exp(x))


_GELU_K = math.sqrt(2.0 / math.pi)


def _gelu(x):
    t = jnp.tanh(_GELU_K * (x + 0.044715 * x * x * x))
    return 0.5 * x * (1.0 + t)


def _gelu_grad(x):
    t = jnp.tanh(_GELU_K * (x + 0.044715 * x * x * x))
    dt = (1.0 - t * t) * _GELU_K * (1.0 + 3.0 * 0.044715 * x * x)
    return 0.5 * (1.0 + t) + 0.5 * x * dt


def _ln(x):
    mu = jnp.mean(x, axis=-1, keepdims=True)
    xc = x - mu
    var = jnp.mean(xc * xc, axis=-1, keepdims=True)
    rs = lax.rsqrt(var + LN_EPS)
    return xc * rs, rs


def _ln_bwd(dy, xhat, rs):
    m1 = jnp.mean(dy, axis=-1, keepdims=True)
    m2 = jnp.mean(dy * xhat, axis=-1, keepdims=True)
    return rs * (dy - m1 - xhat * m2)


def _dot(a, b):
    return lax.dot_general(a, b, (((1,), (0,)), ((), ())), preferred_element_type=F32)


def _dot_nt(a, b):
    return lax.dot_general(a, b, (((1,), (1,)), ((), ())), preferred_element_type=F32)


def _dot_tn(a, b):
    return lax.dot_general(a, b, (((0,), (0,)), ((), ())), preferred_element_type=F32)


def _split3(a):
    a0 = a.astype(BF16)
    r = a - a0.astype(F32)
    a1 = r.astype(BF16)
    a2 = (r - a1.astype(F32)).astype(BF16)
    return a0, a1, a2


def _dot_exact_l(m_bf, a):
    a0, a1, a2 = _split3(a)
    return _dot(m_bf, a0) + _dot(m_bf, a1) + _dot(m_bf, a2)


def _dot_exact_r(a, m_bf):
    a0, a1, a2 = _split3(a)
    return _dot(a0, m_bf) + _dot(a1, m_bf) + _dot(a2, m_bf)


def _dot_hilo_r(a, m_bf):
    a0 = a.astype(BF16)
    a1 = (a - a0.astype(F32)).astype(BF16)
    return _dot(a0, m_bf) + _dot(a1, m_bf)


def _tri(n, upper):
    ii = lax.broadcasted_iota(jnp.int32, (n, n), 0)
    kk = lax.broadcasted_iota(jnp.int32, (n, n), 1)
    m = (kk >= ii) if upper else (kk <= ii)
    return jnp.where(m, 1.0, 0.0).astype(BF16)


def _fit(n, t):
    t = min(t, n)
    while n % t:
        t //= 2
    return t


def _mm(a, b, mode, name, out_dtype=F32, tm=512, tn=512, tk=512, xch=None):
    if mode == "nn":
        M, K = a.shape
        N = b.shape[1]
    elif mode == "nt":
        M, K = a.shape
        N = b.shape[0]
    else:
        K, M = a.shape
        N = b.shape[1]
    tm, tn, tk = _fit(M, tm), _fit(N, tn), _fit(K, tk)
    assert M % tm == 0 and N % tn == 0 and K % tk == 0, (name, M, N, K, tm, tn, tk)
    nk = K // tk
    if mode == "tn":
        a_spec = pl.BlockSpec((tk, tm), lambda i, j, k: (k, i))
    else:
        a_spec = pl.BlockSpec((tm, tk), lambda i, j, k: (i, k))
    if mode == "nt":
        b_spec = pl.BlockSpec((tn, tk), lambda i, j, k: (j, k))
    else:
        b_spec = pl.BlockSpec((tk, tn), lambda i, j, k: (k, j))
    dn = {"nn": (((1,), (0,)), ((), ())), "nt": (((1,), (1,)), ((), ())), "tn": (((0,), (0,)), ((), ()))}[mode]

    def body(a_ref, b_ref, o_ref, acc_ref):
        k = pl.program_id(2)

        @pl.when(k == 0)
        def _():
            acc_ref[...] = jnp.zeros_like(acc_ref)

        acc_ref[...] += lax.dot_general(a_ref[...].astype(BF16), b_ref[...].astype(BF16), dn,
                                        preferred_element_type=F32)

        @pl.when(k == nk - 1)
        def _():
            o_ref[...] = acc_ref[...].astype(o_ref.dtype)

    res = _hosted_call(
        body, xch, name=name, grid=(M // tm, N // tn, nk),
        in_specs=[a_spec, b_spec],
        out_specs=[pl.BlockSpec((tm, tn), lambda i, j, k: (i, j))],
        out_shape=[jax.ShapeDtypeStruct((M, N), out_dtype)],
        scratch_shapes=[pltpu.VMEM((tm, tn), F32)],
        compiler_params=_cp(3), args=(a, b))
    if xch is None:
        return res[0]
    return res[0][0], res[1]


class _Cfg:
    def __init__(self, Bn, T, Tc):
        assert T % Tc == 0 and Tc % CH == 0 and Tc % GRID_W == 0
        self.Bn, self.T, self.Tc = Bn, T, Tc
        self.TT = T + Tc
        self.TB = Tc
        self.nbt = self.TT // self.TB
        self.nbl = T // self.TB
        self.NT = Bn * self.TT
        self.N = Bn * T
        self.nct = Tc // CH
        self.nlt = T // CH
        self.nch = self.nct + self.nlt


def _ln_mod_fwd(cfg, xs, shift_tab, scale_tab):
    TB, nbt = cfg.TB, cfg.nbt

    def body(x_ref, sh_ref, sc_ref, o_ref):
        xhat, _ = _ln(x_ref[...])
        o_ref[...] = (xhat * (1.0 + sc_ref[...]) + sh_ref[...]).astype(BF16)

    tab = pl.BlockSpec((None, 1, D), lambda b, j: (2 * b + jnp.minimum(j, 1), 0, 0))
    return pl.pallas_call(
        body, name="ln_mod_fwd", grid=(cfg.Bn, nbt),
        in_specs=[pl.BlockSpec((TB, D), lambda b, j: (b * nbt + j, 0)), tab, tab],
        out_specs=pl.BlockSpec((TB, D), lambda b, j: (b * nbt + j, 0)),
        out_shape=jax.ShapeDtypeStruct((cfg.NT, D), BF16),
        compiler_params=_cp(2),
    )(xs, shift_tab, scale_tab)


GP = 4
NGB = NG // GP
HPB = GP * HPG


def _head_select(d, gb, to_front):
    r = lax.broadcasted_iota(jnp.int32, (DTW, DTW), 0)
    c = lax.broadcasted_iota(jnp.int32, (DTW, DTW), 1)
    src, dst = (r, c) if to_front else (c, r)
    hit = jnp.logical_and(src == d * NH + gb * HPB + dst, dst < HPB)
    return jnp.where(hit, 1.0, 0.0).astype(BF16)


def _dt_fwd(cfg, dt_raw, dt_bias, a_log):
    def body(raw_ref, bias_ref, alog_ref, dt_ref, dtg_ref, cumg_ref, cumT_ref):
        dt = _softplus(raw_ref[...] + bias_ref[...])
        a = -jnp.exp(alog_ref[...])
        dta = dt * a
        col = lax.broadcasted_iota(jnp.int32, (CH, DTW), 1)
        cf = _dot_exact_l(_tri(CH, False), dta)
        cr = _dot_exact_l(_tri(CH, True), dta)
        cum = jnp.where(col < NH, cf, cr)
        dt_ref[...] = dt
        for d in range(2):
            for gb in range(NGB):
                sel = _head_select(d, gb, True)
                dtg_ref[d, gb] = _dot_exact_r(dt, sel)
                cg = _dot_exact_r(cum, sel)
                cumg_ref[d, gb] = cg
                cumT_ref[d, gb] = cg.T

    blk = pl.BlockSpec((CH, DTW), lambda i: (i, 0))
    row = pl.BlockSpec((1, DTW), lambda i: (0, 0))
    gblk = pl.BlockSpec((2, NGB, CH, DTW), lambda i: (0, 0, i, 0))
    return pl.pallas_call(
        body, name="dt_fwd", grid=(cfg.NT // CH,),
        in_specs=[blk, row, row],
        out_specs=[blk, gblk, gblk, pl.BlockSpec((2, NGB, None, DTW, CH), lambda i: (0, 0, i, 0, 0))],
        out_shape=[jax.ShapeDtypeStruct((cfg.NT, DTW), F32),
                   jax.ShapeDtypeStruct((2, NGB, cfg.NT, DTW), F32),
                   jax.ShapeDtypeStruct((2, NGB, cfg.NT, DTW), F32),
                   jax.ShapeDtypeStruct((2, NGB, cfg.NT // CH, DTW, CH), F32)],
        compiler_params=_cp(1),
    )(dt_raw, dt_bias, a_log)


def _dt_bwd(cfg, dAs, dxxs, dt_raw, dt, dt_bias, a_log):
    def body(dAf_ref, dAr_ref, dxf_ref, dxr_ref, raw_ref, dt_ref, bias_ref, alog_ref, o_ref, acc_ref):
        i = pl.program_id(0)

        @pl.when(i == 0)
        def _():
            acc_ref[...] = jnp.zeros_like(acc_ref)

        a = -jnp.exp(alog_ref[...])
        col = lax.broadcasted_iota(jnp.int32, (CH, DTW), 1)
        dA_v = jnp.zeros((CH, DTW), F32)
        dxx_v = jnp.zeros((CH, DTW), F32)
        for d, (ra, rx) in enumerate(((dAf_ref, dxf_ref), (dAr_ref, dxr_ref))):
            for gb in range(NGB):
                sel = _head_select(d, gb, False)
                dA_v = dA_v + _dot_exact_r(ra[gb], sel)
                dxx_v = dxx_v + _dot_exact_r(rx[gb], sel)
        ddta = jnp.where(col < NH, _dot_exact_l(_tri(CH, True), dA_v), _dot_exact_l(_tri(CH, False), dA_v))
        dtv = dt_ref[...]
        ddt = ddta * a + dxx_v
        draw = ddt * _sigmoid(raw_ref[...] + bias_ref[...])
        draw = jnp.where(col < 2 * NH, draw, 0.0)
        o_ref[...] = draw
        da = jnp.sum(ddta * dtv, axis=0, keepdims=True) * a
        da = jnp.where(col[:1] < 2 * NH, da, 0.0)
        acc_ref[0:1, :] += da
        acc_ref[1:2, :] += jnp.sum(draw, axis=0, keepdims=True)

    blk = pl.BlockSpec((CH, DTW), lambda i: (i, 0))
    row = pl.BlockSpec((1, DTW), lambda i: (0, 0))
    gblk = pl.BlockSpec((NGB, CH, DTW), lambda i: (0, i, 0))
    return pl.pallas_call(
        body, name="dt_bwd", grid=(cfg.NT // CH,),
        in_specs=[gblk, gblk, gblk, gblk, blk, blk, row, row],
        out_specs=[blk, pl.BlockSpec((8, DTW), lambda i: (0, 0))],
        out_shape=[jax.ShapeDtypeStruct((cfg.NT, DTW), F32), jax.ShapeDtypeStruct((8, DTW), F32)],
        compiler_params=_cp(1),
    )(dAs[0], dAs[1], dxxs[0], dxxs[1], dt_raw, dt, dt_bias, a_log)


def _conv_shift(u, s, pos, R):
    n = u.shape[0]
    rolled = pltpu.roll(u, s % n, 0)
    ok = jnp.logical_and(pos - s >= 0, pos - s < R)
    return jnp.where(ok, rolled, 0.0)


_TAPS = (2, 1, 0, -1)


def _conv_fwd(cfg, proj, conv_w, conv_b):
    TB, nbt = cfg.TB, cfg.nbt
    CB = CONVW // 2
    n_act = DI + 2 * NG * NS

    def body(u_ref, w_ref, b_ref, o_ref):
        i = pl.program_id(0)
        j = pl.program_id(1)
        R = jnp.where(i % nbt == 0, cfg.Tc, GRID_W)
        t = lax.broadcasted_iota(jnp.int32, (TB, CB), 0)
        pos = jnp.bitwise_and(t, R - 1)
        u = u_ref[...]
        pre = b_ref[...] + w_ref[2:3, :] * u
        for k in (0, 1, 3):
            pre = pre + w_ref[k:k + 1, :] * _conv_shift(u, _TAPS[k], pos, R)
        col = j * CB + lax.broadcasted_iota(jnp.int32, (1, CB), 1)
        o_ref[...] = jnp.where(col < n_act, _silu(pre), pre)

    return pl.pallas_call(
        body, name="conv_fwd", grid=(cfg.NT // TB, CONVW // CB),
        in_specs=[pl.BlockSpec((TB, CB), lambda i, j: (i, j)),
                  pl.BlockSpec((4, CB), lambda i, j: (0, j)),
                  pl.BlockSpec((1, CB), lambda i, j: (0, j))],
        out_specs=pl.BlockSpec((TB, CB), lambda i, j: (i, j)),
        out_shape=jax.ShapeDtypeStruct((cfg.NT, CONVW), F32),
        compiler_params=_cp(2),
    )(proj, conv_w, conv_b)


_ANY = pl.BlockSpec(memory_space=pl.ANY)


def _conv_bwd(cfg, name, dproj, proj, conv_w, conv_b, addends, col0, width, act, skip=None, xch=None):
    TB, nbt, nbl = cfg.TB, cfg.nbt, cfg.nbl
    CB = 1024
    SUB = 256
    c0 = col0 // CB
    n_add = len(addends)

    def body(*refs):
        u_ref, w_ref, b_ref = refs[1:4]
        add_refs = refs[4:4 + n_add]
        rest = refs[4 + n_add:]
        if skip is not None:
            dy_ref, dv_ref = rest[:2]
            rest = rest[2:]
        o_ref, acc_ref = rest
        i = pl.program_id(1)

        @pl.when(i == 0)
        def _():
            acc_ref[...] = jnp.zeros_like(acc_ref)

        isctx = (i % nbt) == 0
        R = jnp.where(isctx, cfg.Tc, GRID_W)
        t = lax.broadcasted_iota(jnp.int32, (TB, SUB), 0)
        pos = jnp.bitwise_and(t, R - 1)
        for q in range(CB // SUB):
            sl = slice(q * SUB, (q + 1) * SUB)
            u = u_ref[:, sl]
            us = [_conv_shift(u, _TAPS[k], pos, R) for k in range(4)]
            g = add_refs[0][:, sl]
            for r in add_refs[1:]:
                g = g + r[:, sl]
            if skip is not None:
                g = g + jnp.where(isctx, 0.0, dv_ref[:, sl] * dy_ref[:, sl])
            if act:
                pre = b_ref[:, sl]
                for k in range(4):
                    pre = pre + w_ref[k:k + 1, sl] * us[k]
                g = g * _silu_grad(pre)
            dp = jnp.zeros_like(g)
            for k in range(4):
                acc_ref[k:k + 1, sl] += jnp.sum(g * us[k], axis=0, keepdims=True)
                dp = dp + w_ref[k:k + 1, sl] * _conv_shift(g, -_TAPS[k], pos, R)
            acc_ref[4:5, sl] += jnp.sum(g, axis=0, keepdims=True)
            o_ref[:, sl] = dp.astype(BF16)

    blk = pl.BlockSpec((TB, CB), lambda j, i: (i, j))
    in_specs = [_ANY,
                pl.BlockSpec((TB, CB), lambda j, i: (i, c0 + j)),
                pl.BlockSpec((4, CB), lambda j, i: (0, c0 + j)),
                pl.BlockSpec((1, CB), lambda j, i: (0, c0 + j))] + [blk] * n_add
    args = [dproj, proj, conv_w, conv_b] + list(addends)
    if skip is not None:
        def lat(j, i):
            b = i // nbt
            return (b * nbl + jnp.maximum(i % nbt - 1, 0), j)
        in_specs += [pl.BlockSpec((TB, CB), lat), pl.BlockSpec((1, CB), lambda j, i: (0, j))]
        args += list(skip)
    return _hosted_call(
        body, xch, name=name, grid=(width // CB, cfg.NT // TB),
        in_specs=in_specs,
        out_specs=[pl.BlockSpec((TB, CB), lambda j, i: (i, c0 + j)), pl.BlockSpec((8, CB), lambda j, i: (0, j))],
        out_shape=[jax.ShapeDtypeStruct((cfg.NT, PM), BF16), jax.ShapeDtypeStruct((8, width), F32)],
        scratch_shapes=[], compiler_params=_cp(2), args=args, aliases={0: 0})


def _chunk_of_step(cfg, rev):
    nct, nlt = cfg.nct, cfg.nlt
    if not rev:
        return lambda s: s
    return lambda s: jnp.where(s < nct, nct - 1 - s, 2 * nct + nlt - 1 - s)


def _expand4(v, band, base):
    out = v[:, base + 3:base + 4]
    for h in (2, 1, 0):
        out = jnp.where(band == h, v[:, base + h:base + h + 1], out)
    return out


def _ssd_step_tiles(dt_ref, cum_ref, cumT_ref, rev):
    cum_t = cum_ref[...]
    last = 0 if rev else CH - 1
    llast = cum_t[last:last + 1, :]
    return (dt_ref[...], cum_t, cumT_ref[...], llast, jnp.exp(llast), last)


def _ssd_common(gi, x_ref, b_ref, c_ref, tiles, rev):
    dt_t, cum_t, cumT_t, llast, elast, last = tiles
    base = gi * HPG
    xh = x_ref[:, gi * HPG * HD:(gi + 1) * HPG * HD]
    Bm = b_ref[:, gi * NS:(gi + 1) * NS].astype(BF16)
    Cm = c_ref[:, gi * NS:(gi + 1) * NS].astype(BF16)
    band = lax.broadcasted_iota(jnp.int32, (CH, HPG * HD), 1) // HD
    G = _dot_nt(Cm, Bm)
    ii = lax.broadcasted_iota(jnp.int32, (CH, CH), 0)
    jj = lax.broadcasted_iota(jnp.int32, (CH, CH), 1)
    mask = (jj >= ii) if rev else (jj <= ii)
    decs, cbs = [], []
    for h in range(HPG):
        cb = jnp.broadcast_to(cum_t[:, base + h:base + h + 1], (CH, CH))
        cbs.append(cb)
        decs.append(jnp.exp(jnp.where(mask, cb - cumT_t[base + h:base + h + 1, :], -1e30)))
    cum_exp = jnp.concatenate([cbs[3], cbs[3]], axis=1)
    ll_exp = llast[:, base + 3:base + 4]
    for h in (2, 1, 0):
        cum_exp = jnp.where(band == h, jnp.concatenate([cbs[h], cbs[h]], axis=1), cum_exp)
        ll_exp = jnp.where(band[:1] == h, llast[:, base + h:base + h + 1], ll_exp)
    ecum = jnp.exp(cum_exp)
    e_exp = jnp.exp(ll_exp - cum_exp)
    dt_exp = _expand4(dt_t, band, base)
    X = xh * dt_exp
    rb = lax.broadcasted_iota(jnp.int32, (HPG * HD, NS), 0) // HD
    dec_rows = elast[:, base + 3:base + 4]
    for h in (2, 1, 0):
        dec_rows = jnp.where(rb == h, elast[:, base + h:base + h + 1], dec_rows)
    return xh, Bm, Cm, band, e_exp, ecum, dt_exp, X, G, decs, elast, dec_rows, last


def _ssd_specs(cfg, rev):
    nch = cfg.nch
    cmap = _chunk_of_step(cfg, rev)
    d = 1 if rev else 0

    def make(stepmap):
        def row(b, g, sp):
            return b * nch + cmap(stepmap(sp))
        bo, co = DI // (GP * NS), (DI + NG * NS) // (GP * NS)
        return [
            pl.BlockSpec((CH, GP * HPG * HD), lambda b, g, sp: (row(b, g, sp), g)),
            pl.BlockSpec((CH, GP * NS), lambda b, g, sp: (row(b, g, sp), bo + g)),
            pl.BlockSpec((CH, GP * NS), lambda b, g, sp: (row(b, g, sp), co + g)),
            pl.BlockSpec((None, None, CH, DTW), lambda b, g, sp: (d, g, row(b, g, sp), 0)),
            pl.BlockSpec((None, None, CH, DTW), lambda b, g, sp: (d, g, row(b, g, sp), 0)),
            pl.BlockSpec((None, None, None, DTW, CH), lambda b, g, sp: (d, g, row(b, g, sp), 0, 0)),
        ], row
    return make


def _ssd_fwd(cfg, act, dtg, cumg, cumTg, rev):
    nch = cfg.nch
    in_specs, row = _ssd_specs(cfg, rev)(lambda sp: sp)

    def body(x_ref, b_ref, c_ref, dt_ref, cum_ref, cumT_ref, y_ref, hs_ref, h_scr):
        s = pl.program_id(2)

        @pl.when(s == 0)
        def _():
            h_scr[...] = jnp.zeros_like(h_scr)

        tiles = _ssd_step_tiles(dt_ref, cum_ref, cumT_ref, rev)
        for gi in range(GP):
            xh, Bm, Cm, band, e_exp, ecum, dt_exp, X, G, decs, elast, dec_rows, last = _ssd_common(
                gi, x_ref, b_ref, c_ref, tiles, rev)
            H = h_scr[gi]
            Mcat = jnp.concatenate([(G * decs[h]).astype(BF16) for h in range(HPG)], axis=1)
            Xbd = jnp.concatenate([jnp.where(band == h, X, 0.0).astype(BF16) for h in range(HPG)], axis=0)
            Y = ecum * _dot_nt(Cm, H.astype(BF16)) + _dot(Mcat, Xbd)
            y_ref[:, gi * HPG * HD:(gi + 1) * HPG * HD] = Y
            hs_ref[gi] = H
            S = _dot_tn((e_exp * X).astype(BF16), Bm)
            h_scr[gi] = dec_rows * H + S

    return pl.pallas_call(
        body, name="ssd_fwd_rev" if rev else "ssd_fwd", grid=(cfg.Bn, NG // GP, nch),
        in_specs=in_specs,
        out_specs=[pl.BlockSpec((CH, GP * HPG * HD), lambda b, g, s: (row(b, g, s), g)),
                   pl.BlockSpec((None, GP, None, HPG * HD, NS), lambda b, g, s: (b, g, s, 0, 0))],
        out_shape=[jax.ShapeDtypeStruct((cfg.NT, DI), F32),
                   jax.ShapeDtypeStruct((cfg.Bn, NG, nch, HPG * HD, NS), F32)],
        scratch_shapes=[pltpu.VMEM((GP, HPG * HD, NS), F32)],
        compiler_params=_cp(3),
    )(act, act, act, dtg, cumg, cumTg)


def _ssd_bwd(cfg, act, dtg, cumg, cumTg, hs, dy, rev, xch=None):
    nch, nct, nlt = cfg.nch, cfg.nct, cfg.nlt
    cmap = _chunk_of_step(cfg, rev)
    in_specs, row = _ssd_specs(cfg, rev)(lambda sp: nch - 1 - sp)

    def lat_row(b, g, sp):
        c = cmap(nch - 1 - sp)
        return b * nlt + jnp.maximum(c - nct, 0)

    def body(x_ref, b_ref, c_ref, dt_ref, cum_ref, cumT_ref, dy_ref, hs_ref,
             dxh_ref, dB_ref, dC_ref, dA_ref, dxx_ref, dh_scr):
        sp = pl.program_id(2)

        @pl.when(sp == 0)
        def _():
            dh_scr[...] = jnp.zeros_like(dh_scr)

        c = cmap(nch - 1 - sp)
        tiles = _ssd_step_tiles(dt_ref, cum_ref, cumT_ref, rev)
        dA_t = jnp.zeros((CH, DTW), F32)
        dAT_t = jnp.zeros((DTW, CH), F32)
        dxx_t = jnp.zeros((CH, DTW), F32)
        for gi in range(GP):
            dA_g, dAT_g, dxx_g = group_bwd(gi, c < nct, tiles, x_ref, b_ref, c_ref, dy_ref, hs_ref,
                                           dxh_ref, dB_ref, dC_ref, dh_scr)
            dA_t = dA_t + dA_g
            dAT_t = dAT_t + dAT_g
            dxx_t = dxx_t + dxx_g
        dA_ref[...] = dA_t - dAT_t.T
        dxx_ref[...] = dxx_t

    def group_bwd(gi, isctx, tiles, x_ref, b_ref, c_ref, dy_ref, hs_ref, dxh_ref, dB_ref, dC_ref, dh_scr):
        xsl = slice(gi * HPG * HD, (gi + 1) * HPG * HD)
        nsl = slice(gi * NS, (gi + 1) * NS)
        base = gi * HPG
        xh, Bm, Cm, band, e_exp, ecum, dt_exp, X, G, decs, elast, dec_rows, last = _ssd_common(
            gi, x_ref, b_ref, c_ref, tiles, rev)
        dY = jnp.where(isctx, 0.0, dy_ref[:, xsl])
        H = hs_ref[gi]
        Hb = H.astype(BF16)
        dHn = dh_scr[gi]
        dHnb = dHn.astype(BF16)
        dYs = ecum * dY
        dYsb = dYs.astype(BF16)
        BdH = _dot_nt(Bm, dHnb)
        Ys = ecum * _dot_nt(Cm, Hb)
        dX = e_exp * BdH
        dG = jnp.zeros((CH, CH), F32)
        lanei = lax.broadcasted_iota(jnp.int32, (CH, DTW), 1)
        subi = lax.broadcasted_iota(jnp.int32, (DTW, CH), 0)
        dA = jnp.zeros((CH, DTW), F32)
        dAT = jnp.zeros((DTW, CH), F32)
        Xbd = jnp.concatenate([jnp.where(band == h, X, 0.0).astype(BF16) for h in range(HPG)], axis=0)
        dYbd = jnp.concatenate([jnp.where(band == h, dY, 0.0).astype(BF16) for h in range(HPG)], axis=0)
        dMcat = _dot_nt(dY.astype(BF16), Xbd)
        Ms = []
        for h in range(HPG):
            M = G * decs[h]
            dM = dMcat[:, h * CH:(h + 1) * CH]
            W = dM * M
            dG = dG + dM * decs[h]
            Ms.append(M.astype(BF16))
            dA = dA + jnp.where(lanei == base + h, jnp.sum(W, axis=1, keepdims=True), 0.0)
            dAT = dAT + jnp.where(subi == base + h, jnp.sum(W, axis=0, keepdims=True), 0.0)
        dX = dX + _dot_tn(jnp.concatenate(Ms, axis=0), dYbd)
        dGb = dG.astype(BF16)
        eX = e_exp * X
        dC_ref[:, nsl] = _dot(dGb, Bm) + _dot(dYsb, Hb)
        dB_ref[:, nsl] = _dot_tn(dGb, Cm) + _dot(eX.astype(BF16), dHnb)
        dh_scr[gi] = dec_rows * dHn + _dot_tn(dYsb, Cm)
        pb = lax.broadcasted_iota(jnp.int32, (HPG * HD, NS), 0) // HD
        pl_ = lax.broadcasted_iota(jnp.int32, (HPG * HD, NS), 1)
        E = jnp.where(pb + base == pl_, 1.0, 0.0).astype(BF16)
        t2 = _dot_hilo_r(dY * Ys, E)
        q = _dot_hilo_r(eX * BdH, E)
        r = jnp.sum(dHn * H, axis=1, keepdims=True)
        lane1 = lax.broadcasted_iota(jnp.int32, (1, DTW), 1)
        hdot = jnp.zeros((1, DTW), F32)
        for h in range(HPG):
            hv = jnp.sum(r[h * HD:(h + 1) * HD, :], axis=0, keepdims=True)
            hdot = hdot + jnp.where(lane1 == base + h, hv, 0.0)
        dllast = jnp.sum(q, axis=0, keepdims=True) + elast * hdot
        rowi = lax.broadcasted_iota(jnp.int32, (CH, DTW), 0)
        dxh_ref[:, xsl] = dX * dt_exp
        return dA + t2 - q + jnp.where(rowi == last, dllast, 0.0), dAT, _dot_hilo_r(dX * xh, E)

    small = pl.BlockSpec((None, CH, DTW), lambda b, g, sp: (g, row(b, g, sp), 0))
    return _hosted_call(
        body, xch, name="ssd_bwd_rev" if rev else "ssd_bwd", grid=(cfg.Bn, NG // GP, nch),
        in_specs=in_specs + [
            pl.BlockSpec((CH, GP * HPG * HD), lambda b, g, sp: (lat_row(b, g, sp), g)),
            pl.BlockSpec((None, GP, None, HPG * HD, NS), lambda b, g, sp: (b, g, nch - 1 - sp, 0, 0))],
        out_specs=[pl.BlockSpec((CH, GP * HPG * HD), lambda b, g, sp: (row(b, g, sp), g)),
                   pl.BlockSpec((CH, GP * NS), lambda b, g, sp: (row(b, g, sp), g)),
                   pl.BlockSpec((CH, GP * NS), lambda b, g, sp: (row(b, g, sp), g)),
                   small, small],
        out_shape=[jax.ShapeDtypeStruct((cfg.NT, DI), F32),
                   jax.ShapeDtypeStruct((cfg.NT, NG * NS), F32),
                   jax.ShapeDtypeStruct((cfg.NT, NG * NS), F32),
                   jax.ShapeDtypeStruct((NGB, cfg.NT, DTW), F32),
                   jax.ShapeDtypeStruct((NGB, cfg.NT, DTW), F32)],
        scratch_shapes=[pltpu.VMEM((GP, HPG * HD, NS), F32)],
        compiler_params=_cp(3), args=(act, act, act, dtg, cumg, cumTg, dy, hs))


def _shift_rows(v, s, fill, toward_later, rowi):
    n = v.shape[0]
    if toward_later:
        return jnp.where(rowi >= s, pltpu.roll(v, s, 0), fill)
    return jnp.where(rowi < n - s, pltpu.roll(v, n - s, 0), fill)


def _chunk_scan(a, b, carry, later):
    nt = a.shape[0] // 8
    rowi = lax.broadcasted_iota(jnp.int32, (8, a.shape[1]), 0)
    outs = [None] * nt
    for r in (range(nt) if later else range(nt - 1, -1, -1)):
        av = a[r * 8:(r + 1) * 8]
        bv = b[r * 8:(r + 1) * 8]
        for sh in (1, 2, 4):
            a_p = _shift_rows(av, sh, 1.0, later, rowi)
            b_p = _shift_rows(bv, sh, 0.0, later, rowi)
            bv = av * b_p + bv
            av = av * a_p
        h = bv + av * carry
        outs[r] = h
        carry = h[7:8] if later else h[0:1]
    return jnp.concatenate(outs, axis=0), carry


def _lru_gates(u, wa_ref, wi_ref, ba_ref, bi_ref, lam_ref):
    rs, is_ = [], []
    for k in range(LB):
        uk = u[:, k * LBW:(k + 1) * LBW].astype(BF16)
        rs.append(_dot(uk, wa_ref[k].astype(BF16)))
        is_.append(_dot(uk, wi_ref[k].astype(BF16)))
    r = _sigmoid(jnp.concatenate(rs, axis=1) + ba_ref[...])
    ig = _sigmoid(jnp.concatenate(is_, axis=1) + bi_ref[...])
    sp = _softplus(-lam_ref[...])
    la = -LRU_C * r * sp
    a = jnp.exp(la)
    g = jnp.sqrt(_neg_expm1(2.0 * la))
    return r, ig, sp, la, a, g


def _lru_w_specs(d):
    return [pl.BlockSpec((None, LB, LBW, LBW), lambda b, s: (d, 0, 0, 0)),
            pl.BlockSpec((None, LB, LBW, LBW), lambda b, s: (d, 0, 0, 0)),
            pl.BlockSpec((None, 1, LW), lambda b, s: (d, 0, 0)),
            pl.BlockSpec((None, 1, LW), lambda b, s: (d, 0, 0)),
            pl.BlockSpec((None, 1, LW), lambda b, s: (d, 0, 0))]


def _lru_fwd(cfg, act, wa, wi, ba, bi, lam, rev):
    nch = cfg.nch
    cmap = _chunk_of_step(cfg, rev)
    d = 1 if rev else 0
    ucol = (DI + 2 * NG * NS) // LW

    def body(u_ref, wa_ref, wi_ref, ba_ref, bi_ref, lam_ref, h_ref, c_scr):
        s = pl.program_id(1)

        @pl.when(s == 0)
        def _():
            c_scr[...] = jnp.zeros_like(c_scr)

        u = u_ref[...]
        r, ig, sp, la, a, g = _lru_gates(u, wa_ref, wi_ref, ba_ref, bi_ref, lam_ref)
        h, carry = _chunk_scan(a, g * ig * u, c_scr[0:1, :], not rev)
        h_ref[...] = h
        c_scr[0:1, :] = carry

    return pl.pallas_call(
        body, name="lru_fwd_rev" if rev else "lru_fwd", grid=(cfg.Bn, nch),
        in_specs=[pl.BlockSpec((CH, LW), lambda b, s: (b * nch + cmap(s), ucol))] + _lru_w_specs(d),
        out_specs=pl.BlockSpec((CH, LW), lambda b, s: (b * nch + cmap(s), 0)),
        out_shape=jax.ShapeDtypeStruct((cfg.NT, LW), F32),
        scratch_shapes=[pltpu.VMEM((8, LW), F32)],
        compiler_params=_cp(2),
    )(act, wa, wi, ba, bi, lam)


def _lru_bwd(cfg, act, wa, wi, ba, bi, lam, hd, dyl, rev):
    nch, nct, nlt = cfg.nch, cfg.nct, cfg.nlt
    cmap = _chunk_of_step(cfg, rev)
    d = 1 if rev else 0
    ucol = (DI + 2 * NG * NS) // LW

    def srow(b, sp):
        return b * nch + cmap(nch - 1 - sp)

    def prev_rows(b, sp):
        s = nch - 1 - sp
        cp = cmap(jnp.maximum(s - 1, 0))
        base = (b * nch + cp) * (CH // 8)
        return base + (0 if rev else CH // 8 - 1)

    def lat_row(b, sp):
        c = cmap(nch - 1 - sp)
        return b * nlt + jnp.maximum(c - nct, 0)

    def body(u_ref, wa_ref, wi_ref, ba_ref, bi_ref, lam_ref, h_ref, hp_ref, dy_ref,
             du_ref, dwa_ref, dwi_ref, vec_ref, c_scr):
        b = pl.program_id(0)
        sp_id = pl.program_id(1)
        s = nch - 1 - sp_id

        @pl.when(sp_id == 0)
        def _():
            c_scr[...] = jnp.zeros_like(c_scr)

        @pl.when(jnp.logical_and(b == 0, sp_id == 0))
        def _():
            dwa_ref[...] = jnp.zeros_like(dwa_ref)
            dwi_ref[...] = jnp.zeros_like(dwi_ref)
            vec_ref[...] = jnp.zeros_like(vec_ref)

        c = cmap(s)
        u = u_ref[...]
        r, ig, spl, la, a, g = _lru_gates(u, wa_ref, wi_ref, ba_ref, bi_ref, lam_ref)
        dh = jnp.where(c < nct, 0.0, dy_ref[...])
        rowi = lax.broadcasted_iota(jnp.int32, (CH, LW), 0)
        lamv, _ = _chunk_scan(_shift_rows(a, 1, 1.0, rev, rowi), dh, c_scr[0:1, :], rev)
        first = CH - 1 if rev else 0
        c_scr[0:1, :] = (a * lamv)[first:first + 1, :]
        hprow = hp_ref[...][(0 if rev else 7):(1 if rev else 8), :]
        hprow = jnp.where(s > 0, hprow, 0.0)
        h_prev = _shift_rows(h_ref[...], 1, hprow, not rev, rowi)
        da = lamv * h_prev
        db = lamv
        iu = ig * u
        dla = da * a - db * iu * (a * a) / g
        dr = dla * (-LRU_C * spl)
        di = db * g * u
        du = db * g * ig
        drp = dr * r * (1.0 - r)
        dip = di * ig * (1.0 - ig)
        dus = []
        for k in range(LB):
            sl = slice(k * LBW, (k + 1) * LBW)
            drk = drp[:, sl].astype(BF16)
            dik = dip[:, sl].astype(BF16)
            uk = u[:, sl].astype(BF16)
            dus.append(_dot_nt(drk, wa_ref[k].astype(BF16)) + _dot_nt(dik, wi_ref[k].astype(BF16)))
            dwa_ref[k] += _dot_tn(uk, drk)
            dwi_ref[k] += _dot_tn(uk, dik)
        du_ref[...] = du + jnp.concatenate(dus, axis=1)
        vec_ref[0:1, :] += jnp.sum(drp, axis=0, keepdims=True)
        vec_ref[1:2, :] += jnp.sum(dip, axis=0, keepdims=True)
        dsp = jnp.sum(dla * (-LRU_C * r), axis=0, keepdims=True)
        vec_ref[2:3, :] += dsp * (-_sigmoid(-lam_ref[...]))

    return pl.pallas_call(
        body, name="lru_bwd_rev" if rev else "lru_bwd", grid=(cfg.Bn, nch),
        in_specs=[pl.BlockSpec((CH, LW), lambda b, sp: (srow(b, sp), ucol))] + _lru_w_specs(d) + [
            pl.BlockSpec((CH, LW), lambda b, sp: (srow(b, sp), 0)),
            pl.BlockSpec((8, LW), lambda b, sp: (prev_rows(b, sp), 0)),
            pl.BlockSpec((CH, LW), lambda b, sp: (lat_row(b, sp), 0))],
        out_specs=[pl.BlockSpec((CH, LW), lambda b, sp: (srow(b, sp), 0)),
                   pl.BlockSpec((LB, LBW, LBW), lambda b, sp: (0, 0, 0)),
                   pl.BlockSpec((LB, LBW, LBW), lambda b, sp: (0, 0, 0)),
                   pl.BlockSpec((8, LW), lambda b, sp: (0, 0))],
        out_shape=[jax.ShapeDtypeStruct((cfg.NT, LW), F32),
                   jax.ShapeDtypeStruct((LB, LBW, LBW), F32),
                   jax.ShapeDtypeStruct((LB, LBW, LBW), F32),
                   jax.ShapeDtypeStruct((8, LW), F32)],
        scratch_shapes=[pltpu.VMEM((8, LW), F32)],
        compiler_params=_cp(2),
    )(act, wa, wi, ba, bi, lam, hd, hd, dyl)


HB = 1024


def _post_ssd_fwd(cfg, yf, yb, act, proj, dvec, norm_w):
    TB, nbt, nbl = cfg.TB, cfg.nbt, cfg.nbl
    zc = CONVW // HB

    def body(yf_ref, yb_ref, xh_ref, z_ref, dv_ref, w_ref, o_ref):
        y = yf_ref[...] + yb_ref[...] + dv_ref[...] * xh_ref[...]
        u = y * _silu(z_ref[...])
        for gi in range(HB // (DI // NG)):
            sl = slice(gi * 256, (gi + 1) * 256)
            ug = u[:, sl]
            rs = lax.rsqrt(jnp.mean(ug * ug, axis=1, keepdims=True) + RMS_EPS)
            o_ref[:, sl] = (ug * rs * w_ref[:, sl]).astype(BF16)

    def st(b, j, cb):
        return (b * nbt + 1 + j, cb)
    return pl.pallas_call(
        body, name="post_ssd_fwd", grid=(cfg.Bn, nbl, DI // HB),
        in_specs=[pl.BlockSpec((TB, HB), st), pl.BlockSpec((TB, HB), st), pl.BlockSpec((TB, HB), st),
                  pl.BlockSpec((TB, HB), lambda b, j, cb: (b * nbt + 1 + j, zc + cb)),
                  pl.BlockSpec((1, HB), lambda b, j, cb: (0, cb)),
                  pl.BlockSpec((1, HB), lambda b, j, cb: (0, cb))],
        out_specs=pl.BlockSpec((TB, HB), lambda b, j, cb: (b * nbl + j, cb)),
        out_shape=jax.ShapeDtypeStruct((cfg.N, DI), BF16),
        compiler_params=_cp(3),
    )(yf, yb, act, proj, dvec, norm_w)


def _post_ssd_bwd(cfg, dproj, dn, yf, yb, act, proj, dvec, norm_w):
    TB, nbt, nbl = cfg.TB, cfg.nbt, cfg.nbl
    zc = CONVW // HB

    def body(_, dn_ref, yf_ref, yb_ref, xh_ref, z_ref, dv_ref, w_ref, dy_ref, dz_ref, acc_ref):
        b = pl.program_id(1)
        j = pl.program_id(2)

        @pl.when(jnp.logical_and(b == 0, j == 0))
        def _():
            acc_ref[...] = jnp.zeros_like(acc_ref)

        @pl.when(j == 0)
        def _():
            dz_ref[...] = jnp.zeros_like(dz_ref)

        @pl.when(j > 0)
        def _():
            latent(dn_ref, yf_ref, yb_ref, xh_ref, z_ref, dv_ref, w_ref, dy_ref, dz_ref, acc_ref)

    def latent(dn_ref, yf_ref, yb_ref, xh_ref, z_ref, dv_ref, w_ref, dy_ref, dz_ref, acc_ref):
        xh = xh_ref[...]
        z = z_ref[...]
        y = yf_ref[...] + yb_ref[...] + dv_ref[...] * xh
        sz = _silu(z)
        u = y * sz
        dout = dn_ref[...]
        for gi in range(HB // (DI // NG)):
            sl = slice(gi * 256, (gi + 1) * 256)
            ug0 = u[:, sl]
            rs = lax.rsqrt(jnp.mean(ug0 * ug0, axis=1, keepdims=True) + RMS_EPS)
            ug = ug0 * rs
            do = dout[:, sl]
            acc_ref[0:1, sl] += jnp.sum(do * ug, axis=0, keepdims=True)
            dug = do * w_ref[:, sl]
            du = rs * (dug - ug * jnp.mean(dug * ug, axis=1, keepdims=True))
            dy = du * sz[:, sl]
            dy_ref[:, sl] = dy
            dz_ref[:, sl] = (du * y[:, sl] * _silu_grad(z[:, sl])).astype(BF16)
            acc_ref[1:2, sl] += jnp.sum(dy * xh[:, sl], axis=0, keepdims=True)

    def st(cb, b, j):
        return (b * nbt + j, cb)

    def la(cb, b, j):
        return (b * nbl + jnp.maximum(j - 1, 0), cb)
    return pl.pallas_call(
        body, name="post_ssd_bwd", grid=(DI // HB, cfg.Bn, nbt),
        in_specs=[_ANY, pl.BlockSpec((TB, HB), la), pl.BlockSpec((TB, HB), st), pl.BlockSpec((TB, HB), st),
                  pl.BlockSpec((TB, HB), st),
                  pl.BlockSpec((TB, HB), lambda cb, b, j: (b * nbt + j, zc + cb)),
                  pl.BlockSpec((1, HB), lambda cb, b, j: (0, cb)),
                  pl.BlockSpec((1, HB), lambda cb, b, j: (0, cb))],
        out_specs=[pl.BlockSpec((TB, HB), la),
                   pl.BlockSpec((TB, HB), lambda cb, b, j: (b * nbt + j, zc + cb)),
                   pl.BlockSpec((8, HB), lambda cb, b, j: (0, cb))],
        out_shape=[jax.ShapeDtypeStruct((cfg.N, DI), F32), jax.ShapeDtypeStruct((cfg.NT, PM), BF16),
                   jax.ShapeDtypeStruct((8, DI), F32)],
        input_output_aliases={0: 1},
        compiler_params=_cp(3),
    )(dproj, dn, yf, yb, act, proj, dvec, norm_w)


def _post_lru_fwd(cfg, hf, hb, proj):
    TB, nbt, nbl = cfg.TB, cfg.nbt, cfg.nbl
    gc = (CONVW + DI) // HB

    def body(hf_ref, hb_ref, g_ref, o_ref):
        o_ref[...] = ((hf_ref[...] + hb_ref[...]) * _gelu(g_ref[...])).astype(BF16)

    st = pl.BlockSpec((TB, HB), lambda b, j: (b * nbt + 1 + j, 0))
    return pl.pallas_call(
        body, name="post_lru_fwd", grid=(cfg.Bn, nbl),
        in_specs=[st, st, pl.BlockSpec((TB, HB), lambda b, j: (b * nbt + 1 + j, gc))],
        out_specs=pl.BlockSpec((TB, HB), lambda b, j: (b * nbl + j, 0)),
        out_shape=jax.ShapeDtypeStruct((cfg.N, LW), BF16),
        compiler_params=_cp(2),
    )(hf, hb, proj)


def _post_lru_bwd(cfg, dproj, dv, hf, hb, proj):
    TB, nbt, nbl = cfg.TB, cfg.nbt, cfg.nbl
    gc = (CONVW + DI) // HB

    def body(_, dv_ref, hf_ref, hb_ref, g_ref, dy_ref, dg_ref):
        j = pl.program_id(1)

        @pl.when(j == 0)
        def _():
            dg_ref[...] = jnp.zeros_like(dg_ref)

        @pl.when(j > 0)
        def _():
            gt = g_ref[...]
            dvv = dv_ref[...]
            dy_ref[...] = dvv * _gelu(gt)
            dg_ref[...] = (dvv * (hf_ref[...] + hb_ref[...]) * _gelu_grad(gt)).astype(BF16)

    st = pl.BlockSpec((TB, HB), lambda b, j: (b * nbt + j, 0))
    la = pl.BlockSpec((TB, HB), lambda b, j: (b * nbl + jnp.maximum(j - 1, 0), 0))
    gcol = pl.BlockSpec((TB, HB), lambda b, j: (b * nbt + j, gc))
    return pl.pallas_call(
        body, name="post_lru_bwd", grid=(cfg.Bn, nbt),
        in_specs=[_ANY, la, st, st, gcol],
        out_specs=[la, gcol],
        out_shape=[jax.ShapeDtypeStruct((cfg.N, LW), F32), jax.ShapeDtypeStruct((cfg.NT, PM), BF16)],
        input_output_aliases={0: 1},
        compiler_params=_cp(2),
    )(dproj, dv, hf, hb, proj)


def _merge_fwd(cfg, proj, b_gate, br_ssd, br_lru):
    TB, nbt, nbl = cfg.TB, cfg.nbt, cfg.nbl
    mc = (CONVW + DI + LW) // HB

    def body(ms_ref, ml_ref, bg_ref, bs_ref, bl_ref, o_ref):
        gs = _sigmoid(ms_ref[...] + bg_ref[:, :D])
        gl = _sigmoid(ml_ref[...] + bg_ref[:, D:])
        o_ref[...] = (gs * bs_ref[...] + gl * bl_ref[...]).astype(BF16)

    la = pl.BlockSpec((TB, D), lambda b, j: (b * nbl + j, 0))
    return pl.pallas_call(
        body, name="merge_fwd", grid=(cfg.Bn, nbl),
        in_specs=[pl.BlockSpec((TB, HB), lambda b, j: (b * nbt + 1 + j, mc)),
                  pl.BlockSpec((TB, HB), lambda b, j: (b * nbt + 1 + j, mc + 1)),
                  pl.BlockSpec((1, 2 * D), lambda b, j: (0, 0)), la, la],
        out_specs=la,
        out_shape=jax.ShapeDtypeStruct((cfg.N, D), BF16),
        compiler_params=_cp(2),
    )(proj, proj, b_gate, br_ssd, br_lru)


def _merge_bwd(cfg, dmix, proj, b_gate, br_ssd, br_lru):
    TB, nbt, nbl = cfg.TB, cfg.nbt, cfg.nbl
    mc = (CONVW + DI + LW) // HB

    def body(dm_ref, ms_ref, ml_ref, bg_ref, bs_ref, bl_ref, ds_ref, dl_ref, dmg_ref, acc_ref):
        b = pl.program_id(0)
        j = pl.program_id(1)

        @pl.when(jnp.logical_and(b == 0, j == 0))
        def _():
            acc_ref[...] = jnp.zeros_like(acc_ref)

        @pl.when(j == 0)
        def _():
            dmg_ref[...] = jnp.zeros_like(dmg_ref)

        @pl.when(j > 0)
        def _():
            latent(dm_ref, ms_ref, ml_ref, bg_ref, bs_ref, bl_ref, ds_ref, dl_ref, dmg_ref, acc_ref)

    def latent(dm_ref, ms_ref, ml_ref, bg_ref, bs_ref, bl_ref, ds_ref, dl_ref, dmg_ref, acc_ref):
        dm = dm_ref[...]
        gs = _sigmoid(ms_ref[...] + bg_ref[:, :D])
        gl = _sigmoid(ml_ref[...] + bg_ref[:, D:])
        ds_ref[...] = (dm * gs).astype(BF16)
        dl_ref[...] = (dm * gl).astype(BF16)
        dps = dm * bs_ref[...] * gs * (1.0 - gs)
        dpl = dm * bl_ref[...] * gl * (1.0 - gl)
        dmg_ref[:, :D] = dps.astype(BF16)
        dmg_ref[:, D:] = dpl.astype(BF16)
        acc_ref[0:1, :D] += jnp.sum(dps, axis=0, keepdims=True)
        acc_ref[0:1, D:] += jnp.sum(dpl, axis=0, keepdims=True)

    la = pl.BlockSpec((TB, D), lambda b, j: (b * nbl + jnp.maximum(j - 1, 0), 0))
    return pl.pallas_call(
        body, name="merge_bwd", grid=(cfg.Bn, nbt),
        in_specs=[la, pl.BlockSpec((TB, HB), lambda b, j: (b * nbt + j, mc)),
                  pl.BlockSpec((TB, HB), lambda b, j: (b * nbt + j, mc + 1)),
                  pl.BlockSpec((1, 2 * D), lambda b, j: (0, 0)), la, la],
        out_specs=[la, la, pl.BlockSpec((TB, 2 * D), lambda b, j: (b * nbt + j, mc // 2)),
                   pl.BlockSpec((8, 2 * D), lambda b, j: (0, 0))],
        out_shape=[jax.ShapeDtypeStruct((cfg.N, D), BF16), jax.ShapeDtypeStruct((cfg.N, D), BF16),
                   jax.ShapeDtypeStruct((cfg.NT, PM), BF16), jax.ShapeDtypeStruct((8, 2 * D), F32)],
        compiler_params=_cp(2),
    )(dmix, proj, proj, b_gate, br_ssd, br_lru)


def _resid1_fwd(cfg, xs, x_mix, gate1, shift2, scale2, ln1_g, ln1_b):
    TB, nbt, nbl = cfg.TB, cfg.nbt, cfg.nbl

    def body(x_ref, xm_ref, g1_ref, sh_ref, sc_ref, lg_ref, lb_ref, x1_ref, h2_ref):
        r1 = ALPHA * x_ref[...] + g1_ref[...] * xm_ref[...]
        xh, _ = _ln(r1)
        x1 = xh * lg_ref[...] + lb_ref[...]
        x1_ref[...] = x1
        xh2, _ = _ln(x1)
        h2_ref[...] = (xh2 * (1.0 + sc_ref[...]) + sh_ref[...]).astype(BF16)

    la = pl.BlockSpec((TB, D), lambda b, j: (b * nbl + j, 0))
    ex = pl.BlockSpec((None, 1, D), lambda b, j: (b, 0, 0))
    vec = pl.BlockSpec((1, D), lambda b, j: (0, 0))
    return pl.pallas_call(
        body, name="resid1_fwd", grid=(cfg.Bn, nbl),
        in_specs=[pl.BlockSpec((TB, D), lambda b, j: (b * nbt + 1 + j, 0)), la, ex, ex, ex, vec, vec],
        out_specs=[la, la],
        out_shape=[jax.ShapeDtypeStruct((cfg.N, D), F32), jax.ShapeDtypeStruct((cfg.N, D), BF16)],
        compiler_params=_cp(2),
    )(xs, x_mix, gate1, shift2, scale2, ln1_g, ln1_b)


def _resid1_bwd(cfg, dh2, dx1p, x1, xs, x_mix, gate1, scale2, ln1_g):
    TB, nbt, nbl = cfg.TB, cfg.nbt, cfg.nbl

    def body(dh2_ref, dx1p_ref, x1_ref, x_ref, xm_ref, g1_ref, sc_ref, lg_ref,
             dxm_ref, dxp_ref, ex_ref, gl_ref):
        b = pl.program_id(0)
        j = pl.program_id(1)

        @pl.when(j == 0)
        def _():
            ex_ref[...] = jnp.zeros_like(ex_ref)

        @pl.when(jnp.logical_and(b == 0, j == 0))
        def _():
            gl_ref[...] = jnp.zeros_like(gl_ref)

        dh2 = dh2_ref[...]
        xh2, rs2 = _ln(x1_ref[...])
        ex_ref[0:1, :] += jnp.sum(dh2, axis=0, keepdims=True)
        ex_ref[1:2, :] += jnp.sum(dh2 * xh2, axis=0, keepdims=True)
        dx1 = dx1p_ref[...] + _ln_bwd(dh2 * (1.0 + sc_ref[...]), xh2, rs2)
        xm = xm_ref[...]
        g1 = g1_ref[...]
        r1 = ALPHA * x_ref[...] + g1 * xm
        xh1, rs1 = _ln(r1)
        gl_ref[0:1, :] += jnp.sum(dx1 * xh1, axis=0, keepdims=True)
        gl_ref[1:2, :] += jnp.sum(dx1, axis=0, keepdims=True)
        dr1 = _ln_bwd(dx1 * lg_ref[...], xh1, rs1)
        ex_ref[2:3, :] += jnp.sum(dr1 * xm, axis=0, keepdims=True)
        dxm_ref[...] = (dr1 * g1).astype(BF16)
        dxp_ref[...] = ALPHA * dr1

    la = pl.BlockSpec((TB, D), lambda b, j: (b * nbl + j, 0))
    ex = pl.BlockSpec((None, 1, D), lambda b, j: (b, 0, 0))
    vec = pl.BlockSpec((1, D), lambda b, j: (0, 0))
    return pl.pallas_call(
        body, name="resid1_bwd", grid=(cfg.Bn, nbl),
        in_specs=[la, la, la, pl.BlockSpec((TB, D), lambda b, j: (b * nbt + 1 + j, 0)), la, ex, ex, vec],
        out_specs=[la, la, pl.BlockSpec((None, 8, D), lambda b, j: (b, 0, 0)),
                   pl.BlockSpec((8, D), lambda b, j: (0, 0))],
        out_shape=[jax.ShapeDtypeStruct((cfg.N, D), BF16), jax.ShapeDtypeStruct((cfg.N, D), F32),
                   jax.ShapeDtypeStruct((cfg.Bn, 8, D), F32), jax.ShapeDtypeStruct((8, D), F32)],
        compiler_params=_cp(2),
    )(dh2, dx1p, x1, xs, x_mix, gate1, scale2, ln1_g)


def _mlp_act_fwd(cfg, a1, b1):
    TB = cfg.TB

    def body(a_ref, b_ref, o_ref):
        v = jnp.maximum(a_ref[...] + b_ref[...], 0.0)
        o_ref[...] = (v * v).astype(BF16)

    blk = pl.BlockSpec((TB, MLP), lambda i: (i, 0))
    return pl.pallas_call(
        body, name="mlp_act_fwd", grid=(cfg.N // TB,),
        in_specs=[blk, pl.BlockSpec((1, MLP), lambda i: (0, 0))],
        out_specs=blk, out_shape=jax.ShapeDtypeStruct((cfg.N, MLP), BF16),
        compiler_params=_cp(1),
    )(a1, b1)


def _mlp_act_bwd(cfg, dact, a1, b1):
    TB = cfg.TB

    def body(d_ref, a_ref, b_ref, o_ref, acc_ref):
        i = pl.program_id(0)

        @pl.when(i == 0)
        def _():
            acc_ref[...] = jnp.zeros_like(acc_ref)

        da = d_ref[...] * 2.0 * jnp.maximum(a_ref[...] + b_ref[...], 0.0)
        o_ref[...] = da.astype(BF16)
        acc_ref[0:1, :] += jnp.sum(da, axis=0, keepdims=True)

    blk = pl.BlockSpec((TB, MLP), lambda i: (i, 0))
    return pl.pallas_call(
        body, name="mlp_act_bwd", grid=(cfg.N // TB,),
        in_specs=[blk, blk, pl.BlockSpec((1, MLP), lambda i: (0, 0))],
        out_specs=[blk, pl.BlockSpec((8, MLP), lambda i: (0, 0))],
        out_shape=[jax.ShapeDtypeStruct((cfg.N, MLP), BF16), jax.ShapeDtypeStruct((8, MLP), F32)],
        compiler_params=_cp(1),
    )(dact, a1, b1)


def _final_fwd_bwd(cfg, x1, mlp, b2, gate2, ln2_g, ln2_b, target):
    TB, nbl = cfg.TB, cfg.nbl

    def body(x1_ref, m_ref, b2_ref, g2_ref, lg_ref, lb_ref, t_ref, dm_ref, dx_ref, ex_ref, gl_ref):
        b = pl.program_id(0)
        j = pl.program_id(1)

        @pl.when(j == 0)
        def _():
            ex_ref[...] = jnp.zeros_like(ex_ref)

        @pl.when(jnp.logical_and(b == 0, j == 0))
        def _():
            gl_ref[...] = jnp.zeros_like(gl_ref)

        mv = m_ref[...] + b2_ref[...]
        g2 = g2_ref[...]
        r2 = ALPHA * x1_ref[...] + g2 * mv
        xh, rs = _ln(r2)
        lg = lg_ref[...]
        x2 = xh * lg + lb_ref[...]
        err = x2 - t_ref[...]
        ls = jnp.sum(jnp.sum(err * err, axis=1, keepdims=True), axis=0, keepdims=True) * (0.5 / D)
        gl_ref[3:4, :] += ls
        dx2 = err * (1.0 / D)
        gl_ref[0:1, :] += jnp.sum(dx2 * xh, axis=0, keepdims=True)
        gl_ref[1:2, :] += jnp.sum(dx2, axis=0, keepdims=True)
        dr2 = _ln_bwd(dx2 * lg, xh, rs)
        ex_ref[0:1, :] += jnp.sum(dr2 * mv, axis=0, keepdims=True)
        dmv = dr2 * g2
        gl_ref[2:3, :] += jnp.sum(dmv, axis=0, keepdims=True)
        dm_ref[...] = dmv.astype(BF16)
        dx_ref[...] = ALPHA * dr2

    la = pl.BlockSpec((TB, D), lambda b, j: (b * nbl + j, 0))
    ex = pl.BlockSpec((None, 1, D), lambda b, j: (b, 0, 0))
    vec = pl.BlockSpec((1, D), lambda b, j: (0, 0))
    return pl.pallas_call(
        body, name="final_fwd_bwd", grid=(cfg.Bn, nbl),
        in_specs=[la, la, vec, ex, vec, vec, la],
        out_specs=[la, la, pl.BlockSpec((None, 8, D), lambda b, j: (b, 0, 0)),
                   pl.BlockSpec((8, D), lambda b, j: (0, 0))],
        out_shape=[jax.ShapeDtypeStruct((cfg.N, D), BF16), jax.ShapeDtypeStruct((cfg.N, D), F32),
                   jax.ShapeDtypeStruct((cfg.Bn, 8, D), F32), jax.ShapeDtypeStruct((8, D), F32)],
        compiler_params=_cp(2),
    )(x1, mlp, b2, gate2, ln2_g, ln2_b, target)


def _ln_mod_bwd(cfg, dh_a, dh_b, xs, scale_tab, dxp):
    TB, nbt, nbl = cfg.TB, cfg.nbt, cfg.nbl

    def body(da_ref, db_ref, x_ref, sc_ref, dxp_ref, gx_ref, acc_ref):
        j = pl.program_id(1)

        @pl.when(j <= 1)
        def _():
            acc_ref[...] = jnp.zeros_like(acc_ref)

        dh = da_ref[...] + db_ref[...]
        xhat, rs = _ln(x_ref[...])
        acc_ref[0:1, :] += jnp.sum(dh, axis=0, keepdims=True)
        acc_ref[1:2, :] += jnp.sum(dh * xhat, axis=0, keepdims=True)
        gx_ref[...] = dxp_ref[...] + _ln_bwd(dh * (1.0 + sc_ref[...]), xhat, rs)

    st = pl.BlockSpec((TB, D), lambda b, j: (b * nbt + j, 0))
    la = pl.BlockSpec((TB, D), lambda b, j: (b * nbl + jnp.maximum(j - 1, 0), 0))
    return pl.pallas_call(
        body, name="ln_mod_bwd", grid=(cfg.Bn, nbt),
        in_specs=[st, st, st,
                  pl.BlockSpec((None, 1, D), lambda b, j: (2 * b + jnp.minimum(j, 1), 0, 0)), la],
        out_specs=[la, pl.BlockSpec((None, 8, D), lambda b, j: (2 * b + jnp.minimum(j, 1), 0, 0))],
        out_shape=[jax.ShapeDtypeStruct((cfg.N, D), F32), jax.ShapeDtypeStruct((2 * cfg.Bn, 8, D), F32)],
        compiler_params=_cp(2),
    )(dh_a, dh_b, xs, scale_tab, dxp)


def _perm_w_in(w_in):
    w_main = jnp.concatenate([w_in[:, 0:3072], w_in[:, 4160:5184], w_in[:, 3136:4160], w_in[:, 5184:10304]], axis=1)
    w_dt = jnp.pad(w_in[:, 3072:3136], ((0, 0), (0, DTW - 2 * NH)))
    return w_main, w_dt


def _unperm_w_in(dw_main, dw_dt):
    return jnp.concatenate([dw_main[:, 0:3072], dw_dt[:, :2 * NH], dw_main[:, 4096:5120],
                            dw_main[:, 3072:4096], dw_main[:, 5120:]], axis=1)


def _unpack_rest(rest_all):
    out, off = {}, 0
    for n, shp, axis in _BIG[1:]:
        shard_shape = (shp[0] // NDEV, shp[1]) if axis == 0 else (shp[0], shp[1] // NDEV)
        r = math.prod(shard_shape) // 1024
        out[n] = _from_slots(rest_all[:, off:off + r, :].reshape((NDEV,) + shard_shape), axis)
        off += r
    return out


def _local_step(cfg, x, ctx, target, m, mc, W, rest_payload):
    Bn, T, Tc = cfg.Bn, cfg.T, cfg.Tc
    NT, N = cfg.NT, cfg.N
    xs = jnp.concatenate([ctx, x], axis=1).reshape(NT, D)
    mch = [m[:, i * D:(i + 1) * D] for i in range(NMOD)]
    ctx_sh = jnp.broadcast_to(mc[None, :D], (Bn, D))
    ctx_sc = jnp.broadcast_to(mc[None, D:], (Bn, D))
    shift_tab = jnp.stack([ctx_sh, mch[0]], axis=1).reshape(2 * Bn, 1, D)
    scale_tab = jnp.stack([ctx_sc, mch[1]], axis=1).reshape(2 * Bn, 1, D)
    gate1 = mch[2].reshape(Bn, 1, D)
    shift2 = mch[3].reshape(Bn, 1, D)
    scale2 = mch[4].reshape(Bn, 1, D)
    gate2 = mch[5].reshape(Bn, 1, D)

    conv_w = jnp.concatenate([W["ssd_conv_w"], W["lru_conv_w"]], axis=1)
    conv_b = jnp.concatenate([W["ssd_conv_b"], W["lru_conv_b"]], axis=1)
    dt_bias = jnp.pad(W["ssd_dt_bias"].reshape(1, 2 * NH), ((0, 0), (0, DTW - 2 * NH)))
    a_log = jnp.pad(W["ssd_a_log"].reshape(1, 2 * NH), ((0, 0), (0, DTW - 2 * NH)))
    dvec = jnp.repeat(W["ssd_d"].reshape(NH), HD).reshape(1, DI)
    lba = W["lru_ba"].reshape(2, 1, LW)
    lbi = W["lru_bi"].reshape(2, 1, LW)
    llam = W["lru_lambda"].reshape(2, 1, LW)

    h = _ln_mod_fwd(cfg, xs, shift_tab, scale_tab)
    proj, rest_all = _mm(h, W["w_main"], "nn", "mm_proj", tm=1024, tn=1024, tk=1024, xch=(rest_payload, True))
    W = dict(W, **_unpack_rest(rest_all))
    dt_raw = _mm(h, W["w_dt"], "nn", "mm_dt", tm=512, tn=DTW, tk=1024)
    dt, dtg, cumg, cumTg = _dt_fwd(cfg, dt_raw, dt_bias, a_log)
    act = _conv_fwd(cfg, proj, conv_w, conv_b)
    ys, hss, hls = [], [], []
    for rev in (False, True):
        y_d, hs_d = _ssd_fwd(cfg, act, dtg, cumg, cumTg, rev)
        ys.append(y_d)
        hss.append(hs_d)
        hls.append(_lru_fwd(cfg, act, W["lru_wa"], W["lru_wi"], lba, lbi, llam, rev))
    nssd = _post_ssd_fwd(cfg, ys[0], ys[1], act, proj, dvec, W["ssd_norm_w"])
    vlru = _post_lru_fwd(cfg, hls[0], hls[1], proj)
    br_ssd = _mm(nssd, W["w_br_ssd"], "nn", "mm_br_ssd", tm=1024, tn=1024, tk=1024)
    br_lru = _mm(vlru, W["w_br_lru"], "nn", "mm_br_lru", tm=1024, tn=1024, tk=1024)
    mix = _merge_fwd(cfg, proj, W["b_gate"], br_ssd, br_lru)
    x_mix = _mm(mix, W["w_out"], "nn", "mm_out", tm=1024, tn=1024, tk=1024)
    x1, h2 = _resid1_fwd(cfg, xs, x_mix, gate1, shift2, scale2, W["ln1_g"], W["ln1_b"])
    a1 = _mm(h2, W["w_mlp1"], "nn", "mm_mlp1", tm=1024, tn=1024, tk=1024)
    actm = _mlp_act_fwd(cfg, a1, W["b_mlp1"])
    mlp = _mm(actm, W["w_mlp2"], "nn", "mm_mlp2", tm=1024, tn=1024, tk=1024)
    dmlp, dx1p, ex2, gl2 = _final_fwd_bwd(cfg, x1, mlp, W["b_mlp2"], gate2, W["ln2_g"], W["ln2_b"],
                                          target.reshape(N, D))

    g = {}
    g["ln2_g"], g["ln2_b"], g["b_mlp2"] = gl2[0:1], gl2[1:2], gl2[2:3]
    loss_partial = gl2[3, 0]
    gw = {}
    dact = _mm(dmlp, W["w_mlp2"], "nt", "mm_dact", tm=1024, tn=1024, tk=1024)
    gw["w_mlp2"] = _mm(actm, dmlp, "tn", "mm_dw_mlp2", BF16, tm=1024, tn=1024, tk=512)
    da1, accb1 = _mlp_act_bwd(cfg, dact, a1, W["b_mlp1"])
    g["b_mlp1"] = accb1[0:1]
    dh2 = _mm(da1, W["w_mlp1"], "nt", "mm_dh2", tm=1024, tn=1024, tk=1024)
    gw["w_mlp1"] = _mm(h2, da1, "tn", "mm_dw_mlp1", BF16, tm=1024, tn=1024, tk=512)
    dx_mix, dxp, ex1, gl1 = _resid1_bwd(cfg, dh2, dx1p, x1, xs, x_mix, gate1, scale2, W["ln1_g"])
    g["ln1_g"], g["ln1_b"] = gl1[0:1], gl1[1:2]
    dmix = _mm(dx_mix, W["w_out"], "nt", "mm_dmix", tm=1024, tn=1024, tk=1024)
    gw["w_out"] = _mm(mix, dx_mix, "tn", "mm_dw_out", BF16, tm=1024, tn=1024, tk=512)
    dbs, dbl, dproj, accg = _merge_bwd(cfg, dmix, proj, W["b_gate"], br_ssd, br_lru)
    g["b_gate"] = accg[0:1]
    dnssd = _mm(dbs, W["w_br_ssd"], "nt", "mm_dnssd", tm=1024, tn=1024, tk=1024)
    gw["w_br_ssd"] = _mm(nssd, dbs, "tn", "mm_dw_br_ssd", BF16, tm=1024, tn=1024, tk=512)
    dvlru = _mm(dbl, W["w_br_lru"], "nt", "mm_dvlru", tm=1024, tn=1024, tk=1024)
    gw["w_br_lru"] = _mm(vlru, dbl, "tn", "mm_dw_br_lru", BF16, tm=1024, tn=1024, tk=512)
    dy, dproj, accs = _post_ssd_bwd(cfg, dproj, dnssd, ys[0], ys[1], act, proj, dvec, W["ssd_norm_w"])
    g["ssd_norm_w"] = accs[0:1]
    dD_cols = accs[1:2]
    dyl, dproj = _post_lru_bwd(cfg, dproj, dvlru, hls[0], hls[1], proj)

    rest_slots = jnp.concatenate([_to_slots(gw[n], axis).reshape(NDEV, -1, 1024) for n, _, axis in _BIG[1:]], axis=1)
    xres = {}
    dxh, dBs, dCs, dAs, dxxs, dus = [], [], [], [], [], []
    dwas, dwis, lvecs = [], [], []
    for i, rev in enumerate((False, True)):
        if i == 0:
            o, xres["rs_rest"] = _ssd_bwd(cfg, act, dtg, cumg, cumTg, hss[i], dy, rev, xch=(rest_slots, False))
        else:
            o = _ssd_bwd(cfg, act, dtg, cumg, cumTg, hss[i], dy, rev)
        dxh.append(o[0]); dBs.append(o[1]); dCs.append(o[2]); dAs.append(o[3]); dxxs.append(o[4])
        du, dwa, dwi, lv = _lru_bwd(cfg, act, W["lru_wa"], W["lru_wi"], lba, lbi, llam, hls[i], dyl, rev)
        dus.append(du); dwas.append(dwa); dwis.append(dwi); lvecs.append(lv)
    lru_payload = jnp.stack([jnp.stack(dwas), jnp.stack(dwis)]).reshape(-1, 1024)
    g["lru_ba"] = jnp.stack([lvecs[0][0], lvecs[1][0]])
    g["lru_bi"] = jnp.stack([lvecs[0][1], lvecs[1][1]])
    g["lru_lambda"] = jnp.stack([lvecs[0][2], lvecs[1][2]])

    ddt_raw, accdt = _dt_bwd(cfg, dAs, dxxs, dt_raw, dt, dt_bias, a_log)
    g["ssd_a_log"] = accdt[0, :2 * NH].reshape(2, NH)
    g["ssd_dt_bias"] = accdt[1, :2 * NH].reshape(2, NH)

    (dproj, accx), xres["ag_lru"] = _conv_bwd(cfg, "conv_bwd_x", dproj, proj, conv_w, conv_b, [dxh[0], dxh[1]], 0, DI,
                                              True, skip=(dy, dvec), xch=(lru_payload, True))
    dproj, accB = _conv_bwd(cfg, "conv_bwd_b", dproj, proj, conv_w, conv_b, [dBs[0], dBs[1]], DI, NG * NS, True)
    dproj, accC = _conv_bwd(cfg, "conv_bwd_c", dproj, proj, conv_w, conv_b, [dCs[0], dCs[1]], DI + NG * NS, NG * NS,
                            True)
    dproj, accl = _conv_bwd(cfg, "conv_bwd_lru", dproj, proj, conv_w, conv_b, [dus[0], dus[1]], DI + 2 * NG * NS, LW,
                            False)
    accssd = jnp.concatenate([accx, accB, accC], axis=1)
    g["ssd_conv_w"], g["ssd_conv_b"] = accssd[0:4], accssd[4:5]
    g["lru_conv_w"], g["lru_conv_b"] = accl[0:4], accl[4:5]
    dw_main = _mm(h, dproj, "tn", "mm_dw_main", BF16, tm=1024, tn=2048, tk=512)
    dw_dt = _mm(h, ddt_raw, "tn", "mm_dw_dt", BF16, tm=1024, tn=DTW, tk=512)
    w_in_slots = _to_slots(_unperm_w_in(dw_main, dw_dt), 1)
    dh_a, xres["rs_w_in"] = _mm(dproj, W["w_main"], "nt", "mm_dh_main", tm=1024, tn=1024, tk=1024,
                                xch=(w_in_slots, False))
    dh_b = _mm(ddt_raw, W["w_dt"], "nt", "mm_dh_dt", tm=512, tn=1024, tk=DTW)
    grad_x, acct = _ln_mod_bwd(cfg, dh_a, dh_b, xs, scale_tab, dxp)
    acct = acct.reshape(Bn, 2, 8, D)
    dm = jnp.concatenate([acct[:, 1, 0], acct[:, 1, 1], ex1[:, 2], ex1[:, 0], ex1[:, 1], ex2[:, 0]], axis=1)
    dmc = jnp.concatenate([acct[:, 0, 0], acct[:, 0, 1]], axis=1)
    g["ssd_d_cols"] = dD_cols
    return loss_partial, grad_x.reshape(Bn, T, D), g, dm, dmc, xres


MESH = pl.DeviceIdType.MESH
_HBM = pl.BlockSpec(memory_space=pltpu.HBM)


def _me():
    return 4 * lax.axis_index("x") + 2 * lax.axis_index("y") + lax.axis_index("c")


def _peer(k):
    px = (lax.axis_index("x") + ((k >> 2) & 1)) % 2
    py = (lax.axis_index("y") + ((k >> 1) & 1)) % 2
    pc = (lax.axis_index("c") + (k & 1)) % 2
    return (px, py, pc), 4 * px + 2 * py + pc


def _xchg_copies(x_ref, o_ref, send_sems, recv_sems, loc_sem, gather):
    me = _me()
    src_me = x_ref if gather else x_ref.at[me]
    loc = pltpu.make_async_copy(src_me, o_ref.at[me], loc_sem)
    sends, recvs = [], []
    for k in range(1, NDEV):
        peer, pid = _peer(k)
        sends.append(pltpu.make_async_remote_copy(
            src_ref=x_ref if gather else x_ref.at[pid], dst_ref=o_ref.at[me],
            send_sem=send_sems.at[k - 1], recv_sem=recv_sems.at[k - 1],
            device_id=peer, device_id_type=MESH))
        recvs.append(pltpu.make_async_remote_copy(
            src_ref=src_me, dst_ref=o_ref.at[pid],
            send_sem=send_sems.at[k - 1], recv_sem=recv_sems.at[k - 1],
            device_id=peer, device_id_type=MESH))
    return loc, sends, recvs


def _xchg_start(*refs, gather):
    loc, sends, _ = _xchg_copies(*refs, gather)
    loc.start()
    for cp in sends:
        cp.start()


def _xchg_wait(*refs, gather):
    loc, sends, recvs = _xchg_copies(*refs, gather)
    for cp in recvs:
        cp.wait_recv()
    for cp in sends:
        cp.wait_send()
    loc.wait()


_XCHG_SCRATCH = [pltpu.SemaphoreType.DMA((NDEV - 1,)), pltpu.SemaphoreType.DMA((NDEV - 1,)), pltpu.SemaphoreType.DMA]


def _xchg_out_shape(x, gather):
    return jax.ShapeDtypeStruct((NDEV,) + tuple(x.shape if gather else x.shape[1:]), x.dtype)


def _exchange(x, name, gather):
    def body(x_ref, o_ref, send_sems, recv_sems, loc_sem):
        _xchg_start(x_ref, o_ref, send_sems, recv_sems, loc_sem, gather=gather)
        _xchg_wait(x_ref, o_ref, send_sems, recv_sems, loc_sem, gather=gather)

    return pl.pallas_call(
        body, name=name, out_shape=_xchg_out_shape(x, gather),
        in_specs=[_HBM], out_specs=_HBM, scratch_shapes=_XCHG_SCRATCH,
    )(x)


def _gather_two_level(x, name):
    def body(x_ref, o_ref, send_sems, recv_sems, loc_sem):
        mx, my, mc = lax.axis_index("x"), lax.axis_index("y"), lax.axis_index("c")
        me, sibling = (mx, my, mc), (mx, my, 1 - mc)
        chips = [(1 - mx, my), (mx, 1 - my), (1 - mx, 1 - my)]

        def slot(px, py, pc):
            return o_ref.at[4 * px + 2 * py + pc]

        def copy(k, block, to, src=None):
            return pltpu.make_async_remote_copy(
                src_ref=slot(*block) if src is None else src, dst_ref=slot(*block),
                send_sem=send_sems.at[k], recv_sem=recv_sems.at[k], device_id=to, device_id_type=MESH)

        mine = pltpu.make_async_copy(x_ref, slot(*me), loc_sem)
        mine.start()
        first = [copy(0, me, sibling, src=x_ref)]
        first += [copy(1 + j, me, (*chip, mc), src=x_ref) for j, chip in enumerate(chips)]
        for cp in first:
            cp.start()
        passed = [copy(4 + j, (*chip, mc), sibling) for j, chip in enumerate(chips)]
        for j, chip in enumerate(chips):
            copy(1 + j, (*chip, mc), me).wait_recv()
            passed[j].start()
        copy(0, sibling, me).wait_recv()
        for j, chip in enumerate(chips):
            copy(4 + j, (*chip, 1 - mc), me).wait_recv()
        for cp in first + passed:
            cp.wait_send()
        mine.wait()

    return pl.pallas_call(
        body, name=name, out_shape=_xchg_out_shape(x, True),
        in_specs=[_HBM], out_specs=_HBM, scratch_shapes=_XCHG_SCRATCH,
    )(x)


def _hosted_call(body, xch, *, name, grid, in_specs, out_specs, out_shape, scratch_shapes, compiler_params, args,
                 aliases=None):
    aliases = aliases or {}
    if xch is None:
        return pl.pallas_call(body, name=name, grid=grid, in_specs=in_specs, out_specs=out_specs,
                              out_shape=out_shape, scratch_shapes=scratch_shapes, input_output_aliases=aliases,
                              compiler_params=compiler_params)(*args)
    xv, gather = xch
    n_in, n_out, n_scr = len(in_specs), len(out_specs), len(scratch_shapes)

    def wrapped(*refs):
        ins = refs[:n_in]
        x_ref = refs[n_in]
        outs = refs[n_in + 1:n_in + 1 + n_out]
        o_ref = refs[n_in + 1 + n_out]
        scr = refs[n_in + 2 + n_out:]
        own, sems = scr[:n_scr], scr[n_scr:]
        first = functools.reduce(jnp.logical_and, [pl.program_id(a) == 0 for a in range(len(grid))])
        last = functools.reduce(jnp.logical_and, [pl.program_id(a) == grid[a] - 1 for a in range(len(grid))])

        @pl.when(first)
        def _():
            _xchg_start(x_ref, o_ref, *sems, gather=gather)

        body(*ins, *outs, *own)

        @pl.when(last)
        def _():
            _xchg_wait(x_ref, o_ref, *sems, gather=gather)

    res = pl.pallas_call(
        wrapped, name=name, grid=grid, in_specs=list(in_specs) + [_HBM], out_specs=list(out_specs) + [_HBM],
        out_shape=list(out_shape) + [_xchg_out_shape(xv, gather)],
        scratch_shapes=list(scratch_shapes) + _XCHG_SCRATCH, input_output_aliases=aliases,
        compiler_params=compiler_params,
    )(*args, xv)
    return list(res[:n_out]), res[n_out]


def _row_tile(R, cap, mult=8):
    best = mult
    t = mult
    while t <= min(R, cap):
        if R % t == 0:
            best = t
        t += mult
    assert R % best == 0, R
    return best


def _sum_slots(x, name):
    _, R, C = x.shape
    tr = _row_tile(R, 256, 16 if x.dtype == BF16 else 8)

    def body(x_ref, o_ref):
        acc = x_ref[0].astype(F32)
        for i in range(1, NDEV):
            acc = acc + x_ref[i].astype(F32)
        o_ref[...] = acc

    return pl.pallas_call(
        body, name=name, grid=(R // tr,),
        in_specs=[pl.BlockSpec((NDEV, tr, C), lambda i: (0, i, 0))],
        out_specs=pl.BlockSpec((tr, C), lambda i: (i, 0)),
        out_shape=jax.ShapeDtypeStruct((R, C), F32),
        compiler_params=_cp(1),
    )(x)


def _adamw_update(w_ref, g_ref, m_ref, v_ref, d_ref, nm_ref, nv_ref):
    c1 = 1.0 / (1.0 - ADAM_B1 ** ADAM_STEP)
    c2 = 1.0 / (1.0 - ADAM_B2 ** ADAM_STEP)
    gv = g_ref[...]
    nm = ADAM_B1 * m_ref[...] + (1.0 - ADAM_B1) * gv
    nv = ADAM_B2 * v_ref[...] + (1.0 - ADAM_B2) * (gv * gv)
    d_ref[...] = -ADAM_LR * ((nm * c1) / (jnp.sqrt(nv * c2) + ADAM_EPS) + ADAM_WD * w_ref[...])
    nm_ref[...] = nm
    nv_ref[...] = nv


def _adamw_many(ws, gs, ms, vs):
    n = len(ws)

    def body(*refs):
        for i in range(n):
            _adamw_update(refs[i], refs[n + i], refs[2 * n + i], refs[3 * n + i],
                          refs[4 * n + i], refs[5 * n + i], refs[6 * n + i])

    shapes = [jax.ShapeDtypeStruct(w.shape, F32) for w in ws]
    res = pl.pallas_call(
        body, name="adamw_small", out_shape=shapes * 3,
        compiler_params=pltpu.CompilerParams(vmem_limit_bytes=VMEM_LIMIT_BYTES),
    )(*ws, *gs, *ms, *vs)
    return res[:n], res[n:2 * n], res[2 * n:]


def _adamw(w, g, m, v, name):
    R, C = w.shape
    tr = _row_tile(R, 256)

    def body(w_ref, g_ref, m_ref, v_ref, d_ref, nm_ref, nv_ref):
        _adamw_update(w_ref, g_ref, m_ref, v_ref, d_ref, nm_ref, nv_ref)

    blk = pl.BlockSpec((tr, C), lambda i: (i, 0))
    return pl.pallas_call(
        body, name=name, grid=(R // tr,),
        in_specs=[blk] * 4, out_specs=[blk] * 3,
        out_shape=[jax.ShapeDtypeStruct((R, C), F32)] * 3,
        compiler_params=_cp(1),
    )(w, g, m, v)


def _mod_fwd(c_rows, w_shard, b_shard):
    def body(c_ref, w_ref, b_ref, o_ref):
        s = _silu(c_ref[...]).astype(BF16)
        o_ref[...] = _dot(s, w_ref[...].astype(BF16)) + b_ref[...]

    return pl.pallas_call(
        body, name="mod_fwd",
        out_shape=jax.ShapeDtypeStruct((c_rows.shape[0], w_shard.shape[1]), F32),
        compiler_params=pltpu.CompilerParams(vmem_limit_bytes=VMEM_LIMIT_BYTES),
    )(c_rows, w_shard, b_shard)


def _mod_bwd(c_rows, dm_all, dm_shard, w_shard):
    nrow = c_rows.shape[0]

    def body(c_ref, da_ref, ds_ref, w_ref, gw_ref, gb_ref, cc_ref):
        s = _silu(c_ref[...]).astype(BF16)
        ds = ds_ref[...]
        gw_ref[...] = _dot_tn(s, ds.astype(BF16))
        gb_ref[...] = jnp.sum(da_ref[...], axis=0, keepdims=True)
        rowi = lax.broadcasted_iota(jnp.int32, ds.shape, 0)
        dmc = jnp.sum(jnp.where(rowi % 8 >= 4, ds, 0.0), axis=0, keepdims=True)
        dmc8 = jnp.broadcast_to(dmc, (8, ds.shape[1])).astype(BF16)
        cc_ref[...] = _dot_nt(dmc8, w_ref[...].astype(BF16))

    return pl.pallas_call(
        body, name="mod_bwd",
        out_shape=[jax.ShapeDtypeStruct(w_shard.shape, F32),
                   jax.ShapeDtypeStruct((1, dm_all.shape[1]), F32),
                   jax.ShapeDtypeStruct((8, D), F32)],
        compiler_params=pltpu.CompilerParams(vmem_limit_bytes=VMEM_LIMIT_BYTES),
    )(c_rows, dm_all, dm_shard, w_shard)


def _small_finish(cc_pre, c_ctx, dd_cols):
    def body(cc_ref, c_ref, dd_ref, gc_ref, gd_ref):
        gc_ref[...] = cc_ref[...] * _silu_grad(c_ref[...])
        gd_ref[...] = jnp.sum(dd_ref[...], axis=1, keepdims=True)

    return pl.pallas_call(
        body, name="small_finish",
        out_shape=[jax.ShapeDtypeStruct((1, D), F32), jax.ShapeDtypeStruct((NH, 1), F32)],
    )(cc_pre, c_ctx, dd_cols)


_BIG = (("w_in", (D, 10304), 1), ("w_br_ssd", (DI, D), 0), ("w_br_lru", (LW, D), 0), ("w_out", (D, D), 0),
        ("w_mlp1", (D, MLP), 1), ("w_mlp2", (MLP, D), 0))
_SMALL_SH = (("ssd_conv_w", (4, 4096)), ("lru_conv_w", (4, LW)), ("lru_ba", (2, LW)), ("lru_bi", (2, LW)),
             ("lru_lambda", (2, LW)))
_REPL = (("c_ctx", (D,)), ("b_gate", (2 * D,)), ("ssd_conv_b", (4096,)), ("ssd_dt_bias", (2, NH)),
         ("ssd_a_log", (2, NH)), ("ssd_d", (DI,)), ("ssd_norm_w", (DI,)), ("lru_conv_b", (LW,)),
         ("ln1_g", (D,)), ("ln1_b", (D,)),
         ("b_mlp1", (MLP,)), ("b_mlp2", (D,)), ("ln2_g", (D,)), ("ln2_b", (D,)))

_WEIGHT_NAMES = ('c_ctx', 'w_mod', 'b_mod', 'w_in', 'b_gate', 'ssd_conv_w', 'ssd_conv_b', 'ssd_dt_bias', 'ssd_a_log',
                 'ssd_d', 'ssd_norm_w', 'lru_conv_w', 'lru_conv_b', 'lru_wa', 'lru_ba', 'lru_wi', 'lru_bi',
                 'lru_lambda', 'w_br_ssd', 'w_br_lru', 'w_out', 'ln1_g', 'ln1_b', 'w_mlp1', 'b_mlp1', 'w_mlp2',
                 'b_mlp2', 'ln2_g', 'ln2_b')
_ARG_NAMES = ('x', 'c', 'ctx') + _WEIGHT_NAMES + ('loss_target',) + tuple('m_' + n for n in _WEIGHT_NAMES) + tuple(
    'v_' + n for n in _WEIGHT_NAMES)


def _to_slots(full, axis):
    n = full.shape[axis] // NDEV
    if axis == 0:
        return full.reshape(NDEV, n, full.shape[1])
    return full.reshape(full.shape[0], NDEV, n).transpose(1, 0, 2)


def _from_slots(slots, axis):
    if axis == 0:
        return slots.reshape(NDEV * slots.shape[1], slots.shape[2])
    return slots.transpose(1, 0, 2).reshape(slots.shape[1], NDEV * slots.shape[2])


def _pack_rows(arrs, width=1024, mult=8):
    flat = jnp.concatenate([a.reshape(-1) for a in arrs])
    n = flat.shape[0]
    per = width * mult
    tot = -(-n // per) * per
    return jnp.pad(flat, (0, tot - n)).reshape(tot // width, width)


def _unpack_rows(packed, shapes, lead=()):
    nl = len(lead)
    flat = packed.reshape(tuple(lead) + (-1,))
    out, off = [], 0
    for s in shapes:
        n = math.prod(s)
        out.append(flat[..., off:off + n].reshape(tuple(lead) + tuple(s)))
        off += n
    return out


def kernel(x, c, ctx, c_ctx, w_mod, b_mod, w_in, b_gate, ssd_conv_w, ssd_conv_b, ssd_dt_bias, ssd_a_log, ssd_d, ssd_norm_w, lru_conv_w, lru_conv_b, lru_wa, lru_ba, lru_wi, lru_bi, lru_lambda, w_br_ssd, w_br_lru, w_out, ln1_g, ln1_b, w_mlp1, b_mlp1, w_mlp2, b_mlp2, ln2_g, ln2_b, loss_target, m_c_ctx, m_w_mod, m_b_mod, m_w_in, m_b_gate, m_ssd_conv_w, m_ssd_conv_b, m_ssd_dt_bias, m_ssd_a_log, m_ssd_d, m_ssd_norm_w, m_lru_conv_w, m_lru_conv_b, m_lru_wa, m_lru_ba, m_lru_wi, m_lru_bi, m_lru_lambda, m_w_br_ssd, m_w_br_lru, m_w_out, m_ln1_g, m_ln1_b, m_w_mlp1, m_b_mlp1, m_w_mlp2, m_b_mlp2, m_ln2_g, m_ln2_b, v_c_ctx, v_w_mod, v_b_mod, v_w_in, v_b_gate, v_ssd_conv_w, v_ssd_conv_b, v_ssd_dt_bias, v_ssd_a_log, v_ssd_d, v_ssd_norm_w, v_lru_conv_w, v_lru_conv_b, v_lru_wa, v_lru_ba, v_lru_wi, v_lru_bi, v_lru_lambda, v_w_br_ssd, v_w_br_lru, v_w_out, v_ln1_g, v_ln1_b, v_w_mlp1, v_b_mlp1, v_w_mlp2, v_b_mlp2, v_ln2_g, v_ln2_b):
    A = dict(zip(_ARG_NAMES, (x, c, ctx, c_ctx, w_mod, b_mod, w_in, b_gate, ssd_conv_w, ssd_conv_b, ssd_dt_bias, ssd_a_log, ssd_d, ssd_norm_w, lru_conv_w, lru_conv_b, lru_wa, lru_ba, lru_wi, lru_bi, lru_lambda, w_br_ssd, w_br_lru, w_out, ln1_g, ln1_b, w_mlp1, b_mlp1, w_mlp2, b_mlp2, ln2_g, ln2_b, loss_target, m_c_ctx, m_w_mod, m_b_mod, m_w_in, m_b_gate, m_ssd_conv_w, m_ssd_conv_b, m_ssd_dt_bias, m_ssd_a_log, m_ssd_d, m_ssd_norm_w, m_lru_conv_w, m_lru_conv_b, m_lru_wa, m_lru_ba, m_lru_wi, m_lru_bi, m_lru_lambda, m_w_br_ssd, m_w_br_lru, m_w_out, m_ln1_g, m_ln1_b, m_w_mlp1, m_b_mlp1, m_w_mlp2, m_b_mlp2, m_ln2_g, m_ln2_b, v_c_ctx, v_w_mod, v_b_mod, v_w_in, v_b_gate, v_ssd_conv_w, v_ssd_conv_b, v_ssd_dt_bias, v_ssd_a_log, v_ssd_d, v_ssd_norm_w, v_lru_conv_w, v_lru_conv_b, v_lru_wa, v_lru_ba, v_lru_wi, v_lru_bi, v_lru_lambda, v_w_br_ssd, v_w_br_lru, v_w_out, v_ln1_g, v_ln1_b, v_w_mlp1, v_b_mlp1, v_w_mlp2, v_b_mlp2, v_ln2_g, v_ln2_b)))
    Bn, T, _ = x.shape
    Tc = ctx.shape[1]
    cfg = _Cfg(Bn, T, Tc)
    me = _me()
    L = {n: (A[n] if n == "c_ctx" else A[n][0]) for n in _WEIGHT_NAMES}
    nmod = L["w_mod"].shape[1]

    c_all = _exchange(c, "ag_c", True)
    c_rows = jnp.concatenate([c_all.reshape(NDEV * Bn, D), jnp.broadcast_to(c_ctx[None, :], (8, D))], axis=0)
    b_shard = lax.dynamic_slice(L["b_mod"], (me * nmod,), (nmod,)).reshape(1, nmod)
    m_part = _mod_fwd(c_rows, L["w_mod"], b_shard)
    m_all = _exchange(m_part, "ag_mod", True)
    m_full = m_all.transpose(1, 0, 2).reshape(NDEV * Bn + 8, NMOD * D)
    m_mine = lax.dynamic_slice(m_full, (me * Bn, 0), (Bn, NMOD * D))
    mc = m_full[NDEV * Bn, :2 * D]

    w_in_all = _gather_two_level(L["w_in"].astype(BF16), "ag_w_in")
    rest_payload = jnp.concatenate([L[n].astype(BF16).reshape(-1, 1024) for n, _, _ in _BIG[1:]], axis=0)
    small_shapes = [(s[0], s[1] // NDEV) for _, s in _SMALL_SH]
    small_all = _exchange(_pack_rows([L[n] for n, _ in _SMALL_SH], width=512), "ag_w_small", True)
    W = {}
    for (n, shp), piece in zip(_SMALL_SH, _unpack_rows(small_all, small_shapes, lead=(NDEV,))):
        W[n] = piece.transpose(1, 0, 2).reshape(shp)
    W["w_main"], W["w_dt"] = _perm_w_in(_from_slots(w_in_all, 1))
    for n in ("ssd_conv_b", "lru_conv_b", "ssd_norm_w", "b_gate", "ln1_g", "ln1_b", "b_mlp1", "b_mlp2", "ln2_g", "ln2_b"):
        W[n] = L[n].reshape(1, -1)
    for n in ("ssd_dt_bias", "ssd_a_log", "ssd_d", "lru_wa", "lru_wi"):
        W[n] = L[n]

    loss_part, grad_x, g, dm, dmc, xres = _local_step(cfg, x, ctx, loss_target, m_mine, mc, W, rest_payload)
    loss = lax.psum(loss_part, ("x", "y", "c"))

    dmc_pad = jnp.pad(dmc, ((0, 4 - Bn), (0, (NMOD - 2) * D)))
    dm_all = _exchange(jnp.concatenate([jnp.pad(dm, ((0, 4 - Bn), (0, 0))), dmc_pad], axis=0), "ag_dm", True)
    dm_all = dm_all.reshape(NDEV * 8, NMOD * D)
    c_rows_b = jnp.concatenate([jnp.pad(c_all, ((0, 0), (0, 4 - Bn), (0, 0))),
                                jnp.broadcast_to(c_ctx[None, None, :], (NDEV, 4, D))], axis=1).reshape(NDEV * 8, D)
    dm_shard = lax.dynamic_slice(dm_all, (0, me * nmod), (NDEV * 8, nmod))
    g_w_mod, g_b_mod, cc_part = _mod_bwd(c_rows_b, dm_all, dm_shard, L["w_mod"])
    g["c_ctx"] = cc_part[0]

    g["ssd_d"] = g.pop("ssd_d_cols")
    small_names = [n for n, _ in _REPL] + [n for n, _ in _SMALL_SH]
    small_full_shapes = [s for _, s in _REPL] + [s for _, s in _SMALL_SH]
    sm_all = _exchange(_pack_rows([g[n] for n in small_names]), "ag_g_small", True)
    sm_sum = _sum_slots(sm_all, "sum_g_small")
    gs = dict(zip(small_names, _unpack_rows(sm_sum, small_full_shapes)))
    gcc, gdd = _small_finish(gs["c_ctx"].reshape(1, D), c_ctx.reshape(1, D), gs["ssd_d"].reshape(NH, HD))
    gs["c_ctx"] = gcc.reshape(D)
    gs["ssd_d"] = gdd.reshape(NH)
    for n, shp in _SMALL_SH:
        ns = shp[1] // NDEV
        gs[n] = lax.dynamic_slice(gs[n], (0, me * ns), (shp[0], ns))
    gs["b_mod"] = g_b_mod.reshape(NMOD * D)
    lru_sum = _sum_slots(xres["ag_lru"], "sum_g_lru").reshape(2, 2, LB, LBW, LBW)
    gs["lru_wa"], gs["lru_wi"] = lru_sum[0], lru_sum[1]

    gb = {}
    gb["w_in"] = _sum_slots(xres["rs_w_in"], "sum_w_in")
    red_b = _sum_slots(xres["rs_rest"], "sum_w_rest")
    off = 0
    for n, shp, axis in _BIG[1:]:
        shard_shape = (shp[0] // NDEV, shp[1]) if axis == 0 else (shp[0], shp[1] // NDEV)
        r = math.prod(shard_shape) // 1024
        gb[n] = red_b[off:off + r].reshape(shard_shape)
        off += r
    gb["w_mod"] = g_w_mod

    grads, deltas, new_m, new_v = {}, {}, {}, {}
    big_names = ["w_mod"] + [n for n, _, _ in _BIG]
    for n in big_names:
        d_, nm_, nv_ = _adamw(L[n], gb[n], A["m_" + n][0], A["v_" + n][0], "adamw_" + n)
        grads[n], deltas[n], new_m[n], new_v[n] = gb[n], d_, nm_, nv_
    sm_names = [n for n in _WEIGHT_NAMES if n not in big_names]

    def two_d(a):
        return a.reshape(1, -1) if a.ndim == 1 else a
    loc = lambda pre: [two_d(A[pre + n] if n == "c_ctx" else A[pre + n][0]) for n in sm_names]
    gsm = [two_d(gs[n].reshape(L[n].shape)) for n in sm_names]
    ds_, nms_, nvs_ = _adamw_many(loc(""), gsm, loc("m_"), loc("v_"))
    for n, gv, dv, mv, vv in zip(sm_names, gsm, ds_, nms_, nvs_):
        shp = L[n].shape
        grads[n], deltas[n], new_m[n], new_v[n] = gv.reshape(shp), dv.reshape(shp), mv.reshape(shp), vv.reshape(shp)

    def out(dct):
        return [dct[n] if n == "c_ctx" else dct[n][None] for n in _WEIGHT_NAMES]
    return (loss, grad_x, *out(grads), *out(deltas), *out(new_m), *out(new_v))
```

```python
import functools
import math

import jax
import jax.numpy as jnp
from jax import lax
from jax.experimental import pallas as pl
from jax.experimental.pallas import tpu as pltpu

F32 = jnp.float32
BF16 = jnp.bfloat16

D = 1024
GRID_W = 64
DI = 2048
NH = 32
HD = 64
NG = 8
HPG = 4
NS = 128
CH = 128
LW = 1024
LB = 8
LBW = 128
LRU_C = 8.0
MLP = 4096
NMOD = 6
ALPHA = 2.0 ** 0.25
LN_EPS = 1e-6
RMS_EPS = 1e-5
PM = 10240
DTW = 128
CONVW = 5120
NDEV = 8

ADAM_LR = 0.001
ADAM_B1 = 0.9
ADAM_B2 = 0.999
ADAM_EPS = 1e-08
ADAM_WD = 0.01
ADAM_STEP = 10

VMEM_LIMIT_BYTES = 56 * 1024 * 1024


def _cp(n_axes):
    return pltpu.CompilerParams(dimension_semantics=("arbitrary",) * n_axes,
                                vmem_limit_bytes=VMEM_LIMIT_BYTES)


def _sigmoid(x):
    return 0.5 * jnp.tanh(0.5 * x) + 0.5


def _silu(x):
    return x * _sigmoid(x)


def _silu_grad(x):
    s = _sigmoid(x)
    return s * (1.0 + x * (1.0 - s))


def _log1p_pos(e):
    return jnp.where(e < 1e-2, e * (1.0 - e * (0.5 - e * (1.0 / 3.0))), jnp.log(1.0 + e))


def _softplus(x):
    return jnp.maximum(x, 0.0) + _log1p_pos(jnp.exp(-jnp.abs(x)))


_GELU_K = math.sqrt(2.0 / math.pi)


def _gelu(x):
    t = jnp.tanh(_GELU_K * (x + 0.044715 * x * x * x))
    return 0.5 * x * (1.0 + t)


def _gelu_grad(x):
    t = jnp.tanh(_GELU_K * (x + 0.044715 * x * x * x))
    dt = (1.0 - t * t) * _GELU_K * (1.0 + 3.0 * 0.044715 * x * x)
    return 0.5 * (1.0 + t) + 0.5 * x * dt


def _ln(x):
    mu = jnp.mean(x, axis=-1, keepdims=True)
    xc = x - mu
    var = jnp.mean(xc * xc, axis=-1, keepdims=True)
    rs = lax.rsqrt(var + LN_EPS)
    return xc * rs, rs


def _ln_bwd(dy, xhat, rs):
    m1 = jnp.mean(dy, axis=-1, keepdims=True)
    m2 = jnp.mean(dy * xhat, axis=-1, keepdims=True)
    return rs * (dy - m1 - xhat * m2)


def _dot(a, b):
    return lax.dot_general(a, b, (((1,), (0,)), ((), ())), preferred_element_type=F32)


def _dot_nt(a, b):
    return lax.dot_general(a, b, (((1,), (1,)), ((), ())), preferred_element_type=F32)


def _dot_tn(a, b):
    return lax.dot_general(a, b, (((0,), (0,)), ((), ())), preferred_element_type=F32)


def _split3(a):
    a0 = a.astype(BF16)
    r = a - a0.astype(F32)
    a1 = r.astype(BF16)
    a2 = (r - a1.astype(F32)).astype(BF16)
    return a0, a1, a2


def _dot_exact_l(m_bf, a):
    a0, a1, a2 = _split3(a)
    return _dot(m_bf, a0) + _dot(m_bf, a1) + _dot(m_bf, a2)


def _dot_exact_r(a, m_bf):
    a0, a1, a2 = _split3(a)
    return _dot(a0, m_bf) + _dot(a1, m_bf) + _dot(a2, m_bf)


def _dot_hilo_r(a, m_bf):
    a0 = a.astype(BF16)
    a1 = (a - a0.astype(F32)).astype(BF16)
    return _dot(a0, m_bf) + _dot(a1, m_bf)


def _tri(n, upper):
    ii = lax.broadcasted_iota(jnp.int32, (n, n), 0)
    kk = lax.broadcasted_iota(jnp.int32, (n, n), 1)
    m = (kk >= ii) if upper else (kk <= ii)
    return jnp.where(m, 1.0, 0.0).astype(BF16)


def _fit(n, t):
    t = min(t, n)
    while n % t:
        t //= 2
    return t


def _mm(a, b, mode, name, out_dtype=F32, tm=512, tn=512, tk=512, xch=None):
    if mode == "nn":
        M, K = a.shape
        N = b.shape[1]
    elif mode == "nt":
        M, K = a.shape
        N = b.shape[0]
    else:
        K, M = a.shape
        N = b.shape[1]
    tm, tn, tk = _fit(M, tm), _fit(N, tn), _fit(K, tk)
    assert M % tm == 0 and N % tn == 0 and K % tk == 0, (name, M, N, K, tm, tn, tk)
    nk = K // tk
    if mode == "tn":
        a_spec = pl.BlockSpec((tk, tm), lambda i, j, k: (k, i))
    else:
        a_spec = pl.BlockSpec((tm, tk), lambda i, j, k: (i, k))
    if mode == "nt":
        b_spec = pl.BlockSpec((tn, tk), lambda i, j, k: (j, k))
    else:
        b_spec = pl.BlockSpec((tk, tn), lambda i, j, k: (k, j))
    dn = {"nn": (((1,), (0,)), ((), ())), "nt": (((1,), (1,)), ((), ())), "tn": (((0,), (0,)), ((), ()))}[mode]

    def body(a_ref, b_ref, o_ref, acc_ref):
        k = pl.program_id(2)

        @pl.when(k == 0)
        def _():
            acc_ref[...] = jnp.zeros_like(acc_ref)

        acc_ref[...] += lax.dot_general(a_ref[...].astype(BF16), b_ref[...].astype(BF16), dn,
                                        preferred_element_type=F32)

        @pl.when(k == nk - 1)
        def _():
            o_ref[...] = acc_ref[...].astype(o_ref.dtype)

    res = _hosted_call(
        body, xch, name=name, grid=(M // tm, N // tn, nk),
        in_specs=[a_spec, b_spec],
        out_specs=[pl.BlockSpec((tm, tn), lambda i, j, k: (i, j))],
        out_shape=[jax.ShapeDtypeStruct((M, N), out_dtype)],
        scratch_shapes=[pltpu.VMEM((tm, tn), F32)],
        compiler_params=_cp(3), args=(a, b))
    if xch is None:
        return res[0]
    return res[0][0], res[1]


def _mm_mlp1(h2, w1, b1, tm=1024, tn=1024):
    M, K = h2.shape
    N = w1.shape[1]
    tm, tn = _fit(M, tm), _fit(N, tn)

    def body(a_ref, b_ref, bias_ref, a1_ref, act_ref):
        v = _dot(a_ref[...], b_ref[...]) + bias_ref[...]
        a1_ref[...] = v
        r = jnp.maximum(v, 0.0)
        act_ref[...] = (r * r).astype(BF16)

    out = pl.BlockSpec((tm, tn), lambda i, j: (i, j))
    return pl.pallas_call(
        body, name="mm_mlp1", grid=(M // tm, N // tn),
        in_specs=[pl.BlockSpec((tm, K), lambda i, j: (i, 0)), pl.BlockSpec((K, tn), lambda i, j: (0, j)),
                  pl.BlockSpec((1, tn), lambda i, j: (0, j))],
        out_specs=[out, out],
        out_shape=[jax.ShapeDtypeStruct((M, N), F32), jax.ShapeDtypeStruct((M, N), BF16)],
        compiler_params=_cp(2),
    )(h2, w1, b1)


def _mm_dact(dmlp, w2, a1, tm=1024, tn=1024):
    M, K = dmlp.shape
    N = w2.shape[0]
    tm, tn = _fit(M, tm), _fit(N, tn)

    def body(d_ref, w_ref, a1_ref, o_ref, acc_ref):
        i = pl.program_id(1)

        @pl.when(i == 0)
        def _():
            acc_ref[...] = jnp.zeros_like(acc_ref)

        da = _dot_nt(d_ref[...], w_ref[...]) * (2.0 * jnp.maximum(a1_ref[...], 0.0))
        o_ref[...] = da.astype(BF16)
        acc_ref[0:1, :] += jnp.sum(da, axis=0, keepdims=True)

    blk = pl.BlockSpec((tm, tn), lambda j, i: (i, j))
    return pl.pallas_call(
        body, name="mm_dact", grid=(N // tn, M // tm),
        in_specs=[pl.BlockSpec((tm, K), lambda j, i: (i, 0)), pl.BlockSpec((tn, K), lambda j, i: (j, 0)), blk],
        out_specs=[blk, pl.BlockSpec((8, tn), lambda j, i: (0, j))],
        out_shape=[jax.ShapeDtypeStruct((M, N), BF16), jax.ShapeDtypeStruct((8, N), F32)],
        compiler_params=_cp(2),
    )(dmlp, w2, a1)


class _Cfg:
    def __init__(self, Bn, T, Tc):
        assert T % Tc == 0 and Tc % CH == 0 and Tc % GRID_W == 0
        self.Bn, self.T, self.Tc = Bn, T, Tc
        self.TT = T + Tc
        self.TB = Tc
        self.nbt = self.TT // self.TB
        self.nbl = T // self.TB
        self.NT = Bn * self.TT
        self.N = Bn * T
        self.nct = Tc // CH
        self.nlt = T // CH
        self.nch = self.nct + self.nlt


def _ln_mod_fwd(cfg, xs, shift_tab, scale_tab):
    TB, nbt = cfg.TB, cfg.nbt

    def body(x_ref, sh_ref, sc_ref, o_ref):
        xhat, _ = _ln(x_ref[...])
        o_ref[...] = (xhat * (1.0 + sc_ref[...]) + sh_ref[...]).astype(BF16)

    tab = pl.BlockSpec((None, 1, D), lambda b, j: (2 * b + jnp.minimum(j, 1), 0, 0))
    return pl.pallas_call(
        body, name="ln_mod_fwd", grid=(cfg.Bn, nbt),
        in_specs=[pl.BlockSpec((TB, D), lambda b, j: (b * nbt + j, 0)), tab, tab],
        out_specs=pl.BlockSpec((TB, D), lambda b, j: (b * nbt + j, 0)),
        out_shape=jax.ShapeDtypeStruct((cfg.NT, D), BF16),
        compiler_params=_cp(2),
    )(xs, shift_tab, scale_tab)


GP = 4
NGB = NG // GP
HPB = GP * HPG


def _head_select(d, gb, to_front):
    r = lax.broadcasted_iota(jnp.int32, (DTW, DTW), 0)
    c = lax.broadcasted_iota(jnp.int32, (DTW, DTW), 1)
    src, dst = (r, c) if to_front else (c, r)
    hit = jnp.logical_and(src == d * NH + gb * HPB + dst, dst < HPB)
    return jnp.where(hit, 1.0, 0.0).astype(BF16)


def _dt_fwd(cfg, dt_raw, dt_bias, a_log):
    def body(raw_ref, bias_ref, alog_ref, dt_ref, dtg_ref, cumg_ref, cumT_ref):
        dt = _softplus(raw_ref[...] + bias_ref[...])
        a = -jnp.exp(alog_ref[...])
        dta = dt * a
        col = lax.broadcasted_iota(jnp.int32, (CH, DTW), 1)
        cf = _dot_exact_l(_tri(CH, False), dta)
        cr = _dot_exact_l(_tri(CH, True), dta)
        cum = jnp.where(col < NH, cf, cr)
        dt_ref[...] = dt
        for d in range(2):
            for gb in range(NGB):
                sel = _head_select(d, gb, True)
                dtg_ref[d, gb] = _dot_exact_r(dt, sel)
                cg = _dot_exact_r(cum, sel)
                cumg_ref[d, gb] = cg
                cumT_ref[d, gb] = cg.T

    blk = pl.BlockSpec((CH, DTW), lambda i: (i, 0))
    row = pl.BlockSpec((1, DTW), lambda i: (0, 0))
    gblk = pl.BlockSpec((2, NGB, CH, DTW), lambda i: (0, 0, i, 0))
    return pl.pallas_call(
        body, name="dt_fwd", grid=(cfg.NT // CH,),
        in_specs=[blk, row, row],
        out_specs=[blk, gblk, gblk, pl.BlockSpec((2, NGB, None, DTW, CH), lambda i: (0, 0, i, 0, 0))],
        out_shape=[jax.ShapeDtypeStruct((cfg.NT, DTW), F32),
                   jax.ShapeDtypeStruct((2, NGB, cfg.NT, DTW), F32),
                   jax.ShapeDtypeStruct((2, NGB, cfg.NT, DTW), F32),
                   jax.ShapeDtypeStruct((2, NGB, cfg.NT // CH, DTW, CH), F32)],
        compiler_params=_cp(1),
    )(dt_raw, dt_bias, a_log)


def _dt_bwd(cfg, dAs, dxxs, dt_raw, dt, dt_bias, a_log):
    def body(dAf_ref, dAr_ref, dxf_ref, dxr_ref, raw_ref, dt_ref, bias_ref, alog_ref, o_ref, acc_ref):
        i = pl.program_id(0)

        @pl.when(i == 0)
        def _():
            acc_ref[...] = jnp.zeros_like(acc_ref)

        a = -jnp.exp(alog_ref[...])
        col = lax.broadcasted_iota(jnp.int32, (CH, DTW), 1)
        dA_v = jnp.zeros((CH, DTW), F32)
        dxx_v = jnp.zeros((CH, DTW), F32)
        for d, (ra, rx) in enumerate(((dAf_ref, dxf_ref), (dAr_ref, dxr_ref))):
            for gb in range(NGB):
                sel = _head_select(d, gb, False)
                dA_v = dA_v + _dot_exact_r(ra[gb], sel)
                dxx_v = dxx_v + _dot_exact_r(rx[gb], sel)
        ddta = jnp.where(col < NH, _dot_exact_l(_tri(CH, True), dA_v), _dot_exact_l(_tri(CH, False), dA_v))
        dtv = dt_ref[...]
        ddt = ddta * a + dxx_v
        draw = ddt * _sigmoid(raw_ref[...] + bias_ref[...])
        draw = jnp.where(col < 2 * NH, draw, 0.0)
        o_ref[...] = draw
        da = jnp.sum(ddta * dtv, axis=0, keepdims=True) * a
        da = jnp.where(col[:1] < 2 * NH, da, 0.0)
        acc_ref[0:1, :] += da
        acc_ref[1:2, :] += jnp.sum(draw, axis=0, keepdims=True)

    blk = pl.BlockSpec((CH, DTW), lambda i: (i, 0))
    row = pl.BlockSpec((1, DTW), lambda i: (0, 0))
    gblk = pl.BlockSpec((NGB, CH, DTW), lambda i: (0, i, 0))
    return pl.pallas_call(
        body, name="dt_bwd", grid=(cfg.NT // CH,),
        in_specs=[gblk, gblk, gblk, gblk, blk, blk, row, row],
        out_specs=[blk, pl.BlockSpec((8, DTW), lambda i: (0, 0))],
        out_shape=[jax.ShapeDtypeStruct((cfg.NT, DTW), F32), jax.ShapeDtypeStruct((8, DTW), F32)],
        compiler_params=_cp(1),
    )(dAs[0], dAs[1], dxxs[0], dxxs[1], dt_raw, dt, dt_bias, a_log)


_TAPS = (2, 1, 0, -1)


def _conv_fwd(cfg, proj, conv_w, conv_b):
    TB, nbt = cfg.TB, cfg.nbt
    CB = CONVW // 2
    SUB = 256
    n_act = DI + 2 * NG * NS

    def body(u_ref, w_ref, b_ref, o_ref, sg_ref):
        i = pl.program_id(0)
        j = pl.program_id(1)
        R = jnp.where(i % nbt == 0, cfg.Tc, GRID_W)
        t = lax.broadcasted_iota(jnp.int32, (TB, SUB), 0)
        pos = jnp.bitwise_and(t, R - 1)
        keep = {s: jnp.where(jnp.logical_and(pos - s >= 0, pos - s < R), 1.0, 0.0) for s in (2, 1, -1)}
        for q in range(CB // SUB):
            sl = slice(q * SUB, (q + 1) * SUB)
            u = u_ref[:, sl]
            pre = b_ref[:, sl] + w_ref[2:3, sl] * u
            for k in (0, 1, 3):
                pre = pre + w_ref[k:k + 1, sl] * (pltpu.roll(u, _TAPS[k] % TB, 0) * keep[_TAPS[k]])
            col = j * CB + q * SUB + lax.broadcasted_iota(jnp.int32, (1, SUB), 1)
            s = _sigmoid(pre)
            o_ref[:, sl] = jnp.where(col < n_act, pre * s, pre)
            sg_ref[:, sl] = jnp.where(col < n_act, s * (1.0 + pre * (1.0 - s)), 1.0)

    blk = pl.BlockSpec((TB, CB), lambda i, j: (i, j))
    return pl.pallas_call(
        body, name="conv_fwd", grid=(cfg.NT // TB, CONVW // CB),
        in_specs=[blk, pl.BlockSpec((4, CB), lambda i, j: (0, j)), pl.BlockSpec((1, CB), lambda i, j: (0, j))],
        out_specs=[blk, blk],
        out_shape=[jax.ShapeDtypeStruct((cfg.NT, CONVW), F32)] * 2,
        compiler_params=_cp(2),
    )(proj, conv_w, conv_b)


_ANY = pl.BlockSpec(memory_space=pl.ANY)


def _conv_bwd(cfg, name, dproj, proj, conv_w, sgrad, addends, col0, width, skip=None, xch=None):
    TB, nbt, nbl = cfg.TB, cfg.nbt, cfg.nbl
    CB = 1024
    SUB = 256
    c0 = col0 // CB
    n_add = len(addends)

    def body(*refs):
        u_ref, w_ref, sg_ref = refs[1:4]
        add_refs = refs[4:4 + n_add]
        rest = refs[4 + n_add:]
        if skip is not None:
            dy_ref, dv_ref = rest[:2]
            rest = rest[2:]
        o_ref, acc_ref = rest
        i = pl.program_id(1)

        @pl.when(i == 0)
        def _():
            acc_ref[...] = jnp.zeros_like(acc_ref)

        isctx = (i % nbt) == 0
        R = jnp.where(isctx, cfg.Tc, GRID_W)
        t = lax.broadcasted_iota(jnp.int32, (TB, SUB), 0)
        pos = jnp.bitwise_and(t, R - 1)
        keep = {s: jnp.where(jnp.logical_and(pos - s >= 0, pos - s < R), 1.0, 0.0) for s in (2, 1, -1, -2)}

        def shifted(v, s):
            return v if s == 0 else pltpu.roll(v, s % TB, 0) * keep[s]

        for q in range(CB // SUB):
            sl = slice(q * SUB, (q + 1) * SUB)
            u = u_ref[:, sl]
            us = [shifted(u, _TAPS[k]) for k in range(4)]
            g = add_refs[0][:, sl]
            for r in add_refs[1:]:
                g = g + r[:, sl]
            if skip is not None:
                g = g + jnp.where(isctx, 0.0, dv_ref[:, sl] * dy_ref[:, sl])
            g = g * sg_ref[:, sl]
            dp = jnp.zeros_like(g)
            for k in range(4):
                acc_ref[k:k + 1, sl] += jnp.sum(g * us[k], axis=0, keepdims=True)
                dp = dp + w_ref[k:k + 1, sl] * shifted(g, -_TAPS[k])
            acc_ref[4:5, sl] += jnp.sum(g, axis=0, keepdims=True)
            o_ref[:, sl] = dp.astype(BF16)

    blk = pl.BlockSpec((TB, CB), lambda j, i: (i, j))
    in_specs = [_ANY,
                pl.BlockSpec((TB, CB), lambda j, i: (i, c0 + j)),
                pl.BlockSpec((4, CB), lambda j, i: (0, c0 + j)),
                pl.BlockSpec((TB, CB), lambda j, i: (i, c0 + j))] + [blk] * n_add
    args = [dproj, proj, conv_w, sgrad] + list(addends)
    if skip is not None:
        def lat(j, i):
            b = i // nbt
            return (b * nbl + jnp.maximum(i % nbt - 1, 0), j)
        in_specs += [pl.BlockSpec((TB, CB), lat), pl.BlockSpec((1, CB), lambda j, i: (0, j))]
        args += list(skip)
    return _hosted_call(
        body, xch, name=name, grid=(width // CB, cfg.NT // TB),
        in_specs=in_specs,
        out_specs=[pl.BlockSpec((TB, CB), lambda j, i: (i, c0 + j)), pl.BlockSpec((8, CB), lambda j, i: (0, j))],
        out_shape=[jax.ShapeDtypeStruct((cfg.NT, PM), BF16), jax.ShapeDtypeStruct((8, width), F32)],
        scratch_shapes=[], compiler_params=_cp(2), args=args, aliases={0: 0})


def _chunk_of_step(cfg, rev):
    nct, nlt = cfg.nct, cfg.nlt
    if not rev:
        return lambda s: s
    return lambda s: jnp.where(s < nct, nct - 1 - s, 2 * nct + nlt - 1 - s)


def _expand4(v, band, base):
    out = v[:, base + 3:base + 4]
    for h in (2, 1, 0):
        out = jnp.where(band == h, v[:, base + h:base + h + 1], out)
    return out


def _ssd_step_tiles(dt_ref, cum_ref, cumT_ref, rev):
    cum_t = cum_ref[...]
    last = 0 if rev else CH - 1
    llast = cum_t[last:last + 1, :]
    return (dt_ref[...], cum_t, cumT_ref[...], llast, jnp.exp(llast), last)


def _ssd_common(gi, x_ref, b_ref, c_ref, tiles, rev):
    dt_t, cum_t, cumT_t, llast, elast, last = tiles
    base = gi * HPG
    xh = x_ref[:, gi * HPG * HD:(gi + 1) * HPG * HD]
    Bm = b_ref[:, gi * NS:(gi + 1) * NS].astype(BF16)
    Cm = c_ref[:, gi * NS:(gi + 1) * NS].astype(BF16)
    band = lax.broadcasted_iota(jnp.int32, (CH, HPG * HD), 1) // HD
    G = _dot_nt(Cm, Bm)
    ii = lax.broadcasted_iota(jnp.int32, (CH, CH), 0)
    jj = lax.broadcasted_iota(jnp.int32, (CH, CH), 1)
    mask = (jj >= ii) if rev else (jj <= ii)
    decs, cbs = [], []
    for h in range(HPG):
        cb = jnp.broadcast_to(cum_t[:, base + h:base + h + 1], (CH, CH))
        cbs.append(cb)
        decs.append(jnp.exp(jnp.where(mask, cb - cumT_t[base + h:base + h + 1, :], -1e30)))
    cum_exp = jnp.concatenate([cbs[3], cbs[3]], axis=1)
    ll_exp = llast[:, base + 3:base + 4]
    for h in (2, 1, 0):
        cum_exp = jnp.where(band == h, jnp.concatenate([cbs[h], cbs[h]], axis=1), cum_exp)
        ll_exp = jnp.where(band[:1] == h, llast[:, base + h:base + h + 1], ll_exp)
    ecum = jnp.exp(cum_exp)
    e_exp = jnp.exp(ll_exp - cum_exp)
    dt_exp = _expand4(dt_t, band, base)
    X = xh * dt_exp
    rb = lax.broadcasted_iota(jnp.int32, (HPG * HD, NS), 0) // HD
    dec_rows = elast[:, base + 3:base + 4]
    for h in (2, 1, 0):
        dec_rows = jnp.where(rb == h, elast[:, base + h:base + h + 1], dec_rows)
    return xh, Bm, Cm, band, e_exp, ecum, dt_exp, X, G, decs, elast, dec_rows, last


def _ssd_specs(cfg, rev):
    nch = cfg.nch
    cmap = _chunk_of_step(cfg, rev)
    d = 1 if rev else 0

    def make(stepmap):
        def row(b, g, sp):
            return b * nch + cmap(stepmap(sp))
        bo, co = DI // (GP * NS), (DI + NG * NS) // (GP * NS)
        return [
            pl.BlockSpec((CH, GP * HPG * HD), lambda b, g, sp: (row(b, g, sp), g)),
            pl.BlockSpec((CH, GP * NS), lambda b, g, sp: (row(b, g, sp), bo + g)),
            pl.BlockSpec((CH, GP * NS), lambda b, g, sp: (row(b, g, sp), co + g)),
            pl.BlockSpec((None, None, CH, DTW), lambda b, g, sp: (d, g, row(b, g, sp), 0)),
            pl.BlockSpec((None, None, CH, DTW), lambda b, g, sp: (d, g, row(b, g, sp), 0)),
            pl.BlockSpec((None, None, None, DTW, CH), lambda b, g, sp: (d, g, row(b, g, sp), 0, 0)),
        ], row
    return make


def _ssd_fwd(cfg, act, dtg, cumg, cumTg, rev):
    nch = cfg.nch
    in_specs, row = _ssd_specs(cfg, rev)(lambda sp: sp)

    def body(x_ref, b_ref, c_ref, dt_ref, cum_ref, cumT_ref, y_ref, hs_ref, h_scr):
        s = pl.program_id(2)

        @pl.when(s == 0)
        def _():
            h_scr[...] = jnp.zeros_like(h_scr)

        tiles = _ssd_step_tiles(dt_ref, cum_ref, cumT_ref, rev)
        for gi in range(GP):
            xh, Bm, Cm, band, e_exp, ecum, dt_exp, X, G, decs, elast, dec_rows, last = _ssd_common(
                gi, x_ref, b_ref, c_ref, tiles, rev)
            H = h_scr[gi]
            Mcat = jnp.concatenate([(G * decs[h]).astype(BF16) for h in range(HPG)], axis=1)
            Xbd = jnp.concatenate([jnp.where(band == h, X, 0.0).astype(BF16) for h in range(HPG)], axis=0)
            Y = ecum * _dot_nt(Cm, H.astype(BF16)) + _dot(Mcat, Xbd)
            y_ref[:, gi * HPG * HD:(gi + 1) * HPG * HD] = Y
            hs_ref[gi] = H
            S = _dot_tn((e_exp * X).astype(BF16), Bm)
            h_scr[gi] = dec_rows * H + S

    return pl.pallas_call(
        body, name="ssd_fwd_rev" if rev else "ssd_fwd", grid=(cfg.Bn, NG // GP, nch),
        in_specs=in_specs,
        out_specs=[pl.BlockSpec((CH, GP * HPG * HD), lambda b, g, s: (row(b, g, s), g)),
                   pl.BlockSpec((None, GP, None, HPG * HD, NS), lambda b, g, s: (b, g, s, 0, 0))],
        out_shape=[jax.ShapeDtypeStruct((cfg.NT, DI), F32),
                   jax.ShapeDtypeStruct((cfg.Bn, NG, nch, HPG * HD, NS), F32)],
        scratch_shapes=[pltpu.VMEM((GP, HPG * HD, NS), F32)],
        compiler_params=_cp(3),
    )(act, act, act, dtg, cumg, cumTg)


def _ssd_bwd(cfg, act, dtg, cumg, cumTg, hs, dy, rev, xch=None):
    nch, nct, nlt = cfg.nch, cfg.nct, cfg.nlt
    cmap = _chunk_of_step(cfg, rev)
    in_specs, row = _ssd_specs(cfg, rev)(lambda sp: nch - 1 - sp)

    def lat_row(b, g, sp):
        c = cmap(nch - 1 - sp)
        return b * nlt + jnp.maximum(c - nct, 0)

    def body(x_ref, b_ref, c_ref, dt_ref, cum_ref, cumT_ref, dy_ref, hs_ref,
             dxh_ref, dB_ref, dC_ref, dA_ref, dxx_ref, dh_scr):
        sp = pl.program_id(2)

        @pl.when(sp == 0)
        def _():
            dh_scr[...] = jnp.zeros_like(dh_scr)

        c = cmap(nch - 1 - sp)
        tiles = _ssd_step_tiles(dt_ref, cum_ref, cumT_ref, rev)
        dA_t = jnp.zeros((CH, DTW), F32)
        dAT_t = jnp.zeros((DTW, CH), F32)
        dxx_t = jnp.zeros((CH, DTW), F32)
        for gi in range(GP):
            dA_g, dAT_g, dxx_g = group_bwd(gi, c < nct, tiles, x_ref, b_ref, c_ref, dy_ref, hs_ref,
                                           dxh_ref, dB_ref, dC_ref, dh_scr)
            dA_t = dA_t + dA_g
            dAT_t = dAT_t + dAT_g
            dxx_t = dxx_t + dxx_g
        dA_ref[...] = dA_t - dAT_t.T
        dxx_ref[...] = dxx_t

    def group_bwd(gi, isctx, tiles, x_ref, b_ref, c_ref, dy_ref, hs_ref, dxh_ref, dB_ref, dC_ref, dh_scr):
        xsl = slice(gi * HPG * HD, (gi + 1) * HPG * HD)
        nsl = slice(gi * NS, (gi + 1) * NS)
        base = gi * HPG
        xh, Bm, Cm, band, e_exp, ecum, dt_exp, X, G, decs, elast, dec_rows, last = _ssd_common(
            gi, x_ref, b_ref, c_ref, tiles, rev)
        dY = jnp.where(isctx, 0.0, dy_ref[:, xsl])
        H = hs_ref[gi]
        Hb = H.astype(BF16)
        dHn = dh_scr[gi]
        dHnb = dHn.astype(BF16)
        dYs = ecum * dY
        dYsb = dYs.astype(BF16)
        BdH = _dot_nt(Bm, dHnb)
        Ys = ecum * _dot_nt(Cm, Hb)
        dX = e_exp * BdH
        dG = jnp.zeros((CH, CH), F32)
        lanei = lax.broadcasted_iota(jnp.int32, (CH, DTW), 1)
        subi = lax.broadcasted_iota(jnp.int32, (DTW, CH), 0)
        dA = jnp.zeros((CH, DTW), F32)
        dAT = jnp.zeros((DTW, CH), F32)
        Xbd = jnp.concatenate([jnp.where(band == h, X, 0.0).astype(BF16) for h in range(HPG)], axis=0)
        dYbd = jnp.concatenate([jnp.where(band == h, dY, 0.0).astype(BF16) for h in range(HPG)], axis=0)
        dMcat = _dot_nt(dY.astype(BF16), Xbd)
        Ms = []
        for h in range(HPG):
            M = G * decs[h]
            dM = dMcat[:, h * CH:(h + 1) * CH]
            W = dM * M
            dG = dG + dM * decs[h]
            Ms.append(M.astype(BF16))
            dA = dA + jnp.where(lanei == base + h, jnp.sum(W, axis=1, keepdims=True), 0.0)
            dAT = dAT + jnp.where(subi == base + h, jnp.sum(W, axis=0, keepdims=True), 0.0)
        dX = dX + _dot_tn(jnp.concatenate(Ms, axis=0), dYbd)
        dGb = dG.astype(BF16)
        eX = e_exp * X
        dC_ref[:, nsl] = _dot(dGb, Bm) + _dot(dYsb, Hb)
        dB_ref[:, nsl] = _dot_tn(dGb, Cm) + _dot(eX.astype(BF16), dHnb)
        dh_scr[gi] = dec_rows * dHn + _dot_tn(dYsb, Cm)
        pb = lax.broadcasted_iota(jnp.int32, (HPG * HD, NS), 0) // HD
        pl_ = lax.broadcasted_iota(jnp.int32, (HPG * HD, NS), 1)
        E = jnp.where(pb + base == pl_, 1.0, 0.0).astype(BF16)
        t2 = _dot_hilo_r(dY * Ys, E)
        q = _dot_hilo_r(eX * BdH, E)
        r = jnp.sum(dHn * H, axis=1, keepdims=True)
        lane1 = lax.broadcasted_iota(jnp.int32, (1, DTW), 1)
        hdot = jnp.zeros((1, DTW), F32)
        for h in range(HPG):
            hv = jnp.sum(r[h * HD:(h + 1) * HD, :], axis=0, keepdims=True)
            hdot = hdot + jnp.where(lane1 == base + h, hv, 0.0)
        dllast = jnp.sum(q, axis=0, keepdims=True) + elast * hdot
        rowi = lax.broadcasted_iota(jnp.int32, (CH, DTW), 0)
        dxh_ref[:, xsl] = dX * dt_exp
        return dA + t2 - q + jnp.where(rowi == last, dllast, 0.0), dAT, _dot_hilo_r(dX * xh, E)

    small = pl.BlockSpec((None, CH, DTW), lambda b, g, sp: (g, row(b, g, sp), 0))
    return _hosted_call(
        body, xch, name="ssd_bwd_rev" if rev else "ssd_bwd", grid=(cfg.Bn, NG // GP, nch),
        in_specs=in_specs + [
            pl.BlockSpec((CH, GP * HPG * HD), lambda b, g, sp: (lat_row(b, g, sp), g)),
            pl.BlockSpec((None, GP, None, HPG * HD, NS), lambda b, g, sp: (b, g, nch - 1 - sp, 0, 0))],
        out_specs=[pl.BlockSpec((CH, GP * HPG * HD), lambda b, g, sp: (row(b, g, sp), g)),
                   pl.BlockSpec((CH, GP * NS), lambda b, g, sp: (row(b, g, sp), g)),
                   pl.BlockSpec((CH, GP * NS), lambda b, g, sp: (row(b, g, sp), g)),
                   small, small],
        out_shape=[jax.ShapeDtypeStruct((cfg.NT, DI), F32),
                   jax.ShapeDtypeStruct((cfg.NT, NG * NS), F32),
                   jax.ShapeDtypeStruct((cfg.NT, NG * NS), F32),
                   jax.ShapeDtypeStruct((NGB, cfg.NT, DTW), F32),
                   jax.ShapeDtypeStruct((NGB, cfg.NT, DTW), F32)],
        scratch_shapes=[pltpu.VMEM((GP, HPG * HD, NS), F32)],
        compiler_params=_cp(3), args=(act, act, act, dtg, cumg, cumTg, dy, hs))


def _shift_rows(v, s, fill, toward_later, rowi):
    n = v.shape[0]
    if toward_later:
        return jnp.where(rowi >= s, pltpu.roll(v, s, 0), fill)
    return jnp.where(rowi < n - s, pltpu.roll(v, n - s, 0), fill)


def _chunk_scan(a, b, carry, later):
    nt = a.shape[0] // 8
    rowi = lax.broadcasted_iota(jnp.int32, (8, a.shape[1]), 0)
    outs = [None] * nt
    for r in (range(nt) if later else range(nt - 1, -1, -1)):
        av = a[r * 8:(r + 1) * 8]
        bv = b[r * 8:(r + 1) * 8]
        for sh in (1, 2, 4):
            a_p = _shift_rows(av, sh, 1.0, later, rowi)
            b_p = _shift_rows(bv, sh, 0.0, later, rowi)
            bv = av * b_p + bv
            av = av * a_p
        h = bv + av * carry
        outs[r] = h
        carry = h[7:8] if later else h[0:1]
    return jnp.concatenate(outs, axis=0), carry


def _lru_gates(u, wa_ref, wi_ref, ba_ref, bi_ref, lam_ref):
    rs, is_ = [], []
    for k in range(LB):
        uk = u[:, k * LBW:(k + 1) * LBW].astype(BF16)
        rs.append(_dot(uk, wa_ref[k].astype(BF16)))
        is_.append(_dot(uk, wi_ref[k].astype(BF16)))
    r = _sigmoid(jnp.concatenate(rs, axis=1) + ba_ref[...])
    ig = _sigmoid(jnp.concatenate(is_, axis=1) + bi_ref[...])
    sp = _softplus(-lam_ref[...])
    la = -LRU_C * r * sp
    a = jnp.exp(la)
    g = jnp.sqrt((1.0 + a * a) * jnp.tanh(-la))
    return r, ig, sp, la, a, g


def _lru_w_specs(d):
    return [pl.BlockSpec((None, LB, LBW, LBW), lambda b, s: (d, 0, 0, 0)),
            pl.BlockSpec((None, LB, LBW, LBW), lambda b, s: (d, 0, 0, 0)),
            pl.BlockSpec((None, 1, LW), lambda b, s: (d, 0, 0)),
            pl.BlockSpec((None, 1, LW), lambda b, s: (d, 0, 0)),
            pl.BlockSpec((None, 1, LW), lambda b, s: (d, 0, 0))]


def _lru_fwd(cfg, act, wa, wi, ba, bi, lam, rev):
    nch = cfg.nch
    cmap = _chunk_of_step(cfg, rev)
    d = 1 if rev else 0
    ucol = (DI + 2 * NG * NS) // LW

    def body(u_ref, wa_ref, wi_ref, ba_ref, bi_ref, lam_ref, h_ref, c_scr):
        s = pl.program_id(1)

        @pl.when(s == 0)
        def _():
            c_scr[...] = jnp.zeros_like(c_scr)

        u = u_ref[...]
        r, ig, sp, la, a, g = _lru_gates(u, wa_ref, wi_ref, ba_ref, bi_ref, lam_ref)
        h, carry = _chunk_scan(a, g * ig * u, c_scr[0:1, :], not rev)
        h_ref[...] = h
        c_scr[0:1, :] = carry

    return pl.pallas_call(
        body, name="lru_fwd_rev" if rev else "lru_fwd", grid=(cfg.Bn, nch),
        in_specs=[pl.BlockSpec((CH, LW), lambda b, s: (b * nch + cmap(s), ucol))] + _lru_w_specs(d),
        out_specs=pl.BlockSpec((CH, LW), lambda b, s: (b * nch + cmap(s), 0)),
        out_shape=jax.ShapeDtypeStruct((cfg.NT, LW), F32),
        scratch_shapes=[pltpu.VMEM((8, LW), F32)],
        compiler_params=_cp(2),
    )(act, wa, wi, ba, bi, lam)


def _lru_bwd(cfg, act, wa, wi, ba, bi, lam, hd, dyl, rev):
    nch, nct, nlt = cfg.nch, cfg.nct, cfg.nlt
    cmap = _chunk_of_step(cfg, rev)
    d = 1 if rev else 0
    ucol = (DI + 2 * NG * NS) // LW

    def srow(b, sp):
        return b * nch + cmap(nch - 1 - sp)

    def prev_rows(b, sp):
        s = nch - 1 - sp
        cp = cmap(jnp.maximum(s - 1, 0))
        base = (b * nch + cp) * (CH // 8)
        return base + (0 if rev else CH // 8 - 1)

    def lat_row(b, sp):
        c = cmap(nch - 1 - sp)
        return b * nlt + jnp.maximum(c - nct, 0)

    def body(u_ref, wa_ref, wi_ref, ba_ref, bi_ref, lam_ref, h_ref, hp_ref, dy_ref,
             du_ref, dwa_ref, dwi_ref, vec_ref, c_scr):
        b = pl.program_id(0)
        sp_id = pl.program_id(1)
        s = nch - 1 - sp_id

        @pl.when(sp_id == 0)
        def _():
            c_scr[...] = jnp.zeros_like(c_scr)

        @pl.when(jnp.logical_and(b == 0, sp_id == 0))
        def _():
            dwa_ref[...] = jnp.zeros_like(dwa_ref)
            dwi_ref[...] = jnp.zeros_like(dwi_ref)
            vec_ref[...] = jnp.zeros_like(vec_ref)

        c = cmap(s)
        u = u_ref[...]
        r, ig, spl, la, a, g = _lru_gates(u, wa_ref, wi_ref, ba_ref, bi_ref, lam_ref)
        dh = jnp.where(c < nct, 0.0, dy_ref[...])
        rowi = lax.broadcasted_iota(jnp.int32, (CH, LW), 0)
        lamv, _ = _chunk_scan(_shift_rows(a, 1, 1.0, rev, rowi), dh, c_scr[0:1, :], rev)
        first = CH - 1 if rev else 0
        c_scr[0:1, :] = (a * lamv)[first:first + 1, :]
        hprow = hp_ref[...][(0 if rev else 7):(1 if rev else 8), :]
        hprow = jnp.where(s > 0, hprow, 0.0)
        h_prev = _shift_rows(h_ref[...], 1, hprow, not rev, rowi)
        da = lamv * h_prev
        db = lamv
        iu = ig * u
        dla = da * a - db * iu * (a * a) / g
        dr = dla * (-LRU_C * spl)
        di = db * g * u
        du = db * g * ig
        drp = dr * r * (1.0 - r)
        dip = di * ig * (1.0 - ig)
        dus = []
        for k in range(LB):
            sl = slice(k * LBW, (k + 1) * LBW)
            drk = drp[:, sl].astype(BF16)
            dik = dip[:, sl].astype(BF16)
            uk = u[:, sl].astype(BF16)
            dus.append(_dot_nt(drk, wa_ref[k].astype(BF16)) + _dot_nt(dik, wi_ref[k].astype(BF16)))
            dwa_ref[k] += _dot_tn(uk, drk)
            dwi_ref[k] += _dot_tn(uk, dik)
        du_ref[...] = du + jnp.concatenate(dus, axis=1)
        vec_ref[0:1, :] += jnp.sum(drp, axis=0, keepdims=True)
        vec_ref[1:2, :] += jnp.sum(dip, axis=0, keepdims=True)
        dsp = jnp.sum(dla * (-LRU_C * r), axis=0, keepdims=True)
        vec_ref[2:3, :] += dsp * (-_sigmoid(-lam_ref[...]))

    return pl.pallas_call(
        body, name="lru_bwd_rev" if rev else "lru_bwd", grid=(cfg.Bn, nch),
        in_specs=[pl.BlockSpec((CH, LW), lambda b, sp: (srow(b, sp), ucol))] + _lru_w_specs(d) + [
            pl.BlockSpec((CH, LW), lambda b, sp: (srow(b, sp), 0)),
            pl.BlockSpec((8, LW), lambda b, sp: (prev_rows(b, sp), 0)),
            pl.BlockSpec((CH, LW), lambda b, sp: (lat_row(b, sp), 0))],
        out_specs=[pl.BlockSpec((CH, LW), lambda b, sp: (srow(b, sp), 0)),
                   pl.BlockSpec((LB, LBW, LBW), lambda b, sp: (0, 0, 0)),
                   pl.BlockSpec((LB, LBW, LBW), lambda b, sp: (0, 0, 0)),
                   pl.BlockSpec((8, LW), lambda b, sp: (0, 0))],
        out_shape=[jax.ShapeDtypeStruct((cfg.NT, LW), F32),
                   jax.ShapeDtypeStruct((LB, LBW, LBW), F32),
                   jax.ShapeDtypeStruct((LB, LBW, LBW), F32),
                   jax.ShapeDtypeStruct((8, LW), F32)],
        scratch_shapes=[pltpu.VMEM((8, LW), F32)],
        compiler_params=_cp(2),
    )(act, wa, wi, ba, bi, lam, hd, hd, dyl)


HB = 1024


def _post_ssd_fwd(cfg, yf, yb, act, proj, dvec, norm_w):
    TB, nbt, nbl = cfg.TB, cfg.nbt, cfg.nbl
    zc = CONVW // HB

    def body(yf_ref, yb_ref, xh_ref, z_ref, dv_ref, w_ref, o_ref):
        y = yf_ref[...] + yb_ref[...] + dv_ref[...] * xh_ref[...]
        u = y * _silu(z_ref[...])
        for gi in range(HB // (DI // NG)):
            sl = slice(gi * 256, (gi + 1) * 256)
            ug = u[:, sl]
            rs = lax.rsqrt(jnp.mean(ug * ug, axis=1, keepdims=True) + RMS_EPS)
            o_ref[:, sl] = (ug * rs * w_ref[:, sl]).astype(BF16)

    def st(b, j, cb):
        return (b * nbt + 1 + j, cb)
    return pl.pallas_call(
        body, name="post_ssd_fwd", grid=(cfg.Bn, nbl, DI // HB),
        in_specs=[pl.BlockSpec((TB, HB), st), pl.BlockSpec((TB, HB), st), pl.BlockSpec((TB, HB), st),
                  pl.BlockSpec((TB, HB), lambda b, j, cb: (b * nbt + 1 + j, zc + cb)),
                  pl.BlockSpec((1, HB), lambda b, j, cb: (0, cb)),
                  pl.BlockSpec((1, HB), lambda b, j, cb: (0, cb))],
        out_specs=pl.BlockSpec((TB, HB), lambda b, j, cb: (b * nbl + j, cb)),
        out_shape=jax.ShapeDtypeStruct((cfg.N, DI), BF16),
        compiler_params=_cp(3),
    )(yf, yb, act, proj, dvec, norm_w)


def _post_ssd_bwd(cfg, dproj, dn, yf, yb, act, proj, dvec, norm_w):
    TB, nbt, nbl = cfg.TB, cfg.nbt, cfg.nbl
    zc = CONVW // HB

    def body(_, dn_ref, yf_ref, yb_ref, xh_ref, z_ref, dv_ref, w_ref, dy_ref, dz_ref, acc_ref):
        b = pl.program_id(1)
        j = pl.program_id(2)

        @pl.when(jnp.logical_and(b == 0, j == 0))
        def _():
            acc_ref[...] = jnp.zeros_like(acc_ref)

        @pl.when(j == 0)
        def _():
            dz_ref[...] = jnp.zeros_like(dz_ref)

        @pl.when(j > 0)
        def _():
            latent(dn_ref, yf_ref, yb_ref, xh_ref, z_ref, dv_ref, w_ref, dy_ref, dz_ref, acc_ref)

    def latent(dn_ref, yf_ref, yb_ref, xh_ref, z_ref, dv_ref, w_ref, dy_ref, dz_ref, acc_ref):
        xh = xh_ref[...]
        z = z_ref[...]
        y = yf_ref[...] + yb_ref[...] + dv_ref[...] * xh
        sz = _silu(z)
        u = y * sz
        dout = dn_ref[...]
        for gi in range(HB // (DI // NG)):
            sl = slice(gi * 256, (gi + 1) * 256)
            ug0 = u[:, sl]
            rs = lax.rsqrt(jnp.mean(ug0 * ug0, axis=1, keepdims=True) + RMS_EPS)
            ug = ug0 * rs
            do = dout[:, sl]
            acc_ref[0:1, sl] += jnp.sum(do * ug, axis=0, keepdims=True)
            dug = do * w_ref[:, sl]
            du = rs * (dug - ug * jnp.mean(dug * ug, axis=1, keepdims=True))
            dy = du * sz[:, sl]
            dy_ref[:, sl] = dy
            dz_ref[:, sl] = (du * y[:, sl] * _silu_grad(z[:, sl])).astype(BF16)
            acc_ref[1:2, sl] += jnp.sum(dy * xh[:, sl], axis=0, keepdims=True)

    def st(cb, b, j):
        return (b * nbt + j, cb)

    def la(cb, b, j):
        return (b * nbl + jnp.maximum(j - 1, 0), cb)
    return pl.pallas_call(
        body, name="post_ssd_bwd", grid=(DI // HB, cfg.Bn, nbt),
        in_specs=[_ANY, pl.BlockSpec((TB, HB), la), pl.BlockSpec((TB, HB), st), pl.BlockSpec((TB, HB), st),
                  pl.BlockSpec((TB, HB), st),
                  pl.BlockSpec((TB, HB), lambda cb, b, j: (b * nbt + j, zc + cb)),
                  pl.BlockSpec((1, HB), lambda cb, b, j: (0, cb)),
                  pl.BlockSpec((1, HB), lambda cb, b, j: (0, cb))],
        out_specs=[pl.BlockSpec((TB, HB), la),
                   pl.BlockSpec((TB, HB), lambda cb, b, j: (b * nbt + j, zc + cb)),
                   pl.BlockSpec((8, HB), lambda cb, b, j: (0, cb))],
        out_shape=[jax.ShapeDtypeStruct((cfg.N, DI), F32), jax.ShapeDtypeStruct((cfg.NT, PM), BF16),
                   jax.ShapeDtypeStruct((8, DI), F32)],
        input_output_aliases={0: 1},
        compiler_params=_cp(3),
    )(dproj, dn, yf, yb, act, proj, dvec, norm_w)


def _post_lru_fwd(cfg, hf, hb, proj):
    TB, nbt, nbl = cfg.TB, cfg.nbt, cfg.nbl
    gc = (CONVW + DI) // HB

    def body(hf_ref, hb_ref, g_ref, o_ref):
        o_ref[...] = ((hf_ref[...] + hb_ref[...]) * _gelu(g_ref[...])).astype(BF16)

    st = pl.BlockSpec((TB, HB), lambda b, j: (b * nbt + 1 + j, 0))
    return pl.pallas_call(
        body, name="post_lru_fwd", grid=(cfg.Bn, nbl),
        in_specs=[st, st, pl.BlockSpec((TB, HB), lambda b, j: (b * nbt + 1 + j, gc))],
        out_specs=pl.BlockSpec((TB, HB), lambda b, j: (b * nbl + j, 0)),
        out_shape=jax.ShapeDtypeStruct((cfg.N, LW), BF16),
        compiler_params=_cp(2),
    )(hf, hb, proj)


def _post_lru_bwd(cfg, dproj, dv, hf, hb, proj):
    TB, nbt, nbl = cfg.TB, cfg.nbt, cfg.nbl
    gc = (CONVW + DI) // HB

    def body(_, dv_ref, hf_ref, hb_ref, g_ref, dy_ref, dg_ref):
        j = pl.program_id(1)

        @pl.when(j == 0)
        def _():
            dg_ref[...] = jnp.zeros_like(dg_ref)

        @pl.when(j > 0)
        def _():
            gt = g_ref[...]
            dvv = dv_ref[...]
            dy_ref[...] = dvv * _gelu(gt)
            dg_ref[...] = (dvv * (hf_ref[...] + hb_ref[...]) * _gelu_grad(gt)).astype(BF16)

    st = pl.BlockSpec((TB, HB), lambda b, j: (b * nbt + j, 0))
    la = pl.BlockSpec((TB, HB), lambda b, j: (b * nbl + jnp.maximum(j - 1, 0), 0))
    gcol = pl.BlockSpec((TB, HB), lambda b, j: (b * nbt + j, gc))
    return pl.pallas_call(
        body, name="post_lru_bwd", grid=(cfg.Bn, nbt),
        in_specs=[_ANY, la, st, st, gcol],
        out_specs=[la, gcol],
        out_shape=[jax.ShapeDtypeStruct((cfg.N, LW), F32), jax.ShapeDtypeStruct((cfg.NT, PM), BF16)],
        input_output_aliases={0: 1},
        compiler_params=_cp(2),
    )(dproj, dv, hf, hb, proj)


def _merge_fwd(cfg, proj, b_gate, br_ssd, br_lru):
    TB, nbt, nbl = cfg.TB, cfg.nbt, cfg.nbl
    mc = (CONVW + DI + LW) // HB

    def body(ms_ref, ml_ref, bg_ref, bs_ref, bl_ref, o_ref):
        gs = _sigmoid(ms_ref[...] + bg_ref[:, :D])
        gl = _sigmoid(ml_ref[...] + bg_ref[:, D:])
        o_ref[...] = (gs * bs_ref[...] + gl * bl_ref[...]).astype(BF16)

    la = pl.BlockSpec((TB, D), lambda b, j: (b * nbl + j, 0))
    return pl.pallas_call(
        body, name="merge_fwd", grid=(cfg.Bn, nbl),
        in_specs=[pl.BlockSpec((TB, HB), lambda b, j: (b * nbt + 1 + j, mc)),
                  pl.BlockSpec((TB, HB), lambda b, j: (b * nbt + 1 + j, mc + 1)),
                  pl.BlockSpec((1, 2 * D), lambda b, j: (0, 0)), la, la],
        out_specs=la,
        out_shape=jax.ShapeDtypeStruct((cfg.N, D), BF16),
        compiler_params=_cp(2),
    )(proj, proj, b_gate, br_ssd, br_lru)


def _merge_bwd(cfg, dmix, proj, b_gate, br_ssd, br_lru):
    TB, nbt, nbl = cfg.TB, cfg.nbt, cfg.nbl
    mc = (CONVW + DI + LW) // HB

    def body(dm_ref, ms_ref, ml_ref, bg_ref, bs_ref, bl_ref, ds_ref, dl_ref, dmg_ref, acc_ref):
        b = pl.program_id(0)
        j = pl.program_id(1)

        @pl.when(jnp.logical_and(b == 0, j == 0))
        def _():
            acc_ref[...] = jnp.zeros_like(acc_ref)

        @pl.when(j == 0)
        def _():
            dmg_ref[...] = jnp.zeros_like(dmg_ref)

        @pl.when(j > 0)
        def _():
            latent(dm_ref, ms_ref, ml_ref, bg_ref, bs_ref, bl_ref, ds_ref, dl_ref, dmg_ref, acc_ref)

    def latent(dm_ref, ms_ref, ml_ref, bg_ref, bs_ref, bl_ref, ds_ref, dl_ref, dmg_ref, acc_ref):
        dm = dm_ref[...]
        gs = _sigmoid(ms_ref[...] + bg_ref[:, :D])
        gl = _sigmoid(ml_ref[...] + bg_ref[:, D:])
        ds_ref[...] = (dm * gs).astype(BF16)
        dl_ref[...] = (dm * gl).astype(BF16)
        dps = dm * bs_ref[...] * gs * (1.0 - gs)
        dpl = dm * bl_ref[...] * gl * (1.0 - gl)
        dmg_ref[:, :D] = dps.astype(BF16)
        dmg_ref[:, D:] = dpl.astype(BF16)
        acc_ref[0:1, :D] += jnp.sum(dps, axis=0, keepdims=True)
        acc_ref[0:1, D:] += jnp.sum(dpl, axis=0, keepdims=True)

    la = pl.BlockSpec((TB, D), lambda b, j: (b * nbl + jnp.maximum(j - 1, 0), 0))
    return pl.pallas_call(
        body, name="merge_bwd", grid=(cfg.Bn, nbt),
        in_specs=[la, pl.BlockSpec((TB, HB), lambda b, j: (b * nbt + j, mc)),
                  pl.BlockSpec((TB, HB), lambda b, j: (b * nbt + j, mc + 1)),
                  pl.BlockSpec((1, 2 * D), lambda b, j: (0, 0)), la, la],
        out_specs=[la, la, pl.BlockSpec((TB, 2 * D), lambda b, j: (b * nbt + j, mc // 2)),
                   pl.BlockSpec((8, 2 * D), lambda b, j: (0, 0))],
        out_shape=[jax.ShapeDtypeStruct((cfg.N, D), BF16), jax.ShapeDtypeStruct((cfg.N, D), BF16),
                   jax.ShapeDtypeStruct((cfg.NT, PM), BF16), jax.ShapeDtypeStruct((8, 2 * D), F32)],
        compiler_params=_cp(2),
    )(dmix, proj, proj, b_gate, br_ssd, br_lru)


def _resid1_fwd(cfg, xs, x_mix, gate1, shift2, scale2, ln1_g, ln1_b):
    TB, nbt, nbl = cfg.TB, cfg.nbt, cfg.nbl

    def body(x_ref, xm_ref, g1_ref, sh_ref, sc_ref, lg_ref, lb_ref, x1_ref, h2_ref):
        r1 = ALPHA * x_ref[...] + g1_ref[...] * xm_ref[...]
        xh, _ = _ln(r1)
        x1 = xh * lg_ref[...] + lb_ref[...]
        x1_ref[...] = x1
        xh2, _ = _ln(x1)
        h2_ref[...] = (xh2 * (1.0 + sc_ref[...]) + sh_ref[...]).astype(BF16)

    la = pl.BlockSpec((TB, D), lambda b, j: (b * nbl + j, 0))
    ex = pl.BlockSpec((None, 1, D), lambda b, j: (b, 0, 0))
    vec = pl.BlockSpec((1, D), lambda b, j: (0, 0))
    return pl.pallas_call(
        body, name="resid1_fwd", grid=(cfg.Bn, nbl),
        in_specs=[pl.BlockSpec((TB, D), lambda b, j: (b * nbt + 1 + j, 0)), la, ex, ex, ex, vec, vec],
        out_specs=[la, la],
        out_shape=[jax.ShapeDtypeStruct((cfg.N, D), F32), jax.ShapeDtypeStruct((cfg.N, D), BF16)],
        compiler_params=_cp(2),
    )(xs, x_mix, gate1, shift2, scale2, ln1_g, ln1_b)


def _resid1_bwd(cfg, dh2, dx1p, x1, xs, x_mix, gate1, scale2, ln1_g):
    TB, nbt, nbl = cfg.TB, cfg.nbt, cfg.nbl

    def body(dh2_ref, dx1p_ref, x1_ref, x_ref, xm_ref, g1_ref, sc_ref, lg_ref,
             dxm_ref, dxp_ref, ex_ref, gl_ref):
        b = pl.program_id(0)
        j = pl.program_id(1)

        @pl.when(j == 0)
        def _():
            ex_ref[...] = jnp.zeros_like(ex_ref)

        @pl.when(jnp.logical_and(b == 0, j == 0))
        def _():
            gl_ref[...] = jnp.zeros_like(gl_ref)

        dh2 = dh2_ref[...]
        xh2, rs2 = _ln(x1_ref[...])
        ex_ref[0:1, :] += jnp.sum(dh2, axis=0, keepdims=True)
        ex_ref[1:2, :] += jnp.sum(dh2 * xh2, axis=0, keepdims=True)
        dx1 = dx1p_ref[...] + _ln_bwd(dh2 * (1.0 + sc_ref[...]), xh2, rs2)
        xm = xm_ref[...]
        g1 = g1_ref[...]
        r1 = ALPHA * x_ref[...] + g1 * xm
        xh1, rs1 = _ln(r1)
        gl_ref[0:1, :] += jnp.sum(dx1 * xh1, axis=0, keepdims=True)
        gl_ref[1:2, :] += jnp.sum(dx1, axis=0, keepdims=True)
        dr1 = _ln_bwd(dx1 * lg_ref[...], xh1, rs1)
        ex_ref[2:3, :] += jnp.sum(dr1 * xm, axis=0, keepdims=True)
        dxm_ref[...] = (dr1 * g1).astype(BF16)
        dxp_ref[...] = ALPHA * dr1

    la = pl.BlockSpec((TB, D), lambda b, j: (b * nbl + j, 0))
    ex = pl.BlockSpec((None, 1, D), lambda b, j: (b, 0, 0))
    vec = pl.BlockSpec((1, D), lambda b, j: (0, 0))
    return pl.pallas_call(
        body, name="resid1_bwd", grid=(cfg.Bn, nbl),
        in_specs=[la, la, la, pl.BlockSpec((TB, D), lambda b, j: (b * nbt + 1 + j, 0)), la, ex, ex, vec],
        out_specs=[la, la, pl.BlockSpec((None, 8, D), lambda b, j: (b, 0, 0)),
                   pl.BlockSpec((8, D), lambda b, j: (0, 0))],
        out_shape=[jax.ShapeDtypeStruct((cfg.N, D), BF16), jax.ShapeDtypeStruct((cfg.N, D), F32),
                   jax.ShapeDtypeStruct((cfg.Bn, 8, D), F32), jax.ShapeDtypeStruct((8, D), F32)],
        compiler_params=_cp(2),
    )(dh2, dx1p, x1, xs, x_mix, gate1, scale2, ln1_g)


def _final_fwd_bwd(cfg, x1, mlp, b2, gate2, ln2_g, ln2_b, target):
    TB, nbl = cfg.TB, cfg.nbl

    def body(x1_ref, m_ref, b2_ref, g2_ref, lg_ref, lb_ref, t_ref, dm_ref, dx_ref, ex_ref, gl_ref):
        b = pl.program_id(0)
        j = pl.program_id(1)

        @pl.when(j == 0)
        def _():
            ex_ref[...] = jnp.zeros_like(ex_ref)

        @pl.when(jnp.logical_and(b == 0, j == 0))
        def _():
            gl_ref[...] = jnp.zeros_like(gl_ref)

        mv = m_ref[...] + b2_ref[...]
        g2 = g2_ref[...]
        r2 = ALPHA * x1_ref[...] + g2 * mv
        xh, rs = _ln(r2)
        lg = lg_ref[...]
        x2 = xh * lg + lb_ref[...]
        err = x2 - t_ref[...]
        ls = jnp.sum(jnp.sum(err * err, axis=1, keepdims=True), axis=0, keepdims=True) * (0.5 / D)
        gl_ref[3:4, :] += ls
        dx2 = err * (1.0 / D)
        gl_ref[0:1, :] += jnp.sum(dx2 * xh, axis=0, keepdims=True)
        gl_ref[1:2, :] += jnp.sum(dx2, axis=0, keepdims=True)
        dr2 = _ln_bwd(dx2 * lg, xh, rs)
        ex_ref[0:1, :] += jnp.sum(dr2 * mv, axis=0, keepdims=True)
        dmv = dr2 * g2
        gl_ref[2:3, :] += jnp.sum(dmv, axis=0, keepdims=True)
        dm_ref[...] = dmv.astype(BF16)
        dx_ref[...] = ALPHA * dr2

    la = pl.BlockSpec((TB, D), lambda b, j: (b * nbl + j, 0))
    ex = pl.BlockSpec((None, 1, D), lambda b, j: (b, 0, 0))
    vec = pl.BlockSpec((1, D), lambda b, j: (0, 0))
    return pl.pallas_call(
        body, name="final_fwd_bwd", grid=(cfg.Bn, nbl),
        in_specs=[la, la, vec, ex, vec, vec, la],
        out_specs=[la, la, pl.BlockSpec((None, 8, D), lambda b, j: (b, 0, 0)),
                   pl.BlockSpec((8, D), lambda b, j: (0, 0))],
        out_shape=[jax.ShapeDtypeStruct((cfg.N, D), BF16), jax.ShapeDtypeStruct((cfg.N, D), F32),
                   jax.ShapeDtypeStruct((cfg.Bn, 8, D), F32), jax.ShapeDtypeStruct((8, D), F32)],
        compiler_params=_cp(2),
    )(x1, mlp, b2, gate2, ln2_g, ln2_b, target)


def _ln_mod_bwd(cfg, dh_a, dh_b, xs, scale_tab, dxp):
    TB, nbt, nbl = cfg.TB, cfg.nbt, cfg.nbl

    def body(da_ref, db_ref, x_ref, sc_ref, dxp_ref, gx_ref, acc_ref):
        j = pl.program_id(1)

        @pl.when(j <= 1)
        def _():
            acc_ref[...] = jnp.zeros_like(acc_ref)

        dh = da_ref[...] + db_ref[...]
        xhat, rs = _ln(x_ref[...])
        acc_ref[0:1, :] += jnp.sum(dh, axis=0, keepdims=True)
        acc_ref[1:2, :] += jnp.sum(dh * xhat, axis=0, keepdims=True)
        gx_ref[...] = dxp_ref[...] + _ln_bwd(dh * (1.0 + sc_ref[...]), xhat, rs)

    st = pl.BlockSpec((TB, D), lambda b, j: (b * nbt + j, 0))
    la = pl.BlockSpec((TB, D), lambda b, j: (b * nbl + jnp.maximum(j - 1, 0), 0))
    return pl.pallas_call(
        body, name="ln_mod_bwd", grid=(cfg.Bn, nbt),
        in_specs=[st, st, st,
                  pl.BlockSpec((None, 1, D), lambda b, j: (2 * b + jnp.minimum(j, 1), 0, 0)), la],
        out_specs=[la, pl.BlockSpec((None, 8, D), lambda b, j: (2 * b + jnp.minimum(j, 1), 0, 0))],
        out_shape=[jax.ShapeDtypeStruct((cfg.N, D), F32), jax.ShapeDtypeStruct((2 * cfg.Bn, 8, D), F32)],
        compiler_params=_cp(2),
    )(dh_a, dh_b, xs, scale_tab, dxp)


def _perm_w_in(w_in):
    w_main = jnp.concatenate([w_in[:, 0:3072], w_in[:, 4160:5184], w_in[:, 3136:4160], w_in[:, 5184:10304]], axis=1)
    w_dt = jnp.pad(w_in[:, 3072:3136], ((0, 0), (0, DTW - 2 * NH)))
    return w_main, w_dt


def _unperm_w_in(dw_main, dw_dt):
    return jnp.concatenate([dw_main[:, 0:3072], dw_dt[:, :2 * NH], dw_main[:, 4096:5120],
                            dw_main[:, 3072:4096], dw_main[:, 5120:]], axis=1)


def _unpack_rest(rest_all):
    out, off = {}, 0
    for n, shp, axis in _BIG[1:]:
        shard_shape = (shp[0] // NDEV, shp[1]) if axis == 0 else (shp[0], shp[1] // NDEV)
        r = math.prod(shard_shape) // 1024
        out[n] = _from_slots(rest_all[:, off:off + r, :].reshape((NDEV,) + shard_shape), axis)
        off += r
    return out


def _local_step(cfg, x, ctx, target, m, mc, W, rest_payload):
    Bn, T, Tc = cfg.Bn, cfg.T, cfg.Tc
    NT, N = cfg.NT, cfg.N
    xs = jnp.concatenate([ctx, x], axis=1).reshape(NT, D)
    mch = [m[:, i * D:(i + 1) * D] for i in range(NMOD)]
    ctx_sh = jnp.broadcast_to(mc[None, :D], (Bn, D))
    ctx_sc = jnp.broadcast_to(mc[None, D:], (Bn, D))
    shift_tab = jnp.stack([ctx_sh, mch[0]], axis=1).reshape(2 * Bn, 1, D)
    scale_tab = jnp.stack([ctx_sc, mch[1]], axis=1).reshape(2 * Bn, 1, D)
    gate1 = mch[2].reshape(Bn, 1, D)
    shift2 = mch[3].reshape(Bn, 1, D)
    scale2 = mch[4].reshape(Bn, 1, D)
    gate2 = mch[5].reshape(Bn, 1, D)

    conv_w = jnp.concatenate([W["ssd_conv_w"], W["lru_conv_w"]], axis=1)
    conv_b = jnp.concatenate([W["ssd_conv_b"], W["lru_conv_b"]], axis=1)
    dt_bias = jnp.pad(W["ssd_dt_bias"].reshape(1, 2 * NH), ((0, 0), (0, DTW - 2 * NH)))
    a_log = jnp.pad(W["ssd_a_log"].reshape(1, 2 * NH), ((0, 0), (0, DTW - 2 * NH)))
    dvec = jnp.repeat(W["ssd_d"].reshape(NH), HD).reshape(1, DI)
    lba = W["lru_ba"].reshape(2, 1, LW)
    lbi = W["lru_bi"].reshape(2, 1, LW)
    llam = W["lru_lambda"].reshape(2, 1, LW)

    h = _ln_mod_fwd(cfg, xs, shift_tab, scale_tab)
    proj, rest_all = _mm(h, W["w_main"], "nn", "mm_proj", tm=1024, tn=1024, tk=1024, xch=(rest_payload, True))
    W = dict(W, **_unpack_rest(rest_all))
    dt_raw = _mm(h, W["w_dt"], "nn", "mm_dt", tm=512, tn=DTW, tk=1024)
    dt, dtg, cumg, cumTg = _dt_fwd(cfg, dt_raw, dt_bias, a_log)
    act, sgrad = _conv_fwd(cfg, proj, conv_w, conv_b)
    ys, hss, hls = [], [], []
    for rev in (False, True):
        y_d, hs_d = _ssd_fwd(cfg, act, dtg, cumg, cumTg, rev)
        ys.append(y_d)
        hss.append(hs_d)
        hls.append(_lru_fwd(cfg, act, W["lru_wa"], W["lru_wi"], lba, lbi, llam, rev))
    nssd = _post_ssd_fwd(cfg, ys[0], ys[1], act, proj, dvec, W["ssd_norm_w"])
    vlru = _post_lru_fwd(cfg, hls[0], hls[1], proj)
    br_ssd = _mm(nssd, W["w_br_ssd"], "nn", "mm_br_ssd", tm=1024, tn=1024, tk=1024)
    br_lru = _mm(vlru, W["w_br_lru"], "nn", "mm_br_lru", tm=1024, tn=1024, tk=1024)
    mix = _merge_fwd(cfg, proj, W["b_gate"], br_ssd, br_lru)
    x_mix = _mm(mix, W["w_out"], "nn", "mm_out", tm=1024, tn=1024, tk=1024)
    x1, h2 = _resid1_fwd(cfg, xs, x_mix, gate1, shift2, scale2, W["ln1_g"], W["ln1_b"])
    a1, actm = _mm_mlp1(h2, W["w_mlp1"], W["b_mlp1"])
    mlp = _mm(actm, W["w_mlp2"], "nn", "mm_mlp2", tm=1024, tn=1024, tk=1024)
    dmlp, dx1p, ex2, gl2 = _final_fwd_bwd(cfg, x1, mlp, W["b_mlp2"], gate2, W["ln2_g"], W["ln2_b"],
                                          target.reshape(N, D))

    g = {}
    g["ln2_g"], g["ln2_b"], g["b_mlp2"] = gl2[0:1], gl2[1:2], gl2[2:3]
    loss_partial = gl2[3, 0]
    gw = {}
    gw["w_mlp2"] = _mm(actm, dmlp, "tn", "mm_dw_mlp2", BF16, tm=1024, tn=1024, tk=512)
    da1, accb1 = _mm_dact(dmlp, W["w_mlp2"], a1)
    g["b_mlp1"] = accb1[0:1]
    dh2 = _mm(da1, W["w_mlp1"], "nt", "mm_dh2", tm=1024, tn=1024, tk=1024)
    gw["w_mlp1"] = _mm(h2, da1, "tn", "mm_dw_mlp1", BF16, tm=1024, tn=1024, tk=512)
    dx_mix, dxp, ex1, gl1 = _resid1_bwd(cfg, dh2, dx1p, x1, xs, x_mix, gate1, scale2, W["ln1_g"])
    g["ln1_g"], g["ln1_b"] = gl1[0:1], gl1[1:2]
    dmix = _mm(dx_mix, W["w_out"], "nt", "mm_dmix", tm=1024, tn=1024, tk=1024)
    gw["w_out"] = _mm(mix, dx_mix, "tn", "mm_dw_out", BF16, tm=1024, tn=1024, tk=512)
    dbs, dbl, dproj, accg = _merge_bwd(cfg, dmix, proj, W["b_gate"], br_ssd, br_lru)
    g["b_gate"] = accg[0:1]
    dnssd = _mm(dbs, W["w_br_ssd"], "nt", "mm_dnssd", tm=1024, tn=1024, tk=1024)
    gw["w_br_ssd"] = _mm(nssd, dbs, "tn", "mm_dw_br_ssd", BF16, tm=1024, tn=1024, tk=512)
    dvlru = _mm(dbl, W["w_br_lru"], "nt", "mm_dvlru", tm=1024, tn=1024, tk=1024)
    gw["w_br_lru"] = _mm(vlru, dbl, "tn", "mm_dw_br_lru", BF16, tm=1024, tn=1024, tk=512)
    dy, dproj, accs = _post_ssd_bwd(cfg, dproj, dnssd, ys[0], ys[1], act, proj, dvec, W["ssd_norm_w"])
    g["ssd_norm_w"] = accs[0:1]
    dD_cols = accs[1:2]
    dyl, dproj = _post_lru_bwd(cfg, dproj, dvlru, hls[0], hls[1], proj)

    rest_slots = jnp.concatenate([_to_slots(gw[n], axis).reshape(NDEV, -1, 1024) for n, _, axis in _BIG[1:]], axis=1)
    xres = {}
    dxh, dBs, dCs, dAs, dxxs, dus = [], [], [], [], [], []
    dwas, dwis, lvecs = [], [], []
    for i, rev in enumerate((False, True)):
        if i == 0:
            o, xres["rs_rest"] = _ssd_bwd(cfg, act, dtg, cumg, cumTg, hss[i], dy, rev, xch=(rest_slots, False))
        else:
            o = _ssd_bwd(cfg, act, dtg, cumg, cumTg, hss[i], dy, rev)
        dxh.append(o[0]); dBs.append(o[1]); dCs.append(o[2]); dAs.append(o[3]); dxxs.append(o[4])
        du, dwa, dwi, lv = _lru_bwd(cfg, act, W["lru_wa"], W["lru_wi"], lba, lbi, llam, hls[i], dyl, rev)
        dus.append(du); dwas.append(dwa); dwis.append(dwi); lvecs.append(lv)
    lru_payload = jnp.stack([jnp.stack(dwas), jnp.stack(dwis)]).reshape(-1, 1024)
    g["lru_ba"] = jnp.stack([lvecs[0][0], lvecs[1][0]])
    g["lru_bi"] = jnp.stack([lvecs[0][1], lvecs[1][1]])
    g["lru_lambda"] = jnp.stack([lvecs[0][2], lvecs[1][2]])

    ddt_raw, accdt = _dt_bwd(cfg, dAs, dxxs, dt_raw, dt, dt_bias, a_log)
    g["ssd_a_log"] = accdt[0, :2 * NH].reshape(2, NH)
    g["ssd_dt_bias"] = accdt[1, :2 * NH].reshape(2, NH)

    (dproj, accx), xres["ag_lru"] = _conv_bwd(cfg, "conv_bwd_x", dproj, proj, conv_w, sgrad, [dxh[0], dxh[1]], 0, DI,
                                              skip=(dy, dvec), xch=(lru_payload, True))
    dproj, accB = _conv_bwd(cfg, "conv_bwd_b", dproj, proj, conv_w, sgrad, [dBs[0], dBs[1]], DI, NG * NS)
    dproj, accC = _conv_bwd(cfg, "conv_bwd_c", dproj, proj, conv_w, sgrad, [dCs[0], dCs[1]], DI + NG * NS, NG * NS)
    dproj, accl = _conv_bwd(cfg, "conv_bwd_lru", dproj, proj, conv_w, sgrad, [dus[0], dus[1]], DI + 2 * NG * NS, LW)
    accssd = jnp.concatenate([accx, accB, accC], axis=1)
    g["ssd_conv_w"], g["ssd_conv_b"] = accssd[0:4], accssd[4:5]
    g["lru_conv_w"], g["lru_conv_b"] = accl[0:4], accl[4:5]
    dw_main = _mm(h, dproj, "tn", "mm_dw_main", BF16, tm=1024, tn=2048, tk=512)
    dw_dt = _mm(h, ddt_raw, "tn", "mm_dw_dt", BF16, tm=1024, tn=DTW, tk=512)
    w_in_slots = _to_slots(_unperm_w_in(dw_main, dw_dt), 1)
    dh_a, xres["rs_w_in"] = _mm(dproj, W["w_main"], "nt", "mm_dh_main", tm=1024, tn=1024, tk=1024,
                                xch=(w_in_slots, False))
    dh_b = _mm(ddt_raw, W["w_dt"], "nt", "mm_dh_dt", tm=512, tn=1024, tk=DTW)
    grad_x, acct = _ln_mod_bwd(cfg, dh_a, dh_b, xs, scale_tab, dxp)
    acct = acct.reshape(Bn, 2, 8, D)
    dm = jnp.concatenate([acct[:, 1, 0], acct[:, 1, 1], ex1[:, 2], ex1[:, 0], ex1[:, 1], ex2[:, 0]], axis=1)
    dmc = jnp.concatenate([acct[:, 0, 0], acct[:, 0, 1]], axis=1)
    g["ssd_d_cols"] = dD_cols
    return loss_partial, grad_x.reshape(Bn, T, D), g, dm, dmc, xres


MESH = pl.DeviceIdType.MESH
_HBM = pl.BlockSpec(memory_space=pltpu.HBM)


def _me():
    return 4 * lax.axis_index("x") + 2 * lax.axis_index("y") + lax.axis_index("c")


def _peer(k):
    px = (lax.axis_index("x") + ((k >> 2) & 1)) % 2
    py = (lax.axis_index("y") + ((k >> 1) & 1)) % 2
    pc = (lax.axis_index("c") + (k & 1)) % 2
    return (px, py, pc), 4 * px + 2 * py + pc


def _xchg_copies(x_ref, o_ref, send_sems, recv_sems, loc_sem, gather):
    me = _me()
    src_me = x_ref if gather else x_ref.at[me]
    loc = pltpu.make_async_copy(src_me, o_ref.at[me], loc_sem)
    sends, recvs = [], []
    for k in range(1, NDEV):
        peer, pid = _peer(k)
        sends.append(pltpu.make_async_remote_copy(
            src_ref=x_ref if gather else x_ref.at[pid], dst_ref=o_ref.at[me],
            send_sem=send_sems.at[k - 1], recv_sem=recv_sems.at[k - 1],
            device_id=peer, device_id_type=MESH))
        recvs.append(pltpu.make_async_remote_copy(
            src_ref=src_me, dst_ref=o_ref.at[pid],
            send_sem=send_sems.at[k - 1], recv_sem=recv_sems.at[k - 1],
            device_id=peer, device_id_type=MESH))
    return loc, sends, recvs


def _xchg_start(*refs, gather):
    loc, sends, _ = _xchg_copies(*refs, gather)
    loc.start()
    for cp in sends:
        cp.start()


def _xchg_wait(*refs, gather):
    loc, sends, recvs = _xchg_copies(*refs, gather)
    for cp in recvs:
        cp.wait_recv()
    for cp in sends:
        cp.wait_send()
    loc.wait()


_XCHG_SCRATCH = [pltpu.SemaphoreType.DMA((NDEV - 1,)), pltpu.SemaphoreType.DMA((NDEV - 1,)), pltpu.SemaphoreType.DMA]


def _xchg_out_shape(x, gather):
    return jax.ShapeDtypeStruct((NDEV,) + tuple(x.shape if gather else x.shape[1:]), x.dtype)


def _exchange(x, name, gather):
    def body(x_ref, o_ref, send_sems, recv_sems, loc_sem):
        _xchg_start(x_ref, o_ref, send_sems, recv_sems, loc_sem, gather=gather)
        _xchg_wait(x_ref, o_ref, send_sems, recv_sems, loc_sem, gather=gather)

    return pl.pallas_call(
        body, name=name, out_shape=_xchg_out_shape(x, gather),
        in_specs=[_HBM], out_specs=_HBM, scratch_shapes=_XCHG_SCRATCH,
    )(x)


def _gather_two_level(x, name):
    def body(x_ref, o_ref, send_sems, recv_sems, loc_sem):
        mx, my, mc = lax.axis_index("x"), lax.axis_index("y"), lax.axis_index("c")
        me, sibling = (mx, my, mc), (mx, my, 1 - mc)
        chips = [(1 - mx, my), (mx, 1 - my), (1 - mx, 1 - my)]

        def slot(px, py, pc):
            return o_ref.at[4 * px + 2 * py + pc]

        def copy(k, block, to, src=None):
            return pltpu.make_async_remote_copy(
                src_ref=slot(*block) if src is None else src, dst_ref=slot(*block),
                send_sem=send_sems.at[k], recv_sem=recv_sems.at[k], device_id=to, device_id_type=MESH)

        mine = pltpu.make_async_copy(x_ref, slot(*me), loc_sem)
        mine.start()
        first = [copy(0, me, sibling, src=x_ref)]
        first += [copy(1 + j, me, (*chip, mc), src=x_ref) for j, chip in enumerate(chips)]
        for cp in first:
            cp.start()
        passed = [copy(4 + j, (*chip, mc), sibling) for j, chip in enumerate(chips)]
        for j, chip in enumerate(chips):
            copy(1 + j, (*chip, mc), me).wait_recv()
            passed[j].start()
        copy(0, sibling, me).wait_recv()
        for j, chip in enumerate(chips):
            copy(4 + j, (*chip, 1 - mc), me).wait_recv()
        for cp in first + passed:
            cp.wait_send()
        mine.wait()

    return pl.pallas_call(
        body, name=name, out_shape=_xchg_out_shape(x, True),
        in_specs=[_HBM], out_specs=_HBM, scratch_shapes=_XCHG_SCRATCH,
    )(x)


def _hosted_call(body, xch, *, name, grid, in_specs, out_specs, out_shape, scratch_shapes, compiler_params, args,
                 aliases=None):
    aliases = aliases or {}
    if xch is None:
        return pl.pallas_call(body, name=name, grid=grid, in_specs=in_specs, out_specs=out_specs,
                              out_shape=out_shape, scratch_shapes=scratch_shapes, input_output_aliases=aliases,
                              compiler_params=compiler_params)(*args)
    xv, gather = xch
    n_in, n_out, n_scr = len(in_specs), len(out_specs), len(scratch_shapes)

    def wrapped(*refs):
        ins = refs[:n_in]
        x_ref = refs[n_in]
        outs = refs[n_in + 1:n_in + 1 + n_out]
        o_ref = refs[n_in + 1 + n_out]
        scr = refs[n_in + 2 + n_out:]
        own, sems = scr[:n_scr], scr[n_scr:]
        first = functools.reduce(jnp.logical_and, [pl.program_id(a) == 0 for a in range(len(grid))])
        last = functools.reduce(jnp.logical_and, [pl.program_id(a) == grid[a] - 1 for a in range(len(grid))])

        @pl.when(first)
        def _():
            _xchg_start(x_ref, o_ref, *sems, gather=gather)

        body(*ins, *outs, *own)

        @pl.when(last)
        def _():
            _xchg_wait(x_ref, o_ref, *sems, gather=gather)

    res = pl.pallas_call(
        wrapped, name=name, grid=grid, in_specs=list(in_specs) + [_HBM], out_specs=list(out_specs) + [_HBM],
        out_shape=list(out_shape) + [_xchg_out_shape(xv, gather)],
        scratch_shapes=list(scratch_shapes) + _XCHG_SCRATCH, input_output_aliases=aliases,
        compiler_params=compiler_params,
    )(*args, xv)
    return list(res[:n_out]), res[n_out]


def _row_tile(R, cap, mult=8):
    best = mult
    t = mult
    while t <= min(R, cap):
        if R % t == 0:
            best = t
        t += mult
    assert R % best == 0, R
    return best


def _sum_slots(x, name):
    _, R, C = x.shape
    tr = _row_tile(R, 256, 16 if x.dtype == BF16 else 8)

    def body(x_ref, o_ref):
        acc = x_ref[0].astype(F32)
        for i in range(1, NDEV):
            acc = acc + x_ref[i].astype(F32)
        o_ref[...] = acc

    return pl.pallas_call(
        body, name=name, grid=(R // tr,),
        in_specs=[pl.BlockSpec((NDEV, tr, C), lambda i: (0, i, 0))],
        out_specs=pl.BlockSpec((tr, C), lambda i: (i, 0)),
        out_shape=jax.ShapeDtypeStruct((R, C), F32),
        compiler_params=_cp(1),
    )(x)


def _adamw_update(w_ref, g_ref, m_ref, v_ref, d_ref, nm_ref, nv_ref):
    c1 = 1.0 / (1.0 - ADAM_B1 ** ADAM_STEP)
    c2 = 1.0 / (1.0 - ADAM_B2 ** ADAM_STEP)
    gv = g_ref[...]
    nm = ADAM_B1 * m_ref[...] + (1.0 - ADAM_B1) * gv
    nv = ADAM_B2 * v_ref[...] + (1.0 - ADAM_B2) * (gv * gv)
    d_ref[...] = -ADAM_LR * ((nm * c1) / (jnp.sqrt(nv * c2) + ADAM_EPS) + ADAM_WD * w_ref[...])
    nm_ref[...] = nm
    nv_ref[...] = nv


def _adamw_many(ws, gs, ms, vs):
    n = len(ws)

    def body(*refs):
        for i in range(n):
            _adamw_update(refs[i], refs[n + i], refs[2 * n + i], refs[3 * n + i],
                          refs[4 * n + i], refs[5 * n + i], refs[6 * n + i])

    shapes = [jax.ShapeDtypeStruct(w.shape, F32) for w in ws]
    res = pl.pallas_call(
        body, name="adamw_small", out_shape=shapes * 3,
        compiler_params=pltpu.CompilerParams(vmem_limit_bytes=VMEM_LIMIT_BYTES),
    )(*ws, *gs, *ms, *vs)
    return res[:n], res[n:2 * n], res[2 * n:]


def _adamw(w, g, m, v, name):
    R, C = w.shape
    tr = _row_tile(R, 256)

    def body(w_ref, g_ref, m_ref, v_ref, d_ref, nm_ref, nv_ref):
        _adamw_update(w_ref, g_ref, m_ref, v_ref, d_ref, nm_ref, nv_ref)

    blk = pl.BlockSpec((tr, C), lambda i: (i, 0))
    return pl.pallas_call(
        body, name=name, grid=(R // tr,),
        in_specs=[blk] * 4, out_specs=[blk] * 3,
        out_shape=[jax.ShapeDtypeStruct((R, C), F32)] * 3,
        compiler_params=_cp(1),
    )(w, g, m, v)


def _mod_fwd(c_rows, w_shard, b_shard):
    def body(c_ref, w_ref, b_ref, o_ref):
        s = _silu(c_ref[...]).astype(BF16)
        o_ref[...] = _dot(s, w_ref[...].astype(BF16)) + b_ref[...]

    return pl.pallas_call(
        body, name="mod_fwd",
        out_shape=jax.ShapeDtypeStruct((c_rows.shape[0], w_shard.shape[1]), F32),
        compiler_params=pltpu.CompilerParams(vmem_limit_bytes=VMEM_LIMIT_BYTES),
    )(c_rows, w_shard, b_shard)


def _mod_bwd(c_rows, dm_all, dm_shard, w_shard):
    nrow = c_rows.shape[0]

    def body(c_ref, da_ref, ds_ref, w_ref, gw_ref, gb_ref, cc_ref):
        s = _silu(c_ref[...]).astype(BF16)
        ds = ds_ref[...]
        gw_ref[...] = _dot_tn(s, ds.astype(BF16))
        gb_ref[...] = jnp.sum(da_ref[...], axis=0, keepdims=True)
        rowi = lax.broadcasted_iota(jnp.int32, ds.shape, 0)
        dmc = jnp.sum(jnp.where(rowi % 8 >= 4, ds, 0.0), axis=0, keepdims=True)
        dmc8 = jnp.broadcast_to(dmc, (8, ds.shape[1])).astype(BF16)
        cc_ref[...] = _dot_nt(dmc8, w_ref[...].astype(BF16))

    return pl.pallas_call(
        body, name="mod_bwd",
        out_shape=[jax.ShapeDtypeStruct(w_shard.shape, F32),
                   jax.ShapeDtypeStruct((1, dm_all.shape[1]), F32),
                   jax.ShapeDtypeStruct((8, D), F32)],
        compiler_params=pltpu.CompilerParams(vmem_limit_bytes=VMEM_LIMIT_BYTES),
    )(c_rows, dm_all, dm_shard, w_shard)


def _small_finish(cc_pre, c_ctx, dd_cols):
    def body(cc_ref, c_ref, dd_ref, gc_ref, gd_ref):
        gc_ref[...] = cc_ref[...] * _silu_grad(c_ref[...])
        gd_ref[...] = jnp.sum(dd_ref[...], axis=1, keepdims=True)

    return pl.pallas_call(
        body, name="small_finish",
        out_shape=[jax.ShapeDtypeStruct((1, D), F32), jax.ShapeDtypeStruct((NH, 1), F32)],
    )(cc_pre, c_ctx, dd_cols)


_BIG = (("w_in", (D, 10304), 1), ("w_br_ssd", (DI, D), 0), ("w_br_lru", (LW, D), 0), ("w_out", (D, D), 0),
        ("w_mlp1", (D, MLP), 1), ("w_mlp2", (MLP, D), 0))
_SMALL_SH = (("ssd_conv_w", (4, 4096)), ("lru_conv_w", (4, LW)), ("lru_ba", (2, LW)), ("lru_bi", (2, LW)),
             ("lru_lambda", (2, LW)))
_REPL = (("c_ctx", (D,)), ("b_gate", (2 * D,)), ("ssd_conv_b", (4096,)), ("ssd_dt_bias", (2, NH)),
         ("ssd_a_log", (2, NH)), ("ssd_d", (DI,)), ("ssd_norm_w", (DI,)), ("lru_conv_b", (LW,)),
         ("ln1_g", (D,)), ("ln1_b", (D,)),
         ("b_mlp1", (MLP,)), ("b_mlp2", (D,)), ("ln2_g", (D,)), ("ln2_b", (D,)))

_WEIGHT_NAMES = ('c_ctx', 'w_mod', 'b_mod', 'w_in', 'b_gate', 'ssd_conv_w', 'ssd_conv_b', 'ssd_dt_bias', 'ssd_a_log',
                 'ssd_d', 'ssd_norm_w', 'lru_conv_w', 'lru_conv_b', 'lru_wa', 'lru_ba', 'lru_wi', 'lru_bi',
                 'lru_lambda', 'w_br_ssd', 'w_br_lru', 'w_out', 'ln1_g', 'ln1_b', 'w_mlp1', 'b_mlp1', 'w_mlp2',
                 'b_mlp2', 'ln2_g', 'ln2_b')
_ARG_NAMES = ('x', 'c', 'ctx') + _WEIGHT_NAMES + ('loss_target',) + tuple('m_' + n for n in _WEIGHT_NAMES) + tuple(
    'v_' + n for n in _WEIGHT_NAMES)


def _to_slots(full, axis):
    n = full.shape[axis] // NDEV
    if axis == 0:
        return full.reshape(NDEV, n, full.shape[1])
    return full.reshape(full.shape[0], NDEV, n).transpose(1, 0, 2)


def _from_slots(slots, axis):
    if axis == 0:
        return slots.reshape(NDEV * slots.shape[1], slots.shape[2])
    return slots.transpose(1, 0, 2).reshape(slots.shape[1], NDEV * slots.shape[2])


def _pack_rows(arrs, width=1024, mult=8):
    flat = jnp.concatenate([a.reshape(-1) for a in arrs])
    n = flat.shape[0]
    per = width * mult
    tot = -(-n // per) * per
    return jnp.pad(flat, (0, tot - n)).reshape(tot // width, width)


def _unpack_rows(packed, shapes, lead=()):
    nl = len(lead)
    flat = packed.reshape(tuple(lead) + (-1,))
    out, off = [], 0
    for s in shapes:
        n = math.prod(s)
        out.append(flat[..., off:off + n].reshape(tuple(lead) + tuple(s)))
        off += n
    return out


def kernel(x, c, ctx, c_ctx, w_mod, b_mod, w_in, b_gate, ssd_conv_w, ssd_conv_b, ssd_dt_bias, ssd_a_log, ssd_d, ssd_norm_w, lru_conv_w, lru_conv_b, lru_wa, lru_ba, lru_wi, lru_bi, lru_lambda, w_br_ssd, w_br_lru, w_out, ln1_g, ln1_b, w_mlp1, b_mlp1, w_mlp2, b_mlp2, ln2_g, ln2_b, loss_target, m_c_ctx, m_w_mod, m_b_mod, m_w_in, m_b_gate, m_ssd_conv_w, m_ssd_conv_b, m_ssd_dt_bias, m_ssd_a_log, m_ssd_d, m_ssd_norm_w, m_lru_conv_w, m_lru_conv_b, m_lru_wa, m_lru_ba, m_lru_wi, m_lru_bi, m_lru_lambda, m_w_br_ssd, m_w_br_lru, m_w_out, m_ln1_g, m_ln1_b, m_w_mlp1, m_b_mlp1, m_w_mlp2, m_b_mlp2, m_ln2_g, m_ln2_b, v_c_ctx, v_w_mod, v_b_mod, v_w_in, v_b_gate, v_ssd_conv_w, v_ssd_conv_b, v_ssd_dt_bias, v_ssd_a_log, v_ssd_d, v_ssd_norm_w, v_lru_conv_w, v_lru_conv_b, v_lru_wa, v_lru_ba, v_lru_wi, v_lru_bi, v_lru_lambda, v_w_br_ssd, v_w_br_lru, v_w_out, v_ln1_g, v_ln1_b, v_w_mlp1, v_b_mlp1, v_w_mlp2, v_b_mlp2, v_ln2_g, v_ln2_b):
    A = dict(zip(_ARG_NAMES, (x, c, ctx, c_ctx, w_mod, b_mod, w_in, b_gate, ssd_conv_w, ssd_conv_b, ssd_dt_bias, ssd_a_log, ssd_d, ssd_norm_w, lru_conv_w, lru_conv_b, lru_wa, lru_ba, lru_wi, lru_bi, lru_lambda, w_br_ssd, w_br_lru, w_out, ln1_g, ln1_b, w_mlp1, b_mlp1, w_mlp2, b_mlp2, ln2_g, ln2_b, loss_target, m_c_ctx, m_w_mod, m_b_mod, m_w_in, m_b_gate, m_ssd_conv_w, m_ssd_conv_b, m_ssd_dt_bias, m_ssd_a_log, m_ssd_d, m_ssd_norm_w, m_lru_conv_w, m_lru_conv_b, m_lru_wa, m_lru_ba, m_lru_wi, m_lru_bi, m_lru_lambda, m_w_br_ssd, m_w_br_lru, m_w_out, m_ln1_g, m_ln1_b, m_w_mlp1, m_b_mlp1, m_w_mlp2, m_b_mlp2, m_ln2_g, m_ln2_b, v_c_ctx, v_w_mod, v_b_mod, v_w_in, v_b_gate, v_ssd_conv_w, v_ssd_conv_b, v_ssd_dt_bias, v_ssd_a_log, v_ssd_d, v_ssd_norm_w, v_lru_conv_w, v_lru_conv_b, v_lru_wa, v_lru_ba, v_lru_wi, v_lru_bi, v_lru_lambda, v_w_br_ssd, v_w_br_lru, v_w_out, v_ln1_g, v_ln1_b, v_w_mlp1, v_b_mlp1, v_w_mlp2, v_b_mlp2, v_ln2_g, v_ln2_b)))
    Bn, T, _ = x.shape
    Tc = ctx.shape[1]
    cfg = _Cfg(Bn, T, Tc)
    me = _me()
    L = {n: (A[n] if n == "c_ctx" else A[n][0]) for n in _WEIGHT_NAMES}
    nmod = L["w_mod"].shape[1]

    c_all = _exchange(c, "ag_c", True)
    c_rows = jnp.concatenate([c_all.reshape(NDEV * Bn, D), jnp.broadcast_to(c_ctx[None, :], (8, D))], axis=0)
    b_shard = lax.dynamic_slice(L["b_mod"], (me * nmod,), (nmod,)).reshape(1, nmod)
    m_part = _mod_fwd(c_rows, L["w_mod"], b_shard)
    m_all = _exchange(m_part, "ag_mod", True)
    m_full = m_all.transpose(1, 0, 2).reshape(NDEV * Bn + 8, NMOD * D)
    m_mine = lax.dynamic_slice(m_full, (me * Bn, 0), (Bn, NMOD * D))
    mc = m_full[NDEV * Bn, :2 * D]

    w_in_all = _gather_two_level(L["w_in"].astype(BF16), "ag_w_in")
    rest_payload = jnp.concatenate([L[n].astype(BF16).reshape(-1, 1024) for n, _, _ in _BIG[1:]], axis=0)
    small_shapes = [(s[0], s[1] // NDEV) for _, s in _SMALL_SH]
    small_all = _exchange(_pack_rows([L[n] for n, _ in _SMALL_SH], width=512), "ag_w_small", True)
    W = {}
    for (n, shp), piece in zip(_SMALL_SH, _unpack_rows(small_all, small_shapes, lead=(NDEV,))):
        W[n] = piece.transpose(1, 0, 2).reshape(shp)
    W["w_main"], W["w_dt"] = _perm_w_in(_from_slots(w_in_all, 1))
    for n in ("ssd_conv_b", "lru_conv_b", "ssd_norm_w", "b_gate", "ln1_g", "ln1_b", "b_mlp1", "b_mlp2", "ln2_g", "ln2_b"):
        W[n] = L[n].reshape(1, -1)
    for n in ("ssd_dt_bias", "ssd_a_log", "ssd_d", "lru_wa", "lru_wi"):
        W[n] = L[n]

    loss_part, grad_x, g, dm, dmc, xres = _local_step(cfg, x, ctx, loss_target, m_mine, mc, W, rest_payload)
    loss = lax.psum(loss_part, ("x", "y", "c"))

    dmc_pad = jnp.pad(dmc, ((0, 4 - Bn), (0, (NMOD - 2) * D)))
    dm_all = _exchange(jnp.concatenate([jnp.pad(dm, ((0, 4 - Bn), (0, 0))), dmc_pad], axis=0), "ag_dm", True)
    dm_all = dm_all.reshape(NDEV * 8, NMOD * D)
    c_rows_b = jnp.concatenate([jnp.pad(c_all, ((0, 0), (0, 4 - Bn), (0, 0))),
                                jnp.broadcast_to(c_ctx[None, None, :], (NDEV, 4, D))], axis=1).reshape(NDEV * 8, D)
    dm_shard = lax.dynamic_slice(dm_all, (0, me * nmod), (NDEV * 8, nmod))
    g_w_mod, g_b_mod, cc_part = _mod_bwd(c_rows_b, dm_all, dm_shard, L["w_mod"])
    g["c_ctx"] = cc_part[0]

    g["ssd_d"] = g.pop("ssd_d_cols")
    small_names = [n for n, _ in _REPL] + [n for n, _ in _SMALL_SH]
    small_full_shapes = [s for _, s in _REPL] + [s for _, s in _SMALL_SH]
    sm_all = _exchange(_pack_rows([g[n] for n in small_names]), "ag_g_small", True)
    sm_sum = _sum_slots(sm_all, "sum_g_small")
    gs = dict(zip(small_names, _unpack_rows(sm_sum, small_full_shapes)))
    gcc, gdd = _small_finish(gs["c_ctx"].reshape(1, D), c_ctx.reshape(1, D), gs["ssd_d"].reshape(NH, HD))
    gs["c_ctx"] = gcc.reshape(D)
    gs["ssd_d"] = gdd.reshape(NH)
    for n, shp in _SMALL_SH:
        ns = shp[1] // NDEV
        gs[n] = lax.dynamic_slice(gs[n], (0, me * ns), (shp[0], ns))
    gs["b_mod"] = g_b_mod.reshape(NMOD * D)
    lru_sum = _sum_slots(xres["ag_lru"], "sum_g_lru").reshape(2, 2, LB, LBW, LBW)
    gs["lru_wa"], gs["lru_wi"] = lru_sum[0], lru_sum[1]

    gb = {}
    gb["w_in"] = _sum_slots(xres["rs_w_in"], "sum_w_in")
    red_b = _sum_slots(xres["rs_rest"], "sum_w_rest")
    off = 0
    for n, shp, axis in _BIG[1:]:
        shard_shape = (shp[0] // NDEV, shp[1]) if axis == 0 else (shp[0], shp[1] // NDEV)
        r = math.prod(shard_shape) // 1024
        gb[n] = red_b[off:off + r].reshape(shard_shape)
        off += r
    gb["w_mod"] = g_w_mod

    grads, deltas, new_m, new_v = {}, {}, {}, {}
    big_names = ["w_mod"] + [n for n, _, _ in _BIG]
    for n in big_names:
        d_, nm_, nv_ = _adamw(L[n], gb[n], A["m_" + n][0], A["v_" + n][0], "adamw_" + n)
        grads[n], deltas[n], new_m[n], new_v[n] = gb[n], d_, nm_, nv_
    sm_names = [n for n in _WEIGHT_NAMES if n not in big_names]

    def two_d(a):
        return a.reshape(1, -1) if a.ndim == 1 else a
    loc = lambda pre: [two_d(A[pre + n] if n == "c_ctx" else A[pre + n][0]) for n in sm_names]
    gsm = [two_d(gs[n].reshape(L[n].shape)) for n in sm_names]
    ds_, nms_, nvs_ = _adamw_many(loc(""), gsm, loc("m_"), loc("v_"))
    for n, gv, dv, mv, vv in zip(sm_names, gsm, ds_, nms_, nvs_):
        shp = L[n].shape
        grads[n], deltas[n], new_m[n], new_v[n] = gv.reshape(shp), dv.reshape(shp), mv.reshape(shp), vv.reshape(shp)

    def out(dct):
        return [dct[n] if n == "c_ctx" else dct[n][None] for n in _WEIGHT_NAMES]
    return (loss, grad_x, *out(grads), *out(deltas), *out(new_m), *out(new_v))
```

```python
import functools
import math

import jax
import jax.numpy as jnp
from jax import lax
from jax.experimental import pallas as pl
from jax.experimental.pallas import tpu as pltpu

F32 = jnp.float32
BF16 = jnp.bfloat16

D = 1024
GRID_W = 64
DI = 2048
NH = 32
HD = 64
NG = 8
HPG = 4
NS = 128
CH = 128
LW = 1024
LB = 8
LBW = 128
LRU_C = 8.0
MLP = 4096
NMOD = 6
ALPHA = 2.0 ** 0.25
LN_EPS = 1e-6
RMS_EPS = 1e-5
PM = 10240
DTW = 128
CONVW = 5120
NDEV = 8

ADAM_LR = 0.001
ADAM_B1 = 0.9
ADAM_B2 = 0.999
ADAM_EPS = 1e-08
ADAM_WD = 0.01
ADAM_STEP = 10

VMEM_LIMIT_BYTES = 56 * 1024 * 1024


def _cp(n_axes):
    return pltpu.CompilerParams(dimension_semantics=("arbitrary",) * n_axes,
                                vmem_limit_bytes=VMEM_LIMIT_BYTES)


def _sigmoid(x):
    return 0.5 * jnp.tanh(0.5 * x) + 0.5


def _silu(x):
    return x * _sigmoid(x)


def _silu_grad(x):
    s = _sigmoid(x)
    return s * (1.0 + x * (1.0 - s))


def _log1p_pos(e):
    return jnp.where(e < 1e-2, e * (1.0 - e * (0.5 - e * (1.0 / 3.0))), jnp.log(1.0 + e))


def _softplus(x):
    return jnp.maximum(x, 0.0) + _log1p_pos(jnp.exp(-jnp.abs(x)))


_GELU_K = math.sqrt(2.0 / math.pi)


def _gelu(x):
    t = jnp.tanh(_GELU_K * (x + 0.044715 * x * x * x))
    return 0.5 * x * (1.0 + t)


def _gelu_grad(x):
    t = jnp.tanh(_GELU_K * (x + 0.044715 * x * x * x))
    dt = (1.0 - t * t) * _GELU_K * (1.0 + 3.0 * 0.044715 * x * x)
    return 0.5 * (1.0 + t) + 0.5 * x * dt


def _ln(x):
    mu = jnp.mean(x, axis=-1, keepdims=True)
    xc = x - mu
    var = jnp.mean(xc * xc, axis=-1, keepdims=True)
    rs = lax.rsqrt(var + LN_EPS)
    return xc * rs, rs


def _ln_bwd(dy, xhat, rs):
    m1 = jnp.mean(dy, axis=-1, keepdims=True)
    m2 = jnp.mean(dy * xhat, axis=-1, keepdims=True)
    return rs * (dy - m1 - xhat * m2)


def _dot(a, b):
    return lax.dot_general(a, b, (((1,), (0,)), ((), ())), preferred_element_type=F32)


def _dot_nt(a, b):
    return lax.dot_general(a, b, (((1,), (1,)), ((), ())), preferred_element_type=F32)


def _dot_tn(a, b):
    return lax.dot_general(a, b, (((0,), (0,)), ((), ())), preferred_element_type=F32)


def _split3(a):
    a0 = a.astype(BF16)
    r = a - a0.astype(F32)
    a1 = r.astype(BF16)
    a2 = (r - a1.astype(F32)).astype(BF16)
    return a0, a1, a2


def _dot_exact_l(m_bf, a):
    a0, a1, a2 = _split3(a)
    return _dot(m_bf, a0) + _dot(m_bf, a1) + _dot(m_bf, a2)


def _dot_exact_r(a, m_bf):
    a0, a1, a2 = _split3(a)
    return _dot(a0, m_bf) + _dot(a1, m_bf) + _dot(a2, m_bf)


def _dot_hilo_r(a, m_bf):
    a0 = a.astype(BF16)
    a1 = (a - a0.astype(F32)).astype(BF16)
    return _dot(a0, m_bf) + _dot(a1, m_bf)


def _tri(n, upper):
    ii = lax.broadcasted_iota(jnp.int32, (n, n), 0)
    kk = lax.broadcasted_iota(jnp.int32, (n, n), 1)
    m = (kk >= ii) if upper else (kk <= ii)
    return jnp.where(m, 1.0, 0.0).astype(BF16)


def _fit(n, t):
    t = min(t, n)
    while n % t:
        t //= 2
    return t


def _mm(a, b, mode, name, out_dtype=F32, tm=512, tn=512, tk=512, xch=None):
    if mode == "nn":
        M, K = a.shape
        N = b.shape[1]
    elif mode == "nt":
        M, K = a.shape
        N = b.shape[0]
    else:
        K, M = a.shape
        N = b.shape[1]
    tm, tn, tk = _fit(M, tm), _fit(N, tn), _fit(K, tk)
    assert M % tm == 0 and N % tn == 0 and K % tk == 0, (name, M, N, K, tm, tn, tk)
    nk = K // tk
    if mode == "tn":
        a_spec = pl.BlockSpec((tk, tm), lambda i, j, k: (k, i))
    else:
        a_spec = pl.BlockSpec((tm, tk), lambda i, j, k: (i, k))
    if mode == "nt":
        b_spec = pl.BlockSpec((tn, tk), lambda i, j, k: (j, k))
    else:
        b_spec = pl.BlockSpec((tk, tn), lambda i, j, k: (k, j))
    dn = {"nn": (((1,), (0,)), ((), ())), "nt": (((1,), (1,)), ((), ())), "tn": (((0,), (0,)), ((), ()))}[mode]

    def body(a_ref, b_ref, o_ref, acc_ref):
        k = pl.program_id(2)

        @pl.when(k == 0)
        def _():
            acc_ref[...] = jnp.zeros_like(acc_ref)

        acc_ref[...] += lax.dot_general(a_ref[...].astype(BF16), b_ref[...].astype(BF16), dn,
                                        preferred_element_type=F32)

        @pl.when(k == nk - 1)
        def _():
            o_ref[...] = acc_ref[...].astype(o_ref.dtype)

    res = _hosted_call(
        body, xch, name=name, grid=(M // tm, N // tn, nk),
        in_specs=[a_spec, b_spec],
        out_specs=[pl.BlockSpec((tm, tn), lambda i, j, k: (i, j))],
        out_shape=[jax.ShapeDtypeStruct((M, N), out_dtype)],
        scratch_shapes=[pltpu.VMEM((tm, tn), F32)],
        compiler_params=_cp(3), args=(a, b))
    if xch is None:
        return res[0]
    return res[0][0], res[1]


def _mm_mlp1(h2, w1, b1, tm=1024, tn=1024):
    M, K = h2.shape
    N = w1.shape[1]
    tm, tn = _fit(M, tm), _fit(N, tn)

    def body(a_ref, b_ref, bias_ref, a1_ref, act_ref):
        v = _dot(a_ref[...], b_ref[...]) + bias_ref[...]
        a1_ref[...] = v
        r = jnp.maximum(v, 0.0)
        act_ref[...] = (r * r).astype(BF16)

    out = pl.BlockSpec((tm, tn), lambda i, j: (i, j))
    return pl.pallas_call(
        body, name="mm_mlp1", grid=(M // tm, N // tn),
        in_specs=[pl.BlockSpec((tm, K), lambda i, j: (i, 0)), pl.BlockSpec((K, tn), lambda i, j: (0, j)),
                  pl.BlockSpec((1, tn), lambda i, j: (0, j))],
        out_specs=[out, out],
        out_shape=[jax.ShapeDtypeStruct((M, N), F32), jax.ShapeDtypeStruct((M, N), BF16)],
        compiler_params=_cp(2),
    )(h2, w1, b1)


def _mm_dact(dmlp, w2, a1, tm=1024, tn=1024):
    M, K = dmlp.shape
    N = w2.shape[0]
    tm, tn = _fit(M, tm), _fit(N, tn)

    def body(d_ref, w_ref, a1_ref, o_ref, acc_ref):
        i = pl.program_id(1)

        @pl.when(i == 0)
        def _():
            acc_ref[...] = jnp.zeros_like(acc_ref)

        da = _dot_nt(d_ref[...], w_ref[...]) * (2.0 * jnp.maximum(a1_ref[...], 0.0))
        o_ref[...] = da.astype(BF16)
        acc_ref[0:1, :] += jnp.sum(da, axis=0, keepdims=True)

    blk = pl.BlockSpec((tm, tn), lambda j, i: (i, j))
    return pl.pallas_call(
        body, name="mm_dact", grid=(N // tn, M // tm),
        in_specs=[pl.BlockSpec((tm, K), lambda j, i: (i, 0)), pl.BlockSpec((tn, K), lambda j, i: (j, 0)), blk],
        out_specs=[blk, pl.BlockSpec((8, tn), lambda j, i: (0, j))],
        out_shape=[jax.ShapeDtypeStruct((M, N), BF16), jax.ShapeDtypeStruct((8, N), F32)],
        compiler_params=_cp(2),
    )(dmlp, w2, a1)


class _Cfg:
    def __init__(self, Bn, T, Tc):
        assert T % Tc == 0 and Tc % CH == 0 and Tc % GRID_W == 0
        self.Bn, self.T, self.Tc = Bn, T, Tc
        self.TT = T + Tc
        self.TB = Tc
        self.nbt = self.TT // self.TB
        self.nbl = T // self.TB
        self.NT = Bn * self.TT
        self.N = Bn * T
        self.nct = Tc // CH
        self.nlt = T // CH
        self.nch = self.nct + self.nlt


def _ln_mod_fwd(cfg, xs, shift_tab, scale_tab):
    TB, nbt = cfg.TB, cfg.nbt

    def body(x_ref, sh_ref, sc_ref, o_ref):
        xhat, _ = _ln(x_ref[...])
        o_ref[...] = (xhat * (1.0 + sc_ref[...]) + sh_ref[...]).astype(BF16)

    tab = pl.BlockSpec((None, 1, D), lambda b, j: (2 * b + jnp.minimum(j, 1), 0, 0))
    return pl.pallas_call(
        body, name="ln_mod_fwd", grid=(cfg.Bn, nbt),
        in_specs=[pl.BlockSpec((TB, D), lambda b, j: (b * nbt + j, 0)), tab, tab],
        out_specs=pl.BlockSpec((TB, D), lambda b, j: (b * nbt + j, 0)),
        out_shape=jax.ShapeDtypeStruct((cfg.NT, D), BF16),
        compiler_params=_cp(2),
    )(xs, shift_tab, scale_tab)


GP = 8
NGB = NG // GP
HPB = GP * HPG


def _head_select(d, gb, to_front):
    r = lax.broadcasted_iota(jnp.int32, (DTW, DTW), 0)
    c = lax.broadcasted_iota(jnp.int32, (DTW, DTW), 1)
    src, dst = (r, c) if to_front else (c, r)
    hit = jnp.logical_and(src == d * NH + gb * HPB + dst, dst < HPB)
    return jnp.where(hit, 1.0, 0.0).astype(BF16)


def _dt_fwd(cfg, dt_raw, dt_bias, a_log):
    def body(raw_ref, bias_ref, alog_ref, dt_ref, dtg_ref, cumg_ref, cumT_ref):
        dt = _softplus(raw_ref[...] + bias_ref[...])
        a = -jnp.exp(alog_ref[...])
        dta = dt * a
        col = lax.broadcasted_iota(jnp.int32, (CH, DTW), 1)
        cf = _dot_exact_l(_tri(CH, False), dta)
        cr = _dot_exact_l(_tri(CH, True), dta)
        cum = jnp.where(col < NH, cf, cr)
        dt_ref[...] = dt
        for d in range(2):
            for gb in range(NGB):
                sel = _head_select(d, gb, True)
                dtg_ref[d, gb] = _dot_exact_r(dt, sel)
                cg = _dot_exact_r(cum, sel)
                cumg_ref[d, gb] = cg
                cumT_ref[d, gb] = cg.T

    blk = pl.BlockSpec((CH, DTW), lambda i: (i, 0))
    row = pl.BlockSpec((1, DTW), lambda i: (0, 0))
    gblk = pl.BlockSpec((2, NGB, CH, DTW), lambda i: (0, 0, i, 0))
    return pl.pallas_call(
        body, name="dt_fwd", grid=(cfg.NT // CH,),
        in_specs=[blk, row, row],
        out_specs=[blk, gblk, gblk, pl.BlockSpec((2, NGB, None, DTW, CH), lambda i: (0, 0, i, 0, 0))],
        out_shape=[jax.ShapeDtypeStruct((cfg.NT, DTW), F32),
                   jax.ShapeDtypeStruct((2, NGB, cfg.NT, DTW), F32),
                   jax.ShapeDtypeStruct((2, NGB, cfg.NT, DTW), F32),
                   jax.ShapeDtypeStruct((2, NGB, cfg.NT // CH, DTW, CH), F32)],
        compiler_params=_cp(1),
    )(dt_raw, dt_bias, a_log)


def _dt_bwd(cfg, dAs, dxxs, dt_raw, dt, dt_bias, a_log):
    def body(dAf_ref, dAr_ref, dxf_ref, dxr_ref, raw_ref, dt_ref, bias_ref, alog_ref, o_ref, acc_ref):
        i = pl.program_id(0)

        @pl.when(i == 0)
        def _():
            acc_ref[...] = jnp.zeros_like(acc_ref)

        a = -jnp.exp(alog_ref[...])
        col = lax.broadcasted_iota(jnp.int32, (CH, DTW), 1)
        dA_v = jnp.zeros((CH, DTW), F32)
        dxx_v = jnp.zeros((CH, DTW), F32)
        for d, (ra, rx) in enumerate(((dAf_ref, dxf_ref), (dAr_ref, dxr_ref))):
            for gb in range(NGB):
                sel = _head_select(d, gb, False)
                dA_v = dA_v + _dot_exact_r(ra[gb], sel)
                dxx_v = dxx_v + _dot_exact_r(rx[gb], sel)
        ddta = jnp.where(col < NH, _dot_exact_l(_tri(CH, True), dA_v), _dot_exact_l(_tri(CH, False), dA_v))
        dtv = dt_ref[...]
        ddt = ddta * a + dxx_v
        draw = ddt * _sigmoid(raw_ref[...] + bias_ref[...])
        draw = jnp.where(col < 2 * NH, draw, 0.0)
        o_ref[...] = draw
        da = jnp.sum(ddta * dtv, axis=0, keepdims=True) * a
        da = jnp.where(col[:1] < 2 * NH, da, 0.0)
        acc_ref[0:1, :] += da
        acc_ref[1:2, :] += jnp.sum(draw, axis=0, keepdims=True)

    blk = pl.BlockSpec((CH, DTW), lambda i: (i, 0))
    row = pl.BlockSpec((1, DTW), lambda i: (0, 0))
    gblk = pl.BlockSpec((NGB, CH, DTW), lambda i: (0, i, 0))
    return pl.pallas_call(
        body, name="dt_bwd", grid=(cfg.NT // CH,),
        in_specs=[gblk, gblk, gblk, gblk, blk, blk, row, row],
        out_specs=[blk, pl.BlockSpec((8, DTW), lambda i: (0, 0))],
        out_shape=[jax.ShapeDtypeStruct((cfg.NT, DTW), F32), jax.ShapeDtypeStruct((8, DTW), F32)],
        compiler_params=_cp(1),
    )(dAs[0], dAs[1], dxxs[0], dxxs[1], dt_raw, dt, dt_bias, a_log)


_TAPS = (2, 1, 0, -1)


def _conv_fwd(cfg, proj, conv_w, conv_b):
    TB, nbt = cfg.TB, cfg.nbt
    CB = CONVW // 2
    SUB = 256
    n_act = DI + 2 * NG * NS

    def body(u_ref, w_ref, b_ref, o_ref, sg_ref):
        i = pl.program_id(0)
        j = pl.program_id(1)
        R = jnp.where(i % nbt == 0, cfg.Tc, GRID_W)
        t = lax.broadcasted_iota(jnp.int32, (TB, SUB), 0)
        pos = jnp.bitwise_and(t, R - 1)
        keep = {s: jnp.where(jnp.logical_and(pos - s >= 0, pos - s < R), 1.0, 0.0) for s in (2, 1, -1)}
        for q in range(CB // SUB):
            sl = slice(q * SUB, (q + 1) * SUB)
            u = u_ref[:, sl]
            pre = b_ref[:, sl] + w_ref[2:3, sl] * u
            for k in (0, 1, 3):
                pre = pre + w_ref[k:k + 1, sl] * (pltpu.roll(u, _TAPS[k] % TB, 0) * keep[_TAPS[k]])
            col = j * CB + q * SUB + lax.broadcasted_iota(jnp.int32, (1, SUB), 1)
            s = _sigmoid(pre)
            o_ref[:, sl] = jnp.where(col < n_act, pre * s, pre)
            sg_ref[:, sl] = jnp.where(col < n_act, s * (1.0 + pre * (1.0 - s)), 1.0)

    blk = pl.BlockSpec((TB, CB), lambda i, j: (i, j))
    return pl.pallas_call(
        body, name="conv_fwd", grid=(cfg.NT // TB, CONVW // CB),
        in_specs=[blk, pl.BlockSpec((4, CB), lambda i, j: (0, j)), pl.BlockSpec((1, CB), lambda i, j: (0, j))],
        out_specs=[blk, blk],
        out_shape=[jax.ShapeDtypeStruct((cfg.NT, CONVW), F32)] * 2,
        compiler_params=_cp(2),
    )(proj, conv_w, conv_b)


_ANY = pl.BlockSpec(memory_space=pl.ANY)


def _conv_bwd(cfg, name, dproj, proj, conv_w, sgrad, addends, col0, width, skip=None, xch=None):
    TB, nbt, nbl = cfg.TB, cfg.nbt, cfg.nbl
    CB = 1024
    SUB = 256
    c0 = col0 // CB
    n_add = len(addends)

    def body(*refs):
        u_ref, w_ref, sg_ref = refs[1:4]
        add_refs = refs[4:4 + n_add]
        rest = refs[4 + n_add:]
        if skip is not None:
            dy_ref, dv_ref = rest[:2]
            rest = rest[2:]
        o_ref, acc_ref = rest
        i = pl.program_id(1)

        @pl.when(i == 0)
        def _():
            acc_ref[...] = jnp.zeros_like(acc_ref)

        isctx = (i % nbt) == 0
        R = jnp.where(isctx, cfg.Tc, GRID_W)
        t = lax.broadcasted_iota(jnp.int32, (TB, SUB), 0)
        pos = jnp.bitwise_and(t, R - 1)
        keep = {s: jnp.where(jnp.logical_and(pos - s >= 0, pos - s < R), 1.0, 0.0) for s in (2, 1, -1, -2)}

        def shifted(v, s):
            return v if s == 0 else pltpu.roll(v, s % TB, 0) * keep[s]

        for q in range(CB // SUB):
            sl = slice(q * SUB, (q + 1) * SUB)
            u = u_ref[:, sl]
            us = [shifted(u, _TAPS[k]) for k in range(4)]
            g = add_refs[0][:, sl]
            for r in add_refs[1:]:
                g = g + r[:, sl]
            if skip is not None:
                g = g + jnp.where(isctx, 0.0, dv_ref[:, sl] * dy_ref[:, sl])
            g = g * sg_ref[:, sl]
            dp = jnp.zeros_like(g)
            for k in range(4):
                acc_ref[k:k + 1, sl] += jnp.sum(g * us[k], axis=0, keepdims=True)
                dp = dp + w_ref[k:k + 1, sl] * shifted(g, -_TAPS[k])
            acc_ref[4:5, sl] += jnp.sum(g, axis=0, keepdims=True)
            o_ref[:, sl] = dp.astype(BF16)

    blk = pl.BlockSpec((TB, CB), lambda j, i: (i, j))
    in_specs = [_ANY,
                pl.BlockSpec((TB, CB), lambda j, i: (i, c0 + j)),
                pl.BlockSpec((4, CB), lambda j, i: (0, c0 + j)),
                pl.BlockSpec((TB, CB), lambda j, i: (i, c0 + j))] + [blk] * n_add
    args = [dproj, proj, conv_w, sgrad] + list(addends)
    if skip is not None:
        def lat(j, i):
            b = i // nbt
            return (b * nbl + jnp.maximum(i % nbt - 1, 0), j)
        in_specs += [pl.BlockSpec((TB, CB), lat), pl.BlockSpec((1, CB), lambda j, i: (0, j))]
        args += list(skip)
    return _hosted_call(
        body, xch, name=name, grid=(width // CB, cfg.NT // TB),
        in_specs=in_specs,
        out_specs=[pl.BlockSpec((TB, CB), lambda j, i: (i, c0 + j)), pl.BlockSpec((8, CB), lambda j, i: (0, j))],
        out_shape=[jax.ShapeDtypeStruct((cfg.NT, PM), BF16), jax.ShapeDtypeStruct((8, width), F32)],
        scratch_shapes=[], compiler_params=_cp(2), args=args, aliases={0: 0})


def _chunk_of_step(cfg, rev):
    nct, nlt = cfg.nct, cfg.nlt
    if not rev:
        return lambda s: s
    return lambda s: jnp.where(s < nct, nct - 1 - s, 2 * nct + nlt - 1 - s)


def _expand4(v, band, base):
    out = v[:, base + 3:base + 4]
    for h in (2, 1, 0):
        out = jnp.where(band == h, v[:, base + h:base + h + 1], out)
    return out


def _ssd_step_tiles(dt_ref, cum_ref, cumT_ref, rev):
    cum_t = cum_ref[...]
    last = 0 if rev else CH - 1
    llast = cum_t[last:last + 1, :]
    return (dt_ref[...], cum_t, cumT_ref[...], llast, jnp.exp(llast), last)


def _ssd_common(gi, x_ref, b_ref, c_ref, tiles, rev, intra=True):
    dt_t, cum_t, cumT_t, llast, elast, last = tiles
    base = gi * HPG
    xh = x_ref[:, gi * HPG * HD:(gi + 1) * HPG * HD]
    Bm = b_ref[:, gi * NS:(gi + 1) * NS].astype(BF16)
    band = lax.broadcasted_iota(jnp.int32, (CH, HPG * HD), 1) // HD
    cbs = [jnp.broadcast_to(cum_t[:, base + h:base + h + 1], (CH, CH)) for h in range(HPG)]
    Cm = G = decs = None
    if intra:
        Cm = c_ref[:, gi * NS:(gi + 1) * NS].astype(BF16)
        G = _dot_nt(Cm, Bm)
        ii = lax.broadcasted_iota(jnp.int32, (CH, CH), 0)
        jj = lax.broadcasted_iota(jnp.int32, (CH, CH), 1)
        mask = (jj >= ii) if rev else (jj <= ii)
        decs = [jnp.exp(jnp.where(mask, cbs[h] - cumT_t[base + h:base + h + 1, :], -1e30)) for h in range(HPG)]
    cum_exp = jnp.concatenate([cbs[3], cbs[3]], axis=1)
    ll_exp = llast[:, base + 3:base + 4]
    for h in (2, 1, 0):
        cum_exp = jnp.where(band == h, jnp.concatenate([cbs[h], cbs[h]], axis=1), cum_exp)
        ll_exp = jnp.where(band[:1] == h, llast[:, base + h:base + h + 1], ll_exp)
    ecum = jnp.exp(cum_exp) if intra else None
    e_exp = jnp.exp(ll_exp - cum_exp)
    dt_exp = _expand4(dt_t, band, base)
    X = xh * dt_exp
    rb = lax.broadcasted_iota(jnp.int32, (HPG * HD, NS), 0) // HD
    dec_rows = elast[:, base + 3:base + 4]
    for h in (2, 1, 0):
        dec_rows = jnp.where(rb == h, elast[:, base + h:base + h + 1], dec_rows)
    return xh, Bm, Cm, band, e_exp, ecum, dt_exp, X, G, decs, elast, dec_rows, last


def _ssd_specs(cfg, rev):
    nch = cfg.nch
    cmap = _chunk_of_step(cfg, rev)
    d = 1 if rev else 0

    def make(stepmap):
        def row(b, g, sp):
            return b * nch + cmap(stepmap(sp))
        bo, co = DI // (GP * NS), (DI + NG * NS) // (GP * NS)
        return [
            pl.BlockSpec((CH, GP * HPG * HD), lambda b, g, sp: (row(b, g, sp), g)),
            pl.BlockSpec((CH, GP * NS), lambda b, g, sp: (row(b, g, sp), bo + g)),
            pl.BlockSpec((CH, GP * NS), lambda b, g, sp: (row(b, g, sp), co + g)),
            pl.BlockSpec((None, None, CH, DTW), lambda b, g, sp: (d, g, row(b, g, sp), 0)),
            pl.BlockSpec((None, None, CH, DTW), lambda b, g, sp: (d, g, row(b, g, sp), 0)),
            pl.BlockSpec((None, None, None, DTW, CH), lambda b, g, sp: (d, g, row(b, g, sp), 0, 0)),
        ], row
    return make


def _ssd_fwd(cfg, act, dtg, cumg, cumTg, rev):
    nch = cfg.nch
    in_specs, row = _ssd_specs(cfg, rev)(lambda sp: sp)

    def body(x_ref, b_ref, c_ref, dt_ref, cum_ref, cumT_ref, y_ref, hs_ref, h_scr):
        s = pl.program_id(2)

        @pl.when(s == 0)
        def _():
            h_scr[...] = jnp.zeros_like(h_scr)

        def step(intra):
            tiles = _ssd_step_tiles(dt_ref, cum_ref, cumT_ref, rev)
            for gi in range(GP):
                xh, Bm, Cm, band, e_exp, ecum, dt_exp, X, G, decs, elast, dec_rows, last = _ssd_common(
                    gi, x_ref, b_ref, c_ref, tiles, rev, intra)
                H = h_scr[gi]
                if intra:
                    Mcat = jnp.concatenate([(G * decs[h]).astype(BF16) for h in range(HPG)], axis=1)
                    Xbd = jnp.concatenate([jnp.where(band == h, X, 0.0).astype(BF16) for h in range(HPG)], axis=0)
                    y_ref[:, gi * HPG * HD:(gi + 1) * HPG * HD] = (
                        ecum * _dot_nt(Cm, H.astype(BF16)) + _dot(Mcat, Xbd))
                hs_ref[gi] = H
                S = _dot_tn((e_exp * X).astype(BF16), Bm)
                h_scr[gi] = dec_rows * H + S

        isctx = cmap(s) < cfg.nct

        @pl.when(isctx)
        def _():
            step(False)

        @pl.when(jnp.logical_not(isctx))
        def _():
            step(True)

    cmap = _chunk_of_step(cfg, rev)
    return pl.pallas_call(
        body, name="ssd_fwd_rev" if rev else "ssd_fwd", grid=(cfg.Bn, NG // GP, nch),
        in_specs=in_specs,
        out_specs=[pl.BlockSpec((CH, GP * HPG * HD), lambda b, g, s: (row(b, g, s), g)),
                   pl.BlockSpec((None, GP, None, HPG * HD, NS), lambda b, g, s: (b, g, s, 0, 0))],
        out_shape=[jax.ShapeDtypeStruct((cfg.NT, DI), F32),
                   jax.ShapeDtypeStruct((cfg.Bn, NG, nch, HPG * HD, NS), F32)],
        scratch_shapes=[pltpu.VMEM((GP, HPG * HD, NS), F32)],
        compiler_params=_cp(3),
    )(act, act, act, dtg, cumg, cumTg)


def _ssd_bwd(cfg, act, dtg, cumg, cumTg, hs, dy, rev, xch=None):
    nch, nct, nlt = cfg.nch, cfg.nct, cfg.nlt
    cmap = _chunk_of_step(cfg, rev)
    in_specs, row = _ssd_specs(cfg, rev)(lambda sp: nch - 1 - sp)

    def lat_row(b, g, sp):
        c = cmap(nch - 1 - sp)
        return b * nlt + jnp.maximum(c - nct, 0)

    def body(x_ref, b_ref, c_ref, dt_ref, cum_ref, cumT_ref, dy_ref, hs_ref,
             dxh_ref, dB_ref, dC_ref, dA_ref, dxx_ref, dh_scr):
        sp = pl.program_id(2)

        @pl.when(sp == 0)
        def _():
            dh_scr[...] = jnp.zeros_like(dh_scr)

        def step(intra):
            tiles = _ssd_step_tiles(dt_ref, cum_ref, cumT_ref, rev)
            dA_t = jnp.zeros((CH, DTW), F32)
            dAT_t = jnp.zeros((DTW, CH), F32)
            dxx_t = jnp.zeros((CH, DTW), F32)
            for gi in range(GP):
                dA_g, dAT_g, dxx_g = group_bwd(gi, intra, tiles, x_ref, b_ref, c_ref, dy_ref, hs_ref,
                                               dxh_ref, dB_ref, dC_ref, dh_scr)
                dA_t = dA_t + dA_g
                dxx_t = dxx_t + dxx_g
                if intra:
                    dAT_t = dAT_t + dAT_g
            dA_ref[...] = dA_t - dAT_t.T if intra else dA_t
            dxx_ref[...] = dxx_t

        isctx = cmap(nch - 1 - sp) < nct

        @pl.when(isctx)
        def _():
            step(False)

        @pl.when(jnp.logical_not(isctx))
        def _():
            step(True)

    def group_bwd(gi, intra, tiles, x_ref, b_ref, c_ref, dy_ref, hs_ref, dxh_ref, dB_ref, dC_ref, dh_scr):
        xsl = slice(gi * HPG * HD, (gi + 1) * HPG * HD)
        nsl = slice(gi * NS, (gi + 1) * NS)
        base = gi * HPG
        xh, Bm, Cm, band, e_exp, ecum, dt_exp, X, G, decs, elast, dec_rows, last = _ssd_common(
            gi, x_ref, b_ref, c_ref, tiles, rev, intra)
        H = hs_ref[gi]
        dHn = dh_scr[gi]
        dHnb = dHn.astype(BF16)
        BdH = _dot_nt(Bm, dHnb)
        dX = e_exp * BdH
        eX = e_exp * X
        lanei = lax.broadcasted_iota(jnp.int32, (CH, DTW), 1)
        dA = jnp.zeros((CH, DTW), F32)
        dAT = None
        pb = lax.broadcasted_iota(jnp.int32, (HPG * HD, NS), 0) // HD
        pl_ = lax.broadcasted_iota(jnp.int32, (HPG * HD, NS), 1)
        E = jnp.where(pb + base == pl_, 1.0, 0.0).astype(BF16)
        if intra:
            dY = dy_ref[:, xsl]
            Hb = H.astype(BF16)
            dYs = ecum * dY
            dYsb = dYs.astype(BF16)
            Ys = ecum * _dot_nt(Cm, Hb)
            dG = jnp.zeros((CH, CH), F32)
            subi = lax.broadcasted_iota(jnp.int32, (DTW, CH), 0)
            dAT = jnp.zeros((DTW, CH), F32)
            Xbd = jnp.concatenate([jnp.where(band == h, X, 0.0).astype(BF16) for h in range(HPG)], axis=0)
            dYbd = jnp.concatenate([jnp.where(band == h, dY, 0.0).astype(BF16) for h in range(HPG)], axis=0)
            dMcat = _dot_nt(dY.astype(BF16), Xbd)
            Ms = []
            for h in range(HPG):
                M = G * decs[h]
                dM = dMcat[:, h * CH:(h + 1) * CH]
                W = dM * M
                dG = dG + dM * decs[h]
                Ms.append(M.astype(BF16))
                dA = dA + jnp.where(lanei == base + h, jnp.sum(W, axis=1, keepdims=True), 0.0)
                dAT = dAT + jnp.where(subi == base + h, jnp.sum(W, axis=0, keepdims=True), 0.0)
            dX = dX + _dot_tn(jnp.concatenate(Ms, axis=0), dYbd)
            dGb = dG.astype(BF16)
            dC_ref[:, nsl] = _dot(dGb, Bm) + _dot(dYsb, Hb)
            dB_ref[:, nsl] = _dot_tn(dGb, Cm) + _dot(eX.astype(BF16), dHnb)
            dh_scr[gi] = dec_rows * dHn + _dot_tn(dYsb, Cm)
            dA = dA + _dot_hilo_r(dY * Ys, E)
        else:
            dC_ref[:, nsl] = jnp.zeros((CH, NS), F32)
            dB_ref[:, nsl] = _dot(eX.astype(BF16), dHnb)
            dh_scr[gi] = dec_rows * dHn
        q = _dot_hilo_r(eX * BdH, E)
        r = jnp.sum(dHn * H, axis=1, keepdims=True)
        lane1 = lax.broadcasted_iota(jnp.int32, (1, DTW), 1)
        hdot = jnp.zeros((1, DTW), F32)
        for h in range(HPG):
            hv = jnp.sum(r[h * HD:(h + 1) * HD, :], axis=0, keepdims=True)
            hdot = hdot + jnp.where(lane1 == base + h, hv, 0.0)
        dllast = jnp.sum(q, axis=0, keepdims=True) + elast * hdot
        rowi = lax.broadcasted_iota(jnp.int32, (CH, DTW), 0)
        dxh_ref[:, xsl] = dX * dt_exp
        return dA - q + jnp.where(rowi == last, dllast, 0.0), dAT, _dot_hilo_r(dX * xh, E)

    small = pl.BlockSpec((None, CH, DTW), lambda b, g, sp: (g, row(b, g, sp), 0))
    return _hosted_call(
        body, xch, name="ssd_bwd_rev" if rev else "ssd_bwd", grid=(cfg.Bn, NG // GP, nch),
        in_specs=in_specs + [
            pl.BlockSpec((CH, GP * HPG * HD), lambda b, g, sp: (lat_row(b, g, sp), g)),
            pl.BlockSpec((None, GP, None, HPG * HD, NS), lambda b, g, sp: (b, g, nch - 1 - sp, 0, 0))],
        out_specs=[pl.BlockSpec((CH, GP * HPG * HD), lambda b, g, sp: (row(b, g, sp), g)),
                   pl.BlockSpec((CH, GP * NS), lambda b, g, sp: (row(b, g, sp), g)),
                   pl.BlockSpec((CH, GP * NS), lambda b, g, sp: (row(b, g, sp), g)),
                   small, small],
        out_shape=[jax.ShapeDtypeStruct((cfg.NT, DI), F32),
                   jax.ShapeDtypeStruct((cfg.NT, NG * NS), F32),
                   jax.ShapeDtypeStruct((cfg.NT, NG * NS), F32),
                   jax.ShapeDtypeStruct((NGB, cfg.NT, DTW), F32),
                   jax.ShapeDtypeStruct((NGB, cfg.NT, DTW), F32)],
        scratch_shapes=[pltpu.VMEM((GP, HPG * HD, NS), F32)],
        compiler_params=_cp(3), args=(act, act, act, dtg, cumg, cumTg, dy, hs))


def _shift_rows(v, s, fill, toward_later, rowi):
    n = v.shape[0]
    if toward_later:
        return jnp.where(rowi >= s, pltpu.roll(v, s, 0), fill)
    return jnp.where(rowi < n - s, pltpu.roll(v, n - s, 0), fill)


def _chunk_scan(a, b, carry, later):
    nt = a.shape[0] // 8
    rowi = lax.broadcasted_iota(jnp.int32, (8, a.shape[1]), 0)
    outs = [None] * nt
    for r in (range(nt) if later else range(nt - 1, -1, -1)):
        av = a[r * 8:(r + 1) * 8]
        bv = b[r * 8:(r + 1) * 8]
        for sh in (1, 2, 4):
            a_p = _shift_rows(av, sh, 1.0, later, rowi)
            b_p = _shift_rows(bv, sh, 0.0, later, rowi)
            bv = av * b_p + bv
            av = av * a_p
        h = bv + av * carry
        outs[r] = h
        carry = h[7:8] if later else h[0:1]
    return jnp.concatenate(outs, axis=0), carry


def _lru_gates(u, wa_ref, wi_ref, ba_ref, bi_ref, lam_ref):
    rs, is_ = [], []
    for k in range(LB):
        uk = u[:, k * LBW:(k + 1) * LBW].astype(BF16)
        rs.append(_dot(uk, wa_ref[k].astype(BF16)))
        is_.append(_dot(uk, wi_ref[k].astype(BF16)))
    r = _sigmoid(jnp.concatenate(rs, axis=1) + ba_ref[...])
    ig = _sigmoid(jnp.concatenate(is_, axis=1) + bi_ref[...])
    sp = _softplus(-lam_ref[...])
    la = -LRU_C * r * sp
    a = jnp.exp(la)
    g = jnp.sqrt((1.0 + a * a) * jnp.tanh(-la))
    return r, ig, sp, la, a, g


def _lru_w_specs(d):
    return [pl.BlockSpec((None, LB, LBW, LBW), lambda b, s: (d, 0, 0, 0)),
            pl.BlockSpec((None, LB, LBW, LBW), lambda b, s: (d, 0, 0, 0)),
            pl.BlockSpec((None, 1, LW), lambda b, s: (d, 0, 0)),
            pl.BlockSpec((None, 1, LW), lambda b, s: (d, 0, 0)),
            pl.BlockSpec((None, 1, LW), lambda b, s: (d, 0, 0))]


def _lru_fwd(cfg, act, wa, wi, ba, bi, lam, rev):
    nch = cfg.nch
    cmap = _chunk_of_step(cfg, rev)
    d = 1 if rev else 0
    ucol = (DI + 2 * NG * NS) // LW

    def body(u_ref, wa_ref, wi_ref, ba_ref, bi_ref, lam_ref, h_ref, c_scr):
        s = pl.program_id(1)

        @pl.when(s == 0)
        def _():
            c_scr[...] = jnp.zeros_like(c_scr)

        u = u_ref[...]
        r, ig, sp, la, a, g = _lru_gates(u, wa_ref, wi_ref, ba_ref, bi_ref, lam_ref)
        h, carry = _chunk_scan(a, g * ig * u, c_scr[0:1, :], not rev)
        h_ref[...] = h
        c_scr[0:1, :] = carry

    return pl.pallas_call(
        body, name="lru_fwd_rev" if rev else "lru_fwd", grid=(cfg.Bn, nch),
        in_specs=[pl.BlockSpec((CH, LW), lambda b, s: (b * nch + cmap(s), ucol))] + _lru_w_specs(d),
        out_specs=pl.BlockSpec((CH, LW), lambda b, s: (b * nch + cmap(s), 0)),
        out_shape=jax.ShapeDtypeStruct((cfg.NT, LW), F32),
        scratch_shapes=[pltpu.VMEM((8, LW), F32)],
        compiler_params=_cp(2),
    )(act, wa, wi, ba, bi, lam)


def _lru_bwd(cfg, act, wa, wi, ba, bi, lam, hd, dyl, rev):
    nch, nct, nlt = cfg.nch, cfg.nct, cfg.nlt
    cmap = _chunk_of_step(cfg, rev)
    d = 1 if rev else 0
    ucol = (DI + 2 * NG * NS) // LW

    def srow(b, sp):
        return b * nch + cmap(nch - 1 - sp)

    def prev_rows(b, sp):
        s = nch - 1 - sp
        cp = cmap(jnp.maximum(s - 1, 0))
        base = (b * nch + cp) * (CH // 8)
        return base + (0 if rev else CH // 8 - 1)

    def lat_row(b, sp):
        c = cmap(nch - 1 - sp)
        return b * nlt + jnp.maximum(c - nct, 0)

    def body(u_ref, wa_ref, wi_ref, ba_ref, bi_ref, lam_ref, h_ref, hp_ref, dy_ref,
             du_ref, dwa_ref, dwi_ref, vec_ref, c_scr):
        b = pl.program_id(0)
        sp_id = pl.program_id(1)
        s = nch - 1 - sp_id

        @pl.when(sp_id == 0)
        def _():
            c_scr[...] = jnp.zeros_like(c_scr)

        @pl.when(jnp.logical_and(b == 0, sp_id == 0))
        def _():
            dwa_ref[...] = jnp.zeros_like(dwa_ref)
            dwi_ref[...] = jnp.zeros_like(dwi_ref)
            vec_ref[...] = jnp.zeros_like(vec_ref)

        c = cmap(s)
        u = u_ref[...]
        r, ig, spl, la, a, g = _lru_gates(u, wa_ref, wi_ref, ba_ref, bi_ref, lam_ref)
        dh = jnp.where(c < nct, 0.0, dy_ref[...])
        rowi = lax.broadcasted_iota(jnp.int32, (CH, LW), 0)
        lamv, _ = _chunk_scan(_shift_rows(a, 1, 1.0, rev, rowi), dh, c_scr[0:1, :], rev)
        first = CH - 1 if rev else 0
        c_scr[0:1, :] = (a * lamv)[first:first + 1, :]
        hprow = hp_ref[...][(0 if rev else 7):(1 if rev else 8), :]
        hprow = jnp.where(s > 0, hprow, 0.0)
        h_prev = _shift_rows(h_ref[...], 1, hprow, not rev, rowi)
        da = lamv * h_prev
        db = lamv
        iu = ig * u
        dla = da * a - db * iu * (a * a) / g
        dr = dla * (-LRU_C * spl)
        di = db * g * u
        du = db * g * ig
        drp = dr * r * (1.0 - r)
        dip = di * ig * (1.0 - ig)
        dus = []
        for k in range(LB):
            sl = slice(k * LBW, (k + 1) * LBW)
            drk = drp[:, sl].astype(BF16)
            dik = dip[:, sl].astype(BF16)
            uk = u[:, sl].astype(BF16)
            dus.append(_dot_nt(drk, wa_ref[k].astype(BF16)) + _dot_nt(dik, wi_ref[k].astype(BF16)))
            dwa_ref[k] += _dot_tn(uk, drk)
            dwi_ref[k] += _dot_tn(uk, dik)
        du_ref[...] = du + jnp.concatenate(dus, axis=1)
        vec_ref[0:1, :] += jnp.sum(drp, axis=0, keepdims=True)
        vec_ref[1:2, :] += jnp.sum(dip, axis=0, keepdims=True)
        dsp = jnp.sum(dla * (-LRU_C * r), axis=0, keepdims=True)
        vec_ref[2:3, :] += dsp * (-_sigmoid(-lam_ref[...]))

    return pl.pallas_call(
        body, name="lru_bwd_rev" if rev else "lru_bwd", grid=(cfg.Bn, nch),
        in_specs=[pl.BlockSpec((CH, LW), lambda b, sp: (srow(b, sp), ucol))] + _lru_w_specs(d) + [
            pl.BlockSpec((CH, LW), lambda b, sp: (srow(b, sp), 0)),
            pl.BlockSpec((8, LW), lambda b, sp: (prev_rows(b, sp), 0)),
            pl.BlockSpec((CH, LW), lambda b, sp: (lat_row(b, sp), 0))],
        out_specs=[pl.BlockSpec((CH, LW), lambda b, sp: (srow(b, sp), 0)),
                   pl.BlockSpec((LB, LBW, LBW), lambda b, sp: (0, 0, 0)),
                   pl.BlockSpec((LB, LBW, LBW), lambda b, sp: (0, 0, 0)),
                   pl.BlockSpec((8, LW), lambda b, sp: (0, 0))],
        out_shape=[jax.ShapeDtypeStruct((cfg.NT, LW), F32),
                   jax.ShapeDtypeStruct((LB, LBW, LBW), F32),
                   jax.ShapeDtypeStruct((LB, LBW, LBW), F32),
                   jax.ShapeDtypeStruct((8, LW), F32)],
        scratch_shapes=[pltpu.VMEM((8, LW), F32)],
        compiler_params=_cp(2),
    )(act, wa, wi, ba, bi, lam, hd, hd, dyl)


HB = 1024


def _post_ssd_fwd(cfg, yf, yb, act, proj, dvec, norm_w):
    TB, nbt, nbl = cfg.TB, cfg.nbt, cfg.nbl
    zc = CONVW // HB

    def body(yf_ref, yb_ref, xh_ref, z_ref, dv_ref, w_ref, o_ref):
        y = yf_ref[...] + yb_ref[...] + dv_ref[...] * xh_ref[...]
        u = y * _silu(z_ref[...])
        for gi in range(HB // (DI // NG)):
            sl = slice(gi * 256, (gi + 1) * 256)
            ug = u[:, sl]
            rs = lax.rsqrt(jnp.mean(ug * ug, axis=1, keepdims=True) + RMS_EPS)
            o_ref[:, sl] = (ug * rs * w_ref[:, sl]).astype(BF16)

    def st(b, j, cb):
        return (b * nbt + 1 + j, cb)
    return pl.pallas_call(
        body, name="post_ssd_fwd", grid=(cfg.Bn, nbl, DI // HB),
        in_specs=[pl.BlockSpec((TB, HB), st), pl.BlockSpec((TB, HB), st), pl.BlockSpec((TB, HB), st),
                  pl.BlockSpec((TB, HB), lambda b, j, cb: (b * nbt + 1 + j, zc + cb)),
                  pl.BlockSpec((1, HB), lambda b, j, cb: (0, cb)),
                  pl.BlockSpec((1, HB), lambda b, j, cb: (0, cb))],
        out_specs=pl.BlockSpec((TB, HB), lambda b, j, cb: (b * nbl + j, cb)),
        out_shape=jax.ShapeDtypeStruct((cfg.N, DI), BF16),
        compiler_params=_cp(3),
    )(yf, yb, act, proj, dvec, norm_w)


def _post_ssd_bwd(cfg, dproj, dn, yf, yb, act, proj, dvec, norm_w):
    TB, nbt, nbl = cfg.TB, cfg.nbt, cfg.nbl
    zc = CONVW // HB

    def body(_, dn_ref, yf_ref, yb_ref, xh_ref, z_ref, dv_ref, w_ref, dy_ref, dz_ref, acc_ref):
        b = pl.program_id(1)
        j = pl.program_id(2)

        @pl.when(jnp.logical_and(b == 0, j == 0))
        def _():
            acc_ref[...] = jnp.zeros_like(acc_ref)

        @pl.when(j == 0)
        def _():
            dz_ref[...] = jnp.zeros_like(dz_ref)

        @pl.when(j > 0)
        def _():
            latent(dn_ref, yf_ref, yb_ref, xh_ref, z_ref, dv_ref, w_ref, dy_ref, dz_ref, acc_ref)

    def latent(dn_ref, yf_ref, yb_ref, xh_ref, z_ref, dv_ref, w_ref, dy_ref, dz_ref, acc_ref):
        xh = xh_ref[...]
        z = z_ref[...]
        y = yf_ref[...] + yb_ref[...] + dv_ref[...] * xh
        sz = _silu(z)
        u = y * sz
        dout = dn_ref[...]
        for gi in range(HB // (DI // NG)):
            sl = slice(gi * 256, (gi + 1) * 256)
            ug0 = u[:, sl]
            rs = lax.rsqrt(jnp.mean(ug0 * ug0, axis=1, keepdims=True) + RMS_EPS)
            ug = ug0 * rs
            do = dout[:, sl]
            acc_ref[0:1, sl] += jnp.sum(do * ug, axis=0, keepdims=True)
            dug = do * w_ref[:, sl]
            du = rs * (dug - ug * jnp.mean(dug * ug, axis=1, keepdims=True))
            dy = du * sz[:, sl]
            dy_ref[:, sl] = dy
            dz_ref[:, sl] = (du * y[:, sl] * _silu_grad(z[:, sl])).astype(BF16)
            acc_ref[1:2, sl] += jnp.sum(dy * xh[:, sl], axis=0, keepdims=True)

    def st(cb, b, j):
        return (b * nbt + j, cb)

    def la(cb, b, j):
        return (b * nbl + jnp.maximum(j - 1, 0), cb)
    return pl.pallas_call(
        body, name="post_ssd_bwd", grid=(DI // HB, cfg.Bn, nbt),
        in_specs=[_ANY, pl.BlockSpec((TB, HB), la), pl.BlockSpec((TB, HB), st), pl.BlockSpec((TB, HB), st),
                  pl.BlockSpec((TB, HB), st),
                  pl.BlockSpec((TB, HB), lambda cb, b, j: (b * nbt + j, zc + cb)),
                  pl.BlockSpec((1, HB), lambda cb, b, j: (0, cb)),
                  pl.BlockSpec((1, HB), lambda cb, b, j: (0, cb))],
        out_specs=[pl.BlockSpec((TB, HB), la),
                   pl.BlockSpec((TB, HB), lambda cb, b, j: (b * nbt + j, zc + cb)),
                   pl.BlockSpec((8, HB), lambda cb, b, j: (0, cb))],
        out_shape=[jax.ShapeDtypeStruct((cfg.N, DI), F32), jax.ShapeDtypeStruct((cfg.NT, PM), BF16),
                   jax.ShapeDtypeStruct((8, DI), F32)],
        input_output_aliases={0: 1},
        compiler_params=_cp(3),
    )(dproj, dn, yf, yb, act, proj, dvec, norm_w)


def _post_lru_fwd(cfg, hf, hb, proj):
    TB, nbt, nbl = cfg.TB, cfg.nbt, cfg.nbl
    gc = (CONVW + DI) // HB

    def body(hf_ref, hb_ref, g_ref, o_ref):
        o_ref[...] = ((hf_ref[...] + hb_ref[...]) * _gelu(g_ref[...])).astype(BF16)

    st = pl.BlockSpec((TB, HB), lambda b, j: (b * nbt + 1 + j, 0))
    return pl.pallas_call(
        body, name="post_lru_fwd", grid=(cfg.Bn, nbl),
        in_specs=[st, st, pl.BlockSpec((TB, HB), lambda b, j: (b * nbt + 1 + j, gc))],
        out_specs=pl.BlockSpec((TB, HB), lambda b, j: (b * nbl + j, 0)),
        out_shape=jax.ShapeDtypeStruct((cfg.N, LW), BF16),
        compiler_params=_cp(2),
    )(hf, hb, proj)


def _post_lru_bwd(cfg, dproj, dv, hf, hb, proj):
    TB, nbt, nbl = cfg.TB, cfg.nbt, cfg.nbl
    gc = (CONVW + DI) // HB

    def body(_, dv_ref, hf_ref, hb_ref, g_ref, dy_ref, dg_ref):
        j = pl.program_id(1)

        @pl.when(j == 0)
        def _():
            dg_ref[...] = jnp.zeros_like(dg_ref)

        @pl.when(j > 0)
        def _():
            gt = g_ref[...]
            dvv = dv_ref[...]
            dy_ref[...] = dvv * _gelu(gt)
            dg_ref[...] = (dvv * (hf_ref[...] + hb_ref[...]) * _gelu_grad(gt)).astype(BF16)

    st = pl.BlockSpec((TB, HB), lambda b, j: (b * nbt + j, 0))
    la = pl.BlockSpec((TB, HB), lambda b, j: (b * nbl + jnp.maximum(j - 1, 0), 0))
    gcol = pl.BlockSpec((TB, HB), lambda b, j: (b * nbt + j, gc))
    return pl.pallas_call(
        body, name="post_lru_bwd", grid=(cfg.Bn, nbt),
        in_specs=[_ANY, la, st, st, gcol],
        out_specs=[la, gcol],
        out_shape=[jax.ShapeDtypeStruct((cfg.N, LW), F32), jax.ShapeDtypeStruct((cfg.NT, PM), BF16)],
        input_output_aliases={0: 1},
        compiler_params=_cp(2),
    )(dproj, dv, hf, hb, proj)


def _merge_fwd(cfg, proj, b_gate, br_ssd, br_lru):
    TB, nbt, nbl = cfg.TB, cfg.nbt, cfg.nbl
    mc = (CONVW + DI + LW) // HB

    def body(ms_ref, ml_ref, bg_ref, bs_ref, bl_ref, o_ref):
        gs = _sigmoid(ms_ref[...] + bg_ref[:, :D])
        gl = _sigmoid(ml_ref[...] + bg_ref[:, D:])
        o_ref[...] = (gs * bs_ref[...] + gl * bl_ref[...]).astype(BF16)

    la = pl.BlockSpec((TB, D), lambda b, j: (b * nbl + j, 0))
    return pl.pallas_call(
        body, name="merge_fwd", grid=(cfg.Bn, nbl),
        in_specs=[pl.BlockSpec((TB, HB), lambda b, j: (b * nbt + 1 + j, mc)),
                  pl.BlockSpec((TB, HB), lambda b, j: (b * nbt + 1 + j, mc + 1)),
                  pl.BlockSpec((1, 2 * D), lambda b, j: (0, 0)), la, la],
        out_specs=la,
        out_shape=jax.ShapeDtypeStruct((cfg.N, D), BF16),
        compiler_params=_cp(2),
    )(proj, proj, b_gate, br_ssd, br_lru)


def _merge_bwd(cfg, dmix, proj, b_gate, br_ssd, br_lru):
    TB, nbt, nbl = cfg.TB, cfg.nbt, cfg.nbl
    mc = (CONVW + DI + LW) // HB

    def body(dm_ref, ms_ref, ml_ref, bg_ref, bs_ref, bl_ref, ds_ref, dl_ref, dmg_ref, acc_ref):
        b = pl.program_id(0)
        j = pl.program_id(1)

        @pl.when(jnp.logical_and(b == 0, j == 0))
        def _():
            acc_ref[...] = jnp.zeros_like(acc_ref)

        @pl.when(j == 0)
        def _():
            dmg_ref[...] = jnp.zeros_like(dmg_ref)

        @pl.when(j > 0)
        def _():
            latent(dm_ref, ms_ref, ml_ref, bg_ref, bs_ref, bl_ref, ds_ref, dl_ref, dmg_ref, acc_ref)

    def latent(dm_ref, ms_ref, ml_ref, bg_ref, bs_ref, bl_ref, ds_ref, dl_ref, dmg_ref, acc_ref):
        dm = dm_ref[...]
        gs = _sigmoid(ms_ref[...] + bg_ref[:, :D])
        gl = _sigmoid(ml_ref[...] + bg_ref[:, D:])
        ds_ref[...] = (dm * gs).astype(BF16)
        dl_ref[...] = (dm * gl).astype(BF16)
        dps = dm * bs_ref[...] * gs * (1.0 - gs)
        dpl = dm * bl_ref[...] * gl * (1.0 - gl)
        dmg_ref[:, :D] = dps.astype(BF16)
        dmg_ref[:, D:] = dpl.astype(BF16)
        acc_ref[0:1, :D] += jnp.sum(dps, axis=0, keepdims=True)
        acc_ref[0:1, D:] += jnp.sum(dpl, axis=0, keepdims=True)

    la = pl.BlockSpec((TB, D), lambda b, j: (b * nbl + jnp.maximum(j - 1, 0), 0))
    return pl.pallas_call(
        body, name="merge_bwd", grid=(cfg.Bn, nbt),
        in_specs=[la, pl.BlockSpec((TB, HB), lambda b, j: (b * nbt + j, mc)),
                  pl.BlockSpec((TB, HB), lambda b, j: (b * nbt + j, mc + 1)),
                  pl.BlockSpec((1, 2 * D), lambda b, j: (0, 0)), la, la],
        out_specs=[la, la, pl.BlockSpec((TB, 2 * D), lambda b, j: (b * nbt + j, mc // 2)),
                   pl.BlockSpec((8, 2 * D), lambda b, j: (0, 0))],
        out_shape=[jax.ShapeDtypeStruct((cfg.N, D), BF16), jax.ShapeDtypeStruct((cfg.N, D), BF16),
                   jax.ShapeDtypeStruct((cfg.NT, PM), BF16), jax.ShapeDtypeStruct((8, 2 * D), F32)],
        compiler_params=_cp(2),
    )(dmix, proj, proj, b_gate, br_ssd, br_lru)


def _resid1_fwd(cfg, xs, x_mix, gate1, shift2, scale2, ln1_g, ln1_b):
    TB, nbt, nbl = cfg.TB, cfg.nbt, cfg.nbl

    def body(x_ref, xm_ref, g1_ref, sh_ref, sc_ref, lg_ref, lb_ref, x1_ref, h2_ref):
        r1 = ALPHA * x_ref[...] + g1_ref[...] * xm_ref[...]
        xh, _ = _ln(r1)
        x1 = xh * lg_ref[...] + lb_ref[...]
        x1_ref[...] = x1
        xh2, _ = _ln(x1)
        h2_ref[...] = (xh2 * (1.0 + sc_ref[...]) + sh_ref[...]).astype(BF16)

    la = pl.BlockSpec((TB, D), lambda b, j: (b * nbl + j, 0))
    ex = pl.BlockSpec((None, 1, D), lambda b, j: (b, 0, 0))
    vec = pl.BlockSpec((1, D), lambda b, j: (0, 0))
    return pl.pallas_call(
        body, name="resid1_fwd", grid=(cfg.Bn, nbl),
        in_specs=[pl.BlockSpec((TB, D), lambda b, j: (b * nbt + 1 + j, 0)), la, ex, ex, ex, vec, vec],
        out_specs=[la, la],
        out_shape=[jax.ShapeDtypeStruct((cfg.N, D), F32), jax.ShapeDtypeStruct((cfg.N, D), BF16)],
        compiler_params=_cp(2),
    )(xs, x_mix, gate1, shift2, scale2, ln1_g, ln1_b)


def _resid1_bwd(cfg, dh2, dx1p, x1, xs, x_mix, gate1, scale2, ln1_g):
    TB, nbt, nbl = cfg.TB, cfg.nbt, cfg.nbl

    def body(dh2_ref, dx1p_ref, x1_ref, x_ref, xm_ref, g1_ref, sc_ref, lg_ref,
             dxm_ref, dxp_ref, ex_ref, gl_ref):
        b = pl.program_id(0)
        j = pl.program_id(1)

        @pl.when(j == 0)
        def _():
            ex_ref[...] = jnp.zeros_like(ex_ref)

        @pl.when(jnp.logical_and(b == 0, j == 0))
        def _():
            gl_ref[...] = jnp.zeros_like(gl_ref)

        dh2 = dh2_ref[...]
        xh2, rs2 = _ln(x1_ref[...])
        ex_ref[0:1, :] += jnp.sum(dh2, axis=0, keepdims=True)
        ex_ref[1:2, :] += jnp.sum(dh2 * xh2, axis=0, keepdims=True)
        dx1 = dx1p_ref[...] + _ln_bwd(dh2 * (1.0 + sc_ref[...]), xh2, rs2)
        xm = xm_ref[...]
        g1 = g1_ref[...]
        r1 = ALPHA * x_ref[...] + g1 * xm
        xh1, rs1 = _ln(r1)
        gl_ref[0:1, :] += jnp.sum(dx1 * xh1, axis=0, keepdims=True)
        gl_ref[1:2, :] += jnp.sum(dx1, axis=0, keepdims=True)
        dr1 = _ln_bwd(dx1 * lg_ref[...], xh1, rs1)
        ex_ref[2:3, :] += jnp.sum(dr1 * xm, axis=0, keepdims=True)
        dxm_ref[...] = (dr1 * g1).astype(BF16)
        dxp_ref[...] = ALPHA * dr1

    la = pl.BlockSpec((TB, D), lambda b, j: (b * nbl + j, 0))
    ex = pl.BlockSpec((None, 1, D), lambda b, j: (b, 0, 0))
    vec = pl.BlockSpec((1, D), lambda b, j: (0, 0))
    return pl.pallas_call(
        body, name="resid1_bwd", grid=(cfg.Bn, nbl),
        in_specs=[la, la, la, pl.BlockSpec((TB, D), lambda b, j: (b * nbt + 1 + j, 0)), la, ex, ex, vec],
        out_specs=[la, la, pl.BlockSpec((None, 8, D), lambda b, j: (b, 0, 0)),
                   pl.BlockSpec((8, D), lambda b, j: (0, 0))],
        out_shape=[jax.ShapeDtypeStruct((cfg.N, D), BF16), jax.ShapeDtypeStruct((cfg.N, D), F32),
                   jax.ShapeDtypeStruct((cfg.Bn, 8, D), F32), jax.ShapeDtypeStruct((8, D), F32)],
        compiler_params=_cp(2),
    )(dh2, dx1p, x1, xs, x_mix, gate1, scale2, ln1_g)


def _final_fwd_bwd(cfg, x1, mlp, b2, gate2, ln2_g, ln2_b, target):
    TB, nbl = cfg.TB, cfg.nbl

    def body(x1_ref, m_ref, b2_ref, g2_ref, lg_ref, lb_ref, t_ref, dm_ref, dx_ref, ex_ref, gl_ref):
        b = pl.program_id(0)
        j = pl.program_id(1)

        @pl.when(j == 0)
        def _():
            ex_ref[...] = jnp.zeros_like(ex_ref)

        @pl.when(jnp.logical_and(b == 0, j == 0))
        def _():
            gl_ref[...] = jnp.zeros_like(gl_ref)

        mv = m_ref[...] + b2_ref[...]
        g2 = g2_ref[...]
        r2 = ALPHA * x1_ref[...] + g2 * mv
        xh, rs = _ln(r2)
        lg = lg_ref[...]
        x2 = xh * lg + lb_ref[...]
        err = x2 - t_ref[...]
        ls = jnp.sum(jnp.sum(err * err, axis=1, keepdims=True), axis=0, keepdims=True) * (0.5 / D)
        gl_ref[3:4, :] += ls
        dx2 = err * (1.0 / D)
        gl_ref[0:1, :] += jnp.sum(dx2 * xh, axis=0, keepdims=True)
        gl_ref[1:2, :] += jnp.sum(dx2, axis=0, keepdims=True)
        dr2 = _ln_bwd(dx2 * lg, xh, rs)
        ex_ref[0:1, :] += jnp.sum(dr2 * mv, axis=0, keepdims=True)
        dmv = dr2 * g2
        gl_ref[2:3, :] += jnp.sum(dmv, axis=0, keepdims=True)
        dm_ref[...] = dmv.astype(BF16)
        dx_ref[...] = ALPHA * dr2

    la = pl.BlockSpec((TB, D), lambda b, j: (b * nbl + j, 0))
    ex = pl.BlockSpec((None, 1, D), lambda b, j: (b, 0, 0))
    vec = pl.BlockSpec((1, D), lambda b, j: (0, 0))
    return pl.pallas_call(
        body, name="final_fwd_bwd", grid=(cfg.Bn, nbl),
        in_specs=[la, la, vec, ex, vec, vec, la],
        out_specs=[la, la, pl.BlockSpec((None, 8, D), lambda b, j: (b, 0, 0)),
                   pl.BlockSpec((8, D), lambda b, j: (0, 0))],
        out_shape=[jax.ShapeDtypeStruct((cfg.N, D), BF16), jax.ShapeDtypeStruct((cfg.N, D), F32),
                   jax.ShapeDtypeStruct((cfg.Bn, 8, D), F32), jax.ShapeDtypeStruct((8, D), F32)],
        compiler_params=_cp(2),
    )(x1, mlp, b2, gate2, ln2_g, ln2_b, target)


def _ln_mod_bwd(cfg, dh_a, dh_b, xs, scale_tab, dxp):
    TB, nbt, nbl = cfg.TB, cfg.nbt, cfg.nbl

    def body(da_ref, db_ref, x_ref, sc_ref, dxp_ref, gx_ref, acc_ref):
        j = pl.program_id(1)

        @pl.when(j <= 1)
        def _():
            acc_ref[...] = jnp.zeros_like(acc_ref)

        dh = da_ref[...] + db_ref[...]
        xhat, rs = _ln(x_ref[...])
        acc_ref[0:1, :] += jnp.sum(dh, axis=0, keepdims=True)
        acc_ref[1:2, :] += jnp.sum(dh * xhat, axis=0, keepdims=True)
        gx_ref[...] = dxp_ref[...] + _ln_bwd(dh * (1.0 + sc_ref[...]), xhat, rs)

    st = pl.BlockSpec((TB, D), lambda b, j: (b * nbt + j, 0))
    la = pl.BlockSpec((TB, D), lambda b, j: (b * nbl + jnp.maximum(j - 1, 0), 0))
    return pl.pallas_call(
        body, name="ln_mod_bwd", grid=(cfg.Bn, nbt),
        in_specs=[st, st, st,
                  pl.BlockSpec((None, 1, D), lambda b, j: (2 * b + jnp.minimum(j, 1), 0, 0)), la],
        out_specs=[la, pl.BlockSpec((None, 8, D), lambda b, j: (2 * b + jnp.minimum(j, 1), 0, 0))],
        out_shape=[jax.ShapeDtypeStruct((cfg.N, D), F32), jax.ShapeDtypeStruct((2 * cfg.Bn, 8, D), F32)],
        compiler_params=_cp(2),
    )(dh_a, dh_b, xs, scale_tab, dxp)


def _perm_w_in(w_in):
    w_main = jnp.concatenate([w_in[:, 0:3072], w_in[:, 4160:5184], w_in[:, 3136:4160], w_in[:, 5184:10304]], axis=1)
    w_dt = jnp.pad(w_in[:, 3072:3136], ((0, 0), (0, DTW - 2 * NH)))
    return w_main, w_dt


def _unperm_w_in(dw_main, dw_dt):
    return jnp.concatenate([dw_main[:, 0:3072], dw_dt[:, :2 * NH], dw_main[:, 4096:5120],
                            dw_main[:, 3072:4096], dw_main[:, 5120:]], axis=1)


def _unpack_rest(rest_all):
    out, off = {}, 0
    for n, shp, axis in _BIG[1:]:
        shard_shape = (shp[0] // NDEV, shp[1]) if axis == 0 else (shp[0], shp[1] // NDEV)
        r = math.prod(shard_shape) // 1024
        out[n] = _from_slots(rest_all[:, off:off + r, :].reshape((NDEV,) + shard_shape), axis)
        off += r
    return out


def _local_step(cfg, x, ctx, target, m, mc, W, rest_payload):
    Bn, T, Tc = cfg.Bn, cfg.T, cfg.Tc
    NT, N = cfg.NT, cfg.N
    xs = jnp.concatenate([ctx, x], axis=1).reshape(NT, D)
    mch = [m[:, i * D:(i + 1) * D] for i in range(NMOD)]
    ctx_sh = jnp.broadcast_to(mc[None, :D], (Bn, D))
    ctx_sc = jnp.broadcast_to(mc[None, D:], (Bn, D))
    shift_tab = jnp.stack([ctx_sh, mch[0]], axis=1).reshape(2 * Bn, 1, D)
    scale_tab = jnp.stack([ctx_sc, mch[1]], axis=1).reshape(2 * Bn, 1, D)
    gate1 = mch[2].reshape(Bn, 1, D)
    shift2 = mch[3].reshape(Bn, 1, D)
    scale2 = mch[4].reshape(Bn, 1, D)
    gate2 = mch[5].reshape(Bn, 1, D)

    conv_w = jnp.concatenate([W["ssd_conv_w"], W["lru_conv_w"]], axis=1)
    conv_b = jnp.concatenate([W["ssd_conv_b"], W["lru_conv_b"]], axis=1)
    dt_bias = jnp.pad(W["ssd_dt_bias"].reshape(1, 2 * NH), ((0, 0), (0, DTW - 2 * NH)))
    a_log = jnp.pad(W["ssd_a_log"].reshape(1, 2 * NH), ((0, 0), (0, DTW - 2 * NH)))
    dvec = jnp.repeat(W["ssd_d"].reshape(NH), HD).reshape(1, DI)
    lba = W["lru_ba"].reshape(2, 1, LW)
    lbi = W["lru_bi"].reshape(2, 1, LW)
    llam = W["lru_lambda"].reshape(2, 1, LW)

    h = _ln_mod_fwd(cfg, xs, shift_tab, scale_tab)
    proj, rest_all = _mm(h, W["w_main"], "nn", "mm_proj", tm=1024, tn=1024, tk=1024, xch=(rest_payload, True))
    W = dict(W, **_unpack_rest(rest_all))
    dt_raw = _mm(h, W["w_dt"], "nn", "mm_dt", tm=512, tn=DTW, tk=1024)
    dt, dtg, cumg, cumTg = _dt_fwd(cfg, dt_raw, dt_bias, a_log)
    act, sgrad = _conv_fwd(cfg, proj, conv_w, conv_b)
    ys, hss, hls = [], [], []
    for rev in (False, True):
        y_d, hs_d = _ssd_fwd(cfg, act, dtg, cumg, cumTg, rev)
        ys.append(y_d)
        hss.append(hs_d)
        hls.append(_lru_fwd(cfg, act, W["lru_wa"], W["lru_wi"], lba, lbi, llam, rev))
    nssd = _post_ssd_fwd(cfg, ys[0], ys[1], act, proj, dvec, W["ssd_norm_w"])
    vlru = _post_lru_fwd(cfg, hls[0], hls[1], proj)
    br_ssd = _mm(nssd, W["w_br_ssd"], "nn", "mm_br_ssd", tm=1024, tn=1024, tk=1024)
    br_lru = _mm(vlru, W["w_br_lru"], "nn", "mm_br_lru", tm=1024, tn=1024, tk=1024)
    mix = _merge_fwd(cfg, proj, W["b_gate"], br_ssd, br_lru)
    x_mix = _mm(mix, W["w_out"], "nn", "mm_out", tm=1024, tn=1024, tk=1024)
    x1, h2 = _resid1_fwd(cfg, xs, x_mix, gate1, shift2, scale2, W["ln1_g"], W["ln1_b"])
    a1, actm = _mm_mlp1(h2, W["w_mlp1"], W["b_mlp1"])
    mlp = _mm(actm, W["w_mlp2"], "nn", "mm_mlp2", tm=1024, tn=1024, tk=1024)
    dmlp, dx1p, ex2, gl2 = _final_fwd_bwd(cfg, x1, mlp, W["b_mlp2"], gate2, W["ln2_g"], W["ln2_b"],
                                          target.reshape(N, D))

    g = {}
    g["ln2_g"], g["ln2_b"], g["b_mlp2"] = gl2[0:1], gl2[1:2], gl2[2:3]
    loss_partial = gl2[3, 0]
    gw = {}
    gw["w_mlp2"] = _mm(actm, dmlp, "tn", "mm_dw_mlp2", BF16, tm=1024, tn=1024, tk=512)
    da1, accb1 = _mm_dact(dmlp, W["w_mlp2"], a1)
    g["b_mlp1"] = accb1[0:1]
    dh2 = _mm(da1, W["w_mlp1"], "nt", "mm_dh2", tm=1024, tn=1024, tk=1024)
    gw["w_mlp1"] = _mm(h2, da1, "tn", "mm_dw_mlp1", BF16, tm=1024, tn=1024, tk=512)
    dx_mix, dxp, ex1, gl1 = _resid1_bwd(cfg, dh2, dx1p, x1, xs, x_mix, gate1, scale2, W["ln1_g"])
    g["ln1_g"], g["ln1_b"] = gl1[0:1], gl1[1:2]
    dmix = _mm(dx_mix, W["w_out"], "nt", "mm_dmix", tm=1024, tn=1024, tk=1024)
    gw["w_out"] = _mm(mix, dx_mix, "tn", "mm_dw_out", BF16, tm=1024, tn=1024, tk=512)
    dbs, dbl, dproj, accg = _merge_bwd(cfg, dmix, proj, W["b_gate"], br_ssd, br_lru)
    g["b_gate"] = accg[0:1]
    dnssd = _mm(dbs, W["w_br_ssd"], "nt", "mm_dnssd", tm=1024, tn=1024, tk=1024)
    gw["w_br_ssd"] = _mm(nssd, dbs, "tn", "mm_dw_br_ssd", BF16, tm=1024, tn=1024, tk=512)
    dvlru = _mm(dbl, W["w_br_lru"], "nt", "mm_dvlru", tm=1024, tn=1024, tk=1024)
    gw["w_br_lru"] = _mm(vlru, dbl, "tn", "mm_dw_br_lru", BF16, tm=1024, tn=1024, tk=512)
    dy, dproj, accs = _post_ssd_bwd(cfg, dproj, dnssd, ys[0], ys[1], act, proj, dvec, W["ssd_norm_w"])
    g["ssd_norm_w"] = accs[0:1]
    dD_cols = accs[1:2]
    dyl, dproj = _post_lru_bwd(cfg, dproj, dvlru, hls[0], hls[1], proj)

    rest_slots = jnp.concatenate([_to_slots(gw[n], axis).reshape(NDEV, -1, 1024) for n, _, axis in _BIG[1:]], axis=1)
    xres = {}
    dxh, dBs, dCs, dAs, dxxs, dus = [], [], [], [], [], []
    dwas, dwis, lvecs = [], [], []
    for i, rev in enumerate((False, True)):
        if i == 0:
            o, xres["rs_rest"] = _ssd_bwd(cfg, act, dtg, cumg, cumTg, hss[i], dy, rev, xch=(rest_slots, False))
        else:
            o = _ssd_bwd(cfg, act, dtg, cumg, cumTg, hss[i], dy, rev)
        dxh.append(o[0]); dBs.append(o[1]); dCs.append(o[2]); dAs.append(o[3]); dxxs.append(o[4])
        du, dwa, dwi, lv = _lru_bwd(cfg, act, W["lru_wa"], W["lru_wi"], lba, lbi, llam, hls[i], dyl, rev)
        dus.append(du); dwas.append(dwa); dwis.append(dwi); lvecs.append(lv)
    lru_payload = jnp.stack([jnp.stack(dwas), jnp.stack(dwis)]).reshape(-1, 1024)
    g["lru_ba"] = jnp.stack([lvecs[0][0], lvecs[1][0]])
    g["lru_bi"] = jnp.stack([lvecs[0][1], lvecs[1][1]])
    g["lru_lambda"] = jnp.stack([lvecs[0][2], lvecs[1][2]])

    ddt_raw, accdt = _dt_bwd(cfg, dAs, dxxs, dt_raw, dt, dt_bias, a_log)
    g["ssd_a_log"] = accdt[0, :2 * NH].reshape(2, NH)
    g["ssd_dt_bias"] = accdt[1, :2 * NH].reshape(2, NH)

    (dproj, accx), xres["ag_lru"] = _conv_bwd(cfg, "conv_bwd_x", dproj, proj, conv_w, sgrad, [dxh[0], dxh[1]], 0, DI,
                                              skip=(dy, dvec), xch=(lru_payload, True))
    dproj, accB = _conv_bwd(cfg, "conv_bwd_b", dproj, proj, conv_w, sgrad, [dBs[0], dBs[1]], DI, NG * NS)
    dproj, accC = _conv_bwd(cfg, "conv_bwd_c", dproj, proj, conv_w, sgrad, [dCs[0], dCs[1]], DI + NG * NS, NG * NS)
    dproj, accl = _conv_bwd(cfg, "conv_bwd_lru", dproj, proj, conv_w, sgrad, [dus[0], dus[1]], DI + 2 * NG * NS, LW)
    accssd = jnp.concatenate([accx, accB, accC], axis=1)
    g["ssd_conv_w"], g["ssd_conv_b"] = accssd[0:4], accssd[4:5]
    g["lru_conv_w"], g["lru_conv_b"] = accl[0:4], accl[4:5]
    dw_main = _mm(h, dproj, "tn", "mm_dw_main", BF16, tm=1024, tn=2048, tk=512)
    dw_dt = _mm(h, ddt_raw, "tn", "mm_dw_dt", BF16, tm=1024, tn=DTW, tk=512)
    w_in_slots = _to_slots(_unperm_w_in(dw_main, dw_dt), 1)
    dh_a, xres["rs_w_in"] = _mm(dproj, W["w_main"], "nt", "mm_dh_main", tm=1024, tn=1024, tk=1024,
                                xch=(w_in_slots, False))
    dh_b = _mm(ddt_raw, W["w_dt"], "nt", "mm_dh_dt", tm=512, tn=1024, tk=DTW)
    grad_x, acct = _ln_mod_bwd(cfg, dh_a, dh_b, xs, scale_tab, dxp)
    acct = acct.reshape(Bn, 2, 8, D)
    dm = jnp.concatenate([acct[:, 1, 0], acct[:, 1, 1], ex1[:, 2], ex1[:, 0], ex1[:, 1], ex2[:, 0]], axis=1)
    dmc = jnp.concatenate([acct[:, 0, 0], acct[:, 0, 1]], axis=1)
    g["ssd_d_cols"] = dD_cols
    return loss_partial, grad_x.reshape(Bn, T, D), g, dm, dmc, xres


MESH = pl.DeviceIdType.MESH
_HBM = pl.BlockSpec(memory_space=pltpu.HBM)


def _me():
    return 4 * lax.axis_index("x") + 2 * lax.axis_index("y") + lax.axis_index("c")


def _peer(k):
    px = (lax.axis_index("x") + ((k >> 2) & 1)) % 2
    py = (lax.axis_index("y") + ((k >> 1) & 1)) % 2
    pc = (lax.axis_index("c") + (k & 1)) % 2
    return (px, py, pc), 4 * px + 2 * py + pc


def _xchg_copies(x_ref, o_ref, send_sems, recv_sems, loc_sem, gather):
    me = _me()
    src_me = x_ref if gather else x_ref.at[me]
    loc = pltpu.make_async_copy(src_me, o_ref.at[me], loc_sem)
    sends, recvs = [], []
    for k in range(1, NDEV):
        peer, pid = _peer(k)
        sends.append(pltpu.make_async_remote_copy(
            src_ref=x_ref if gather else x_ref.at[pid], dst_ref=o_ref.at[me],
            send_sem=send_sems.at[k - 1], recv_sem=recv_sems.at[k - 1],
            device_id=peer, device_id_type=MESH))
        recvs.append(pltpu.make_async_remote_copy(
            src_ref=src_me, dst_ref=o_ref.at[pid],
            send_sem=send_sems.at[k - 1], recv_sem=recv_sems.at[k - 1],
            device_id=peer, device_id_type=MESH))
    return loc, sends, recvs


def _xchg_start(*refs, gather):
    loc, sends, _ = _xchg_copies(*refs, gather)
    loc.start()
    for cp in sends:
        cp.start()


def _xchg_wait(*refs, gather):
    loc, sends, recvs = _xchg_copies(*refs, gather)
    for cp in recvs:
        cp.wait_recv()
    for cp in sends:
        cp.wait_send()
    loc.wait()


_XCHG_SCRATCH = [pltpu.SemaphoreType.DMA((NDEV - 1,)), pltpu.SemaphoreType.DMA((NDEV - 1,)), pltpu.SemaphoreType.DMA]


def _xchg_out_shape(x, gather):
    return jax.ShapeDtypeStruct((NDEV,) + tuple(x.shape if gather else x.shape[1:]), x.dtype)


def _exchange(x, name, gather):
    def body(x_ref, o_ref, send_sems, recv_sems, loc_sem):
        _xchg_start(x_ref, o_ref, send_sems, recv_sems, loc_sem, gather=gather)
        _xchg_wait(x_ref, o_ref, send_sems, recv_sems, loc_sem, gather=gather)

    return pl.pallas_call(
        body, name=name, out_shape=_xchg_out_shape(x, gather),
        in_specs=[_HBM], out_specs=_HBM, scratch_shapes=_XCHG_SCRATCH,
    )(x)


def _gather_two_level(x, name):
    def body(x_ref, o_ref, send_sems, recv_sems, loc_sem):
        mx, my, mc = lax.axis_index("x"), lax.axis_index("y"), lax.axis_index("c")
        me, sibling = (mx, my, mc), (mx, my, 1 - mc)
        chips = [(1 - mx, my), (mx, 1 - my), (1 - mx, 1 - my)]

        def slot(px, py, pc):
            return o_ref.at[4 * px + 2 * py + pc]

        def copy(k, block, to, src=None):
            return pltpu.make_async_remote_copy(
                src_ref=slot(*block) if src is None else src, dst_ref=slot(*block),
                send_sem=send_sems.at[k], recv_sem=recv_sems.at[k], device_id=to, device_id_type=MESH)

        mine = pltpu.make_async_copy(x_ref, slot(*me), loc_sem)
        mine.start()
        first = [copy(0, me, sibling, src=x_ref)]
        first += [copy(1 + j, me, (*chip, mc), src=x_ref) for j, chip in enumerate(chips)]
        for cp in first:
            cp.start()
        passed = [copy(4 + j, (*chip, mc), sibling) for j, chip in enumerate(chips)]
        for j, chip in enumerate(chips):
            copy(1 + j, (*chip, mc), me).wait_recv()
            passed[j].start()
        copy(0, sibling, me).wait_recv()
        for j, chip in enumerate(chips):
            copy(4 + j, (*chip, 1 - mc), me).wait_recv()
        for cp in first + passed:
            cp.wait_send()
        mine.wait()

    return pl.pallas_call(
        body, name=name, out_shape=_xchg_out_shape(x, True),
        in_specs=[_HBM], out_specs=_HBM, scratch_shapes=_XCHG_SCRATCH,
    )(x)


def _hosted_call(body, xch, *, name, grid, in_specs, out_specs, out_shape, scratch_shapes, compiler_params, args,
                 aliases=None):
    aliases = aliases or {}
    if xch is None:
        return pl.pallas_call(body, name=name, grid=grid, in_specs=in_specs, out_specs=out_specs,
                              out_shape=out_shape, scratch_shapes=scratch_shapes, input_output_aliases=aliases,
                              compiler_params=compiler_params)(*args)
    xv, gather = xch
    n_in, n_out, n_scr = len(in_specs), len(out_specs), len(scratch_shapes)

    def wrapped(*refs):
        ins = refs[:n_in]
        x_ref = refs[n_in]
        outs = refs[n_in + 1:n_in + 1 + n_out]
        o_ref = refs[n_in + 1 + n_out]
        scr = refs[n_in + 2 + n_out:]
        own, sems = scr[:n_scr], scr[n_scr:]
        first = functools.reduce(jnp.logical_and, [pl.program_id(a) == 0 for a in range(len(grid))])
        last = functools.reduce(jnp.logical_and, [pl.program_id(a) == grid[a] - 1 for a in range(len(grid))])

        @pl.when(first)
        def _():
            _xchg_start(x_ref, o_ref, *sems, gather=gather)

        body(*ins, *outs, *own)

        @pl.when(last)
        def _():
            _xchg_wait(x_ref, o_ref, *sems, gather=gather)

    res = pl.pallas_call(
        wrapped, name=name, grid=grid, in_specs=list(in_specs) + [_HBM], out_specs=list(out_specs) + [_HBM],
        out_shape=list(out_shape) + [_xchg_out_shape(xv, gather)],
        scratch_shapes=list(scratch_shapes) + _XCHG_SCRATCH, input_output_aliases=aliases,
        compiler_params=compiler_params,
    )(*args, xv)
    return list(res[:n_out]), res[n_out]


def _row_tile(R, cap, mult=8):
    best = mult
    t = mult
    while t <= min(R, cap):
        if R % t == 0:
            best = t
        t += mult
    assert R % best == 0, R
    return best


def _sum_slots(x, name):
    _, R, C = x.shape
    tr = _row_tile(R, 256, 16 if x.dtype == BF16 else 8)

    def body(x_ref, o_ref):
        acc = x_ref[0].astype(F32)
        for i in range(1, NDEV):
            acc = acc + x_ref[i].astype(F32)
        o_ref[...] = acc

    return pl.pallas_call(
        body, name=name, grid=(R // tr,),
        in_specs=[pl.BlockSpec((NDEV, tr, C), lambda i: (0, i, 0))],
        out_specs=pl.BlockSpec((tr, C), lambda i: (i, 0)),
        out_shape=jax.ShapeDtypeStruct((R, C), F32),
        compiler_params=_cp(1),
    )(x)


def _adamw_update(w_ref, g_ref, m_ref, v_ref, d_ref, nm_ref, nv_ref):
    c1 = 1.0 / (1.0 - ADAM_B1 ** ADAM_STEP)
    c2 = 1.0 / (1.0 - ADAM_B2 ** ADAM_STEP)
    gv = g_ref[...]
    nm = ADAM_B1 * m_ref[...] + (1.0 - ADAM_B1) * gv
    nv = ADAM_B2 * v_ref[...] + (1.0 - ADAM_B2) * (gv * gv)
    d_ref[...] = -ADAM_LR * ((nm * c1) / (jnp.sqrt(nv * c2) + ADAM_EPS) + ADAM_WD * w_ref[...])
    nm_ref[...] = nm
    nv_ref[...] = nv


def _adamw_many(ws, gs, ms, vs):
    n = len(ws)

    def body(*refs):
        for i in range(n):
            _adamw_update(refs[i], refs[n + i], refs[2 * n + i], refs[3 * n + i],
                          refs[4 * n + i], refs[5 * n + i], refs[6 * n + i])

    shapes = [jax.ShapeDtypeStruct(w.shape, F32) for w in ws]
    res = pl.pallas_call(
        body, name="adamw_small", out_shape=shapes * 3,
        compiler_params=pltpu.CompilerParams(vmem_limit_bytes=VMEM_LIMIT_BYTES),
    )(*ws, *gs, *ms, *vs)
    return res[:n], res[n:2 * n], res[2 * n:]


def _adamw(w, g, m, v, name):
    R, C = w.shape
    tr = _row_tile(R, 256)

    def body(w_ref, g_ref, m_ref, v_ref, d_ref, nm_ref, nv_ref):
        _adamw_update(w_ref, g_ref, m_ref, v_ref, d_ref, nm_ref, nv_ref)

    blk = pl.BlockSpec((tr, C), lambda i: (i, 0))
    return pl.pallas_call(
        body, name=name, grid=(R // tr,),
        in_specs=[blk] * 4, out_specs=[blk] * 3,
        out_shape=[jax.ShapeDtypeStruct((R, C), F32)] * 3,
        compiler_params=_cp(1),
    )(w, g, m, v)


def _mod_fwd(c_rows, w_shard, b_shard):
    def body(c_ref, w_ref, b_ref, o_ref):
        s = _silu(c_ref[...]).astype(BF16)
        o_ref[...] = _dot(s, w_ref[...].astype(BF16)) + b_ref[...]

    return pl.pallas_call(
        body, name="mod_fwd",
        out_shape=jax.ShapeDtypeStruct((c_rows.shape[0], w_shard.shape[1]), F32),
        compiler_params=pltpu.CompilerParams(vmem_limit_bytes=VMEM_LIMIT_BYTES),
    )(c_rows, w_shard, b_shard)


def _mod_bwd(c_rows, dm_all, dm_shard, w_shard):
    nrow = c_rows.shape[0]

    def body(c_ref, da_ref, ds_ref, w_ref, gw_ref, gb_ref, cc_ref):
        s = _silu(c_ref[...]).astype(BF16)
        ds = ds_ref[...]
        gw_ref[...] = _dot_tn(s, ds.astype(BF16))
        gb_ref[...] = jnp.sum(da_ref[...], axis=0, keepdims=True)
        rowi = lax.broadcasted_iota(jnp.int32, ds.shape, 0)
        dmc = jnp.sum(jnp.where(rowi % 8 >= 4, ds, 0.0), axis=0, keepdims=True)
        dmc8 = jnp.broadcast_to(dmc, (8, ds.shape[1])).astype(BF16)
        cc_ref[...] = _dot_nt(dmc8, w_ref[...].astype(BF16))

    return pl.pallas_call(
        body, name="mod_bwd",
        out_shape=[jax.ShapeDtypeStruct(w_shard.shape, F32),
                   jax.ShapeDtypeStruct((1, dm_all.shape[1]), F32),
                   jax.ShapeDtypeStruct((8, D), F32)],
        compiler_params=pltpu.CompilerParams(vmem_limit_bytes=VMEM_LIMIT_BYTES),
    )(c_rows, dm_all, dm_shard, w_shard)


def _small_finish(cc_pre, c_ctx, dd_cols):
    def body(cc_ref, c_ref, dd_ref, gc_ref, gd_ref):
        gc_ref[...] = cc_ref[...] * _silu_grad(c_ref[...])
        gd_ref[...] = jnp.sum(dd_ref[...], axis=1, keepdims=True)

    return pl.pallas_call(
        body, name="small_finish",
        out_shape=[jax.ShapeDtypeStruct((1, D), F32), jax.ShapeDtypeStruct((NH, 1), F32)],
    )(cc_pre, c_ctx, dd_cols)


_BIG = (("w_in", (D, 10304), 1), ("w_br_ssd", (DI, D), 0), ("w_br_lru", (LW, D), 0), ("w_out", (D, D), 0),
        ("w_mlp1", (D, MLP), 1), ("w_mlp2", (MLP, D), 0))
_SMALL_SH = (("ssd_conv_w", (4, 4096)), ("lru_conv_w", (4, LW)), ("lru_ba", (2, LW)), ("lru_bi", (2, LW)),
             ("lru_lambda", (2, LW)))
_REPL = (("c_ctx", (D,)), ("b_gate", (2 * D,)), ("ssd_conv_b", (4096,)), ("ssd_dt_bias", (2, NH)),
         ("ssd_a_log", (2, NH)), ("ssd_d", (DI,)), ("ssd_norm_w", (DI,)), ("lru_conv_b", (LW,)),
         ("ln1_g", (D,)), ("ln1_b", (D,)),
         ("b_mlp1", (MLP,)), ("b_mlp2", (D,)), ("ln2_g", (D,)), ("ln2_b", (D,)))

_WEIGHT_NAMES = ('c_ctx', 'w_mod', 'b_mod', 'w_in', 'b_gate', 'ssd_conv_w', 'ssd_conv_b', 'ssd_dt_bias', 'ssd_a_log',
                 'ssd_d', 'ssd_norm_w', 'lru_conv_w', 'lru_conv_b', 'lru_wa', 'lru_ba', 'lru_wi', 'lru_bi',
                 'lru_lambda', 'w_br_ssd', 'w_br_lru', 'w_out', 'ln1_g', 'ln1_b', 'w_mlp1', 'b_mlp1', 'w_mlp2',
                 'b_mlp2', 'ln2_g', 'ln2_b')
_ARG_NAMES = ('x', 'c', 'ctx') + _WEIGHT_NAMES + ('loss_target',) + tuple('m_' + n for n in _WEIGHT_NAMES) + tuple(
    'v_' + n for n in _WEIGHT_NAMES)


def _to_slots(full, axis):
    n = full.shape[axis] // NDEV
    if axis == 0:
        return full.reshape(NDEV, n, full.shape[1])
    return full.reshape(full.shape[0], NDEV, n).transpose(1, 0, 2)


def _from_slots(slots, axis):
    if axis == 0:
        return slots.reshape(NDEV * slots.shape[1], slots.shape[2])
    return slots.transpose(1, 0, 2).reshape(slots.shape[1], NDEV * slots.shape[2])


def _pack_rows(arrs, width=1024, mult=8):
    flat = jnp.concatenate([a.reshape(-1) for a in arrs])
    n = flat.shape[0]
    per = width * mult
    tot = -(-n // per) * per
    return jnp.pad(flat, (0, tot - n)).reshape(tot // width, width)


def _unpack_rows(packed, shapes, lead=()):
    nl = len(lead)
    flat = packed.reshape(tuple(lead) + (-1,))
    out, off = [], 0
    for s in shapes:
        n = math.prod(s)
        out.append(flat[..., off:off + n].reshape(tuple(lead) + tuple(s)))
        off += n
    return out


def kernel(x, c, ctx, c_ctx, w_mod, b_mod, w_in, b_gate, ssd_conv_w, ssd_conv_b, ssd_dt_bias, ssd_a_log, ssd_d, ssd_norm_w, lru_conv_w, lru_conv_b, lru_wa, lru_ba, lru_wi, lru_bi, lru_lambda, w_br_ssd, w_br_lru, w_out, ln1_g, ln1_b, w_mlp1, b_mlp1, w_mlp2, b_mlp2, ln2_g, ln2_b, loss_target, m_c_ctx, m_w_mod, m_b_mod, m_w_in, m_b_gate, m_ssd_conv_w, m_ssd_conv_b, m_ssd_dt_bias, m_ssd_a_log, m_ssd_d, m_ssd_norm_w, m_lru_conv_w, m_lru_conv_b, m_lru_wa, m_lru_ba, m_lru_wi, m_lru_bi, m_lru_lambda, m_w_br_ssd, m_w_br_lru, m_w_out, m_ln1_g, m_ln1_b, m_w_mlp1, m_b_mlp1, m_w_mlp2, m_b_mlp2, m_ln2_g, m_ln2_b, v_c_ctx, v_w_mod, v_b_mod, v_w_in, v_b_gate, v_ssd_conv_w, v_ssd_conv_b, v_ssd_dt_bias, v_ssd_a_log, v_ssd_d, v_ssd_norm_w, v_lru_conv_w, v_lru_conv_b, v_lru_wa, v_lru_ba, v_lru_wi, v_lru_bi, v_lru_lambda, v_w_br_ssd, v_w_br_lru, v_w_out, v_ln1_g, v_ln1_b, v_w_mlp1, v_b_mlp1, v_w_mlp2, v_b_mlp2, v_ln2_g, v_ln2_b):
    A = dict(zip(_ARG_NAMES, (x, c, ctx, c_ctx, w_mod, b_mod, w_in, b_gate, ssd_conv_w, ssd_conv_b, ssd_dt_bias, ssd_a_log, ssd_d, ssd_norm_w, lru_conv_w, lru_conv_b, lru_wa, lru_ba, lru_wi, lru_bi, lru_lambda, w_br_ssd, w_br_lru, w_out, ln1_g, ln1_b, w_mlp1, b_mlp1, w_mlp2, b_mlp2, ln2_g, ln2_b, loss_target, m_c_ctx, m_w_mod, m_b_mod, m_w_in, m_b_gate, m_ssd_conv_w, m_ssd_conv_b, m_ssd_dt_bias, m_ssd_a_log, m_ssd_d, m_ssd_norm_w, m_lru_conv_w, m_lru_conv_b, m_lru_wa, m_lru_ba, m_lru_wi, m_lru_bi, m_lru_lambda, m_w_br_ssd, m_w_br_lru, m_w_out, m_ln1_g, m_ln1_b, m_w_mlp1, m_b_mlp1, m_w_mlp2, m_b_mlp2, m_ln2_g, m_ln2_b, v_c_ctx, v_w_mod, v_b_mod, v_w_in, v_b_gate, v_ssd_conv_w, v_ssd_conv_b, v_ssd_dt_bias, v_ssd_a_log, v_ssd_d, v_ssd_norm_w, v_lru_conv_w, v_lru_conv_b, v_lru_wa, v_lru_ba, v_lru_wi, v_lru_bi, v_lru_lambda, v_w_br_ssd, v_w_br_lru, v_w_out, v_ln1_g, v_ln1_b, v_w_mlp1, v_b_mlp1, v_w_mlp2, v_b_mlp2, v_ln2_g, v_ln2_b)))
    Bn, T, _ = x.shape
    Tc = ctx.shape[1]
    cfg = _Cfg(Bn, T, Tc)
    me = _me()
    L = {n: (A[n] if n == "c_ctx" else A[n][0]) for n in _WEIGHT_NAMES}
    nmod = L["w_mod"].shape[1]

    c_all = _exchange(c, "ag_c", True)
    c_rows = jnp.concatenate([c_all.reshape(NDEV * Bn, D), jnp.broadcast_to(c_ctx[None, :], (8, D))], axis=0)
    b_shard = lax.dynamic_slice(L["b_mod"], (me * nmod,), (nmod,)).reshape(1, nmod)
    m_part = _mod_fwd(c_rows, L["w_mod"], b_shard)
    m_all = _exchange(m_part, "ag_mod", True)
    m_full = m_all.transpose(1, 0, 2).reshape(NDEV * Bn + 8, NMOD * D)
    m_mine = lax.dynamic_slice(m_full, (me * Bn, 0), (Bn, NMOD * D))
    mc = m_full[NDEV * Bn, :2 * D]

    w_in_all = _gather_two_level(L["w_in"].astype(BF16), "ag_w_in")
    rest_payload = jnp.concatenate([L[n].astype(BF16).reshape(-1, 1024) for n, _, _ in _BIG[1:]], axis=0)
    small_shapes = [(s[0], s[1] // NDEV) for _, s in _SMALL_SH]
    small_all = _exchange(_pack_rows([L[n] for n, _ in _SMALL_SH], width=512), "ag_w_small", True)
    W = {}
    for (n, shp), piece in zip(_SMALL_SH, _unpack_rows(small_all, small_shapes, lead=(NDEV,))):
        W[n] = piece.transpose(1, 0, 2).reshape(shp)
    W["w_main"], W["w_dt"] = _perm_w_in(_from_slots(w_in_all, 1))
    for n in ("ssd_conv_b", "lru_conv_b", "ssd_norm_w", "b_gate", "ln1_g", "ln1_b", "b_mlp1", "b_mlp2", "ln2_g", "ln2_b"):
        W[n] = L[n].reshape(1, -1)
    for n in ("ssd_dt_bias", "ssd_a_log", "ssd_d", "lru_wa", "lru_wi"):
        W[n] = L[n]

    loss_part, grad_x, g, dm, dmc, xres = _local_step(cfg, x, ctx, loss_target, m_mine, mc, W, rest_payload)
    loss = lax.psum(loss_part, ("x", "y", "c"))

    dmc_pad = jnp.pad(dmc, ((0, 4 - Bn), (0, (NMOD - 2) * D)))
    dm_all = _exchange(jnp.concatenate([jnp.pad(dm, ((0, 4 - Bn), (0, 0))), dmc_pad], axis=0), "ag_dm", True)
    dm_all = dm_all.reshape(NDEV * 8, NMOD * D)
    c_rows_b = jnp.concatenate([jnp.pad(c_all, ((0, 0), (0, 4 - Bn), (0, 0))),
                                jnp.broadcast_to(c_ctx[None, None, :], (NDEV, 4, D))], axis=1).reshape(NDEV * 8, D)
    dm_shard = lax.dynamic_slice(dm_all, (0, me * nmod), (NDEV * 8, nmod))
    g_w_mod, g_b_mod, cc_part = _mod_bwd(c_rows_b, dm_all, dm_shard, L["w_mod"])
    g["c_ctx"] = cc_part[0]

    g["ssd_d"] = g.pop("ssd_d_cols")
    small_names = [n for n, _ in _REPL] + [n for n, _ in _SMALL_SH]
    small_full_shapes = [s for _, s in _REPL] + [s for _, s in _SMALL_SH]
    sm_all = _exchange(_pack_rows([g[n] for n in small_names]), "ag_g_small", True)
    sm_sum = _sum_slots(sm_all, "sum_g_small")
    gs = dict(zip(small_names, _unpack_rows(sm_sum, small_full_shapes)))
    gcc, gdd = _small_finish(gs["c_ctx"].reshape(1, D), c_ctx.reshape(1, D), gs["ssd_d"].reshape(NH, HD))
    gs["c_ctx"] = gcc.reshape(D)
    gs["ssd_d"] = gdd.reshape(NH)
    for n, shp in _SMALL_SH:
        ns = shp[1] // NDEV
        gs[n] = lax.dynamic_slice(gs[n], (0, me * ns), (shp[0], ns))
    gs["b_mod"] = g_b_mod.reshape(NMOD * D)
    lru_sum = _sum_slots(xres["ag_lru"], "sum_g_lru").reshape(2, 2, LB, LBW, LBW)
    gs["lru_wa"], gs["lru_wi"] = lru_sum[0], lru_sum[1]

    gb = {}
    gb["w_in"] = _sum_slots(xres["rs_w_in"], "sum_w_in")
    red_b = _sum_slots(xres["rs_rest"], "sum_w_rest")
    off = 0
    for n, shp, axis in _BIG[1:]:
        shard_shape = (shp[0] // NDEV, shp[1]) if axis == 0 else (shp[0], shp[1] // NDEV)
        r = math.prod(shard_shape) // 1024
        gb[n] = red_b[off:off + r].reshape(shard_shape)
        off += r
    gb["w_mod"] = g_w_mod

    grads, deltas, new_m, new_v = {}, {}, {}, {}
    big_names = ["w_mod"] + [n for n, _, _ in _BIG]
    for n in big_names:
        d_, nm_, nv_ = _adamw(L[n], gb[n], A["m_" + n][0], A["v_" + n][0], "adamw_" + n)
        grads[n], deltas[n], new_m[n], new_v[n] = gb[n], d_, nm_, nv_
    sm_names = [n for n in _WEIGHT_NAMES if n not in big_names]

    def two_d(a):
        return a.reshape(1, -1) if a.ndim == 1 else a
    loc = lambda pre: [two_d(A[pre + n] if n == "c_ctx" else A[pre + n][0]) for n in sm_names]
    gsm = [two_d(gs[n].reshape(L[n].shape)) for n in sm_names]
    ds_, nms_, nvs_ = _adamw_many(loc(""), gsm, loc("m_"), loc("v_"))
    for n, gv, dv, mv, vv in zip(sm_names, gsm, ds_, nms_, nvs_):
        shp = L[n].shape
        grads[n], deltas[n], new_m[n], new_v[n] = gv.reshape(shp), dv.reshape(shp), mv.reshape(shp), vv.reshape(shp)

    def out(dct):
        return [dct[n] if n == "c_ctx" else dct[n][None] for n in _WEIGHT_NAMES]
    return (loss, grad_x, *out(grads), *out(deltas), *out(new_m), *out(new_v))
```

```python
import functools
import math

import jax
import jax.numpy as jnp
from jax import lax
from jax.experimental import pallas as pl
from jax.experimental.pallas import tpu as pltpu

F32 = jnp.float32
BF16 = jnp.bfloat16

D = 1024
GRID_W = 64
DI = 2048
NH = 32
HD = 64
NG = 8
HPG = 4
NS = 128
CH = 128
LW = 1024
LB = 8
LBW = 128
LRU_C = 8.0
MLP = 4096
NMOD = 6
ALPHA = 2.0 ** 0.25
LN_EPS = 1e-6
RMS_EPS = 1e-5
PM = 10240
DTW = 128
CONVW = 5120
NDEV = 8

ADAM_LR = 0.001
ADAM_B1 = 0.9
ADAM_B2 = 0.999
ADAM_EPS = 1e-08
ADAM_WD = 0.01
ADAM_STEP = 10

VMEM_LIMIT_BYTES = 56 * 1024 * 1024


def _cp(n_axes):
    return pltpu.CompilerParams(dimension_semantics=("arbitrary",) * n_axes,
                                vmem_limit_bytes=VMEM_LIMIT_BYTES)


def _sigmoid(x):
    return 0.5 * jnp.tanh(0.5 * x) + 0.5


def _silu(x):
    return x * _sigmoid(x)


def _silu_grad(x):
    s = _sigmoid(x)
    return s * (1.0 + x * (1.0 - s))


def _log1p_pos(e):
    return jnp.where(e < 1e-2, e * (1.0 - e * (0.5 - e * (1.0 / 3.0))), jnp.log(1.0 + e))


def _softplus(x):
    return jnp.maximum(x, 0.0) + _log1p_pos(jnp.exp(-jnp.abs(x)))


_GELU_K = math.sqrt(2.0 / math.pi)


def _gelu(x):
    t = jnp.tanh(_GELU_K * (x + 0.044715 * x * x * x))
    return 0.5 * x * (1.0 + t)


def _gelu_grad(x):
    t = jnp.tanh(_GELU_K * (x + 0.044715 * x * x * x))
    dt = (1.0 - t * t) * _GELU_K * (1.0 + 3.0 * 0.044715 * x * x)
    return 0.5 * (1.0 + t) + 0.5 * x * dt


def _ln(x):
    mu = jnp.mean(x, axis=-1, keepdims=True)
    xc = x - mu
    var = jnp.mean(xc * xc, axis=-1, keepdims=True)
    rs = lax.rsqrt(var + LN_EPS)
    return xc * rs, rs


def _ln_bwd(dy, xhat, rs):
    m1 = jnp.mean(dy, axis=-1, keepdims=True)
    m2 = jnp.mean(dy * xhat, axis=-1, keepdims=True)
    return rs * (dy - m1 - xhat * m2)


def _dot(a, b):
    return lax.dot_general(a, b, (((1,), (0,)), ((), ())), preferred_element_type=F32)


def _dot_nt(a, b):
    return lax.dot_general(a, b, (((1,), (1,)), ((), ())), preferred_element_type=F32)


def _dot_tn(a, b):
    return lax.dot_general(a, b, (((0,), (0,)), ((), ())), preferred_element_type=F32)


def _split3(a):
    a0 = a.astype(BF16)
    r = a - a0.astype(F32)
    a1 = r.astype(BF16)
    a2 = (r - a1.astype(F32)).astype(BF16)
    return a0, a1, a2


def _dot_exact_l(m_bf, a):
    a0, a1, a2 = _split3(a)
    return _dot(m_bf, a0) + _dot(m_bf, a1) + _dot(m_bf, a2)


def _dot_exact_r(a, m_bf):
    a0, a1, a2 = _split3(a)
    return _dot(a0, m_bf) + _dot(a1, m_bf) + _dot(a2, m_bf)


def _dot_hilo_r(a, m_bf):
    a0 = a.astype(BF16)
    a1 = (a - a0.astype(F32)).astype(BF16)
    return _dot(a0, m_bf) + _dot(a1, m_bf)


def _tri(n, upper):
    ii = lax.broadcasted_iota(jnp.int32, (n, n), 0)
    kk = lax.broadcasted_iota(jnp.int32, (n, n), 1)
    m = (kk >= ii) if upper else (kk <= ii)
    return jnp.where(m, 1.0, 0.0).astype(BF16)


def _fit(n, t):
    t = min(t, n)
    while n % t:
        t //= 2
    return t


def _mm(a, b, mode, name, out_dtype=F32, tm=512, tn=512, tk=512, xch=None):
    if mode == "nn":
        M, K = a.shape
        N = b.shape[1]
    elif mode == "nt":
        M, K = a.shape
        N = b.shape[0]
    else:
        K, M = a.shape
        N = b.shape[1]
    tm, tn, tk = _fit(M, tm), _fit(N, tn), _fit(K, tk)
    assert M % tm == 0 and N % tn == 0 and K % tk == 0, (name, M, N, K, tm, tn, tk)
    nk = K // tk
    if mode == "tn":
        a_spec = pl.BlockSpec((tk, tm), lambda i, j, k: (k, i))
    else:
        a_spec = pl.BlockSpec((tm, tk), lambda i, j, k: (i, k))
    if mode == "nt":
        b_spec = pl.BlockSpec((tn, tk), lambda i, j, k: (j, k))
    else:
        b_spec = pl.BlockSpec((tk, tn), lambda i, j, k: (k, j))
    dn = {"nn": (((1,), (0,)), ((), ())), "nt": (((1,), (1,)), ((), ())), "tn": (((0,), (0,)), ((), ()))}[mode]

    def body(a_ref, b_ref, o_ref, acc_ref):
        k = pl.program_id(2)

        @pl.when(k == 0)
        def _():
            acc_ref[...] = jnp.zeros_like(acc_ref)

        acc_ref[...] += lax.dot_general(a_ref[...].astype(BF16), b_ref[...].astype(BF16), dn,
                                        preferred_element_type=F32)

        @pl.when(k == nk - 1)
        def _():
            o_ref[...] = acc_ref[...].astype(o_ref.dtype)

    res = _hosted_call(
        body, xch, name=name, grid=(M // tm, N // tn, nk),
        in_specs=[a_spec, b_spec],
        out_specs=[pl.BlockSpec((tm, tn), lambda i, j, k: (i, j))],
        out_shape=[jax.ShapeDtypeStruct((M, N), out_dtype)],
        scratch_shapes=[pltpu.VMEM((tm, tn), F32)],
        compiler_params=_cp(3), args=(a, b))
    if xch is None:
        return res[0]
    return res[0][0], res[1]


def _mm_mlp1(h2, w1, b1, tm=1024, tn=1024):
    M, K = h2.shape
    N = w1.shape[1]
    tm, tn = _fit(M, tm), _fit(N, tn)

    def body(a_ref, b_ref, bias_ref, a1_ref, act_ref):
        v = _dot(a_ref[...], b_ref[...]) + bias_ref[...]
        a1_ref[...] = v
        r = jnp.maximum(v, 0.0)
        act_ref[...] = (r * r).astype(BF16)

    out = pl.BlockSpec((tm, tn), lambda i, j: (i, j))
    return pl.pallas_call(
        body, name="mm_mlp1", grid=(M // tm, N // tn),
        in_specs=[pl.BlockSpec((tm, K), lambda i, j: (i, 0)), pl.BlockSpec((K, tn), lambda i, j: (0, j)),
                  pl.BlockSpec((1, tn), lambda i, j: (0, j))],
        out_specs=[out, out],
        out_shape=[jax.ShapeDtypeStruct((M, N), F32), jax.ShapeDtypeStruct((M, N), BF16)],
        compiler_params=_cp(2),
    )(h2, w1, b1)


def _mm_dact(dmlp, w2, a1, tm=1024, tn=1024):
    M, K = dmlp.shape
    N = w2.shape[0]
    tm, tn = _fit(M, tm), _fit(N, tn)

    def body(d_ref, w_ref, a1_ref, o_ref, acc_ref):
        i = pl.program_id(1)

        @pl.when(i == 0)
        def _():
            acc_ref[...] = jnp.zeros_like(acc_ref)

        da = _dot_nt(d_ref[...], w_ref[...]) * (2.0 * jnp.maximum(a1_ref[...], 0.0))
        o_ref[...] = da.astype(BF16)
        acc_ref[0:1, :] += jnp.sum(da, axis=0, keepdims=True)

    blk = pl.BlockSpec((tm, tn), lambda j, i: (i, j))
    return pl.pallas_call(
        body, name="mm_dact", grid=(N // tn, M // tm),
        in_specs=[pl.BlockSpec((tm, K), lambda j, i: (i, 0)), pl.BlockSpec((tn, K), lambda j, i: (j, 0)), blk],
        out_specs=[blk, pl.BlockSpec((8, tn), lambda j, i: (0, j))],
        out_shape=[jax.ShapeDtypeStruct((M, N), BF16), jax.ShapeDtypeStruct((8, N), F32)],
        compiler_params=_cp(2),
    )(dmlp, w2, a1)


class _Cfg:
    def __init__(self, Bn, T, Tc):
        assert T % Tc == 0 and Tc % CH == 0 and Tc % GRID_W == 0
        self.Bn, self.T, self.Tc = Bn, T, Tc
        self.TT = T + Tc
        self.TB = Tc
        self.nbt = self.TT // self.TB
        self.nbl = T // self.TB
        self.NT = Bn * self.TT
        self.N = Bn * T
        self.nct = Tc // CH
        self.nlt = T // CH
        self.nch = self.nct + self.nlt


def _ln_mod_fwd(cfg, xs, shift_tab, scale_tab):
    TB, nbt = cfg.TB, cfg.nbt

    def body(x_ref, sh_ref, sc_ref, o_ref):
        xhat, _ = _ln(x_ref[...])
        o_ref[...] = (xhat * (1.0 + sc_ref[...]) + sh_ref[...]).astype(BF16)

    tab = pl.BlockSpec((None, 1, D), lambda b, j: (2 * b + jnp.minimum(j, 1), 0, 0))
    return pl.pallas_call(
        body, name="ln_mod_fwd", grid=(cfg.Bn, nbt),
        in_specs=[pl.BlockSpec((TB, D), lambda b, j: (b * nbt + j, 0)), tab, tab],
        out_specs=pl.BlockSpec((TB, D), lambda b, j: (b * nbt + j, 0)),
        out_shape=jax.ShapeDtypeStruct((cfg.NT, D), BF16),
        compiler_params=_cp(2),
    )(xs, shift_tab, scale_tab)


GP = 8
NGB = NG // GP
HPB = GP * HPG


def _head_select(d, gb, to_front):
    r = lax.broadcasted_iota(jnp.int32, (DTW, DTW), 0)
    c = lax.broadcasted_iota(jnp.int32, (DTW, DTW), 1)
    src, dst = (r, c) if to_front else (c, r)
    hit = jnp.logical_and(src == d * NH + gb * HPB + dst, dst < HPB)
    return jnp.where(hit, 1.0, 0.0).astype(BF16)


def _dt_fwd(cfg, dt_raw, dt_bias, a_log):
    def body(raw_ref, bias_ref, alog_ref, dt_ref, dtg_ref, cumg_ref, cumT_ref):
        dt = _softplus(raw_ref[...] + bias_ref[...])
        a = -jnp.exp(alog_ref[...])
        dta = dt * a
        col = lax.broadcasted_iota(jnp.int32, (CH, DTW), 1)
        cf = _dot_exact_l(_tri(CH, False), dta)
        cr = _dot_exact_l(_tri(CH, True), dta)
        cum = jnp.where(col < NH, cf, cr)
        dt_ref[...] = dt
        for d in range(2):
            for gb in range(NGB):
                sel = _head_select(d, gb, True)
                dtg_ref[d, gb] = _dot_exact_r(dt, sel)
                cg = _dot_exact_r(cum, sel)
                cumg_ref[d, gb] = cg
                cumT_ref[d, gb] = cg.T

    blk = pl.BlockSpec((CH, DTW), lambda i: (i, 0))
    row = pl.BlockSpec((1, DTW), lambda i: (0, 0))
    gblk = pl.BlockSpec((2, NGB, CH, DTW), lambda i: (0, 0, i, 0))
    return pl.pallas_call(
        body, name="dt_fwd", grid=(cfg.NT // CH,),
        in_specs=[blk, row, row],
        out_specs=[blk, gblk, gblk, pl.BlockSpec((2, NGB, None, DTW, CH), lambda i: (0, 0, i, 0, 0))],
        out_shape=[jax.ShapeDtypeStruct((cfg.NT, DTW), F32),
                   jax.ShapeDtypeStruct((2, NGB, cfg.NT, DTW), F32),
                   jax.ShapeDtypeStruct((2, NGB, cfg.NT, DTW), F32),
                   jax.ShapeDtypeStruct((2, NGB, cfg.NT // CH, DTW, CH), F32)],
        compiler_params=_cp(1),
    )(dt_raw, dt_bias, a_log)


def _dt_bwd(cfg, dAs, dxxs, dt_raw, dt, dt_bias, a_log):
    def body(dAf_ref, dAr_ref, dxf_ref, dxr_ref, raw_ref, dt_ref, bias_ref, alog_ref, o_ref, acc_ref):
        i = pl.program_id(0)

        @pl.when(i == 0)
        def _():
            acc_ref[...] = jnp.zeros_like(acc_ref)

        a = -jnp.exp(alog_ref[...])
        col = lax.broadcasted_iota(jnp.int32, (CH, DTW), 1)
        dA_v = jnp.zeros((CH, DTW), F32)
        dxx_v = jnp.zeros((CH, DTW), F32)
        for d, (ra, rx) in enumerate(((dAf_ref, dxf_ref), (dAr_ref, dxr_ref))):
            for gb in range(NGB):
                sel = _head_select(d, gb, False)
                dA_v = dA_v + _dot_exact_r(ra[gb], sel)
                dxx_v = dxx_v + _dot_exact_r(rx[gb], sel)
        ddta = jnp.where(col < NH, _dot_exact_l(_tri(CH, True), dA_v), _dot_exact_l(_tri(CH, False), dA_v))
        dtv = dt_ref[...]
        ddt = ddta * a + dxx_v
        draw = ddt * _sigmoid(raw_ref[...] + bias_ref[...])
        draw = jnp.where(col < 2 * NH, draw, 0.0)
        o_ref[...] = draw
        da = jnp.sum(ddta * dtv, axis=0, keepdims=True) * a
        da = jnp.where(col[:1] < 2 * NH, da, 0.0)
        acc_ref[0:1, :] += da
        acc_ref[1:2, :] += jnp.sum(draw, axis=0, keepdims=True)

    blk = pl.BlockSpec((CH, DTW), lambda i: (i, 0))
    row = pl.BlockSpec((1, DTW), lambda i: (0, 0))
    gblk = pl.BlockSpec((NGB, CH, DTW), lambda i: (0, i, 0))
    return pl.pallas_call(
        body, name="dt_bwd", grid=(cfg.NT // CH,),
        in_specs=[gblk, gblk, gblk, gblk, blk, blk, row, row],
        out_specs=[blk, pl.BlockSpec((8, DTW), lambda i: (0, 0))],
        out_shape=[jax.ShapeDtypeStruct((cfg.NT, DTW), F32), jax.ShapeDtypeStruct((8, DTW), F32)],
        compiler_params=_cp(1),
    )(dAs[0], dAs[1], dxxs[0], dxxs[1], dt_raw, dt, dt_bias, a_log)


_TAPS = (2, 1, 0, -1)


def _conv_fwd(cfg, proj, conv_w, conv_b):
    TB, nbt = cfg.TB, cfg.nbt
    CB = CONVW // 2
    SUB = 256
    n_act = DI + 2 * NG * NS

    def body(u_ref, w_ref, b_ref, o_ref, sg_ref):
        i = pl.program_id(0)
        j = pl.program_id(1)
        R = jnp.where(i % nbt == 0, cfg.Tc, GRID_W)
        t = lax.broadcasted_iota(jnp.int32, (TB, SUB), 0)
        pos = jnp.bitwise_and(t, R - 1)
        keep = {s: jnp.where(jnp.logical_and(pos - s >= 0, pos - s < R), 1.0, 0.0) for s in (2, 1, -1)}
        for q in range(CB // SUB):
            sl = slice(q * SUB, (q + 1) * SUB)
            u = u_ref[:, sl]
            pre = b_ref[:, sl] + w_ref[2:3, sl] * u
            for k in (0, 1, 3):
                pre = pre + w_ref[k:k + 1, sl] * (pltpu.roll(u, _TAPS[k] % TB, 0) * keep[_TAPS[k]])
            col = j * CB + q * SUB + lax.broadcasted_iota(jnp.int32, (1, SUB), 1)
            s = _sigmoid(pre)
            o_ref[:, sl] = jnp.where(col < n_act, pre * s, pre)
            sg_ref[:, sl] = jnp.where(col < n_act, s * (1.0 + pre * (1.0 - s)), 1.0)

    blk = pl.BlockSpec((TB, CB), lambda i, j: (i, j))
    return pl.pallas_call(
        body, name="conv_fwd", grid=(cfg.NT // TB, CONVW // CB),
        in_specs=[blk, pl.BlockSpec((4, CB), lambda i, j: (0, j)), pl.BlockSpec((1, CB), lambda i, j: (0, j))],
        out_specs=[blk, blk],
        out_shape=[jax.ShapeDtypeStruct((cfg.NT, CONVW), F32)] * 2,
        compiler_params=_cp(2),
    )(proj, conv_w, conv_b)


_ANY = pl.BlockSpec(memory_space=pl.ANY)


def _conv_bwd(cfg, name, dproj, proj, conv_w, sgrad, addends, col0, width, skip=None, xch=None):
    TB, nbt, nbl = cfg.TB, cfg.nbt, cfg.nbl
    CB = 1024
    SUB = 256
    c0 = col0 // CB
    n_add = len(addends)

    def body(*refs):
        u_ref, w_ref, sg_ref = refs[1:4]
        add_refs = refs[4:4 + n_add]
        rest = refs[4 + n_add:]
        if skip is not None:
            dy_ref, dv_ref = rest[:2]
            rest = rest[2:]
        o_ref, acc_ref = rest
        i = pl.program_id(1)

        @pl.when(i == 0)
        def _():
            acc_ref[...] = jnp.zeros_like(acc_ref)

        isctx = (i % nbt) == 0
        R = jnp.where(isctx, cfg.Tc, GRID_W)
        t = lax.broadcasted_iota(jnp.int32, (TB, SUB), 0)
        pos = jnp.bitwise_and(t, R - 1)
        keep = {s: jnp.where(jnp.logical_and(pos - s >= 0, pos - s < R), 1.0, 0.0) for s in (2, 1, -1, -2)}

        def shifted(v, s):
            return v if s == 0 else pltpu.roll(v, s % TB, 0) * keep[s]

        for q in range(CB // SUB):
            sl = slice(q * SUB, (q + 1) * SUB)
            u = u_ref[:, sl]
            us = [shifted(u, _TAPS[k]) for k in range(4)]
            g = add_refs[0][:, sl]
            for r in add_refs[1:]:
                g = g + r[:, sl]
            if skip is not None:
                g = g + jnp.where(isctx, 0.0, dv_ref[:, sl] * dy_ref[:, sl])
            g = g * sg_ref[:, sl]
            dp = jnp.zeros_like(g)
            for k in range(4):
                acc_ref[k:k + 1, sl] += jnp.sum(g * us[k], axis=0, keepdims=True)
                dp = dp + w_ref[k:k + 1, sl] * shifted(g, -_TAPS[k])
            acc_ref[4:5, sl] += jnp.sum(g, axis=0, keepdims=True)
            o_ref[:, sl] = dp.astype(BF16)

    blk = pl.BlockSpec((TB, CB), lambda j, i: (i, j))
    in_specs = [_ANY,
                pl.BlockSpec((TB, CB), lambda j, i: (i, c0 + j)),
                pl.BlockSpec((4, CB), lambda j, i: (0, c0 + j)),
                pl.BlockSpec((TB, CB), lambda j, i: (i, c0 + j))] + [blk] * n_add
    args = [dproj, proj, conv_w, sgrad] + list(addends)
    if skip is not None:
        def lat(j, i):
            b = i // nbt
            return (b * nbl + jnp.maximum(i % nbt - 1, 0), j)
        in_specs += [pl.BlockSpec((TB, CB), lat), pl.BlockSpec((1, CB), lambda j, i: (0, j))]
        args += list(skip)
    return _hosted_call(
        body, xch, name=name, grid=(width // CB, cfg.NT // TB),
        in_specs=in_specs,
        out_specs=[pl.BlockSpec((TB, CB), lambda j, i: (i, c0 + j)), pl.BlockSpec((8, CB), lambda j, i: (0, j))],
        out_shape=[jax.ShapeDtypeStruct((cfg.NT, PM), BF16), jax.ShapeDtypeStruct((8, width), F32)],
        scratch_shapes=[], compiler_params=_cp(2), args=args, aliases={0: 0})


def _chunk_of_step(cfg, rev):
    nct, nlt = cfg.nct, cfg.nlt
    if not rev:
        return lambda s: s
    return lambda s: jnp.where(s < nct, nct - 1 - s, 2 * nct + nlt - 1 - s)


def _expand4(v, band, base):
    out = v[:, base + 3:base + 4]
    for h in (2, 1, 0):
        out = jnp.where(band == h, v[:, base + h:base + h + 1], out)
    return out


def _ssd_step_tiles(dt_ref, cum_ref, cumT_ref, rev):
    cum_t = cum_ref[...]
    last = 0 if rev else CH - 1
    llast = cum_t[last:last + 1, :]
    return (dt_ref[...], cum_t, cumT_ref[...], llast, jnp.exp(llast), last)


def _ssd_common(gi, x_ref, b_ref, c_ref, tiles, rev, intra=True):
    dt_t, cum_t, cumT_t, llast, elast, last = tiles
    base = gi * HPG
    xh = x_ref[:, gi * HPG * HD:(gi + 1) * HPG * HD]
    Bm = b_ref[:, gi * NS:(gi + 1) * NS].astype(BF16)
    band = lax.broadcasted_iota(jnp.int32, (CH, HPG * HD), 1) // HD
    cbs = [jnp.broadcast_to(cum_t[:, base + h:base + h + 1], (CH, CH)) for h in range(HPG)]
    Cm = G = decs = None
    if intra:
        Cm = c_ref[:, gi * NS:(gi + 1) * NS].astype(BF16)
        G = _dot_nt(Cm, Bm)
        ii = lax.broadcasted_iota(jnp.int32, (CH, CH), 0)
        jj = lax.broadcasted_iota(jnp.int32, (CH, CH), 1)
        mask = (jj >= ii) if rev else (jj <= ii)
        decs = [jnp.exp(jnp.where(mask, cbs[h] - cumT_t[base + h:base + h + 1, :], -1e30)) for h in range(HPG)]
    cum_exp = jnp.concatenate([cbs[3], cbs[3]], axis=1)
    ll_exp = llast[:, base + 3:base + 4]
    for h in (2, 1, 0):
        cum_exp = jnp.where(band == h, jnp.concatenate([cbs[h], cbs[h]], axis=1), cum_exp)
        ll_exp = jnp.where(band[:1] == h, llast[:, base + h:base + h + 1], ll_exp)
    ecum = jnp.exp(cum_exp) if intra else None
    e_exp = jnp.exp(ll_exp - cum_exp)
    dt_exp = _expand4(dt_t, band, base)
    X = xh * dt_exp
    rb = lax.broadcasted_iota(jnp.int32, (HPG * HD, NS), 0) // HD
    dec_rows = elast[:, base + 3:base + 4]
    for h in (2, 1, 0):
        dec_rows = jnp.where(rb == h, elast[:, base + h:base + h + 1], dec_rows)
    return xh, Bm, Cm, band, e_exp, ecum, dt_exp, X, G, decs, elast, dec_rows, last


def _ssd_specs(cfg, rev):
    nch = cfg.nch
    cmap = _chunk_of_step(cfg, rev)
    d = 1 if rev else 0

    def make(stepmap):
        def row(b, g, sp):
            return b * nch + cmap(stepmap(sp))
        bo, co = DI // (GP * NS), (DI + NG * NS) // (GP * NS)
        return [
            pl.BlockSpec((CH, GP * HPG * HD), lambda b, g, sp: (row(b, g, sp), g)),
            pl.BlockSpec((CH, GP * NS), lambda b, g, sp: (row(b, g, sp), bo + g)),
            pl.BlockSpec((CH, GP * NS), lambda b, g, sp: (row(b, g, sp), co + g)),
            pl.BlockSpec((None, None, CH, DTW), lambda b, g, sp: (d, g, row(b, g, sp), 0)),
            pl.BlockSpec((None, None, CH, DTW), lambda b, g, sp: (d, g, row(b, g, sp), 0)),
            pl.BlockSpec((None, None, None, DTW, CH), lambda b, g, sp: (d, g, row(b, g, sp), 0, 0)),
        ], row
    return make


def _ssd_fwd(cfg, act, dtg, cumg, cumTg, rev, y_other=None, dvec=None):
    nch = cfg.nch
    in_specs, row = _ssd_specs(cfg, rev)(lambda sp: sp)
    total = y_other is not None

    def body(*refs):
        x_ref, b_ref, c_ref, dt_ref, cum_ref, cumT_ref = refs[:6]
        if total:
            yo_ref, dv_ref = refs[6:8]
        y_ref, hs_ref, h_scr = refs[-3:]
        s = pl.program_id(2)

        @pl.when(s == 0)
        def _():
            h_scr[...] = jnp.zeros_like(h_scr)

        def step(intra):
            tiles = _ssd_step_tiles(dt_ref, cum_ref, cumT_ref, rev)
            for gi in range(GP):
                xh, Bm, Cm, band, e_exp, ecum, dt_exp, X, G, decs, elast, dec_rows, last = _ssd_common(
                    gi, x_ref, b_ref, c_ref, tiles, rev, intra)
                H = h_scr[gi]
                if intra:
                    Mcat = jnp.concatenate([(G * decs[h]).astype(BF16) for h in range(HPG)], axis=1)
                    Xbd = jnp.concatenate([jnp.where(band == h, X, 0.0).astype(BF16) for h in range(HPG)], axis=0)
                    xsl = slice(gi * HPG * HD, (gi + 1) * HPG * HD)
                    Y = ecum * _dot_nt(Cm, H.astype(BF16)) + _dot(Mcat, Xbd)
                    if total:
                        Y = Y + yo_ref[:, xsl] + dv_ref[:, xsl] * xh
                    y_ref[:, xsl] = Y
                hs_ref[gi] = H
                S = _dot_tn((e_exp * X).astype(BF16), Bm)
                h_scr[gi] = dec_rows * H + S

        isctx = cmap(s) < cfg.nct

        @pl.when(isctx)
        def _():
            step(False)

        @pl.when(jnp.logical_not(isctx))
        def _():
            step(True)

    cmap = _chunk_of_step(cfg, rev)
    yblk = pl.BlockSpec((CH, GP * HPG * HD), lambda b, g, s: (row(b, g, s), g))
    args = [act, act, act, dtg, cumg, cumTg]
    if total:
        in_specs = in_specs + [yblk, pl.BlockSpec((1, GP * HPG * HD), lambda b, g, s: (0, g))]
        args += [y_other, dvec]
    return pl.pallas_call(
        body, name="ssd_fwd_rev" if rev else "ssd_fwd", grid=(cfg.Bn, NG // GP, nch),
        in_specs=in_specs,
        out_specs=[yblk, pl.BlockSpec((None, GP, None, HPG * HD, NS), lambda b, g, s: (b, g, s, 0, 0))],
        out_shape=[jax.ShapeDtypeStruct((cfg.NT, DI), F32),
                   jax.ShapeDtypeStruct((cfg.Bn, NG, nch, HPG * HD, NS), F32)],
        scratch_shapes=[pltpu.VMEM((GP, HPG * HD, NS), F32)],
        compiler_params=_cp(3),
    )(*args)


def _ssd_bwd(cfg, act, dtg, cumg, cumTg, hs, dy, rev, xch=None):
    nch, nct, nlt = cfg.nch, cfg.nct, cfg.nlt
    cmap = _chunk_of_step(cfg, rev)
    in_specs, row = _ssd_specs(cfg, rev)(lambda sp: nch - 1 - sp)

    def lat_row(b, g, sp):
        c = cmap(nch - 1 - sp)
        return b * nlt + jnp.maximum(c - nct, 0)

    def body(x_ref, b_ref, c_ref, dt_ref, cum_ref, cumT_ref, dy_ref, hs_ref,
             dxh_ref, dB_ref, dC_ref, dA_ref, dxx_ref, dh_scr):
        sp = pl.program_id(2)

        @pl.when(sp == 0)
        def _():
            dh_scr[...] = jnp.zeros_like(dh_scr)

        def step(intra):
            tiles = _ssd_step_tiles(dt_ref, cum_ref, cumT_ref, rev)
            dA_t = jnp.zeros((CH, DTW), F32)
            dAT_t = jnp.zeros((DTW, CH), F32)
            dxx_t = jnp.zeros((CH, DTW), F32)
            for gi in range(GP):
                dA_g, dAT_g, dxx_g = group_bwd(gi, intra, tiles, x_ref, b_ref, c_ref, dy_ref, hs_ref,
                                               dxh_ref, dB_ref, dC_ref, dh_scr)
                dA_t = dA_t + dA_g
                dxx_t = dxx_t + dxx_g
                if intra:
                    dAT_t = dAT_t + dAT_g
            dA_ref[...] = dA_t - dAT_t.T if intra else dA_t
            dxx_ref[...] = dxx_t

        isctx = cmap(nch - 1 - sp) < nct

        @pl.when(isctx)
        def _():
            step(False)

        @pl.when(jnp.logical_not(isctx))
        def _():
            step(True)

    def group_bwd(gi, intra, tiles, x_ref, b_ref, c_ref, dy_ref, hs_ref, dxh_ref, dB_ref, dC_ref, dh_scr):
        xsl = slice(gi * HPG * HD, (gi + 1) * HPG * HD)
        nsl = slice(gi * NS, (gi + 1) * NS)
        base = gi * HPG
        xh, Bm, Cm, band, e_exp, ecum, dt_exp, X, G, decs, elast, dec_rows, last = _ssd_common(
            gi, x_ref, b_ref, c_ref, tiles, rev, intra)
        H = hs_ref[gi]
        dHn = dh_scr[gi]
        dHnb = dHn.astype(BF16)
        BdH = _dot_nt(Bm, dHnb)
        dX = e_exp * BdH
        eX = e_exp * X
        lanei = lax.broadcasted_iota(jnp.int32, (CH, DTW), 1)
        dA = jnp.zeros((CH, DTW), F32)
        dAT = None
        pb = lax.broadcasted_iota(jnp.int32, (HPG * HD, NS), 0) // HD
        pl_ = lax.broadcasted_iota(jnp.int32, (HPG * HD, NS), 1)
        E = jnp.where(pb + base == pl_, 1.0, 0.0).astype(BF16)
        if intra:
            dY = dy_ref[:, xsl]
            Hb = H.astype(BF16)
            dYs = ecum * dY
            dYsb = dYs.astype(BF16)
            Ys = ecum * _dot_nt(Cm, Hb)
            dG = jnp.zeros((CH, CH), F32)
            subi = lax.broadcasted_iota(jnp.int32, (DTW, CH), 0)
            dAT = jnp.zeros((DTW, CH), F32)
            Xbd = jnp.concatenate([jnp.where(band == h, X, 0.0).astype(BF16) for h in range(HPG)], axis=0)
            dYbd = jnp.concatenate([jnp.where(band == h, dY, 0.0).astype(BF16) for h in range(HPG)], axis=0)
            dMcat = _dot_nt(dY.astype(BF16), Xbd)
            Ms = []
            for h in range(HPG):
                M = G * decs[h]
                dM = dMcat[:, h * CH:(h + 1) * CH]
                W = dM * M
                dG = dG + dM * decs[h]
                Ms.append(M.astype(BF16))
                dA = dA + jnp.where(lanei == base + h, jnp.sum(W, axis=1, keepdims=True), 0.0)
                dAT = dAT + jnp.where(subi == base + h, jnp.sum(W, axis=0, keepdims=True), 0.0)
            dX = dX + _dot_tn(jnp.concatenate(Ms, axis=0), dYbd)
            dGb = dG.astype(BF16)
            dC_ref[:, nsl] = _dot(dGb, Bm) + _dot(dYsb, Hb)
            dB_ref[:, nsl] = _dot_tn(dGb, Cm) + _dot(eX.astype(BF16), dHnb)
            dh_scr[gi] = dec_rows * dHn + _dot_tn(dYsb, Cm)
            dA = dA + _dot_hilo_r(dY * Ys, E)
        else:
            dC_ref[:, nsl] = jnp.zeros((CH, NS), F32)
            dB_ref[:, nsl] = _dot(eX.astype(BF16), dHnb)
            dh_scr[gi] = dec_rows * dHn
        q = _dot_hilo_r(eX * BdH, E)
        r = jnp.sum(dHn * H, axis=1, keepdims=True)
        lane1 = lax.broadcasted_iota(jnp.int32, (1, DTW), 1)
        hdot = jnp.zeros((1, DTW), F32)
        for h in range(HPG):
            hv = jnp.sum(r[h * HD:(h + 1) * HD, :], axis=0, keepdims=True)
            hdot = hdot + jnp.where(lane1 == base + h, hv, 0.0)
        dllast = jnp.sum(q, axis=0, keepdims=True) + elast * hdot
        rowi = lax.broadcasted_iota(jnp.int32, (CH, DTW), 0)
        dxh_ref[:, xsl] = dX * dt_exp
        return dA - q + jnp.where(rowi == last, dllast, 0.0), dAT, _dot_hilo_r(dX * xh, E)

    small = pl.BlockSpec((None, CH, DTW), lambda b, g, sp: (g, row(b, g, sp), 0))
    return _hosted_call(
        body, xch, name="ssd_bwd_rev" if rev else "ssd_bwd", grid=(cfg.Bn, NG // GP, nch),
        in_specs=in_specs + [
            pl.BlockSpec((CH, GP * HPG * HD), lambda b, g, sp: (lat_row(b, g, sp), g)),
            pl.BlockSpec((None, GP, None, HPG * HD, NS), lambda b, g, sp: (b, g, nch - 1 - sp, 0, 0))],
        out_specs=[pl.BlockSpec((CH, GP * HPG * HD), lambda b, g, sp: (row(b, g, sp), g)),
                   pl.BlockSpec((CH, GP * NS), lambda b, g, sp: (row(b, g, sp), g)),
                   pl.BlockSpec((CH, GP * NS), lambda b, g, sp: (row(b, g, sp), g)),
                   small, small],
        out_shape=[jax.ShapeDtypeStruct((cfg.NT, DI), F32),
                   jax.ShapeDtypeStruct((cfg.NT, NG * NS), F32),
                   jax.ShapeDtypeStruct((cfg.NT, NG * NS), F32),
                   jax.ShapeDtypeStruct((NGB, cfg.NT, DTW), F32),
                   jax.ShapeDtypeStruct((NGB, cfg.NT, DTW), F32)],
        scratch_shapes=[pltpu.VMEM((GP, HPG * HD, NS), F32)],
        compiler_params=_cp(3), args=(act, act, act, dtg, cumg, cumTg, dy, hs))


def _shift_rows(v, s, fill, toward_later, rowi):
    n = v.shape[0]
    if toward_later:
        return jnp.where(rowi >= s, pltpu.roll(v, s, 0), fill)
    return jnp.where(rowi < n - s, pltpu.roll(v, n - s, 0), fill)


def _chunk_scan(a, b, carry, later):
    nt = a.shape[0] // 8
    rowi = lax.broadcasted_iota(jnp.int32, (8, a.shape[1]), 0)
    outs = [None] * nt
    for r in (range(nt) if later else range(nt - 1, -1, -1)):
        av = a[r * 8:(r + 1) * 8]
        bv = b[r * 8:(r + 1) * 8]
        for sh in (1, 2, 4):
            a_p = _shift_rows(av, sh, 1.0, later, rowi)
            b_p = _shift_rows(bv, sh, 0.0, later, rowi)
            bv = av * b_p + bv
            av = av * a_p
        h = bv + av * carry
        outs[r] = h
        carry = h[7:8] if later else h[0:1]
    return jnp.concatenate(outs, axis=0), carry


def _lru_gates(u, wa_ref, wi_ref, ba_ref, bi_ref, lam_ref):
    rs, is_ = [], []
    for k in range(LB):
        uk = u[:, k * LBW:(k + 1) * LBW].astype(BF16)
        rs.append(_dot(uk, wa_ref[k].astype(BF16)))
        is_.append(_dot(uk, wi_ref[k].astype(BF16)))
    r = _sigmoid(jnp.concatenate(rs, axis=1) + ba_ref[...])
    ig = _sigmoid(jnp.concatenate(is_, axis=1) + bi_ref[...])
    sp = _softplus(-lam_ref[...])
    la = -LRU_C * r * sp
    a = jnp.exp(la)
    g = jnp.sqrt((1.0 + a * a) * jnp.tanh(-la))
    return r, ig, sp, la, a, g


def _lru_w_specs(d):
    return [pl.BlockSpec((None, LB, LBW, LBW), lambda b, s: (d, 0, 0, 0)),
            pl.BlockSpec((None, LB, LBW, LBW), lambda b, s: (d, 0, 0, 0)),
            pl.BlockSpec((None, 1, LW), lambda b, s: (d, 0, 0)),
            pl.BlockSpec((None, 1, LW), lambda b, s: (d, 0, 0)),
            pl.BlockSpec((None, 1, LW), lambda b, s: (d, 0, 0))]


def _lru_block_of_step(cfg, rev):
    nbl = cfg.nbl
    if not rev:
        return lambda s: s
    return lambda s: jnp.where(s < 1, 0, 1 + nbl - s)


def _lru_fwd(cfg, act, wa, wi, ba, bi, lam, rev):
    nch, CH = cfg.nbt, cfg.TB
    cmap = _lru_block_of_step(cfg, rev)
    d = 1 if rev else 0
    ucol = (DI + 2 * NG * NS) // LW

    def body(u_ref, wa_ref, wi_ref, ba_ref, bi_ref, lam_ref, h_ref, c_scr):
        s = pl.program_id(1)

        @pl.when(s == 0)
        def _():
            c_scr[...] = jnp.zeros_like(c_scr)

        u = u_ref[...]
        r, ig, sp, la, a, g = _lru_gates(u, wa_ref, wi_ref, ba_ref, bi_ref, lam_ref)
        h, carry = _chunk_scan(a, g * ig * u, c_scr[0:1, :], not rev)
        h_ref[...] = h
        c_scr[0:1, :] = carry

    return pl.pallas_call(
        body, name="lru_fwd_rev" if rev else "lru_fwd", grid=(cfg.Bn, nch),
        in_specs=[pl.BlockSpec((CH, LW), lambda b, s: (b * nch + cmap(s), ucol))] + _lru_w_specs(d),
        out_specs=pl.BlockSpec((CH, LW), lambda b, s: (b * nch + cmap(s), 0)),
        out_shape=jax.ShapeDtypeStruct((cfg.NT, LW), F32),
        scratch_shapes=[pltpu.VMEM((8, LW), F32)],
        compiler_params=_cp(2),
    )(act, wa, wi, ba, bi, lam)


def _lru_bwd(cfg, act, wa, wi, ba, bi, lam, hd, dyl, rev):
    nch, nct, nlt, CH = cfg.nbt, 1, cfg.nbl, cfg.TB
    cmap = _lru_block_of_step(cfg, rev)
    d = 1 if rev else 0
    ucol = (DI + 2 * NG * NS) // LW

    def srow(b, sp):
        return b * nch + cmap(nch - 1 - sp)

    def prev_rows(b, sp):
        s = nch - 1 - sp
        cp = cmap(jnp.maximum(s - 1, 0))
        base = (b * nch + cp) * (CH // 8)
        return base + (0 if rev else CH // 8 - 1)

    def lat_row(b, sp):
        c = cmap(nch - 1 - sp)
        return b * nlt + jnp.maximum(c - nct, 0)

    def body(u_ref, wa_ref, wi_ref, ba_ref, bi_ref, lam_ref, h_ref, hp_ref, dy_ref,
             du_ref, dwa_ref, dwi_ref, vec_ref, c_scr):
        b = pl.program_id(0)
        sp_id = pl.program_id(1)
        s = nch - 1 - sp_id

        @pl.when(sp_id == 0)
        def _():
            c_scr[...] = jnp.zeros_like(c_scr)

        @pl.when(jnp.logical_and(b == 0, sp_id == 0))
        def _():
            dwa_ref[...] = jnp.zeros_like(dwa_ref)
            dwi_ref[...] = jnp.zeros_like(dwi_ref)
            vec_ref[...] = jnp.zeros_like(vec_ref)

        c = cmap(s)
        u = u_ref[...]
        r, ig, spl, la, a, g = _lru_gates(u, wa_ref, wi_ref, ba_ref, bi_ref, lam_ref)
        dh = jnp.where(c < nct, 0.0, dy_ref[...])
        rowi = lax.broadcasted_iota(jnp.int32, (CH, LW), 0)
        lamv, _ = _chunk_scan(_shift_rows(a, 1, 1.0, rev, rowi), dh, c_scr[0:1, :], rev)
        first = CH - 1 if rev else 0
        c_scr[0:1, :] = (a * lamv)[first:first + 1, :]
        hprow = hp_ref[...][(0 if rev else 7):(1 if rev else 8), :]
        hprow = jnp.where(s > 0, hprow, 0.0)
        h_prev = _shift_rows(h_ref[...], 1, hprow, not rev, rowi)
        da = lamv * h_prev
        db = lamv
        iu = ig * u
        dla = da * a - db * iu * (a * a) / g
        dr = dla * (-LRU_C * spl)
        di = db * g * u
        du = db * g * ig
        drp = dr * r * (1.0 - r)
        dip = di * ig * (1.0 - ig)
        dus = []
        for k in range(LB):
            sl = slice(k * LBW, (k + 1) * LBW)
            drk = drp[:, sl].astype(BF16)
            dik = dip[:, sl].astype(BF16)
            uk = u[:, sl].astype(BF16)
            dus.append(_dot_nt(drk, wa_ref[k].astype(BF16)) + _dot_nt(dik, wi_ref[k].astype(BF16)))
            dwa_ref[k] += _dot_tn(uk, drk)
            dwi_ref[k] += _dot_tn(uk, dik)
        du_ref[...] = du + jnp.concatenate(dus, axis=1)
        vec_ref[0:1, :] += jnp.sum(drp, axis=0, keepdims=True)
        vec_ref[1:2, :] += jnp.sum(dip, axis=0, keepdims=True)
        dsp = jnp.sum(dla * (-LRU_C * r), axis=0, keepdims=True)
        vec_ref[2:3, :] += dsp * (-_sigmoid(-lam_ref[...]))

    return pl.pallas_call(
        body, name="lru_bwd_rev" if rev else "lru_bwd", grid=(cfg.Bn, nch),
        in_specs=[pl.BlockSpec((CH, LW), lambda b, sp: (srow(b, sp), ucol))] + _lru_w_specs(d) + [
            pl.BlockSpec((CH, LW), lambda b, sp: (srow(b, sp), 0)),
            pl.BlockSpec((8, LW), lambda b, sp: (prev_rows(b, sp), 0)),
            pl.BlockSpec((CH, LW), lambda b, sp: (lat_row(b, sp), 0))],
        out_specs=[pl.BlockSpec((CH, LW), lambda b, sp: (srow(b, sp), 0)),
                   pl.BlockSpec((LB, LBW, LBW), lambda b, sp: (0, 0, 0)),
                   pl.BlockSpec((LB, LBW, LBW), lambda b, sp: (0, 0, 0)),
                   pl.BlockSpec((8, LW), lambda b, sp: (0, 0))],
        out_shape=[jax.ShapeDtypeStruct((cfg.NT, LW), F32),
                   jax.ShapeDtypeStruct((LB, LBW, LBW), F32),
                   jax.ShapeDtypeStruct((LB, LBW, LBW), F32),
                   jax.ShapeDtypeStruct((8, LW), F32)],
        scratch_shapes=[pltpu.VMEM((8, LW), F32)],
        compiler_params=_cp(2),
    )(act, wa, wi, ba, bi, lam, hd, hd, dyl)


HB = 1024


def _post_ssd_fwd(cfg, y, proj, norm_w):
    TB, nbt, nbl = cfg.TB, cfg.nbt, cfg.nbl
    zc = CONVW // HB

    def body(y_ref, z_ref, w_ref, o_ref):
        u = y_ref[...] * _silu(z_ref[...])
        for gi in range(HB // (DI // NG)):
            sl = slice(gi * 256, (gi + 1) * 256)
            ug = u[:, sl]
            rs = lax.rsqrt(jnp.mean(ug * ug, axis=1, keepdims=True) + RMS_EPS)
            o_ref[:, sl] = (ug * rs * w_ref[:, sl]).astype(BF16)

    def st(b, j, cb):
        return (b * nbt + 1 + j, cb)
    return pl.pallas_call(
        body, name="post_ssd_fwd", grid=(cfg.Bn, nbl, DI // HB),
        in_specs=[pl.BlockSpec((TB, HB), st),
                  pl.BlockSpec((TB, HB), lambda b, j, cb: (b * nbt + 1 + j, zc + cb)),
                  pl.BlockSpec((1, HB), lambda b, j, cb: (0, cb))],
        out_specs=pl.BlockSpec((TB, HB), lambda b, j, cb: (b * nbl + j, cb)),
        out_shape=jax.ShapeDtypeStruct((cfg.N, DI), BF16),
        compiler_params=_cp(3),
    )(y, proj, norm_w)


def _post_ssd_bwd(cfg, dproj, dn, y, act, proj, norm_w):
    TB, nbt, nbl = cfg.TB, cfg.nbt, cfg.nbl
    zc = CONVW // HB

    def body(_, dn_ref, y_ref, xh_ref, z_ref, w_ref, dy_ref, dz_ref, acc_ref):
        b = pl.program_id(1)
        j = pl.program_id(2)

        @pl.when(jnp.logical_and(b == 0, j == 0))
        def _():
            acc_ref[...] = jnp.zeros_like(acc_ref)

        @pl.when(j == 0)
        def _():
            dz_ref[...] = jnp.zeros_like(dz_ref)

        @pl.when(j > 0)
        def _():
            latent(dn_ref, y_ref, xh_ref, z_ref, w_ref, dy_ref, dz_ref, acc_ref)

    def latent(dn_ref, y_ref, xh_ref, z_ref, w_ref, dy_ref, dz_ref, acc_ref):
        xh = xh_ref[...]
        z = z_ref[...]
        y = y_ref[...]
        sz = _silu(z)
        u = y * sz
        dout = dn_ref[...]
        for gi in range(HB // (DI // NG)):
            sl = slice(gi * 256, (gi + 1) * 256)
            ug0 = u[:, sl]
            rs = lax.rsqrt(jnp.mean(ug0 * ug0, axis=1, keepdims=True) + RMS_EPS)
            ug = ug0 * rs
            do = dout[:, sl]
            acc_ref[0:1, sl] += jnp.sum(do * ug, axis=0, keepdims=True)
            dug = do * w_ref[:, sl]
            du = rs * (dug - ug * jnp.mean(dug * ug, axis=1, keepdims=True))
            dy = du * sz[:, sl]
            dy_ref[:, sl] = dy
            dz_ref[:, sl] = (du * y[:, sl] * _silu_grad(z[:, sl])).astype(BF16)
            acc_ref[1:2, sl] += jnp.sum(dy * xh[:, sl], axis=0, keepdims=True)

    def st(cb, b, j):
        return (b * nbt + j, cb)

    def la(cb, b, j):
        return (b * nbl + jnp.maximum(j - 1, 0), cb)
    return pl.pallas_call(
        body, name="post_ssd_bwd", grid=(DI // HB, cfg.Bn, nbt),
        in_specs=[_ANY, pl.BlockSpec((TB, HB), la), pl.BlockSpec((TB, HB), st), pl.BlockSpec((TB, HB), st),
                  pl.BlockSpec((TB, HB), lambda cb, b, j: (b * nbt + j, zc + cb)),
                  pl.BlockSpec((1, HB), lambda cb, b, j: (0, cb))],
        out_specs=[pl.BlockSpec((TB, HB), la),
                   pl.BlockSpec((TB, HB), lambda cb, b, j: (b * nbt + j, zc + cb)),
                   pl.BlockSpec((8, HB), lambda cb, b, j: (0, cb))],
        out_shape=[jax.ShapeDtypeStruct((cfg.N, DI), F32), jax.ShapeDtypeStruct((cfg.NT, PM), BF16),
                   jax.ShapeDtypeStruct((8, DI), F32)],
        input_output_aliases={0: 1},
        compiler_params=_cp(3),
    )(dproj, dn, y, act, proj, norm_w)


def _post_lru_fwd(cfg, hf, hb, proj):
    TB, nbt, nbl = cfg.TB, cfg.nbt, cfg.nbl
    gc = (CONVW + DI) // HB

    def body(hf_ref, hb_ref, g_ref, o_ref):
        o_ref[...] = ((hf_ref[...] + hb_ref[...]) * _gelu(g_ref[...])).astype(BF16)

    st = pl.BlockSpec((TB, HB), lambda b, j: (b * nbt + 1 + j, 0))
    return pl.pallas_call(
        body, name="post_lru_fwd", grid=(cfg.Bn, nbl),
        in_specs=[st, st, pl.BlockSpec((TB, HB), lambda b, j: (b * nbt + 1 + j, gc))],
        out_specs=pl.BlockSpec((TB, HB), lambda b, j: (b * nbl + j, 0)),
        out_shape=jax.ShapeDtypeStruct((cfg.N, LW), BF16),
        compiler_params=_cp(2),
    )(hf, hb, proj)


def _post_lru_bwd(cfg, dproj, dv, hf, hb, proj):
    TB, nbt, nbl = cfg.TB, cfg.nbt, cfg.nbl
    gc = (CONVW + DI) // HB

    def body(_, dv_ref, hf_ref, hb_ref, g_ref, dy_ref, dg_ref):
        j = pl.program_id(1)

        @pl.when(j == 0)
        def _():
            dg_ref[...] = jnp.zeros_like(dg_ref)

        @pl.when(j > 0)
        def _():
            gt = g_ref[...]
            dvv = dv_ref[...]
            dy_ref[...] = dvv * _gelu(gt)
            dg_ref[...] = (dvv * (hf_ref[...] + hb_ref[...]) * _gelu_grad(gt)).astype(BF16)

    st = pl.BlockSpec((TB, HB), lambda b, j: (b * nbt + j, 0))
    la = pl.BlockSpec((TB, HB), lambda b, j: (b * nbl + jnp.maximum(j - 1, 0), 0))
    gcol = pl.BlockSpec((TB, HB), lambda b, j: (b * nbt + j, gc))
    return pl.pallas_call(
        body, name="post_lru_bwd", grid=(cfg.Bn, nbt),
        in_specs=[_ANY, la, st, st, gcol],
        out_specs=[la, gcol],
        out_shape=[jax.ShapeDtypeStruct((cfg.N, LW), F32), jax.ShapeDtypeStruct((cfg.NT, PM), BF16)],
        input_output_aliases={0: 1},
        compiler_params=_cp(2),
    )(dproj, dv, hf, hb, proj)


def _merge_fwd(cfg, proj, b_gate, br_ssd, br_lru):
    TB, nbt, nbl = cfg.TB, cfg.nbt, cfg.nbl
    mc = (CONVW + DI + LW) // HB

    def body(ms_ref, ml_ref, bg_ref, bs_ref, bl_ref, o_ref):
        gs = _sigmoid(ms_ref[...] + bg_ref[:, :D])
        gl = _sigmoid(ml_ref[...] + bg_ref[:, D:])
        o_ref[...] = (gs * bs_ref[...] + gl * bl_ref[...]).astype(BF16)

    la = pl.BlockSpec((TB, D), lambda b, j: (b * nbl + j, 0))
    return pl.pallas_call(
        body, name="merge_fwd", grid=(cfg.Bn, nbl),
        in_specs=[pl.BlockSpec((TB, HB), lambda b, j: (b * nbt + 1 + j, mc)),
                  pl.BlockSpec((TB, HB), lambda b, j: (b * nbt + 1 + j, mc + 1)),
                  pl.BlockSpec((1, 2 * D), lambda b, j: (0, 0)), la, la],
        out_specs=la,
        out_shape=jax.ShapeDtypeStruct((cfg.N, D), BF16),
        compiler_params=_cp(2),
    )(proj, proj, b_gate, br_ssd, br_lru)


def _merge_bwd(cfg, dmix, proj, b_gate, br_ssd, br_lru):
    TB, nbt, nbl = cfg.TB, cfg.nbt, cfg.nbl
    mc = (CONVW + DI + LW) // HB

    def body(dm_ref, ms_ref, ml_ref, bg_ref, bs_ref, bl_ref, ds_ref, dl_ref, dmg_ref, acc_ref):
        b = pl.program_id(0)
        j = pl.program_id(1)

        @pl.when(jnp.logical_and(b == 0, j == 0))
        def _():
            acc_ref[...] = jnp.zeros_like(acc_ref)

        @pl.when(j == 0)
        def _():
            dmg_ref[...] = jnp.zeros_like(dmg_ref)

        @pl.when(j > 0)
        def _():
            latent(dm_ref, ms_ref, ml_ref, bg_ref, bs_ref, bl_ref, ds_ref, dl_ref, dmg_ref, acc_ref)

    def latent(dm_ref, ms_ref, ml_ref, bg_ref, bs_ref, bl_ref, ds_ref, dl_ref, dmg_ref, acc_ref):
        dm = dm_ref[...]
        gs = _sigmoid(ms_ref[...] + bg_ref[:, :D])
        gl = _sigmoid(ml_ref[...] + bg_ref[:, D:])
        ds_ref[...] = (dm * gs).astype(BF16)
        dl_ref[...] = (dm * gl).astype(BF16)
        dps = dm * bs_ref[...] * gs * (1.0 - gs)
        dpl = dm * bl_ref[...] * gl * (1.0 - gl)
        dmg_ref[:, :D] = dps.astype(BF16)
        dmg_ref[:, D:] = dpl.astype(BF16)
        acc_ref[0:1, :D] += jnp.sum(dps, axis=0, keepdims=True)
        acc_ref[0:1, D:] += jnp.sum(dpl, axis=0, keepdims=True)

    la = pl.BlockSpec((TB, D), lambda b, j: (b * nbl + jnp.maximum(j - 1, 0), 0))
    return pl.pallas_call(
        body, name="merge_bwd", grid=(cfg.Bn, nbt),
        in_specs=[la, pl.BlockSpec((TB, HB), lambda b, j: (b * nbt + j, mc)),
                  pl.BlockSpec((TB, HB), lambda b, j: (b * nbt + j, mc + 1)),
                  pl.BlockSpec((1, 2 * D), lambda b, j: (0, 0)), la, la],
        out_specs=[la, la, pl.BlockSpec((TB, 2 * D), lambda b, j: (b * nbt + j, mc // 2)),
                   pl.BlockSpec((8, 2 * D), lambda b, j: (0, 0))],
        out_shape=[jax.ShapeDtypeStruct((cfg.N, D), BF16), jax.ShapeDtypeStruct((cfg.N, D), BF16),
                   jax.ShapeDtypeStruct((cfg.NT, PM), BF16), jax.ShapeDtypeStruct((8, 2 * D), F32)],
        compiler_params=_cp(2),
    )(dmix, proj, proj, b_gate, br_ssd, br_lru)


def _resid1_fwd(cfg, xs, x_mix, gate1, shift2, scale2, ln1_g, ln1_b):
    TB, nbt, nbl = cfg.TB, cfg.nbt, cfg.nbl

    def body(x_ref, xm_ref, g1_ref, sh_ref, sc_ref, lg_ref, lb_ref, x1_ref, h2_ref):
        r1 = ALPHA * x_ref[...] + g1_ref[...] * xm_ref[...]
        xh, _ = _ln(r1)
        x1 = xh * lg_ref[...] + lb_ref[...]
        x1_ref[...] = x1
        xh2, _ = _ln(x1)
        h2_ref[...] = (xh2 * (1.0 + sc_ref[...]) + sh_ref[...]).astype(BF16)

    la = pl.BlockSpec((TB, D), lambda b, j: (b * nbl + j, 0))
    ex = pl.BlockSpec((None, 1, D), lambda b, j: (b, 0, 0))
    vec = pl.BlockSpec((1, D), lambda b, j: (0, 0))
    return pl.pallas_call(
        body, name="resid1_fwd", grid=(cfg.Bn, nbl),
        in_specs=[pl.BlockSpec((TB, D), lambda b, j: (b * nbt + 1 + j, 0)), la, ex, ex, ex, vec, vec],
        out_specs=[la, la],
        out_shape=[jax.ShapeDtypeStruct((cfg.N, D), F32), jax.ShapeDtypeStruct((cfg.N, D), BF16)],
        compiler_params=_cp(2),
    )(xs, x_mix, gate1, shift2, scale2, ln1_g, ln1_b)


def _resid1_bwd(cfg, dh2, dx1p, x1, xs, x_mix, gate1, scale2, ln1_g):
    TB, nbt, nbl = cfg.TB, cfg.nbt, cfg.nbl

    def body(dh2_ref, dx1p_ref, x1_ref, x_ref, xm_ref, g1_ref, sc_ref, lg_ref,
             dxm_ref, dxp_ref, ex_ref, gl_ref):
        b = pl.program_id(0)
        j = pl.program_id(1)

        @pl.when(j == 0)
        def _():
            ex_ref[...] = jnp.zeros_like(ex_ref)

        @pl.when(jnp.logical_and(b == 0, j == 0))
        def _():
            gl_ref[...] = jnp.zeros_like(gl_ref)

        dh2 = dh2_ref[...]
        xh2, rs2 = _ln(x1_ref[...])
        ex_ref[0:1, :] += jnp.sum(dh2, axis=0, keepdims=True)
        ex_ref[1:2, :] += jnp.sum(dh2 * xh2, axis=0, keepdims=True)
        dx1 = dx1p_ref[...] + _ln_bwd(dh2 * (1.0 + sc_ref[...]), xh2, rs2)
        xm = xm_ref[...]
        g1 = g1_ref[...]
        r1 = ALPHA * x_ref[...] + g1 * xm
        xh1, rs1 = _ln(r1)
        gl_ref[0:1, :] += jnp.sum(dx1 * xh1, axis=0, keepdims=True)
        gl_ref[1:2, :] += jnp.sum(dx1, axis=0, keepdims=True)
        dr1 = _ln_bwd(dx1 * lg_ref[...], xh1, rs1)
        ex_ref[2:3, :] += jnp.sum(dr1 * xm, axis=0, keepdims=True)
        dxm_ref[...] = (dr1 * g1).astype(BF16)
        dxp_ref[...] = ALPHA * dr1

    la = pl.BlockSpec((TB, D), lambda b, j: (b * nbl + j, 0))
    ex = pl.BlockSpec((None, 1, D), lambda b, j: (b, 0, 0))
    vec = pl.BlockSpec((1, D), lambda b, j: (0, 0))
    return pl.pallas_call(
        body, name="resid1_bwd", grid=(cfg.Bn, nbl),
        in_specs=[la, la, la, pl.BlockSpec((TB, D), lambda b, j: (b * nbt + 1 + j, 0)), la, ex, ex, vec],
        out_specs=[la, la, pl.BlockSpec((None, 8, D), lambda b, j: (b, 0, 0)),
                   pl.BlockSpec((8, D), lambda b, j: (0, 0))],
        out_shape=[jax.ShapeDtypeStruct((cfg.N, D), BF16), jax.ShapeDtypeStruct((cfg.N, D), F32),
                   jax.ShapeDtypeStruct((cfg.Bn, 8, D), F32), jax.ShapeDtypeStruct((8, D), F32)],
        compiler_params=_cp(2),
    )(dh2, dx1p, x1, xs, x_mix, gate1, scale2, ln1_g)


def _final_fwd_bwd(cfg, x1, mlp, b2, gate2, ln2_g, ln2_b, target):
    TB, nbl = cfg.TB, cfg.nbl

    def body(x1_ref, m_ref, b2_ref, g2_ref, lg_ref, lb_ref, t_ref, dm_ref, dx_ref, ex_ref, gl_ref):
        b = pl.program_id(0)
        j = pl.program_id(1)

        @pl.when(j == 0)
        def _():
            ex_ref[...] = jnp.zeros_like(ex_ref)

        @pl.when(jnp.logical_and(b == 0, j == 0))
        def _():
            gl_ref[...] = jnp.zeros_like(gl_ref)

        mv = m_ref[...] + b2_ref[...]
        g2 = g2_ref[...]
        r2 = ALPHA * x1_ref[...] + g2 * mv
        xh, rs = _ln(r2)
        lg = lg_ref[...]
        x2 = xh * lg + lb_ref[...]
        err = x2 - t_ref[...]
        ls = jnp.sum(jnp.sum(err * err, axis=1, keepdims=True), axis=0, keepdims=True) * (0.5 / D)
        gl_ref[3:4, :] += ls
        dx2 = err * (1.0 / D)
        gl_ref[0:1, :] += jnp.sum(dx2 * xh, axis=0, keepdims=True)
        gl_ref[1:2, :] += jnp.sum(dx2, axis=0, keepdims=True)
        dr2 = _ln_bwd(dx2 * lg, xh, rs)
        ex_ref[0:1, :] += jnp.sum(dr2 * mv, axis=0, keepdims=True)
        dmv = dr2 * g2
        gl_ref[2:3, :] += jnp.sum(dmv, axis=0, keepdims=True)
        dm_ref[...] = dmv.astype(BF16)
        dx_ref[...] = ALPHA * dr2

    la = pl.BlockSpec((TB, D), lambda b, j: (b * nbl + j, 0))
    ex = pl.BlockSpec((None, 1, D), lambda b, j: (b, 0, 0))
    vec = pl.BlockSpec((1, D), lambda b, j: (0, 0))
    return pl.pallas_call(
        body, name="final_fwd_bwd", grid=(cfg.Bn, nbl),
        in_specs=[la, la, vec, ex, vec, vec, la],
        out_specs=[la, la, pl.BlockSpec((None, 8, D), lambda b, j: (b, 0, 0)),
                   pl.BlockSpec((8, D), lambda b, j: (0, 0))],
        out_shape=[jax.ShapeDtypeStruct((cfg.N, D), BF16), jax.ShapeDtypeStruct((cfg.N, D), F32),
                   jax.ShapeDtypeStruct((cfg.Bn, 8, D), F32), jax.ShapeDtypeStruct((8, D), F32)],
        compiler_params=_cp(2),
    )(x1, mlp, b2, gate2, ln2_g, ln2_b, target)


def _ln_mod_bwd(cfg, dh_a, dh_b, xs, scale_tab, dxp):
    TB, nbt, nbl = cfg.TB, cfg.nbt, cfg.nbl

    def body(da_ref, db_ref, x_ref, sc_ref, dxp_ref, gx_ref, acc_ref):
        j = pl.program_id(1)

        @pl.when(j <= 1)
        def _():
            acc_ref[...] = jnp.zeros_like(acc_ref)

        dh = da_ref[...] + db_ref[...]
        xhat, rs = _ln(x_ref[...])
        acc_ref[0:1, :] += jnp.sum(dh, axis=0, keepdims=True)
        acc_ref[1:2, :] += jnp.sum(dh * xhat, axis=0, keepdims=True)
        gx_ref[...] = dxp_ref[...] + _ln_bwd(dh * (1.0 + sc_ref[...]), xhat, rs)

    st = pl.BlockSpec((TB, D), lambda b, j: (b * nbt + j, 0))
    la = pl.BlockSpec((TB, D), lambda b, j: (b * nbl + jnp.maximum(j - 1, 0), 0))
    return pl.pallas_call(
        body, name="ln_mod_bwd", grid=(cfg.Bn, nbt),
        in_specs=[st, st, st,
                  pl.BlockSpec((None, 1, D), lambda b, j: (2 * b + jnp.minimum(j, 1), 0, 0)), la],
        out_specs=[la, pl.BlockSpec((None, 8, D), lambda b, j: (2 * b + jnp.minimum(j, 1), 0, 0))],
        out_shape=[jax.ShapeDtypeStruct((cfg.N, D), F32), jax.ShapeDtypeStruct((2 * cfg.Bn, 8, D), F32)],
        compiler_params=_cp(2),
    )(dh_a, dh_b, xs, scale_tab, dxp)


def _perm_w_in(w_in):
    w_main = jnp.concatenate([w_in[:, 0:3072], w_in[:, 4160:5184], w_in[:, 3136:4160], w_in[:, 5184:10304]], axis=1)
    w_dt = jnp.pad(w_in[:, 3072:3136], ((0, 0), (0, DTW - 2 * NH)))
    return w_main, w_dt


def _unperm_w_in(dw_main, dw_dt):
    return jnp.concatenate([dw_main[:, 0:3072], dw_dt[:, :2 * NH], dw_main[:, 4096:5120],
                            dw_main[:, 3072:4096], dw_main[:, 5120:]], axis=1)


def _unpack_rest(rest_all):
    out, off = {}, 0
    for n, shp, axis in _BIG[1:]:
        shard_shape = (shp[0] // NDEV, shp[1]) if axis == 0 else (shp[0], shp[1] // NDEV)
        r = math.prod(shard_shape) // 1024
        out[n] = _from_slots(rest_all[:, off:off + r, :].reshape((NDEV,) + shard_shape), axis)
        off += r
    return out


def _local_step(cfg, x, ctx, target, m, mc, W, rest_payload):
    Bn, T, Tc = cfg.Bn, cfg.T, cfg.Tc
    NT, N = cfg.NT, cfg.N
    xs = jnp.concatenate([ctx, x], axis=1).reshape(NT, D)
    mch = [m[:, i * D:(i + 1) * D] for i in range(NMOD)]
    ctx_sh = jnp.broadcast_to(mc[None, :D], (Bn, D))
    ctx_sc = jnp.broadcast_to(mc[None, D:], (Bn, D))
    shift_tab = jnp.stack([ctx_sh, mch[0]], axis=1).reshape(2 * Bn, 1, D)
    scale_tab = jnp.stack([ctx_sc, mch[1]], axis=1).reshape(2 * Bn, 1, D)
    gate1 = mch[2].reshape(Bn, 1, D)
    shift2 = mch[3].reshape(Bn, 1, D)
    scale2 = mch[4].reshape(Bn, 1, D)
    gate2 = mch[5].reshape(Bn, 1, D)

    conv_w = jnp.concatenate([W["ssd_conv_w"], W["lru_conv_w"]], axis=1)
    conv_b = jnp.concatenate([W["ssd_conv_b"], W["lru_conv_b"]], axis=1)
    dt_bias = jnp.pad(W["ssd_dt_bias"].reshape(1, 2 * NH), ((0, 0), (0, DTW - 2 * NH)))
    a_log = jnp.pad(W["ssd_a_log"].reshape(1, 2 * NH), ((0, 0), (0, DTW - 2 * NH)))
    dvec = jnp.repeat(W["ssd_d"].reshape(NH), HD).reshape(1, DI)
    lba = W["lru_ba"].reshape(2, 1, LW)
    lbi = W["lru_bi"].reshape(2, 1, LW)
    llam = W["lru_lambda"].reshape(2, 1, LW)

    h = _ln_mod_fwd(cfg, xs, shift_tab, scale_tab)
    proj, rest_all = _mm(h, W["w_main"], "nn", "mm_proj", tm=1024, tn=1024, tk=1024, xch=(rest_payload, True))
    W = dict(W, **_unpack_rest(rest_all))
    dt_raw = _mm(h, W["w_dt"], "nn", "mm_dt", tm=512, tn=DTW, tk=1024)
    dt, dtg, cumg, cumTg = _dt_fwd(cfg, dt_raw, dt_bias, a_log)
    act, sgrad = _conv_fwd(cfg, proj, conv_w, conv_b)
    y_f, hs_f = _ssd_fwd(cfg, act, dtg, cumg, cumTg, False)
    y, hs_b = _ssd_fwd(cfg, act, dtg, cumg, cumTg, True, y_other=y_f, dvec=dvec)
    hss = [hs_f, hs_b]
    hls = [_lru_fwd(cfg, act, W["lru_wa"], W["lru_wi"], lba, lbi, llam, rev) for rev in (False, True)]
    nssd = _post_ssd_fwd(cfg, y, proj, W["ssd_norm_w"])
    vlru = _post_lru_fwd(cfg, hls[0], hls[1], proj)
    br_ssd = _mm(nssd, W["w_br_ssd"], "nn", "mm_br_ssd", tm=1024, tn=1024, tk=1024)
    br_lru = _mm(vlru, W["w_br_lru"], "nn", "mm_br_lru", tm=1024, tn=1024, tk=1024)
    mix = _merge_fwd(cfg, proj, W["b_gate"], br_ssd, br_lru)
    x_mix = _mm(mix, W["w_out"], "nn", "mm_out", tm=1024, tn=1024, tk=1024)
    x1, h2 = _resid1_fwd(cfg, xs, x_mix, gate1, shift2, scale2, W["ln1_g"], W["ln1_b"])
    a1, actm = _mm_mlp1(h2, W["w_mlp1"], W["b_mlp1"])
    mlp = _mm(actm, W["w_mlp2"], "nn", "mm_mlp2", tm=1024, tn=1024, tk=1024)
    dmlp, dx1p, ex2, gl2 = _final_fwd_bwd(cfg, x1, mlp, W["b_mlp2"], gate2, W["ln2_g"], W["ln2_b"],
                                          target.reshape(N, D))

    g = {}
    g["ln2_g"], g["ln2_b"], g["b_mlp2"] = gl2[0:1], gl2[1:2], gl2[2:3]
    loss_partial = gl2[3, 0]
    gw = {}
    gw["w_mlp2"] = _mm(actm, dmlp, "tn", "mm_dw_mlp2", BF16, tm=1024, tn=1024, tk=512)
    da1, accb1 = _mm_dact(dmlp, W["w_mlp2"], a1)
    g["b_mlp1"] = accb1[0:1]
    dh2 = _mm(da1, W["w_mlp1"], "nt", "mm_dh2", tm=1024, tn=1024, tk=1024)
    gw["w_mlp1"] = _mm(h2, da1, "tn", "mm_dw_mlp1", BF16, tm=1024, tn=1024, tk=512)
    dx_mix, dxp, ex1, gl1 = _resid1_bwd(cfg, dh2, dx1p, x1, xs, x_mix, gate1, scale2, W["ln1_g"])
    g["ln1_g"], g["ln1_b"] = gl1[0:1], gl1[1:2]
    dmix = _mm(dx_mix, W["w_out"], "nt", "mm_dmix", tm=1024, tn=1024, tk=1024)
    gw["w_out"] = _mm(mix, dx_mix, "tn", "mm_dw_out", BF16, tm=1024, tn=1024, tk=512)
    dbs, dbl, dproj, accg = _merge_bwd(cfg, dmix, proj, W["b_gate"], br_ssd, br_lru)
    g["b_gate"] = accg[0:1]
    dnssd = _mm(dbs, W["w_br_ssd"], "nt", "mm_dnssd", tm=1024, tn=1024, tk=1024)
    gw["w_br_ssd"] = _mm(nssd, dbs, "tn", "mm_dw_br_ssd", BF16, tm=1024, tn=1024, tk=512)
    dvlru = _mm(dbl, W["w_br_lru"], "nt", "mm_dvlru", tm=1024, tn=1024, tk=1024)
    gw["w_br_lru"] = _mm(vlru, dbl, "tn", "mm_dw_br_lru", BF16, tm=1024, tn=1024, tk=512)
    dy, dproj, accs = _post_ssd_bwd(cfg, dproj, dnssd, y, act, proj, W["ssd_norm_w"])
    g["ssd_norm_w"] = accs[0:1]
    dD_cols = accs[1:2]
    dyl, dproj = _post_lru_bwd(cfg, dproj, dvlru, hls[0], hls[1], proj)

    rest_slots = jnp.concatenate([_to_slots(gw[n], axis).reshape(NDEV, -1, 1024) for n, _, axis in _BIG[1:]], axis=1)
    xres = {}
    dxh, dBs, dCs, dAs, dxxs, dus = [], [], [], [], [], []
    dwas, dwis, lvecs = [], [], []
    for i, rev in enumerate((False, True)):
        if i == 0:
            o, xres["rs_rest"] = _ssd_bwd(cfg, act, dtg, cumg, cumTg, hss[i], dy, rev, xch=(rest_slots, False))
        else:
            o = _ssd_bwd(cfg, act, dtg, cumg, cumTg, hss[i], dy, rev)
        dxh.append(o[0]); dBs.append(o[1]); dCs.append(o[2]); dAs.append(o[3]); dxxs.append(o[4])
        du, dwa, dwi, lv = _lru_bwd(cfg, act, W["lru_wa"], W["lru_wi"], lba, lbi, llam, hls[i], dyl, rev)
        dus.append(du); dwas.append(dwa); dwis.append(dwi); lvecs.append(lv)
    lru_payload = jnp.stack([jnp.stack(dwas), jnp.stack(dwis)]).reshape(-1, 1024)
    g["lru_ba"] = jnp.stack([lvecs[0][0], lvecs[1][0]])
    g["lru_bi"] = jnp.stack([lvecs[0][1], lvecs[1][1]])
    g["lru_lambda"] = jnp.stack([lvecs[0][2], lvecs[1][2]])

    ddt_raw, accdt = _dt_bwd(cfg, dAs, dxxs, dt_raw, dt, dt_bias, a_log)
    g["ssd_a_log"] = accdt[0, :2 * NH].reshape(2, NH)
    g["ssd_dt_bias"] = accdt[1, :2 * NH].reshape(2, NH)

    (dproj, accx), xres["ag_lru"] = _conv_bwd(cfg, "conv_bwd_x", dproj, proj, conv_w, sgrad, [dxh[0], dxh[1]], 0, DI,
                                              skip=(dy, dvec), xch=(lru_payload, True))
    dproj, accB = _conv_bwd(cfg, "conv_bwd_b", dproj, proj, conv_w, sgrad, [dBs[0], dBs[1]], DI, NG * NS)
    dproj, accC = _conv_bwd(cfg, "conv_bwd_c", dproj, proj, conv_w, sgrad, [dCs[0], dCs[1]], DI + NG * NS, NG * NS)
    dproj, accl = _conv_bwd(cfg, "conv_bwd_lru", dproj, proj, conv_w, sgrad, [dus[0], dus[1]], DI + 2 * NG * NS, LW)
    accssd = jnp.concatenate([accx, accB, accC], axis=1)
    g["ssd_conv_w"], g["ssd_conv_b"] = accssd[0:4], accssd[4:5]
    g["lru_conv_w"], g["lru_conv_b"] = accl[0:4], accl[4:5]
    dw_main = _mm(h, dproj, "tn", "mm_dw_main", BF16, tm=1024, tn=2048, tk=512)
    dw_dt = _mm(h, ddt_raw, "tn", "mm_dw_dt", BF16, tm=1024, tn=DTW, tk=512)
    w_in_slots = _to_slots(_unperm_w_in(dw_main, dw_dt), 1)
    dh_a, xres["rs_w_in"] = _mm(dproj, W["w_main"], "nt", "mm_dh_main", tm=1024, tn=1024, tk=1024,
                                xch=(w_in_slots, False))
    dh_b = _mm(ddt_raw, W["w_dt"], "nt", "mm_dh_dt", tm=512, tn=1024, tk=DTW)
    grad_x, acct = _ln_mod_bwd(cfg, dh_a, dh_b, xs, scale_tab, dxp)
    acct = acct.reshape(Bn, 2, 8, D)
    dm = jnp.concatenate([acct[:, 1, 0], acct[:, 1, 1], ex1[:, 2], ex1[:, 0], ex1[:, 1], ex2[:, 0]], axis=1)
    dmc = jnp.concatenate([acct[:, 0, 0], acct[:, 0, 1]], axis=1)
    g["ssd_d_cols"] = dD_cols
    return loss_partial, grad_x.reshape(Bn, T, D), g, dm, dmc, xres


MESH = pl.DeviceIdType.MESH
_HBM = pl.BlockSpec(memory_space=pltpu.HBM)


def _me():
    return 4 * lax.axis_index("x") + 2 * lax.axis_index("y") + lax.axis_index("c")


def _peer(k):
    px = (lax.axis_index("x") + ((k >> 2) & 1)) % 2
    py = (lax.axis_index("y") + ((k >> 1) & 1)) % 2
    pc = (lax.axis_index("c") + (k & 1)) % 2
    return (px, py, pc), 4 * px + 2 * py + pc


def _xchg_copies(x_ref, o_ref, send_sems, recv_sems, loc_sem, gather):
    me = _me()
    src_me = x_ref if gather else x_ref.at[me]
    loc = pltpu.make_async_copy(src_me, o_ref.at[me], loc_sem)
    sends, recvs = [], []
    for k in range(1, NDEV):
        peer, pid = _peer(k)
        sends.append(pltpu.make_async_remote_copy(
            src_ref=x_ref if gather else x_ref.at[pid], dst_ref=o_ref.at[me],
            send_sem=send_sems.at[k - 1], recv_sem=recv_sems.at[k - 1],
            device_id=peer, device_id_type=MESH))
        recvs.append(pltpu.make_async_remote_copy(
            src_ref=src_me, dst_ref=o_ref.at[pid],
            send_sem=send_sems.at[k - 1], recv_sem=recv_sems.at[k - 1],
            device_id=peer, device_id_type=MESH))
    return loc, sends, recvs


def _xchg_start(*refs, gather):
    loc, sends, _ = _xchg_copies(*refs, gather)
    loc.start()
    for cp in sends:
        cp.start()


def _xchg_wait(*refs, gather):
    loc, sends, recvs = _xchg_copies(*refs, gather)
    for cp in recvs:
        cp.wait_recv()
    for cp in sends:
        cp.wait_send()
    loc.wait()


_XCHG_SCRATCH = [pltpu.SemaphoreType.DMA((NDEV - 1,)), pltpu.SemaphoreType.DMA((NDEV - 1,)), pltpu.SemaphoreType.DMA]


def _xchg_out_shape(x, gather):
    return jax.ShapeDtypeStruct((NDEV,) + tuple(x.shape if gather else x.shape[1:]), x.dtype)


def _exchange(x, name, gather):
    def body(x_ref, o_ref, send_sems, recv_sems, loc_sem):
        _xchg_start(x_ref, o_ref, send_sems, recv_sems, loc_sem, gather=gather)
        _xchg_wait(x_ref, o_ref, send_sems, recv_sems, loc_sem, gather=gather)

    return pl.pallas_call(
        body, name=name, out_shape=_xchg_out_shape(x, gather),
        in_specs=[_HBM], out_specs=_HBM, scratch_shapes=_XCHG_SCRATCH,
    )(x)


def _gather_two_level(x, name):
    def body(x_ref, o_ref, send_sems, recv_sems, loc_sem):
        mx, my, mc = lax.axis_index("x"), lax.axis_index("y"), lax.axis_index("c")
        me, sibling = (mx, my, mc), (mx, my, 1 - mc)
        chips = [(1 - mx, my), (mx, 1 - my), (1 - mx, 1 - my)]

        def slot(px, py, pc):
            return o_ref.at[4 * px + 2 * py + pc]

        def copy(k, block, to, src=None):
            return pltpu.make_async_remote_copy(
                src_ref=slot(*block) if src is None else src, dst_ref=slot(*block),
                send_sem=send_sems.at[k], recv_sem=recv_sems.at[k], device_id=to, device_id_type=MESH)

        mine = pltpu.make_async_copy(x_ref, slot(*me), loc_sem)
        mine.start()
        first = [copy(0, me, sibling, src=x_ref)]
        first += [copy(1 + j, me, (*chip, mc), src=x_ref) for j, chip in enumerate(chips)]
        for cp in first:
            cp.start()
        passed = [copy(4 + j, (*chip, mc), sibling) for j, chip in enumerate(chips)]
        for j, chip in enumerate(chips):
            copy(1 + j, (*chip, mc), me).wait_recv()
            passed[j].start()
        copy(0, sibling, me).wait_recv()
        for j, chip in enumerate(chips):
            copy(4 + j, (*chip, 1 - mc), me).wait_recv()
        for cp in first + passed:
            cp.wait_send()
        mine.wait()

    return pl.pallas_call(
        body, name=name, out_shape=_xchg_out_shape(x, True),
        in_specs=[_HBM], out_specs=_HBM, scratch_shapes=_XCHG_SCRATCH,
    )(x)


def _hosted_call(body, xch, *, name, grid, in_specs, out_specs, out_shape, scratch_shapes, compiler_params, args,
                 aliases=None):
    aliases = aliases or {}
    if xch is None:
        return pl.pallas_call(body, name=name, grid=grid, in_specs=in_specs, out_specs=out_specs,
                              out_shape=out_shape, scratch_shapes=scratch_shapes, input_output_aliases=aliases,
                              compiler_params=compiler_params)(*args)
    xv, gather = xch
    n_in, n_out, n_scr = len(in_specs), len(out_specs), len(scratch_shapes)

    def wrapped(*refs):
        ins = refs[:n_in]
        x_ref = refs[n_in]
        outs = refs[n_in + 1:n_in + 1 + n_out]
        o_ref = refs[n_in + 1 + n_out]
        scr = refs[n_in + 2 + n_out:]
        own, sems = scr[:n_scr], scr[n_scr:]
        first = functools.reduce(jnp.logical_and, [pl.program_id(a) == 0 for a in range(len(grid))])
        last = functools.reduce(jnp.logical_and, [pl.program_id(a) == grid[a] - 1 for a in range(len(grid))])

        @pl.when(first)
        def _():
            _xchg_start(x_ref, o_ref, *sems, gather=gather)

        body(*ins, *outs, *own)

        @pl.when(last)
        def _():
            _xchg_wait(x_ref, o_ref, *sems, gather=gather)

    res = pl.pallas_call(
        wrapped, name=name, grid=grid, in_specs=list(in_specs) + [_HBM], out_specs=list(out_specs) + [_HBM],
        out_shape=list(out_shape) + [_xchg_out_shape(xv, gather)],
        scratch_shapes=list(scratch_shapes) + _XCHG_SCRATCH, input_output_aliases=aliases,
        compiler_params=compiler_params,
    )(*args, xv)
    return list(res[:n_out]), res[n_out]


def _row_tile(R, cap, mult=8):
    best = mult
    t = mult
    while t <= min(R, cap):
        if R % t == 0:
            best = t
        t += mult
    assert R % best == 0, R
    return best


def _sum_slots(x, name):
    _, R, C = x.shape
    tr = _row_tile(R, 256, 16 if x.dtype == BF16 else 8)

    def body(x_ref, o_ref):
        acc = x_ref[0].astype(F32)
        for i in range(1, NDEV):
            acc = acc + x_ref[i].astype(F32)
        o_ref[...] = acc

    return pl.pallas_call(
        body, name=name, grid=(R // tr,),
        in_specs=[pl.BlockSpec((NDEV, tr, C), lambda i: (0, i, 0))],
        out_specs=pl.BlockSpec((tr, C), lambda i: (i, 0)),
        out_shape=jax.ShapeDtypeStruct((R, C), F32),
        compiler_params=_cp(1),
    )(x)


def _adamw_update(w_ref, g_ref, m_ref, v_ref, d_ref, nm_ref, nv_ref):
    c1 = 1.0 / (1.0 - ADAM_B1 ** ADAM_STEP)
    c2 = 1.0 / (1.0 - ADAM_B2 ** ADAM_STEP)
    gv = g_ref[...]
    nm = ADAM_B1 * m_ref[...] + (1.0 - ADAM_B1) * gv
    nv = ADAM_B2 * v_ref[...] + (1.0 - ADAM_B2) * (gv * gv)
    d_ref[...] = -ADAM_LR * ((nm * c1) / (jnp.sqrt(nv * c2) + ADAM_EPS) + ADAM_WD * w_ref[...])
    nm_ref[...] = nm
    nv_ref[...] = nv


def _adamw_many(ws, gs, ms, vs):
    n = len(ws)

    def body(*refs):
        for i in range(n):
            _adamw_update(refs[i], refs[n + i], refs[2 * n + i], refs[3 * n + i],
                          refs[4 * n + i], refs[5 * n + i], refs[6 * n + i])

    shapes = [jax.ShapeDtypeStruct(w.shape, F32) for w in ws]
    res = pl.pallas_call(
        body, name="adamw_small", out_shape=shapes * 3,
        compiler_params=pltpu.CompilerParams(vmem_limit_bytes=VMEM_LIMIT_BYTES),
    )(*ws, *gs, *ms, *vs)
    return res[:n], res[n:2 * n], res[2 * n:]


def _adamw(w, g, m, v, name):
    R, C = w.shape
    tr = _row_tile(R, 256)

    def body(w_ref, g_ref, m_ref, v_ref, d_ref, nm_ref, nv_ref):
        _adamw_update(w_ref, g_ref, m_ref, v_ref, d_ref, nm_ref, nv_ref)

    blk = pl.BlockSpec((tr, C), lambda i: (i, 0))
    return pl.pallas_call(
        body, name=name, grid=(R // tr,),
        in_specs=[blk] * 4, out_specs=[blk] * 3,
        out_shape=[jax.ShapeDtypeStruct((R, C), F32)] * 3,
        compiler_params=_cp(1),
    )(w, g, m, v)


def _mod_fwd(c_rows, w_shard, b_shard):
    def body(c_ref, w_ref, b_ref, o_ref):
        s = _silu(c_ref[...]).astype(BF16)
        o_ref[...] = _dot(s, w_ref[...].astype(BF16)) + b_ref[...]

    return pl.pallas_call(
        body, name="mod_fwd",
        out_shape=jax.ShapeDtypeStruct((c_rows.shape[0], w_shard.shape[1]), F32),
        compiler_params=pltpu.CompilerParams(vmem_limit_bytes=VMEM_LIMIT_BYTES),
    )(c_rows, w_shard, b_shard)


def _mod_bwd(c_rows, dm_all, dm_shard, w_shard):
    nrow = c_rows.shape[0]

    def body(c_ref, da_ref, ds_ref, w_ref, gw_ref, gb_ref, cc_ref):
        s = _silu(c_ref[...]).astype(BF16)
        ds = ds_ref[...]
        gw_ref[...] = _dot_tn(s, ds.astype(BF16))
        gb_ref[...] = jnp.sum(da_ref[...], axis=0, keepdims=True)
        rowi = lax.broadcasted_iota(jnp.int32, ds.shape, 0)
        dmc = jnp.sum(jnp.where(rowi % 8 >= 4, ds, 0.0), axis=0, keepdims=True)
        dmc8 = jnp.broadcast_to(dmc, (8, ds.shape[1])).astype(BF16)
        cc_ref[...] = _dot_nt(dmc8, w_ref[...].astype(BF16))

    return pl.pallas_call(
        body, name="mod_bwd",
        out_shape=[jax.ShapeDtypeStruct(w_shard.shape, F32),
                   jax.ShapeDtypeStruct((1, dm_all.shape[1]), F32),
                   jax.ShapeDtypeStruct((8, D), F32)],
        compiler_params=pltpu.CompilerParams(vmem_limit_bytes=VMEM_LIMIT_BYTES),
    )(c_rows, dm_all, dm_shard, w_shard)


def _small_finish(cc_pre, c_ctx, dd_cols):
    def body(cc_ref, c_ref, dd_ref, gc_ref, gd_ref):
        gc_ref[...] = cc_ref[...] * _silu_grad(c_ref[...])
        gd_ref[...] = jnp.sum(dd_ref[...], axis=1, keepdims=True)

    return pl.pallas_call(
        body, name="small_finish",
        out_shape=[jax.ShapeDtypeStruct((1, D), F32), jax.ShapeDtypeStruct((NH, 1), F32)],
    )(cc_pre, c_ctx, dd_cols)


_BIG = (("w_in", (D, 10304), 1), ("w_br_ssd", (DI, D), 0), ("w_br_lru", (LW, D), 0), ("w_out", (D, D), 0),
        ("w_mlp1", (D, MLP), 1), ("w_mlp2", (MLP, D), 0))
_SMALL_SH = (("ssd_conv_w", (4, 4096)), ("lru_conv_w", (4, LW)), ("lru_ba", (2, LW)), ("lru_bi", (2, LW)),
             ("lru_lambda", (2, LW)))
_REPL = (("c_ctx", (D,)), ("b_gate", (2 * D,)), ("ssd_conv_b", (4096,)), ("ssd_dt_bias", (2, NH)),
         ("ssd_a_log", (2, NH)), ("ssd_d", (DI,)), ("ssd_norm_w", (DI,)), ("lru_conv_b", (LW,)),
         ("ln1_g", (D,)), ("ln1_b", (D,)),
         ("b_mlp1", (MLP,)), ("b_mlp2", (D,)), ("ln2_g", (D,)), ("ln2_b", (D,)))

_WEIGHT_NAMES = ('c_ctx', 'w_mod', 'b_mod', 'w_in', 'b_gate', 'ssd_conv_w', 'ssd_conv_b', 'ssd_dt_bias', 'ssd_a_log',
                 'ssd_d', 'ssd_norm_w', 'lru_conv_w', 'lru_conv_b', 'lru_wa', 'lru_ba', 'lru_wi', 'lru_bi',
                 'lru_lambda', 'w_br_ssd', 'w_br_lru', 'w_out', 'ln1_g', 'ln1_b', 'w_mlp1', 'b_mlp1', 'w_mlp2',
                 'b_mlp2', 'ln2_g', 'ln2_b')
_ARG_NAMES = ('x', 'c', 'ctx') + _WEIGHT_NAMES + ('loss_target',) + tuple('m_' + n for n in _WEIGHT_NAMES) + tuple(
    'v_' + n for n in _WEIGHT_NAMES)


def _to_slots(full, axis):
    n = full.shape[axis] // NDEV
    if axis == 0:
        return full.reshape(NDEV, n, full.shape[1])
    return full.reshape(full.shape[0], NDEV, n).transpose(1, 0, 2)


def _from_slots(slots, axis):
    if axis == 0:
        return slots.reshape(NDEV * slots.shape[1], slots.shape[2])
    return slots.transpose(1, 0, 2).reshape(slots.shape[1], NDEV * slots.shape[2])


def _pack_rows(arrs, width=1024, mult=8):
    flat = jnp.concatenate([a.reshape(-1) for a in arrs])
    n = flat.shape[0]
    per = width * mult
    tot = -(-n // per) * per
    return jnp.pad(flat, (0, tot - n)).reshape(tot // width, width)


def _unpack_rows(packed, shapes, lead=()):
    nl = len(lead)
    flat = packed.reshape(tuple(lead) + (-1,))
    out, off = [], 0
    for s in shapes:
        n = math.prod(s)
        out.append(flat[..., off:off + n].reshape(tuple(lead) + tuple(s)))
        off += n
    return out


def kernel(x, c, ctx, c_ctx, w_mod, b_mod, w_in, b_gate, ssd_conv_w, ssd_conv_b, ssd_dt_bias, ssd_a_log, ssd_d, ssd_norm_w, lru_conv_w, lru_conv_b, lru_wa, lru_ba, lru_wi, lru_bi, lru_lambda, w_br_ssd, w_br_lru, w_out, ln1_g, ln1_b, w_mlp1, b_mlp1, w_mlp2, b_mlp2, ln2_g, ln2_b, loss_target, m_c_ctx, m_w_mod, m_b_mod, m_w_in, m_b_gate, m_ssd_conv_w, m_ssd_conv_b, m_ssd_dt_bias, m_ssd_a_log, m_ssd_d, m_ssd_norm_w, m_lru_conv_w, m_lru_conv_b, m_lru_wa, m_lru_ba, m_lru_wi, m_lru_bi, m_lru_lambda, m_w_br_ssd, m_w_br_lru, m_w_out, m_ln1_g, m_ln1_b, m_w_mlp1, m_b_mlp1, m_w_mlp2, m_b_mlp2, m_ln2_g, m_ln2_b, v_c_ctx, v_w_mod, v_b_mod, v_w_in, v_b_gate, v_ssd_conv_w, v_ssd_conv_b, v_ssd_dt_bias, v_ssd_a_log, v_ssd_d, v_ssd_norm_w, v_lru_conv_w, v_lru_conv_b, v_lru_wa, v_lru_ba, v_lru_wi, v_lru_bi, v_lru_lambda, v_w_br_ssd, v_w_br_lru, v_w_out, v_ln1_g, v_ln1_b, v_w_mlp1, v_b_mlp1, v_w_mlp2, v_b_mlp2, v_ln2_g, v_ln2_b):
    A = dict(zip(_ARG_NAMES, (x, c, ctx, c_ctx, w_mod, b_mod, w_in, b_gate, ssd_conv_w, ssd_conv_b, ssd_dt_bias, ssd_a_log, ssd_d, ssd_norm_w, lru_conv_w, lru_conv_b, lru_wa, lru_ba, lru_wi, lru_bi, lru_lambda, w_br_ssd, w_br_lru, w_out, ln1_g, ln1_b, w_mlp1, b_mlp1, w_mlp2, b_mlp2, ln2_g, ln2_b, loss_target, m_c_ctx, m_w_mod, m_b_mod, m_w_in, m_b_gate, m_ssd_conv_w, m_ssd_conv_b, m_ssd_dt_bias, m_ssd_a_log, m_ssd_d, m_ssd_norm_w, m_lru_conv_w, m_lru_conv_b, m_lru_wa, m_lru_ba, m_lru_wi, m_lru_bi, m_lru_lambda, m_w_br_ssd, m_w_br_lru, m_w_out, m_ln1_g, m_ln1_b, m_w_mlp1, m_b_mlp1, m_w_mlp2, m_b_mlp2, m_ln2_g, m_ln2_b, v_c_ctx, v_w_mod, v_b_mod, v_w_in, v_b_gate, v_ssd_conv_w, v_ssd_conv_b, v_ssd_dt_bias, v_ssd_a_log, v_ssd_d, v_ssd_norm_w, v_lru_conv_w, v_lru_conv_b, v_lru_wa, v_lru_ba, v_lru_wi, v_lru_bi, v_lru_lambda, v_w_br_ssd, v_w_br_lru, v_w_out, v_ln1_g, v_ln1_b, v_w_mlp1, v_b_mlp1, v_w_mlp2, v_b_mlp2, v_ln2_g, v_ln2_b)))
    Bn, T, _ = x.shape
    Tc = ctx.shape[1]
    cfg = _Cfg(Bn, T, Tc)
    me = _me()
    L = {n: (A[n] if n == "c_ctx" else A[n][0]) for n in _WEIGHT_NAMES}
    nmod = L["w_mod"].shape[1]

    c_all = _exchange(c, "ag_c", True)
    c_rows = jnp.concatenate([c_all.reshape(NDEV * Bn, D), jnp.broadcast_to(c_ctx[None, :], (8, D))], axis=0)
    b_shard = lax.dynamic_slice(L["b_mod"], (me * nmod,), (nmod,)).reshape(1, nmod)
    m_part = _mod_fwd(c_rows, L["w_mod"], b_shard)
    m_all = _exchange(m_part, "ag_mod", True)
    m_full = m_all.transpose(1, 0, 2).reshape(NDEV * Bn + 8, NMOD * D)
    m_mine = lax.dynamic_slice(m_full, (me * Bn, 0), (Bn, NMOD * D))
    mc = m_full[NDEV * Bn, :2 * D]

    w_in_all = _gather_two_level(L["w_in"].astype(BF16), "ag_w_in")
    rest_payload = jnp.concatenate([L[n].astype(BF16).reshape(-1, 1024) for n, _, _ in _BIG[1:]], axis=0)
    small_shapes = [(s[0], s[1] // NDEV) for _, s in _SMALL_SH]
    small_all = _exchange(_pack_rows([L[n] for n, _ in _SMALL_SH], width=512), "ag_w_small", True)
    W = {}
    for (n, shp), piece in zip(_SMALL_SH, _unpack_rows(small_all, small_shapes, lead=(NDEV,))):
        W[n] = piece.transpose(1, 0, 2).reshape(shp)
    W["w_main"], W["w_dt"] = _perm_w_in(_from_slots(w_in_all, 1))
    for n in ("ssd_conv_b", "lru_conv_b", "ssd_norm_w", "b_gate", "ln1_g", "ln1_b", "b_mlp1", "b_mlp2", "ln2_g", "ln2_b"):
        W[n] = L[n].reshape(1, -1)
    for n in ("ssd_dt_bias", "ssd_a_log", "ssd_d", "lru_wa", "lru_wi"):
        W[n] = L[n]

    loss_part, grad_x, g, dm, dmc, xres = _local_step(cfg, x, ctx, loss_target, m_mine, mc, W, rest_payload)
    loss = lax.psum(loss_part, ("x", "y", "c"))

    dmc_pad = jnp.pad(dmc, ((0, 4 - Bn), (0, (NMOD - 2) * D)))
    dm_all = _exchange(jnp.concatenate([jnp.pad(dm, ((0, 4 - Bn), (0, 0))), dmc_pad], axis=0), "ag_dm", True)
    dm_all = dm_all.reshape(NDEV * 8, NMOD * D)
    c_rows_b = jnp.concatenate([jnp.pad(c_all, ((0, 0), (0, 4 - Bn), (0, 0))),
                                jnp.broadcast_to(c_ctx[None, None, :], (NDEV, 4, D))], axis=1).reshape(NDEV * 8, D)
    dm_shard = lax.dynamic_slice(dm_all, (0, me * nmod), (NDEV * 8, nmod))
    g_w_mod, g_b_mod, cc_part = _mod_bwd(c_rows_b, dm_all, dm_shard, L["w_mod"])
    g["c_ctx"] = cc_part[0]

    g["ssd_d"] = g.pop("ssd_d_cols")
    small_names = [n for n, _ in _REPL] + [n for n, _ in _SMALL_SH]
    small_full_shapes = [s for _, s in _REPL] + [s for _, s in _SMALL_SH]
    sm_all = _exchange(_pack_rows([g[n] for n in small_names]), "ag_g_small", True)
    sm_sum = _sum_slots(sm_all, "sum_g_small")
    gs = dict(zip(small_names, _unpack_rows(sm_sum, small_full_shapes)))
    gcc, gdd = _small_finish(gs["c_ctx"].reshape(1, D), c_ctx.reshape(1, D), gs["ssd_d"].reshape(NH, HD))
    gs["c_ctx"] = gcc.reshape(D)
    gs["ssd_d"] = gdd.reshape(NH)
    for n, shp in _SMALL_SH:
        ns = shp[1] // NDEV
        gs[n] = lax.dynamic_slice(gs[n], (0, me * ns), (shp[0], ns))
    gs["b_mod"] = g_b_mod.reshape(NMOD * D)
    lru_sum = _sum_slots(xres["ag_lru"], "sum_g_lru").reshape(2, 2, LB, LBW, LBW)
    gs["lru_wa"], gs["lru_wi"] = lru_sum[0], lru_sum[1]

    gb = {}
    gb["w_in"] = _sum_slots(xres["rs_w_in"], "sum_w_in")
    red_b = _sum_slots(xres["rs_rest"], "sum_w_rest")
    off = 0
    for n, shp, axis in _BIG[1:]:
        shard_shape = (shp[0] // NDEV, shp[1]) if axis == 0 else (shp[0], shp[1] // NDEV)
        r = math.prod(shard_shape) // 1024
        gb[n] = red_b[off:off + r].reshape(shard_shape)
        off += r
    gb["w_mod"] = g_w_mod

    grads, deltas, new_m, new_v = {}, {}, {}, {}
    big_names = ["w_mod"] + [n for n, _, _ in _BIG]
    for n in big_names:
        d_, nm_, nv_ = _adamw(L[n], gb[n], A["m_" + n][0], A["v_" + n][0], "adamw_" + n)
        grads[n], deltas[n], new_m[n], new_v[n] = gb[n], d_, nm_, nv_
    sm_names = [n for n in _WEIGHT_NAMES if n not in big_names]

    def two_d(a):
        return a.reshape(1, -1) if a.ndim == 1 else a
    loc = lambda pre: [two_d(A[pre + n] if n == "c_ctx" else A[pre + n][0]) for n in sm_names]
    gsm = [two_d(gs[n].reshape(L[n].shape)) for n in sm_names]
    ds_, nms_, nvs_ = _adamw_many(loc(""), gsm, loc("m_"), loc("v_"))
    for n, gv, dv, mv, vv in zip(sm_names, gsm, ds_, nms_, nvs_):
        shp = L[n].shape
        grads[n], deltas[n], new_m[n], new_v[n] = gv.reshape(shp), dv.reshape(shp), mv.reshape(shp), vv.reshape(shp)

    def out(dct):
        return [dct[n] if n == "c_ctx" else dct[n][None] for n in _WEIGHT_NAMES]
    return (loss, grad_x, *out(grads), *out(deltas), *out(new_m), *out(new_v))
```

```python
import functools
import math

import jax
import jax.numpy as jnp
from jax import lax
from jax.experimental import pallas as pl
from jax.experimental.pallas import tpu as pltpu

F32 = jnp.float32
BF16 = jnp.bfloat16

D = 1024
GRID_W = 64
DI = 2048
NH = 32
HD = 64
NG = 8
HPG = 4
NS = 128
CH = 128
LW = 1024
LB = 8
LBW = 128
LRU_C = 8.0
MLP = 4096
NMOD = 6
ALPHA = 2.0 ** 0.25
LN_EPS = 1e-6
RMS_EPS = 1e-5
PM = 10240
DTW = 128
CONVW = 5120
NDEV = 8

ADAM_LR = 0.001
ADAM_B1 = 0.9
ADAM_B2 = 0.999
ADAM_EPS = 1e-08
ADAM_WD = 0.01
ADAM_STEP = 10

VMEM_LIMIT_BYTES = 56 * 1024 * 1024


def _cp(n_axes):
    return pltpu.CompilerParams(dimension_semantics=("arbitrary",) * n_axes,
                                vmem_limit_bytes=VMEM_LIMIT_BYTES)


def _sigmoid(x):
    return 0.5 * jnp.tanh(0.5 * x) + 0.5


def _silu(x):
    return x * _sigmoid(x)


def _silu_grad(x):
    s = _sigmoid(x)
    return s * (1.0 + x * (1.0 - s))


def _log1p_pos(e):
    return jnp.where(e < 1e-2, e * (1.0 - e * (0.5 - e * (1.0 / 3.0))), jnp.log(1.0 + e))


def _softplus(x):
    return jnp.maximum(x, 0.0) + _log1p_pos(jnp.exp(-jnp.abs(x)))


_GELU_K = math.sqrt(2.0 / math.pi)


def _gelu(x):
    t = jnp.tanh(_GELU_K * (x + 0.044715 * x * x * x))
    return 0.5 * x * (1.0 + t)


def _gelu_grad(x):
    t = jnp.tanh(_GELU_K * (x + 0.044715 * x * x * x))
    dt = (1.0 - t * t) * _GELU_K * (1.0 + 3.0 * 0.044715 * x * x)
    return 0.5 * (1.0 + t) + 0.5 * x * dt


def _ln(x):
    mu = jnp.mean(x, axis=-1, keepdims=True)
    xc = x - mu
    var = jnp.mean(xc * xc, axis=-1, keepdims=True)
    rs = lax.rsqrt(var + LN_EPS)
    return xc * rs, rs


def _ln_bwd(dy, xhat, rs):
    m1 = jnp.mean(dy, axis=-1, keepdims=True)
    m2 = jnp.mean(dy * xhat, axis=-1, keepdims=True)
    return rs * (dy - m1 - xhat * m2)


def _dot(a, b):
    return lax.dot_general(a, b, (((1,), (0,)), ((), ())), preferred_element_type=F32)


def _dot_nt(a, b):
    return lax.dot_general(a, b, (((1,), (1,)), ((), ())), preferred_element_type=F32)


def _dot_tn(a, b):
    return lax.dot_general(a, b, (((0,), (0,)), ((), ())), preferred_element_type=F32)


def _split3(a):
    a0 = a.astype(BF16)
    r = a - a0.astype(F32)
    a1 = r.astype(BF16)
    a2 = (r - a1.astype(F32)).astype(BF16)
    return a0, a1, a2


def _dot_exact_l(m_bf, a):
    a0, a1, a2 = _split3(a)
    return _dot(m_bf, a0) + _dot(m_bf, a1) + _dot(m_bf, a2)


def _dot_exact_r(a, m_bf):
    a0, a1, a2 = _split3(a)
    return _dot(a0, m_bf) + _dot(a1, m_bf) + _dot(a2, m_bf)


def _dot_hilo_r(a, m_bf):
    a0 = a.astype(BF16)
    a1 = (a - a0.astype(F32)).astype(BF16)
    return _dot(a0, m_bf) + _dot(a1, m_bf)


def _tri(n, upper):
    ii = lax.broadcasted_iota(jnp.int32, (n, n), 0)
    kk = lax.broadcasted_iota(jnp.int32, (n, n), 1)
    m = (kk >= ii) if upper else (kk <= ii)
    return jnp.where(m, 1.0, 0.0).astype(BF16)


def _fit(n, t):
    t = min(t, n)
    while n % t:
        t //= 2
    return t


def _mm(a, b, mode, name, out_dtype=F32, tm=512, tn=512, tk=512, xch=None):
    if mode == "nn":
        M, K = a.shape
        N = b.shape[1]
    elif mode == "nt":
        M, K = a.shape
        N = b.shape[0]
    else:
        K, M = a.shape
        N = b.shape[1]
    tm, tn, tk = _fit(M, tm), _fit(N, tn), _fit(K, tk)
    assert M % tm == 0 and N % tn == 0 and K % tk == 0, (name, M, N, K, tm, tn, tk)
    nk = K // tk
    if mode == "tn":
        a_spec = pl.BlockSpec((tk, tm), lambda i, j, k: (k, i))
    else:
        a_spec = pl.BlockSpec((tm, tk), lambda i, j, k: (i, k))
    if mode == "nt":
        b_spec = pl.BlockSpec((tn, tk), lambda i, j, k: (j, k))
    else:
        b_spec = pl.BlockSpec((tk, tn), lambda i, j, k: (k, j))
    dn = {"nn": (((1,), (0,)), ((), ())), "nt": (((1,), (1,)), ((), ())), "tn": (((0,), (0,)), ((), ()))}[mode]

    def body(a_ref, b_ref, o_ref, acc_ref):
        k = pl.program_id(2)

        @pl.when(k == 0)
        def _():
            acc_ref[...] = jnp.zeros_like(acc_ref)

        acc_ref[...] += lax.dot_general(a_ref[...].astype(BF16), b_ref[...].astype(BF16), dn,
                                        preferred_element_type=F32)

        @pl.when(k == nk - 1)
        def _():
            o_ref[...] = acc_ref[...].astype(o_ref.dtype)

    res = _hosted_call(
        body, xch, name=name, grid=(M // tm, N // tn, nk),
        in_specs=[a_spec, b_spec],
        out_specs=[pl.BlockSpec((tm, tn), lambda i, j, k: (i, j))],
        out_shape=[jax.ShapeDtypeStruct((M, N), out_dtype)],
        scratch_shapes=[pltpu.VMEM((tm, tn), F32)],
        compiler_params=_cp(3), args=(a, b))
    if xch is None:
        return res[0]
    return res[0][0], res[1]


def _mm_mlp1(h2, w1, b1, tm=1024, tn=1024):
    M, K = h2.shape
    N = w1.shape[1]
    tm, tn = _fit(M, tm), _fit(N, tn)

    def body(a_ref, b_ref, bias_ref, a1_ref, act_ref):
        v = _dot(a_ref[...], b_ref[...]) + bias_ref[...]
        a1_ref[...] = v
        r = jnp.maximum(v, 0.0)
        act_ref[...] = (r * r).astype(BF16)

    out = pl.BlockSpec((tm, tn), lambda i, j: (i, j))
    return pl.pallas_call(
        body, name="mm_mlp1", grid=(M // tm, N // tn),
        in_specs=[pl.BlockSpec((tm, K), lambda i, j: (i, 0)), pl.BlockSpec((K, tn), lambda i, j: (0, j)),
                  pl.BlockSpec((1, tn), lambda i, j: (0, j))],
        out_specs=[out, out],
        out_shape=[jax.ShapeDtypeStruct((M, N), F32), jax.ShapeDtypeStruct((M, N), BF16)],
        compiler_params=_cp(2),
    )(h2, w1, b1)


def _mm_dact(dmlp, w2, a1, tm=1024, tn=1024):
    M, K = dmlp.shape
    N = w2.shape[0]
    tm, tn = _fit(M, tm), _fit(N, tn)

    def body(d_ref, w_ref, a1_ref, o_ref, acc_ref):
        i = pl.program_id(1)

        @pl.when(i == 0)
        def _():
            acc_ref[...] = jnp.zeros_like(acc_ref)

        da = _dot_nt(d_ref[...], w_ref[...]) * (2.0 * jnp.maximum(a1_ref[...], 0.0))
        o_ref[...] = da.astype(BF16)
        acc_ref[0:1, :] += jnp.sum(da, axis=0, keepdims=True)

    blk = pl.BlockSpec((tm, tn), lambda j, i: (i, j))
    return pl.pallas_call(
        body, name="mm_dact", grid=(N // tn, M // tm),
        in_specs=[pl.BlockSpec((tm, K), lambda j, i: (i, 0)), pl.BlockSpec((tn, K), lambda j, i: (j, 0)), blk],
        out_specs=[blk, pl.BlockSpec((8, tn), lambda j, i: (0, j))],
        out_shape=[jax.ShapeDtypeStruct((M, N), BF16), jax.ShapeDtypeStruct((8, N), F32)],
        compiler_params=_cp(2),
    )(dmlp, w2, a1)


class _Cfg:
    def __init__(self, Bn, T, Tc):
        assert T % Tc == 0 and Tc % CH == 0 and Tc % GRID_W == 0
        self.Bn, self.T, self.Tc = Bn, T, Tc
        self.TT = T + Tc
        self.TB = Tc
        self.nbt = self.TT // self.TB
        self.nbl = T // self.TB
        self.NT = Bn * self.TT
        self.N = Bn * T
        self.nct = Tc // CH
        self.nlt = T // CH
        self.nch = self.nct + self.nlt


def _ln_mod_fwd(cfg, xs, shift_tab, scale_tab):
    TB, nbt = cfg.TB, cfg.nbt

    def body(x_ref, sh_ref, sc_ref, o_ref):
        xhat, _ = _ln(x_ref[...])
        o_ref[...] = (xhat * (1.0 + sc_ref[...]) + sh_ref[...]).astype(BF16)

    tab = pl.BlockSpec((None, 1, D), lambda b, j: (2 * b + jnp.minimum(j, 1), 0, 0))
    return pl.pallas_call(
        body, name="ln_mod_fwd", grid=(cfg.Bn, nbt),
        in_specs=[pl.BlockSpec((TB, D), lambda b, j: (b * nbt + j, 0)), tab, tab],
        out_specs=pl.BlockSpec((TB, D), lambda b, j: (b * nbt + j, 0)),
        out_shape=jax.ShapeDtypeStruct((cfg.NT, D), BF16),
        compiler_params=_cp(2),
    )(xs, shift_tab, scale_tab)


GP = 8
NGB = NG // GP
HPB = GP * HPG


def _head_select(d, gb, to_front):
    r = lax.broadcasted_iota(jnp.int32, (DTW, DTW), 0)
    c = lax.broadcasted_iota(jnp.int32, (DTW, DTW), 1)
    src, dst = (r, c) if to_front else (c, r)
    hit = jnp.logical_and(src == d * NH + gb * HPB + dst, dst < HPB)
    return jnp.where(hit, 1.0, 0.0).astype(BF16)


def _dt_fwd(cfg, dt_raw, dt_bias, a_log):
    def body(raw_ref, bias_ref, alog_ref, dt_ref, dtg_ref, cumg_ref, cumT_ref):
        dt = _softplus(raw_ref[...] + bias_ref[...])
        a = -jnp.exp(alog_ref[...])
        dta = dt * a
        col = lax.broadcasted_iota(jnp.int32, (CH, DTW), 1)
        cf = _dot_exact_l(_tri(CH, False), dta)
        cr = _dot_exact_l(_tri(CH, True), dta)
        cum = jnp.where(col < NH, cf, cr)
        dt_ref[...] = dt
        for d in range(2):
            for gb in range(NGB):
                sel = _head_select(d, gb, True)
                dtg_ref[d, gb] = _dot_exact_r(dt, sel)
                cg = _dot_exact_r(cum, sel)
                cumg_ref[d, gb] = cg
                cumT_ref[d, gb] = cg.T

    blk = pl.BlockSpec((CH, DTW), lambda i: (i, 0))
    row = pl.BlockSpec((1, DTW), lambda i: (0, 0))
    gblk = pl.BlockSpec((2, NGB, CH, DTW), lambda i: (0, 0, i, 0))
    return pl.pallas_call(
        body, name="dt_fwd", grid=(cfg.NT // CH,),
        in_specs=[blk, row, row],
        out_specs=[blk, gblk, gblk, pl.BlockSpec((2, NGB, None, DTW, CH), lambda i: (0, 0, i, 0, 0))],
        out_shape=[jax.ShapeDtypeStruct((cfg.NT, DTW), F32),
                   jax.ShapeDtypeStruct((2, NGB, cfg.NT, DTW), F32),
                   jax.ShapeDtypeStruct((2, NGB, cfg.NT, DTW), F32),
                   jax.ShapeDtypeStruct((2, NGB, cfg.NT // CH, DTW, CH), F32)],
        compiler_params=_cp(1),
    )(dt_raw, dt_bias, a_log)


def _dt_bwd(cfg, dAs, dxxs, dt_raw, dt, dt_bias, a_log):
    def body(dAf_ref, dAr_ref, dxf_ref, dxr_ref, raw_ref, dt_ref, bias_ref, alog_ref, o_ref, acc_ref):
        i = pl.program_id(0)

        @pl.when(i == 0)
        def _():
            acc_ref[...] = jnp.zeros_like(acc_ref)

        a = -jnp.exp(alog_ref[...])
        col = lax.broadcasted_iota(jnp.int32, (CH, DTW), 1)
        dA_v = jnp.zeros((CH, DTW), F32)
        dxx_v = jnp.zeros((CH, DTW), F32)
        for d, (ra, rx) in enumerate(((dAf_ref, dxf_ref), (dAr_ref, dxr_ref))):
            for gb in range(NGB):
                sel = _head_select(d, gb, False)
                dA_v = dA_v + _dot_exact_r(ra[gb], sel)
                dxx_v = dxx_v + _dot_exact_r(rx[gb], sel)
        ddta = jnp.where(col < NH, _dot_exact_l(_tri(CH, True), dA_v), _dot_exact_l(_tri(CH, False), dA_v))
        dtv = dt_ref[...]
        ddt = ddta * a + dxx_v
        draw = ddt * _sigmoid(raw_ref[...] + bias_ref[...])
        draw = jnp.where(col < 2 * NH, draw, 0.0)
        o_ref[...] = draw
        da = jnp.sum(ddta * dtv, axis=0, keepdims=True) * a
        da = jnp.where(col[:1] < 2 * NH, da, 0.0)
        acc_ref[0:1, :] += da
        acc_ref[1:2, :] += jnp.sum(draw, axis=0, keepdims=True)

    blk = pl.BlockSpec((CH, DTW), lambda i: (i, 0))
    row = pl.BlockSpec((1, DTW), lambda i: (0, 0))
    gblk = pl.BlockSpec((NGB, CH, DTW), lambda i: (0, i, 0))
    return pl.pallas_call(
        body, name="dt_bwd", grid=(cfg.NT // CH,),
        in_specs=[gblk, gblk, gblk, gblk, blk, blk, row, row],
        out_specs=[blk, pl.BlockSpec((8, DTW), lambda i: (0, 0))],
        out_shape=[jax.ShapeDtypeStruct((cfg.NT, DTW), F32), jax.ShapeDtypeStruct((8, DTW), F32)],
        compiler_params=_cp(1),
    )(dAs[0], dAs[1], dxxs[0], dxxs[1], dt_raw, dt, dt_bias, a_log)


_TAPS = (2, 1, 0, -1)


def _conv_fwd(cfg, proj, conv_w, conv_b):
    TB, nbt = cfg.TB, cfg.nbt
    CB = CONVW // 2
    SUB = 256
    n_act = DI + 2 * NG * NS

    def body(u_ref, w_ref, b_ref, o_ref, sg_ref):
        i = pl.program_id(0)
        j = pl.program_id(1)
        R = jnp.where(i % nbt == 0, cfg.Tc, GRID_W)
        t = lax.broadcasted_iota(jnp.int32, (TB, SUB), 0)
        pos = jnp.bitwise_and(t, R - 1)
        keep = {s: jnp.where(jnp.logical_and(pos - s >= 0, pos - s < R), 1.0, 0.0) for s in (2, 1, -1)}
        def sub_tile(q, act):
            sl = slice(q * SUB, (q + 1) * SUB)
            u = u_ref[:, sl]
            pre = b_ref[:, sl] + w_ref[2:3, sl] * u
            for k in (0, 1, 3):
                pre = pre + w_ref[k:k + 1, sl] * (pltpu.roll(u, _TAPS[k] % TB, 0) * keep[_TAPS[k]])
            if act:
                s = _sigmoid(pre)
                o_ref[:, sl] = pre * s
                sg_ref[:, sl] = s * (1.0 + pre * (1.0 - s))
            else:
                o_ref[:, sl] = pre

        for q in range(CB // SUB):
            if q * SUB >= n_act - CB:
                pl.when(j == 0)(functools.partial(sub_tile, q, True))
                pl.when(j == 1)(functools.partial(sub_tile, q, False))
            else:
                sub_tile(q, True)

    blk = pl.BlockSpec((TB, CB), lambda i, j: (i, j))
    return pl.pallas_call(
        body, name="conv_fwd", grid=(cfg.NT // TB, CONVW // CB),
        in_specs=[blk, pl.BlockSpec((4, CB), lambda i, j: (0, j)), pl.BlockSpec((1, CB), lambda i, j: (0, j))],
        out_specs=[blk, blk],
        out_shape=[jax.ShapeDtypeStruct((cfg.NT, CONVW), F32)] * 2,
        compiler_params=_cp(2),
    )(proj, conv_w, conv_b)


_ANY = pl.BlockSpec(memory_space=pl.ANY)


def _conv_bwd(cfg, name, dproj, proj, conv_w, sgrad, addends, col0, width, skip=None, xch=None):
    TB, nbt, nbl = cfg.TB, cfg.nbt, cfg.nbl
    CB = 1024
    SUB = 256
    c0 = col0 // CB
    addends = list(addends) + ([] if sgrad is None else [sgrad])
    n_add = len(addends)

    def body(*refs):
        u_ref, w_ref = refs[1:3]
        add_refs = refs[3:3 + n_add]
        rest = refs[3 + n_add:]
        if sgrad is not None:
            add_refs, sg_ref = add_refs[:-1], add_refs[-1]
        if skip is not None:
            dy_ref, dv_ref = rest[:2]
            rest = rest[2:]
        o_ref, acc_ref = rest
        i = pl.program_id(1)

        @pl.when(i == 0)
        def _():
            acc_ref[...] = jnp.zeros_like(acc_ref)

        isctx = (i % nbt) == 0
        R = jnp.where(isctx, cfg.Tc, GRID_W)
        t = lax.broadcasted_iota(jnp.int32, (TB, SUB), 0)
        pos = jnp.bitwise_and(t, R - 1)
        keep = {s: jnp.where(jnp.logical_and(pos - s >= 0, pos - s < R), 1.0, 0.0) for s in (2, 1, -1, -2)}

        def shifted(v, s):
            return v if s == 0 else pltpu.roll(v, s % TB, 0) * keep[s]

        for q in range(CB // SUB):
            sl = slice(q * SUB, (q + 1) * SUB)
            u = u_ref[:, sl]
            us = [shifted(u, _TAPS[k]) for k in range(4)]
            g = add_refs[0][:, sl]
            for r in add_refs[1:]:
                g = g + r[:, sl]
            if skip is not None:
                g = g + jnp.where(isctx, 0.0, dv_ref[:, sl] * dy_ref[:, sl])
            if sgrad is not None:
                g = g * sg_ref[:, sl]
            dp = jnp.zeros_like(g)
            for k in range(4):
                acc_ref[k:k + 1, sl] += jnp.sum(g * us[k], axis=0, keepdims=True)
                dp = dp + w_ref[k:k + 1, sl] * shifted(g, -_TAPS[k])
            acc_ref[4:5, sl] += jnp.sum(g, axis=0, keepdims=True)
            o_ref[:, sl] = dp.astype(BF16)

    blk = pl.BlockSpec((TB, CB), lambda j, i: (i, j))
    wide = pl.BlockSpec((TB, CB), lambda j, i: (i, c0 + j))
    in_specs = [_ANY, wide, pl.BlockSpec((4, CB), lambda j, i: (0, c0 + j))]
    in_specs += [blk] * (n_add if sgrad is None else n_add - 1) + ([] if sgrad is None else [wide])
    args = [dproj, proj, conv_w] + addends
    if skip is not None:
        def lat(j, i):
            b = i // nbt
            return (b * nbl + jnp.maximum(i % nbt - 1, 0), j)
        in_specs += [pl.BlockSpec((TB, CB), lat), pl.BlockSpec((1, CB), lambda j, i: (0, j))]
        args += list(skip)
    return _hosted_call(
        body, xch, name=name, grid=(width // CB, cfg.NT // TB),
        in_specs=in_specs,
        out_specs=[pl.BlockSpec((TB, CB), lambda j, i: (i, c0 + j)), pl.BlockSpec((8, CB), lambda j, i: (0, j))],
        out_shape=[jax.ShapeDtypeStruct((cfg.NT, PM), BF16), jax.ShapeDtypeStruct((8, width), F32)],
        scratch_shapes=[], compiler_params=_cp(2), args=args, aliases={0: 0})


def _chunk_of_step(cfg, rev):
    nct, nlt = cfg.nct, cfg.nlt
    if not rev:
        return lambda s: s
    return lambda s: jnp.where(s < nct, nct - 1 - s, 2 * nct + nlt - 1 - s)


def _expand4(v, band, base):
    out = v[:, base + 3:base + 4]
    for h in (2, 1, 0):
        out = jnp.where(band == h, v[:, base + h:base + h + 1], out)
    return out


def _ssd_step_tiles(dt_ref, cum_ref, cumT_ref, rev):
    cum_t = cum_ref[...]
    last = 0 if rev else CH - 1
    llast = cum_t[last:last + 1, :]
    return (dt_ref[...], cum_t, cumT_ref[...], llast, jnp.exp(llast), last)


def _ssd_common(gi, x_ref, b_ref, c_ref, tiles, rev, intra=True):
    dt_t, cum_t, cumT_t, llast, elast, last = tiles
    base = gi * HPG
    xh = x_ref[:, gi * HPG * HD:(gi + 1) * HPG * HD]
    Bm = b_ref[:, gi * NS:(gi + 1) * NS].astype(BF16)
    band = lax.broadcasted_iota(jnp.int32, (CH, HPG * HD), 1) // HD
    cbs = [jnp.broadcast_to(cum_t[:, base + h:base + h + 1], (CH, CH)) for h in range(HPG)]
    Cm = G = decs = None
    if intra:
        Cm = c_ref[:, gi * NS:(gi + 1) * NS].astype(BF16)
        G = _dot_nt(Cm, Bm)
        ii = lax.broadcasted_iota(jnp.int32, (CH, CH), 0)
        jj = lax.broadcasted_iota(jnp.int32, (CH, CH), 1)
        mask = (jj >= ii) if rev else (jj <= ii)
        decs = [jnp.exp(jnp.where(mask, cbs[h] - cumT_t[base + h:base + h + 1, :], -1e30)) for h in range(HPG)]
    cum_exp = jnp.concatenate([cbs[3], cbs[3]], axis=1)
    ll_exp = llast[:, base + 3:base + 4]
    for h in (2, 1, 0):
        cum_exp = jnp.where(band == h, jnp.concatenate([cbs[h], cbs[h]], axis=1), cum_exp)
        ll_exp = jnp.where(band[:1] == h, llast[:, base + h:base + h + 1], ll_exp)
    ecum = jnp.exp(cum_exp) if intra else None
    e_exp = jnp.exp(ll_exp - cum_exp)
    dt_exp = _expand4(dt_t, band, base)
    X = xh * dt_exp
    rb = lax.broadcasted_iota(jnp.int32, (HPG * HD, NS), 0) // HD
    dec_rows = elast[:, base + 3:base + 4]
    for h in (2, 1, 0):
        dec_rows = jnp.where(rb == h, elast[:, base + h:base + h + 1], dec_rows)
    return xh, Bm, Cm, band, e_exp, ecum, dt_exp, X, G, decs, elast, dec_rows, last


def _ssd_specs(cfg, rev):
    nch = cfg.nch
    cmap = _chunk_of_step(cfg, rev)
    d = 1 if rev else 0

    def make(stepmap):
        def row(b, g, sp):
            return b * nch + cmap(stepmap(sp))
        bo, co = DI // (GP * NS), (DI + NG * NS) // (GP * NS)
        return [
            pl.BlockSpec((CH, GP * HPG * HD), lambda b, g, sp: (row(b, g, sp), g)),
            pl.BlockSpec((CH, GP * NS), lambda b, g, sp: (row(b, g, sp), bo + g)),
            pl.BlockSpec((CH, GP * NS), lambda b, g, sp: (row(b, g, sp), co + g)),
            pl.BlockSpec((None, None, CH, DTW), lambda b, g, sp: (d, g, row(b, g, sp), 0)),
            pl.BlockSpec((None, None, CH, DTW), lambda b, g, sp: (d, g, row(b, g, sp), 0)),
            pl.BlockSpec((None, None, None, DTW, CH), lambda b, g, sp: (d, g, row(b, g, sp), 0, 0)),
        ], row
    return make


def _ssd_fwd(cfg, act, dtg, cumg, cumTg, rev, y_other=None, dvec=None):
    nch = cfg.nch
    in_specs, row = _ssd_specs(cfg, rev)(lambda sp: sp)
    total = y_other is not None

    def body(*refs):
        x_ref, b_ref, c_ref, dt_ref, cum_ref, cumT_ref = refs[:6]
        if total:
            yo_ref, dv_ref = refs[6:8]
        y_ref, hs_ref, h_scr = refs[-3:]
        s = pl.program_id(2)

        @pl.when(s == 0)
        def _():
            h_scr[...] = jnp.zeros_like(h_scr)

        def step(intra):
            tiles = _ssd_step_tiles(dt_ref, cum_ref, cumT_ref, rev)
            for gi in range(GP):
                xh, Bm, Cm, band, e_exp, ecum, dt_exp, X, G, decs, elast, dec_rows, last = _ssd_common(
                    gi, x_ref, b_ref, c_ref, tiles, rev, intra)
                H = h_scr[gi]
                if intra:
                    Mcat = jnp.concatenate([(G * decs[h]).astype(BF16) for h in range(HPG)], axis=1)
                    Xbd = jnp.concatenate([jnp.where(band == h, X, 0.0).astype(BF16) for h in range(HPG)], axis=0)
                    xsl = slice(gi * HPG * HD, (gi + 1) * HPG * HD)
                    Y = ecum * _dot_nt(Cm, H.astype(BF16)) + _dot(Mcat, Xbd)
                    if total:
                        Y = Y + yo_ref[:, xsl] + dv_ref[:, xsl] * xh
                    y_ref[:, xsl] = Y
                hs_ref[gi] = H
                S = _dot_tn((e_exp * X).astype(BF16), Bm)
                h_scr[gi] = dec_rows * H + S

        isctx = cmap(s) < cfg.nct

        @pl.when(isctx)
        def _():
            step(False)

        @pl.when(jnp.logical_not(isctx))
        def _():
            step(True)

    cmap = _chunk_of_step(cfg, rev)
    yblk = pl.BlockSpec((CH, GP * HPG * HD), lambda b, g, s: (row(b, g, s), g))
    args = [act, act, act, dtg, cumg, cumTg]
    if total:
        in_specs = in_specs + [yblk, pl.BlockSpec((1, GP * HPG * HD), lambda b, g, s: (0, g))]
        args += [y_other, dvec]
    return pl.pallas_call(
        body, name="ssd_fwd_rev" if rev else "ssd_fwd", grid=(cfg.Bn, NG // GP, nch),
        in_specs=in_specs,
        out_specs=[yblk, pl.BlockSpec((None, GP, None, HPG * HD, NS), lambda b, g, s: (b, g, s, 0, 0))],
        out_shape=[jax.ShapeDtypeStruct((cfg.NT, DI), F32),
                   jax.ShapeDtypeStruct((cfg.Bn, NG, nch, HPG * HD, NS), F32)],
        scratch_shapes=[pltpu.VMEM((GP, HPG * HD, NS), F32)],
        compiler_params=_cp(3),
    )(*args)


def _ssd_bwd(cfg, act, dtg, cumg, cumTg, hs, dy, rev, xch=None):
    nch, nct, nlt = cfg.nch, cfg.nct, cfg.nlt
    cmap = _chunk_of_step(cfg, rev)
    in_specs, row = _ssd_specs(cfg, rev)(lambda sp: nch - 1 - sp)

    def lat_row(b, g, sp):
        c = cmap(nch - 1 - sp)
        return b * nlt + jnp.maximum(c - nct, 0)

    def body(x_ref, b_ref, c_ref, dt_ref, cum_ref, cumT_ref, dy_ref, hs_ref,
             dxh_ref, dB_ref, dC_ref, dA_ref, dxx_ref, dh_scr):
        sp = pl.program_id(2)

        @pl.when(sp == 0)
        def _():
            dh_scr[...] = jnp.zeros_like(dh_scr)

        def step(intra):
            tiles = _ssd_step_tiles(dt_ref, cum_ref, cumT_ref, rev)
            dA_t = jnp.zeros((CH, DTW), F32)
            dAT_t = jnp.zeros((DTW, CH), F32)
            dxx_t = jnp.zeros((CH, DTW), F32)
            for gi in range(GP):
                dA_g, dAT_g, dxx_g = group_bwd(gi, intra, tiles, x_ref, b_ref, c_ref, dy_ref, hs_ref,
                                               dxh_ref, dB_ref, dC_ref, dh_scr)
                dA_t = dA_t + dA_g
                dxx_t = dxx_t + dxx_g
                if intra:
                    dAT_t = dAT_t + dAT_g
            dA_ref[...] = dA_t - dAT_t.T if intra else dA_t
            dxx_ref[...] = dxx_t

        isctx = cmap(nch - 1 - sp) < nct

        @pl.when(isctx)
        def _():
            step(False)

        @pl.when(jnp.logical_not(isctx))
        def _():
            step(True)

    def group_bwd(gi, intra, tiles, x_ref, b_ref, c_ref, dy_ref, hs_ref, dxh_ref, dB_ref, dC_ref, dh_scr):
        xsl = slice(gi * HPG * HD, (gi + 1) * HPG * HD)
        nsl = slice(gi * NS, (gi + 1) * NS)
        base = gi * HPG
        xh, Bm, Cm, band, e_exp, ecum, dt_exp, X, G, decs, elast, dec_rows, last = _ssd_common(
            gi, x_ref, b_ref, c_ref, tiles, rev, intra)
        H = hs_ref[gi]
        dHn = dh_scr[gi]
        dHnb = dHn.astype(BF16)
        BdH = _dot_nt(Bm, dHnb)
        dX = e_exp * BdH
        eX = e_exp * X
        lanei = lax.broadcasted_iota(jnp.int32, (CH, DTW), 1)
        dA = jnp.zeros((CH, DTW), F32)
        dAT = None
        pb = lax.broadcasted_iota(jnp.int32, (HPG * HD, NS), 0) // HD
        pl_ = lax.broadcasted_iota(jnp.int32, (HPG * HD, NS), 1)
        E = jnp.where(pb + base == pl_, 1.0, 0.0).astype(BF16)
        if intra:
            dY = dy_ref[:, xsl]
            Hb = H.astype(BF16)
            dYs = ecum * dY
            dYsb = dYs.astype(BF16)
            Ys = ecum * _dot_nt(Cm, Hb)
            dG = jnp.zeros((CH, CH), F32)
            subi = lax.broadcasted_iota(jnp.int32, (DTW, CH), 0)
            dAT = jnp.zeros((DTW, CH), F32)
            Xbd = jnp.concatenate([jnp.where(band == h, X, 0.0).astype(BF16) for h in range(HPG)], axis=0)
            dYbd = jnp.concatenate([jnp.where(band == h, dY, 0.0).astype(BF16) for h in range(HPG)], axis=0)
            dMcat = _dot_nt(dY.astype(BF16), Xbd)
            Ms = []
            for h in range(HPG):
                M = G * decs[h]
                dM = dMcat[:, h * CH:(h + 1) * CH]
                W = dM * M
                dG = dG + dM * decs[h]
                Ms.append(M.astype(BF16))
                dA = dA + jnp.where(lanei == base + h, jnp.sum(W, axis=1, keepdims=True), 0.0)
                dAT = dAT + jnp.where(subi == base + h, jnp.sum(W, axis=0, keepdims=True), 0.0)
            dX = dX + _dot_tn(jnp.concatenate(Ms, axis=0), dYbd)
            dGb = dG.astype(BF16)
            dC_ref[:, nsl] = _dot(dGb, Bm) + _dot(dYsb, Hb)
            dB_ref[:, nsl] = _dot_tn(dGb, Cm) + _dot(eX.astype(BF16), dHnb)
            dh_scr[gi] = dec_rows * dHn + _dot_tn(dYsb, Cm)
            dA = dA + _dot_hilo_r(dY * Ys, E)
        else:
            dC_ref[:, nsl] = jnp.zeros((CH, NS), F32)
            dB_ref[:, nsl] = _dot(eX.astype(BF16), dHnb)
            dh_scr[gi] = dec_rows * dHn
        q = _dot_hilo_r(eX * BdH, E)
        r = jnp.sum(dHn * H, axis=1, keepdims=True)
        lane1 = lax.broadcasted_iota(jnp.int32, (1, DTW), 1)
        hdot = jnp.zeros((1, DTW), F32)
        for h in range(HPG):
            hv = jnp.sum(r[h * HD:(h + 1) * HD, :], axis=0, keepdims=True)
            hdot = hdot + jnp.where(lane1 == base + h, hv, 0.0)
        dllast = jnp.sum(q, axis=0, keepdims=True) + elast * hdot
        rowi = lax.broadcasted_iota(jnp.int32, (CH, DTW), 0)
        dxh_ref[:, xsl] = dX * dt_exp
        return dA - q + jnp.where(rowi == last, dllast, 0.0), dAT, _dot_hilo_r(dX * xh, E)

    small = pl.BlockSpec((None, CH, DTW), lambda b, g, sp: (g, row(b, g, sp), 0))
    return _hosted_call(
        body, xch, name="ssd_bwd_rev" if rev else "ssd_bwd", grid=(cfg.Bn, NG // GP, nch),
        in_specs=in_specs + [
            pl.BlockSpec((CH, GP * HPG * HD), lambda b, g, sp: (lat_row(b, g, sp), g)),
            pl.BlockSpec((None, GP, None, HPG * HD, NS), lambda b, g, sp: (b, g, nch - 1 - sp, 0, 0))],
        out_specs=[pl.BlockSpec((CH, GP * HPG * HD), lambda b, g, sp: (row(b, g, sp), g)),
                   pl.BlockSpec((CH, GP * NS), lambda b, g, sp: (row(b, g, sp), g)),
                   pl.BlockSpec((CH, GP * NS), lambda b, g, sp: (row(b, g, sp), g)),
                   small, small],
        out_shape=[jax.ShapeDtypeStruct((cfg.NT, DI), F32),
                   jax.ShapeDtypeStruct((cfg.NT, NG * NS), F32),
                   jax.ShapeDtypeStruct((cfg.NT, NG * NS), F32),
                   jax.ShapeDtypeStruct((NGB, cfg.NT, DTW), F32),
                   jax.ShapeDtypeStruct((NGB, cfg.NT, DTW), F32)],
        scratch_shapes=[pltpu.VMEM((GP, HPG * HD, NS), F32)],
        compiler_params=_cp(3), args=(act, act, act, dtg, cumg, cumTg, dy, hs))


def _shift_rows(v, s, fill, toward_later, rowi):
    n = v.shape[0]
    if toward_later:
        return jnp.where(rowi >= s, pltpu.roll(v, s, 0), fill)
    return jnp.where(rowi < n - s, pltpu.roll(v, n - s, 0), fill)


def _chunk_scan(a, b, carry, later):
    nt = a.shape[0] // 8
    rowi = lax.broadcasted_iota(jnp.int32, (8, a.shape[1]), 0)
    outs = [None] * nt
    for r in (range(nt) if later else range(nt - 1, -1, -1)):
        av = a[r * 8:(r + 1) * 8]
        bv = b[r * 8:(r + 1) * 8]
        for sh in (1, 2, 4):
            a_p = _shift_rows(av, sh, 1.0, later, rowi)
            b_p = _shift_rows(bv, sh, 0.0, later, rowi)
            bv = av * b_p + bv
            av = av * a_p
        h = bv + av * carry
        outs[r] = h
        carry = h[7:8] if later else h[0:1]
    return jnp.concatenate(outs, axis=0), carry


def _lru_gates(u, wa_ref, wi_ref, ba_ref, bi_ref, lam_ref):
    rs, is_ = [], []
    for k in range(LB):
        uk = u[:, k * LBW:(k + 1) * LBW].astype(BF16)
        rs.append(_dot(uk, wa_ref[k].astype(BF16)))
        is_.append(_dot(uk, wi_ref[k].astype(BF16)))
    r = _sigmoid(jnp.concatenate(rs, axis=1) + ba_ref[...])
    ig = _sigmoid(jnp.concatenate(is_, axis=1) + bi_ref[...])
    sp = _softplus(-lam_ref[...])
    la = -LRU_C * r * sp
    a = jnp.exp(la)
    g = jnp.sqrt((1.0 + a * a) * jnp.tanh(-la))
    return r, ig, sp, la, a, g


def _lru_w_specs(d):
    return [pl.BlockSpec((None, LB, LBW, LBW), lambda b, s: (d, 0, 0, 0)),
            pl.BlockSpec((None, LB, LBW, LBW), lambda b, s: (d, 0, 0, 0)),
            pl.BlockSpec((None, 1, LW), lambda b, s: (d, 0, 0)),
            pl.BlockSpec((None, 1, LW), lambda b, s: (d, 0, 0)),
            pl.BlockSpec((None, 1, LW), lambda b, s: (d, 0, 0))]


def _lru_block_of_step(cfg, rev):
    nbl = cfg.nbl
    if not rev:
        return lambda s: s
    return lambda s: jnp.where(s < 1, 0, 1 + nbl - s)


def _lru_fwd(cfg, act, wa, wi, ba, bi, lam, rev):
    nch, CH = cfg.nbt, cfg.TB
    cmap = _lru_block_of_step(cfg, rev)
    d = 1 if rev else 0
    ucol = (DI + 2 * NG * NS) // LW

    def body(u_ref, wa_ref, wi_ref, ba_ref, bi_ref, lam_ref, h_ref, c_scr):
        s = pl.program_id(1)

        @pl.when(s == 0)
        def _():
            c_scr[...] = jnp.zeros_like(c_scr)

        u = u_ref[...]
        r, ig, sp, la, a, g = _lru_gates(u, wa_ref, wi_ref, ba_ref, bi_ref, lam_ref)
        h, carry = _chunk_scan(a, g * ig * u, c_scr[0:1, :], not rev)
        h_ref[...] = h
        c_scr[0:1, :] = carry

    return pl.pallas_call(
        body, name="lru_fwd_rev" if rev else "lru_fwd", grid=(cfg.Bn, nch),
        in_specs=[pl.BlockSpec((CH, LW), lambda b, s: (b * nch + cmap(s), ucol))] + _lru_w_specs(d),
        out_specs=pl.BlockSpec((CH, LW), lambda b, s: (b * nch + cmap(s), 0)),
        out_shape=jax.ShapeDtypeStruct((cfg.NT, LW), F32),
        scratch_shapes=[pltpu.VMEM((8, LW), F32)],
        compiler_params=_cp(2),
    )(act, wa, wi, ba, bi, lam)


def _lru_bwd(cfg, act, wa, wi, ba, bi, lam, hd, dyl, rev):
    nch, nct, nlt, CH = cfg.nbt, 1, cfg.nbl, cfg.TB
    cmap = _lru_block_of_step(cfg, rev)
    d = 1 if rev else 0
    ucol = (DI + 2 * NG * NS) // LW

    def srow(b, sp):
        return b * nch + cmap(nch - 1 - sp)

    def prev_rows(b, sp):
        s = nch - 1 - sp
        cp = cmap(jnp.maximum(s - 1, 0))
        base = (b * nch + cp) * (CH // 8)
        return base + (0 if rev else CH // 8 - 1)

    def lat_row(b, sp):
        c = cmap(nch - 1 - sp)
        return b * nlt + jnp.maximum(c - nct, 0)

    def body(u_ref, wa_ref, wi_ref, ba_ref, bi_ref, lam_ref, h_ref, hp_ref, dy_ref,
             du_ref, dwa_ref, dwi_ref, vec_ref, c_scr):
        b = pl.program_id(0)
        sp_id = pl.program_id(1)
        s = nch - 1 - sp_id

        @pl.when(sp_id == 0)
        def _():
            c_scr[...] = jnp.zeros_like(c_scr)

        @pl.when(jnp.logical_and(b == 0, sp_id == 0))
        def _():
            dwa_ref[...] = jnp.zeros_like(dwa_ref)
            dwi_ref[...] = jnp.zeros_like(dwi_ref)
            vec_ref[...] = jnp.zeros_like(vec_ref)

        c = cmap(s)
        u = u_ref[...]
        r, ig, spl, la, a, g = _lru_gates(u, wa_ref, wi_ref, ba_ref, bi_ref, lam_ref)
        dh = jnp.where(c < nct, 0.0, dy_ref[...])
        rowi = lax.broadcasted_iota(jnp.int32, (CH, LW), 0)
        lamv, _ = _chunk_scan(_shift_rows(a, 1, 1.0, rev, rowi), dh, c_scr[0:1, :], rev)
        first = CH - 1 if rev else 0
        c_scr[0:1, :] = (a * lamv)[first:first + 1, :]
        hprow = hp_ref[...][(0 if rev else 7):(1 if rev else 8), :]
        hprow = jnp.where(s > 0, hprow, 0.0)
        h_prev = _shift_rows(h_ref[...], 1, hprow, not rev, rowi)
        da = lamv * h_prev
        db = lamv
        iu = ig * u
        dla = da * a - db * iu * (a * a) / g
        dr = dla * (-LRU_C * spl)
        di = db * g * u
        du = db * g * ig
        drp = dr * r * (1.0 - r)
        dip = di * ig * (1.0 - ig)
        dus = []
        for k in range(LB):
            sl = slice(k * LBW, (k + 1) * LBW)
            drk = drp[:, sl].astype(BF16)
            dik = dip[:, sl].astype(BF16)
            uk = u[:, sl].astype(BF16)
            dus.append(_dot_nt(drk, wa_ref[k].astype(BF16)) + _dot_nt(dik, wi_ref[k].astype(BF16)))
            dwa_ref[k] += _dot_tn(uk, drk)
            dwi_ref[k] += _dot_tn(uk, dik)
        du_ref[...] = du + jnp.concatenate(dus, axis=1)
        vec_ref[0:1, :] += jnp.sum(drp, axis=0, keepdims=True)
        vec_ref[1:2, :] += jnp.sum(dip, axis=0, keepdims=True)
        dsp = jnp.sum(dla * (-LRU_C * r), axis=0, keepdims=True)
        vec_ref[2:3, :] += dsp * (-_sigmoid(-lam_ref[...]))

    return pl.pallas_call(
        body, name="lru_bwd_rev" if rev else "lru_bwd", grid=(cfg.Bn, nch),
        in_specs=[pl.BlockSpec((CH, LW), lambda b, sp: (srow(b, sp), ucol))] + _lru_w_specs(d) + [
            pl.BlockSpec((CH, LW), lambda b, sp: (srow(b, sp), 0)),
            pl.BlockSpec((8, LW), lambda b, sp: (prev_rows(b, sp), 0)),
            pl.BlockSpec((CH, LW), lambda b, sp: (lat_row(b, sp), 0))],
        out_specs=[pl.BlockSpec((CH, LW), lambda b, sp: (srow(b, sp), 0)),
                   pl.BlockSpec((LB, LBW, LBW), lambda b, sp: (0, 0, 0)),
                   pl.BlockSpec((LB, LBW, LBW), lambda b, sp: (0, 0, 0)),
                   pl.BlockSpec((8, LW), lambda b, sp: (0, 0))],
        out_shape=[jax.ShapeDtypeStruct((cfg.NT, LW), F32),
                   jax.ShapeDtypeStruct((LB, LBW, LBW), F32),
                   jax.ShapeDtypeStruct((LB, LBW, LBW), F32),
                   jax.ShapeDtypeStruct((8, LW), F32)],
        scratch_shapes=[pltpu.VMEM((8, LW), F32)],
        compiler_params=_cp(2),
    )(act, wa, wi, ba, bi, lam, hd, hd, dyl)


HB = 1024


def _post_ssd_fwd(cfg, y, proj, norm_w):
    TB, nbt, nbl = cfg.TB, cfg.nbt, cfg.nbl
    zc = CONVW // HB

    def body(y_ref, z_ref, w_ref, o_ref):
        u = y_ref[...] * _silu(z_ref[...])
        for gi in range(HB // (DI // NG)):
            sl = slice(gi * 256, (gi + 1) * 256)
            ug = u[:, sl]
            rs = lax.rsqrt(jnp.mean(ug * ug, axis=1, keepdims=True) + RMS_EPS)
            o_ref[:, sl] = (ug * rs * w_ref[:, sl]).astype(BF16)

    def st(b, j, cb):
        return (b * nbt + 1 + j, cb)
    return pl.pallas_call(
        body, name="post_ssd_fwd", grid=(cfg.Bn, nbl, DI // HB),
        in_specs=[pl.BlockSpec((TB, HB), st),
                  pl.BlockSpec((TB, HB), lambda b, j, cb: (b * nbt + 1 + j, zc + cb)),
                  pl.BlockSpec((1, HB), lambda b, j, cb: (0, cb))],
        out_specs=pl.BlockSpec((TB, HB), lambda b, j, cb: (b * nbl + j, cb)),
        out_shape=jax.ShapeDtypeStruct((cfg.N, DI), BF16),
        compiler_params=_cp(3),
    )(y, proj, norm_w)


def _post_ssd_bwd(cfg, dproj, dn, y, act, proj, norm_w):
    TB, nbt, nbl = cfg.TB, cfg.nbt, cfg.nbl
    zc = CONVW // HB

    def body(_, dn_ref, y_ref, xh_ref, z_ref, w_ref, dy_ref, dz_ref, acc_ref):
        b = pl.program_id(1)
        j = pl.program_id(2)

        @pl.when(jnp.logical_and(b == 0, j == 0))
        def _():
            acc_ref[...] = jnp.zeros_like(acc_ref)

        @pl.when(j == 0)
        def _():
            dz_ref[...] = jnp.zeros_like(dz_ref)

        @pl.when(j > 0)
        def _():
            latent(dn_ref, y_ref, xh_ref, z_ref, w_ref, dy_ref, dz_ref, acc_ref)

    def latent(dn_ref, y_ref, xh_ref, z_ref, w_ref, dy_ref, dz_ref, acc_ref):
        xh = xh_ref[...]
        z = z_ref[...]
        y = y_ref[...]
        sz = _silu(z)
        u = y * sz
        dout = dn_ref[...]
        for gi in range(HB // (DI // NG)):
            sl = slice(gi * 256, (gi + 1) * 256)
            ug0 = u[:, sl]
            rs = lax.rsqrt(jnp.mean(ug0 * ug0, axis=1, keepdims=True) + RMS_EPS)
            ug = ug0 * rs
            do = dout[:, sl]
            acc_ref[0:1, sl] += jnp.sum(do * ug, axis=0, keepdims=True)
            dug = do * w_ref[:, sl]
            du = rs * (dug - ug * jnp.mean(dug * ug, axis=1, keepdims=True))
            dy = du * sz[:, sl]
            dy_ref[:, sl] = dy
            dz_ref[:, sl] = (du * y[:, sl] * _silu_grad(z[:, sl])).astype(BF16)
            acc_ref[1:2, sl] += jnp.sum(dy * xh[:, sl], axis=0, keepdims=True)

    def st(cb, b, j):
        return (b * nbt + j, cb)

    def la(cb, b, j):
        return (b * nbl + jnp.maximum(j - 1, 0), cb)
    return pl.pallas_call(
        body, name="post_ssd_bwd", grid=(DI // HB, cfg.Bn, nbt),
        in_specs=[_ANY, pl.BlockSpec((TB, HB), la), pl.BlockSpec((TB, HB), st), pl.BlockSpec((TB, HB), st),
                  pl.BlockSpec((TB, HB), lambda cb, b, j: (b * nbt + j, zc + cb)),
                  pl.BlockSpec((1, HB), lambda cb, b, j: (0, cb))],
        out_specs=[pl.BlockSpec((TB, HB), la),
                   pl.BlockSpec((TB, HB), lambda cb, b, j: (b * nbt + j, zc + cb)),
                   pl.BlockSpec((8, HB), lambda cb, b, j: (0, cb))],
        out_shape=[jax.ShapeDtypeStruct((cfg.N, DI), F32), jax.ShapeDtypeStruct((cfg.NT, PM), BF16),
                   jax.ShapeDtypeStruct((8, DI), F32)],
        input_output_aliases={0: 1},
        compiler_params=_cp(3),
    )(dproj, dn, y, act, proj, norm_w)


def _post_lru_fwd(cfg, hf, hb, proj):
    TB, nbt, nbl = cfg.TB, cfg.nbt, cfg.nbl
    gc = (CONVW + DI) // HB

    def body(hf_ref, hb_ref, g_ref, o_ref):
        o_ref[...] = ((hf_ref[...] + hb_ref[...]) * _gelu(g_ref[...])).astype(BF16)

    st = pl.BlockSpec((TB, HB), lambda b, j: (b * nbt + 1 + j, 0))
    return pl.pallas_call(
        body, name="post_lru_fwd", grid=(cfg.Bn, nbl),
        in_specs=[st, st, pl.BlockSpec((TB, HB), lambda b, j: (b * nbt + 1 + j, gc))],
        out_specs=pl.BlockSpec((TB, HB), lambda b, j: (b * nbl + j, 0)),
        out_shape=jax.ShapeDtypeStruct((cfg.N, LW), BF16),
        compiler_params=_cp(2),
    )(hf, hb, proj)


def _post_lru_bwd(cfg, dproj, dv, hf, hb, proj):
    TB, nbt, nbl = cfg.TB, cfg.nbt, cfg.nbl
    gc = (CONVW + DI) // HB

    def body(_, dv_ref, hf_ref, hb_ref, g_ref, dy_ref, dg_ref):
        j = pl.program_id(1)

        @pl.when(j == 0)
        def _():
            dg_ref[...] = jnp.zeros_like(dg_ref)

        @pl.when(j > 0)
        def _():
            gt = g_ref[...]
            dvv = dv_ref[...]
            dy_ref[...] = dvv * _gelu(gt)
            dg_ref[...] = (dvv * (hf_ref[...] + hb_ref[...]) * _gelu_grad(gt)).astype(BF16)

    st = pl.BlockSpec((TB, HB), lambda b, j: (b * nbt + j, 0))
    la = pl.BlockSpec((TB, HB), lambda b, j: (b * nbl + jnp.maximum(j - 1, 0), 0))
    gcol = pl.BlockSpec((TB, HB), lambda b, j: (b * nbt + j, gc))
    return pl.pallas_call(
        body, name="post_lru_bwd", grid=(cfg.Bn, nbt),
        in_specs=[_ANY, la, st, st, gcol],
        out_specs=[la, gcol],
        out_shape=[jax.ShapeDtypeStruct((cfg.N, LW), F32), jax.ShapeDtypeStruct((cfg.NT, PM), BF16)],
        input_output_aliases={0: 1},
        compiler_params=_cp(2),
    )(dproj, dv, hf, hb, proj)


def _merge_fwd(cfg, proj, b_gate, br_ssd, br_lru):
    TB, nbt, nbl = cfg.TB, cfg.nbt, cfg.nbl
    mc = (CONVW + DI + LW) // HB

    def body(ms_ref, ml_ref, bg_ref, bs_ref, bl_ref, o_ref):
        gs = _sigmoid(ms_ref[...] + bg_ref[:, :D])
        gl = _sigmoid(ml_ref[...] + bg_ref[:, D:])
        o_ref[...] = (gs * bs_ref[...] + gl * bl_ref[...]).astype(BF16)

    la = pl.BlockSpec((TB, D), lambda b, j: (b * nbl + j, 0))
    return pl.pallas_call(
        body, name="merge_fwd", grid=(cfg.Bn, nbl),
        in_specs=[pl.BlockSpec((TB, HB), lambda b, j: (b * nbt + 1 + j, mc)),
                  pl.BlockSpec((TB, HB), lambda b, j: (b * nbt + 1 + j, mc + 1)),
                  pl.BlockSpec((1, 2 * D), lambda b, j: (0, 0)), la, la],
        out_specs=la,
        out_shape=jax.ShapeDtypeStruct((cfg.N, D), BF16),
        compiler_params=_cp(2),
    )(proj, proj, b_gate, br_ssd, br_lru)


def _merge_bwd(cfg, dmix, proj, b_gate, br_ssd, br_lru):
    TB, nbt, nbl = cfg.TB, cfg.nbt, cfg.nbl
    mc = (CONVW + DI + LW) // HB

    def body(dm_ref, ms_ref, ml_ref, bg_ref, bs_ref, bl_ref, ds_ref, dl_ref, dmg_ref, acc_ref):
        b = pl.program_id(0)
        j = pl.program_id(1)

        @pl.when(jnp.logical_and(b == 0, j == 0))
        def _():
            acc_ref[...] = jnp.zeros_like(acc_ref)

        @pl.when(j == 0)
        def _():
            dmg_ref[...] = jnp.zeros_like(dmg_ref)

        @pl.when(j > 0)
        def _():
            latent(dm_ref, ms_ref, ml_ref, bg_ref, bs_ref, bl_ref, ds_ref, dl_ref, dmg_ref, acc_ref)

    def latent(dm_ref, ms_ref, ml_ref, bg_ref, bs_ref, bl_ref, ds_ref, dl_ref, dmg_ref, acc_ref):
        dm = dm_ref[...]
        gs = _sigmoid(ms_ref[...] + bg_ref[:, :D])
        gl = _sigmoid(ml_ref[...] + bg_ref[:, D:])
        ds_ref[...] = (dm * gs).astype(BF16)
        dl_ref[...] = (dm * gl).astype(BF16)
        dps = dm * bs_ref[...] * gs * (1.0 - gs)
        dpl = dm * bl_ref[...] * gl * (1.0 - gl)
        dmg_ref[:, :D] = dps.astype(BF16)
        dmg_ref[:, D:] = dpl.astype(BF16)
        acc_ref[0:1, :D] += jnp.sum(dps, axis=0, keepdims=True)
        acc_ref[0:1, D:] += jnp.sum(dpl, axis=0, keepdims=True)

    la = pl.BlockSpec((TB, D), lambda b, j: (b * nbl + jnp.maximum(j - 1, 0), 0))
    return pl.pallas_call(
        body, name="merge_bwd", grid=(cfg.Bn, nbt),
        in_specs=[la, pl.BlockSpec((TB, HB), lambda b, j: (b * nbt + j, mc)),
                  pl.BlockSpec((TB, HB), lambda b, j: (b * nbt + j, mc + 1)),
                  pl.BlockSpec((1, 2 * D), lambda b, j: (0, 0)), la, la],
        out_specs=[la, la, pl.BlockSpec((TB, 2 * D), lambda b, j: (b * nbt + j, mc // 2)),
                   pl.BlockSpec((8, 2 * D), lambda b, j: (0, 0))],
        out_shape=[jax.ShapeDtypeStruct((cfg.N, D), BF16), jax.ShapeDtypeStruct((cfg.N, D), BF16),
                   jax.ShapeDtypeStruct((cfg.NT, PM), BF16), jax.ShapeDtypeStruct((8, 2 * D), F32)],
        compiler_params=_cp(2),
    )(dmix, proj, proj, b_gate, br_ssd, br_lru)


def _resid1_fwd(cfg, xs, x_mix, gate1, shift2, scale2, ln1_g, ln1_b):
    TB, nbt, nbl = cfg.TB, cfg.nbt, cfg.nbl

    def body(x_ref, xm_ref, g1_ref, sh_ref, sc_ref, lg_ref, lb_ref, x1_ref, h2_ref):
        r1 = ALPHA * x_ref[...] + g1_ref[...] * xm_ref[...]
        xh, _ = _ln(r1)
        x1 = xh * lg_ref[...] + lb_ref[...]
        x1_ref[...] = x1
        xh2, _ = _ln(x1)
        h2_ref[...] = (xh2 * (1.0 + sc_ref[...]) + sh_ref[...]).astype(BF16)

    la = pl.BlockSpec((TB, D), lambda b, j: (b * nbl + j, 0))
    ex = pl.BlockSpec((None, 1, D), lambda b, j: (b, 0, 0))
    vec = pl.BlockSpec((1, D), lambda b, j: (0, 0))
    return pl.pallas_call(
        body, name="resid1_fwd", grid=(cfg.Bn, nbl),
        in_specs=[pl.BlockSpec((TB, D), lambda b, j: (b * nbt + 1 + j, 0)), la, ex, ex, ex, vec, vec],
        out_specs=[la, la],
        out_shape=[jax.ShapeDtypeStruct((cfg.N, D), F32), jax.ShapeDtypeStruct((cfg.N, D), BF16)],
        compiler_params=_cp(2),
    )(xs, x_mix, gate1, shift2, scale2, ln1_g, ln1_b)


def _resid1_bwd(cfg, dh2, dx1p, x1, xs, x_mix, gate1, scale2, ln1_g):
    TB, nbt, nbl = cfg.TB, cfg.nbt, cfg.nbl

    def body(dh2_ref, dx1p_ref, x1_ref, x_ref, xm_ref, g1_ref, sc_ref, lg_ref,
             dxm_ref, dxp_ref, ex_ref, gl_ref):
        b = pl.program_id(0)
        j = pl.program_id(1)

        @pl.when(j == 0)
        def _():
            ex_ref[...] = jnp.zeros_like(ex_ref)

        @pl.when(jnp.logical_and(b == 0, j == 0))
        def _():
            gl_ref[...] = jnp.zeros_like(gl_ref)

        dh2 = dh2_ref[...]
        xh2, rs2 = _ln(x1_ref[...])
        ex_ref[0:1, :] += jnp.sum(dh2, axis=0, keepdims=True)
        ex_ref[1:2, :] += jnp.sum(dh2 * xh2, axis=0, keepdims=True)
        dx1 = dx1p_ref[...] + _ln_bwd(dh2 * (1.0 + sc_ref[...]), xh2, rs2)
        xm = xm_ref[...]
        g1 = g1_ref[...]
        r1 = ALPHA * x_ref[...] + g1 * xm
        xh1, rs1 = _ln(r1)
        gl_ref[0:1, :] += jnp.sum(dx1 * xh1, axis=0, keepdims=True)
        gl_ref[1:2, :] += jnp.sum(dx1, axis=0, keepdims=True)
        dr1 = _ln_bwd(dx1 * lg_ref[...], xh1, rs1)
        ex_ref[2:3, :] += jnp.sum(dr1 * xm, axis=0, keepdims=True)
        dxm_ref[...] = (dr1 * g1).astype(BF16)
        dxp_ref[...] = ALPHA * dr1

    la = pl.BlockSpec((TB, D), lambda b, j: (b * nbl + j, 0))
    ex = pl.BlockSpec((None, 1, D), lambda b, j: (b, 0, 0))
    vec = pl.BlockSpec((1, D), lambda b, j: (0, 0))
    return pl.pallas_call(
        body, name="resid1_bwd", grid=(cfg.Bn, nbl),
        in_specs=[la, la, la, pl.BlockSpec((TB, D), lambda b, j: (b * nbt + 1 + j, 0)), la, ex, ex, vec],
        out_specs=[la, la, pl.BlockSpec((None, 8, D), lambda b, j: (b, 0, 0)),
                   pl.BlockSpec((8, D), lambda b, j: (0, 0))],
        out_shape=[jax.ShapeDtypeStruct((cfg.N, D), BF16), jax.ShapeDtypeStruct((cfg.N, D), F32),
                   jax.ShapeDtypeStruct((cfg.Bn, 8, D), F32), jax.ShapeDtypeStruct((8, D), F32)],
        compiler_params=_cp(2),
    )(dh2, dx1p, x1, xs, x_mix, gate1, scale2, ln1_g)


def _final_fwd_bwd(cfg, x1, mlp, b2, gate2, ln2_g, ln2_b, target):
    TB, nbl = cfg.TB, cfg.nbl

    def body(x1_ref, m_ref, b2_ref, g2_ref, lg_ref, lb_ref, t_ref, dm_ref, dx_ref, ex_ref, gl_ref):
        b = pl.program_id(0)
        j = pl.program_id(1)

        @pl.when(j == 0)
        def _():
            ex_ref[...] = jnp.zeros_like(ex_ref)

        @pl.when(jnp.logical_and(b == 0, j == 0))
        def _():
            gl_ref[...] = jnp.zeros_like(gl_ref)

        mv = m_ref[...] + b2_ref[...]
        g2 = g2_ref[...]
        r2 = ALPHA * x1_ref[...] + g2 * mv
        xh, rs = _ln(r2)
        lg = lg_ref[...]
        x2 = xh * lg + lb_ref[...]
        err = x2 - t_ref[...]
        ls = jnp.sum(jnp.sum(err * err, axis=1, keepdims=True), axis=0, keepdims=True) * (0.5 / D)
        gl_ref[3:4, :] += ls
        dx2 = err * (1.0 / D)
        gl_ref[0:1, :] += jnp.sum(dx2 * xh, axis=0, keepdims=True)
        gl_ref[1:2, :] += jnp.sum(dx2, axis=0, keepdims=True)
        dr2 = _ln_bwd(dx2 * lg, xh, rs)
        ex_ref[0:1, :] += jnp.sum(dr2 * mv, axis=0, keepdims=True)
        dmv = dr2 * g2
        gl_ref[2:3, :] += jnp.sum(dmv, axis=0, keepdims=True)
        dm_ref[...] = dmv.astype(BF16)
        dx_ref[...] = ALPHA * dr2

    la = pl.BlockSpec((TB, D), lambda b, j: (b * nbl + j, 0))
    ex = pl.BlockSpec((None, 1, D), lambda b, j: (b, 0, 0))
    vec = pl.BlockSpec((1, D), lambda b, j: (0, 0))
    return pl.pallas_call(
        body, name="final_fwd_bwd", grid=(cfg.Bn, nbl),
        in_specs=[la, la, vec, ex, vec, vec, la],
        out_specs=[la, la, pl.BlockSpec((None, 8, D), lambda b, j: (b, 0, 0)),
                   pl.BlockSpec((8, D), lambda b, j: (0, 0))],
        out_shape=[jax.ShapeDtypeStruct((cfg.N, D), BF16), jax.ShapeDtypeStruct((cfg.N, D), F32),
                   jax.ShapeDtypeStruct((cfg.Bn, 8, D), F32), jax.ShapeDtypeStruct((8, D), F32)],
        compiler_params=_cp(2),
    )(x1, mlp, b2, gate2, ln2_g, ln2_b, target)


def _ln_mod_bwd(cfg, dh_a, dh_b, xs, scale_tab, dxp):
    TB, nbt, nbl = cfg.TB, cfg.nbt, cfg.nbl

    def body(da_ref, db_ref, x_ref, sc_ref, dxp_ref, gx_ref, acc_ref):
        j = pl.program_id(1)

        @pl.when(j <= 1)
        def _():
            acc_ref[...] = jnp.zeros_like(acc_ref)

        dh = da_ref[...] + db_ref[...]
        xhat, rs = _ln(x_ref[...])
        acc_ref[0:1, :] += jnp.sum(dh, axis=0, keepdims=True)
        acc_ref[1:2, :] += jnp.sum(dh * xhat, axis=0, keepdims=True)
        gx_ref[...] = dxp_ref[...] + _ln_bwd(dh * (1.0 + sc_ref[...]), xhat, rs)

    st = pl.BlockSpec((TB, D), lambda b, j: (b * nbt + j, 0))
    la = pl.BlockSpec((TB, D), lambda b, j: (b * nbl + jnp.maximum(j - 1, 0), 0))
    return pl.pallas_call(
        body, name="ln_mod_bwd", grid=(cfg.Bn, nbt),
        in_specs=[st, st, st,
                  pl.BlockSpec((None, 1, D), lambda b, j: (2 * b + jnp.minimum(j, 1), 0, 0)), la],
        out_specs=[la, pl.BlockSpec((None, 8, D), lambda b, j: (2 * b + jnp.minimum(j, 1), 0, 0))],
        out_shape=[jax.ShapeDtypeStruct((cfg.N, D), F32), jax.ShapeDtypeStruct((2 * cfg.Bn, 8, D), F32)],
        compiler_params=_cp(2),
    )(dh_a, dh_b, xs, scale_tab, dxp)


def _perm_w_in(w_in):
    w_main = jnp.concatenate([w_in[:, 0:3072], w_in[:, 4160:5184], w_in[:, 3136:4160], w_in[:, 5184:10304]], axis=1)
    w_dt = jnp.pad(w_in[:, 3072:3136], ((0, 0), (0, DTW - 2 * NH)))
    return w_main, w_dt


def _unperm_w_in(dw_main, dw_dt):
    return jnp.concatenate([dw_main[:, 0:3072], dw_dt[:, :2 * NH], dw_main[:, 4096:5120],
                            dw_main[:, 3072:4096], dw_main[:, 5120:]], axis=1)


def _unpack_rest(rest_all):
    out, off = {}, 0
    for n, shp, axis in _BIG[1:]:
        shard_shape = (shp[0] // NDEV, shp[1]) if axis == 0 else (shp[0], shp[1] // NDEV)
        r = math.prod(shard_shape) // 1024
        out[n] = _from_slots(rest_all[:, off:off + r, :].reshape((NDEV,) + shard_shape), axis)
        off += r
    return out


def _local_step(cfg, x, ctx, target, m, mc, W, rest_payload):
    Bn, T, Tc = cfg.Bn, cfg.T, cfg.Tc
    NT, N = cfg.NT, cfg.N
    xs = jnp.concatenate([ctx, x], axis=1).reshape(NT, D)
    mch = [m[:, i * D:(i + 1) * D] for i in range(NMOD)]
    ctx_sh = jnp.broadcast_to(mc[None, :D], (Bn, D))
    ctx_sc = jnp.broadcast_to(mc[None, D:], (Bn, D))
    shift_tab = jnp.stack([ctx_sh, mch[0]], axis=1).reshape(2 * Bn, 1, D)
    scale_tab = jnp.stack([ctx_sc, mch[1]], axis=1).reshape(2 * Bn, 1, D)
    gate1 = mch[2].reshape(Bn, 1, D)
    shift2 = mch[3].reshape(Bn, 1, D)
    scale2 = mch[4].reshape(Bn, 1, D)
    gate2 = mch[5].reshape(Bn, 1, D)

    conv_w = jnp.concatenate([W["ssd_conv_w"], W["lru_conv_w"]], axis=1)
    conv_b = jnp.concatenate([W["ssd_conv_b"], W["lru_conv_b"]], axis=1)
    dt_bias = jnp.pad(W["ssd_dt_bias"].reshape(1, 2 * NH), ((0, 0), (0, DTW - 2 * NH)))
    a_log = jnp.pad(W["ssd_a_log"].reshape(1, 2 * NH), ((0, 0), (0, DTW - 2 * NH)))
    dvec = jnp.repeat(W["ssd_d"].reshape(NH), HD).reshape(1, DI)
    lba = W["lru_ba"].reshape(2, 1, LW)
    lbi = W["lru_bi"].reshape(2, 1, LW)
    llam = W["lru_lambda"].reshape(2, 1, LW)

    h = _ln_mod_fwd(cfg, xs, shift_tab, scale_tab)
    proj, rest_all = _mm(h, W["w_main"], "nn", "mm_proj", tm=1024, tn=1024, tk=1024, xch=(rest_payload, True))
    W = dict(W, **_unpack_rest(rest_all))
    dt_raw = _mm(h, W["w_dt"], "nn", "mm_dt", tm=512, tn=DTW, tk=1024)
    dt, dtg, cumg, cumTg = _dt_fwd(cfg, dt_raw, dt_bias, a_log)
    act, sgrad = _conv_fwd(cfg, proj, conv_w, conv_b)
    y_f, hs_f = _ssd_fwd(cfg, act, dtg, cumg, cumTg, False)
    y, hs_b = _ssd_fwd(cfg, act, dtg, cumg, cumTg, True, y_other=y_f, dvec=dvec)
    hss = [hs_f, hs_b]
    hls = [_lru_fwd(cfg, act, W["lru_wa"], W["lru_wi"], lba, lbi, llam, rev) for rev in (False, True)]
    nssd = _post_ssd_fwd(cfg, y, proj, W["ssd_norm_w"])
    vlru = _post_lru_fwd(cfg, hls[0], hls[1], proj)
    br_ssd = _mm(nssd, W["w_br_ssd"], "nn", "mm_br_ssd", tm=1024, tn=1024, tk=1024)
    br_lru = _mm(vlru, W["w_br_lru"], "nn", "mm_br_lru", tm=1024, tn=1024, tk=1024)
    mix = _merge_fwd(cfg, proj, W["b_gate"], br_ssd, br_lru)
    x_mix = _mm(mix, W["w_out"], "nn", "mm_out", tm=1024, tn=1024, tk=1024)
    x1, h2 = _resid1_fwd(cfg, xs, x_mix, gate1, shift2, scale2, W["ln1_g"], W["ln1_b"])
    a1, actm = _mm_mlp1(h2, W["w_mlp1"], W["b_mlp1"])
    mlp = _mm(actm, W["w_mlp2"], "nn", "mm_mlp2", tm=1024, tn=1024, tk=1024)
    dmlp, dx1p, ex2, gl2 = _final_fwd_bwd(cfg, x1, mlp, W["b_mlp2"], gate2, W["ln2_g"], W["ln2_b"],
                                          target.reshape(N, D))

    g = {}
    g["ln2_g"], g["ln2_b"], g["b_mlp2"] = gl2[0:1], gl2[1:2], gl2[2:3]
    loss_partial = gl2[3, 0]
    gw = {}
    gw["w_mlp2"] = _mm(actm, dmlp, "tn", "mm_dw_mlp2", BF16, tm=1024, tn=1024, tk=512)
    da1, accb1 = _mm_dact(dmlp, W["w_mlp2"], a1)
    g["b_mlp1"] = accb1[0:1]
    dh2 = _mm(da1, W["w_mlp1"], "nt", "mm_dh2", tm=1024, tn=1024, tk=1024)
    gw["w_mlp1"] = _mm(h2, da1, "tn", "mm_dw_mlp1", BF16, tm=1024, tn=1024, tk=512)
    dx_mix, dxp, ex1, gl1 = _resid1_bwd(cfg, dh2, dx1p, x1, xs, x_mix, gate1, scale2, W["ln1_g"])
    g["ln1_g"], g["ln1_b"] = gl1[0:1], gl1[1:2]
    dmix = _mm(dx_mix, W["w_out"], "nt", "mm_dmix", tm=1024, tn=1024, tk=1024)
    gw["w_out"] = _mm(mix, dx_mix, "tn", "mm_dw_out", BF16, tm=1024, tn=1024, tk=512)
    dbs, dbl, dproj, accg = _merge_bwd(cfg, dmix, proj, W["b_gate"], br_ssd, br_lru)
    g["b_gate"] = accg[0:1]
    dnssd = _mm(dbs, W["w_br_ssd"], "nt", "mm_dnssd", tm=1024, tn=1024, tk=1024)
    gw["w_br_ssd"] = _mm(nssd, dbs, "tn", "mm_dw_br_ssd", BF16, tm=1024, tn=1024, tk=512)
    dvlru = _mm(dbl, W["w_br_lru"], "nt", "mm_dvlru", tm=1024, tn=1024, tk=1024)
    gw["w_br_lru"] = _mm(vlru, dbl, "tn", "mm_dw_br_lru", BF16, tm=1024, tn=1024, tk=512)
    dy, dproj, accs = _post_ssd_bwd(cfg, dproj, dnssd, y, act, proj, W["ssd_norm_w"])
    g["ssd_norm_w"] = accs[0:1]
    dD_cols = accs[1:2]
    dyl, dproj = _post_lru_bwd(cfg, dproj, dvlru, hls[0], hls[1], proj)

    rest_slots = jnp.concatenate([_to_slots(gw[n], axis).reshape(NDEV, -1, 1024) for n, _, axis in _BIG[1:]], axis=1)
    xres = {}
    dxh, dBs, dCs, dAs, dxxs, dus = [], [], [], [], [], []
    dwas, dwis, lvecs = [], [], []
    for i, rev in enumerate((False, True)):
        if i == 0:
            o, xres["rs_rest"] = _ssd_bwd(cfg, act, dtg, cumg, cumTg, hss[i], dy, rev, xch=(rest_slots, False))
        else:
            o = _ssd_bwd(cfg, act, dtg, cumg, cumTg, hss[i], dy, rev)
        dxh.append(o[0]); dBs.append(o[1]); dCs.append(o[2]); dAs.append(o[3]); dxxs.append(o[4])
        du, dwa, dwi, lv = _lru_bwd(cfg, act, W["lru_wa"], W["lru_wi"], lba, lbi, llam, hls[i], dyl, rev)
        dus.append(du); dwas.append(dwa); dwis.append(dwi); lvecs.append(lv)
    lru_payload = jnp.stack([jnp.stack(dwas), jnp.stack(dwis)]).reshape(-1, 1024)
    g["lru_ba"] = jnp.stack([lvecs[0][0], lvecs[1][0]])
    g["lru_bi"] = jnp.stack([lvecs[0][1], lvecs[1][1]])
    g["lru_lambda"] = jnp.stack([lvecs[0][2], lvecs[1][2]])

    ddt_raw, accdt = _dt_bwd(cfg, dAs, dxxs, dt_raw, dt, dt_bias, a_log)
    g["ssd_a_log"] = accdt[0, :2 * NH].reshape(2, NH)
    g["ssd_dt_bias"] = accdt[1, :2 * NH].reshape(2, NH)

    (dproj, accx), xres["ag_lru"] = _conv_bwd(cfg, "conv_bwd_x", dproj, proj, conv_w, sgrad, [dxh[0], dxh[1]], 0, DI,
                                              skip=(dy, dvec), xch=(lru_payload, True))
    dproj, accB = _conv_bwd(cfg, "conv_bwd_b", dproj, proj, conv_w, sgrad, [dBs[0], dBs[1]], DI, NG * NS)
    dproj, accC = _conv_bwd(cfg, "conv_bwd_c", dproj, proj, conv_w, sgrad, [dCs[0], dCs[1]], DI + NG * NS, NG * NS)
    dproj, accl = _conv_bwd(cfg, "conv_bwd_lru", dproj, proj, conv_w, None, [dus[0], dus[1]], DI + 2 * NG * NS, LW)
    accssd = jnp.concatenate([accx, accB, accC], axis=1)
    g["ssd_conv_w"], g["ssd_conv_b"] = accssd[0:4], accssd[4:5]
    g["lru_conv_w"], g["lru_conv_b"] = accl[0:4], accl[4:5]
    dw_main = _mm(h, dproj, "tn", "mm_dw_main", BF16, tm=1024, tn=2048, tk=512)
    dw_dt = _mm(h, ddt_raw, "tn", "mm_dw_dt", BF16, tm=1024, tn=DTW, tk=512)
    w_in_slots = _to_slots(_unperm_w_in(dw_main, dw_dt), 1)
    dh_a, xres["rs_w_in"] = _mm(dproj, W["w_main"], "nt", "mm_dh_main", tm=1024, tn=1024, tk=1024,
                                xch=(w_in_slots, False))
    dh_b = _mm(ddt_raw, W["w_dt"], "nt", "mm_dh_dt", tm=512, tn=1024, tk=DTW)
    grad_x, acct = _ln_mod_bwd(cfg, dh_a, dh_b, xs, scale_tab, dxp)
    acct = acct.reshape(Bn, 2, 8, D)
    dm = jnp.concatenate([acct[:, 1, 0], acct[:, 1, 1], ex1[:, 2], ex1[:, 0], ex1[:, 1], ex2[:, 0]], axis=1)
    dmc = jnp.concatenate([acct[:, 0, 0], acct[:, 0, 1]], axis=1)
    g["ssd_d_cols"] = dD_cols
    return loss_partial, grad_x.reshape(Bn, T, D), g, dm, dmc, xres


MESH = pl.DeviceIdType.MESH
_HBM = pl.BlockSpec(memory_space=pltpu.HBM)


def _me():
    return 4 * lax.axis_index("x") + 2 * lax.axis_index("y") + lax.axis_index("c")


def _peer(k):
    px = (lax.axis_index("x") + ((k >> 2) & 1)) % 2
    py = (lax.axis_index("y") + ((k >> 1) & 1)) % 2
    pc = (lax.axis_index("c") + (k & 1)) % 2
    return (px, py, pc), 4 * px + 2 * py + pc


def _xchg_copies(x_ref, o_ref, send_sems, recv_sems, loc_sem, gather):
    me = _me()
    src_me = x_ref if gather else x_ref.at[me]
    loc = pltpu.make_async_copy(src_me, o_ref.at[me], loc_sem)
    sends, recvs = [], []
    for k in range(1, NDEV):
        peer, pid = _peer(k)
        sends.append(pltpu.make_async_remote_copy(
            src_ref=x_ref if gather else x_ref.at[pid], dst_ref=o_ref.at[me],
            send_sem=send_sems.at[k - 1], recv_sem=recv_sems.at[k - 1],
            device_id=peer, device_id_type=MESH))
        recvs.append(pltpu.make_async_remote_copy(
            src_ref=src_me, dst_ref=o_ref.at[pid],
            send_sem=send_sems.at[k - 1], recv_sem=recv_sems.at[k - 1],
            device_id=peer, device_id_type=MESH))
    return loc, sends, recvs


def _xchg_start(*refs, gather):
    loc, sends, _ = _xchg_copies(*refs, gather)
    loc.start()
    for cp in sends:
        cp.start()


def _xchg_wait(*refs, gather):
    loc, sends, recvs = _xchg_copies(*refs, gather)
    for cp in recvs:
        cp.wait_recv()
    for cp in sends:
        cp.wait_send()
    loc.wait()


_XCHG_SCRATCH = [pltpu.SemaphoreType.DMA((NDEV - 1,)), pltpu.SemaphoreType.DMA((NDEV - 1,)), pltpu.SemaphoreType.DMA]


def _xchg_out_shape(x, gather):
    return jax.ShapeDtypeStruct((NDEV,) + tuple(x.shape if gather else x.shape[1:]), x.dtype)


def _exchange(x, name, gather):
    def body(x_ref, o_ref, send_sems, recv_sems, loc_sem):
        _xchg_start(x_ref, o_ref, send_sems, recv_sems, loc_sem, gather=gather)
        _xchg_wait(x_ref, o_ref, send_sems, recv_sems, loc_sem, gather=gather)

    return pl.pallas_call(
        body, name=name, out_shape=_xchg_out_shape(x, gather),
        in_specs=[_HBM], out_specs=_HBM, scratch_shapes=_XCHG_SCRATCH,
    )(x)


def _gather_two_level(x, name):
    def body(x_ref, o_ref, send_sems, recv_sems, loc_sem):
        mx, my, mc = lax.axis_index("x"), lax.axis_index("y"), lax.axis_index("c")
        me, sibling = (mx, my, mc), (mx, my, 1 - mc)
        chips = [(1 - mx, my), (mx, 1 - my), (1 - mx, 1 - my)]

        def slot(px, py, pc):
            return o_ref.at[4 * px + 2 * py + pc]

        def copy(k, block, to, src=None):
            return pltpu.make_async_remote_copy(
                src_ref=slot(*block) if src is None else src, dst_ref=slot(*block),
                send_sem=send_sems.at[k], recv_sem=recv_sems.at[k], device_id=to, device_id_type=MESH)

        mine = pltpu.make_async_copy(x_ref, slot(*me), loc_sem)
        mine.start()
        first = [copy(0, me, sibling, src=x_ref)]
        first += [copy(1 + j, me, (*chip, mc), src=x_ref) for j, chip in enumerate(chips)]
        for cp in first:
            cp.start()
        passed = [copy(4 + j, (*chip, mc), sibling) for j, chip in enumerate(chips)]
        for j, chip in enumerate(chips):
            copy(1 + j, (*chip, mc), me).wait_recv()
            passed[j].start()
        copy(0, sibling, me).wait_recv()
        for j, chip in enumerate(chips):
            copy(4 + j, (*chip, 1 - mc), me).wait_recv()
        for cp in first + passed:
            cp.wait_send()
        mine.wait()

    return pl.pallas_call(
        body, name=name, out_shape=_xchg_out_shape(x, True),
        in_specs=[_HBM], out_specs=_HBM, scratch_shapes=_XCHG_SCRATCH,
    )(x)


def _hosted_call(body, xch, *, name, grid, in_specs, out_specs, out_shape, scratch_shapes, compiler_params, args,
                 aliases=None):
    aliases = aliases or {}
    if xch is None:
        return pl.pallas_call(body, name=name, grid=grid, in_specs=in_specs, out_specs=out_specs,
                              out_shape=out_shape, scratch_shapes=scratch_shapes, input_output_aliases=aliases,
                              compiler_params=compiler_params)(*args)
    xv, gather = xch
    n_in, n_out, n_scr = len(in_specs), len(out_specs), len(scratch_shapes)

    def wrapped(*refs):
        ins = refs[:n_in]
        x_ref = refs[n_in]
        outs = refs[n_in + 1:n_in + 1 + n_out]
        o_ref = refs[n_in + 1 + n_out]
        scr = refs[n_in + 2 + n_out:]
        own, sems = scr[:n_scr], scr[n_scr:]
        first = functools.reduce(jnp.logical_and, [pl.program_id(a) == 0 for a in range(len(grid))])
        last = functools.reduce(jnp.logical_and, [pl.program_id(a) == grid[a] - 1 for a in range(len(grid))])

        @pl.when(first)
        def _():
            _xchg_start(x_ref, o_ref, *sems, gather=gather)

        body(*ins, *outs, *own)

        @pl.when(last)
        def _():
            _xchg_wait(x_ref, o_ref, *sems, gather=gather)

    res = pl.pallas_call(
        wrapped, name=name, grid=grid, in_specs=list(in_specs) + [_HBM], out_specs=list(out_specs) + [_HBM],
        out_shape=list(out_shape) + [_xchg_out_shape(xv, gather)],
        scratch_shapes=list(scratch_shapes) + _XCHG_SCRATCH, input_output_aliases=aliases,
        compiler_params=compiler_params,
    )(*args, xv)
    return list(res[:n_out]), res[n_out]


def _row_tile(R, cap, mult=8):
    best = mult
    t = mult
    while t <= min(R, cap):
        if R % t == 0:
            best = t
        t += mult
    assert R % best == 0, R
    return best


def _sum_slots(x, name, xch=None):
    _, R, C = x.shape
    tr = _row_tile(R, 256, 16 if x.dtype == BF16 else 8)

    def body(x_ref, o_ref):
        o_ref[...] = _slot_sum(x_ref)

    res = _hosted_call(
        body, xch, name=name, grid=(R // tr,),
        in_specs=[pl.BlockSpec((NDEV, tr, C), lambda i: (0, i, 0))],
        out_specs=[pl.BlockSpec((tr, C), lambda i: (i, 0))],
        out_shape=[jax.ShapeDtypeStruct((R, C), F32)],
        scratch_shapes=[], compiler_params=_cp(1), args=(x,))
    if xch is None:
        return res[0]
    return res[0][0], res[1]


def _slot_sum(x_ref):
    acc = x_ref[0].astype(F32)
    for i in range(1, NDEV):
        acc = acc + x_ref[i].astype(F32)
    return acc


def _sum_adamw(slots, w, m, v, name, xch=None):
    _, R, C = slots.shape
    tr = _row_tile(R, 128, 16 if slots.dtype == BF16 else 8)

    def body(x_ref, w_ref, m_ref, v_ref, g_ref, d_ref, nm_ref, nv_ref):
        g_ref[...] = _slot_sum(x_ref)
        _adamw_update(w_ref, g_ref, m_ref, v_ref, d_ref, nm_ref, nv_ref)

    blk = pl.BlockSpec((tr, C), lambda i: (i, 0))
    res = _hosted_call(
        body, xch, name=name, grid=(R // tr,),
        in_specs=[pl.BlockSpec((NDEV, tr, C), lambda i: (0, i, 0)), blk, blk, blk],
        out_specs=[blk] * 4, out_shape=[jax.ShapeDtypeStruct((R, C), F32)] * 4,
        scratch_shapes=[], compiler_params=_cp(1), args=(slots, w, m, v))
    if xch is None:
        return res
    return res[0], res[1]


def _adamw_update(w_ref, g_ref, m_ref, v_ref, d_ref, nm_ref, nv_ref):
    c1 = 1.0 / (1.0 - ADAM_B1 ** ADAM_STEP)
    c2 = 1.0 / (1.0 - ADAM_B2 ** ADAM_STEP)
    gv = g_ref[...]
    nm = ADAM_B1 * m_ref[...] + (1.0 - ADAM_B1) * gv
    nv = ADAM_B2 * v_ref[...] + (1.0 - ADAM_B2) * (gv * gv)
    d_ref[...] = -ADAM_LR * ((nm * c1) / (jnp.sqrt(nv * c2) + ADAM_EPS) + ADAM_WD * w_ref[...])
    nm_ref[...] = nm
    nv_ref[...] = nv


def _adamw_many(ws, gs, ms, vs):
    n = len(ws)

    def body(*refs):
        for i in range(n):
            _adamw_update(refs[i], refs[n + i], refs[2 * n + i], refs[3 * n + i],
                          refs[4 * n + i], refs[5 * n + i], refs[6 * n + i])

    shapes = [jax.ShapeDtypeStruct(w.shape, F32) for w in ws]
    res = pl.pallas_call(
        body, name="adamw_small", out_shape=shapes * 3,
        compiler_params=pltpu.CompilerParams(vmem_limit_bytes=VMEM_LIMIT_BYTES),
    )(*ws, *gs, *ms, *vs)
    return res[:n], res[n:2 * n], res[2 * n:]


def _adamw(w, g, m, v, name):
    R, C = w.shape
    tr = _row_tile(R, 256)

    def body(w_ref, g_ref, m_ref, v_ref, d_ref, nm_ref, nv_ref):
        _adamw_update(w_ref, g_ref, m_ref, v_ref, d_ref, nm_ref, nv_ref)

    blk = pl.BlockSpec((tr, C), lambda i: (i, 0))
    return pl.pallas_call(
        body, name=name, grid=(R // tr,),
        in_specs=[blk] * 4, out_specs=[blk] * 3,
        out_shape=[jax.ShapeDtypeStruct((R, C), F32)] * 3,
        compiler_params=_cp(1),
    )(w, g, m, v)


def _mod_fwd(c_rows, w_shard, b_shard):
    def body(c_ref, w_ref, b_ref, o_ref):
        s = _silu(c_ref[...]).astype(BF16)
        o_ref[...] = _dot(s, w_ref[...].astype(BF16)) + b_ref[...]

    return pl.pallas_call(
        body, name="mod_fwd",
        out_shape=jax.ShapeDtypeStruct((c_rows.shape[0], w_shard.shape[1]), F32),
        compiler_params=pltpu.CompilerParams(vmem_limit_bytes=VMEM_LIMIT_BYTES),
    )(c_rows, w_shard, b_shard)


def _mod_bwd(c_rows, dm_all, dm_shard, w_shard):
    nrow = c_rows.shape[0]

    def body(c_ref, da_ref, ds_ref, w_ref, gw_ref, gb_ref, cc_ref):
        s = _silu(c_ref[...]).astype(BF16)
        ds = ds_ref[...]
        gw_ref[...] = _dot_tn(s, ds.astype(BF16))
        gb_ref[...] = jnp.sum(da_ref[...], axis=0, keepdims=True)
        rowi = lax.broadcasted_iota(jnp.int32, ds.shape, 0)
        dmc = jnp.sum(jnp.where(rowi % 8 >= 4, ds, 0.0), axis=0, keepdims=True)
        dmc8 = jnp.broadcast_to(dmc, (8, ds.shape[1])).astype(BF16)
        cc_ref[...] = _dot_nt(dmc8, w_ref[...].astype(BF16))

    return pl.pallas_call(
        body, name="mod_bwd",
        out_shape=[jax.ShapeDtypeStruct(w_shard.shape, F32),
                   jax.ShapeDtypeStruct((1, dm_all.shape[1]), F32),
                   jax.ShapeDtypeStruct((8, D), F32)],
        compiler_params=pltpu.CompilerParams(vmem_limit_bytes=VMEM_LIMIT_BYTES),
    )(c_rows, dm_all, dm_shard, w_shard)


def _small_finish(cc_pre, c_ctx, dd_cols):
    def body(cc_ref, c_ref, dd_ref, gc_ref, gd_ref):
        gc_ref[...] = cc_ref[...] * _silu_grad(c_ref[...])
        gd_ref[...] = jnp.sum(dd_ref[...], axis=1, keepdims=True)

    return pl.pallas_call(
        body, name="small_finish",
        out_shape=[jax.ShapeDtypeStruct((1, D), F32), jax.ShapeDtypeStruct((NH, 1), F32)],
    )(cc_pre, c_ctx, dd_cols)


_BIG = (("w_in", (D, 10304), 1), ("w_br_ssd", (DI, D), 0), ("w_br_lru", (LW, D), 0), ("w_out", (D, D), 0),
        ("w_mlp1", (D, MLP), 1), ("w_mlp2", (MLP, D), 0))
_SMALL_SH = (("ssd_conv_w", (4, 4096)), ("lru_conv_w", (4, LW)), ("lru_ba", (2, LW)), ("lru_bi", (2, LW)),
             ("lru_lambda", (2, LW)))
_REPL = (("c_ctx", (D,)), ("b_gate", (2 * D,)), ("ssd_conv_b", (4096,)), ("ssd_dt_bias", (2, NH)),
         ("ssd_a_log", (2, NH)), ("ssd_d", (DI,)), ("ssd_norm_w", (DI,)), ("lru_conv_b", (LW,)),
         ("ln1_g", (D,)), ("ln1_b", (D,)),
         ("b_mlp1", (MLP,)), ("b_mlp2", (D,)), ("ln2_g", (D,)), ("ln2_b", (D,)))

_WEIGHT_NAMES = ('c_ctx', 'w_mod', 'b_mod', 'w_in', 'b_gate', 'ssd_conv_w', 'ssd_conv_b', 'ssd_dt_bias', 'ssd_a_log',
                 'ssd_d', 'ssd_norm_w', 'lru_conv_w', 'lru_conv_b', 'lru_wa', 'lru_ba', 'lru_wi', 'lru_bi',
                 'lru_lambda', 'w_br_ssd', 'w_br_lru', 'w_out', 'ln1_g', 'ln1_b', 'w_mlp1', 'b_mlp1', 'w_mlp2',
                 'b_mlp2', 'ln2_g', 'ln2_b')
_ARG_NAMES = ('x', 'c', 'ctx') + _WEIGHT_NAMES + ('loss_target',) + tuple('m_' + n for n in _WEIGHT_NAMES) + tuple(
    'v_' + n for n in _WEIGHT_NAMES)


def _to_slots(full, axis):
    n = full.shape[axis] // NDEV
    if axis == 0:
        return full.reshape(NDEV, n, full.shape[1])
    return full.reshape(full.shape[0], NDEV, n).transpose(1, 0, 2)


def _from_slots(slots, axis):
    if axis == 0:
        return slots.reshape(NDEV * slots.shape[1], slots.shape[2])
    return slots.transpose(1, 0, 2).reshape(slots.shape[1], NDEV * slots.shape[2])


def _pack_rows(arrs, width=1024, mult=8):
    flat = jnp.concatenate([a.reshape(-1) for a in arrs])
    n = flat.shape[0]
    per = width * mult
    tot = -(-n // per) * per
    return jnp.pad(flat, (0, tot - n)).reshape(tot // width, width)


def _unpack_rows(packed, shapes, lead=()):
    nl = len(lead)
    flat = packed.reshape(tuple(lead) + (-1,))
    out, off = [], 0
    for s in shapes:
        n = math.prod(s)
        out.append(flat[..., off:off + n].reshape(tuple(lead) + tuple(s)))
        off += n
    return out


def kernel(x, c, ctx, c_ctx, w_mod, b_mod, w_in, b_gate, ssd_conv_w, ssd_conv_b, ssd_dt_bias, ssd_a_log, ssd_d, ssd_norm_w, lru_conv_w, lru_conv_b, lru_wa, lru_ba, lru_wi, lru_bi, lru_lambda, w_br_ssd, w_br_lru, w_out, ln1_g, ln1_b, w_mlp1, b_mlp1, w_mlp2, b_mlp2, ln2_g, ln2_b, loss_target, m_c_ctx, m_w_mod, m_b_mod, m_w_in, m_b_gate, m_ssd_conv_w, m_ssd_conv_b, m_ssd_dt_bias, m_ssd_a_log, m_ssd_d, m_ssd_norm_w, m_lru_conv_w, m_lru_conv_b, m_lru_wa, m_lru_ba, m_lru_wi, m_lru_bi, m_lru_lambda, m_w_br_ssd, m_w_br_lru, m_w_out, m_ln1_g, m_ln1_b, m_w_mlp1, m_b_mlp1, m_w_mlp2, m_b_mlp2, m_ln2_g, m_ln2_b, v_c_ctx, v_w_mod, v_b_mod, v_w_in, v_b_gate, v_ssd_conv_w, v_ssd_conv_b, v_ssd_dt_bias, v_ssd_a_log, v_ssd_d, v_ssd_norm_w, v_lru_conv_w, v_lru_conv_b, v_lru_wa, v_lru_ba, v_lru_wi, v_lru_bi, v_lru_lambda, v_w_br_ssd, v_w_br_lru, v_w_out, v_ln1_g, v_ln1_b, v_w_mlp1, v_b_mlp1, v_w_mlp2, v_b_mlp2, v_ln2_g, v_ln2_b):
    A = dict(zip(_ARG_NAMES, (x, c, ctx, c_ctx, w_mod, b_mod, w_in, b_gate, ssd_conv_w, ssd_conv_b, ssd_dt_bias, ssd_a_log, ssd_d, ssd_norm_w, lru_conv_w, lru_conv_b, lru_wa, lru_ba, lru_wi, lru_bi, lru_lambda, w_br_ssd, w_br_lru, w_out, ln1_g, ln1_b, w_mlp1, b_mlp1, w_mlp2, b_mlp2, ln2_g, ln2_b, loss_target, m_c_ctx, m_w_mod, m_b_mod, m_w_in, m_b_gate, m_ssd_conv_w, m_ssd_conv_b, m_ssd_dt_bias, m_ssd_a_log, m_ssd_d, m_ssd_norm_w, m_lru_conv_w, m_lru_conv_b, m_lru_wa, m_lru_ba, m_lru_wi, m_lru_bi, m_lru_lambda, m_w_br_ssd, m_w_br_lru, m_w_out, m_ln1_g, m_ln1_b, m_w_mlp1, m_b_mlp1, m_w_mlp2, m_b_mlp2, m_ln2_g, m_ln2_b, v_c_ctx, v_w_mod, v_b_mod, v_w_in, v_b_gate, v_ssd_conv_w, v_ssd_conv_b, v_ssd_dt_bias, v_ssd_a_log, v_ssd_d, v_ssd_norm_w, v_lru_conv_w, v_lru_conv_b, v_lru_wa, v_lru_ba, v_lru_wi, v_lru_bi, v_lru_lambda, v_w_br_ssd, v_w_br_lru, v_w_out, v_ln1_g, v_ln1_b, v_w_mlp1, v_b_mlp1, v_w_mlp2, v_b_mlp2, v_ln2_g, v_ln2_b)))
    Bn, T, _ = x.shape
    Tc = ctx.shape[1]
    cfg = _Cfg(Bn, T, Tc)
    me = _me()
    L = {n: (A[n] if n == "c_ctx" else A[n][0]) for n in _WEIGHT_NAMES}
    nmod = L["w_mod"].shape[1]

    c_all = _exchange(c, "ag_c", True)
    c_rows = jnp.concatenate([c_all.reshape(NDEV * Bn, D), jnp.broadcast_to(c_ctx[None, :], (8, D))], axis=0)
    b_shard = lax.dynamic_slice(L["b_mod"], (me * nmod,), (nmod,)).reshape(1, nmod)
    m_part = _mod_fwd(c_rows, L["w_mod"], b_shard)
    m_all = _exchange(m_part, "ag_mod", True)
    m_full = m_all.transpose(1, 0, 2).reshape(NDEV * Bn + 8, NMOD * D)
    m_mine = lax.dynamic_slice(m_full, (me * Bn, 0), (Bn, NMOD * D))
    mc = m_full[NDEV * Bn, :2 * D]

    w_in_all = _gather_two_level(L["w_in"].astype(BF16), "ag_w_in")
    rest_payload = jnp.concatenate([L[n].astype(BF16).reshape(-1, 1024) for n, _, _ in _BIG[1:]], axis=0)
    small_shapes = [(s[0], s[1] // NDEV) for _, s in _SMALL_SH]
    small_all = _exchange(_pack_rows([L[n] for n, _ in _SMALL_SH], width=512), "ag_w_small", True)
    W = {}
    for (n, shp), piece in zip(_SMALL_SH, _unpack_rows(small_all, small_shapes, lead=(NDEV,))):
        W[n] = piece.transpose(1, 0, 2).reshape(shp)
    W["w_main"], W["w_dt"] = _perm_w_in(_from_slots(w_in_all, 1))
    for n in ("ssd_conv_b", "lru_conv_b", "ssd_norm_w", "b_gate", "ln1_g", "ln1_b", "b_mlp1", "b_mlp2", "ln2_g", "ln2_b"):
        W[n] = L[n].reshape(1, -1)
    for n in ("ssd_dt_bias", "ssd_a_log", "ssd_d", "lru_wa", "lru_wi"):
        W[n] = L[n]

    loss_part, grad_x, g, dm, dmc, xres = _local_step(cfg, x, ctx, loss_target, m_mine, mc, W, rest_payload)
    loss = lax.psum(loss_part, ("x", "y", "c"))

    dmc_pad = jnp.pad(dmc, ((0, 4 - Bn), (0, (NMOD - 2) * D)))
    dm_payload = jnp.concatenate([jnp.pad(dm, ((0, 4 - Bn), (0, 0))), dmc_pad], axis=0)
    upd_w_in, dm_all = _sum_adamw(xres["rs_w_in"], L["w_in"], A["m_w_in"][0], A["v_w_in"][0], "sum_adamw_w_in",
                                  xch=(dm_payload, True))
    dm_all = dm_all.reshape(NDEV * 8, NMOD * D)
    c_rows_b = jnp.concatenate([jnp.pad(c_all, ((0, 0), (0, 4 - Bn), (0, 0))),
                                jnp.broadcast_to(c_ctx[None, None, :], (NDEV, 4, D))], axis=1).reshape(NDEV * 8, D)
    dm_shard = lax.dynamic_slice(dm_all, (0, me * nmod), (NDEV * 8, nmod))
    g_w_mod, g_b_mod, cc_part = _mod_bwd(c_rows_b, dm_all, dm_shard, L["w_mod"])
    g["c_ctx"] = cc_part[0]

    g["ssd_d"] = g.pop("ssd_d_cols")
    small_names = [n for n, _ in _REPL] + [n for n, _ in _SMALL_SH]
    small_full_shapes = [s for _, s in _REPL] + [s for _, s in _SMALL_SH]
    red_b, sm_all = _sum_slots(xres["rs_rest"], "sum_w_rest", xch=(_pack_rows([g[n] for n in small_names]), True))
    sm_sum = _sum_slots(sm_all, "sum_g_small")
    gs = dict(zip(small_names, _unpack_rows(sm_sum, small_full_shapes)))
    gcc, gdd = _small_finish(gs["c_ctx"].reshape(1, D), c_ctx.reshape(1, D), gs["ssd_d"].reshape(NH, HD))
    gs["c_ctx"] = gcc.reshape(D)
    gs["ssd_d"] = gdd.reshape(NH)
    for n, shp in _SMALL_SH:
        ns = shp[1] // NDEV
        gs[n] = lax.dynamic_slice(gs[n], (0, me * ns), (shp[0], ns))
    gs["b_mod"] = g_b_mod.reshape(NMOD * D)
    lru_sum = _sum_slots(xres["ag_lru"], "sum_g_lru").reshape(2, 2, LB, LBW, LBW)
    gs["lru_wa"], gs["lru_wi"] = lru_sum[0], lru_sum[1]

    gb = {}
    off = 0
    for n, shp, axis in _BIG[1:]:
        shard_shape = (shp[0] // NDEV, shp[1]) if axis == 0 else (shp[0], shp[1] // NDEV)
        r = math.prod(shard_shape) // 1024
        gb[n] = red_b[off:off + r].reshape(shard_shape)
        off += r
    gb["w_mod"] = g_w_mod

    grads, deltas, new_m, new_v = {}, {}, {}, {}
    big_names = ["w_mod"] + [n for n, _, _ in _BIG]
    grads["w_in"], deltas["w_in"], new_m["w_in"], new_v["w_in"] = upd_w_in
    for n in big_names:
        if n == "w_in":
            continue
        d_, nm_, nv_ = _adamw(L[n], gb[n], A["m_" + n][0], A["v_" + n][0], "adamw_" + n)
        grads[n], deltas[n], new_m[n], new_v[n] = gb[n], d_, nm_, nv_
    sm_names = [n for n in _WEIGHT_NAMES if n not in big_names]

    def two_d(a):
        return a.reshape(1, -1) if a.ndim == 1 else a
    loc = lambda pre: [two_d(A[pre + n] if n == "c_ctx" else A[pre + n][0]) for n in sm_names]
    gsm = [two_d(gs[n].reshape(L[n].shape)) for n in sm_names]
    ds_, nms_, nvs_ = _adamw_many(loc(""), gsm, loc("m_"), loc("v_"))
    for n, gv, dv, mv, vv in zip(sm_names, gsm, ds_, nms_, nvs_):
        shp = L[n].shape
        grads[n], deltas[n], new_m[n], new_v[n] = gv.reshape(shp), dv.reshape(shp), mv.reshape(shp), vv.reshape(shp)

    def out(dct):
        return [dct[n] if n == "c_ctx" else dct[n][None] for n in _WEIGHT_NAMES]
    return (loss, grad_x, *out(grads), *out(deltas), *out(new_m), *out(new_v))
```

```python
import functools
import math

import jax
import jax.numpy as jnp
from jax import lax
from jax.experimental import pallas as pl
from jax.experimental.pallas import tpu as pltpu

F32 = jnp.float32
BF16 = jnp.bfloat16

D = 1024
GRID_W = 64
DI = 2048
NH = 32
HD = 64
NG = 8
HPG = 4
NS = 128
CH = 128
LW = 1024
LB = 8
LBW = 128
LRU_C = 8.0
MLP = 4096
NMOD = 6
ALPHA = 2.0 ** 0.25
LN_EPS = 1e-6
RMS_EPS = 1e-5
PM = 10240
DTW = 128
CONVW = 5120
NDEV = 8

ADAM_LR = 0.001
ADAM_B1 = 0.9
ADAM_B2 = 0.999
ADAM_EPS = 1e-08
ADAM_WD = 0.01
ADAM_STEP = 10

VMEM_LIMIT_BYTES = 56 * 1024 * 1024


def _cp(n_axes):
    return pltpu.CompilerParams(dimension_semantics=("arbitrary",) * n_axes,
                                vmem_limit_bytes=VMEM_LIMIT_BYTES)


def _sigmoid(x):
    return 0.5 * jnp.tanh(0.5 * x) + 0.5


def _silu(x):
    return x * _sigmoid(x)


def _silu_grad(x):
    s = _sigmoid(x)
    return s * (1.0 + x * (1.0 - s))


def _log1p_pos(e):
    return jnp.where(e < 1e-2, e * (1.0 - e * (0.5 - e * (1.0 / 3.0))), jnp.log(1.0 + e))


def _softplus(x):
    return jnp.maximum(x, 0.0) + _log1p_pos(jnp.exp(-jnp.abs(x)))


_GELU_K = math.sqrt(2.0 / math.pi)


def _gelu(x):
    t = jnp.tanh(_GELU_K * (x + 0.044715 * x * x * x))
    return 0.5 * x * (1.0 + t)


def _gelu_grad(x):
    t = jnp.tanh(_GELU_K * (x + 0.044715 * x * x * x))
    dt = (1.0 - t * t) * _GELU_K * (1.0 + 3.0 * 0.044715 * x * x)
    return 0.5 * (1.0 + t) + 0.5 * x * dt


def _ln(x):
    mu = jnp.mean(x, axis=-1, keepdims=True)
    xc = x - mu
    var = jnp.mean(xc * xc, axis=-1, keepdims=True)
    rs = lax.rsqrt(var + LN_EPS)
    return xc * rs, rs


def _ln_bwd(dy, xhat, rs):
    m1 = jnp.mean(dy, axis=-1, keepdims=True)
    m2 = jnp.mean(dy * xhat, axis=-1, keepdims=True)
    return rs * (dy - m1 - xhat * m2)


def _dot(a, b):
    return lax.dot_general(a, b, (((1,), (0,)), ((), ())), preferred_element_type=F32)


def _dot_nt(a, b):
    return lax.dot_general(a, b, (((1,), (1,)), ((), ())), preferred_element_type=F32)


def _dot_tn(a, b):
    return lax.dot_general(a, b, (((0,), (0,)), ((), ())), preferred_element_type=F32)


def _split3(a):
    a0 = a.astype(BF16)
    r = a - a0.astype(F32)
    a1 = r.astype(BF16)
    a2 = (r - a1.astype(F32)).astype(BF16)
    return a0, a1, a2


def _dot_exact_l(m_bf, a):
    a0, a1, a2 = _split3(a)
    return _dot(m_bf, a0) + _dot(m_bf, a1) + _dot(m_bf, a2)


def _dot_exact_r(a, m_bf):
    a0, a1, a2 = _split3(a)
    return _dot(a0, m_bf) + _dot(a1, m_bf) + _dot(a2, m_bf)


def _dot_hilo_r(a, m_bf):
    a0 = a.astype(BF16)
    a1 = (a - a0.astype(F32)).astype(BF16)
    return _dot(a0, m_bf) + _dot(a1, m_bf)


def _tri(n, upper):
    ii = lax.broadcasted_iota(jnp.int32, (n, n), 0)
    kk = lax.broadcasted_iota(jnp.int32, (n, n), 1)
    m = (kk >= ii) if upper else (kk <= ii)
    return jnp.where(m, 1.0, 0.0).astype(BF16)


def _fit(n, t):
    t = min(t, n)
    while n % t:
        t //= 2
    return t


def _mm(a, b, mode, name, out_dtype=F32, tm=512, tn=512, tk=512, xch=None):
    if mode == "nn":
        M, K = a.shape
        N = b.shape[1]
    elif mode == "nt":
        M, K = a.shape
        N = b.shape[0]
    else:
        K, M = a.shape
        N = b.shape[1]
    tm, tn, tk = _fit(M, tm), _fit(N, tn), _fit(K, tk)
    assert M % tm == 0 and N % tn == 0 and K % tk == 0, (name, M, N, K, tm, tn, tk)
    nk = K // tk
    if mode == "tn":
        a_spec = pl.BlockSpec((tk, tm), lambda i, j, k: (k, i))
    else:
        a_spec = pl.BlockSpec((tm, tk), lambda i, j, k: (i, k))
    if mode == "nt":
        b_spec = pl.BlockSpec((tn, tk), lambda i, j, k: (j, k))
    else:
        b_spec = pl.BlockSpec((tk, tn), lambda i, j, k: (k, j))
    dn = {"nn": (((1,), (0,)), ((), ())), "nt": (((1,), (1,)), ((), ())), "tn": (((0,), (0,)), ((), ()))}[mode]

    def body(a_ref, b_ref, o_ref, acc_ref):
        k = pl.program_id(2)

        @pl.when(k == 0)
        def _():
            acc_ref[...] = jnp.zeros_like(acc_ref)

        acc_ref[...] += lax.dot_general(a_ref[...].astype(BF16), b_ref[...].astype(BF16), dn,
                                        preferred_element_type=F32)

        @pl.when(k == nk - 1)
        def _():
            o_ref[...] = acc_ref[...].astype(o_ref.dtype)

    def body_one_step(a_ref, b_ref, o_ref):
        o_ref[...] = lax.dot_general(a_ref[...].astype(BF16), b_ref[...].astype(BF16), dn,
                                     preferred_element_type=F32).astype(o_ref.dtype)

    res = _hosted_call(
        body if nk > 1 else body_one_step, xch, name=name, grid=(M // tm, N // tn, nk),
        in_specs=[a_spec, b_spec],
        out_specs=[pl.BlockSpec((tm, tn), lambda i, j, k: (i, j))],
        out_shape=[jax.ShapeDtypeStruct((M, N), out_dtype)],
        scratch_shapes=[pltpu.VMEM((tm, tn), F32)] if nk > 1 else [],
        compiler_params=_cp(3), args=(a, b))
    if xch is None:
        return res[0]
    return res[0][0], res[1]


def _mm_mlp1(h2, w1, b1, tm=1024, tn=1024):
    M, K = h2.shape
    N = w1.shape[1]
    tm, tn = _fit(M, tm), _fit(N, tn)

    def body(a_ref, b_ref, bias_ref, a1_ref, act_ref):
        v = _dot(a_ref[...], b_ref[...]) + bias_ref[...]
        a1_ref[...] = v
        r = jnp.maximum(v, 0.0)
        act_ref[...] = (r * r).astype(BF16)

    out = pl.BlockSpec((tm, tn), lambda i, j: (i, j))
    return pl.pallas_call(
        body, name="mm_mlp1", grid=(M // tm, N // tn),
        in_specs=[pl.BlockSpec((tm, K), lambda i, j: (i, 0)), pl.BlockSpec((K, tn), lambda i, j: (0, j)),
                  pl.BlockSpec((1, tn), lambda i, j: (0, j))],
        out_specs=[out, out],
        out_shape=[jax.ShapeDtypeStruct((M, N), F32), jax.ShapeDtypeStruct((M, N), BF16)],
        compiler_params=_cp(2),
    )(h2, w1, b1)


def _mm_dact(dmlp, w2, a1, tm=1024, tn=1024):
    M, K = dmlp.shape
    N = w2.shape[0]
    tm, tn = _fit(M, tm), _fit(N, tn)

    def body(d_ref, w_ref, a1_ref, o_ref, acc_ref):
        i = pl.program_id(1)

        @pl.when(i == 0)
        def _():
            acc_ref[...] = jnp.zeros_like(acc_ref)

        da = _dot_nt(d_ref[...], w_ref[...]) * (2.0 * jnp.maximum(a1_ref[...], 0.0))
        o_ref[...] = da.astype(BF16)
        acc_ref[0:1, :] += jnp.sum(da, axis=0, keepdims=True)

    blk = pl.BlockSpec((tm, tn), lambda j, i: (i, j))
    return pl.pallas_call(
        body, name="mm_dact", grid=(N // tn, M // tm),
        in_specs=[pl.BlockSpec((tm, K), lambda j, i: (i, 0)), pl.BlockSpec((tn, K), lambda j, i: (j, 0)), blk],
        out_specs=[blk, pl.BlockSpec((8, tn), lambda j, i: (0, j))],
        out_shape=[jax.ShapeDtypeStruct((M, N), BF16), jax.ShapeDtypeStruct((8, N), F32)],
        compiler_params=_cp(2),
    )(dmlp, w2, a1)


class _Cfg:
    def __init__(self, Bn, T, Tc):
        assert T % Tc == 0 and Tc % CH == 0 and Tc % GRID_W == 0
        self.Bn, self.T, self.Tc = Bn, T, Tc
        self.TT = T + Tc
        self.TB = Tc
        self.nbt = self.TT // self.TB
        self.nbl = T // self.TB
        self.NT = Bn * self.TT
        self.N = Bn * T
        self.nct = Tc // CH
        self.nlt = T // CH
        self.nch = self.nct + self.nlt


def _ln_mod_fwd(cfg, xs, shift_tab, scale_tab):
    TB, nbt = cfg.TB, cfg.nbt

    def body(x_ref, sh_ref, sc_ref, o_ref):
        xhat, _ = _ln(x_ref[...])
        o_ref[...] = (xhat * (1.0 + sc_ref[...]) + sh_ref[...]).astype(BF16)

    tab = pl.BlockSpec((None, 1, D), lambda b, j: (2 * b + jnp.minimum(j, 1), 0, 0))
    return pl.pallas_call(
        body, name="ln_mod_fwd", grid=(cfg.Bn, nbt),
        in_specs=[pl.BlockSpec((TB, D), lambda b, j: (b * nbt + j, 0)), tab, tab],
        out_specs=pl.BlockSpec((TB, D), lambda b, j: (b * nbt + j, 0)),
        out_shape=jax.ShapeDtypeStruct((cfg.NT, D), BF16),
        compiler_params=_cp(2),
    )(xs, shift_tab, scale_tab)


GP = 8
NGB = NG // GP
HPB = GP * HPG


def _head_select(d, gb, to_front):
    r = lax.broadcasted_iota(jnp.int32, (DTW, DTW), 0)
    c = lax.broadcasted_iota(jnp.int32, (DTW, DTW), 1)
    src, dst = (r, c) if to_front else (c, r)
    hit = jnp.logical_and(src == d * NH + gb * HPB + dst, dst < HPB)
    return jnp.where(hit, 1.0, 0.0).astype(BF16)


def _dt_fwd(cfg, dt_raw, dt_bias, a_log):
    def body(raw_ref, bias_ref, alog_ref, dt_ref, dtg_ref, cumg_ref, cumT_ref):
        dt = _softplus(raw_ref[...] + bias_ref[...])
        a = -jnp.exp(alog_ref[...])
        dta = dt * a
        col = lax.broadcasted_iota(jnp.int32, (CH, DTW), 1)
        cf = _dot_exact_l(_tri(CH, False), dta)
        cr = _dot_exact_l(_tri(CH, True), dta)
        cum = jnp.where(col < NH, cf, cr)
        dt_ref[...] = dt
        for d in range(2):
            for gb in range(NGB):
                sel = _head_select(d, gb, True)
                dtg_ref[d, gb] = _dot_exact_r(dt, sel)
                cg = _dot_exact_r(cum, sel)
                cumg_ref[d, gb] = cg
                cumT_ref[d, gb] = cg.T

    blk = pl.BlockSpec((CH, DTW), lambda i: (i, 0))
    row = pl.BlockSpec((1, DTW), lambda i: (0, 0))
    gblk = pl.BlockSpec((2, NGB, CH, DTW), lambda i: (0, 0, i, 0))
    return pl.pallas_call(
        body, name="dt_fwd", grid=(cfg.NT // CH,),
        in_specs=[blk, row, row],
        out_specs=[blk, gblk, gblk, pl.BlockSpec((2, NGB, None, DTW, CH), lambda i: (0, 0, i, 0, 0))],
        out_shape=[jax.ShapeDtypeStruct((cfg.NT, DTW), F32),
                   jax.ShapeDtypeStruct((2, NGB, cfg.NT, DTW), F32),
                   jax.ShapeDtypeStruct((2, NGB, cfg.NT, DTW), F32),
                   jax.ShapeDtypeStruct((2, NGB, cfg.NT // CH, DTW, CH), F32)],
        compiler_params=_cp(1),
    )(dt_raw, dt_bias, a_log)


def _dt_bwd(cfg, dAs, dxxs, dt_raw, dt, dt_bias, a_log):
    def body(dAf_ref, dAr_ref, dxf_ref, dxr_ref, raw_ref, dt_ref, bias_ref, alog_ref, o_ref, acc_ref):
        i = pl.program_id(0)

        @pl.when(i == 0)
        def _():
            acc_ref[...] = jnp.zeros_like(acc_ref)

        a = -jnp.exp(alog_ref[...])
        col = lax.broadcasted_iota(jnp.int32, (CH, DTW), 1)
        dA_v = jnp.zeros((CH, DTW), F32)
        dxx_v = jnp.zeros((CH, DTW), F32)
        for d, (ra, rx) in enumerate(((dAf_ref, dxf_ref), (dAr_ref, dxr_ref))):
            for gb in range(NGB):
                sel = _head_select(d, gb, False)
                dA_v = dA_v + _dot_exact_r(ra[gb], sel)
                dxx_v = dxx_v + _dot_exact_r(rx[gb], sel)
        ddta = jnp.where(col < NH, _dot_exact_l(_tri(CH, True), dA_v), _dot_exact_l(_tri(CH, False), dA_v))
        dtv = dt_ref[...]
        ddt = ddta * a + dxx_v
        draw = ddt * _sigmoid(raw_ref[...] + bias_ref[...])
        draw = jnp.where(col < 2 * NH, draw, 0.0)
        o_ref[...] = draw
        da = jnp.sum(ddta * dtv, axis=0, keepdims=True) * a
        da = jnp.where(col[:1] < 2 * NH, da, 0.0)
        acc_ref[0:1, :] += da
        acc_ref[1:2, :] += jnp.sum(draw, axis=0, keepdims=True)

    blk = pl.BlockSpec((CH, DTW), lambda i: (i, 0))
    row = pl.BlockSpec((1, DTW), lambda i: (0, 0))
    gblk = pl.BlockSpec((NGB, CH, DTW), lambda i: (0, i, 0))
    return pl.pallas_call(
        body, name="dt_bwd", grid=(cfg.NT // CH,),
        in_specs=[gblk, gblk, gblk, gblk, blk, blk, row, row],
        out_specs=[blk, pl.BlockSpec((8, DTW), lambda i: (0, 0))],
        out_shape=[jax.ShapeDtypeStruct((cfg.NT, DTW), F32), jax.ShapeDtypeStruct((8, DTW), F32)],
        compiler_params=_cp(1),
    )(dAs[0], dAs[1], dxxs[0], dxxs[1], dt_raw, dt, dt_bias, a_log)


_TAPS = (2, 1, 0, -1)


def _conv_fwd(cfg, proj, conv_w, conv_b):
    TB, nbt = cfg.TB, cfg.nbt
    CB = CONVW // 2
    SUB = 256
    n_act = DI + 2 * NG * NS

    def body(u_ref, w_ref, b_ref, o_ref, sg_ref):
        i = pl.program_id(0)
        j = pl.program_id(1)
        R = jnp.where(i % nbt == 0, cfg.Tc, GRID_W)
        t = lax.broadcasted_iota(jnp.int32, (TB, SUB), 0)
        pos = jnp.bitwise_and(t, R - 1)
        keep = {s: jnp.where(jnp.logical_and(pos - s >= 0, pos - s < R), 1.0, 0.0) for s in (2, 1, -1)}
        def sub_tile(q, act):
            sl = slice(q * SUB, (q + 1) * SUB)
            u = u_ref[:, sl]
            pre = b_ref[:, sl] + w_ref[2:3, sl] * u
            for k in (0, 1, 3):
                pre = pre + w_ref[k:k + 1, sl] * (pltpu.roll(u, _TAPS[k] % TB, 0) * keep[_TAPS[k]])
            if act:
                s = _sigmoid(pre)
                o_ref[:, sl] = pre * s
                sg_ref[:, sl] = s * (1.0 + pre * (1.0 - s))
            else:
                o_ref[:, sl] = pre

        for q in range(CB // SUB):
            if q * SUB >= n_act - CB:
                pl.when(j == 0)(functools.partial(sub_tile, q, True))
                pl.when(j == 1)(functools.partial(sub_tile, q, False))
            else:
                sub_tile(q, True)

    blk = pl.BlockSpec((TB, CB), lambda i, j: (i, j))
    return pl.pallas_call(
        body, name="conv_fwd", grid=(cfg.NT // TB, CONVW // CB),
        in_specs=[blk, pl.BlockSpec((4, CB), lambda i, j: (0, j)), pl.BlockSpec((1, CB), lambda i, j: (0, j))],
        out_specs=[blk, blk],
        out_shape=[jax.ShapeDtypeStruct((cfg.NT, CONVW), F32)] * 2,
        compiler_params=_cp(2),
    )(proj, conv_w, conv_b)


_ANY = pl.BlockSpec(memory_space=pl.ANY)


def _conv_bwd(cfg, name, dproj, proj, conv_w, sgrad, addends, col0, width, skip=None, xch=None):
    TB, nbt, nbl = cfg.TB, cfg.nbt, cfg.nbl
    CB = 1024
    SUB = 256
    c0 = col0 // CB
    addends = list(addends) + ([] if sgrad is None else [sgrad])
    n_add = len(addends)

    def body(*refs):
        u_ref, w_ref = refs[1:3]
        add_refs = refs[3:3 + n_add]
        rest = refs[3 + n_add:]
        if sgrad is not None:
            add_refs, sg_ref = add_refs[:-1], add_refs[-1]
        if skip is not None:
            dy_ref, dv_ref = rest[:2]
            rest = rest[2:]
        o_ref, acc_ref = rest
        i = pl.program_id(1)

        @pl.when(i == 0)
        def _():
            acc_ref[...] = jnp.zeros_like(acc_ref)

        isctx = (i % nbt) == 0
        R = jnp.where(isctx, cfg.Tc, GRID_W)
        t = lax.broadcasted_iota(jnp.int32, (TB, SUB), 0)
        pos = jnp.bitwise_and(t, R - 1)
        keep = {s: jnp.where(jnp.logical_and(pos - s >= 0, pos - s < R), 1.0, 0.0) for s in (2, 1, -1, -2)}

        def shifted(v, s):
            return v if s == 0 else pltpu.roll(v, s % TB, 0) * keep[s]

        for q in range(CB // SUB):
            sl = slice(q * SUB, (q + 1) * SUB)
            u = u_ref[:, sl]
            us = [shifted(u, _TAPS[k]) for k in range(4)]
            g = add_refs[0][:, sl]
            for r in add_refs[1:]:
                g = g + r[:, sl]
            if skip is not None:
                g = g + jnp.where(isctx, 0.0, dv_ref[:, sl] * dy_ref[:, sl])
            if sgrad is not None:
                g = g * sg_ref[:, sl]
            dp = jnp.zeros_like(g)
            for k in range(4):
                acc_ref[k:k + 1, sl] += jnp.sum(g * us[k], axis=0, keepdims=True)
                dp = dp + w_ref[k:k + 1, sl] * shifted(g, -_TAPS[k])
            acc_ref[4:5, sl] += jnp.sum(g, axis=0, keepdims=True)
            o_ref[:, sl] = dp.astype(BF16)

    blk = pl.BlockSpec((TB, CB), lambda j, i: (i, j))
    wide = pl.BlockSpec((TB, CB), lambda j, i: (i, c0 + j))
    in_specs = [_ANY, wide, pl.BlockSpec((4, CB), lambda j, i: (0, c0 + j))]
    in_specs += [blk] * (n_add if sgrad is None else n_add - 1) + ([] if sgrad is None else [wide])
    args = [dproj, proj, conv_w] + addends
    if skip is not None:
        def lat(j, i):
            b = i // nbt
            return (b * nbl + jnp.maximum(i % nbt - 1, 0), j)
        in_specs += [pl.BlockSpec((TB, CB), lat), pl.BlockSpec((1, CB), lambda j, i: (0, j))]
        args += list(skip)
    return _hosted_call(
        body, xch, name=name, grid=(width // CB, cfg.NT // TB),
        in_specs=in_specs,
        out_specs=[pl.BlockSpec((TB, CB), lambda j, i: (i, c0 + j)), pl.BlockSpec((8, CB), lambda j, i: (0, j))],
        out_shape=[jax.ShapeDtypeStruct((cfg.NT, PM), BF16), jax.ShapeDtypeStruct((8, width), F32)],
        scratch_shapes=[], compiler_params=_cp(2), args=args, aliases={0: 0})


def _chunk_of_step(cfg, rev):
    nct, nlt = cfg.nct, cfg.nlt
    if not rev:
        return lambda s: s
    return lambda s: jnp.where(s < nct, nct - 1 - s, 2 * nct + nlt - 1 - s)


def _expand4(v, band, base):
    out = v[:, base + 3:base + 4]
    for h in (2, 1, 0):
        out = jnp.where(band == h, v[:, base + h:base + h + 1], out)
    return out


def _ssd_step_tiles(dt_ref, cum_ref, cumT_ref, rev):
    cum_t = cum_ref[...]
    last = 0 if rev else CH - 1
    llast = cum_t[last:last + 1, :]
    return (dt_ref[...], cum_t, cumT_ref[...], llast, jnp.exp(llast), last)


def _ssd_common(gi, x_ref, b_ref, c_ref, tiles, rev, intra=True):
    dt_t, cum_t, cumT_t, llast, elast, last = tiles
    base = gi * HPG
    xh = x_ref[:, gi * HPG * HD:(gi + 1) * HPG * HD]
    Bm = b_ref[:, gi * NS:(gi + 1) * NS].astype(BF16)
    band = lax.broadcasted_iota(jnp.int32, (CH, HPG * HD), 1) // HD
    cbs = [jnp.broadcast_to(cum_t[:, base + h:base + h + 1], (CH, CH)) for h in range(HPG)]
    Cm = G = decs = None
    if intra:
        Cm = c_ref[:, gi * NS:(gi + 1) * NS].astype(BF16)
        G = _dot_nt(Cm, Bm)
        ii = lax.broadcasted_iota(jnp.int32, (CH, CH), 0)
        jj = lax.broadcasted_iota(jnp.int32, (CH, CH), 1)
        mask = (jj >= ii) if rev else (jj <= ii)
        decs = [jnp.exp(jnp.where(mask, cbs[h] - cumT_t[base + h:base + h + 1, :], -1e30)) for h in range(HPG)]
    cum_exp = jnp.concatenate([cbs[3], cbs[3]], axis=1)
    ll_exp = llast[:, base + 3:base + 4]
    for h in (2, 1, 0):
        cum_exp = jnp.where(band == h, jnp.concatenate([cbs[h], cbs[h]], axis=1), cum_exp)
        ll_exp = jnp.where(band[:1] == h, llast[:, base + h:base + h + 1], ll_exp)
    ecum = jnp.exp(cum_exp) if intra else None
    e_exp = jnp.exp(ll_exp - cum_exp)
    dt_exp = _expand4(dt_t, band, base)
    X = xh * dt_exp
    rb = lax.broadcasted_iota(jnp.int32, (HPG * HD, NS), 0) // HD
    dec_rows = elast[:, base + 3:base + 4]
    for h in (2, 1, 0):
        dec_rows = jnp.where(rb == h, elast[:, base + h:base + h + 1], dec_rows)
    return xh, Bm, Cm, band, e_exp, ecum, dt_exp, X, G, decs, elast, dec_rows, last


def _ssd_specs(cfg, rev):
    nch = cfg.nch
    cmap = _chunk_of_step(cfg, rev)
    d = 1 if rev else 0

    def make(stepmap):
        def row(b, g, sp):
            return b * nch + cmap(stepmap(sp))
        bo, co = DI // (GP * NS), (DI + NG * NS) // (GP * NS)
        return [
            pl.BlockSpec((CH, GP * HPG * HD), lambda b, g, sp: (row(b, g, sp), g)),
            pl.BlockSpec((CH, GP * NS), lambda b, g, sp: (row(b, g, sp), bo + g)),
            pl.BlockSpec((CH, GP * NS), lambda b, g, sp: (row(b, g, sp), co + g)),
            pl.BlockSpec((None, None, CH, DTW), lambda b, g, sp: (d, g, row(b, g, sp), 0)),
            pl.BlockSpec((None, None, CH, DTW), lambda b, g, sp: (d, g, row(b, g, sp), 0)),
            pl.BlockSpec((None, None, None, DTW, CH), lambda b, g, sp: (d, g, row(b, g, sp), 0, 0)),
        ], row
    return make


def _ssd_fwd(cfg, act, dtg, cumg, cumTg, rev, y_other=None, dvec=None):
    nch = cfg.nch
    in_specs, row = _ssd_specs(cfg, rev)(lambda sp: sp)
    total = y_other is not None

    def body(*refs):
        x_ref, b_ref, c_ref, dt_ref, cum_ref, cumT_ref = refs[:6]
        if total:
            yo_ref, dv_ref = refs[6:8]
        y_ref, hs_ref, h_scr = refs[-3:]
        s = pl.program_id(2)

        @pl.when(s == 0)
        def _():
            h_scr[...] = jnp.zeros_like(h_scr)

        def step(intra):
            tiles = _ssd_step_tiles(dt_ref, cum_ref, cumT_ref, rev)
            for gi in range(GP):
                xh, Bm, Cm, band, e_exp, ecum, dt_exp, X, G, decs, elast, dec_rows, last = _ssd_common(
                    gi, x_ref, b_ref, c_ref, tiles, rev, intra)
                H = h_scr[gi]
                if intra:
                    Mcat = jnp.concatenate([(G * decs[h]).astype(BF16) for h in range(HPG)], axis=1)
                    Xbd = jnp.concatenate([jnp.where(band == h, X, 0.0).astype(BF16) for h in range(HPG)], axis=0)
                    xsl = slice(gi * HPG * HD, (gi + 1) * HPG * HD)
                    Y = ecum * _dot_nt(Cm, H.astype(BF16)) + _dot(Mcat, Xbd)
                    if total:
                        Y = Y + yo_ref[:, xsl] + dv_ref[:, xsl] * xh
                    y_ref[:, xsl] = Y
                hs_ref[gi] = H
                S = _dot_tn((e_exp * X).astype(BF16), Bm)
                h_scr[gi] = dec_rows * H + S

        isctx = cmap(s) < cfg.nct

        @pl.when(isctx)
        def _():
            step(False)

        @pl.when(jnp.logical_not(isctx))
        def _():
            step(True)

    cmap = _chunk_of_step(cfg, rev)
    yblk = pl.BlockSpec((CH, GP * HPG * HD), lambda b, g, s: (row(b, g, s), g))
    args = [act, act, act, dtg, cumg, cumTg]
    if total:
        in_specs = in_specs + [yblk, pl.BlockSpec((1, GP * HPG * HD), lambda b, g, s: (0, g))]
        args += [y_other, dvec]
    return pl.pallas_call(
        body, name="ssd_fwd_rev" if rev else "ssd_fwd", grid=(cfg.Bn, NG // GP, nch),
        in_specs=in_specs,
        out_specs=[yblk, pl.BlockSpec((None, GP, None, HPG * HD, NS), lambda b, g, s: (b, g, s, 0, 0))],
        out_shape=[jax.ShapeDtypeStruct((cfg.NT, DI), F32),
                   jax.ShapeDtypeStruct((cfg.Bn, NG, nch, HPG * HD, NS), F32)],
        scratch_shapes=[pltpu.VMEM((GP, HPG * HD, NS), F32)],
        compiler_params=_cp(3),
    )(*args)


def _ssd_bwd(cfg, act, dtg, cumg, cumTg, hs, dy, rev, xch=None):
    nch, nct, nlt = cfg.nch, cfg.nct, cfg.nlt
    cmap = _chunk_of_step(cfg, rev)
    in_specs, row = _ssd_specs(cfg, rev)(lambda sp: nch - 1 - sp)

    def lat_row(b, g, sp):
        c = cmap(nch - 1 - sp)
        return b * nlt + jnp.maximum(c - nct, 0)

    def body(x_ref, b_ref, c_ref, dt_ref, cum_ref, cumT_ref, dy_ref, hs_ref,
             dxh_ref, dB_ref, dC_ref, dA_ref, dxx_ref, dh_scr):
        sp = pl.program_id(2)

        @pl.when(sp == 0)
        def _():
            dh_scr[...] = jnp.zeros_like(dh_scr)

        def step(intra):
            tiles = _ssd_step_tiles(dt_ref, cum_ref, cumT_ref, rev)
            dA_t = jnp.zeros((CH, DTW), F32)
            dAT_t = jnp.zeros((DTW, CH), F32)
            dxx_t = jnp.zeros((CH, DTW), F32)
            for gi in range(GP):
                dA_g, dAT_g, dxx_g = group_bwd(gi, intra, tiles, x_ref, b_ref, c_ref, dy_ref, hs_ref,
                                               dxh_ref, dB_ref, dC_ref, dh_scr)
                dA_t = dA_t + dA_g
                dxx_t = dxx_t + dxx_g
                if intra:
                    dAT_t = dAT_t + dAT_g
            dA_ref[...] = dA_t - dAT_t.T if intra else dA_t
            dxx_ref[...] = dxx_t

        isctx = cmap(nch - 1 - sp) < nct

        @pl.when(isctx)
        def _():
            step(False)

        @pl.when(jnp.logical_not(isctx))
        def _():
            step(True)

    def group_bwd(gi, intra, tiles, x_ref, b_ref, c_ref, dy_ref, hs_ref, dxh_ref, dB_ref, dC_ref, dh_scr):
        xsl = slice(gi * HPG * HD, (gi + 1) * HPG * HD)
        nsl = slice(gi * NS, (gi + 1) * NS)
        base = gi * HPG
        xh, Bm, Cm, band, e_exp, ecum, dt_exp, X, G, decs, elast, dec_rows, last = _ssd_common(
            gi, x_ref, b_ref, c_ref, tiles, rev, intra)
        H = hs_ref[gi]
        dHn = dh_scr[gi]
        dHnb = dHn.astype(BF16)
        BdH = _dot_nt(Bm, dHnb)
        dX = e_exp * BdH
        eX = e_exp * X
        lanei = lax.broadcasted_iota(jnp.int32, (CH, DTW), 1)
        dA = jnp.zeros((CH, DTW), F32)
        dAT = None
        pb = lax.broadcasted_iota(jnp.int32, (HPG * HD, NS), 0) // HD
        pl_ = lax.broadcasted_iota(jnp.int32, (HPG * HD, NS), 1)
        E = jnp.where(pb + base == pl_, 1.0, 0.0).astype(BF16)
        if intra:
            dY = dy_ref[:, xsl]
            Hb = H.astype(BF16)
            dYs = ecum * dY
            dYsb = dYs.astype(BF16)
            Ys = ecum * _dot_nt(Cm, Hb)
            dG = jnp.zeros((CH, CH), F32)
            subi = lax.broadcasted_iota(jnp.int32, (DTW, CH), 0)
            dAT = jnp.zeros((DTW, CH), F32)
            Xbd = jnp.concatenate([jnp.where(band == h, X, 0.0).astype(BF16) for h in range(HPG)], axis=0)
            dYbd = jnp.concatenate([jnp.where(band == h, dY, 0.0).astype(BF16) for h in range(HPG)], axis=0)
            dMcat = _dot_nt(dY.astype(BF16), Xbd)
            Ms = []
            for h in range(HPG):
                M = G * decs[h]
                dM = dMcat[:, h * CH:(h + 1) * CH]
                W = dM * M
                dG = dG + dM * decs[h]
                Ms.append(M.astype(BF16))
                dA = dA + jnp.where(lanei == base + h, jnp.sum(W, axis=1, keepdims=True), 0.0)
                dAT = dAT + jnp.where(subi == base + h, jnp.sum(W, axis=0, keepdims=True), 0.0)
            dX = dX + _dot_tn(jnp.concatenate(Ms, axis=0), dYbd)
            dGb = dG.astype(BF16)
            dC_ref[:, nsl] = _dot(dGb, Bm) + _dot(dYsb, Hb)
            dB_ref[:, nsl] = _dot_tn(dGb, Cm) + _dot(eX.astype(BF16), dHnb)
            dh_scr[gi] = dec_rows * dHn + _dot_tn(dYsb, Cm)
            dA = dA + _dot_hilo_r(dY * Ys, E)
        else:
            dC_ref[:, nsl] = jnp.zeros((CH, NS), F32)
            dB_ref[:, nsl] = _dot(eX.astype(BF16), dHnb)
            dh_scr[gi] = dec_rows * dHn
        q = _dot_hilo_r(eX * BdH, E)
        r = jnp.sum(dHn * H, axis=1, keepdims=True)
        lane1 = lax.broadcasted_iota(jnp.int32, (1, DTW), 1)
        hdot = jnp.zeros((1, DTW), F32)
        for h in range(HPG):
            hv = jnp.sum(r[h * HD:(h + 1) * HD, :], axis=0, keepdims=True)
            hdot = hdot + jnp.where(lane1 == base + h, hv, 0.0)
        dllast = jnp.sum(q, axis=0, keepdims=True) + elast * hdot
        rowi = lax.broadcasted_iota(jnp.int32, (CH, DTW), 0)
        dxh_ref[:, xsl] = dX * dt_exp
        return dA - q + jnp.where(rowi == last, dllast, 0.0), dAT, _dot_hilo_r(dX * xh, E)

    small = pl.BlockSpec((None, CH, DTW), lambda b, g, sp: (g, row(b, g, sp), 0))
    return _hosted_call(
        body, xch, name="ssd_bwd_rev" if rev else "ssd_bwd", grid=(cfg.Bn, NG // GP, nch),
        in_specs=in_specs + [
            pl.BlockSpec((CH, GP * HPG * HD), lambda b, g, sp: (lat_row(b, g, sp), g)),
            pl.BlockSpec((None, GP, None, HPG * HD, NS), lambda b, g, sp: (b, g, nch - 1 - sp, 0, 0))],
        out_specs=[pl.BlockSpec((CH, GP * HPG * HD), lambda b, g, sp: (row(b, g, sp), g)),
                   pl.BlockSpec((CH, GP * NS), lambda b, g, sp: (row(b, g, sp), g)),
                   pl.BlockSpec((CH, GP * NS), lambda b, g, sp: (row(b, g, sp), g)),
                   small, small],
        out_shape=[jax.ShapeDtypeStruct((cfg.NT, DI), F32),
                   jax.ShapeDtypeStruct((cfg.NT, NG * NS), F32),
                   jax.ShapeDtypeStruct((cfg.NT, NG * NS), F32),
                   jax.ShapeDtypeStruct((NGB, cfg.NT, DTW), F32),
                   jax.ShapeDtypeStruct((NGB, cfg.NT, DTW), F32)],
        scratch_shapes=[pltpu.VMEM((GP, HPG * HD, NS), F32)],
        compiler_params=_cp(3), args=(act, act, act, dtg, cumg, cumTg, dy, hs))


def _shift_rows(v, s, fill, toward_later, rowi):
    n = v.shape[0]
    if toward_later:
        return jnp.where(rowi >= s, pltpu.roll(v, s, 0), fill)
    return jnp.where(rowi < n - s, pltpu.roll(v, n - s, 0), fill)


def _chunk_scan(a, b, carry, later):
    nt = a.shape[0] // 8
    rowi = lax.broadcasted_iota(jnp.int32, (8, a.shape[1]), 0)
    outs = [None] * nt
    for r in (range(nt) if later else range(nt - 1, -1, -1)):
        av = a[r * 8:(r + 1) * 8]
        bv = b[r * 8:(r + 1) * 8]
        for sh in (1, 2, 4):
            a_p = _shift_rows(av, sh, 1.0, later, rowi)
            b_p = _shift_rows(bv, sh, 0.0, later, rowi)
            bv = av * b_p + bv
            av = av * a_p
        h = bv + av * carry
        outs[r] = h
        carry = h[7:8] if later else h[0:1]
    return jnp.concatenate(outs, axis=0), carry


def _lru_gates(u, wa_ref, wi_ref, ba_ref, bi_ref, lam_ref):
    rs, is_ = [], []
    for k in range(LB):
        uk = u[:, k * LBW:(k + 1) * LBW].astype(BF16)
        rs.append(_dot(uk, wa_ref[k].astype(BF16)))
        is_.append(_dot(uk, wi_ref[k].astype(BF16)))
    r = _sigmoid(jnp.concatenate(rs, axis=1) + ba_ref[...])
    ig = _sigmoid(jnp.concatenate(is_, axis=1) + bi_ref[...])
    sp = _softplus(-lam_ref[...])
    la = -LRU_C * r * sp
    a = jnp.exp(la)
    g = jnp.sqrt((1.0 + a * a) * jnp.tanh(-la))
    return r, ig, sp, la, a, g


def _lru_w_specs(d):
    return [pl.BlockSpec((None, LB, LBW, LBW), lambda b, s: (d, 0, 0, 0)),
            pl.BlockSpec((None, LB, LBW, LBW), lambda b, s: (d, 0, 0, 0)),
            pl.BlockSpec((None, 1, LW), lambda b, s: (d, 0, 0)),
            pl.BlockSpec((None, 1, LW), lambda b, s: (d, 0, 0)),
            pl.BlockSpec((None, 1, LW), lambda b, s: (d, 0, 0))]


def _lru_block_of_step(cfg, rev):
    nbl = cfg.nbl
    if not rev:
        return lambda s: s
    return lambda s: jnp.where(s < 1, 0, 1 + nbl - s)


def _lru_fwd(cfg, act, wa, wi, ba, bi, lam, rev):
    nch, CH = cfg.nbt, cfg.TB
    cmap = _lru_block_of_step(cfg, rev)
    d = 1 if rev else 0
    ucol = (DI + 2 * NG * NS) // LW

    def body(u_ref, wa_ref, wi_ref, ba_ref, bi_ref, lam_ref, h_ref, c_scr):
        s = pl.program_id(1)

        @pl.when(s == 0)
        def _():
            c_scr[...] = jnp.zeros_like(c_scr)

        u = u_ref[...]
        r, ig, sp, la, a, g = _lru_gates(u, wa_ref, wi_ref, ba_ref, bi_ref, lam_ref)
        h, carry = _chunk_scan(a, g * ig * u, c_scr[0:1, :], not rev)
        h_ref[...] = h
        c_scr[0:1, :] = carry

    return pl.pallas_call(
        body, name="lru_fwd_rev" if rev else "lru_fwd", grid=(cfg.Bn, nch),
        in_specs=[pl.BlockSpec((CH, LW), lambda b, s: (b * nch + cmap(s), ucol))] + _lru_w_specs(d),
        out_specs=pl.BlockSpec((CH, LW), lambda b, s: (b * nch + cmap(s), 0)),
        out_shape=jax.ShapeDtypeStruct((cfg.NT, LW), F32),
        scratch_shapes=[pltpu.VMEM((8, LW), F32)],
        compiler_params=_cp(2),
    )(act, wa, wi, ba, bi, lam)


def _lru_bwd(cfg, act, wa, wi, ba, bi, lam, hd, dyl, rev):
    nch, nct, nlt, CH = cfg.nbt, 1, cfg.nbl, cfg.TB
    cmap = _lru_block_of_step(cfg, rev)
    d = 1 if rev else 0
    ucol = (DI + 2 * NG * NS) // LW

    def srow(b, sp):
        return b * nch + cmap(nch - 1 - sp)

    def prev_rows(b, sp):
        s = nch - 1 - sp
        cp = cmap(jnp.maximum(s - 1, 0))
        base = (b * nch + cp) * (CH // 8)
        return base + (0 if rev else CH // 8 - 1)

    def lat_row(b, sp):
        c = cmap(nch - 1 - sp)
        return b * nlt + jnp.maximum(c - nct, 0)

    def body(u_ref, wa_ref, wi_ref, ba_ref, bi_ref, lam_ref, h_ref, hp_ref, dy_ref,
             du_ref, dwa_ref, dwi_ref, vec_ref, c_scr):
        b = pl.program_id(0)
        sp_id = pl.program_id(1)
        s = nch - 1 - sp_id

        @pl.when(sp_id == 0)
        def _():
            c_scr[...] = jnp.zeros_like(c_scr)

        @pl.when(jnp.logical_and(b == 0, sp_id == 0))
        def _():
            dwa_ref[...] = jnp.zeros_like(dwa_ref)
            dwi_ref[...] = jnp.zeros_like(dwi_ref)
            vec_ref[...] = jnp.zeros_like(vec_ref)

        c = cmap(s)
        u = u_ref[...]
        r, ig, spl, la, a, g = _lru_gates(u, wa_ref, wi_ref, ba_ref, bi_ref, lam_ref)
        dh = jnp.where(c < nct, 0.0, dy_ref[...])
        rowi = lax.broadcasted_iota(jnp.int32, (CH, LW), 0)
        lamv, _ = _chunk_scan(_shift_rows(a, 1, 1.0, rev, rowi), dh, c_scr[0:1, :], rev)
        first = CH - 1 if rev else 0
        c_scr[0:1, :] = (a * lamv)[first:first + 1, :]
        hprow = hp_ref[...][(0 if rev else 7):(1 if rev else 8), :]
        hprow = jnp.where(s > 0, hprow, 0.0)
        h_prev = _shift_rows(h_ref[...], 1, hprow, not rev, rowi)
        da = lamv * h_prev
        db = lamv
        iu = ig * u
        dla = da * a - db * iu * (a * a) / g
        dr = dla * (-LRU_C * spl)
        di = db * g * u
        du = db * g * ig
        drp = dr * r * (1.0 - r)
        dip = di * ig * (1.0 - ig)
        dus = []
        for k in range(LB):
            sl = slice(k * LBW, (k + 1) * LBW)
            drk = drp[:, sl].astype(BF16)
            dik = dip[:, sl].astype(BF16)
            uk = u[:, sl].astype(BF16)
            dus.append(_dot_nt(drk, wa_ref[k].astype(BF16)) + _dot_nt(dik, wi_ref[k].astype(BF16)))
            dwa_ref[k] += _dot_tn(uk, drk)
            dwi_ref[k] += _dot_tn(uk, dik)
        du_ref[...] = du + jnp.concatenate(dus, axis=1)
        vec_ref[0:1, :] += jnp.sum(drp, axis=0, keepdims=True)
        vec_ref[1:2, :] += jnp.sum(dip, axis=0, keepdims=True)
        dsp = jnp.sum(dla * (-LRU_C * r), axis=0, keepdims=True)
        vec_ref[2:3, :] += dsp * (-_sigmoid(-lam_ref[...]))

    return pl.pallas_call(
        body, name="lru_bwd_rev" if rev else "lru_bwd", grid=(cfg.Bn, nch),
        in_specs=[pl.BlockSpec((CH, LW), lambda b, sp: (srow(b, sp), ucol))] + _lru_w_specs(d) + [
            pl.BlockSpec((CH, LW), lambda b, sp: (srow(b, sp), 0)),
            pl.BlockSpec((8, LW), lambda b, sp: (prev_rows(b, sp), 0)),
            pl.BlockSpec((CH, LW), lambda b, sp: (lat_row(b, sp), 0))],
        out_specs=[pl.BlockSpec((CH, LW), lambda b, sp: (srow(b, sp), 0)),
                   pl.BlockSpec((LB, LBW, LBW), lambda b, sp: (0, 0, 0)),
                   pl.BlockSpec((LB, LBW, LBW), lambda b, sp: (0, 0, 0)),
                   pl.BlockSpec((8, LW), lambda b, sp: (0, 0))],
        out_shape=[jax.ShapeDtypeStruct((cfg.NT, LW), F32),
                   jax.ShapeDtypeStruct((LB, LBW, LBW), F32),
                   jax.ShapeDtypeStruct((LB, LBW, LBW), F32),
                   jax.ShapeDtypeStruct((8, LW), F32)],
        scratch_shapes=[pltpu.VMEM((8, LW), F32)],
        compiler_params=_cp(2),
    )(act, wa, wi, ba, bi, lam, hd, hd, dyl)


HB = 1024


def _post_ssd_fwd(cfg, y, proj, norm_w):
    TB, nbt, nbl = cfg.TB, cfg.nbt, cfg.nbl
    zc = CONVW // HB

    def body(y_ref, z_ref, w_ref, o_ref):
        u = y_ref[...] * _silu(z_ref[...])
        for gi in range(HB // (DI // NG)):
            sl = slice(gi * 256, (gi + 1) * 256)
            ug = u[:, sl]
            rs = lax.rsqrt(jnp.mean(ug * ug, axis=1, keepdims=True) + RMS_EPS)
            o_ref[:, sl] = (ug * rs * w_ref[:, sl]).astype(BF16)

    def st(b, j, cb):
        return (b * nbt + 1 + j, cb)
    return pl.pallas_call(
        body, name="post_ssd_fwd", grid=(cfg.Bn, nbl, DI // HB),
        in_specs=[pl.BlockSpec((TB, HB), st),
                  pl.BlockSpec((TB, HB), lambda b, j, cb: (b * nbt + 1 + j, zc + cb)),
                  pl.BlockSpec((1, HB), lambda b, j, cb: (0, cb))],
        out_specs=pl.BlockSpec((TB, HB), lambda b, j, cb: (b * nbl + j, cb)),
        out_shape=jax.ShapeDtypeStruct((cfg.N, DI), BF16),
        compiler_params=_cp(3),
    )(y, proj, norm_w)


def _post_ssd_bwd(cfg, dproj, dn, y, act, proj, norm_w):
    TB, nbt, nbl = cfg.TB, cfg.nbt, cfg.nbl
    zc = CONVW // HB

    def body(_, dn_ref, y_ref, xh_ref, z_ref, w_ref, dy_ref, dz_ref, acc_ref):
        b = pl.program_id(1)
        j = pl.program_id(2)

        @pl.when(jnp.logical_and(b == 0, j == 0))
        def _():
            acc_ref[...] = jnp.zeros_like(acc_ref)

        @pl.when(j == 0)
        def _():
            dz_ref[...] = jnp.zeros_like(dz_ref)

        @pl.when(j > 0)
        def _():
            latent(dn_ref, y_ref, xh_ref, z_ref, w_ref, dy_ref, dz_ref, acc_ref)

    def latent(dn_ref, y_ref, xh_ref, z_ref, w_ref, dy_ref, dz_ref, acc_ref):
        xh = xh_ref[...]
        z = z_ref[...]
        y = y_ref[...]
        sz = _silu(z)
        u = y * sz
        dout = dn_ref[...]
        for gi in range(HB // (DI // NG)):
            sl = slice(gi * 256, (gi + 1) * 256)
            ug0 = u[:, sl]
            rs = lax.rsqrt(jnp.mean(ug0 * ug0, axis=1, keepdims=True) + RMS_EPS)
            ug = ug0 * rs
            do = dout[:, sl]
            acc_ref[0:1, sl] += jnp.sum(do * ug, axis=0, keepdims=True)
            dug = do * w_ref[:, sl]
            du = rs * (dug - ug * jnp.mean(dug * ug, axis=1, keepdims=True))
            dy = du * sz[:, sl]
            dy_ref[:, sl] = dy
            dz_ref[:, sl] = (du * y[:, sl] * _silu_grad(z[:, sl])).astype(BF16)
            acc_ref[1:2, sl] += jnp.sum(dy * xh[:, sl], axis=0, keepdims=True)

    def st(cb, b, j):
        return (b * nbt + j, cb)

    def la(cb, b, j):
        return (b * nbl + jnp.maximum(j - 1, 0), cb)
    return pl.pallas_call(
        body, name="post_ssd_bwd", grid=(DI // HB, cfg.Bn, nbt),
        in_specs=[_ANY, pl.BlockSpec((TB, HB), la), pl.BlockSpec((TB, HB), st), pl.BlockSpec((TB, HB), st),
                  pl.BlockSpec((TB, HB), lambda cb, b, j: (b * nbt + j, zc + cb)),
                  pl.BlockSpec((1, HB), lambda cb, b, j: (0, cb))],
        out_specs=[pl.BlockSpec((TB, HB), la),
                   pl.BlockSpec((TB, HB), lambda cb, b, j: (b * nbt + j, zc + cb)),
                   pl.BlockSpec((8, HB), lambda cb, b, j: (0, cb))],
        out_shape=[jax.ShapeDtypeStruct((cfg.N, DI), F32), jax.ShapeDtypeStruct((cfg.NT, PM), BF16),
                   jax.ShapeDtypeStruct((8, DI), F32)],
        input_output_aliases={0: 1},
        compiler_params=_cp(3),
    )(dproj, dn, y, act, proj, norm_w)


def _post_lru_fwd(cfg, hf, hb, proj):
    TB, nbt, nbl = cfg.TB, cfg.nbt, cfg.nbl
    gc = (CONVW + DI) // HB

    def body(hf_ref, hb_ref, g_ref, o_ref):
        o_ref[...] = ((hf_ref[...] + hb_ref[...]) * _gelu(g_ref[...])).astype(BF16)

    st = pl.BlockSpec((TB, HB), lambda b, j: (b * nbt + 1 + j, 0))
    return pl.pallas_call(
        body, name="post_lru_fwd", grid=(cfg.Bn, nbl),
        in_specs=[st, st, pl.BlockSpec((TB, HB), lambda b, j: (b * nbt + 1 + j, gc))],
        out_specs=pl.BlockSpec((TB, HB), lambda b, j: (b * nbl + j, 0)),
        out_shape=jax.ShapeDtypeStruct((cfg.N, LW), BF16),
        compiler_params=_cp(2),
    )(hf, hb, proj)


def _post_lru_bwd(cfg, dproj, dv, hf, hb, proj):
    TB, nbt, nbl = cfg.TB, cfg.nbt, cfg.nbl
    gc = (CONVW + DI) // HB

    def body(_, dv_ref, hf_ref, hb_ref, g_ref, dy_ref, dg_ref):
        j = pl.program_id(1)

        @pl.when(j == 0)
        def _():
            dg_ref[...] = jnp.zeros_like(dg_ref)

        @pl.when(j > 0)
        def _():
            gt = g_ref[...]
            dvv = dv_ref[...]
            dy_ref[...] = dvv * _gelu(gt)
            dg_ref[...] = (dvv * (hf_ref[...] + hb_ref[...]) * _gelu_grad(gt)).astype(BF16)

    st = pl.BlockSpec((TB, HB), lambda b, j: (b * nbt + j, 0))
    la = pl.BlockSpec((TB, HB), lambda b, j: (b * nbl + jnp.maximum(j - 1, 0), 0))
    gcol = pl.BlockSpec((TB, HB), lambda b, j: (b * nbt + j, gc))
    return pl.pallas_call(
        body, name="post_lru_bwd", grid=(cfg.Bn, nbt),
        in_specs=[_ANY, la, st, st, gcol],
        out_specs=[la, gcol],
        out_shape=[jax.ShapeDtypeStruct((cfg.N, LW), F32), jax.ShapeDtypeStruct((cfg.NT, PM), BF16)],
        input_output_aliases={0: 1},
        compiler_params=_cp(2),
    )(dproj, dv, hf, hb, proj)


def _merge_fwd(cfg, proj, b_gate, br_ssd, br_lru):
    TB, nbt, nbl = cfg.TB, cfg.nbt, cfg.nbl
    mc = (CONVW + DI + LW) // HB

    def body(ms_ref, ml_ref, bg_ref, bs_ref, bl_ref, o_ref):
        gs = _sigmoid(ms_ref[...] + bg_ref[:, :D])
        gl = _sigmoid(ml_ref[...] + bg_ref[:, D:])
        o_ref[...] = (gs * bs_ref[...] + gl * bl_ref[...]).astype(BF16)

    la = pl.BlockSpec((TB, D), lambda b, j: (b * nbl + j, 0))
    return pl.pallas_call(
        body, name="merge_fwd", grid=(cfg.Bn, nbl),
        in_specs=[pl.BlockSpec((TB, HB), lambda b, j: (b * nbt + 1 + j, mc)),
                  pl.BlockSpec((TB, HB), lambda b, j: (b * nbt + 1 + j, mc + 1)),
                  pl.BlockSpec((1, 2 * D), lambda b, j: (0, 0)), la, la],
        out_specs=la,
        out_shape=jax.ShapeDtypeStruct((cfg.N, D), BF16),
        compiler_params=_cp(2),
    )(proj, proj, b_gate, br_ssd, br_lru)


def _merge_bwd(cfg, dmix, proj, b_gate, br_ssd, br_lru):
    TB, nbt, nbl = cfg.TB, cfg.nbt, cfg.nbl
    mc = (CONVW + DI + LW) // HB

    def body(dm_ref, ms_ref, ml_ref, bg_ref, bs_ref, bl_ref, ds_ref, dl_ref, dmg_ref, acc_ref):
        b = pl.program_id(0)
        j = pl.program_id(1)

        @pl.when(jnp.logical_and(b == 0, j == 0))
        def _():
            acc_ref[...] = jnp.zeros_like(acc_ref)

        @pl.when(j == 0)
        def _():
            dmg_ref[...] = jnp.zeros_like(dmg_ref)

        @pl.when(j > 0)
        def _():
            latent(dm_ref, ms_ref, ml_ref, bg_ref, bs_ref, bl_ref, ds_ref, dl_ref, dmg_ref, acc_ref)

    def latent(dm_ref, ms_ref, ml_ref, bg_ref, bs_ref, bl_ref, ds_ref, dl_ref, dmg_ref, acc_ref):
        dm = dm_ref[...]
        gs = _sigmoid(ms_ref[...] + bg_ref[:, :D])
        gl = _sigmoid(ml_ref[...] + bg_ref[:, D:])
        ds_ref[...] = (dm * gs).astype(BF16)
        dl_ref[...] = (dm * gl).astype(BF16)
        dps = dm * bs_ref[...] * gs * (1.0 - gs)
        dpl = dm * bl_ref[...] * gl * (1.0 - gl)
        dmg_ref[:, :D] = dps.astype(BF16)
        dmg_ref[:, D:] = dpl.astype(BF16)
        acc_ref[0:1, :D] += jnp.sum(dps, axis=0, keepdims=True)
        acc_ref[0:1, D:] += jnp.sum(dpl, axis=0, keepdims=True)

    la = pl.BlockSpec((TB, D), lambda b, j: (b * nbl + jnp.maximum(j - 1, 0), 0))
    return pl.pallas_call(
        body, name="merge_bwd", grid=(cfg.Bn, nbt),
        in_specs=[la, pl.BlockSpec((TB, HB), lambda b, j: (b * nbt + j, mc)),
                  pl.BlockSpec((TB, HB), lambda b, j: (b * nbt + j, mc + 1)),
                  pl.BlockSpec((1, 2 * D), lambda b, j: (0, 0)), la, la],
        out_specs=[la, la, pl.BlockSpec((TB, 2 * D), lambda b, j: (b * nbt + j, mc // 2)),
                   pl.BlockSpec((8, 2 * D), lambda b, j: (0, 0))],
        out_shape=[jax.ShapeDtypeStruct((cfg.N, D), BF16), jax.ShapeDtypeStruct((cfg.N, D), BF16),
                   jax.ShapeDtypeStruct((cfg.NT, PM), BF16), jax.ShapeDtypeStruct((8, 2 * D), F32)],
        compiler_params=_cp(2),
    )(dmix, proj, proj, b_gate, br_ssd, br_lru)


def _resid1_fwd(cfg, xs, x_mix, gate1, shift2, scale2, ln1_g, ln1_b):
    TB, nbt, nbl = cfg.TB, cfg.nbt, cfg.nbl

    def body(x_ref, xm_ref, g1_ref, sh_ref, sc_ref, lg_ref, lb_ref, x1_ref, h2_ref):
        r1 = ALPHA * x_ref[...] + g1_ref[...] * xm_ref[...]
        xh, _ = _ln(r1)
        x1 = xh * lg_ref[...] + lb_ref[...]
        x1_ref[...] = x1
        xh2, _ = _ln(x1)
        h2_ref[...] = (xh2 * (1.0 + sc_ref[...]) + sh_ref[...]).astype(BF16)

    la = pl.BlockSpec((TB, D), lambda b, j: (b * nbl + j, 0))
    ex = pl.BlockSpec((None, 1, D), lambda b, j: (b, 0, 0))
    vec = pl.BlockSpec((1, D), lambda b, j: (0, 0))
    return pl.pallas_call(
        body, name="resid1_fwd", grid=(cfg.Bn, nbl),
        in_specs=[pl.BlockSpec((TB, D), lambda b, j: (b * nbt + 1 + j, 0)), la, ex, ex, ex, vec, vec],
        out_specs=[la, la],
        out_shape=[jax.ShapeDtypeStruct((cfg.N, D), F32), jax.ShapeDtypeStruct((cfg.N, D), BF16)],
        compiler_params=_cp(2),
    )(xs, x_mix, gate1, shift2, scale2, ln1_g, ln1_b)


def _resid1_bwd(cfg, dh2, dx1p, x1, xs, x_mix, gate1, scale2, ln1_g):
    TB, nbt, nbl = cfg.TB, cfg.nbt, cfg.nbl

    def body(dh2_ref, dx1p_ref, x1_ref, x_ref, xm_ref, g1_ref, sc_ref, lg_ref,
             dxm_ref, dxp_ref, ex_ref, gl_ref):
        b = pl.program_id(0)
        j = pl.program_id(1)

        @pl.when(j == 0)
        def _():
            ex_ref[...] = jnp.zeros_like(ex_ref)

        @pl.when(jnp.logical_and(b == 0, j == 0))
        def _():
            gl_ref[...] = jnp.zeros_like(gl_ref)

        dh2 = dh2_ref[...]
        xh2, rs2 = _ln(x1_ref[...])
        ex_ref[0:1, :] += jnp.sum(dh2, axis=0, keepdims=True)
        ex_ref[1:2, :] += jnp.sum(dh2 * xh2, axis=0, keepdims=True)
        dx1 = dx1p_ref[...] + _ln_bwd(dh2 * (1.0 + sc_ref[...]), xh2, rs2)
        xm = xm_ref[...]
        g1 = g1_ref[...]
        r1 = ALPHA * x_ref[...] + g1 * xm
        xh1, rs1 = _ln(r1)
        gl_ref[0:1, :] += jnp.sum(dx1 * xh1, axis=0, keepdims=True)
        gl_ref[1:2, :] += jnp.sum(dx1, axis=0, keepdims=True)
        dr1 = _ln_bwd(dx1 * lg_ref[...], xh1, rs1)
        ex_ref[2:3, :] += jnp.sum(dr1 * xm, axis=0, keepdims=True)
        dxm_ref[...] = (dr1 * g1).astype(BF16)
        dxp_ref[...] = ALPHA * dr1

    la = pl.BlockSpec((TB, D), lambda b, j: (b * nbl + j, 0))
    ex = pl.BlockSpec((None, 1, D), lambda b, j: (b, 0, 0))
    vec = pl.BlockSpec((1, D), lambda b, j: (0, 0))
    return pl.pallas_call(
        body, name="resid1_bwd", grid=(cfg.Bn, nbl),
        in_specs=[la, la, la, pl.BlockSpec((TB, D), lambda b, j: (b * nbt + 1 + j, 0)), la, ex, ex, vec],
        out_specs=[la, la, pl.BlockSpec((None, 8, D), lambda b, j: (b, 0, 0)),
                   pl.BlockSpec((8, D), lambda b, j: (0, 0))],
        out_shape=[jax.ShapeDtypeStruct((cfg.N, D), BF16), jax.ShapeDtypeStruct((cfg.N, D), F32),
                   jax.ShapeDtypeStruct((cfg.Bn, 8, D), F32), jax.ShapeDtypeStruct((8, D), F32)],
        compiler_params=_cp(2),
    )(dh2, dx1p, x1, xs, x_mix, gate1, scale2, ln1_g)


def _final_fwd_bwd(cfg, x1, mlp, b2, gate2, ln2_g, ln2_b, target):
    TB, nbl = cfg.TB, cfg.nbl

    def body(x1_ref, m_ref, b2_ref, g2_ref, lg_ref, lb_ref, t_ref, dm_ref, dx_ref, ex_ref, gl_ref):
        b = pl.program_id(0)
        j = pl.program_id(1)

        @pl.when(j == 0)
        def _():
            ex_ref[...] = jnp.zeros_like(ex_ref)

        @pl.when(jnp.logical_and(b == 0, j == 0))
        def _():
            gl_ref[...] = jnp.zeros_like(gl_ref)

        mv = m_ref[...] + b2_ref[...]
        g2 = g2_ref[...]
        r2 = ALPHA * x1_ref[...] + g2 * mv
        xh, rs = _ln(r2)
        lg = lg_ref[...]
        x2 = xh * lg + lb_ref[...]
        err = x2 - t_ref[...]
        ls = jnp.sum(jnp.sum(err * err, axis=1, keepdims=True), axis=0, keepdims=True) * (0.5 / D)
        gl_ref[3:4, :] += ls
        dx2 = err * (1.0 / D)
        gl_ref[0:1, :] += jnp.sum(dx2 * xh, axis=0, keepdims=True)
        gl_ref[1:2, :] += jnp.sum(dx2, axis=0, keepdims=True)
        dr2 = _ln_bwd(dx2 * lg, xh, rs)
        ex_ref[0:1, :] += jnp.sum(dr2 * mv, axis=0, keepdims=True)
        dmv = dr2 * g2
        gl_ref[2:3, :] += jnp.sum(dmv, axis=0, keepdims=True)
        dm_ref[...] = dmv.astype(BF16)
        dx_ref[...] = ALPHA * dr2

    la = pl.BlockSpec((TB, D), lambda b, j: (b * nbl + j, 0))
    ex = pl.BlockSpec((None, 1, D), lambda b, j: (b, 0, 0))
    vec = pl.BlockSpec((1, D), lambda b, j: (0, 0))
    return pl.pallas_call(
        body, name="final_fwd_bwd", grid=(cfg.Bn, nbl),
        in_specs=[la, la, vec, ex, vec, vec, la],
        out_specs=[la, la, pl.BlockSpec((None, 8, D), lambda b, j: (b, 0, 0)),
                   pl.BlockSpec((8, D), lambda b, j: (0, 0))],
        out_shape=[jax.ShapeDtypeStruct((cfg.N, D), BF16), jax.ShapeDtypeStruct((cfg.N, D), F32),
                   jax.ShapeDtypeStruct((cfg.Bn, 8, D), F32), jax.ShapeDtypeStruct((8, D), F32)],
        compiler_params=_cp(2),
    )(x1, mlp, b2, gate2, ln2_g, ln2_b, target)


def _ln_mod_bwd(cfg, dh_a, dh_b, xs, scale_tab, dxp):
    TB, nbt, nbl = cfg.TB, cfg.nbt, cfg.nbl

    def body(da_ref, db_ref, x_ref, sc_ref, dxp_ref, gx_ref, acc_ref):
        j = pl.program_id(1)

        @pl.when(j <= 1)
        def _():
            acc_ref[...] = jnp.zeros_like(acc_ref)

        dh = da_ref[...] + db_ref[...]
        xhat, rs = _ln(x_ref[...])
        acc_ref[0:1, :] += jnp.sum(dh, axis=0, keepdims=True)
        acc_ref[1:2, :] += jnp.sum(dh * xhat, axis=0, keepdims=True)
        gx_ref[...] = dxp_ref[...] + _ln_bwd(dh * (1.0 + sc_ref[...]), xhat, rs)

    st = pl.BlockSpec((TB, D), lambda b, j: (b * nbt + j, 0))
    la = pl.BlockSpec((TB, D), lambda b, j: (b * nbl + jnp.maximum(j - 1, 0), 0))
    return pl.pallas_call(
        body, name="ln_mod_bwd", grid=(cfg.Bn, nbt),
        in_specs=[st, st, st,
                  pl.BlockSpec((None, 1, D), lambda b, j: (2 * b + jnp.minimum(j, 1), 0, 0)), la],
        out_specs=[la, pl.BlockSpec((None, 8, D), lambda b, j: (2 * b + jnp.minimum(j, 1), 0, 0))],
        out_shape=[jax.ShapeDtypeStruct((cfg.N, D), F32), jax.ShapeDtypeStruct((2 * cfg.Bn, 8, D), F32)],
        compiler_params=_cp(2),
    )(dh_a, dh_b, xs, scale_tab, dxp)


def _perm_w_in(w_in):
    w_main = jnp.concatenate([w_in[:, 0:3072], w_in[:, 4160:5184], w_in[:, 3136:4160], w_in[:, 5184:10304]], axis=1)
    w_dt = jnp.pad(w_in[:, 3072:3136], ((0, 0), (0, DTW - 2 * NH)))
    return w_main, w_dt


def _unperm_w_in(dw_main, dw_dt):
    return jnp.concatenate([dw_main[:, 0:3072], dw_dt[:, :2 * NH], dw_main[:, 4096:5120],
                            dw_main[:, 3072:4096], dw_main[:, 5120:]], axis=1)


def _unpack_rest(rest_all):
    out, off = {}, 0
    for n, shp, axis in _BIG[1:]:
        shard_shape = (shp[0] // NDEV, shp[1]) if axis == 0 else (shp[0], shp[1] // NDEV)
        r = math.prod(shard_shape) // 1024
        out[n] = _from_slots(rest_all[:, off:off + r, :].reshape((NDEV,) + shard_shape), axis)
        off += r
    return out


def _local_step(cfg, x, ctx, target, m, mc, W, rest_payload):
    Bn, T, Tc = cfg.Bn, cfg.T, cfg.Tc
    NT, N = cfg.NT, cfg.N
    xs = jnp.concatenate([ctx, x], axis=1).reshape(NT, D)
    mch = [m[:, i * D:(i + 1) * D] for i in range(NMOD)]
    ctx_sh = jnp.broadcast_to(mc[None, :D], (Bn, D))
    ctx_sc = jnp.broadcast_to(mc[None, D:], (Bn, D))
    shift_tab = jnp.stack([ctx_sh, mch[0]], axis=1).reshape(2 * Bn, 1, D)
    scale_tab = jnp.stack([ctx_sc, mch[1]], axis=1).reshape(2 * Bn, 1, D)
    gate1 = mch[2].reshape(Bn, 1, D)
    shift2 = mch[3].reshape(Bn, 1, D)
    scale2 = mch[4].reshape(Bn, 1, D)
    gate2 = mch[5].reshape(Bn, 1, D)

    conv_w = jnp.concatenate([W["ssd_conv_w"], W["lru_conv_w"]], axis=1)
    conv_b = jnp.concatenate([W["ssd_conv_b"], W["lru_conv_b"]], axis=1)
    dt_bias = jnp.pad(W["ssd_dt_bias"].reshape(1, 2 * NH), ((0, 0), (0, DTW - 2 * NH)))
    a_log = jnp.pad(W["ssd_a_log"].reshape(1, 2 * NH), ((0, 0), (0, DTW - 2 * NH)))
    dvec = jnp.repeat(W["ssd_d"].reshape(NH), HD).reshape(1, DI)
    lba = W["lru_ba"].reshape(2, 1, LW)
    lbi = W["lru_bi"].reshape(2, 1, LW)
    llam = W["lru_lambda"].reshape(2, 1, LW)

    h = _ln_mod_fwd(cfg, xs, shift_tab, scale_tab)
    proj, rest_all = _mm(h, W["w_main"], "nn", "mm_proj", tm=1024, tn=2048, tk=1024, xch=(rest_payload, True))
    W = dict(W, **_unpack_rest(rest_all))
    dt_raw = _mm(h, W["w_dt"], "nn", "mm_dt", tm=512, tn=DTW, tk=1024)
    dt, dtg, cumg, cumTg = _dt_fwd(cfg, dt_raw, dt_bias, a_log)
    act, sgrad = _conv_fwd(cfg, proj, conv_w, conv_b)
    y_f, hs_f = _ssd_fwd(cfg, act, dtg, cumg, cumTg, False)
    y, hs_b = _ssd_fwd(cfg, act, dtg, cumg, cumTg, True, y_other=y_f, dvec=dvec)
    hss = [hs_f, hs_b]
    hls = [_lru_fwd(cfg, act, W["lru_wa"], W["lru_wi"], lba, lbi, llam, rev) for rev in (False, True)]
    nssd = _post_ssd_fwd(cfg, y, proj, W["ssd_norm_w"])
    vlru = _post_lru_fwd(cfg, hls[0], hls[1], proj)
    br_ssd = _mm(nssd, W["w_br_ssd"], "nn", "mm_br_ssd", tm=1024, tn=1024, tk=1024)
    br_lru = _mm(vlru, W["w_br_lru"], "nn", "mm_br_lru", tm=1024, tn=1024, tk=1024)
    mix = _merge_fwd(cfg, proj, W["b_gate"], br_ssd, br_lru)
    x_mix = _mm(mix, W["w_out"], "nn", "mm_out", tm=1024, tn=1024, tk=1024)
    x1, h2 = _resid1_fwd(cfg, xs, x_mix, gate1, shift2, scale2, W["ln1_g"], W["ln1_b"])
    a1, actm = _mm_mlp1(h2, W["w_mlp1"], W["b_mlp1"])
    mlp = _mm(actm, W["w_mlp2"], "nn", "mm_mlp2", tm=1024, tn=1024, tk=2048)
    dmlp, dx1p, ex2, gl2 = _final_fwd_bwd(cfg, x1, mlp, W["b_mlp2"], gate2, W["ln2_g"], W["ln2_b"],
                                          target.reshape(N, D))

    g = {}
    g["ln2_g"], g["ln2_b"], g["b_mlp2"] = gl2[0:1], gl2[1:2], gl2[2:3]
    loss_partial = gl2[3, 0]
    gw = {}
    gw["w_mlp2"] = _mm(actm, dmlp, "tn", "mm_dw_mlp2", BF16, tm=1024, tn=1024, tk=1024)
    da1, accb1 = _mm_dact(dmlp, W["w_mlp2"], a1)
    g["b_mlp1"] = accb1[0:1]
    dh2 = _mm(da1, W["w_mlp1"], "nt", "mm_dh2", tm=1024, tn=1024, tk=2048)
    gw["w_mlp1"] = _mm(h2, da1, "tn", "mm_dw_mlp1", BF16, tm=1024, tn=1024, tk=1024)
    dx_mix, dxp, ex1, gl1 = _resid1_bwd(cfg, dh2, dx1p, x1, xs, x_mix, gate1, scale2, W["ln1_g"])
    g["ln1_g"], g["ln1_b"] = gl1[0:1], gl1[1:2]
    dmix = _mm(dx_mix, W["w_out"], "nt", "mm_dmix", tm=1024, tn=1024, tk=1024)
    gw["w_out"] = _mm(mix, dx_mix, "tn", "mm_dw_out", BF16, tm=1024, tn=1024, tk=1024)
    dbs, dbl, dproj, accg = _merge_bwd(cfg, dmix, proj, W["b_gate"], br_ssd, br_lru)
    g["b_gate"] = accg[0:1]
    dnssd = _mm(dbs, W["w_br_ssd"], "nt", "mm_dnssd", tm=1024, tn=1024, tk=1024)
    gw["w_br_ssd"] = _mm(nssd, dbs, "tn", "mm_dw_br_ssd", BF16, tm=1024, tn=1024, tk=1024)
    dvlru = _mm(dbl, W["w_br_lru"], "nt", "mm_dvlru", tm=1024, tn=1024, tk=1024)
    gw["w_br_lru"] = _mm(vlru, dbl, "tn", "mm_dw_br_lru", BF16, tm=1024, tn=1024, tk=1024)
    dy, dproj, accs = _post_ssd_bwd(cfg, dproj, dnssd, y, act, proj, W["ssd_norm_w"])
    g["ssd_norm_w"] = accs[0:1]
    dD_cols = accs[1:2]
    dyl, dproj = _post_lru_bwd(cfg, dproj, dvlru, hls[0], hls[1], proj)

    rest_slots = jnp.concatenate([_to_slots(gw[n], axis).reshape(NDEV, -1, 1024) for n, _, axis in _BIG[1:]], axis=1)
    xres = {}
    dxh, dBs, dCs, dAs, dxxs, dus = [], [], [], [], [], []
    dwas, dwis, lvecs = [], [], []
    for i, rev in enumerate((False, True)):
        if i == 0:
            o, xres["rs_rest"] = _ssd_bwd(cfg, act, dtg, cumg, cumTg, hss[i], dy, rev, xch=(rest_slots, False))
        else:
            o = _ssd_bwd(cfg, act, dtg, cumg, cumTg, hss[i], dy, rev)
        dxh.append(o[0]); dBs.append(o[1]); dCs.append(o[2]); dAs.append(o[3]); dxxs.append(o[4])
        du, dwa, dwi, lv = _lru_bwd(cfg, act, W["lru_wa"], W["lru_wi"], lba, lbi, llam, hls[i], dyl, rev)
        dus.append(du); dwas.append(dwa); dwis.append(dwi); lvecs.append(lv)
    lru_payload = jnp.stack([jnp.stack(dwas), jnp.stack(dwis)]).reshape(-1, 1024)
    g["lru_ba"] = jnp.stack([lvecs[0][0], lvecs[1][0]])
    g["lru_bi"] = jnp.stack([lvecs[0][1], lvecs[1][1]])
    g["lru_lambda"] = jnp.stack([lvecs[0][2], lvecs[1][2]])

    ddt_raw, accdt = _dt_bwd(cfg, dAs, dxxs, dt_raw, dt, dt_bias, a_log)
    g["ssd_a_log"] = accdt[0, :2 * NH].reshape(2, NH)
    g["ssd_dt_bias"] = accdt[1, :2 * NH].reshape(2, NH)

    (dproj, accx), xres["ag_lru"] = _conv_bwd(cfg, "conv_bwd_x", dproj, proj, conv_w, sgrad, [dxh[0], dxh[1]], 0, DI,
                                              skip=(dy, dvec), xch=(lru_payload, True))
    dproj, accB = _conv_bwd(cfg, "conv_bwd_b", dproj, proj, conv_w, sgrad, [dBs[0], dBs[1]], DI, NG * NS)
    dproj, accC = _conv_bwd(cfg, "conv_bwd_c", dproj, proj, conv_w, sgrad, [dCs[0], dCs[1]], DI + NG * NS, NG * NS)
    dproj, accl = _conv_bwd(cfg, "conv_bwd_lru", dproj, proj, conv_w, None, [dus[0], dus[1]], DI + 2 * NG * NS, LW)
    accssd = jnp.concatenate([accx, accB, accC], axis=1)
    g["ssd_conv_w"], g["ssd_conv_b"] = accssd[0:4], accssd[4:5]
    g["lru_conv_w"], g["lru_conv_b"] = accl[0:4], accl[4:5]
    dw_main = _mm(h, dproj, "tn", "mm_dw_main", BF16, tm=1024, tn=2048, tk=1024)
    dw_dt = _mm(h, ddt_raw, "tn", "mm_dw_dt", BF16, tm=1024, tn=DTW, tk=512)
    w_in_slots = _to_slots(_unperm_w_in(dw_main, dw_dt), 1)
    dh_a, xres["rs_w_in"] = _mm(dproj, W["w_main"], "nt", "mm_dh_main", tm=1024, tn=1024, tk=2048,
                                xch=(w_in_slots, False))
    dh_b = _mm(ddt_raw, W["w_dt"], "nt", "mm_dh_dt", tm=512, tn=1024, tk=DTW)
    grad_x, acct = _ln_mod_bwd(cfg, dh_a, dh_b, xs, scale_tab, dxp)
    acct = acct.reshape(Bn, 2, 8, D)
    dm = jnp.concatenate([acct[:, 1, 0], acct[:, 1, 1], ex1[:, 2], ex1[:, 0], ex1[:, 1], ex2[:, 0]], axis=1)
    dmc = jnp.concatenate([acct[:, 0, 0], acct[:, 0, 1]], axis=1)
    g["ssd_d_cols"] = dD_cols
    return loss_partial, grad_x.reshape(Bn, T, D), g, dm, dmc, xres


MESH = pl.DeviceIdType.MESH
_HBM = pl.BlockSpec(memory_space=pltpu.HBM)


def _me():
    return 4 * lax.axis_index("x") + 2 * lax.axis_index("y") + lax.axis_index("c")


def _peer(k):
    px = (lax.axis_index("x") + ((k >> 2) & 1)) % 2
    py = (lax.axis_index("y") + ((k >> 1) & 1)) % 2
    pc = (lax.axis_index("c") + (k & 1)) % 2
    return (px, py, pc), 4 * px + 2 * py + pc


def _xchg_copies(x_ref, o_ref, send_sems, recv_sems, loc_sem, gather):
    me = _me()
    src_me = x_ref if gather else x_ref.at[me]
    loc = pltpu.make_async_copy(src_me, o_ref.at[me], loc_sem)
    sends, recvs = [], []
    for k in range(1, NDEV):
        peer, pid = _peer(k)
        sends.append(pltpu.make_async_remote_copy(
            src_ref=x_ref if gather else x_ref.at[pid], dst_ref=o_ref.at[me],
            send_sem=send_sems.at[k - 1], recv_sem=recv_sems.at[k - 1],
            device_id=peer, device_id_type=MESH))
        recvs.append(pltpu.make_async_remote_copy(
            src_ref=src_me, dst_ref=o_ref.at[pid],
            send_sem=send_sems.at[k - 1], recv_sem=recv_sems.at[k - 1],
            device_id=peer, device_id_type=MESH))
    return loc, sends, recvs


def _xchg_start(*refs, gather):
    loc, sends, _ = _xchg_copies(*refs, gather)
    loc.start()
    for cp in sends:
        cp.start()


def _xchg_wait(*refs, gather):
    loc, sends, recvs = _xchg_copies(*refs, gather)
    for cp in recvs:
        cp.wait_recv()
    for cp in sends:
        cp.wait_send()
    loc.wait()


_XCHG_SCRATCH = [pltpu.SemaphoreType.DMA((NDEV - 1,)), pltpu.SemaphoreType.DMA((NDEV - 1,)), pltpu.SemaphoreType.DMA]


def _xchg_out_shape(x, gather):
    return jax.ShapeDtypeStruct((NDEV,) + tuple(x.shape if gather else x.shape[1:]), x.dtype)


def _exchange(x, name, gather):
    def body(x_ref, o_ref, send_sems, recv_sems, loc_sem):
        _xchg_start(x_ref, o_ref, send_sems, recv_sems, loc_sem, gather=gather)
        _xchg_wait(x_ref, o_ref, send_sems, recv_sems, loc_sem, gather=gather)

    return pl.pallas_call(
        body, name=name, out_shape=_xchg_out_shape(x, gather),
        in_specs=[_HBM], out_specs=_HBM, scratch_shapes=_XCHG_SCRATCH,
    )(x)


def _gather_two_level(x, name):
    def body(x_ref, o_ref, send_sems, recv_sems, loc_sem):
        mx, my, mc = lax.axis_index("x"), lax.axis_index("y"), lax.axis_index("c")
        me, sibling = (mx, my, mc), (mx, my, 1 - mc)
        chips = [(1 - mx, my), (mx, 1 - my), (1 - mx, 1 - my)]

        def slot(px, py, pc):
            return o_ref.at[4 * px + 2 * py + pc]

        def copy(k, block, to, src=None):
            return pltpu.make_async_remote_copy(
                src_ref=slot(*block) if src is None else src, dst_ref=slot(*block),
                send_sem=send_sems.at[k], recv_sem=recv_sems.at[k], device_id=to, device_id_type=MESH)

        mine = pltpu.make_async_copy(x_ref, slot(*me), loc_sem)
        mine.start()
        first = [copy(0, me, sibling, src=x_ref)]
        first += [copy(1 + j, me, (*chip, mc), src=x_ref) for j, chip in enumerate(chips)]
        for cp in first:
            cp.start()
        passed = [copy(4 + j, (*chip, mc), sibling) for j, chip in enumerate(chips)]
        for j, chip in enumerate(chips):
            copy(1 + j, (*chip, mc), me).wait_recv()
            passed[j].start()
        copy(0, sibling, me).wait_recv()
        for j, chip in enumerate(chips):
            copy(4 + j, (*chip, 1 - mc), me).wait_recv()
        for cp in first + passed:
            cp.wait_send()
        mine.wait()

    return pl.pallas_call(
        body, name=name, out_shape=_xchg_out_shape(x, True),
        in_specs=[_HBM], out_specs=_HBM, scratch_shapes=_XCHG_SCRATCH,
    )(x)


def _hosted_call(body, xch, *, name, grid, in_specs, out_specs, out_shape, scratch_shapes, compiler_params, args,
                 aliases=None):
    aliases = aliases or {}
    if xch is None:
        return pl.pallas_call(body, name=name, grid=grid, in_specs=in_specs, out_specs=out_specs,
                              out_shape=out_shape, scratch_shapes=scratch_shapes, input_output_aliases=aliases,
                              compiler_params=compiler_params)(*args)
    xv, gather = xch
    n_in, n_out, n_scr = len(in_specs), len(out_specs), len(scratch_shapes)

    def wrapped(*refs):
        ins = refs[:n_in]
        x_ref = refs[n_in]
        outs = refs[n_in + 1:n_in + 1 + n_out]
        o_ref = refs[n_in + 1 + n_out]
        scr = refs[n_in + 2 + n_out:]
        own, sems = scr[:n_scr], scr[n_scr:]
        first = functools.reduce(jnp.logical_and, [pl.program_id(a) == 0 for a in range(len(grid))])
        last = functools.reduce(jnp.logical_and, [pl.program_id(a) == grid[a] - 1 for a in range(len(grid))])

        @pl.when(first)
        def _():
            _xchg_start(x_ref, o_ref, *sems, gather=gather)

        body(*ins, *outs, *own)

        @pl.when(last)
        def _():
            _xchg_wait(x_ref, o_ref, *sems, gather=gather)

    res = pl.pallas_call(
        wrapped, name=name, grid=grid, in_specs=list(in_specs) + [_HBM], out_specs=list(out_specs) + [_HBM],
        out_shape=list(out_shape) + [_xchg_out_shape(xv, gather)],
        scratch_shapes=list(scratch_shapes) + _XCHG_SCRATCH, input_output_aliases=aliases,
        compiler_params=compiler_params,
    )(*args, xv)
    return list(res[:n_out]), res[n_out]


def _row_tile(R, cap, mult=8):
    best = mult
    t = mult
    while t <= min(R, cap):
        if R % t == 0:
            best = t
        t += mult
    assert R % best == 0, R
    return best


def _sum_slots(x, name, xch=None):
    _, R, C = x.shape
    tr = _row_tile(R, 256, 16 if x.dtype == BF16 else 8)

    def body(x_ref, o_ref):
        o_ref[...] = _slot_sum(x_ref)

    res = _hosted_call(
        body, xch, name=name, grid=(R // tr,),
        in_specs=[pl.BlockSpec((NDEV, tr, C), lambda i: (0, i, 0))],
        out_specs=[pl.BlockSpec((tr, C), lambda i: (i, 0))],
        out_shape=[jax.ShapeDtypeStruct((R, C), F32)],
        scratch_shapes=[], compiler_params=_cp(1), args=(x,))
    if xch is None:
        return res[0]
    return res[0][0], res[1]


def _slot_sum(x_ref):
    acc = x_ref[0].astype(F32)
    for i in range(1, NDEV):
        acc = acc + x_ref[i].astype(F32)
    return acc


def _sum_adamw(slots, w, m, v, name, xch=None):
    _, R, C = slots.shape
    tr = _row_tile(R, 128, 16 if slots.dtype == BF16 else 8)

    def body(x_ref, w_ref, m_ref, v_ref, g_ref, d_ref, nm_ref, nv_ref):
        g_ref[...] = _slot_sum(x_ref)
        _adamw_update(w_ref, g_ref, m_ref, v_ref, d_ref, nm_ref, nv_ref)

    blk = pl.BlockSpec((tr, C), lambda i: (i, 0))
    res = _hosted_call(
        body, xch, name=name, grid=(R // tr,),
        in_specs=[pl.BlockSpec((NDEV, tr, C), lambda i: (0, i, 0)), blk, blk, blk],
        out_specs=[blk] * 4, out_shape=[jax.ShapeDtypeStruct((R, C), F32)] * 4,
        scratch_shapes=[], compiler_params=_cp(1), args=(slots, w, m, v))
    if xch is None:
        return res
    return res[0], res[1]


def _adamw_update(w_ref, g_ref, m_ref, v_ref, d_ref, nm_ref, nv_ref):
    c1 = 1.0 / (1.0 - ADAM_B1 ** ADAM_STEP)
    c2 = 1.0 / (1.0 - ADAM_B2 ** ADAM_STEP)
    gv = g_ref[...]
    nm = ADAM_B1 * m_ref[...] + (1.0 - ADAM_B1) * gv
    nv = ADAM_B2 * v_ref[...] + (1.0 - ADAM_B2) * (gv * gv)
    d_ref[...] = -ADAM_LR * ((nm * c1) / (jnp.sqrt(nv * c2) + ADAM_EPS) + ADAM_WD * w_ref[...])
    nm_ref[...] = nm
    nv_ref[...] = nv


def _adamw_many(ws, gs, ms, vs):
    n = len(ws)

    def body(*refs):
        for i in range(n):
            _adamw_update(refs[i], refs[n + i], refs[2 * n + i], refs[3 * n + i],
                          refs[4 * n + i], refs[5 * n + i], refs[6 * n + i])

    shapes = [jax.ShapeDtypeStruct(w.shape, F32) for w in ws]
    res = pl.pallas_call(
        body, name="adamw_small", out_shape=shapes * 3,
        compiler_params=pltpu.CompilerParams(vmem_limit_bytes=VMEM_LIMIT_BYTES),
    )(*ws, *gs, *ms, *vs)
    return res[:n], res[n:2 * n], res[2 * n:]


def _adamw(w, g, m, v, name):
    R, C = w.shape
    tr = _row_tile(R, 256)

    def body(w_ref, g_ref, m_ref, v_ref, d_ref, nm_ref, nv_ref):
        _adamw_update(w_ref, g_ref, m_ref, v_ref, d_ref, nm_ref, nv_ref)

    blk = pl.BlockSpec((tr, C), lambda i: (i, 0))
    return pl.pallas_call(
        body, name=name, grid=(R // tr,),
        in_specs=[blk] * 4, out_specs=[blk] * 3,
        out_shape=[jax.ShapeDtypeStruct((R, C), F32)] * 3,
        compiler_params=_cp(1),
    )(w, g, m, v)


def _mod_fwd(c_rows, w_shard, b_shard):
    def body(c_ref, w_ref, b_ref, o_ref):
        s = _silu(c_ref[...]).astype(BF16)
        o_ref[...] = _dot(s, w_ref[...].astype(BF16)) + b_ref[...]

    return pl.pallas_call(
        body, name="mod_fwd",
        out_shape=jax.ShapeDtypeStruct((c_rows.shape[0], w_shard.shape[1]), F32),
        compiler_params=pltpu.CompilerParams(vmem_limit_bytes=VMEM_LIMIT_BYTES),
    )(c_rows, w_shard, b_shard)


def _mod_bwd(c_rows, dm_all, dm_shard, w_shard):
    nrow = c_rows.shape[0]

    def body(c_ref, da_ref, ds_ref, w_ref, gw_ref, gb_ref, cc_ref):
        s = _silu(c_ref[...]).astype(BF16)
        ds = ds_ref[...]
        gw_ref[...] = _dot_tn(s, ds.astype(BF16))
        gb_ref[...] = jnp.sum(da_ref[...], axis=0, keepdims=True)
        rowi = lax.broadcasted_iota(jnp.int32, ds.shape, 0)
        dmc = jnp.sum(jnp.where(rowi % 8 >= 4, ds, 0.0), axis=0, keepdims=True)
        dmc8 = jnp.broadcast_to(dmc, (8, ds.shape[1])).astype(BF16)
        cc_ref[...] = _dot_nt(dmc8, w_ref[...].astype(BF16))

    return pl.pallas_call(
        body, name="mod_bwd",
        out_shape=[jax.ShapeDtypeStruct(w_shard.shape, F32),
                   jax.ShapeDtypeStruct((1, dm_all.shape[1]), F32),
                   jax.ShapeDtypeStruct((8, D), F32)],
        compiler_params=pltpu.CompilerParams(vmem_limit_bytes=VMEM_LIMIT_BYTES),
    )(c_rows, dm_all, dm_shard, w_shard)


def _small_finish(cc_pre, c_ctx, dd_cols):
    def body(cc_ref, c_ref, dd_ref, gc_ref, gd_ref):
        gc_ref[...] = cc_ref[...] * _silu_grad(c_ref[...])
        gd_ref[...] = jnp.sum(dd_ref[...], axis=1, keepdims=True)

    return pl.pallas_call(
        body, name="small_finish",
        out_shape=[jax.ShapeDtypeStruct((1, D), F32), jax.ShapeDtypeStruct((NH, 1), F32)],
    )(cc_pre, c_ctx, dd_cols)


_BIG = (("w_in", (D, 10304), 1), ("w_br_ssd", (DI, D), 0), ("w_br_lru", (LW, D), 0), ("w_out", (D, D), 0),
        ("w_mlp1", (D, MLP), 1), ("w_mlp2", (MLP, D), 0))
_SMALL_SH = (("ssd_conv_w", (4, 4096)), ("lru_conv_w", (4, LW)), ("lru_ba", (2, LW)), ("lru_bi", (2, LW)),
             ("lru_lambda", (2, LW)))
_REPL = (("c_ctx", (D,)), ("b_gate", (2 * D,)), ("ssd_conv_b", (4096,)), ("ssd_dt_bias", (2, NH)),
         ("ssd_a_log", (2, NH)), ("ssd_d", (DI,)), ("ssd_norm_w", (DI,)), ("lru_conv_b", (LW,)),
         ("ln1_g", (D,)), ("ln1_b", (D,)),
         ("b_mlp1", (MLP,)), ("b_mlp2", (D,)), ("ln2_g", (D,)), ("ln2_b", (D,)))

_WEIGHT_NAMES = ('c_ctx', 'w_mod', 'b_mod', 'w_in', 'b_gate', 'ssd_conv_w', 'ssd_conv_b', 'ssd_dt_bias', 'ssd_a_log',
                 'ssd_d', 'ssd_norm_w', 'lru_conv_w', 'lru_conv_b', 'lru_wa', 'lru_ba', 'lru_wi', 'lru_bi',
                 'lru_lambda', 'w_br_ssd', 'w_br_lru', 'w_out', 'ln1_g', 'ln1_b', 'w_mlp1', 'b_mlp1', 'w_mlp2',
                 'b_mlp2', 'ln2_g', 'ln2_b')
_ARG_NAMES = ('x', 'c', 'ctx') + _WEIGHT_NAMES + ('loss_target',) + tuple('m_' + n for n in _WEIGHT_NAMES) + tuple(
    'v_' + n for n in _WEIGHT_NAMES)


def _to_slots(full, axis):
    n = full.shape[axis] // NDEV
    if axis == 0:
        return full.reshape(NDEV, n, full.shape[1])
    return full.reshape(full.shape[0], NDEV, n).transpose(1, 0, 2)


def _from_slots(slots, axis):
    if axis == 0:
        return slots.reshape(NDEV * slots.shape[1], slots.shape[2])
    return slots.transpose(1, 0, 2).reshape(slots.shape[1], NDEV * slots.shape[2])


def _pack_rows(arrs, width=1024, mult=8):
    flat = jnp.concatenate([a.reshape(-1) for a in arrs])
    n = flat.shape[0]
    per = width * mult
    tot = -(-n // per) * per
    return jnp.pad(flat, (0, tot - n)).reshape(tot // width, width)


def _unpack_rows(packed, shapes, lead=()):
    nl = len(lead)
    flat = packed.reshape(tuple(lead) + (-1,))
    out, off = [], 0
    for s in shapes:
        n = math.prod(s)
        out.append(flat[..., off:off + n].reshape(tuple(lead) + tuple(s)))
        off += n
    return out


def kernel(x, c, ctx, c_ctx, w_mod, b_mod, w_in, b_gate, ssd_conv_w, ssd_conv_b, ssd_dt_bias, ssd_a_log, ssd_d, ssd_norm_w, lru_conv_w, lru_conv_b, lru_wa, lru_ba, lru_wi, lru_bi, lru_lambda, w_br_ssd, w_br_lru, w_out, ln1_g, ln1_b, w_mlp1, b_mlp1, w_mlp2, b_mlp2, ln2_g, ln2_b, loss_target, m_c_ctx, m_w_mod, m_b_mod, m_w_in, m_b_gate, m_ssd_conv_w, m_ssd_conv_b, m_ssd_dt_bias, m_ssd_a_log, m_ssd_d, m_ssd_norm_w, m_lru_conv_w, m_lru_conv_b, m_lru_wa, m_lru_ba, m_lru_wi, m_lru_bi, m_lru_lambda, m_w_br_ssd, m_w_br_lru, m_w_out, m_ln1_g, m_ln1_b, m_w_mlp1, m_b_mlp1, m_w_mlp2, m_b_mlp2, m_ln2_g, m_ln2_b, v_c_ctx, v_w_mod, v_b_mod, v_w_in, v_b_gate, v_ssd_conv_w, v_ssd_conv_b, v_ssd_dt_bias, v_ssd_a_log, v_ssd_d, v_ssd_norm_w, v_lru_conv_w, v_lru_conv_b, v_lru_wa, v_lru_ba, v_lru_wi, v_lru_bi, v_lru_lambda, v_w_br_ssd, v_w_br_lru, v_w_out, v_ln1_g, v_ln1_b, v_w_mlp1, v_b_mlp1, v_w_mlp2, v_b_mlp2, v_ln2_g, v_ln2_b):
    A = dict(zip(_ARG_NAMES, (x, c, ctx, c_ctx, w_mod, b_mod, w_in, b_gate, ssd_conv_w, ssd_conv_b, ssd_dt_bias, ssd_a_log, ssd_d, ssd_norm_w, lru_conv_w, lru_conv_b, lru_wa, lru_ba, lru_wi, lru_bi, lru_lambda, w_br_ssd, w_br_lru, w_out, ln1_g, ln1_b, w_mlp1, b_mlp1, w_mlp2, b_mlp2, ln2_g, ln2_b, loss_target, m_c_ctx, m_w_mod, m_b_mod, m_w_in, m_b_gate, m_ssd_conv_w, m_ssd_conv_b, m_ssd_dt_bias, m_ssd_a_log, m_ssd_d, m_ssd_norm_w, m_lru_conv_w, m_lru_conv_b, m_lru_wa, m_lru_ba, m_lru_wi, m_lru_bi, m_lru_lambda, m_w_br_ssd, m_w_br_lru, m_w_out, m_ln1_g, m_ln1_b, m_w_mlp1, m_b_mlp1, m_w_mlp2, m_b_mlp2, m_ln2_g, m_ln2_b, v_c_ctx, v_w_mod, v_b_mod, v_w_in, v_b_gate, v_ssd_conv_w, v_ssd_conv_b, v_ssd_dt_bias, v_ssd_a_log, v_ssd_d, v_ssd_norm_w, v_lru_conv_w, v_lru_conv_b, v_lru_wa, v_lru_ba, v_lru_wi, v_lru_bi, v_lru_lambda, v_w_br_ssd, v_w_br_lru, v_w_out, v_ln1_g, v_ln1_b, v_w_mlp1, v_b_mlp1, v_w_mlp2, v_b_mlp2, v_ln2_g, v_ln2_b)))
    Bn, T, _ = x.shape
    Tc = ctx.shape[1]
    cfg = _Cfg(Bn, T, Tc)
    me = _me()
    L = {n: (A[n] if n == "c_ctx" else A[n][0]) for n in _WEIGHT_NAMES}
    nmod = L["w_mod"].shape[1]

    c_all = _exchange(c, "ag_c", True)
    c_rows = jnp.concatenate([c_all.reshape(NDEV * Bn, D), jnp.broadcast_to(c_ctx[None, :], (8, D))], axis=0)
    b_shard = lax.dynamic_slice(L["b_mod"], (me * nmod,), (nmod,)).reshape(1, nmod)
    m_part = _mod_fwd(c_rows, L["w_mod"], b_shard)
    m_all = _exchange(m_part, "ag_mod", True)
    m_full = m_all.transpose(1, 0, 2).reshape(NDEV * Bn + 8, NMOD * D)
    m_mine = lax.dynamic_slice(m_full, (me * Bn, 0), (Bn, NMOD * D))
    mc = m_full[NDEV * Bn, :2 * D]

    w_in_all = _gather_two_level(L["w_in"].astype(BF16), "ag_w_in")
    rest_payload = jnp.concatenate([L[n].astype(BF16).reshape(-1, 1024) for n, _, _ in _BIG[1:]], axis=0)
    small_shapes = [(s[0], s[1] // NDEV) for _, s in _SMALL_SH]
    small_all = _exchange(_pack_rows([L[n] for n, _ in _SMALL_SH], width=512), "ag_w_small", True)
    W = {}
    for (n, shp), piece in zip(_SMALL_SH, _unpack_rows(small_all, small_shapes, lead=(NDEV,))):
        W[n] = piece.transpose(1, 0, 2).reshape(shp)
    W["w_main"], W["w_dt"] = _perm_w_in(_from_slots(w_in_all, 1))
    for n in ("ssd_conv_b", "lru_conv_b", "ssd_norm_w", "b_gate", "ln1_g", "ln1_b", "b_mlp1", "b_mlp2", "ln2_g", "ln2_b"):
        W[n] = L[n].reshape(1, -1)
    for n in ("ssd_dt_bias", "ssd_a_log", "ssd_d", "lru_wa", "lru_wi"):
        W[n] = L[n]

    loss_part, grad_x, g, dm, dmc, xres = _local_step(cfg, x, ctx, loss_target, m_mine, mc, W, rest_payload)
    loss = lax.psum(loss_part, ("x", "y", "c"))

    dmc_pad = jnp.pad(dmc, ((0, 4 - Bn), (0, (NMOD - 2) * D)))
    dm_payload = jnp.concatenate([jnp.pad(dm, ((0, 4 - Bn), (0, 0))), dmc_pad], axis=0)
    upd_w_in, dm_all = _sum_adamw(xres["rs_w_in"], L["w_in"], A["m_w_in"][0], A["v_w_in"][0], "sum_adamw_w_in",
                                  xch=(dm_payload, True))
    dm_all = dm_all.reshape(NDEV * 8, NMOD * D)
    c_rows_b = jnp.concatenate([jnp.pad(c_all, ((0, 0), (0, 4 - Bn), (0, 0))),
                                jnp.broadcast_to(c_ctx[None, None, :], (NDEV, 4, D))], axis=1).reshape(NDEV * 8, D)
    dm_shard = lax.dynamic_slice(dm_all, (0, me * nmod), (NDEV * 8, nmod))
    g_w_mod, g_b_mod, cc_part = _mod_bwd(c_rows_b, dm_all, dm_shard, L["w_mod"])
    g["c_ctx"] = cc_part[0]

    g["ssd_d"] = g.pop("ssd_d_cols")
    small_names = [n for n, _ in _REPL] + [n for n, _ in _SMALL_SH]
    small_full_shapes = [s for _, s in _REPL] + [s for _, s in _SMALL_SH]
    red_b, sm_all = _sum_slots(xres["rs_rest"], "sum_w_rest", xch=(_pack_rows([g[n] for n in small_names]), True))
    sm_sum = _sum_slots(sm_all, "sum_g_small")
    gs = dict(zip(small_names, _unpack_rows(sm_sum, small_full_shapes)))
    gcc, gdd = _small_finish(gs["c_ctx"].reshape(1, D), c_ctx.reshape(1, D), gs["ssd_d"].reshape(NH, HD))
    gs["c_ctx"] = gcc.reshape(D)
    gs["ssd_d"] = gdd.reshape(NH)
    for n, shp in _SMALL_SH:
        ns = shp[1] // NDEV
        gs[n] = lax.dynamic_slice(gs[n], (0, me * ns), (shp[0], ns))
    gs["b_mod"] = g_b_mod.reshape(NMOD * D)
    lru_sum = _sum_slots(xres["ag_lru"], "sum_g_lru").reshape(2, 2, LB, LBW, LBW)
    gs["lru_wa"], gs["lru_wi"] = lru_sum[0], lru_sum[1]

    gb = {}
    off = 0
    for n, shp, axis in _BIG[1:]:
        shard_shape = (shp[0] // NDEV, shp[1]) if axis == 0 else (shp[0], shp[1] // NDEV)
        r = math.prod(shard_shape) // 1024
        gb[n] = red_b[off:off + r].reshape(shard_shape)
        off += r
    gb["w_mod"] = g_w_mod

    grads, deltas, new_m, new_v = {}, {}, {}, {}
    big_names = ["w_mod"] + [n for n, _, _ in _BIG]
    grads["w_in"], deltas["w_in"], new_m["w_in"], new_v["w_in"] = upd_w_in
    for n in big_names:
        if n == "w_in":
            continue
        d_, nm_, nv_ = _adamw(L[n], gb[n], A["m_" + n][0], A["v_" + n][0], "adamw_" + n)
        grads[n], deltas[n], new_m[n], new_v[n] = gb[n], d_, nm_, nv_
    sm_names = [n for n in _WEIGHT_NAMES if n not in big_names]

    def two_d(a):
        return a.reshape(1, -1) if a.ndim == 1 else a
    loc = lambda pre: [two_d(A[pre + n] if n == "c_ctx" else A[pre + n][0]) for n in sm_names]
    gsm = [two_d(gs[n].reshape(L[n].shape)) for n in sm_names]
    ds_, nms_, nvs_ = _adamw_many(loc(""), gsm, loc("m_"), loc("v_"))
    for n, gv, dv, mv, vv in zip(sm_names, gsm, ds_, nms_, nvs_):
        shp = L[n].shape
        grads[n], deltas[n], new_m[n], new_v[n] = gv.reshape(shp), dv.reshape(shp), mv.reshape(shp), vv.reshape(shp)

    def out(dct):
        return [dct[n] if n == "c_ctx" else dct[n][None] for n in _WEIGHT_NAMES]
    return (loss, grad_x, *out(grads), *out(deltas), *out(new_m), *out(new_v))
```

```python
import functools
import math

import jax
import jax.numpy as jnp
from jax import lax
from jax.experimental import pallas as pl
from jax.experimental.pallas import tpu as pltpu

F32 = jnp.float32
BF16 = jnp.bfloat16

D = 1024
GRID_W = 64
DI = 2048
NH = 32
HD = 64
NG = 8
HPG = 4
NS = 128
CH = 128
LW = 1024
LB = 8
LBW = 128
LRU_C = 8.0
MLP = 4096
NMOD = 6
ALPHA = 2.0 ** 0.25
LN_EPS = 1e-6
RMS_EPS = 1e-5
PM = 10240
DTW = 128
CONVW = 5120
NDEV = 8

ADAM_LR = 0.001
ADAM_B1 = 0.9
ADAM_B2 = 0.999
ADAM_EPS = 1e-08
ADAM_WD = 0.01
ADAM_STEP = 10

VMEM_LIMIT_BYTES = 56 * 1024 * 1024


def _cp(n_axes):
    return pltpu.CompilerParams(dimension_semantics=("arbitrary",) * n_axes,
                                vmem_limit_bytes=VMEM_LIMIT_BYTES)


def _sigmoid(x):
    return 0.5 * jnp.tanh(0.5 * x) + 0.5


def _silu(x):
    return x * _sigmoid(x)


def _silu_grad(x):
    s = _sigmoid(x)
    return s * (1.0 + x * (1.0 - s))


def _log1p_pos(e):
    return jnp.where(e < 1e-2, e * (1.0 - e * (0.5 - e * (1.0 / 3.0))), jnp.log(1.0 + e))


def _softplus(x):
    return jnp.maximum(x, 0.0) + _log1p_pos(jnp.exp(-jnp.abs(x)))


_GELU_K = math.sqrt(2.0 / math.pi)


def _gelu(x):
    t = jnp.tanh(_GELU_K * (x + 0.044715 * x * x * x))
    return 0.5 * x * (1.0 + t)


def _gelu_grad(x):
    t = jnp.tanh(_GELU_K * (x + 0.044715 * x * x * x))
    dt = (1.0 - t * t) * _GELU_K * (1.0 + 3.0 * 0.044715 * x * x)
    return 0.5 * (1.0 + t) + 0.5 * x * dt


def _ln(x):
    mu = jnp.mean(x, axis=-1, keepdims=True)
    xc = x - mu
    var = jnp.mean(xc * xc, axis=-1, keepdims=True)
    rs = lax.rsqrt(var + LN_EPS)
    return xc * rs, rs


def _ln_bwd(dy, xhat, rs):
    m1 = jnp.mean(dy, axis=-1, keepdims=True)
    m2 = jnp.mean(dy * xhat, axis=-1, keepdims=True)
    return rs * (dy - m1 - xhat * m2)


def _dot(a, b):
    return lax.dot_general(a, b, (((1,), (0,)), ((), ())), preferred_element_type=F32)


def _dot_nt(a, b):
    return lax.dot_general(a, b, (((1,), (1,)), ((), ())), preferred_element_type=F32)


def _dot_tn(a, b):
    return lax.dot_general(a, b, (((0,), (0,)), ((), ())), preferred_element_type=F32)


def _split3(a):
    a0 = a.astype(BF16)
    r = a - a0.astype(F32)
    a1 = r.astype(BF16)
    a2 = (r - a1.astype(F32)).astype(BF16)
    return a0, a1, a2


def _dot_exact_l(m_bf, a):
    a0, a1, a2 = _split3(a)
    return _dot(m_bf, a0) + _dot(m_bf, a1) + _dot(m_bf, a2)


def _dot_hilo_r(a, m_bf):
    a0 = a.astype(BF16)
    a1 = (a - a0.astype(F32)).astype(BF16)
    return _dot(a0, m_bf) + _dot(a1, m_bf)


def _tri(n, upper):
    ii = lax.broadcasted_iota(jnp.int32, (n, n), 0)
    kk = lax.broadcasted_iota(jnp.int32, (n, n), 1)
    m = (kk >= ii) if upper else (kk <= ii)
    return jnp.where(m, 1.0, 0.0).astype(BF16)


def _fit(n, t):
    t = min(t, n)
    while n % t:
        t //= 2
    return t


def _mm(a, b, mode, name, out_dtype=F32, tm=512, tn=512, tk=512, xch=None):
    if mode == "nn":
        M, K = a.shape
        N = b.shape[1]
    elif mode == "nt":
        M, K = a.shape
        N = b.shape[0]
    else:
        K, M = a.shape
        N = b.shape[1]
    tm, tn, tk = _fit(M, tm), _fit(N, tn), _fit(K, tk)
    assert M % tm == 0 and N % tn == 0 and K % tk == 0, (name, M, N, K, tm, tn, tk)
    nk = K // tk
    if mode == "tn":
        a_spec = pl.BlockSpec((tk, tm), lambda i, j, k: (k, i))
    else:
        a_spec = pl.BlockSpec((tm, tk), lambda i, j, k: (i, k))
    if mode == "nt":
        b_spec = pl.BlockSpec((tn, tk), lambda i, j, k: (j, k))
    else:
        b_spec = pl.BlockSpec((tk, tn), lambda i, j, k: (k, j))
    dn = {"nn": (((1,), (0,)), ((), ())), "nt": (((1,), (1,)), ((), ())), "tn": (((0,), (0,)), ((), ()))}[mode]

    def body(a_ref, b_ref, o_ref, acc_ref):
        k = pl.program_id(2)

        @pl.when(k == 0)
        def _():
            acc_ref[...] = jnp.zeros_like(acc_ref)

        acc_ref[...] += lax.dot_general(a_ref[...].astype(BF16), b_ref[...].astype(BF16), dn,
                                        preferred_element_type=F32)

        @pl.when(k == nk - 1)
        def _():
            o_ref[...] = acc_ref[...].astype(o_ref.dtype)

    def body_one_step(a_ref, b_ref, o_ref):
        o_ref[...] = lax.dot_general(a_ref[...].astype(BF16), b_ref[...].astype(BF16), dn,
                                     preferred_element_type=F32).astype(o_ref.dtype)

    res = _hosted_call(
        body if nk > 1 else body_one_step, xch, name=name, grid=(M // tm, N // tn, nk),
        in_specs=[a_spec, b_spec],
        out_specs=[pl.BlockSpec((tm, tn), lambda i, j, k: (i, j))],
        out_shape=[jax.ShapeDtypeStruct((M, N), out_dtype)],
        scratch_shapes=[pltpu.VMEM((tm, tn), F32)] if nk > 1 else [],
        compiler_params=_cp(3), args=(a, b))
    if xch is None:
        return res[0]
    return res[0][0], res[1]


def _mm_mlp1(h2, w1, b1, tm=1024, tn=1024):
    M, K = h2.shape
    N = w1.shape[1]
    tm, tn = _fit(M, tm), _fit(N, tn)

    def body(a_ref, b_ref, bias_ref, a1_ref, act_ref):
        v = _dot(a_ref[...], b_ref[...]) + bias_ref[...]
        a1_ref[...] = v
        r = jnp.maximum(v, 0.0)
        act_ref[...] = (r * r).astype(BF16)

    out = pl.BlockSpec((tm, tn), lambda i, j: (i, j))
    return pl.pallas_call(
        body, name="mm_mlp1", grid=(M // tm, N // tn),
        in_specs=[pl.BlockSpec((tm, K), lambda i, j: (i, 0)), pl.BlockSpec((K, tn), lambda i, j: (0, j)),
                  pl.BlockSpec((1, tn), lambda i, j: (0, j))],
        out_specs=[out, out],
        out_shape=[jax.ShapeDtypeStruct((M, N), F32), jax.ShapeDtypeStruct((M, N), BF16)],
        compiler_params=_cp(2),
    )(h2, w1, b1)


def _mm_dact(dmlp, w2, a1, tm=1024, tn=1024):
    M, K = dmlp.shape
    N = w2.shape[0]
    tm, tn = _fit(M, tm), _fit(N, tn)

    def body(d_ref, w_ref, a1_ref, o_ref, acc_ref):
        i = pl.program_id(1)

        @pl.when(i == 0)
        def _():
            acc_ref[...] = jnp.zeros_like(acc_ref)

        da = _dot_nt(d_ref[...], w_ref[...]) * (2.0 * jnp.maximum(a1_ref[...], 0.0))
        o_ref[...] = da.astype(BF16)
        acc_ref[0:1, :] += jnp.sum(da, axis=0, keepdims=True)

    blk = pl.BlockSpec((tm, tn), lambda j, i: (i, j))
    return pl.pallas_call(
        body, name="mm_dact", grid=(N // tn, M // tm),
        in_specs=[pl.BlockSpec((tm, K), lambda j, i: (i, 0)), pl.BlockSpec((tn, K), lambda j, i: (j, 0)), blk],
        out_specs=[blk, pl.BlockSpec((8, tn), lambda j, i: (0, j))],
        out_shape=[jax.ShapeDtypeStruct((M, N), BF16), jax.ShapeDtypeStruct((8, N), F32)],
        compiler_params=_cp(2),
    )(dmlp, w2, a1)


class _Cfg:
    def __init__(self, Bn, T, Tc):
        assert T % Tc == 0 and Tc % CH == 0 and Tc % GRID_W == 0
        self.Bn, self.T, self.Tc = Bn, T, Tc
        self.TT = T + Tc
        self.TB = Tc
        self.nbt = self.TT // self.TB
        self.nbl = T // self.TB
        self.NT = Bn * self.TT
        self.N = Bn * T
        self.nct = Tc // CH
        self.nlt = T // CH
        self.nch = self.nct + self.nlt


def _ln_mod_fwd(cfg, ctx2, x2, shift_tab, scale_tab):
    TB, nbt = cfg.TB, cfg.nbt

    nbl = cfg.nbl

    def body(c_ref, x_ref, sh_ref, sc_ref, o_ref):
        j = pl.program_id(1)
        xhat, _ = _ln(jnp.where(j == 0, c_ref[...], x_ref[...]))
        o_ref[...] = (xhat * (1.0 + sc_ref[...]) + sh_ref[...]).astype(BF16)

    tab = pl.BlockSpec((None, 1, D), lambda b, j: (2 * b + jnp.minimum(j, 1), 0, 0))
    return pl.pallas_call(
        body, name="ln_mod_fwd", grid=(cfg.Bn, nbt),
        in_specs=[pl.BlockSpec((TB, D), lambda b, j: (b, 0)),
                  pl.BlockSpec((TB, D), lambda b, j: (b * nbl + jnp.maximum(j - 1, 0), 0)), tab, tab],
        out_specs=pl.BlockSpec((TB, D), lambda b, j: (b * nbt + j, 0)),
        out_shape=jax.ShapeDtypeStruct((cfg.NT, D), BF16),
        compiler_params=_cp(2),
    )(ctx2, x2, shift_tab, scale_tab)


GP = 8
NGB = NG // GP
HPB = GP * HPG


def _heads_to_front(x, d, gb, inverse=False):
    off = d * NH + gb * HPB
    return pltpu.roll(x, off if inverse else (DTW - off) % DTW, 1)


def _dt_fwd(cfg, dt_raw, dt_bias, a_log):
    def body(raw_ref, bias_ref, alog_ref, dt_ref, dtg_ref, cumg_ref, cumT_ref):
        dt = _softplus(raw_ref[...] + bias_ref[...])
        a = -jnp.exp(alog_ref[...])
        dta = dt * a
        col = lax.broadcasted_iota(jnp.int32, (CH, DTW), 1)
        cf = _dot_exact_l(_tri(CH, False), dta)
        cr = _dot_exact_l(_tri(CH, True), dta)
        cum = jnp.where(col < NH, cf, cr)
        dt_ref[...] = dt
        for d in range(2):
            for gb in range(NGB):
                dtg_ref[d, gb] = _heads_to_front(dt, d, gb)
                cg = _heads_to_front(cum, d, gb)
                cumg_ref[d, gb] = cg
                cumT_ref[d, gb] = cg.T

    blk = pl.BlockSpec((CH, DTW), lambda i: (i, 0))
    row = pl.BlockSpec((1, DTW), lambda i: (0, 0))
    gblk = pl.BlockSpec((2, NGB, CH, DTW), lambda i: (0, 0, i, 0))
    return pl.pallas_call(
        body, name="dt_fwd", grid=(cfg.NT // CH,),
        in_specs=[blk, row, row],
        out_specs=[blk, gblk, gblk, pl.BlockSpec((2, NGB, None, DTW, CH), lambda i: (0, 0, i, 0, 0))],
        out_shape=[jax.ShapeDtypeStruct((cfg.NT, DTW), F32),
                   jax.ShapeDtypeStruct((2, NGB, cfg.NT, DTW), F32),
                   jax.ShapeDtypeStruct((2, NGB, cfg.NT, DTW), F32),
                   jax.ShapeDtypeStruct((2, NGB, cfg.NT // CH, DTW, CH), F32)],
        compiler_params=_cp(1),
    )(dt_raw, dt_bias, a_log)


def _dt_bwd(cfg, dAs, dxxs, dt_raw, dt, dt_bias, a_log):
    def body(dAf_ref, dAr_ref, dxf_ref, dxr_ref, raw_ref, dt_ref, bias_ref, alog_ref, o_ref, acc_ref):
        i = pl.program_id(0)

        @pl.when(i == 0)
        def _():
            acc_ref[...] = jnp.zeros_like(acc_ref)

        a = -jnp.exp(alog_ref[...])
        col = lax.broadcasted_iota(jnp.int32, (CH, DTW), 1)
        dA_v = jnp.zeros((CH, DTW), F32)
        dxx_v = jnp.zeros((CH, DTW), F32)
        for d, (ra, rx) in enumerate(((dAf_ref, dxf_ref), (dAr_ref, dxr_ref))):
            for gb in range(NGB):
                dA_v = dA_v + _heads_to_front(ra[gb], d, gb, inverse=True)
                dxx_v = dxx_v + _heads_to_front(rx[gb], d, gb, inverse=True)
        ddta = jnp.where(col < NH, _dot_exact_l(_tri(CH, True), dA_v), _dot_exact_l(_tri(CH, False), dA_v))
        dtv = dt_ref[...]
        ddt = ddta * a + dxx_v
        draw = ddt * _sigmoid(raw_ref[...] + bias_ref[...])
        draw = jnp.where(col < 2 * NH, draw, 0.0)
        o_ref[...] = draw
        da = jnp.sum(ddta * dtv, axis=0, keepdims=True) * a
        da = jnp.where(col[:1] < 2 * NH, da, 0.0)
        acc_ref[0:1, :] += da
        acc_ref[1:2, :] += jnp.sum(draw, axis=0, keepdims=True)

    blk = pl.BlockSpec((CH, DTW), lambda i: (i, 0))
    row = pl.BlockSpec((1, DTW), lambda i: (0, 0))
    gblk = pl.BlockSpec((NGB, CH, DTW), lambda i: (0, i, 0))
    return pl.pallas_call(
        body, name="dt_bwd", grid=(cfg.NT // CH,),
        in_specs=[gblk, gblk, gblk, gblk, blk, blk, row, row],
        out_specs=[blk, pl.BlockSpec((8, DTW), lambda i: (0, 0))],
        out_shape=[jax.ShapeDtypeStruct((cfg.NT, DTW), F32), jax.ShapeDtypeStruct((8, DTW), F32)],
        compiler_params=_cp(1),
    )(dAs[0], dAs[1], dxxs[0], dxxs[1], dt_raw, dt, dt_bias, a_log)


_TAPS = (2, 1, 0, -1)


def _conv_fwd(cfg, proj, conv_w, conv_b):
    TB, nbt = cfg.TB, cfg.nbt
    CB = CONVW // 2
    SUB = 256
    n_act = DI + 2 * NG * NS

    def body(u_ref, w_ref, b_ref, o_ref, sg_ref):
        i = pl.program_id(0)
        j = pl.program_id(1)
        R = jnp.where(i % nbt == 0, cfg.Tc, GRID_W)
        t = lax.broadcasted_iota(jnp.int32, (TB, SUB), 0)
        pos = jnp.bitwise_and(t, R - 1)
        keep = {s: jnp.where(jnp.logical_and(pos - s >= 0, pos - s < R), 1.0, 0.0) for s in (2, 1, -1)}
        def sub_tile(q, act):
            sl = slice(q * SUB, (q + 1) * SUB)
            u = u_ref[:, sl]
            pre = b_ref[:, sl] + w_ref[2:3, sl] * u
            for k in (0, 1, 3):
                pre = pre + w_ref[k:k + 1, sl] * (pltpu.roll(u, _TAPS[k] % TB, 0) * keep[_TAPS[k]])
            if act:
                s = _sigmoid(pre)
                o_ref[:, sl] = pre * s
                sg_ref[:, sl] = s * (1.0 + pre * (1.0 - s))
            else:
                o_ref[:, sl] = pre

        for q in range(CB // SUB):
            if q * SUB >= n_act - CB:
                pl.when(j == 0)(functools.partial(sub_tile, q, True))
                pl.when(j == 1)(functools.partial(sub_tile, q, False))
            else:
                sub_tile(q, True)

    blk = pl.BlockSpec((TB, CB), lambda i, j: (i, j))
    return pl.pallas_call(
        body, name="conv_fwd", grid=(cfg.NT // TB, CONVW // CB),
        in_specs=[blk, pl.BlockSpec((4, CB), lambda i, j: (0, j)), pl.BlockSpec((1, CB), lambda i, j: (0, j))],
        out_specs=[blk, blk],
        out_shape=[jax.ShapeDtypeStruct((cfg.NT, CONVW), F32)] * 2,
        compiler_params=_cp(2),
    )(proj, conv_w, conv_b)


_ANY = pl.BlockSpec(memory_space=pl.ANY)


def _conv_bwd(cfg, name, dproj, proj, conv_w, sgrad, addends, col0, width, skip=None, xch=None):
    TB, nbt, nbl = cfg.TB, cfg.nbt, cfg.nbl
    CB = 1024
    SUB = 256
    c0 = col0 // CB
    addends = list(addends) + ([] if sgrad is None else [sgrad])
    n_add = len(addends)

    def body(*refs):
        u_ref, w_ref = refs[1:3]
        add_refs = refs[3:3 + n_add]
        rest = refs[3 + n_add:]
        if sgrad is not None:
            add_refs, sg_ref = add_refs[:-1], add_refs[-1]
        if skip is not None:
            dy_ref, dv_ref = rest[:2]
            rest = rest[2:]
        o_ref, acc_ref = rest
        i = pl.program_id(1)

        @pl.when(i == 0)
        def _():
            acc_ref[...] = jnp.zeros_like(acc_ref)

        isctx = (i % nbt) == 0
        R = jnp.where(isctx, cfg.Tc, GRID_W)
        t = lax.broadcasted_iota(jnp.int32, (TB, SUB), 0)
        pos = jnp.bitwise_and(t, R - 1)
        keep = {s: jnp.where(jnp.logical_and(pos - s >= 0, pos - s < R), 1.0, 0.0) for s in (2, 1, -1, -2)}

        def shifted(v, s):
            return v if s == 0 else pltpu.roll(v, s % TB, 0) * keep[s]

        for q in range(CB // SUB):
            sl = slice(q * SUB, (q + 1) * SUB)
            u = u_ref[:, sl]
            us = [shifted(u, _TAPS[k]) for k in range(4)]
            g = add_refs[0][:, sl]
            for r in add_refs[1:]:
                g = g + r[:, sl]
            if skip is not None:
                g = g + jnp.where(isctx, 0.0, dv_ref[:, sl] * dy_ref[:, sl])
            if sgrad is not None:
                g = g * sg_ref[:, sl]
            dp = jnp.zeros_like(g)
            for k in range(4):
                acc_ref[k:k + 1, sl] += jnp.sum(g * us[k], axis=0, keepdims=True)
                dp = dp + w_ref[k:k + 1, sl] * shifted(g, -_TAPS[k])
            acc_ref[4:5, sl] += jnp.sum(g, axis=0, keepdims=True)
            o_ref[:, sl] = dp.astype(BF16)

    blk = pl.BlockSpec((TB, CB), lambda j, i: (i, j))
    wide = pl.BlockSpec((TB, CB), lambda j, i: (i, c0 + j))
    in_specs = [_ANY, wide, pl.BlockSpec((4, CB), lambda j, i: (0, c0 + j))]
    in_specs += [blk] * (n_add if sgrad is None else n_add - 1) + ([] if sgrad is None else [wide])
    args = [dproj, proj, conv_w] + addends
    if skip is not None:
        def lat(j, i):
            b = i // nbt
            return (b * nbl + jnp.maximum(i % nbt - 1, 0), j)
        in_specs += [pl.BlockSpec((TB, CB), lat), pl.BlockSpec((1, CB), lambda j, i: (0, j))]
        args += list(skip)
    return _hosted_call(
        body, xch, name=name, grid=(width // CB, cfg.NT // TB),
        in_specs=in_specs,
        out_specs=[pl.BlockSpec((TB, CB), lambda j, i: (i, c0 + j)), pl.BlockSpec((8, CB), lambda j, i: (0, j))],
        out_shape=[jax.ShapeDtypeStruct((cfg.NT, PM), BF16), jax.ShapeDtypeStruct((8, width), F32)],
        scratch_shapes=[], compiler_params=_cp(2), args=args, aliases={0: 0})


def _chunk_of_step(cfg, rev):
    nct, nlt = cfg.nct, cfg.nlt
    if not rev:
        return lambda s: s
    return lambda s: jnp.where(s < nct, nct - 1 - s, 2 * nct + nlt - 1 - s)


def _expand4(v, band, base):
    out = v[:, base + 3:base + 4]
    for h in (2, 1, 0):
        out = jnp.where(band == h, v[:, base + h:base + h + 1], out)
    return out


def _ssd_step_tiles(dt_ref, cum_ref, cumT_ref, rev):
    cum_t = cum_ref[...]
    last = 0 if rev else CH - 1
    llast = cum_t[last:last + 1, :]
    return (dt_ref[...], cum_t, cumT_ref[...], llast, jnp.exp(llast), last)


def _ssd_common(gi, x_ref, b_ref, c_ref, tiles, rev, intra=True):
    dt_t, cum_t, cumT_t, llast, elast, last = tiles
    base = gi * HPG
    xh = x_ref[:, gi * HPG * HD:(gi + 1) * HPG * HD]
    Bm = b_ref[:, gi * NS:(gi + 1) * NS].astype(BF16)
    band = lax.broadcasted_iota(jnp.int32, (CH, HPG * HD), 1) // HD
    cbs = [jnp.broadcast_to(cum_t[:, base + h:base + h + 1], (CH, CH)) for h in range(HPG)]
    Cm = G = decs = None
    if intra:
        Cm = c_ref[:, gi * NS:(gi + 1) * NS].astype(BF16)
        G = _dot_nt(Cm, Bm)
        ii = lax.broadcasted_iota(jnp.int32, (CH, CH), 0)
        jj = lax.broadcasted_iota(jnp.int32, (CH, CH), 1)
        mask = (jj >= ii) if rev else (jj <= ii)
        decs = [jnp.exp(jnp.where(mask, cbs[h] - cumT_t[base + h:base + h + 1, :], -1e30)) for h in range(HPG)]
    cum_exp = jnp.concatenate([cbs[3], cbs[3]], axis=1)
    ll_exp = llast[:, base + 3:base + 4]
    for h in (2, 1, 0):
        cum_exp = jnp.where(band == h, jnp.concatenate([cbs[h], cbs[h]], axis=1), cum_exp)
        ll_exp = jnp.where(band[:1] == h, llast[:, base + h:base + h + 1], ll_exp)
    ecum = jnp.exp(cum_exp) if intra else None
    e_exp = jnp.exp(ll_exp - cum_exp)
    dt_exp = _expand4(dt_t, band, base)
    X = xh * dt_exp
    rb = lax.broadcasted_iota(jnp.int32, (HPG * HD, NS), 0) // HD
    dec_rows = elast[:, base + 3:base + 4]
    for h in (2, 1, 0):
        dec_rows = jnp.where(rb == h, elast[:, base + h:base + h + 1], dec_rows)
    return xh, Bm, Cm, band, e_exp, ecum, dt_exp, X, G, decs, elast, dec_rows, last


def _ssd_specs(cfg, rev):
    nch = cfg.nch
    cmap = _chunk_of_step(cfg, rev)
    d = 1 if rev else 0

    def make(stepmap):
        def row(b, g, sp):
            return b * nch + cmap(stepmap(sp))
        bo, co = DI // (GP * NS), (DI + NG * NS) // (GP * NS)
        return [
            pl.BlockSpec((CH, GP * HPG * HD), lambda b, g, sp: (row(b, g, sp), g)),
            pl.BlockSpec((CH, GP * NS), lambda b, g, sp: (row(b, g, sp), bo + g)),
            pl.BlockSpec((CH, GP * NS), lambda b, g, sp: (row(b, g, sp), co + g)),
            pl.BlockSpec((None, None, CH, DTW), lambda b, g, sp: (d, g, row(b, g, sp), 0)),
            pl.BlockSpec((None, None, CH, DTW), lambda b, g, sp: (d, g, row(b, g, sp), 0)),
            pl.BlockSpec((None, None, None, DTW, CH), lambda b, g, sp: (d, g, row(b, g, sp), 0, 0)),
        ], row
    return make


def _ssd_fwd(cfg, act, dtg, cumg, cumTg, rev, y_other=None, dvec=None):
    nch = cfg.nch
    in_specs, row = _ssd_specs(cfg, rev)(lambda sp: sp)
    total = y_other is not None

    def body(*refs):
        x_ref, b_ref, c_ref, dt_ref, cum_ref, cumT_ref = refs[:6]
        if total:
            yo_ref, dv_ref = refs[6:8]
        y_ref, hs_ref, h_scr = refs[-3:]
        s = pl.program_id(2)

        @pl.when(s == 0)
        def _():
            h_scr[...] = jnp.zeros_like(h_scr)

        def step(intra):
            tiles = _ssd_step_tiles(dt_ref, cum_ref, cumT_ref, rev)
            for gi in range(GP):
                xh, Bm, Cm, band, e_exp, ecum, dt_exp, X, G, decs, elast, dec_rows, last = _ssd_common(
                    gi, x_ref, b_ref, c_ref, tiles, rev, intra)
                H = h_scr[gi]
                if intra:
                    Mcat = jnp.concatenate([(G * decs[h]).astype(BF16) for h in range(HPG)], axis=1)
                    Xbd = jnp.concatenate([jnp.where(band == h, X, 0.0).astype(BF16) for h in range(HPG)], axis=0)
                    xsl = slice(gi * HPG * HD, (gi + 1) * HPG * HD)
                    Y = ecum * _dot_nt(Cm, H.astype(BF16)) + _dot(Mcat, Xbd)
                    if total:
                        Y = Y + yo_ref[:, xsl] + dv_ref[:, xsl] * xh
                    y_ref[:, xsl] = Y
                hs_ref[gi] = H
                S = _dot_tn((e_exp * X).astype(BF16), Bm)
                h_scr[gi] = dec_rows * H + S

        isctx = cmap(s) < cfg.nct

        @pl.when(isctx)
        def _():
            step(False)

        @pl.when(jnp.logical_not(isctx))
        def _():
            step(True)

    cmap = _chunk_of_step(cfg, rev)
    yblk = pl.BlockSpec((CH, GP * HPG * HD), lambda b, g, s: (row(b, g, s), g))
    args = [act, act, act, dtg, cumg, cumTg]
    if total:
        in_specs = in_specs + [yblk, pl.BlockSpec((1, GP * HPG * HD), lambda b, g, s: (0, g))]
        args += [y_other, dvec]
    return pl.pallas_call(
        body, name="ssd_fwd_rev" if rev else "ssd_fwd", grid=(cfg.Bn, NG // GP, nch),
        in_specs=in_specs,
        out_specs=[yblk, pl.BlockSpec((None, GP, None, HPG * HD, NS), lambda b, g, s: (b, g, s, 0, 0))],
        out_shape=[jax.ShapeDtypeStruct((cfg.NT, DI), F32),
                   jax.ShapeDtypeStruct((cfg.Bn, NG, nch, HPG * HD, NS), F32)],
        scratch_shapes=[pltpu.VMEM((GP, HPG * HD, NS), F32)],
        compiler_params=_cp(3),
    )(*args)


def _ssd_bwd(cfg, act, dtg, cumg, cumTg, hs, dy, rev, xch=None):
    nch, nct, nlt = cfg.nch, cfg.nct, cfg.nlt
    cmap = _chunk_of_step(cfg, rev)
    in_specs, row = _ssd_specs(cfg, rev)(lambda sp: nch - 1 - sp)

    def lat_row(b, g, sp):
        c = cmap(nch - 1 - sp)
        return b * nlt + jnp.maximum(c - nct, 0)

    def body(x_ref, b_ref, c_ref, dt_ref, cum_ref, cumT_ref, dy_ref, hs_ref,
             dxh_ref, dB_ref, dC_ref, dA_ref, dxx_ref, dh_scr):
        sp = pl.program_id(2)

        @pl.when(sp == 0)
        def _():
            dh_scr[...] = jnp.zeros_like(dh_scr)

        def step(intra):
            tiles = _ssd_step_tiles(dt_ref, cum_ref, cumT_ref, rev)
            dA_t = jnp.zeros((CH, DTW), F32)
            dAT_t = jnp.zeros((DTW, CH), F32)
            dxx_t = jnp.zeros((CH, DTW), F32)
            for gi in range(GP):
                dA_g, dAT_g, dxx_g = group_bwd(gi, intra, tiles, x_ref, b_ref, c_ref, dy_ref, hs_ref,
                                               dxh_ref, dB_ref, dC_ref, dh_scr)
                dA_t = dA_t + dA_g
                dxx_t = dxx_t + dxx_g
                if intra:
                    dAT_t = dAT_t + dAT_g
            dA_ref[...] = dA_t - dAT_t.T if intra else dA_t
            dxx_ref[...] = dxx_t

        isctx = cmap(nch - 1 - sp) < nct

        @pl.when(isctx)
        def _():
            step(False)

        @pl.when(jnp.logical_not(isctx))
        def _():
            step(True)

    def group_bwd(gi, intra, tiles, x_ref, b_ref, c_ref, dy_ref, hs_ref, dxh_ref, dB_ref, dC_ref, dh_scr):
        xsl = slice(gi * HPG * HD, (gi + 1) * HPG * HD)
        nsl = slice(gi * NS, (gi + 1) * NS)
        base = gi * HPG
        xh, Bm, Cm, band, e_exp, ecum, dt_exp, X, G, decs, elast, dec_rows, last = _ssd_common(
            gi, x_ref, b_ref, c_ref, tiles, rev, intra)
        H = hs_ref[gi]
        dHn = dh_scr[gi]
        dHnb = dHn.astype(BF16)
        BdH = _dot_nt(Bm, dHnb)
        dX = e_exp * BdH
        eX = e_exp * X
        lanei = lax.broadcasted_iota(jnp.int32, (CH, DTW), 1)
        dA = jnp.zeros((CH, DTW), F32)
        dAT = None
        pb = lax.broadcasted_iota(jnp.int32, (HPG * HD, NS), 0) // HD
        pl_ = lax.broadcasted_iota(jnp.int32, (HPG * HD, NS), 1)
        E = jnp.where(pb + base == pl_, 1.0, 0.0).astype(BF16)
        if intra:
            dY = dy_ref[:, xsl]
            Hb = H.astype(BF16)
            dYs = ecum * dY
            dYsb = dYs.astype(BF16)
            Ys = ecum * _dot_nt(Cm, Hb)
            dG = jnp.zeros((CH, CH), F32)
            subi = lax.broadcasted_iota(jnp.int32, (DTW, CH), 0)
            dAT = jnp.zeros((DTW, CH), F32)
            Xbd = jnp.concatenate([jnp.where(band == h, X, 0.0).astype(BF16) for h in range(HPG)], axis=0)
            dYbd = jnp.concatenate([jnp.where(band == h, dY, 0.0).astype(BF16) for h in range(HPG)], axis=0)
            dMcat = _dot_nt(dY.astype(BF16), Xbd)
            Ms = []
            for h in range(HPG):
                M = G * decs[h]
                dM = dMcat[:, h * CH:(h + 1) * CH]
                W = dM * M
                dG = dG + dM * decs[h]
                Ms.append(M.astype(BF16))
                dA = dA + jnp.where(lanei == base + h, jnp.sum(W, axis=1, keepdims=True), 0.0)
                dAT = dAT + jnp.where(subi == base + h, jnp.sum(W, axis=0, keepdims=True), 0.0)
            dX = dX + _dot_tn(jnp.concatenate(Ms, axis=0), dYbd)
            dGb = dG.astype(BF16)
            dC_ref[:, nsl] = _dot(dGb, Bm) + _dot(dYsb, Hb)
            dB_ref[:, nsl] = _dot_tn(dGb, Cm) + _dot(eX.astype(BF16), dHnb)
            dh_scr[gi] = dec_rows * dHn + _dot_tn(dYsb, Cm)
            dA = dA + _dot_hilo_r(dY * Ys, E)
        else:
            dC_ref[:, nsl] = jnp.zeros((CH, NS), F32)
            dB_ref[:, nsl] = _dot(eX.astype(BF16), dHnb)
            dh_scr[gi] = dec_rows * dHn
        q = _dot_hilo_r(eX * BdH, E)
        r = jnp.sum(dHn * H, axis=1, keepdims=True)
        lane1 = lax.broadcasted_iota(jnp.int32, (1, DTW), 1)
        hdot = jnp.zeros((1, DTW), F32)
        for h in range(HPG):
            hv = jnp.sum(r[h * HD:(h + 1) * HD, :], axis=0, keepdims=True)
            hdot = hdot + jnp.where(lane1 == base + h, hv, 0.0)
        dllast = jnp.sum(q, axis=0, keepdims=True) + elast * hdot
        rowi = lax.broadcasted_iota(jnp.int32, (CH, DTW), 0)
        dxh_ref[:, xsl] = dX * dt_exp
        return dA - q + jnp.where(rowi == last, dllast, 0.0), dAT, _dot_hilo_r(dX * xh, E)

    small = pl.BlockSpec((None, CH, DTW), lambda b, g, sp: (g, row(b, g, sp), 0))
    return _hosted_call(
        body, xch, name="ssd_bwd_rev" if rev else "ssd_bwd", grid=(cfg.Bn, NG // GP, nch),
        in_specs=in_specs + [
            pl.BlockSpec((CH, GP * HPG * HD), lambda b, g, sp: (lat_row(b, g, sp), g)),
            pl.BlockSpec((None, GP, None, HPG * HD, NS), lambda b, g, sp: (b, g, nch - 1 - sp, 0, 0))],
        out_specs=[pl.BlockSpec((CH, GP * HPG * HD), lambda b, g, sp: (row(b, g, sp), g)),
                   pl.BlockSpec((CH, GP * NS), lambda b, g, sp: (row(b, g, sp), g)),
                   pl.BlockSpec((CH, GP * NS), lambda b, g, sp: (row(b, g, sp), g)),
                   small, small],
        out_shape=[jax.ShapeDtypeStruct((cfg.NT, DI), F32),
                   jax.ShapeDtypeStruct((cfg.NT, NG * NS), F32),
                   jax.ShapeDtypeStruct((cfg.NT, NG * NS), F32),
                   jax.ShapeDtypeStruct((NGB, cfg.NT, DTW), F32),
                   jax.ShapeDtypeStruct((NGB, cfg.NT, DTW), F32)],
        scratch_shapes=[pltpu.VMEM((GP, HPG * HD, NS), F32)],
        compiler_params=_cp(3), args=(act, act, act, dtg, cumg, cumTg, dy, hs))


def _shift_rows(v, s, fill, toward_later, rowi):
    n = v.shape[0]
    if toward_later:
        return jnp.where(rowi >= s, pltpu.roll(v, s, 0), fill)
    return jnp.where(rowi < n - s, pltpu.roll(v, n - s, 0), fill)


def _chunk_scan(a, b, carry, later):
    nt = a.shape[0] // 8
    rowi = lax.broadcasted_iota(jnp.int32, (8, a.shape[1]), 0)
    outs = [None] * nt
    for r in (range(nt) if later else range(nt - 1, -1, -1)):
        av = a[r * 8:(r + 1) * 8]
        bv = b[r * 8:(r + 1) * 8]
        for sh in (1, 2, 4):
            a_p = _shift_rows(av, sh, 1.0, later, rowi)
            b_p = _shift_rows(bv, sh, 0.0, later, rowi)
            bv = av * b_p + bv
            av = av * a_p
        h = bv + av * carry
        outs[r] = h
        carry = h[7:8] if later else h[0:1]
    return jnp.concatenate(outs, axis=0), carry


def _lru_gates(u, wa_ref, wi_ref, ba_ref, bi_ref, lam_ref):
    rs, is_ = [], []
    for k in range(LB):
        uk = u[:, k * LBW:(k + 1) * LBW].astype(BF16)
        rs.append(_dot(uk, wa_ref[k].astype(BF16)))
        is_.append(_dot(uk, wi_ref[k].astype(BF16)))
    r = _sigmoid(jnp.concatenate(rs, axis=1) + ba_ref[...])
    ig = _sigmoid(jnp.concatenate(is_, axis=1) + bi_ref[...])
    sp = _softplus(-lam_ref[...])
    la = -LRU_C * r * sp
    a = jnp.exp(la)
    g = jnp.sqrt((1.0 + a * a) * jnp.tanh(-la))
    return r, ig, sp, la, a, g


def _lru_w_specs(d):
    return [pl.BlockSpec((None, LB, LBW, LBW), lambda b, s: (d, 0, 0, 0)),
            pl.BlockSpec((None, LB, LBW, LBW), lambda b, s: (d, 0, 0, 0)),
            pl.BlockSpec((None, 1, LW), lambda b, s: (d, 0, 0)),
            pl.BlockSpec((None, 1, LW), lambda b, s: (d, 0, 0)),
            pl.BlockSpec((None, 1, LW), lambda b, s: (d, 0, 0))]


def _lru_block_of_step(cfg, rev):
    nbl = cfg.nbl
    if not rev:
        return lambda s: s
    return lambda s: jnp.where(s < 1, 0, 1 + nbl - s)


def _lru_fwd(cfg, act, wa, wi, ba, bi, lam, rev):
    nch, CH = cfg.nbt, cfg.TB
    cmap = _lru_block_of_step(cfg, rev)
    d = 1 if rev else 0
    ucol = (DI + 2 * NG * NS) // LW

    def body(u_ref, wa_ref, wi_ref, ba_ref, bi_ref, lam_ref, h_ref, c_scr):
        s = pl.program_id(1)

        @pl.when(s == 0)
        def _():
            c_scr[...] = jnp.zeros_like(c_scr)

        u = u_ref[...]
        r, ig, sp, la, a, g = _lru_gates(u, wa_ref, wi_ref, ba_ref, bi_ref, lam_ref)
        h, carry = _chunk_scan(a, g * ig * u, c_scr[0:1, :], not rev)
        h_ref[...] = h
        c_scr[0:1, :] = carry

    return pl.pallas_call(
        body, name="lru_fwd_rev" if rev else "lru_fwd", grid=(cfg.Bn, nch),
        in_specs=[pl.BlockSpec((CH, LW), lambda b, s: (b * nch + cmap(s), ucol))] + _lru_w_specs(d),
        out_specs=pl.BlockSpec((CH, LW), lambda b, s: (b * nch + cmap(s), 0)),
        out_shape=jax.ShapeDtypeStruct((cfg.NT, LW), F32),
        scratch_shapes=[pltpu.VMEM((8, LW), F32)],
        compiler_params=_cp(2),
    )(act, wa, wi, ba, bi, lam)


def _lru_bwd(cfg, act, wa, wi, ba, bi, lam, hd, dyl, rev):
    nch, nct, nlt, CH = cfg.nbt, 1, cfg.nbl, cfg.TB
    cmap = _lru_block_of_step(cfg, rev)
    d = 1 if rev else 0
    ucol = (DI + 2 * NG * NS) // LW

    def srow(b, sp):
        return b * nch + cmap(nch - 1 - sp)

    def prev_rows(b, sp):
        s = nch - 1 - sp
        cp = cmap(jnp.maximum(s - 1, 0))
        base = (b * nch + cp) * (CH // 8)
        return base + (0 if rev else CH // 8 - 1)

    def lat_row(b, sp):
        c = cmap(nch - 1 - sp)
        return b * nlt + jnp.maximum(c - nct, 0)

    def body(u_ref, wa_ref, wi_ref, ba_ref, bi_ref, lam_ref, h_ref, hp_ref, dy_ref,
             du_ref, dwa_ref, dwi_ref, vec_ref, c_scr):
        b = pl.program_id(0)
        sp_id = pl.program_id(1)
        s = nch - 1 - sp_id

        @pl.when(sp_id == 0)
        def _():
            c_scr[...] = jnp.zeros_like(c_scr)

        @pl.when(jnp.logical_and(b == 0, sp_id == 0))
        def _():
            dwa_ref[...] = jnp.zeros_like(dwa_ref)
            dwi_ref[...] = jnp.zeros_like(dwi_ref)
            vec_ref[...] = jnp.zeros_like(vec_ref)

        c = cmap(s)
        u = u_ref[...]
        r, ig, spl, la, a, g = _lru_gates(u, wa_ref, wi_ref, ba_ref, bi_ref, lam_ref)
        dh = jnp.where(c < nct, 0.0, dy_ref[...])
        rowi = lax.broadcasted_iota(jnp.int32, (CH, LW), 0)
        lamv, _ = _chunk_scan(_shift_rows(a, 1, 1.0, rev, rowi), dh, c_scr[0:1, :], rev)
        first = CH - 1 if rev else 0
        c_scr[0:1, :] = (a * lamv)[first:first + 1, :]
        hprow = hp_ref[...][(0 if rev else 7):(1 if rev else 8), :]
        hprow = jnp.where(s > 0, hprow, 0.0)
        h_prev = _shift_rows(h_ref[...], 1, hprow, not rev, rowi)
        da = lamv * h_prev
        db = lamv
        iu = ig * u
        dla = da * a - db * iu * (a * a) / g
        dr = dla * (-LRU_C * spl)
        di = db * g * u
        du = db * g * ig
        drp = dr * r * (1.0 - r)
        dip = di * ig * (1.0 - ig)
        dus = []
        for k in range(LB):
            sl = slice(k * LBW, (k + 1) * LBW)
            drk = drp[:, sl].astype(BF16)
            dik = dip[:, sl].astype(BF16)
            uk = u[:, sl].astype(BF16)
            dus.append(_dot_nt(drk, wa_ref[k].astype(BF16)) + _dot_nt(dik, wi_ref[k].astype(BF16)))
            dwa_ref[k] += _dot_tn(uk, drk)
            dwi_ref[k] += _dot_tn(uk, dik)
        du_ref[...] = du + jnp.concatenate(dus, axis=1)
        vec_ref[0:1, :] += jnp.sum(drp, axis=0, keepdims=True)
        vec_ref[1:2, :] += jnp.sum(dip, axis=0, keepdims=True)
        dsp = jnp.sum(dla * (-LRU_C * r), axis=0, keepdims=True)
        vec_ref[2:3, :] += dsp * (-_sigmoid(-lam_ref[...]))

    return pl.pallas_call(
        body, name="lru_bwd_rev" if rev else "lru_bwd", grid=(cfg.Bn, nch),
        in_specs=[pl.BlockSpec((CH, LW), lambda b, sp: (srow(b, sp), ucol))] + _lru_w_specs(d) + [
            pl.BlockSpec((CH, LW), lambda b, sp: (srow(b, sp), 0)),
            pl.BlockSpec((8, LW), lambda b, sp: (prev_rows(b, sp), 0)),
            pl.BlockSpec((CH, LW), lambda b, sp: (lat_row(b, sp), 0))],
        out_specs=[pl.BlockSpec((CH, LW), lambda b, sp: (srow(b, sp), 0)),
                   pl.BlockSpec((LB, LBW, LBW), lambda b, sp: (0, 0, 0)),
                   pl.BlockSpec((LB, LBW, LBW), lambda b, sp: (0, 0, 0)),
                   pl.BlockSpec((8, LW), lambda b, sp: (0, 0))],
        out_shape=[jax.ShapeDtypeStruct((cfg.NT, LW), F32),
                   jax.ShapeDtypeStruct((LB, LBW, LBW), F32),
                   jax.ShapeDtypeStruct((LB, LBW, LBW), F32),
                   jax.ShapeDtypeStruct((8, LW), F32)],
        scratch_shapes=[pltpu.VMEM((8, LW), F32)],
        compiler_params=_cp(2),
    )(act, wa, wi, ba, bi, lam, hd, hd, dyl)


HB = 1024


def _post_ssd_fwd(cfg, y, proj, norm_w):
    TB, nbt, nbl = cfg.TB, cfg.nbt, cfg.nbl
    zc = CONVW // HB

    def body(y_ref, z_ref, w_ref, o_ref):
        u = y_ref[...] * _silu(z_ref[...])
        for gi in range(HB // (DI // NG)):
            sl = slice(gi * 256, (gi + 1) * 256)
            ug = u[:, sl]
            rs = lax.rsqrt(jnp.mean(ug * ug, axis=1, keepdims=True) + RMS_EPS)
            o_ref[:, sl] = (ug * rs * w_ref[:, sl]).astype(BF16)

    def st(b, j, cb):
        return (b * nbt + 1 + j, cb)
    return pl.pallas_call(
        body, name="post_ssd_fwd", grid=(cfg.Bn, nbl, DI // HB),
        in_specs=[pl.BlockSpec((TB, HB), st),
                  pl.BlockSpec((TB, HB), lambda b, j, cb: (b * nbt + 1 + j, zc + cb)),
                  pl.BlockSpec((1, HB), lambda b, j, cb: (0, cb))],
        out_specs=pl.BlockSpec((TB, HB), lambda b, j, cb: (b * nbl + j, cb)),
        out_shape=jax.ShapeDtypeStruct((cfg.N, DI), BF16),
        compiler_params=_cp(3),
    )(y, proj, norm_w)


def _post_ssd_bwd(cfg, dproj, dn, y, act, proj, norm_w):
    TB, nbt, nbl = cfg.TB, cfg.nbt, cfg.nbl
    zc = CONVW // HB

    def body(_, dn_ref, y_ref, xh_ref, z_ref, w_ref, dy_ref, dz_ref, acc_ref):
        b = pl.program_id(1)
        j = pl.program_id(2)

        @pl.when(jnp.logical_and(b == 0, j == 0))
        def _():
            acc_ref[...] = jnp.zeros_like(acc_ref)

        @pl.when(j == 0)
        def _():
            dz_ref[...] = jnp.zeros_like(dz_ref)

        @pl.when(j > 0)
        def _():
            latent(dn_ref, y_ref, xh_ref, z_ref, w_ref, dy_ref, dz_ref, acc_ref)

    def latent(dn_ref, y_ref, xh_ref, z_ref, w_ref, dy_ref, dz_ref, acc_ref):
        xh = xh_ref[...]
        z = z_ref[...]
        y = y_ref[...]
        sz = _silu(z)
        u = y * sz
        dout = dn_ref[...]
        for gi in range(HB // (DI // NG)):
            sl = slice(gi * 256, (gi + 1) * 256)
            ug0 = u[:, sl]
            rs = lax.rsqrt(jnp.mean(ug0 * ug0, axis=1, keepdims=True) + RMS_EPS)
            ug = ug0 * rs
            do = dout[:, sl]
            acc_ref[0:1, sl] += jnp.sum(do * ug, axis=0, keepdims=True)
            dug = do * w_ref[:, sl]
            du = rs * (dug - ug * jnp.mean(dug * ug, axis=1, keepdims=True))
            dy = du * sz[:, sl]
            dy_ref[:, sl] = dy
            dz_ref[:, sl] = (du * y[:, sl] * _silu_grad(z[:, sl])).astype(BF16)
            acc_ref[1:2, sl] += jnp.sum(dy * xh[:, sl], axis=0, keepdims=True)

    def st(cb, b, j):
        return (b * nbt + j, cb)

    def la(cb, b, j):
        return (b * nbl + jnp.maximum(j - 1, 0), cb)
    return pl.pallas_call(
        body, name="post_ssd_bwd", grid=(DI // HB, cfg.Bn, nbt),
        in_specs=[_ANY, pl.BlockSpec((TB, HB), la), pl.BlockSpec((TB, HB), st), pl.BlockSpec((TB, HB), st),
                  pl.BlockSpec((TB, HB), lambda cb, b, j: (b * nbt + j, zc + cb)),
                  pl.BlockSpec((1, HB), lambda cb, b, j: (0, cb))],
        out_specs=[pl.BlockSpec((TB, HB), la),
                   pl.BlockSpec((TB, HB), lambda cb, b, j: (b * nbt + j, zc + cb)),
                   pl.BlockSpec((8, HB), lambda cb, b, j: (0, cb))],
        out_shape=[jax.ShapeDtypeStruct((cfg.N, DI), F32), jax.ShapeDtypeStruct((cfg.NT, PM), BF16),
                   jax.ShapeDtypeStruct((8, DI), F32)],
        input_output_aliases={0: 1},
        compiler_params=_cp(3),
    )(dproj, dn, y, act, proj, norm_w)


def _post_lru_fwd(cfg, hf, hb, proj):
    TB, nbt, nbl = cfg.TB, cfg.nbt, cfg.nbl
    gc = (CONVW + DI) // HB

    def body(hf_ref, hb_ref, g_ref, o_ref):
        o_ref[...] = ((hf_ref[...] + hb_ref[...]) * _gelu(g_ref[...])).astype(BF16)

    st = pl.BlockSpec((TB, HB), lambda b, j: (b * nbt + 1 + j, 0))
    return pl.pallas_call(
        body, name="post_lru_fwd", grid=(cfg.Bn, nbl),
        in_specs=[st, st, pl.BlockSpec((TB, HB), lambda b, j: (b * nbt + 1 + j, gc))],
        out_specs=pl.BlockSpec((TB, HB), lambda b, j: (b * nbl + j, 0)),
        out_shape=jax.ShapeDtypeStruct((cfg.N, LW), BF16),
        compiler_params=_cp(2),
    )(hf, hb, proj)


def _post_lru_bwd(cfg, dproj, dv, hf, hb, proj):
    TB, nbt, nbl = cfg.TB, cfg.nbt, cfg.nbl
    gc = (CONVW + DI) // HB

    def body(_, dv_ref, hf_ref, hb_ref, g_ref, dy_ref, dg_ref):
        j = pl.program_id(1)

        @pl.when(j == 0)
        def _():
            dg_ref[...] = jnp.zeros_like(dg_ref)

        @pl.when(j > 0)
        def _():
            gt = g_ref[...]
            dvv = dv_ref[...]
            dy_ref[...] = dvv * _gelu(gt)
            dg_ref[...] = (dvv * (hf_ref[...] + hb_ref[...]) * _gelu_grad(gt)).astype(BF16)

    st = pl.BlockSpec((TB, HB), lambda b, j: (b * nbt + j, 0))
    la = pl.BlockSpec((TB, HB), lambda b, j: (b * nbl + jnp.maximum(j - 1, 0), 0))
    gcol = pl.BlockSpec((TB, HB), lambda b, j: (b * nbt + j, gc))
    return pl.pallas_call(
        body, name="post_lru_bwd", grid=(cfg.Bn, nbt),
        in_specs=[_ANY, la, st, st, gcol],
        out_specs=[la, gcol],
        out_shape=[jax.ShapeDtypeStruct((cfg.N, LW), F32), jax.ShapeDtypeStruct((cfg.NT, PM), BF16)],
        input_output_aliases={0: 1},
        compiler_params=_cp(2),
    )(dproj, dv, hf, hb, proj)


def _merge_fwd(cfg, proj, b_gate, br_ssd, br_lru):
    TB, nbt, nbl = cfg.TB, cfg.nbt, cfg.nbl
    mc = (CONVW + DI + LW) // HB

    def body(ms_ref, ml_ref, bg_ref, bs_ref, bl_ref, o_ref):
        gs = _sigmoid(ms_ref[...] + bg_ref[:, :D])
        gl = _sigmoid(ml_ref[...] + bg_ref[:, D:])
        o_ref[...] = (gs * bs_ref[...] + gl * bl_ref[...]).astype(BF16)

    la = pl.BlockSpec((TB, D), lambda b, j: (b * nbl + j, 0))
    return pl.pallas_call(
        body, name="merge_fwd", grid=(cfg.Bn, nbl),
        in_specs=[pl.BlockSpec((TB, HB), lambda b, j: (b * nbt + 1 + j, mc)),
                  pl.BlockSpec((TB, HB), lambda b, j: (b * nbt + 1 + j, mc + 1)),
                  pl.BlockSpec((1, 2 * D), lambda b, j: (0, 0)), la, la],
        out_specs=la,
        out_shape=jax.ShapeDtypeStruct((cfg.N, D), BF16),
        compiler_params=_cp(2),
    )(proj, proj, b_gate, br_ssd, br_lru)


def _merge_bwd(cfg, dmix, proj, b_gate, br_ssd, br_lru):
    TB, nbt, nbl = cfg.TB, cfg.nbt, cfg.nbl
    mc = (CONVW + DI + LW) // HB

    def body(dm_ref, ms_ref, ml_ref, bg_ref, bs_ref, bl_ref, ds_ref, dl_ref, dmg_ref, acc_ref):
        b = pl.program_id(0)
        j = pl.program_id(1)

        @pl.when(jnp.logical_and(b == 0, j == 0))
        def _():
            acc_ref[...] = jnp.zeros_like(acc_ref)

        @pl.when(j == 0)
        def _():
            dmg_ref[...] = jnp.zeros_like(dmg_ref)

        @pl.when(j > 0)
        def _():
            latent(dm_ref, ms_ref, ml_ref, bg_ref, bs_ref, bl_ref, ds_ref, dl_ref, dmg_ref, acc_ref)

    def latent(dm_ref, ms_ref, ml_ref, bg_ref, bs_ref, bl_ref, ds_ref, dl_ref, dmg_ref, acc_ref):
        dm = dm_ref[...]
        gs = _sigmoid(ms_ref[...] + bg_ref[:, :D])
        gl = _sigmoid(ml_ref[...] + bg_ref[:, D:])
        ds_ref[...] = (dm * gs).astype(BF16)
        dl_ref[...] = (dm * gl).astype(BF16)
        dps = dm * bs_ref[...] * gs * (1.0 - gs)
        dpl = dm * bl_ref[...] * gl * (1.0 - gl)
        dmg_ref[:, :D] = dps.astype(BF16)
        dmg_ref[:, D:] = dpl.astype(BF16)
        acc_ref[0:1, :D] += jnp.sum(dps, axis=0, keepdims=True)
        acc_ref[0:1, D:] += jnp.sum(dpl, axis=0, keepdims=True)

    la = pl.BlockSpec((TB, D), lambda b, j: (b * nbl + jnp.maximum(j - 1, 0), 0))
    return pl.pallas_call(
        body, name="merge_bwd", grid=(cfg.Bn, nbt),
        in_specs=[la, pl.BlockSpec((TB, HB), lambda b, j: (b * nbt + j, mc)),
                  pl.BlockSpec((TB, HB), lambda b, j: (b * nbt + j, mc + 1)),
                  pl.BlockSpec((1, 2 * D), lambda b, j: (0, 0)), la, la],
        out_specs=[la, la, pl.BlockSpec((TB, 2 * D), lambda b, j: (b * nbt + j, mc // 2)),
                   pl.BlockSpec((8, 2 * D), lambda b, j: (0, 0))],
        out_shape=[jax.ShapeDtypeStruct((cfg.N, D), BF16), jax.ShapeDtypeStruct((cfg.N, D), BF16),
                   jax.ShapeDtypeStruct((cfg.NT, PM), BF16), jax.ShapeDtypeStruct((8, 2 * D), F32)],
        compiler_params=_cp(2),
    )(dmix, proj, proj, b_gate, br_ssd, br_lru)


def _resid1_fwd(cfg, x2, x_mix, gate1, shift2, scale2, ln1_g, ln1_b):
    TB, nbt, nbl = cfg.TB, cfg.nbt, cfg.nbl

    def body(x_ref, xm_ref, g1_ref, sh_ref, sc_ref, lg_ref, lb_ref, x1_ref, h2_ref):
        r1 = ALPHA * x_ref[...] + g1_ref[...] * xm_ref[...]
        xh, _ = _ln(r1)
        x1 = xh * lg_ref[...] + lb_ref[...]
        x1_ref[...] = x1
        xh2, _ = _ln(x1)
        h2_ref[...] = (xh2 * (1.0 + sc_ref[...]) + sh_ref[...]).astype(BF16)

    la = pl.BlockSpec((TB, D), lambda b, j: (b * nbl + j, 0))
    ex = pl.BlockSpec((None, 1, D), lambda b, j: (b, 0, 0))
    vec = pl.BlockSpec((1, D), lambda b, j: (0, 0))
    return pl.pallas_call(
        body, name="resid1_fwd", grid=(cfg.Bn, nbl),
        in_specs=[la, la, ex, ex, ex, vec, vec],
        out_specs=[la, la],
        out_shape=[jax.ShapeDtypeStruct((cfg.N, D), F32), jax.ShapeDtypeStruct((cfg.N, D), BF16)],
        compiler_params=_cp(2),
    )(x2, x_mix, gate1, shift2, scale2, ln1_g, ln1_b)


def _resid1_bwd(cfg, dh2, dx1p, x1, x2, x_mix, gate1, scale2, ln1_g):
    TB, nbt, nbl = cfg.TB, cfg.nbt, cfg.nbl

    def body(dh2_ref, dx1p_ref, x1_ref, x_ref, xm_ref, g1_ref, sc_ref, lg_ref,
             dxm_ref, dxp_ref, ex_ref, gl_ref):
        b = pl.program_id(0)
        j = pl.program_id(1)

        @pl.when(j == 0)
        def _():
            ex_ref[...] = jnp.zeros_like(ex_ref)

        @pl.when(jnp.logical_and(b == 0, j == 0))
        def _():
            gl_ref[...] = jnp.zeros_like(gl_ref)

        dh2 = dh2_ref[...]
        xh2, rs2 = _ln(x1_ref[...])
        ex_ref[0:1, :] += jnp.sum(dh2, axis=0, keepdims=True)
        ex_ref[1:2, :] += jnp.sum(dh2 * xh2, axis=0, keepdims=True)
        dx1 = dx1p_ref[...] + _ln_bwd(dh2 * (1.0 + sc_ref[...]), xh2, rs2)
        xm = xm_ref[...]
        g1 = g1_ref[...]
        r1 = ALPHA * x_ref[...] + g1 * xm
        xh1, rs1 = _ln(r1)
        gl_ref[0:1, :] += jnp.sum(dx1 * xh1, axis=0, keepdims=True)
        gl_ref[1:2, :] += jnp.sum(dx1, axis=0, keepdims=True)
        dr1 = _ln_bwd(dx1 * lg_ref[...], xh1, rs1)
        ex_ref[2:3, :] += jnp.sum(dr1 * xm, axis=0, keepdims=True)
        dxm_ref[...] = (dr1 * g1).astype(BF16)
        dxp_ref[...] = ALPHA * dr1

    la = pl.BlockSpec((TB, D), lambda b, j: (b * nbl + j, 0))
    ex = pl.BlockSpec((None, 1, D), lambda b, j: (b, 0, 0))
    vec = pl.BlockSpec((1, D), lambda b, j: (0, 0))
    return pl.pallas_call(
        body, name="resid1_bwd", grid=(cfg.Bn, nbl),
        in_specs=[la, la, la, la, la, ex, ex, vec],
        out_specs=[la, la, pl.BlockSpec((None, 8, D), lambda b, j: (b, 0, 0)),
                   pl.BlockSpec((8, D), lambda b, j: (0, 0))],
        out_shape=[jax.ShapeDtypeStruct((cfg.N, D), BF16), jax.ShapeDtypeStruct((cfg.N, D), F32),
                   jax.ShapeDtypeStruct((cfg.Bn, 8, D), F32), jax.ShapeDtypeStruct((8, D), F32)],
        compiler_params=_cp(2),
    )(dh2, dx1p, x1, x2, x_mix, gate1, scale2, ln1_g)


def _final_fwd_bwd(cfg, x1, mlp, b2, gate2, ln2_g, ln2_b, target):
    TB, nbl = cfg.TB, cfg.nbl

    def body(x1_ref, m_ref, b2_ref, g2_ref, lg_ref, lb_ref, t_ref, dm_ref, dx_ref, ex_ref, gl_ref):
        b = pl.program_id(0)
        j = pl.program_id(1)

        @pl.when(j == 0)
        def _():
            ex_ref[...] = jnp.zeros_like(ex_ref)

        @pl.when(jnp.logical_and(b == 0, j == 0))
        def _():
            gl_ref[...] = jnp.zeros_like(gl_ref)

        mv = m_ref[...] + b2_ref[...]
        g2 = g2_ref[...]
        r2 = ALPHA * x1_ref[...] + g2 * mv
        xh, rs = _ln(r2)
        lg = lg_ref[...]
        x2 = xh * lg + lb_ref[...]
        err = x2 - t_ref[...]
        ls = jnp.sum(jnp.sum(err * err, axis=1, keepdims=True), axis=0, keepdims=True) * (0.5 / D)
        gl_ref[3:4, :] += ls
        dx2 = err * (1.0 / D)
        gl_ref[0:1, :] += jnp.sum(dx2 * xh, axis=0, keepdims=True)
        gl_ref[1:2, :] += jnp.sum(dx2, axis=0, keepdims=True)
        dr2 = _ln_bwd(dx2 * lg, xh, rs)
        ex_ref[0:1, :] += jnp.sum(dr2 * mv, axis=0, keepdims=True)
        dmv = dr2 * g2
        gl_ref[2:3, :] += jnp.sum(dmv, axis=0, keepdims=True)
        dm_ref[...] = dmv.astype(BF16)
        dx_ref[...] = ALPHA * dr2

    la = pl.BlockSpec((TB, D), lambda b, j: (b * nbl + j, 0))
    ex = pl.BlockSpec((None, 1, D), lambda b, j: (b, 0, 0))
    vec = pl.BlockSpec((1, D), lambda b, j: (0, 0))
    return pl.pallas_call(
        body, name="final_fwd_bwd", grid=(cfg.Bn, nbl),
        in_specs=[la, la, vec, ex, vec, vec, la],
        out_specs=[la, la, pl.BlockSpec((None, 8, D), lambda b, j: (b, 0, 0)),
                   pl.BlockSpec((8, D), lambda b, j: (0, 0))],
        out_shape=[jax.ShapeDtypeStruct((cfg.N, D), BF16), jax.ShapeDtypeStruct((cfg.N, D), F32),
                   jax.ShapeDtypeStruct((cfg.Bn, 8, D), F32), jax.ShapeDtypeStruct((8, D), F32)],
        compiler_params=_cp(2),
    )(x1, mlp, b2, gate2, ln2_g, ln2_b, target)


def _ln_mod_bwd(cfg, dh_a, dh_b, ctx2, x2, scale_tab, dxp):
    TB, nbt, nbl = cfg.TB, cfg.nbt, cfg.nbl

    def body(da_ref, db_ref, c_ref, x_ref, sc_ref, dxp_ref, gx_ref, acc_ref):
        j = pl.program_id(1)

        @pl.when(j <= 1)
        def _():
            acc_ref[...] = jnp.zeros_like(acc_ref)

        dh = da_ref[...] + db_ref[...]
        xhat, rs = _ln(jnp.where(j == 0, c_ref[...], x_ref[...]))
        acc_ref[0:1, :] += jnp.sum(dh, axis=0, keepdims=True)
        acc_ref[1:2, :] += jnp.sum(dh * xhat, axis=0, keepdims=True)
        gx_ref[...] = dxp_ref[...] + _ln_bwd(dh * (1.0 + sc_ref[...]), xhat, rs)

    st = pl.BlockSpec((TB, D), lambda b, j: (b * nbt + j, 0))
    la = pl.BlockSpec((TB, D), lambda b, j: (b * nbl + jnp.maximum(j - 1, 0), 0))
    return pl.pallas_call(
        body, name="ln_mod_bwd", grid=(cfg.Bn, nbt),
        in_specs=[st, st, pl.BlockSpec((TB, D), lambda b, j: (b, 0)), la,
                  pl.BlockSpec((None, 1, D), lambda b, j: (2 * b + jnp.minimum(j, 1), 0, 0)), la],
        out_specs=[la, pl.BlockSpec((None, 8, D), lambda b, j: (2 * b + jnp.minimum(j, 1), 0, 0))],
        out_shape=[jax.ShapeDtypeStruct((cfg.N, D), F32), jax.ShapeDtypeStruct((2 * cfg.Bn, 8, D), F32)],
        compiler_params=_cp(2),
    )(dh_a, dh_b, ctx2, x2, scale_tab, dxp)


def _perm_w_in(w_in):
    w_main = jnp.concatenate([w_in[:, 0:3072], w_in[:, 4160:5184], w_in[:, 3136:4160], w_in[:, 5184:10304]], axis=1)
    w_dt = jnp.pad(w_in[:, 3072:3136], ((0, 0), (0, DTW - 2 * NH)))
    return w_main, w_dt


def _unperm_w_in(dw_main, dw_dt):
    return jnp.concatenate([dw_main[:, 0:3072], dw_dt[:, :2 * NH], dw_main[:, 4096:5120],
                            dw_main[:, 3072:4096], dw_main[:, 5120:]], axis=1)


def _unpack_rest(rest_all):
    out, off = {}, 0
    for n, shp, axis in _BIG[1:]:
        shard_shape = (shp[0] // NDEV, shp[1]) if axis == 0 else (shp[0], shp[1] // NDEV)
        r = math.prod(shard_shape) // 1024
        out[n] = _from_slots(rest_all[:, off:off + r, :].reshape((NDEV,) + shard_shape), axis)
        off += r
    return out


def _local_step(cfg, x, ctx, target, m, mc, W, rest_payload):
    Bn, T, Tc = cfg.Bn, cfg.T, cfg.Tc
    NT, N = cfg.NT, cfg.N
    ctx2, x2 = ctx.reshape(Bn * Tc, D), x.reshape(N, D)
    mch = [m[:, i * D:(i + 1) * D] for i in range(NMOD)]
    ctx_sh = jnp.broadcast_to(mc[None, :D], (Bn, D))
    ctx_sc = jnp.broadcast_to(mc[None, D:], (Bn, D))
    shift_tab = jnp.stack([ctx_sh, mch[0]], axis=1).reshape(2 * Bn, 1, D)
    scale_tab = jnp.stack([ctx_sc, mch[1]], axis=1).reshape(2 * Bn, 1, D)
    gate1 = mch[2].reshape(Bn, 1, D)
    shift2 = mch[3].reshape(Bn, 1, D)
    scale2 = mch[4].reshape(Bn, 1, D)
    gate2 = mch[5].reshape(Bn, 1, D)

    conv_w = jnp.concatenate([W["ssd_conv_w"], W["lru_conv_w"]], axis=1)
    conv_b = jnp.concatenate([W["ssd_conv_b"], W["lru_conv_b"]], axis=1)
    dt_bias = jnp.pad(W["ssd_dt_bias"].reshape(1, 2 * NH), ((0, 0), (0, DTW - 2 * NH)))
    a_log = jnp.pad(W["ssd_a_log"].reshape(1, 2 * NH), ((0, 0), (0, DTW - 2 * NH)))
    dvec = jnp.repeat(W["ssd_d"].reshape(NH), HD).reshape(1, DI)
    lba = W["lru_ba"].reshape(2, 1, LW)
    lbi = W["lru_bi"].reshape(2, 1, LW)
    llam = W["lru_lambda"].reshape(2, 1, LW)

    h = _ln_mod_fwd(cfg, ctx2, x2, shift_tab, scale_tab)
    proj, rest_all = _mm(h, W["w_main"], "nn", "mm_proj", tm=1024, tn=2048, tk=1024, xch=(rest_payload, True))
    W = dict(W, **_unpack_rest(rest_all))
    dt_raw = _mm(h, W["w_dt"], "nn", "mm_dt", tm=512, tn=DTW, tk=1024)
    dt, dtg, cumg, cumTg = _dt_fwd(cfg, dt_raw, dt_bias, a_log)
    act, sgrad = _conv_fwd(cfg, proj, conv_w, conv_b)
    y_f, hs_f = _ssd_fwd(cfg, act, dtg, cumg, cumTg, False)
    y, hs_b = _ssd_fwd(cfg, act, dtg, cumg, cumTg, True, y_other=y_f, dvec=dvec)
    hss = [hs_f, hs_b]
    hls = [_lru_fwd(cfg, act, W["lru_wa"], W["lru_wi"], lba, lbi, llam, rev) for rev in (False, True)]
    nssd = _post_ssd_fwd(cfg, y, proj, W["ssd_norm_w"])
    vlru = _post_lru_fwd(cfg, hls[0], hls[1], proj)
    br_ssd = _mm(nssd, W["w_br_ssd"], "nn", "mm_br_ssd", tm=1024, tn=1024, tk=1024)
    br_lru = _mm(vlru, W["w_br_lru"], "nn", "mm_br_lru", tm=1024, tn=1024, tk=1024)
    mix = _merge_fwd(cfg, proj, W["b_gate"], br_ssd, br_lru)
    x_mix = _mm(mix, W["w_out"], "nn", "mm_out", tm=1024, tn=1024, tk=1024)
    x1, h2 = _resid1_fwd(cfg, x2, x_mix, gate1, shift2, scale2, W["ln1_g"], W["ln1_b"])
    a1, actm = _mm_mlp1(h2, W["w_mlp1"], W["b_mlp1"])
    mlp = _mm(actm, W["w_mlp2"], "nn", "mm_mlp2", tm=1024, tn=1024, tk=2048)
    dmlp, dx1p, ex2, gl2 = _final_fwd_bwd(cfg, x1, mlp, W["b_mlp2"], gate2, W["ln2_g"], W["ln2_b"],
                                          target.reshape(N, D))

    g = {}
    g["ln2_g"], g["ln2_b"], g["b_mlp2"] = gl2[0:1], gl2[1:2], gl2[2:3]
    loss_partial = gl2[3, 0]
    gw = {}
    gw["w_mlp2"] = _mm(actm, dmlp, "tn", "mm_dw_mlp2", BF16, tm=1024, tn=1024, tk=1024)
    da1, accb1 = _mm_dact(dmlp, W["w_mlp2"], a1)
    g["b_mlp1"] = accb1[0:1]
    dh2 = _mm(da1, W["w_mlp1"], "nt", "mm_dh2", tm=1024, tn=1024, tk=2048)
    gw["w_mlp1"] = _mm(h2, da1, "tn", "mm_dw_mlp1", BF16, tm=1024, tn=1024, tk=1024)
    dx_mix, dxp, ex1, gl1 = _resid1_bwd(cfg, dh2, dx1p, x1, x2, x_mix, gate1, scale2, W["ln1_g"])
    g["ln1_g"], g["ln1_b"] = gl1[0:1], gl1[1:2]
    dmix = _mm(dx_mix, W["w_out"], "nt", "mm_dmix", tm=1024, tn=1024, tk=1024)
    gw["w_out"] = _mm(mix, dx_mix, "tn", "mm_dw_out", BF16, tm=1024, tn=1024, tk=1024)
    dbs, dbl, dproj, accg = _merge_bwd(cfg, dmix, proj, W["b_gate"], br_ssd, br_lru)
    g["b_gate"] = accg[0:1]
    dnssd = _mm(dbs, W["w_br_ssd"], "nt", "mm_dnssd", tm=1024, tn=1024, tk=1024)
    gw["w_br_ssd"] = _mm(nssd, dbs, "tn", "mm_dw_br_ssd", BF16, tm=1024, tn=1024, tk=1024)
    dvlru = _mm(dbl, W["w_br_lru"], "nt", "mm_dvlru", tm=1024, tn=1024, tk=1024)
    gw["w_br_lru"] = _mm(vlru, dbl, "tn", "mm_dw_br_lru", BF16, tm=1024, tn=1024, tk=1024)
    dy, dproj, accs = _post_ssd_bwd(cfg, dproj, dnssd, y, act, proj, W["ssd_norm_w"])
    g["ssd_norm_w"] = accs[0:1]
    dD_cols = accs[1:2]
    dyl, dproj = _post_lru_bwd(cfg, dproj, dvlru, hls[0], hls[1], proj)

    rest_slots = jnp.concatenate([_to_slots(gw[n], axis).reshape(NDEV, -1, 1024) for n, _, axis in _BIG[1:]], axis=1)
    xres = {}
    dxh, dBs, dCs, dAs, dxxs, dus = [], [], [], [], [], []
    dwas, dwis, lvecs = [], [], []
    for i, rev in enumerate((False, True)):
        if i == 0:
            o, xres["rs_rest"] = _ssd_bwd(cfg, act, dtg, cumg, cumTg, hss[i], dy, rev, xch=(rest_slots, False))
        else:
            o = _ssd_bwd(cfg, act, dtg, cumg, cumTg, hss[i], dy, rev)
        dxh.append(o[0]); dBs.append(o[1]); dCs.append(o[2]); dAs.append(o[3]); dxxs.append(o[4])
        du, dwa, dwi, lv = _lru_bwd(cfg, act, W["lru_wa"], W["lru_wi"], lba, lbi, llam, hls[i], dyl, rev)
        dus.append(du); dwas.append(dwa); dwis.append(dwi); lvecs.append(lv)
    lru_payload = jnp.stack([jnp.stack(dwas), jnp.stack(dwis)]).reshape(-1, 1024)
    g["lru_ba"] = jnp.stack([lvecs[0][0], lvecs[1][0]])
    g["lru_bi"] = jnp.stack([lvecs[0][1], lvecs[1][1]])
    g["lru_lambda"] = jnp.stack([lvecs[0][2], lvecs[1][2]])

    ddt_raw, accdt = _dt_bwd(cfg, dAs, dxxs, dt_raw, dt, dt_bias, a_log)
    g["ssd_a_log"] = accdt[0, :2 * NH].reshape(2, NH)
    g["ssd_dt_bias"] = accdt[1, :2 * NH].reshape(2, NH)

    (dproj, accx), xres["ag_lru"] = _conv_bwd(cfg, "conv_bwd_x", dproj, proj, conv_w, sgrad, [dxh[0], dxh[1]], 0, DI,
                                              skip=(dy, dvec), xch=(lru_payload, True))
    dproj, accB = _conv_bwd(cfg, "conv_bwd_b", dproj, proj, conv_w, sgrad, [dBs[0], dBs[1]], DI, NG * NS)
    dproj, accC = _conv_bwd(cfg, "conv_bwd_c", dproj, proj, conv_w, sgrad, [dCs[0], dCs[1]], DI + NG * NS, NG * NS)
    dproj, accl = _conv_bwd(cfg, "conv_bwd_lru", dproj, proj, conv_w, None, [dus[0], dus[1]], DI + 2 * NG * NS, LW)
    accssd = jnp.concatenate([accx, accB, accC], axis=1)
    g["ssd_conv_w"], g["ssd_conv_b"] = accssd[0:4], accssd[4:5]
    g["lru_conv_w"], g["lru_conv_b"] = accl[0:4], accl[4:5]
    dw_main = _mm(h, dproj, "tn", "mm_dw_main", BF16, tm=1024, tn=2048, tk=1024)
    dw_dt = _mm(h, ddt_raw, "tn", "mm_dw_dt", BF16, tm=1024, tn=DTW, tk=512)
    w_in_slots = _to_slots(_unperm_w_in(dw_main, dw_dt), 1)
    dh_a, xres["rs_w_in"] = _mm(dproj, W["w_main"], "nt", "mm_dh_main", tm=1024, tn=1024, tk=2048,
                                xch=(w_in_slots, False))
    dh_b = _mm(ddt_raw, W["w_dt"], "nt", "mm_dh_dt", tm=512, tn=1024, tk=DTW)
    grad_x, acct = _ln_mod_bwd(cfg, dh_a, dh_b, ctx2, x2, scale_tab, dxp)
    acct = acct.reshape(Bn, 2, 8, D)
    dm = jnp.concatenate([acct[:, 1, 0], acct[:, 1, 1], ex1[:, 2], ex1[:, 0], ex1[:, 1], ex2[:, 0]], axis=1)
    dmc = jnp.concatenate([acct[:, 0, 0], acct[:, 0, 1]], axis=1)
    g["ssd_d_cols"] = dD_cols
    return loss_partial, grad_x.reshape(Bn, T, D), g, dm, dmc, xres


MESH = pl.DeviceIdType.MESH
_HBM = pl.BlockSpec(memory_space=pltpu.HBM)


def _me():
    return 4 * lax.axis_index("x") + 2 * lax.axis_index("y") + lax.axis_index("c")


def _peer(k):
    px = (lax.axis_index("x") + ((k >> 2) & 1)) % 2
    py = (lax.axis_index("y") + ((k >> 1) & 1)) % 2
    pc = (lax.axis_index("c") + (k & 1)) % 2
    return (px, py, pc), 4 * px + 2 * py + pc


def _xchg_copies(x_ref, o_ref, send_sems, recv_sems, loc_sem, gather):
    me = _me()
    src_me = x_ref if gather else x_ref.at[me]
    loc = pltpu.make_async_copy(src_me, o_ref.at[me], loc_sem)
    sends, recvs = [], []
    for k in range(1, NDEV):
        peer, pid = _peer(k)
        sends.append(pltpu.make_async_remote_copy(
            src_ref=x_ref if gather else x_ref.at[pid], dst_ref=o_ref.at[me],
            send_sem=send_sems.at[k - 1], recv_sem=recv_sems.at[k - 1],
            device_id=peer, device_id_type=MESH))
        recvs.append(pltpu.make_async_remote_copy(
            src_ref=src_me, dst_ref=o_ref.at[pid],
            send_sem=send_sems.at[k - 1], recv_sem=recv_sems.at[k - 1],
            device_id=peer, device_id_type=MESH))
    return loc, sends, recvs


def _xchg_start(*refs, gather):
    loc, sends, _ = _xchg_copies(*refs, gather)
    loc.start()
    for cp in sends:
        cp.start()


def _xchg_wait(*refs, gather):
    loc, sends, recvs = _xchg_copies(*refs, gather)
    for cp in recvs:
        cp.wait_recv()
    for cp in sends:
        cp.wait_send()
    loc.wait()


_XCHG_SCRATCH = [pltpu.SemaphoreType.DMA((NDEV - 1,)), pltpu.SemaphoreType.DMA((NDEV - 1,)), pltpu.SemaphoreType.DMA]


def _xchg_out_shape(x, gather):
    return jax.ShapeDtypeStruct((NDEV,) + tuple(x.shape if gather else x.shape[1:]), x.dtype)


def _exchange(x, name, gather):
    def body(x_ref, o_ref, send_sems, recv_sems, loc_sem):
        _xchg_start(x_ref, o_ref, send_sems, recv_sems, loc_sem, gather=gather)
        _xchg_wait(x_ref, o_ref, send_sems, recv_sems, loc_sem, gather=gather)

    return pl.pallas_call(
        body, name=name, out_shape=_xchg_out_shape(x, gather),
        in_specs=[_HBM], out_specs=_HBM, scratch_shapes=_XCHG_SCRATCH,
    )(x)


def _gather_two_level(x, name):
    def body(x_ref, o_ref, send_sems, recv_sems, loc_sem):
        mx, my, mc = lax.axis_index("x"), lax.axis_index("y"), lax.axis_index("c")
        me, sibling = (mx, my, mc), (mx, my, 1 - mc)
        chips = [(1 - mx, my), (mx, 1 - my), (1 - mx, 1 - my)]

        def slot(px, py, pc):
            return o_ref.at[4 * px + 2 * py + pc]

        def copy(k, block, to, src=None):
            return pltpu.make_async_remote_copy(
                src_ref=slot(*block) if src is None else src, dst_ref=slot(*block),
                send_sem=send_sems.at[k], recv_sem=recv_sems.at[k], device_id=to, device_id_type=MESH)

        mine = pltpu.make_async_copy(x_ref, slot(*me), loc_sem)
        mine.start()
        first = [copy(0, me, sibling, src=x_ref)]
        first += [copy(1 + j, me, (*chip, mc), src=x_ref) for j, chip in enumerate(chips)]
        for cp in first:
            cp.start()
        passed = [copy(4 + j, (*chip, mc), sibling) for j, chip in enumerate(chips)]
        for j, chip in enumerate(chips):
            copy(1 + j, (*chip, mc), me).wait_recv()
            passed[j].start()
        copy(0, sibling, me).wait_recv()
        for j, chip in enumerate(chips):
            copy(4 + j, (*chip, 1 - mc), me).wait_recv()
        for cp in first + passed:
            cp.wait_send()
        mine.wait()

    return pl.pallas_call(
        body, name=name, out_shape=_xchg_out_shape(x, True),
        in_specs=[_HBM], out_specs=_HBM, scratch_shapes=_XCHG_SCRATCH,
    )(x)


def _hosted_call(body, xch, *, name, grid, in_specs, out_specs, out_shape, scratch_shapes, compiler_params, args,
                 aliases=None):
    aliases = aliases or {}
    if xch is None:
        return pl.pallas_call(body, name=name, grid=grid, in_specs=in_specs, out_specs=out_specs,
                              out_shape=out_shape, scratch_shapes=scratch_shapes, input_output_aliases=aliases,
                              compiler_params=compiler_params)(*args)
    xv, gather = xch
    n_in, n_out, n_scr = len(in_specs), len(out_specs), len(scratch_shapes)

    def wrapped(*refs):
        ins = refs[:n_in]
        x_ref = refs[n_in]
        outs = refs[n_in + 1:n_in + 1 + n_out]
        o_ref = refs[n_in + 1 + n_out]
        scr = refs[n_in + 2 + n_out:]
        own, sems = scr[:n_scr], scr[n_scr:]
        first = functools.reduce(jnp.logical_and, [pl.program_id(a) == 0 for a in range(len(grid))])
        last = functools.reduce(jnp.logical_and, [pl.program_id(a) == grid[a] - 1 for a in range(len(grid))])

        @pl.when(first)
        def _():
            _xchg_start(x_ref, o_ref, *sems, gather=gather)

        body(*ins, *outs, *own)

        @pl.when(last)
        def _():
            _xchg_wait(x_ref, o_ref, *sems, gather=gather)

    res = pl.pallas_call(
        wrapped, name=name, grid=grid, in_specs=list(in_specs) + [_HBM], out_specs=list(out_specs) + [_HBM],
        out_shape=list(out_shape) + [_xchg_out_shape(xv, gather)],
        scratch_shapes=list(scratch_shapes) + _XCHG_SCRATCH, input_output_aliases=aliases,
        compiler_params=compiler_params,
    )(*args, xv)
    return list(res[:n_out]), res[n_out]


def _row_tile(R, cap, mult=8):
    best = mult
    t = mult
    while t <= min(R, cap):
        if R % t == 0:
            best = t
        t += mult
    assert R % best == 0, R
    return best


def _sum_slots(x, name, xch=None):
    _, R, C = x.shape
    tr = _row_tile(R, 256, 16 if x.dtype == BF16 else 8)

    def body(x_ref, o_ref):
        o_ref[...] = _slot_sum(x_ref)

    res = _hosted_call(
        body, xch, name=name, grid=(R // tr,),
        in_specs=[pl.BlockSpec((NDEV, tr, C), lambda i: (0, i, 0))],
        out_specs=[pl.BlockSpec((tr, C), lambda i: (i, 0))],
        out_shape=[jax.ShapeDtypeStruct((R, C), F32)],
        scratch_shapes=[], compiler_params=_cp(1), args=(x,))
    if xch is None:
        return res[0]
    return res[0][0], res[1]


def _slot_sum(x_ref):
    acc = x_ref[0].astype(F32)
    for i in range(1, NDEV):
        acc = acc + x_ref[i].astype(F32)
    return acc


def _sum_adamw(slots, w, m, v, name, xch=None):
    _, R, C = slots.shape
    tr = _row_tile(R, 128, 16 if slots.dtype == BF16 else 8)

    def body(x_ref, w_ref, m_ref, v_ref, g_ref, d_ref, nm_ref, nv_ref):
        g_ref[...] = _slot_sum(x_ref)
        _adamw_update(w_ref, g_ref, m_ref, v_ref, d_ref, nm_ref, nv_ref)

    blk = pl.BlockSpec((tr, C), lambda i: (i, 0))
    res = _hosted_call(
        body, xch, name=name, grid=(R // tr,),
        in_specs=[pl.BlockSpec((NDEV, tr, C), lambda i: (0, i, 0)), blk, blk, blk],
        out_specs=[blk] * 4, out_shape=[jax.ShapeDtypeStruct((R, C), F32)] * 4,
        scratch_shapes=[], compiler_params=_cp(1), args=(slots, w, m, v))
    if xch is None:
        return res
    return res[0], res[1]


def _adamw_update(w_ref, g_ref, m_ref, v_ref, d_ref, nm_ref, nv_ref):
    c1 = 1.0 / (1.0 - ADAM_B1 ** ADAM_STEP)
    c2 = 1.0 / (1.0 - ADAM_B2 ** ADAM_STEP)
    gv = g_ref[...]
    nm = ADAM_B1 * m_ref[...] + (1.0 - ADAM_B1) * gv
    nv = ADAM_B2 * v_ref[...] + (1.0 - ADAM_B2) * (gv * gv)
    d_ref[...] = -ADAM_LR * ((nm * c1) / (jnp.sqrt(nv * c2) + ADAM_EPS) + ADAM_WD * w_ref[...])
    nm_ref[...] = nm
    nv_ref[...] = nv


def _adamw_many(ws, gs, ms, vs):
    n = len(ws)

    def body(*refs):
        for i in range(n):
            _adamw_update(refs[i], refs[n + i], refs[2 * n + i], refs[3 * n + i],
                          refs[4 * n + i], refs[5 * n + i], refs[6 * n + i])

    shapes = [jax.ShapeDtypeStruct(w.shape, F32) for w in ws]
    res = pl.pallas_call(
        body, name="adamw_small", out_shape=shapes * 3,
        compiler_params=pltpu.CompilerParams(vmem_limit_bytes=VMEM_LIMIT_BYTES),
    )(*ws, *gs, *ms, *vs)
    return res[:n], res[n:2 * n], res[2 * n:]


def _adamw(w, g, m, v, name):
    R, C = w.shape
    tr = _row_tile(R, 256)

    def body(w_ref, g_ref, m_ref, v_ref, d_ref, nm_ref, nv_ref):
        _adamw_update(w_ref, g_ref, m_ref, v_ref, d_ref, nm_ref, nv_ref)

    blk = pl.BlockSpec((tr, C), lambda i: (i, 0))
    return pl.pallas_call(
        body, name=name, grid=(R // tr,),
        in_specs=[blk] * 4, out_specs=[blk] * 3,
        out_shape=[jax.ShapeDtypeStruct((R, C), F32)] * 3,
        compiler_params=_cp(1),
    )(w, g, m, v)


def _mod_fwd(c_rows, w_shard, b_shard):
    def body(c_ref, w_ref, b_ref, o_ref):
        s = _silu(c_ref[...]).astype(BF16)
        o_ref[...] = _dot(s, w_ref[...].astype(BF16)) + b_ref[...]

    return pl.pallas_call(
        body, name="mod_fwd",
        out_shape=jax.ShapeDtypeStruct((c_rows.shape[0], w_shard.shape[1]), F32),
        compiler_params=pltpu.CompilerParams(vmem_limit_bytes=VMEM_LIMIT_BYTES),
    )(c_rows, w_shard, b_shard)


def _mod_bwd(c_rows, dm_all, dm_shard, w_shard):
    nrow = c_rows.shape[0]

    def body(c_ref, da_ref, ds_ref, w_ref, gw_ref, gb_ref, cc_ref):
        s = _silu(c_ref[...]).astype(BF16)
        ds = ds_ref[...]
        gw_ref[...] = _dot_tn(s, ds.astype(BF16))
        gb_ref[...] = jnp.sum(da_ref[...], axis=0, keepdims=True)
        rowi = lax.broadcasted_iota(jnp.int32, ds.shape, 0)
        dmc = jnp.sum(jnp.where(rowi % 8 >= 4, ds, 0.0), axis=0, keepdims=True)
        dmc8 = jnp.broadcast_to(dmc, (8, ds.shape[1])).astype(BF16)
        cc_ref[...] = _dot_nt(dmc8, w_ref[...].astype(BF16))

    return pl.pallas_call(
        body, name="mod_bwd",
        out_shape=[jax.ShapeDtypeStruct(w_shard.shape, F32),
                   jax.ShapeDtypeStruct((1, dm_all.shape[1]), F32),
                   jax.ShapeDtypeStruct((8, D), F32)],
        compiler_params=pltpu.CompilerParams(vmem_limit_bytes=VMEM_LIMIT_BYTES),
    )(c_rows, dm_all, dm_shard, w_shard)


def _small_finish(cc_pre, c_ctx, dd_cols):
    def body(cc_ref, c_ref, dd_ref, gc_ref, gd_ref):
        gc_ref[...] = cc_ref[...] * _silu_grad(c_ref[...])
        gd_ref[...] = jnp.sum(dd_ref[...], axis=1, keepdims=True)

    return pl.pallas_call(
        body, name="small_finish",
        out_shape=[jax.ShapeDtypeStruct((1, D), F32), jax.ShapeDtypeStruct((NH, 1), F32)],
    )(cc_pre, c_ctx, dd_cols)


_BIG = (("w_in", (D, 10304), 1), ("w_br_ssd", (DI, D), 0), ("w_br_lru", (LW, D), 0), ("w_out", (D, D), 0),
        ("w_mlp1", (D, MLP), 1), ("w_mlp2", (MLP, D), 0))
_SMALL_SH = (("ssd_conv_w", (4, 4096)), ("lru_conv_w", (4, LW)), ("lru_ba", (2, LW)), ("lru_bi", (2, LW)),
             ("lru_lambda", (2, LW)))
_REPL = (("c_ctx", (D,)), ("b_gate", (2 * D,)), ("ssd_conv_b", (4096,)), ("ssd_dt_bias", (2, NH)),
         ("ssd_a_log", (2, NH)), ("ssd_d", (DI,)), ("ssd_norm_w", (DI,)), ("lru_conv_b", (LW,)),
         ("ln1_g", (D,)), ("ln1_b", (D,)),
         ("b_mlp1", (MLP,)), ("b_mlp2", (D,)), ("ln2_g", (D,)), ("ln2_b", (D,)))

_WEIGHT_NAMES = ('c_ctx', 'w_mod', 'b_mod', 'w_in', 'b_gate', 'ssd_conv_w', 'ssd_conv_b', 'ssd_dt_bias', 'ssd_a_log',
                 'ssd_d', 'ssd_norm_w', 'lru_conv_w', 'lru_conv_b', 'lru_wa', 'lru_ba', 'lru_wi', 'lru_bi',
                 'lru_lambda', 'w_br_ssd', 'w_br_lru', 'w_out', 'ln1_g', 'ln1_b', 'w_mlp1', 'b_mlp1', 'w_mlp2',
                 'b_mlp2', 'ln2_g', 'ln2_b')
_ARG_NAMES = ('x', 'c', 'ctx') + _WEIGHT_NAMES + ('loss_target',) + tuple('m_' + n for n in _WEIGHT_NAMES) + tuple(
    'v_' + n for n in _WEIGHT_NAMES)


def _to_slots(full, axis):
    n = full.shape[axis] // NDEV
    if axis == 0:
        return full.reshape(NDEV, n, full.shape[1])
    return full.reshape(full.shape[0], NDEV, n).transpose(1, 0, 2)


def _from_slots(slots, axis):
    if axis == 0:
        return slots.reshape(NDEV * slots.shape[1], slots.shape[2])
    return slots.transpose(1, 0, 2).reshape(slots.shape[1], NDEV * slots.shape[2])


def _pack_rows(arrs, width=1024, mult=8):
    flat = jnp.concatenate([a.reshape(-1) for a in arrs])
    n = flat.shape[0]
    per = width * mult
    tot = -(-n // per) * per
    return jnp.pad(flat, (0, tot - n)).reshape(tot // width, width)


def _unpack_rows(packed, shapes, lead=()):
    nl = len(lead)
    flat = packed.reshape(tuple(lead) + (-1,))
    out, off = [], 0
    for s in shapes:
        n = math.prod(s)
        out.append(flat[..., off:off + n].reshape(tuple(lead) + tuple(s)))
        off += n
    return out


def kernel(x, c, ctx, c_ctx, w_mod, b_mod, w_in, b_gate, ssd_conv_w, ssd_conv_b, ssd_dt_bias, ssd_a_log, ssd_d, ssd_norm_w, lru_conv_w, lru_conv_b, lru_wa, lru_ba, lru_wi, lru_bi, lru_lambda, w_br_ssd, w_br_lru, w_out, ln1_g, ln1_b, w_mlp1, b_mlp1, w_mlp2, b_mlp2, ln2_g, ln2_b, loss_target, m_c_ctx, m_w_mod, m_b_mod, m_w_in, m_b_gate, m_ssd_conv_w, m_ssd_conv_b, m_ssd_dt_bias, m_ssd_a_log, m_ssd_d, m_ssd_norm_w, m_lru_conv_w, m_lru_conv_b, m_lru_wa, m_lru_ba, m_lru_wi, m_lru_bi, m_lru_lambda, m_w_br_ssd, m_w_br_lru, m_w_out, m_ln1_g, m_ln1_b, m_w_mlp1, m_b_mlp1, m_w_mlp2, m_b_mlp2, m_ln2_g, m_ln2_b, v_c_ctx, v_w_mod, v_b_mod, v_w_in, v_b_gate, v_ssd_conv_w, v_ssd_conv_b, v_ssd_dt_bias, v_ssd_a_log, v_ssd_d, v_ssd_norm_w, v_lru_conv_w, v_lru_conv_b, v_lru_wa, v_lru_ba, v_lru_wi, v_lru_bi, v_lru_lambda, v_w_br_ssd, v_w_br_lru, v_w_out, v_ln1_g, v_ln1_b, v_w_mlp1, v_b_mlp1, v_w_mlp2, v_b_mlp2, v_ln2_g, v_ln2_b):
    A = dict(zip(_ARG_NAMES, (x, c, ctx, c_ctx, w_mod, b_mod, w_in, b_gate, ssd_conv_w, ssd_conv_b, ssd_dt_bias, ssd_a_log, ssd_d, ssd_norm_w, lru_conv_w, lru_conv_b, lru_wa, lru_ba, lru_wi, lru_bi, lru_lambda, w_br_ssd, w_br_lru, w_out, ln1_g, ln1_b, w_mlp1, b_mlp1, w_mlp2, b_mlp2, ln2_g, ln2_b, loss_target, m_c_ctx, m_w_mod, m_b_mod, m_w_in, m_b_gate, m_ssd_conv_w, m_ssd_conv_b, m_ssd_dt_bias, m_ssd_a_log, m_ssd_d, m_ssd_norm_w, m_lru_conv_w, m_lru_conv_b, m_lru_wa, m_lru_ba, m_lru_wi, m_lru_bi, m_lru_lambda, m_w_br_ssd, m_w_br_lru, m_w_out, m_ln1_g, m_ln1_b, m_w_mlp1, m_b_mlp1, m_w_mlp2, m_b_mlp2, m_ln2_g, m_ln2_b, v_c_ctx, v_w_mod, v_b_mod, v_w_in, v_b_gate, v_ssd_conv_w, v_ssd_conv_b, v_ssd_dt_bias, v_ssd_a_log, v_ssd_d, v_ssd_norm_w, v_lru_conv_w, v_lru_conv_b, v_lru_wa, v_lru_ba, v_lru_wi, v_lru_bi, v_lru_lambda, v_w_br_ssd, v_w_br_lru, v_w_out, v_ln1_g, v_ln1_b, v_w_mlp1, v_b_mlp1, v_w_mlp2, v_b_mlp2, v_ln2_g, v_ln2_b)))
    Bn, T, _ = x.shape
    Tc = ctx.shape[1]
    cfg = _Cfg(Bn, T, Tc)
    me = _me()
    L = {n: (A[n] if n == "c_ctx" else A[n][0]) for n in _WEIGHT_NAMES}
    nmod = L["w_mod"].shape[1]

    c_all = _exchange(c, "ag_c", True)
    c_rows = jnp.concatenate([c_all.reshape(NDEV * Bn, D), jnp.broadcast_to(c_ctx[None, :], (8, D))], axis=0)
    b_shard = lax.dynamic_slice(L["b_mod"], (me * nmod,), (nmod,)).reshape(1, nmod)
    m_part = _mod_fwd(c_rows, L["w_mod"], b_shard)
    m_all = _exchange(m_part, "ag_mod", True)
    m_full = m_all.transpose(1, 0, 2).reshape(NDEV * Bn + 8, NMOD * D)
    m_mine = lax.dynamic_slice(m_full, (me * Bn, 0), (Bn, NMOD * D))
    mc = m_full[NDEV * Bn, :2 * D]

    w_in_all = _gather_two_level(L["w_in"].astype(BF16), "ag_w_in")
    rest_payload = jnp.concatenate([L[n].astype(BF16).reshape(-1, 1024) for n, _, _ in _BIG[1:]], axis=0)
    small_shapes = [(s[0], s[1] // NDEV) for _, s in _SMALL_SH]
    small_all = _exchange(_pack_rows([L[n] for n, _ in _SMALL_SH], width=512), "ag_w_small", True)
    W = {}
    for (n, shp), piece in zip(_SMALL_SH, _unpack_rows(small_all, small_shapes, lead=(NDEV,))):
        W[n] = piece.transpose(1, 0, 2).reshape(shp)
    W["w_main"], W["w_dt"] = _perm_w_in(_from_slots(w_in_all, 1))
    for n in ("ssd_conv_b", "lru_conv_b", "ssd_norm_w", "b_gate", "ln1_g", "ln1_b", "b_mlp1", "b_mlp2", "ln2_g", "ln2_b"):
        W[n] = L[n].reshape(1, -1)
    for n in ("ssd_dt_bias", "ssd_a_log", "ssd_d", "lru_wa", "lru_wi"):
        W[n] = L[n]

    loss_part, grad_x, g, dm, dmc, xres = _local_step(cfg, x, ctx, loss_target, m_mine, mc, W, rest_payload)
    loss = lax.psum(loss_part, ("x", "y", "c"))

    dmc_pad = jnp.pad(dmc, ((0, 4 - Bn), (0, (NMOD - 2) * D)))
    dm_payload = jnp.concatenate([jnp.pad(dm, ((0, 4 - Bn), (0, 0))), dmc_pad], axis=0)
    upd_w_in, dm_all = _sum_adamw(xres["rs_w_in"], L["w_in"], A["m_w_in"][0], A["v_w_in"][0], "sum_adamw_w_in",
                                  xch=(dm_payload, True))
    dm_all = dm_all.reshape(NDEV * 8, NMOD * D)
    c_rows_b = jnp.concatenate([jnp.pad(c_all, ((0, 0), (0, 4 - Bn), (0, 0))),
                                jnp.broadcast_to(c_ctx[None, None, :], (NDEV, 4, D))], axis=1).reshape(NDEV * 8, D)
    dm_shard = lax.dynamic_slice(dm_all, (0, me * nmod), (NDEV * 8, nmod))
    g_w_mod, g_b_mod, cc_part = _mod_bwd(c_rows_b, dm_all, dm_shard, L["w_mod"])
    g["c_ctx"] = cc_part[0]

    g["ssd_d"] = g.pop("ssd_d_cols")
    small_names = [n for n, _ in _REPL] + [n for n, _ in _SMALL_SH]
    small_full_shapes = [s for _, s in _REPL] + [s for _, s in _SMALL_SH]
    red_b, sm_all = _sum_slots(xres["rs_rest"], "sum_w_rest", xch=(_pack_rows([g[n] for n in small_names]), True))
    sm_sum = _sum_slots(sm_all, "sum_g_small")
    gs = dict(zip(small_names, _unpack_rows(sm_sum, small_full_shapes)))
    gcc, gdd = _small_finish(gs["c_ctx"].reshape(1, D), c_ctx.reshape(1, D), gs["ssd_d"].reshape(NH, HD))
    gs["c_ctx"] = gcc.reshape(D)
    gs["ssd_d"] = gdd.reshape(NH)
    for n, shp in _SMALL_SH:
        ns = shp[1] // NDEV
        gs[n] = lax.dynamic_slice(gs[n], (0, me * ns), (shp[0], ns))
    gs["b_mod"] = g_b_mod.reshape(NMOD * D)
    lru_sum = _sum_slots(xres["ag_lru"], "sum_g_lru").reshape(2, 2, LB, LBW, LBW)
    gs["lru_wa"], gs["lru_wi"] = lru_sum[0], lru_sum[1]

    gb = {}
    off = 0
    for n, shp, axis in _BIG[1:]:
        shard_shape = (shp[0] // NDEV, shp[1]) if axis == 0 else (shp[0], shp[1] // NDEV)
        r = math.prod(shard_shape) // 1024
        gb[n] = red_b[off:off + r].reshape(shard_shape)
        off += r
    gb["w_mod"] = g_w_mod

    grads, deltas, new_m, new_v = {}, {}, {}, {}
    big_names = ["w_mod"] + [n for n, _, _ in _BIG]
    grads["w_in"], deltas["w_in"], new_m["w_in"], new_v["w_in"] = upd_w_in
    for n in big_names:
        if n == "w_in":
            continue
        d_, nm_, nv_ = _adamw(L[n], gb[n], A["m_" + n][0], A["v_" + n][0], "adamw_" + n)
        grads[n], deltas[n], new_m[n], new_v[n] = gb[n], d_, nm_, nv_
    sm_names = [n for n in _WEIGHT_NAMES if n not in big_names]

    def two_d(a):
        return a.reshape(1, -1) if a.ndim == 1 else a
    loc = lambda pre: [two_d(A[pre + n] if n == "c_ctx" else A[pre + n][0]) for n in sm_names]
    gsm = [two_d(gs[n].reshape(L[n].shape)) for n in sm_names]
    ds_, nms_, nvs_ = _adamw_many(loc(""), gsm, loc("m_"), loc("v_"))
    for n, gv, dv, mv, vv in zip(sm_names, gsm, ds_, nms_, nvs_):
        shp = L[n].shape
        grads[n], deltas[n], new_m[n], new_v[n] = gv.reshape(shp), dv.reshape(shp), mv.reshape(shp), vv.reshape(shp)

    def out(dct):
        return [dct[n] if n == "c_ctx" else dct[n][None] for n in _WEIGHT_NAMES]
    return (loss, grad_x, *out(grads), *out(deltas), *out(new_m), *out(new_v))
```

```python
import functools
import math

import jax
import jax.numpy as jnp
from jax import lax
from jax.experimental import pallas as pl
from jax.experimental.pallas import tpu as pltpu

F32 = jnp.float32
BF16 = jnp.bfloat16

D = 1024
GRID_W = 64
DI = 2048
NH = 32
HD = 64
NG = 8
HPG = 4
NS = 128
CH = 128
LW = 1024
LB = 8
LBW = 128
LRU_C = 8.0
MLP = 4096
NMOD = 6
ALPHA = 2.0 ** 0.25
LN_EPS = 1e-6
RMS_EPS = 1e-5
PM = 10240
DTW = 128
CONVW = 5120
NDEV = 8

ADAM_LR = 0.001
ADAM_B1 = 0.9
ADAM_B2 = 0.999
ADAM_EPS = 1e-08
ADAM_WD = 0.01
ADAM_STEP = 10

VMEM_LIMIT_BYTES = 56 * 1024 * 1024


def _cp(n_axes):
    return pltpu.CompilerParams(dimension_semantics=("arbitrary",) * n_axes,
                                vmem_limit_bytes=VMEM_LIMIT_BYTES)


def _sigmoid(x):
    return 0.5 * jnp.tanh(0.5 * x) + 0.5


def _silu(x):
    return x * _sigmoid(x)


def _silu_grad(x):
    s = _sigmoid(x)
    return s * (1.0 + x * (1.0 - s))


def _log1p_pos(e):
    return jnp.where(e < 1e-2, e * (1.0 - e * (0.5 - e * (1.0 / 3.0))), jnp.log(1.0 + e))


def _softplus(x):
    return jnp.maximum(x, 0.0) + _log1p_pos(jnp.exp(-jnp.abs(x)))


_GELU_K = math.sqrt(2.0 / math.pi)


def _gelu(x):
    t = jnp.tanh(_GELU_K * (x + 0.044715 * x * x * x))
    return 0.5 * x * (1.0 + t)


def _gelu_and_grad(x):
    x2 = x * x
    t = jnp.tanh(_GELU_K * x * (1.0 + 0.044715 * x2))
    dt = (1.0 - t * t) * _GELU_K * (1.0 + 3.0 * 0.044715 * x2)
    h = 0.5 * (1.0 + t)
    return x * h, h + 0.5 * x * dt


def _ln(x):
    mu = jnp.mean(x, axis=-1, keepdims=True)
    xc = x - mu
    var = jnp.mean(xc * xc, axis=-1, keepdims=True)
    rs = lax.rsqrt(var + LN_EPS)
    return xc * rs, rs


def _ln_bwd(dy, xhat, rs):
    m1 = jnp.mean(dy, axis=-1, keepdims=True)
    m2 = jnp.mean(dy * xhat, axis=-1, keepdims=True)
    return rs * (dy - m1 - xhat * m2)


def _dot(a, b):
    return lax.dot_general(a, b, (((1,), (0,)), ((), ())), preferred_element_type=F32)


def _dot_nt(a, b):
    return lax.dot_general(a, b, (((1,), (1,)), ((), ())), preferred_element_type=F32)


def _dot_tn(a, b):
    return lax.dot_general(a, b, (((0,), (0,)), ((), ())), preferred_element_type=F32)


def _split3(a):
    a0 = a.astype(BF16)
    r = a - a0.astype(F32)
    a1 = r.astype(BF16)
    a2 = (r - a1.astype(F32)).astype(BF16)
    return a0, a1, a2


def _dot_exact_l(m_bf, a):
    a0, a1, a2 = _split3(a)
    return _dot(m_bf, a0) + _dot(m_bf, a1) + _dot(m_bf, a2)


def _dot_hilo_r(a, m_bf):
    a0 = a.astype(BF16)
    a1 = (a - a0.astype(F32)).astype(BF16)
    return _dot(a0, m_bf) + _dot(a1, m_bf)


def _tri(n, upper):
    ii = lax.broadcasted_iota(jnp.int32, (n, n), 0)
    kk = lax.broadcasted_iota(jnp.int32, (n, n), 1)
    m = (kk >= ii) if upper else (kk <= ii)
    return jnp.where(m, 1.0, 0.0).astype(BF16)


def _fit(n, t):
    t = min(t, n)
    while n % t:
        t //= 2
    return t


def _mm(a, b, mode, name, out_dtype=F32, tm=512, tn=512, tk=512, xch=None):
    if mode == "nn":
        M, K = a.shape
        N = b.shape[1]
    elif mode == "nt":
        M, K = a.shape
        N = b.shape[0]
    else:
        K, M = a.shape
        N = b.shape[1]
    tm, tn, tk = _fit(M, tm), _fit(N, tn), _fit(K, tk)
    assert M % tm == 0 and N % tn == 0 and K % tk == 0, (name, M, N, K, tm, tn, tk)
    nk = K // tk
    if mode == "tn":
        a_spec = pl.BlockSpec((tk, tm), lambda i, j, k: (k, i))
    else:
        a_spec = pl.BlockSpec((tm, tk), lambda i, j, k: (i, k))
    if mode == "nt":
        b_spec = pl.BlockSpec((tn, tk), lambda i, j, k: (j, k))
    else:
        b_spec = pl.BlockSpec((tk, tn), lambda i, j, k: (k, j))
    dn = {"nn": (((1,), (0,)), ((), ())), "nt": (((1,), (1,)), ((), ())), "tn": (((0,), (0,)), ((), ()))}[mode]

    def body(a_ref, b_ref, o_ref, acc_ref):
        k = pl.program_id(2)

        @pl.when(k == 0)
        def _():
            acc_ref[...] = jnp.zeros_like(acc_ref)

        acc_ref[...] += lax.dot_general(a_ref[...].astype(BF16), b_ref[...].astype(BF16), dn,
                                        preferred_element_type=F32)

        @pl.when(k == nk - 1)
        def _():
            o_ref[...] = acc_ref[...].astype(o_ref.dtype)

    def body_one_step(a_ref, b_ref, o_ref):
        o_ref[...] = lax.dot_general(a_ref[...].astype(BF16), b_ref[...].astype(BF16), dn,
                                     preferred_element_type=F32).astype(o_ref.dtype)

    res = _hosted_call(
        body if nk > 1 else body_one_step, xch, name=name, grid=(M // tm, N // tn, nk),
        in_specs=[a_spec, b_spec],
        out_specs=[pl.BlockSpec((tm, tn), lambda i, j, k: (i, j))],
        out_shape=[jax.ShapeDtypeStruct((M, N), out_dtype)],
        scratch_shapes=[pltpu.VMEM((tm, tn), F32)] if nk > 1 else [],
        compiler_params=_cp(3), args=(a, b))
    if xch is None:
        return res[0]
    return res[0][0], res[1]


def _mm_mlp1(h2, w1, b1, tm=1024, tn=1024):
    M, K = h2.shape
    N = w1.shape[1]
    tm, tn = _fit(M, tm), _fit(N, tn)

    def body(a_ref, b_ref, bias_ref, a1_ref, act_ref):
        v = _dot(a_ref[...], b_ref[...]) + bias_ref[...]
        a1_ref[...] = v
        r = jnp.maximum(v, 0.0)
        act_ref[...] = (r * r).astype(BF16)

    out = pl.BlockSpec((tm, tn), lambda i, j: (i, j))
    return pl.pallas_call(
        body, name="mm_mlp1", grid=(M // tm, N // tn),
        in_specs=[pl.BlockSpec((tm, K), lambda i, j: (i, 0)), pl.BlockSpec((K, tn), lambda i, j: (0, j)),
                  pl.BlockSpec((1, tn), lambda i, j: (0, j))],
        out_specs=[out, out],
        out_shape=[jax.ShapeDtypeStruct((M, N), F32), jax.ShapeDtypeStruct((M, N), BF16)],
        compiler_params=_cp(2),
    )(h2, w1, b1)


def _mm_dact(dmlp, w2, a1, tm=1024, tn=1024):
    M, K = dmlp.shape
    N = w2.shape[0]
    tm, tn = _fit(M, tm), _fit(N, tn)

    def body(d_ref, w_ref, a1_ref, o_ref, acc_ref):
        i = pl.program_id(1)

        @pl.when(i == 0)
        def _():
            acc_ref[...] = jnp.zeros_like(acc_ref)

        da = _dot_nt(d_ref[...], w_ref[...]) * (2.0 * jnp.maximum(a1_ref[...], 0.0))
        o_ref[...] = da.astype(BF16)
        acc_ref[0:1, :] += jnp.sum(da, axis=0, keepdims=True)

    blk = pl.BlockSpec((tm, tn), lambda j, i: (i, j))
    return pl.pallas_call(
        body, name="mm_dact", grid=(N // tn, M // tm),
        in_specs=[pl.BlockSpec((tm, K), lambda j, i: (i, 0)), pl.BlockSpec((tn, K), lambda j, i: (j, 0)), blk],
        out_specs=[blk, pl.BlockSpec((8, tn), lambda j, i: (0, j))],
        out_shape=[jax.ShapeDtypeStruct((M, N), BF16), jax.ShapeDtypeStruct((8, N), F32)],
        compiler_params=_cp(2),
    )(dmlp, w2, a1)


class _Cfg:
    def __init__(self, Bn, T, Tc):
        assert T % Tc == 0 and Tc % CH == 0 and Tc % GRID_W == 0
        self.Bn, self.T, self.Tc = Bn, T, Tc
        self.TT = T + Tc
        self.TB = Tc
        self.nbt = self.TT // self.TB
        self.nbl = T // self.TB
        self.NT = Bn * self.TT
        self.N = Bn * T
        self.nct = Tc // CH
        self.nlt = T // CH
        self.nch = self.nct + self.nlt


def _ln_mod_fwd(cfg, ctx2, x2, shift_tab, scale_tab):
    TB, nbt = cfg.TB, cfg.nbt

    nbl = cfg.nbl

    def body(c_ref, x_ref, sh_ref, sc_ref, o_ref):
        j = pl.program_id(1)
        xhat, _ = _ln(jnp.where(j == 0, c_ref[...], x_ref[...]))
        o_ref[...] = (xhat * (1.0 + sc_ref[...]) + sh_ref[...]).astype(BF16)

    tab = pl.BlockSpec((None, 1, D), lambda b, j: (2 * b + jnp.minimum(j, 1), 0, 0))
    return pl.pallas_call(
        body, name="ln_mod_fwd", grid=(cfg.Bn, nbt),
        in_specs=[pl.BlockSpec((TB, D), lambda b, j: (b, 0)),
                  pl.BlockSpec((TB, D), lambda b, j: (b * nbl + jnp.maximum(j - 1, 0), 0)), tab, tab],
        out_specs=pl.BlockSpec((TB, D), lambda b, j: (b * nbt + j, 0)),
        out_shape=jax.ShapeDtypeStruct((cfg.NT, D), BF16),
        compiler_params=_cp(2),
    )(ctx2, x2, shift_tab, scale_tab)


GP = 8
NGB = NG // GP
HPB = GP * HPG


def _heads_to_front(x, d, gb, inverse=False):
    off = d * NH + gb * HPB
    return pltpu.roll(x, off if inverse else (DTW - off) % DTW, 1)


def _chunks_per_step(cfg):
    n = cfg.NT // CH
    return max(c for c in (4, 3, 2, 1) if n % c == 0)


def _dt_fwd(cfg, dt_raw, dt_bias, a_log):
    cps = _chunks_per_step(cfg)

    def body(raw_ref, bias_ref, alog_ref, dt_ref, dtg_ref, cumg_ref, cumT_ref):
        a = -jnp.exp(alog_ref[...])
        col = lax.broadcasted_iota(jnp.int32, (CH, DTW), 1)
        for c in range(cps):
            rows = slice(c * CH, (c + 1) * CH)
            dt = _softplus(raw_ref[rows, :] + bias_ref[...])
            dta = dt * a
            cf = _dot_exact_l(_tri(CH, False), dta)
            cr = _dot_exact_l(_tri(CH, True), dta)
            cum = jnp.where(col < NH, cf, cr)
            dt_ref[rows, :] = dt
            for d in range(2):
                for gb in range(NGB):
                    dtg_ref[d, gb, rows, :] = _heads_to_front(dt, d, gb)
                    cg = _heads_to_front(cum, d, gb)
                    cumg_ref[d, gb, rows, :] = cg
                    cumT_ref[d, gb, c] = cg.T

    blk = pl.BlockSpec((cps * CH, DTW), lambda i: (i, 0))
    row = pl.BlockSpec((1, DTW), lambda i: (0, 0))
    gblk = pl.BlockSpec((2, NGB, cps * CH, DTW), lambda i: (0, 0, i, 0))
    return pl.pallas_call(
        body, name="dt_fwd", grid=(cfg.NT // (cps * CH),),
        in_specs=[blk, row, row],
        out_specs=[blk, gblk, gblk, pl.BlockSpec((2, NGB, cps, DTW, CH), lambda i: (0, 0, i, 0, 0))],
        out_shape=[jax.ShapeDtypeStruct((cfg.NT, DTW), F32),
                   jax.ShapeDtypeStruct((2, NGB, cfg.NT, DTW), F32),
                   jax.ShapeDtypeStruct((2, NGB, cfg.NT, DTW), F32),
                   jax.ShapeDtypeStruct((2, NGB, cfg.NT // CH, DTW, CH), F32)],
        compiler_params=_cp(1),
    )(dt_raw, dt_bias, a_log)


def _dt_bwd(cfg, dAs, dxxs, dt_raw, dt, dt_bias, a_log):
    def body(dAf_ref, dAr_ref, dxf_ref, dxr_ref, raw_ref, dt_ref, bias_ref, alog_ref, o_ref, acc_ref):
        i = pl.program_id(0)

        @pl.when(i == 0)
        def _():
            acc_ref[...] = jnp.zeros_like(acc_ref)

        a = -jnp.exp(alog_ref[...])
        col = lax.broadcasted_iota(jnp.int32, (CH, DTW), 1)
        for c in range(cps):
            rows = slice(c * CH, (c + 1) * CH)
            dA_v = jnp.zeros((CH, DTW), F32)
            dxx_v = jnp.zeros((CH, DTW), F32)
            for d, (ra, rx) in enumerate(((dAf_ref, dxf_ref), (dAr_ref, dxr_ref))):
                for gb in range(NGB):
                    dA_v = dA_v + _heads_to_front(ra[gb, rows, :], d, gb, inverse=True)
                    dxx_v = dxx_v + _heads_to_front(rx[gb, rows, :], d, gb, inverse=True)
            ddta = jnp.where(col < NH, _dot_exact_l(_tri(CH, True), dA_v), _dot_exact_l(_tri(CH, False), dA_v))
            dtv = dt_ref[rows, :]
            ddt = ddta * a + dxx_v
            draw = ddt * _sigmoid(raw_ref[rows, :] + bias_ref[...])
            draw = jnp.where(col < 2 * NH, draw, 0.0)
            o_ref[rows, :] = draw
            da = jnp.sum(ddta * dtv, axis=0, keepdims=True) * a
            da = jnp.where(col[:1] < 2 * NH, da, 0.0)
            acc_ref[0:1, :] += da
            acc_ref[1:2, :] += jnp.sum(draw, axis=0, keepdims=True)

    cps = _chunks_per_step(cfg)
    blk = pl.BlockSpec((cps * CH, DTW), lambda i: (i, 0))
    row = pl.BlockSpec((1, DTW), lambda i: (0, 0))
    gblk = pl.BlockSpec((NGB, cps * CH, DTW), lambda i: (0, i, 0))
    return pl.pallas_call(
        body, name="dt_bwd", grid=(cfg.NT // (cps * CH),),
        in_specs=[gblk, gblk, gblk, gblk, blk, blk, row, row],
        out_specs=[blk, pl.BlockSpec((8, DTW), lambda i: (0, 0))],
        out_shape=[jax.ShapeDtypeStruct((cfg.NT, DTW), F32), jax.ShapeDtypeStruct((8, DTW), F32)],
        compiler_params=_cp(1),
    )(dAs[0], dAs[1], dxxs[0], dxxs[1], dt_raw, dt, dt_bias, a_log)


_TAPS = (2, 1, 0, -1)


def _conv_fwd(cfg, proj, conv_w, conv_b):
    TB, nbt = cfg.TB, cfg.nbt
    CB = CONVW // 2
    SUB = 256
    n_act = DI + 2 * NG * NS

    def body(u_ref, w_ref, b_ref, o_ref, sg_ref):
        i = pl.program_id(0)
        j = pl.program_id(1)
        R = jnp.where(i % nbt == 0, cfg.Tc, GRID_W)
        t = lax.broadcasted_iota(jnp.int32, (TB, SUB), 0)
        pos = jnp.bitwise_and(t, R - 1)
        keep = {s: jnp.where(jnp.logical_and(pos - s >= 0, pos - s < R), 1.0, 0.0) for s in (2, 1, -1)}
        def sub_tile(q, act):
            sl = slice(q * SUB, (q + 1) * SUB)
            u = u_ref[:, sl]
            pre = b_ref[:, sl] + w_ref[2:3, sl] * u
            for k in (0, 1, 3):
                pre = pre + w_ref[k:k + 1, sl] * (pltpu.roll(u, _TAPS[k] % TB, 0) * keep[_TAPS[k]])
            if act:
                s = _sigmoid(pre)
                o_ref[:, sl] = pre * s
                sg_ref[:, sl] = s * (1.0 + pre * (1.0 - s))
            else:
                o_ref[:, sl] = pre

        for q in range(CB // SUB):
            if q * SUB >= n_act - CB:
                pl.when(j == 0)(functools.partial(sub_tile, q, True))
                pl.when(j == 1)(functools.partial(sub_tile, q, False))
            else:
                sub_tile(q, True)

    blk = pl.BlockSpec((TB, CB), lambda i, j: (i, j))
    return pl.pallas_call(
        body, name="conv_fwd", grid=(cfg.NT // TB, CONVW // CB),
        in_specs=[blk, pl.BlockSpec((4, CB), lambda i, j: (0, j)), pl.BlockSpec((1, CB), lambda i, j: (0, j))],
        out_specs=[blk, blk],
        out_shape=[jax.ShapeDtypeStruct((cfg.NT, CONVW), F32)] * 2,
        compiler_params=_cp(2),
    )(proj, conv_w, conv_b)


_ANY = pl.BlockSpec(memory_space=pl.ANY)


def _conv_bwd(cfg, name, dproj, proj, conv_w, sgrad, addends, col0, width, skip=None, xch=None):
    TB, nbt, nbl = cfg.TB, cfg.nbt, cfg.nbl
    CB = 1024
    SUB = 256
    c0 = col0 // CB
    addends = list(addends) + ([] if sgrad is None else [sgrad])
    n_add = len(addends)

    def body(*refs):
        u_ref, w_ref = refs[1:3]
        add_refs = refs[3:3 + n_add]
        rest = refs[3 + n_add:]
        if sgrad is not None:
            add_refs, sg_ref = add_refs[:-1], add_refs[-1]
        if skip is not None:
            dy_ref, dv_ref = rest[:2]
            rest = rest[2:]
        o_ref, acc_ref = rest
        i = pl.program_id(1)

        @pl.when(i == 0)
        def _():
            acc_ref[...] = jnp.zeros_like(acc_ref)

        isctx = (i % nbt) == 0
        R = jnp.where(isctx, cfg.Tc, GRID_W)
        t = lax.broadcasted_iota(jnp.int32, (TB, SUB), 0)
        pos = jnp.bitwise_and(t, R - 1)
        keep = {s: jnp.where(jnp.logical_and(pos - s >= 0, pos - s < R), 1.0, 0.0) for s in (2, 1, -1, -2)}

        def shifted(v, s):
            return v if s == 0 else pltpu.roll(v, s % TB, 0) * keep[s]

        for q in range(CB // SUB):
            sl = slice(q * SUB, (q + 1) * SUB)
            u = u_ref[:, sl]
            us = [shifted(u, _TAPS[k]) for k in range(4)]
            g = add_refs[0][:, sl]
            for r in add_refs[1:]:
                g = g + r[:, sl]
            if skip is not None:
                g = g + jnp.where(isctx, 0.0, dv_ref[:, sl] * dy_ref[:, sl])
            if sgrad is not None:
                g = g * sg_ref[:, sl]
            dp = jnp.zeros_like(g)
            for k in range(4):
                acc_ref[k:k + 1, sl] += jnp.sum(g * us[k], axis=0, keepdims=True)
                dp = dp + w_ref[k:k + 1, sl] * shifted(g, -_TAPS[k])
            acc_ref[4:5, sl] += jnp.sum(g, axis=0, keepdims=True)
            o_ref[:, sl] = dp.astype(BF16)

    blk = pl.BlockSpec((TB, CB), lambda j, i: (i, j))
    wide = pl.BlockSpec((TB, CB), lambda j, i: (i, c0 + j))
    in_specs = [_ANY, wide, pl.BlockSpec((4, CB), lambda j, i: (0, c0 + j))]
    in_specs += [blk] * (n_add if sgrad is None else n_add - 1) + ([] if sgrad is None else [wide])
    args = [dproj, proj, conv_w] + addends
    if skip is not None:
        def lat(j, i):
            b = i // nbt
            return (b * nbl + jnp.maximum(i % nbt - 1, 0), j)
        in_specs += [pl.BlockSpec((TB, CB), lat), pl.BlockSpec((1, CB), lambda j, i: (0, j))]
        args += list(skip)
    return _hosted_call(
        body, xch, name=name, grid=(width // CB, cfg.NT // TB),
        in_specs=in_specs,
        out_specs=[pl.BlockSpec((TB, CB), lambda j, i: (i, c0 + j)), pl.BlockSpec((8, CB), lambda j, i: (0, j))],
        out_shape=[jax.ShapeDtypeStruct((cfg.NT, PM), BF16), jax.ShapeDtypeStruct((8, width), F32)],
        scratch_shapes=[], compiler_params=_cp(2), args=args, aliases={0: 0})


def _chunk_of_step(cfg, rev):
    nct, nlt = cfg.nct, cfg.nlt
    if not rev:
        return lambda s: s
    return lambda s: jnp.where(s < nct, nct - 1 - s, 2 * nct + nlt - 1 - s)


def _expand4(v, band, base):
    out = v[:, base + 3:base + 4]
    for h in (2, 1, 0):
        out = jnp.where(band == h, v[:, base + h:base + h + 1], out)
    return out


def _ssd_step_tiles(dt_ref, cum_ref, cumT_ref, rev):
    cum_t = cum_ref[...]
    last = 0 if rev else CH - 1
    llast = cum_t[last:last + 1, :]
    return (dt_ref[...], cum_t, cumT_ref[...], llast, jnp.exp(llast), last)


def _ssd_common(gi, x_ref, b_ref, c_ref, tiles, rev, intra=True):
    dt_t, cum_t, cumT_t, llast, elast, last = tiles
    base = gi * HPG
    xh = x_ref[:, gi * HPG * HD:(gi + 1) * HPG * HD]
    Bm = b_ref[:, gi * NS:(gi + 1) * NS].astype(BF16)
    band = lax.broadcasted_iota(jnp.int32, (CH, HPG * HD), 1) // HD
    cbs = [jnp.broadcast_to(cum_t[:, base + h:base + h + 1], (CH, CH)) for h in range(HPG)]
    Cm = G = decs = None
    if intra:
        Cm = c_ref[:, gi * NS:(gi + 1) * NS].astype(BF16)
        G = _dot_nt(Cm, Bm)
        ii = lax.broadcasted_iota(jnp.int32, (CH, CH), 0)
        jj = lax.broadcasted_iota(jnp.int32, (CH, CH), 1)
        mask = (jj >= ii) if rev else (jj <= ii)
        decs = [jnp.exp(jnp.where(mask, cbs[h] - cumT_t[base + h:base + h + 1, :], -1e30)) for h in range(HPG)]
    cum_exp = jnp.concatenate([cbs[3], cbs[3]], axis=1)
    ll_exp = llast[:, base + 3:base + 4]
    for h in (2, 1, 0):
        cum_exp = jnp.where(band == h, jnp.concatenate([cbs[h], cbs[h]], axis=1), cum_exp)
        ll_exp = jnp.where(band[:1] == h, llast[:, base + h:base + h + 1], ll_exp)
    ecum = jnp.exp(cum_exp) if intra else None
    e_exp = jnp.exp(ll_exp - cum_exp)
    dt_exp = _expand4(dt_t, band, base)
    X = xh * dt_exp
    rb = lax.broadcasted_iota(jnp.int32, (HPG * HD, NS), 0) // HD
    dec_rows = elast[:, base + 3:base + 4]
    for h in (2, 1, 0):
        dec_rows = jnp.where(rb == h, elast[:, base + h:base + h + 1], dec_rows)
    return xh, Bm, Cm, band, e_exp, ecum, dt_exp, X, G, decs, elast, dec_rows, last


def _ssd_specs(cfg, rev):
    nch = cfg.nch
    cmap = _chunk_of_step(cfg, rev)
    d = 1 if rev else 0

    def make(stepmap):
        def row(b, g, sp):
            return b * nch + cmap(stepmap(sp))
        bo, co = DI // (GP * NS), (DI + NG * NS) // (GP * NS)
        return [
            pl.BlockSpec((CH, GP * HPG * HD), lambda b, g, sp: (row(b, g, sp), g)),
            pl.BlockSpec((CH, GP * NS), lambda b, g, sp: (row(b, g, sp), bo + g)),
            pl.BlockSpec((CH, GP * NS), lambda b, g, sp: (row(b, g, sp), co + g)),
            pl.BlockSpec((None, None, CH, DTW), lambda b, g, sp: (d, g, row(b, g, sp), 0)),
            pl.BlockSpec((None, None, CH, DTW), lambda b, g, sp: (d, g, row(b, g, sp), 0)),
            pl.BlockSpec((None, None, None, DTW, CH), lambda b, g, sp: (d, g, row(b, g, sp), 0, 0)),
        ], row
    return make


def _ssd_fwd(cfg, act, dtg, cumg, cumTg, rev, y_other=None, dvec=None):
    nch = cfg.nch
    in_specs, row = _ssd_specs(cfg, rev)(lambda sp: sp)
    total = y_other is not None

    def body(*refs):
        x_ref, b_ref, c_ref, dt_ref, cum_ref, cumT_ref = refs[:6]
        if total:
            yo_ref, dv_ref = refs[6:8]
        y_ref, hs_ref, h_scr = refs[-3:]
        s = pl.program_id(2)

        @pl.when(s == 0)
        def _():
            h_scr[...] = jnp.zeros_like(h_scr)

        def step(intra):
            tiles = _ssd_step_tiles(dt_ref, cum_ref, cumT_ref, rev)
            for gi in range(GP):
                xh, Bm, Cm, band, e_exp, ecum, dt_exp, X, G, decs, elast, dec_rows, last = _ssd_common(
                    gi, x_ref, b_ref, c_ref, tiles, rev, intra)
                H = h_scr[gi]
                if intra:
                    Mcat = jnp.concatenate([(G * decs[h]).astype(BF16) for h in range(HPG)], axis=1)
                    Xbd = jnp.concatenate([jnp.where(band == h, X, 0.0).astype(BF16) for h in range(HPG)], axis=0)
                    xsl = slice(gi * HPG * HD, (gi + 1) * HPG * HD)
                    Y = ecum * _dot_nt(Cm, H.astype(BF16)) + _dot(Mcat, Xbd)
                    if total:
                        Y = Y + yo_ref[:, xsl] + dv_ref[:, xsl] * xh
                    y_ref[:, xsl] = Y
                hs_ref[gi] = H
                S = _dot_tn((e_exp * X).astype(BF16), Bm)
                h_scr[gi] = dec_rows * H + S

        isctx = cmap(s) < cfg.nct

        @pl.when(isctx)
        def _():
            step(False)

        @pl.when(jnp.logical_not(isctx))
        def _():
            step(True)

    cmap = _chunk_of_step(cfg, rev)
    yblk = pl.BlockSpec((CH, GP * HPG * HD), lambda b, g, s: (row(b, g, s), g))
    args = [act, act, act, dtg, cumg, cumTg]
    if total:
        in_specs = in_specs + [yblk, pl.BlockSpec((1, GP * HPG * HD), lambda b, g, s: (0, g))]
        args += [y_other, dvec]
    return pl.pallas_call(
        body, name="ssd_fwd_rev" if rev else "ssd_fwd", grid=(cfg.Bn, NG // GP, nch),
        in_specs=in_specs,
        out_specs=[yblk, pl.BlockSpec((None, GP, None, HPG * HD, NS), lambda b, g, s: (b, g, s, 0, 0))],
        out_shape=[jax.ShapeDtypeStruct((cfg.NT, DI), F32),
                   jax.ShapeDtypeStruct((cfg.Bn, NG, nch, HPG * HD, NS), F32)],
        scratch_shapes=[pltpu.VMEM((GP, HPG * HD, NS), F32)],
        compiler_params=_cp(3),
    )(*args)


def _ssd_bwd(cfg, act, dtg, cumg, cumTg, hs, dy, rev, xch=None):
    nch, nct, nlt = cfg.nch, cfg.nct, cfg.nlt
    cmap = _chunk_of_step(cfg, rev)
    in_specs, row = _ssd_specs(cfg, rev)(lambda sp: nch - 1 - sp)

    def lat_row(b, g, sp):
        c = cmap(nch - 1 - sp)
        return b * nlt + jnp.maximum(c - nct, 0)

    def body(x_ref, b_ref, c_ref, dt_ref, cum_ref, cumT_ref, dy_ref, hs_ref,
             dxh_ref, dB_ref, dC_ref, dA_ref, dxx_ref, dh_scr):
        sp = pl.program_id(2)

        @pl.when(sp == 0)
        def _():
            dh_scr[...] = jnp.zeros_like(dh_scr)

        def step(intra):
            tiles = _ssd_step_tiles(dt_ref, cum_ref, cumT_ref, rev)
            dA_t = jnp.zeros((CH, DTW), F32)
            dAT_t = jnp.zeros((DTW, CH), F32)
            dxx_t = jnp.zeros((CH, DTW), F32)
            for gi in range(GP):
                dA_g, dAT_g, dxx_g = group_bwd(gi, intra, tiles, x_ref, b_ref, c_ref, dy_ref, hs_ref,
                                               dxh_ref, dB_ref, dC_ref, dh_scr)
                dA_t = dA_t + dA_g
                dxx_t = dxx_t + dxx_g
                if intra:
                    dAT_t = dAT_t + dAT_g
            dA_ref[...] = dA_t - dAT_t.T if intra else dA_t
            dxx_ref[...] = dxx_t

        isctx = cmap(nch - 1 - sp) < nct

        @pl.when(isctx)
        def _():
            step(False)

        @pl.when(jnp.logical_not(isctx))
        def _():
            step(True)

    def group_bwd(gi, intra, tiles, x_ref, b_ref, c_ref, dy_ref, hs_ref, dxh_ref, dB_ref, dC_ref, dh_scr):
        xsl = slice(gi * HPG * HD, (gi + 1) * HPG * HD)
        nsl = slice(gi * NS, (gi + 1) * NS)
        base = gi * HPG
        xh, Bm, Cm, band, e_exp, ecum, dt_exp, X, G, decs, elast, dec_rows, last = _ssd_common(
            gi, x_ref, b_ref, c_ref, tiles, rev, intra)
        H = hs_ref[gi]
        dHn = dh_scr[gi]
        dHnb = dHn.astype(BF16)
        BdH = _dot_nt(Bm, dHnb)
        dX = e_exp * BdH
        eX = e_exp * X
        lanei = lax.broadcasted_iota(jnp.int32, (CH, DTW), 1)
        dA = jnp.zeros((CH, DTW), F32)
        dAT = None
        pb = lax.broadcasted_iota(jnp.int32, (HPG * HD, NS), 0) // HD
        pl_ = lax.broadcasted_iota(jnp.int32, (HPG * HD, NS), 1)
        E = jnp.where(pb + base == pl_, 1.0, 0.0).astype(BF16)
        if intra:
            dY = dy_ref[:, xsl]
            Hb = H.astype(BF16)
            dYs = ecum * dY
            dYsb = dYs.astype(BF16)
            Ys = ecum * _dot_nt(Cm, Hb)
            dG = jnp.zeros((CH, CH), F32)
            subi = lax.broadcasted_iota(jnp.int32, (DTW, CH), 0)
            dAT = jnp.zeros((DTW, CH), F32)
            Xbd = jnp.concatenate([jnp.where(band == h, X, 0.0).astype(BF16) for h in range(HPG)], axis=0)
            dYbd = jnp.concatenate([jnp.where(band == h, dY, 0.0).astype(BF16) for h in range(HPG)], axis=0)
            dMcat = _dot_nt(dY.astype(BF16), Xbd)
            Ms = []
            for h in range(HPG):
                M = G * decs[h]
                dM = dMcat[:, h * CH:(h + 1) * CH]
                W = dM * M
                dG = dG + dM * decs[h]
                Ms.append(M.astype(BF16))
                dA = dA + jnp.where(lanei == base + h, jnp.sum(W, axis=1, keepdims=True), 0.0)
                dAT = dAT + jnp.where(subi == base + h, jnp.sum(W, axis=0, keepdims=True), 0.0)
            dX = dX + _dot_tn(jnp.concatenate(Ms, axis=0), dYbd)
            dGb = dG.astype(BF16)
            dC_ref[:, nsl] = _dot(dGb, Bm) + _dot(dYsb, Hb)
            dB_ref[:, nsl] = _dot_tn(dGb, Cm) + _dot(eX.astype(BF16), dHnb)
            dh_scr[gi] = dec_rows * dHn + _dot_tn(dYsb, Cm)
            dA = dA + _dot_hilo_r(dY * Ys, E)
        else:
            dC_ref[:, nsl] = jnp.zeros((CH, NS), F32)
            dB_ref[:, nsl] = _dot(eX.astype(BF16), dHnb)
            dh_scr[gi] = dec_rows * dHn
        q = _dot_hilo_r(eX * BdH, E)
        r = jnp.sum(dHn * H, axis=1, keepdims=True)
        lane1 = lax.broadcasted_iota(jnp.int32, (1, DTW), 1)
        hdot = jnp.zeros((1, DTW), F32)
        for h in range(HPG):
            hv = jnp.sum(r[h * HD:(h + 1) * HD, :], axis=0, keepdims=True)
            hdot = hdot + jnp.where(lane1 == base + h, hv, 0.0)
        dllast = jnp.sum(q, axis=0, keepdims=True) + elast * hdot
        rowi = lax.broadcasted_iota(jnp.int32, (CH, DTW), 0)
        dxh_ref[:, xsl] = dX * dt_exp
        return dA - q + jnp.where(rowi == last, dllast, 0.0), dAT, _dot_hilo_r(dX * xh, E)

    small = pl.BlockSpec((None, CH, DTW), lambda b, g, sp: (g, row(b, g, sp), 0))
    return _hosted_call(
        body, xch, name="ssd_bwd_rev" if rev else "ssd_bwd", grid=(cfg.Bn, NG // GP, nch),
        in_specs=in_specs + [
            pl.BlockSpec((CH, GP * HPG * HD), lambda b, g, sp: (lat_row(b, g, sp), g)),
            pl.BlockSpec((None, GP, None, HPG * HD, NS), lambda b, g, sp: (b, g, nch - 1 - sp, 0, 0))],
        out_specs=[pl.BlockSpec((CH, GP * HPG * HD), lambda b, g, sp: (row(b, g, sp), g)),
                   pl.BlockSpec((CH, GP * NS), lambda b, g, sp: (row(b, g, sp), g)),
                   pl.BlockSpec((CH, GP * NS), lambda b, g, sp: (row(b, g, sp), g)),
                   small, small],
        out_shape=[jax.ShapeDtypeStruct((cfg.NT, DI), F32),
                   jax.ShapeDtypeStruct((cfg.NT, NG * NS), F32),
                   jax.ShapeDtypeStruct((cfg.NT, NG * NS), F32),
                   jax.ShapeDtypeStruct((NGB, cfg.NT, DTW), F32),
                   jax.ShapeDtypeStruct((NGB, cfg.NT, DTW), F32)],
        scratch_shapes=[pltpu.VMEM((GP, HPG * HD, NS), F32)],
        compiler_params=_cp(3), args=(act, act, act, dtg, cumg, cumTg, dy, hs))


def _shift_rows(v, s, fill, toward_later, rowi):
    n = v.shape[0]
    if toward_later:
        return jnp.where(rowi >= s, pltpu.roll(v, s, 0), fill)
    return jnp.where(rowi < n - s, pltpu.roll(v, n - s, 0), fill)


def _chunk_scan(a, b, carry, later):
    nt = a.shape[0] // 8
    rowi = lax.broadcasted_iota(jnp.int32, (8, a.shape[1]), 0)
    outs = [None] * nt
    for r in (range(nt) if later else range(nt - 1, -1, -1)):
        av = a[r * 8:(r + 1) * 8]
        bv = b[r * 8:(r + 1) * 8]
        for sh in (1, 2, 4):
            a_p = _shift_rows(av, sh, 1.0, later, rowi)
            b_p = _shift_rows(bv, sh, 0.0, later, rowi)
            bv = av * b_p + bv
            av = av * a_p
        h = bv + av * carry
        outs[r] = h
        carry = h[7:8] if later else h[0:1]
    return jnp.concatenate(outs, axis=0), carry


def _lru_gates(u, wa_ref, wi_ref, ba_ref, bi_ref, lam_ref):
    rs, is_ = [], []
    for k in range(LB):
        uk = u[:, k * LBW:(k + 1) * LBW].astype(BF16)
        rs.append(_dot(uk, wa_ref[k].astype(BF16)))
        is_.append(_dot(uk, wi_ref[k].astype(BF16)))
    r = _sigmoid(jnp.concatenate(rs, axis=1) + ba_ref[...])
    ig = _sigmoid(jnp.concatenate(is_, axis=1) + bi_ref[...])
    sp = _softplus(-lam_ref[...])
    la = -LRU_C * r * sp
    a = jnp.exp(la)
    g = jnp.sqrt((1.0 + a * a) * jnp.tanh(-la))
    return r, ig, sp, la, a, g


def _lru_w_specs(d):
    return [pl.BlockSpec((None, LB, LBW, LBW), lambda b, s: (d, 0, 0, 0)),
            pl.BlockSpec((None, LB, LBW, LBW), lambda b, s: (d, 0, 0, 0)),
            pl.BlockSpec((None, 1, LW), lambda b, s: (d, 0, 0)),
            pl.BlockSpec((None, 1, LW), lambda b, s: (d, 0, 0)),
            pl.BlockSpec((None, 1, LW), lambda b, s: (d, 0, 0))]


def _lru_block_of_step(cfg, rev):
    nbl = cfg.nbl
    if not rev:
        return lambda s: s
    return lambda s: jnp.where(s < 1, 0, 1 + nbl - s)


def _lru_fwd(cfg, act, wa, wi, ba, bi, lam, rev):
    nch, CH = cfg.nbt, cfg.TB
    cmap = _lru_block_of_step(cfg, rev)
    d = 1 if rev else 0
    ucol = (DI + 2 * NG * NS) // LW

    def body(u_ref, wa_ref, wi_ref, ba_ref, bi_ref, lam_ref, h_ref, c_scr):
        s = pl.program_id(1)

        @pl.when(s == 0)
        def _():
            c_scr[...] = jnp.zeros_like(c_scr)

        u = u_ref[...]
        r, ig, sp, la, a, g = _lru_gates(u, wa_ref, wi_ref, ba_ref, bi_ref, lam_ref)
        h, carry = _chunk_scan(a, g * ig * u, c_scr[0:1, :], not rev)
        h_ref[...] = h
        c_scr[0:1, :] = carry

    return pl.pallas_call(
        body, name="lru_fwd_rev" if rev else "lru_fwd", grid=(cfg.Bn, nch),
        in_specs=[pl.BlockSpec((CH, LW), lambda b, s: (b * nch + cmap(s), ucol))] + _lru_w_specs(d),
        out_specs=pl.BlockSpec((CH, LW), lambda b, s: (b * nch + cmap(s), 0)),
        out_shape=jax.ShapeDtypeStruct((cfg.NT, LW), F32),
        scratch_shapes=[pltpu.VMEM((8, LW), F32)],
        compiler_params=_cp(2),
    )(act, wa, wi, ba, bi, lam)


def _lru_bwd(cfg, act, wa, wi, ba, bi, lam, hd, dyl, rev):
    nch, nct, nlt, CH = cfg.nbt, 1, cfg.nbl, cfg.TB
    cmap = _lru_block_of_step(cfg, rev)
    d = 1 if rev else 0
    ucol = (DI + 2 * NG * NS) // LW

    def srow(b, sp):
        return b * nch + cmap(nch - 1 - sp)

    def prev_rows(b, sp):
        s = nch - 1 - sp
        cp = cmap(jnp.maximum(s - 1, 0))
        base = (b * nch + cp) * (CH // 8)
        return base + (0 if rev else CH // 8 - 1)

    def lat_row(b, sp):
        c = cmap(nch - 1 - sp)
        return b * nlt + jnp.maximum(c - nct, 0)

    def body(u_ref, wa_ref, wi_ref, ba_ref, bi_ref, lam_ref, h_ref, hp_ref, dy_ref,
             du_ref, dwa_ref, dwi_ref, vec_ref, c_scr):
        b = pl.program_id(0)
        sp_id = pl.program_id(1)
        s = nch - 1 - sp_id

        @pl.when(sp_id == 0)
        def _():
            c_scr[...] = jnp.zeros_like(c_scr)

        @pl.when(jnp.logical_and(b == 0, sp_id == 0))
        def _():
            dwa_ref[...] = jnp.zeros_like(dwa_ref)
            dwi_ref[...] = jnp.zeros_like(dwi_ref)
            vec_ref[...] = jnp.zeros_like(vec_ref)

        c = cmap(s)
        u = u_ref[...]
        r, ig, spl, la, a, g = _lru_gates(u, wa_ref, wi_ref, ba_ref, bi_ref, lam_ref)
        dh = jnp.where(c < nct, 0.0, dy_ref[...])
        rowi = lax.broadcasted_iota(jnp.int32, (CH, LW), 0)
        lamv, _ = _chunk_scan(_shift_rows(a, 1, 1.0, rev, rowi), dh, c_scr[0:1, :], rev)
        first = CH - 1 if rev else 0
        c_scr[0:1, :] = (a * lamv)[first:first + 1, :]
        hprow = hp_ref[...][(0 if rev else 7):(1 if rev else 8), :]
        hprow = jnp.where(s > 0, hprow, 0.0)
        h_prev = _shift_rows(h_ref[...], 1, hprow, not rev, rowi)
        da = lamv * h_prev
        db = lamv
        iu = ig * u
        dla = da * a - db * iu * (a * a) / g
        dr = dla * (-LRU_C * spl)
        di = db * g * u
        du = db * g * ig
        drp = dr * r * (1.0 - r)
        dip = di * ig * (1.0 - ig)
        dus = []
        for k in range(LB):
            sl = slice(k * LBW, (k + 1) * LBW)
            drk = drp[:, sl].astype(BF16)
            dik = dip[:, sl].astype(BF16)
            uk = u[:, sl].astype(BF16)
            dus.append(_dot_nt(drk, wa_ref[k].astype(BF16)) + _dot_nt(dik, wi_ref[k].astype(BF16)))
            dwa_ref[k] += _dot_tn(uk, drk)
            dwi_ref[k] += _dot_tn(uk, dik)
        du_ref[...] = du + jnp.concatenate(dus, axis=1)
        vec_ref[0:1, :] += jnp.sum(drp, axis=0, keepdims=True)
        vec_ref[1:2, :] += jnp.sum(dip, axis=0, keepdims=True)
        dsp = jnp.sum(dla * (-LRU_C * r), axis=0, keepdims=True)
        vec_ref[2:3, :] += dsp * (-_sigmoid(-lam_ref[...]))

    return pl.pallas_call(
        body, name="lru_bwd_rev" if rev else "lru_bwd", grid=(cfg.Bn, nch),
        in_specs=[pl.BlockSpec((CH, LW), lambda b, sp: (srow(b, sp), ucol))] + _lru_w_specs(d) + [
            pl.BlockSpec((CH, LW), lambda b, sp: (srow(b, sp), 0)),
            pl.BlockSpec((8, LW), lambda b, sp: (prev_rows(b, sp), 0)),
            pl.BlockSpec((CH, LW), lambda b, sp: (lat_row(b, sp), 0))],
        out_specs=[pl.BlockSpec((CH, LW), lambda b, sp: (srow(b, sp), 0)),
                   pl.BlockSpec((LB, LBW, LBW), lambda b, sp: (0, 0, 0)),
                   pl.BlockSpec((LB, LBW, LBW), lambda b, sp: (0, 0, 0)),
                   pl.BlockSpec((8, LW), lambda b, sp: (0, 0))],
        out_shape=[jax.ShapeDtypeStruct((cfg.NT, LW), F32),
                   jax.ShapeDtypeStruct((LB, LBW, LBW), F32),
                   jax.ShapeDtypeStruct((LB, LBW, LBW), F32),
                   jax.ShapeDtypeStruct((8, LW), F32)],
        scratch_shapes=[pltpu.VMEM((8, LW), F32)],
        compiler_params=_cp(2),
    )(act, wa, wi, ba, bi, lam, hd, hd, dyl)


HB = 1024


def _post_ssd_fwd(cfg, y, proj, norm_w):
    TB, nbt, nbl = cfg.TB, cfg.nbt, cfg.nbl
    zc = CONVW // HB

    def body(y_ref, z_ref, w_ref, o_ref):
        u = y_ref[...] * _silu(z_ref[...])
        for gi in range(HB // (DI // NG)):
            sl = slice(gi * 256, (gi + 1) * 256)
            ug = u[:, sl]
            rs = lax.rsqrt(jnp.mean(ug * ug, axis=1, keepdims=True) + RMS_EPS)
            o_ref[:, sl] = (ug * rs * w_ref[:, sl]).astype(BF16)

    def st(b, j, cb):
        return (b * nbt + 1 + j, cb)
    return pl.pallas_call(
        body, name="post_ssd_fwd", grid=(cfg.Bn, nbl, DI // HB),
        in_specs=[pl.BlockSpec((TB, HB), st),
                  pl.BlockSpec((TB, HB), lambda b, j, cb: (b * nbt + 1 + j, zc + cb)),
                  pl.BlockSpec((1, HB), lambda b, j, cb: (0, cb))],
        out_specs=pl.BlockSpec((TB, HB), lambda b, j, cb: (b * nbl + j, cb)),
        out_shape=jax.ShapeDtypeStruct((cfg.N, DI), BF16),
        compiler_params=_cp(3),
    )(y, proj, norm_w)


def _post_ssd_bwd(cfg, dproj, dn, y, act, proj, norm_w):
    TB, nbt, nbl = cfg.TB, cfg.nbt, cfg.nbl
    zc = CONVW // HB

    def body(_, dn_ref, y_ref, xh_ref, z_ref, w_ref, dy_ref, dz_ref, acc_ref):
        b = pl.program_id(1)
        j = pl.program_id(2)

        @pl.when(jnp.logical_and(b == 0, j == 0))
        def _():
            acc_ref[...] = jnp.zeros_like(acc_ref)

        @pl.when(j == 0)
        def _():
            dz_ref[...] = jnp.zeros_like(dz_ref)

        @pl.when(j > 0)
        def _():
            latent(dn_ref, y_ref, xh_ref, z_ref, w_ref, dy_ref, dz_ref, acc_ref)

    def latent(dn_ref, y_ref, xh_ref, z_ref, w_ref, dy_ref, dz_ref, acc_ref):
        xh = xh_ref[...]
        z = z_ref[...]
        y = y_ref[...]
        sz = _silu(z)
        u = y * sz
        dout = dn_ref[...]
        for gi in range(HB // (DI // NG)):
            sl = slice(gi * 256, (gi + 1) * 256)
            ug0 = u[:, sl]
            rs = lax.rsqrt(jnp.mean(ug0 * ug0, axis=1, keepdims=True) + RMS_EPS)
            ug = ug0 * rs
            do = dout[:, sl]
            acc_ref[0:1, sl] += jnp.sum(do * ug, axis=0, keepdims=True)
            dug = do * w_ref[:, sl]
            du = rs * (dug - ug * jnp.mean(dug * ug, axis=1, keepdims=True))
            dy = du * sz[:, sl]
            dy_ref[:, sl] = dy
            dz_ref[:, sl] = (du * y[:, sl] * _silu_grad(z[:, sl])).astype(BF16)
            acc_ref[1:2, sl] += jnp.sum(dy * xh[:, sl], axis=0, keepdims=True)

    def st(cb, b, j):
        return (b * nbt + j, cb)

    def la(cb, b, j):
        return (b * nbl + jnp.maximum(j - 1, 0), cb)
    return pl.pallas_call(
        body, name="post_ssd_bwd", grid=(DI // HB, cfg.Bn, nbt),
        in_specs=[_ANY, pl.BlockSpec((TB, HB), la), pl.BlockSpec((TB, HB), st), pl.BlockSpec((TB, HB), st),
                  pl.BlockSpec((TB, HB), lambda cb, b, j: (b * nbt + j, zc + cb)),
                  pl.BlockSpec((1, HB), lambda cb, b, j: (0, cb))],
        out_specs=[pl.BlockSpec((TB, HB), la),
                   pl.BlockSpec((TB, HB), lambda cb, b, j: (b * nbt + j, zc + cb)),
                   pl.BlockSpec((8, HB), lambda cb, b, j: (0, cb))],
        out_shape=[jax.ShapeDtypeStruct((cfg.N, DI), F32), jax.ShapeDtypeStruct((cfg.NT, PM), BF16),
                   jax.ShapeDtypeStruct((8, DI), F32)],
        input_output_aliases={0: 1},
        compiler_params=_cp(3),
    )(dproj, dn, y, act, proj, norm_w)


def _post_lru_fwd(cfg, hf, hb, proj):
    TB, nbt, nbl = cfg.TB, cfg.nbt, cfg.nbl
    gc = (CONVW + DI) // HB

    def body(hf_ref, hb_ref, g_ref, o_ref):
        o_ref[...] = ((hf_ref[...] + hb_ref[...]) * _gelu(g_ref[...])).astype(BF16)

    st = pl.BlockSpec((TB, HB), lambda b, j: (b * nbt + 1 + j, 0))
    return pl.pallas_call(
        body, name="post_lru_fwd", grid=(cfg.Bn, nbl),
        in_specs=[st, st, pl.BlockSpec((TB, HB), lambda b, j: (b * nbt + 1 + j, gc))],
        out_specs=pl.BlockSpec((TB, HB), lambda b, j: (b * nbl + j, 0)),
        out_shape=jax.ShapeDtypeStruct((cfg.N, LW), BF16),
        compiler_params=_cp(2),
    )(hf, hb, proj)


def _post_lru_bwd(cfg, dproj, dv, hf, hb, proj):
    TB, nbt, nbl = cfg.TB, cfg.nbt, cfg.nbl
    gc = (CONVW + DI) // HB

    def body(_, dv_ref, hf_ref, hb_ref, g_ref, dy_ref, dg_ref):
        j = pl.program_id(1)

        @pl.when(j == 0)
        def _():
            dg_ref[...] = jnp.zeros_like(dg_ref)

        @pl.when(j > 0)
        def _():
            gt = g_ref[...]
            dvv = dv_ref[...]
            gl, dgl = _gelu_and_grad(gt)
            dy_ref[...] = dvv * gl
            dg_ref[...] = (dvv * (hf_ref[...] + hb_ref[...]) * dgl).astype(BF16)

    st = pl.BlockSpec((TB, HB), lambda b, j: (b * nbt + j, 0))
    la = pl.BlockSpec((TB, HB), lambda b, j: (b * nbl + jnp.maximum(j - 1, 0), 0))
    gcol = pl.BlockSpec((TB, HB), lambda b, j: (b * nbt + j, gc))
    return pl.pallas_call(
        body, name="post_lru_bwd", grid=(cfg.Bn, nbt),
        in_specs=[_ANY, la, st, st, gcol],
        out_specs=[la, gcol],
        out_shape=[jax.ShapeDtypeStruct((cfg.N, LW), F32), jax.ShapeDtypeStruct((cfg.NT, PM), BF16)],
        input_output_aliases={0: 1},
        compiler_params=_cp(2),
    )(dproj, dv, hf, hb, proj)


def _merge_fwd(cfg, proj, b_gate, br_ssd, br_lru):
    TB, nbt, nbl = cfg.TB, cfg.nbt, cfg.nbl
    mc = (CONVW + DI + LW) // HB

    def body(ms_ref, ml_ref, bg_ref, bs_ref, bl_ref, o_ref):
        gs = _sigmoid(ms_ref[...] + bg_ref[:, :D])
        gl = _sigmoid(ml_ref[...] + bg_ref[:, D:])
        o_ref[...] = (gs * bs_ref[...] + gl * bl_ref[...]).astype(BF16)

    la = pl.BlockSpec((TB, D), lambda b, j: (b * nbl + j, 0))
    return pl.pallas_call(
        body, name="merge_fwd", grid=(cfg.Bn, nbl),
        in_specs=[pl.BlockSpec((TB, HB), lambda b, j: (b * nbt + 1 + j, mc)),
                  pl.BlockSpec((TB, HB), lambda b, j: (b * nbt + 1 + j, mc + 1)),
                  pl.BlockSpec((1, 2 * D), lambda b, j: (0, 0)), la, la],
        out_specs=la,
        out_shape=jax.ShapeDtypeStruct((cfg.N, D), BF16),
        compiler_params=_cp(2),
    )(proj, proj, b_gate, br_ssd, br_lru)


def _merge_bwd(cfg, dmix, proj, b_gate, br_ssd, br_lru):
    TB, nbt, nbl = cfg.TB, cfg.nbt, cfg.nbl
    mc = (CONVW + DI + LW) // HB

    def body(dm_ref, ms_ref, ml_ref, bg_ref, bs_ref, bl_ref, ds_ref, dl_ref, dmg_ref, acc_ref):
        b = pl.program_id(0)
        j = pl.program_id(1)

        @pl.when(jnp.logical_and(b == 0, j == 0))
        def _():
            acc_ref[...] = jnp.zeros_like(acc_ref)

        @pl.when(j == 0)
        def _():
            dmg_ref[...] = jnp.zeros_like(dmg_ref)

        @pl.when(j > 0)
        def _():
            latent(dm_ref, ms_ref, ml_ref, bg_ref, bs_ref, bl_ref, ds_ref, dl_ref, dmg_ref, acc_ref)

    def latent(dm_ref, ms_ref, ml_ref, bg_ref, bs_ref, bl_ref, ds_ref, dl_ref, dmg_ref, acc_ref):
        dm = dm_ref[...]
        gs = _sigmoid(ms_ref[...] + bg_ref[:, :D])
        gl = _sigmoid(ml_ref[...] + bg_ref[:, D:])
        ds_ref[...] = (dm * gs).astype(BF16)
        dl_ref[...] = (dm * gl).astype(BF16)
        dps = dm * bs_ref[...] * gs * (1.0 - gs)
        dpl = dm * bl_ref[...] * gl * (1.0 - gl)
        dmg_ref[:, :D] = dps.astype(BF16)
        dmg_ref[:, D:] = dpl.astype(BF16)
        acc_ref[0:1, :D] += jnp.sum(dps, axis=0, keepdims=True)
        acc_ref[0:1, D:] += jnp.sum(dpl, axis=0, keepdims=True)

    la = pl.BlockSpec((TB, D), lambda b, j: (b * nbl + jnp.maximum(j - 1, 0), 0))
    return pl.pallas_call(
        body, name="merge_bwd", grid=(cfg.Bn, nbt),
        in_specs=[la, pl.BlockSpec((TB, HB), lambda b, j: (b * nbt + j, mc)),
                  pl.BlockSpec((TB, HB), lambda b, j: (b * nbt + j, mc + 1)),
                  pl.BlockSpec((1, 2 * D), lambda b, j: (0, 0)), la, la],
        out_specs=[la, la, pl.BlockSpec((TB, 2 * D), lambda b, j: (b * nbt + j, mc // 2)),
                   pl.BlockSpec((8, 2 * D), lambda b, j: (0, 0))],
        out_shape=[jax.ShapeDtypeStruct((cfg.N, D), BF16), jax.ShapeDtypeStruct((cfg.N, D), BF16),
                   jax.ShapeDtypeStruct((cfg.NT, PM), BF16), jax.ShapeDtypeStruct((8, 2 * D), F32)],
        compiler_params=_cp(2),
    )(dmix, proj, proj, b_gate, br_ssd, br_lru)


def _resid1_fwd(cfg, x2, x_mix, gate1, shift2, scale2, ln1_g, ln1_b):
    TB, nbt, nbl = cfg.TB, cfg.nbt, cfg.nbl

    def body(x_ref, xm_ref, g1_ref, sh_ref, sc_ref, lg_ref, lb_ref, x1_ref, h2_ref):
        r1 = ALPHA * x_ref[...] + g1_ref[...] * xm_ref[...]
        xh, _ = _ln(r1)
        x1 = xh * lg_ref[...] + lb_ref[...]
        x1_ref[...] = x1
        xh2, _ = _ln(x1)
        h2_ref[...] = (xh2 * (1.0 + sc_ref[...]) + sh_ref[...]).astype(BF16)

    la = pl.BlockSpec((TB, D), lambda b, j: (b * nbl + j, 0))
    ex = pl.BlockSpec((None, 1, D), lambda b, j: (b, 0, 0))
    vec = pl.BlockSpec((1, D), lambda b, j: (0, 0))
    return pl.pallas_call(
        body, name="resid1_fwd", grid=(cfg.Bn, nbl),
        in_specs=[la, la, ex, ex, ex, vec, vec],
        out_specs=[la, la],
        out_shape=[jax.ShapeDtypeStruct((cfg.N, D), F32), jax.ShapeDtypeStruct((cfg.N, D), BF16)],
        compiler_params=_cp(2),
    )(x2, x_mix, gate1, shift2, scale2, ln1_g, ln1_b)


def _resid1_bwd(cfg, dh2, dx1p, x1, x2, x_mix, gate1, scale2, ln1_g):
    TB, nbt, nbl = cfg.TB, cfg.nbt, cfg.nbl

    def body(dh2_ref, dx1p_ref, x1_ref, x_ref, xm_ref, g1_ref, sc_ref, lg_ref,
             dxm_ref, dxp_ref, ex_ref, gl_ref):
        b = pl.program_id(0)
        j = pl.program_id(1)

        @pl.when(j == 0)
        def _():
            ex_ref[...] = jnp.zeros_like(ex_ref)

        @pl.when(jnp.logical_and(b == 0, j == 0))
        def _():
            gl_ref[...] = jnp.zeros_like(gl_ref)

        dh2 = dh2_ref[...]
        xh2, rs2 = _ln(x1_ref[...])
        ex_ref[0:1, :] += jnp.sum(dh2, axis=0, keepdims=True)
        ex_ref[1:2, :] += jnp.sum(dh2 * xh2, axis=0, keepdims=True)
        dx1 = dx1p_ref[...] + _ln_bwd(dh2 * (1.0 + sc_ref[...]), xh2, rs2)
        xm = xm_ref[...]
        g1 = g1_ref[...]
        r1 = ALPHA * x_ref[...] + g1 * xm
        xh1, rs1 = _ln(r1)
        gl_ref[0:1, :] += jnp.sum(dx1 * xh1, axis=0, keepdims=True)
        gl_ref[1:2, :] += jnp.sum(dx1, axis=0, keepdims=True)
        dr1 = _ln_bwd(dx1 * lg_ref[...], xh1, rs1)
        ex_ref[2:3, :] += jnp.sum(dr1 * xm, axis=0, keepdims=True)
        dxm_ref[...] = (dr1 * g1).astype(BF16)
        dxp_ref[...] = ALPHA * dr1

    la = pl.BlockSpec((TB, D), lambda b, j: (b * nbl + j, 0))
    ex = pl.BlockSpec((None, 1, D), lambda b, j: (b, 0, 0))
    vec = pl.BlockSpec((1, D), lambda b, j: (0, 0))
    return pl.pallas_call(
        body, name="resid1_bwd", grid=(cfg.Bn, nbl),
        in_specs=[la, la, la, la, la, ex, ex, vec],
        out_specs=[la, la, pl.BlockSpec((None, 8, D), lambda b, j: (b, 0, 0)),
                   pl.BlockSpec((8, D), lambda b, j: (0, 0))],
        out_shape=[jax.ShapeDtypeStruct((cfg.N, D), BF16), jax.ShapeDtypeStruct((cfg.N, D), F32),
                   jax.ShapeDtypeStruct((cfg.Bn, 8, D), F32), jax.ShapeDtypeStruct((8, D), F32)],
        compiler_params=_cp(2),
    )(dh2, dx1p, x1, x2, x_mix, gate1, scale2, ln1_g)


def _final_fwd_bwd(cfg, x1, mlp, b2, gate2, ln2_g, ln2_b, target):
    TB, nbl = cfg.TB, cfg.nbl

    def body(x1_ref, m_ref, b2_ref, g2_ref, lg_ref, lb_ref, t_ref, dm_ref, dx_ref, ex_ref, gl_ref):
        b = pl.program_id(0)
        j = pl.program_id(1)

        @pl.when(j == 0)
        def _():
            ex_ref[...] = jnp.zeros_like(ex_ref)

        @pl.when(jnp.logical_and(b == 0, j == 0))
        def _():
            gl_ref[...] = jnp.zeros_like(gl_ref)

        mv = m_ref[...] + b2_ref[...]
        g2 = g2_ref[...]
        r2 = ALPHA * x1_ref[...] + g2 * mv
        xh, rs = _ln(r2)
        lg = lg_ref[...]
        x2 = xh * lg + lb_ref[...]
        err = x2 - t_ref[...]
        ls = jnp.sum(jnp.sum(err * err, axis=1, keepdims=True), axis=0, keepdims=True) * (0.5 / D)
        gl_ref[3:4, :] += ls
        dx2 = err * (1.0 / D)
        gl_ref[0:1, :] += jnp.sum(dx2 * xh, axis=0, keepdims=True)
        gl_ref[1:2, :] += jnp.sum(dx2, axis=0, keepdims=True)
        dr2 = _ln_bwd(dx2 * lg, xh, rs)
        ex_ref[0:1, :] += jnp.sum(dr2 * mv, axis=0, keepdims=True)
        dmv = dr2 * g2
        gl_ref[2:3, :] += jnp.sum(dmv, axis=0, keepdims=True)
        dm_ref[...] = dmv.astype(BF16)
        dx_ref[...] = ALPHA * dr2

    la = pl.BlockSpec((TB, D), lambda b, j: (b * nbl + j, 0))
    ex = pl.BlockSpec((None, 1, D), lambda b, j: (b, 0, 0))
    vec = pl.BlockSpec((1, D), lambda b, j: (0, 0))
    return pl.pallas_call(
        body, name="final_fwd_bwd", grid=(cfg.Bn, nbl),
        in_specs=[la, la, vec, ex, vec, vec, la],
        out_specs=[la, la, pl.BlockSpec((None, 8, D), lambda b, j: (b, 0, 0)),
                   pl.BlockSpec((8, D), lambda b, j: (0, 0))],
        out_shape=[jax.ShapeDtypeStruct((cfg.N, D), BF16), jax.ShapeDtypeStruct((cfg.N, D), F32),
                   jax.ShapeDtypeStruct((cfg.Bn, 8, D), F32), jax.ShapeDtypeStruct((8, D), F32)],
        compiler_params=_cp(2),
    )(x1, mlp, b2, gate2, ln2_g, ln2_b, target)


def _ln_mod_bwd(cfg, dh_a, dh_b, ctx2, x2, scale_tab, dxp):
    TB, nbt, nbl = cfg.TB, cfg.nbt, cfg.nbl

    def body(da_ref, db_ref, c_ref, x_ref, sc_ref, dxp_ref, gx_ref, acc_ref):
        j = pl.program_id(1)

        @pl.when(j <= 1)
        def _():
            acc_ref[...] = jnp.zeros_like(acc_ref)

        dh = da_ref[...] + db_ref[...]
        xhat, rs = _ln(jnp.where(j == 0, c_ref[...], x_ref[...]))
        acc_ref[0:1, :] += jnp.sum(dh, axis=0, keepdims=True)
        acc_ref[1:2, :] += jnp.sum(dh * xhat, axis=0, keepdims=True)
        gx_ref[...] = dxp_ref[...] + _ln_bwd(dh * (1.0 + sc_ref[...]), xhat, rs)

    st = pl.BlockSpec((TB, D), lambda b, j: (b * nbt + j, 0))
    la = pl.BlockSpec((TB, D), lambda b, j: (b * nbl + jnp.maximum(j - 1, 0), 0))
    return pl.pallas_call(
        body, name="ln_mod_bwd", grid=(cfg.Bn, nbt),
        in_specs=[st, st, pl.BlockSpec((TB, D), lambda b, j: (b, 0)), la,
                  pl.BlockSpec((None, 1, D), lambda b, j: (2 * b + jnp.minimum(j, 1), 0, 0)), la],
        out_specs=[la, pl.BlockSpec((None, 8, D), lambda b, j: (2 * b + jnp.minimum(j, 1), 0, 0))],
        out_shape=[jax.ShapeDtypeStruct((cfg.N, D), F32), jax.ShapeDtypeStruct((2 * cfg.Bn, 8, D), F32)],
        compiler_params=_cp(2),
    )(dh_a, dh_b, ctx2, x2, scale_tab, dxp)


def _perm_w_in(w_in):
    w_main = jnp.concatenate([w_in[:, 0:3072], w_in[:, 4160:5184], w_in[:, 3136:4160], w_in[:, 5184:10304]], axis=1)
    w_dt = jnp.pad(w_in[:, 3072:3136], ((0, 0), (0, DTW - 2 * NH)))
    return w_main, w_dt


def _unperm_w_in(dw_main, dw_dt):
    return jnp.concatenate([dw_main[:, 0:3072], dw_dt[:, :2 * NH], dw_main[:, 4096:5120],
                            dw_main[:, 3072:4096], dw_main[:, 5120:]], axis=1)


def _unpack_rest(rest_all):
    out, off = {}, 0
    for n, shp, axis in _BIG[1:]:
        shard_shape = (shp[0] // NDEV, shp[1]) if axis == 0 else (shp[0], shp[1] // NDEV)
        r = math.prod(shard_shape) // 1024
        out[n] = _from_slots(rest_all[:, off:off + r, :].reshape((NDEV,) + shard_shape), axis)
        off += r
    return out


def _local_step(cfg, x, ctx, target, m, mc, W, rest_payload):
    Bn, T, Tc = cfg.Bn, cfg.T, cfg.Tc
    NT, N = cfg.NT, cfg.N
    ctx2, x2 = ctx.reshape(Bn * Tc, D), x.reshape(N, D)
    mch = [m[:, i * D:(i + 1) * D] for i in range(NMOD)]
    ctx_sh = jnp.broadcast_to(mc[None, :D], (Bn, D))
    ctx_sc = jnp.broadcast_to(mc[None, D:], (Bn, D))
    shift_tab = jnp.stack([ctx_sh, mch[0]], axis=1).reshape(2 * Bn, 1, D)
    scale_tab = jnp.stack([ctx_sc, mch[1]], axis=1).reshape(2 * Bn, 1, D)
    gate1 = mch[2].reshape(Bn, 1, D)
    shift2 = mch[3].reshape(Bn, 1, D)
    scale2 = mch[4].reshape(Bn, 1, D)
    gate2 = mch[5].reshape(Bn, 1, D)

    conv_w = jnp.concatenate([W["ssd_conv_w"], W["lru_conv_w"]], axis=1)
    conv_b = jnp.concatenate([W["ssd_conv_b"], W["lru_conv_b"]], axis=1)
    dt_bias = jnp.pad(W["ssd_dt_bias"].reshape(1, 2 * NH), ((0, 0), (0, DTW - 2 * NH)))
    a_log = jnp.pad(W["ssd_a_log"].reshape(1, 2 * NH), ((0, 0), (0, DTW - 2 * NH)))
    dvec = jnp.repeat(W["ssd_d"].reshape(NH), HD).reshape(1, DI)
    lba = W["lru_ba"].reshape(2, 1, LW)
    lbi = W["lru_bi"].reshape(2, 1, LW)
    llam = W["lru_lambda"].reshape(2, 1, LW)

    h = _ln_mod_fwd(cfg, ctx2, x2, shift_tab, scale_tab)
    proj, rest_all = _mm(h, W["w_main"], "nn", "mm_proj", tm=1024, tn=2048, tk=1024, xch=(rest_payload, True))
    W = dict(W, **_unpack_rest(rest_all))
    dt_raw = _mm(h, W["w_dt"], "nn", "mm_dt", tm=512, tn=DTW, tk=1024)
    dt, dtg, cumg, cumTg = _dt_fwd(cfg, dt_raw, dt_bias, a_log)
    act, sgrad = _conv_fwd(cfg, proj, conv_w, conv_b)
    y_f, hs_f = _ssd_fwd(cfg, act, dtg, cumg, cumTg, False)
    y, hs_b = _ssd_fwd(cfg, act, dtg, cumg, cumTg, True, y_other=y_f, dvec=dvec)
    hss = [hs_f, hs_b]
    hls = [_lru_fwd(cfg, act, W["lru_wa"], W["lru_wi"], lba, lbi, llam, rev) for rev in (False, True)]
    nssd = _post_ssd_fwd(cfg, y, proj, W["ssd_norm_w"])
    vlru = _post_lru_fwd(cfg, hls[0], hls[1], proj)
    br_ssd = _mm(nssd, W["w_br_ssd"], "nn", "mm_br_ssd", tm=1024, tn=1024, tk=1024)
    br_lru = _mm(vlru, W["w_br_lru"], "nn", "mm_br_lru", tm=1024, tn=1024, tk=1024)
    mix = _merge_fwd(cfg, proj, W["b_gate"], br_ssd, br_lru)
    x_mix = _mm(mix, W["w_out"], "nn", "mm_out", tm=1024, tn=1024, tk=1024)
    x1, h2 = _resid1_fwd(cfg, x2, x_mix, gate1, shift2, scale2, W["ln1_g"], W["ln1_b"])
    a1, actm = _mm_mlp1(h2, W["w_mlp1"], W["b_mlp1"])
    mlp = _mm(actm, W["w_mlp2"], "nn", "mm_mlp2", tm=1024, tn=1024, tk=2048)
    dmlp, dx1p, ex2, gl2 = _final_fwd_bwd(cfg, x1, mlp, W["b_mlp2"], gate2, W["ln2_g"], W["ln2_b"],
                                          target.reshape(N, D))

    g = {}
    g["ln2_g"], g["ln2_b"], g["b_mlp2"] = gl2[0:1], gl2[1:2], gl2[2:3]
    loss_partial = gl2[3, 0]
    gw = {}
    gw["w_mlp2"] = _mm(actm, dmlp, "tn", "mm_dw_mlp2", BF16, tm=1024, tn=1024, tk=1024)
    da1, accb1 = _mm_dact(dmlp, W["w_mlp2"], a1)
    g["b_mlp1"] = accb1[0:1]
    dh2 = _mm(da1, W["w_mlp1"], "nt", "mm_dh2", tm=1024, tn=1024, tk=2048)
    gw["w_mlp1"] = _mm(h2, da1, "tn", "mm_dw_mlp1", BF16, tm=1024, tn=1024, tk=1024)
    dx_mix, dxp, ex1, gl1 = _resid1_bwd(cfg, dh2, dx1p, x1, x2, x_mix, gate1, scale2, W["ln1_g"])
    g["ln1_g"], g["ln1_b"] = gl1[0:1], gl1[1:2]
    dmix = _mm(dx_mix, W["w_out"], "nt", "mm_dmix", tm=1024, tn=1024, tk=1024)
    gw["w_out"] = _mm(mix, dx_mix, "tn", "mm_dw_out", BF16, tm=1024, tn=1024, tk=1024)
    dbs, dbl, dproj, accg = _merge_bwd(cfg, dmix, proj, W["b_gate"], br_ssd, br_lru)
    g["b_gate"] = accg[0:1]
    dnssd = _mm(dbs, W["w_br_ssd"], "nt", "mm_dnssd", tm=1024, tn=1024, tk=1024)
    gw["w_br_ssd"] = _mm(nssd, dbs, "tn", "mm_dw_br_ssd", BF16, tm=1024, tn=1024, tk=1024)
    dvlru = _mm(dbl, W["w_br_lru"], "nt", "mm_dvlru", tm=1024, tn=1024, tk=1024)
    gw["w_br_lru"] = _mm(vlru, dbl, "tn", "mm_dw_br_lru", BF16, tm=1024, tn=1024, tk=1024)
    dy, dproj, accs = _post_ssd_bwd(cfg, dproj, dnssd, y, act, proj, W["ssd_norm_w"])
    g["ssd_norm_w"] = accs[0:1]
    dD_cols = accs[1:2]
    dyl, dproj = _post_lru_bwd(cfg, dproj, dvlru, hls[0], hls[1], proj)

    rest_slots = jnp.concatenate([_to_slots(gw[n], axis).reshape(NDEV, -1, 1024) for n, _, axis in _BIG[1:]], axis=1)
    xres = {}
    dxh, dBs, dCs, dAs, dxxs, dus = [], [], [], [], [], []
    dwas, dwis, lvecs = [], [], []
    for i, rev in enumerate((False, True)):
        if i == 0:
            o, xres["rs_rest"] = _ssd_bwd(cfg, act, dtg, cumg, cumTg, hss[i], dy, rev, xch=(rest_slots, False))
        else:
            o = _ssd_bwd(cfg, act, dtg, cumg, cumTg, hss[i], dy, rev)
        dxh.append(o[0]); dBs.append(o[1]); dCs.append(o[2]); dAs.append(o[3]); dxxs.append(o[4])
        du, dwa, dwi, lv = _lru_bwd(cfg, act, W["lru_wa"], W["lru_wi"], lba, lbi, llam, hls[i], dyl, rev)
        dus.append(du); dwas.append(dwa); dwis.append(dwi); lvecs.append(lv)
    lru_payload = jnp.stack([jnp.stack(dwas), jnp.stack(dwis)]).reshape(-1, 1024)
    g["lru_ba"] = jnp.stack([lvecs[0][0], lvecs[1][0]])
    g["lru_bi"] = jnp.stack([lvecs[0][1], lvecs[1][1]])
    g["lru_lambda"] = jnp.stack([lvecs[0][2], lvecs[1][2]])

    ddt_raw, accdt = _dt_bwd(cfg, dAs, dxxs, dt_raw, dt, dt_bias, a_log)
    g["ssd_a_log"] = accdt[0, :2 * NH].reshape(2, NH)
    g["ssd_dt_bias"] = accdt[1, :2 * NH].reshape(2, NH)

    (dproj, accx), xres["ag_lru"] = _conv_bwd(cfg, "conv_bwd_x", dproj, proj, conv_w, sgrad, [dxh[0], dxh[1]], 0, DI,
                                              skip=(dy, dvec), xch=(lru_payload, True))
    dproj, accB = _conv_bwd(cfg, "conv_bwd_b", dproj, proj, conv_w, sgrad, [dBs[0], dBs[1]], DI, NG * NS)
    dproj, accC = _conv_bwd(cfg, "conv_bwd_c", dproj, proj, conv_w, sgrad, [dCs[0], dCs[1]], DI + NG * NS, NG * NS)
    dproj, accl = _conv_bwd(cfg, "conv_bwd_lru", dproj, proj, conv_w, None, [dus[0], dus[1]], DI + 2 * NG * NS, LW)
    accssd = jnp.concatenate([accx, accB, accC], axis=1)
    g["ssd_conv_w"], g["ssd_conv_b"] = accssd[0:4], accssd[4:5]
    g["lru_conv_w"], g["lru_conv_b"] = accl[0:4], accl[4:5]
    dw_main = _mm(h, dproj, "tn", "mm_dw_main", BF16, tm=1024, tn=2048, tk=1024)
    dw_dt = _mm(h, ddt_raw, "tn", "mm_dw_dt", BF16, tm=1024, tn=DTW, tk=512)
    w_in_slots = _to_slots(_unperm_w_in(dw_main, dw_dt), 1)
    dh_a, xres["rs_w_in"] = _mm(dproj, W["w_main"], "nt", "mm_dh_main", tm=1024, tn=1024, tk=2048,
                                xch=(w_in_slots, False))
    dh_b = _mm(ddt_raw, W["w_dt"], "nt", "mm_dh_dt", tm=512, tn=1024, tk=DTW)
    grad_x, acct = _ln_mod_bwd(cfg, dh_a, dh_b, ctx2, x2, scale_tab, dxp)
    acct = acct.reshape(Bn, 2, 8, D)
    dm = jnp.concatenate([acct[:, 1, 0], acct[:, 1, 1], ex1[:, 2], ex1[:, 0], ex1[:, 1], ex2[:, 0]], axis=1)
    dmc = jnp.concatenate([acct[:, 0, 0], acct[:, 0, 1]], axis=1)
    g["ssd_d_cols"] = dD_cols
    return loss_partial, grad_x.reshape(Bn, T, D), g, dm, dmc, xres


MESH = pl.DeviceIdType.MESH
_HBM = pl.BlockSpec(memory_space=pltpu.HBM)


def _me():
    return 4 * lax.axis_index("x") + 2 * lax.axis_index("y") + lax.axis_index("c")


def _peer(k):
    px = (lax.axis_index("x") + ((k >> 2) & 1)) % 2
    py = (lax.axis_index("y") + ((k >> 1) & 1)) % 2
    pc = (lax.axis_index("c") + (k & 1)) % 2
    return (px, py, pc), 4 * px + 2 * py + pc


def _xchg_copies(x_ref, o_ref, send_sems, recv_sems, loc_sem, gather):
    me = _me()
    src_me = x_ref if gather else x_ref.at[me]
    loc = pltpu.make_async_copy(src_me, o_ref.at[me], loc_sem)
    sends, recvs = [], []
    for k in range(1, NDEV):
        peer, pid = _peer(k)
        sends.append(pltpu.make_async_remote_copy(
            src_ref=x_ref if gather else x_ref.at[pid], dst_ref=o_ref.at[me],
            send_sem=send_sems.at[k - 1], recv_sem=recv_sems.at[k - 1],
            device_id=peer, device_id_type=MESH))
        recvs.append(pltpu.make_async_remote_copy(
            src_ref=src_me, dst_ref=o_ref.at[pid],
            send_sem=send_sems.at[k - 1], recv_sem=recv_sems.at[k - 1],
            device_id=peer, device_id_type=MESH))
    return loc, sends, recvs


def _xchg_start(*refs, gather):
    loc, sends, _ = _xchg_copies(*refs, gather)
    loc.start()
    for cp in sends:
        cp.start()


def _xchg_wait(*refs, gather):
    loc, sends, recvs = _xchg_copies(*refs, gather)
    for cp in recvs:
        cp.wait_recv()
    for cp in sends:
        cp.wait_send()
    loc.wait()


_XCHG_SCRATCH = [pltpu.SemaphoreType.DMA((NDEV - 1,)), pltpu.SemaphoreType.DMA((NDEV - 1,)), pltpu.SemaphoreType.DMA]


def _xchg_out_shape(x, gather):
    return jax.ShapeDtypeStruct((NDEV,) + tuple(x.shape if gather else x.shape[1:]), x.dtype)


def _exchange(x, name, gather):
    def body(x_ref, o_ref, send_sems, recv_sems, loc_sem):
        _xchg_start(x_ref, o_ref, send_sems, recv_sems, loc_sem, gather=gather)
        _xchg_wait(x_ref, o_ref, send_sems, recv_sems, loc_sem, gather=gather)

    return pl.pallas_call(
        body, name=name, out_shape=_xchg_out_shape(x, gather),
        in_specs=[_HBM], out_specs=_HBM, scratch_shapes=_XCHG_SCRATCH,
    )(x)


def _gather_two_level(x, name):
    def body(x_ref, o_ref, send_sems, recv_sems, loc_sem):
        mx, my, mc = lax.axis_index("x"), lax.axis_index("y"), lax.axis_index("c")
        me, sibling = (mx, my, mc), (mx, my, 1 - mc)
        chips = [(1 - mx, my), (mx, 1 - my), (1 - mx, 1 - my)]

        def slot(px, py, pc):
            return o_ref.at[4 * px + 2 * py + pc]

        def copy(k, block, to, src=None):
            return pltpu.make_async_remote_copy(
                src_ref=slot(*block) if src is None else src, dst_ref=slot(*block),
                send_sem=send_sems.at[k], recv_sem=recv_sems.at[k], device_id=to, device_id_type=MESH)

        mine = pltpu.make_async_copy(x_ref, slot(*me), loc_sem)
        mine.start()
        first = [copy(0, me, sibling, src=x_ref)]
        first += [copy(1 + j, me, (*chip, mc), src=x_ref) for j, chip in enumerate(chips)]
        for cp in first:
            cp.start()
        passed = [copy(4 + j, (*chip, mc), sibling) for j, chip in enumerate(chips)]
        for j, chip in enumerate(chips):
            copy(1 + j, (*chip, mc), me).wait_recv()
            passed[j].start()
        copy(0, sibling, me).wait_recv()
        for j, chip in enumerate(chips):
            copy(4 + j, (*chip, 1 - mc), me).wait_recv()
        for cp in first + passed:
            cp.wait_send()
        mine.wait()

    return pl.pallas_call(
        body, name=name, out_shape=_xchg_out_shape(x, True),
        in_specs=[_HBM], out_specs=_HBM, scratch_shapes=_XCHG_SCRATCH,
    )(x)


def _hosted_call(body, xch, *, name, grid, in_specs, out_specs, out_shape, scratch_shapes, compiler_params, args,
                 aliases=None):
    aliases = aliases or {}
    if xch is None:
        return pl.pallas_call(body, name=name, grid=grid, in_specs=in_specs, out_specs=out_specs,
                              out_shape=out_shape, scratch_shapes=scratch_shapes, input_output_aliases=aliases,
                              compiler_params=compiler_params)(*args)
    xv, gather = xch
    n_in, n_out, n_scr = len(in_specs), len(out_specs), len(scratch_shapes)

    def wrapped(*refs):
        ins = refs[:n_in]
        x_ref = refs[n_in]
        outs = refs[n_in + 1:n_in + 1 + n_out]
        o_ref = refs[n_in + 1 + n_out]
        scr = refs[n_in + 2 + n_out:]
        own, sems = scr[:n_scr], scr[n_scr:]
        first = functools.reduce(jnp.logical_and, [pl.program_id(a) == 0 for a in range(len(grid))])
        last = functools.reduce(jnp.logical_and, [pl.program_id(a) == grid[a] - 1 for a in range(len(grid))])

        @pl.when(first)
        def _():
            _xchg_start(x_ref, o_ref, *sems, gather=gather)

        body(*ins, *outs, *own)

        @pl.when(last)
        def _():
            _xchg_wait(x_ref, o_ref, *sems, gather=gather)

    res = pl.pallas_call(
        wrapped, name=name, grid=grid, in_specs=list(in_specs) + [_HBM], out_specs=list(out_specs) + [_HBM],
        out_shape=list(out_shape) + [_xchg_out_shape(xv, gather)],
        scratch_shapes=list(scratch_shapes) + _XCHG_SCRATCH, input_output_aliases=aliases,
        compiler_params=compiler_params,
    )(*args, xv)
    return list(res[:n_out]), res[n_out]


def _row_tile(R, cap, mult=8):
    best = mult
    t = mult
    while t <= min(R, cap):
        if R % t == 0:
            best = t
        t += mult
    assert R % best == 0, R
    return best


def _sum_slots(x, name, xch=None):
    _, R, C = x.shape
    tr = _row_tile(R, 256, 16 if x.dtype == BF16 else 8)

    def body(x_ref, o_ref):
        o_ref[...] = _slot_sum(x_ref)

    res = _hosted_call(
        body, xch, name=name, grid=(R // tr,),
        in_specs=[pl.BlockSpec((NDEV, tr, C), lambda i: (0, i, 0))],
        out_specs=[pl.BlockSpec((tr, C), lambda i: (i, 0))],
        out_shape=[jax.ShapeDtypeStruct((R, C), F32)],
        scratch_shapes=[], compiler_params=_cp(1), args=(x,))
    if xch is None:
        return res[0]
    return res[0][0], res[1]


def _slot_sum(x_ref):
    acc = x_ref[0].astype(F32)
    for i in range(1, NDEV):
        acc = acc + x_ref[i].astype(F32)
    return acc


def _sum_adamw(slots, w, m, v, name, xch=None):
    _, R, C = slots.shape
    tr = _row_tile(R, 128, 16 if slots.dtype == BF16 else 8)

    def body(x_ref, w_ref, m_ref, v_ref, g_ref, d_ref, nm_ref, nv_ref):
        g_ref[...] = _slot_sum(x_ref)
        _adamw_update(w_ref, g_ref, m_ref, v_ref, d_ref, nm_ref, nv_ref)

    blk = pl.BlockSpec((tr, C), lambda i: (i, 0))
    res = _hosted_call(
        body, xch, name=name, grid=(R // tr,),
        in_specs=[pl.BlockSpec((NDEV, tr, C), lambda i: (0, i, 0)), blk, blk, blk],
        out_specs=[blk] * 4, out_shape=[jax.ShapeDtypeStruct((R, C), F32)] * 4,
        scratch_shapes=[], compiler_params=_cp(1), args=(slots, w, m, v))
    if xch is None:
        return res
    return res[0], res[1]


def _adamw_update(w_ref, g_ref, m_ref, v_ref, d_ref, nm_ref, nv_ref):
    c1 = 1.0 / (1.0 - ADAM_B1 ** ADAM_STEP)
    c2 = 1.0 / (1.0 - ADAM_B2 ** ADAM_STEP)
    gv = g_ref[...]
    nm = ADAM_B1 * m_ref[...] + (1.0 - ADAM_B1) * gv
    nv = ADAM_B2 * v_ref[...] + (1.0 - ADAM_B2) * (gv * gv)
    d_ref[...] = -ADAM_LR * ((nm * c1) / (jnp.sqrt(nv * c2) + ADAM_EPS) + ADAM_WD * w_ref[...])
    nm_ref[...] = nm
    nv_ref[...] = nv


def _adamw_many(ws, gs, ms, vs):
    n = len(ws)

    def body(*refs):
        for i in range(n):
            _adamw_update(refs[i], refs[n + i], refs[2 * n + i], refs[3 * n + i],
                          refs[4 * n + i], refs[5 * n + i], refs[6 * n + i])

    shapes = [jax.ShapeDtypeStruct(w.shape, F32) for w in ws]
    res = pl.pallas_call(
        body, name="adamw_small", out_shape=shapes * 3,
        compiler_params=pltpu.CompilerParams(vmem_limit_bytes=VMEM_LIMIT_BYTES),
    )(*ws, *gs, *ms, *vs)
    return res[:n], res[n:2 * n], res[2 * n:]


def _adamw(w, g, m, v, name):
    R, C = w.shape
    tr = _row_tile(R, 256)

    def body(w_ref, g_ref, m_ref, v_ref, d_ref, nm_ref, nv_ref):
        _adamw_update(w_ref, g_ref, m_ref, v_ref, d_ref, nm_ref, nv_ref)

    blk = pl.BlockSpec((tr, C), lambda i: (i, 0))
    return pl.pallas_call(
        body, name=name, grid=(R // tr,),
        in_specs=[blk] * 4, out_specs=[blk] * 3,
        out_shape=[jax.ShapeDtypeStruct((R, C), F32)] * 3,
        compiler_params=_cp(1),
    )(w, g, m, v)


def _mod_fwd(c_rows, w_shard, b_shard):
    def body(c_ref, w_ref, b_ref, o_ref):
        s = _silu(c_ref[...]).astype(BF16)
        o_ref[...] = _dot(s, w_ref[...].astype(BF16)) + b_ref[...]

    return pl.pallas_call(
        body, name="mod_fwd",
        out_shape=jax.ShapeDtypeStruct((c_rows.shape[0], w_shard.shape[1]), F32),
        compiler_params=pltpu.CompilerParams(vmem_limit_bytes=VMEM_LIMIT_BYTES),
    )(c_rows, w_shard, b_shard)


def _mod_bwd(c_rows, dm_all, dm_shard, w_shard):
    nrow = c_rows.shape[0]

    def body(c_ref, da_ref, ds_ref, w_ref, gw_ref, gb_ref, cc_ref):
        s = _silu(c_ref[...]).astype(BF16)
        ds = ds_ref[...]
        gw_ref[...] = _dot_tn(s, ds.astype(BF16))
        gb_ref[...] = jnp.sum(da_ref[...], axis=0, keepdims=True)
        rowi = lax.broadcasted_iota(jnp.int32, ds.shape, 0)
        dmc = jnp.sum(jnp.where(rowi % 8 >= 4, ds, 0.0), axis=0, keepdims=True)
        dmc8 = jnp.broadcast_to(dmc, (8, ds.shape[1])).astype(BF16)
        cc_ref[...] = _dot_nt(dmc8, w_ref[...].astype(BF16))

    return pl.pallas_call(
        body, name="mod_bwd",
        out_shape=[jax.ShapeDtypeStruct(w_shard.shape, F32),
                   jax.ShapeDtypeStruct((1, dm_all.shape[1]), F32),
                   jax.ShapeDtypeStruct((8, D), F32)],
        compiler_params=pltpu.CompilerParams(vmem_limit_bytes=VMEM_LIMIT_BYTES),
    )(c_rows, dm_all, dm_shard, w_shard)


def _small_finish(cc_pre, c_ctx, dd_cols):
    def body(cc_ref, c_ref, dd_ref, gc_ref, gd_ref):
        gc_ref[...] = cc_ref[...] * _silu_grad(c_ref[...])
        gd_ref[...] = jnp.sum(dd_ref[...], axis=1, keepdims=True)

    return pl.pallas_call(
        body, name="small_finish",
        out_shape=[jax.ShapeDtypeStruct((1, D), F32), jax.ShapeDtypeStruct((NH, 1), F32)],
    )(cc_pre, c_ctx, dd_cols)


_BIG = (("w_in", (D, 10304), 1), ("w_br_ssd", (DI, D), 0), ("w_br_lru", (LW, D), 0), ("w_out", (D, D), 0),
        ("w_mlp1", (D, MLP), 1), ("w_mlp2", (MLP, D), 0))
_SMALL_SH = (("ssd_conv_w", (4, 4096)), ("lru_conv_w", (4, LW)), ("lru_ba", (2, LW)), ("lru_bi", (2, LW)),
             ("lru_lambda", (2, LW)))
_REPL = (("c_ctx", (D,)), ("b_gate", (2 * D,)), ("ssd_conv_b", (4096,)), ("ssd_dt_bias", (2, NH)),
         ("ssd_a_log", (2, NH)), ("ssd_d", (DI,)), ("ssd_norm_w", (DI,)), ("lru_conv_b", (LW,)),
         ("ln1_g", (D,)), ("ln1_b", (D,)),
         ("b_mlp1", (MLP,)), ("b_mlp2", (D,)), ("ln2_g", (D,)), ("ln2_b", (D,)))

_WEIGHT_NAMES = ('c_ctx', 'w_mod', 'b_mod', 'w_in', 'b_gate', 'ssd_conv_w', 'ssd_conv_b', 'ssd_dt_bias', 'ssd_a_log',
                 'ssd_d', 'ssd_norm_w', 'lru_conv_w', 'lru_conv_b', 'lru_wa', 'lru_ba', 'lru_wi', 'lru_bi',
                 'lru_lambda', 'w_br_ssd', 'w_br_lru', 'w_out', 'ln1_g', 'ln1_b', 'w_mlp1', 'b_mlp1', 'w_mlp2',
                 'b_mlp2', 'ln2_g', 'ln2_b')
_ARG_NAMES = ('x', 'c', 'ctx') + _WEIGHT_NAMES + ('loss_target',) + tuple('m_' + n for n in _WEIGHT_NAMES) + tuple(
    'v_' + n for n in _WEIGHT_NAMES)


def _to_slots(full, axis):
    n = full.shape[axis] // NDEV
    if axis == 0:
        return full.reshape(NDEV, n, full.shape[1])
    return full.reshape(full.shape[0], NDEV, n).transpose(1, 0, 2)


def _from_slots(slots, axis):
    if axis == 0:
        return slots.reshape(NDEV * slots.shape[1], slots.shape[2])
    return slots.transpose(1, 0, 2).reshape(slots.shape[1], NDEV * slots.shape[2])


def _pack_rows(arrs, width=1024, mult=8):
    flat = jnp.concatenate([a.reshape(-1) for a in arrs])
    n = flat.shape[0]
    per = width * mult
    tot = -(-n // per) * per
    return jnp.pad(flat, (0, tot - n)).reshape(tot // width, width)


def _unpack_rows(packed, shapes, lead=()):
    nl = len(lead)
    flat = packed.reshape(tuple(lead) + (-1,))
    out, off = [], 0
    for s in shapes:
        n = math.prod(s)
        out.append(flat[..., off:off + n].reshape(tuple(lead) + tuple(s)))
        off += n
    return out


def kernel(x, c, ctx, c_ctx, w_mod, b_mod, w_in, b_gate, ssd_conv_w, ssd_conv_b, ssd_dt_bias, ssd_a_log, ssd_d, ssd_norm_w, lru_conv_w, lru_conv_b, lru_wa, lru_ba, lru_wi, lru_bi, lru_lambda, w_br_ssd, w_br_lru, w_out, ln1_g, ln1_b, w_mlp1, b_mlp1, w_mlp2, b_mlp2, ln2_g, ln2_b, loss_target, m_c_ctx, m_w_mod, m_b_mod, m_w_in, m_b_gate, m_ssd_conv_w, m_ssd_conv_b, m_ssd_dt_bias, m_ssd_a_log, m_ssd_d, m_ssd_norm_w, m_lru_conv_w, m_lru_conv_b, m_lru_wa, m_lru_ba, m_lru_wi, m_lru_bi, m_lru_lambda, m_w_br_ssd, m_w_br_lru, m_w_out, m_ln1_g, m_ln1_b, m_w_mlp1, m_b_mlp1, m_w_mlp2, m_b_mlp2, m_ln2_g, m_ln2_b, v_c_ctx, v_w_mod, v_b_mod, v_w_in, v_b_gate, v_ssd_conv_w, v_ssd_conv_b, v_ssd_dt_bias, v_ssd_a_log, v_ssd_d, v_ssd_norm_w, v_lru_conv_w, v_lru_conv_b, v_lru_wa, v_lru_ba, v_lru_wi, v_lru_bi, v_lru_lambda, v_w_br_ssd, v_w_br_lru, v_w_out, v_ln1_g, v_ln1_b, v_w_mlp1, v_b_mlp1, v_w_mlp2, v_b_mlp2, v_ln2_g, v_ln2_b):
    A = dict(zip(_ARG_NAMES, (x, c, ctx, c_ctx, w_mod, b_mod, w_in, b_gate, ssd_conv_w, ssd_conv_b, ssd_dt_bias, ssd_a_log, ssd_d, ssd_norm_w, lru_conv_w, lru_conv_b, lru_wa, lru_ba, lru_wi, lru_bi, lru_lambda, w_br_ssd, w_br_lru, w_out, ln1_g, ln1_b, w_mlp1, b_mlp1, w_mlp2, b_mlp2, ln2_g, ln2_b, loss_target, m_c_ctx, m_w_mod, m_b_mod, m_w_in, m_b_gate, m_ssd_conv_w, m_ssd_conv_b, m_ssd_dt_bias, m_ssd_a_log, m_ssd_d, m_ssd_norm_w, m_lru_conv_w, m_lru_conv_b, m_lru_wa, m_lru_ba, m_lru_wi, m_lru_bi, m_lru_lambda, m_w_br_ssd, m_w_br_lru, m_w_out, m_ln1_g, m_ln1_b, m_w_mlp1, m_b_mlp1, m_w_mlp2, m_b_mlp2, m_ln2_g, m_ln2_b, v_c_ctx, v_w_mod, v_b_mod, v_w_in, v_b_gate, v_ssd_conv_w, v_ssd_conv_b, v_ssd_dt_bias, v_ssd_a_log, v_ssd_d, v_ssd_norm_w, v_lru_conv_w, v_lru_conv_b, v_lru_wa, v_lru_ba, v_lru_wi, v_lru_bi, v_lru_lambda, v_w_br_ssd, v_w_br_lru, v_w_out, v_ln1_g, v_ln1_b, v_w_mlp1, v_b_mlp1, v_w_mlp2, v_b_mlp2, v_ln2_g, v_ln2_b)))
    Bn, T, _ = x.shape
    Tc = ctx.shape[1]
    cfg = _Cfg(Bn, T, Tc)
    me = _me()
    L = {n: (A[n] if n == "c_ctx" else A[n][0]) for n in _WEIGHT_NAMES}
    nmod = L["w_mod"].shape[1]

    c_all = _exchange(c, "ag_c", True)
    c_rows = jnp.concatenate([c_all.reshape(NDEV * Bn, D), jnp.broadcast_to(c_ctx[None, :], (8, D))], axis=0)
    b_shard = lax.dynamic_slice(L["b_mod"], (me * nmod,), (nmod,)).reshape(1, nmod)
    m_part = _mod_fwd(c_rows, L["w_mod"], b_shard)
    m_all = _exchange(m_part, "ag_mod", True)
    m_full = m_all.transpose(1, 0, 2).reshape(NDEV * Bn + 8, NMOD * D)
    m_mine = lax.dynamic_slice(m_full, (me * Bn, 0), (Bn, NMOD * D))
    mc = m_full[NDEV * Bn, :2 * D]

    w_in_all = _gather_two_level(L["w_in"].astype(BF16), "ag_w_in")
    rest_payload = jnp.concatenate([L[n].astype(BF16).reshape(-1, 1024) for n, _, _ in _BIG[1:]], axis=0)
    small_shapes = [(s[0], s[1] // NDEV) for _, s in _SMALL_SH]
    small_all = _exchange(_pack_rows([L[n] for n, _ in _SMALL_SH], width=512), "ag_w_small", True)
    W = {}
    for (n, shp), piece in zip(_SMALL_SH, _unpack_rows(small_all, small_shapes, lead=(NDEV,))):
        W[n] = piece.transpose(1, 0, 2).reshape(shp)
    W["w_main"], W["w_dt"] = _perm_w_in(_from_slots(w_in_all, 1))
    for n in ("ssd_conv_b", "lru_conv_b", "ssd_norm_w", "b_gate", "ln1_g", "ln1_b", "b_mlp1", "b_mlp2", "ln2_g", "ln2_b"):
        W[n] = L[n].reshape(1, -1)
    for n in ("ssd_dt_bias", "ssd_a_log", "ssd_d", "lru_wa", "lru_wi"):
        W[n] = L[n]

    loss_part, grad_x, g, dm, dmc, xres = _local_step(cfg, x, ctx, loss_target, m_mine, mc, W, rest_payload)
    loss = lax.psum(loss_part, ("x", "y", "c"))

    dmc_pad = jnp.pad(dmc, ((0, 4 - Bn), (0, (NMOD - 2) * D)))
    dm_payload = jnp.concatenate([jnp.pad(dm, ((0, 4 - Bn), (0, 0))), dmc_pad], axis=0)
    upd_w_in, dm_all = _sum_adamw(xres["rs_w_in"], L["w_in"], A["m_w_in"][0], A["v_w_in"][0], "sum_adamw_w_in",
                                  xch=(dm_payload, True))
    dm_all = dm_all.reshape(NDEV * 8, NMOD * D)
    c_rows_b = jnp.concatenate([jnp.pad(c_all, ((0, 0), (0, 4 - Bn), (0, 0))),
                                jnp.broadcast_to(c_ctx[None, None, :], (NDEV, 4, D))], axis=1).reshape(NDEV * 8, D)
    dm_shard = lax.dynamic_slice(dm_all, (0, me * nmod), (NDEV * 8, nmod))
    g_w_mod, g_b_mod, cc_part = _mod_bwd(c_rows_b, dm_all, dm_shard, L["w_mod"])
    g["c_ctx"] = cc_part[0]

    g["ssd_d"] = g.pop("ssd_d_cols")
    small_names = [n for n, _ in _REPL] + [n for n, _ in _SMALL_SH]
    small_full_shapes = [s for _, s in _REPL] + [s for _, s in _SMALL_SH]
    red_b, sm_all = _sum_slots(xres["rs_rest"], "sum_w_rest", xch=(_pack_rows([g[n] for n in small_names]), True))
    sm_sum = _sum_slots(sm_all, "sum_g_small")
    gs = dict(zip(small_names, _unpack_rows(sm_sum, small_full_shapes)))
    gcc, gdd = _small_finish(gs["c_ctx"].reshape(1, D), c_ctx.reshape(1, D), gs["ssd_d"].reshape(NH, HD))
    gs["c_ctx"] = gcc.reshape(D)
    gs["ssd_d"] = gdd.reshape(NH)
    for n, shp in _SMALL_SH:
        ns = shp[1] // NDEV
        gs[n] = lax.dynamic_slice(gs[n], (0, me * ns), (shp[0], ns))
    gs["b_mod"] = g_b_mod.reshape(NMOD * D)
    lru_sum = _sum_slots(xres["ag_lru"], "sum_g_lru").reshape(2, 2, LB, LBW, LBW)
    gs["lru_wa"], gs["lru_wi"] = lru_sum[0], lru_sum[1]

    gb = {}
    off = 0
    for n, shp, axis in _BIG[1:]:
        shard_shape = (shp[0] // NDEV, shp[1]) if axis == 0 else (shp[0], shp[1] // NDEV)
        r = math.prod(shard_shape) // 1024
        gb[n] = red_b[off:off + r].reshape(shard_shape)
        off += r
    gb["w_mod"] = g_w_mod

    grads, deltas, new_m, new_v = {}, {}, {}, {}
    big_names = ["w_mod"] + [n for n, _, _ in _BIG]
    grads["w_in"], deltas["w_in"], new_m["w_in"], new_v["w_in"] = upd_w_in
    for n in big_names:
        if n == "w_in":
            continue
        d_, nm_, nv_ = _adamw(L[n], gb[n], A["m_" + n][0], A["v_" + n][0], "adamw_" + n)
        grads[n], deltas[n], new_m[n], new_v[n] = gb[n], d_, nm_, nv_
    sm_names = [n for n in _WEIGHT_NAMES if n not in big_names]

    def two_d(a):
        return a.reshape(1, -1) if a.ndim == 1 else a
    loc = lambda pre: [two_d(A[pre + n] if n == "c_ctx" else A[pre + n][0]) for n in sm_names]
    gsm = [two_d(gs[n].reshape(L[n].shape)) for n in sm_names]
    ds_, nms_, nvs_ = _adamw_many(loc(""), gsm, loc("m_"), loc("v_"))
    for n, gv, dv, mv, vv in zip(sm_names, gsm, ds_, nms_, nvs_):
        shp = L[n].shape
        grads[n], deltas[n], new_m[n], new_v[n] = gv.reshape(shp), dv.reshape(shp), mv.reshape(shp), vv.reshape(shp)

    def out(dct):
        return [dct[n] if n == "c_ctx" else dct[n][None] for n in _WEIGHT_NAMES]
    return (loss, grad_x, *out(grads), *out(deltas), *out(new_m), *out(new_v))
```

```python
import functools
import math

import jax
import jax.numpy as jnp
from jax import lax
from jax.experimental import pallas as pl
from jax.experimental.pallas import tpu as pltpu

F32 = jnp.float32
BF16 = jnp.bfloat16

D = 1024
GRID_W = 64
DI = 2048
NH = 32
HD = 64
NG = 8
HPG = 4
NS = 128
CH = 128
LW = 1024
LB = 8
LBW = 128
LRU_C = 8.0
MLP = 4096
NMOD = 6
ALPHA = 2.0 ** 0.25
LN_EPS = 1e-6
RMS_EPS = 1e-5
PM = 10240
DTW = 128
CONVW = 5120
NDEV = 8

ADAM_LR = 0.001
ADAM_B1 = 0.9
ADAM_B2 = 0.999
ADAM_EPS = 1e-08
ADAM_WD = 0.01
ADAM_STEP = 10

VMEM_LIMIT_BYTES = 56 * 1024 * 1024


def _cp(n_axes):
    return pltpu.CompilerParams(dimension_semantics=("arbitrary",) * n_axes,
                                vmem_limit_bytes=VMEM_LIMIT_BYTES)


def _sigmoid(x):
    return 0.5 * jnp.tanh(0.5 * x) + 0.5


def _silu(x):
    return x * _sigmoid(x)


def _silu_grad(x):
    s = _sigmoid(x)
    return s * (1.0 + x * (1.0 - s))


def _log1p_pos(e):
    return jnp.where(e < 1e-2, e * (1.0 - e * (0.5 - e * (1.0 / 3.0))), jnp.log(1.0 + e))


def _softplus(x):
    return jnp.maximum(x, 0.0) + _log1p_pos(jnp.exp(-jnp.abs(x)))


_GELU_K = math.sqrt(2.0 / math.pi)


def _gelu(x):
    t = jnp.tanh(_GELU_K * (x + 0.044715 * x * x * x))
    return 0.5 * x * (1.0 + t)


def _gelu_and_grad(x):
    x2 = x * x
    t = jnp.tanh(_GELU_K * x * (1.0 + 0.044715 * x2))
    dt = (1.0 - t * t) * _GELU_K * (1.0 + 3.0 * 0.044715 * x2)
    h = 0.5 * (1.0 + t)
    return x * h, h + 0.5 * x * dt


def _ln(x):
    mu = jnp.mean(x, axis=-1, keepdims=True)
    xc = x - mu
    var = jnp.mean(xc * xc, axis=-1, keepdims=True)
    rs = lax.rsqrt(var + LN_EPS)
    return xc * rs, rs


def _ln_bwd(dy, xhat, rs):
    m1 = jnp.mean(dy, axis=-1, keepdims=True)
    m2 = jnp.mean(dy * xhat, axis=-1, keepdims=True)
    return rs * (dy - m1 - xhat * m2)


def _dot(a, b):
    return lax.dot_general(a, b, (((1,), (0,)), ((), ())), preferred_element_type=F32)


def _dot_nt(a, b):
    return lax.dot_general(a, b, (((1,), (1,)), ((), ())), preferred_element_type=F32)


def _dot_tn(a, b):
    return lax.dot_general(a, b, (((0,), (0,)), ((), ())), preferred_element_type=F32)


def _split3(a):
    a0 = a.astype(BF16)
    r = a - a0.astype(F32)
    a1 = r.astype(BF16)
    a2 = (r - a1.astype(F32)).astype(BF16)
    return a0, a1, a2


def _dot_exact_l(m_bf, a):
    a0, a1, a2 = _split3(a)
    return _dot(m_bf, a0) + _dot(m_bf, a1) + _dot(m_bf, a2)


def _dot_hilo_r(a, m_bf):
    a0 = a.astype(BF16)
    a1 = (a - a0.astype(F32)).astype(BF16)
    return _dot(a0, m_bf) + _dot(a1, m_bf)


def _tri(n, upper):
    ii = lax.broadcasted_iota(jnp.int32, (n, n), 0)
    kk = lax.broadcasted_iota(jnp.int32, (n, n), 1)
    m = (kk >= ii) if upper else (kk <= ii)
    return jnp.where(m, 1.0, 0.0).astype(BF16)


def _fit(n, t):
    t = min(t, n)
    while n % t:
        t //= 2
    return t


def _mm(a, b, mode, name, out_dtype=F32, tm=512, tn=512, tk=512, xch=None):
    if mode == "nn":
        M, K = a.shape
        N = b.shape[1]
    elif mode == "nt":
        M, K = a.shape
        N = b.shape[0]
    else:
        K, M = a.shape
        N = b.shape[1]
    tm, tn, tk = _fit(M, tm), _fit(N, tn), _fit(K, tk)
    assert M % tm == 0 and N % tn == 0 and K % tk == 0, (name, M, N, K, tm, tn, tk)
    nk = K // tk
    if mode == "tn":
        a_spec = pl.BlockSpec((tk, tm), lambda i, j, k: (k, i))
    else:
        a_spec = pl.BlockSpec((tm, tk), lambda i, j, k: (i, k))
    if mode == "nt":
        b_spec = pl.BlockSpec((tn, tk), lambda i, j, k: (j, k))
    else:
        b_spec = pl.BlockSpec((tk, tn), lambda i, j, k: (k, j))
    dn = {"nn": (((1,), (0,)), ((), ())), "nt": (((1,), (1,)), ((), ())), "tn": (((0,), (0,)), ((), ()))}[mode]

    def body(a_ref, b_ref, o_ref, acc_ref):
        k = pl.program_id(2)

        @pl.when(k == 0)
        def _():
            acc_ref[...] = jnp.zeros_like(acc_ref)

        acc_ref[...] += lax.dot_general(a_ref[...].astype(BF16), b_ref[...].astype(BF16), dn,
                                        preferred_element_type=F32)

        @pl.when(k == nk - 1)
        def _():
            o_ref[...] = acc_ref[...].astype(o_ref.dtype)

    def body_one_step(a_ref, b_ref, o_ref):
        o_ref[...] = lax.dot_general(a_ref[...].astype(BF16), b_ref[...].astype(BF16), dn,
                                     preferred_element_type=F32).astype(o_ref.dtype)

    res = _hosted_call(
        body if nk > 1 else body_one_step, xch, name=name, grid=(M // tm, N // tn, nk),
        in_specs=[a_spec, b_spec],
        out_specs=[pl.BlockSpec((tm, tn), lambda i, j, k: (i, j))],
        out_shape=[jax.ShapeDtypeStruct((M, N), out_dtype)],
        scratch_shapes=[pltpu.VMEM((tm, tn), F32)] if nk > 1 else [],
        compiler_params=_cp(3), args=(a, b))
    if xch is None:
        return res[0]
    return res[0][0], res[1]


def _mm_mlp1(h2, w1, b1, tm=1024, tn=1024):
    M, K = h2.shape
    N = w1.shape[1]
    tm, tn = _fit(M, tm), _fit(N, tn)

    def body(a_ref, b_ref, bias_ref, a1_ref, act_ref):
        v = _dot(a_ref[...], b_ref[...]) + bias_ref[...]
        a1_ref[...] = v
        r = jnp.maximum(v, 0.0)
        act_ref[...] = (r * r).astype(BF16)

    out = pl.BlockSpec((tm, tn), lambda i, j: (i, j))
    return pl.pallas_call(
        body, name="mm_mlp1", grid=(M // tm, N // tn),
        in_specs=[pl.BlockSpec((tm, K), lambda i, j: (i, 0)), pl.BlockSpec((K, tn), lambda i, j: (0, j)),
                  pl.BlockSpec((1, tn), lambda i, j: (0, j))],
        out_specs=[out, out],
        out_shape=[jax.ShapeDtypeStruct((M, N), F32), jax.ShapeDtypeStruct((M, N), BF16)],
        compiler_params=_cp(2),
    )(h2, w1, b1)


def _mm_dact(dmlp, w2, a1, tm=1024, tn=1024):
    M, K = dmlp.shape
    N = w2.shape[0]
    tm, tn = _fit(M, tm), _fit(N, tn)

    def body(d_ref, w_ref, a1_ref, o_ref, acc_ref):
        i = pl.program_id(1)

        @pl.when(i == 0)
        def _():
            acc_ref[...] = jnp.zeros_like(acc_ref)

        da = _dot_nt(d_ref[...], w_ref[...]) * (2.0 * jnp.maximum(a1_ref[...], 0.0))
        o_ref[...] = da.astype(BF16)
        acc_ref[0:1, :] += jnp.sum(da, axis=0, keepdims=True)

    blk = pl.BlockSpec((tm, tn), lambda j, i: (i, j))
    return pl.pallas_call(
        body, name="mm_dact", grid=(N // tn, M // tm),
        in_specs=[pl.BlockSpec((tm, K), lambda j, i: (i, 0)), pl.BlockSpec((tn, K), lambda j, i: (j, 0)), blk],
        out_specs=[blk, pl.BlockSpec((8, tn), lambda j, i: (0, j))],
        out_shape=[jax.ShapeDtypeStruct((M, N), BF16), jax.ShapeDtypeStruct((8, N), F32)],
        compiler_params=_cp(2),
    )(dmlp, w2, a1)


class _Cfg:
    def __init__(self, Bn, T, Tc):
        assert T % Tc == 0 and Tc % CH == 0 and Tc % GRID_W == 0
        self.Bn, self.T, self.Tc = Bn, T, Tc
        self.TT = T + Tc
        self.TB = Tc
        self.nbt = self.TT // self.TB
        self.nbl = T // self.TB
        self.NT = Bn * self.TT
        self.N = Bn * T
        self.nct = Tc // CH
        self.nlt = T // CH
        self.nch = self.nct + self.nlt


def _ln_mod_fwd(cfg, ctx2, x2, shift_tab, scale_tab):
    TB, nbt = cfg.TB, cfg.nbt

    nbl = cfg.nbl

    def body(c_ref, x_ref, sh_ref, sc_ref, o_ref):
        j = pl.program_id(1)
        xhat, _ = _ln(jnp.where(j == 0, c_ref[...], x_ref[...]))
        o_ref[...] = (xhat * (1.0 + sc_ref[...]) + sh_ref[...]).astype(BF16)

    tab = pl.BlockSpec((None, 1, D), lambda b, j: (2 * b + jnp.minimum(j, 1), 0, 0))
    return pl.pallas_call(
        body, name="ln_mod_fwd", grid=(cfg.Bn, nbt),
        in_specs=[pl.BlockSpec((TB, D), lambda b, j: (b, 0)),
                  pl.BlockSpec((TB, D), lambda b, j: (b * nbl + jnp.maximum(j - 1, 0), 0)), tab, tab],
        out_specs=pl.BlockSpec((TB, D), lambda b, j: (b * nbt + j, 0)),
        out_shape=jax.ShapeDtypeStruct((cfg.NT, D), BF16),
        compiler_params=_cp(2),
    )(ctx2, x2, shift_tab, scale_tab)


GP = 8
NGB = NG // GP
HPB = GP * HPG


def _heads_to_front(x, d, gb, inverse=False):
    off = d * NH + gb * HPB
    return pltpu.roll(x, off if inverse else (DTW - off) % DTW, 1)


def _chunks_per_step(cfg):
    n = cfg.NT // CH
    return max(c for c in (4, 3, 2, 1) if n % c == 0)


def _dt_fwd(cfg, dt_raw, dt_bias, a_log):
    cps = _chunks_per_step(cfg)

    def body(raw_ref, bias_ref, alog_ref, dt_ref, dtg_ref, cumg_ref, cumT_ref):
        a = -jnp.exp(alog_ref[...])
        col = lax.broadcasted_iota(jnp.int32, (CH, DTW), 1)
        for c in range(cps):
            rows = slice(c * CH, (c + 1) * CH)
            dt = _softplus(raw_ref[rows, :] + bias_ref[...])
            dta = dt * a
            cf = _dot_exact_l(_tri(CH, False), dta)
            cr = _dot_exact_l(_tri(CH, True), dta)
            cum = jnp.where(col < NH, cf, cr)
            dt_ref[rows, :] = dt
            for d in range(2):
                for gb in range(NGB):
                    dtg_ref[d, gb, rows, :] = _heads_to_front(dt, d, gb)
                    cg = _heads_to_front(cum, d, gb)
                    cumg_ref[d, gb, rows, :] = cg
                    cumT_ref[d, gb, c] = cg.T

    blk = pl.BlockSpec((cps * CH, DTW), lambda i: (i, 0))
    row = pl.BlockSpec((1, DTW), lambda i: (0, 0))
    gblk = pl.BlockSpec((2, NGB, cps * CH, DTW), lambda i: (0, 0, i, 0))
    return pl.pallas_call(
        body, name="dt_fwd", grid=(cfg.NT // (cps * CH),),
        in_specs=[blk, row, row],
        out_specs=[blk, gblk, gblk, pl.BlockSpec((2, NGB, cps, DTW, CH), lambda i: (0, 0, i, 0, 0))],
        out_shape=[jax.ShapeDtypeStruct((cfg.NT, DTW), F32),
                   jax.ShapeDtypeStruct((2, NGB, cfg.NT, DTW), F32),
                   jax.ShapeDtypeStruct((2, NGB, cfg.NT, DTW), F32),
                   jax.ShapeDtypeStruct((2, NGB, cfg.NT // CH, DTW, CH), F32)],
        compiler_params=_cp(1),
    )(dt_raw, dt_bias, a_log)


def _dt_bwd(cfg, dAs, dxxs, dt_raw, dt, dt_bias, a_log):
    def body(dAf_ref, dAr_ref, dxf_ref, dxr_ref, raw_ref, dt_ref, bias_ref, alog_ref, o_ref, acc_ref):
        i = pl.program_id(0)

        @pl.when(i == 0)
        def _():
            acc_ref[...] = jnp.zeros_like(acc_ref)

        a = -jnp.exp(alog_ref[...])
        col = lax.broadcasted_iota(jnp.int32, (CH, DTW), 1)
        for c in range(cps):
            rows = slice(c * CH, (c + 1) * CH)
            dA_v = jnp.zeros((CH, DTW), F32)
            dxx_v = jnp.zeros((CH, DTW), F32)
            for d, (ra, rx) in enumerate(((dAf_ref, dxf_ref), (dAr_ref, dxr_ref))):
                for gb in range(NGB):
                    dA_v = dA_v + _heads_to_front(ra[gb, rows, :], d, gb, inverse=True)
                    dxx_v = dxx_v + _heads_to_front(rx[gb, rows, :], d, gb, inverse=True)
            ddta = jnp.where(col < NH, _dot_exact_l(_tri(CH, True), dA_v), _dot_exact_l(_tri(CH, False), dA_v))
            dtv = dt_ref[rows, :]
            ddt = ddta * a + dxx_v
            draw = ddt * _sigmoid(raw_ref[rows, :] + bias_ref[...])
            draw = jnp.where(col < 2 * NH, draw, 0.0)
            o_ref[rows, :] = draw
            da = jnp.sum(ddta * dtv, axis=0, keepdims=True) * a
            da = jnp.where(col[:1] < 2 * NH, da, 0.0)
            acc_ref[0:1, :] += da
            acc_ref[1:2, :] += jnp.sum(draw, axis=0, keepdims=True)

    cps = _chunks_per_step(cfg)
    blk = pl.BlockSpec((cps * CH, DTW), lambda i: (i, 0))
    row = pl.BlockSpec((1, DTW), lambda i: (0, 0))
    gblk = pl.BlockSpec((NGB, cps * CH, DTW), lambda i: (0, i, 0))
    return pl.pallas_call(
        body, name="dt_bwd", grid=(cfg.NT // (cps * CH),),
        in_specs=[gblk, gblk, gblk, gblk, blk, blk, row, row],
        out_specs=[blk, pl.BlockSpec((8, DTW), lambda i: (0, 0))],
        out_shape=[jax.ShapeDtypeStruct((cfg.NT, DTW), F32), jax.ShapeDtypeStruct((8, DTW), F32)],
        compiler_params=_cp(1),
    )(dAs[0], dAs[1], dxxs[0], dxxs[1], dt_raw, dt, dt_bias, a_log)


_TAPS = (2, 1, 0, -1)


def _conv_fwd(cfg, proj, conv_w, conv_b):
    TB, nbt = cfg.TB, cfg.nbt
    CB = CONVW // 2
    SUB = 256
    n_act = DI + 2 * NG * NS

    def body(u_ref, w_ref, b_ref, o_ref, sg_ref):
        i = pl.program_id(0)
        j = pl.program_id(1)
        R = jnp.where(i % nbt == 0, cfg.Tc, GRID_W)
        t = lax.broadcasted_iota(jnp.int32, (TB, SUB), 0)
        pos = jnp.bitwise_and(t, R - 1)
        keep = {s: jnp.where(jnp.logical_and(pos - s >= 0, pos - s < R), 1.0, 0.0) for s in (2, 1, -1)}
        def sub_tile(q, act):
            sl = slice(q * SUB, (q + 1) * SUB)
            u = u_ref[:, sl]
            pre = b_ref[:, sl] + w_ref[2:3, sl] * u
            for k in (0, 1, 3):
                pre = pre + w_ref[k:k + 1, sl] * (pltpu.roll(u, _TAPS[k] % TB, 0) * keep[_TAPS[k]])
            if act:
                s = _sigmoid(pre)
                o_ref[:, sl] = pre * s
                sg_ref[:, sl] = s * (1.0 + pre * (1.0 - s))
            else:
                o_ref[:, sl] = pre

        for q in range(CB // SUB):
            if q * SUB >= n_act - CB:
                pl.when(j == 0)(functools.partial(sub_tile, q, True))
                pl.when(j == 1)(functools.partial(sub_tile, q, False))
            else:
                sub_tile(q, True)

    blk = pl.BlockSpec((TB, CB), lambda i, j: (i, j))
    return pl.pallas_call(
        body, name="conv_fwd", grid=(cfg.NT // TB, CONVW // CB),
        in_specs=[blk, pl.BlockSpec((4, CB), lambda i, j: (0, j)), pl.BlockSpec((1, CB), lambda i, j: (0, j))],
        out_specs=[blk, blk],
        out_shape=[jax.ShapeDtypeStruct((cfg.NT, CONVW), F32)] * 2,
        compiler_params=_cp(2),
    )(proj, conv_w, conv_b)


_ANY = pl.BlockSpec(memory_space=pl.ANY)


def _conv_bwd(cfg, name, dproj, proj, conv_w, sgrad, addends, col0, width, skip=None, xch=None):
    TB, nbt, nbl = cfg.TB, cfg.nbt, cfg.nbl
    CB = 1024
    SUB = 256
    c0 = col0 // CB
    addends = list(addends) + ([] if sgrad is None else [sgrad])
    n_add = len(addends)

    def body(*refs):
        u_ref, w_ref = refs[1:3]
        add_refs = refs[3:3 + n_add]
        rest = refs[3 + n_add:]
        if sgrad is not None:
            add_refs, sg_ref = add_refs[:-1], add_refs[-1]
        if skip is not None:
            dy_ref, dv_ref = rest[:2]
            rest = rest[2:]
        o_ref, acc_ref = rest
        i = pl.program_id(1)

        @pl.when(i == 0)
        def _():
            acc_ref[...] = jnp.zeros_like(acc_ref)

        isctx = (i % nbt) == 0
        R = jnp.where(isctx, cfg.Tc, GRID_W)
        t = lax.broadcasted_iota(jnp.int32, (TB, SUB), 0)
        pos = jnp.bitwise_and(t, R - 1)
        keep = {s: jnp.where(jnp.logical_and(pos - s >= 0, pos - s < R), 1.0, 0.0) for s in (2, 1, -1, -2)}

        def shifted(v, s):
            return v if s == 0 else pltpu.roll(v, s % TB, 0) * keep[s]

        for q in range(CB // SUB):
            sl = slice(q * SUB, (q + 1) * SUB)
            u = u_ref[:, sl]
            us = [shifted(u, _TAPS[k]) for k in range(4)]
            g = add_refs[0][:, sl]
            for r in add_refs[1:]:
                g = g + r[:, sl]
            if skip is not None:
                g = g + jnp.where(isctx, 0.0, dv_ref[:, sl] * dy_ref[:, sl])
            if sgrad is not None:
                g = g * sg_ref[:, sl]
            dp = jnp.zeros_like(g)
            for k in range(4):
                acc_ref[k:k + 1, sl] += jnp.sum(g * us[k], axis=0, keepdims=True)
                dp = dp + w_ref[k:k + 1, sl] * shifted(g, -_TAPS[k])
            acc_ref[4:5, sl] += jnp.sum(g, axis=0, keepdims=True)
            o_ref[:, sl] = dp.astype(BF16)

    blk = pl.BlockSpec((TB, CB), lambda j, i: (i, j))
    wide = pl.BlockSpec((TB, CB), lambda j, i: (i, c0 + j))
    in_specs = [_ANY, wide, pl.BlockSpec((4, CB), lambda j, i: (0, c0 + j))]
    in_specs += [blk] * (n_add if sgrad is None else n_add - 1) + ([] if sgrad is None else [wide])
    args = [dproj, proj, conv_w] + addends
    if skip is not None:
        def lat(j, i):
            b = i // nbt
            return (b * nbl + jnp.maximum(i % nbt - 1, 0), j)
        in_specs += [pl.BlockSpec((TB, CB), lat), pl.BlockSpec((1, CB), lambda j, i: (0, j))]
        args += list(skip)
    return _hosted_call(
        body, xch, name=name, grid=(width // CB, cfg.NT // TB),
        in_specs=in_specs,
        out_specs=[pl.BlockSpec((TB, CB), lambda j, i: (i, c0 + j)), pl.BlockSpec((8, CB), lambda j, i: (0, j))],
        out_shape=[jax.ShapeDtypeStruct((cfg.NT, PM), BF16), jax.ShapeDtypeStruct((8, width), F32)],
        scratch_shapes=[], compiler_params=_cp(2), args=args, aliases={0: 0})


def _chunk_of_step(cfg, rev):
    nct, nlt = cfg.nct, cfg.nlt
    if not rev:
        return lambda s: s
    return lambda s: jnp.where(s < nct, nct - 1 - s, 2 * nct + nlt - 1 - s)


def _expand4(v, band, base):
    out = v[:, base + 3:base + 4]
    for h in (2, 1, 0):
        out = jnp.where(band == h, v[:, base + h:base + h + 1], out)
    return out


def _ssd_step_tiles(dt_ref, cum_ref, cumT_ref, rev):
    cum_t = cum_ref[...]
    last = 0 if rev else CH - 1
    llast = cum_t[last:last + 1, :]
    return (dt_ref[...], cum_t, cumT_ref[...], llast, jnp.exp(llast), last)


def _ssd_common(gi, x_ref, b_ref, c_ref, tiles, rev, intra=True):
    dt_t, cum_t, cumT_t, llast, elast, last = tiles
    base = gi * HPG
    xh = x_ref[:, gi * HPG * HD:(gi + 1) * HPG * HD]
    Bm = b_ref[:, gi * NS:(gi + 1) * NS].astype(BF16)
    band = lax.broadcasted_iota(jnp.int32, (CH, HPG * HD), 1) // HD
    cbs = [jnp.broadcast_to(cum_t[:, base + h:base + h + 1], (CH, CH)) for h in range(HPG)]
    Cm = G = decs = None
    if intra:
        Cm = c_ref[:, gi * NS:(gi + 1) * NS].astype(BF16)
        G = _dot_nt(Cm, Bm)
        ii = lax.broadcasted_iota(jnp.int32, (CH, CH), 0)
        jj = lax.broadcasted_iota(jnp.int32, (CH, CH), 1)
        mask = (jj >= ii) if rev else (jj <= ii)
        decs = [jnp.exp(jnp.where(mask, cbs[h] - cumT_t[base + h:base + h + 1, :], -1e30)) for h in range(HPG)]
    cum_exp = jnp.concatenate([cbs[3], cbs[3]], axis=1)
    ll_exp = llast[:, base + 3:base + 4]
    for h in (2, 1, 0):
        cum_exp = jnp.where(band == h, jnp.concatenate([cbs[h], cbs[h]], axis=1), cum_exp)
        ll_exp = jnp.where(band[:1] == h, llast[:, base + h:base + h + 1], ll_exp)
    ecum = jnp.exp(cum_exp) if intra else None
    e_exp = jnp.exp(ll_exp - cum_exp)
    dt_exp = _expand4(dt_t, band, base)
    X = xh * dt_exp
    rb = lax.broadcasted_iota(jnp.int32, (HPG * HD, NS), 0) // HD
    dec_rows = elast[:, base + 3:base + 4]
    for h in (2, 1, 0):
        dec_rows = jnp.where(rb == h, elast[:, base + h:base + h + 1], dec_rows)
    return xh, Bm, Cm, band, e_exp, ecum, dt_exp, X, G, decs, elast, dec_rows, last


def _ssd_specs(cfg, rev):
    nch = cfg.nch
    cmap = _chunk_of_step(cfg, rev)
    d = 1 if rev else 0

    def make(stepmap):
        def row(b, g, sp):
            return b * nch + cmap(stepmap(sp))
        bo, co = DI // (GP * NS), (DI + NG * NS) // (GP * NS)
        return [
            pl.BlockSpec((CH, GP * HPG * HD), lambda b, g, sp: (row(b, g, sp), g)),
            pl.BlockSpec((CH, GP * NS), lambda b, g, sp: (row(b, g, sp), bo + g)),
            pl.BlockSpec((CH, GP * NS), lambda b, g, sp: (row(b, g, sp), co + g)),
            pl.BlockSpec((None, None, CH, DTW), lambda b, g, sp: (d, g, row(b, g, sp), 0)),
            pl.BlockSpec((None, None, CH, DTW), lambda b, g, sp: (d, g, row(b, g, sp), 0)),
            pl.BlockSpec((None, None, None, DTW, CH), lambda b, g, sp: (d, g, row(b, g, sp), 0, 0)),
        ], row
    return make


def _ssd_fwd(cfg, act, dtg, cumg, cumTg, rev, y_other=None, dvec=None):
    nch = cfg.nch
    in_specs, row = _ssd_specs(cfg, rev)(lambda sp: sp)
    total = y_other is not None

    def body(*refs):
        x_ref, b_ref, c_ref, dt_ref, cum_ref, cumT_ref = refs[:6]
        if total:
            yo_ref, dv_ref = refs[6:8]
        y_ref, hs_ref, h_scr = refs[-3:]
        s = pl.program_id(2)

        @pl.when(s == 0)
        def _():
            h_scr[...] = jnp.zeros_like(h_scr)

        def step(intra):
            tiles = _ssd_step_tiles(dt_ref, cum_ref, cumT_ref, rev)
            for gi in range(GP):
                xh, Bm, Cm, band, e_exp, ecum, dt_exp, X, G, decs, elast, dec_rows, last = _ssd_common(
                    gi, x_ref, b_ref, c_ref, tiles, rev, intra)
                H = h_scr[gi]
                if intra:
                    Mcat = jnp.concatenate([(G * decs[h]).astype(BF16) for h in range(HPG)], axis=1)
                    Xbd = jnp.concatenate([jnp.where(band == h, X, 0.0).astype(BF16) for h in range(HPG)], axis=0)
                    xsl = slice(gi * HPG * HD, (gi + 1) * HPG * HD)
                    Y = ecum * _dot_nt(Cm, H.astype(BF16)) + _dot(Mcat, Xbd)
                    if total:
                        Y = Y + yo_ref[:, xsl] + dv_ref[:, xsl] * xh
                    y_ref[:, xsl] = Y
                hs_ref[gi] = H
                S = _dot_tn((e_exp * X).astype(BF16), Bm)
                h_scr[gi] = dec_rows * H + S

        isctx = cmap(s) < cfg.nct

        @pl.when(isctx)
        def _():
            step(False)

        @pl.when(jnp.logical_not(isctx))
        def _():
            step(True)

    cmap = _chunk_of_step(cfg, rev)
    yblk = pl.BlockSpec((CH, GP * HPG * HD), lambda b, g, s: (row(b, g, s), g))
    args = [act, act, act, dtg, cumg, cumTg]
    if total:
        in_specs = in_specs + [yblk, pl.BlockSpec((1, GP * HPG * HD), lambda b, g, s: (0, g))]
        args += [y_other, dvec]
    return pl.pallas_call(
        body, name="ssd_fwd_rev" if rev else "ssd_fwd", grid=(cfg.Bn, NG // GP, nch),
        in_specs=in_specs,
        out_specs=[yblk, pl.BlockSpec((None, GP, None, HPG * HD, NS), lambda b, g, s: (b, g, s, 0, 0))],
        out_shape=[jax.ShapeDtypeStruct((cfg.NT, DI), F32),
                   jax.ShapeDtypeStruct((cfg.Bn, NG, nch, HPG * HD, NS), F32)],
        scratch_shapes=[pltpu.VMEM((GP, HPG * HD, NS), F32)],
        compiler_params=_cp(3),
    )(*args)


def _ssd_bwd(cfg, act, dtg, cumg, cumTg, hs, dy, rev, xch=None):
    nch, nct, nlt = cfg.nch, cfg.nct, cfg.nlt
    cmap = _chunk_of_step(cfg, rev)
    in_specs, row = _ssd_specs(cfg, rev)(lambda sp: nch - 1 - sp)

    def lat_row(b, g, sp):
        c = cmap(nch - 1 - sp)
        return b * nlt + jnp.maximum(c - nct, 0)

    def body(x_ref, b_ref, c_ref, dt_ref, cum_ref, cumT_ref, dy_ref, hs_ref,
             dxh_ref, dB_ref, dC_ref, dA_ref, dxx_ref, dh_scr):
        sp = pl.program_id(2)

        @pl.when(sp == 0)
        def _():
            dh_scr[...] = jnp.zeros_like(dh_scr)

        def step(intra):
            tiles = _ssd_step_tiles(dt_ref, cum_ref, cumT_ref, rev)
            dA_t = jnp.zeros((CH, DTW), F32)
            dAT_t = jnp.zeros((DTW, CH), F32)
            dxx_t = jnp.zeros((CH, DTW), F32)
            for gi in range(GP):
                dA_g, dAT_g, dxx_g = group_bwd(gi, intra, tiles, x_ref, b_ref, c_ref, dy_ref, hs_ref,
                                               dxh_ref, dB_ref, dC_ref, dh_scr)
                dA_t = dA_t + dA_g
                dxx_t = dxx_t + dxx_g
                if intra:
                    dAT_t = dAT_t + dAT_g
            dA_ref[...] = dA_t - dAT_t.T if intra else dA_t
            dxx_ref[...] = dxx_t

        isctx = cmap(nch - 1 - sp) < nct

        @pl.when(isctx)
        def _():
            step(False)

        @pl.when(jnp.logical_not(isctx))
        def _():
            step(True)

    def group_bwd(gi, intra, tiles, x_ref, b_ref, c_ref, dy_ref, hs_ref, dxh_ref, dB_ref, dC_ref, dh_scr):
        xsl = slice(gi * HPG * HD, (gi + 1) * HPG * HD)
        nsl = slice(gi * NS, (gi + 1) * NS)
        base = gi * HPG
        xh, Bm, Cm, band, e_exp, ecum, dt_exp, X, G, decs, elast, dec_rows, last = _ssd_common(
            gi, x_ref, b_ref, c_ref, tiles, rev, intra)
        H = hs_ref[gi]
        dHn = dh_scr[gi]
        dHnb = dHn.astype(BF16)
        BdH = _dot_nt(Bm, dHnb)
        dX = e_exp * BdH
        eX = e_exp * X
        lanei = lax.broadcasted_iota(jnp.int32, (CH, DTW), 1)
        dA = jnp.zeros((CH, DTW), F32)
        dAT = None
        pb = lax.broadcasted_iota(jnp.int32, (HPG * HD, NS), 0) // HD
        pl_ = lax.broadcasted_iota(jnp.int32, (HPG * HD, NS), 1)
        E = jnp.where(pb + base == pl_, 1.0, 0.0).astype(BF16)
        if intra:
            dY = dy_ref[:, xsl]
            Hb = H.astype(BF16)
            dYs = ecum * dY
            dYsb = dYs.astype(BF16)
            Ys = ecum * _dot_nt(Cm, Hb)
            dG = jnp.zeros((CH, CH), F32)
            subi = lax.broadcasted_iota(jnp.int32, (DTW, CH), 0)
            dAT = jnp.zeros((DTW, CH), F32)
            Xbd = jnp.concatenate([jnp.where(band == h, X, 0.0).astype(BF16) for h in range(HPG)], axis=0)
            dYbd = jnp.concatenate([jnp.where(band == h, dY, 0.0).astype(BF16) for h in range(HPG)], axis=0)
            dMcat = _dot_nt(dY.astype(BF16), Xbd)
            Ms = []
            for h in range(HPG):
                M = G * decs[h]
                dM = dMcat[:, h * CH:(h + 1) * CH]
                W = dM * M
                dG = dG + dM * decs[h]
                Ms.append(M.astype(BF16))
                dA = dA + jnp.where(lanei == base + h, jnp.sum(W, axis=1, keepdims=True), 0.0)
                dAT = dAT + jnp.where(subi == base + h, jnp.sum(W, axis=0, keepdims=True), 0.0)
            dX = dX + _dot_tn(jnp.concatenate(Ms, axis=0), dYbd)
            dGb = dG.astype(BF16)
            dC_ref[:, nsl] = _dot(dGb, Bm) + _dot(dYsb, Hb)
            dB_ref[:, nsl] = _dot_tn(dGb, Cm) + _dot(eX.astype(BF16), dHnb)
            dh_scr[gi] = dec_rows * dHn + _dot_tn(dYsb, Cm)
            dA = dA + _dot_hilo_r(dY * Ys, E)
        else:
            dC_ref[:, nsl] = jnp.zeros((CH, NS), F32)
            dB_ref[:, nsl] = _dot(eX.astype(BF16), dHnb)
            dh_scr[gi] = dec_rows * dHn
        q = _dot_hilo_r(eX * BdH, E)
        r = jnp.sum(dHn * H, axis=1, keepdims=True)
        lane1 = lax.broadcasted_iota(jnp.int32, (1, DTW), 1)
        hdot = jnp.zeros((1, DTW), F32)
        for h in range(HPG):
            hv = jnp.sum(r[h * HD:(h + 1) * HD, :], axis=0, keepdims=True)
            hdot = hdot + jnp.where(lane1 == base + h, hv, 0.0)
        dllast = jnp.sum(q, axis=0, keepdims=True) + elast * hdot
        rowi = lax.broadcasted_iota(jnp.int32, (CH, DTW), 0)
        dxh_ref[:, xsl] = dX * dt_exp
        return dA - q + jnp.where(rowi == last, dllast, 0.0), dAT, _dot_hilo_r(dX * xh, E)

    small = pl.BlockSpec((None, CH, DTW), lambda b, g, sp: (g, row(b, g, sp), 0))
    return _hosted_call(
        body, xch, name="ssd_bwd_rev" if rev else "ssd_bwd", grid=(cfg.Bn, NG // GP, nch),
        in_specs=in_specs + [
            pl.BlockSpec((CH, GP * HPG * HD), lambda b, g, sp: (lat_row(b, g, sp), g)),
            pl.BlockSpec((None, GP, None, HPG * HD, NS), lambda b, g, sp: (b, g, nch - 1 - sp, 0, 0))],
        out_specs=[pl.BlockSpec((CH, GP * HPG * HD), lambda b, g, sp: (row(b, g, sp), g)),
                   pl.BlockSpec((CH, GP * NS), lambda b, g, sp: (row(b, g, sp), g)),
                   pl.BlockSpec((CH, GP * NS), lambda b, g, sp: (row(b, g, sp), g)),
                   small, small],
        out_shape=[jax.ShapeDtypeStruct((cfg.NT, DI), F32),
                   jax.ShapeDtypeStruct((cfg.NT, NG * NS), F32),
                   jax.ShapeDtypeStruct((cfg.NT, NG * NS), F32),
                   jax.ShapeDtypeStruct((NGB, cfg.NT, DTW), F32),
                   jax.ShapeDtypeStruct((NGB, cfg.NT, DTW), F32)],
        scratch_shapes=[pltpu.VMEM((GP, HPG * HD, NS), F32)],
        compiler_params=_cp(3), args=(act, act, act, dtg, cumg, cumTg, dy, hs))


def _shift_rows(v, s, fill, toward_later, rowi):
    n = v.shape[0]
    if toward_later:
        return jnp.where(rowi >= s, pltpu.roll(v, s, 0), fill)
    return jnp.where(rowi < n - s, pltpu.roll(v, n - s, 0), fill)


def _chunk_scan(a, b, carry, later):
    nt = a.shape[0] // 8
    rowi = lax.broadcasted_iota(jnp.int32, (8, a.shape[1]), 0)
    outs = [None] * nt
    for r in (range(nt) if later else range(nt - 1, -1, -1)):
        av = a[r * 8:(r + 1) * 8]
        bv = b[r * 8:(r + 1) * 8]
        for sh in (1, 2, 4):
            a_p = _shift_rows(av, sh, 1.0, later, rowi)
            b_p = _shift_rows(bv, sh, 0.0, later, rowi)
            bv = av * b_p + bv
            av = av * a_p
        h = bv + av * carry
        outs[r] = h
        carry = h[7:8] if later else h[0:1]
    return jnp.concatenate(outs, axis=0), carry


def _lru_gates(u, wa_ref, wi_ref, ba_ref, bi_ref, lam_ref):
    rs, is_ = [], []
    for k in range(LB):
        uk = u[:, k * LBW:(k + 1) * LBW].astype(BF16)
        rs.append(_dot(uk, wa_ref[k].astype(BF16)))
        is_.append(_dot(uk, wi_ref[k].astype(BF16)))
    r = 1.0 / (1.0 + jnp.exp(-(jnp.concatenate(rs, axis=1) + ba_ref[...])))
    ig = _sigmoid(jnp.concatenate(is_, axis=1) + bi_ref[...])
    sp = _softplus(-lam_ref[...])
    la = -LRU_C * r * sp
    a = jnp.exp(la)
    q = (1.0 + a * a) * jnp.tanh(-la)
    return r, ig, sp, la, a, jnp.sqrt(q), lax.rsqrt(q)


def _lru_w_specs(d):
    return [pl.BlockSpec((None, LB, LBW, LBW), lambda b, s: (d, 0, 0, 0)),
            pl.BlockSpec((None, LB, LBW, LBW), lambda b, s: (d, 0, 0, 0)),
            pl.BlockSpec((None, 1, LW), lambda b, s: (d, 0, 0)),
            pl.BlockSpec((None, 1, LW), lambda b, s: (d, 0, 0)),
            pl.BlockSpec((None, 1, LW), lambda b, s: (d, 0, 0))]


def _lru_block_of_step(cfg, rev):
    nbl = cfg.nbl
    if not rev:
        return lambda s: s
    return lambda s: jnp.where(s < 1, 0, 1 + nbl - s)


def _lru_fwd(cfg, act, wa, wi, ba, bi, lam, rev):
    nch, CH = cfg.nbt, cfg.TB
    cmap = _lru_block_of_step(cfg, rev)
    d = 1 if rev else 0
    ucol = (DI + 2 * NG * NS) // LW

    def body(u_ref, wa_ref, wi_ref, ba_ref, bi_ref, lam_ref, h_ref, c_scr):
        s = pl.program_id(1)

        @pl.when(s == 0)
        def _():
            c_scr[...] = jnp.zeros_like(c_scr)

        u = u_ref[...]
        r, ig, sp, la, a, g, _ = _lru_gates(u, wa_ref, wi_ref, ba_ref, bi_ref, lam_ref)
        h, carry = _chunk_scan(a, g * ig * u, c_scr[0:1, :], not rev)
        h_ref[...] = h
        c_scr[0:1, :] = carry

    return pl.pallas_call(
        body, name="lru_fwd_rev" if rev else "lru_fwd", grid=(cfg.Bn, nch),
        in_specs=[pl.BlockSpec((CH, LW), lambda b, s: (b * nch + cmap(s), ucol))] + _lru_w_specs(d),
        out_specs=pl.BlockSpec((CH, LW), lambda b, s: (b * nch + cmap(s), 0)),
        out_shape=jax.ShapeDtypeStruct((cfg.NT, LW), F32),
        scratch_shapes=[pltpu.VMEM((8, LW), F32)],
        compiler_params=_cp(2),
    )(act, wa, wi, ba, bi, lam)


def _lru_bwd(cfg, act, wa, wi, ba, bi, lam, hd, dyl, rev):
    nch, nct, nlt, CH = cfg.nbt, 1, cfg.nbl, cfg.TB
    cmap = _lru_block_of_step(cfg, rev)
    d = 1 if rev else 0
    ucol = (DI + 2 * NG * NS) // LW

    def srow(b, sp):
        return b * nch + cmap(nch - 1 - sp)

    def prev_rows(b, sp):
        s = nch - 1 - sp
        cp = cmap(jnp.maximum(s - 1, 0))
        base = (b * nch + cp) * (CH // 8)
        return base + (0 if rev else CH // 8 - 1)

    def lat_row(b, sp):
        c = cmap(nch - 1 - sp)
        return b * nlt + jnp.maximum(c - nct, 0)

    def body(u_ref, wa_ref, wi_ref, ba_ref, bi_ref, lam_ref, h_ref, hp_ref, dy_ref,
             du_ref, dwa_ref, dwi_ref, vec_ref, c_scr):
        b = pl.program_id(0)
        sp_id = pl.program_id(1)
        s = nch - 1 - sp_id

        @pl.when(sp_id == 0)
        def _():
            c_scr[...] = jnp.zeros_like(c_scr)

        @pl.when(jnp.logical_and(b == 0, sp_id == 0))
        def _():
            dwa_ref[...] = jnp.zeros_like(dwa_ref)
            dwi_ref[...] = jnp.zeros_like(dwi_ref)
            vec_ref[...] = jnp.zeros_like(vec_ref)

        c = cmap(s)
        u = u_ref[...]
        r, ig, spl, la, a, g, ginv = _lru_gates(u, wa_ref, wi_ref, ba_ref, bi_ref, lam_ref)
        dh = jnp.where(c < nct, 0.0, dy_ref[...])
        rowi = lax.broadcasted_iota(jnp.int32, (CH, LW), 0)
        lamv, _ = _chunk_scan(_shift_rows(a, 1, 1.0, rev, rowi), dh, c_scr[0:1, :], rev)
        first = CH - 1 if rev else 0
        c_scr[0:1, :] = (a * lamv)[first:first + 1, :]
        hprow = hp_ref[...][(0 if rev else 7):(1 if rev else 8), :]
        hprow = jnp.where(s > 0, hprow, 0.0)
        h_prev = _shift_rows(h_ref[...], 1, hprow, not rev, rowi)
        da = lamv * h_prev
        db = lamv
        iu = ig * u
        dla = da * a - db * iu * (a * a) * ginv
        dr = dla * (-LRU_C * spl)
        di = db * g * u
        du = db * g * ig
        drp = dr * r * (1.0 - r)
        dip = di * ig * (1.0 - ig)
        dus = []
        for k in range(LB):
            sl = slice(k * LBW, (k + 1) * LBW)
            drk = drp[:, sl].astype(BF16)
            dik = dip[:, sl].astype(BF16)
            uk = u[:, sl].astype(BF16)
            dus.append(_dot_nt(drk, wa_ref[k].astype(BF16)) + _dot_nt(dik, wi_ref[k].astype(BF16)))
            dwa_ref[k] += _dot_tn(uk, drk)
            dwi_ref[k] += _dot_tn(uk, dik)
        du_ref[...] = du + jnp.concatenate(dus, axis=1)
        vec_ref[0:1, :] += jnp.sum(drp, axis=0, keepdims=True)
        vec_ref[1:2, :] += jnp.sum(dip, axis=0, keepdims=True)
        dsp = jnp.sum(dla * (-LRU_C * r), axis=0, keepdims=True)
        vec_ref[2:3, :] += dsp * (-_sigmoid(-lam_ref[...]))

    return pl.pallas_call(
        body, name="lru_bwd_rev" if rev else "lru_bwd", grid=(cfg.Bn, nch),
        in_specs=[pl.BlockSpec((CH, LW), lambda b, sp: (srow(b, sp), ucol))] + _lru_w_specs(d) + [
            pl.BlockSpec((CH, LW), lambda b, sp: (srow(b, sp), 0)),
            pl.BlockSpec((8, LW), lambda b, sp: (prev_rows(b, sp), 0)),
            pl.BlockSpec((CH, LW), lambda b, sp: (lat_row(b, sp), 0))],
        out_specs=[pl.BlockSpec((CH, LW), lambda b, sp: (srow(b, sp), 0)),
                   pl.BlockSpec((LB, LBW, LBW), lambda b, sp: (0, 0, 0)),
                   pl.BlockSpec((LB, LBW, LBW), lambda b, sp: (0, 0, 0)),
                   pl.BlockSpec((8, LW), lambda b, sp: (0, 0))],
        out_shape=[jax.ShapeDtypeStruct((cfg.NT, LW), F32),
                   jax.ShapeDtypeStruct((LB, LBW, LBW), F32),
                   jax.ShapeDtypeStruct((LB, LBW, LBW), F32),
                   jax.ShapeDtypeStruct((8, LW), F32)],
        scratch_shapes=[pltpu.VMEM((8, LW), F32)],
        compiler_params=_cp(2),
    )(act, wa, wi, ba, bi, lam, hd, hd, dyl)


HB = 1024


def _post_ssd_fwd(cfg, y, proj, norm_w):
    TB, nbt, nbl = cfg.TB, cfg.nbt, cfg.nbl
    zc = CONVW // HB

    def body(y_ref, z_ref, w_ref, o_ref):
        u = y_ref[...] * _silu(z_ref[...])
        for gi in range(HB // (DI // NG)):
            sl = slice(gi * 256, (gi + 1) * 256)
            ug = u[:, sl]
            rs = lax.rsqrt(jnp.mean(ug * ug, axis=1, keepdims=True) + RMS_EPS)
            o_ref[:, sl] = (ug * rs * w_ref[:, sl]).astype(BF16)

    def st(b, j, cb):
        return (b * nbt + 1 + j, cb)
    return pl.pallas_call(
        body, name="post_ssd_fwd", grid=(cfg.Bn, nbl, DI // HB),
        in_specs=[pl.BlockSpec((TB, HB), st),
                  pl.BlockSpec((TB, HB), lambda b, j, cb: (b * nbt + 1 + j, zc + cb)),
                  pl.BlockSpec((1, HB), lambda b, j, cb: (0, cb))],
        out_specs=pl.BlockSpec((TB, HB), lambda b, j, cb: (b * nbl + j, cb)),
        out_shape=jax.ShapeDtypeStruct((cfg.N, DI), BF16),
        compiler_params=_cp(3),
    )(y, proj, norm_w)


def _post_ssd_bwd(cfg, dproj, dn, y, act, proj, norm_w):
    TB, nbt, nbl = cfg.TB, cfg.nbt, cfg.nbl
    zc = CONVW // HB

    def body(_, dn_ref, y_ref, xh_ref, z_ref, w_ref, dy_ref, dz_ref, acc_ref):
        b = pl.program_id(1)
        j = pl.program_id(2)

        @pl.when(jnp.logical_and(b == 0, j == 0))
        def _():
            acc_ref[...] = jnp.zeros_like(acc_ref)

        @pl.when(j == 0)
        def _():
            dz_ref[...] = jnp.zeros_like(dz_ref)

        @pl.when(j > 0)
        def _():
            latent(dn_ref, y_ref, xh_ref, z_ref, w_ref, dy_ref, dz_ref, acc_ref)

    def latent(dn_ref, y_ref, xh_ref, z_ref, w_ref, dy_ref, dz_ref, acc_ref):
        xh = xh_ref[...]
        z = z_ref[...]
        y = y_ref[...]
        sg = _sigmoid(z)
        sz = z * sg
        dsz = sg * (1.0 + z * (1.0 - sg))
        u = y * sz
        dout = dn_ref[...]
        for gi in range(HB // (DI // NG)):
            sl = slice(gi * 256, (gi + 1) * 256)
            ug0 = u[:, sl]
            rs = lax.rsqrt(jnp.mean(ug0 * ug0, axis=1, keepdims=True) + RMS_EPS)
            ug = ug0 * rs
            do = dout[:, sl]
            acc_ref[0:1, sl] += jnp.sum(do * ug, axis=0, keepdims=True)
            dug = do * w_ref[:, sl]
            du = rs * (dug - ug * jnp.mean(dug * ug, axis=1, keepdims=True))
            dy = du * sz[:, sl]
            dy_ref[:, sl] = dy
            dz_ref[:, sl] = (du * y[:, sl] * dsz[:, sl]).astype(BF16)
            acc_ref[1:2, sl] += jnp.sum(dy * xh[:, sl], axis=0, keepdims=True)

    def st(cb, b, j):
        return (b * nbt + j, cb)

    def la(cb, b, j):
        return (b * nbl + jnp.maximum(j - 1, 0), cb)
    return pl.pallas_call(
        body, name="post_ssd_bwd", grid=(DI // HB, cfg.Bn, nbt),
        in_specs=[_ANY, pl.BlockSpec((TB, HB), la), pl.BlockSpec((TB, HB), st), pl.BlockSpec((TB, HB), st),
                  pl.BlockSpec((TB, HB), lambda cb, b, j: (b * nbt + j, zc + cb)),
                  pl.BlockSpec((1, HB), lambda cb, b, j: (0, cb))],
        out_specs=[pl.BlockSpec((TB, HB), la),
                   pl.BlockSpec((TB, HB), lambda cb, b, j: (b * nbt + j, zc + cb)),
                   pl.BlockSpec((8, HB), lambda cb, b, j: (0, cb))],
        out_shape=[jax.ShapeDtypeStruct((cfg.N, DI), F32), jax.ShapeDtypeStruct((cfg.NT, PM), BF16),
                   jax.ShapeDtypeStruct((8, DI), F32)],
        input_output_aliases={0: 1},
        compiler_params=_cp(3),
    )(dproj, dn, y, act, proj, norm_w)


def _post_lru_fwd(cfg, hf, hb, proj):
    TB, nbt, nbl = cfg.TB, cfg.nbt, cfg.nbl
    gc = (CONVW + DI) // HB

    def body(hf_ref, hb_ref, g_ref, o_ref):
        o_ref[...] = ((hf_ref[...] + hb_ref[...]) * _gelu(g_ref[...])).astype(BF16)

    st = pl.BlockSpec((TB, HB), lambda b, j: (b * nbt + 1 + j, 0))
    return pl.pallas_call(
        body, name="post_lru_fwd", grid=(cfg.Bn, nbl),
        in_specs=[st, st, pl.BlockSpec((TB, HB), lambda b, j: (b * nbt + 1 + j, gc))],
        out_specs=pl.BlockSpec((TB, HB), lambda b, j: (b * nbl + j, 0)),
        out_shape=jax.ShapeDtypeStruct((cfg.N, LW), BF16),
        compiler_params=_cp(2),
    )(hf, hb, proj)


def _post_lru_bwd(cfg, dproj, dv, hf, hb, proj):
    TB, nbt, nbl = cfg.TB, cfg.nbt, cfg.nbl
    gc = (CONVW + DI) // HB

    def body(_, dv_ref, hf_ref, hb_ref, g_ref, dy_ref, dg_ref):
        j = pl.program_id(1)

        @pl.when(j == 0)
        def _():
            dg_ref[...] = jnp.zeros_like(dg_ref)

        @pl.when(j > 0)
        def _():
            gt = g_ref[...]
            dvv = dv_ref[...]
            gl, dgl = _gelu_and_grad(gt)
            dy_ref[...] = dvv * gl
            dg_ref[...] = (dvv * (hf_ref[...] + hb_ref[...]) * dgl).astype(BF16)

    st = pl.BlockSpec((TB, HB), lambda b, j: (b * nbt + j, 0))
    la = pl.BlockSpec((TB, HB), lambda b, j: (b * nbl + jnp.maximum(j - 1, 0), 0))
    gcol = pl.BlockSpec((TB, HB), lambda b, j: (b * nbt + j, gc))
    return pl.pallas_call(
        body, name="post_lru_bwd", grid=(cfg.Bn, nbt),
        in_specs=[_ANY, la, st, st, gcol],
        out_specs=[la, gcol],
        out_shape=[jax.ShapeDtypeStruct((cfg.N, LW), F32), jax.ShapeDtypeStruct((cfg.NT, PM), BF16)],
        input_output_aliases={0: 1},
        compiler_params=_cp(2),
    )(dproj, dv, hf, hb, proj)


def _merge_fwd(cfg, proj, b_gate, br_ssd, br_lru):
    TB, nbt, nbl = cfg.TB, cfg.nbt, cfg.nbl
    mc = (CONVW + DI + LW) // HB

    def body(ms_ref, ml_ref, bg_ref, bs_ref, bl_ref, o_ref):
        gs = _sigmoid(ms_ref[...] + bg_ref[:, :D])
        gl = _sigmoid(ml_ref[...] + bg_ref[:, D:])
        o_ref[...] = (gs * bs_ref[...] + gl * bl_ref[...]).astype(BF16)

    la = pl.BlockSpec((TB, D), lambda b, j: (b * nbl + j, 0))
    return pl.pallas_call(
        body, name="merge_fwd", grid=(cfg.Bn, nbl),
        in_specs=[pl.BlockSpec((TB, HB), lambda b, j: (b * nbt + 1 + j, mc)),
                  pl.BlockSpec((TB, HB), lambda b, j: (b * nbt + 1 + j, mc + 1)),
                  pl.BlockSpec((1, 2 * D), lambda b, j: (0, 0)), la, la],
        out_specs=la,
        out_shape=jax.ShapeDtypeStruct((cfg.N, D), BF16),
        compiler_params=_cp(2),
    )(proj, proj, b_gate, br_ssd, br_lru)


def _merge_bwd(cfg, dmix, proj, b_gate, br_ssd, br_lru):
    TB, nbt, nbl = cfg.TB, cfg.nbt, cfg.nbl
    mc = (CONVW + DI + LW) // HB

    def body(dm_ref, ms_ref, ml_ref, bg_ref, bs_ref, bl_ref, ds_ref, dl_ref, dmg_ref, acc_ref):
        b = pl.program_id(0)
        j = pl.program_id(1)

        @pl.when(jnp.logical_and(b == 0, j == 0))
        def _():
            acc_ref[...] = jnp.zeros_like(acc_ref)

        @pl.when(j == 0)
        def _():
            dmg_ref[...] = jnp.zeros_like(dmg_ref)

        @pl.when(j > 0)
        def _():
            latent(dm_ref, ms_ref, ml_ref, bg_ref, bs_ref, bl_ref, ds_ref, dl_ref, dmg_ref, acc_ref)

    def latent(dm_ref, ms_ref, ml_ref, bg_ref, bs_ref, bl_ref, ds_ref, dl_ref, dmg_ref, acc_ref):
        dm = dm_ref[...]
        gs = _sigmoid(ms_ref[...] + bg_ref[:, :D])
        gl = _sigmoid(ml_ref[...] + bg_ref[:, D:])
        ds_ref[...] = (dm * gs).astype(BF16)
        dl_ref[...] = (dm * gl).astype(BF16)
        dps = dm * bs_ref[...] * gs * (1.0 - gs)
        dpl = dm * bl_ref[...] * gl * (1.0 - gl)
        dmg_ref[:, :D] = dps.astype(BF16)
        dmg_ref[:, D:] = dpl.astype(BF16)
        acc_ref[0:1, :D] += jnp.sum(dps, axis=0, keepdims=True)
        acc_ref[0:1, D:] += jnp.sum(dpl, axis=0, keepdims=True)

    la = pl.BlockSpec((TB, D), lambda b, j: (b * nbl + jnp.maximum(j - 1, 0), 0))
    return pl.pallas_call(
        body, name="merge_bwd", grid=(cfg.Bn, nbt),
        in_specs=[la, pl.BlockSpec((TB, HB), lambda b, j: (b * nbt + j, mc)),
                  pl.BlockSpec((TB, HB), lambda b, j: (b * nbt + j, mc + 1)),
                  pl.BlockSpec((1, 2 * D), lambda b, j: (0, 0)), la, la],
        out_specs=[la, la, pl.BlockSpec((TB, 2 * D), lambda b, j: (b * nbt + j, mc // 2)),
                   pl.BlockSpec((8, 2 * D), lambda b, j: (0, 0))],
        out_shape=[jax.ShapeDtypeStruct((cfg.N, D), BF16), jax.ShapeDtypeStruct((cfg.N, D), BF16),
                   jax.ShapeDtypeStruct((cfg.NT, PM), BF16), jax.ShapeDtypeStruct((8, 2 * D), F32)],
        compiler_params=_cp(2),
    )(dmix, proj, proj, b_gate, br_ssd, br_lru)


def _resid1_fwd(cfg, x2, x_mix, gate1, shift2, scale2, ln1_g, ln1_b):
    TB, nbt, nbl = cfg.TB, cfg.nbt, cfg.nbl

    def body(x_ref, xm_ref, g1_ref, sh_ref, sc_ref, lg_ref, lb_ref, x1_ref, h2_ref):
        r1 = ALPHA * x_ref[...] + g1_ref[...] * xm_ref[...]
        xh, _ = _ln(r1)
        x1 = xh * lg_ref[...] + lb_ref[...]
        x1_ref[...] = x1
        xh2, _ = _ln(x1)
        h2_ref[...] = (xh2 * (1.0 + sc_ref[...]) + sh_ref[...]).astype(BF16)

    la = pl.BlockSpec((TB, D), lambda b, j: (b * nbl + j, 0))
    ex = pl.BlockSpec((None, 1, D), lambda b, j: (b, 0, 0))
    vec = pl.BlockSpec((1, D), lambda b, j: (0, 0))
    return pl.pallas_call(
        body, name="resid1_fwd", grid=(cfg.Bn, nbl),
        in_specs=[la, la, ex, ex, ex, vec, vec],
        out_specs=[la, la],
        out_shape=[jax.ShapeDtypeStruct((cfg.N, D), F32), jax.ShapeDtypeStruct((cfg.N, D), BF16)],
        compiler_params=_cp(2),
    )(x2, x_mix, gate1, shift2, scale2, ln1_g, ln1_b)


def _resid1_bwd(cfg, dh2, dx1p, x1, x2, x_mix, gate1, scale2, ln1_g):
    TB, nbt, nbl = cfg.TB, cfg.nbt, cfg.nbl

    def body(dh2_ref, dx1p_ref, x1_ref, x_ref, xm_ref, g1_ref, sc_ref, lg_ref,
             dxm_ref, dxp_ref, ex_ref, gl_ref):
        b = pl.program_id(0)
        j = pl.program_id(1)

        @pl.when(j == 0)
        def _():
            ex_ref[...] = jnp.zeros_like(ex_ref)

        @pl.when(jnp.logical_and(b == 0, j == 0))
        def _():
            gl_ref[...] = jnp.zeros_like(gl_ref)

        dh2 = dh2_ref[...]
        xh2, rs2 = _ln(x1_ref[...])
        ex_ref[0:1, :] += jnp.sum(dh2, axis=0, keepdims=True)
        ex_ref[1:2, :] += jnp.sum(dh2 * xh2, axis=0, keepdims=True)
        dx1 = dx1p_ref[...] + _ln_bwd(dh2 * (1.0 + sc_ref[...]), xh2, rs2)
        xm = xm_ref[...]
        g1 = g1_ref[...]
        r1 = ALPHA * x_ref[...] + g1 * xm
        xh1, rs1 = _ln(r1)
        gl_ref[0:1, :] += jnp.sum(dx1 * xh1, axis=0, keepdims=True)
        gl_ref[1:2, :] += jnp.sum(dx1, axis=0, keepdims=True)
        dr1 = _ln_bwd(dx1 * lg_ref[...], xh1, rs1)
        ex_ref[2:3, :] += jnp.sum(dr1 * xm, axis=0, keepdims=True)
        dxm_ref[...] = (dr1 * g1).astype(BF16)
        dxp_ref[...] = ALPHA * dr1

    la = pl.BlockSpec((TB, D), lambda b, j: (b * nbl + j, 0))
    ex = pl.BlockSpec((None, 1, D), lambda b, j: (b, 0, 0))
    vec = pl.BlockSpec((1, D), lambda b, j: (0, 0))
    return pl.pallas_call(
        body, name="resid1_bwd", grid=(cfg.Bn, nbl),
        in_specs=[la, la, la, la, la, ex, ex, vec],
        out_specs=[la, la, pl.BlockSpec((None, 8, D), lambda b, j: (b, 0, 0)),
                   pl.BlockSpec((8, D), lambda b, j: (0, 0))],
        out_shape=[jax.ShapeDtypeStruct((cfg.N, D), BF16), jax.ShapeDtypeStruct((cfg.N, D), F32),
                   jax.ShapeDtypeStruct((cfg.Bn, 8, D), F32), jax.ShapeDtypeStruct((8, D), F32)],
        compiler_params=_cp(2),
    )(dh2, dx1p, x1, x2, x_mix, gate1, scale2, ln1_g)


def _final_fwd_bwd(cfg, x1, mlp, b2, gate2, ln2_g, ln2_b, target):
    TB, nbl = cfg.TB, cfg.nbl

    def body(x1_ref, m_ref, b2_ref, g2_ref, lg_ref, lb_ref, t_ref, dm_ref, dx_ref, ex_ref, gl_ref):
        b = pl.program_id(0)
        j = pl.program_id(1)

        @pl.when(j == 0)
        def _():
            ex_ref[...] = jnp.zeros_like(ex_ref)

        @pl.when(jnp.logical_and(b == 0, j == 0))
        def _():
            gl_ref[...] = jnp.zeros_like(gl_ref)

        mv = m_ref[...] + b2_ref[...]
        g2 = g2_ref[...]
        r2 = ALPHA * x1_ref[...] + g2 * mv
        xh, rs = _ln(r2)
        lg = lg_ref[...]
        x2 = xh * lg + lb_ref[...]
        err = x2 - t_ref[...]
        ls = jnp.sum(jnp.sum(err * err, axis=1, keepdims=True), axis=0, keepdims=True) * (0.5 / D)
        gl_ref[3:4, :] += ls
        dx2 = err * (1.0 / D)
        gl_ref[0:1, :] += jnp.sum(dx2 * xh, axis=0, keepdims=True)
        gl_ref[1:2, :] += jnp.sum(dx2, axis=0, keepdims=True)
        dr2 = _ln_bwd(dx2 * lg, xh, rs)
        ex_ref[0:1, :] += jnp.sum(dr2 * mv, axis=0, keepdims=True)
        dmv = dr2 * g2
        gl_ref[2:3, :] += jnp.sum(dmv, axis=0, keepdims=True)
        dm_ref[...] = dmv.astype(BF16)
        dx_ref[...] = ALPHA * dr2

    la = pl.BlockSpec((TB, D), lambda b, j: (b * nbl + j, 0))
    ex = pl.BlockSpec((None, 1, D), lambda b, j: (b, 0, 0))
    vec = pl.BlockSpec((1, D), lambda b, j: (0, 0))
    return pl.pallas_call(
        body, name="final_fwd_bwd", grid=(cfg.Bn, nbl),
        in_specs=[la, la, vec, ex, vec, vec, la],
        out_specs=[la, la, pl.BlockSpec((None, 8, D), lambda b, j: (b, 0, 0)),
                   pl.BlockSpec((8, D), lambda b, j: (0, 0))],
        out_shape=[jax.ShapeDtypeStruct((cfg.N, D), BF16), jax.ShapeDtypeStruct((cfg.N, D), F32),
                   jax.ShapeDtypeStruct((cfg.Bn, 8, D), F32), jax.ShapeDtypeStruct((8, D), F32)],
        compiler_params=_cp(2),
    )(x1, mlp, b2, gate2, ln2_g, ln2_b, target)


def _ln_mod_bwd(cfg, dh_a, dh_b, ctx2, x2, scale_tab, dxp):
    TB, nbt, nbl = cfg.TB, cfg.nbt, cfg.nbl

    def body(da_ref, db_ref, c_ref, x_ref, sc_ref, dxp_ref, gx_ref, acc_ref):
        j = pl.program_id(1)

        @pl.when(j <= 1)
        def _():
            acc_ref[...] = jnp.zeros_like(acc_ref)

        dh = da_ref[...] + db_ref[...]
        xhat, rs = _ln(jnp.where(j == 0, c_ref[...], x_ref[...]))
        acc_ref[0:1, :] += jnp.sum(dh, axis=0, keepdims=True)
        acc_ref[1:2, :] += jnp.sum(dh * xhat, axis=0, keepdims=True)
        gx_ref[...] = dxp_ref[...] + _ln_bwd(dh * (1.0 + sc_ref[...]), xhat, rs)

    st = pl.BlockSpec((TB, D), lambda b, j: (b * nbt + j, 0))
    la = pl.BlockSpec((TB, D), lambda b, j: (b * nbl + jnp.maximum(j - 1, 0), 0))
    return pl.pallas_call(
        body, name="ln_mod_bwd", grid=(cfg.Bn, nbt),
        in_specs=[st, st, pl.BlockSpec((TB, D), lambda b, j: (b, 0)), la,
                  pl.BlockSpec((None, 1, D), lambda b, j: (2 * b + jnp.minimum(j, 1), 0, 0)), la],
        out_specs=[la, pl.BlockSpec((None, 8, D), lambda b, j: (2 * b + jnp.minimum(j, 1), 0, 0))],
        out_shape=[jax.ShapeDtypeStruct((cfg.N, D), F32), jax.ShapeDtypeStruct((2 * cfg.Bn, 8, D), F32)],
        compiler_params=_cp(2),
    )(dh_a, dh_b, ctx2, x2, scale_tab, dxp)


def _perm_w_in(w_in):
    w_main = jnp.concatenate([w_in[:, 0:3072], w_in[:, 4160:5184], w_in[:, 3136:4160], w_in[:, 5184:10304]], axis=1)
    w_dt = jnp.pad(w_in[:, 3072:3136], ((0, 0), (0, DTW - 2 * NH)))
    return w_main, w_dt


def _unperm_w_in(dw_main, dw_dt):
    return jnp.concatenate([dw_main[:, 0:3072], dw_dt[:, :2 * NH], dw_main[:, 4096:5120],
                            dw_main[:, 3072:4096], dw_main[:, 5120:]], axis=1)


def _unpack_rest(rest_all):
    out, off = {}, 0
    for n, shp, axis in _BIG[1:]:
        shard_shape = (shp[0] // NDEV, shp[1]) if axis == 0 else (shp[0], shp[1] // NDEV)
        r = math.prod(shard_shape) // 1024
        out[n] = _from_slots(rest_all[:, off:off + r, :].reshape((NDEV,) + shard_shape), axis)
        off += r
    return out


def _local_step(cfg, x, ctx, target, m, mc, W, rest_payload):
    Bn, T, Tc = cfg.Bn, cfg.T, cfg.Tc
    NT, N = cfg.NT, cfg.N
    ctx2, x2 = ctx.reshape(Bn * Tc, D), x.reshape(N, D)
    mch = [m[:, i * D:(i + 1) * D] for i in range(NMOD)]
    ctx_sh = jnp.broadcast_to(mc[None, :D], (Bn, D))
    ctx_sc = jnp.broadcast_to(mc[None, D:], (Bn, D))
    shift_tab = jnp.stack([ctx_sh, mch[0]], axis=1).reshape(2 * Bn, 1, D)
    scale_tab = jnp.stack([ctx_sc, mch[1]], axis=1).reshape(2 * Bn, 1, D)
    gate1 = mch[2].reshape(Bn, 1, D)
    shift2 = mch[3].reshape(Bn, 1, D)
    scale2 = mch[4].reshape(Bn, 1, D)
    gate2 = mch[5].reshape(Bn, 1, D)

    conv_w = jnp.concatenate([W["ssd_conv_w"], W["lru_conv_w"]], axis=1)
    conv_b = jnp.concatenate([W["ssd_conv_b"], W["lru_conv_b"]], axis=1)
    dt_bias = jnp.pad(W["ssd_dt_bias"].reshape(1, 2 * NH), ((0, 0), (0, DTW - 2 * NH)))
    a_log = jnp.pad(W["ssd_a_log"].reshape(1, 2 * NH), ((0, 0), (0, DTW - 2 * NH)))
    dvec = jnp.repeat(W["ssd_d"].reshape(NH), HD).reshape(1, DI)
    lba = W["lru_ba"].reshape(2, 1, LW)
    lbi = W["lru_bi"].reshape(2, 1, LW)
    llam = W["lru_lambda"].reshape(2, 1, LW)

    h = _ln_mod_fwd(cfg, ctx2, x2, shift_tab, scale_tab)
    proj, rest_all = _mm(h, W["w_main"], "nn", "mm_proj", tm=1024, tn=2048, tk=1024, xch=(rest_payload, True))
    W = dict(W, **_unpack_rest(rest_all))
    dt_raw = _mm(h, W["w_dt"], "nn", "mm_dt", tm=512, tn=DTW, tk=1024)
    dt, dtg, cumg, cumTg = _dt_fwd(cfg, dt_raw, dt_bias, a_log)
    act, sgrad = _conv_fwd(cfg, proj, conv_w, conv_b)
    y_f, hs_f = _ssd_fwd(cfg, act, dtg, cumg, cumTg, False)
    y, hs_b = _ssd_fwd(cfg, act, dtg, cumg, cumTg, True, y_other=y_f, dvec=dvec)
    hss = [hs_f, hs_b]
    hls = [_lru_fwd(cfg, act, W["lru_wa"], W["lru_wi"], lba, lbi, llam, rev) for rev in (False, True)]
    nssd = _post_ssd_fwd(cfg, y, proj, W["ssd_norm_w"])
    vlru = _post_lru_fwd(cfg, hls[0], hls[1], proj)
    br_ssd = _mm(nssd, W["w_br_ssd"], "nn", "mm_br_ssd", tm=1024, tn=1024, tk=1024)
    br_lru = _mm(vlru, W["w_br_lru"], "nn", "mm_br_lru", tm=1024, tn=1024, tk=1024)
    mix = _merge_fwd(cfg, proj, W["b_gate"], br_ssd, br_lru)
    x_mix = _mm(mix, W["w_out"], "nn", "mm_out", tm=1024, tn=1024, tk=1024)
    x1, h2 = _resid1_fwd(cfg, x2, x_mix, gate1, shift2, scale2, W["ln1_g"], W["ln1_b"])
    a1, actm = _mm_mlp1(h2, W["w_mlp1"], W["b_mlp1"])
    mlp = _mm(actm, W["w_mlp2"], "nn", "mm_mlp2", tm=1024, tn=1024, tk=2048)
    dmlp, dx1p, ex2, gl2 = _final_fwd_bwd(cfg, x1, mlp, W["b_mlp2"], gate2, W["ln2_g"], W["ln2_b"],
                                          target.reshape(N, D))

    g = {}
    g["ln2_g"], g["ln2_b"], g["b_mlp2"] = gl2[0:1], gl2[1:2], gl2[2:3]
    loss_partial = gl2[3, 0]
    gw = {}
    gw["w_mlp2"] = _mm(actm, dmlp, "tn", "mm_dw_mlp2", BF16, tm=1024, tn=1024, tk=1024)
    da1, accb1 = _mm_dact(dmlp, W["w_mlp2"], a1)
    g["b_mlp1"] = accb1[0:1]
    dh2 = _mm(da1, W["w_mlp1"], "nt", "mm_dh2", tm=1024, tn=1024, tk=2048)
    gw["w_mlp1"] = _mm(h2, da1, "tn", "mm_dw_mlp1", BF16, tm=1024, tn=1024, tk=1024)
    dx_mix, dxp, ex1, gl1 = _resid1_bwd(cfg, dh2, dx1p, x1, x2, x_mix, gate1, scale2, W["ln1_g"])
    g["ln1_g"], g["ln1_b"] = gl1[0:1], gl1[1:2]
    dmix = _mm(dx_mix, W["w_out"], "nt", "mm_dmix", tm=1024, tn=1024, tk=1024)
    gw["w_out"] = _mm(mix, dx_mix, "tn", "mm_dw_out", BF16, tm=1024, tn=1024, tk=1024)
    dbs, dbl, dproj, accg = _merge_bwd(cfg, dmix, proj, W["b_gate"], br_ssd, br_lru)
    g["b_gate"] = accg[0:1]
    dnssd = _mm(dbs, W["w_br_ssd"], "nt", "mm_dnssd", tm=1024, tn=1024, tk=1024)
    gw["w_br_ssd"] = _mm(nssd, dbs, "tn", "mm_dw_br_ssd", BF16, tm=1024, tn=1024, tk=1024)
    dvlru = _mm(dbl, W["w_br_lru"], "nt", "mm_dvlru", tm=1024, tn=1024, tk=1024)
    gw["w_br_lru"] = _mm(vlru, dbl, "tn", "mm_dw_br_lru", BF16, tm=1024, tn=1024, tk=1024)
    dy, dproj, accs = _post_ssd_bwd(cfg, dproj, dnssd, y, act, proj, W["ssd_norm_w"])
    g["ssd_norm_w"] = accs[0:1]
    dD_cols = accs[1:2]
    dyl, dproj = _post_lru_bwd(cfg, dproj, dvlru, hls[0], hls[1], proj)

    rest_slots = jnp.concatenate([_to_slots(gw[n], axis).reshape(NDEV, -1, 1024) for n, _, axis in _BIG[1:]], axis=1)
    xres = {}
    dxh, dBs, dCs, dAs, dxxs, dus = [], [], [], [], [], []
    dwas, dwis, lvecs = [], [], []
    for i, rev in enumerate((False, True)):
        if i == 0:
            o, xres["rs_rest"] = _ssd_bwd(cfg, act, dtg, cumg, cumTg, hss[i], dy, rev, xch=(rest_slots, False))
        else:
            o = _ssd_bwd(cfg, act, dtg, cumg, cumTg, hss[i], dy, rev)
        dxh.append(o[0]); dBs.append(o[1]); dCs.append(o[2]); dAs.append(o[3]); dxxs.append(o[4])
        du, dwa, dwi, lv = _lru_bwd(cfg, act, W["lru_wa"], W["lru_wi"], lba, lbi, llam, hls[i], dyl, rev)
        dus.append(du); dwas.append(dwa); dwis.append(dwi); lvecs.append(lv)
    lru_payload = jnp.stack([jnp.stack(dwas), jnp.stack(dwis)]).reshape(-1, 1024)
    g["lru_ba"] = jnp.stack([lvecs[0][0], lvecs[1][0]])
    g["lru_bi"] = jnp.stack([lvecs[0][1], lvecs[1][1]])
    g["lru_lambda"] = jnp.stack([lvecs[0][2], lvecs[1][2]])

    ddt_raw, accdt = _dt_bwd(cfg, dAs, dxxs, dt_raw, dt, dt_bias, a_log)
    g["ssd_a_log"] = accdt[0, :2 * NH].reshape(2, NH)
    g["ssd_dt_bias"] = accdt[1, :2 * NH].reshape(2, NH)

    (dproj, accx), xres["ag_lru"] = _conv_bwd(cfg, "conv_bwd_x", dproj, proj, conv_w, sgrad, [dxh[0], dxh[1]], 0, DI,
                                              skip=(dy, dvec), xch=(lru_payload, True))
    dproj, accB = _conv_bwd(cfg, "conv_bwd_b", dproj, proj, conv_w, sgrad, [dBs[0], dBs[1]], DI, NG * NS)
    dproj, accC = _conv_bwd(cfg, "conv_bwd_c", dproj, proj, conv_w, sgrad, [dCs[0], dCs[1]], DI + NG * NS, NG * NS)
    dproj, accl = _conv_bwd(cfg, "conv_bwd_lru", dproj, proj, conv_w, None, [dus[0], dus[1]], DI + 2 * NG * NS, LW)
    accssd = jnp.concatenate([accx, accB, accC], axis=1)
    g["ssd_conv_w"], g["ssd_conv_b"] = accssd[0:4], accssd[4:5]
    g["lru_conv_w"], g["lru_conv_b"] = accl[0:4], accl[4:5]
    dw_main = _mm(h, dproj, "tn", "mm_dw_main", BF16, tm=1024, tn=2048, tk=1024)
    dw_dt = _mm(h, ddt_raw, "tn", "mm_dw_dt", BF16, tm=1024, tn=DTW, tk=512)
    w_in_slots = _to_slots(_unperm_w_in(dw_main, dw_dt), 1)
    dh_a, xres["rs_w_in"] = _mm(dproj, W["w_main"], "nt", "mm_dh_main", tm=1024, tn=1024, tk=2048,
                                xch=(w_in_slots, False))
    dh_b = _mm(ddt_raw, W["w_dt"], "nt", "mm_dh_dt", tm=512, tn=1024, tk=DTW)
    grad_x, acct = _ln_mod_bwd(cfg, dh_a, dh_b, ctx2, x2, scale_tab, dxp)
    acct = acct.reshape(Bn, 2, 8, D)
    dm = jnp.concatenate([acct[:, 1, 0], acct[:, 1, 1], ex1[:, 2], ex1[:, 0], ex1[:, 1], ex2[:, 0]], axis=1)
    dmc = jnp.concatenate([acct[:, 0, 0], acct[:, 0, 1]], axis=1)
    g["ssd_d_cols"] = dD_cols
    return loss_partial, grad_x.reshape(Bn, T, D), g, dm, dmc, xres


MESH = pl.DeviceIdType.MESH
_HBM = pl.BlockSpec(memory_space=pltpu.HBM)


def _me():
    return 4 * lax.axis_index("x") + 2 * lax.axis_index("y") + lax.axis_index("c")


def _peer(k):
    px = (lax.axis_index("x") + ((k >> 2) & 1)) % 2
    py = (lax.axis_index("y") + ((k >> 1) & 1)) % 2
    pc = (lax.axis_index("c") + (k & 1)) % 2
    return (px, py, pc), 4 * px + 2 * py + pc


def _xchg_copies(x_ref, o_ref, send_sems, recv_sems, loc_sem, gather):
    me = _me()
    src_me = x_ref if gather else x_ref.at[me]
    loc = pltpu.make_async_copy(src_me, o_ref.at[me], loc_sem)
    sends, recvs = [], []
    for k in range(1, NDEV):
        peer, pid = _peer(k)
        sends.append(pltpu.make_async_remote_copy(
            src_ref=x_ref if gather else x_ref.at[pid], dst_ref=o_ref.at[me],
            send_sem=send_sems.at[k - 1], recv_sem=recv_sems.at[k - 1],
            device_id=peer, device_id_type=MESH))
        recvs.append(pltpu.make_async_remote_copy(
            src_ref=src_me, dst_ref=o_ref.at[pid],
            send_sem=send_sems.at[k - 1], recv_sem=recv_sems.at[k - 1],
            device_id=peer, device_id_type=MESH))
    return loc, sends, recvs


def _xchg_start(*refs, gather):
    loc, sends, _ = _xchg_copies(*refs, gather)
    loc.start()
    for cp in sends:
        cp.start()


def _xchg_wait(*refs, gather):
    loc, sends, recvs = _xchg_copies(*refs, gather)
    for cp in recvs:
        cp.wait_recv()
    for cp in sends:
        cp.wait_send()
    loc.wait()


_XCHG_SCRATCH = [pltpu.SemaphoreType.DMA((NDEV - 1,)), pltpu.SemaphoreType.DMA((NDEV - 1,)), pltpu.SemaphoreType.DMA]


def _xchg_out_shape(x, gather):
    return jax.ShapeDtypeStruct((NDEV,) + tuple(x.shape if gather else x.shape[1:]), x.dtype)


def _exchange(x, name, gather):
    def body(x_ref, o_ref, send_sems, recv_sems, loc_sem):
        _xchg_start(x_ref, o_ref, send_sems, recv_sems, loc_sem, gather=gather)
        _xchg_wait(x_ref, o_ref, send_sems, recv_sems, loc_sem, gather=gather)

    return pl.pallas_call(
        body, name=name, out_shape=_xchg_out_shape(x, gather),
        in_specs=[_HBM], out_specs=_HBM, scratch_shapes=_XCHG_SCRATCH,
    )(x)


def _gather_two_level(x, name):
    def body(x_ref, o_ref, send_sems, recv_sems, loc_sem):
        mx, my, mc = lax.axis_index("x"), lax.axis_index("y"), lax.axis_index("c")
        me, sibling = (mx, my, mc), (mx, my, 1 - mc)
        chips = [(1 - mx, my), (mx, 1 - my), (1 - mx, 1 - my)]

        def slot(px, py, pc):
            return o_ref.at[4 * px + 2 * py + pc]

        def copy(k, block, to, src=None):
            return pltpu.make_async_remote_copy(
                src_ref=slot(*block) if src is None else src, dst_ref=slot(*block),
                send_sem=send_sems.at[k], recv_sem=recv_sems.at[k], device_id=to, device_id_type=MESH)

        mine = pltpu.make_async_copy(x_ref, slot(*me), loc_sem)
        mine.start()
        first = [copy(0, me, sibling, src=x_ref)]
        first += [copy(1 + j, me, (*chip, mc), src=x_ref) for j, chip in enumerate(chips)]
        for cp in first:
            cp.start()
        passed = [copy(4 + j, (*chip, mc), sibling) for j, chip in enumerate(chips)]
        for j, chip in enumerate(chips):
            copy(1 + j, (*chip, mc), me).wait_recv()
            passed[j].start()
        copy(0, sibling, me).wait_recv()
        for j, chip in enumerate(chips):
            copy(4 + j, (*chip, 1 - mc), me).wait_recv()
        for cp in first + passed:
            cp.wait_send()
        mine.wait()

    return pl.pallas_call(
        body, name=name, out_shape=_xchg_out_shape(x, True),
        in_specs=[_HBM], out_specs=_HBM, scratch_shapes=_XCHG_SCRATCH,
    )(x)


def _hosted_call(body, xch, *, name, grid, in_specs, out_specs, out_shape, scratch_shapes, compiler_params, args,
                 aliases=None):
    aliases = aliases or {}
    if xch is None:
        return pl.pallas_call(body, name=name, grid=grid, in_specs=in_specs, out_specs=out_specs,
                              out_shape=out_shape, scratch_shapes=scratch_shapes, input_output_aliases=aliases,
                              compiler_params=compiler_params)(*args)
    xv, gather = xch
    n_in, n_out, n_scr = len(in_specs), len(out_specs), len(scratch_shapes)

    def wrapped(*refs):
        ins = refs[:n_in]
        x_ref = refs[n_in]
        outs = refs[n_in + 1:n_in + 1 + n_out]
        o_ref = refs[n_in + 1 + n_out]
        scr = refs[n_in + 2 + n_out:]
        own, sems = scr[:n_scr], scr[n_scr:]
        first = functools.reduce(jnp.logical_and, [pl.program_id(a) == 0 for a in range(len(grid))])
        last = functools.reduce(jnp.logical_and, [pl.program_id(a) == grid[a] - 1 for a in range(len(grid))])

        @pl.when(first)
        def _():
            _xchg_start(x_ref, o_ref, *sems, gather=gather)

        body(*ins, *outs, *own)

        @pl.when(last)
        def _():
            _xchg_wait(x_ref, o_ref, *sems, gather=gather)

    res = pl.pallas_call(
        wrapped, name=name, grid=grid, in_specs=list(in_specs) + [_HBM], out_specs=list(out_specs) + [_HBM],
        out_shape=list(out_shape) + [_xchg_out_shape(xv, gather)],
        scratch_shapes=list(scratch_shapes) + _XCHG_SCRATCH, input_output_aliases=aliases,
        compiler_params=compiler_params,
    )(*args, xv)
    return list(res[:n_out]), res[n_out]


def _row_tile(R, cap, mult=8):
    best = mult
    t = mult
    while t <= min(R, cap):
        if R % t == 0:
            best = t
        t += mult
    assert R % best == 0, R
    return best


def _sum_slots(x, name, xch=None):
    _, R, C = x.shape
    tr = _row_tile(R, 256, 16 if x.dtype == BF16 else 8)

    def body(x_ref, o_ref):
        o_ref[...] = _slot_sum(x_ref)

    res = _hosted_call(
        body, xch, name=name, grid=(R // tr,),
        in_specs=[pl.BlockSpec((NDEV, tr, C), lambda i: (0, i, 0))],
        out_specs=[pl.BlockSpec((tr, C), lambda i: (i, 0))],
        out_shape=[jax.ShapeDtypeStruct((R, C), F32)],
        scratch_shapes=[], compiler_params=_cp(1), args=(x,))
    if xch is None:
        return res[0]
    return res[0][0], res[1]


def _slot_sum(x_ref):
    acc = x_ref[0].astype(F32)
    for i in range(1, NDEV):
        acc = acc + x_ref[i].astype(F32)
    return acc


def _sum_adamw(slots, w, m, v, name, xch=None):
    _, R, C = slots.shape
    tr = _row_tile(R, 128, 16 if slots.dtype == BF16 else 8)

    def body(x_ref, w_ref, m_ref, v_ref, g_ref, d_ref, nm_ref, nv_ref):
        g_ref[...] = _slot_sum(x_ref)
        _adamw_update(w_ref, g_ref, m_ref, v_ref, d_ref, nm_ref, nv_ref)

    blk = pl.BlockSpec((tr, C), lambda i: (i, 0))
    res = _hosted_call(
        body, xch, name=name, grid=(R // tr,),
        in_specs=[pl.BlockSpec((NDEV, tr, C), lambda i: (0, i, 0)), blk, blk, blk],
        out_specs=[blk] * 4, out_shape=[jax.ShapeDtypeStruct((R, C), F32)] * 4,
        scratch_shapes=[], compiler_params=_cp(1), args=(slots, w, m, v))
    if xch is None:
        return res
    return res[0], res[1]


def _adamw_update(w_ref, g_ref, m_ref, v_ref, d_ref, nm_ref, nv_ref):
    c1 = 1.0 / (1.0 - ADAM_B1 ** ADAM_STEP)
    c2 = 1.0 / (1.0 - ADAM_B2 ** ADAM_STEP)
    gv = g_ref[...]
    nm = ADAM_B1 * m_ref[...] + (1.0 - ADAM_B1) * gv
    nv = ADAM_B2 * v_ref[...] + (1.0 - ADAM_B2) * (gv * gv)
    d_ref[...] = -ADAM_LR * ((nm * c1) / (jnp.sqrt(nv * c2) + ADAM_EPS) + ADAM_WD * w_ref[...])
    nm_ref[...] = nm
    nv_ref[...] = nv


def _adamw_many(ws, gs, ms, vs):
    n = len(ws)

    def body(*refs):
        for i in range(n):
            _adamw_update(refs[i], refs[n + i], refs[2 * n + i], refs[3 * n + i],
                          refs[4 * n + i], refs[5 * n + i], refs[6 * n + i])

    shapes = [jax.ShapeDtypeStruct(w.shape, F32) for w in ws]
    res = pl.pallas_call(
        body, name="adamw_small", out_shape=shapes * 3,
        compiler_params=pltpu.CompilerParams(vmem_limit_bytes=VMEM_LIMIT_BYTES),
    )(*ws, *gs, *ms, *vs)
    return res[:n], res[n:2 * n], res[2 * n:]


def _adamw(w, g, m, v, name):
    R, C = w.shape
    tr = _row_tile(R, 256)

    def body(w_ref, g_ref, m_ref, v_ref, d_ref, nm_ref, nv_ref):
        _adamw_update(w_ref, g_ref, m_ref, v_ref, d_ref, nm_ref, nv_ref)

    blk = pl.BlockSpec((tr, C), lambda i: (i, 0))
    return pl.pallas_call(
        body, name=name, grid=(R // tr,),
        in_specs=[blk] * 4, out_specs=[blk] * 3,
        out_shape=[jax.ShapeDtypeStruct((R, C), F32)] * 3,
        compiler_params=_cp(1),
    )(w, g, m, v)


def _mod_fwd(c_rows, w_shard, b_shard):
    def body(c_ref, w_ref, b_ref, o_ref):
        s = _silu(c_ref[...]).astype(BF16)
        o_ref[...] = _dot(s, w_ref[...].astype(BF16)) + b_ref[...]

    return pl.pallas_call(
        body, name="mod_fwd",
        out_shape=jax.ShapeDtypeStruct((c_rows.shape[0], w_shard.shape[1]), F32),
        compiler_params=pltpu.CompilerParams(vmem_limit_bytes=VMEM_LIMIT_BYTES),
    )(c_rows, w_shard, b_shard)


def _mod_bwd(c_rows, dm_all, dm_shard, w_shard):
    nrow = c_rows.shape[0]

    def body(c_ref, da_ref, ds_ref, w_ref, gw_ref, gb_ref, cc_ref):
        s = _silu(c_ref[...]).astype(BF16)
        ds = ds_ref[...]
        gw_ref[...] = _dot_tn(s, ds.astype(BF16))
        gb_ref[...] = jnp.sum(da_ref[...], axis=0, keepdims=True)
        rowi = lax.broadcasted_iota(jnp.int32, ds.shape, 0)
        dmc = jnp.sum(jnp.where(rowi % 8 >= 4, ds, 0.0), axis=0, keepdims=True)
        dmc8 = jnp.broadcast_to(dmc, (8, ds.shape[1])).astype(BF16)
        cc_ref[...] = _dot_nt(dmc8, w_ref[...].astype(BF16))

    return pl.pallas_call(
        body, name="mod_bwd",
        out_shape=[jax.ShapeDtypeStruct(w_shard.shape, F32),
                   jax.ShapeDtypeStruct((1, dm_all.shape[1]), F32),
                   jax.ShapeDtypeStruct((8, D), F32)],
        compiler_params=pltpu.CompilerParams(vmem_limit_bytes=VMEM_LIMIT_BYTES),
    )(c_rows, dm_all, dm_shard, w_shard)


def _small_finish(cc_pre, c_ctx, dd_cols):
    def body(cc_ref, c_ref, dd_ref, gc_ref, gd_ref):
        gc_ref[...] = cc_ref[...] * _silu_grad(c_ref[...])
        gd_ref[...] = jnp.sum(dd_ref[...], axis=1, keepdims=True)

    return pl.pallas_call(
        body, name="small_finish",
        out_shape=[jax.ShapeDtypeStruct((1, D), F32), jax.ShapeDtypeStruct((NH, 1), F32)],
    )(cc_pre, c_ctx, dd_cols)


_BIG = (("w_in", (D, 10304), 1), ("w_br_ssd", (DI, D), 0), ("w_br_lru", (LW, D), 0), ("w_out", (D, D), 0),
        ("w_mlp1", (D, MLP), 1), ("w_mlp2", (MLP, D), 0))
_SMALL_SH = (("ssd_conv_w", (4, 4096)), ("lru_conv_w", (4, LW)), ("lru_ba", (2, LW)), ("lru_bi", (2, LW)),
             ("lru_lambda", (2, LW)))
_REPL = (("c_ctx", (D,)), ("b_gate", (2 * D,)), ("ssd_conv_b", (4096,)), ("ssd_dt_bias", (2, NH)),
         ("ssd_a_log", (2, NH)), ("ssd_d", (DI,)), ("ssd_norm_w", (DI,)), ("lru_conv_b", (LW,)),
         ("ln1_g", (D,)), ("ln1_b", (D,)),
         ("b_mlp1", (MLP,)), ("b_mlp2", (D,)), ("ln2_g", (D,)), ("ln2_b", (D,)))

_WEIGHT_NAMES = ('c_ctx', 'w_mod', 'b_mod', 'w_in', 'b_gate', 'ssd_conv_w', 'ssd_conv_b', 'ssd_dt_bias', 'ssd_a_log',
                 'ssd_d', 'ssd_norm_w', 'lru_conv_w', 'lru_conv_b', 'lru_wa', 'lru_ba', 'lru_wi', 'lru_bi',
                 'lru_lambda', 'w_br_ssd', 'w_br_lru', 'w_out', 'ln1_g', 'ln1_b', 'w_mlp1', 'b_mlp1', 'w_mlp2',
                 'b_mlp2', 'ln2_g', 'ln2_b')
_ARG_NAMES = ('x', 'c', 'ctx') + _WEIGHT_NAMES + ('loss_target',) + tuple('m_' + n for n in _WEIGHT_NAMES) + tuple(
    'v_' + n for n in _WEIGHT_NAMES)


def _to_slots(full, axis):
    n = full.shape[axis] // NDEV
    if axis == 0:
        return full.reshape(NDEV, n, full.shape[1])
    return full.reshape(full.shape[0], NDEV, n).transpose(1, 0, 2)


def _from_slots(slots, axis):
    if axis == 0:
        return slots.reshape(NDEV * slots.shape[1], slots.shape[2])
    return slots.transpose(1, 0, 2).reshape(slots.shape[1], NDEV * slots.shape[2])


def _pack_rows(arrs, width=1024, mult=8):
    flat = jnp.concatenate([a.reshape(-1) for a in arrs])
    n = flat.shape[0]
    per = width * mult
    tot = -(-n // per) * per
    return jnp.pad(flat, (0, tot - n)).reshape(tot // width, width)


def _unpack_rows(packed, shapes, lead=()):
    nl = len(lead)
    flat = packed.reshape(tuple(lead) + (-1,))
    out, off = [], 0
    for s in shapes:
        n = math.prod(s)
        out.append(flat[..., off:off + n].reshape(tuple(lead) + tuple(s)))
        off += n
    return out


def kernel(x, c, ctx, c_ctx, w_mod, b_mod, w_in, b_gate, ssd_conv_w, ssd_conv_b, ssd_dt_bias, ssd_a_log, ssd_d, ssd_norm_w, lru_conv_w, lru_conv_b, lru_wa, lru_ba, lru_wi, lru_bi, lru_lambda, w_br_ssd, w_br_lru, w_out, ln1_g, ln1_b, w_mlp1, b_mlp1, w_mlp2, b_mlp2, ln2_g, ln2_b, loss_target, m_c_ctx, m_w_mod, m_b_mod, m_w_in, m_b_gate, m_ssd_conv_w, m_ssd_conv_b, m_ssd_dt_bias, m_ssd_a_log, m_ssd_d, m_ssd_norm_w, m_lru_conv_w, m_lru_conv_b, m_lru_wa, m_lru_ba, m_lru_wi, m_lru_bi, m_lru_lambda, m_w_br_ssd, m_w_br_lru, m_w_out, m_ln1_g, m_ln1_b, m_w_mlp1, m_b_mlp1, m_w_mlp2, m_b_mlp2, m_ln2_g, m_ln2_b, v_c_ctx, v_w_mod, v_b_mod, v_w_in, v_b_gate, v_ssd_conv_w, v_ssd_conv_b, v_ssd_dt_bias, v_ssd_a_log, v_ssd_d, v_ssd_norm_w, v_lru_conv_w, v_lru_conv_b, v_lru_wa, v_lru_ba, v_lru_wi, v_lru_bi, v_lru_lambda, v_w_br_ssd, v_w_br_lru, v_w_out, v_ln1_g, v_ln1_b, v_w_mlp1, v_b_mlp1, v_w_mlp2, v_b_mlp2, v_ln2_g, v_ln2_b):
    A = dict(zip(_ARG_NAMES, (x, c, ctx, c_ctx, w_mod, b_mod, w_in, b_gate, ssd_conv_w, ssd_conv_b, ssd_dt_bias, ssd_a_log, ssd_d, ssd_norm_w, lru_conv_w, lru_conv_b, lru_wa, lru_ba, lru_wi, lru_bi, lru_lambda, w_br_ssd, w_br_lru, w_out, ln1_g, ln1_b, w_mlp1, b_mlp1, w_mlp2, b_mlp2, ln2_g, ln2_b, loss_target, m_c_ctx, m_w_mod, m_b_mod, m_w_in, m_b_gate, m_ssd_conv_w, m_ssd_conv_b, m_ssd_dt_bias, m_ssd_a_log, m_ssd_d, m_ssd_norm_w, m_lru_conv_w, m_lru_conv_b, m_lru_wa, m_lru_ba, m_lru_wi, m_lru_bi, m_lru_lambda, m_w_br_ssd, m_w_br_lru, m_w_out, m_ln1_g, m_ln1_b, m_w_mlp1, m_b_mlp1, m_w_mlp2, m_b_mlp2, m_ln2_g, m_ln2_b, v_c_ctx, v_w_mod, v_b_mod, v_w_in, v_b_gate, v_ssd_conv_w, v_ssd_conv_b, v_ssd_dt_bias, v_ssd_a_log, v_ssd_d, v_ssd_norm_w, v_lru_conv_w, v_lru_conv_b, v_lru_wa, v_lru_ba, v_lru_wi, v_lru_bi, v_lru_lambda, v_w_br_ssd, v_w_br_lru, v_w_out, v_ln1_g, v_ln1_b, v_w_mlp1, v_b_mlp1, v_w_mlp2, v_b_mlp2, v_ln2_g, v_ln2_b)))
    Bn, T, _ = x.shape
    Tc = ctx.shape[1]
    cfg = _Cfg(Bn, T, Tc)
    me = _me()
    L = {n: (A[n] if n == "c_ctx" else A[n][0]) for n in _WEIGHT_NAMES}
    nmod = L["w_mod"].shape[1]

    c_all = _exchange(c, "ag_c", True)
    c_rows = jnp.concatenate([c_all.reshape(NDEV * Bn, D), jnp.broadcast_to(c_ctx[None, :], (8, D))], axis=0)
    b_shard = lax.dynamic_slice(L["b_mod"], (me * nmod,), (nmod,)).reshape(1, nmod)
    m_part = _mod_fwd(c_rows, L["w_mod"], b_shard)
    m_all = _exchange(m_part, "ag_mod", True)
    m_full = m_all.transpose(1, 0, 2).reshape(NDEV * Bn + 8, NMOD * D)
    m_mine = lax.dynamic_slice(m_full, (me * Bn, 0), (Bn, NMOD * D))
    mc = m_full[NDEV * Bn, :2 * D]

    w_in_all = _gather_two_level(L["w_in"].astype(BF16), "ag_w_in")
    rest_payload = jnp.concatenate([L[n].astype(BF16).reshape(-1, 1024) for n, _, _ in _BIG[1:]], axis=0)
    small_shapes = [(s[0], s[1] // NDEV) for _, s in _SMALL_SH]
    small_all = _exchange(_pack_rows([L[n] for n, _ in _SMALL_SH], width=512), "ag_w_small", True)
    W = {}
    for (n, shp), piece in zip(_SMALL_SH, _unpack_rows(small_all, small_shapes, lead=(NDEV,))):
        W[n] = piece.transpose(1, 0, 2).reshape(shp)
    W["w_main"], W["w_dt"] = _perm_w_in(_from_slots(w_in_all, 1))
    for n in ("ssd_conv_b", "lru_conv_b", "ssd_norm_w", "b_gate", "ln1_g", "ln1_b", "b_mlp1", "b_mlp2", "ln2_g", "ln2_b"):
        W[n] = L[n].reshape(1, -1)
    for n in ("ssd_dt_bias", "ssd_a_log", "ssd_d", "lru_wa", "lru_wi"):
        W[n] = L[n]

    loss_part, grad_x, g, dm, dmc, xres = _local_step(cfg, x, ctx, loss_target, m_mine, mc, W, rest_payload)
    loss = lax.psum(loss_part, ("x", "y", "c"))

    dmc_pad = jnp.pad(dmc, ((0, 4 - Bn), (0, (NMOD - 2) * D)))
    dm_payload = jnp.concatenate([jnp.pad(dm, ((0, 4 - Bn), (0, 0))), dmc_pad], axis=0)
    upd_w_in, dm_all = _sum_adamw(xres["rs_w_in"], L["w_in"], A["m_w_in"][0], A["v_w_in"][0], "sum_adamw_w_in",
                                  xch=(dm_payload, True))
    dm_all = dm_all.reshape(NDEV * 8, NMOD * D)
    c_rows_b = jnp.concatenate([jnp.pad(c_all, ((0, 0), (0, 4 - Bn), (0, 0))),
                                jnp.broadcast_to(c_ctx[None, None, :], (NDEV, 4, D))], axis=1).reshape(NDEV * 8, D)
    dm_shard = lax.dynamic_slice(dm_all, (0, me * nmod), (NDEV * 8, nmod))
    g_w_mod, g_b_mod, cc_part = _mod_bwd(c_rows_b, dm_all, dm_shard, L["w_mod"])
    g["c_ctx"] = cc_part[0]

    g["ssd_d"] = g.pop("ssd_d_cols")
    small_names = [n for n, _ in _REPL] + [n for n, _ in _SMALL_SH]
    small_full_shapes = [s for _, s in _REPL] + [s for _, s in _SMALL_SH]
    red_b, sm_all = _sum_slots(xres["rs_rest"], "sum_w_rest", xch=(_pack_rows([g[n] for n in small_names]), True))
    sm_sum = _sum_slots(sm_all, "sum_g_small")
    gs = dict(zip(small_names, _unpack_rows(sm_sum, small_full_shapes)))
    gcc, gdd = _small_finish(gs["c_ctx"].reshape(1, D), c_ctx.reshape(1, D), gs["ssd_d"].reshape(NH, HD))
    gs["c_ctx"] = gcc.reshape(D)
    gs["ssd_d"] = gdd.reshape(NH)
    for n, shp in _SMALL_SH:
        ns = shp[1] // NDEV
        gs[n] = lax.dynamic_slice(gs[n], (0, me * ns), (shp[0], ns))
    gs["b_mod"] = g_b_mod.reshape(NMOD * D)
    lru_sum = _sum_slots(xres["ag_lru"], "sum_g_lru").reshape(2, 2, LB, LBW, LBW)
    gs["lru_wa"], gs["lru_wi"] = lru_sum[0], lru_sum[1]

    gb = {}
    off = 0
    for n, shp, axis in _BIG[1:]:
        shard_shape = (shp[0] // NDEV, shp[1]) if axis == 0 else (shp[0], shp[1] // NDEV)
        r = math.prod(shard_shape) // 1024
        gb[n] = red_b[off:off + r].reshape(shard_shape)
        off += r
    gb["w_mod"] = g_w_mod

    grads, deltas, new_m, new_v = {}, {}, {}, {}
    big_names = ["w_mod"] + [n for n, _, _ in _BIG]
    grads["w_in"], deltas["w_in"], new_m["w_in"], new_v["w_in"] = upd_w_in
    for n in big_names:
        if n == "w_in":
            continue
        d_, nm_, nv_ = _adamw(L[n], gb[n], A["m_" + n][0], A["v_" + n][0], "adamw_" + n)
        grads[n], deltas[n], new_m[n], new_v[n] = gb[n], d_, nm_, nv_
    sm_names = [n for n in _WEIGHT_NAMES if n not in big_names]

    def two_d(a):
        return a.reshape(1, -1) if a.ndim == 1 else a
    loc = lambda pre: [two_d(A[pre + n] if n == "c_ctx" else A[pre + n][0]) for n in sm_names]
    gsm = [two_d(gs[n].reshape(L[n].shape)) for n in sm_names]
    ds_, nms_, nvs_ = _adamw_many(loc(""), gsm, loc("m_"), loc("v_"))
    for n, gv, dv, mv, vv in zip(sm_names, gsm, ds_, nms_, nvs_):
        shp = L[n].shape
        grads[n], deltas[n], new_m[n], new_v[n] = gv.reshape(shp), dv.reshape(shp), mv.reshape(shp), vv.reshape(shp)

    def out(dct):
        return [dct[n] if n == "c_ctx" else dct[n][None] for n in _WEIGHT_NAMES]
    return (loss, grad_x, *out(grads), *out(deltas), *out(new_m), *out(new_v))
```

```python
import functools
import math

import jax
import jax.numpy as jnp
from jax import lax
from jax.experimental import pallas as pl
from jax.experimental.pallas import tpu as pltpu

F32 = jnp.float32
BF16 = jnp.bfloat16

D = 1024
GRID_W = 64
DI = 2048
NH = 32
HD = 64
NG = 8
HPG = 4
NS = 128
CH = 128
LW = 1024
LB = 8
LBW = 128
LRU_C = 8.0
MLP = 4096
NMOD = 6
ALPHA = 2.0 ** 0.25
LN_EPS = 1e-6
RMS_EPS = 1e-5
PM = 10240
DTW = 128
CONVW = 5120
NDEV = 8

ADAM_LR = 0.001
ADAM_B1 = 0.9
ADAM_B2 = 0.999
ADAM_EPS = 1e-08
ADAM_WD = 0.01
ADAM_STEP = 10

VMEM_LIMIT_BYTES = 56 * 1024 * 1024


def _cp(n_axes):
    return pltpu.CompilerParams(dimension_semantics=("arbitrary",) * n_axes,
                                vmem_limit_bytes=VMEM_LIMIT_BYTES)


def _sigmoid(x):
    return 0.5 * jnp.tanh(0.5 * x) + 0.5


def _silu(x):
    return x * _sigmoid(x)


def _silu_grad(x):
    s = _sigmoid(x)
    return s * (1.0 + x * (1.0 - s))


def _log1p_pos(e):
    return jnp.where(e < 1e-2, e * (1.0 - e * (0.5 - e * (1.0 / 3.0))), jnp.log(1.0 + e))


def _softplus(x):
    return jnp.maximum(x, 0.0) + _log1p_pos(jnp.exp(-jnp.abs(x)))


_GELU_K = math.sqrt(2.0 / math.pi)


def _gelu(x):
    t = jnp.tanh(_GELU_K * (x + 0.044715 * x * x * x))
    return 0.5 * x * (1.0 + t)


def _gelu_and_grad(x):
    x2 = x * x
    t = jnp.tanh(_GELU_K * x * (1.0 + 0.044715 * x2))
    dt = (1.0 - t * t) * _GELU_K * (1.0 + 3.0 * 0.044715 * x2)
    h = 0.5 * (1.0 + t)
    return x * h, h + 0.5 * x * dt


def _ln(x):
    mu = jnp.mean(x, axis=-1, keepdims=True)
    xc = x - mu
    var = jnp.mean(xc * xc, axis=-1, keepdims=True)
    rs = lax.rsqrt(var + LN_EPS)
    return xc * rs, rs


def _ln_bwd(dy, xhat, rs):
    m1 = jnp.mean(dy, axis=-1, keepdims=True)
    m2 = jnp.mean(dy * xhat, axis=-1, keepdims=True)
    return rs * (dy - m1 - xhat * m2)


def _dot(a, b):
    return lax.dot_general(a, b, (((1,), (0,)), ((), ())), preferred_element_type=F32)


def _dot_nt(a, b):
    return lax.dot_general(a, b, (((1,), (1,)), ((), ())), preferred_element_type=F32)


def _dot_tn(a, b):
    return lax.dot_general(a, b, (((0,), (0,)), ((), ())), preferred_element_type=F32)


def _split3(a):
    a0 = a.astype(BF16)
    r = a - a0.astype(F32)
    a1 = r.astype(BF16)
    a2 = (r - a1.astype(F32)).astype(BF16)
    return a0, a1, a2


def _dot_exact_l(m_bf, a):
    a0, a1, a2 = _split3(a)
    return _dot(m_bf, a0) + _dot(m_bf, a1) + _dot(m_bf, a2)


def _dot_hilo_r(a, m_bf):
    a0 = a.astype(BF16)
    a1 = (a - a0.astype(F32)).astype(BF16)
    return _dot(a0, m_bf) + _dot(a1, m_bf)


def _tri(n, upper):
    ii = lax.broadcasted_iota(jnp.int32, (n, n), 0)
    kk = lax.broadcasted_iota(jnp.int32, (n, n), 1)
    m = (kk >= ii) if upper else (kk <= ii)
    return jnp.where(m, 1.0, 0.0).astype(BF16)


def _fit(n, t):
    t = min(t, n)
    while n % t:
        t //= 2
    return t


def _mm(a, b, mode, name, out_dtype=F32, tm=512, tn=512, tk=512, xch=None):
    if mode == "nn":
        M, K = a.shape
        N = b.shape[1]
    elif mode == "nt":
        M, K = a.shape
        N = b.shape[0]
    else:
        K, M = a.shape
        N = b.shape[1]
    tm, tn, tk = _fit(M, tm), _fit(N, tn), _fit(K, tk)
    assert M % tm == 0 and N % tn == 0 and K % tk == 0, (name, M, N, K, tm, tn, tk)
    nk = K // tk
    if mode == "tn":
        a_spec = pl.BlockSpec((tk, tm), lambda i, j, k: (k, i))
    else:
        a_spec = pl.BlockSpec((tm, tk), lambda i, j, k: (i, k))
    if mode == "nt":
        b_spec = pl.BlockSpec((tn, tk), lambda i, j, k: (j, k))
    else:
        b_spec = pl.BlockSpec((tk, tn), lambda i, j, k: (k, j))
    dn = {"nn": (((1,), (0,)), ((), ())), "nt": (((1,), (1,)), ((), ())), "tn": (((0,), (0,)), ((), ()))}[mode]

    def body(a_ref, b_ref, o_ref, acc_ref):
        k = pl.program_id(2)

        @pl.when(k == 0)
        def _():
            acc_ref[...] = jnp.zeros_like(acc_ref)

        acc_ref[...] += lax.dot_general(a_ref[...].astype(BF16), b_ref[...].astype(BF16), dn,
                                        preferred_element_type=F32)

        @pl.when(k == nk - 1)
        def _():
            o_ref[...] = acc_ref[...].astype(o_ref.dtype)

    def body_one_step(a_ref, b_ref, o_ref):
        o_ref[...] = lax.dot_general(a_ref[...].astype(BF16), b_ref[...].astype(BF16), dn,
                                     preferred_element_type=F32).astype(o_ref.dtype)

    res = _hosted_call(
        body if nk > 1 else body_one_step, xch, name=name, grid=(M // tm, N // tn, nk),
        in_specs=[a_spec, b_spec],
        out_specs=[pl.BlockSpec((tm, tn), lambda i, j, k: (i, j))],
        out_shape=[jax.ShapeDtypeStruct((M, N), out_dtype)],
        scratch_shapes=[pltpu.VMEM((tm, tn), F32)] if nk > 1 else [],
        compiler_params=_cp(3), args=(a, b))
    if xch is None:
        return res[0]
    return res[0][0], res[1]


def _mm_mlp1(h2, w1, b1, tm=1024, tn=1024):
    M, K = h2.shape
    N = w1.shape[1]
    tm, tn = _fit(M, tm), _fit(N, tn)

    def body(a_ref, b_ref, bias_ref, a1_ref, act_ref):
        v = _dot(a_ref[...], b_ref[...]) + bias_ref[...]
        a1_ref[...] = v
        r = jnp.maximum(v, 0.0)
        act_ref[...] = (r * r).astype(BF16)

    out = pl.BlockSpec((tm, tn), lambda i, j: (i, j))
    return pl.pallas_call(
        body, name="mm_mlp1", grid=(M // tm, N // tn),
        in_specs=[pl.BlockSpec((tm, K), lambda i, j: (i, 0)), pl.BlockSpec((K, tn), lambda i, j: (0, j)),
                  pl.BlockSpec((1, tn), lambda i, j: (0, j))],
        out_specs=[out, out],
        out_shape=[jax.ShapeDtypeStruct((M, N), F32), jax.ShapeDtypeStruct((M, N), BF16)],
        compiler_params=_cp(2),
    )(h2, w1, b1)


def _mm_dact(dmlp, w2, a1, tm=1024, tn=1024):
    M, K = dmlp.shape
    N = w2.shape[0]
    tm, tn = _fit(M, tm), _fit(N, tn)

    def body(d_ref, w_ref, a1_ref, o_ref, acc_ref):
        i = pl.program_id(1)

        @pl.when(i == 0)
        def _():
            acc_ref[...] = jnp.zeros_like(acc_ref)

        da = _dot_nt(d_ref[...], w_ref[...]) * (2.0 * jnp.maximum(a1_ref[...], 0.0))
        o_ref[...] = da.astype(BF16)
        acc_ref[0:1, :] += jnp.sum(da, axis=0, keepdims=True)

    blk = pl.BlockSpec((tm, tn), lambda j, i: (i, j))
    return pl.pallas_call(
        body, name="mm_dact", grid=(N // tn, M // tm),
        in_specs=[pl.BlockSpec((tm, K), lambda j, i: (i, 0)), pl.BlockSpec((tn, K), lambda j, i: (j, 0)), blk],
        out_specs=[blk, pl.BlockSpec((8, tn), lambda j, i: (0, j))],
        out_shape=[jax.ShapeDtypeStruct((M, N), BF16), jax.ShapeDtypeStruct((8, N), F32)],
        compiler_params=_cp(2),
    )(dmlp, w2, a1)


class _Cfg:
    def __init__(self, Bn, T, Tc):
        assert T % Tc == 0 and Tc % CH == 0 and Tc % GRID_W == 0
        self.Bn, self.T, self.Tc = Bn, T, Tc
        self.TT = T + Tc
        self.TB = Tc
        self.nbt = self.TT // self.TB
        self.nbl = T // self.TB
        self.NT = Bn * self.TT
        self.N = Bn * T
        self.nct = Tc // CH
        self.nlt = T // CH
        self.nch = self.nct + self.nlt


def _ln_mod_fwd(cfg, ctx2, x2, shift_tab, scale_tab, xch):
    TB, nbt, nbl = cfg.TB, cfg.nbt, cfg.nbl

    def body(c_ref, x_ref, sh_ref, sc_ref, o_ref):
        j = pl.program_id(1)
        xhat, _ = _ln(jnp.where(j == 0, c_ref[...], x_ref[...]))
        o_ref[...] = (xhat * (1.0 + sc_ref[...]) + sh_ref[...]).astype(BF16)

    tab = pl.BlockSpec((None, 1, D), lambda b, j: (2 * b + jnp.minimum(j, 1), 0, 0))
    outs, xres = _hosted_call(
        body, xch, name="ln_mod_fwd", grid=(cfg.Bn, nbt),
        in_specs=[pl.BlockSpec((TB, D), lambda b, j: (b, 0)),
                  pl.BlockSpec((TB, D), lambda b, j: (b * nbl + jnp.maximum(j - 1, 0), 0)), tab, tab],
        out_specs=[pl.BlockSpec((TB, D), lambda b, j: (b * nbt + j, 0))],
        out_shape=[jax.ShapeDtypeStruct((cfg.NT, D), BF16)],
        scratch_shapes=[], compiler_params=_cp(2), args=(ctx2, x2, shift_tab, scale_tab))
    return outs[0], xres


GP = 8
NGB = NG // GP
HPB = GP * HPG


def _heads_to_front(x, d, gb, inverse=False):
    off = d * NH + gb * HPB
    return pltpu.roll(x, off if inverse else (DTW - off) % DTW, 1)


def _chunks_per_step(cfg):
    n = cfg.NT // CH
    return max(c for c in (4, 3, 2, 1) if n % c == 0)


def _dt_fwd(cfg, dt_raw, dt_bias, a_log):
    cps = _chunks_per_step(cfg)

    def body(raw_ref, bias_ref, alog_ref, dt_ref, dtg_ref, cumg_ref, cumT_ref):
        a = -jnp.exp(alog_ref[...])
        col = lax.broadcasted_iota(jnp.int32, (CH, DTW), 1)
        for c in range(cps):
            rows = slice(c * CH, (c + 1) * CH)
            dt = _softplus(raw_ref[rows, :] + bias_ref[...])
            dta = dt * a
            cf = _dot_exact_l(_tri(CH, False), dta)
            cr = _dot_exact_l(_tri(CH, True), dta)
            cum = jnp.where(col < NH, cf, cr)
            dt_ref[rows, :] = dt
            for d in range(2):
                for gb in range(NGB):
                    dtg_ref[d, gb, rows, :] = _heads_to_front(dt, d, gb)
                    cg = _heads_to_front(cum, d, gb)
                    cumg_ref[d, gb, rows, :] = cg
                    cumT_ref[d, gb, c] = cg.T

    blk = pl.BlockSpec((cps * CH, DTW), lambda i: (i, 0))
    row = pl.BlockSpec((1, DTW), lambda i: (0, 0))
    gblk = pl.BlockSpec((2, NGB, cps * CH, DTW), lambda i: (0, 0, i, 0))
    return pl.pallas_call(
        body, name="dt_fwd", grid=(cfg.NT // (cps * CH),),
        in_specs=[blk, row, row],
        out_specs=[blk, gblk, gblk, pl.BlockSpec((2, NGB, cps, DTW, CH), lambda i: (0, 0, i, 0, 0))],
        out_shape=[jax.ShapeDtypeStruct((cfg.NT, DTW), F32),
                   jax.ShapeDtypeStruct((2, NGB, cfg.NT, DTW), F32),
                   jax.ShapeDtypeStruct((2, NGB, cfg.NT, DTW), F32),
                   jax.ShapeDtypeStruct((2, NGB, cfg.NT // CH, DTW, CH), F32)],
        compiler_params=_cp(1),
    )(dt_raw, dt_bias, a_log)


def _dt_bwd(cfg, dAs, dxxs, dt_raw, dt, dt_bias, a_log):
    def body(dAf_ref, dAr_ref, dxf_ref, dxr_ref, raw_ref, dt_ref, bias_ref, alog_ref, o_ref, acc_ref):
        i = pl.program_id(0)

        @pl.when(i == 0)
        def _():
            acc_ref[...] = jnp.zeros_like(acc_ref)

        a = -jnp.exp(alog_ref[...])
        col = lax.broadcasted_iota(jnp.int32, (CH, DTW), 1)
        for c in range(cps):
            rows = slice(c * CH, (c + 1) * CH)
            dA_v = jnp.zeros((CH, DTW), F32)
            dxx_v = jnp.zeros((CH, DTW), F32)
            for d, (ra, rx) in enumerate(((dAf_ref, dxf_ref), (dAr_ref, dxr_ref))):
                for gb in range(NGB):
                    dA_v = dA_v + _heads_to_front(ra[gb, rows, :], d, gb, inverse=True)
                    dxx_v = dxx_v + _heads_to_front(rx[gb, rows, :], d, gb, inverse=True)
            ddta = jnp.where(col < NH, _dot_exact_l(_tri(CH, True), dA_v), _dot_exact_l(_tri(CH, False), dA_v))
            dtv = dt_ref[rows, :]
            ddt = ddta * a + dxx_v
            draw = ddt * _sigmoid(raw_ref[rows, :] + bias_ref[...])
            draw = jnp.where(col < 2 * NH, draw, 0.0)
            o_ref[rows, :] = draw
            da = jnp.sum(ddta * dtv, axis=0, keepdims=True) * a
            da = jnp.where(col[:1] < 2 * NH, da, 0.0)
            acc_ref[0:1, :] += da
            acc_ref[1:2, :] += jnp.sum(draw, axis=0, keepdims=True)

    cps = _chunks_per_step(cfg)
    blk = pl.BlockSpec((cps * CH, DTW), lambda i: (i, 0))
    row = pl.BlockSpec((1, DTW), lambda i: (0, 0))
    gblk = pl.BlockSpec((NGB, cps * CH, DTW), lambda i: (0, i, 0))
    return pl.pallas_call(
        body, name="dt_bwd", grid=(cfg.NT // (cps * CH),),
        in_specs=[gblk, gblk, gblk, gblk, blk, blk, row, row],
        out_specs=[blk, pl.BlockSpec((8, DTW), lambda i: (0, 0))],
        out_shape=[jax.ShapeDtypeStruct((cfg.NT, DTW), F32), jax.ShapeDtypeStruct((8, DTW), F32)],
        compiler_params=_cp(1),
    )(dAs[0], dAs[1], dxxs[0], dxxs[1], dt_raw, dt, dt_bias, a_log)


_TAPS = (2, 1, 0, -1)


def _conv_fwd(cfg, proj, conv_w, conv_b):
    TB, nbt = cfg.TB, cfg.nbt
    CB = CONVW // 2
    SUB = 256
    n_act = DI + 2 * NG * NS

    def body(u_ref, w_ref, b_ref, o_ref, sg_ref):
        i = pl.program_id(0)
        j = pl.program_id(1)
        R = jnp.where(i % nbt == 0, cfg.Tc, GRID_W)
        t = lax.broadcasted_iota(jnp.int32, (TB, SUB), 0)
        pos = jnp.bitwise_and(t, R - 1)
        keep = {s: jnp.where(jnp.logical_and(pos - s >= 0, pos - s < R), 1.0, 0.0) for s in (2, 1, -1)}
        def sub_tile(q, act):
            sl = slice(q * SUB, (q + 1) * SUB)
            u = u_ref[:, sl]
            pre = b_ref[:, sl] + w_ref[2:3, sl] * u
            for k in (0, 1, 3):
                pre = pre + w_ref[k:k + 1, sl] * (pltpu.roll(u, _TAPS[k] % TB, 0) * keep[_TAPS[k]])
            if act:
                s = _sigmoid(pre)
                o_ref[:, sl] = pre * s
                sg_ref[:, sl] = s * (1.0 + pre * (1.0 - s))
            else:
                o_ref[:, sl] = pre

        for q in range(CB // SUB):
            if q * SUB >= n_act - CB:
                pl.when(j == 0)(functools.partial(sub_tile, q, True))
                pl.when(j == 1)(functools.partial(sub_tile, q, False))
            else:
                sub_tile(q, True)

    blk = pl.BlockSpec((TB, CB), lambda i, j: (i, j))
    return pl.pallas_call(
        body, name="conv_fwd", grid=(cfg.NT // TB, CONVW // CB),
        in_specs=[blk, pl.BlockSpec((4, CB), lambda i, j: (0, j)), pl.BlockSpec((1, CB), lambda i, j: (0, j))],
        out_specs=[blk, blk],
        out_shape=[jax.ShapeDtypeStruct((cfg.NT, CONVW), F32)] * 2,
        compiler_params=_cp(2),
    )(proj, conv_w, conv_b)


_ANY = pl.BlockSpec(memory_space=pl.ANY)


def _conv_bwd(cfg, name, dproj, proj, conv_w, sgrad, addends, col0, width, skip=None, xch=None):
    TB, nbt, nbl = cfg.TB, cfg.nbt, cfg.nbl
    CB = 1024
    SUB = 256
    c0 = col0 // CB
    addends = list(addends) + ([] if sgrad is None else [sgrad])
    n_add = len(addends)

    def body(*refs):
        u_ref, w_ref = refs[1:3]
        add_refs = refs[3:3 + n_add]
        rest = refs[3 + n_add:]
        if sgrad is not None:
            add_refs, sg_ref = add_refs[:-1], add_refs[-1]
        if skip is not None:
            dy_ref, dv_ref = rest[:2]
            rest = rest[2:]
        o_ref, acc_ref = rest
        i = pl.program_id(1)

        @pl.when(i == 0)
        def _():
            acc_ref[...] = jnp.zeros_like(acc_ref)

        isctx = (i % nbt) == 0
        R = jnp.where(isctx, cfg.Tc, GRID_W)
        t = lax.broadcasted_iota(jnp.int32, (TB, SUB), 0)
        pos = jnp.bitwise_and(t, R - 1)
        keep = {s: jnp.where(jnp.logical_and(pos - s >= 0, pos - s < R), 1.0, 0.0) for s in (2, 1, -1, -2)}

        def shifted(v, s):
            return v if s == 0 else pltpu.roll(v, s % TB, 0) * keep[s]

        for q in range(CB // SUB):
            sl = slice(q * SUB, (q + 1) * SUB)
            u = u_ref[:, sl]
            us = [shifted(u, _TAPS[k]) for k in range(4)]
            g = add_refs[0][:, sl]
            for r in add_refs[1:]:
                g = g + r[:, sl]
            if skip is not None:
                g = g + jnp.where(isctx, 0.0, dv_ref[:, sl] * dy_ref[:, sl])
            if sgrad is not None:
                g = g * sg_ref[:, sl]
            dp = jnp.zeros_like(g)
            for k in range(4):
                acc_ref[k:k + 1, sl] += jnp.sum(g * us[k], axis=0, keepdims=True)
                dp = dp + w_ref[k:k + 1, sl] * shifted(g, -_TAPS[k])
            acc_ref[4:5, sl] += jnp.sum(g, axis=0, keepdims=True)
            o_ref[:, sl] = dp.astype(BF16)

    blk = pl.BlockSpec((TB, CB), lambda j, i: (i, j))
    wide = pl.BlockSpec((TB, CB), lambda j, i: (i, c0 + j))
    in_specs = [_ANY, wide, pl.BlockSpec((4, CB), lambda j, i: (0, c0 + j))]
    in_specs += [blk] * (n_add if sgrad is None else n_add - 1) + ([] if sgrad is None else [wide])
    args = [dproj, proj, conv_w] + addends
    if skip is not None:
        def lat(j, i):
            b = i // nbt
            return (b * nbl + jnp.maximum(i % nbt - 1, 0), j)
        in_specs += [pl.BlockSpec((TB, CB), lat), pl.BlockSpec((1, CB), lambda j, i: (0, j))]
        args += list(skip)
    return _hosted_call(
        body, xch, name=name, grid=(width // CB, cfg.NT // TB),
        in_specs=in_specs,
        out_specs=[pl.BlockSpec((TB, CB), lambda j, i: (i, c0 + j)), pl.BlockSpec((8, CB), lambda j, i: (0, j))],
        out_shape=[jax.ShapeDtypeStruct((cfg.NT, PM), BF16), jax.ShapeDtypeStruct((8, width), F32)],
        scratch_shapes=[], compiler_params=_cp(2), args=args, aliases={0: 0})


def _chunk_of_step(cfg, rev):
    nct, nlt = cfg.nct, cfg.nlt
    if not rev:
        return lambda s: s
    return lambda s: jnp.where(s < nct, nct - 1 - s, 2 * nct + nlt - 1 - s)


def _expand4(v, band, base):
    out = v[:, base + 3:base + 4]
    for h in (2, 1, 0):
        out = jnp.where(band == h, v[:, base + h:base + h + 1], out)
    return out


def _ssd_step_tiles(dt_ref, cum_ref, cumT_ref, rev):
    cum_t = cum_ref[...]
    last = 0 if rev else CH - 1
    llast = cum_t[last:last + 1, :]
    return (dt_ref[...], cum_t, cumT_ref[...], llast, jnp.exp(llast), last)


def _ssd_common(gi, x_ref, b_ref, c_ref, tiles, rev, intra=True):
    dt_t, cum_t, cumT_t, llast, elast, last = tiles
    base = gi * HPG
    xh = x_ref[:, gi * HPG * HD:(gi + 1) * HPG * HD]
    Bm = b_ref[:, gi * NS:(gi + 1) * NS].astype(BF16)
    band = lax.broadcasted_iota(jnp.int32, (CH, HPG * HD), 1) // HD
    cbs = [jnp.broadcast_to(cum_t[:, base + h:base + h + 1], (CH, CH)) for h in range(HPG)]
    Cm = G = decs = None
    if intra:
        Cm = c_ref[:, gi * NS:(gi + 1) * NS].astype(BF16)
        G = _dot_nt(Cm, Bm)
        ii = lax.broadcasted_iota(jnp.int32, (CH, CH), 0)
        jj = lax.broadcasted_iota(jnp.int32, (CH, CH), 1)
        mask = (jj >= ii) if rev else (jj <= ii)
        decs = [jnp.exp(jnp.where(mask, cbs[h] - cumT_t[base + h:base + h + 1, :], -1e30)) for h in range(HPG)]
    cum_exp = jnp.concatenate([cbs[3], cbs[3]], axis=1)
    ll_exp = llast[:, base + 3:base + 4]
    for h in (2, 1, 0):
        cum_exp = jnp.where(band == h, jnp.concatenate([cbs[h], cbs[h]], axis=1), cum_exp)
        ll_exp = jnp.where(band[:1] == h, llast[:, base + h:base + h + 1], ll_exp)
    ecum = jnp.exp(cum_exp) if intra else None
    e_exp = jnp.exp(ll_exp - cum_exp)
    dt_exp = _expand4(dt_t, band, base)
    X = xh * dt_exp
    rb = lax.broadcasted_iota(jnp.int32, (HPG * HD, NS), 0) // HD
    dec_rows = elast[:, base + 3:base + 4]
    for h in (2, 1, 0):
        dec_rows = jnp.where(rb == h, elast[:, base + h:base + h + 1], dec_rows)
    return xh, Bm, Cm, band, e_exp, ecum, dt_exp, X, G, decs, elast, dec_rows, last


def _ssd_specs(cfg, rev):
    nch = cfg.nch
    cmap = _chunk_of_step(cfg, rev)
    d = 1 if rev else 0

    def make(stepmap):
        def row(b, g, sp):
            return b * nch + cmap(stepmap(sp))
        bo, co = DI // (GP * NS), (DI + NG * NS) // (GP * NS)
        return [
            pl.BlockSpec((CH, GP * HPG * HD), lambda b, g, sp: (row(b, g, sp), g)),
            pl.BlockSpec((CH, GP * NS), lambda b, g, sp: (row(b, g, sp), bo + g)),
            pl.BlockSpec((CH, GP * NS), lambda b, g, sp: (row(b, g, sp), co + g)),
            pl.BlockSpec((None, None, CH, DTW), lambda b, g, sp: (d, g, row(b, g, sp), 0)),
            pl.BlockSpec((None, None, CH, DTW), lambda b, g, sp: (d, g, row(b, g, sp), 0)),
            pl.BlockSpec((None, None, None, DTW, CH), lambda b, g, sp: (d, g, row(b, g, sp), 0, 0)),
        ], row
    return make


def _ssd_fwd(cfg, act, dtg, cumg, cumTg, rev, y_other=None, dvec=None):
    nch = cfg.nch
    in_specs, row = _ssd_specs(cfg, rev)(lambda sp: sp)
    total = y_other is not None

    def body(*refs):
        x_ref, b_ref, c_ref, dt_ref, cum_ref, cumT_ref = refs[:6]
        if total:
            yo_ref, dv_ref = refs[6:8]
        y_ref, hs_ref, h_scr = refs[-3:]
        s = pl.program_id(2)

        @pl.when(s == 0)
        def _():
            h_scr[...] = jnp.zeros_like(h_scr)

        def step(intra):
            tiles = _ssd_step_tiles(dt_ref, cum_ref, cumT_ref, rev)
            for gi in range(GP):
                xh, Bm, Cm, band, e_exp, ecum, dt_exp, X, G, decs, elast, dec_rows, last = _ssd_common(
                    gi, x_ref, b_ref, c_ref, tiles, rev, intra)
                H = h_scr[gi]
                if intra:
                    Mcat = jnp.concatenate([(G * decs[h]).astype(BF16) for h in range(HPG)], axis=1)
                    Xbd = jnp.concatenate([jnp.where(band == h, X, 0.0).astype(BF16) for h in range(HPG)], axis=0)
                    xsl = slice(gi * HPG * HD, (gi + 1) * HPG * HD)
                    Y = ecum * _dot_nt(Cm, H.astype(BF16)) + _dot(Mcat, Xbd)
                    if total:
                        Y = Y + yo_ref[:, xsl] + dv_ref[:, xsl] * xh
                    y_ref[:, xsl] = Y
                hs_ref[gi] = H
                S = _dot_tn((e_exp * X).astype(BF16), Bm)
                h_scr[gi] = dec_rows * H + S

        isctx = cmap(s) < cfg.nct

        @pl.when(isctx)
        def _():
            step(False)

        @pl.when(jnp.logical_not(isctx))
        def _():
            step(True)

    cmap = _chunk_of_step(cfg, rev)
    yblk = pl.BlockSpec((CH, GP * HPG * HD), lambda b, g, s: (row(b, g, s), g))
    args = [act, act, act, dtg, cumg, cumTg]
    if total:
        in_specs = in_specs + [yblk, pl.BlockSpec((1, GP * HPG * HD), lambda b, g, s: (0, g))]
        args += [y_other, dvec]
    return pl.pallas_call(
        body, name="ssd_fwd_rev" if rev else "ssd_fwd", grid=(cfg.Bn, NG // GP, nch),
        in_specs=in_specs,
        out_specs=[yblk, pl.BlockSpec((None, GP, None, HPG * HD, NS), lambda b, g, s: (b, g, s, 0, 0))],
        out_shape=[jax.ShapeDtypeStruct((cfg.NT, DI), F32),
                   jax.ShapeDtypeStruct((cfg.Bn, NG, nch, HPG * HD, NS), F32)],
        scratch_shapes=[pltpu.VMEM((GP, HPG * HD, NS), F32)],
        compiler_params=_cp(3),
    )(*args)


def _ssd_bwd(cfg, act, dtg, cumg, cumTg, hs, dy, rev, xch=None):
    nch, nct, nlt = cfg.nch, cfg.nct, cfg.nlt
    cmap = _chunk_of_step(cfg, rev)
    in_specs, row = _ssd_specs(cfg, rev)(lambda sp: nch - 1 - sp)

    def lat_row(b, g, sp):
        c = cmap(nch - 1 - sp)
        return b * nlt + jnp.maximum(c - nct, 0)

    def body(x_ref, b_ref, c_ref, dt_ref, cum_ref, cumT_ref, dy_ref, hs_ref,
             dxh_ref, dB_ref, dC_ref, dA_ref, dxx_ref, dh_scr):
        sp = pl.program_id(2)

        @pl.when(sp == 0)
        def _():
            dh_scr[...] = jnp.zeros_like(dh_scr)

        def step(intra):
            tiles = _ssd_step_tiles(dt_ref, cum_ref, cumT_ref, rev)
            dA_t = jnp.zeros((CH, DTW), F32)
            dAT_t = jnp.zeros((DTW, CH), F32)
            dxx_t = jnp.zeros((CH, DTW), F32)
            for gi in range(GP):
                dA_g, dAT_g, dxx_g = group_bwd(gi, intra, tiles, x_ref, b_ref, c_ref, dy_ref, hs_ref,
                                               dxh_ref, dB_ref, dC_ref, dh_scr)
                dA_t = dA_t + dA_g
                dxx_t = dxx_t + dxx_g
                if intra:
                    dAT_t = dAT_t + dAT_g
            dA_ref[...] = dA_t - dAT_t.T if intra else dA_t
            dxx_ref[...] = dxx_t

        isctx = cmap(nch - 1 - sp) < nct

        @pl.when(isctx)
        def _():
            step(False)

        @pl.when(jnp.logical_not(isctx))
        def _():
            step(True)

    def group_bwd(gi, intra, tiles, x_ref, b_ref, c_ref, dy_ref, hs_ref, dxh_ref, dB_ref, dC_ref, dh_scr):
        xsl = slice(gi * HPG * HD, (gi + 1) * HPG * HD)
        nsl = slice(gi * NS, (gi + 1) * NS)
        base = gi * HPG
        xh, Bm, Cm, band, e_exp, ecum, dt_exp, X, G, decs, elast, dec_rows, last = _ssd_common(
            gi, x_ref, b_ref, c_ref, tiles, rev, intra)
        H = hs_ref[gi]
        dHn = dh_scr[gi]
        dHnb = dHn.astype(BF16)
        BdH = _dot_nt(Bm, dHnb)
        dX = e_exp * BdH
        eX = e_exp * X
        lanei = lax.broadcasted_iota(jnp.int32, (CH, DTW), 1)
        dA = jnp.zeros((CH, DTW), F32)
        dAT = None
        pb = lax.broadcasted_iota(jnp.int32, (HPG * HD, NS), 0) // HD
        pl_ = lax.broadcasted_iota(jnp.int32, (HPG * HD, NS), 1)
        E = jnp.where(pb + base == pl_, 1.0, 0.0).astype(BF16)
        if intra:
            dY = dy_ref[:, xsl]
            Hb = H.astype(BF16)
            dYs = ecum * dY
            dYsb = dYs.astype(BF16)
            Ys = ecum * _dot_nt(Cm, Hb)
            dG = jnp.zeros((CH, CH), F32)
            subi = lax.broadcasted_iota(jnp.int32, (DTW, CH), 0)
            dAT = jnp.zeros((DTW, CH), F32)
            Xbd = jnp.concatenate([jnp.where(band == h, X, 0.0).astype(BF16) for h in range(HPG)], axis=0)
            dYbd = jnp.concatenate([jnp.where(band == h, dY, 0.0).astype(BF16) for h in range(HPG)], axis=0)
            dMcat = _dot_nt(dY.astype(BF16), Xbd)
            Ms = []
            for h in range(HPG):
                M = G * decs[h]
                dM = dMcat[:, h * CH:(h + 1) * CH]
                W = dM * M
                dG = dG + dM * decs[h]
                Ms.append(M.astype(BF16))
                dA = dA + jnp.where(lanei == base + h, jnp.sum(W, axis=1, keepdims=True), 0.0)
                dAT = dAT + jnp.where(subi == base + h, jnp.sum(W, axis=0, keepdims=True), 0.0)
            dX = dX + _dot_tn(jnp.concatenate(Ms, axis=0), dYbd)
            dGb = dG.astype(BF16)
            dC_ref[:, nsl] = _dot(dGb, Bm) + _dot(dYsb, Hb)
            dB_ref[:, nsl] = _dot_tn(dGb, Cm) + _dot(eX.astype(BF16), dHnb)
            dh_scr[gi] = dec_rows * dHn + _dot_tn(dYsb, Cm)
            dA = dA + _dot_hilo_r(dY * Ys, E)
        else:
            dC_ref[:, nsl] = jnp.zeros((CH, NS), F32)
            dB_ref[:, nsl] = _dot(eX.astype(BF16), dHnb)
            dh_scr[gi] = dec_rows * dHn
        q = _dot_hilo_r(eX * BdH, E)
        r = jnp.sum(dHn * H, axis=1, keepdims=True)
        lane1 = lax.broadcasted_iota(jnp.int32, (1, DTW), 1)
        hdot = jnp.zeros((1, DTW), F32)
        for h in range(HPG):
            hv = jnp.sum(r[h * HD:(h + 1) * HD, :], axis=0, keepdims=True)
            hdot = hdot + jnp.where(lane1 == base + h, hv, 0.0)
        dllast = jnp.sum(q, axis=0, keepdims=True) + elast * hdot
        rowi = lax.broadcasted_iota(jnp.int32, (CH, DTW), 0)
        dxh_ref[:, xsl] = dX * dt_exp
        return dA - q + jnp.where(rowi == last, dllast, 0.0), dAT, _dot_hilo_r(dX * xh, E)

    small = pl.BlockSpec((None, CH, DTW), lambda b, g, sp: (g, row(b, g, sp), 0))
    return _hosted_call(
        body, xch, name="ssd_bwd_rev" if rev else "ssd_bwd", grid=(cfg.Bn, NG // GP, nch),
        in_specs=in_specs + [
            pl.BlockSpec((CH, GP * HPG * HD), lambda b, g, sp: (lat_row(b, g, sp), g)),
            pl.BlockSpec((None, GP, None, HPG * HD, NS), lambda b, g, sp: (b, g, nch - 1 - sp, 0, 0))],
        out_specs=[pl.BlockSpec((CH, GP * HPG * HD), lambda b, g, sp: (row(b, g, sp), g)),
                   pl.BlockSpec((CH, GP * NS), lambda b, g, sp: (row(b, g, sp), g)),
                   pl.BlockSpec((CH, GP * NS), lambda b, g, sp: (row(b, g, sp), g)),
                   small, small],
        out_shape=[jax.ShapeDtypeStruct((cfg.NT, DI), F32),
                   jax.ShapeDtypeStruct((cfg.NT, NG * NS), F32),
                   jax.ShapeDtypeStruct((cfg.NT, NG * NS), F32),
                   jax.ShapeDtypeStruct((NGB, cfg.NT, DTW), F32),
                   jax.ShapeDtypeStruct((NGB, cfg.NT, DTW), F32)],
        scratch_shapes=[pltpu.VMEM((GP, HPG * HD, NS), F32)],
        compiler_params=_cp(3), args=(act, act, act, dtg, cumg, cumTg, dy, hs))


def _shift_rows(v, s, fill, toward_later, rowi):
    n = v.shape[0]
    if toward_later:
        return jnp.where(rowi >= s, pltpu.roll(v, s, 0), fill)
    return jnp.where(rowi < n - s, pltpu.roll(v, n - s, 0), fill)


def _chunk_scan(a, b, carry, later):
    nt = a.shape[0] // 8
    rowi = lax.broadcasted_iota(jnp.int32, (8, a.shape[1]), 0)
    outs = [None] * nt
    for r in (range(nt) if later else range(nt - 1, -1, -1)):
        av = a[r * 8:(r + 1) * 8]
        bv = b[r * 8:(r + 1) * 8]
        for sh in (1, 2, 4):
            a_p = _shift_rows(av, sh, 1.0, later, rowi)
            b_p = _shift_rows(bv, sh, 0.0, later, rowi)
            bv = av * b_p + bv
            av = av * a_p
        h = bv + av * carry
        outs[r] = h
        carry = h[7:8] if later else h[0:1]
    return jnp.concatenate(outs, axis=0), carry


def _lru_gates(u, wa_ref, wi_ref, ba_ref, bi_ref, lam_ref):
    rs, is_ = [], []
    for k in range(LB):
        uk = u[:, k * LBW:(k + 1) * LBW].astype(BF16)
        rs.append(_dot(uk, wa_ref[k].astype(BF16)))
        is_.append(_dot(uk, wi_ref[k].astype(BF16)))
    r = 1.0 / (1.0 + jnp.exp(-(jnp.concatenate(rs, axis=1) + ba_ref[...])))
    ig = _sigmoid(jnp.concatenate(is_, axis=1) + bi_ref[...])
    sp = _softplus(-lam_ref[...])
    la = -LRU_C * r * sp
    a = jnp.exp(la)
    q = (1.0 + a * a) * jnp.tanh(-la)
    return r, ig, sp, la, a, jnp.sqrt(q), lax.rsqrt(q)


def _lru_w_specs(d):
    return [pl.BlockSpec((None, LB, LBW, LBW), lambda b, s: (d, 0, 0, 0)),
            pl.BlockSpec((None, LB, LBW, LBW), lambda b, s: (d, 0, 0, 0)),
            pl.BlockSpec((None, 1, LW), lambda b, s: (d, 0, 0)),
            pl.BlockSpec((None, 1, LW), lambda b, s: (d, 0, 0)),
            pl.BlockSpec((None, 1, LW), lambda b, s: (d, 0, 0))]


def _lru_block_of_step(cfg, rev):
    nbl = cfg.nbl
    if not rev:
        return lambda s: s
    return lambda s: jnp.where(s < 1, 0, 1 + nbl - s)


def _lru_fwd(cfg, act, wa, wi, ba, bi, lam, rev):
    nch, CH = cfg.nbt, cfg.TB
    cmap = _lru_block_of_step(cfg, rev)
    d = 1 if rev else 0
    ucol = (DI + 2 * NG * NS) // LW

    def body(u_ref, wa_ref, wi_ref, ba_ref, bi_ref, lam_ref, h_ref, c_scr):
        s = pl.program_id(1)

        @pl.when(s == 0)
        def _():
            c_scr[...] = jnp.zeros_like(c_scr)

        u = u_ref[...]
        r, ig, sp, la, a, g, _ = _lru_gates(u, wa_ref, wi_ref, ba_ref, bi_ref, lam_ref)
        h, carry = _chunk_scan(a, g * ig * u, c_scr[0:1, :], not rev)
        h_ref[...] = h
        c_scr[0:1, :] = carry

    return pl.pallas_call(
        body, name="lru_fwd_rev" if rev else "lru_fwd", grid=(cfg.Bn, nch),
        in_specs=[pl.BlockSpec((CH, LW), lambda b, s: (b * nch + cmap(s), ucol))] + _lru_w_specs(d),
        out_specs=pl.BlockSpec((CH, LW), lambda b, s: (b * nch + cmap(s), 0)),
        out_shape=jax.ShapeDtypeStruct((cfg.NT, LW), F32),
        scratch_shapes=[pltpu.VMEM((8, LW), F32)],
        compiler_params=_cp(2),
    )(act, wa, wi, ba, bi, lam)


def _lru_bwd(cfg, act, wa, wi, ba, bi, lam, hd, dyl, rev):
    nch, nct, nlt, CH = cfg.nbt, 1, cfg.nbl, cfg.TB
    cmap = _lru_block_of_step(cfg, rev)
    d = 1 if rev else 0
    ucol = (DI + 2 * NG * NS) // LW

    def srow(b, sp):
        return b * nch + cmap(nch - 1 - sp)

    def prev_rows(b, sp):
        s = nch - 1 - sp
        cp = cmap(jnp.maximum(s - 1, 0))
        base = (b * nch + cp) * (CH // 8)
        return base + (0 if rev else CH // 8 - 1)

    def lat_row(b, sp):
        c = cmap(nch - 1 - sp)
        return b * nlt + jnp.maximum(c - nct, 0)

    def body(u_ref, wa_ref, wi_ref, ba_ref, bi_ref, lam_ref, h_ref, hp_ref, dy_ref,
             du_ref, dwa_ref, dwi_ref, vec_ref, c_scr):
        b = pl.program_id(0)
        sp_id = pl.program_id(1)
        s = nch - 1 - sp_id

        @pl.when(sp_id == 0)
        def _():
            c_scr[...] = jnp.zeros_like(c_scr)

        @pl.when(jnp.logical_and(b == 0, sp_id == 0))
        def _():
            dwa_ref[...] = jnp.zeros_like(dwa_ref)
            dwi_ref[...] = jnp.zeros_like(dwi_ref)
            vec_ref[...] = jnp.zeros_like(vec_ref)

        c = cmap(s)
        u = u_ref[...]
        r, ig, spl, la, a, g, ginv = _lru_gates(u, wa_ref, wi_ref, ba_ref, bi_ref, lam_ref)
        dh = jnp.where(c < nct, 0.0, dy_ref[...])
        rowi = lax.broadcasted_iota(jnp.int32, (CH, LW), 0)
        lamv, _ = _chunk_scan(_shift_rows(a, 1, 1.0, rev, rowi), dh, c_scr[0:1, :], rev)
        first = CH - 1 if rev else 0
        c_scr[0:1, :] = (a * lamv)[first:first + 1, :]
        hprow = hp_ref[...][(0 if rev else 7):(1 if rev else 8), :]
        hprow = jnp.where(s > 0, hprow, 0.0)
        h_prev = _shift_rows(h_ref[...], 1, hprow, not rev, rowi)
        da = lamv * h_prev
        db = lamv
        iu = ig * u
        dla = da * a - db * iu * (a * a) * ginv
        dr = dla * (-LRU_C * spl)
        di = db * g * u
        du = db * g * ig
        drp = dr * r * (1.0 - r)
        dip = di * ig * (1.0 - ig)
        dus = []
        for k in range(LB):
            sl = slice(k * LBW, (k + 1) * LBW)
            drk = drp[:, sl].astype(BF16)
            dik = dip[:, sl].astype(BF16)
            uk = u[:, sl].astype(BF16)
            dus.append(_dot_nt(drk, wa_ref[k].astype(BF16)) + _dot_nt(dik, wi_ref[k].astype(BF16)))
            dwa_ref[k] += _dot_tn(uk, drk)
            dwi_ref[k] += _dot_tn(uk, dik)
        du_ref[...] = du + jnp.concatenate(dus, axis=1)
        vec_ref[0:1, :] += jnp.sum(drp, axis=0, keepdims=True)
        vec_ref[1:2, :] += jnp.sum(dip, axis=0, keepdims=True)
        dsp = jnp.sum(dla * (-LRU_C * r), axis=0, keepdims=True)
        vec_ref[2:3, :] += dsp * (-_sigmoid(-lam_ref[...]))

    return pl.pallas_call(
        body, name="lru_bwd_rev" if rev else "lru_bwd", grid=(cfg.Bn, nch),
        in_specs=[pl.BlockSpec((CH, LW), lambda b, sp: (srow(b, sp), ucol))] + _lru_w_specs(d) + [
            pl.BlockSpec((CH, LW), lambda b, sp: (srow(b, sp), 0)),
            pl.BlockSpec((8, LW), lambda b, sp: (prev_rows(b, sp), 0)),
            pl.BlockSpec((CH, LW), lambda b, sp: (lat_row(b, sp), 0))],
        out_specs=[pl.BlockSpec((CH, LW), lambda b, sp: (srow(b, sp), 0)),
                   pl.BlockSpec((LB, LBW, LBW), lambda b, sp: (0, 0, 0)),
                   pl.BlockSpec((LB, LBW, LBW), lambda b, sp: (0, 0, 0)),
                   pl.BlockSpec((8, LW), lambda b, sp: (0, 0))],
        out_shape=[jax.ShapeDtypeStruct((cfg.NT, LW), F32),
                   jax.ShapeDtypeStruct((LB, LBW, LBW), F32),
                   jax.ShapeDtypeStruct((LB, LBW, LBW), F32),
                   jax.ShapeDtypeStruct((8, LW), F32)],
        scratch_shapes=[pltpu.VMEM((8, LW), F32)],
        compiler_params=_cp(2),
    )(act, wa, wi, ba, bi, lam, hd, hd, dyl)


HB = 1024


def _post_ssd_fwd(cfg, y, proj, norm_w):
    TB, nbt, nbl = cfg.TB, cfg.nbt, cfg.nbl
    zc = CONVW // HB

    def body(y_ref, z_ref, w_ref, o_ref):
        u = y_ref[...] * _silu(z_ref[...])
        for gi in range(HB // (DI // NG)):
            sl = slice(gi * 256, (gi + 1) * 256)
            ug = u[:, sl]
            rs = lax.rsqrt(jnp.mean(ug * ug, axis=1, keepdims=True) + RMS_EPS)
            o_ref[:, sl] = (ug * rs * w_ref[:, sl]).astype(BF16)

    def st(b, j, cb):
        return (b * nbt + 1 + j, cb)
    return pl.pallas_call(
        body, name="post_ssd_fwd", grid=(cfg.Bn, nbl, DI // HB),
        in_specs=[pl.BlockSpec((TB, HB), st),
                  pl.BlockSpec((TB, HB), lambda b, j, cb: (b * nbt + 1 + j, zc + cb)),
                  pl.BlockSpec((1, HB), lambda b, j, cb: (0, cb))],
        out_specs=pl.BlockSpec((TB, HB), lambda b, j, cb: (b * nbl + j, cb)),
        out_shape=jax.ShapeDtypeStruct((cfg.N, DI), BF16),
        compiler_params=_cp(3),
    )(y, proj, norm_w)


def _post_ssd_bwd(cfg, dproj, dn, y, act, proj, norm_w):
    TB, nbt, nbl = cfg.TB, cfg.nbt, cfg.nbl
    zc = CONVW // HB

    def body(_, dn_ref, y_ref, xh_ref, z_ref, w_ref, dy_ref, dz_ref, acc_ref):
        b = pl.program_id(1)
        j = pl.program_id(2)

        @pl.when(jnp.logical_and(b == 0, j == 0))
        def _():
            acc_ref[...] = jnp.zeros_like(acc_ref)

        @pl.when(j == 0)
        def _():
            dz_ref[...] = jnp.zeros_like(dz_ref)

        @pl.when(j > 0)
        def _():
            latent(dn_ref, y_ref, xh_ref, z_ref, w_ref, dy_ref, dz_ref, acc_ref)

    def latent(dn_ref, y_ref, xh_ref, z_ref, w_ref, dy_ref, dz_ref, acc_ref):
        xh = xh_ref[...]
        z = z_ref[...]
        y = y_ref[...]
        sg = _sigmoid(z)
        sz = z * sg
        dsz = sg * (1.0 + z * (1.0 - sg))
        u = y * sz
        dout = dn_ref[...]
        for gi in range(HB // (DI // NG)):
            sl = slice(gi * 256, (gi + 1) * 256)
            ug0 = u[:, sl]
            rs = lax.rsqrt(jnp.mean(ug0 * ug0, axis=1, keepdims=True) + RMS_EPS)
            ug = ug0 * rs
            do = dout[:, sl]
            acc_ref[0:1, sl] += jnp.sum(do * ug, axis=0, keepdims=True)
            dug = do * w_ref[:, sl]
            du = rs * (dug - ug * jnp.mean(dug * ug, axis=1, keepdims=True))
            dy = du * sz[:, sl]
            dy_ref[:, sl] = dy
            dz_ref[:, sl] = (du * y[:, sl] * dsz[:, sl]).astype(BF16)
            acc_ref[1:2, sl] += jnp.sum(dy * xh[:, sl], axis=0, keepdims=True)

    def st(cb, b, j):
        return (b * nbt + j, cb)

    def la(cb, b, j):
        return (b * nbl + jnp.maximum(j - 1, 0), cb)
    return pl.pallas_call(
        body, name="post_ssd_bwd", grid=(DI // HB, cfg.Bn, nbt),
        in_specs=[_ANY, pl.BlockSpec((TB, HB), la), pl.BlockSpec((TB, HB), st), pl.BlockSpec((TB, HB), st),
                  pl.BlockSpec((TB, HB), lambda cb, b, j: (b * nbt + j, zc + cb)),
                  pl.BlockSpec((1, HB), lambda cb, b, j: (0, cb))],
        out_specs=[pl.BlockSpec((TB, HB), la),
                   pl.BlockSpec((TB, HB), lambda cb, b, j: (b * nbt + j, zc + cb)),
                   pl.BlockSpec((8, HB), lambda cb, b, j: (0, cb))],
        out_shape=[jax.ShapeDtypeStruct((cfg.N, DI), F32), jax.ShapeDtypeStruct((cfg.NT, PM), BF16),
                   jax.ShapeDtypeStruct((8, DI), F32)],
        input_output_aliases={0: 1},
        compiler_params=_cp(3),
    )(dproj, dn, y, act, proj, norm_w)


def _post_lru_fwd(cfg, hf, hb, proj):
    TB, nbt, nbl = cfg.TB, cfg.nbt, cfg.nbl
    gc = (CONVW + DI) // HB

    def body(hf_ref, hb_ref, g_ref, o_ref):
        o_ref[...] = ((hf_ref[...] + hb_ref[...]) * _gelu(g_ref[...])).astype(BF16)

    st = pl.BlockSpec((TB, HB), lambda b, j: (b * nbt + 1 + j, 0))
    return pl.pallas_call(
        body, name="post_lru_fwd", grid=(cfg.Bn, nbl),
        in_specs=[st, st, pl.BlockSpec((TB, HB), lambda b, j: (b * nbt + 1 + j, gc))],
        out_specs=pl.BlockSpec((TB, HB), lambda b, j: (b * nbl + j, 0)),
        out_shape=jax.ShapeDtypeStruct((cfg.N, LW), BF16),
        compiler_params=_cp(2),
    )(hf, hb, proj)


def _post_lru_bwd(cfg, dproj, dv, hf, hb, proj):
    TB, nbt, nbl = cfg.TB, cfg.nbt, cfg.nbl
    gc = (CONVW + DI) // HB

    def body(_, dv_ref, hf_ref, hb_ref, g_ref, dy_ref, dg_ref):
        j = pl.program_id(1)

        @pl.when(j == 0)
        def _():
            dg_ref[...] = jnp.zeros_like(dg_ref)

        @pl.when(j > 0)
        def _():
            gt = g_ref[...]
            dvv = dv_ref[...]
            gl, dgl = _gelu_and_grad(gt)
            dy_ref[...] = dvv * gl
            dg_ref[...] = (dvv * (hf_ref[...] + hb_ref[...]) * dgl).astype(BF16)

    st = pl.BlockSpec((TB, HB), lambda b, j: (b * nbt + j, 0))
    la = pl.BlockSpec((TB, HB), lambda b, j: (b * nbl + jnp.maximum(j - 1, 0), 0))
    gcol = pl.BlockSpec((TB, HB), lambda b, j: (b * nbt + j, gc))
    return pl.pallas_call(
        body, name="post_lru_bwd", grid=(cfg.Bn, nbt),
        in_specs=[_ANY, la, st, st, gcol],
        out_specs=[la, gcol],
        out_shape=[jax.ShapeDtypeStruct((cfg.N, LW), F32), jax.ShapeDtypeStruct((cfg.NT, PM), BF16)],
        input_output_aliases={0: 1},
        compiler_params=_cp(2),
    )(dproj, dv, hf, hb, proj)


def _merge_fwd(cfg, proj, b_gate, br_ssd, br_lru):
    TB, nbt, nbl = cfg.TB, cfg.nbt, cfg.nbl
    mc = (CONVW + DI + LW) // HB

    def body(ms_ref, ml_ref, bg_ref, bs_ref, bl_ref, o_ref):
        gs = _sigmoid(ms_ref[...] + bg_ref[:, :D])
        gl = _sigmoid(ml_ref[...] + bg_ref[:, D:])
        o_ref[...] = (gs * bs_ref[...] + gl * bl_ref[...]).astype(BF16)

    la = pl.BlockSpec((TB, D), lambda b, j: (b * nbl + j, 0))
    return pl.pallas_call(
        body, name="merge_fwd", grid=(cfg.Bn, nbl),
        in_specs=[pl.BlockSpec((TB, HB), lambda b, j: (b * nbt + 1 + j, mc)),
                  pl.BlockSpec((TB, HB), lambda b, j: (b * nbt + 1 + j, mc + 1)),
                  pl.BlockSpec((1, 2 * D), lambda b, j: (0, 0)), la, la],
        out_specs=la,
        out_shape=jax.ShapeDtypeStruct((cfg.N, D), BF16),
        compiler_params=_cp(2),
    )(proj, proj, b_gate, br_ssd, br_lru)


def _merge_bwd(cfg, dmix, proj, b_gate, br_ssd, br_lru):
    TB, nbt, nbl = cfg.TB, cfg.nbt, cfg.nbl
    mc = (CONVW + DI + LW) // HB

    def body(dm_ref, ms_ref, ml_ref, bg_ref, bs_ref, bl_ref, ds_ref, dl_ref, dmg_ref, acc_ref):
        b = pl.program_id(0)
        j = pl.program_id(1)

        @pl.when(jnp.logical_and(b == 0, j == 0))
        def _():
            acc_ref[...] = jnp.zeros_like(acc_ref)

        @pl.when(j == 0)
        def _():
            dmg_ref[...] = jnp.zeros_like(dmg_ref)

        @pl.when(j > 0)
        def _():
            latent(dm_ref, ms_ref, ml_ref, bg_ref, bs_ref, bl_ref, ds_ref, dl_ref, dmg_ref, acc_ref)

    def latent(dm_ref, ms_ref, ml_ref, bg_ref, bs_ref, bl_ref, ds_ref, dl_ref, dmg_ref, acc_ref):
        dm = dm_ref[...]
        gs = _sigmoid(ms_ref[...] + bg_ref[:, :D])
        gl = _sigmoid(ml_ref[...] + bg_ref[:, D:])
        ds_ref[...] = (dm * gs).astype(BF16)
        dl_ref[...] = (dm * gl).astype(BF16)
        dps = dm * bs_ref[...] * gs * (1.0 - gs)
        dpl = dm * bl_ref[...] * gl * (1.0 - gl)
        dmg_ref[:, :D] = dps.astype(BF16)
        dmg_ref[:, D:] = dpl.astype(BF16)
        acc_ref[0:1, :D] += jnp.sum(dps, axis=0, keepdims=True)
        acc_ref[0:1, D:] += jnp.sum(dpl, axis=0, keepdims=True)

    la = pl.BlockSpec((TB, D), lambda b, j: (b * nbl + jnp.maximum(j - 1, 0), 0))
    return pl.pallas_call(
        body, name="merge_bwd", grid=(cfg.Bn, nbt),
        in_specs=[la, pl.BlockSpec((TB, HB), lambda b, j: (b * nbt + j, mc)),
                  pl.BlockSpec((TB, HB), lambda b, j: (b * nbt + j, mc + 1)),
                  pl.BlockSpec((1, 2 * D), lambda b, j: (0, 0)), la, la],
        out_specs=[la, la, pl.BlockSpec((TB, 2 * D), lambda b, j: (b * nbt + j, mc // 2)),
                   pl.BlockSpec((8, 2 * D), lambda b, j: (0, 0))],
        out_shape=[jax.ShapeDtypeStruct((cfg.N, D), BF16), jax.ShapeDtypeStruct((cfg.N, D), BF16),
                   jax.ShapeDtypeStruct((cfg.NT, PM), BF16), jax.ShapeDtypeStruct((8, 2 * D), F32)],
        compiler_params=_cp(2),
    )(dmix, proj, proj, b_gate, br_ssd, br_lru)


def _resid1_fwd(cfg, x2, x_mix, gate1, shift2, scale2, ln1_g, ln1_b):
    TB, nbt, nbl = cfg.TB, cfg.nbt, cfg.nbl

    def body(x_ref, xm_ref, g1_ref, sh_ref, sc_ref, lg_ref, lb_ref, x1_ref, h2_ref):
        r1 = ALPHA * x_ref[...] + g1_ref[...] * xm_ref[...]
        xh, _ = _ln(r1)
        x1 = xh * lg_ref[...] + lb_ref[...]
        x1_ref[...] = x1
        xh2, _ = _ln(x1)
        h2_ref[...] = (xh2 * (1.0 + sc_ref[...]) + sh_ref[...]).astype(BF16)

    la = pl.BlockSpec((TB, D), lambda b, j: (b * nbl + j, 0))
    ex = pl.BlockSpec((None, 1, D), lambda b, j: (b, 0, 0))
    vec = pl.BlockSpec((1, D), lambda b, j: (0, 0))
    return pl.pallas_call(
        body, name="resid1_fwd", grid=(cfg.Bn, nbl),
        in_specs=[la, la, ex, ex, ex, vec, vec],
        out_specs=[la, la],
        out_shape=[jax.ShapeDtypeStruct((cfg.N, D), F32), jax.ShapeDtypeStruct((cfg.N, D), BF16)],
        compiler_params=_cp(2),
    )(x2, x_mix, gate1, shift2, scale2, ln1_g, ln1_b)


def _resid1_bwd(cfg, dh2, dx1p, x1, x2, x_mix, gate1, scale2, ln1_g):
    TB, nbt, nbl = cfg.TB, cfg.nbt, cfg.nbl

    def body(dh2_ref, dx1p_ref, x1_ref, x_ref, xm_ref, g1_ref, sc_ref, lg_ref,
             dxm_ref, dxp_ref, ex_ref, gl_ref):
        b = pl.program_id(0)
        j = pl.program_id(1)

        @pl.when(j == 0)
        def _():
            ex_ref[...] = jnp.zeros_like(ex_ref)

        @pl.when(jnp.logical_and(b == 0, j == 0))
        def _():
            gl_ref[...] = jnp.zeros_like(gl_ref)

        dh2 = dh2_ref[...]
        xh2, rs2 = _ln(x1_ref[...])
        ex_ref[0:1, :] += jnp.sum(dh2, axis=0, keepdims=True)
        ex_ref[1:2, :] += jnp.sum(dh2 * xh2, axis=0, keepdims=True)
        dx1 = dx1p_ref[...] + _ln_bwd(dh2 * (1.0 + sc_ref[...]), xh2, rs2)
        xm = xm_ref[...]
        g1 = g1_ref[...]
        r1 = ALPHA * x_ref[...] + g1 * xm
        xh1, rs1 = _ln(r1)
        gl_ref[0:1, :] += jnp.sum(dx1 * xh1, axis=0, keepdims=True)
        gl_ref[1:2, :] += jnp.sum(dx1, axis=0, keepdims=True)
        dr1 = _ln_bwd(dx1 * lg_ref[...], xh1, rs1)
        ex_ref[2:3, :] += jnp.sum(dr1 * xm, axis=0, keepdims=True)
        dxm_ref[...] = (dr1 * g1).astype(BF16)
        dxp_ref[...] = ALPHA * dr1

    la = pl.BlockSpec((TB, D), lambda b, j: (b * nbl + j, 0))
    ex = pl.BlockSpec((None, 1, D), lambda b, j: (b, 0, 0))
    vec = pl.BlockSpec((1, D), lambda b, j: (0, 0))
    return pl.pallas_call(
        body, name="resid1_bwd", grid=(cfg.Bn, nbl),
        in_specs=[la, la, la, la, la, ex, ex, vec],
        out_specs=[la, la, pl.BlockSpec((None, 8, D), lambda b, j: (b, 0, 0)),
                   pl.BlockSpec((8, D), lambda b, j: (0, 0))],
        out_shape=[jax.ShapeDtypeStruct((cfg.N, D), BF16), jax.ShapeDtypeStruct((cfg.N, D), F32),
                   jax.ShapeDtypeStruct((cfg.Bn, 8, D), F32), jax.ShapeDtypeStruct((8, D), F32)],
        compiler_params=_cp(2),
    )(dh2, dx1p, x1, x2, x_mix, gate1, scale2, ln1_g)


def _final_fwd_bwd(cfg, x1, mlp, b2, gate2, ln2_g, ln2_b, target):
    TB, nbl = cfg.TB, cfg.nbl

    def body(x1_ref, m_ref, b2_ref, g2_ref, lg_ref, lb_ref, t_ref, dm_ref, dx_ref, ex_ref, gl_ref):
        b = pl.program_id(0)
        j = pl.program_id(1)

        @pl.when(j == 0)
        def _():
            ex_ref[...] = jnp.zeros_like(ex_ref)

        @pl.when(jnp.logical_and(b == 0, j == 0))
        def _():
            gl_ref[...] = jnp.zeros_like(gl_ref)

        mv = m_ref[...] + b2_ref[...]
        g2 = g2_ref[...]
        r2 = ALPHA * x1_ref[...] + g2 * mv
        xh, rs = _ln(r2)
        lg = lg_ref[...]
        x2 = xh * lg + lb_ref[...]
        err = x2 - t_ref[...]
        ls = jnp.sum(jnp.sum(err * err, axis=1, keepdims=True), axis=0, keepdims=True) * (0.5 / D)
        gl_ref[3:4, :] += ls
        dx2 = err * (1.0 / D)
        gl_ref[0:1, :] += jnp.sum(dx2 * xh, axis=0, keepdims=True)
        gl_ref[1:2, :] += jnp.sum(dx2, axis=0, keepdims=True)
        dr2 = _ln_bwd(dx2 * lg, xh, rs)
        ex_ref[0:1, :] += jnp.sum(dr2 * mv, axis=0, keepdims=True)
        dmv = dr2 * g2
        gl_ref[2:3, :] += jnp.sum(dmv, axis=0, keepdims=True)
        dm_ref[...] = dmv.astype(BF16)
        dx_ref[...] = ALPHA * dr2

    la = pl.BlockSpec((TB, D), lambda b, j: (b * nbl + j, 0))
    ex = pl.BlockSpec((None, 1, D), lambda b, j: (b, 0, 0))
    vec = pl.BlockSpec((1, D), lambda b, j: (0, 0))
    return pl.pallas_call(
        body, name="final_fwd_bwd", grid=(cfg.Bn, nbl),
        in_specs=[la, la, vec, ex, vec, vec, la],
        out_specs=[la, la, pl.BlockSpec((None, 8, D), lambda b, j: (b, 0, 0)),
                   pl.BlockSpec((8, D), lambda b, j: (0, 0))],
        out_shape=[jax.ShapeDtypeStruct((cfg.N, D), BF16), jax.ShapeDtypeStruct((cfg.N, D), F32),
                   jax.ShapeDtypeStruct((cfg.Bn, 8, D), F32), jax.ShapeDtypeStruct((8, D), F32)],
        compiler_params=_cp(2),
    )(x1, mlp, b2, gate2, ln2_g, ln2_b, target)


def _ln_mod_bwd(cfg, dh_a, dh_b, ctx2, x2, scale_tab, dxp):
    TB, nbt, nbl = cfg.TB, cfg.nbt, cfg.nbl

    def body(da_ref, db_ref, c_ref, x_ref, sc_ref, dxp_ref, gx_ref, acc_ref):
        j = pl.program_id(1)

        @pl.when(j <= 1)
        def _():
            acc_ref[...] = jnp.zeros_like(acc_ref)

        dh = da_ref[...] + db_ref[...]
        xhat, rs = _ln(jnp.where(j == 0, c_ref[...], x_ref[...]))
        acc_ref[0:1, :] += jnp.sum(dh, axis=0, keepdims=True)
        acc_ref[1:2, :] += jnp.sum(dh * xhat, axis=0, keepdims=True)
        gx_ref[...] = dxp_ref[...] + _ln_bwd(dh * (1.0 + sc_ref[...]), xhat, rs)

    st = pl.BlockSpec((TB, D), lambda b, j: (b * nbt + j, 0))
    la = pl.BlockSpec((TB, D), lambda b, j: (b * nbl + jnp.maximum(j - 1, 0), 0))
    return pl.pallas_call(
        body, name="ln_mod_bwd", grid=(cfg.Bn, nbt),
        in_specs=[st, st, pl.BlockSpec((TB, D), lambda b, j: (b, 0)), la,
                  pl.BlockSpec((None, 1, D), lambda b, j: (2 * b + jnp.minimum(j, 1), 0, 0)), la],
        out_specs=[la, pl.BlockSpec((None, 8, D), lambda b, j: (2 * b + jnp.minimum(j, 1), 0, 0))],
        out_shape=[jax.ShapeDtypeStruct((cfg.N, D), F32), jax.ShapeDtypeStruct((2 * cfg.Bn, 8, D), F32)],
        compiler_params=_cp(2),
    )(dh_a, dh_b, ctx2, x2, scale_tab, dxp)


def _perm_w_in(w_in):
    w_main = jnp.concatenate([w_in[:, 0:3072], w_in[:, 4160:5184], w_in[:, 3136:4160], w_in[:, 5184:10304]], axis=1)
    w_dt = jnp.pad(w_in[:, 3072:3136], ((0, 0), (0, DTW - 2 * NH)))
    return w_main, w_dt


def _unperm_w_in(dw_main, dw_dt):
    return jnp.concatenate([dw_main[:, 0:3072], dw_dt[:, :2 * NH], dw_main[:, 4096:5120],
                            dw_main[:, 3072:4096], dw_main[:, 5120:]], axis=1)


def _unpack_rest(rest_all):
    out, off = {}, 0
    for n, shp, axis in _BIG[1:]:
        shard_shape = (shp[0] // NDEV, shp[1]) if axis == 0 else (shp[0], shp[1] // NDEV)
        r = math.prod(shard_shape) // 1024
        out[n] = _from_slots(rest_all[:, off:off + r, :].reshape((NDEV,) + shard_shape), axis)
        off += r
    return out


def _local_step(cfg, x, ctx, target, m, mc, W, w_in_shard, rest_payload):
    Bn, T, Tc = cfg.Bn, cfg.T, cfg.Tc
    NT, N = cfg.NT, cfg.N
    ctx2, x2 = ctx.reshape(Bn * Tc, D), x.reshape(N, D)
    mch = [m[:, i * D:(i + 1) * D] for i in range(NMOD)]
    ctx_sh = jnp.broadcast_to(mc[None, :D], (Bn, D))
    ctx_sc = jnp.broadcast_to(mc[None, D:], (Bn, D))
    shift_tab = jnp.stack([ctx_sh, mch[0]], axis=1).reshape(2 * Bn, 1, D)
    scale_tab = jnp.stack([ctx_sc, mch[1]], axis=1).reshape(2 * Bn, 1, D)
    gate1 = mch[2].reshape(Bn, 1, D)
    shift2 = mch[3].reshape(Bn, 1, D)
    scale2 = mch[4].reshape(Bn, 1, D)
    gate2 = mch[5].reshape(Bn, 1, D)

    conv_w = jnp.concatenate([W["ssd_conv_w"], W["lru_conv_w"]], axis=1)
    conv_b = jnp.concatenate([W["ssd_conv_b"], W["lru_conv_b"]], axis=1)
    dt_bias = jnp.pad(W["ssd_dt_bias"].reshape(1, 2 * NH), ((0, 0), (0, DTW - 2 * NH)))
    a_log = jnp.pad(W["ssd_a_log"].reshape(1, 2 * NH), ((0, 0), (0, DTW - 2 * NH)))
    dvec = jnp.repeat(W["ssd_d"].reshape(NH), HD).reshape(1, DI)
    lba = W["lru_ba"].reshape(2, 1, LW)
    lbi = W["lru_bi"].reshape(2, 1, LW)
    llam = W["lru_lambda"].reshape(2, 1, LW)

    h, w_in_all = _ln_mod_fwd(cfg, ctx2, x2, shift_tab, scale_tab, xch=(w_in_shard, TWO_LEVEL))
    w_main, w_dt = _perm_w_in(_from_slots(w_in_all, 1))
    proj, rest_all = _mm(h, w_main, "nn", "mm_proj", tm=1024, tn=2048, tk=1024, xch=(rest_payload, True))
    W = dict(W, w_main=w_main, w_dt=w_dt, **_unpack_rest(rest_all))
    dt_raw = _mm(h, W["w_dt"], "nn", "mm_dt", tm=512, tn=DTW, tk=1024)
    dt, dtg, cumg, cumTg = _dt_fwd(cfg, dt_raw, dt_bias, a_log)
    act, sgrad = _conv_fwd(cfg, proj, conv_w, conv_b)
    y_f, hs_f = _ssd_fwd(cfg, act, dtg, cumg, cumTg, False)
    y, hs_b = _ssd_fwd(cfg, act, dtg, cumg, cumTg, True, y_other=y_f, dvec=dvec)
    hss = [hs_f, hs_b]
    hls = [_lru_fwd(cfg, act, W["lru_wa"], W["lru_wi"], lba, lbi, llam, rev) for rev in (False, True)]
    nssd = _post_ssd_fwd(cfg, y, proj, W["ssd_norm_w"])
    vlru = _post_lru_fwd(cfg, hls[0], hls[1], proj)
    br_ssd = _mm(nssd, W["w_br_ssd"], "nn", "mm_br_ssd", tm=1024, tn=1024, tk=1024)
    br_lru = _mm(vlru, W["w_br_lru"], "nn", "mm_br_lru", tm=1024, tn=1024, tk=1024)
    mix = _merge_fwd(cfg, proj, W["b_gate"], br_ssd, br_lru)
    x_mix = _mm(mix, W["w_out"], "nn", "mm_out", tm=1024, tn=1024, tk=1024)
    x1, h2 = _resid1_fwd(cfg, x2, x_mix, gate1, shift2, scale2, W["ln1_g"], W["ln1_b"])
    a1, actm = _mm_mlp1(h2, W["w_mlp1"], W["b_mlp1"])
    mlp = _mm(actm, W["w_mlp2"], "nn", "mm_mlp2", tm=1024, tn=1024, tk=2048)
    dmlp, dx1p, ex2, gl2 = _final_fwd_bwd(cfg, x1, mlp, W["b_mlp2"], gate2, W["ln2_g"], W["ln2_b"],
                                          target.reshape(N, D))

    g = {}
    g["ln2_g"], g["ln2_b"], g["b_mlp2"] = gl2[0:1], gl2[1:2], gl2[2:3]
    loss_partial = gl2[3, 0]
    gw = {}
    gw["w_mlp2"] = _mm(actm, dmlp, "tn", "mm_dw_mlp2", BF16, tm=1024, tn=1024, tk=1024)
    da1, accb1 = _mm_dact(dmlp, W["w_mlp2"], a1)
    g["b_mlp1"] = accb1[0:1]
    dh2 = _mm(da1, W["w_mlp1"], "nt", "mm_dh2", tm=1024, tn=1024, tk=2048)
    gw["w_mlp1"] = _mm(h2, da1, "tn", "mm_dw_mlp1", BF16, tm=1024, tn=1024, tk=1024)
    dx_mix, dxp, ex1, gl1 = _resid1_bwd(cfg, dh2, dx1p, x1, x2, x_mix, gate1, scale2, W["ln1_g"])
    g["ln1_g"], g["ln1_b"] = gl1[0:1], gl1[1:2]
    dmix = _mm(dx_mix, W["w_out"], "nt", "mm_dmix", tm=1024, tn=1024, tk=1024)
    gw["w_out"] = _mm(mix, dx_mix, "tn", "mm_dw_out", BF16, tm=1024, tn=1024, tk=1024)
    dbs, dbl, dproj, accg = _merge_bwd(cfg, dmix, proj, W["b_gate"], br_ssd, br_lru)
    g["b_gate"] = accg[0:1]
    dnssd = _mm(dbs, W["w_br_ssd"], "nt", "mm_dnssd", tm=1024, tn=1024, tk=1024)
    gw["w_br_ssd"] = _mm(nssd, dbs, "tn", "mm_dw_br_ssd", BF16, tm=1024, tn=1024, tk=1024)
    dvlru = _mm(dbl, W["w_br_lru"], "nt", "mm_dvlru", tm=1024, tn=1024, tk=1024)
    gw["w_br_lru"] = _mm(vlru, dbl, "tn", "mm_dw_br_lru", BF16, tm=1024, tn=1024, tk=1024)
    dy, dproj, accs = _post_ssd_bwd(cfg, dproj, dnssd, y, act, proj, W["ssd_norm_w"])
    g["ssd_norm_w"] = accs[0:1]
    dD_cols = accs[1:2]
    dyl, dproj = _post_lru_bwd(cfg, dproj, dvlru, hls[0], hls[1], proj)

    rest_slots = jnp.concatenate([_to_slots(gw[n], axis).reshape(NDEV, -1, 1024) for n, _, axis in _BIG[1:]], axis=1)
    xres = {}
    dxh, dBs, dCs, dAs, dxxs, dus = [], [], [], [], [], []
    dwas, dwis, lvecs = [], [], []
    for i, rev in enumerate((False, True)):
        if i == 0:
            o, xres["rs_rest"] = _ssd_bwd(cfg, act, dtg, cumg, cumTg, hss[i], dy, rev, xch=(rest_slots, False))
        else:
            o = _ssd_bwd(cfg, act, dtg, cumg, cumTg, hss[i], dy, rev)
        dxh.append(o[0]); dBs.append(o[1]); dCs.append(o[2]); dAs.append(o[3]); dxxs.append(o[4])
        du, dwa, dwi, lv = _lru_bwd(cfg, act, W["lru_wa"], W["lru_wi"], lba, lbi, llam, hls[i], dyl, rev)
        dus.append(du); dwas.append(dwa); dwis.append(dwi); lvecs.append(lv)
    lru_payload = jnp.stack([jnp.stack(dwas), jnp.stack(dwis)]).reshape(-1, 1024)
    g["lru_ba"] = jnp.stack([lvecs[0][0], lvecs[1][0]])
    g["lru_bi"] = jnp.stack([lvecs[0][1], lvecs[1][1]])
    g["lru_lambda"] = jnp.stack([lvecs[0][2], lvecs[1][2]])

    ddt_raw, accdt = _dt_bwd(cfg, dAs, dxxs, dt_raw, dt, dt_bias, a_log)
    g["ssd_a_log"] = accdt[0, :2 * NH].reshape(2, NH)
    g["ssd_dt_bias"] = accdt[1, :2 * NH].reshape(2, NH)

    (dproj, accx), xres["ag_lru"] = _conv_bwd(cfg, "conv_bwd_x", dproj, proj, conv_w, sgrad, [dxh[0], dxh[1]], 0, DI,
                                              skip=(dy, dvec), xch=(lru_payload, True))
    dproj, accB = _conv_bwd(cfg, "conv_bwd_b", dproj, proj, conv_w, sgrad, [dBs[0], dBs[1]], DI, NG * NS)
    dproj, accC = _conv_bwd(cfg, "conv_bwd_c", dproj, proj, conv_w, sgrad, [dCs[0], dCs[1]], DI + NG * NS, NG * NS)
    dproj, accl = _conv_bwd(cfg, "conv_bwd_lru", dproj, proj, conv_w, None, [dus[0], dus[1]], DI + 2 * NG * NS, LW)
    accssd = jnp.concatenate([accx, accB, accC], axis=1)
    g["ssd_conv_w"], g["ssd_conv_b"] = accssd[0:4], accssd[4:5]
    g["lru_conv_w"], g["lru_conv_b"] = accl[0:4], accl[4:5]
    dw_main = _mm(h, dproj, "tn", "mm_dw_main", BF16, tm=1024, tn=2048, tk=1024)
    dw_dt = _mm(h, ddt_raw, "tn", "mm_dw_dt", BF16, tm=1024, tn=DTW, tk=512)
    w_in_slots = _to_slots(_unperm_w_in(dw_main, dw_dt), 1)
    dh_a, xres["rs_w_in"] = _mm(dproj, W["w_main"], "nt", "mm_dh_main", tm=1024, tn=1024, tk=2048,
                                xch=(w_in_slots, False))
    dh_b = _mm(ddt_raw, W["w_dt"], "nt", "mm_dh_dt", tm=512, tn=1024, tk=DTW)
    grad_x, acct = _ln_mod_bwd(cfg, dh_a, dh_b, ctx2, x2, scale_tab, dxp)
    acct = acct.reshape(Bn, 2, 8, D)
    dm = jnp.concatenate([acct[:, 1, 0], acct[:, 1, 1], ex1[:, 2], ex1[:, 0], ex1[:, 1], ex2[:, 0]], axis=1)
    dmc = jnp.concatenate([acct[:, 0, 0], acct[:, 0, 1]], axis=1)
    g["ssd_d_cols"] = dD_cols
    return loss_partial, grad_x.reshape(Bn, T, D), g, dm, dmc, xres


MESH = pl.DeviceIdType.MESH
_HBM = pl.BlockSpec(memory_space=pltpu.HBM)


def _me():
    return 4 * lax.axis_index("x") + 2 * lax.axis_index("y") + lax.axis_index("c")


def _peer(k):
    px = (lax.axis_index("x") + ((k >> 2) & 1)) % 2
    py = (lax.axis_index("y") + ((k >> 1) & 1)) % 2
    pc = (lax.axis_index("c") + (k & 1)) % 2
    return (px, py, pc), 4 * px + 2 * py + pc


def _xchg_copies(x_ref, o_ref, send_sems, recv_sems, loc_sem, gather):
    me = _me()
    src_me = x_ref if gather else x_ref.at[me]
    loc = pltpu.make_async_copy(src_me, o_ref.at[me], loc_sem)
    sends, recvs = [], []
    for k in range(1, NDEV):
        peer, pid = _peer(k)
        sends.append(pltpu.make_async_remote_copy(
            src_ref=x_ref if gather else x_ref.at[pid], dst_ref=o_ref.at[me],
            send_sem=send_sems.at[k - 1], recv_sem=recv_sems.at[k - 1],
            device_id=peer, device_id_type=MESH))
        recvs.append(pltpu.make_async_remote_copy(
            src_ref=src_me, dst_ref=o_ref.at[pid],
            send_sem=send_sems.at[k - 1], recv_sem=recv_sems.at[k - 1],
            device_id=peer, device_id_type=MESH))
    return loc, sends, recvs


def _xchg_start(*refs, gather):
    loc, sends, _ = _xchg_copies(*refs, gather)
    loc.start()
    for cp in sends:
        cp.start()


def _xchg_wait(*refs, gather):
    loc, sends, recvs = _xchg_copies(*refs, gather)
    for cp in recvs:
        cp.wait_recv()
    for cp in sends:
        cp.wait_send()
    loc.wait()


_XCHG_SCRATCH = [pltpu.SemaphoreType.DMA((NDEV - 1,)), pltpu.SemaphoreType.DMA((NDEV - 1,)), pltpu.SemaphoreType.DMA]


def _xchg_out_shape(x, gather):
    return jax.ShapeDtypeStruct((NDEV,) + tuple(x.shape if gather else x.shape[1:]), x.dtype)


def _exchange(x, name, gather):
    def body(x_ref, o_ref, send_sems, recv_sems, loc_sem):
        _xchg_start(x_ref, o_ref, send_sems, recv_sems, loc_sem, gather=gather)
        _xchg_wait(x_ref, o_ref, send_sems, recv_sems, loc_sem, gather=gather)

    return pl.pallas_call(
        body, name=name, out_shape=_xchg_out_shape(x, gather),
        in_specs=[_HBM], out_specs=_HBM, scratch_shapes=_XCHG_SCRATCH,
    )(x)


TWO_LEVEL = "two_level"


def _two_level_copies(x_ref, o_ref, send_sems, recv_sems, loc_sem):
    mx, my, mc = lax.axis_index("x"), lax.axis_index("y"), lax.axis_index("c")
    me, sibling = (mx, my, mc), (mx, my, 1 - mc)
    chips = [(1 - mx, my), (mx, 1 - my), (1 - mx, 1 - my)]

    def slot(px, py, pc):
        return o_ref.at[4 * px + 2 * py + pc]

    def copy(k, block, to, src=None):
        return pltpu.make_async_remote_copy(
            src_ref=slot(*block) if src is None else src, dst_ref=slot(*block),
            send_sem=send_sems.at[k], recv_sem=recv_sems.at[k], device_id=to, device_id_type=MESH)

    mine = pltpu.make_async_copy(x_ref, slot(*me), loc_sem)
    first = [copy(0, me, sibling, src=x_ref)] + [copy(1 + j, me, (*chip, mc), src=x_ref) for j, chip in enumerate(chips)]
    passed = [copy(4 + j, (*chip, mc), sibling) for j, chip in enumerate(chips)]
    landed = [copy(1 + j, (*chip, mc), me) for j, chip in enumerate(chips)]
    from_sibling = [copy(0, sibling, me)] + [copy(4 + j, (*chip, 1 - mc), me) for j, chip in enumerate(chips)]
    return mine, first, passed, landed, from_sibling


def _two_level_start(*refs):
    mine, first, _, _, _ = _two_level_copies(*refs)
    mine.start()
    for cp in first:
        cp.start()


def _two_level_finish(*refs):
    mine, first, passed, landed, from_sibling = _two_level_copies(*refs)
    for cp, fwd in zip(landed, passed):
        cp.wait_recv()
        fwd.start()
    for cp in from_sibling:
        cp.wait_recv()
    for cp in first + passed:
        cp.wait_send()
    mine.wait()


def _hosted_call(body, xch, *, name, grid, in_specs, out_specs, out_shape, scratch_shapes, compiler_params, args,
                 aliases=None):
    aliases = aliases or {}
    if xch is None:
        return pl.pallas_call(body, name=name, grid=grid, in_specs=in_specs, out_specs=out_specs,
                              out_shape=out_shape, scratch_shapes=scratch_shapes, input_output_aliases=aliases,
                              compiler_params=compiler_params)(*args)
    xv, gather = xch
    n_in, n_out, n_scr = len(in_specs), len(out_specs), len(scratch_shapes)

    def wrapped(*refs):
        ins = refs[:n_in]
        x_ref = refs[n_in]
        outs = refs[n_in + 1:n_in + 1 + n_out]
        o_ref = refs[n_in + 1 + n_out]
        scr = refs[n_in + 2 + n_out:]
        own, sems = scr[:n_scr], scr[n_scr:]
        first = functools.reduce(jnp.logical_and, [pl.program_id(a) == 0 for a in range(len(grid))])
        last = functools.reduce(jnp.logical_and, [pl.program_id(a) == grid[a] - 1 for a in range(len(grid))])

        @pl.when(first)
        def _():
            if gather == TWO_LEVEL:
                _two_level_start(x_ref, o_ref, *sems)
            else:
                _xchg_start(x_ref, o_ref, *sems, gather=gather)

        body(*ins, *outs, *own)

        @pl.when(last)
        def _():
            if gather == TWO_LEVEL:
                _two_level_finish(x_ref, o_ref, *sems)
            else:
                _xchg_wait(x_ref, o_ref, *sems, gather=gather)

    res = pl.pallas_call(
        wrapped, name=name, grid=grid, in_specs=list(in_specs) + [_HBM], out_specs=list(out_specs) + [_HBM],
        out_shape=list(out_shape) + [_xchg_out_shape(xv, gather)],
        scratch_shapes=list(scratch_shapes) + _XCHG_SCRATCH, input_output_aliases=aliases,
        compiler_params=compiler_params,
    )(*args, xv)
    return list(res[:n_out]), res[n_out]


def _row_tile(R, cap, mult=8):
    best = mult
    t = mult
    while t <= min(R, cap):
        if R % t == 0:
            best = t
        t += mult
    assert R % best == 0, R
    return best


def _sum_slots(x, name, xch=None):
    _, R, C = x.shape
    tr = _row_tile(R, 256, 16 if x.dtype == BF16 else 8)

    def body(x_ref, o_ref):
        o_ref[...] = _slot_sum(x_ref)

    res = _hosted_call(
        body, xch, name=name, grid=(R // tr,),
        in_specs=[pl.BlockSpec((NDEV, tr, C), lambda i: (0, i, 0))],
        out_specs=[pl.BlockSpec((tr, C), lambda i: (i, 0))],
        out_shape=[jax.ShapeDtypeStruct((R, C), F32)],
        scratch_shapes=[], compiler_params=_cp(1), args=(x,))
    if xch is None:
        return res[0]
    return res[0][0], res[1]


def _slot_sum(x_ref):
    acc = x_ref[0].astype(F32)
    for i in range(1, NDEV):
        acc = acc + x_ref[i].astype(F32)
    return acc


def _sum_adamw(slots, w, m, v, name, xch=None):
    _, R, C = slots.shape
    tr = _row_tile(R, 128, 16 if slots.dtype == BF16 else 8)

    def body(x_ref, w_ref, m_ref, v_ref, g_ref, d_ref, nm_ref, nv_ref):
        g_ref[...] = _slot_sum(x_ref)
        _adamw_update(w_ref, g_ref, m_ref, v_ref, d_ref, nm_ref, nv_ref)

    blk = pl.BlockSpec((tr, C), lambda i: (i, 0))
    res = _hosted_call(
        body, xch, name=name, grid=(R // tr,),
        in_specs=[pl.BlockSpec((NDEV, tr, C), lambda i: (0, i, 0)), blk, blk, blk],
        out_specs=[blk] * 4, out_shape=[jax.ShapeDtypeStruct((R, C), F32)] * 4,
        scratch_shapes=[], compiler_params=_cp(1), args=(slots, w, m, v))
    if xch is None:
        return res
    return res[0], res[1]


def _adamw_update(w_ref, g_ref, m_ref, v_ref, d_ref, nm_ref, nv_ref):
    c1 = 1.0 / (1.0 - ADAM_B1 ** ADAM_STEP)
    c2 = 1.0 / (1.0 - ADAM_B2 ** ADAM_STEP)
    gv = g_ref[...]
    nm = ADAM_B1 * m_ref[...] + (1.0 - ADAM_B1) * gv
    nv = ADAM_B2 * v_ref[...] + (1.0 - ADAM_B2) * (gv * gv)
    d_ref[...] = -ADAM_LR * ((nm * c1) / (jnp.sqrt(nv * c2) + ADAM_EPS) + ADAM_WD * w_ref[...])
    nm_ref[...] = nm
    nv_ref[...] = nv


def _adamw_many(ws, gs, ms, vs):
    n = len(ws)

    def body(*refs):
        for i in range(n):
            _adamw_update(refs[i], refs[n + i], refs[2 * n + i], refs[3 * n + i],
                          refs[4 * n + i], refs[5 * n + i], refs[6 * n + i])

    shapes = [jax.ShapeDtypeStruct(w.shape, F32) for w in ws]
    res = pl.pallas_call(
        body, name="adamw_small", out_shape=shapes * 3,
        compiler_params=pltpu.CompilerParams(vmem_limit_bytes=VMEM_LIMIT_BYTES),
    )(*ws, *gs, *ms, *vs)
    return res[:n], res[n:2 * n], res[2 * n:]


def _adamw(w, g, m, v, name):
    R, C = w.shape
    tr = _row_tile(R, 256)

    def body(w_ref, g_ref, m_ref, v_ref, d_ref, nm_ref, nv_ref):
        _adamw_update(w_ref, g_ref, m_ref, v_ref, d_ref, nm_ref, nv_ref)

    blk = pl.BlockSpec((tr, C), lambda i: (i, 0))
    return pl.pallas_call(
        body, name=name, grid=(R // tr,),
        in_specs=[blk] * 4, out_specs=[blk] * 3,
        out_shape=[jax.ShapeDtypeStruct((R, C), F32)] * 3,
        compiler_params=_cp(1),
    )(w, g, m, v)


def _mod_fwd(c_rows, w_shard, b_shard):
    def body(c_ref, w_ref, b_ref, o_ref):
        s = _silu(c_ref[...]).astype(BF16)
        o_ref[...] = _dot(s, w_ref[...].astype(BF16)) + b_ref[...]

    return pl.pallas_call(
        body, name="mod_fwd",
        out_shape=jax.ShapeDtypeStruct((c_rows.shape[0], w_shard.shape[1]), F32),
        compiler_params=pltpu.CompilerParams(vmem_limit_bytes=VMEM_LIMIT_BYTES),
    )(c_rows, w_shard, b_shard)


def _mod_bwd(c_rows, dm_all, dm_shard, w_shard):
    nrow = c_rows.shape[0]

    def body(c_ref, da_ref, ds_ref, w_ref, gw_ref, gb_ref, cc_ref):
        s = _silu(c_ref[...]).astype(BF16)
        ds = ds_ref[...]
        gw_ref[...] = _dot_tn(s, ds.astype(BF16))
        gb_ref[...] = jnp.sum(da_ref[...], axis=0, keepdims=True)
        rowi = lax.broadcasted_iota(jnp.int32, ds.shape, 0)
        dmc = jnp.sum(jnp.where(rowi % 8 >= 4, ds, 0.0), axis=0, keepdims=True)
        dmc8 = jnp.broadcast_to(dmc, (8, ds.shape[1])).astype(BF16)
        cc_ref[...] = _dot_nt(dmc8, w_ref[...].astype(BF16))

    return pl.pallas_call(
        body, name="mod_bwd",
        out_shape=[jax.ShapeDtypeStruct(w_shard.shape, F32),
                   jax.ShapeDtypeStruct((1, dm_all.shape[1]), F32),
                   jax.ShapeDtypeStruct((8, D), F32)],
        compiler_params=pltpu.CompilerParams(vmem_limit_bytes=VMEM_LIMIT_BYTES),
    )(c_rows, dm_all, dm_shard, w_shard)


def _small_finish(cc_pre, c_ctx, dd_cols):
    def body(cc_ref, c_ref, dd_ref, gc_ref, gd_ref):
        gc_ref[...] = cc_ref[...] * _silu_grad(c_ref[...])
        gd_ref[...] = jnp.sum(dd_ref[...], axis=1, keepdims=True)

    return pl.pallas_call(
        body, name="small_finish",
        out_shape=[jax.ShapeDtypeStruct((1, D), F32), jax.ShapeDtypeStruct((NH, 1), F32)],
    )(cc_pre, c_ctx, dd_cols)


_BIG = (("w_in", (D, 10304), 1), ("w_br_ssd", (DI, D), 0), ("w_br_lru", (LW, D), 0), ("w_out", (D, D), 0),
        ("w_mlp1", (D, MLP), 1), ("w_mlp2", (MLP, D), 0))
_SMALL_SH = (("ssd_conv_w", (4, 4096)), ("lru_conv_w", (4, LW)), ("lru_ba", (2, LW)), ("lru_bi", (2, LW)),
             ("lru_lambda", (2, LW)))
_REPL = (("c_ctx", (D,)), ("b_gate", (2 * D,)), ("ssd_conv_b", (4096,)), ("ssd_dt_bias", (2, NH)),
         ("ssd_a_log", (2, NH)), ("ssd_d", (DI,)), ("ssd_norm_w", (DI,)), ("lru_conv_b", (LW,)),
         ("ln1_g", (D,)), ("ln1_b", (D,)),
         ("b_mlp1", (MLP,)), ("b_mlp2", (D,)), ("ln2_g", (D,)), ("ln2_b", (D,)))

_WEIGHT_NAMES = ('c_ctx', 'w_mod', 'b_mod', 'w_in', 'b_gate', 'ssd_conv_w', 'ssd_conv_b', 'ssd_dt_bias', 'ssd_a_log',
                 'ssd_d', 'ssd_norm_w', 'lru_conv_w', 'lru_conv_b', 'lru_wa', 'lru_ba', 'lru_wi', 'lru_bi',
                 'lru_lambda', 'w_br_ssd', 'w_br_lru', 'w_out', 'ln1_g', 'ln1_b', 'w_mlp1', 'b_mlp1', 'w_mlp2',
                 'b_mlp2', 'ln2_g', 'ln2_b')
_ARG_NAMES = ('x', 'c', 'ctx') + _WEIGHT_NAMES + ('loss_target',) + tuple('m_' + n for n in _WEIGHT_NAMES) + tuple(
    'v_' + n for n in _WEIGHT_NAMES)


def _to_slots(full, axis):
    n = full.shape[axis] // NDEV
    if axis == 0:
        return full.reshape(NDEV, n, full.shape[1])
    return full.reshape(full.shape[0], NDEV, n).transpose(1, 0, 2)


def _from_slots(slots, axis):
    if axis == 0:
        return slots.reshape(NDEV * slots.shape[1], slots.shape[2])
    return slots.transpose(1, 0, 2).reshape(slots.shape[1], NDEV * slots.shape[2])


def _pack_rows(arrs, width=1024, mult=8):
    flat = jnp.concatenate([a.reshape(-1) for a in arrs])
    n = flat.shape[0]
    per = width * mult
    tot = -(-n // per) * per
    return jnp.pad(flat, (0, tot - n)).reshape(tot // width, width)


def _unpack_rows(packed, shapes, lead=()):
    nl = len(lead)
    flat = packed.reshape(tuple(lead) + (-1,))
    out, off = [], 0
    for s in shapes:
        n = math.prod(s)
        out.append(flat[..., off:off + n].reshape(tuple(lead) + tuple(s)))
        off += n
    return out


def kernel(x, c, ctx, c_ctx, w_mod, b_mod, w_in, b_gate, ssd_conv_w, ssd_conv_b, ssd_dt_bias, ssd_a_log, ssd_d, ssd_norm_w, lru_conv_w, lru_conv_b, lru_wa, lru_ba, lru_wi, lru_bi, lru_lambda, w_br_ssd, w_br_lru, w_out, ln1_g, ln1_b, w_mlp1, b_mlp1, w_mlp2, b_mlp2, ln2_g, ln2_b, loss_target, m_c_ctx, m_w_mod, m_b_mod, m_w_in, m_b_gate, m_ssd_conv_w, m_ssd_conv_b, m_ssd_dt_bias, m_ssd_a_log, m_ssd_d, m_ssd_norm_w, m_lru_conv_w, m_lru_conv_b, m_lru_wa, m_lru_ba, m_lru_wi, m_lru_bi, m_lru_lambda, m_w_br_ssd, m_w_br_lru, m_w_out, m_ln1_g, m_ln1_b, m_w_mlp1, m_b_mlp1, m_w_mlp2, m_b_mlp2, m_ln2_g, m_ln2_b, v_c_ctx, v_w_mod, v_b_mod, v_w_in, v_b_gate, v_ssd_conv_w, v_ssd_conv_b, v_ssd_dt_bias, v_ssd_a_log, v_ssd_d, v_ssd_norm_w, v_lru_conv_w, v_lru_conv_b, v_lru_wa, v_lru_ba, v_lru_wi, v_lru_bi, v_lru_lambda, v_w_br_ssd, v_w_br_lru, v_w_out, v_ln1_g, v_ln1_b, v_w_mlp1, v_b_mlp1, v_w_mlp2, v_b_mlp2, v_ln2_g, v_ln2_b):
    A = dict(zip(_ARG_NAMES, (x, c, ctx, c_ctx, w_mod, b_mod, w_in, b_gate, ssd_conv_w, ssd_conv_b, ssd_dt_bias, ssd_a_log, ssd_d, ssd_norm_w, lru_conv_w, lru_conv_b, lru_wa, lru_ba, lru_wi, lru_bi, lru_lambda, w_br_ssd, w_br_lru, w_out, ln1_g, ln1_b, w_mlp1, b_mlp1, w_mlp2, b_mlp2, ln2_g, ln2_b, loss_target, m_c_ctx, m_w_mod, m_b_mod, m_w_in, m_b_gate, m_ssd_conv_w, m_ssd_conv_b, m_ssd_dt_bias, m_ssd_a_log, m_ssd_d, m_ssd_norm_w, m_lru_conv_w, m_lru_conv_b, m_lru_wa, m_lru_ba, m_lru_wi, m_lru_bi, m_lru_lambda, m_w_br_ssd, m_w_br_lru, m_w_out, m_ln1_g, m_ln1_b, m_w_mlp1, m_b_mlp1, m_w_mlp2, m_b_mlp2, m_ln2_g, m_ln2_b, v_c_ctx, v_w_mod, v_b_mod, v_w_in, v_b_gate, v_ssd_conv_w, v_ssd_conv_b, v_ssd_dt_bias, v_ssd_a_log, v_ssd_d, v_ssd_norm_w, v_lru_conv_w, v_lru_conv_b, v_lru_wa, v_lru_ba, v_lru_wi, v_lru_bi, v_lru_lambda, v_w_br_ssd, v_w_br_lru, v_w_out, v_ln1_g, v_ln1_b, v_w_mlp1, v_b_mlp1, v_w_mlp2, v_b_mlp2, v_ln2_g, v_ln2_b)))
    Bn, T, _ = x.shape
    Tc = ctx.shape[1]
    cfg = _Cfg(Bn, T, Tc)
    me = _me()
    L = {n: (A[n] if n == "c_ctx" else A[n][0]) for n in _WEIGHT_NAMES}
    nmod = L["w_mod"].shape[1]

    c_all = _exchange(c, "ag_c", True)
    c_rows = jnp.concatenate([c_all.reshape(NDEV * Bn, D), jnp.broadcast_to(c_ctx[None, :], (8, D))], axis=0)
    b_shard = lax.dynamic_slice(L["b_mod"], (me * nmod,), (nmod,)).reshape(1, nmod)
    m_part = _mod_fwd(c_rows, L["w_mod"], b_shard)
    m_all = _exchange(m_part, "ag_mod", True)
    m_full = m_all.transpose(1, 0, 2).reshape(NDEV * Bn + 8, NMOD * D)
    m_mine = lax.dynamic_slice(m_full, (me * Bn, 0), (Bn, NMOD * D))
    mc = m_full[NDEV * Bn, :2 * D]

    rest_payload = jnp.concatenate([L[n].astype(BF16).reshape(-1, 1024) for n, _, _ in _BIG[1:]], axis=0)
    small_shapes = [(s[0], s[1] // NDEV) for _, s in _SMALL_SH]
    small_all = _exchange(_pack_rows([L[n] for n, _ in _SMALL_SH], width=512), "ag_w_small", True)
    W = {}
    for (n, shp), piece in zip(_SMALL_SH, _unpack_rows(small_all, small_shapes, lead=(NDEV,))):
        W[n] = piece.transpose(1, 0, 2).reshape(shp)
    for n in ("ssd_conv_b", "lru_conv_b", "ssd_norm_w", "b_gate", "ln1_g", "ln1_b", "b_mlp1", "b_mlp2", "ln2_g", "ln2_b"):
        W[n] = L[n].reshape(1, -1)
    for n in ("ssd_dt_bias", "ssd_a_log", "ssd_d", "lru_wa", "lru_wi"):
        W[n] = L[n]

    loss_part, grad_x, g, dm, dmc, xres = _local_step(cfg, x, ctx, loss_target, m_mine, mc, W,
                                                      L["w_in"].astype(BF16), rest_payload)
    loss = lax.psum(loss_part, ("x", "y", "c"))

    dmc_pad = jnp.pad(dmc, ((0, 4 - Bn), (0, (NMOD - 2) * D)))
    dm_payload = jnp.concatenate([jnp.pad(dm, ((0, 4 - Bn), (0, 0))), dmc_pad], axis=0)
    upd_w_in, dm_all = _sum_adamw(xres["rs_w_in"], L["w_in"], A["m_w_in"][0], A["v_w_in"][0], "sum_adamw_w_in",
                                  xch=(dm_payload, True))
    dm_all = dm_all.reshape(NDEV * 8, NMOD * D)
    c_rows_b = jnp.concatenate([jnp.pad(c_all, ((0, 0), (0, 4 - Bn), (0, 0))),
                                jnp.broadcast_to(c_ctx[None, None, :], (NDEV, 4, D))], axis=1).reshape(NDEV * 8, D)
    dm_shard = lax.dynamic_slice(dm_all, (0, me * nmod), (NDEV * 8, nmod))
    g_w_mod, g_b_mod, cc_part = _mod_bwd(c_rows_b, dm_all, dm_shard, L["w_mod"])
    g["c_ctx"] = cc_part[0]

    g["ssd_d"] = g.pop("ssd_d_cols")
    small_names = [n for n, _ in _REPL] + [n for n, _ in _SMALL_SH]
    small_full_shapes = [s for _, s in _REPL] + [s for _, s in _SMALL_SH]
    red_b, sm_all = _sum_slots(xres["rs_rest"], "sum_w_rest", xch=(_pack_rows([g[n] for n in small_names]), True))
    sm_sum = _sum_slots(sm_all, "sum_g_small")
    gs = dict(zip(small_names, _unpack_rows(sm_sum, small_full_shapes)))
    gcc, gdd = _small_finish(gs["c_ctx"].reshape(1, D), c_ctx.reshape(1, D), gs["ssd_d"].reshape(NH, HD))
    gs["c_ctx"] = gcc.reshape(D)
    gs["ssd_d"] = gdd.reshape(NH)
    for n, shp in _SMALL_SH:
        ns = shp[1] // NDEV
        gs[n] = lax.dynamic_slice(gs[n], (0, me * ns), (shp[0], ns))
    gs["b_mod"] = g_b_mod.reshape(NMOD * D)
    lru_sum = _sum_slots(xres["ag_lru"], "sum_g_lru").reshape(2, 2, LB, LBW, LBW)
    gs["lru_wa"], gs["lru_wi"] = lru_sum[0], lru_sum[1]

    gb = {}
    off = 0
    for n, shp, axis in _BIG[1:]:
        shard_shape = (shp[0] // NDEV, shp[1]) if axis == 0 else (shp[0], shp[1] // NDEV)
        r = math.prod(shard_shape) // 1024
        gb[n] = red_b[off:off + r].reshape(shard_shape)
        off += r
    gb["w_mod"] = g_w_mod

    grads, deltas, new_m, new_v = {}, {}, {}, {}
    big_names = ["w_mod"] + [n for n, _, _ in _BIG]
    grads["w_in"], deltas["w_in"], new_m["w_in"], new_v["w_in"] = upd_w_in
    for n in big_names:
        if n == "w_in":
            continue
        d_, nm_, nv_ = _adamw(L[n], gb[n], A["m_" + n][0], A["v_" + n][0], "adamw_" + n)
        grads[n], deltas[n], new_m[n], new_v[n] = gb[n], d_, nm_, nv_
    sm_names = [n for n in _WEIGHT_NAMES if n not in big_names]

    def two_d(a):
        return a.reshape(1, -1) if a.ndim == 1 else a
    loc = lambda pre: [two_d(A[pre + n] if n == "c_ctx" else A[pre + n][0]) for n in sm_names]
    gsm = [two_d(gs[n].reshape(L[n].shape)) for n in sm_names]
    ds_, nms_, nvs_ = _adamw_many(loc(""), gsm, loc("m_"), loc("v_"))
    for n, gv, dv, mv, vv in zip(sm_names, gsm, ds_, nms_, nvs_):
        shp = L[n].shape
        grads[n], deltas[n], new_m[n], new_v[n] = gv.reshape(shp), dv.reshape(shp), mv.reshape(shp), vv.reshape(shp)

    def out(dct):
        return [dct[n] if n == "c_ctx" else dct[n][None] for n in _WEIGHT_NAMES]
    return (loss, grad_x, *out(grads), *out(deltas), *out(new_m), *out(new_v))
```

```python
import functools
import math

import jax
import jax.numpy as jnp
from jax import lax
from jax.experimental import pallas as pl
from jax.experimental.pallas import tpu as pltpu

F32 = jnp.float32
BF16 = jnp.bfloat16

D = 1024
GRID_W = 64
DI = 2048
NH = 32
HD = 64
NG = 8
HPG = 4
NS = 128
CH = 128
LW = 1024
LB = 8
LBW = 128
LRU_C = 8.0
MLP = 4096
NMOD = 6
ALPHA = 2.0 ** 0.25
LN_EPS = 1e-6
RMS_EPS = 1e-5
PM = 10240
DTW = 128
CONVW = 5120
NDEV = 8

ADAM_LR = 0.001
ADAM_B1 = 0.9
ADAM_B2 = 0.999
ADAM_EPS = 1e-08
ADAM_WD = 0.01
ADAM_STEP = 10

VMEM_LIMIT_BYTES = 56 * 1024 * 1024


def _cp(n_axes):
    return pltpu.CompilerParams(dimension_semantics=("arbitrary",) * n_axes,
                                vmem_limit_bytes=VMEM_LIMIT_BYTES)


def _sigmoid(x):
    return 0.5 * jnp.tanh(0.5 * x) + 0.5


def _silu(x):
    return x * _sigmoid(x)


def _silu_grad(x):
    s = _sigmoid(x)
    return s * (1.0 + x * (1.0 - s))


def _log1p_pos(e):
    return jnp.where(e < 1e-2, e * (1.0 - e * (0.5 - e * (1.0 / 3.0))), jnp.log(1.0 + e))


def _softplus(x):
    return jnp.maximum(x, 0.0) + _log1p_pos(jnp.exp(-jnp.abs(x)))


_GELU_K = math.sqrt(2.0 / math.pi)


def _gelu(x):
    t = jnp.tanh(_GELU_K * (x + 0.044715 * x * x * x))
    return 0.5 * x * (1.0 + t)


def _gelu_and_grad(x):
    x2 = x * x
    t = jnp.tanh(_GELU_K * x * (1.0 + 0.044715 * x2))
    dt = (1.0 - t * t) * _GELU_K * (1.0 + 3.0 * 0.044715 * x2)
    h = 0.5 * (1.0 + t)
    return x * h, h + 0.5 * x * dt


def _ln(x):
    mu = jnp.mean(x, axis=-1, keepdims=True)
    xc = x - mu
    var = jnp.mean(xc * xc, axis=-1, keepdims=True)
    rs = lax.rsqrt(var + LN_EPS)
    return xc * rs, rs


def _ln_bwd(dy, xhat, rs):
    m1 = jnp.mean(dy, axis=-1, keepdims=True)
    m2 = jnp.mean(dy * xhat, axis=-1, keepdims=True)
    return rs * (dy - m1 - xhat * m2)


def _dot(a, b):
    return lax.dot_general(a, b, (((1,), (0,)), ((), ())), preferred_element_type=F32)


def _dot_nt(a, b):
    return lax.dot_general(a, b, (((1,), (1,)), ((), ())), preferred_element_type=F32)


def _dot_tn(a, b):
    return lax.dot_general(a, b, (((0,), (0,)), ((), ())), preferred_element_type=F32)


def _split3(a):
    a0 = a.astype(BF16)
    r = a - a0.astype(F32)
    a1 = r.astype(BF16)
    a2 = (r - a1.astype(F32)).astype(BF16)
    return a0, a1, a2


def _dot_exact_l(m_bf, a):
    a0, a1, a2 = _split3(a)
    return _dot(m_bf, a0) + _dot(m_bf, a1) + _dot(m_bf, a2)


def _dot_hilo_r(a, m_bf):
    a0 = a.astype(BF16)
    a1 = (a - a0.astype(F32)).astype(BF16)
    return _dot(a0, m_bf) + _dot(a1, m_bf)


def _tri(n, upper):
    ii = lax.broadcasted_iota(jnp.int32, (n, n), 0)
    kk = lax.broadcasted_iota(jnp.int32, (n, n), 1)
    m = (kk >= ii) if upper else (kk <= ii)
    return jnp.where(m, 1.0, 0.0).astype(BF16)


def _fit(n, t):
    t = min(t, n)
    while n % t:
        t //= 2
    return t


def _mm(a, b, mode, name, out_dtype=F32, tm=512, tn=512, tk=512, xch=None):
    if mode == "nn":
        M, K = a.shape
        N = b.shape[1]
    elif mode == "nt":
        M, K = a.shape
        N = b.shape[0]
    else:
        K, M = a.shape
        N = b.shape[1]
    tm, tn, tk = _fit(M, tm), _fit(N, tn), _fit(K, tk)
    assert M % tm == 0 and N % tn == 0 and K % tk == 0, (name, M, N, K, tm, tn, tk)
    nk = K // tk
    if mode == "tn":
        a_spec = pl.BlockSpec((tk, tm), lambda i, j, k: (k, i))
    else:
        a_spec = pl.BlockSpec((tm, tk), lambda i, j, k: (i, k))
    if mode == "nt":
        b_spec = pl.BlockSpec((tn, tk), lambda i, j, k: (j, k))
    else:
        b_spec = pl.BlockSpec((tk, tn), lambda i, j, k: (k, j))
    dn = {"nn": (((1,), (0,)), ((), ())), "nt": (((1,), (1,)), ((), ())), "tn": (((0,), (0,)), ((), ()))}[mode]

    def body(a_ref, b_ref, o_ref, acc_ref):
        k = pl.program_id(2)

        @pl.when(k == 0)
        def _():
            acc_ref[...] = jnp.zeros_like(acc_ref)

        acc_ref[...] += lax.dot_general(a_ref[...].astype(BF16), b_ref[...].astype(BF16), dn,
                                        preferred_element_type=F32)

        @pl.when(k == nk - 1)
        def _():
            o_ref[...] = acc_ref[...].astype(o_ref.dtype)

    def body_one_step(a_ref, b_ref, o_ref):
        o_ref[...] = lax.dot_general(a_ref[...].astype(BF16), b_ref[...].astype(BF16), dn,
                                     preferred_element_type=F32).astype(o_ref.dtype)

    res = _hosted_call(
        body if nk > 1 else body_one_step, xch, name=name, grid=(M // tm, N // tn, nk),
        in_specs=[a_spec, b_spec],
        out_specs=[pl.BlockSpec((tm, tn), lambda i, j, k: (i, j))],
        out_shape=[jax.ShapeDtypeStruct((M, N), out_dtype)],
        scratch_shapes=[pltpu.VMEM((tm, tn), F32)] if nk > 1 else [],
        compiler_params=_cp(3), args=(a, b))
    if xch is None:
        return res[0]
    return res[0][0], res[1]


def _mm_mlp1(h2, w1, b1, tm=1024, tn=1024):
    M, K = h2.shape
    N = w1.shape[1]
    tm, tn = _fit(M, tm), _fit(N, tn)

    def body(a_ref, b_ref, bias_ref, a1_ref, act_ref):
        v = _dot(a_ref[...], b_ref[...]) + bias_ref[...]
        a1_ref[...] = v
        r = jnp.maximum(v, 0.0)
        act_ref[...] = (r * r).astype(BF16)

    out = pl.BlockSpec((tm, tn), lambda i, j: (i, j))
    return pl.pallas_call(
        body, name="mm_mlp1", grid=(M // tm, N // tn),
        in_specs=[pl.BlockSpec((tm, K), lambda i, j: (i, 0)), pl.BlockSpec((K, tn), lambda i, j: (0, j)),
                  pl.BlockSpec((1, tn), lambda i, j: (0, j))],
        out_specs=[out, out],
        out_shape=[jax.ShapeDtypeStruct((M, N), F32), jax.ShapeDtypeStruct((M, N), BF16)],
        compiler_params=_cp(2),
    )(h2, w1, b1)


def _mm_dact(dmlp, w2, a1, tm=1024, tn=1024):
    M, K = dmlp.shape
    N = w2.shape[0]
    tm, tn = _fit(M, tm), _fit(N, tn)

    def body(d_ref, w_ref, a1_ref, o_ref, acc_ref):
        i = pl.program_id(1)

        @pl.when(i == 0)
        def _():
            acc_ref[...] = jnp.zeros_like(acc_ref)

        da = _dot_nt(d_ref[...], w_ref[...]) * (2.0 * jnp.maximum(a1_ref[...], 0.0))
        o_ref[...] = da.astype(BF16)
        acc_ref[0:1, :] += jnp.sum(da, axis=0, keepdims=True)

    blk = pl.BlockSpec((tm, tn), lambda j, i: (i, j))
    return pl.pallas_call(
        body, name="mm_dact", grid=(N // tn, M // tm),
        in_specs=[pl.BlockSpec((tm, K), lambda j, i: (i, 0)), pl.BlockSpec((tn, K), lambda j, i: (j, 0)), blk],
        out_specs=[blk, pl.BlockSpec((8, tn), lambda j, i: (0, j))],
        out_shape=[jax.ShapeDtypeStruct((M, N), BF16), jax.ShapeDtypeStruct((8, N), F32)],
        compiler_params=_cp(2),
    )(dmlp, w2, a1)


class _Cfg:
    def __init__(self, Bn, T, Tc):
        assert T % Tc == 0 and Tc % CH == 0 and Tc % GRID_W == 0
        self.Bn, self.T, self.Tc = Bn, T, Tc
        self.TT = T + Tc
        self.TB = Tc
        self.nbt = self.TT // self.TB
        self.nbl = T // self.TB
        self.NT = Bn * self.TT
        self.N = Bn * T
        self.nct = Tc // CH
        self.nlt = T // CH
        self.nch = self.nct + self.nlt


def _ln_mod_fwd(cfg, ctx2, x2, shift_tab, scale_tab, xch):
    TB, nbt, nbl = cfg.TB, cfg.nbt, cfg.nbl

    def body(c_ref, x_ref, sh_ref, sc_ref, o_ref):
        j = pl.program_id(1)
        xhat, _ = _ln(jnp.where(j == 0, c_ref[...], x_ref[...]))
        o_ref[...] = (xhat * (1.0 + sc_ref[...]) + sh_ref[...]).astype(BF16)

    tab = pl.BlockSpec((None, 1, D), lambda b, j: (2 * b + jnp.minimum(j, 1), 0, 0))
    outs, xres = _hosted_call(
        body, xch, name="ln_mod_fwd", grid=(cfg.Bn, nbt),
        in_specs=[pl.BlockSpec((TB, D), lambda b, j: (b, 0)),
                  pl.BlockSpec((TB, D), lambda b, j: (b * nbl + jnp.maximum(j - 1, 0), 0)), tab, tab],
        out_specs=[pl.BlockSpec((TB, D), lambda b, j: (b * nbt + j, 0))],
        out_shape=[jax.ShapeDtypeStruct((cfg.NT, D), BF16)],
        scratch_shapes=[], compiler_params=_cp(2), args=(ctx2, x2, shift_tab, scale_tab))
    return outs[0], xres


GP = 8
NGB = NG // GP
HPB = GP * HPG


def _heads_to_front(x, d, gb, inverse=False):
    off = d * NH + gb * HPB
    return pltpu.roll(x, off if inverse else (DTW - off) % DTW, 1)


def _chunks_per_step(cfg):
    n = cfg.NT // CH
    return max(c for c in (4, 3, 2, 1) if n % c == 0)


def _dt_fwd(cfg, dt_raw, dt_bias, a_log):
    cps = _chunks_per_step(cfg)

    def body(raw_ref, bias_ref, alog_ref, dt_ref, dtg_ref, cumg_ref, cumT_ref):
        a = -jnp.exp(alog_ref[...])
        col = lax.broadcasted_iota(jnp.int32, (CH, DTW), 1)
        for c in range(cps):
            rows = slice(c * CH, (c + 1) * CH)
            dt = _softplus(raw_ref[rows, :] + bias_ref[...])
            dta = dt * a
            cf = _dot_exact_l(_tri(CH, False), dta)
            cr = _dot_exact_l(_tri(CH, True), dta)
            cum = jnp.where(col < NH, cf, cr)
            dt_ref[rows, :] = dt
            for d in range(2):
                for gb in range(NGB):
                    dtg_ref[d, gb, rows, :] = _heads_to_front(dt, d, gb)
                    cg = _heads_to_front(cum, d, gb)
                    cumg_ref[d, gb, rows, :] = cg
                    cumT_ref[d, gb, c] = cg.T

    blk = pl.BlockSpec((cps * CH, DTW), lambda i: (i, 0))
    row = pl.BlockSpec((1, DTW), lambda i: (0, 0))
    gblk = pl.BlockSpec((2, NGB, cps * CH, DTW), lambda i: (0, 0, i, 0))
    return pl.pallas_call(
        body, name="dt_fwd", grid=(cfg.NT // (cps * CH),),
        in_specs=[blk, row, row],
        out_specs=[blk, gblk, gblk, pl.BlockSpec((2, NGB, cps, DTW, CH), lambda i: (0, 0, i, 0, 0))],
        out_shape=[jax.ShapeDtypeStruct((cfg.NT, DTW), F32),
                   jax.ShapeDtypeStruct((2, NGB, cfg.NT, DTW), F32),
                   jax.ShapeDtypeStruct((2, NGB, cfg.NT, DTW), F32),
                   jax.ShapeDtypeStruct((2, NGB, cfg.NT // CH, DTW, CH), F32)],
        compiler_params=_cp(1),
    )(dt_raw, dt_bias, a_log)


def _dt_bwd(cfg, dAs, dxxs, dt_raw, dt, dt_bias, a_log, h):
    def body(dAf_ref, dAr_ref, dxf_ref, dxr_ref, raw_ref, dt_ref, bias_ref, alog_ref, h_ref, o_ref, acc_ref, dw_ref):
        i = pl.program_id(0)

        @pl.when(i == 0)
        def _():
            acc_ref[...] = jnp.zeros_like(acc_ref)
            dw_ref[...] = jnp.zeros_like(dw_ref)

        a = -jnp.exp(alog_ref[...])
        col = lax.broadcasted_iota(jnp.int32, (CH, DTW), 1)
        for c in range(cps):
            rows = slice(c * CH, (c + 1) * CH)
            dA_v = jnp.zeros((CH, DTW), F32)
            dxx_v = jnp.zeros((CH, DTW), F32)
            for d, (ra, rx) in enumerate(((dAf_ref, dxf_ref), (dAr_ref, dxr_ref))):
                for gb in range(NGB):
                    dA_v = dA_v + _heads_to_front(ra[gb, rows, :], d, gb, inverse=True)
                    dxx_v = dxx_v + _heads_to_front(rx[gb, rows, :], d, gb, inverse=True)
            ddta = jnp.where(col < NH, _dot_exact_l(_tri(CH, True), dA_v), _dot_exact_l(_tri(CH, False), dA_v))
            dtv = dt_ref[rows, :]
            ddt = ddta * a + dxx_v
            draw = ddt * _sigmoid(raw_ref[rows, :] + bias_ref[...])
            draw = jnp.where(col < 2 * NH, draw, 0.0)
            o_ref[rows, :] = draw
            dw_ref[...] += _dot_tn(h_ref[rows, :], draw.astype(BF16))
            da = jnp.sum(ddta * dtv, axis=0, keepdims=True) * a
            da = jnp.where(col[:1] < 2 * NH, da, 0.0)
            acc_ref[0:1, :] += da
            acc_ref[1:2, :] += jnp.sum(draw, axis=0, keepdims=True)

    cps = _chunks_per_step(cfg)
    blk = pl.BlockSpec((cps * CH, DTW), lambda i: (i, 0))
    row = pl.BlockSpec((1, DTW), lambda i: (0, 0))
    gblk = pl.BlockSpec((NGB, cps * CH, DTW), lambda i: (0, i, 0))
    return pl.pallas_call(
        body, name="dt_bwd", grid=(cfg.NT // (cps * CH),),
        in_specs=[gblk, gblk, gblk, gblk, blk, blk, row, row, pl.BlockSpec((cps * CH, D), lambda i: (i, 0))],
        out_specs=[blk, pl.BlockSpec((8, DTW), lambda i: (0, 0)), pl.BlockSpec((D, DTW), lambda i: (0, 0))],
        out_shape=[jax.ShapeDtypeStruct((cfg.NT, DTW), F32), jax.ShapeDtypeStruct((8, DTW), F32),
                   jax.ShapeDtypeStruct((D, DTW), F32)],
        compiler_params=_cp(1),
    )(dAs[0], dAs[1], dxxs[0], dxxs[1], dt_raw, dt, dt_bias, a_log, h)


_TAPS = (2, 1, 0, -1)


def _conv_fwd(cfg, proj, conv_w, conv_b):
    TB, nbt = cfg.TB, cfg.nbt
    CB = CONVW // 2
    SUB = 256
    n_act = DI + 2 * NG * NS

    def body(u_ref, w_ref, b_ref, o_ref, sg_ref):
        i = pl.program_id(0)
        j = pl.program_id(1)
        R = jnp.where(i % nbt == 0, cfg.Tc, GRID_W)
        t = lax.broadcasted_iota(jnp.int32, (TB, SUB), 0)
        pos = jnp.bitwise_and(t, R - 1)
        keep = {s: jnp.where(jnp.logical_and(pos - s >= 0, pos - s < R), 1.0, 0.0) for s in (2, 1, -1)}
        def sub_tile(q, act):
            sl = slice(q * SUB, (q + 1) * SUB)
            u = u_ref[:, sl]
            pre = b_ref[:, sl] + w_ref[2:3, sl] * u
            for k in (0, 1, 3):
                pre = pre + w_ref[k:k + 1, sl] * (pltpu.roll(u, _TAPS[k] % TB, 0) * keep[_TAPS[k]])
            if act:
                s = _sigmoid(pre)
                o_ref[:, sl] = pre * s
                sg_ref[:, sl] = s * (1.0 + pre * (1.0 - s))
            else:
                o_ref[:, sl] = pre

        for q in range(CB // SUB):
            if q * SUB >= n_act - CB:
                pl.when(j == 0)(functools.partial(sub_tile, q, True))
                pl.when(j == 1)(functools.partial(sub_tile, q, False))
            else:
                sub_tile(q, True)

    blk = pl.BlockSpec((TB, CB), lambda i, j: (i, j))
    return pl.pallas_call(
        body, name="conv_fwd", grid=(cfg.NT // TB, CONVW // CB),
        in_specs=[blk, pl.BlockSpec((4, CB), lambda i, j: (0, j)), pl.BlockSpec((1, CB), lambda i, j: (0, j))],
        out_specs=[blk, blk],
        out_shape=[jax.ShapeDtypeStruct((cfg.NT, CONVW), F32)] * 2,
        compiler_params=_cp(2),
    )(proj, conv_w, conv_b)


_ANY = pl.BlockSpec(memory_space=pl.ANY)


def _conv_bwd(cfg, name, dproj, proj, conv_w, sgrad, addends, col0, width, skip=None, xch=None):
    TB, nbt, nbl = cfg.TB, cfg.nbt, cfg.nbl
    CB = 1024
    SUB = 256
    c0 = col0 // CB
    addends = list(addends) + ([] if sgrad is None else [sgrad])
    n_add = len(addends)

    def body(*refs):
        u_ref, w_ref = refs[1:3]
        add_refs = refs[3:3 + n_add]
        rest = refs[3 + n_add:]
        if sgrad is not None:
            add_refs, sg_ref = add_refs[:-1], add_refs[-1]
        if skip is not None:
            dy_ref, dv_ref = rest[:2]
            rest = rest[2:]
        o_ref, acc_ref = rest
        i = pl.program_id(1)

        @pl.when(i == 0)
        def _():
            acc_ref[...] = jnp.zeros_like(acc_ref)

        isctx = (i % nbt) == 0
        R = jnp.where(isctx, cfg.Tc, GRID_W)
        t = lax.broadcasted_iota(jnp.int32, (TB, SUB), 0)
        pos = jnp.bitwise_and(t, R - 1)
        keep = {s: jnp.where(jnp.logical_and(pos - s >= 0, pos - s < R), 1.0, 0.0) for s in (2, 1, -1, -2)}

        def shifted(v, s):
            return v if s == 0 else pltpu.roll(v, s % TB, 0) * keep[s]

        for q in range(CB // SUB):
            sl = slice(q * SUB, (q + 1) * SUB)
            u = u_ref[:, sl]
            us = [shifted(u, _TAPS[k]) for k in range(4)]
            g = add_refs[0][:, sl]
            for r in add_refs[1:]:
                g = g + r[:, sl]
            if skip is not None:
                g = g + jnp.where(isctx, 0.0, dv_ref[:, sl] * dy_ref[:, sl])
            if sgrad is not None:
                g = g * sg_ref[:, sl]
            dp = jnp.zeros_like(g)
            for k in range(4):
                acc_ref[k:k + 1, sl] += jnp.sum(g * us[k], axis=0, keepdims=True)
                dp = dp + w_ref[k:k + 1, sl] * shifted(g, -_TAPS[k])
            acc_ref[4:5, sl] += jnp.sum(g, axis=0, keepdims=True)
            o_ref[:, sl] = dp.astype(BF16)

    blk = pl.BlockSpec((TB, CB), lambda j, i: (i, j))
    wide = pl.BlockSpec((TB, CB), lambda j, i: (i, c0 + j))
    in_specs = [_ANY, wide, pl.BlockSpec((4, CB), lambda j, i: (0, c0 + j))]
    in_specs += [blk] * (n_add if sgrad is None else n_add - 1) + ([] if sgrad is None else [wide])
    args = [dproj, proj, conv_w] + addends
    if skip is not None:
        def lat(j, i):
            b = i // nbt
            return (b * nbl + jnp.maximum(i % nbt - 1, 0), j)
        in_specs += [pl.BlockSpec((TB, CB), lat), pl.BlockSpec((1, CB), lambda j, i: (0, j))]
        args += list(skip)
    return _hosted_call(
        body, xch, name=name, grid=(width // CB, cfg.NT // TB),
        in_specs=in_specs,
        out_specs=[pl.BlockSpec((TB, CB), lambda j, i: (i, c0 + j)), pl.BlockSpec((8, CB), lambda j, i: (0, j))],
        out_shape=[jax.ShapeDtypeStruct((cfg.NT, PM), BF16), jax.ShapeDtypeStruct((8, width), F32)],
        scratch_shapes=[], compiler_params=_cp(2), args=args, aliases={0: 0})


def _chunk_of_step(cfg, rev):
    nct, nlt = cfg.nct, cfg.nlt
    if not rev:
        return lambda s: s
    return lambda s: jnp.where(s < nct, nct - 1 - s, 2 * nct + nlt - 1 - s)


def _expand4(v, band, base):
    out = v[:, base + 3:base + 4]
    for h in (2, 1, 0):
        out = jnp.where(band == h, v[:, base + h:base + h + 1], out)
    return out


def _ssd_step_tiles(dt_ref, cum_ref, cumT_ref, rev):
    cum_t = cum_ref[...]
    last = 0 if rev else CH - 1
    llast = cum_t[last:last + 1, :]
    return (dt_ref[...], cum_t, cumT_ref[...], llast, jnp.exp(llast), last)


def _ssd_common(gi, x_ref, b_ref, c_ref, tiles, rev, intra=True):
    dt_t, cum_t, cumT_t, llast, elast, last = tiles
    base = gi * HPG
    xh = x_ref[:, gi * HPG * HD:(gi + 1) * HPG * HD]
    Bm = b_ref[:, gi * NS:(gi + 1) * NS].astype(BF16)
    band = lax.broadcasted_iota(jnp.int32, (CH, HPG * HD), 1) // HD
    cbs = [jnp.broadcast_to(cum_t[:, base + h:base + h + 1], (CH, CH)) for h in range(HPG)]
    Cm = G = decs = None
    if intra:
        Cm = c_ref[:, gi * NS:(gi + 1) * NS].astype(BF16)
        G = _dot_nt(Cm, Bm)
        ii = lax.broadcasted_iota(jnp.int32, (CH, CH), 0)
        jj = lax.broadcasted_iota(jnp.int32, (CH, CH), 1)
        mask = (jj >= ii) if rev else (jj <= ii)
        decs = [jnp.exp(jnp.where(mask, cbs[h] - cumT_t[base + h:base + h + 1, :], -1e30)) for h in range(HPG)]
    cum_exp = jnp.concatenate([cbs[3], cbs[3]], axis=1)
    ll_exp = llast[:, base + 3:base + 4]
    for h in (2, 1, 0):
        cum_exp = jnp.where(band == h, jnp.concatenate([cbs[h], cbs[h]], axis=1), cum_exp)
        ll_exp = jnp.where(band[:1] == h, llast[:, base + h:base + h + 1], ll_exp)
    ecum = jnp.exp(cum_exp) if intra else None
    e_exp = jnp.exp(ll_exp - cum_exp)
    dt_exp = _expand4(dt_t, band, base)
    X = xh * dt_exp
    rb = lax.broadcasted_iota(jnp.int32, (HPG * HD, NS), 0) // HD
    dec_rows = elast[:, base + 3:base + 4]
    for h in (2, 1, 0):
        dec_rows = jnp.where(rb == h, elast[:, base + h:base + h + 1], dec_rows)
    return xh, Bm, Cm, band, e_exp, ecum, dt_exp, X, G, decs, elast, dec_rows, last


def _ssd_specs(cfg, rev):
    nch = cfg.nch
    cmap = _chunk_of_step(cfg, rev)
    d = 1 if rev else 0

    def make(stepmap):
        def row(b, g, sp):
            return b * nch + cmap(stepmap(sp))
        bo, co = DI // (GP * NS), (DI + NG * NS) // (GP * NS)
        return [
            pl.BlockSpec((CH, GP * HPG * HD), lambda b, g, sp: (row(b, g, sp), g)),
            pl.BlockSpec((CH, GP * NS), lambda b, g, sp: (row(b, g, sp), bo + g)),
            pl.BlockSpec((CH, GP * NS), lambda b, g, sp: (row(b, g, sp), co + g)),
            pl.BlockSpec((None, None, CH, DTW), lambda b, g, sp: (d, g, row(b, g, sp), 0)),
            pl.BlockSpec((None, None, CH, DTW), lambda b, g, sp: (d, g, row(b, g, sp), 0)),
            pl.BlockSpec((None, None, None, DTW, CH), lambda b, g, sp: (d, g, row(b, g, sp), 0, 0)),
        ], row
    return make


def _ssd_fwd(cfg, act, dtg, cumg, cumTg, rev, y_other=None, dvec=None):
    nch = cfg.nch
    in_specs, row = _ssd_specs(cfg, rev)(lambda sp: sp)
    total = y_other is not None

    def body(*refs):
        x_ref, b_ref, c_ref, dt_ref, cum_ref, cumT_ref = refs[:6]
        if total:
            yo_ref, dv_ref = refs[6:8]
        y_ref, hs_ref, h_scr = refs[-3:]
        s = pl.program_id(2)

        @pl.when(s == 0)
        def _():
            h_scr[...] = jnp.zeros_like(h_scr)

        def step(intra):
            tiles = _ssd_step_tiles(dt_ref, cum_ref, cumT_ref, rev)
            for gi in range(GP):
                xh, Bm, Cm, band, e_exp, ecum, dt_exp, X, G, decs, elast, dec_rows, last = _ssd_common(
                    gi, x_ref, b_ref, c_ref, tiles, rev, intra)
                H = h_scr[gi]
                if intra:
                    Mcat = jnp.concatenate([(G * decs[h]).astype(BF16) for h in range(HPG)], axis=1)
                    Xbd = jnp.concatenate([jnp.where(band == h, X, 0.0).astype(BF16) for h in range(HPG)], axis=0)
                    xsl = slice(gi * HPG * HD, (gi + 1) * HPG * HD)
                    Y = ecum * _dot_nt(Cm, H.astype(BF16)) + _dot(Mcat, Xbd)
                    if total:
                        Y = Y + yo_ref[:, xsl] + dv_ref[:, xsl] * xh
                    y_ref[:, xsl] = Y
                hs_ref[gi] = H
                S = _dot_tn((e_exp * X).astype(BF16), Bm)
                h_scr[gi] = dec_rows * H + S

        isctx = cmap(s) < cfg.nct

        @pl.when(isctx)
        def _():
            step(False)

        @pl.when(jnp.logical_not(isctx))
        def _():
            step(True)

    cmap = _chunk_of_step(cfg, rev)
    yblk = pl.BlockSpec((CH, GP * HPG * HD), lambda b, g, s: (row(b, g, s), g))
    args = [act, act, act, dtg, cumg, cumTg]
    if total:
        in_specs = in_specs + [yblk, pl.BlockSpec((1, GP * HPG * HD), lambda b, g, s: (0, g))]
        args += [y_other, dvec]
    return pl.pallas_call(
        body, name="ssd_fwd_rev" if rev else "ssd_fwd", grid=(cfg.Bn, NG // GP, nch),
        in_specs=in_specs,
        out_specs=[yblk, pl.BlockSpec((None, GP, None, HPG * HD, NS), lambda b, g, s: (b, g, s, 0, 0))],
        out_shape=[jax.ShapeDtypeStruct((cfg.NT, DI), F32),
                   jax.ShapeDtypeStruct((cfg.Bn, NG, nch, HPG * HD, NS), F32)],
        scratch_shapes=[pltpu.VMEM((GP, HPG * HD, NS), F32)],
        compiler_params=_cp(3),
    )(*args)


def _ssd_bwd(cfg, act, dtg, cumg, cumTg, hs, dy, rev, xch=None):
    nch, nct, nlt = cfg.nch, cfg.nct, cfg.nlt
    cmap = _chunk_of_step(cfg, rev)
    in_specs, row = _ssd_specs(cfg, rev)(lambda sp: nch - 1 - sp)

    def lat_row(b, g, sp):
        c = cmap(nch - 1 - sp)
        return b * nlt + jnp.maximum(c - nct, 0)

    def body(x_ref, b_ref, c_ref, dt_ref, cum_ref, cumT_ref, dy_ref, hs_ref,
             dxh_ref, dB_ref, dC_ref, dA_ref, dxx_ref, dh_scr):
        sp = pl.program_id(2)

        @pl.when(sp == 0)
        def _():
            dh_scr[...] = jnp.zeros_like(dh_scr)

        def step(intra):
            tiles = _ssd_step_tiles(dt_ref, cum_ref, cumT_ref, rev)
            dA_t = jnp.zeros((CH, DTW), F32)
            dAT_t = jnp.zeros((DTW, CH), F32)
            dxx_t = jnp.zeros((CH, DTW), F32)
            for gi in range(GP):
                dA_g, dAT_g, dxx_g = group_bwd(gi, intra, tiles, x_ref, b_ref, c_ref, dy_ref, hs_ref,
                                               dxh_ref, dB_ref, dC_ref, dh_scr)
                dA_t = dA_t + dA_g
                dxx_t = dxx_t + dxx_g
                if intra:
                    dAT_t = dAT_t + dAT_g
            dA_ref[...] = dA_t - dAT_t.T if intra else dA_t
            dxx_ref[...] = dxx_t

        isctx = cmap(nch - 1 - sp) < nct

        @pl.when(isctx)
        def _():
            step(False)

        @pl.when(jnp.logical_not(isctx))
        def _():
            step(True)

    def group_bwd(gi, intra, tiles, x_ref, b_ref, c_ref, dy_ref, hs_ref, dxh_ref, dB_ref, dC_ref, dh_scr):
        xsl = slice(gi * HPG * HD, (gi + 1) * HPG * HD)
        nsl = slice(gi * NS, (gi + 1) * NS)
        base = gi * HPG
        xh, Bm, Cm, band, e_exp, ecum, dt_exp, X, G, decs, elast, dec_rows, last = _ssd_common(
            gi, x_ref, b_ref, c_ref, tiles, rev, intra)
        H = hs_ref[gi]
        dHn = dh_scr[gi]
        dHnb = dHn.astype(BF16)
        BdH = _dot_nt(Bm, dHnb)
        dX = e_exp * BdH
        eX = e_exp * X
        lanei = lax.broadcasted_iota(jnp.int32, (CH, DTW), 1)
        dA = jnp.zeros((CH, DTW), F32)
        dAT = None
        pb = lax.broadcasted_iota(jnp.int32, (HPG * HD, NS), 0) // HD
        pl_ = lax.broadcasted_iota(jnp.int32, (HPG * HD, NS), 1)
        E = jnp.where(pb + base == pl_, 1.0, 0.0).astype(BF16)
        if intra:
            dY = dy_ref[:, xsl]
            Hb = H.astype(BF16)
            dYs = ecum * dY
            dYsb = dYs.astype(BF16)
            Ys = ecum * _dot_nt(Cm, Hb)
            dG = jnp.zeros((CH, CH), F32)
            subi = lax.broadcasted_iota(jnp.int32, (DTW, CH), 0)
            dAT = jnp.zeros((DTW, CH), F32)
            Xbd = jnp.concatenate([jnp.where(band == h, X, 0.0).astype(BF16) for h in range(HPG)], axis=0)
            dYbd = jnp.concatenate([jnp.where(band == h, dY, 0.0).astype(BF16) for h in range(HPG)], axis=0)
            dMcat = _dot_nt(dY.astype(BF16), Xbd)
            Ms = []
            for h in range(HPG):
                M = G * decs[h]
                dM = dMcat[:, h * CH:(h + 1) * CH]
                W = dM * M
                dG = dG + dM * decs[h]
                Ms.append(M.astype(BF16))
                dA = dA + jnp.where(lanei == base + h, jnp.sum(W, axis=1, keepdims=True), 0.0)
                dAT = dAT + jnp.where(subi == base + h, jnp.sum(W, axis=0, keepdims=True), 0.0)
            dX = dX + _dot_tn(jnp.concatenate(Ms, axis=0), dYbd)
            dGb = dG.astype(BF16)
            dC_ref[:, nsl] = _dot(dGb, Bm) + _dot(dYsb, Hb)
            dB_ref[:, nsl] = _dot_tn(dGb, Cm) + _dot(eX.astype(BF16), dHnb)
            dh_scr[gi] = dec_rows * dHn + _dot_tn(dYsb, Cm)
            dA = dA + _dot_hilo_r(dY * Ys, E)
        else:
            dC_ref[:, nsl] = jnp.zeros((CH, NS), F32)
            dB_ref[:, nsl] = _dot(eX.astype(BF16), dHnb)
            dh_scr[gi] = dec_rows * dHn
        q = _dot_hilo_r(eX * BdH, E)
        r = jnp.sum(dHn * H, axis=1, keepdims=True)
        lane1 = lax.broadcasted_iota(jnp.int32, (1, DTW), 1)
        hdot = jnp.zeros((1, DTW), F32)
        for h in range(HPG):
            hv = jnp.sum(r[h * HD:(h + 1) * HD, :], axis=0, keepdims=True)
            hdot = hdot + jnp.where(lane1 == base + h, hv, 0.0)
        dllast = jnp.sum(q, axis=0, keepdims=True) + elast * hdot
        rowi = lax.broadcasted_iota(jnp.int32, (CH, DTW), 0)
        dxh_ref[:, xsl] = dX * dt_exp
        return dA - q + jnp.where(rowi == last, dllast, 0.0), dAT, _dot_hilo_r(dX * xh, E)

    small = pl.BlockSpec((None, CH, DTW), lambda b, g, sp: (g, row(b, g, sp), 0))
    return _hosted_call(
        body, xch, name="ssd_bwd_rev" if rev else "ssd_bwd", grid=(cfg.Bn, NG // GP, nch),
        in_specs=in_specs + [
            pl.BlockSpec((CH, GP * HPG * HD), lambda b, g, sp: (lat_row(b, g, sp), g)),
            pl.BlockSpec((None, GP, None, HPG * HD, NS), lambda b, g, sp: (b, g, nch - 1 - sp, 0, 0))],
        out_specs=[pl.BlockSpec((CH, GP * HPG * HD), lambda b, g, sp: (row(b, g, sp), g)),
                   pl.BlockSpec((CH, GP * NS), lambda b, g, sp: (row(b, g, sp), g)),
                   pl.BlockSpec((CH, GP * NS), lambda b, g, sp: (row(b, g, sp), g)),
                   small, small],
        out_shape=[jax.ShapeDtypeStruct((cfg.NT, DI), F32),
                   jax.ShapeDtypeStruct((cfg.NT, NG * NS), F32),
                   jax.ShapeDtypeStruct((cfg.NT, NG * NS), F32),
                   jax.ShapeDtypeStruct((NGB, cfg.NT, DTW), F32),
                   jax.ShapeDtypeStruct((NGB, cfg.NT, DTW), F32)],
        scratch_shapes=[pltpu.VMEM((GP, HPG * HD, NS), F32)],
        compiler_params=_cp(3), args=(act, act, act, dtg, cumg, cumTg, dy, hs))


def _shift_rows(v, s, fill, toward_later, rowi):
    n = v.shape[0]
    if toward_later:
        return jnp.where(rowi >= s, pltpu.roll(v, s, 0), fill)
    return jnp.where(rowi < n - s, pltpu.roll(v, n - s, 0), fill)


def _chunk_scan(a, b, carry, later):
    nt = a.shape[0] // 8
    rowi = lax.broadcasted_iota(jnp.int32, (8, a.shape[1]), 0)
    outs = [None] * nt
    for r in (range(nt) if later else range(nt - 1, -1, -1)):
        av = a[r * 8:(r + 1) * 8]
        bv = b[r * 8:(r + 1) * 8]
        for sh in (1, 2, 4):
            a_p = _shift_rows(av, sh, 1.0, later, rowi)
            b_p = _shift_rows(bv, sh, 0.0, later, rowi)
            bv = av * b_p + bv
            av = av * a_p
        h = bv + av * carry
        outs[r] = h
        carry = h[7:8] if later else h[0:1]
    return jnp.concatenate(outs, axis=0), carry


def _lru_gates(u, wa_ref, wi_ref, ba_ref, bi_ref, lam_ref):
    rs, is_ = [], []
    for k in range(LB):
        uk = u[:, k * LBW:(k + 1) * LBW].astype(BF16)
        rs.append(_dot(uk, wa_ref[k].astype(BF16)))
        is_.append(_dot(uk, wi_ref[k].astype(BF16)))
    r = 1.0 / (1.0 + jnp.exp(-(jnp.concatenate(rs, axis=1) + ba_ref[...])))
    ig = _sigmoid(jnp.concatenate(is_, axis=1) + bi_ref[...])
    sp = _softplus(-lam_ref[...])
    la = -LRU_C * r * sp
    a = jnp.exp(la)
    q = (1.0 + a * a) * jnp.tanh(-la)
    return r, ig, sp, la, a, jnp.sqrt(q), lax.rsqrt(q)


def _lru_w_specs(d):
    return [pl.BlockSpec((None, LB, LBW, LBW), lambda b, s: (d, 0, 0, 0)),
            pl.BlockSpec((None, LB, LBW, LBW), lambda b, s: (d, 0, 0, 0)),
            pl.BlockSpec((None, 1, LW), lambda b, s: (d, 0, 0)),
            pl.BlockSpec((None, 1, LW), lambda b, s: (d, 0, 0)),
            pl.BlockSpec((None, 1, LW), lambda b, s: (d, 0, 0))]


def _lru_block_of_step(cfg, rev):
    nbl = cfg.nbl
    if not rev:
        return lambda s: s
    return lambda s: jnp.where(s < 1, 0, 1 + nbl - s)


def _lru_fwd(cfg, act, wa, wi, ba, bi, lam, rev):
    nch, CH = cfg.nbt, cfg.TB
    cmap = _lru_block_of_step(cfg, rev)
    d = 1 if rev else 0
    ucol = (DI + 2 * NG * NS) // LW

    def body(u_ref, wa_ref, wi_ref, ba_ref, bi_ref, lam_ref, h_ref, c_scr):
        s = pl.program_id(1)

        @pl.when(s == 0)
        def _():
            c_scr[...] = jnp.zeros_like(c_scr)

        u = u_ref[...]
        r, ig, sp, la, a, g, _ = _lru_gates(u, wa_ref, wi_ref, ba_ref, bi_ref, lam_ref)
        h, carry = _chunk_scan(a, g * ig * u, c_scr[0:1, :], not rev)
        h_ref[...] = h
        c_scr[0:1, :] = carry

    return pl.pallas_call(
        body, name="lru_fwd_rev" if rev else "lru_fwd", grid=(cfg.Bn, nch),
        in_specs=[pl.BlockSpec((CH, LW), lambda b, s: (b * nch + cmap(s), ucol))] + _lru_w_specs(d),
        out_specs=pl.BlockSpec((CH, LW), lambda b, s: (b * nch + cmap(s), 0)),
        out_shape=jax.ShapeDtypeStruct((cfg.NT, LW), F32),
        scratch_shapes=[pltpu.VMEM((8, LW), F32)],
        compiler_params=_cp(2),
    )(act, wa, wi, ba, bi, lam)


def _lru_bwd(cfg, act, wa, wi, ba, bi, lam, hd, dyl, rev):
    nch, nct, nlt, CH = cfg.nbt, 1, cfg.nbl, cfg.TB
    cmap = _lru_block_of_step(cfg, rev)
    d = 1 if rev else 0
    ucol = (DI + 2 * NG * NS) // LW

    def srow(b, sp):
        return b * nch + cmap(nch - 1 - sp)

    def prev_rows(b, sp):
        s = nch - 1 - sp
        cp = cmap(jnp.maximum(s - 1, 0))
        base = (b * nch + cp) * (CH // 8)
        return base + (0 if rev else CH // 8 - 1)

    def lat_row(b, sp):
        c = cmap(nch - 1 - sp)
        return b * nlt + jnp.maximum(c - nct, 0)

    def body(u_ref, wa_ref, wi_ref, ba_ref, bi_ref, lam_ref, h_ref, hp_ref, dy_ref,
             du_ref, dwa_ref, dwi_ref, vec_ref, c_scr):
        b = pl.program_id(0)
        sp_id = pl.program_id(1)
        s = nch - 1 - sp_id

        @pl.when(sp_id == 0)
        def _():
            c_scr[...] = jnp.zeros_like(c_scr)

        @pl.when(jnp.logical_and(b == 0, sp_id == 0))
        def _():
            dwa_ref[...] = jnp.zeros_like(dwa_ref)
            dwi_ref[...] = jnp.zeros_like(dwi_ref)
            vec_ref[...] = jnp.zeros_like(vec_ref)

        c = cmap(s)
        u = u_ref[...]
        r, ig, spl, la, a, g, ginv = _lru_gates(u, wa_ref, wi_ref, ba_ref, bi_ref, lam_ref)
        dh = jnp.where(c < nct, 0.0, dy_ref[...])
        rowi = lax.broadcasted_iota(jnp.int32, (CH, LW), 0)
        lamv, _ = _chunk_scan(_shift_rows(a, 1, 1.0, rev, rowi), dh, c_scr[0:1, :], rev)
        first = CH - 1 if rev else 0
        c_scr[0:1, :] = (a * lamv)[first:first + 1, :]
        hprow = hp_ref[...][(0 if rev else 7):(1 if rev else 8), :]
        hprow = jnp.where(s > 0, hprow, 0.0)
        h_prev = _shift_rows(h_ref[...], 1, hprow, not rev, rowi)
        da = lamv * h_prev
        db = lamv
        iu = ig * u
        dla = da * a - db * iu * (a * a) * ginv
        dr = dla * (-LRU_C * spl)
        di = db * g * u
        du = db * g * ig
        drp = dr * r * (1.0 - r)
        dip = di * ig * (1.0 - ig)
        dus = []
        for k in range(LB):
            sl = slice(k * LBW, (k + 1) * LBW)
            drk = drp[:, sl].astype(BF16)
            dik = dip[:, sl].astype(BF16)
            uk = u[:, sl].astype(BF16)
            dus.append(_dot_nt(drk, wa_ref[k].astype(BF16)) + _dot_nt(dik, wi_ref[k].astype(BF16)))
            dwa_ref[k] += _dot_tn(uk, drk)
            dwi_ref[k] += _dot_tn(uk, dik)
        du_ref[...] = du + jnp.concatenate(dus, axis=1)
        vec_ref[0:1, :] += jnp.sum(drp, axis=0, keepdims=True)
        vec_ref[1:2, :] += jnp.sum(dip, axis=0, keepdims=True)
        dsp = jnp.sum(dla * (-LRU_C * r), axis=0, keepdims=True)
        vec_ref[2:3, :] += dsp * (-_sigmoid(-lam_ref[...]))

    return pl.pallas_call(
        body, name="lru_bwd_rev" if rev else "lru_bwd", grid=(cfg.Bn, nch),
        in_specs=[pl.BlockSpec((CH, LW), lambda b, sp: (srow(b, sp), ucol))] + _lru_w_specs(d) + [
            pl.BlockSpec((CH, LW), lambda b, sp: (srow(b, sp), 0)),
            pl.BlockSpec((8, LW), lambda b, sp: (prev_rows(b, sp), 0)),
            pl.BlockSpec((CH, LW), lambda b, sp: (lat_row(b, sp), 0))],
        out_specs=[pl.BlockSpec((CH, LW), lambda b, sp: (srow(b, sp), 0)),
                   pl.BlockSpec((LB, LBW, LBW), lambda b, sp: (0, 0, 0)),
                   pl.BlockSpec((LB, LBW, LBW), lambda b, sp: (0, 0, 0)),
                   pl.BlockSpec((8, LW), lambda b, sp: (0, 0))],
        out_shape=[jax.ShapeDtypeStruct((cfg.NT, LW), F32),
                   jax.ShapeDtypeStruct((LB, LBW, LBW), F32),
                   jax.ShapeDtypeStruct((LB, LBW, LBW), F32),
                   jax.ShapeDtypeStruct((8, LW), F32)],
        scratch_shapes=[pltpu.VMEM((8, LW), F32)],
        compiler_params=_cp(2),
    )(act, wa, wi, ba, bi, lam, hd, hd, dyl)


HB = 1024


def _post_ssd_fwd(cfg, y, proj, norm_w):
    TB, nbt, nbl = cfg.TB, cfg.nbt, cfg.nbl
    zc = CONVW // HB

    def body(y_ref, z_ref, w_ref, o_ref):
        u = y_ref[...] * _silu(z_ref[...])
        for gi in range(HB // (DI // NG)):
            sl = slice(gi * 256, (gi + 1) * 256)
            ug = u[:, sl]
            rs = lax.rsqrt(jnp.mean(ug * ug, axis=1, keepdims=True) + RMS_EPS)
            o_ref[:, sl] = (ug * rs * w_ref[:, sl]).astype(BF16)

    def st(b, j, cb):
        return (b * nbt + 1 + j, cb)
    return pl.pallas_call(
        body, name="post_ssd_fwd", grid=(cfg.Bn, nbl, DI // HB),
        in_specs=[pl.BlockSpec((TB, HB), st),
                  pl.BlockSpec((TB, HB), lambda b, j, cb: (b * nbt + 1 + j, zc + cb)),
                  pl.BlockSpec((1, HB), lambda b, j, cb: (0, cb))],
        out_specs=pl.BlockSpec((TB, HB), lambda b, j, cb: (b * nbl + j, cb)),
        out_shape=jax.ShapeDtypeStruct((cfg.N, DI), BF16),
        compiler_params=_cp(3),
    )(y, proj, norm_w)


def _post_ssd_bwd(cfg, dproj, dn, y, act, proj, norm_w):
    TB, nbt, nbl = cfg.TB, cfg.nbt, cfg.nbl
    zc = CONVW // HB

    def body(_, dn_ref, y_ref, xh_ref, z_ref, w_ref, dy_ref, dz_ref, acc_ref):
        b = pl.program_id(1)
        j = pl.program_id(2)

        @pl.when(jnp.logical_and(b == 0, j == 0))
        def _():
            acc_ref[...] = jnp.zeros_like(acc_ref)

        @pl.when(j == 0)
        def _():
            dz_ref[...] = jnp.zeros_like(dz_ref)

        @pl.when(j > 0)
        def _():
            latent(dn_ref, y_ref, xh_ref, z_ref, w_ref, dy_ref, dz_ref, acc_ref)

    def latent(dn_ref, y_ref, xh_ref, z_ref, w_ref, dy_ref, dz_ref, acc_ref):
        xh = xh_ref[...]
        z = z_ref[...]
        y = y_ref[...]
        sg = _sigmoid(z)
        sz = z * sg
        dsz = sg * (1.0 + z * (1.0 - sg))
        u = y * sz
        dout = dn_ref[...]
        for gi in range(HB // (DI // NG)):
            sl = slice(gi * 256, (gi + 1) * 256)
            ug0 = u[:, sl]
            rs = lax.rsqrt(jnp.mean(ug0 * ug0, axis=1, keepdims=True) + RMS_EPS)
            ug = ug0 * rs
            do = dout[:, sl]
            acc_ref[0:1, sl] += jnp.sum(do * ug, axis=0, keepdims=True)
            dug = do * w_ref[:, sl]
            du = rs * (dug - ug * jnp.mean(dug * ug, axis=1, keepdims=True))
            dy = du * sz[:, sl]
            dy_ref[:, sl] = dy
            dz_ref[:, sl] = (du * y[:, sl] * dsz[:, sl]).astype(BF16)
            acc_ref[1:2, sl] += jnp.sum(dy * xh[:, sl], axis=0, keepdims=True)

    def st(cb, b, j):
        return (b * nbt + j, cb)

    def la(cb, b, j):
        return (b * nbl + jnp.maximum(j - 1, 0), cb)
    return pl.pallas_call(
        body, name="post_ssd_bwd", grid=(DI // HB, cfg.Bn, nbt),
        in_specs=[_ANY, pl.BlockSpec((TB, HB), la), pl.BlockSpec((TB, HB), st), pl.BlockSpec((TB, HB), st),
                  pl.BlockSpec((TB, HB), lambda cb, b, j: (b * nbt + j, zc + cb)),
                  pl.BlockSpec((1, HB), lambda cb, b, j: (0, cb))],
        out_specs=[pl.BlockSpec((TB, HB), la),
                   pl.BlockSpec((TB, HB), lambda cb, b, j: (b * nbt + j, zc + cb)),
                   pl.BlockSpec((8, HB), lambda cb, b, j: (0, cb))],
        out_shape=[jax.ShapeDtypeStruct((cfg.N, DI), F32), jax.ShapeDtypeStruct((cfg.NT, PM), BF16),
                   jax.ShapeDtypeStruct((8, DI), F32)],
        input_output_aliases={0: 1},
        compiler_params=_cp(3),
    )(dproj, dn, y, act, proj, norm_w)


def _post_lru_fwd(cfg, hf, hb, proj):
    TB, nbt, nbl = cfg.TB, cfg.nbt, cfg.nbl
    gc = (CONVW + DI) // HB

    def body(hf_ref, hb_ref, g_ref, o_ref):
        o_ref[...] = ((hf_ref[...] + hb_ref[...]) * _gelu(g_ref[...])).astype(BF16)

    st = pl.BlockSpec((TB, HB), lambda b, j: (b * nbt + 1 + j, 0))
    return pl.pallas_call(
        body, name="post_lru_fwd", grid=(cfg.Bn, nbl),
        in_specs=[st, st, pl.BlockSpec((TB, HB), lambda b, j: (b * nbt + 1 + j, gc))],
        out_specs=pl.BlockSpec((TB, HB), lambda b, j: (b * nbl + j, 0)),
        out_shape=jax.ShapeDtypeStruct((cfg.N, LW), BF16),
        compiler_params=_cp(2),
    )(hf, hb, proj)


def _post_lru_bwd(cfg, dproj, dv, hf, hb, proj):
    TB, nbt, nbl = cfg.TB, cfg.nbt, cfg.nbl
    gc = (CONVW + DI) // HB

    def body(_, dv_ref, hf_ref, hb_ref, g_ref, dy_ref, dg_ref):
        j = pl.program_id(1)

        @pl.when(j == 0)
        def _():
            dg_ref[...] = jnp.zeros_like(dg_ref)

        @pl.when(j > 0)
        def _():
            gt = g_ref[...]
            dvv = dv_ref[...]
            gl, dgl = _gelu_and_grad(gt)
            dy_ref[...] = dvv * gl
            dg_ref[...] = (dvv * (hf_ref[...] + hb_ref[...]) * dgl).astype(BF16)

    st = pl.BlockSpec((TB, HB), lambda b, j: (b * nbt + j, 0))
    la = pl.BlockSpec((TB, HB), lambda b, j: (b * nbl + jnp.maximum(j - 1, 0), 0))
    gcol = pl.BlockSpec((TB, HB), lambda b, j: (b * nbt + j, gc))
    return pl.pallas_call(
        body, name="post_lru_bwd", grid=(cfg.Bn, nbt),
        in_specs=[_ANY, la, st, st, gcol],
        out_specs=[la, gcol],
        out_shape=[jax.ShapeDtypeStruct((cfg.N, LW), F32), jax.ShapeDtypeStruct((cfg.NT, PM), BF16)],
        input_output_aliases={0: 1},
        compiler_params=_cp(2),
    )(dproj, dv, hf, hb, proj)


def _merge_fwd(cfg, proj, b_gate, br_ssd, br_lru):
    TB, nbt, nbl = cfg.TB, cfg.nbt, cfg.nbl
    mc = (CONVW + DI + LW) // HB

    def body(ms_ref, ml_ref, bg_ref, bs_ref, bl_ref, o_ref):
        gs = _sigmoid(ms_ref[...] + bg_ref[:, :D])
        gl = _sigmoid(ml_ref[...] + bg_ref[:, D:])
        o_ref[...] = (gs * bs_ref[...] + gl * bl_ref[...]).astype(BF16)

    la = pl.BlockSpec((TB, D), lambda b, j: (b * nbl + j, 0))
    return pl.pallas_call(
        body, name="merge_fwd", grid=(cfg.Bn, nbl),
        in_specs=[pl.BlockSpec((TB, HB), lambda b, j: (b * nbt + 1 + j, mc)),
                  pl.BlockSpec((TB, HB), lambda b, j: (b * nbt + 1 + j, mc + 1)),
                  pl.BlockSpec((1, 2 * D), lambda b, j: (0, 0)), la, la],
        out_specs=la,
        out_shape=jax.ShapeDtypeStruct((cfg.N, D), BF16),
        compiler_params=_cp(2),
    )(proj, proj, b_gate, br_ssd, br_lru)


def _merge_bwd(cfg, dmix, proj, b_gate, br_ssd, br_lru):
    TB, nbt, nbl = cfg.TB, cfg.nbt, cfg.nbl
    mc = (CONVW + DI + LW) // HB

    def body(dm_ref, ms_ref, ml_ref, bg_ref, bs_ref, bl_ref, ds_ref, dl_ref, dmg_ref, acc_ref):
        b = pl.program_id(0)
        j = pl.program_id(1)

        @pl.when(jnp.logical_and(b == 0, j == 0))
        def _():
            acc_ref[...] = jnp.zeros_like(acc_ref)

        @pl.when(j == 0)
        def _():
            dmg_ref[...] = jnp.zeros_like(dmg_ref)

        @pl.when(j > 0)
        def _():
            latent(dm_ref, ms_ref, ml_ref, bg_ref, bs_ref, bl_ref, ds_ref, dl_ref, dmg_ref, acc_ref)

    def latent(dm_ref, ms_ref, ml_ref, bg_ref, bs_ref, bl_ref, ds_ref, dl_ref, dmg_ref, acc_ref):
        dm = dm_ref[...]
        gs = _sigmoid(ms_ref[...] + bg_ref[:, :D])
        gl = _sigmoid(ml_ref[...] + bg_ref[:, D:])
        ds_ref[...] = (dm * gs).astype(BF16)
        dl_ref[...] = (dm * gl).astype(BF16)
        dps = dm * bs_ref[...] * gs * (1.0 - gs)
        dpl = dm * bl_ref[...] * gl * (1.0 - gl)
        dmg_ref[:, :D] = dps.astype(BF16)
        dmg_ref[:, D:] = dpl.astype(BF16)
        acc_ref[0:1, :D] += jnp.sum(dps, axis=0, keepdims=True)
        acc_ref[0:1, D:] += jnp.sum(dpl, axis=0, keepdims=True)

    la = pl.BlockSpec((TB, D), lambda b, j: (b * nbl + jnp.maximum(j - 1, 0), 0))
    return pl.pallas_call(
        body, name="merge_bwd", grid=(cfg.Bn, nbt),
        in_specs=[la, pl.BlockSpec((TB, HB), lambda b, j: (b * nbt + j, mc)),
                  pl.BlockSpec((TB, HB), lambda b, j: (b * nbt + j, mc + 1)),
                  pl.BlockSpec((1, 2 * D), lambda b, j: (0, 0)), la, la],
        out_specs=[la, la, pl.BlockSpec((TB, 2 * D), lambda b, j: (b * nbt + j, mc // 2)),
                   pl.BlockSpec((8, 2 * D), lambda b, j: (0, 0))],
        out_shape=[jax.ShapeDtypeStruct((cfg.N, D), BF16), jax.ShapeDtypeStruct((cfg.N, D), BF16),
                   jax.ShapeDtypeStruct((cfg.NT, PM), BF16), jax.ShapeDtypeStruct((8, 2 * D), F32)],
        compiler_params=_cp(2),
    )(dmix, proj, proj, b_gate, br_ssd, br_lru)


def _resid1_fwd(cfg, x2, x_mix, gate1, shift2, scale2, ln1_g, ln1_b):
    TB, nbt, nbl = cfg.TB, cfg.nbt, cfg.nbl

    def body(x_ref, xm_ref, g1_ref, sh_ref, sc_ref, lg_ref, lb_ref, x1_ref, h2_ref):
        r1 = ALPHA * x_ref[...] + g1_ref[...] * xm_ref[...]
        xh, _ = _ln(r1)
        x1 = xh * lg_ref[...] + lb_ref[...]
        x1_ref[...] = x1
        xh2, _ = _ln(x1)
        h2_ref[...] = (xh2 * (1.0 + sc_ref[...]) + sh_ref[...]).astype(BF16)

    la = pl.BlockSpec((TB, D), lambda b, j: (b * nbl + j, 0))
    ex = pl.BlockSpec((None, 1, D), lambda b, j: (b, 0, 0))
    vec = pl.BlockSpec((1, D), lambda b, j: (0, 0))
    return pl.pallas_call(
        body, name="resid1_fwd", grid=(cfg.Bn, nbl),
        in_specs=[la, la, ex, ex, ex, vec, vec],
        out_specs=[la, la],
        out_shape=[jax.ShapeDtypeStruct((cfg.N, D), F32), jax.ShapeDtypeStruct((cfg.N, D), BF16)],
        compiler_params=_cp(2),
    )(x2, x_mix, gate1, shift2, scale2, ln1_g, ln1_b)


def _resid1_bwd(cfg, dh2, dx1p, x1, x2, x_mix, gate1, scale2, ln1_g):
    TB, nbt, nbl = cfg.TB, cfg.nbt, cfg.nbl

    def body(dh2_ref, dx1p_ref, x1_ref, x_ref, xm_ref, g1_ref, sc_ref, lg_ref,
             dxm_ref, dxp_ref, ex_ref, gl_ref):
        b = pl.program_id(0)
        j = pl.program_id(1)

        @pl.when(j == 0)
        def _():
            ex_ref[...] = jnp.zeros_like(ex_ref)

        @pl.when(jnp.logical_and(b == 0, j == 0))
        def _():
            gl_ref[...] = jnp.zeros_like(gl_ref)

        dh2 = dh2_ref[...]
        xh2, rs2 = _ln(x1_ref[...])
        ex_ref[0:1, :] += jnp.sum(dh2, axis=0, keepdims=True)
        ex_ref[1:2, :] += jnp.sum(dh2 * xh2, axis=0, keepdims=True)
        dx1 = dx1p_ref[...] + _ln_bwd(dh2 * (1.0 + sc_ref[...]), xh2, rs2)
        xm = xm_ref[...]
        g1 = g1_ref[...]
        r1 = ALPHA * x_ref[...] + g1 * xm
        xh1, rs1 = _ln(r1)
        gl_ref[0:1, :] += jnp.sum(dx1 * xh1, axis=0, keepdims=True)
        gl_ref[1:2, :] += jnp.sum(dx1, axis=0, keepdims=True)
        dr1 = _ln_bwd(dx1 * lg_ref[...], xh1, rs1)
        ex_ref[2:3, :] += jnp.sum(dr1 * xm, axis=0, keepdims=True)
        dxm_ref[...] = (dr1 * g1).astype(BF16)
        dxp_ref[...] = ALPHA * dr1

    la = pl.BlockSpec((TB, D), lambda b, j: (b * nbl + j, 0))
    ex = pl.BlockSpec((None, 1, D), lambda b, j: (b, 0, 0))
    vec = pl.BlockSpec((1, D), lambda b, j: (0, 0))
    return pl.pallas_call(
        body, name="resid1_bwd", grid=(cfg.Bn, nbl),
        in_specs=[la, la, la, la, la, ex, ex, vec],
        out_specs=[la, la, pl.BlockSpec((None, 8, D), lambda b, j: (b, 0, 0)),
                   pl.BlockSpec((8, D), lambda b, j: (0, 0))],
        out_shape=[jax.ShapeDtypeStruct((cfg.N, D), BF16), jax.ShapeDtypeStruct((cfg.N, D), F32),
                   jax.ShapeDtypeStruct((cfg.Bn, 8, D), F32), jax.ShapeDtypeStruct((8, D), F32)],
        compiler_params=_cp(2),
    )(dh2, dx1p, x1, x2, x_mix, gate1, scale2, ln1_g)


def _final_fwd_bwd(cfg, x1, mlp, b2, gate2, ln2_g, ln2_b, target):
    TB, nbl = cfg.TB, cfg.nbl

    def body(x1_ref, m_ref, b2_ref, g2_ref, lg_ref, lb_ref, t_ref, dm_ref, dx_ref, ex_ref, gl_ref):
        b = pl.program_id(0)
        j = pl.program_id(1)

        @pl.when(j == 0)
        def _():
            ex_ref[...] = jnp.zeros_like(ex_ref)

        @pl.when(jnp.logical_and(b == 0, j == 0))
        def _():
            gl_ref[...] = jnp.zeros_like(gl_ref)

        mv = m_ref[...] + b2_ref[...]
        g2 = g2_ref[...]
        r2 = ALPHA * x1_ref[...] + g2 * mv
        xh, rs = _ln(r2)
        lg = lg_ref[...]
        x2 = xh * lg + lb_ref[...]
        err = x2 - t_ref[...]
        ls = jnp.sum(jnp.sum(err * err, axis=1, keepdims=True), axis=0, keepdims=True) * (0.5 / D)
        gl_ref[3:4, :] += ls
        dx2 = err * (1.0 / D)
        gl_ref[0:1, :] += jnp.sum(dx2 * xh, axis=0, keepdims=True)
        gl_ref[1:2, :] += jnp.sum(dx2, axis=0, keepdims=True)
        dr2 = _ln_bwd(dx2 * lg, xh, rs)
        ex_ref[0:1, :] += jnp.sum(dr2 * mv, axis=0, keepdims=True)
        dmv = dr2 * g2
        gl_ref[2:3, :] += jnp.sum(dmv, axis=0, keepdims=True)
        dm_ref[...] = dmv.astype(BF16)
        dx_ref[...] = ALPHA * dr2

    la = pl.BlockSpec((TB, D), lambda b, j: (b * nbl + j, 0))
    ex = pl.BlockSpec((None, 1, D), lambda b, j: (b, 0, 0))
    vec = pl.BlockSpec((1, D), lambda b, j: (0, 0))
    return pl.pallas_call(
        body, name="final_fwd_bwd", grid=(cfg.Bn, nbl),
        in_specs=[la, la, vec, ex, vec, vec, la],
        out_specs=[la, la, pl.BlockSpec((None, 8, D), lambda b, j: (b, 0, 0)),
                   pl.BlockSpec((8, D), lambda b, j: (0, 0))],
        out_shape=[jax.ShapeDtypeStruct((cfg.N, D), BF16), jax.ShapeDtypeStruct((cfg.N, D), F32),
                   jax.ShapeDtypeStruct((cfg.Bn, 8, D), F32), jax.ShapeDtypeStruct((8, D), F32)],
        compiler_params=_cp(2),
    )(x1, mlp, b2, gate2, ln2_g, ln2_b, target)


def _ln_mod_bwd(cfg, dh_a, ddt_raw, w_dt, ctx2, x2, scale_tab, dxp):
    TB, nbt, nbl = cfg.TB, cfg.nbt, cfg.nbl

    def body(da_ref, dd_ref, wd_ref, c_ref, x_ref, sc_ref, dxp_ref, gx_ref, acc_ref):
        j = pl.program_id(1)

        @pl.when(j <= 1)
        def _():
            acc_ref[...] = jnp.zeros_like(acc_ref)

        dh = da_ref[...] + _dot_nt(dd_ref[...].astype(BF16), wd_ref[...])
        xhat, rs = _ln(jnp.where(j == 0, c_ref[...], x_ref[...]))
        acc_ref[0:1, :] += jnp.sum(dh, axis=0, keepdims=True)
        acc_ref[1:2, :] += jnp.sum(dh * xhat, axis=0, keepdims=True)
        gx_ref[...] = dxp_ref[...] + _ln_bwd(dh * (1.0 + sc_ref[...]), xhat, rs)

    st = pl.BlockSpec((TB, D), lambda b, j: (b * nbt + j, 0))
    la = pl.BlockSpec((TB, D), lambda b, j: (b * nbl + jnp.maximum(j - 1, 0), 0))
    return pl.pallas_call(
        body, name="ln_mod_bwd", grid=(cfg.Bn, nbt),
        in_specs=[st, pl.BlockSpec((TB, DTW), lambda b, j: (b * nbt + j, 0)),
                  pl.BlockSpec((D, DTW), lambda b, j: (0, 0)),
                  pl.BlockSpec((TB, D), lambda b, j: (b, 0)), la,
                  pl.BlockSpec((None, 1, D), lambda b, j: (2 * b + jnp.minimum(j, 1), 0, 0)), la],
        out_specs=[la, pl.BlockSpec((None, 8, D), lambda b, j: (2 * b + jnp.minimum(j, 1), 0, 0))],
        out_shape=[jax.ShapeDtypeStruct((cfg.N, D), F32), jax.ShapeDtypeStruct((2 * cfg.Bn, 8, D), F32)],
        compiler_params=_cp(2),
    )(dh_a, ddt_raw, w_dt, ctx2, x2, scale_tab, dxp)


def _perm_w_in(w_in):
    w_main = jnp.concatenate([w_in[:, 0:3072], w_in[:, 4160:5184], w_in[:, 3136:4160], w_in[:, 5184:10304]], axis=1)
    w_dt = jnp.pad(w_in[:, 3072:3136], ((0, 0), (0, DTW - 2 * NH)))
    return w_main, w_dt


def _unperm_w_in(dw_main, dw_dt):
    return jnp.concatenate([dw_main[:, 0:3072], dw_dt[:, :2 * NH], dw_main[:, 4096:5120],
                            dw_main[:, 3072:4096], dw_main[:, 5120:]], axis=1)


def _unpack_rest(rest_all):
    out, off = {}, 0
    for n, shp, axis in _BIG[1:]:
        shard_shape = (shp[0] // NDEV, shp[1]) if axis == 0 else (shp[0], shp[1] // NDEV)
        r = math.prod(shard_shape) // 1024
        out[n] = _from_slots(rest_all[:, off:off + r, :].reshape((NDEV,) + shard_shape), axis)
        off += r
    return out


def _local_step(cfg, x, ctx, target, m, mc, W, w_in_shard, rest_payload):
    Bn, T, Tc = cfg.Bn, cfg.T, cfg.Tc
    NT, N = cfg.NT, cfg.N
    ctx2, x2 = ctx.reshape(Bn * Tc, D), x.reshape(N, D)
    mch = [m[:, i * D:(i + 1) * D] for i in range(NMOD)]
    ctx_sh = jnp.broadcast_to(mc[None, :D], (Bn, D))
    ctx_sc = jnp.broadcast_to(mc[None, D:], (Bn, D))
    shift_tab = jnp.stack([ctx_sh, mch[0]], axis=1).reshape(2 * Bn, 1, D)
    scale_tab = jnp.stack([ctx_sc, mch[1]], axis=1).reshape(2 * Bn, 1, D)
    gate1 = mch[2].reshape(Bn, 1, D)
    shift2 = mch[3].reshape(Bn, 1, D)
    scale2 = mch[4].reshape(Bn, 1, D)
    gate2 = mch[5].reshape(Bn, 1, D)

    conv_w = jnp.concatenate([W["ssd_conv_w"], W["lru_conv_w"]], axis=1)
    conv_b = jnp.concatenate([W["ssd_conv_b"], W["lru_conv_b"]], axis=1)
    dt_bias = jnp.pad(W["ssd_dt_bias"].reshape(1, 2 * NH), ((0, 0), (0, DTW - 2 * NH)))
    a_log = jnp.pad(W["ssd_a_log"].reshape(1, 2 * NH), ((0, 0), (0, DTW - 2 * NH)))
    dvec = jnp.repeat(W["ssd_d"].reshape(NH), HD).reshape(1, DI)
    lba = W["lru_ba"].reshape(2, 1, LW)
    lbi = W["lru_bi"].reshape(2, 1, LW)
    llam = W["lru_lambda"].reshape(2, 1, LW)

    h, w_in_all = _ln_mod_fwd(cfg, ctx2, x2, shift_tab, scale_tab, xch=(w_in_shard, TWO_LEVEL))
    w_main, w_dt = _perm_w_in(_from_slots(w_in_all, 1))
    proj, rest_all = _mm(h, w_main, "nn", "mm_proj", tm=1024, tn=2048, tk=1024, xch=(rest_payload, True))
    W = dict(W, w_main=w_main, w_dt=w_dt, **_unpack_rest(rest_all))
    dt_raw = _mm(h, W["w_dt"], "nn", "mm_dt", tm=512, tn=DTW, tk=1024)
    dt, dtg, cumg, cumTg = _dt_fwd(cfg, dt_raw, dt_bias, a_log)
    act, sgrad = _conv_fwd(cfg, proj, conv_w, conv_b)
    y_f, hs_f = _ssd_fwd(cfg, act, dtg, cumg, cumTg, False)
    y, hs_b = _ssd_fwd(cfg, act, dtg, cumg, cumTg, True, y_other=y_f, dvec=dvec)
    hss = [hs_f, hs_b]
    hls = [_lru_fwd(cfg, act, W["lru_wa"], W["lru_wi"], lba, lbi, llam, rev) for rev in (False, True)]
    nssd = _post_ssd_fwd(cfg, y, proj, W["ssd_norm_w"])
    vlru = _post_lru_fwd(cfg, hls[0], hls[1], proj)
    br_ssd = _mm(nssd, W["w_br_ssd"], "nn", "mm_br_ssd", tm=1024, tn=1024, tk=1024)
    br_lru = _mm(vlru, W["w_br_lru"], "nn", "mm_br_lru", tm=1024, tn=1024, tk=1024)
    mix = _merge_fwd(cfg, proj, W["b_gate"], br_ssd, br_lru)
    x_mix = _mm(mix, W["w_out"], "nn", "mm_out", tm=1024, tn=1024, tk=1024)
    x1, h2 = _resid1_fwd(cfg, x2, x_mix, gate1, shift2, scale2, W["ln1_g"], W["ln1_b"])
    a1, actm = _mm_mlp1(h2, W["w_mlp1"], W["b_mlp1"])
    mlp = _mm(actm, W["w_mlp2"], "nn", "mm_mlp2", tm=1024, tn=1024, tk=2048)
    dmlp, dx1p, ex2, gl2 = _final_fwd_bwd(cfg, x1, mlp, W["b_mlp2"], gate2, W["ln2_g"], W["ln2_b"],
                                          target.reshape(N, D))

    g = {}
    g["ln2_g"], g["ln2_b"], g["b_mlp2"] = gl2[0:1], gl2[1:2], gl2[2:3]
    loss_partial = gl2[3, 0]
    gw = {}
    gw["w_mlp2"] = _mm(actm, dmlp, "tn", "mm_dw_mlp2", BF16, tm=1024, tn=1024, tk=1024)
    da1, accb1 = _mm_dact(dmlp, W["w_mlp2"], a1)
    g["b_mlp1"] = accb1[0:1]
    dh2 = _mm(da1, W["w_mlp1"], "nt", "mm_dh2", tm=1024, tn=1024, tk=2048)
    gw["w_mlp1"] = _mm(h2, da1, "tn", "mm_dw_mlp1", BF16, tm=1024, tn=1024, tk=1024)
    dx_mix, dxp, ex1, gl1 = _resid1_bwd(cfg, dh2, dx1p, x1, x2, x_mix, gate1, scale2, W["ln1_g"])
    g["ln1_g"], g["ln1_b"] = gl1[0:1], gl1[1:2]
    dmix = _mm(dx_mix, W["w_out"], "nt", "mm_dmix", tm=1024, tn=1024, tk=1024)
    gw["w_out"] = _mm(mix, dx_mix, "tn", "mm_dw_out", BF16, tm=1024, tn=1024, tk=1024)
    dbs, dbl, dproj, accg = _merge_bwd(cfg, dmix, proj, W["b_gate"], br_ssd, br_lru)
    g["b_gate"] = accg[0:1]
    dnssd = _mm(dbs, W["w_br_ssd"], "nt", "mm_dnssd", tm=1024, tn=1024, tk=1024)
    gw["w_br_ssd"] = _mm(nssd, dbs, "tn", "mm_dw_br_ssd", BF16, tm=1024, tn=1024, tk=1024)
    dvlru = _mm(dbl, W["w_br_lru"], "nt", "mm_dvlru", tm=1024, tn=1024, tk=1024)
    gw["w_br_lru"] = _mm(vlru, dbl, "tn", "mm_dw_br_lru", BF16, tm=1024, tn=1024, tk=1024)
    dy, dproj, accs = _post_ssd_bwd(cfg, dproj, dnssd, y, act, proj, W["ssd_norm_w"])
    g["ssd_norm_w"] = accs[0:1]
    dD_cols = accs[1:2]
    dyl, dproj = _post_lru_bwd(cfg, dproj, dvlru, hls[0], hls[1], proj)

    rest_slots = jnp.concatenate([_to_slots(gw[n], axis).reshape(NDEV, -1, 1024) for n, _, axis in _BIG[1:]], axis=1)
    xres = {}
    dxh, dBs, dCs, dAs, dxxs, dus = [], [], [], [], [], []
    dwas, dwis, lvecs = [], [], []
    for i, rev in enumerate((False, True)):
        if i == 0:
            o, xres["rs_rest"] = _ssd_bwd(cfg, act, dtg, cumg, cumTg, hss[i], dy, rev, xch=(rest_slots, False))
        else:
            o = _ssd_bwd(cfg, act, dtg, cumg, cumTg, hss[i], dy, rev)
        dxh.append(o[0]); dBs.append(o[1]); dCs.append(o[2]); dAs.append(o[3]); dxxs.append(o[4])
        du, dwa, dwi, lv = _lru_bwd(cfg, act, W["lru_wa"], W["lru_wi"], lba, lbi, llam, hls[i], dyl, rev)
        dus.append(du); dwas.append(dwa); dwis.append(dwi); lvecs.append(lv)
    lru_payload = jnp.stack([jnp.stack(dwas), jnp.stack(dwis)]).reshape(-1, 1024)
    g["lru_ba"] = jnp.stack([lvecs[0][0], lvecs[1][0]])
    g["lru_bi"] = jnp.stack([lvecs[0][1], lvecs[1][1]])
    g["lru_lambda"] = jnp.stack([lvecs[0][2], lvecs[1][2]])

    ddt_raw, accdt, dw_dt = _dt_bwd(cfg, dAs, dxxs, dt_raw, dt, dt_bias, a_log, h)
    g["ssd_a_log"] = accdt[0, :2 * NH].reshape(2, NH)
    g["ssd_dt_bias"] = accdt[1, :2 * NH].reshape(2, NH)

    (dproj, accx), xres["ag_lru"] = _conv_bwd(cfg, "conv_bwd_x", dproj, proj, conv_w, sgrad, [dxh[0], dxh[1]], 0, DI,
                                              skip=(dy, dvec), xch=(lru_payload, True))
    dproj, accB = _conv_bwd(cfg, "conv_bwd_b", dproj, proj, conv_w, sgrad, [dBs[0], dBs[1]], DI, NG * NS)
    dproj, accC = _conv_bwd(cfg, "conv_bwd_c", dproj, proj, conv_w, sgrad, [dCs[0], dCs[1]], DI + NG * NS, NG * NS)
    dproj, accl = _conv_bwd(cfg, "conv_bwd_lru", dproj, proj, conv_w, None, [dus[0], dus[1]], DI + 2 * NG * NS, LW)
    accssd = jnp.concatenate([accx, accB, accC], axis=1)
    g["ssd_conv_w"], g["ssd_conv_b"] = accssd[0:4], accssd[4:5]
    g["lru_conv_w"], g["lru_conv_b"] = accl[0:4], accl[4:5]
    dw_main = _mm(h, dproj, "tn", "mm_dw_main", BF16, tm=1024, tn=2048, tk=1024)
    w_in_slots = _to_slots(_unperm_w_in(dw_main, dw_dt.astype(BF16)), 1)
    dh_a, xres["rs_w_in"] = _mm(dproj, W["w_main"], "nt", "mm_dh_main", tm=1024, tn=1024, tk=2048,
                                xch=(w_in_slots, False))
    grad_x, acct = _ln_mod_bwd(cfg, dh_a, ddt_raw, W["w_dt"], ctx2, x2, scale_tab, dxp)
    acct = acct.reshape(Bn, 2, 8, D)
    dm = jnp.concatenate([acct[:, 1, 0], acct[:, 1, 1], ex1[:, 2], ex1[:, 0], ex1[:, 1], ex2[:, 0]], axis=1)
    dmc = jnp.concatenate([acct[:, 0, 0], acct[:, 0, 1]], axis=1)
    g["ssd_d_cols"] = dD_cols
    return loss_partial, grad_x.reshape(Bn, T, D), g, dm, dmc, xres


MESH = pl.DeviceIdType.MESH
_HBM = pl.BlockSpec(memory_space=pltpu.HBM)


def _me():
    return 4 * lax.axis_index("x") + 2 * lax.axis_index("y") + lax.axis_index("c")


def _peer(k):
    px = (lax.axis_index("x") + ((k >> 2) & 1)) % 2
    py = (lax.axis_index("y") + ((k >> 1) & 1)) % 2
    pc = (lax.axis_index("c") + (k & 1)) % 2
    return (px, py, pc), 4 * px + 2 * py + pc


def _xchg_copies(x_ref, o_ref, send_sems, recv_sems, loc_sem, gather):
    me = _me()
    src_me = x_ref if gather else x_ref.at[me]
    loc = pltpu.make_async_copy(src_me, o_ref.at[me], loc_sem)
    sends, recvs = [], []
    for k in range(1, NDEV):
        peer, pid = _peer(k)
        sends.append(pltpu.make_async_remote_copy(
            src_ref=x_ref if gather else x_ref.at[pid], dst_ref=o_ref.at[me],
            send_sem=send_sems.at[k - 1], recv_sem=recv_sems.at[k - 1],
            device_id=peer, device_id_type=MESH))
        recvs.append(pltpu.make_async_remote_copy(
            src_ref=src_me, dst_ref=o_ref.at[pid],
            send_sem=send_sems.at[k - 1], recv_sem=recv_sems.at[k - 1],
            device_id=peer, device_id_type=MESH))
    return loc, sends, recvs


def _xchg_start(*refs, gather):
    loc, sends, _ = _xchg_copies(*refs, gather)
    loc.start()
    for cp in sends:
        cp.start()


def _xchg_wait(*refs, gather):
    loc, sends, recvs = _xchg_copies(*refs, gather)
    for cp in recvs:
        cp.wait_recv()
    for cp in sends:
        cp.wait_send()
    loc.wait()


_XCHG_SCRATCH = [pltpu.SemaphoreType.DMA((NDEV - 1,)), pltpu.SemaphoreType.DMA((NDEV - 1,)), pltpu.SemaphoreType.DMA]


def _xchg_out_shape(x, gather):
    return jax.ShapeDtypeStruct((NDEV,) + tuple(x.shape if gather else x.shape[1:]), x.dtype)


def _exchange(x, name, gather):
    def body(x_ref, o_ref, send_sems, recv_sems, loc_sem):
        _xchg_start(x_ref, o_ref, send_sems, recv_sems, loc_sem, gather=gather)
        _xchg_wait(x_ref, o_ref, send_sems, recv_sems, loc_sem, gather=gather)

    return pl.pallas_call(
        body, name=name, out_shape=_xchg_out_shape(x, gather),
        in_specs=[_HBM], out_specs=_HBM, scratch_shapes=_XCHG_SCRATCH,
    )(x)


TWO_LEVEL = "two_level"


def _two_level_copies(x_ref, o_ref, send_sems, recv_sems, loc_sem):
    mx, my, mc = lax.axis_index("x"), lax.axis_index("y"), lax.axis_index("c")
    me, sibling = (mx, my, mc), (mx, my, 1 - mc)
    chips = [(1 - mx, my), (mx, 1 - my), (1 - mx, 1 - my)]

    def slot(px, py, pc):
        return o_ref.at[4 * px + 2 * py + pc]

    def copy(k, block, to, src=None):
        return pltpu.make_async_remote_copy(
            src_ref=slot(*block) if src is None else src, dst_ref=slot(*block),
            send_sem=send_sems.at[k], recv_sem=recv_sems.at[k], device_id=to, device_id_type=MESH)

    mine = pltpu.make_async_copy(x_ref, slot(*me), loc_sem)
    first = [copy(0, me, sibling, src=x_ref)] + [copy(1 + j, me, (*chip, mc), src=x_ref) for j, chip in enumerate(chips)]
    passed = [copy(4 + j, (*chip, mc), sibling) for j, chip in enumerate(chips)]
    landed = [copy(1 + j, (*chip, mc), me) for j, chip in enumerate(chips)]
    from_sibling = [copy(0, sibling, me)] + [copy(4 + j, (*chip, 1 - mc), me) for j, chip in enumerate(chips)]
    return mine, first, passed, landed, from_sibling


def _two_level_start(*refs):
    mine, first, _, _, _ = _two_level_copies(*refs)
    mine.start()
    for cp in first:
        cp.start()


def _two_level_finish(*refs):
    mine, first, passed, landed, from_sibling = _two_level_copies(*refs)
    for cp, fwd in zip(landed, passed):
        cp.wait_recv()
        fwd.start()
    for cp in from_sibling:
        cp.wait_recv()
    for cp in first + passed:
        cp.wait_send()
    mine.wait()


def _hosted_call(body, xch, *, name, grid, in_specs, out_specs, out_shape, scratch_shapes, compiler_params, args,
                 aliases=None):
    aliases = aliases or {}
    if xch is None:
        return pl.pallas_call(body, name=name, grid=grid, in_specs=in_specs, out_specs=out_specs,
                              out_shape=out_shape, scratch_shapes=scratch_shapes, input_output_aliases=aliases,
                              compiler_params=compiler_params)(*args)
    xv, gather = xch
    n_in, n_out, n_scr = len(in_specs), len(out_specs), len(scratch_shapes)

    def wrapped(*refs):
        ins = refs[:n_in]
        x_ref = refs[n_in]
        outs = refs[n_in + 1:n_in + 1 + n_out]
        o_ref = refs[n_in + 1 + n_out]
        scr = refs[n_in + 2 + n_out:]
        own, sems = scr[:n_scr], scr[n_scr:]
        first = functools.reduce(jnp.logical_and, [pl.program_id(a) == 0 for a in range(len(grid))])
        last = functools.reduce(jnp.logical_and, [pl.program_id(a) == grid[a] - 1 for a in range(len(grid))])

        @pl.when(first)
        def _():
            if gather == TWO_LEVEL:
                _two_level_start(x_ref, o_ref, *sems)
            else:
                _xchg_start(x_ref, o_ref, *sems, gather=gather)

        body(*ins, *outs, *own)

        @pl.when(last)
        def _():
            if gather == TWO_LEVEL:
                _two_level_finish(x_ref, o_ref, *sems)
            else:
                _xchg_wait(x_ref, o_ref, *sems, gather=gather)

    res = pl.pallas_call(
        wrapped, name=name, grid=grid, in_specs=list(in_specs) + [_HBM], out_specs=list(out_specs) + [_HBM],
        out_shape=list(out_shape) + [_xchg_out_shape(xv, gather)],
        scratch_shapes=list(scratch_shapes) + _XCHG_SCRATCH, input_output_aliases=aliases,
        compiler_params=compiler_params,
    )(*args, xv)
    return list(res[:n_out]), res[n_out]


def _row_tile(R, cap, mult=8):
    best = mult
    t = mult
    while t <= min(R, cap):
        if R % t == 0:
            best = t
        t += mult
    assert R % best == 0, R
    return best


def _sum_slots(x, name, xch=None):
    _, R, C = x.shape
    tr = _row_tile(R, 256, 16 if x.dtype == BF16 else 8)

    def body(x_ref, o_ref):
        o_ref[...] = _slot_sum(x_ref)

    res = _hosted_call(
        body, xch, name=name, grid=(R // tr,),
        in_specs=[pl.BlockSpec((NDEV, tr, C), lambda i: (0, i, 0))],
        out_specs=[pl.BlockSpec((tr, C), lambda i: (i, 0))],
        out_shape=[jax.ShapeDtypeStruct((R, C), F32)],
        scratch_shapes=[], compiler_params=_cp(1), args=(x,))
    if xch is None:
        return res[0]
    return res[0][0], res[1]


def _slot_sum(x_ref):
    acc = x_ref[0].astype(F32)
    for i in range(1, NDEV):
        acc = acc + x_ref[i].astype(F32)
    return acc


def _sum_adamw(slots, w, m, v, name, xch=None):
    _, R, C = slots.shape
    tr = _row_tile(R, 128, 16 if slots.dtype == BF16 else 8)

    def body(x_ref, w_ref, m_ref, v_ref, g_ref, d_ref, nm_ref, nv_ref):
        g_ref[...] = _slot_sum(x_ref)
        _adamw_update(w_ref, g_ref, m_ref, v_ref, d_ref, nm_ref, nv_ref)

    blk = pl.BlockSpec((tr, C), lambda i: (i, 0))
    res = _hosted_call(
        body, xch, name=name, grid=(R // tr,),
        in_specs=[pl.BlockSpec((NDEV, tr, C), lambda i: (0, i, 0)), blk, blk, blk],
        out_specs=[blk] * 4, out_shape=[jax.ShapeDtypeStruct((R, C), F32)] * 4,
        scratch_shapes=[], compiler_params=_cp(1), args=(slots, w, m, v))
    if xch is None:
        return res
    return res[0], res[1]


def _adamw_update(w_ref, g_ref, m_ref, v_ref, d_ref, nm_ref, nv_ref):
    c1 = 1.0 / (1.0 - ADAM_B1 ** ADAM_STEP)
    c2 = 1.0 / (1.0 - ADAM_B2 ** ADAM_STEP)
    gv = g_ref[...]
    nm = ADAM_B1 * m_ref[...] + (1.0 - ADAM_B1) * gv
    nv = ADAM_B2 * v_ref[...] + (1.0 - ADAM_B2) * (gv * gv)
    d_ref[...] = -ADAM_LR * ((nm * c1) / (jnp.sqrt(nv * c2) + ADAM_EPS) + ADAM_WD * w_ref[...])
    nm_ref[...] = nm
    nv_ref[...] = nv


def _adamw_many(ws, gs, ms, vs):
    n = len(ws)

    def body(*refs):
        for i in range(n):
            _adamw_update(refs[i], refs[n + i], refs[2 * n + i], refs[3 * n + i],
                          refs[4 * n + i], refs[5 * n + i], refs[6 * n + i])

    shapes = [jax.ShapeDtypeStruct(w.shape, F32) for w in ws]
    res = pl.pallas_call(
        body, name="adamw_small", out_shape=shapes * 3,
        compiler_params=pltpu.CompilerParams(vmem_limit_bytes=VMEM_LIMIT_BYTES),
    )(*ws, *gs, *ms, *vs)
    return res[:n], res[n:2 * n], res[2 * n:]


def _adamw(w, g, m, v, name):
    R, C = w.shape
    tr = _row_tile(R, 256)

    def body(w_ref, g_ref, m_ref, v_ref, d_ref, nm_ref, nv_ref):
        _adamw_update(w_ref, g_ref, m_ref, v_ref, d_ref, nm_ref, nv_ref)

    blk = pl.BlockSpec((tr, C), lambda i: (i, 0))
    return pl.pallas_call(
        body, name=name, grid=(R // tr,),
        in_specs=[blk] * 4, out_specs=[blk] * 3,
        out_shape=[jax.ShapeDtypeStruct((R, C), F32)] * 3,
        compiler_params=_cp(1),
    )(w, g, m, v)


def _mod_fwd(c_rows, w_shard, b_shard):
    def body(c_ref, w_ref, b_ref, o_ref):
        s = _silu(c_ref[...]).astype(BF16)
        o_ref[...] = _dot(s, w_ref[...].astype(BF16)) + b_ref[...]

    return pl.pallas_call(
        body, name="mod_fwd",
        out_shape=jax.ShapeDtypeStruct((c_rows.shape[0], w_shard.shape[1]), F32),
        compiler_params=pltpu.CompilerParams(vmem_limit_bytes=VMEM_LIMIT_BYTES),
    )(c_rows, w_shard, b_shard)


def _mod_bwd(c_rows, dm_all, dm_shard, w_shard):
    nrow = c_rows.shape[0]

    def body(c_ref, da_ref, ds_ref, w_ref, gw_ref, gb_ref, cc_ref):
        s = _silu(c_ref[...]).astype(BF16)
        ds = ds_ref[...]
        gw_ref[...] = _dot_tn(s, ds.astype(BF16))
        gb_ref[...] = jnp.sum(da_ref[...], axis=0, keepdims=True)
        rowi = lax.broadcasted_iota(jnp.int32, ds.shape, 0)
        dmc = jnp.sum(jnp.where(rowi % 8 >= 4, ds, 0.0), axis=0, keepdims=True)
        dmc8 = jnp.broadcast_to(dmc, (8, ds.shape[1])).astype(BF16)
        cc_ref[...] = _dot_nt(dmc8, w_ref[...].astype(BF16))

    return pl.pallas_call(
        body, name="mod_bwd",
        out_shape=[jax.ShapeDtypeStruct(w_shard.shape, F32),
                   jax.ShapeDtypeStruct((1, dm_all.shape[1]), F32),
                   jax.ShapeDtypeStruct((8, D), F32)],
        compiler_params=pltpu.CompilerParams(vmem_limit_bytes=VMEM_LIMIT_BYTES),
    )(c_rows, dm_all, dm_shard, w_shard)


def _small_finish(cc_pre, c_ctx, dd_cols):
    def body(cc_ref, c_ref, dd_ref, gc_ref, gd_ref):
        gc_ref[...] = cc_ref[...] * _silu_grad(c_ref[...])
        gd_ref[...] = jnp.sum(dd_ref[...], axis=1, keepdims=True)

    return pl.pallas_call(
        body, name="small_finish",
        out_shape=[jax.ShapeDtypeStruct((1, D), F32), jax.ShapeDtypeStruct((NH, 1), F32)],
    )(cc_pre, c_ctx, dd_cols)


_BIG = (("w_in", (D, 10304), 1), ("w_br_ssd", (DI, D), 0), ("w_br_lru", (LW, D), 0), ("w_out", (D, D), 0),
        ("w_mlp1", (D, MLP), 1), ("w_mlp2", (MLP, D), 0))
_SMALL_SH = (("ssd_conv_w", (4, 4096)), ("lru_conv_w", (4, LW)), ("lru_ba", (2, LW)), ("lru_bi", (2, LW)),
             ("lru_lambda", (2, LW)))
_REPL = (("c_ctx", (D,)), ("b_gate", (2 * D,)), ("ssd_conv_b", (4096,)), ("ssd_dt_bias", (2, NH)),
         ("ssd_a_log", (2, NH)), ("ssd_d", (DI,)), ("ssd_norm_w", (DI,)), ("lru_conv_b", (LW,)),
         ("ln1_g", (D,)), ("ln1_b", (D,)),
         ("b_mlp1", (MLP,)), ("b_mlp2", (D,)), ("ln2_g", (D,)), ("ln2_b", (D,)))

_WEIGHT_NAMES = ('c_ctx', 'w_mod', 'b_mod', 'w_in', 'b_gate', 'ssd_conv_w', 'ssd_conv_b', 'ssd_dt_bias', 'ssd_a_log',
                 'ssd_d', 'ssd_norm_w', 'lru_conv_w', 'lru_conv_b', 'lru_wa', 'lru_ba', 'lru_wi', 'lru_bi',
                 'lru_lambda', 'w_br_ssd', 'w_br_lru', 'w_out', 'ln1_g', 'ln1_b', 'w_mlp1', 'b_mlp1', 'w_mlp2',
                 'b_mlp2', 'ln2_g', 'ln2_b')
_ARG_NAMES = ('x', 'c', 'ctx') + _WEIGHT_NAMES + ('loss_target',) + tuple('m_' + n for n in _WEIGHT_NAMES) + tuple(
    'v_' + n for n in _WEIGHT_NAMES)


def _to_slots(full, axis):
    n = full.shape[axis] // NDEV
    if axis == 0:
        return full.reshape(NDEV, n, full.shape[1])
    return full.reshape(full.shape[0], NDEV, n).transpose(1, 0, 2)


def _from_slots(slots, axis):
    if axis == 0:
        return slots.reshape(NDEV * slots.shape[1], slots.shape[2])
    return slots.transpose(1, 0, 2).reshape(slots.shape[1], NDEV * slots.shape[2])


def _pack_rows(arrs, width=1024, mult=8):
    flat = jnp.concatenate([a.reshape(-1) for a in arrs])
    n = flat.shape[0]
    per = width * mult
    tot = -(-n // per) * per
    return jnp.pad(flat, (0, tot - n)).reshape(tot // width, width)


def _unpack_rows(packed, shapes, lead=()):
    nl = len(lead)
    flat = packed.reshape(tuple(lead) + (-1,))
    out, off = [], 0
    for s in shapes:
        n = math.prod(s)
        out.append(flat[..., off:off + n].reshape(tuple(lead) + tuple(s)))
        off += n
    return out


def kernel(x, c, ctx, c_ctx, w_mod, b_mod, w_in, b_gate, ssd_conv_w, ssd_conv_b, ssd_dt_bias, ssd_a_log, ssd_d, ssd_norm_w, lru_conv_w, lru_conv_b, lru_wa, lru_ba, lru_wi, lru_bi, lru_lambda, w_br_ssd, w_br_lru, w_out, ln1_g, ln1_b, w_mlp1, b_mlp1, w_mlp2, b_mlp2, ln2_g, ln2_b, loss_target, m_c_ctx, m_w_mod, m_b_mod, m_w_in, m_b_gate, m_ssd_conv_w, m_ssd_conv_b, m_ssd_dt_bias, m_ssd_a_log, m_ssd_d, m_ssd_norm_w, m_lru_conv_w, m_lru_conv_b, m_lru_wa, m_lru_ba, m_lru_wi, m_lru_bi, m_lru_lambda, m_w_br_ssd, m_w_br_lru, m_w_out, m_ln1_g, m_ln1_b, m_w_mlp1, m_b_mlp1, m_w_mlp2, m_b_mlp2, m_ln2_g, m_ln2_b, v_c_ctx, v_w_mod, v_b_mod, v_w_in, v_b_gate, v_ssd_conv_w, v_ssd_conv_b, v_ssd_dt_bias, v_ssd_a_log, v_ssd_d, v_ssd_norm_w, v_lru_conv_w, v_lru_conv_b, v_lru_wa, v_lru_ba, v_lru_wi, v_lru_bi, v_lru_lambda, v_w_br_ssd, v_w_br_lru, v_w_out, v_ln1_g, v_ln1_b, v_w_mlp1, v_b_mlp1, v_w_mlp2, v_b_mlp2, v_ln2_g, v_ln2_b):
    A = dict(zip(_ARG_NAMES, (x, c, ctx, c_ctx, w_mod, b_mod, w_in, b_gate, ssd_conv_w, ssd_conv_b, ssd_dt_bias, ssd_a_log, ssd_d, ssd_norm_w, lru_conv_w, lru_conv_b, lru_wa, lru_ba, lru_wi, lru_bi, lru_lambda, w_br_ssd, w_br_lru, w_out, ln1_g, ln1_b, w_mlp1, b_mlp1, w_mlp2, b_mlp2, ln2_g, ln2_b, loss_target, m_c_ctx, m_w_mod, m_b_mod, m_w_in, m_b_gate, m_ssd_conv_w, m_ssd_conv_b, m_ssd_dt_bias, m_ssd_a_log, m_ssd_d, m_ssd_norm_w, m_lru_conv_w, m_lru_conv_b, m_lru_wa, m_lru_ba, m_lru_wi, m_lru_bi, m_lru_lambda, m_w_br_ssd, m_w_br_lru, m_w_out, m_ln1_g, m_ln1_b, m_w_mlp1, m_b_mlp1, m_w_mlp2, m_b_mlp2, m_ln2_g, m_ln2_b, v_c_ctx, v_w_mod, v_b_mod, v_w_in, v_b_gate, v_ssd_conv_w, v_ssd_conv_b, v_ssd_dt_bias, v_ssd_a_log, v_ssd_d, v_ssd_norm_w, v_lru_conv_w, v_lru_conv_b, v_lru_wa, v_lru_ba, v_lru_wi, v_lru_bi, v_lru_lambda, v_w_br_ssd, v_w_br_lru, v_w_out, v_ln1_g, v_ln1_b, v_w_mlp1, v_b_mlp1, v_w_mlp2, v_b_mlp2, v_ln2_g, v_ln2_b)))
    Bn, T, _ = x.shape
    Tc = ctx.shape[1]
    cfg = _Cfg(Bn, T, Tc)
    me = _me()
    L = {n: (A[n] if n == "c_ctx" else A[n][0]) for n in _WEIGHT_NAMES}
    nmod = L["w_mod"].shape[1]

    c_all = _exchange(c, "ag_c", True)
    c_rows = jnp.concatenate([c_all.reshape(NDEV * Bn, D), jnp.broadcast_to(c_ctx[None, :], (8, D))], axis=0)
    b_shard = lax.dynamic_slice(L["b_mod"], (me * nmod,), (nmod,)).reshape(1, nmod)
    m_part = _mod_fwd(c_rows, L["w_mod"], b_shard)
    m_all = _exchange(m_part, "ag_mod", True)
    m_full = m_all.transpose(1, 0, 2).reshape(NDEV * Bn + 8, NMOD * D)
    m_mine = lax.dynamic_slice(m_full, (me * Bn, 0), (Bn, NMOD * D))
    mc = m_full[NDEV * Bn, :2 * D]

    rest_payload = jnp.concatenate([L[n].astype(BF16).reshape(-1, 1024) for n, _, _ in _BIG[1:]], axis=0)
    small_shapes = [(s[0], s[1] // NDEV) for _, s in _SMALL_SH]
    small_all = _exchange(_pack_rows([L[n] for n, _ in _SMALL_SH], width=512), "ag_w_small", True)
    W = {}
    for (n, shp), piece in zip(_SMALL_SH, _unpack_rows(small_all, small_shapes, lead=(NDEV,))):
        W[n] = piece.transpose(1, 0, 2).reshape(shp)
    for n in ("ssd_conv_b", "lru_conv_b", "ssd_norm_w", "b_gate", "ln1_g", "ln1_b", "b_mlp1", "b_mlp2", "ln2_g", "ln2_b"):
        W[n] = L[n].reshape(1, -1)
    for n in ("ssd_dt_bias", "ssd_a_log", "ssd_d", "lru_wa", "lru_wi"):
        W[n] = L[n]

    loss_part, grad_x, g, dm, dmc, xres = _local_step(cfg, x, ctx, loss_target, m_mine, mc, W,
                                                      L["w_in"].astype(BF16), rest_payload)
    loss = lax.psum(loss_part, ("x", "y", "c"))

    dmc_pad = jnp.pad(dmc, ((0, 4 - Bn), (0, (NMOD - 2) * D)))
    dm_payload = jnp.concatenate([jnp.pad(dm, ((0, 4 - Bn), (0, 0))), dmc_pad], axis=0)
    upd_w_in, dm_all = _sum_adamw(xres["rs_w_in"], L["w_in"], A["m_w_in"][0], A["v_w_in"][0], "sum_adamw_w_in",
                                  xch=(dm_payload, True))
    dm_all = dm_all.reshape(NDEV * 8, NMOD * D)
    c_rows_b = jnp.concatenate([jnp.pad(c_all, ((0, 0), (0, 4 - Bn), (0, 0))),
                                jnp.broadcast_to(c_ctx[None, None, :], (NDEV, 4, D))], axis=1).reshape(NDEV * 8, D)
    dm_shard = lax.dynamic_slice(dm_all, (0, me * nmod), (NDEV * 8, nmod))
    g_w_mod, g_b_mod, cc_part = _mod_bwd(c_rows_b, dm_all, dm_shard, L["w_mod"])
    g["c_ctx"] = cc_part[0]

    g["ssd_d"] = g.pop("ssd_d_cols")
    small_names = [n for n, _ in _REPL] + [n for n, _ in _SMALL_SH]
    small_full_shapes = [s for _, s in _REPL] + [s for _, s in _SMALL_SH]
    red_b, sm_all = _sum_slots(xres["rs_rest"], "sum_w_rest", xch=(_pack_rows([g[n] for n in small_names]), True))
    sm_sum = _sum_slots(sm_all, "sum_g_small")
    gs = dict(zip(small_names, _unpack_rows(sm_sum, small_full_shapes)))
    gcc, gdd = _small_finish(gs["c_ctx"].reshape(1, D), c_ctx.reshape(1, D), gs["ssd_d"].reshape(NH, HD))
    gs["c_ctx"] = gcc.reshape(D)
    gs["ssd_d"] = gdd.reshape(NH)
    for n, shp in _SMALL_SH:
        ns = shp[1] // NDEV
        gs[n] = lax.dynamic_slice(gs[n], (0, me * ns), (shp[0], ns))
    gs["b_mod"] = g_b_mod.reshape(NMOD * D)
    lru_sum = _sum_slots(xres["ag_lru"], "sum_g_lru").reshape(2, 2, LB, LBW, LBW)
    gs["lru_wa"], gs["lru_wi"] = lru_sum[0], lru_sum[1]

    gb = {}
    off = 0
    for n, shp, axis in _BIG[1:]:
        shard_shape = (shp[0] // NDEV, shp[1]) if axis == 0 else (shp[0], shp[1] // NDEV)
        r = math.prod(shard_shape) // 1024
        gb[n] = red_b[off:off + r].reshape(shard_shape)
        off += r
    gb["w_mod"] = g_w_mod

    grads, deltas, new_m, new_v = {}, {}, {}, {}
    big_names = ["w_mod"] + [n for n, _, _ in _BIG]
    grads["w_in"], deltas["w_in"], new_m["w_in"], new_v["w_in"] = upd_w_in
    for n in big_names:
        if n == "w_in":
            continue
        d_, nm_, nv_ = _adamw(L[n], gb[n], A["m_" + n][0], A["v_" + n][0], "adamw_" + n)
        grads[n], deltas[n], new_m[n], new_v[n] = gb[n], d_, nm_, nv_
    sm_names = [n for n in _WEIGHT_NAMES if n not in big_names]

    def two_d(a):
        return a.reshape(1, -1) if a.ndim == 1 else a
    loc = lambda pre: [two_d(A[pre + n] if n == "c_ctx" else A[pre + n][0]) for n in sm_names]
    gsm = [two_d(gs[n].reshape(L[n].shape)) for n in sm_names]
    ds_, nms_, nvs_ = _adamw_many(loc(""), gsm, loc("m_"), loc("v_"))
    for n, gv, dv, mv, vv in zip(sm_names, gsm, ds_, nms_, nvs_):
        shp = L[n].shape
        grads[n], deltas[n], new_m[n], new_v[n] = gv.reshape(shp), dv.reshape(shp), mv.reshape(shp), vv.reshape(shp)

    def out(dct):
        return [dct[n] if n == "c_ctx" else dct[n][None] for n in _WEIGHT_NAMES]
    return (loss, grad_x, *out(grads), *out(deltas), *out(new_m), *out(new_v))
```

```python
import functools
import math

import jax
import jax.numpy as jnp
from jax import lax
from jax.experimental import pallas as pl
from jax.experimental.pallas import tpu as pltpu

F32 = jnp.float32
BF16 = jnp.bfloat16

D = 1024
GRID_W = 64
DI = 2048
NH = 32
HD = 64
NG = 8
HPG = 4
NS = 128
CH = 128
LW = 1024
LB = 8
LBW = 128
LRU_C = 8.0
MLP = 4096
NMOD = 6
ALPHA = 2.0 ** 0.25
LN_EPS = 1e-6
RMS_EPS = 1e-5
PM = 10240
DTW = 128
CONVW = 5120
NDEV = 8

ADAM_LR = 0.001
ADAM_B1 = 0.9
ADAM_B2 = 0.999
ADAM_EPS = 1e-08
ADAM_WD = 0.01
ADAM_STEP = 10

VMEM_LIMIT_BYTES = 56 * 1024 * 1024


def _cp(n_axes):
    return pltpu.CompilerParams(dimension_semantics=("arbitrary",) * n_axes,
                                vmem_limit_bytes=VMEM_LIMIT_BYTES)


def _sigmoid(x):
    return 0.5 * jnp.tanh(0.5 * x) + 0.5


def _silu(x):
    return x * _sigmoid(x)


def _silu_grad(x):
    s = _sigmoid(x)
    return s * (1.0 + x * (1.0 - s))


def _log1p_pos(e):
    return jnp.where(e < 1e-2, e * (1.0 - e * (0.5 - e * (1.0 / 3.0))), jnp.log(1.0 + e))


def _softplus(x):
    return jnp.maximum(x, 0.0) + _log1p_pos(jnp.exp(-jnp.abs(x)))


_GELU_K = math.sqrt(2.0 / math.pi)


def _gelu(x):
    t = jnp.tanh(_GELU_K * (x + 0.044715 * x * x * x))
    return 0.5 * x * (1.0 + t)


def _gelu_and_grad(x):
    x2 = x * x
    t = jnp.tanh(_GELU_K * x * (1.0 + 0.044715 * x2))
    dt = (1.0 - t * t) * _GELU_K * (1.0 + 3.0 * 0.044715 * x2)
    h = 0.5 * (1.0 + t)
    return x * h, h + 0.5 * x * dt


def _ln(x):
    mu = jnp.mean(x, axis=-1, keepdims=True)
    xc = x - mu
    var = jnp.mean(xc * xc, axis=-1, keepdims=True)
    rs = lax.rsqrt(var + LN_EPS)
    return xc * rs, rs


def _ln_bwd(dy, xhat, rs):
    m1 = jnp.mean(dy, axis=-1, keepdims=True)
    m2 = jnp.mean(dy * xhat, axis=-1, keepdims=True)
    return rs * (dy - m1 - xhat * m2)


def _dot(a, b):
    return lax.dot_general(a, b, (((1,), (0,)), ((), ())), preferred_element_type=F32)


def _dot_nt(a, b):
    return lax.dot_general(a, b, (((1,), (1,)), ((), ())), preferred_element_type=F32)


def _dot_tn(a, b):
    return lax.dot_general(a, b, (((0,), (0,)), ((), ())), preferred_element_type=F32)


def _split3(a):
    a0 = a.astype(BF16)
    r = a - a0.astype(F32)
    a1 = r.astype(BF16)
    a2 = (r - a1.astype(F32)).astype(BF16)
    return a0, a1, a2


def _dot_exact_l(m_bf, a):
    a0, a1, a2 = _split3(a)
    return _dot(m_bf, a0) + _dot(m_bf, a1) + _dot(m_bf, a2)


def _dot_hilo_r(a, m_bf):
    a0 = a.astype(BF16)
    a1 = (a - a0.astype(F32)).astype(BF16)
    return _dot(a0, m_bf) + _dot(a1, m_bf)


def _tri(n, upper):
    ii = lax.broadcasted_iota(jnp.int32, (n, n), 0)
    kk = lax.broadcasted_iota(jnp.int32, (n, n), 1)
    m = (kk >= ii) if upper else (kk <= ii)
    return jnp.where(m, 1.0, 0.0).astype(BF16)


def _fit(n, t):
    t = min(t, n)
    while n % t:
        t //= 2
    return t


def _mm(a, b, mode, name, out_dtype=F32, tm=512, tn=512, tk=512, xch=None):
    if mode == "nn":
        M, K = a.shape
        N = b.shape[1]
    elif mode == "nt":
        M, K = a.shape
        N = b.shape[0]
    else:
        K, M = a.shape
        N = b.shape[1]
    tm, tn, tk = _fit(M, tm), _fit(N, tn), _fit(K, tk)
    assert M % tm == 0 and N % tn == 0 and K % tk == 0, (name, M, N, K, tm, tn, tk)
    nk = K // tk
    if mode == "tn":
        a_spec = pl.BlockSpec((tk, tm), lambda i, j, k: (k, i))
    else:
        a_spec = pl.BlockSpec((tm, tk), lambda i, j, k: (i, k))
    if mode == "nt":
        b_spec = pl.BlockSpec((tn, tk), lambda i, j, k: (j, k))
    else:
        b_spec = pl.BlockSpec((tk, tn), lambda i, j, k: (k, j))
    dn = {"nn": (((1,), (0,)), ((), ())), "nt": (((1,), (1,)), ((), ())), "tn": (((0,), (0,)), ((), ()))}[mode]

    def body(a_ref, b_ref, o_ref, acc_ref):
        k = pl.program_id(2)

        @pl.when(k == 0)
        def _():
            acc_ref[...] = jnp.zeros_like(acc_ref)

        acc_ref[...] += lax.dot_general(a_ref[...].astype(BF16), b_ref[...].astype(BF16), dn,
                                        preferred_element_type=F32)

        @pl.when(k == nk - 1)
        def _():
            o_ref[...] = acc_ref[...].astype(o_ref.dtype)

    def body_one_step(a_ref, b_ref, o_ref):
        o_ref[...] = lax.dot_general(a_ref[...].astype(BF16), b_ref[...].astype(BF16), dn,
                                     preferred_element_type=F32).astype(o_ref.dtype)

    res = _hosted_call(
        body if nk > 1 else body_one_step, xch, name=name, grid=(M // tm, N // tn, nk),
        in_specs=[a_spec, b_spec],
        out_specs=[pl.BlockSpec((tm, tn), lambda i, j, k: (i, j))],
        out_shape=[jax.ShapeDtypeStruct((M, N), out_dtype)],
        scratch_shapes=[pltpu.VMEM((tm, tn), F32)] if nk > 1 else [],
        compiler_params=_cp(3), args=(a, b))
    if xch is None:
        return res[0]
    return res[0][0], res[1]


def _mm_mlp1(h2, w1, b1, tm=1024, tn=1024):
    M, K = h2.shape
    N = w1.shape[1]
    tm, tn = _fit(M, tm), _fit(N, tn)

    def body(a_ref, b_ref, bias_ref, a1_ref, act_ref):
        v = _dot(a_ref[...], b_ref[...]) + bias_ref[...]
        a1_ref[...] = v
        r = jnp.maximum(v, 0.0)
        act_ref[...] = (r * r).astype(BF16)

    out = pl.BlockSpec((tm, tn), lambda i, j: (i, j))
    return pl.pallas_call(
        body, name="mm_mlp1", grid=(M // tm, N // tn),
        in_specs=[pl.BlockSpec((tm, K), lambda i, j: (i, 0)), pl.BlockSpec((K, tn), lambda i, j: (0, j)),
                  pl.BlockSpec((1, tn), lambda i, j: (0, j))],
        out_specs=[out, out],
        out_shape=[jax.ShapeDtypeStruct((M, N), F32), jax.ShapeDtypeStruct((M, N), BF16)],
        compiler_params=_cp(2),
    )(h2, w1, b1)


def _mm_dact(dmlp, w2, a1, tm=1024, tn=1024):
    M, K = dmlp.shape
    N = w2.shape[0]
    tm, tn = _fit(M, tm), _fit(N, tn)

    def body(d_ref, w_ref, a1_ref, o_ref, acc_ref):
        i = pl.program_id(1)

        @pl.when(i == 0)
        def _():
            acc_ref[...] = jnp.zeros_like(acc_ref)

        da = _dot_nt(d_ref[...], w_ref[...]) * (2.0 * jnp.maximum(a1_ref[...], 0.0))
        o_ref[...] = da.astype(BF16)
        acc_ref[0:1, :] += jnp.sum(da, axis=0, keepdims=True)

    blk = pl.BlockSpec((tm, tn), lambda j, i: (i, j))
    return pl.pallas_call(
        body, name="mm_dact", grid=(N // tn, M // tm),
        in_specs=[pl.BlockSpec((tm, K), lambda j, i: (i, 0)), pl.BlockSpec((tn, K), lambda j, i: (j, 0)), blk],
        out_specs=[blk, pl.BlockSpec((8, tn), lambda j, i: (0, j))],
        out_shape=[jax.ShapeDtypeStruct((M, N), BF16), jax.ShapeDtypeStruct((8, N), F32)],
        compiler_params=_cp(2),
    )(dmlp, w2, a1)


class _Cfg:
    def __init__(self, Bn, T, Tc):
        assert T % Tc == 0 and Tc % CH == 0 and Tc % GRID_W == 0
        self.Bn, self.T, self.Tc = Bn, T, Tc
        self.TT = T + Tc
        self.TB = Tc
        self.nbt = self.TT // self.TB
        self.nbl = T // self.TB
        self.NT = Bn * self.TT
        self.N = Bn * T
        self.nct = Tc // CH
        self.nlt = T // CH
        self.nch = self.nct + self.nlt


def _ln_mod_fwd(cfg, ctx2, x2, shift_tab, scale_tab, xch):
    TB, nbt, nbl = cfg.TB, cfg.nbt, cfg.nbl

    def body(c_ref, x_ref, sh_ref, sc_ref, o_ref):
        j = pl.program_id(1)
        xhat, _ = _ln(jnp.where(j == 0, c_ref[...], x_ref[...]))
        o_ref[...] = (xhat * (1.0 + sc_ref[...]) + sh_ref[...]).astype(BF16)

    tab = pl.BlockSpec((None, 1, D), lambda b, j: (2 * b + jnp.minimum(j, 1), 0, 0))
    outs, xres = _hosted_call(
        body, xch, name="ln_mod_fwd", grid=(cfg.Bn, nbt),
        in_specs=[pl.BlockSpec((TB, D), lambda b, j: (b, 0)),
                  pl.BlockSpec((TB, D), lambda b, j: (b * nbl + jnp.maximum(j - 1, 0), 0)), tab, tab],
        out_specs=[pl.BlockSpec((TB, D), lambda b, j: (b * nbt + j, 0))],
        out_shape=[jax.ShapeDtypeStruct((cfg.NT, D), BF16)],
        scratch_shapes=[], compiler_params=_cp(2), args=(ctx2, x2, shift_tab, scale_tab))
    return outs[0], xres


GP = 8
NGB = NG // GP
HPB = GP * HPG


def _heads_to_front(x, d, gb, inverse=False):
    off = d * NH + gb * HPB
    return pltpu.roll(x, off if inverse else (DTW - off) % DTW, 1)


def _chunks_per_step(cfg):
    n = cfg.NT // CH
    return max(c for c in (4, 3, 2, 1) if n % c == 0)


def _dt_fwd(cfg, h, w_dt, dt_bias, a_log, xch):
    cps = _chunks_per_step(cfg)

    def body(h_ref, wd_ref, bias_ref, alog_ref, raw_ref, dt_ref, dtg_ref, cumg_ref, cumT_ref):
        a = -jnp.exp(alog_ref[...])
        col = lax.broadcasted_iota(jnp.int32, (CH, DTW), 1)
        for c in range(cps):
            rows = slice(c * CH, (c + 1) * CH)
            raw = _dot(h_ref[rows, :], wd_ref[...])
            raw_ref[rows, :] = raw
            dt = _softplus(raw + bias_ref[...])
            dta = dt * a
            cf = _dot_exact_l(_tri(CH, False), dta)
            cr = _dot_exact_l(_tri(CH, True), dta)
            cum = jnp.where(col < NH, cf, cr)
            dt_ref[rows, :] = dt
            for d in range(2):
                for gb in range(NGB):
                    dtg_ref[d, gb, rows, :] = _heads_to_front(dt, d, gb)
                    cg = _heads_to_front(cum, d, gb)
                    cumg_ref[d, gb, rows, :] = cg
                    cumT_ref[d, gb, c] = cg.T

    blk = pl.BlockSpec((cps * CH, DTW), lambda i: (i, 0))
    row = pl.BlockSpec((1, DTW), lambda i: (0, 0))
    gblk = pl.BlockSpec((2, NGB, cps * CH, DTW), lambda i: (0, 0, i, 0))
    return _hosted_call(
        body, xch, name="dt_fwd", grid=(cfg.NT // (cps * CH),),
        in_specs=[pl.BlockSpec((cps * CH, D), lambda i: (i, 0)), pl.BlockSpec((D, DTW), lambda i: (0, 0)), row, row],
        out_specs=[blk, blk, gblk, gblk, pl.BlockSpec((2, NGB, cps, DTW, CH), lambda i: (0, 0, i, 0, 0))],
        out_shape=[jax.ShapeDtypeStruct((cfg.NT, DTW), F32),
                   jax.ShapeDtypeStruct((cfg.NT, DTW), F32),
                   jax.ShapeDtypeStruct((2, NGB, cfg.NT, DTW), F32),
                   jax.ShapeDtypeStruct((2, NGB, cfg.NT, DTW), F32),
                   jax.ShapeDtypeStruct((2, NGB, cfg.NT // CH, DTW, CH), F32)],
        scratch_shapes=[], compiler_params=_cp(1), args=(h, w_dt, dt_bias, a_log))


def _dt_bwd(cfg, dAs, dxxs, dt_raw, dt, dt_bias, a_log, h):
    def body(dAf_ref, dAr_ref, dxf_ref, dxr_ref, raw_ref, dt_ref, bias_ref, alog_ref, h_ref, o_ref, acc_ref, dw_ref):
        i = pl.program_id(0)

        @pl.when(i == 0)
        def _():
            acc_ref[...] = jnp.zeros_like(acc_ref)
            dw_ref[...] = jnp.zeros_like(dw_ref)

        a = -jnp.exp(alog_ref[...])
        col = lax.broadcasted_iota(jnp.int32, (CH, DTW), 1)
        for c in range(cps):
            rows = slice(c * CH, (c + 1) * CH)
            dA_v = jnp.zeros((CH, DTW), F32)
            dxx_v = jnp.zeros((CH, DTW), F32)
            for d, (ra, rx) in enumerate(((dAf_ref, dxf_ref), (dAr_ref, dxr_ref))):
                for gb in range(NGB):
                    dA_v = dA_v + _heads_to_front(ra[gb, rows, :], d, gb, inverse=True)
                    dxx_v = dxx_v + _heads_to_front(rx[gb, rows, :], d, gb, inverse=True)
            ddta = jnp.where(col < NH, _dot_exact_l(_tri(CH, True), dA_v), _dot_exact_l(_tri(CH, False), dA_v))
            dtv = dt_ref[rows, :]
            ddt = ddta * a + dxx_v
            draw = ddt * _sigmoid(raw_ref[rows, :] + bias_ref[...])
            draw = jnp.where(col < 2 * NH, draw, 0.0)
            o_ref[rows, :] = draw
            dw_ref[...] += _dot_tn(h_ref[rows, :], draw.astype(BF16))
            da = jnp.sum(ddta * dtv, axis=0, keepdims=True) * a
            da = jnp.where(col[:1] < 2 * NH, da, 0.0)
            acc_ref[0:1, :] += da
            acc_ref[1:2, :] += jnp.sum(draw, axis=0, keepdims=True)

    cps = _chunks_per_step(cfg)
    blk = pl.BlockSpec((cps * CH, DTW), lambda i: (i, 0))
    row = pl.BlockSpec((1, DTW), lambda i: (0, 0))
    gblk = pl.BlockSpec((NGB, cps * CH, DTW), lambda i: (0, i, 0))
    return pl.pallas_call(
        body, name="dt_bwd", grid=(cfg.NT // (cps * CH),),
        in_specs=[gblk, gblk, gblk, gblk, blk, blk, row, row, pl.BlockSpec((cps * CH, D), lambda i: (i, 0))],
        out_specs=[blk, pl.BlockSpec((8, DTW), lambda i: (0, 0)), pl.BlockSpec((D, DTW), lambda i: (0, 0))],
        out_shape=[jax.ShapeDtypeStruct((cfg.NT, DTW), F32), jax.ShapeDtypeStruct((8, DTW), F32),
                   jax.ShapeDtypeStruct((D, DTW), F32)],
        compiler_params=_cp(1),
    )(dAs[0], dAs[1], dxxs[0], dxxs[1], dt_raw, dt, dt_bias, a_log, h)


_TAPS = (2, 1, 0, -1)


def _conv_fwd(cfg, proj, conv_w, conv_b):
    TB, nbt = cfg.TB, cfg.nbt
    CB = CONVW // 2
    SUB = 256
    n_act = DI + 2 * NG * NS

    def body(u_ref, w_ref, b_ref, o_ref, sg_ref):
        i = pl.program_id(0)
        j = pl.program_id(1)
        R = jnp.where(i % nbt == 0, cfg.Tc, GRID_W)
        t = lax.broadcasted_iota(jnp.int32, (TB, SUB), 0)
        pos = jnp.bitwise_and(t, R - 1)
        keep = {s: jnp.where(jnp.logical_and(pos - s >= 0, pos - s < R), 1.0, 0.0) for s in (2, 1, -1)}
        def sub_tile(q, act):
            sl = slice(q * SUB, (q + 1) * SUB)
            u = u_ref[:, sl]
            pre = b_ref[:, sl] + w_ref[2:3, sl] * u
            for k in (0, 1, 3):
                pre = pre + w_ref[k:k + 1, sl] * (pltpu.roll(u, _TAPS[k] % TB, 0) * keep[_TAPS[k]])
            if act:
                s = _sigmoid(pre)
                o_ref[:, sl] = pre * s
                sg_ref[:, sl] = s * (1.0 + pre * (1.0 - s))
            else:
                o_ref[:, sl] = pre

        for q in range(CB // SUB):
            if q * SUB >= n_act - CB:
                pl.when(j == 0)(functools.partial(sub_tile, q, True))
                pl.when(j == 1)(functools.partial(sub_tile, q, False))
            else:
                sub_tile(q, True)

    blk = pl.BlockSpec((TB, CB), lambda i, j: (i, j))
    return pl.pallas_call(
        body, name="conv_fwd", grid=(cfg.NT // TB, CONVW // CB),
        in_specs=[blk, pl.BlockSpec((4, CB), lambda i, j: (0, j)), pl.BlockSpec((1, CB), lambda i, j: (0, j))],
        out_specs=[blk, blk],
        out_shape=[jax.ShapeDtypeStruct((cfg.NT, CONVW), F32)] * 2,
        compiler_params=_cp(2),
    )(proj, conv_w, conv_b)


_ANY = pl.BlockSpec(memory_space=pl.ANY)


def _conv_bwd(cfg, name, dproj, proj, conv_w, sgrad, addends, col0, width, skip=None, xch=None):
    TB, nbt, nbl = cfg.TB, cfg.nbt, cfg.nbl
    CB = 1024
    SUB = 256
    c0 = col0 // CB
    addends = list(addends) + ([] if sgrad is None else [sgrad])
    n_add = len(addends)

    def body(*refs):
        u_ref, w_ref = refs[1:3]
        add_refs = refs[3:3 + n_add]
        rest = refs[3 + n_add:]
        if sgrad is not None:
            add_refs, sg_ref = add_refs[:-1], add_refs[-1]
        if skip is not None:
            dy_ref, dv_ref = rest[:2]
            rest = rest[2:]
        o_ref, acc_ref = rest
        i = pl.program_id(1)

        @pl.when(i == 0)
        def _():
            acc_ref[...] = jnp.zeros_like(acc_ref)

        isctx = (i % nbt) == 0
        R = jnp.where(isctx, cfg.Tc, GRID_W)
        t = lax.broadcasted_iota(jnp.int32, (TB, SUB), 0)
        pos = jnp.bitwise_and(t, R - 1)
        keep = {s: jnp.where(jnp.logical_and(pos - s >= 0, pos - s < R), 1.0, 0.0) for s in (2, 1, -1, -2)}

        def shifted(v, s):
            return v if s == 0 else pltpu.roll(v, s % TB, 0) * keep[s]

        for q in range(CB // SUB):
            sl = slice(q * SUB, (q + 1) * SUB)
            u = u_ref[:, sl]
            us = [shifted(u, _TAPS[k]) for k in range(4)]
            g = add_refs[0][:, sl]
            for r in add_refs[1:]:
                g = g + r[:, sl]
            if skip is not None:
                g = g + jnp.where(isctx, 0.0, dv_ref[:, sl] * dy_ref[:, sl])
            if sgrad is not None:
                g = g * sg_ref[:, sl]
            dp = jnp.zeros_like(g)
            for k in range(4):
                acc_ref[k:k + 1, sl] += jnp.sum(g * us[k], axis=0, keepdims=True)
                dp = dp + w_ref[k:k + 1, sl] * shifted(g, -_TAPS[k])
            acc_ref[4:5, sl] += jnp.sum(g, axis=0, keepdims=True)
            o_ref[:, sl] = dp.astype(BF16)

    blk = pl.BlockSpec((TB, CB), lambda j, i: (i, j))
    wide = pl.BlockSpec((TB, CB), lambda j, i: (i, c0 + j))
    in_specs = [_ANY, wide, pl.BlockSpec((4, CB), lambda j, i: (0, c0 + j))]
    in_specs += [blk] * (n_add if sgrad is None else n_add - 1) + ([] if sgrad is None else [wide])
    args = [dproj, proj, conv_w] + addends
    if skip is not None:
        def lat(j, i):
            b = i // nbt
            return (b * nbl + jnp.maximum(i % nbt - 1, 0), j)
        in_specs += [pl.BlockSpec((TB, CB), lat), pl.BlockSpec((1, CB), lambda j, i: (0, j))]
        args += list(skip)
    return _hosted_call(
        body, xch, name=name, grid=(width // CB, cfg.NT // TB),
        in_specs=in_specs,
        out_specs=[pl.BlockSpec((TB, CB), lambda j, i: (i, c0 + j)), pl.BlockSpec((8, CB), lambda j, i: (0, j))],
        out_shape=[jax.ShapeDtypeStruct((cfg.NT, PM), BF16), jax.ShapeDtypeStruct((8, width), F32)],
        scratch_shapes=[], compiler_params=_cp(2), args=args, aliases={0: 0})


def _chunk_of_step(cfg, rev):
    nct, nlt = cfg.nct, cfg.nlt
    if not rev:
        return lambda s: s
    return lambda s: jnp.where(s < nct, nct - 1 - s, 2 * nct + nlt - 1 - s)


def _expand4(v, band, base):
    out = v[:, base + 3:base + 4]
    for h in (2, 1, 0):
        out = jnp.where(band == h, v[:, base + h:base + h + 1], out)
    return out


def _ssd_step_tiles(dt_ref, cum_ref, cumT_ref, rev):
    cum_t = cum_ref[...]
    last = 0 if rev else CH - 1
    llast = cum_t[last:last + 1, :]
    return (dt_ref[...], cum_t, cumT_ref[...], llast, jnp.exp(llast), last)


def _ssd_common(gi, x_ref, b_ref, c_ref, tiles, rev, intra=True):
    dt_t, cum_t, cumT_t, llast, elast, last = tiles
    base = gi * HPG
    xh = x_ref[:, gi * HPG * HD:(gi + 1) * HPG * HD]
    Bm = b_ref[:, gi * NS:(gi + 1) * NS].astype(BF16)
    band = lax.broadcasted_iota(jnp.int32, (CH, HPG * HD), 1) // HD
    cbs = [jnp.broadcast_to(cum_t[:, base + h:base + h + 1], (CH, CH)) for h in range(HPG)]
    Cm = G = decs = None
    if intra:
        Cm = c_ref[:, gi * NS:(gi + 1) * NS].astype(BF16)
        G = _dot_nt(Cm, Bm)
        ii = lax.broadcasted_iota(jnp.int32, (CH, CH), 0)
        jj = lax.broadcasted_iota(jnp.int32, (CH, CH), 1)
        mask = (jj >= ii) if rev else (jj <= ii)
        decs = [jnp.exp(jnp.where(mask, cbs[h] - cumT_t[base + h:base + h + 1, :], -1e30)) for h in range(HPG)]
    cum_exp = jnp.concatenate([cbs[3], cbs[3]], axis=1)
    ll_exp = llast[:, base + 3:base + 4]
    for h in (2, 1, 0):
        cum_exp = jnp.where(band == h, jnp.concatenate([cbs[h], cbs[h]], axis=1), cum_exp)
        ll_exp = jnp.where(band[:1] == h, llast[:, base + h:base + h + 1], ll_exp)
    ecum = jnp.exp(cum_exp) if intra else None
    e_exp = jnp.exp(ll_exp - cum_exp)
    dt_exp = _expand4(dt_t, band, base)
    X = xh * dt_exp
    rb = lax.broadcasted_iota(jnp.int32, (HPG * HD, NS), 0) // HD
    dec_rows = elast[:, base + 3:base + 4]
    for h in (2, 1, 0):
        dec_rows = jnp.where(rb == h, elast[:, base + h:base + h + 1], dec_rows)
    return xh, Bm, Cm, band, e_exp, ecum, dt_exp, X, G, decs, elast, dec_rows, last


def _ssd_specs(cfg, rev):
    nch = cfg.nch
    cmap = _chunk_of_step(cfg, rev)
    d = 1 if rev else 0

    def make(stepmap):
        def row(b, g, sp):
            return b * nch + cmap(stepmap(sp))
        bo, co = DI // (GP * NS), (DI + NG * NS) // (GP * NS)
        return [
            pl.BlockSpec((CH, GP * HPG * HD), lambda b, g, sp: (row(b, g, sp), g)),
            pl.BlockSpec((CH, GP * NS), lambda b, g, sp: (row(b, g, sp), bo + g)),
            pl.BlockSpec((CH, GP * NS), lambda b, g, sp: (row(b, g, sp), co + g)),
            pl.BlockSpec((None, None, CH, DTW), lambda b, g, sp: (d, g, row(b, g, sp), 0)),
            pl.BlockSpec((None, None, CH, DTW), lambda b, g, sp: (d, g, row(b, g, sp), 0)),
            pl.BlockSpec((None, None, None, DTW, CH), lambda b, g, sp: (d, g, row(b, g, sp), 0, 0)),
        ], row
    return make


def _ssd_fwd(cfg, act, dtg, cumg, cumTg, rev, y_other=None, dvec=None):
    nch = cfg.nch
    in_specs, row = _ssd_specs(cfg, rev)(lambda sp: sp)
    total = y_other is not None

    def body(*refs):
        x_ref, b_ref, c_ref, dt_ref, cum_ref, cumT_ref = refs[:6]
        if total:
            yo_ref, dv_ref = refs[6:8]
        y_ref, hs_ref, h_scr = refs[-3:]
        s = pl.program_id(2)

        @pl.when(s == 0)
        def _():
            h_scr[...] = jnp.zeros_like(h_scr)

        def step(intra):
            tiles = _ssd_step_tiles(dt_ref, cum_ref, cumT_ref, rev)
            for gi in range(GP):
                xh, Bm, Cm, band, e_exp, ecum, dt_exp, X, G, decs, elast, dec_rows, last = _ssd_common(
                    gi, x_ref, b_ref, c_ref, tiles, rev, intra)
                H = h_scr[gi]
                if intra:
                    Mcat = jnp.concatenate([(G * decs[h]).astype(BF16) for h in range(HPG)], axis=1)
                    Xbd = jnp.concatenate([jnp.where(band == h, X, 0.0).astype(BF16) for h in range(HPG)], axis=0)
                    xsl = slice(gi * HPG * HD, (gi + 1) * HPG * HD)
                    Y = ecum * _dot_nt(Cm, H.astype(BF16)) + _dot(Mcat, Xbd)
                    if total:
                        Y = Y + yo_ref[:, xsl] + dv_ref[:, xsl] * xh
                    y_ref[:, xsl] = Y
                hs_ref[gi] = H
                S = _dot_tn((e_exp * X).astype(BF16), Bm)
                h_scr[gi] = dec_rows * H + S

        isctx = cmap(s) < cfg.nct

        @pl.when(isctx)
        def _():
            step(False)

        @pl.when(jnp.logical_not(isctx))
        def _():
            step(True)

    cmap = _chunk_of_step(cfg, rev)
    yblk = pl.BlockSpec((CH, GP * HPG * HD), lambda b, g, s: (row(b, g, s), g))
    args = [act, act, act, dtg, cumg, cumTg]
    if total:
        in_specs = in_specs + [yblk, pl.BlockSpec((1, GP * HPG * HD), lambda b, g, s: (0, g))]
        args += [y_other, dvec]
    return pl.pallas_call(
        body, name="ssd_fwd_rev" if rev else "ssd_fwd", grid=(cfg.Bn, NG // GP, nch),
        in_specs=in_specs,
        out_specs=[yblk, pl.BlockSpec((None, GP, None, HPG * HD, NS), lambda b, g, s: (b, g, s, 0, 0))],
        out_shape=[jax.ShapeDtypeStruct((cfg.NT, DI), F32),
                   jax.ShapeDtypeStruct((cfg.Bn, NG, nch, HPG * HD, NS), F32)],
        scratch_shapes=[pltpu.VMEM((GP, HPG * HD, NS), F32)],
        compiler_params=_cp(3),
    )(*args)


def _ssd_bwd(cfg, act, dtg, cumg, cumTg, hs, dy, rev, xch=None):
    nch, nct, nlt = cfg.nch, cfg.nct, cfg.nlt
    cmap = _chunk_of_step(cfg, rev)
    in_specs, row = _ssd_specs(cfg, rev)(lambda sp: nch - 1 - sp)

    def lat_row(b, g, sp):
        c = cmap(nch - 1 - sp)
        return b * nlt + jnp.maximum(c - nct, 0)

    def body(x_ref, b_ref, c_ref, dt_ref, cum_ref, cumT_ref, dy_ref, hs_ref,
             dxh_ref, dB_ref, dC_ref, dA_ref, dxx_ref, dh_scr):
        sp = pl.program_id(2)

        @pl.when(sp == 0)
        def _():
            dh_scr[...] = jnp.zeros_like(dh_scr)

        def step(intra):
            tiles = _ssd_step_tiles(dt_ref, cum_ref, cumT_ref, rev)
            dA_t = jnp.zeros((CH, DTW), F32)
            dAT_t = jnp.zeros((DTW, CH), F32)
            dxx_t = jnp.zeros((CH, DTW), F32)
            for gi in range(GP):
                dA_g, dAT_g, dxx_g = group_bwd(gi, intra, tiles, x_ref, b_ref, c_ref, dy_ref, hs_ref,
                                               dxh_ref, dB_ref, dC_ref, dh_scr)
                dA_t = dA_t + dA_g
                dxx_t = dxx_t + dxx_g
                if intra:
                    dAT_t = dAT_t + dAT_g
            dA_ref[...] = dA_t - dAT_t.T if intra else dA_t
            dxx_ref[...] = dxx_t

        isctx = cmap(nch - 1 - sp) < nct

        @pl.when(isctx)
        def _():
            step(False)

        @pl.when(jnp.logical_not(isctx))
        def _():
            step(True)

    def group_bwd(gi, intra, tiles, x_ref, b_ref, c_ref, dy_ref, hs_ref, dxh_ref, dB_ref, dC_ref, dh_scr):
        xsl = slice(gi * HPG * HD, (gi + 1) * HPG * HD)
        nsl = slice(gi * NS, (gi + 1) * NS)
        base = gi * HPG
        xh, Bm, Cm, band, e_exp, ecum, dt_exp, X, G, decs, elast, dec_rows, last = _ssd_common(
            gi, x_ref, b_ref, c_ref, tiles, rev, intra)
        H = hs_ref[gi]
        dHn = dh_scr[gi]
        dHnb = dHn.astype(BF16)
        BdH = _dot_nt(Bm, dHnb)
        dX = e_exp * BdH
        eX = e_exp * X
        lanei = lax.broadcasted_iota(jnp.int32, (CH, DTW), 1)
        dA = jnp.zeros((CH, DTW), F32)
        dAT = None
        pb = lax.broadcasted_iota(jnp.int32, (HPG * HD, NS), 0) // HD
        pl_ = lax.broadcasted_iota(jnp.int32, (HPG * HD, NS), 1)
        E = jnp.where(pb + base == pl_, 1.0, 0.0).astype(BF16)
        if intra:
            dY = dy_ref[:, xsl]
            Hb = H.astype(BF16)
            dYs = ecum * dY
            dYsb = dYs.astype(BF16)
            Ys = ecum * _dot_nt(Cm, Hb)
            dG = jnp.zeros((CH, CH), F32)
            subi = lax.broadcasted_iota(jnp.int32, (DTW, CH), 0)
            dAT = jnp.zeros((DTW, CH), F32)
            Xbd = jnp.concatenate([jnp.where(band == h, X, 0.0).astype(BF16) for h in range(HPG)], axis=0)
            dYbd = jnp.concatenate([jnp.where(band == h, dY, 0.0).astype(BF16) for h in range(HPG)], axis=0)
            dMcat = _dot_nt(dY.astype(BF16), Xbd)
            Ms = []
            for h in range(HPG):
                M = G * decs[h]
                dM = dMcat[:, h * CH:(h + 1) * CH]
                W = dM * M
                dG = dG + dM * decs[h]
                Ms.append(M.astype(BF16))
                dA = dA + jnp.where(lanei == base + h, jnp.sum(W, axis=1, keepdims=True), 0.0)
                dAT = dAT + jnp.where(subi == base + h, jnp.sum(W, axis=0, keepdims=True), 0.0)
            dX = dX + _dot_tn(jnp.concatenate(Ms, axis=0), dYbd)
            dGb = dG.astype(BF16)
            dC_ref[:, nsl] = _dot(dGb, Bm) + _dot(dYsb, Hb)
            dB_ref[:, nsl] = _dot_tn(dGb, Cm) + _dot(eX.astype(BF16), dHnb)
            dh_scr[gi] = dec_rows * dHn + _dot_tn(dYsb, Cm)
            dA = dA + _dot_hilo_r(dY * Ys, E)
        else:
            dC_ref[:, nsl] = jnp.zeros((CH, NS), F32)
            dB_ref[:, nsl] = _dot(eX.astype(BF16), dHnb)
            dh_scr[gi] = dec_rows * dHn
        q = _dot_hilo_r(eX * BdH, E)
        r = jnp.sum(dHn * H, axis=1, keepdims=True)
        lane1 = lax.broadcasted_iota(jnp.int32, (1, DTW), 1)
        hdot = jnp.zeros((1, DTW), F32)
        for h in range(HPG):
            hv = jnp.sum(r[h * HD:(h + 1) * HD, :], axis=0, keepdims=True)
            hdot = hdot + jnp.where(lane1 == base + h, hv, 0.0)
        dllast = jnp.sum(q, axis=0, keepdims=True) + elast * hdot
        rowi = lax.broadcasted_iota(jnp.int32, (CH, DTW), 0)
        dxh_ref[:, xsl] = dX * dt_exp
        return dA - q + jnp.where(rowi == last, dllast, 0.0), dAT, _dot_hilo_r(dX * xh, E)

    small = pl.BlockSpec((None, CH, DTW), lambda b, g, sp: (g, row(b, g, sp), 0))
    return _hosted_call(
        body, xch, name="ssd_bwd_rev" if rev else "ssd_bwd", grid=(cfg.Bn, NG // GP, nch),
        in_specs=in_specs + [
            pl.BlockSpec((CH, GP * HPG * HD), lambda b, g, sp: (lat_row(b, g, sp), g)),
            pl.BlockSpec((None, GP, None, HPG * HD, NS), lambda b, g, sp: (b, g, nch - 1 - sp, 0, 0))],
        out_specs=[pl.BlockSpec((CH, GP * HPG * HD), lambda b, g, sp: (row(b, g, sp), g)),
                   pl.BlockSpec((CH, GP * NS), lambda b, g, sp: (row(b, g, sp), g)),
                   pl.BlockSpec((CH, GP * NS), lambda b, g, sp: (row(b, g, sp), g)),
                   small, small],
        out_shape=[jax.ShapeDtypeStruct((cfg.NT, DI), F32),
                   jax.ShapeDtypeStruct((cfg.NT, NG * NS), F32),
                   jax.ShapeDtypeStruct((cfg.NT, NG * NS), F32),
                   jax.ShapeDtypeStruct((NGB, cfg.NT, DTW), F32),
                   jax.ShapeDtypeStruct((NGB, cfg.NT, DTW), F32)],
        scratch_shapes=[pltpu.VMEM((GP, HPG * HD, NS), F32)],
        compiler_params=_cp(3), args=(act, act, act, dtg, cumg, cumTg, dy, hs))


def _shift_rows(v, s, fill, toward_later, rowi):
    n = v.shape[0]
    if toward_later:
        return jnp.where(rowi >= s, pltpu.roll(v, s, 0), fill)
    return jnp.where(rowi < n - s, pltpu.roll(v, n - s, 0), fill)


def _chunk_scan(a, b, carry, later):
    nt = a.shape[0] // 8
    rowi = lax.broadcasted_iota(jnp.int32, (8, a.shape[1]), 0)
    outs = [None] * nt
    for r in (range(nt) if later else range(nt - 1, -1, -1)):
        av = a[r * 8:(r + 1) * 8]
        bv = b[r * 8:(r + 1) * 8]
        for sh in (1, 2, 4):
            a_p = _shift_rows(av, sh, 1.0, later, rowi)
            b_p = _shift_rows(bv, sh, 0.0, later, rowi)
            bv = av * b_p + bv
            av = av * a_p
        h = bv + av * carry
        outs[r] = h
        carry = h[7:8] if later else h[0:1]
    return jnp.concatenate(outs, axis=0), carry


def _lru_gates(u, wa_ref, wi_ref, ba_ref, bi_ref, lam_ref):
    rs, is_ = [], []
    for k in range(LB):
        uk = u[:, k * LBW:(k + 1) * LBW].astype(BF16)
        rs.append(_dot(uk, wa_ref[k].astype(BF16)))
        is_.append(_dot(uk, wi_ref[k].astype(BF16)))
    r = 1.0 / (1.0 + jnp.exp(-(jnp.concatenate(rs, axis=1) + ba_ref[...])))
    ig = _sigmoid(jnp.concatenate(is_, axis=1) + bi_ref[...])
    sp = _softplus(-lam_ref[...])
    la = -LRU_C * r * sp
    a = jnp.exp(la)
    q = (1.0 + a * a) * jnp.tanh(-la)
    return r, ig, sp, la, a, jnp.sqrt(q), lax.rsqrt(q)


def _lru_w_specs(d):
    return [pl.BlockSpec((None, LB, LBW, LBW), lambda b, s: (d, 0, 0, 0)),
            pl.BlockSpec((None, LB, LBW, LBW), lambda b, s: (d, 0, 0, 0)),
            pl.BlockSpec((None, 1, LW), lambda b, s: (d, 0, 0)),
            pl.BlockSpec((None, 1, LW), lambda b, s: (d, 0, 0)),
            pl.BlockSpec((None, 1, LW), lambda b, s: (d, 0, 0))]


def _lru_block_of_step(cfg, rev):
    nbl = cfg.nbl
    if not rev:
        return lambda s: s
    return lambda s: jnp.where(s < 1, 0, 1 + nbl - s)


def _lru_fwd(cfg, act, wa, wi, ba, bi, lam, rev):
    nch, CH = cfg.nbt, cfg.TB
    cmap = _lru_block_of_step(cfg, rev)
    d = 1 if rev else 0
    ucol = (DI + 2 * NG * NS) // LW

    def body(u_ref, wa_ref, wi_ref, ba_ref, bi_ref, lam_ref, h_ref, c_scr):
        s = pl.program_id(1)

        @pl.when(s == 0)
        def _():
            c_scr[...] = jnp.zeros_like(c_scr)

        u = u_ref[...]
        r, ig, sp, la, a, g, _ = _lru_gates(u, wa_ref, wi_ref, ba_ref, bi_ref, lam_ref)
        h, carry = _chunk_scan(a, g * ig * u, c_scr[0:1, :], not rev)
        h_ref[...] = h
        c_scr[0:1, :] = carry

    return pl.pallas_call(
        body, name="lru_fwd_rev" if rev else "lru_fwd", grid=(cfg.Bn, nch),
        in_specs=[pl.BlockSpec((CH, LW), lambda b, s: (b * nch + cmap(s), ucol))] + _lru_w_specs(d),
        out_specs=pl.BlockSpec((CH, LW), lambda b, s: (b * nch + cmap(s), 0)),
        out_shape=jax.ShapeDtypeStruct((cfg.NT, LW), F32),
        scratch_shapes=[pltpu.VMEM((8, LW), F32)],
        compiler_params=_cp(2),
    )(act, wa, wi, ba, bi, lam)


def _lru_bwd(cfg, act, wa, wi, ba, bi, lam, hd, dyl, rev):
    nch, nct, nlt, CH = cfg.nbt, 1, cfg.nbl, cfg.TB
    cmap = _lru_block_of_step(cfg, rev)
    d = 1 if rev else 0
    ucol = (DI + 2 * NG * NS) // LW

    def srow(b, sp):
        return b * nch + cmap(nch - 1 - sp)

    def prev_rows(b, sp):
        s = nch - 1 - sp
        cp = cmap(jnp.maximum(s - 1, 0))
        base = (b * nch + cp) * (CH // 8)
        return base + (0 if rev else CH // 8 - 1)

    def lat_row(b, sp):
        c = cmap(nch - 1 - sp)
        return b * nlt + jnp.maximum(c - nct, 0)

    def body(u_ref, wa_ref, wi_ref, ba_ref, bi_ref, lam_ref, h_ref, hp_ref, dy_ref,
             du_ref, dwa_ref, dwi_ref, vec_ref, c_scr):
        b = pl.program_id(0)
        sp_id = pl.program_id(1)
        s = nch - 1 - sp_id

        @pl.when(sp_id == 0)
        def _():
            c_scr[...] = jnp.zeros_like(c_scr)

        @pl.when(jnp.logical_and(b == 0, sp_id == 0))
        def _():
            dwa_ref[...] = jnp.zeros_like(dwa_ref)
            dwi_ref[...] = jnp.zeros_like(dwi_ref)
            vec_ref[...] = jnp.zeros_like(vec_ref)

        c = cmap(s)
        u = u_ref[...]
        r, ig, spl, la, a, g, ginv = _lru_gates(u, wa_ref, wi_ref, ba_ref, bi_ref, lam_ref)
        dh = jnp.where(c < nct, 0.0, dy_ref[...])
        rowi = lax.broadcasted_iota(jnp.int32, (CH, LW), 0)
        lamv, _ = _chunk_scan(_shift_rows(a, 1, 1.0, rev, rowi), dh, c_scr[0:1, :], rev)
        first = CH - 1 if rev else 0
        c_scr[0:1, :] = (a * lamv)[first:first + 1, :]
        hprow = hp_ref[...][(0 if rev else 7):(1 if rev else 8), :]
        hprow = jnp.where(s > 0, hprow, 0.0)
        h_prev = _shift_rows(h_ref[...], 1, hprow, not rev, rowi)
        da = lamv * h_prev
        db = lamv
        iu = ig * u
        dla = da * a - db * iu * (a * a) * ginv
        dr = dla * (-LRU_C * spl)
        di = db * g * u
        du = db * g * ig
        drp = dr * r * (1.0 - r)
        dip = di * ig * (1.0 - ig)
        dus = []
        for k in range(LB):
            sl = slice(k * LBW, (k + 1) * LBW)
            drk = drp[:, sl].astype(BF16)
            dik = dip[:, sl].astype(BF16)
            uk = u[:, sl].astype(BF16)
            dus.append(_dot_nt(drk, wa_ref[k].astype(BF16)) + _dot_nt(dik, wi_ref[k].astype(BF16)))
            dwa_ref[k] += _dot_tn(uk, drk)
            dwi_ref[k] += _dot_tn(uk, dik)
        du_ref[...] = du + jnp.concatenate(dus, axis=1)
        vec_ref[0:1, :] += jnp.sum(drp, axis=0, keepdims=True)
        vec_ref[1:2, :] += jnp.sum(dip, axis=0, keepdims=True)
        dsp = jnp.sum(dla * (-LRU_C * r), axis=0, keepdims=True)
        vec_ref[2:3, :] += dsp * (-_sigmoid(-lam_ref[...]))

    return pl.pallas_call(
        body, name="lru_bwd_rev" if rev else "lru_bwd", grid=(cfg.Bn, nch),
        in_specs=[pl.BlockSpec((CH, LW), lambda b, sp: (srow(b, sp), ucol))] + _lru_w_specs(d) + [
            pl.BlockSpec((CH, LW), lambda b, sp: (srow(b, sp), 0)),
            pl.BlockSpec((8, LW), lambda b, sp: (prev_rows(b, sp), 0)),
            pl.BlockSpec((CH, LW), lambda b, sp: (lat_row(b, sp), 0))],
        out_specs=[pl.BlockSpec((CH, LW), lambda b, sp: (srow(b, sp), 0)),
                   pl.BlockSpec((LB, LBW, LBW), lambda b, sp: (0, 0, 0)),
                   pl.BlockSpec((LB, LBW, LBW), lambda b, sp: (0, 0, 0)),
                   pl.BlockSpec((8, LW), lambda b, sp: (0, 0))],
        out_shape=[jax.ShapeDtypeStruct((cfg.NT, LW), F32),
                   jax.ShapeDtypeStruct((LB, LBW, LBW), F32),
                   jax.ShapeDtypeStruct((LB, LBW, LBW), F32),
                   jax.ShapeDtypeStruct((8, LW), F32)],
        scratch_shapes=[pltpu.VMEM((8, LW), F32)],
        compiler_params=_cp(2),
    )(act, wa, wi, ba, bi, lam, hd, hd, dyl)


HB = 1024


def _post_ssd_fwd(cfg, y, proj, norm_w):
    TB, nbt, nbl = cfg.TB, cfg.nbt, cfg.nbl
    zc = CONVW // HB

    def body(y_ref, z_ref, w_ref, o_ref):
        u = y_ref[...] * _silu(z_ref[...])
        for gi in range(HB // (DI // NG)):
            sl = slice(gi * 256, (gi + 1) * 256)
            ug = u[:, sl]
            rs = lax.rsqrt(jnp.mean(ug * ug, axis=1, keepdims=True) + RMS_EPS)
            o_ref[:, sl] = (ug * rs * w_ref[:, sl]).astype(BF16)

    def st(b, j, cb):
        return (b * nbt + 1 + j, cb)
    return pl.pallas_call(
        body, name="post_ssd_fwd", grid=(cfg.Bn, nbl, DI // HB),
        in_specs=[pl.BlockSpec((TB, HB), st),
                  pl.BlockSpec((TB, HB), lambda b, j, cb: (b * nbt + 1 + j, zc + cb)),
                  pl.BlockSpec((1, HB), lambda b, j, cb: (0, cb))],
        out_specs=pl.BlockSpec((TB, HB), lambda b, j, cb: (b * nbl + j, cb)),
        out_shape=jax.ShapeDtypeStruct((cfg.N, DI), BF16),
        compiler_params=_cp(3),
    )(y, proj, norm_w)


def _post_ssd_bwd(cfg, dproj, dn, y, act, proj, norm_w):
    TB, nbt, nbl = cfg.TB, cfg.nbt, cfg.nbl
    zc = CONVW // HB

    def body(_, dn_ref, y_ref, xh_ref, z_ref, w_ref, dy_ref, dz_ref, acc_ref):
        b = pl.program_id(1)
        j = pl.program_id(2)

        @pl.when(jnp.logical_and(b == 0, j == 0))
        def _():
            acc_ref[...] = jnp.zeros_like(acc_ref)

        @pl.when(j == 0)
        def _():
            dz_ref[...] = jnp.zeros_like(dz_ref)

        @pl.when(j > 0)
        def _():
            latent(dn_ref, y_ref, xh_ref, z_ref, w_ref, dy_ref, dz_ref, acc_ref)

    def latent(dn_ref, y_ref, xh_ref, z_ref, w_ref, dy_ref, dz_ref, acc_ref):
        xh = xh_ref[...]
        z = z_ref[...]
        y = y_ref[...]
        sg = _sigmoid(z)
        sz = z * sg
        dsz = sg * (1.0 + z * (1.0 - sg))
        u = y * sz
        dout = dn_ref[...]
        for gi in range(HB // (DI // NG)):
            sl = slice(gi * 256, (gi + 1) * 256)
            ug0 = u[:, sl]
            rs = lax.rsqrt(jnp.mean(ug0 * ug0, axis=1, keepdims=True) + RMS_EPS)
            ug = ug0 * rs
            do = dout[:, sl]
            acc_ref[0:1, sl] += jnp.sum(do * ug, axis=0, keepdims=True)
            dug = do * w_ref[:, sl]
            du = rs * (dug - ug * jnp.mean(dug * ug, axis=1, keepdims=True))
            dy = du * sz[:, sl]
            dy_ref[:, sl] = dy
            dz_ref[:, sl] = (du * y[:, sl] * dsz[:, sl]).astype(BF16)
            acc_ref[1:2, sl] += jnp.sum(dy * xh[:, sl], axis=0, keepdims=True)

    def st(cb, b, j):
        return (b * nbt + j, cb)

    def la(cb, b, j):
        return (b * nbl + jnp.maximum(j - 1, 0), cb)
    return pl.pallas_call(
        body, name="post_ssd_bwd", grid=(DI // HB, cfg.Bn, nbt),
        in_specs=[_ANY, pl.BlockSpec((TB, HB), la), pl.BlockSpec((TB, HB), st), pl.BlockSpec((TB, HB), st),
                  pl.BlockSpec((TB, HB), lambda cb, b, j: (b * nbt + j, zc + cb)),
                  pl.BlockSpec((1, HB), lambda cb, b, j: (0, cb))],
        out_specs=[pl.BlockSpec((TB, HB), la),
                   pl.BlockSpec((TB, HB), lambda cb, b, j: (b * nbt + j, zc + cb)),
                   pl.BlockSpec((8, HB), lambda cb, b, j: (0, cb))],
        out_shape=[jax.ShapeDtypeStruct((cfg.N, DI), F32), jax.ShapeDtypeStruct((cfg.NT, PM), BF16),
                   jax.ShapeDtypeStruct((8, DI), F32)],
        input_output_aliases={0: 1},
        compiler_params=_cp(3),
    )(dproj, dn, y, act, proj, norm_w)


def _post_lru_fwd(cfg, hf, hb, proj):
    TB, nbt, nbl = cfg.TB, cfg.nbt, cfg.nbl
    gc = (CONVW + DI) // HB

    def body(hf_ref, hb_ref, g_ref, o_ref):
        o_ref[...] = ((hf_ref[...] + hb_ref[...]) * _gelu(g_ref[...])).astype(BF16)

    st = pl.BlockSpec((TB, HB), lambda b, j: (b * nbt + 1 + j, 0))
    return pl.pallas_call(
        body, name="post_lru_fwd", grid=(cfg.Bn, nbl),
        in_specs=[st, st, pl.BlockSpec((TB, HB), lambda b, j: (b * nbt + 1 + j, gc))],
        out_specs=pl.BlockSpec((TB, HB), lambda b, j: (b * nbl + j, 0)),
        out_shape=jax.ShapeDtypeStruct((cfg.N, LW), BF16),
        compiler_params=_cp(2),
    )(hf, hb, proj)


def _post_lru_bwd(cfg, dproj, dv, hf, hb, proj):
    TB, nbt, nbl = cfg.TB, cfg.nbt, cfg.nbl
    gc = (CONVW + DI) // HB

    def body(_, dv_ref, hf_ref, hb_ref, g_ref, dy_ref, dg_ref):
        j = pl.program_id(1)

        @pl.when(j == 0)
        def _():
            dg_ref[...] = jnp.zeros_like(dg_ref)

        @pl.when(j > 0)
        def _():
            gt = g_ref[...]
            dvv = dv_ref[...]
            gl, dgl = _gelu_and_grad(gt)
            dy_ref[...] = dvv * gl
            dg_ref[...] = (dvv * (hf_ref[...] + hb_ref[...]) * dgl).astype(BF16)

    st = pl.BlockSpec((TB, HB), lambda b, j: (b * nbt + j, 0))
    la = pl.BlockSpec((TB, HB), lambda b, j: (b * nbl + jnp.maximum(j - 1, 0), 0))
    gcol = pl.BlockSpec((TB, HB), lambda b, j: (b * nbt + j, gc))
    return pl.pallas_call(
        body, name="post_lru_bwd", grid=(cfg.Bn, nbt),
        in_specs=[_ANY, la, st, st, gcol],
        out_specs=[la, gcol],
        out_shape=[jax.ShapeDtypeStruct((cfg.N, LW), F32), jax.ShapeDtypeStruct((cfg.NT, PM), BF16)],
        input_output_aliases={0: 1},
        compiler_params=_cp(2),
    )(dproj, dv, hf, hb, proj)


def _merge_fwd(cfg, proj, b_gate, br_ssd, br_lru):
    TB, nbt, nbl = cfg.TB, cfg.nbt, cfg.nbl
    mc = (CONVW + DI + LW) // HB

    def body(ms_ref, ml_ref, bg_ref, bs_ref, bl_ref, o_ref):
        gs = _sigmoid(ms_ref[...] + bg_ref[:, :D])
        gl = _sigmoid(ml_ref[...] + bg_ref[:, D:])
        o_ref[...] = (gs * bs_ref[...] + gl * bl_ref[...]).astype(BF16)

    la = pl.BlockSpec((TB, D), lambda b, j: (b * nbl + j, 0))
    return pl.pallas_call(
        body, name="merge_fwd", grid=(cfg.Bn, nbl),
        in_specs=[pl.BlockSpec((TB, HB), lambda b, j: (b * nbt + 1 + j, mc)),
                  pl.BlockSpec((TB, HB), lambda b, j: (b * nbt + 1 + j, mc + 1)),
                  pl.BlockSpec((1, 2 * D), lambda b, j: (0, 0)), la, la],
        out_specs=la,
        out_shape=jax.ShapeDtypeStruct((cfg.N, D), BF16),
        compiler_params=_cp(2),
    )(proj, proj, b_gate, br_ssd, br_lru)


def _merge_bwd(cfg, dmix, proj, b_gate, br_ssd, br_lru):
    TB, nbt, nbl = cfg.TB, cfg.nbt, cfg.nbl
    mc = (CONVW + DI + LW) // HB

    def body(dm_ref, ms_ref, ml_ref, bg_ref, bs_ref, bl_ref, ds_ref, dl_ref, dmg_ref, acc_ref):
        b = pl.program_id(0)
        j = pl.program_id(1)

        @pl.when(jnp.logical_and(b == 0, j == 0))
        def _():
            acc_ref[...] = jnp.zeros_like(acc_ref)

        @pl.when(j == 0)
        def _():
            dmg_ref[...] = jnp.zeros_like(dmg_ref)

        @pl.when(j > 0)
        def _():
            latent(dm_ref, ms_ref, ml_ref, bg_ref, bs_ref, bl_ref, ds_ref, dl_ref, dmg_ref, acc_ref)

    def latent(dm_ref, ms_ref, ml_ref, bg_ref, bs_ref, bl_ref, ds_ref, dl_ref, dmg_ref, acc_ref):
        dm = dm_ref[...]
        gs = _sigmoid(ms_ref[...] + bg_ref[:, :D])
        gl = _sigmoid(ml_ref[...] + bg_ref[:, D:])
        ds_ref[...] = (dm * gs).astype(BF16)
        dl_ref[...] = (dm * gl).astype(BF16)
        dps = dm * bs_ref[...] * gs * (1.0 - gs)
        dpl = dm * bl_ref[...] * gl * (1.0 - gl)
        dmg_ref[:, :D] = dps.astype(BF16)
        dmg_ref[:, D:] = dpl.astype(BF16)
        acc_ref[0:1, :D] += jnp.sum(dps, axis=0, keepdims=True)
        acc_ref[0:1, D:] += jnp.sum(dpl, axis=0, keepdims=True)

    la = pl.BlockSpec((TB, D), lambda b, j: (b * nbl + jnp.maximum(j - 1, 0), 0))
    return pl.pallas_call(
        body, name="merge_bwd", grid=(cfg.Bn, nbt),
        in_specs=[la, pl.BlockSpec((TB, HB), lambda b, j: (b * nbt + j, mc)),
                  pl.BlockSpec((TB, HB), lambda b, j: (b * nbt + j, mc + 1)),
                  pl.BlockSpec((1, 2 * D), lambda b, j: (0, 0)), la, la],
        out_specs=[la, la, pl.BlockSpec((TB, 2 * D), lambda b, j: (b * nbt + j, mc // 2)),
                   pl.BlockSpec((8, 2 * D), lambda b, j: (0, 0))],
        out_shape=[jax.ShapeDtypeStruct((cfg.N, D), BF16), jax.ShapeDtypeStruct((cfg.N, D), BF16),
                   jax.ShapeDtypeStruct((cfg.NT, PM), BF16), jax.ShapeDtypeStruct((8, 2 * D), F32)],
        compiler_params=_cp(2),
    )(dmix, proj, proj, b_gate, br_ssd, br_lru)


def _resid1_fwd(cfg, x2, x_mix, gate1, shift2, scale2, ln1_g, ln1_b):
    TB, nbt, nbl = cfg.TB, cfg.nbt, cfg.nbl

    def body(x_ref, xm_ref, g1_ref, sh_ref, sc_ref, lg_ref, lb_ref, x1_ref, h2_ref):
        r1 = ALPHA * x_ref[...] + g1_ref[...] * xm_ref[...]
        xh, _ = _ln(r1)
        x1 = xh * lg_ref[...] + lb_ref[...]
        x1_ref[...] = x1
        xh2, _ = _ln(x1)
        h2_ref[...] = (xh2 * (1.0 + sc_ref[...]) + sh_ref[...]).astype(BF16)

    la = pl.BlockSpec((TB, D), lambda b, j: (b * nbl + j, 0))
    ex = pl.BlockSpec((None, 1, D), lambda b, j: (b, 0, 0))
    vec = pl.BlockSpec((1, D), lambda b, j: (0, 0))
    return pl.pallas_call(
        body, name="resid1_fwd", grid=(cfg.Bn, nbl),
        in_specs=[la, la, ex, ex, ex, vec, vec],
        out_specs=[la, la],
        out_shape=[jax.ShapeDtypeStruct((cfg.N, D), F32), jax.ShapeDtypeStruct((cfg.N, D), BF16)],
        compiler_params=_cp(2),
    )(x2, x_mix, gate1, shift2, scale2, ln1_g, ln1_b)


def _resid1_bwd(cfg, dh2, dx1p, x1, x2, x_mix, gate1, scale2, ln1_g):
    TB, nbt, nbl = cfg.TB, cfg.nbt, cfg.nbl

    def body(dh2_ref, dx1p_ref, x1_ref, x_ref, xm_ref, g1_ref, sc_ref, lg_ref,
             dxm_ref, dxp_ref, ex_ref, gl_ref):
        b = pl.program_id(0)
        j = pl.program_id(1)

        @pl.when(j == 0)
        def _():
            ex_ref[...] = jnp.zeros_like(ex_ref)

        @pl.when(jnp.logical_and(b == 0, j == 0))
        def _():
            gl_ref[...] = jnp.zeros_like(gl_ref)

        dh2 = dh2_ref[...]
        xh2, rs2 = _ln(x1_ref[...])
        ex_ref[0:1, :] += jnp.sum(dh2, axis=0, keepdims=True)
        ex_ref[1:2, :] += jnp.sum(dh2 * xh2, axis=0, keepdims=True)
        dx1 = dx1p_ref[...] + _ln_bwd(dh2 * (1.0 + sc_ref[...]), xh2, rs2)
        xm = xm_ref[...]
        g1 = g1_ref[...]
        r1 = ALPHA * x_ref[...] + g1 * xm
        xh1, rs1 = _ln(r1)
        gl_ref[0:1, :] += jnp.sum(dx1 * xh1, axis=0, keepdims=True)
        gl_ref[1:2, :] += jnp.sum(dx1, axis=0, keepdims=True)
        dr1 = _ln_bwd(dx1 * lg_ref[...], xh1, rs1)
        ex_ref[2:3, :] += jnp.sum(dr1 * xm, axis=0, keepdims=True)
        dxm_ref[...] = (dr1 * g1).astype(BF16)
        dxp_ref[...] = ALPHA * dr1

    la = pl.BlockSpec((TB, D), lambda b, j: (b * nbl + j, 0))
    ex = pl.BlockSpec((None, 1, D), lambda b, j: (b, 0, 0))
    vec = pl.BlockSpec((1, D), lambda b, j: (0, 0))
    return pl.pallas_call(
        body, name="resid1_bwd", grid=(cfg.Bn, nbl),
        in_specs=[la, la, la, la, la, ex, ex, vec],
        out_specs=[la, la, pl.BlockSpec((None, 8, D), lambda b, j: (b, 0, 0)),
                   pl.BlockSpec((8, D), lambda b, j: (0, 0))],
        out_shape=[jax.ShapeDtypeStruct((cfg.N, D), BF16), jax.ShapeDtypeStruct((cfg.N, D), F32),
                   jax.ShapeDtypeStruct((cfg.Bn, 8, D), F32), jax.ShapeDtypeStruct((8, D), F32)],
        compiler_params=_cp(2),
    )(dh2, dx1p, x1, x2, x_mix, gate1, scale2, ln1_g)


def _final_fwd_bwd(cfg, x1, mlp, b2, gate2, ln2_g, ln2_b, target):
    TB, nbl = cfg.TB, cfg.nbl

    def body(x1_ref, m_ref, b2_ref, g2_ref, lg_ref, lb_ref, t_ref, dm_ref, dx_ref, ex_ref, gl_ref):
        b = pl.program_id(0)
        j = pl.program_id(1)

        @pl.when(j == 0)
        def _():
            ex_ref[...] = jnp.zeros_like(ex_ref)

        @pl.when(jnp.logical_and(b == 0, j == 0))
        def _():
            gl_ref[...] = jnp.zeros_like(gl_ref)

        mv = m_ref[...] + b2_ref[...]
        g2 = g2_ref[...]
        r2 = ALPHA * x1_ref[...] + g2 * mv
        xh, rs = _ln(r2)
        lg = lg_ref[...]
        x2 = xh * lg + lb_ref[...]
        err = x2 - t_ref[...]
        ls = jnp.sum(jnp.sum(err * err, axis=1, keepdims=True), axis=0, keepdims=True) * (0.5 / D)
        gl_ref[3:4, :] += ls
        dx2 = err * (1.0 / D)
        gl_ref[0:1, :] += jnp.sum(dx2 * xh, axis=0, keepdims=True)
        gl_ref[1:2, :] += jnp.sum(dx2, axis=0, keepdims=True)
        dr2 = _ln_bwd(dx2 * lg, xh, rs)
        ex_ref[0:1, :] += jnp.sum(dr2 * mv, axis=0, keepdims=True)
        dmv = dr2 * g2
        gl_ref[2:3, :] += jnp.sum(dmv, axis=0, keepdims=True)
        dm_ref[...] = dmv.astype(BF16)
        dx_ref[...] = ALPHA * dr2

    la = pl.BlockSpec((TB, D), lambda b, j: (b * nbl + j, 0))
    ex = pl.BlockSpec((None, 1, D), lambda b, j: (b, 0, 0))
    vec = pl.BlockSpec((1, D), lambda b, j: (0, 0))
    return pl.pallas_call(
        body, name="final_fwd_bwd", grid=(cfg.Bn, nbl),
        in_specs=[la, la, vec, ex, vec, vec, la],
        out_specs=[la, la, pl.BlockSpec((None, 8, D), lambda b, j: (b, 0, 0)),
                   pl.BlockSpec((8, D), lambda b, j: (0, 0))],
        out_shape=[jax.ShapeDtypeStruct((cfg.N, D), BF16), jax.ShapeDtypeStruct((cfg.N, D), F32),
                   jax.ShapeDtypeStruct((cfg.Bn, 8, D), F32), jax.ShapeDtypeStruct((8, D), F32)],
        compiler_params=_cp(2),
    )(x1, mlp, b2, gate2, ln2_g, ln2_b, target)


def _ln_mod_bwd(cfg, dh_a, ddt_raw, w_dt, ctx2, x2, scale_tab, dxp):
    TB, nbt, nbl = cfg.TB, cfg.nbt, cfg.nbl

    def body(da_ref, dd_ref, wd_ref, c_ref, x_ref, sc_ref, dxp_ref, gx_ref, acc_ref):
        j = pl.program_id(1)

        @pl.when(j <= 1)
        def _():
            acc_ref[...] = jnp.zeros_like(acc_ref)

        dh = da_ref[...] + _dot_nt(dd_ref[...].astype(BF16), wd_ref[...])
        xhat, rs = _ln(jnp.where(j == 0, c_ref[...], x_ref[...]))
        acc_ref[0:1, :] += jnp.sum(dh, axis=0, keepdims=True)
        acc_ref[1:2, :] += jnp.sum(dh * xhat, axis=0, keepdims=True)
        gx_ref[...] = dxp_ref[...] + _ln_bwd(dh * (1.0 + sc_ref[...]), xhat, rs)

    st = pl.BlockSpec((TB, D), lambda b, j: (b * nbt + j, 0))
    la = pl.BlockSpec((TB, D), lambda b, j: (b * nbl + jnp.maximum(j - 1, 0), 0))
    return pl.pallas_call(
        body, name="ln_mod_bwd", grid=(cfg.Bn, nbt),
        in_specs=[st, pl.BlockSpec((TB, DTW), lambda b, j: (b * nbt + j, 0)),
                  pl.BlockSpec((D, DTW), lambda b, j: (0, 0)),
                  pl.BlockSpec((TB, D), lambda b, j: (b, 0)), la,
                  pl.BlockSpec((None, 1, D), lambda b, j: (2 * b + jnp.minimum(j, 1), 0, 0)), la],
        out_specs=[la, pl.BlockSpec((None, 8, D), lambda b, j: (2 * b + jnp.minimum(j, 1), 0, 0))],
        out_shape=[jax.ShapeDtypeStruct((cfg.N, D), F32), jax.ShapeDtypeStruct((2 * cfg.Bn, 8, D), F32)],
        compiler_params=_cp(2),
    )(dh_a, ddt_raw, w_dt, ctx2, x2, scale_tab, dxp)


def _perm_w_in(w_in):
    w_main = jnp.concatenate([w_in[:, 0:3072], w_in[:, 4160:5184], w_in[:, 3136:4160], w_in[:, 5184:10304]], axis=1)
    w_dt = jnp.pad(w_in[:, 3072:3136], ((0, 0), (0, DTW - 2 * NH)))
    return w_main, w_dt


def _unperm_w_in(dw_main, dw_dt):
    return jnp.concatenate([dw_main[:, 0:3072], dw_dt[:, :2 * NH], dw_main[:, 4096:5120],
                            dw_main[:, 3072:4096], dw_main[:, 5120:]], axis=1)


def _unpack_rest(rest_all):
    out, off = {}, 0
    for n, shp, axis in _BIG[1:]:
        shard_shape = (shp[0] // NDEV, shp[1]) if axis == 0 else (shp[0], shp[1] // NDEV)
        r = math.prod(shard_shape) // 1024
        out[n] = _from_slots(rest_all[:, off:off + r, :].reshape((NDEV,) + shard_shape), axis)
        off += r
    return out


def _unpack_small(small_all):
    shard_shapes = [(s[0], s[1] // NDEV) for _, s in _SMALL_SH]
    pieces = _unpack_rows(small_all, shard_shapes, lead=(NDEV,))
    return {n: p.transpose(1, 0, 2).reshape(shp) for (n, shp), p in zip(_SMALL_SH, pieces)}


def _local_step(cfg, x, ctx, target, m, mc, W, w_in_shard, rest_payload, small_payload):
    Bn, T, Tc = cfg.Bn, cfg.T, cfg.Tc
    NT, N = cfg.NT, cfg.N
    ctx2, x2 = ctx.reshape(Bn * Tc, D), x.reshape(N, D)
    mch = [m[:, i * D:(i + 1) * D] for i in range(NMOD)]
    ctx_sh = jnp.broadcast_to(mc[None, :D], (Bn, D))
    ctx_sc = jnp.broadcast_to(mc[None, D:], (Bn, D))
    shift_tab = jnp.stack([ctx_sh, mch[0]], axis=1).reshape(2 * Bn, 1, D)
    scale_tab = jnp.stack([ctx_sc, mch[1]], axis=1).reshape(2 * Bn, 1, D)
    gate1 = mch[2].reshape(Bn, 1, D)
    shift2 = mch[3].reshape(Bn, 1, D)
    scale2 = mch[4].reshape(Bn, 1, D)
    gate2 = mch[5].reshape(Bn, 1, D)

    conv_b = jnp.concatenate([W["ssd_conv_b"], W["lru_conv_b"]], axis=1)
    dt_bias = jnp.pad(W["ssd_dt_bias"].reshape(1, 2 * NH), ((0, 0), (0, DTW - 2 * NH)))
    a_log = jnp.pad(W["ssd_a_log"].reshape(1, 2 * NH), ((0, 0), (0, DTW - 2 * NH)))
    dvec = jnp.repeat(W["ssd_d"].reshape(NH), HD).reshape(1, DI)

    h, w_in_all = _ln_mod_fwd(cfg, ctx2, x2, shift_tab, scale_tab, xch=(w_in_shard, TWO_LEVEL))
    w_main, w_dt = _perm_w_in(_from_slots(w_in_all, 1))
    proj, rest_all = _mm(h, w_main, "nn", "mm_proj", tm=1024, tn=2048, tk=1024, xch=(rest_payload, True))
    W = dict(W, w_main=w_main, w_dt=w_dt, **_unpack_rest(rest_all))
    (dt_raw, dt, dtg, cumg, cumTg), small_all = _dt_fwd(cfg, h, w_dt, dt_bias, a_log, xch=(small_payload, True))
    W = dict(W, **_unpack_small(small_all))
    conv_w = jnp.concatenate([W["ssd_conv_w"], W["lru_conv_w"]], axis=1)
    lba = W["lru_ba"].reshape(2, 1, LW)
    lbi = W["lru_bi"].reshape(2, 1, LW)
    llam = W["lru_lambda"].reshape(2, 1, LW)
    act, sgrad = _conv_fwd(cfg, proj, conv_w, conv_b)
    y_f, hs_f = _ssd_fwd(cfg, act, dtg, cumg, cumTg, False)
    y, hs_b = _ssd_fwd(cfg, act, dtg, cumg, cumTg, True, y_other=y_f, dvec=dvec)
    hss = [hs_f, hs_b]
    hls = [_lru_fwd(cfg, act, W["lru_wa"], W["lru_wi"], lba, lbi, llam, rev) for rev in (False, True)]
    nssd = _post_ssd_fwd(cfg, y, proj, W["ssd_norm_w"])
    vlru = _post_lru_fwd(cfg, hls[0], hls[1], proj)
    br_ssd = _mm(nssd, W["w_br_ssd"], "nn", "mm_br_ssd", tm=1024, tn=1024, tk=1024)
    br_lru = _mm(vlru, W["w_br_lru"], "nn", "mm_br_lru", tm=1024, tn=1024, tk=1024)
    mix = _merge_fwd(cfg, proj, W["b_gate"], br_ssd, br_lru)
    x_mix = _mm(mix, W["w_out"], "nn", "mm_out", tm=1024, tn=1024, tk=1024)
    x1, h2 = _resid1_fwd(cfg, x2, x_mix, gate1, shift2, scale2, W["ln1_g"], W["ln1_b"])
    a1, actm = _mm_mlp1(h2, W["w_mlp1"], W["b_mlp1"])
    mlp = _mm(actm, W["w_mlp2"], "nn", "mm_mlp2", tm=1024, tn=1024, tk=2048)
    dmlp, dx1p, ex2, gl2 = _final_fwd_bwd(cfg, x1, mlp, W["b_mlp2"], gate2, W["ln2_g"], W["ln2_b"],
                                          target.reshape(N, D))

    g = {}
    g["ln2_g"], g["ln2_b"], g["b_mlp2"] = gl2[0:1], gl2[1:2], gl2[2:3]
    loss_partial = gl2[3, 0]
    gw = {}
    gw["w_mlp2"] = _mm(actm, dmlp, "tn", "mm_dw_mlp2", BF16, tm=1024, tn=1024, tk=1024)
    da1, accb1 = _mm_dact(dmlp, W["w_mlp2"], a1)
    g["b_mlp1"] = accb1[0:1]
    dh2 = _mm(da1, W["w_mlp1"], "nt", "mm_dh2", tm=1024, tn=1024, tk=2048)
    gw["w_mlp1"] = _mm(h2, da1, "tn", "mm_dw_mlp1", BF16, tm=1024, tn=1024, tk=1024)
    dx_mix, dxp, ex1, gl1 = _resid1_bwd(cfg, dh2, dx1p, x1, x2, x_mix, gate1, scale2, W["ln1_g"])
    g["ln1_g"], g["ln1_b"] = gl1[0:1], gl1[1:2]
    dmix = _mm(dx_mix, W["w_out"], "nt", "mm_dmix", tm=1024, tn=1024, tk=1024)
    gw["w_out"] = _mm(mix, dx_mix, "tn", "mm_dw_out", BF16, tm=1024, tn=1024, tk=1024)
    dbs, dbl, dproj, accg = _merge_bwd(cfg, dmix, proj, W["b_gate"], br_ssd, br_lru)
    g["b_gate"] = accg[0:1]
    dnssd = _mm(dbs, W["w_br_ssd"], "nt", "mm_dnssd", tm=1024, tn=1024, tk=1024)
    gw["w_br_ssd"] = _mm(nssd, dbs, "tn", "mm_dw_br_ssd", BF16, tm=1024, tn=1024, tk=1024)
    dvlru = _mm(dbl, W["w_br_lru"], "nt", "mm_dvlru", tm=1024, tn=1024, tk=1024)
    gw["w_br_lru"] = _mm(vlru, dbl, "tn", "mm_dw_br_lru", BF16, tm=1024, tn=1024, tk=1024)
    dy, dproj, accs = _post_ssd_bwd(cfg, dproj, dnssd, y, act, proj, W["ssd_norm_w"])
    g["ssd_norm_w"] = accs[0:1]
    dD_cols = accs[1:2]
    dyl, dproj = _post_lru_bwd(cfg, dproj, dvlru, hls[0], hls[1], proj)

    rest_slots = jnp.concatenate([_to_slots(gw[n], axis).reshape(NDEV, -1, 1024) for n, _, axis in _BIG[1:]], axis=1)
    xres = {}
    dxh, dBs, dCs, dAs, dxxs, dus = [], [], [], [], [], []
    dwas, dwis, lvecs = [], [], []
    for i, rev in enumerate((False, True)):
        if i == 0:
            o, xres["rs_rest"] = _ssd_bwd(cfg, act, dtg, cumg, cumTg, hss[i], dy, rev, xch=(rest_slots, False))
        else:
            o = _ssd_bwd(cfg, act, dtg, cumg, cumTg, hss[i], dy, rev)
        dxh.append(o[0]); dBs.append(o[1]); dCs.append(o[2]); dAs.append(o[3]); dxxs.append(o[4])
        du, dwa, dwi, lv = _lru_bwd(cfg, act, W["lru_wa"], W["lru_wi"], lba, lbi, llam, hls[i], dyl, rev)
        dus.append(du); dwas.append(dwa); dwis.append(dwi); lvecs.append(lv)
    lru_payload = jnp.stack([jnp.stack(dwas), jnp.stack(dwis)]).reshape(-1, 1024)
    g["lru_ba"] = jnp.stack([lvecs[0][0], lvecs[1][0]])
    g["lru_bi"] = jnp.stack([lvecs[0][1], lvecs[1][1]])
    g["lru_lambda"] = jnp.stack([lvecs[0][2], lvecs[1][2]])

    ddt_raw, accdt, dw_dt = _dt_bwd(cfg, dAs, dxxs, dt_raw, dt, dt_bias, a_log, h)
    g["ssd_a_log"] = accdt[0, :2 * NH].reshape(2, NH)
    g["ssd_dt_bias"] = accdt[1, :2 * NH].reshape(2, NH)

    (dproj, accx), xres["ag_lru"] = _conv_bwd(cfg, "conv_bwd_x", dproj, proj, conv_w, sgrad, [dxh[0], dxh[1]], 0, DI,
                                              skip=(dy, dvec), xch=(lru_payload, True))
    dproj, accB = _conv_bwd(cfg, "conv_bwd_b", dproj, proj, conv_w, sgrad, [dBs[0], dBs[1]], DI, NG * NS)
    dproj, accC = _conv_bwd(cfg, "conv_bwd_c", dproj, proj, conv_w, sgrad, [dCs[0], dCs[1]], DI + NG * NS, NG * NS)
    dproj, accl = _conv_bwd(cfg, "conv_bwd_lru", dproj, proj, conv_w, None, [dus[0], dus[1]], DI + 2 * NG * NS, LW)
    accssd = jnp.concatenate([accx, accB, accC], axis=1)
    g["ssd_conv_w"], g["ssd_conv_b"] = accssd[0:4], accssd[4:5]
    g["lru_conv_w"], g["lru_conv_b"] = accl[0:4], accl[4:5]
    dw_main = _mm(h, dproj, "tn", "mm_dw_main", BF16, tm=1024, tn=2048, tk=1024)
    w_in_slots = _to_slots(_unperm_w_in(dw_main, dw_dt.astype(BF16)), 1)
    dh_a, xres["rs_w_in"] = _mm(dproj, W["w_main"], "nt", "mm_dh_main", tm=1024, tn=1024, tk=2048,
                                xch=(w_in_slots, False))
    grad_x, acct = _ln_mod_bwd(cfg, dh_a, ddt_raw, W["w_dt"], ctx2, x2, scale_tab, dxp)
    acct = acct.reshape(Bn, 2, 8, D)
    dm = jnp.concatenate([acct[:, 1, 0], acct[:, 1, 1], ex1[:, 2], ex1[:, 0], ex1[:, 1], ex2[:, 0]], axis=1)
    dmc = jnp.concatenate([acct[:, 0, 0], acct[:, 0, 1]], axis=1)
    g["ssd_d_cols"] = dD_cols
    return loss_partial, grad_x.reshape(Bn, T, D), g, dm, dmc, xres


MESH = pl.DeviceIdType.MESH
_HBM = pl.BlockSpec(memory_space=pltpu.HBM)


def _me():
    return 4 * lax.axis_index("x") + 2 * lax.axis_index("y") + lax.axis_index("c")


def _peer(k):
    px = (lax.axis_index("x") + ((k >> 2) & 1)) % 2
    py = (lax.axis_index("y") + ((k >> 1) & 1)) % 2
    pc = (lax.axis_index("c") + (k & 1)) % 2
    return (px, py, pc), 4 * px + 2 * py + pc


def _xchg_copies(x_ref, o_ref, send_sems, recv_sems, loc_sem, gather):
    me = _me()
    src_me = x_ref if gather else x_ref.at[me]
    loc = pltpu.make_async_copy(src_me, o_ref.at[me], loc_sem)
    sends, recvs = [], []
    for k in range(1, NDEV):
        peer, pid = _peer(k)
        sends.append(pltpu.make_async_remote_copy(
            src_ref=x_ref if gather else x_ref.at[pid], dst_ref=o_ref.at[me],
            send_sem=send_sems.at[k - 1], recv_sem=recv_sems.at[k - 1],
            device_id=peer, device_id_type=MESH))
        recvs.append(pltpu.make_async_remote_copy(
            src_ref=src_me, dst_ref=o_ref.at[pid],
            send_sem=send_sems.at[k - 1], recv_sem=recv_sems.at[k - 1],
            device_id=peer, device_id_type=MESH))
    return loc, sends, recvs


def _xchg_start(*refs, gather):
    loc, sends, _ = _xchg_copies(*refs, gather)
    loc.start()
    for cp in sends:
        cp.start()


def _xchg_wait(*refs, gather):
    loc, sends, recvs = _xchg_copies(*refs, gather)
    for cp in recvs:
        cp.wait_recv()
    for cp in sends:
        cp.wait_send()
    loc.wait()


_XCHG_SCRATCH = [pltpu.SemaphoreType.DMA((NDEV - 1,)), pltpu.SemaphoreType.DMA((NDEV - 1,)), pltpu.SemaphoreType.DMA]


def _xchg_out_shape(x, gather):
    return jax.ShapeDtypeStruct((NDEV,) + tuple(x.shape if gather else x.shape[1:]), x.dtype)


def _exchange(x, name, gather):
    def body(x_ref, o_ref, send_sems, recv_sems, loc_sem):
        _xchg_start(x_ref, o_ref, send_sems, recv_sems, loc_sem, gather=gather)
        _xchg_wait(x_ref, o_ref, send_sems, recv_sems, loc_sem, gather=gather)

    return pl.pallas_call(
        body, name=name, out_shape=_xchg_out_shape(x, gather),
        in_specs=[_HBM], out_specs=_HBM, scratch_shapes=_XCHG_SCRATCH,
    )(x)


TWO_LEVEL = "two_level"


def _two_level_copies(x_ref, o_ref, send_sems, recv_sems, loc_sem):
    mx, my, mc = lax.axis_index("x"), lax.axis_index("y"), lax.axis_index("c")
    me, sibling = (mx, my, mc), (mx, my, 1 - mc)
    chips = [(1 - mx, my), (mx, 1 - my), (1 - mx, 1 - my)]

    def slot(px, py, pc):
        return o_ref.at[4 * px + 2 * py + pc]

    def copy(k, block, to, src=None):
        return pltpu.make_async_remote_copy(
            src_ref=slot(*block) if src is None else src, dst_ref=slot(*block),
            send_sem=send_sems.at[k], recv_sem=recv_sems.at[k], device_id=to, device_id_type=MESH)

    mine = pltpu.make_async_copy(x_ref, slot(*me), loc_sem)
    first = [copy(0, me, sibling, src=x_ref)] + [copy(1 + j, me, (*chip, mc), src=x_ref) for j, chip in enumerate(chips)]
    passed = [copy(4 + j, (*chip, mc), sibling) for j, chip in enumerate(chips)]
    landed = [copy(1 + j, (*chip, mc), me) for j, chip in enumerate(chips)]
    from_sibling = [copy(0, sibling, me)] + [copy(4 + j, (*chip, 1 - mc), me) for j, chip in enumerate(chips)]
    return mine, first, passed, landed, from_sibling


def _two_level_start(*refs):
    mine, first, _, _, _ = _two_level_copies(*refs)
    mine.start()
    for cp in first:
        cp.start()


def _two_level_finish(*refs):
    mine, first, passed, landed, from_sibling = _two_level_copies(*refs)
    for cp, fwd in zip(landed, passed):
        cp.wait_recv()
        fwd.start()
    for cp in from_sibling:
        cp.wait_recv()
    for cp in first + passed:
        cp.wait_send()
    mine.wait()


def _hosted_call(body, xch, *, name, grid, in_specs, out_specs, out_shape, scratch_shapes, compiler_params, args,
                 aliases=None):
    aliases = aliases or {}
    if xch is None:
        return pl.pallas_call(body, name=name, grid=grid, in_specs=in_specs, out_specs=out_specs,
                              out_shape=out_shape, scratch_shapes=scratch_shapes, input_output_aliases=aliases,
                              compiler_params=compiler_params)(*args)
    xv, gather = xch
    n_in, n_out, n_scr = len(in_specs), len(out_specs), len(scratch_shapes)

    def wrapped(*refs):
        ins = refs[:n_in]
        x_ref = refs[n_in]
        outs = refs[n_in + 1:n_in + 1 + n_out]
        o_ref = refs[n_in + 1 + n_out]
        scr = refs[n_in + 2 + n_out:]
        own, sems = scr[:n_scr], scr[n_scr:]
        first = functools.reduce(jnp.logical_and, [pl.program_id(a) == 0 for a in range(len(grid))])
        last = functools.reduce(jnp.logical_and, [pl.program_id(a) == grid[a] - 1 for a in range(len(grid))])

        @pl.when(first)
        def _():
            if gather == TWO_LEVEL:
                _two_level_start(x_ref, o_ref, *sems)
            else:
                _xchg_start(x_ref, o_ref, *sems, gather=gather)

        body(*ins, *outs, *own)

        @pl.when(last)
        def _():
            if gather == TWO_LEVEL:
                _two_level_finish(x_ref, o_ref, *sems)
            else:
                _xchg_wait(x_ref, o_ref, *sems, gather=gather)

    res = pl.pallas_call(
        wrapped, name=name, grid=grid, in_specs=list(in_specs) + [_HBM], out_specs=list(out_specs) + [_HBM],
        out_shape=list(out_shape) + [_xchg_out_shape(xv, gather)],
        scratch_shapes=list(scratch_shapes) + _XCHG_SCRATCH, input_output_aliases=aliases,
        compiler_params=compiler_params,
    )(*args, xv)
    return list(res[:n_out]), res[n_out]


def _row_tile(R, cap, mult=8):
    best = mult
    t = mult
    while t <= min(R, cap):
        if R % t == 0:
            best = t
        t += mult
    assert R % best == 0, R
    return best


def _sum_slots(x, name, xch=None):
    _, R, C = x.shape
    tr = _row_tile(R, 256, 16 if x.dtype == BF16 else 8)

    def body(x_ref, o_ref):
        o_ref[...] = _slot_sum(x_ref)

    res = _hosted_call(
        body, xch, name=name, grid=(R // tr,),
        in_specs=[pl.BlockSpec((NDEV, tr, C), lambda i: (0, i, 0))],
        out_specs=[pl.BlockSpec((tr, C), lambda i: (i, 0))],
        out_shape=[jax.ShapeDtypeStruct((R, C), F32)],
        scratch_shapes=[], compiler_params=_cp(1), args=(x,))
    if xch is None:
        return res[0]
    return res[0][0], res[1]


def _slot_sum(x_ref):
    acc = x_ref[0].astype(F32)
    for i in range(1, NDEV):
        acc = acc + x_ref[i].astype(F32)
    return acc


def _sum_adamw(slots, w, m, v, name, xch=None):
    _, R, C = slots.shape
    tr = _row_tile(R, 128, 16 if slots.dtype == BF16 else 8)

    def body(x_ref, w_ref, m_ref, v_ref, g_ref, d_ref, nm_ref, nv_ref):
        g_ref[...] = _slot_sum(x_ref)
        _adamw_update(w_ref, g_ref, m_ref, v_ref, d_ref, nm_ref, nv_ref)

    blk = pl.BlockSpec((tr, C), lambda i: (i, 0))
    res = _hosted_call(
        body, xch, name=name, grid=(R // tr,),
        in_specs=[pl.BlockSpec((NDEV, tr, C), lambda i: (0, i, 0)), blk, blk, blk],
        out_specs=[blk] * 4, out_shape=[jax.ShapeDtypeStruct((R, C), F32)] * 4,
        scratch_shapes=[], compiler_params=_cp(1), args=(slots, w, m, v))
    if xch is None:
        return res
    return res[0], res[1]


def _adamw_update(w_ref, g_ref, m_ref, v_ref, d_ref, nm_ref, nv_ref):
    c1 = 1.0 / (1.0 - ADAM_B1 ** ADAM_STEP)
    c2 = 1.0 / (1.0 - ADAM_B2 ** ADAM_STEP)
    gv = g_ref[...]
    nm = ADAM_B1 * m_ref[...] + (1.0 - ADAM_B1) * gv
    nv = ADAM_B2 * v_ref[...] + (1.0 - ADAM_B2) * (gv * gv)
    d_ref[...] = -ADAM_LR * ((nm * c1) / (jnp.sqrt(nv * c2) + ADAM_EPS) + ADAM_WD * w_ref[...])
    nm_ref[...] = nm
    nv_ref[...] = nv


def _adamw_many(ws, gs, ms, vs):
    n = len(ws)

    def body(*refs):
        for i in range(n):
            _adamw_update(refs[i], refs[n + i], refs[2 * n + i], refs[3 * n + i],
                          refs[4 * n + i], refs[5 * n + i], refs[6 * n + i])

    shapes = [jax.ShapeDtypeStruct(w.shape, F32) for w in ws]
    res = pl.pallas_call(
        body, name="adamw_small", out_shape=shapes * 3,
        compiler_params=pltpu.CompilerParams(vmem_limit_bytes=VMEM_LIMIT_BYTES),
    )(*ws, *gs, *ms, *vs)
    return res[:n], res[n:2 * n], res[2 * n:]


def _adamw(w, g, m, v, name):
    R, C = w.shape
    tr = _row_tile(R, 256)

    def body(w_ref, g_ref, m_ref, v_ref, d_ref, nm_ref, nv_ref):
        _adamw_update(w_ref, g_ref, m_ref, v_ref, d_ref, nm_ref, nv_ref)

    blk = pl.BlockSpec((tr, C), lambda i: (i, 0))
    return pl.pallas_call(
        body, name=name, grid=(R // tr,),
        in_specs=[blk] * 4, out_specs=[blk] * 3,
        out_shape=[jax.ShapeDtypeStruct((R, C), F32)] * 3,
        compiler_params=_cp(1),
    )(w, g, m, v)


def _mod_fwd(c_rows, w_shard, b_shard):
    def body(c_ref, w_ref, b_ref, o_ref):
        s = _silu(c_ref[...]).astype(BF16)
        o_ref[...] = _dot(s, w_ref[...].astype(BF16)) + b_ref[...]

    return pl.pallas_call(
        body, name="mod_fwd",
        out_shape=jax.ShapeDtypeStruct((c_rows.shape[0], w_shard.shape[1]), F32),
        compiler_params=pltpu.CompilerParams(vmem_limit_bytes=VMEM_LIMIT_BYTES),
    )(c_rows, w_shard, b_shard)


def _mod_bwd(c_rows, dm_all, dm_shard, w_shard):
    nrow = c_rows.shape[0]

    def body(c_ref, da_ref, ds_ref, w_ref, gw_ref, gb_ref, cc_ref):
        s = _silu(c_ref[...]).astype(BF16)
        ds = ds_ref[...]
        gw_ref[...] = _dot_tn(s, ds.astype(BF16))
        gb_ref[...] = jnp.sum(da_ref[...], axis=0, keepdims=True)
        rowi = lax.broadcasted_iota(jnp.int32, ds.shape, 0)
        dmc = jnp.sum(jnp.where(rowi % 8 >= 4, ds, 0.0), axis=0, keepdims=True)
        dmc8 = jnp.broadcast_to(dmc, (8, ds.shape[1])).astype(BF16)
        cc_ref[...] = _dot_nt(dmc8, w_ref[...].astype(BF16))

    return pl.pallas_call(
        body, name="mod_bwd",
        out_shape=[jax.ShapeDtypeStruct(w_shard.shape, F32),
                   jax.ShapeDtypeStruct((1, dm_all.shape[1]), F32),
                   jax.ShapeDtypeStruct((8, D), F32)],
        compiler_params=pltpu.CompilerParams(vmem_limit_bytes=VMEM_LIMIT_BYTES),
    )(c_rows, dm_all, dm_shard, w_shard)


def _small_finish(cc_pre, c_ctx, dd_cols):
    def body(cc_ref, c_ref, dd_ref, gc_ref, gd_ref):
        gc_ref[...] = cc_ref[...] * _silu_grad(c_ref[...])
        gd_ref[...] = jnp.sum(dd_ref[...], axis=1, keepdims=True)

    return pl.pallas_call(
        body, name="small_finish",
        out_shape=[jax.ShapeDtypeStruct((1, D), F32), jax.ShapeDtypeStruct((NH, 1), F32)],
    )(cc_pre, c_ctx, dd_cols)


_BIG = (("w_in", (D, 10304), 1), ("w_br_ssd", (DI, D), 0), ("w_br_lru", (LW, D), 0), ("w_out", (D, D), 0),
        ("w_mlp1", (D, MLP), 1), ("w_mlp2", (MLP, D), 0))
_SMALL_SH = (("ssd_conv_w", (4, 4096)), ("lru_conv_w", (4, LW)), ("lru_ba", (2, LW)), ("lru_bi", (2, LW)),
             ("lru_lambda", (2, LW)))
_REPL = (("c_ctx", (D,)), ("b_gate", (2 * D,)), ("ssd_conv_b", (4096,)), ("ssd_dt_bias", (2, NH)),
         ("ssd_a_log", (2, NH)), ("ssd_d", (DI,)), ("ssd_norm_w", (DI,)), ("lru_conv_b", (LW,)),
         ("ln1_g", (D,)), ("ln1_b", (D,)),
         ("b_mlp1", (MLP,)), ("b_mlp2", (D,)), ("ln2_g", (D,)), ("ln2_b", (D,)))

_WEIGHT_NAMES = ('c_ctx', 'w_mod', 'b_mod', 'w_in', 'b_gate', 'ssd_conv_w', 'ssd_conv_b', 'ssd_dt_bias', 'ssd_a_log',
                 'ssd_d', 'ssd_norm_w', 'lru_conv_w', 'lru_conv_b', 'lru_wa', 'lru_ba', 'lru_wi', 'lru_bi',
                 'lru_lambda', 'w_br_ssd', 'w_br_lru', 'w_out', 'ln1_g', 'ln1_b', 'w_mlp1', 'b_mlp1', 'w_mlp2',
                 'b_mlp2', 'ln2_g', 'ln2_b')
_ARG_NAMES = ('x', 'c', 'ctx') + _WEIGHT_NAMES + ('loss_target',) + tuple('m_' + n for n in _WEIGHT_NAMES) + tuple(
    'v_' + n for n in _WEIGHT_NAMES)


def _to_slots(full, axis):
    n = full.shape[axis] // NDEV
    if axis == 0:
        return full.reshape(NDEV, n, full.shape[1])
    return full.reshape(full.shape[0], NDEV, n).transpose(1, 0, 2)


def _from_slots(slots, axis):
    if axis == 0:
        return slots.reshape(NDEV * slots.shape[1], slots.shape[2])
    return slots.transpose(1, 0, 2).reshape(slots.shape[1], NDEV * slots.shape[2])


def _pack_rows(arrs, width=1024, mult=8):
    flat = jnp.concatenate([a.reshape(-1) for a in arrs])
    n = flat.shape[0]
    per = width * mult
    tot = -(-n // per) * per
    return jnp.pad(flat, (0, tot - n)).reshape(tot // width, width)


def _unpack_rows(packed, shapes, lead=()):
    nl = len(lead)
    flat = packed.reshape(tuple(lead) + (-1,))
    out, off = [], 0
    for s in shapes:
        n = math.prod(s)
        out.append(flat[..., off:off + n].reshape(tuple(lead) + tuple(s)))
        off += n
    return out


def kernel(x, c, ctx, c_ctx, w_mod, b_mod, w_in, b_gate, ssd_conv_w, ssd_conv_b, ssd_dt_bias, ssd_a_log, ssd_d, ssd_norm_w, lru_conv_w, lru_conv_b, lru_wa, lru_ba, lru_wi, lru_bi, lru_lambda, w_br_ssd, w_br_lru, w_out, ln1_g, ln1_b, w_mlp1, b_mlp1, w_mlp2, b_mlp2, ln2_g, ln2_b, loss_target, m_c_ctx, m_w_mod, m_b_mod, m_w_in, m_b_gate, m_ssd_conv_w, m_ssd_conv_b, m_ssd_dt_bias, m_ssd_a_log, m_ssd_d, m_ssd_norm_w, m_lru_conv_w, m_lru_conv_b, m_lru_wa, m_lru_ba, m_lru_wi, m_lru_bi, m_lru_lambda, m_w_br_ssd, m_w_br_lru, m_w_out, m_ln1_g, m_ln1_b, m_w_mlp1, m_b_mlp1, m_w_mlp2, m_b_mlp2, m_ln2_g, m_ln2_b, v_c_ctx, v_w_mod, v_b_mod, v_w_in, v_b_gate, v_ssd_conv_w, v_ssd_conv_b, v_ssd_dt_bias, v_ssd_a_log, v_ssd_d, v_ssd_norm_w, v_lru_conv_w, v_lru_conv_b, v_lru_wa, v_lru_ba, v_lru_wi, v_lru_bi, v_lru_lambda, v_w_br_ssd, v_w_br_lru, v_w_out, v_ln1_g, v_ln1_b, v_w_mlp1, v_b_mlp1, v_w_mlp2, v_b_mlp2, v_ln2_g, v_ln2_b):
    A = dict(zip(_ARG_NAMES, (x, c, ctx, c_ctx, w_mod, b_mod, w_in, b_gate, ssd_conv_w, ssd_conv_b, ssd_dt_bias, ssd_a_log, ssd_d, ssd_norm_w, lru_conv_w, lru_conv_b, lru_wa, lru_ba, lru_wi, lru_bi, lru_lambda, w_br_ssd, w_br_lru, w_out, ln1_g, ln1_b, w_mlp1, b_mlp1, w_mlp2, b_mlp2, ln2_g, ln2_b, loss_target, m_c_ctx, m_w_mod, m_b_mod, m_w_in, m_b_gate, m_ssd_conv_w, m_ssd_conv_b, m_ssd_dt_bias, m_ssd_a_log, m_ssd_d, m_ssd_norm_w, m_lru_conv_w, m_lru_conv_b, m_lru_wa, m_lru_ba, m_lru_wi, m_lru_bi, m_lru_lambda, m_w_br_ssd, m_w_br_lru, m_w_out, m_ln1_g, m_ln1_b, m_w_mlp1, m_b_mlp1, m_w_mlp2, m_b_mlp2, m_ln2_g, m_ln2_b, v_c_ctx, v_w_mod, v_b_mod, v_w_in, v_b_gate, v_ssd_conv_w, v_ssd_conv_b, v_ssd_dt_bias, v_ssd_a_log, v_ssd_d, v_ssd_norm_w, v_lru_conv_w, v_lru_conv_b, v_lru_wa, v_lru_ba, v_lru_wi, v_lru_bi, v_lru_lambda, v_w_br_ssd, v_w_br_lru, v_w_out, v_ln1_g, v_ln1_b, v_w_mlp1, v_b_mlp1, v_w_mlp2, v_b_mlp2, v_ln2_g, v_ln2_b)))
    Bn, T, _ = x.shape
    Tc = ctx.shape[1]
    cfg = _Cfg(Bn, T, Tc)
    me = _me()
    L = {n: (A[n] if n == "c_ctx" else A[n][0]) for n in _WEIGHT_NAMES}
    nmod = L["w_mod"].shape[1]

    c_all = _exchange(c, "ag_c", True)
    c_rows = jnp.concatenate([c_all.reshape(NDEV * Bn, D), jnp.broadcast_to(c_ctx[None, :], (8, D))], axis=0)
    b_shard = lax.dynamic_slice(L["b_mod"], (me * nmod,), (nmod,)).reshape(1, nmod)
    m_part = _mod_fwd(c_rows, L["w_mod"], b_shard)
    m_all = _exchange(m_part, "ag_mod", True)
    m_full = m_all.transpose(1, 0, 2).reshape(NDEV * Bn + 8, NMOD * D)
    m_mine = lax.dynamic_slice(m_full, (me * Bn, 0), (Bn, NMOD * D))
    mc = m_full[NDEV * Bn, :2 * D]

    rest_payload = jnp.concatenate([L[n].astype(BF16).reshape(-1, 1024) for n, _, _ in _BIG[1:]], axis=0)
    small_payload = _pack_rows([L[n] for n, _ in _SMALL_SH], width=512)
    W = {}
    for n in ("ssd_conv_b", "lru_conv_b", "ssd_norm_w", "b_gate", "ln1_g", "ln1_b", "b_mlp1", "b_mlp2", "ln2_g", "ln2_b"):
        W[n] = L[n].reshape(1, -1)
    for n in ("ssd_dt_bias", "ssd_a_log", "ssd_d", "lru_wa", "lru_wi"):
        W[n] = L[n]

    loss_part, grad_x, g, dm, dmc, xres = _local_step(cfg, x, ctx, loss_target, m_mine, mc, W,
                                                      L["w_in"].astype(BF16), rest_payload, small_payload)
    loss = lax.psum(loss_part, ("x", "y", "c"))

    dmc_pad = jnp.pad(dmc, ((0, 4 - Bn), (0, (NMOD - 2) * D)))
    dm_payload = jnp.concatenate([jnp.pad(dm, ((0, 4 - Bn), (0, 0))), dmc_pad], axis=0)
    upd_w_in, dm_all = _sum_adamw(xres["rs_w_in"], L["w_in"], A["m_w_in"][0], A["v_w_in"][0], "sum_adamw_w_in",
                                  xch=(dm_payload, True))
    dm_all = dm_all.reshape(NDEV * 8, NMOD * D)
    c_rows_b = jnp.concatenate([jnp.pad(c_all, ((0, 0), (0, 4 - Bn), (0, 0))),
                                jnp.broadcast_to(c_ctx[None, None, :], (NDEV, 4, D))], axis=1).reshape(NDEV * 8, D)
    dm_shard = lax.dynamic_slice(dm_all, (0, me * nmod), (NDEV * 8, nmod))
    g_w_mod, g_b_mod, cc_part = _mod_bwd(c_rows_b, dm_all, dm_shard, L["w_mod"])
    g["c_ctx"] = cc_part[0]

    g["ssd_d"] = g.pop("ssd_d_cols")
    small_names = [n for n, _ in _REPL] + [n for n, _ in _SMALL_SH]
    small_full_shapes = [s for _, s in _REPL] + [s for _, s in _SMALL_SH]
    red_b, sm_all = _sum_slots(xres["rs_rest"], "sum_w_rest", xch=(_pack_rows([g[n] for n in small_names]), True))
    sm_sum = _sum_slots(sm_all, "sum_g_small")
    gs = dict(zip(small_names, _unpack_rows(sm_sum, small_full_shapes)))
    gcc, gdd = _small_finish(gs["c_ctx"].reshape(1, D), c_ctx.reshape(1, D), gs["ssd_d"].reshape(NH, HD))
    gs["c_ctx"] = gcc.reshape(D)
    gs["ssd_d"] = gdd.reshape(NH)
    for n, shp in _SMALL_SH:
        ns = shp[1] // NDEV
        gs[n] = lax.dynamic_slice(gs[n], (0, me * ns), (shp[0], ns))
    gs["b_mod"] = g_b_mod.reshape(NMOD * D)
    lru_sum = _sum_slots(xres["ag_lru"], "sum_g_lru").reshape(2, 2, LB, LBW, LBW)
    gs["lru_wa"], gs["lru_wi"] = lru_sum[0], lru_sum[1]

    gb = {}
    off = 0
    for n, shp, axis in _BIG[1:]:
        shard_shape = (shp[0] // NDEV, shp[1]) if axis == 0 else (shp[0], shp[1] // NDEV)
        r = math.prod(shard_shape) // 1024
        gb[n] = red_b[off:off + r].reshape(shard_shape)
        off += r
    gb["w_mod"] = g_w_mod

    grads, deltas, new_m, new_v = {}, {}, {}, {}
    big_names = ["w_mod"] + [n for n, _, _ in _BIG]
    grads["w_in"], deltas["w_in"], new_m["w_in"], new_v["w_in"] = upd_w_in
    for n in big_names:
        if n == "w_in":
            continue
        d_, nm_, nv_ = _adamw(L[n], gb[n], A["m_" + n][0], A["v_" + n][0], "adamw_" + n)
        grads[n], deltas[n], new_m[n], new_v[n] = gb[n], d_, nm_, nv_
    sm_names = [n for n in _WEIGHT_NAMES if n not in big_names]

    def two_d(a):
        return a.reshape(1, -1) if a.ndim == 1 else a
    loc = lambda pre: [two_d(A[pre + n] if n == "c_ctx" else A[pre + n][0]) for n in sm_names]
    gsm = [two_d(gs[n].reshape(L[n].shape)) for n in sm_names]
    ds_, nms_, nvs_ = _adamw_many(loc(""), gsm, loc("m_"), loc("v_"))
    for n, gv, dv, mv, vv in zip(sm_names, gsm, ds_, nms_, nvs_):
        shp = L[n].shape
        grads[n], deltas[n], new_m[n], new_v[n] = gv.reshape(shp), dv.reshape(shp), mv.reshape(shp), vv.reshape(shp)

    def out(dct):
        return [dct[n] if n == "c_ctx" else dct[n][None] for n in _WEIGHT_NAMES]
    return (loss, grad_x, *out(grads), *out(deltas), *out(new_m), *out(new_v))
```

```python
import functools
import math

import jax
import jax.numpy as jnp
from jax import lax
from jax.experimental import pallas as pl
from jax.experimental.pallas import tpu as pltpu

F32 = jnp.float32
BF16 = jnp.bfloat16

D = 1024
GRID_W = 64
DI = 2048
NH = 32
HD = 64
NG = 8
HPG = 4
NS = 128
CH = 128
LW = 1024
LB = 8
LBW = 128
LRU_C = 8.0
MLP = 4096
NMOD = 6
ALPHA = 2.0 ** 0.25
LN_EPS = 1e-6
RMS_EPS = 1e-5
PM = 10240
DTW = 128
CONVW = 5120
NDEV = 8

ADAM_LR = 0.001
ADAM_B1 = 0.9
ADAM_B2 = 0.999
ADAM_EPS = 1e-08
ADAM_WD = 0.01
ADAM_STEP = 10

VMEM_LIMIT_BYTES = 56 * 1024 * 1024


def _cp(n_axes):
    return pltpu.CompilerParams(dimension_semantics=("arbitrary",) * n_axes,
                                vmem_limit_bytes=VMEM_LIMIT_BYTES)


def _sigmoid(x):
    return 0.5 * jnp.tanh(0.5 * x) + 0.5


def _silu(x):
    return x * _sigmoid(x)


def _silu_grad(x):
    s = _sigmoid(x)
    return s * (1.0 + x * (1.0 - s))


def _log1p_pos(e):
    return jnp.where(e < 1e-2, e * (1.0 - e * (0.5 - e * (1.0 / 3.0))), jnp.log(1.0 + e))


def _softplus(x):
    return jnp.maximum(x, 0.0) + _log1p_pos(jnp.exp(-jnp.abs(x)))


_GELU_K = math.sqrt(2.0 / math.pi)


def _gelu(x):
    t = jnp.tanh(_GELU_K * (x + 0.044715 * x * x * x))
    return 0.5 * x * (1.0 + t)


def _gelu_and_grad(x):
    x2 = x * x
    t = jnp.tanh(_GELU_K * x * (1.0 + 0.044715 * x2))
    dt = (1.0 - t * t) * _GELU_K * (1.0 + 3.0 * 0.044715 * x2)
    h = 0.5 * (1.0 + t)
    return x * h, h + 0.5 * x * dt


def _ln(x):
    mu = jnp.mean(x, axis=-1, keepdims=True)
    xc = x - mu
    var = jnp.mean(xc * xc, axis=-1, keepdims=True)
    rs = lax.rsqrt(var + LN_EPS)
    return xc * rs, rs


def _ln_bwd(dy, xhat, rs):
    m1 = jnp.mean(dy, axis=-1, keepdims=True)
    m2 = jnp.mean(dy * xhat, axis=-1, keepdims=True)
    return rs * (dy - m1 - xhat * m2)


def _dot(a, b):
    return lax.dot_general(a, b, (((1,), (0,)), ((), ())), preferred_element_type=F32)


def _dot_nt(a, b):
    return lax.dot_general(a, b, (((1,), (1,)), ((), ())), preferred_element_type=F32)


def _dot_tn(a, b):
    return lax.dot_general(a, b, (((0,), (0,)), ((), ())), preferred_element_type=F32)


def _split3(a):
    a0 = a.astype(BF16)
    r = a - a0.astype(F32)
    a1 = r.astype(BF16)
    a2 = (r - a1.astype(F32)).astype(BF16)
    return a0, a1, a2


def _dot_exact_l(m_bf, a):
    a0, a1, a2 = _split3(a)
    return _dot(m_bf, a0) + _dot(m_bf, a1) + _dot(m_bf, a2)


def _dot_hilo_r(a, m_bf):
    a0 = a.astype(BF16)
    a1 = (a - a0.astype(F32)).astype(BF16)
    return _dot(a0, m_bf) + _dot(a1, m_bf)


def _tri(n, upper):
    ii = lax.broadcasted_iota(jnp.int32, (n, n), 0)
    kk = lax.broadcasted_iota(jnp.int32, (n, n), 1)
    m = (kk >= ii) if upper else (kk <= ii)
    return jnp.where(m, 1.0, 0.0).astype(BF16)


def _fit(n, t):
    t = min(t, n)
    while n % t:
        t //= 2
    return t


def _mm(a, b, mode, name, out_dtype=F32, tm=512, tn=512, tk=512, xch=None):
    if mode == "nn":
        M, K = a.shape
        N = b.shape[1]
    elif mode == "nt":
        M, K = a.shape
        N = b.shape[0]
    else:
        K, M = a.shape
        N = b.shape[1]
    tm, tn, tk = _fit(M, tm), _fit(N, tn), _fit(K, tk)
    assert M % tm == 0 and N % tn == 0 and K % tk == 0, (name, M, N, K, tm, tn, tk)
    nk = K // tk
    if mode == "tn":
        a_spec = pl.BlockSpec((tk, tm), lambda i, j, k: (k, i))
    else:
        a_spec = pl.BlockSpec((tm, tk), lambda i, j, k: (i, k))
    if mode == "nt":
        b_spec = pl.BlockSpec((tn, tk), lambda i, j, k: (j, k))
    else:
        b_spec = pl.BlockSpec((tk, tn), lambda i, j, k: (k, j))
    dn = {"nn": (((1,), (0,)), ((), ())), "nt": (((1,), (1,)), ((), ())), "tn": (((0,), (0,)), ((), ()))}[mode]

    def body(a_ref, b_ref, o_ref, acc_ref):
        k = pl.program_id(2)

        @pl.when(k == 0)
        def _():
            acc_ref[...] = jnp.zeros_like(acc_ref)

        acc_ref[...] += lax.dot_general(a_ref[...].astype(BF16), b_ref[...].astype(BF16), dn,
                                        preferred_element_type=F32)

        @pl.when(k == nk - 1)
        def _():
            o_ref[...] = acc_ref[...].astype(o_ref.dtype)

    def body_one_step(a_ref, b_ref, o_ref):
        o_ref[...] = lax.dot_general(a_ref[...].astype(BF16), b_ref[...].astype(BF16), dn,
                                     preferred_element_type=F32).astype(o_ref.dtype)

    res = _hosted_call(
        body if nk > 1 else body_one_step, xch, name=name, grid=(M // tm, N // tn, nk),
        in_specs=[a_spec, b_spec],
        out_specs=[pl.BlockSpec((tm, tn), lambda i, j, k: (i, j))],
        out_shape=[jax.ShapeDtypeStruct((M, N), out_dtype)],
        scratch_shapes=[pltpu.VMEM((tm, tn), F32)] if nk > 1 else [],
        compiler_params=_cp(3), args=(a, b))
    if xch is None:
        return res[0]
    return res[0][0], res[1]


def _mm_mlp1(h2, w1, b1, tm=1024, tn=1024):
    M, K = h2.shape
    N = w1.shape[1]
    tm, tn = _fit(M, tm), _fit(N, tn)

    def body(a_ref, b_ref, bias_ref, a1_ref, act_ref):
        v = _dot(a_ref[...], b_ref[...]) + bias_ref[...]
        a1_ref[...] = v
        r = jnp.maximum(v, 0.0)
        act_ref[...] = (r * r).astype(BF16)

    out = pl.BlockSpec((tm, tn), lambda i, j: (i, j))
    return pl.pallas_call(
        body, name="mm_mlp1", grid=(M // tm, N // tn),
        in_specs=[pl.BlockSpec((tm, K), lambda i, j: (i, 0)), pl.BlockSpec((K, tn), lambda i, j: (0, j)),
                  pl.BlockSpec((1, tn), lambda i, j: (0, j))],
        out_specs=[out, out],
        out_shape=[jax.ShapeDtypeStruct((M, N), F32), jax.ShapeDtypeStruct((M, N), BF16)],
        compiler_params=_cp(2),
    )(h2, w1, b1)


def _mm_dact(dmlp, w2, a1, tm=1024, tn=1024):
    M, K = dmlp.shape
    N = w2.shape[0]
    tm, tn = _fit(M, tm), _fit(N, tn)

    def body(d_ref, w_ref, a1_ref, o_ref, acc_ref):
        i = pl.program_id(1)

        @pl.when(i == 0)
        def _():
            acc_ref[...] = jnp.zeros_like(acc_ref)

        da = _dot_nt(d_ref[...], w_ref[...]) * (2.0 * jnp.maximum(a1_ref[...], 0.0))
        o_ref[...] = da.astype(BF16)
        acc_ref[0:1, :] += jnp.sum(da, axis=0, keepdims=True)

    blk = pl.BlockSpec((tm, tn), lambda j, i: (i, j))
    return pl.pallas_call(
        body, name="mm_dact", grid=(N // tn, M // tm),
        in_specs=[pl.BlockSpec((tm, K), lambda j, i: (i, 0)), pl.BlockSpec((tn, K), lambda j, i: (j, 0)), blk],
        out_specs=[blk, pl.BlockSpec((8, tn), lambda j, i: (0, j))],
        out_shape=[jax.ShapeDtypeStruct((M, N), BF16), jax.ShapeDtypeStruct((8, N), F32)],
        compiler_params=_cp(2),
    )(dmlp, w2, a1)


class _Cfg:
    def __init__(self, Bn, T, Tc):
        assert T % Tc == 0 and Tc % CH == 0 and Tc % GRID_W == 0
        self.Bn, self.T, self.Tc = Bn, T, Tc
        self.TT = T + Tc
        self.TB = Tc
        self.nbt = self.TT // self.TB
        self.nbl = T // self.TB
        self.NT = Bn * self.TT
        self.N = Bn * T
        self.nct = Tc // CH
        self.nlt = T // CH
        self.nch = self.nct + self.nlt


def _ln_mod_fwd(cfg, ctx2, x2, shift_tab, scale_tab, xch):
    TB, nbt, nbl = cfg.TB, cfg.nbt, cfg.nbl

    def body(c_ref, x_ref, sh_ref, sc_ref, o_ref):
        j = pl.program_id(1)
        xhat, _ = _ln(jnp.where(j == 0, c_ref[...], x_ref[...]))
        o_ref[...] = (xhat * (1.0 + sc_ref[...]) + sh_ref[...]).astype(BF16)

    tab = pl.BlockSpec((None, 1, D), lambda b, j: (2 * b + jnp.minimum(j, 1), 0, 0))
    outs, xres = _hosted_call(
        body, xch, name="ln_mod_fwd", grid=(cfg.Bn, nbt),
        in_specs=[pl.BlockSpec((TB, D), lambda b, j: (b, 0)),
                  pl.BlockSpec((TB, D), lambda b, j: (b * nbl + jnp.maximum(j - 1, 0), 0)), tab, tab],
        out_specs=[pl.BlockSpec((TB, D), lambda b, j: (b * nbt + j, 0))],
        out_shape=[jax.ShapeDtypeStruct((cfg.NT, D), BF16)],
        scratch_shapes=[], compiler_params=_cp(2), args=(ctx2, x2, shift_tab, scale_tab))
    return outs[0], xres


GP = 8
NGB = NG // GP
HPB = GP * HPG


def _heads_to_front(x, d, gb, inverse=False):
    off = d * NH + gb * HPB
    return pltpu.roll(x, off if inverse else (DTW - off) % DTW, 1)


def _chunks_per_step(cfg):
    n = cfg.NT // CH
    return max(c for c in (4, 3, 2, 1) if n % c == 0)


def _dt_fwd(cfg, dt_raw, dt_bias, a_log):
    cps = _chunks_per_step(cfg)

    def body(raw_ref, bias_ref, alog_ref, dt_ref, dtg_ref, cumg_ref, cumT_ref):
        a = -jnp.exp(alog_ref[...])
        col = lax.broadcasted_iota(jnp.int32, (CH, DTW), 1)
        for c in range(cps):
            rows = slice(c * CH, (c + 1) * CH)
            dt = _softplus(raw_ref[rows, :] + bias_ref[...])
            dta = dt * a
            cf = _dot_exact_l(_tri(CH, False), dta)
            cr = _dot_exact_l(_tri(CH, True), dta)
            cum = jnp.where(col < NH, cf, cr)
            dt_ref[rows, :] = dt
            for d in range(2):
                for gb in range(NGB):
                    dtg_ref[d, gb, rows, :] = _heads_to_front(dt, d, gb)
                    cg = _heads_to_front(cum, d, gb)
                    cumg_ref[d, gb, rows, :] = cg
                    cumT_ref[d, gb, c] = cg.T

    blk = pl.BlockSpec((cps * CH, DTW), lambda i: (i, 0))
    row = pl.BlockSpec((1, DTW), lambda i: (0, 0))
    gblk = pl.BlockSpec((2, NGB, cps * CH, DTW), lambda i: (0, 0, i, 0))
    return pl.pallas_call(
        body, name="dt_fwd", grid=(cfg.NT // (cps * CH),),
        in_specs=[blk, row, row],
        out_specs=[blk, gblk, gblk, pl.BlockSpec((2, NGB, cps, DTW, CH), lambda i: (0, 0, i, 0, 0))],
        out_shape=[jax.ShapeDtypeStruct((cfg.NT, DTW), F32),
                   jax.ShapeDtypeStruct((2, NGB, cfg.NT, DTW), F32),
                   jax.ShapeDtypeStruct((2, NGB, cfg.NT, DTW), F32),
                   jax.ShapeDtypeStruct((2, NGB, cfg.NT // CH, DTW, CH), F32)],
        compiler_params=_cp(1),
    )(dt_raw, dt_bias, a_log)


def _dt_bwd(cfg, dAs, dxxs, dt_raw, dt, dt_bias, a_log, h):
    def body(dAf_ref, dAr_ref, dxf_ref, dxr_ref, raw_ref, dt_ref, bias_ref, alog_ref, h_ref, o_ref, acc_ref, dw_ref):
        i = pl.program_id(0)

        @pl.when(i == 0)
        def _():
            acc_ref[...] = jnp.zeros_like(acc_ref)
            dw_ref[...] = jnp.zeros_like(dw_ref)

        a = -jnp.exp(alog_ref[...])
        col = lax.broadcasted_iota(jnp.int32, (CH, DTW), 1)
        for c in range(cps):
            rows = slice(c * CH, (c + 1) * CH)
            dA_v = jnp.zeros((CH, DTW), F32)
            dxx_v = jnp.zeros((CH, DTW), F32)
            for d, (ra, rx) in enumerate(((dAf_ref, dxf_ref), (dAr_ref, dxr_ref))):
                for gb in range(NGB):
                    dA_v = dA_v + _heads_to_front(ra[gb, rows, :], d, gb, inverse=True)
                    dxx_v = dxx_v + _heads_to_front(rx[gb, rows, :], d, gb, inverse=True)
            ddta = jnp.where(col < NH, _dot_exact_l(_tri(CH, True), dA_v), _dot_exact_l(_tri(CH, False), dA_v))
            dtv = dt_ref[rows, :]
            ddt = ddta * a + dxx_v
            draw = ddt * _sigmoid(raw_ref[rows, :] + bias_ref[...])
            draw = jnp.where(col < 2 * NH, draw, 0.0)
            o_ref[rows, :] = draw
            dw_ref[...] += _dot_tn(h_ref[rows, :], draw.astype(BF16))
            da = jnp.sum(ddta * dtv, axis=0, keepdims=True) * a
            da = jnp.where(col[:1] < 2 * NH, da, 0.0)
            acc_ref[0:1, :] += da
            acc_ref[1:2, :] += jnp.sum(draw, axis=0, keepdims=True)

    cps = _chunks_per_step(cfg)
    blk = pl.BlockSpec((cps * CH, DTW), lambda i: (i, 0))
    row = pl.BlockSpec((1, DTW), lambda i: (0, 0))
    gblk = pl.BlockSpec((NGB, cps * CH, DTW), lambda i: (0, i, 0))
    return pl.pallas_call(
        body, name="dt_bwd", grid=(cfg.NT // (cps * CH),),
        in_specs=[gblk, gblk, gblk, gblk, blk, blk, row, row, pl.BlockSpec((cps * CH, D), lambda i: (i, 0))],
        out_specs=[blk, pl.BlockSpec((8, DTW), lambda i: (0, 0)), pl.BlockSpec((D, DTW), lambda i: (0, 0))],
        out_shape=[jax.ShapeDtypeStruct((cfg.NT, DTW), F32), jax.ShapeDtypeStruct((8, DTW), F32),
                   jax.ShapeDtypeStruct((D, DTW), F32)],
        compiler_params=_cp(1),
    )(dAs[0], dAs[1], dxxs[0], dxxs[1], dt_raw, dt, dt_bias, a_log, h)


_TAPS = (2, 1, 0, -1)


def _conv_fwd(cfg, proj, conv_w, conv_b):
    TB, nbt = cfg.TB, cfg.nbt
    SUB = 256
    n_act = DI + 2 * NG * NS

    def body(u_ref, w_ref, b_ref, o_ref, sg_ref):
        i = pl.program_id(0)
        R = jnp.where(i % nbt == 0, cfg.Tc, GRID_W)
        t = lax.broadcasted_iota(jnp.int32, (TB, SUB), 0)
        pos = jnp.bitwise_and(t, R - 1)
        keep = {s: jnp.where(jnp.logical_and(pos - s >= 0, pos - s < R), 1.0, 0.0) for s in (2, 1, -1)}
        for q in range(CONVW // SUB):
            sl = slice(q * SUB, (q + 1) * SUB)
            u = u_ref[:, sl]
            pre = b_ref[:, sl] + w_ref[2:3, sl] * u
            for k in (0, 1, 3):
                pre = pre + w_ref[k:k + 1, sl] * (pltpu.roll(u, _TAPS[k] % TB, 0) * keep[_TAPS[k]])
            if q * SUB < n_act:
                s = _sigmoid(pre)
                o_ref[:, sl] = pre * s
                sg_ref[:, sl] = s * (1.0 + pre * (1.0 - s))
            else:
                o_ref[:, sl] = pre

    blk = pl.BlockSpec((TB, CONVW), lambda i: (i, 0))
    return pl.pallas_call(
        body, name="conv_fwd", grid=(cfg.NT // TB,),
        in_specs=[blk, pl.BlockSpec((4, CONVW), lambda i: (0, 0)), pl.BlockSpec((1, CONVW), lambda i: (0, 0))],
        out_specs=[blk, blk],
        out_shape=[jax.ShapeDtypeStruct((cfg.NT, CONVW), F32)] * 2,
        compiler_params=_cp(1),
    )(proj, conv_w, conv_b)


_ANY = pl.BlockSpec(memory_space=pl.ANY)


def _conv_bwd(cfg, name, dproj, proj, conv_w, sgrad, addends, col0, width, skip=None, xch=None):
    TB, nbt, nbl = cfg.TB, cfg.nbt, cfg.nbl
    CB = 1024
    SUB = 256
    c0 = col0 // CB
    addends = list(addends) + ([] if sgrad is None else [sgrad])
    n_add = len(addends)

    def body(*refs):
        u_ref, w_ref = refs[1:3]
        add_refs = refs[3:3 + n_add]
        rest = refs[3 + n_add:]
        if sgrad is not None:
            add_refs, sg_ref = add_refs[:-1], add_refs[-1]
        if skip is not None:
            dy_ref, dv_ref = rest[:2]
            rest = rest[2:]
        o_ref, acc_ref = rest
        i = pl.program_id(1)

        @pl.when(i == 0)
        def _():
            acc_ref[...] = jnp.zeros_like(acc_ref)

        isctx = (i % nbt) == 0
        R = jnp.where(isctx, cfg.Tc, GRID_W)
        t = lax.broadcasted_iota(jnp.int32, (TB, SUB), 0)
        pos = jnp.bitwise_and(t, R - 1)
        keep = {s: jnp.where(jnp.logical_and(pos - s >= 0, pos - s < R), 1.0, 0.0) for s in (2, 1, -1, -2)}

        def shifted(v, s):
            return v if s == 0 else pltpu.roll(v, s % TB, 0) * keep[s]

        for q in range(CB // SUB):
            sl = slice(q * SUB, (q + 1) * SUB)
            u = u_ref[:, sl]
            us = [shifted(u, _TAPS[k]) for k in range(4)]
            g = add_refs[0][:, sl]
            for r in add_refs[1:]:
                g = g + r[:, sl]
            if skip is not None:
                g = g + jnp.where(isctx, 0.0, dv_ref[:, sl] * dy_ref[:, sl])
            if sgrad is not None:
                g = g * sg_ref[:, sl]
            dp = jnp.zeros_like(g)
            for k in range(4):
                acc_ref[k:k + 1, sl] += jnp.sum(g * us[k], axis=0, keepdims=True)
                dp = dp + w_ref[k:k + 1, sl] * shifted(g, -_TAPS[k])
            acc_ref[4:5, sl] += jnp.sum(g, axis=0, keepdims=True)
            o_ref[:, sl] = dp.astype(BF16)

    blk = pl.BlockSpec((TB, CB), lambda j, i: (i, j))
    wide = pl.BlockSpec((TB, CB), lambda j, i: (i, c0 + j))
    in_specs = [_ANY, wide, pl.BlockSpec((4, CB), lambda j, i: (0, c0 + j))]
    in_specs += [blk] * (n_add if sgrad is None else n_add - 1) + ([] if sgrad is None else [wide])
    args = [dproj, proj, conv_w] + addends
    if skip is not None:
        def lat(j, i):
            b = i // nbt
            return (b * nbl + jnp.maximum(i % nbt - 1, 0), j)
        in_specs += [pl.BlockSpec((TB, CB), lat), pl.BlockSpec((1, CB), lambda j, i: (0, j))]
        args += list(skip)
    return _hosted_call(
        body, xch, name=name, grid=(width // CB, cfg.NT // TB),
        in_specs=in_specs,
        out_specs=[pl.BlockSpec((TB, CB), lambda j, i: (i, c0 + j)), pl.BlockSpec((8, CB), lambda j, i: (0, j))],
        out_shape=[jax.ShapeDtypeStruct((cfg.NT, PM), BF16), jax.ShapeDtypeStruct((8, width), F32)],
        scratch_shapes=[], compiler_params=_cp(2), args=args, aliases={0: 0})


def _chunk_of_step(cfg, rev):
    nct, nlt = cfg.nct, cfg.nlt
    if not rev:
        return lambda s: s
    return lambda s: jnp.where(s < nct, nct - 1 - s, 2 * nct + nlt - 1 - s)


def _expand4(v, band, base):
    out = v[:, base + 3:base + 4]
    for h in (2, 1, 0):
        out = jnp.where(band == h, v[:, base + h:base + h + 1], out)
    return out


def _ssd_step_tiles(dt_ref, cum_ref, cumT_ref, rev):
    cum_t = cum_ref[...]
    last = 0 if rev else CH - 1
    llast = cum_t[last:last + 1, :]
    return (dt_ref[...], cum_t, cumT_ref[...], llast, jnp.exp(llast), last)


def _ssd_common(gi, x_ref, b_ref, c_ref, tiles, rev, intra=True):
    dt_t, cum_t, cumT_t, llast, elast, last = tiles
    base = gi * HPG
    xh = x_ref[:, gi * HPG * HD:(gi + 1) * HPG * HD]
    Bm = b_ref[:, gi * NS:(gi + 1) * NS].astype(BF16)
    band = lax.broadcasted_iota(jnp.int32, (CH, HPG * HD), 1) // HD
    cbs = [jnp.broadcast_to(cum_t[:, base + h:base + h + 1], (CH, CH)) for h in range(HPG)]
    Cm = G = decs = None
    if intra:
        Cm = c_ref[:, gi * NS:(gi + 1) * NS].astype(BF16)
        G = _dot_nt(Cm, Bm)
        ii = lax.broadcasted_iota(jnp.int32, (CH, CH), 0)
        jj = lax.broadcasted_iota(jnp.int32, (CH, CH), 1)
        mask = (jj >= ii) if rev else (jj <= ii)
        decs = [jnp.exp(jnp.where(mask, cbs[h] - cumT_t[base + h:base + h + 1, :], -1e30)) for h in range(HPG)]
    cum_exp = jnp.concatenate([cbs[3], cbs[3]], axis=1)
    ll_exp = llast[:, base + 3:base + 4]
    for h in (2, 1, 0):
        cum_exp = jnp.where(band == h, jnp.concatenate([cbs[h], cbs[h]], axis=1), cum_exp)
        ll_exp = jnp.where(band[:1] == h, llast[:, base + h:base + h + 1], ll_exp)
    ecum = jnp.exp(cum_exp) if intra else None
    e_exp = jnp.exp(ll_exp - cum_exp)
    dt_exp = _expand4(dt_t, band, base)
    X = xh * dt_exp
    rb = lax.broadcasted_iota(jnp.int32, (HPG * HD, NS), 0) // HD
    dec_rows = elast[:, base + 3:base + 4]
    for h in (2, 1, 0):
        dec_rows = jnp.where(rb == h, elast[:, base + h:base + h + 1], dec_rows)
    return xh, Bm, Cm, band, e_exp, ecum, dt_exp, X, G, decs, elast, dec_rows, last


def _ssd_specs(cfg, rev):
    nch = cfg.nch
    cmap = _chunk_of_step(cfg, rev)
    d = 1 if rev else 0

    def make(stepmap):
        def row(b, g, sp):
            return b * nch + cmap(stepmap(sp))
        bo, co = DI // (GP * NS), (DI + NG * NS) // (GP * NS)
        return [
            pl.BlockSpec((CH, GP * HPG * HD), lambda b, g, sp: (row(b, g, sp), g)),
            pl.BlockSpec((CH, GP * NS), lambda b, g, sp: (row(b, g, sp), bo + g)),
            pl.BlockSpec((CH, GP * NS), lambda b, g, sp: (row(b, g, sp), co + g)),
            pl.BlockSpec((None, None, CH, DTW), lambda b, g, sp: (d, g, row(b, g, sp), 0)),
            pl.BlockSpec((None, None, CH, DTW), lambda b, g, sp: (d, g, row(b, g, sp), 0)),
            pl.BlockSpec((None, None, None, DTW, CH), lambda b, g, sp: (d, g, row(b, g, sp), 0, 0)),
        ], row
    return make


def _ssd_fwd(cfg, act, dtg, cumg, cumTg, rev, y_other=None, dvec=None):
    nch = cfg.nch
    in_specs, row = _ssd_specs(cfg, rev)(lambda sp: sp)
    total = y_other is not None

    def body(*refs):
        x_ref, b_ref, c_ref, dt_ref, cum_ref, cumT_ref = refs[:6]
        if total:
            yo_ref, dv_ref = refs[6:8]
        y_ref, hs_ref, h_scr = refs[-3:]
        s = pl.program_id(2)

        @pl.when(s == 0)
        def _():
            h_scr[...] = jnp.zeros_like(h_scr)

        def step(intra):
            tiles = _ssd_step_tiles(dt_ref, cum_ref, cumT_ref, rev)
            for gi in range(GP):
                xh, Bm, Cm, band, e_exp, ecum, dt_exp, X, G, decs, elast, dec_rows, last = _ssd_common(
                    gi, x_ref, b_ref, c_ref, tiles, rev, intra)
                H = h_scr[gi]
                if intra:
                    Mcat = jnp.concatenate([(G * decs[h]).astype(BF16) for h in range(HPG)], axis=1)
                    Xbd = jnp.concatenate([jnp.where(band == h, X, 0.0).astype(BF16) for h in range(HPG)], axis=0)
                    xsl = slice(gi * HPG * HD, (gi + 1) * HPG * HD)
                    Y = ecum * _dot_nt(Cm, H.astype(BF16)) + _dot(Mcat, Xbd)
                    if total:
                        Y = Y + yo_ref[:, xsl] + dv_ref[:, xsl] * xh
                    y_ref[:, xsl] = Y
                hs_ref[gi] = H
                S = _dot_tn((e_exp * X).astype(BF16), Bm)
                h_scr[gi] = dec_rows * H + S

        isctx = cmap(s) < cfg.nct

        @pl.when(isctx)
        def _():
            step(False)

        @pl.when(jnp.logical_not(isctx))
        def _():
            step(True)

    cmap = _chunk_of_step(cfg, rev)
    yblk = pl.BlockSpec((CH, GP * HPG * HD), lambda b, g, s: (row(b, g, s), g))
    args = [act, act, act, dtg, cumg, cumTg]
    if total:
        in_specs = in_specs + [yblk, pl.BlockSpec((1, GP * HPG * HD), lambda b, g, s: (0, g))]
        args += [y_other, dvec]
    return pl.pallas_call(
        body, name="ssd_fwd_rev" if rev else "ssd_fwd", grid=(cfg.Bn, NG // GP, nch),
        in_specs=in_specs,
        out_specs=[yblk, pl.BlockSpec((None, GP, None, HPG * HD, NS), lambda b, g, s: (b, g, s, 0, 0))],
        out_shape=[jax.ShapeDtypeStruct((cfg.NT, DI), F32),
                   jax.ShapeDtypeStruct((cfg.Bn, NG, nch, HPG * HD, NS), F32)],
        scratch_shapes=[pltpu.VMEM((GP, HPG * HD, NS), F32)],
        compiler_params=_cp(3),
    )(*args)


def _ssd_bwd(cfg, act, dtg, cumg, cumTg, hs, dy, rev, xch=None):
    nch, nct, nlt = cfg.nch, cfg.nct, cfg.nlt
    cmap = _chunk_of_step(cfg, rev)
    in_specs, row = _ssd_specs(cfg, rev)(lambda sp: nch - 1 - sp)

    def lat_row(b, g, sp):
        c = cmap(nch - 1 - sp)
        return b * nlt + jnp.maximum(c - nct, 0)

    def body(x_ref, b_ref, c_ref, dt_ref, cum_ref, cumT_ref, dy_ref, hs_ref,
             dxh_ref, dB_ref, dC_ref, dA_ref, dxx_ref, dh_scr):
        sp = pl.program_id(2)

        @pl.when(sp == 0)
        def _():
            dh_scr[...] = jnp.zeros_like(dh_scr)

        def step(intra):
            tiles = _ssd_step_tiles(dt_ref, cum_ref, cumT_ref, rev)
            dA_t = jnp.zeros((CH, DTW), F32)
            dAT_t = jnp.zeros((DTW, CH), F32)
            dxx_t = jnp.zeros((CH, DTW), F32)
            for gi in range(GP):
                dA_g, dAT_g, dxx_g = group_bwd(gi, intra, tiles, x_ref, b_ref, c_ref, dy_ref, hs_ref,
                                               dxh_ref, dB_ref, dC_ref, dh_scr)
                dA_t = dA_t + dA_g
                dxx_t = dxx_t + dxx_g
                if intra:
                    dAT_t = dAT_t + dAT_g
            dA_ref[...] = dA_t - dAT_t.T if intra else dA_t
            dxx_ref[...] = dxx_t

        isctx = cmap(nch - 1 - sp) < nct

        @pl.when(isctx)
        def _():
            step(False)

        @pl.when(jnp.logical_not(isctx))
        def _():
            step(True)

    def group_bwd(gi, intra, tiles, x_ref, b_ref, c_ref, dy_ref, hs_ref, dxh_ref, dB_ref, dC_ref, dh_scr):
        xsl = slice(gi * HPG * HD, (gi + 1) * HPG * HD)
        nsl = slice(gi * NS, (gi + 1) * NS)
        base = gi * HPG
        xh, Bm, Cm, band, e_exp, ecum, dt_exp, X, G, decs, elast, dec_rows, last = _ssd_common(
            gi, x_ref, b_ref, c_ref, tiles, rev, intra)
        H = hs_ref[gi]
        dHn = dh_scr[gi]
        dHnb = dHn.astype(BF16)
        BdH = _dot_nt(Bm, dHnb)
        dX = e_exp * BdH
        eX = e_exp * X
        lanei = lax.broadcasted_iota(jnp.int32, (CH, DTW), 1)
        dA = jnp.zeros((CH, DTW), F32)
        dAT = None
        pb = lax.broadcasted_iota(jnp.int32, (HPG * HD, NS), 0) // HD
        pl_ = lax.broadcasted_iota(jnp.int32, (HPG * HD, NS), 1)
        E = jnp.where(pb + base == pl_, 1.0, 0.0).astype(BF16)
        if intra:
            dY = dy_ref[:, xsl]
            Hb = H.astype(BF16)
            dYs = ecum * dY
            dYsb = dYs.astype(BF16)
            Ys = ecum * _dot_nt(Cm, Hb)
            dG = jnp.zeros((CH, CH), F32)
            subi = lax.broadcasted_iota(jnp.int32, (DTW, CH), 0)
            dAT = jnp.zeros((DTW, CH), F32)
            Xbd = jnp.concatenate([jnp.where(band == h, X, 0.0).astype(BF16) for h in range(HPG)], axis=0)
            dYbd = jnp.concatenate([jnp.where(band == h, dY, 0.0).astype(BF16) for h in range(HPG)], axis=0)
            dMcat = _dot_nt(dY.astype(BF16), Xbd)
            Ms = []
            for h in range(HPG):
                M = G * decs[h]
                dM = dMcat[:, h * CH:(h + 1) * CH]
                W = dM * M
                dG = dG + dM * decs[h]
                Ms.append(M.astype(BF16))
                dA = dA + jnp.where(lanei == base + h, jnp.sum(W, axis=1, keepdims=True), 0.0)
                dAT = dAT + jnp.where(subi == base + h, jnp.sum(W, axis=0, keepdims=True), 0.0)
            dX = dX + _dot_tn(jnp.concatenate(Ms, axis=0), dYbd)
            dGb = dG.astype(BF16)
            dC_ref[:, nsl] = _dot(dGb, Bm) + _dot(dYsb, Hb)
            dB_ref[:, nsl] = _dot_tn(dGb, Cm) + _dot(eX.astype(BF16), dHnb)
            dh_scr[gi] = dec_rows * dHn + _dot_tn(dYsb, Cm)
            dA = dA + _dot_hilo_r(dY * Ys, E)
        else:
            dC_ref[:, nsl] = jnp.zeros((CH, NS), F32)
            dB_ref[:, nsl] = _dot(eX.astype(BF16), dHnb)
            dh_scr[gi] = dec_rows * dHn
        q = _dot_hilo_r(eX * BdH, E)
        r = jnp.sum(dHn * H, axis=1, keepdims=True)
        lane1 = lax.broadcasted_iota(jnp.int32, (1, DTW), 1)
        hdot = jnp.zeros((1, DTW), F32)
        for h in range(HPG):
            hv = jnp.sum(r[h * HD:(h + 1) * HD, :], axis=0, keepdims=True)
            hdot = hdot + jnp.where(lane1 == base + h, hv, 0.0)
        dllast = jnp.sum(q, axis=0, keepdims=True) + elast * hdot
        rowi = lax.broadcasted_iota(jnp.int32, (CH, DTW), 0)
        dxh_ref[:, xsl] = dX * dt_exp
        return dA - q + jnp.where(rowi == last, dllast, 0.0), dAT, _dot_hilo_r(dX * xh, E)

    small = pl.BlockSpec((None, CH, DTW), lambda b, g, sp: (g, row(b, g, sp), 0))
    return _hosted_call(
        body, xch, name="ssd_bwd_rev" if rev else "ssd_bwd", grid=(cfg.Bn, NG // GP, nch),
        in_specs=in_specs + [
            pl.BlockSpec((CH, GP * HPG * HD), lambda b, g, sp: (lat_row(b, g, sp), g)),
            pl.BlockSpec((None, GP, None, HPG * HD, NS), lambda b, g, sp: (b, g, nch - 1 - sp, 0, 0))],
        out_specs=[pl.BlockSpec((CH, GP * HPG * HD), lambda b, g, sp: (row(b, g, sp), g)),
                   pl.BlockSpec((CH, GP * NS), lambda b, g, sp: (row(b, g, sp), g)),
                   pl.BlockSpec((CH, GP * NS), lambda b, g, sp: (row(b, g, sp), g)),
                   small, small],
        out_shape=[jax.ShapeDtypeStruct((cfg.NT, DI), F32),
                   jax.ShapeDtypeStruct((cfg.NT, NG * NS), F32),
                   jax.ShapeDtypeStruct((cfg.NT, NG * NS), F32),
                   jax.ShapeDtypeStruct((NGB, cfg.NT, DTW), F32),
                   jax.ShapeDtypeStruct((NGB, cfg.NT, DTW), F32)],
        scratch_shapes=[pltpu.VMEM((GP, HPG * HD, NS), F32)],
        compiler_params=_cp(3), args=(act, act, act, dtg, cumg, cumTg, dy, hs))


def _shift_rows(v, s, fill, toward_later, rowi):
    n = v.shape[0]
    if toward_later:
        return jnp.where(rowi >= s, pltpu.roll(v, s, 0), fill)
    return jnp.where(rowi < n - s, pltpu.roll(v, n - s, 0), fill)


def _chunk_scan(a, b, carry, later):
    nt = a.shape[0] // 8
    rowi = lax.broadcasted_iota(jnp.int32, (8, a.shape[1]), 0)
    outs = [None] * nt
    for r in (range(nt) if later else range(nt - 1, -1, -1)):
        av = a[r * 8:(r + 1) * 8]
        bv = b[r * 8:(r + 1) * 8]
        for sh in (1, 2, 4):
            a_p = _shift_rows(av, sh, 1.0, later, rowi)
            b_p = _shift_rows(bv, sh, 0.0, later, rowi)
            bv = av * b_p + bv
            av = av * a_p
        h = bv + av * carry
        outs[r] = h
        carry = h[7:8] if later else h[0:1]
    return jnp.concatenate(outs, axis=0), carry


def _lru_gates(u, wa_ref, wi_ref, ba_ref, bi_ref, lam_ref):
    rs, is_ = [], []
    for k in range(LB):
        uk = u[:, k * LBW:(k + 1) * LBW].astype(BF16)
        rs.append(_dot(uk, wa_ref[k].astype(BF16)))
        is_.append(_dot(uk, wi_ref[k].astype(BF16)))
    r = 1.0 / (1.0 + jnp.exp(-(jnp.concatenate(rs, axis=1) + ba_ref[...])))
    ig = _sigmoid(jnp.concatenate(is_, axis=1) + bi_ref[...])
    sp = _softplus(-lam_ref[...])
    la = -LRU_C * r * sp
    a = jnp.exp(la)
    q = (1.0 + a * a) * jnp.tanh(-la)
    return r, ig, sp, la, a, jnp.sqrt(q), lax.rsqrt(q)


def _lru_w_specs(d):
    return [pl.BlockSpec((None, LB, LBW, LBW), lambda b, s: (d, 0, 0, 0)),
            pl.BlockSpec((None, LB, LBW, LBW), lambda b, s: (d, 0, 0, 0)),
            pl.BlockSpec((None, 1, LW), lambda b, s: (d, 0, 0)),
            pl.BlockSpec((None, 1, LW), lambda b, s: (d, 0, 0)),
            pl.BlockSpec((None, 1, LW), lambda b, s: (d, 0, 0))]


def _lru_block_of_step(cfg, rev):
    nbl = cfg.nbl
    if not rev:
        return lambda s: s
    return lambda s: jnp.where(s < 1, 0, 1 + nbl - s)


def _lru_fwd(cfg, act, wa, wi, ba, bi, lam, rev):
    nch, CH = cfg.nbt, cfg.TB
    cmap = _lru_block_of_step(cfg, rev)
    d = 1 if rev else 0
    ucol = (DI + 2 * NG * NS) // LW

    def body(u_ref, wa_ref, wi_ref, ba_ref, bi_ref, lam_ref, h_ref, c_scr):
        s = pl.program_id(1)

        @pl.when(s == 0)
        def _():
            c_scr[...] = jnp.zeros_like(c_scr)

        u = u_ref[...]
        r, ig, sp, la, a, g, _ = _lru_gates(u, wa_ref, wi_ref, ba_ref, bi_ref, lam_ref)
        h, carry = _chunk_scan(a, g * ig * u, c_scr[0:1, :], not rev)
        h_ref[...] = h
        c_scr[0:1, :] = carry

    return pl.pallas_call(
        body, name="lru_fwd_rev" if rev else "lru_fwd", grid=(cfg.Bn, nch),
        in_specs=[pl.BlockSpec((CH, LW), lambda b, s: (b * nch + cmap(s), ucol))] + _lru_w_specs(d),
        out_specs=pl.BlockSpec((CH, LW), lambda b, s: (b * nch + cmap(s), 0)),
        out_shape=jax.ShapeDtypeStruct((cfg.NT, LW), F32),
        scratch_shapes=[pltpu.VMEM((8, LW), F32)],
        compiler_params=_cp(2),
    )(act, wa, wi, ba, bi, lam)


def _lru_bwd(cfg, act, wa, wi, ba, bi, lam, hd, dyl, rev):
    nch, nct, nlt, CH = cfg.nbt, 1, cfg.nbl, cfg.TB
    cmap = _lru_block_of_step(cfg, rev)
    d = 1 if rev else 0
    ucol = (DI + 2 * NG * NS) // LW

    def srow(b, sp):
        return b * nch + cmap(nch - 1 - sp)

    def prev_rows(b, sp):
        s = nch - 1 - sp
        cp = cmap(jnp.maximum(s - 1, 0))
        base = (b * nch + cp) * (CH // 8)
        return base + (0 if rev else CH // 8 - 1)

    def lat_row(b, sp):
        c = cmap(nch - 1 - sp)
        return b * nlt + jnp.maximum(c - nct, 0)

    def body(u_ref, wa_ref, wi_ref, ba_ref, bi_ref, lam_ref, h_ref, hp_ref, dy_ref,
             du_ref, dwa_ref, dwi_ref, vec_ref, c_scr):
        b = pl.program_id(0)
        sp_id = pl.program_id(1)
        s = nch - 1 - sp_id

        @pl.when(sp_id == 0)
        def _():
            c_scr[...] = jnp.zeros_like(c_scr)

        @pl.when(jnp.logical_and(b == 0, sp_id == 0))
        def _():
            dwa_ref[...] = jnp.zeros_like(dwa_ref)
            dwi_ref[...] = jnp.zeros_like(dwi_ref)
            vec_ref[...] = jnp.zeros_like(vec_ref)

        c = cmap(s)
        u = u_ref[...]
        r, ig, spl, la, a, g, ginv = _lru_gates(u, wa_ref, wi_ref, ba_ref, bi_ref, lam_ref)
        dh = jnp.where(c < nct, 0.0, dy_ref[...])
        rowi = lax.broadcasted_iota(jnp.int32, (CH, LW), 0)
        lamv, _ = _chunk_scan(_shift_rows(a, 1, 1.0, rev, rowi), dh, c_scr[0:1, :], rev)
        first = CH - 1 if rev else 0
        c_scr[0:1, :] = (a * lamv)[first:first + 1, :]
        hprow = hp_ref[...][(0 if rev else 7):(1 if rev else 8), :]
        hprow = jnp.where(s > 0, hprow, 0.0)
        h_prev = _shift_rows(h_ref[...], 1, hprow, not rev, rowi)
        da = lamv * h_prev
        db = lamv
        iu = ig * u
        dla = da * a - db * iu * (a * a) * ginv
        dr = dla * (-LRU_C * spl)
        di = db * g * u
        du = db * g * ig
        drp = dr * r * (1.0 - r)
        dip = di * ig * (1.0 - ig)
        dus = []
        for k in range(LB):
            sl = slice(k * LBW, (k + 1) * LBW)
            drk = drp[:, sl].astype(BF16)
            dik = dip[:, sl].astype(BF16)
            uk = u[:, sl].astype(BF16)
            dus.append(_dot_nt(drk, wa_ref[k].astype(BF16)) + _dot_nt(dik, wi_ref[k].astype(BF16)))
            dwa_ref[k] += _dot_tn(uk, drk)
            dwi_ref[k] += _dot_tn(uk, dik)
        du_ref[...] = du + jnp.concatenate(dus, axis=1)
        vec_ref[0:1, :] += jnp.sum(drp, axis=0, keepdims=True)
        vec_ref[1:2, :] += jnp.sum(dip, axis=0, keepdims=True)
        dsp = jnp.sum(dla * (-LRU_C * r), axis=0, keepdims=True)
        vec_ref[2:3, :] += dsp * (-_sigmoid(-lam_ref[...]))

    return pl.pallas_call(
        body, name="lru_bwd_rev" if rev else "lru_bwd", grid=(cfg.Bn, nch),
        in_specs=[pl.BlockSpec((CH, LW), lambda b, sp: (srow(b, sp), ucol))] + _lru_w_specs(d) + [
            pl.BlockSpec((CH, LW), lambda b, sp: (srow(b, sp), 0)),
            pl.BlockSpec((8, LW), lambda b, sp: (prev_rows(b, sp), 0)),
            pl.BlockSpec((CH, LW), lambda b, sp: (lat_row(b, sp), 0))],
        out_specs=[pl.BlockSpec((CH, LW), lambda b, sp: (srow(b, sp), 0)),
                   pl.BlockSpec((LB, LBW, LBW), lambda b, sp: (0, 0, 0)),
                   pl.BlockSpec((LB, LBW, LBW), lambda b, sp: (0, 0, 0)),
                   pl.BlockSpec((8, LW), lambda b, sp: (0, 0))],
        out_shape=[jax.ShapeDtypeStruct((cfg.NT, LW), F32),
                   jax.ShapeDtypeStruct((LB, LBW, LBW), F32),
                   jax.ShapeDtypeStruct((LB, LBW, LBW), F32),
                   jax.ShapeDtypeStruct((8, LW), F32)],
        scratch_shapes=[pltpu.VMEM((8, LW), F32)],
        compiler_params=_cp(2),
    )(act, wa, wi, ba, bi, lam, hd, hd, dyl)


HB = 1024


def _post_ssd_fwd(cfg, y, proj, norm_w):
    TB, nbt, nbl = cfg.TB, cfg.nbt, cfg.nbl
    zc = CONVW // HB

    def body(y_ref, z_ref, w_ref, o_ref):
        u = y_ref[...] * _silu(z_ref[...])
        for gi in range(HB // (DI // NG)):
            sl = slice(gi * 256, (gi + 1) * 256)
            ug = u[:, sl]
            rs = lax.rsqrt(jnp.mean(ug * ug, axis=1, keepdims=True) + RMS_EPS)
            o_ref[:, sl] = (ug * rs * w_ref[:, sl]).astype(BF16)

    def st(b, j, cb):
        return (b * nbt + 1 + j, cb)
    return pl.pallas_call(
        body, name="post_ssd_fwd", grid=(cfg.Bn, nbl, DI // HB),
        in_specs=[pl.BlockSpec((TB, HB), st),
                  pl.BlockSpec((TB, HB), lambda b, j, cb: (b * nbt + 1 + j, zc + cb)),
                  pl.BlockSpec((1, HB), lambda b, j, cb: (0, cb))],
        out_specs=pl.BlockSpec((TB, HB), lambda b, j, cb: (b * nbl + j, cb)),
        out_shape=jax.ShapeDtypeStruct((cfg.N, DI), BF16),
        compiler_params=_cp(3),
    )(y, proj, norm_w)


def _post_ssd_bwd(cfg, dproj, dn, y, act, proj, norm_w):
    TB, nbt, nbl = cfg.TB, cfg.nbt, cfg.nbl
    zc = CONVW // HB

    def body(_, dn_ref, y_ref, xh_ref, z_ref, w_ref, dy_ref, dz_ref, acc_ref):
        b = pl.program_id(1)
        j = pl.program_id(2)

        @pl.when(jnp.logical_and(b == 0, j == 0))
        def _():
            acc_ref[...] = jnp.zeros_like(acc_ref)

        @pl.when(j == 0)
        def _():
            dz_ref[...] = jnp.zeros_like(dz_ref)

        @pl.when(j > 0)
        def _():
            latent(dn_ref, y_ref, xh_ref, z_ref, w_ref, dy_ref, dz_ref, acc_ref)

    def latent(dn_ref, y_ref, xh_ref, z_ref, w_ref, dy_ref, dz_ref, acc_ref):
        xh = xh_ref[...]
        z = z_ref[...]
        y = y_ref[...]
        sg = _sigmoid(z)
        sz = z * sg
        dsz = sg * (1.0 + z * (1.0 - sg))
        u = y * sz
        dout = dn_ref[...]
        for gi in range(HB // (DI // NG)):
            sl = slice(gi * 256, (gi + 1) * 256)
            ug0 = u[:, sl]
            rs = lax.rsqrt(jnp.mean(ug0 * ug0, axis=1, keepdims=True) + RMS_EPS)
            ug = ug0 * rs
            do = dout[:, sl]
            acc_ref[0:1, sl] += jnp.sum(do * ug, axis=0, keepdims=True)
            dug = do * w_ref[:, sl]
            du = rs * (dug - ug * jnp.mean(dug * ug, axis=1, keepdims=True))
            dy = du * sz[:, sl]
            dy_ref[:, sl] = dy
            dz_ref[:, sl] = (du * y[:, sl] * dsz[:, sl]).astype(BF16)
            acc_ref[1:2, sl] += jnp.sum(dy * xh[:, sl], axis=0, keepdims=True)

    def st(cb, b, j):
        return (b * nbt + j, cb)

    def la(cb, b, j):
        return (b * nbl + jnp.maximum(j - 1, 0), cb)
    return pl.pallas_call(
        body, name="post_ssd_bwd", grid=(DI // HB, cfg.Bn, nbt),
        in_specs=[_ANY, pl.BlockSpec((TB, HB), la), pl.BlockSpec((TB, HB), st), pl.BlockSpec((TB, HB), st),
                  pl.BlockSpec((TB, HB), lambda cb, b, j: (b * nbt + j, zc + cb)),
                  pl.BlockSpec((1, HB), lambda cb, b, j: (0, cb))],
        out_specs=[pl.BlockSpec((TB, HB), la),
                   pl.BlockSpec((TB, HB), lambda cb, b, j: (b * nbt + j, zc + cb)),
                   pl.BlockSpec((8, HB), lambda cb, b, j: (0, cb))],
        out_shape=[jax.ShapeDtypeStruct((cfg.N, DI), F32), jax.ShapeDtypeStruct((cfg.NT, PM), BF16),
                   jax.ShapeDtypeStruct((8, DI), F32)],
        input_output_aliases={0: 1},
        compiler_params=_cp(3),
    )(dproj, dn, y, act, proj, norm_w)


def _post_lru_fwd(cfg, hf, hb, proj):
    TB, nbt, nbl = cfg.TB, cfg.nbt, cfg.nbl
    gc = (CONVW + DI) // HB

    def body(hf_ref, hb_ref, g_ref, o_ref):
        o_ref[...] = ((hf_ref[...] + hb_ref[...]) * _gelu(g_ref[...])).astype(BF16)

    st = pl.BlockSpec((TB, HB), lambda b, j: (b * nbt + 1 + j, 0))
    return pl.pallas_call(
        body, name="post_lru_fwd", grid=(cfg.Bn, nbl),
        in_specs=[st, st, pl.BlockSpec((TB, HB), lambda b, j: (b * nbt + 1 + j, gc))],
        out_specs=pl.BlockSpec((TB, HB), lambda b, j: (b * nbl + j, 0)),
        out_shape=jax.ShapeDtypeStruct((cfg.N, LW), BF16),
        compiler_params=_cp(2),
    )(hf, hb, proj)


def _post_lru_bwd(cfg, dproj, dv, hf, hb, proj):
    TB, nbt, nbl = cfg.TB, cfg.nbt, cfg.nbl
    gc = (CONVW + DI) // HB

    def body(_, dv_ref, hf_ref, hb_ref, g_ref, dy_ref, dg_ref):
        j = pl.program_id(1)

        @pl.when(j == 0)
        def _():
            dg_ref[...] = jnp.zeros_like(dg_ref)

        @pl.when(j > 0)
        def _():
            gt = g_ref[...]
            dvv = dv_ref[...]
            gl, dgl = _gelu_and_grad(gt)
            dy_ref[...] = dvv * gl
            dg_ref[...] = (dvv * (hf_ref[...] + hb_ref[...]) * dgl).astype(BF16)

    st = pl.BlockSpec((TB, HB), lambda b, j: (b * nbt + j, 0))
    la = pl.BlockSpec((TB, HB), lambda b, j: (b * nbl + jnp.maximum(j - 1, 0), 0))
    gcol = pl.BlockSpec((TB, HB), lambda b, j: (b * nbt + j, gc))
    return pl.pallas_call(
        body, name="post_lru_bwd", grid=(cfg.Bn, nbt),
        in_specs=[_ANY, la, st, st, gcol],
        out_specs=[la, gcol],
        out_shape=[jax.ShapeDtypeStruct((cfg.N, LW), F32), jax.ShapeDtypeStruct((cfg.NT, PM), BF16)],
        input_output_aliases={0: 1},
        compiler_params=_cp(2),
    )(dproj, dv, hf, hb, proj)


def _merge_fwd(cfg, proj, b_gate, br_ssd, br_lru):
    TB, nbt, nbl = cfg.TB, cfg.nbt, cfg.nbl
    mc = (CONVW + DI + LW) // HB

    def body(ms_ref, ml_ref, bg_ref, bs_ref, bl_ref, o_ref):
        gs = _sigmoid(ms_ref[...] + bg_ref[:, :D])
        gl = _sigmoid(ml_ref[...] + bg_ref[:, D:])
        o_ref[...] = (gs * bs_ref[...] + gl * bl_ref[...]).astype(BF16)

    la = pl.BlockSpec((TB, D), lambda b, j: (b * nbl + j, 0))
    return pl.pallas_call(
        body, name="merge_fwd", grid=(cfg.Bn, nbl),
        in_specs=[pl.BlockSpec((TB, HB), lambda b, j: (b * nbt + 1 + j, mc)),
                  pl.BlockSpec((TB, HB), lambda b, j: (b * nbt + 1 + j, mc + 1)),
                  pl.BlockSpec((1, 2 * D), lambda b, j: (0, 0)), la, la],
        out_specs=la,
        out_shape=jax.ShapeDtypeStruct((cfg.N, D), BF16),
        compiler_params=_cp(2),
    )(proj, proj, b_gate, br_ssd, br_lru)


def _merge_bwd(cfg, dmix, proj, b_gate, br_ssd, br_lru):
    TB, nbt, nbl = cfg.TB, cfg.nbt, cfg.nbl
    mc = (CONVW + DI + LW) // HB

    def body(dm_ref, ms_ref, ml_ref, bg_ref, bs_ref, bl_ref, ds_ref, dl_ref, dmg_ref, acc_ref):
        b = pl.program_id(0)
        j = pl.program_id(1)

        @pl.when(jnp.logical_and(b == 0, j == 0))
        def _():
            acc_ref[...] = jnp.zeros_like(acc_ref)

        @pl.when(j == 0)
        def _():
            dmg_ref[...] = jnp.zeros_like(dmg_ref)

        @pl.when(j > 0)
        def _():
            latent(dm_ref, ms_ref, ml_ref, bg_ref, bs_ref, bl_ref, ds_ref, dl_ref, dmg_ref, acc_ref)

    def latent(dm_ref, ms_ref, ml_ref, bg_ref, bs_ref, bl_ref, ds_ref, dl_ref, dmg_ref, acc_ref):
        dm = dm_ref[...]
        gs = _sigmoid(ms_ref[...] + bg_ref[:, :D])
        gl = _sigmoid(ml_ref[...] + bg_ref[:, D:])
        ds_ref[...] = (dm * gs).astype(BF16)
        dl_ref[...] = (dm * gl).astype(BF16)
        dps = dm * bs_ref[...] * gs * (1.0 - gs)
        dpl = dm * bl_ref[...] * gl * (1.0 - gl)
        dmg_ref[:, :D] = dps.astype(BF16)
        dmg_ref[:, D:] = dpl.astype(BF16)
        acc_ref[0:1, :D] += jnp.sum(dps, axis=0, keepdims=True)
        acc_ref[0:1, D:] += jnp.sum(dpl, axis=0, keepdims=True)

    la = pl.BlockSpec((TB, D), lambda b, j: (b * nbl + jnp.maximum(j - 1, 0), 0))
    return pl.pallas_call(
        body, name="merge_bwd", grid=(cfg.Bn, nbt),
        in_specs=[la, pl.BlockSpec((TB, HB), lambda b, j: (b * nbt + j, mc)),
                  pl.BlockSpec((TB, HB), lambda b, j: (b * nbt + j, mc + 1)),
                  pl.BlockSpec((1, 2 * D), lambda b, j: (0, 0)), la, la],
        out_specs=[la, la, pl.BlockSpec((TB, 2 * D), lambda b, j: (b * nbt + j, mc // 2)),
                   pl.BlockSpec((8, 2 * D), lambda b, j: (0, 0))],
        out_shape=[jax.ShapeDtypeStruct((cfg.N, D), BF16), jax.ShapeDtypeStruct((cfg.N, D), BF16),
                   jax.ShapeDtypeStruct((cfg.NT, PM), BF16), jax.ShapeDtypeStruct((8, 2 * D), F32)],
        compiler_params=_cp(2),
    )(dmix, proj, proj, b_gate, br_ssd, br_lru)


def _resid1_fwd(cfg, x2, x_mix, gate1, shift2, scale2, ln1_g, ln1_b):
    TB, nbt, nbl = cfg.TB, cfg.nbt, cfg.nbl

    def body(x_ref, xm_ref, g1_ref, sh_ref, sc_ref, lg_ref, lb_ref, x1_ref, h2_ref):
        r1 = ALPHA * x_ref[...] + g1_ref[...] * xm_ref[...]
        xh, _ = _ln(r1)
        x1 = xh * lg_ref[...] + lb_ref[...]
        x1_ref[...] = x1
        xh2, _ = _ln(x1)
        h2_ref[...] = (xh2 * (1.0 + sc_ref[...]) + sh_ref[...]).astype(BF16)

    la = pl.BlockSpec((TB, D), lambda b, j: (b * nbl + j, 0))
    ex = pl.BlockSpec((None, 1, D), lambda b, j: (b, 0, 0))
    vec = pl.BlockSpec((1, D), lambda b, j: (0, 0))
    return pl.pallas_call(
        body, name="resid1_fwd", grid=(cfg.Bn, nbl),
        in_specs=[la, la, ex, ex, ex, vec, vec],
        out_specs=[la, la],
        out_shape=[jax.ShapeDtypeStruct((cfg.N, D), F32), jax.ShapeDtypeStruct((cfg.N, D), BF16)],
        compiler_params=_cp(2),
    )(x2, x_mix, gate1, shift2, scale2, ln1_g, ln1_b)


def _resid1_bwd(cfg, dh2, dx1p, x1, x2, x_mix, gate1, scale2, ln1_g):
    TB, nbt, nbl = cfg.TB, cfg.nbt, cfg.nbl

    def body(dh2_ref, dx1p_ref, x1_ref, x_ref, xm_ref, g1_ref, sc_ref, lg_ref,
             dxm_ref, dxp_ref, ex_ref, gl_ref):
        b = pl.program_id(0)
        j = pl.program_id(1)

        @pl.when(j == 0)
        def _():
            ex_ref[...] = jnp.zeros_like(ex_ref)

        @pl.when(jnp.logical_and(b == 0, j == 0))
        def _():
            gl_ref[...] = jnp.zeros_like(gl_ref)

        dh2 = dh2_ref[...]
        xh2, rs2 = _ln(x1_ref[...])
        ex_ref[0:1, :] += jnp.sum(dh2, axis=0, keepdims=True)
        ex_ref[1:2, :] += jnp.sum(dh2 * xh2, axis=0, keepdims=True)
        dx1 = dx1p_ref[...] + _ln_bwd(dh2 * (1.0 + sc_ref[...]), xh2, rs2)
        xm = xm_ref[...]
        g1 = g1_ref[...]
        r1 = ALPHA * x_ref[...] + g1 * xm
        xh1, rs1 = _ln(r1)
        gl_ref[0:1, :] += jnp.sum(dx1 * xh1, axis=0, keepdims=True)
        gl_ref[1:2, :] += jnp.sum(dx1, axis=0, keepdims=True)
        dr1 = _ln_bwd(dx1 * lg_ref[...], xh1, rs1)
        ex_ref[2:3, :] += jnp.sum(dr1 * xm, axis=0, keepdims=True)
        dxm_ref[...] = (dr1 * g1).astype(BF16)
        dxp_ref[...] = ALPHA * dr1

    la = pl.BlockSpec((TB, D), lambda b, j: (b * nbl + j, 0))
    ex = pl.BlockSpec((None, 1, D), lambda b, j: (b, 0, 0))
    vec = pl.BlockSpec((1, D), lambda b, j: (0, 0))
    return pl.pallas_call(
        body, name="resid1_bwd", grid=(cfg.Bn, nbl),
        in_specs=[la, la, la, la, la, ex, ex, vec],
        out_specs=[la, la, pl.BlockSpec((None, 8, D), lambda b, j: (b, 0, 0)),
                   pl.BlockSpec((8, D), lambda b, j: (0, 0))],
        out_shape=[jax.ShapeDtypeStruct((cfg.N, D), BF16), jax.ShapeDtypeStruct((cfg.N, D), F32),
                   jax.ShapeDtypeStruct((cfg.Bn, 8, D), F32), jax.ShapeDtypeStruct((8, D), F32)],
        compiler_params=_cp(2),
    )(dh2, dx1p, x1, x2, x_mix, gate1, scale2, ln1_g)


def _final_fwd_bwd(cfg, x1, mlp, b2, gate2, ln2_g, ln2_b, target):
    TB, nbl = cfg.TB, cfg.nbl

    def body(x1_ref, m_ref, b2_ref, g2_ref, lg_ref, lb_ref, t_ref, dm_ref, dx_ref, ex_ref, gl_ref):
        b = pl.program_id(0)
        j = pl.program_id(1)

        @pl.when(j == 0)
        def _():
            ex_ref[...] = jnp.zeros_like(ex_ref)

        @pl.when(jnp.logical_and(b == 0, j == 0))
        def _():
            gl_ref[...] = jnp.zeros_like(gl_ref)

        mv = m_ref[...] + b2_ref[...]
        g2 = g2_ref[...]
        r2 = ALPHA * x1_ref[...] + g2 * mv
        xh, rs = _ln(r2)
        lg = lg_ref[...]
        x2 = xh * lg + lb_ref[...]
        err = x2 - t_ref[...]
        ls = jnp.sum(jnp.sum(err * err, axis=1, keepdims=True), axis=0, keepdims=True) * (0.5 / D)
        gl_ref[3:4, :] += ls
        dx2 = err * (1.0 / D)
        gl_ref[0:1, :] += jnp.sum(dx2 * xh, axis=0, keepdims=True)
        gl_ref[1:2, :] += jnp.sum(dx2, axis=0, keepdims=True)
        dr2 = _ln_bwd(dx2 * lg, xh, rs)
        ex_ref[0:1, :] += jnp.sum(dr2 * mv, axis=0, keepdims=True)
        dmv = dr2 * g2
        gl_ref[2:3, :] += jnp.sum(dmv, axis=0, keepdims=True)
        dm_ref[...] = dmv.astype(BF16)
        dx_ref[...] = ALPHA * dr2

    la = pl.BlockSpec((TB, D), lambda b, j: (b * nbl + j, 0))
    ex = pl.BlockSpec((None, 1, D), lambda b, j: (b, 0, 0))
    vec = pl.BlockSpec((1, D), lambda b, j: (0, 0))
    return pl.pallas_call(
        body, name="final_fwd_bwd", grid=(cfg.Bn, nbl),
        in_specs=[la, la, vec, ex, vec, vec, la],
        out_specs=[la, la, pl.BlockSpec((None, 8, D), lambda b, j: (b, 0, 0)),
                   pl.BlockSpec((8, D), lambda b, j: (0, 0))],
        out_shape=[jax.ShapeDtypeStruct((cfg.N, D), BF16), jax.ShapeDtypeStruct((cfg.N, D), F32),
                   jax.ShapeDtypeStruct((cfg.Bn, 8, D), F32), jax.ShapeDtypeStruct((8, D), F32)],
        compiler_params=_cp(2),
    )(x1, mlp, b2, gate2, ln2_g, ln2_b, target)


def _ln_mod_bwd(cfg, dh_a, ddt_raw, w_dt, ctx2, x2, scale_tab, dxp):
    TB, nbt, nbl = cfg.TB, cfg.nbt, cfg.nbl

    def body(da_ref, dd_ref, wd_ref, c_ref, x_ref, sc_ref, dxp_ref, gx_ref, acc_ref):
        j = pl.program_id(1)

        @pl.when(j <= 1)
        def _():
            acc_ref[...] = jnp.zeros_like(acc_ref)

        dh = da_ref[...] + _dot_nt(dd_ref[...].astype(BF16), wd_ref[...])
        xhat, rs = _ln(jnp.where(j == 0, c_ref[...], x_ref[...]))
        acc_ref[0:1, :] += jnp.sum(dh, axis=0, keepdims=True)
        acc_ref[1:2, :] += jnp.sum(dh * xhat, axis=0, keepdims=True)
        gx_ref[...] = dxp_ref[...] + _ln_bwd(dh * (1.0 + sc_ref[...]), xhat, rs)

    st = pl.BlockSpec((TB, D), lambda b, j: (b * nbt + j, 0))
    la = pl.BlockSpec((TB, D), lambda b, j: (b * nbl + jnp.maximum(j - 1, 0), 0))
    return pl.pallas_call(
        body, name="ln_mod_bwd", grid=(cfg.Bn, nbt),
        in_specs=[st, pl.BlockSpec((TB, DTW), lambda b, j: (b * nbt + j, 0)),
                  pl.BlockSpec((D, DTW), lambda b, j: (0, 0)),
                  pl.BlockSpec((TB, D), lambda b, j: (b, 0)), la,
                  pl.BlockSpec((None, 1, D), lambda b, j: (2 * b + jnp.minimum(j, 1), 0, 0)), la],
        out_specs=[la, pl.BlockSpec((None, 8, D), lambda b, j: (2 * b + jnp.minimum(j, 1), 0, 0))],
        out_shape=[jax.ShapeDtypeStruct((cfg.N, D), F32), jax.ShapeDtypeStruct((2 * cfg.Bn, 8, D), F32)],
        compiler_params=_cp(2),
    )(dh_a, ddt_raw, w_dt, ctx2, x2, scale_tab, dxp)


def _perm_w_in(w_in):
    w_main = jnp.concatenate([w_in[:, 0:3072], w_in[:, 4160:5184], w_in[:, 3136:4160], w_in[:, 5184:10304]], axis=1)
    w_dt = jnp.pad(w_in[:, 3072:3136], ((0, 0), (0, DTW - 2 * NH)))
    return w_main, w_dt


def _unperm_w_in(dw_main, dw_dt):
    return jnp.concatenate([dw_main[:, 0:3072], dw_dt[:, :2 * NH], dw_main[:, 4096:5120],
                            dw_main[:, 3072:4096], dw_main[:, 5120:]], axis=1)


def _unpack_rest(rest_all):
    out, off = {}, 0
    for n, shp, axis in _BIG[1:]:
        shard_shape = (shp[0] // NDEV, shp[1]) if axis == 0 else (shp[0], shp[1] // NDEV)
        r = math.prod(shard_shape) // 1024
        out[n] = _from_slots(rest_all[:, off:off + r, :].reshape((NDEV,) + shard_shape), axis)
        off += r
    return out


def _local_step(cfg, x, ctx, target, m, mc, W, w_in_shard, rest_payload):
    Bn, T, Tc = cfg.Bn, cfg.T, cfg.Tc
    NT, N = cfg.NT, cfg.N
    ctx2, x2 = ctx.reshape(Bn * Tc, D), x.reshape(N, D)
    mch = [m[:, i * D:(i + 1) * D] for i in range(NMOD)]
    ctx_sh = jnp.broadcast_to(mc[None, :D], (Bn, D))
    ctx_sc = jnp.broadcast_to(mc[None, D:], (Bn, D))
    shift_tab = jnp.stack([ctx_sh, mch[0]], axis=1).reshape(2 * Bn, 1, D)
    scale_tab = jnp.stack([ctx_sc, mch[1]], axis=1).reshape(2 * Bn, 1, D)
    gate1 = mch[2].reshape(Bn, 1, D)
    shift2 = mch[3].reshape(Bn, 1, D)
    scale2 = mch[4].reshape(Bn, 1, D)
    gate2 = mch[5].reshape(Bn, 1, D)

    conv_w = jnp.concatenate([W["ssd_conv_w"], W["lru_conv_w"]], axis=1)
    conv_b = jnp.concatenate([W["ssd_conv_b"], W["lru_conv_b"]], axis=1)
    dt_bias = jnp.pad(W["ssd_dt_bias"].reshape(1, 2 * NH), ((0, 0), (0, DTW - 2 * NH)))
    a_log = jnp.pad(W["ssd_a_log"].reshape(1, 2 * NH), ((0, 0), (0, DTW - 2 * NH)))
    dvec = jnp.repeat(W["ssd_d"].reshape(NH), HD).reshape(1, DI)
    lba = W["lru_ba"].reshape(2, 1, LW)
    lbi = W["lru_bi"].reshape(2, 1, LW)
    llam = W["lru_lambda"].reshape(2, 1, LW)

    h, w_in_all = _ln_mod_fwd(cfg, ctx2, x2, shift_tab, scale_tab, xch=(w_in_shard, TWO_LEVEL))
    w_main, w_dt = _perm_w_in(_from_slots(w_in_all, 1))
    proj, rest_all = _mm(h, w_main, "nn", "mm_proj", tm=1024, tn=2048, tk=1024, xch=(rest_payload, True))
    W = dict(W, w_main=w_main, w_dt=w_dt, **_unpack_rest(rest_all))
    dt_raw = _mm(h, W["w_dt"], "nn", "mm_dt", tm=512, tn=DTW, tk=1024)
    dt, dtg, cumg, cumTg = _dt_fwd(cfg, dt_raw, dt_bias, a_log)
    act, sgrad = _conv_fwd(cfg, proj, conv_w, conv_b)
    y_f, hs_f = _ssd_fwd(cfg, act, dtg, cumg, cumTg, False)
    y, hs_b = _ssd_fwd(cfg, act, dtg, cumg, cumTg, True, y_other=y_f, dvec=dvec)
    hss = [hs_f, hs_b]
    hls = [_lru_fwd(cfg, act, W["lru_wa"], W["lru_wi"], lba, lbi, llam, rev) for rev in (False, True)]
    nssd = _post_ssd_fwd(cfg, y, proj, W["ssd_norm_w"])
    vlru = _post_lru_fwd(cfg, hls[0], hls[1], proj)
    br_ssd = _mm(nssd, W["w_br_ssd"], "nn", "mm_br_ssd", tm=1024, tn=1024, tk=1024)
    br_lru = _mm(vlru, W["w_br_lru"], "nn", "mm_br_lru", tm=1024, tn=1024, tk=1024)
    mix = _merge_fwd(cfg, proj, W["b_gate"], br_ssd, br_lru)
    x_mix = _mm(mix, W["w_out"], "nn", "mm_out", tm=1024, tn=1024, tk=1024)
    x1, h2 = _resid1_fwd(cfg, x2, x_mix, gate1, shift2, scale2, W["ln1_g"], W["ln1_b"])
    a1, actm = _mm_mlp1(h2, W["w_mlp1"], W["b_mlp1"])
    mlp = _mm(actm, W["w_mlp2"], "nn", "mm_mlp2", tm=1024, tn=1024, tk=2048)
    dmlp, dx1p, ex2, gl2 = _final_fwd_bwd(cfg, x1, mlp, W["b_mlp2"], gate2, W["ln2_g"], W["ln2_b"],
                                          target.reshape(N, D))

    g = {}
    g["ln2_g"], g["ln2_b"], g["b_mlp2"] = gl2[0:1], gl2[1:2], gl2[2:3]
    loss_partial = gl2[3, 0]
    gw = {}
    gw["w_mlp2"] = _mm(actm, dmlp, "tn", "mm_dw_mlp2", BF16, tm=1024, tn=1024, tk=1024)
    da1, accb1 = _mm_dact(dmlp, W["w_mlp2"], a1)
    g["b_mlp1"] = accb1[0:1]
    dh2 = _mm(da1, W["w_mlp1"], "nt", "mm_dh2", tm=1024, tn=1024, tk=2048)
    gw["w_mlp1"] = _mm(h2, da1, "tn", "mm_dw_mlp1", BF16, tm=1024, tn=1024, tk=1024)
    dx_mix, dxp, ex1, gl1 = _resid1_bwd(cfg, dh2, dx1p, x1, x2, x_mix, gate1, scale2, W["ln1_g"])
    g["ln1_g"], g["ln1_b"] = gl1[0:1], gl1[1:2]
    dmix = _mm(dx_mix, W["w_out"], "nt", "mm_dmix", tm=1024, tn=1024, tk=1024)
    gw["w_out"] = _mm(mix, dx_mix, "tn", "mm_dw_out", BF16, tm=1024, tn=1024, tk=1024)
    dbs, dbl, dproj, accg = _merge_bwd(cfg, dmix, proj, W["b_gate"], br_ssd, br_lru)
    g["b_gate"] = accg[0:1]
    dnssd = _mm(dbs, W["w_br_ssd"], "nt", "mm_dnssd", tm=1024, tn=1024, tk=1024)
    gw["w_br_ssd"] = _mm(nssd, dbs, "tn", "mm_dw_br_ssd", BF16, tm=1024, tn=1024, tk=1024)
    dvlru = _mm(dbl, W["w_br_lru"], "nt", "mm_dvlru", tm=1024, tn=1024, tk=1024)
    gw["w_br_lru"] = _mm(vlru, dbl, "tn", "mm_dw_br_lru", BF16, tm=1024, tn=1024, tk=1024)
    dy, dproj, accs = _post_ssd_bwd(cfg, dproj, dnssd, y, act, proj, W["ssd_norm_w"])
    g["ssd_norm_w"] = accs[0:1]
    dD_cols = accs[1:2]
    dyl, dproj = _post_lru_bwd(cfg, dproj, dvlru, hls[0], hls[1], proj)

    rest_slots = jnp.concatenate([_to_slots(gw[n], axis).reshape(NDEV, -1, 1024) for n, _, axis in _BIG[1:]], axis=1)
    xres = {}
    dxh, dBs, dCs, dAs, dxxs, dus = [], [], [], [], [], []
    dwas, dwis, lvecs = [], [], []
    for i, rev in enumerate((False, True)):
        if i == 0:
            o, xres["rs_rest"] = _ssd_bwd(cfg, act, dtg, cumg, cumTg, hss[i], dy, rev, xch=(rest_slots, False))
        else:
            o = _ssd_bwd(cfg, act, dtg, cumg, cumTg, hss[i], dy, rev)
        dxh.append(o[0]); dBs.append(o[1]); dCs.append(o[2]); dAs.append(o[3]); dxxs.append(o[4])
        du, dwa, dwi, lv = _lru_bwd(cfg, act, W["lru_wa"], W["lru_wi"], lba, lbi, llam, hls[i], dyl, rev)
        dus.append(du); dwas.append(dwa); dwis.append(dwi); lvecs.append(lv)
    lru_payload = jnp.stack([jnp.stack(dwas), jnp.stack(dwis)]).reshape(-1, 1024)
    g["lru_ba"] = jnp.stack([lvecs[0][0], lvecs[1][0]])
    g["lru_bi"] = jnp.stack([lvecs[0][1], lvecs[1][1]])
    g["lru_lambda"] = jnp.stack([lvecs[0][2], lvecs[1][2]])

    ddt_raw, accdt, dw_dt = _dt_bwd(cfg, dAs, dxxs, dt_raw, dt, dt_bias, a_log, h)
    g["ssd_a_log"] = accdt[0, :2 * NH].reshape(2, NH)
    g["ssd_dt_bias"] = accdt[1, :2 * NH].reshape(2, NH)

    (dproj, accx), xres["ag_lru"] = _conv_bwd(cfg, "conv_bwd_x", dproj, proj, conv_w, sgrad, [dxh[0], dxh[1]], 0, DI,
                                              skip=(dy, dvec), xch=(lru_payload, True))
    dproj, accB = _conv_bwd(cfg, "conv_bwd_b", dproj, proj, conv_w, sgrad, [dBs[0], dBs[1]], DI, NG * NS)
    dproj, accC = _conv_bwd(cfg, "conv_bwd_c", dproj, proj, conv_w, sgrad, [dCs[0], dCs[1]], DI + NG * NS, NG * NS)
    dproj, accl = _conv_bwd(cfg, "conv_bwd_lru", dproj, proj, conv_w, None, [dus[0], dus[1]], DI + 2 * NG * NS, LW)
    accssd = jnp.concatenate([accx, accB, accC], axis=1)
    g["ssd_conv_w"], g["ssd_conv_b"] = accssd[0:4], accssd[4:5]
    g["lru_conv_w"], g["lru_conv_b"] = accl[0:4], accl[4:5]
    dw_main = _mm(h, dproj, "tn", "mm_dw_main", BF16, tm=1024, tn=2048, tk=1024)
    w_in_slots = _to_slots(_unperm_w_in(dw_main, dw_dt.astype(BF16)), 1)
    dh_a, xres["rs_w_in"] = _mm(dproj, W["w_main"], "nt", "mm_dh_main", tm=1024, tn=1024, tk=2048,
                                xch=(w_in_slots, False))
    grad_x, acct = _ln_mod_bwd(cfg, dh_a, ddt_raw, W["w_dt"], ctx2, x2, scale_tab, dxp)
    acct = acct.reshape(Bn, 2, 8, D)
    dm = jnp.concatenate([acct[:, 1, 0], acct[:, 1, 1], ex1[:, 2], ex1[:, 0], ex1[:, 1], ex2[:, 0]], axis=1)
    dmc = jnp.concatenate([acct[:, 0, 0], acct[:, 0, 1]], axis=1)
    g["ssd_d_cols"] = dD_cols
    return loss_partial, grad_x.reshape(Bn, T, D), g, dm, dmc, xres


MESH = pl.DeviceIdType.MESH
_HBM = pl.BlockSpec(memory_space=pltpu.HBM)


def _me():
    return 4 * lax.axis_index("x") + 2 * lax.axis_index("y") + lax.axis_index("c")


def _peer(k):
    px = (lax.axis_index("x") + ((k >> 2) & 1)) % 2
    py = (lax.axis_index("y") + ((k >> 1) & 1)) % 2
    pc = (lax.axis_index("c") + (k & 1)) % 2
    return (px, py, pc), 4 * px + 2 * py + pc


def _xchg_copies(x_ref, o_ref, send_sems, recv_sems, loc_sem, gather):
    me = _me()
    src_me = x_ref if gather else x_ref.at[me]
    loc = pltpu.make_async_copy(src_me, o_ref.at[me], loc_sem)
    sends, recvs = [], []
    for k in range(1, NDEV):
        peer, pid = _peer(k)
        sends.append(pltpu.make_async_remote_copy(
            src_ref=x_ref if gather else x_ref.at[pid], dst_ref=o_ref.at[me],
            send_sem=send_sems.at[k - 1], recv_sem=recv_sems.at[k - 1],
            device_id=peer, device_id_type=MESH))
        recvs.append(pltpu.make_async_remote_copy(
            src_ref=src_me, dst_ref=o_ref.at[pid],
            send_sem=send_sems.at[k - 1], recv_sem=recv_sems.at[k - 1],
            device_id=peer, device_id_type=MESH))
    return loc, sends, recvs


def _xchg_start(*refs, gather):
    loc, sends, _ = _xchg_copies(*refs, gather)
    loc.start()
    for cp in sends:
        cp.start()


def _xchg_wait(*refs, gather):
    loc, sends, recvs = _xchg_copies(*refs, gather)
    for cp in recvs:
        cp.wait_recv()
    for cp in sends:
        cp.wait_send()
    loc.wait()


_XCHG_SCRATCH = [pltpu.SemaphoreType.DMA((NDEV - 1,)), pltpu.SemaphoreType.DMA((NDEV - 1,)), pltpu.SemaphoreType.DMA]


def _xchg_out_shape(x, gather):
    return jax.ShapeDtypeStruct((NDEV,) + tuple(x.shape if gather else x.shape[1:]), x.dtype)


def _exchange(x, name, gather):
    def body(x_ref, o_ref, send_sems, recv_sems, loc_sem):
        _xchg_start(x_ref, o_ref, send_sems, recv_sems, loc_sem, gather=gather)
        _xchg_wait(x_ref, o_ref, send_sems, recv_sems, loc_sem, gather=gather)

    return pl.pallas_call(
        body, name=name, out_shape=_xchg_out_shape(x, gather),
        in_specs=[_HBM], out_specs=_HBM, scratch_shapes=_XCHG_SCRATCH,
    )(x)


TWO_LEVEL = "two_level"


def _two_level_copies(x_ref, o_ref, send_sems, recv_sems, loc_sem):
    mx, my, mc = lax.axis_index("x"), lax.axis_index("y"), lax.axis_index("c")
    me, sibling = (mx, my, mc), (mx, my, 1 - mc)
    chips = [(1 - mx, my), (mx, 1 - my), (1 - mx, 1 - my)]

    def slot(px, py, pc):
        return o_ref.at[4 * px + 2 * py + pc]

    def copy(k, block, to, src=None):
        return pltpu.make_async_remote_copy(
            src_ref=slot(*block) if src is None else src, dst_ref=slot(*block),
            send_sem=send_sems.at[k], recv_sem=recv_sems.at[k], device_id=to, device_id_type=MESH)

    mine = pltpu.make_async_copy(x_ref, slot(*me), loc_sem)
    first = [copy(0, me, sibling, src=x_ref)] + [copy(1 + j, me, (*chip, mc), src=x_ref) for j, chip in enumerate(chips)]
    passed = [copy(4 + j, (*chip, mc), sibling) for j, chip in enumerate(chips)]
    landed = [copy(1 + j, (*chip, mc), me) for j, chip in enumerate(chips)]
    from_sibling = [copy(0, sibling, me)] + [copy(4 + j, (*chip, 1 - mc), me) for j, chip in enumerate(chips)]
    return mine, first, passed, landed, from_sibling


def _two_level_start(*refs):
    mine, first, _, _, _ = _two_level_copies(*refs)
    mine.start()
    for cp in first:
        cp.start()


def _two_level_finish(*refs):
    mine, first, passed, landed, from_sibling = _two_level_copies(*refs)
    for cp, fwd in zip(landed, passed):
        cp.wait_recv()
        fwd.start()
    for cp in from_sibling:
        cp.wait_recv()
    for cp in first + passed:
        cp.wait_send()
    mine.wait()


def _hosted_call(body, xch, *, name, grid, in_specs, out_specs, out_shape, scratch_shapes, compiler_params, args,
                 aliases=None):
    aliases = aliases or {}
    if xch is None:
        return pl.pallas_call(body, name=name, grid=grid, in_specs=in_specs, out_specs=out_specs,
                              out_shape=out_shape, scratch_shapes=scratch_shapes, input_output_aliases=aliases,
                              compiler_params=compiler_params)(*args)
    xv, gather = xch
    n_in, n_out, n_scr = len(in_specs), len(out_specs), len(scratch_shapes)

    def wrapped(*refs):
        ins = refs[:n_in]
        x_ref = refs[n_in]
        outs = refs[n_in + 1:n_in + 1 + n_out]
        o_ref = refs[n_in + 1 + n_out]
        scr = refs[n_in + 2 + n_out:]
        own, sems = scr[:n_scr], scr[n_scr:]
        first = functools.reduce(jnp.logical_and, [pl.program_id(a) == 0 for a in range(len(grid))])
        last = functools.reduce(jnp.logical_and, [pl.program_id(a) == grid[a] - 1 for a in range(len(grid))])

        @pl.when(first)
        def _():
            if gather == TWO_LEVEL:
                _two_level_start(x_ref, o_ref, *sems)
            else:
                _xchg_start(x_ref, o_ref, *sems, gather=gather)

        body(*ins, *outs, *own)

        @pl.when(last)
        def _():
            if gather == TWO_LEVEL:
                _two_level_finish(x_ref, o_ref, *sems)
            else:
                _xchg_wait(x_ref, o_ref, *sems, gather=gather)

    res = pl.pallas_call(
        wrapped, name=name, grid=grid, in_specs=list(in_specs) + [_HBM], out_specs=list(out_specs) + [_HBM],
        out_shape=list(out_shape) + [_xchg_out_shape(xv, gather)],
        scratch_shapes=list(scratch_shapes) + _XCHG_SCRATCH, input_output_aliases=aliases,
        compiler_params=compiler_params,
    )(*args, xv)
    return list(res[:n_out]), res[n_out]


def _row_tile(R, cap, mult=8):
    best = mult
    t = mult
    while t <= min(R, cap):
        if R % t == 0:
            best = t
        t += mult
    assert R % best == 0, R
    return best


def _sum_slots(x, name, xch=None):
    _, R, C = x.shape
    tr = _row_tile(R, 256, 16 if x.dtype == BF16 else 8)

    def body(x_ref, o_ref):
        o_ref[...] = _slot_sum(x_ref)

    res = _hosted_call(
        body, xch, name=name, grid=(R // tr,),
        in_specs=[pl.BlockSpec((NDEV, tr, C), lambda i: (0, i, 0))],
        out_specs=[pl.BlockSpec((tr, C), lambda i: (i, 0))],
        out_shape=[jax.ShapeDtypeStruct((R, C), F32)],
        scratch_shapes=[], compiler_params=_cp(1), args=(x,))
    if xch is None:
        return res[0]
    return res[0][0], res[1]


def _slot_sum(x_ref):
    acc = x_ref[0].astype(F32)
    for i in range(1, NDEV):
        acc = acc + x_ref[i].astype(F32)
    return acc


def _sum_adamw(slots, w, m, v, name, xch=None):
    _, R, C = slots.shape
    tr = _row_tile(R, 128, 16 if slots.dtype == BF16 else 8)

    def body(x_ref, w_ref, m_ref, v_ref, g_ref, d_ref, nm_ref, nv_ref):
        g_ref[...] = _slot_sum(x_ref)
        _adamw_update(w_ref, g_ref, m_ref, v_ref, d_ref, nm_ref, nv_ref)

    blk = pl.BlockSpec((tr, C), lambda i: (i, 0))
    res = _hosted_call(
        body, xch, name=name, grid=(R // tr,),
        in_specs=[pl.BlockSpec((NDEV, tr, C), lambda i: (0, i, 0)), blk, blk, blk],
        out_specs=[blk] * 4, out_shape=[jax.ShapeDtypeStruct((R, C), F32)] * 4,
        scratch_shapes=[], compiler_params=_cp(1), args=(slots, w, m, v))
    if xch is None:
        return res
    return res[0], res[1]


def _adamw_update(w_ref, g_ref, m_ref, v_ref, d_ref, nm_ref, nv_ref):
    c1 = 1.0 / (1.0 - ADAM_B1 ** ADAM_STEP)
    c2 = 1.0 / (1.0 - ADAM_B2 ** ADAM_STEP)
    gv = g_ref[...]
    nm = ADAM_B1 * m_ref[...] + (1.0 - ADAM_B1) * gv
    nv = ADAM_B2 * v_ref[...] + (1.0 - ADAM_B2) * (gv * gv)
    d_ref[...] = -ADAM_LR * ((nm * c1) / (jnp.sqrt(nv * c2) + ADAM_EPS) + ADAM_WD * w_ref[...])
    nm_ref[...] = nm
    nv_ref[...] = nv


def _adamw_many(ws, gs, ms, vs):
    n = len(ws)

    def body(*refs):
        for i in range(n):
            _adamw_update(refs[i], refs[n + i], refs[2 * n + i], refs[3 * n + i],
                          refs[4 * n + i], refs[5 * n + i], refs[6 * n + i])

    shapes = [jax.ShapeDtypeStruct(w.shape, F32) for w in ws]
    res = pl.pallas_call(
        body, name="adamw_small", out_shape=shapes * 3,
        compiler_params=pltpu.CompilerParams(vmem_limit_bytes=VMEM_LIMIT_BYTES),
    )(*ws, *gs, *ms, *vs)
    return res[:n], res[n:2 * n], res[2 * n:]


def _adamw(w, g, m, v, name):
    R, C = w.shape
    tr = _row_tile(R, 256)

    def body(w_ref, g_ref, m_ref, v_ref, d_ref, nm_ref, nv_ref):
        _adamw_update(w_ref, g_ref, m_ref, v_ref, d_ref, nm_ref, nv_ref)

    blk = pl.BlockSpec((tr, C), lambda i: (i, 0))
    return pl.pallas_call(
        body, name=name, grid=(R // tr,),
        in_specs=[blk] * 4, out_specs=[blk] * 3,
        out_shape=[jax.ShapeDtypeStruct((R, C), F32)] * 3,
        compiler_params=_cp(1),
    )(w, g, m, v)


def _mod_fwd(c_rows, w_shard, b_shard):
    def body(c_ref, w_ref, b_ref, o_ref):
        s = _silu(c_ref[...]).astype(BF16)
        o_ref[...] = _dot(s, w_ref[...].astype(BF16)) + b_ref[...]

    return pl.pallas_call(
        body, name="mod_fwd",
        out_shape=jax.ShapeDtypeStruct((c_rows.shape[0], w_shard.shape[1]), F32),
        compiler_params=pltpu.CompilerParams(vmem_limit_bytes=VMEM_LIMIT_BYTES),
    )(c_rows, w_shard, b_shard)


def _mod_bwd(c_rows, dm_all, dm_shard, w_shard):
    nrow = c_rows.shape[0]

    def body(c_ref, da_ref, ds_ref, w_ref, gw_ref, gb_ref, cc_ref):
        s = _silu(c_ref[...]).astype(BF16)
        ds = ds_ref[...]
        gw_ref[...] = _dot_tn(s, ds.astype(BF16))
        gb_ref[...] = jnp.sum(da_ref[...], axis=0, keepdims=True)
        rowi = lax.broadcasted_iota(jnp.int32, ds.shape, 0)
        dmc = jnp.sum(jnp.where(rowi % 8 >= 4, ds, 0.0), axis=0, keepdims=True)
        dmc8 = jnp.broadcast_to(dmc, (8, ds.shape[1])).astype(BF16)
        cc_ref[...] = _dot_nt(dmc8, w_ref[...].astype(BF16))

    return pl.pallas_call(
        body, name="mod_bwd",
        out_shape=[jax.ShapeDtypeStruct(w_shard.shape, F32),
                   jax.ShapeDtypeStruct((1, dm_all.shape[1]), F32),
                   jax.ShapeDtypeStruct((8, D), F32)],
        compiler_params=pltpu.CompilerParams(vmem_limit_bytes=VMEM_LIMIT_BYTES),
    )(c_rows, dm_all, dm_shard, w_shard)


def _small_finish(cc_pre, c_ctx, dd_cols):
    def body(cc_ref, c_ref, dd_ref, gc_ref, gd_ref):
        gc_ref[...] = cc_ref[...] * _silu_grad(c_ref[...])
        gd_ref[...] = jnp.sum(dd_ref[...], axis=1, keepdims=True)

    return pl.pallas_call(
        body, name="small_finish",
        out_shape=[jax.ShapeDtypeStruct((1, D), F32), jax.ShapeDtypeStruct((NH, 1), F32)],
    )(cc_pre, c_ctx, dd_cols)


_BIG = (("w_in", (D, 10304), 1), ("w_br_ssd", (DI, D), 0), ("w_br_lru", (LW, D), 0), ("w_out", (D, D), 0),
        ("w_mlp1", (D, MLP), 1), ("w_mlp2", (MLP, D), 0))
_SMALL_SH = (("ssd_conv_w", (4, 4096)), ("lru_conv_w", (4, LW)), ("lru_ba", (2, LW)), ("lru_bi", (2, LW)),
             ("lru_lambda", (2, LW)))
_REPL = (("c_ctx", (D,)), ("b_gate", (2 * D,)), ("ssd_conv_b", (4096,)), ("ssd_dt_bias", (2, NH)),
         ("ssd_a_log", (2, NH)), ("ssd_d", (DI,)), ("ssd_norm_w", (DI,)), ("lru_conv_b", (LW,)),
         ("ln1_g", (D,)), ("ln1_b", (D,)),
         ("b_mlp1", (MLP,)), ("b_mlp2", (D,)), ("ln2_g", (D,)), ("ln2_b", (D,)))

_WEIGHT_NAMES = ('c_ctx', 'w_mod', 'b_mod', 'w_in', 'b_gate', 'ssd_conv_w', 'ssd_conv_b', 'ssd_dt_bias', 'ssd_a_log',
                 'ssd_d', 'ssd_norm_w', 'lru_conv_w', 'lru_conv_b', 'lru_wa', 'lru_ba', 'lru_wi', 'lru_bi',
                 'lru_lambda', 'w_br_ssd', 'w_br_lru', 'w_out', 'ln1_g', 'ln1_b', 'w_mlp1', 'b_mlp1', 'w_mlp2',
                 'b_mlp2', 'ln2_g', 'ln2_b')
_ARG_NAMES = ('x', 'c', 'ctx') + _WEIGHT_NAMES + ('loss_target',) + tuple('m_' + n for n in _WEIGHT_NAMES) + tuple(
    'v_' + n for n in _WEIGHT_NAMES)


def _to_slots(full, axis):
    n = full.shape[axis] // NDEV
    if axis == 0:
        return full.reshape(NDEV, n, full.shape[1])
    return full.reshape(full.shape[0], NDEV, n).transpose(1, 0, 2)


def _from_slots(slots, axis):
    if axis == 0:
        return slots.reshape(NDEV * slots.shape[1], slots.shape[2])
    return slots.transpose(1, 0, 2).reshape(slots.shape[1], NDEV * slots.shape[2])


def _pack_rows(arrs, width=1024, mult=8):
    flat = jnp.concatenate([a.reshape(-1) for a in arrs])
    n = flat.shape[0]
    per = width * mult
    tot = -(-n // per) * per
    return jnp.pad(flat, (0, tot - n)).reshape(tot // width, width)


def _unpack_rows(packed, shapes, lead=()):
    nl = len(lead)
    flat = packed.reshape(tuple(lead) + (-1,))
    out, off = [], 0
    for s in shapes:
        n = math.prod(s)
        out.append(flat[..., off:off + n].reshape(tuple(lead) + tuple(s)))
        off += n
    return out


def kernel(x, c, ctx, c_ctx, w_mod, b_mod, w_in, b_gate, ssd_conv_w, ssd_conv_b, ssd_dt_bias, ssd_a_log, ssd_d, ssd_norm_w, lru_conv_w, lru_conv_b, lru_wa, lru_ba, lru_wi, lru_bi, lru_lambda, w_br_ssd, w_br_lru, w_out, ln1_g, ln1_b, w_mlp1, b_mlp1, w_mlp2, b_mlp2, ln2_g, ln2_b, loss_target, m_c_ctx, m_w_mod, m_b_mod, m_w_in, m_b_gate, m_ssd_conv_w, m_ssd_conv_b, m_ssd_dt_bias, m_ssd_a_log, m_ssd_d, m_ssd_norm_w, m_lru_conv_w, m_lru_conv_b, m_lru_wa, m_lru_ba, m_lru_wi, m_lru_bi, m_lru_lambda, m_w_br_ssd, m_w_br_lru, m_w_out, m_ln1_g, m_ln1_b, m_w_mlp1, m_b_mlp1, m_w_mlp2, m_b_mlp2, m_ln2_g, m_ln2_b, v_c_ctx, v_w_mod, v_b_mod, v_w_in, v_b_gate, v_ssd_conv_w, v_ssd_conv_b, v_ssd_dt_bias, v_ssd_a_log, v_ssd_d, v_ssd_norm_w, v_lru_conv_w, v_lru_conv_b, v_lru_wa, v_lru_ba, v_lru_wi, v_lru_bi, v_lru_lambda, v_w_br_ssd, v_w_br_lru, v_w_out, v_ln1_g, v_ln1_b, v_w_mlp1, v_b_mlp1, v_w_mlp2, v_b_mlp2, v_ln2_g, v_ln2_b):
    A = dict(zip(_ARG_NAMES, (x, c, ctx, c_ctx, w_mod, b_mod, w_in, b_gate, ssd_conv_w, ssd_conv_b, ssd_dt_bias, ssd_a_log, ssd_d, ssd_norm_w, lru_conv_w, lru_conv_b, lru_wa, lru_ba, lru_wi, lru_bi, lru_lambda, w_br_ssd, w_br_lru, w_out, ln1_g, ln1_b, w_mlp1, b_mlp1, w_mlp2, b_mlp2, ln2_g, ln2_b, loss_target, m_c_ctx, m_w_mod, m_b_mod, m_w_in, m_b_gate, m_ssd_conv_w, m_ssd_conv_b, m_ssd_dt_bias, m_ssd_a_log, m_ssd_d, m_ssd_norm_w, m_lru_conv_w, m_lru_conv_b, m_lru_wa, m_lru_ba, m_lru_wi, m_lru_bi, m_lru_lambda, m_w_br_ssd, m_w_br_lru, m_w_out, m_ln1_g, m_ln1_b, m_w_mlp1, m_b_mlp1, m_w_mlp2, m_b_mlp2, m_ln2_g, m_ln2_b, v_c_ctx, v_w_mod, v_b_mod, v_w_in, v_b_gate, v_ssd_conv_w, v_ssd_conv_b, v_ssd_dt_bias, v_ssd_a_log, v_ssd_d, v_ssd_norm_w, v_lru_conv_w, v_lru_conv_b, v_lru_wa, v_lru_ba, v_lru_wi, v_lru_bi, v_lru_lambda, v_w_br_ssd, v_w_br_lru, v_w_out, v_ln1_g, v_ln1_b, v_w_mlp1, v_b_mlp1, v_w_mlp2, v_b_mlp2, v_ln2_g, v_ln2_b)))
    Bn, T, _ = x.shape
    Tc = ctx.shape[1]
    cfg = _Cfg(Bn, T, Tc)
    me = _me()
    L = {n: (A[n] if n == "c_ctx" else A[n][0]) for n in _WEIGHT_NAMES}
    nmod = L["w_mod"].shape[1]

    c_all = _exchange(c, "ag_c", True)
    c_rows = jnp.concatenate([c_all.reshape(NDEV * Bn, D), jnp.broadcast_to(c_ctx[None, :], (8, D))], axis=0)
    b_shard = lax.dynamic_slice(L["b_mod"], (me * nmod,), (nmod,)).reshape(1, nmod)
    m_part = _mod_fwd(c_rows, L["w_mod"], b_shard)
    m_all = _exchange(m_part, "ag_mod", True)
    m_full = m_all.transpose(1, 0, 2).reshape(NDEV * Bn + 8, NMOD * D)
    m_mine = lax.dynamic_slice(m_full, (me * Bn, 0), (Bn, NMOD * D))
    mc = m_full[NDEV * Bn, :2 * D]

    rest_payload = jnp.concatenate([L[n].astype(BF16).reshape(-1, 1024) for n, _, _ in _BIG[1:]], axis=0)
    small_shapes = [(s[0], s[1] // NDEV) for _, s in _SMALL_SH]
    small_all = _exchange(_pack_rows([L[n] for n, _ in _SMALL_SH], width=512), "ag_w_small", True)
    W = {}
    for (n, shp), piece in zip(_SMALL_SH, _unpack_rows(small_all, small_shapes, lead=(NDEV,))):
        W[n] = piece.transpose(1, 0, 2).reshape(shp)
    for n in ("ssd_conv_b", "lru_conv_b", "ssd_norm_w", "b_gate", "ln1_g", "ln1_b", "b_mlp1", "b_mlp2", "ln2_g", "ln2_b"):
        W[n] = L[n].reshape(1, -1)
    for n in ("ssd_dt_bias", "ssd_a_log", "ssd_d", "lru_wa", "lru_wi"):
        W[n] = L[n]

    loss_part, grad_x, g, dm, dmc, xres = _local_step(cfg, x, ctx, loss_target, m_mine, mc, W,
                                                      L["w_in"].astype(BF16), rest_payload)
    loss = lax.psum(loss_part, ("x", "y", "c"))

    dmc_pad = jnp.pad(dmc, ((0, 4 - Bn), (0, (NMOD - 2) * D)))
    dm_payload = jnp.concatenate([jnp.pad(dm, ((0, 4 - Bn), (0, 0))), dmc_pad], axis=0)
    upd_w_in, dm_all = _sum_adamw(xres["rs_w_in"], L["w_in"], A["m_w_in"][0], A["v_w_in"][0], "sum_adamw_w_in",
                                  xch=(dm_payload, True))
    dm_all = dm_all.reshape(NDEV * 8, NMOD * D)
    c_rows_b = jnp.concatenate([jnp.pad(c_all, ((0, 0), (0, 4 - Bn), (0, 0))),
                                jnp.broadcast_to(c_ctx[None, None, :], (NDEV, 4, D))], axis=1).reshape(NDEV * 8, D)
    dm_shard = lax.dynamic_slice(dm_all, (0, me * nmod), (NDEV * 8, nmod))
    g_w_mod, g_b_mod, cc_part = _mod_bwd(c_rows_b, dm_all, dm_shard, L["w_mod"])
    g["c_ctx"] = cc_part[0]

    g["ssd_d"] = g.pop("ssd_d_cols")
    small_names = [n for n, _ in _REPL] + [n for n, _ in _SMALL_SH]
    small_full_shapes = [s for _, s in _REPL] + [s for _, s in _SMALL_SH]
    red_b, sm_all = _sum_slots(xres["rs_rest"], "sum_w_rest", xch=(_pack_rows([g[n] for n in small_names]), True))
    sm_sum = _sum_slots(sm_all, "sum_g_small")
    gs = dict(zip(small_names, _unpack_rows(sm_sum, small_full_shapes)))
    gcc, gdd = _small_finish(gs["c_ctx"].reshape(1, D), c_ctx.reshape(1, D), gs["ssd_d"].reshape(NH, HD))
    gs["c_ctx"] = gcc.reshape(D)
    gs["ssd_d"] = gdd.reshape(NH)
    for n, shp in _SMALL_SH:
        ns = shp[1] // NDEV
        gs[n] = lax.dynamic_slice(gs[n], (0, me * ns), (shp[0], ns))
    gs["b_mod"] = g_b_mod.reshape(NMOD * D)
    lru_sum = _sum_slots(xres["ag_lru"], "sum_g_lru").reshape(2, 2, LB, LBW, LBW)
    gs["lru_wa"], gs["lru_wi"] = lru_sum[0], lru_sum[1]

    gb = {}
    off = 0
    for n, shp, axis in _BIG[1:]:
        shard_shape = (shp[0] // NDEV, shp[1]) if axis == 0 else (shp[0], shp[1] // NDEV)
        r = math.prod(shard_shape) // 1024
        gb[n] = red_b[off:off + r].reshape(shard_shape)
        off += r
    gb["w_mod"] = g_w_mod

    grads, deltas, new_m, new_v = {}, {}, {}, {}
    big_names = ["w_mod"] + [n for n, _, _ in _BIG]
    grads["w_in"], deltas["w_in"], new_m["w_in"], new_v["w_in"] = upd_w_in
    for n in big_names:
        if n == "w_in":
            continue
        d_, nm_, nv_ = _adamw(L[n], gb[n], A["m_" + n][0], A["v_" + n][0], "adamw_" + n)
        grads[n], deltas[n], new_m[n], new_v[n] = gb[n], d_, nm_, nv_
    sm_names = [n for n in _WEIGHT_NAMES if n not in big_names]

    def two_d(a):
        return a.reshape(1, -1) if a.ndim == 1 else a
    loc = lambda pre: [two_d(A[pre + n] if n == "c_ctx" else A[pre + n][0]) for n in sm_names]
    gsm = [two_d(gs[n].reshape(L[n].shape)) for n in sm_names]
    ds_, nms_, nvs_ = _adamw_many(loc(""), gsm, loc("m_"), loc("v_"))
    for n, gv, dv, mv, vv in zip(sm_names, gsm, ds_, nms_, nvs_):
        shp = L[n].shape
        grads[n], deltas[n], new_m[n], new_v[n] = gv.reshape(shp), dv.reshape(shp), mv.reshape(shp), vv.reshape(shp)

    def out(dct):
        return [dct[n] if n == "c_ctx" else dct[n][None] for n in _WEIGHT_NAMES]
    return (loss, grad_x, *out(grads), *out(deltas), *out(new_m), *out(new_v))
```

```python
import functools
import math

import jax
import jax.numpy as jnp
from jax import lax
from jax.experimental import pallas as pl
from jax.experimental.pallas import tpu as pltpu

F32 = jnp.float32
BF16 = jnp.bfloat16

D = 1024
GRID_W = 64
DI = 2048
NH = 32
HD = 64
NG = 8
HPG = 4
NS = 128
CH = 128
LW = 1024
LB = 8
LBW = 128
LRU_C = 8.0
MLP = 4096
NMOD = 6
ALPHA = 2.0 ** 0.25
LN_EPS = 1e-6
RMS_EPS = 1e-5
PM = 10240
DTW = 128
CONVW = 5120
NDEV = 8

ADAM_LR = 0.001
ADAM_B1 = 0.9
ADAM_B2 = 0.999
ADAM_EPS = 1e-08
ADAM_WD = 0.01
ADAM_STEP = 10

VMEM_LIMIT_BYTES = 56 * 1024 * 1024


def _cp(n_axes):
    return pltpu.CompilerParams(dimension_semantics=("arbitrary",) * n_axes,
                                vmem_limit_bytes=VMEM_LIMIT_BYTES)


def _sigmoid(x):
    return 0.5 * jnp.tanh(0.5 * x) + 0.5


def _silu(x):
    return x * _sigmoid(x)


def _silu_grad(x):
    s = _sigmoid(x)
    return s * (1.0 + x * (1.0 - s))


def _log1p_pos(e):
    return jnp.where(e < 1e-2, e * (1.0 - e * (0.5 - e * (1.0 / 3.0))), jnp.log(1.0 + e))


def _softplus(x):
    return jnp.maximum(x, 0.0) + _log1p_pos(jnp.exp(-jnp.abs(x)))


_GELU_K = math.sqrt(2.0 / math.pi)


def _gelu(x):
    t = jnp.tanh(_GELU_K * (x + 0.044715 * x * x * x))
    return 0.5 * x * (1.0 + t)


def _gelu_and_grad(x):
    x2 = x * x
    t = jnp.tanh(_GELU_K * x * (1.0 + 0.044715 * x2))
    dt = (1.0 - t * t) * _GELU_K * (1.0 + 3.0 * 0.044715 * x2)
    h = 0.5 * (1.0 + t)
    return x * h, h + 0.5 * x * dt


def _ln(x):
    mu = jnp.mean(x, axis=-1, keepdims=True)
    xc = x - mu
    var = jnp.mean(xc * xc, axis=-1, keepdims=True)
    rs = lax.rsqrt(var + LN_EPS)
    return xc * rs, rs


def _ln_bwd(dy, xhat, rs):
    m1 = jnp.mean(dy, axis=-1, keepdims=True)
    m2 = jnp.mean(dy * xhat, axis=-1, keepdims=True)
    return rs * (dy - m1 - xhat * m2)


def _dot(a, b):
    return lax.dot_general(a, b, (((1,), (0,)), ((), ())), preferred_element_type=F32)


def _dot_nt(a, b):
    return lax.dot_general(a, b, (((1,), (1,)), ((), ())), preferred_element_type=F32)


def _dot_tn(a, b):
    return lax.dot_general(a, b, (((0,), (0,)), ((), ())), preferred_element_type=F32)


def _split3(a):
    a0 = a.astype(BF16)
    r = a - a0.astype(F32)
    a1 = r.astype(BF16)
    a2 = (r - a1.astype(F32)).astype(BF16)
    return a0, a1, a2


def _dot_exact_l(m_bf, a):
    a0, a1, a2 = _split3(a)
    return _dot(m_bf, a0) + _dot(m_bf, a1) + _dot(m_bf, a2)


def _dot_hilo_r(a, m_bf):
    a0 = a.astype(BF16)
    a1 = (a - a0.astype(F32)).astype(BF16)
    return _dot(a0, m_bf) + _dot(a1, m_bf)


def _tri(n, upper):
    ii = lax.broadcasted_iota(jnp.int32, (n, n), 0)
    kk = lax.broadcasted_iota(jnp.int32, (n, n), 1)
    m = (kk >= ii) if upper else (kk <= ii)
    return jnp.where(m, 1.0, 0.0).astype(BF16)


def _fit(n, t):
    t = min(t, n)
    while n % t:
        t //= 2
    return t


def _mm(a, b, mode, name, out_dtype=F32, tm=512, tn=512, tk=512, xch=None):
    if mode == "nn":
        M, K = a.shape
        N = b.shape[1]
    elif mode == "nt":
        M, K = a.shape
        N = b.shape[0]
    else:
        K, M = a.shape
        N = b.shape[1]
    tm, tn, tk = _fit(M, tm), _fit(N, tn), _fit(K, tk)
    assert M % tm == 0 and N % tn == 0 and K % tk == 0, (name, M, N, K, tm, tn, tk)
    nk = K // tk
    if mode == "tn":
        a_spec = pl.BlockSpec((tk, tm), lambda i, j, k: (k, i))
    else:
        a_spec = pl.BlockSpec((tm, tk), lambda i, j, k: (i, k))
    if mode == "nt":
        b_spec = pl.BlockSpec((tn, tk), lambda i, j, k: (j, k))
    else:
        b_spec = pl.BlockSpec((tk, tn), lambda i, j, k: (k, j))
    dn = {"nn": (((1,), (0,)), ((), ())), "nt": (((1,), (1,)), ((), ())), "tn": (((0,), (0,)), ((), ()))}[mode]

    def body(a_ref, b_ref, o_ref, acc_ref):
        k = pl.program_id(2)

        @pl.when(k == 0)
        def _():
            acc_ref[...] = jnp.zeros_like(acc_ref)

        acc_ref[...] += lax.dot_general(a_ref[...].astype(BF16), b_ref[...].astype(BF16), dn,
                                        preferred_element_type=F32)

        @pl.when(k == nk - 1)
        def _():
            o_ref[...] = acc_ref[...].astype(o_ref.dtype)

    def body_one_step(a_ref, b_ref, o_ref):
        o_ref[...] = lax.dot_general(a_ref[...].astype(BF16), b_ref[...].astype(BF16), dn,
                                     preferred_element_type=F32).astype(o_ref.dtype)

    res = _hosted_call(
        body if nk > 1 else body_one_step, xch, name=name, grid=(M // tm, N // tn, nk),
        in_specs=[a_spec, b_spec],
        out_specs=[pl.BlockSpec((tm, tn), lambda i, j, k: (i, j))],
        out_shape=[jax.ShapeDtypeStruct((M, N), out_dtype)],
        scratch_shapes=[pltpu.VMEM((tm, tn), F32)] if nk > 1 else [],
        compiler_params=_cp(3), args=(a, b))
    if xch is None:
        return res[0]
    return res[0][0], res[1]


def _mm_mlp1(h2, w1, b1, tm=1024, tn=1024):
    M, K = h2.shape
    N = w1.shape[1]
    tm, tn = _fit(M, tm), _fit(N, tn)

    def body(a_ref, b_ref, bias_ref, a1_ref, act_ref):
        v = _dot(a_ref[...], b_ref[...]) + bias_ref[...]
        a1_ref[...] = v
        r = jnp.maximum(v, 0.0)
        act_ref[...] = (r * r).astype(BF16)

    out = pl.BlockSpec((tm, tn), lambda i, j: (i, j))
    return pl.pallas_call(
        body, name="mm_mlp1", grid=(M // tm, N // tn),
        in_specs=[pl.BlockSpec((tm, K), lambda i, j: (i, 0)), pl.BlockSpec((K, tn), lambda i, j: (0, j)),
                  pl.BlockSpec((1, tn), lambda i, j: (0, j))],
        out_specs=[out, out],
        out_shape=[jax.ShapeDtypeStruct((M, N), F32), jax.ShapeDtypeStruct((M, N), BF16)],
        compiler_params=_cp(2),
    )(h2, w1, b1)


def _mm_dact(dmlp, w2, a1, tm=1024, tn=1024):
    M, K = dmlp.shape
    N = w2.shape[0]
    tm, tn = _fit(M, tm), _fit(N, tn)

    def body(d_ref, w_ref, a1_ref, o_ref, acc_ref):
        i = pl.program_id(1)

        @pl.when(i == 0)
        def _():
            acc_ref[...] = jnp.zeros_like(acc_ref)

        da = _dot_nt(d_ref[...], w_ref[...]) * (2.0 * jnp.maximum(a1_ref[...], 0.0))
        o_ref[...] = da.astype(BF16)
        acc_ref[0:1, :] += jnp.sum(da, axis=0, keepdims=True)

    blk = pl.BlockSpec((tm, tn), lambda j, i: (i, j))
    return pl.pallas_call(
        body, name="mm_dact", grid=(N // tn, M // tm),
        in_specs=[pl.BlockSpec((tm, K), lambda j, i: (i, 0)), pl.BlockSpec((tn, K), lambda j, i: (j, 0)), blk],
        out_specs=[blk, pl.BlockSpec((8, tn), lambda j, i: (0, j))],
        out_shape=[jax.ShapeDtypeStruct((M, N), BF16), jax.ShapeDtypeStruct((8, N), F32)],
        compiler_params=_cp(2),
    )(dmlp, w2, a1)


class _Cfg:
    def __init__(self, Bn, T, Tc):
        assert T % Tc == 0 and Tc % CH == 0 and Tc % GRID_W == 0
        self.Bn, self.T, self.Tc = Bn, T, Tc
        self.TT = T + Tc
        self.TB = Tc
        self.nbt = self.TT // self.TB
        self.nbl = T // self.TB
        self.NT = Bn * self.TT
        self.N = Bn * T
        self.nct = Tc // CH
        self.nlt = T // CH
        self.nch = self.nct + self.nlt


def _ln_mod_fwd(cfg, ctx2, x2, shift_tab, scale_tab, xch):
    TB, nbt, nbl = cfg.TB, cfg.nbt, cfg.nbl

    def body(c_ref, x_ref, sh_ref, sc_ref, o_ref):
        j = pl.program_id(1)
        xhat, _ = _ln(jnp.where(j == 0, c_ref[...], x_ref[...]))
        o_ref[...] = (xhat * (1.0 + sc_ref[...]) + sh_ref[...]).astype(BF16)

    tab = pl.BlockSpec((None, 1, D), lambda b, j: (2 * b + jnp.minimum(j, 1), 0, 0))
    outs, xres = _hosted_call(
        body, xch, name="ln_mod_fwd", grid=(cfg.Bn, nbt),
        in_specs=[pl.BlockSpec((TB, D), lambda b, j: (b, 0)),
                  pl.BlockSpec((TB, D), lambda b, j: (b * nbl + jnp.maximum(j - 1, 0), 0)), tab, tab],
        out_specs=[pl.BlockSpec((TB, D), lambda b, j: (b * nbt + j, 0))],
        out_shape=[jax.ShapeDtypeStruct((cfg.NT, D), BF16)],
        scratch_shapes=[], compiler_params=_cp(2), args=(ctx2, x2, shift_tab, scale_tab))
    return outs[0], xres


GP = 8
NGB = NG // GP
HPB = GP * HPG


def _heads_to_front(x, d, gb, inverse=False):
    off = d * NH + gb * HPB
    return pltpu.roll(x, off if inverse else (DTW - off) % DTW, 1)


def _chunks_per_step(cfg):
    n = cfg.NT // CH
    return max(c for c in (4, 3, 2, 1) if n % c == 0)


def _dt_fwd(cfg, dt_raw, dt_bias, a_log):
    cps = _chunks_per_step(cfg)

    def body(raw_ref, bias_ref, alog_ref, dt_ref, dtg_ref, cumg_ref, cumT_ref):
        a = -jnp.exp(alog_ref[...])
        col = lax.broadcasted_iota(jnp.int32, (CH, DTW), 1)
        for c in range(cps):
            rows = slice(c * CH, (c + 1) * CH)
            dt = _softplus(raw_ref[rows, :] + bias_ref[...])
            dta = dt * a
            cf = _dot_exact_l(_tri(CH, False), dta)
            cr = _dot_exact_l(_tri(CH, True), dta)
            cum = jnp.where(col < NH, cf, cr)
            dt_ref[rows, :] = dt
            for d in range(2):
                for gb in range(NGB):
                    dtg_ref[d, gb, rows, :] = _heads_to_front(dt, d, gb)
                    cg = _heads_to_front(cum, d, gb)
                    cumg_ref[d, gb, rows, :] = cg
                    cumT_ref[d, gb, c] = cg.T

    blk = pl.BlockSpec((cps * CH, DTW), lambda i: (i, 0))
    row = pl.BlockSpec((1, DTW), lambda i: (0, 0))
    gblk = pl.BlockSpec((2, NGB, cps * CH, DTW), lambda i: (0, 0, i, 0))
    return pl.pallas_call(
        body, name="dt_fwd", grid=(cfg.NT // (cps * CH),),
        in_specs=[blk, row, row],
        out_specs=[blk, gblk, gblk, pl.BlockSpec((2, NGB, cps, DTW, CH), lambda i: (0, 0, i, 0, 0))],
        out_shape=[jax.ShapeDtypeStruct((cfg.NT, DTW), F32),
                   jax.ShapeDtypeStruct((2, NGB, cfg.NT, DTW), F32),
                   jax.ShapeDtypeStruct((2, NGB, cfg.NT, DTW), F32),
                   jax.ShapeDtypeStruct((2, NGB, cfg.NT // CH, DTW, CH), F32)],
        compiler_params=_cp(1),
    )(dt_raw, dt_bias, a_log)


def _dt_bwd(cfg, dAs, dxxs, dt_raw, dt, dt_bias, a_log, h):
    def body(dAf_ref, dAr_ref, dxf_ref, dxr_ref, raw_ref, dt_ref, bias_ref, alog_ref, h_ref, o_ref, acc_ref, dw_ref):
        i = pl.program_id(0)

        @pl.when(i == 0)
        def _():
            acc_ref[...] = jnp.zeros_like(acc_ref)
            dw_ref[...] = jnp.zeros_like(dw_ref)

        a = -jnp.exp(alog_ref[...])
        col = lax.broadcasted_iota(jnp.int32, (CH, DTW), 1)
        for c in range(cps):
            rows = slice(c * CH, (c + 1) * CH)
            dA_v = jnp.zeros((CH, DTW), F32)
            dxx_v = jnp.zeros((CH, DTW), F32)
            for d, (ra, rx) in enumerate(((dAf_ref, dxf_ref), (dAr_ref, dxr_ref))):
                for gb in range(NGB):
                    dA_v = dA_v + _heads_to_front(ra[gb, rows, :], d, gb, inverse=True)
                    dxx_v = dxx_v + _heads_to_front(rx[gb, rows, :], d, gb, inverse=True)
            ddta = jnp.where(col < NH, _dot_exact_l(_tri(CH, True), dA_v), _dot_exact_l(_tri(CH, False), dA_v))
            dtv = dt_ref[rows, :]
            ddt = ddta * a + dxx_v
            draw = ddt * _sigmoid(raw_ref[rows, :] + bias_ref[...])
            draw = jnp.where(col < 2 * NH, draw, 0.0)
            o_ref[rows, :] = draw
            dw_ref[...] += _dot_tn(h_ref[rows, :], draw.astype(BF16))
            da = jnp.sum(ddta * dtv, axis=0, keepdims=True) * a
            da = jnp.where(col[:1] < 2 * NH, da, 0.0)
            acc_ref[0:1, :] += da
            acc_ref[1:2, :] += jnp.sum(draw, axis=0, keepdims=True)

    cps = _chunks_per_step(cfg)
    blk = pl.BlockSpec((cps * CH, DTW), lambda i: (i, 0))
    row = pl.BlockSpec((1, DTW), lambda i: (0, 0))
    gblk = pl.BlockSpec((NGB, cps * CH, DTW), lambda i: (0, i, 0))
    return pl.pallas_call(
        body, name="dt_bwd", grid=(cfg.NT // (cps * CH),),
        in_specs=[gblk, gblk, gblk, gblk, blk, blk, row, row, pl.BlockSpec((cps * CH, D), lambda i: (i, 0))],
        out_specs=[blk, pl.BlockSpec((8, DTW), lambda i: (0, 0)), pl.BlockSpec((D, DTW), lambda i: (0, 0))],
        out_shape=[jax.ShapeDtypeStruct((cfg.NT, DTW), F32), jax.ShapeDtypeStruct((8, DTW), F32),
                   jax.ShapeDtypeStruct((D, DTW), F32)],
        compiler_params=_cp(1),
    )(dAs[0], dAs[1], dxxs[0], dxxs[1], dt_raw, dt, dt_bias, a_log, h)


_TAPS = (2, 1, 0, -1)


def _conv_fwd(cfg, proj, conv_w, conv_b):
    TB, nbt = cfg.TB, cfg.nbt
    SUB = 256
    n_act = DI + 2 * NG * NS

    def body(u_ref, w_ref, b_ref, o_ref, sg_ref):
        i = pl.program_id(0)
        R = jnp.where(i % nbt == 0, cfg.Tc, GRID_W)
        t = lax.broadcasted_iota(jnp.int32, (TB, SUB), 0)
        pos = jnp.bitwise_and(t, R - 1)
        keep = {s: jnp.where(jnp.logical_and(pos - s >= 0, pos - s < R), 1.0, 0.0) for s in (2, 1, -1)}
        for q in range(CONVW // SUB):
            sl = slice(q * SUB, (q + 1) * SUB)
            u = u_ref[:, sl]
            pre = b_ref[:, sl] + w_ref[2:3, sl] * u
            for k in (0, 1, 3):
                pre = pre + w_ref[k:k + 1, sl] * (pltpu.roll(u, _TAPS[k] % TB, 0) * keep[_TAPS[k]])
            if q * SUB < n_act:
                s = _sigmoid(pre)
                o_ref[:, sl] = pre * s
                sg_ref[:, sl] = s * (1.0 + pre * (1.0 - s))
            else:
                o_ref[:, sl] = pre

    blk = pl.BlockSpec((TB, CONVW), lambda i: (i, 0))
    return pl.pallas_call(
        body, name="conv_fwd", grid=(cfg.NT // TB,),
        in_specs=[blk, pl.BlockSpec((4, CONVW), lambda i: (0, 0)), pl.BlockSpec((1, CONVW), lambda i: (0, 0))],
        out_specs=[blk, blk],
        out_shape=[jax.ShapeDtypeStruct((cfg.NT, CONVW), F32)] * 2,
        compiler_params=_cp(1),
    )(proj, conv_w, conv_b)


_ANY = pl.BlockSpec(memory_space=pl.ANY)


def _conv_bwd(cfg, name, dproj, proj, conv_w, sgrad, addends, col0, width, skip=None, xch=None):
    TB, nbt, nbl = cfg.TB, cfg.nbt, cfg.nbl
    CB = min(width, 2048)
    SUB = 256
    c0 = col0 // CB
    addends = list(addends) + ([] if sgrad is None else [sgrad])
    n_add = len(addends)

    def body(*refs):
        u_ref, w_ref = refs[1:3]
        add_refs = refs[3:3 + n_add]
        rest = refs[3 + n_add:]
        if sgrad is not None:
            add_refs, sg_ref = add_refs[:-1], add_refs[-1]
        if skip is not None:
            dy_ref, dv_ref = rest[:2]
            rest = rest[2:]
        o_ref, acc_ref = rest
        i = pl.program_id(1)

        @pl.when(i == 0)
        def _():
            acc_ref[...] = jnp.zeros_like(acc_ref)

        isctx = (i % nbt) == 0
        R = jnp.where(isctx, cfg.Tc, GRID_W)
        t = lax.broadcasted_iota(jnp.int32, (TB, SUB), 0)
        pos = jnp.bitwise_and(t, R - 1)
        keep = {s: jnp.where(jnp.logical_and(pos - s >= 0, pos - s < R), 1.0, 0.0) for s in (2, 1, -1, -2)}

        def shifted(v, s):
            return v if s == 0 else pltpu.roll(v, s % TB, 0) * keep[s]

        for q in range(CB // SUB):
            sl = slice(q * SUB, (q + 1) * SUB)
            u = u_ref[:, sl]
            us = [shifted(u, _TAPS[k]) for k in range(4)]
            g = add_refs[0][:, sl]
            for r in add_refs[1:]:
                g = g + r[:, sl]
            if skip is not None:
                g = g + jnp.where(isctx, 0.0, dv_ref[:, sl] * dy_ref[:, sl])
            if sgrad is not None:
                g = g * sg_ref[:, sl]
            dp = jnp.zeros_like(g)
            for k in range(4):
                acc_ref[k:k + 1, sl] += jnp.sum(g * us[k], axis=0, keepdims=True)
                dp = dp + w_ref[k:k + 1, sl] * shifted(g, -_TAPS[k])
            acc_ref[4:5, sl] += jnp.sum(g, axis=0, keepdims=True)
            o_ref[:, sl] = dp.astype(BF16)

    blk = pl.BlockSpec((TB, CB), lambda j, i: (i, j))
    wide = pl.BlockSpec((TB, CB), lambda j, i: (i, c0 + j))
    in_specs = [_ANY, wide, pl.BlockSpec((4, CB), lambda j, i: (0, c0 + j))]
    in_specs += [blk] * (n_add if sgrad is None else n_add - 1) + ([] if sgrad is None else [wide])
    args = [dproj, proj, conv_w] + addends
    if skip is not None:
        def lat(j, i):
            b = i // nbt
            return (b * nbl + jnp.maximum(i % nbt - 1, 0), j)
        in_specs += [pl.BlockSpec((TB, CB), lat), pl.BlockSpec((1, CB), lambda j, i: (0, j))]
        args += list(skip)
    return _hosted_call(
        body, xch, name=name, grid=(width // CB, cfg.NT // TB),
        in_specs=in_specs,
        out_specs=[pl.BlockSpec((TB, CB), lambda j, i: (i, c0 + j)), pl.BlockSpec((8, CB), lambda j, i: (0, j))],
        out_shape=[jax.ShapeDtypeStruct((cfg.NT, PM), BF16), jax.ShapeDtypeStruct((8, width), F32)],
        scratch_shapes=[], compiler_params=_cp(2), args=args, aliases={0: 0})


def _chunk_of_step(cfg, rev):
    nct, nlt = cfg.nct, cfg.nlt
    if not rev:
        return lambda s: s
    return lambda s: jnp.where(s < nct, nct - 1 - s, 2 * nct + nlt - 1 - s)


def _expand4(v, band, base):
    out = v[:, base + 3:base + 4]
    for h in (2, 1, 0):
        out = jnp.where(band == h, v[:, base + h:base + h + 1], out)
    return out


def _ssd_step_tiles(dt_ref, cum_ref, cumT_ref, rev):
    cum_t = cum_ref[...]
    last = 0 if rev else CH - 1
    llast = cum_t[last:last + 1, :]
    return (dt_ref[...], cum_t, cumT_ref[...], llast, jnp.exp(llast), last)


def _ssd_common(gi, x_ref, b_ref, c_ref, tiles, rev, intra=True):
    dt_t, cum_t, cumT_t, llast, elast, last = tiles
    base = gi * HPG
    xh = x_ref[:, gi * HPG * HD:(gi + 1) * HPG * HD]
    Bm = b_ref[:, gi * NS:(gi + 1) * NS].astype(BF16)
    band = lax.broadcasted_iota(jnp.int32, (CH, HPG * HD), 1) // HD
    cbs = [jnp.broadcast_to(cum_t[:, base + h:base + h + 1], (CH, CH)) for h in range(HPG)]
    Cm = G = decs = None
    if intra:
        Cm = c_ref[:, gi * NS:(gi + 1) * NS].astype(BF16)
        G = _dot_nt(Cm, Bm)
        ii = lax.broadcasted_iota(jnp.int32, (CH, CH), 0)
        jj = lax.broadcasted_iota(jnp.int32, (CH, CH), 1)
        mask = (jj >= ii) if rev else (jj <= ii)
        decs = [jnp.exp(jnp.where(mask, cbs[h] - cumT_t[base + h:base + h + 1, :], -1e30)) for h in range(HPG)]
    cum_exp = jnp.concatenate([cbs[3], cbs[3]], axis=1)
    ll_exp = llast[:, base + 3:base + 4]
    for h in (2, 1, 0):
        cum_exp = jnp.where(band == h, jnp.concatenate([cbs[h], cbs[h]], axis=1), cum_exp)
        ll_exp = jnp.where(band[:1] == h, llast[:, base + h:base + h + 1], ll_exp)
    ecum = jnp.exp(cum_exp) if intra else None
    e_exp = jnp.exp(ll_exp - cum_exp)
    dt_exp = _expand4(dt_t, band, base)
    X = xh * dt_exp
    rb = lax.broadcasted_iota(jnp.int32, (HPG * HD, NS), 0) // HD
    dec_rows = elast[:, base + 3:base + 4]
    for h in (2, 1, 0):
        dec_rows = jnp.where(rb == h, elast[:, base + h:base + h + 1], dec_rows)
    return xh, Bm, Cm, band, e_exp, ecum, dt_exp, X, G, decs, elast, dec_rows, last


def _ssd_specs(cfg, rev):
    nch = cfg.nch
    cmap = _chunk_of_step(cfg, rev)
    d = 1 if rev else 0

    def make(stepmap):
        def row(b, g, sp):
            return b * nch + cmap(stepmap(sp))
        bo, co = DI // (GP * NS), (DI + NG * NS) // (GP * NS)
        return [
            pl.BlockSpec((CH, GP * HPG * HD), lambda b, g, sp: (row(b, g, sp), g)),
            pl.BlockSpec((CH, GP * NS), lambda b, g, sp: (row(b, g, sp), bo + g)),
            pl.BlockSpec((CH, GP * NS), lambda b, g, sp: (row(b, g, sp), co + g)),
            pl.BlockSpec((None, None, CH, DTW), lambda b, g, sp: (d, g, row(b, g, sp), 0)),
            pl.BlockSpec((None, None, CH, DTW), lambda b, g, sp: (d, g, row(b, g, sp), 0)),
            pl.BlockSpec((None, None, None, DTW, CH), lambda b, g, sp: (d, g, row(b, g, sp), 0, 0)),
        ], row
    return make


def _ssd_fwd(cfg, act, dtg, cumg, cumTg, rev, y_other=None, dvec=None):
    nch = cfg.nch
    in_specs, row = _ssd_specs(cfg, rev)(lambda sp: sp)
    total = y_other is not None

    def body(*refs):
        x_ref, b_ref, c_ref, dt_ref, cum_ref, cumT_ref = refs[:6]
        if total:
            yo_ref, dv_ref = refs[6:8]
        y_ref, hs_ref, h_scr = refs[-3:]
        s = pl.program_id(2)

        @pl.when(s == 0)
        def _():
            h_scr[...] = jnp.zeros_like(h_scr)

        def step(intra):
            tiles = _ssd_step_tiles(dt_ref, cum_ref, cumT_ref, rev)
            for gi in range(GP):
                xh, Bm, Cm, band, e_exp, ecum, dt_exp, X, G, decs, elast, dec_rows, last = _ssd_common(
                    gi, x_ref, b_ref, c_ref, tiles, rev, intra)
                H = h_scr[gi]
                if intra:
                    Mcat = jnp.concatenate([(G * decs[h]).astype(BF16) for h in range(HPG)], axis=1)
                    Xbd = jnp.concatenate([jnp.where(band == h, X, 0.0).astype(BF16) for h in range(HPG)], axis=0)
                    xsl = slice(gi * HPG * HD, (gi + 1) * HPG * HD)
                    Y = ecum * _dot_nt(Cm, H.astype(BF16)) + _dot(Mcat, Xbd)
                    if total:
                        Y = Y + yo_ref[:, xsl] + dv_ref[:, xsl] * xh
                    y_ref[:, xsl] = Y
                hs_ref[gi] = H
                S = _dot_tn((e_exp * X).astype(BF16), Bm)
                h_scr[gi] = dec_rows * H + S

        isctx = cmap(s) < cfg.nct

        @pl.when(isctx)
        def _():
            step(False)

        @pl.when(jnp.logical_not(isctx))
        def _():
            step(True)

    cmap = _chunk_of_step(cfg, rev)
    yblk = pl.BlockSpec((CH, GP * HPG * HD), lambda b, g, s: (row(b, g, s), g))
    args = [act, act, act, dtg, cumg, cumTg]
    if total:
        in_specs = in_specs + [yblk, pl.BlockSpec((1, GP * HPG * HD), lambda b, g, s: (0, g))]
        args += [y_other, dvec]
    return pl.pallas_call(
        body, name="ssd_fwd_rev" if rev else "ssd_fwd", grid=(cfg.Bn, NG // GP, nch),
        in_specs=in_specs,
        out_specs=[yblk, pl.BlockSpec((None, GP, None, HPG * HD, NS), lambda b, g, s: (b, g, s, 0, 0))],
        out_shape=[jax.ShapeDtypeStruct((cfg.NT, DI), F32),
                   jax.ShapeDtypeStruct((cfg.Bn, NG, nch, HPG * HD, NS), F32)],
        scratch_shapes=[pltpu.VMEM((GP, HPG * HD, NS), F32)],
        compiler_params=_cp(3),
    )(*args)


def _ssd_bwd(cfg, act, dtg, cumg, cumTg, hs, dy, rev, xch=None):
    nch, nct, nlt = cfg.nch, cfg.nct, cfg.nlt
    cmap = _chunk_of_step(cfg, rev)
    in_specs, row = _ssd_specs(cfg, rev)(lambda sp: nch - 1 - sp)

    def lat_row(b, g, sp):
        c = cmap(nch - 1 - sp)
        return b * nlt + jnp.maximum(c - nct, 0)

    def body(x_ref, b_ref, c_ref, dt_ref, cum_ref, cumT_ref, dy_ref, hs_ref,
             dxh_ref, dB_ref, dC_ref, dA_ref, dxx_ref, dh_scr):
        sp = pl.program_id(2)

        @pl.when(sp == 0)
        def _():
            dh_scr[...] = jnp.zeros_like(dh_scr)

        def step(intra):
            tiles = _ssd_step_tiles(dt_ref, cum_ref, cumT_ref, rev)
            dA_t = jnp.zeros((CH, DTW), F32)
            dAT_t = jnp.zeros((DTW, CH), F32)
            dxx_t = jnp.zeros((CH, DTW), F32)
            for gi in range(GP):
                dA_g, dAT_g, dxx_g = group_bwd(gi, intra, tiles, x_ref, b_ref, c_ref, dy_ref, hs_ref,
                                               dxh_ref, dB_ref, dC_ref, dh_scr)
                dA_t = dA_t + dA_g
                dxx_t = dxx_t + dxx_g
                if intra:
                    dAT_t = dAT_t + dAT_g
            dA_ref[...] = dA_t - dAT_t.T if intra else dA_t
            dxx_ref[...] = dxx_t

        isctx = cmap(nch - 1 - sp) < nct

        @pl.when(isctx)
        def _():
            step(False)

        @pl.when(jnp.logical_not(isctx))
        def _():
            step(True)

    def group_bwd(gi, intra, tiles, x_ref, b_ref, c_ref, dy_ref, hs_ref, dxh_ref, dB_ref, dC_ref, dh_scr):
        xsl = slice(gi * HPG * HD, (gi + 1) * HPG * HD)
        nsl = slice(gi * NS, (gi + 1) * NS)
        base = gi * HPG
        xh, Bm, Cm, band, e_exp, ecum, dt_exp, X, G, decs, elast, dec_rows, last = _ssd_common(
            gi, x_ref, b_ref, c_ref, tiles, rev, intra)
        H = hs_ref[gi]
        dHn = dh_scr[gi]
        dHnb = dHn.astype(BF16)
        BdH = _dot_nt(Bm, dHnb)
        dX = e_exp * BdH
        eX = e_exp * X
        lanei = lax.broadcasted_iota(jnp.int32, (CH, DTW), 1)
        dA = jnp.zeros((CH, DTW), F32)
        dAT = None
        pb = lax.broadcasted_iota(jnp.int32, (HPG * HD, NS), 0) // HD
        pl_ = lax.broadcasted_iota(jnp.int32, (HPG * HD, NS), 1)
        E = jnp.where(pb + base == pl_, 1.0, 0.0).astype(BF16)
        if intra:
            dY = dy_ref[:, xsl]
            Hb = H.astype(BF16)
            dYs = ecum * dY
            dYsb = dYs.astype(BF16)
            Ys = ecum * _dot_nt(Cm, Hb)
            dG = jnp.zeros((CH, CH), F32)
            subi = lax.broadcasted_iota(jnp.int32, (DTW, CH), 0)
            dAT = jnp.zeros((DTW, CH), F32)
            Xbd = jnp.concatenate([jnp.where(band == h, X, 0.0).astype(BF16) for h in range(HPG)], axis=0)
            dYbd = jnp.concatenate([jnp.where(band == h, dY, 0.0).astype(BF16) for h in range(HPG)], axis=0)
            dMcat = _dot_nt(dY.astype(BF16), Xbd)
            Ms = []
            for h in range(HPG):
                M = G * decs[h]
                dM = dMcat[:, h * CH:(h + 1) * CH]
                W = dM * M
                dG = dG + dM * decs[h]
                Ms.append(M.astype(BF16))
                dA = dA + jnp.where(lanei == base + h, jnp.sum(W, axis=1, keepdims=True), 0.0)
                dAT = dAT + jnp.where(subi == base + h, jnp.sum(W, axis=0, keepdims=True), 0.0)
            dX = dX + _dot_tn(jnp.concatenate(Ms, axis=0), dYbd)
            dGb = dG.astype(BF16)
            dC_ref[:, nsl] = _dot(dGb, Bm) + _dot(dYsb, Hb)
            dB_ref[:, nsl] = _dot_tn(dGb, Cm) + _dot(eX.astype(BF16), dHnb)
            dh_scr[gi] = dec_rows * dHn + _dot_tn(dYsb, Cm)
            dA = dA + _dot_hilo_r(dY * Ys, E)
        else:
            dC_ref[:, nsl] = jnp.zeros((CH, NS), F32)
            dB_ref[:, nsl] = _dot(eX.astype(BF16), dHnb)
            dh_scr[gi] = dec_rows * dHn
        q = _dot_hilo_r(eX * BdH, E)
        r = jnp.sum(dHn * H, axis=1, keepdims=True)
        lane1 = lax.broadcasted_iota(jnp.int32, (1, DTW), 1)
        hdot = jnp.zeros((1, DTW), F32)
        for h in range(HPG):
            hv = jnp.sum(r[h * HD:(h + 1) * HD, :], axis=0, keepdims=True)
            hdot = hdot + jnp.where(lane1 == base + h, hv, 0.0)
        dllast = jnp.sum(q, axis=0, keepdims=True) + elast * hdot
        rowi = lax.broadcasted_iota(jnp.int32, (CH, DTW), 0)
        dxh_ref[:, xsl] = dX * dt_exp
        return dA - q + jnp.where(rowi == last, dllast, 0.0), dAT, _dot_hilo_r(dX * xh, E)

    small = pl.BlockSpec((None, CH, DTW), lambda b, g, sp: (g, row(b, g, sp), 0))
    return _hosted_call(
        body, xch, name="ssd_bwd_rev" if rev else "ssd_bwd", grid=(cfg.Bn, NG // GP, nch),
        in_specs=in_specs + [
            pl.BlockSpec((CH, GP * HPG * HD), lambda b, g, sp: (lat_row(b, g, sp), g)),
            pl.BlockSpec((None, GP, None, HPG * HD, NS), lambda b, g, sp: (b, g, nch - 1 - sp, 0, 0))],
        out_specs=[pl.BlockSpec((CH, GP * HPG * HD), lambda b, g, sp: (row(b, g, sp), g)),
                   pl.BlockSpec((CH, GP * NS), lambda b, g, sp: (row(b, g, sp), g)),
                   pl.BlockSpec((CH, GP * NS), lambda b, g, sp: (row(b, g, sp), g)),
                   small, small],
        out_shape=[jax.ShapeDtypeStruct((cfg.NT, DI), F32),
                   jax.ShapeDtypeStruct((cfg.NT, NG * NS), F32),
                   jax.ShapeDtypeStruct((cfg.NT, NG * NS), F32),
                   jax.ShapeDtypeStruct((NGB, cfg.NT, DTW), F32),
                   jax.ShapeDtypeStruct((NGB, cfg.NT, DTW), F32)],
        scratch_shapes=[pltpu.VMEM((GP, HPG * HD, NS), F32)],
        compiler_params=_cp(3), args=(act, act, act, dtg, cumg, cumTg, dy, hs))


def _shift_rows(v, s, fill, toward_later, rowi):
    n = v.shape[0]
    if toward_later:
        return jnp.where(rowi >= s, pltpu.roll(v, s, 0), fill)
    return jnp.where(rowi < n - s, pltpu.roll(v, n - s, 0), fill)


def _chunk_scan(a, b, carry, later):
    nt = a.shape[0] // 8
    rowi = lax.broadcasted_iota(jnp.int32, (8, a.shape[1]), 0)
    outs = [None] * nt
    for r in (range(nt) if later else range(nt - 1, -1, -1)):
        av = a[r * 8:(r + 1) * 8]
        bv = b[r * 8:(r + 1) * 8]
        for sh in (1, 2, 4):
            a_p = _shift_rows(av, sh, 1.0, later, rowi)
            b_p = _shift_rows(bv, sh, 0.0, later, rowi)
            bv = av * b_p + bv
            av = av * a_p
        h = bv + av * carry
        outs[r] = h
        carry = h[7:8] if later else h[0:1]
    return jnp.concatenate(outs, axis=0), carry


def _lru_gates(u, wa_ref, wi_ref, ba_ref, bi_ref, lam_ref):
    rs, is_ = [], []
    for k in range(LB):
        uk = u[:, k * LBW:(k + 1) * LBW].astype(BF16)
        rs.append(_dot(uk, wa_ref[k].astype(BF16)))
        is_.append(_dot(uk, wi_ref[k].astype(BF16)))
    r = 1.0 / (1.0 + jnp.exp(-(jnp.concatenate(rs, axis=1) + ba_ref[...])))
    ig = _sigmoid(jnp.concatenate(is_, axis=1) + bi_ref[...])
    sp = _softplus(-lam_ref[...])
    la = -LRU_C * r * sp
    a = jnp.exp(la)
    q = (1.0 + a * a) * jnp.tanh(-la)
    return r, ig, sp, la, a, jnp.sqrt(q), lax.rsqrt(q)


def _lru_w_specs(d):
    return [pl.BlockSpec((None, LB, LBW, LBW), lambda b, s: (d, 0, 0, 0)),
            pl.BlockSpec((None, LB, LBW, LBW), lambda b, s: (d, 0, 0, 0)),
            pl.BlockSpec((None, 1, LW), lambda b, s: (d, 0, 0)),
            pl.BlockSpec((None, 1, LW), lambda b, s: (d, 0, 0)),
            pl.BlockSpec((None, 1, LW), lambda b, s: (d, 0, 0))]


def _lru_block_of_step(cfg, rev):
    nbl = cfg.nbl
    if not rev:
        return lambda s: s
    return lambda s: jnp.where(s < 1, 0, 1 + nbl - s)


def _lru_fwd(cfg, act, wa, wi, ba, bi, lam, rev):
    nch, CH = cfg.nbt, cfg.TB
    cmap = _lru_block_of_step(cfg, rev)
    d = 1 if rev else 0
    ucol = (DI + 2 * NG * NS) // LW

    def body(u_ref, wa_ref, wi_ref, ba_ref, bi_ref, lam_ref, h_ref, c_scr):
        s = pl.program_id(1)

        @pl.when(s == 0)
        def _():
            c_scr[...] = jnp.zeros_like(c_scr)

        u = u_ref[...]
        r, ig, sp, la, a, g, _ = _lru_gates(u, wa_ref, wi_ref, ba_ref, bi_ref, lam_ref)
        h, carry = _chunk_scan(a, g * ig * u, c_scr[0:1, :], not rev)
        h_ref[...] = h
        c_scr[0:1, :] = carry

    return pl.pallas_call(
        body, name="lru_fwd_rev" if rev else "lru_fwd", grid=(cfg.Bn, nch),
        in_specs=[pl.BlockSpec((CH, LW), lambda b, s: (b * nch + cmap(s), ucol))] + _lru_w_specs(d),
        out_specs=pl.BlockSpec((CH, LW), lambda b, s: (b * nch + cmap(s), 0)),
        out_shape=jax.ShapeDtypeStruct((cfg.NT, LW), F32),
        scratch_shapes=[pltpu.VMEM((8, LW), F32)],
        compiler_params=_cp(2),
    )(act, wa, wi, ba, bi, lam)


def _lru_bwd(cfg, act, wa, wi, ba, bi, lam, hd, dyl, rev):
    nch, nct, nlt, CH = cfg.nbt, 1, cfg.nbl, cfg.TB
    cmap = _lru_block_of_step(cfg, rev)
    d = 1 if rev else 0
    ucol = (DI + 2 * NG * NS) // LW

    def srow(b, sp):
        return b * nch + cmap(nch - 1 - sp)

    def prev_rows(b, sp):
        s = nch - 1 - sp
        cp = cmap(jnp.maximum(s - 1, 0))
        base = (b * nch + cp) * (CH // 8)
        return base + (0 if rev else CH // 8 - 1)

    def lat_row(b, sp):
        c = cmap(nch - 1 - sp)
        return b * nlt + jnp.maximum(c - nct, 0)

    def body(u_ref, wa_ref, wi_ref, ba_ref, bi_ref, lam_ref, h_ref, hp_ref, dy_ref,
             du_ref, dwa_ref, dwi_ref, vec_ref, c_scr):
        b = pl.program_id(0)
        sp_id = pl.program_id(1)
        s = nch - 1 - sp_id

        @pl.when(sp_id == 0)
        def _():
            c_scr[...] = jnp.zeros_like(c_scr)

        @pl.when(jnp.logical_and(b == 0, sp_id == 0))
        def _():
            dwa_ref[...] = jnp.zeros_like(dwa_ref)
            dwi_ref[...] = jnp.zeros_like(dwi_ref)
            vec_ref[...] = jnp.zeros_like(vec_ref)

        c = cmap(s)
        u = u_ref[...]
        r, ig, spl, la, a, g, ginv = _lru_gates(u, wa_ref, wi_ref, ba_ref, bi_ref, lam_ref)
        dh = jnp.where(c < nct, 0.0, dy_ref[...])
        rowi = lax.broadcasted_iota(jnp.int32, (CH, LW), 0)
        lamv, _ = _chunk_scan(_shift_rows(a, 1, 1.0, rev, rowi), dh, c_scr[0:1, :], rev)
        first = CH - 1 if rev else 0
        c_scr[0:1, :] = (a * lamv)[first:first + 1, :]
        hprow = hp_ref[...][(0 if rev else 7):(1 if rev else 8), :]
        hprow = jnp.where(s > 0, hprow, 0.0)
        h_prev = _shift_rows(h_ref[...], 1, hprow, not rev, rowi)
        da = lamv * h_prev
        db = lamv
        iu = ig * u
        dla = da * a - db * iu * (a * a) * ginv
        dr = dla * (-LRU_C * spl)
        di = db * g * u
        du = db * g * ig
        drp = dr * r * (1.0 - r)
        dip = di * ig * (1.0 - ig)
        dus = []
        for k in range(LB):
            sl = slice(k * LBW, (k + 1) * LBW)
            drk = drp[:, sl].astype(BF16)
            dik = dip[:, sl].astype(BF16)
            uk = u[:, sl].astype(BF16)
            dus.append(_dot_nt(drk, wa_ref[k].astype(BF16)) + _dot_nt(dik, wi_ref[k].astype(BF16)))
            dwa_ref[k] += _dot_tn(uk, drk)
            dwi_ref[k] += _dot_tn(uk, dik)
        du_ref[...] = du + jnp.concatenate(dus, axis=1)
        vec_ref[0:1, :] += jnp.sum(drp, axis=0, keepdims=True)
        vec_ref[1:2, :] += jnp.sum(dip, axis=0, keepdims=True)
        dsp = jnp.sum(dla * (-LRU_C * r), axis=0, keepdims=True)
        vec_ref[2:3, :] += dsp * (-_sigmoid(-lam_ref[...]))

    return pl.pallas_call(
        body, name="lru_bwd_rev" if rev else "lru_bwd", grid=(cfg.Bn, nch),
        in_specs=[pl.BlockSpec((CH, LW), lambda b, sp: (srow(b, sp), ucol))] + _lru_w_specs(d) + [
            pl.BlockSpec((CH, LW), lambda b, sp: (srow(b, sp), 0)),
            pl.BlockSpec((8, LW), lambda b, sp: (prev_rows(b, sp), 0)),
            pl.BlockSpec((CH, LW), lambda b, sp: (lat_row(b, sp), 0))],
        out_specs=[pl.BlockSpec((CH, LW), lambda b, sp: (srow(b, sp), 0)),
                   pl.BlockSpec((LB, LBW, LBW), lambda b, sp: (0, 0, 0)),
                   pl.BlockSpec((LB, LBW, LBW), lambda b, sp: (0, 0, 0)),
                   pl.BlockSpec((8, LW), lambda b, sp: (0, 0))],
        out_shape=[jax.ShapeDtypeStruct((cfg.NT, LW), F32),
                   jax.ShapeDtypeStruct((LB, LBW, LBW), F32),
                   jax.ShapeDtypeStruct((LB, LBW, LBW), F32),
                   jax.ShapeDtypeStruct((8, LW), F32)],
        scratch_shapes=[pltpu.VMEM((8, LW), F32)],
        compiler_params=_cp(2),
    )(act, wa, wi, ba, bi, lam, hd, hd, dyl)


HB = 1024


def _post_ssd_fwd(cfg, y, proj, norm_w):
    TB, nbt, nbl = cfg.TB, cfg.nbt, cfg.nbl
    zc = CONVW // HB

    def body(y_ref, z_ref, w_ref, o_ref):
        u = y_ref[...] * _silu(z_ref[...])
        for gi in range(HB // (DI // NG)):
            sl = slice(gi * 256, (gi + 1) * 256)
            ug = u[:, sl]
            rs = lax.rsqrt(jnp.mean(ug * ug, axis=1, keepdims=True) + RMS_EPS)
            o_ref[:, sl] = (ug * rs * w_ref[:, sl]).astype(BF16)

    def st(b, j, cb):
        return (b * nbt + 1 + j, cb)
    return pl.pallas_call(
        body, name="post_ssd_fwd", grid=(cfg.Bn, nbl, DI // HB),
        in_specs=[pl.BlockSpec((TB, HB), st),
                  pl.BlockSpec((TB, HB), lambda b, j, cb: (b * nbt + 1 + j, zc + cb)),
                  pl.BlockSpec((1, HB), lambda b, j, cb: (0, cb))],
        out_specs=pl.BlockSpec((TB, HB), lambda b, j, cb: (b * nbl + j, cb)),
        out_shape=jax.ShapeDtypeStruct((cfg.N, DI), BF16),
        compiler_params=_cp(3),
    )(y, proj, norm_w)


def _post_ssd_bwd(cfg, dproj, dn, y, act, proj, norm_w):
    TB, nbt, nbl = cfg.TB, cfg.nbt, cfg.nbl
    zc = CONVW // HB

    def body(_, dn_ref, y_ref, xh_ref, z_ref, w_ref, dy_ref, dz_ref, acc_ref):
        b = pl.program_id(1)
        j = pl.program_id(2)

        @pl.when(jnp.logical_and(b == 0, j == 0))
        def _():
            acc_ref[...] = jnp.zeros_like(acc_ref)

        @pl.when(j == 0)
        def _():
            dz_ref[...] = jnp.zeros_like(dz_ref)

        @pl.when(j > 0)
        def _():
            latent(dn_ref, y_ref, xh_ref, z_ref, w_ref, dy_ref, dz_ref, acc_ref)

    def latent(dn_ref, y_ref, xh_ref, z_ref, w_ref, dy_ref, dz_ref, acc_ref):
        xh = xh_ref[...]
        z = z_ref[...]
        y = y_ref[...]
        sg = _sigmoid(z)
        sz = z * sg
        dsz = sg * (1.0 + z * (1.0 - sg))
        u = y * sz
        dout = dn_ref[...]
        for gi in range(HB // (DI // NG)):
            sl = slice(gi * 256, (gi + 1) * 256)
            ug0 = u[:, sl]
            rs = lax.rsqrt(jnp.mean(ug0 * ug0, axis=1, keepdims=True) + RMS_EPS)
            ug = ug0 * rs
            do = dout[:, sl]
            acc_ref[0:1, sl] += jnp.sum(do * ug, axis=0, keepdims=True)
            dug = do * w_ref[:, sl]
            du = rs * (dug - ug * jnp.mean(dug * ug, axis=1, keepdims=True))
            dy = du * sz[:, sl]
            dy_ref[:, sl] = dy
            dz_ref[:, sl] = (du * y[:, sl] * dsz[:, sl]).astype(BF16)
            acc_ref[1:2, sl] += jnp.sum(dy * xh[:, sl], axis=0, keepdims=True)

    def st(cb, b, j):
        return (b * nbt + j, cb)

    def la(cb, b, j):
        return (b * nbl + jnp.maximum(j - 1, 0), cb)
    return pl.pallas_call(
        body, name="post_ssd_bwd", grid=(DI // HB, cfg.Bn, nbt),
        in_specs=[_ANY, pl.BlockSpec((TB, HB), la), pl.BlockSpec((TB, HB), st), pl.BlockSpec((TB, HB), st),
                  pl.BlockSpec((TB, HB), lambda cb, b, j: (b * nbt + j, zc + cb)),
                  pl.BlockSpec((1, HB), lambda cb, b, j: (0, cb))],
        out_specs=[pl.BlockSpec((TB, HB), la),
                   pl.BlockSpec((TB, HB), lambda cb, b, j: (b * nbt + j, zc + cb)),
                   pl.BlockSpec((8, HB), lambda cb, b, j: (0, cb))],
        out_shape=[jax.ShapeDtypeStruct((cfg.N, DI), F32), jax.ShapeDtypeStruct((cfg.NT, PM), BF16),
                   jax.ShapeDtypeStruct((8, DI), F32)],
        input_output_aliases={0: 1},
        compiler_params=_cp(3),
    )(dproj, dn, y, act, proj, norm_w)


def _post_lru_fwd(cfg, hf, hb, proj):
    TB, nbt, nbl = cfg.TB, cfg.nbt, cfg.nbl
    gc = (CONVW + DI) // HB

    def body(hf_ref, hb_ref, g_ref, o_ref):
        o_ref[...] = ((hf_ref[...] + hb_ref[...]) * _gelu(g_ref[...])).astype(BF16)

    st = pl.BlockSpec((TB, HB), lambda b, j: (b * nbt + 1 + j, 0))
    return pl.pallas_call(
        body, name="post_lru_fwd", grid=(cfg.Bn, nbl),
        in_specs=[st, st, pl.BlockSpec((TB, HB), lambda b, j: (b * nbt + 1 + j, gc))],
        out_specs=pl.BlockSpec((TB, HB), lambda b, j: (b * nbl + j, 0)),
        out_shape=jax.ShapeDtypeStruct((cfg.N, LW), BF16),
        compiler_params=_cp(2),
    )(hf, hb, proj)


def _post_lru_bwd(cfg, dproj, dv, hf, hb, proj):
    TB, nbt, nbl = cfg.TB, cfg.nbt, cfg.nbl
    gc = (CONVW + DI) // HB

    def body(_, dv_ref, hf_ref, hb_ref, g_ref, dy_ref, dg_ref):
        j = pl.program_id(1)

        @pl.when(j == 0)
        def _():
            dg_ref[...] = jnp.zeros_like(dg_ref)

        @pl.when(j > 0)
        def _():
            gt = g_ref[...]
            dvv = dv_ref[...]
            gl, dgl = _gelu_and_grad(gt)
            dy_ref[...] = dvv * gl
            dg_ref[...] = (dvv * (hf_ref[...] + hb_ref[...]) * dgl).astype(BF16)

    st = pl.BlockSpec((TB, HB), lambda b, j: (b * nbt + j, 0))
    la = pl.BlockSpec((TB, HB), lambda b, j: (b * nbl + jnp.maximum(j - 1, 0), 0))
    gcol = pl.BlockSpec((TB, HB), lambda b, j: (b * nbt + j, gc))
    return pl.pallas_call(
        body, name="post_lru_bwd", grid=(cfg.Bn, nbt),
        in_specs=[_ANY, la, st, st, gcol],
        out_specs=[la, gcol],
        out_shape=[jax.ShapeDtypeStruct((cfg.N, LW), F32), jax.ShapeDtypeStruct((cfg.NT, PM), BF16)],
        input_output_aliases={0: 1},
        compiler_params=_cp(2),
    )(dproj, dv, hf, hb, proj)


def _merge_fwd(cfg, proj, b_gate, br_ssd, br_lru):
    TB, nbt, nbl = cfg.TB, cfg.nbt, cfg.nbl
    mc = (CONVW + DI + LW) // HB

    def body(ms_ref, ml_ref, bg_ref, bs_ref, bl_ref, o_ref):
        gs = _sigmoid(ms_ref[...] + bg_ref[:, :D])
        gl = _sigmoid(ml_ref[...] + bg_ref[:, D:])
        o_ref[...] = (gs * bs_ref[...] + gl * bl_ref[...]).astype(BF16)

    la = pl.BlockSpec((TB, D), lambda b, j: (b * nbl + j, 0))
    return pl.pallas_call(
        body, name="merge_fwd", grid=(cfg.Bn, nbl),
        in_specs=[pl.BlockSpec((TB, HB), lambda b, j: (b * nbt + 1 + j, mc)),
                  pl.BlockSpec((TB, HB), lambda b, j: (b * nbt + 1 + j, mc + 1)),
                  pl.BlockSpec((1, 2 * D), lambda b, j: (0, 0)), la, la],
        out_specs=la,
        out_shape=jax.ShapeDtypeStruct((cfg.N, D), BF16),
        compiler_params=_cp(2),
    )(proj, proj, b_gate, br_ssd, br_lru)


def _merge_bwd(cfg, dmix, proj, b_gate, br_ssd, br_lru):
    TB, nbt, nbl = cfg.TB, cfg.nbt, cfg.nbl
    mc = (CONVW + DI + LW) // HB

    def body(dm_ref, ms_ref, ml_ref, bg_ref, bs_ref, bl_ref, ds_ref, dl_ref, dmg_ref, acc_ref):
        b = pl.program_id(0)
        j = pl.program_id(1)

        @pl.when(jnp.logical_and(b == 0, j == 0))
        def _():
            acc_ref[...] = jnp.zeros_like(acc_ref)

        @pl.when(j == 0)
        def _():
            dmg_ref[...] = jnp.zeros_like(dmg_ref)

        @pl.when(j > 0)
        def _():
            latent(dm_ref, ms_ref, ml_ref, bg_ref, bs_ref, bl_ref, ds_ref, dl_ref, dmg_ref, acc_ref)

    def latent(dm_ref, ms_ref, ml_ref, bg_ref, bs_ref, bl_ref, ds_ref, dl_ref, dmg_ref, acc_ref):
        dm = dm_ref[...]
        gs = _sigmoid(ms_ref[...] + bg_ref[:, :D])
        gl = _sigmoid(ml_ref[...] + bg_ref[:, D:])
        ds_ref[...] = (dm * gs).astype(BF16)
        dl_ref[...] = (dm * gl).astype(BF16)
        dps = dm * bs_ref[...] * gs * (1.0 - gs)
        dpl = dm * bl_ref[...] * gl * (1.0 - gl)
        dmg_ref[:, :D] = dps.astype(BF16)
        dmg_ref[:, D:] = dpl.astype(BF16)
        acc_ref[0:1, :D] += jnp.sum(dps, axis=0, keepdims=True)
        acc_ref[0:1, D:] += jnp.sum(dpl, axis=0, keepdims=True)

    la = pl.BlockSpec((TB, D), lambda b, j: (b * nbl + jnp.maximum(j - 1, 0), 0))
    return pl.pallas_call(
        body, name="merge_bwd", grid=(cfg.Bn, nbt),
        in_specs=[la, pl.BlockSpec((TB, HB), lambda b, j: (b * nbt + j, mc)),
                  pl.BlockSpec((TB, HB), lambda b, j: (b * nbt + j, mc + 1)),
                  pl.BlockSpec((1, 2 * D), lambda b, j: (0, 0)), la, la],
        out_specs=[la, la, pl.BlockSpec((TB, 2 * D), lambda b, j: (b * nbt + j, mc // 2)),
                   pl.BlockSpec((8, 2 * D), lambda b, j: (0, 0))],
        out_shape=[jax.ShapeDtypeStruct((cfg.N, D), BF16), jax.ShapeDtypeStruct((cfg.N, D), BF16),
                   jax.ShapeDtypeStruct((cfg.NT, PM), BF16), jax.ShapeDtypeStruct((8, 2 * D), F32)],
        compiler_params=_cp(2),
    )(dmix, proj, proj, b_gate, br_ssd, br_lru)


def _resid1_fwd(cfg, x2, x_mix, gate1, shift2, scale2, ln1_g, ln1_b):
    TB, nbt, nbl = cfg.TB, cfg.nbt, cfg.nbl

    def body(x_ref, xm_ref, g1_ref, sh_ref, sc_ref, lg_ref, lb_ref, x1_ref, h2_ref):
        r1 = ALPHA * x_ref[...] + g1_ref[...] * xm_ref[...]
        xh, _ = _ln(r1)
        x1 = xh * lg_ref[...] + lb_ref[...]
        x1_ref[...] = x1
        xh2, _ = _ln(x1)
        h2_ref[...] = (xh2 * (1.0 + sc_ref[...]) + sh_ref[...]).astype(BF16)

    la = pl.BlockSpec((TB, D), lambda b, j: (b * nbl + j, 0))
    ex = pl.BlockSpec((None, 1, D), lambda b, j: (b, 0, 0))
    vec = pl.BlockSpec((1, D), lambda b, j: (0, 0))
    return pl.pallas_call(
        body, name="resid1_fwd", grid=(cfg.Bn, nbl),
        in_specs=[la, la, ex, ex, ex, vec, vec],
        out_specs=[la, la],
        out_shape=[jax.ShapeDtypeStruct((cfg.N, D), F32), jax.ShapeDtypeStruct((cfg.N, D), BF16)],
        compiler_params=_cp(2),
    )(x2, x_mix, gate1, shift2, scale2, ln1_g, ln1_b)


def _resid1_bwd(cfg, dh2, dx1p, x1, x2, x_mix, gate1, scale2, ln1_g):
    TB, nbt, nbl = cfg.TB, cfg.nbt, cfg.nbl

    def body(dh2_ref, dx1p_ref, x1_ref, x_ref, xm_ref, g1_ref, sc_ref, lg_ref,
             dxm_ref, dxp_ref, ex_ref, gl_ref):
        b = pl.program_id(0)
        j = pl.program_id(1)

        @pl.when(j == 0)
        def _():
            ex_ref[...] = jnp.zeros_like(ex_ref)

        @pl.when(jnp.logical_and(b == 0, j == 0))
        def _():
            gl_ref[...] = jnp.zeros_like(gl_ref)

        dh2 = dh2_ref[...]
        xh2, rs2 = _ln(x1_ref[...])
        ex_ref[0:1, :] += jnp.sum(dh2, axis=0, keepdims=True)
        ex_ref[1:2, :] += jnp.sum(dh2 * xh2, axis=0, keepdims=True)
        dx1 = dx1p_ref[...] + _ln_bwd(dh2 * (1.0 + sc_ref[...]), xh2, rs2)
        xm = xm_ref[...]
        g1 = g1_ref[...]
        r1 = ALPHA * x_ref[...] + g1 * xm
        xh1, rs1 = _ln(r1)
        gl_ref[0:1, :] += jnp.sum(dx1 * xh1, axis=0, keepdims=True)
        gl_ref[1:2, :] += jnp.sum(dx1, axis=0, keepdims=True)
        dr1 = _ln_bwd(dx1 * lg_ref[...], xh1, rs1)
        ex_ref[2:3, :] += jnp.sum(dr1 * xm, axis=0, keepdims=True)
        dxm_ref[...] = (dr1 * g1).astype(BF16)
        dxp_ref[...] = ALPHA * dr1

    la = pl.BlockSpec((TB, D), lambda b, j: (b * nbl + j, 0))
    ex = pl.BlockSpec((None, 1, D), lambda b, j: (b, 0, 0))
    vec = pl.BlockSpec((1, D), lambda b, j: (0, 0))
    return pl.pallas_call(
        body, name="resid1_bwd", grid=(cfg.Bn, nbl),
        in_specs=[la, la, la, la, la, ex, ex, vec],
        out_specs=[la, la, pl.BlockSpec((None, 8, D), lambda b, j: (b, 0, 0)),
                   pl.BlockSpec((8, D), lambda b, j: (0, 0))],
        out_shape=[jax.ShapeDtypeStruct((cfg.N, D), BF16), jax.ShapeDtypeStruct((cfg.N, D), F32),
                   jax.ShapeDtypeStruct((cfg.Bn, 8, D), F32), jax.ShapeDtypeStruct((8, D), F32)],
        compiler_params=_cp(2),
    )(dh2, dx1p, x1, x2, x_mix, gate1, scale2, ln1_g)


def _final_fwd_bwd(cfg, x1, mlp, b2, gate2, ln2_g, ln2_b, target):
    TB, nbl = cfg.TB, cfg.nbl

    def body(x1_ref, m_ref, b2_ref, g2_ref, lg_ref, lb_ref, t_ref, dm_ref, dx_ref, ex_ref, gl_ref):
        b = pl.program_id(0)
        j = pl.program_id(1)

        @pl.when(j == 0)
        def _():
            ex_ref[...] = jnp.zeros_like(ex_ref)

        @pl.when(jnp.logical_and(b == 0, j == 0))
        def _():
            gl_ref[...] = jnp.zeros_like(gl_ref)

        mv = m_ref[...] + b2_ref[...]
        g2 = g2_ref[...]
        r2 = ALPHA * x1_ref[...] + g2 * mv
        xh, rs = _ln(r2)
        lg = lg_ref[...]
        x2 = xh * lg + lb_ref[...]
        err = x2 - t_ref[...]
        ls = jnp.sum(jnp.sum(err * err, axis=1, keepdims=True), axis=0, keepdims=True) * (0.5 / D)
        gl_ref[3:4, :] += ls
        dx2 = err * (1.0 / D)
        gl_ref[0:1, :] += jnp.sum(dx2 * xh, axis=0, keepdims=True)
        gl_ref[1:2, :] += jnp.sum(dx2, axis=0, keepdims=True)
        dr2 = _ln_bwd(dx2 * lg, xh, rs)
        ex_ref[0:1, :] += jnp.sum(dr2 * mv, axis=0, keepdims=True)
        dmv = dr2 * g2
        gl_ref[2:3, :] += jnp.sum(dmv, axis=0, keepdims=True)
        dm_ref[...] = dmv.astype(BF16)
        dx_ref[...] = ALPHA * dr2

    la = pl.BlockSpec((TB, D), lambda b, j: (b * nbl + j, 0))
    ex = pl.BlockSpec((None, 1, D), lambda b, j: (b, 0, 0))
    vec = pl.BlockSpec((1, D), lambda b, j: (0, 0))
    return pl.pallas_call(
        body, name="final_fwd_bwd", grid=(cfg.Bn, nbl),
        in_specs=[la, la, vec, ex, vec, vec, la],
        out_specs=[la, la, pl.BlockSpec((None, 8, D), lambda b, j: (b, 0, 0)),
                   pl.BlockSpec((8, D), lambda b, j: (0, 0))],
        out_shape=[jax.ShapeDtypeStruct((cfg.N, D), BF16), jax.ShapeDtypeStruct((cfg.N, D), F32),
                   jax.ShapeDtypeStruct((cfg.Bn, 8, D), F32), jax.ShapeDtypeStruct((8, D), F32)],
        compiler_params=_cp(2),
    )(x1, mlp, b2, gate2, ln2_g, ln2_b, target)


def _ln_mod_bwd(cfg, dh_a, ddt_raw, w_dt, ctx2, x2, scale_tab, dxp):
    TB, nbt, nbl = cfg.TB, cfg.nbt, cfg.nbl

    def body(da_ref, dd_ref, wd_ref, c_ref, x_ref, sc_ref, dxp_ref, gx_ref, acc_ref):
        j = pl.program_id(1)

        @pl.when(j <= 1)
        def _():
            acc_ref[...] = jnp.zeros_like(acc_ref)

        dh = da_ref[...] + _dot_nt(dd_ref[...].astype(BF16), wd_ref[...])
        xhat, rs = _ln(jnp.where(j == 0, c_ref[...], x_ref[...]))
        acc_ref[0:1, :] += jnp.sum(dh, axis=0, keepdims=True)
        acc_ref[1:2, :] += jnp.sum(dh * xhat, axis=0, keepdims=True)
        gx_ref[...] = dxp_ref[...] + _ln_bwd(dh * (1.0 + sc_ref[...]), xhat, rs)

    st = pl.BlockSpec((TB, D), lambda b, j: (b * nbt + j, 0))
    la = pl.BlockSpec((TB, D), lambda b, j: (b * nbl + jnp.maximum(j - 1, 0), 0))
    return pl.pallas_call(
        body, name="ln_mod_bwd", grid=(cfg.Bn, nbt),
        in_specs=[st, pl.BlockSpec((TB, DTW), lambda b, j: (b * nbt + j, 0)),
                  pl.BlockSpec((D, DTW), lambda b, j: (0, 0)),
                  pl.BlockSpec((TB, D), lambda b, j: (b, 0)), la,
                  pl.BlockSpec((None, 1, D), lambda b, j: (2 * b + jnp.minimum(j, 1), 0, 0)), la],
        out_specs=[la, pl.BlockSpec((None, 8, D), lambda b, j: (2 * b + jnp.minimum(j, 1), 0, 0))],
        out_shape=[jax.ShapeDtypeStruct((cfg.N, D), F32), jax.ShapeDtypeStruct((2 * cfg.Bn, 8, D), F32)],
        compiler_params=_cp(2),
    )(dh_a, ddt_raw, w_dt, ctx2, x2, scale_tab, dxp)


def _perm_w_in(w_in):
    w_main = jnp.concatenate([w_in[:, 0:3072], w_in[:, 4160:5184], w_in[:, 3136:4160], w_in[:, 5184:10304]], axis=1)
    w_dt = jnp.pad(w_in[:, 3072:3136], ((0, 0), (0, DTW - 2 * NH)))
    return w_main, w_dt


def _unperm_w_in(dw_main, dw_dt):
    return jnp.concatenate([dw_main[:, 0:3072], dw_dt[:, :2 * NH], dw_main[:, 4096:5120],
                            dw_main[:, 3072:4096], dw_main[:, 5120:]], axis=1)


def _unpack_rest(rest_all):
    out, off = {}, 0
    for n, shp, axis in _BIG[1:]:
        shard_shape = (shp[0] // NDEV, shp[1]) if axis == 0 else (shp[0], shp[1] // NDEV)
        r = math.prod(shard_shape) // 1024
        out[n] = _from_slots(rest_all[:, off:off + r, :].reshape((NDEV,) + shard_shape), axis)
        off += r
    return out


def _local_step(cfg, x, ctx, target, m, mc, W, w_in_shard, rest_payload):
    Bn, T, Tc = cfg.Bn, cfg.T, cfg.Tc
    NT, N = cfg.NT, cfg.N
    ctx2, x2 = ctx.reshape(Bn * Tc, D), x.reshape(N, D)
    mch = [m[:, i * D:(i + 1) * D] for i in range(NMOD)]
    ctx_sh = jnp.broadcast_to(mc[None, :D], (Bn, D))
    ctx_sc = jnp.broadcast_to(mc[None, D:], (Bn, D))
    shift_tab = jnp.stack([ctx_sh, mch[0]], axis=1).reshape(2 * Bn, 1, D)
    scale_tab = jnp.stack([ctx_sc, mch[1]], axis=1).reshape(2 * Bn, 1, D)
    gate1 = mch[2].reshape(Bn, 1, D)
    shift2 = mch[3].reshape(Bn, 1, D)
    scale2 = mch[4].reshape(Bn, 1, D)
    gate2 = mch[5].reshape(Bn, 1, D)

    conv_w = jnp.concatenate([W["ssd_conv_w"], W["lru_conv_w"]], axis=1)
    conv_b = jnp.concatenate([W["ssd_conv_b"], W["lru_conv_b"]], axis=1)
    dt_bias = jnp.pad(W["ssd_dt_bias"].reshape(1, 2 * NH), ((0, 0), (0, DTW - 2 * NH)))
    a_log = jnp.pad(W["ssd_a_log"].reshape(1, 2 * NH), ((0, 0), (0, DTW - 2 * NH)))
    dvec = jnp.repeat(W["ssd_d"].reshape(NH), HD).reshape(1, DI)
    lba = W["lru_ba"].reshape(2, 1, LW)
    lbi = W["lru_bi"].reshape(2, 1, LW)
    llam = W["lru_lambda"].reshape(2, 1, LW)

    h, w_in_all = _ln_mod_fwd(cfg, ctx2, x2, shift_tab, scale_tab, xch=(w_in_shard, TWO_LEVEL))
    w_main, w_dt = _perm_w_in(_from_slots(w_in_all, 1))
    proj, rest_all = _mm(h, w_main, "nn", "mm_proj", tm=1024, tn=2048, tk=1024, xch=(rest_payload, True))
    W = dict(W, w_main=w_main, w_dt=w_dt, **_unpack_rest(rest_all))
    dt_raw = _mm(h, W["w_dt"], "nn", "mm_dt", tm=512, tn=DTW, tk=1024)
    dt, dtg, cumg, cumTg = _dt_fwd(cfg, dt_raw, dt_bias, a_log)
    act, sgrad = _conv_fwd(cfg, proj, conv_w, conv_b)
    y_f, hs_f = _ssd_fwd(cfg, act, dtg, cumg, cumTg, False)
    y, hs_b = _ssd_fwd(cfg, act, dtg, cumg, cumTg, True, y_other=y_f, dvec=dvec)
    hss = [hs_f, hs_b]
    hls = [_lru_fwd(cfg, act, W["lru_wa"], W["lru_wi"], lba, lbi, llam, rev) for rev in (False, True)]
    nssd = _post_ssd_fwd(cfg, y, proj, W["ssd_norm_w"])
    vlru = _post_lru_fwd(cfg, hls[0], hls[1], proj)
    br_ssd = _mm(nssd, W["w_br_ssd"], "nn", "mm_br_ssd", tm=1024, tn=1024, tk=1024)
    br_lru = _mm(vlru, W["w_br_lru"], "nn", "mm_br_lru", tm=1024, tn=1024, tk=1024)
    mix = _merge_fwd(cfg, proj, W["b_gate"], br_ssd, br_lru)
    x_mix = _mm(mix, W["w_out"], "nn", "mm_out", tm=1024, tn=1024, tk=1024)
    x1, h2 = _resid1_fwd(cfg, x2, x_mix, gate1, shift2, scale2, W["ln1_g"], W["ln1_b"])
    a1, actm = _mm_mlp1(h2, W["w_mlp1"], W["b_mlp1"])
    mlp = _mm(actm, W["w_mlp2"], "nn", "mm_mlp2", tm=1024, tn=1024, tk=2048)
    dmlp, dx1p, ex2, gl2 = _final_fwd_bwd(cfg, x1, mlp, W["b_mlp2"], gate2, W["ln2_g"], W["ln2_b"],
                                          target.reshape(N, D))

    g = {}
    g["ln2_g"], g["ln2_b"], g["b_mlp2"] = gl2[0:1], gl2[1:2], gl2[2:3]
    loss_partial = gl2[3, 0]
    gw = {}
    gw["w_mlp2"] = _mm(actm, dmlp, "tn", "mm_dw_mlp2", BF16, tm=1024, tn=1024, tk=1024)
    da1, accb1 = _mm_dact(dmlp, W["w_mlp2"], a1)
    g["b_mlp1"] = accb1[0:1]
    dh2 = _mm(da1, W["w_mlp1"], "nt", "mm_dh2", tm=1024, tn=1024, tk=2048)
    gw["w_mlp1"] = _mm(h2, da1, "tn", "mm_dw_mlp1", BF16, tm=1024, tn=1024, tk=1024)
    dx_mix, dxp, ex1, gl1 = _resid1_bwd(cfg, dh2, dx1p, x1, x2, x_mix, gate1, scale2, W["ln1_g"])
    g["ln1_g"], g["ln1_b"] = gl1[0:1], gl1[1:2]
    dmix = _mm(dx_mix, W["w_out"], "nt", "mm_dmix", tm=1024, tn=1024, tk=1024)
    gw["w_out"] = _mm(mix, dx_mix, "tn", "mm_dw_out", BF16, tm=1024, tn=1024, tk=1024)
    dbs, dbl, dproj, accg = _merge_bwd(cfg, dmix, proj, W["b_gate"], br_ssd, br_lru)
    g["b_gate"] = accg[0:1]
    dnssd = _mm(dbs, W["w_br_ssd"], "nt", "mm_dnssd", tm=1024, tn=1024, tk=1024)
    gw["w_br_ssd"] = _mm(nssd, dbs, "tn", "mm_dw_br_ssd", BF16, tm=1024, tn=1024, tk=1024)
    dvlru = _mm(dbl, W["w_br_lru"], "nt", "mm_dvlru", tm=1024, tn=1024, tk=1024)
    gw["w_br_lru"] = _mm(vlru, dbl, "tn", "mm_dw_br_lru", BF16, tm=1024, tn=1024, tk=1024)
    dy, dproj, accs = _post_ssd_bwd(cfg, dproj, dnssd, y, act, proj, W["ssd_norm_w"])
    g["ssd_norm_w"] = accs[0:1]
    dD_cols = accs[1:2]
    dyl, dproj = _post_lru_bwd(cfg, dproj, dvlru, hls[0], hls[1], proj)

    rest_slots = jnp.concatenate([_to_slots(gw[n], axis).reshape(NDEV, -1, 1024) for n, _, axis in _BIG[1:]], axis=1)
    xres = {}
    dxh, dBs, dCs, dAs, dxxs, dus = [], [], [], [], [], []
    dwas, dwis, lvecs = [], [], []
    for i, rev in enumerate((False, True)):
        if i == 0:
            o, xres["rs_rest"] = _ssd_bwd(cfg, act, dtg, cumg, cumTg, hss[i], dy, rev, xch=(rest_slots, False))
        else:
            o = _ssd_bwd(cfg, act, dtg, cumg, cumTg, hss[i], dy, rev)
        dxh.append(o[0]); dBs.append(o[1]); dCs.append(o[2]); dAs.append(o[3]); dxxs.append(o[4])
        du, dwa, dwi, lv = _lru_bwd(cfg, act, W["lru_wa"], W["lru_wi"], lba, lbi, llam, hls[i], dyl, rev)
        dus.append(du); dwas.append(dwa); dwis.append(dwi); lvecs.append(lv)
    lru_payload = jnp.stack([jnp.stack(dwas), jnp.stack(dwis)]).reshape(-1, 1024)
    g["lru_ba"] = jnp.stack([lvecs[0][0], lvecs[1][0]])
    g["lru_bi"] = jnp.stack([lvecs[0][1], lvecs[1][1]])
    g["lru_lambda"] = jnp.stack([lvecs[0][2], lvecs[1][2]])

    ddt_raw, accdt, dw_dt = _dt_bwd(cfg, dAs, dxxs, dt_raw, dt, dt_bias, a_log, h)
    g["ssd_a_log"] = accdt[0, :2 * NH].reshape(2, NH)
    g["ssd_dt_bias"] = accdt[1, :2 * NH].reshape(2, NH)

    (dproj, accx), xres["ag_lru"] = _conv_bwd(cfg, "conv_bwd_x", dproj, proj, conv_w, sgrad, [dxh[0], dxh[1]], 0, DI,
                                              skip=(dy, dvec), xch=(lru_payload, True))
    dproj, accB = _conv_bwd(cfg, "conv_bwd_b", dproj, proj, conv_w, sgrad, [dBs[0], dBs[1]], DI, NG * NS)
    dproj, accC = _conv_bwd(cfg, "conv_bwd_c", dproj, proj, conv_w, sgrad, [dCs[0], dCs[1]], DI + NG * NS, NG * NS)
    dproj, accl = _conv_bwd(cfg, "conv_bwd_lru", dproj, proj, conv_w, None, [dus[0], dus[1]], DI + 2 * NG * NS, LW)
    accssd = jnp.concatenate([accx, accB, accC], axis=1)
    g["ssd_conv_w"], g["ssd_conv_b"] = accssd[0:4], accssd[4:5]
    g["lru_conv_w"], g["lru_conv_b"] = accl[0:4], accl[4:5]
    dw_main = _mm(h, dproj, "tn", "mm_dw_main", BF16, tm=1024, tn=2048, tk=1024)
    w_in_slots = _to_slots(_unperm_w_in(dw_main, dw_dt.astype(BF16)), 1)
    dh_a, xres["rs_w_in"] = _mm(dproj, W["w_main"], "nt", "mm_dh_main", tm=1024, tn=1024, tk=2048,
                                xch=(w_in_slots, False))
    grad_x, acct = _ln_mod_bwd(cfg, dh_a, ddt_raw, W["w_dt"], ctx2, x2, scale_tab, dxp)
    acct = acct.reshape(Bn, 2, 8, D)
    dm = jnp.concatenate([acct[:, 1, 0], acct[:, 1, 1], ex1[:, 2], ex1[:, 0], ex1[:, 1], ex2[:, 0]], axis=1)
    dmc = jnp.concatenate([acct[:, 0, 0], acct[:, 0, 1]], axis=1)
    g["ssd_d_cols"] = dD_cols
    return loss_partial, grad_x.reshape(Bn, T, D), g, dm, dmc, xres


MESH = pl.DeviceIdType.MESH
_HBM = pl.BlockSpec(memory_space=pltpu.HBM)


def _me():
    return 4 * lax.axis_index("x") + 2 * lax.axis_index("y") + lax.axis_index("c")


def _peer(k):
    px = (lax.axis_index("x") + ((k >> 2) & 1)) % 2
    py = (lax.axis_index("y") + ((k >> 1) & 1)) % 2
    pc = (lax.axis_index("c") + (k & 1)) % 2
    return (px, py, pc), 4 * px + 2 * py + pc


def _xchg_copies(x_ref, o_ref, send_sems, recv_sems, loc_sem, gather):
    me = _me()
    src_me = x_ref if gather else x_ref.at[me]
    loc = pltpu.make_async_copy(src_me, o_ref.at[me], loc_sem)
    sends, recvs = [], []
    for k in range(1, NDEV):
        peer, pid = _peer(k)
        sends.append(pltpu.make_async_remote_copy(
            src_ref=x_ref if gather else x_ref.at[pid], dst_ref=o_ref.at[me],
            send_sem=send_sems.at[k - 1], recv_sem=recv_sems.at[k - 1],
            device_id=peer, device_id_type=MESH))
        recvs.append(pltpu.make_async_remote_copy(
            src_ref=src_me, dst_ref=o_ref.at[pid],
            send_sem=send_sems.at[k - 1], recv_sem=recv_sems.at[k - 1],
            device_id=peer, device_id_type=MESH))
    return loc, sends, recvs


def _xchg_start(*refs, gather):
    loc, sends, _ = _xchg_copies(*refs, gather)
    loc.start()
    for cp in sends:
        cp.start()


def _xchg_wait(*refs, gather):
    loc, sends, recvs = _xchg_copies(*refs, gather)
    for cp in recvs:
        cp.wait_recv()
    for cp in sends:
        cp.wait_send()
    loc.wait()


_XCHG_SCRATCH = [pltpu.SemaphoreType.DMA((NDEV - 1,)), pltpu.SemaphoreType.DMA((NDEV - 1,)), pltpu.SemaphoreType.DMA]


def _xchg_out_shape(x, gather):
    return jax.ShapeDtypeStruct((NDEV,) + tuple(x.shape if gather else x.shape[1:]), x.dtype)


def _exchange(x, name, gather):
    def body(x_ref, o_ref, send_sems, recv_sems, loc_sem):
        _xchg_start(x_ref, o_ref, send_sems, recv_sems, loc_sem, gather=gather)
        _xchg_wait(x_ref, o_ref, send_sems, recv_sems, loc_sem, gather=gather)

    return pl.pallas_call(
        body, name=name, out_shape=_xchg_out_shape(x, gather),
        in_specs=[_HBM], out_specs=_HBM, scratch_shapes=_XCHG_SCRATCH,
    )(x)


TWO_LEVEL = "two_level"


def _two_level_copies(x_ref, o_ref, send_sems, recv_sems, loc_sem):
    mx, my, mc = lax.axis_index("x"), lax.axis_index("y"), lax.axis_index("c")
    me, sibling = (mx, my, mc), (mx, my, 1 - mc)
    chips = [(1 - mx, my), (mx, 1 - my), (1 - mx, 1 - my)]

    def slot(px, py, pc):
        return o_ref.at[4 * px + 2 * py + pc]

    def copy(k, block, to, src=None):
        return pltpu.make_async_remote_copy(
            src_ref=slot(*block) if src is None else src, dst_ref=slot(*block),
            send_sem=send_sems.at[k], recv_sem=recv_sems.at[k], device_id=to, device_id_type=MESH)

    mine = pltpu.make_async_copy(x_ref, slot(*me), loc_sem)
    first = [copy(0, me, sibling, src=x_ref)] + [copy(1 + j, me, (*chip, mc), src=x_ref) for j, chip in enumerate(chips)]
    passed = [copy(4 + j, (*chip, mc), sibling) for j, chip in enumerate(chips)]
    landed = [copy(1 + j, (*chip, mc), me) for j, chip in enumerate(chips)]
    from_sibling = [copy(0, sibling, me)] + [copy(4 + j, (*chip, 1 - mc), me) for j, chip in enumerate(chips)]
    return mine, first, passed, landed, from_sibling


def _two_level_start(*refs):
    mine, first, _, _, _ = _two_level_copies(*refs)
    mine.start()
    for cp in first:
        cp.start()


def _two_level_finish(*refs):
    mine, first, passed, landed, from_sibling = _two_level_copies(*refs)
    for cp, fwd in zip(landed, passed):
        cp.wait_recv()
        fwd.start()
    for cp in from_sibling:
        cp.wait_recv()
    for cp in first + passed:
        cp.wait_send()
    mine.wait()


def _hosted_call(body, xch, *, name, grid, in_specs, out_specs, out_shape, scratch_shapes, compiler_params, args,
                 aliases=None):
    aliases = aliases or {}
    if xch is None:
        return pl.pallas_call(body, name=name, grid=grid, in_specs=in_specs, out_specs=out_specs,
                              out_shape=out_shape, scratch_shapes=scratch_shapes, input_output_aliases=aliases,
                              compiler_params=compiler_params)(*args)
    xv, gather = xch
    n_in, n_out, n_scr = len(in_specs), len(out_specs), len(scratch_shapes)

    def wrapped(*refs):
        ins = refs[:n_in]
        x_ref = refs[n_in]
        outs = refs[n_in + 1:n_in + 1 + n_out]
        o_ref = refs[n_in + 1 + n_out]
        scr = refs[n_in + 2 + n_out:]
        own, sems = scr[:n_scr], scr[n_scr:]
        first = functools.reduce(jnp.logical_and, [pl.program_id(a) == 0 for a in range(len(grid))])
        last = functools.reduce(jnp.logical_and, [pl.program_id(a) == grid[a] - 1 for a in range(len(grid))])

        @pl.when(first)
        def _():
            if gather == TWO_LEVEL:
                _two_level_start(x_ref, o_ref, *sems)
            else:
                _xchg_start(x_ref, o_ref, *sems, gather=gather)

        body(*ins, *outs, *own)

        @pl.when(last)
        def _():
            if gather == TWO_LEVEL:
                _two_level_finish(x_ref, o_ref, *sems)
            else:
                _xchg_wait(x_ref, o_ref, *sems, gather=gather)

    res = pl.pallas_call(
        wrapped, name=name, grid=grid, in_specs=list(in_specs) + [_HBM], out_specs=list(out_specs) + [_HBM],
        out_shape=list(out_shape) + [_xchg_out_shape(xv, gather)],
        scratch_shapes=list(scratch_shapes) + _XCHG_SCRATCH, input_output_aliases=aliases,
        compiler_params=compiler_params,
    )(*args, xv)
    return list(res[:n_out]), res[n_out]


def _row_tile(R, cap, mult=8):
    best = mult
    t = mult
    while t <= min(R, cap):
        if R % t == 0:
            best = t
        t += mult
    assert R % best == 0, R
    return best


def _sum_slots(x, name, xch=None):
    _, R, C = x.shape
    tr = _row_tile(R, 256, 16 if x.dtype == BF16 else 8)

    def body(x_ref, o_ref):
        o_ref[...] = _slot_sum(x_ref)

    res = _hosted_call(
        body, xch, name=name, grid=(R // tr,),
        in_specs=[pl.BlockSpec((NDEV, tr, C), lambda i: (0, i, 0))],
        out_specs=[pl.BlockSpec((tr, C), lambda i: (i, 0))],
        out_shape=[jax.ShapeDtypeStruct((R, C), F32)],
        scratch_shapes=[], compiler_params=_cp(1), args=(x,))
    if xch is None:
        return res[0]
    return res[0][0], res[1]


def _slot_sum(x_ref):
    acc = x_ref[0].astype(F32)
    for i in range(1, NDEV):
        acc = acc + x_ref[i].astype(F32)
    return acc


def _sum_adamw(slots, w, m, v, name, xch=None):
    _, R, C = slots.shape
    tr = _row_tile(R, 128, 16 if slots.dtype == BF16 else 8)

    def body(x_ref, w_ref, m_ref, v_ref, g_ref, d_ref, nm_ref, nv_ref):
        g_ref[...] = _slot_sum(x_ref)
        _adamw_update(w_ref, g_ref, m_ref, v_ref, d_ref, nm_ref, nv_ref)

    blk = pl.BlockSpec((tr, C), lambda i: (i, 0))
    res = _hosted_call(
        body, xch, name=name, grid=(R // tr,),
        in_specs=[pl.BlockSpec((NDEV, tr, C), lambda i: (0, i, 0)), blk, blk, blk],
        out_specs=[blk] * 4, out_shape=[jax.ShapeDtypeStruct((R, C), F32)] * 4,
        scratch_shapes=[], compiler_params=_cp(1), args=(slots, w, m, v))
    if xch is None:
        return res
    return res[0], res[1]


def _adamw_update(w_ref, g_ref, m_ref, v_ref, d_ref, nm_ref, nv_ref):
    c1 = 1.0 / (1.0 - ADAM_B1 ** ADAM_STEP)
    c2 = 1.0 / (1.0 - ADAM_B2 ** ADAM_STEP)
    gv = g_ref[...]
    nm = ADAM_B1 * m_ref[...] + (1.0 - ADAM_B1) * gv
    nv = ADAM_B2 * v_ref[...] + (1.0 - ADAM_B2) * (gv * gv)
    d_ref[...] = -ADAM_LR * ((nm * c1) / (jnp.sqrt(nv * c2) + ADAM_EPS) + ADAM_WD * w_ref[...])
    nm_ref[...] = nm
    nv_ref[...] = nv


def _adamw_many(ws, gs, ms, vs):
    n = len(ws)

    def body(*refs):
        for i in range(n):
            _adamw_update(refs[i], refs[n + i], refs[2 * n + i], refs[3 * n + i],
                          refs[4 * n + i], refs[5 * n + i], refs[6 * n + i])

    shapes = [jax.ShapeDtypeStruct(w.shape, F32) for w in ws]
    res = pl.pallas_call(
        body, name="adamw_small", out_shape=shapes * 3,
        compiler_params=pltpu.CompilerParams(vmem_limit_bytes=VMEM_LIMIT_BYTES),
    )(*ws, *gs, *ms, *vs)
    return res[:n], res[n:2 * n], res[2 * n:]


def _adamw(w, g, m, v, name):
    R, C = w.shape
    tr = _row_tile(R, 256)

    def body(w_ref, g_ref, m_ref, v_ref, d_ref, nm_ref, nv_ref):
        _adamw_update(w_ref, g_ref, m_ref, v_ref, d_ref, nm_ref, nv_ref)

    blk = pl.BlockSpec((tr, C), lambda i: (i, 0))
    return pl.pallas_call(
        body, name=name, grid=(R // tr,),
        in_specs=[blk] * 4, out_specs=[blk] * 3,
        out_shape=[jax.ShapeDtypeStruct((R, C), F32)] * 3,
        compiler_params=_cp(1),
    )(w, g, m, v)


def _mod_fwd(c_rows, w_shard, b_shard):
    def body(c_ref, w_ref, b_ref, o_ref):
        s = _silu(c_ref[...]).astype(BF16)
        o_ref[...] = _dot(s, w_ref[...].astype(BF16)) + b_ref[...]

    return pl.pallas_call(
        body, name="mod_fwd",
        out_shape=jax.ShapeDtypeStruct((c_rows.shape[0], w_shard.shape[1]), F32),
        compiler_params=pltpu.CompilerParams(vmem_limit_bytes=VMEM_LIMIT_BYTES),
    )(c_rows, w_shard, b_shard)


def _mod_bwd(c_rows, dm_all, dm_shard, w_shard):
    nrow = c_rows.shape[0]

    def body(c_ref, da_ref, ds_ref, w_ref, gw_ref, gb_ref, cc_ref):
        s = _silu(c_ref[...]).astype(BF16)
        ds = ds_ref[...]
        gw_ref[...] = _dot_tn(s, ds.astype(BF16))
        gb_ref[...] = jnp.sum(da_ref[...], axis=0, keepdims=True)
        rowi = lax.broadcasted_iota(jnp.int32, ds.shape, 0)
        dmc = jnp.sum(jnp.where(rowi % 8 >= 4, ds, 0.0), axis=0, keepdims=True)
        dmc8 = jnp.broadcast_to(dmc, (8, ds.shape[1])).astype(BF16)
        cc_ref[...] = _dot_nt(dmc8, w_ref[...].astype(BF16))

    return pl.pallas_call(
        body, name="mod_bwd",
        out_shape=[jax.ShapeDtypeStruct(w_shard.shape, F32),
                   jax.ShapeDtypeStruct((1, dm_all.shape[1]), F32),
                   jax.ShapeDtypeStruct((8, D), F32)],
        compiler_params=pltpu.CompilerParams(vmem_limit_bytes=VMEM_LIMIT_BYTES),
    )(c_rows, dm_all, dm_shard, w_shard)


def _small_finish(cc_pre, c_ctx, dd_cols):
    def body(cc_ref, c_ref, dd_ref, gc_ref, gd_ref):
        gc_ref[...] = cc_ref[...] * _silu_grad(c_ref[...])
        gd_ref[...] = jnp.sum(dd_ref[...], axis=1, keepdims=True)

    return pl.pallas_call(
        body, name="small_finish",
        out_shape=[jax.ShapeDtypeStruct((1, D), F32), jax.ShapeDtypeStruct((NH, 1), F32)],
    )(cc_pre, c_ctx, dd_cols)


_BIG = (("w_in", (D, 10304), 1), ("w_br_ssd", (DI, D), 0), ("w_br_lru", (LW, D), 0), ("w_out", (D, D), 0),
        ("w_mlp1", (D, MLP), 1), ("w_mlp2", (MLP, D), 0))
_SMALL_SH = (("ssd_conv_w", (4, 4096)), ("lru_conv_w", (4, LW)), ("lru_ba", (2, LW)), ("lru_bi", (2, LW)),
             ("lru_lambda", (2, LW)))
_REPL = (("c_ctx", (D,)), ("b_gate", (2 * D,)), ("ssd_conv_b", (4096,)), ("ssd_dt_bias", (2, NH)),
         ("ssd_a_log", (2, NH)), ("ssd_d", (DI,)), ("ssd_norm_w", (DI,)), ("lru_conv_b", (LW,)),
         ("ln1_g", (D,)), ("ln1_b", (D,)),
         ("b_mlp1", (MLP,)), ("b_mlp2", (D,)), ("ln2_g", (D,)), ("ln2_b", (D,)))

_WEIGHT_NAMES = ('c_ctx', 'w_mod', 'b_mod', 'w_in', 'b_gate', 'ssd_conv_w', 'ssd_conv_b', 'ssd_dt_bias', 'ssd_a_log',
                 'ssd_d', 'ssd_norm_w', 'lru_conv_w', 'lru_conv_b', 'lru_wa', 'lru_ba', 'lru_wi', 'lru_bi',
                 'lru_lambda', 'w_br_ssd', 'w_br_lru', 'w_out', 'ln1_g', 'ln1_b', 'w_mlp1', 'b_mlp1', 'w_mlp2',
                 'b_mlp2', 'ln2_g', 'ln2_b')
_ARG_NAMES = ('x', 'c', 'ctx') + _WEIGHT_NAMES + ('loss_target',) + tuple('m_' + n for n in _WEIGHT_NAMES) + tuple(
    'v_' + n for n in _WEIGHT_NAMES)


def _to_slots(full, axis):
    n = full.shape[axis] // NDEV
    if axis == 0:
        return full.reshape(NDEV, n, full.shape[1])
    return full.reshape(full.shape[0], NDEV, n).transpose(1, 0, 2)


def _from_slots(slots, axis):
    if axis == 0:
        return slots.reshape(NDEV * slots.shape[1], slots.shape[2])
    return slots.transpose(1, 0, 2).reshape(slots.shape[1], NDEV * slots.shape[2])


def _pack_rows(arrs, width=1024, mult=8):
    flat = jnp.concatenate([a.reshape(-1) for a in arrs])
    n = flat.shape[0]
    per = width * mult
    tot = -(-n // per) * per
    return jnp.pad(flat, (0, tot - n)).reshape(tot // width, width)


def _unpack_rows(packed, shapes, lead=()):
    nl = len(lead)
    flat = packed.reshape(tuple(lead) + (-1,))
    out, off = [], 0
    for s in shapes:
        n = math.prod(s)
        out.append(flat[..., off:off + n].reshape(tuple(lead) + tuple(s)))
        off += n
    return out


def kernel(x, c, ctx, c_ctx, w_mod, b_mod, w_in, b_gate, ssd_conv_w, ssd_conv_b, ssd_dt_bias, ssd_a_log, ssd_d, ssd_norm_w, lru_conv_w, lru_conv_b, lru_wa, lru_ba, lru_wi, lru_bi, lru_lambda, w_br_ssd, w_br_lru, w_out, ln1_g, ln1_b, w_mlp1, b_mlp1, w_mlp2, b_mlp2, ln2_g, ln2_b, loss_target, m_c_ctx, m_w_mod, m_b_mod, m_w_in, m_b_gate, m_ssd_conv_w, m_ssd_conv_b, m_ssd_dt_bias, m_ssd_a_log, m_ssd_d, m_ssd_norm_w, m_lru_conv_w, m_lru_conv_b, m_lru_wa, m_lru_ba, m_lru_wi, m_lru_bi, m_lru_lambda, m_w_br_ssd, m_w_br_lru, m_w_out, m_ln1_g, m_ln1_b, m_w_mlp1, m_b_mlp1, m_w_mlp2, m_b_mlp2, m_ln2_g, m_ln2_b, v_c_ctx, v_w_mod, v_b_mod, v_w_in, v_b_gate, v_ssd_conv_w, v_ssd_conv_b, v_ssd_dt_bias, v_ssd_a_log, v_ssd_d, v_ssd_norm_w, v_lru_conv_w, v_lru_conv_b, v_lru_wa, v_lru_ba, v_lru_wi, v_lru_bi, v_lru_lambda, v_w_br_ssd, v_w_br_lru, v_w_out, v_ln1_g, v_ln1_b, v_w_mlp1, v_b_mlp1, v_w_mlp2, v_b_mlp2, v_ln2_g, v_ln2_b):
    A = dict(zip(_ARG_NAMES, (x, c, ctx, c_ctx, w_mod, b_mod, w_in, b_gate, ssd_conv_w, ssd_conv_b, ssd_dt_bias, ssd_a_log, ssd_d, ssd_norm_w, lru_conv_w, lru_conv_b, lru_wa, lru_ba, lru_wi, lru_bi, lru_lambda, w_br_ssd, w_br_lru, w_out, ln1_g, ln1_b, w_mlp1, b_mlp1, w_mlp2, b_mlp2, ln2_g, ln2_b, loss_target, m_c_ctx, m_w_mod, m_b_mod, m_w_in, m_b_gate, m_ssd_conv_w, m_ssd_conv_b, m_ssd_dt_bias, m_ssd_a_log, m_ssd_d, m_ssd_norm_w, m_lru_conv_w, m_lru_conv_b, m_lru_wa, m_lru_ba, m_lru_wi, m_lru_bi, m_lru_lambda, m_w_br_ssd, m_w_br_lru, m_w_out, m_ln1_g, m_ln1_b, m_w_mlp1, m_b_mlp1, m_w_mlp2, m_b_mlp2, m_ln2_g, m_ln2_b, v_c_ctx, v_w_mod, v_b_mod, v_w_in, v_b_gate, v_ssd_conv_w, v_ssd_conv_b, v_ssd_dt_bias, v_ssd_a_log, v_ssd_d, v_ssd_norm_w, v_lru_conv_w, v_lru_conv_b, v_lru_wa, v_lru_ba, v_lru_wi, v_lru_bi, v_lru_lambda, v_w_br_ssd, v_w_br_lru, v_w_out, v_ln1_g, v_ln1_b, v_w_mlp1, v_b_mlp1, v_w_mlp2, v_b_mlp2, v_ln2_g, v_ln2_b)))
    Bn, T, _ = x.shape
    Tc = ctx.shape[1]
    cfg = _Cfg(Bn, T, Tc)
    me = _me()
    L = {n: (A[n] if n == "c_ctx" else A[n][0]) for n in _WEIGHT_NAMES}
    nmod = L["w_mod"].shape[1]

    c_all = _exchange(c, "ag_c", True)
    c_rows = jnp.concatenate([c_all.reshape(NDEV * Bn, D), jnp.broadcast_to(c_ctx[None, :], (8, D))], axis=0)
    b_shard = lax.dynamic_slice(L["b_mod"], (me * nmod,), (nmod,)).reshape(1, nmod)
    m_part = _mod_fwd(c_rows, L["w_mod"], b_shard)
    m_all = _exchange(m_part, "ag_mod", True)
    m_full = m_all.transpose(1, 0, 2).reshape(NDEV * Bn + 8, NMOD * D)
    m_mine = lax.dynamic_slice(m_full, (me * Bn, 0), (Bn, NMOD * D))
    mc = m_full[NDEV * Bn, :2 * D]

    rest_payload = jnp.concatenate([L[n].astype(BF16).reshape(-1, 1024) for n, _, _ in _BIG[1:]], axis=0)
    small_shapes = [(s[0], s[1] // NDEV) for _, s in _SMALL_SH]
    small_all = _exchange(_pack_rows([L[n] for n, _ in _SMALL_SH], width=512), "ag_w_small", True)
    W = {}
    for (n, shp), piece in zip(_SMALL_SH, _unpack_rows(small_all, small_shapes, lead=(NDEV,))):
        W[n] = piece.transpose(1, 0, 2).reshape(shp)
    for n in ("ssd_conv_b", "lru_conv_b", "ssd_norm_w", "b_gate", "ln1_g", "ln1_b", "b_mlp1", "b_mlp2", "ln2_g", "ln2_b"):
        W[n] = L[n].reshape(1, -1)
    for n in ("ssd_dt_bias", "ssd_a_log", "ssd_d", "lru_wa", "lru_wi"):
        W[n] = L[n]

    loss_part, grad_x, g, dm, dmc, xres = _local_step(cfg, x, ctx, loss_target, m_mine, mc, W,
                                                      L["w_in"].astype(BF16), rest_payload)
    loss = lax.psum(loss_part, ("x", "y", "c"))

    dmc_pad = jnp.pad(dmc, ((0, 4 - Bn), (0, (NMOD - 2) * D)))
    dm_payload = jnp.concatenate([jnp.pad(dm, ((0, 4 - Bn), (0, 0))), dmc_pad], axis=0)
    upd_w_in, dm_all = _sum_adamw(xres["rs_w_in"], L["w_in"], A["m_w_in"][0], A["v_w_in"][0], "sum_adamw_w_in",
                                  xch=(dm_payload, True))
    dm_all = dm_all.reshape(NDEV * 8, NMOD * D)
    c_rows_b = jnp.concatenate([jnp.pad(c_all, ((0, 0), (0, 4 - Bn), (0, 0))),
                                jnp.broadcast_to(c_ctx[None, None, :], (NDEV, 4, D))], axis=1).reshape(NDEV * 8, D)
    dm_shard = lax.dynamic_slice(dm_all, (0, me * nmod), (NDEV * 8, nmod))
    g_w_mod, g_b_mod, cc_part = _mod_bwd(c_rows_b, dm_all, dm_shard, L["w_mod"])
    g["c_ctx"] = cc_part[0]

    g["ssd_d"] = g.pop("ssd_d_cols")
    small_names = [n for n, _ in _REPL] + [n for n, _ in _SMALL_SH]
    small_full_shapes = [s for _, s in _REPL] + [s for _, s in _SMALL_SH]
    red_b, sm_all = _sum_slots(xres["rs_rest"], "sum_w_rest", xch=(_pack_rows([g[n] for n in small_names]), True))
    sm_sum = _sum_slots(sm_all, "sum_g_small")
    gs = dict(zip(small_names, _unpack_rows(sm_sum, small_full_shapes)))
    gcc, gdd = _small_finish(gs["c_ctx"].reshape(1, D), c_ctx.reshape(1, D), gs["ssd_d"].reshape(NH, HD))
    gs["c_ctx"] = gcc.reshape(D)
    gs["ssd_d"] = gdd.reshape(NH)
    for n, shp in _SMALL_SH:
        ns = shp[1] // NDEV
        gs[n] = lax.dynamic_slice(gs[n], (0, me * ns), (shp[0], ns))
    gs["b_mod"] = g_b_mod.reshape(NMOD * D)
    lru_sum = _sum_slots(xres["ag_lru"], "sum_g_lru").reshape(2, 2, LB, LBW, LBW)
    gs["lru_wa"], gs["lru_wi"] = lru_sum[0], lru_sum[1]

    gb = {}
    off = 0
    for n, shp, axis in _BIG[1:]:
        shard_shape = (shp[0] // NDEV, shp[1]) if axis == 0 else (shp[0], shp[1] // NDEV)
        r = math.prod(shard_shape) // 1024
        gb[n] = red_b[off:off + r].reshape(shard_shape)
        off += r
    gb["w_mod"] = g_w_mod

    grads, deltas, new_m, new_v = {}, {}, {}, {}
    big_names = ["w_mod"] + [n for n, _, _ in _BIG]
    grads["w_in"], deltas["w_in"], new_m["w_in"], new_v["w_in"] = upd_w_in
    for n in big_names:
        if n == "w_in":
            continue
        d_, nm_, nv_ = _adamw(L[n], gb[n], A["m_" + n][0], A["v_" + n][0], "adamw_" + n)
        grads[n], deltas[n], new_m[n], new_v[n] = gb[n], d_, nm_, nv_
    sm_names = [n for n in _WEIGHT_NAMES if n not in big_names]

    def two_d(a):
        return a.reshape(1, -1) if a.ndim == 1 else a
    loc = lambda pre: [two_d(A[pre + n] if n == "c_ctx" else A[pre + n][0]) for n in sm_names]
    gsm = [two_d(gs[n].reshape(L[n].shape)) for n in sm_names]
    ds_, nms_, nvs_ = _adamw_many(loc(""), gsm, loc("m_"), loc("v_"))
    for n, gv, dv, mv, vv in zip(sm_names, gsm, ds_, nms_, nvs_):
        shp = L[n].shape
        grads[n], deltas[n], new_m[n], new_v[n] = gv.reshape(shp), dv.reshape(shp), mv.reshape(shp), vv.reshape(shp)

    def out(dct):
        return [dct[n] if n == "c_ctx" else dct[n][None] for n in _WEIGHT_NAMES]
    return (loss, grad_x, *out(grads), *out(deltas), *out(new_m), *out(new_v))
```
